```python
import math
import jax, jax.numpy as jnp
from jax import lax
import numpy as np

D_MODEL = 1024
BATCH = 8
SEQ = 2048
DEPTH = 1

SC_WIDTH = D_MODEL
SC_GROUPS = 16
SC_KERNEL = 3
SSM_EXPAND = 2
SSM_INNER = SSM_EXPAND * D_MODEL
SSM_HEADDIM = 64
SSM_HEADS = SSM_INNER // SSM_HEADDIM
SSM_GROUPS = 8
SSM_STATE = 128
SSM_CONV = 4
SSM_CHUNK = 128
SSM_CONV_DIM = SSM_INNER + 2 * SSM_GROUPS * SSM_STATE
D_FF = 4 * D_MODEL
EPS = 1e-6

COL_SC = 3 * SC_WIDTH
COL_SSM = SSM_INNER + SSM_CONV_DIM + SSM_HEADS
COL_GATE = 2 * D_MODEL
D_IN_PROJ = COL_SC + COL_SSM + COL_GATE

kernel_name = "hybrid_shortconv_ssd_gated_merge"


def rmsnorm(x, w):
    xf = x.astype(jnp.float32)
    xf = xf * lax.rsqrt(jnp.mean(xf * xf, axis=-1, keepdims=True) + EPS)
    return xf.astype(x.dtype) * w


def causal_depthwise_conv(u, w):
    K = w.shape[0]
    L = u.shape[1]
    up = jnp.pad(u, ((0, 0), (K - 1, 0), (0, 0)))
    y = up[:, 0:L] * w[0]
    for k in range(1, K):
        y = y + up[:, k:k + L] * w[k]
    return y


def ssd_chunked(xh, dt, A, Bg, Cg):
    b, l, h, p = xh.shape
    g, n = Bg.shape[2], Bg.shape[3]
    r = h // g
    q = SSM_CHUNK
    c = l // q
    x = xh.astype(jnp.float32).reshape(b, c, q, g, r, p)
    dt = dt.astype(jnp.float32).reshape(b, c, q, g, r)
    B = Bg.astype(jnp.float32).reshape(b, c, q, g, n)
    C = Cg.astype(jnp.float32).reshape(b, c, q, g, n)
    dA = dt * A.astype(jnp.float32).reshape(g, r)
    dA_cs = jnp.cumsum(dA, axis=2)
    xdt = x * dt[..., None]
    seg = dA_cs[:, :, :, None] - dA_cs[:, :, None, :]
    mask = jnp.tril(jnp.ones((q, q), dtype=bool))[:, :, None, None]
    Lmat = jnp.exp(jnp.where(mask, seg, -jnp.inf))
    CB = jnp.einsum('bcign,bcjgn->bcijg', C, B)
    W = CB[..., None] * Lmat
    y_diag = jnp.einsum('bcijgr,bcjgrp->bcigrp', W, xdt)
    decay = jnp.exp(dA_cs[:, :, -1:] - dA_cs)
    states = jnp.einsum('bcjgn,bcjgr,bcjgrp->bcgrpn', B, decay, xdt)
    chunk_decay = jnp.exp(dA_cs[:, :, -1])

    def step(carry, inp):
        s_c, d_c = inp
        new = carry * d_c[..., None, None] + s_c
        return new, carry

    init = jnp.zeros((b, g, r, p, n), jnp.float32)
    _, prev = lax.scan(step, init, (jnp.moveaxis(states, 1, 0), jnp.moveaxis(chunk_decay, 1, 0)))
    prev = jnp.moveaxis(prev, 0, 1)
    y_off = jnp.einsum('bcign,bcgrpn,bcigr->bcigrp', C, prev, jnp.exp(dA_cs))
    return (y_diag + y_off).reshape(b, l, h, p)


def _fwd_setup_inputs(seed: int = 0) -> dict:
    key = jax.random.key(seed)
    ks = jax.random.split(key, 20)
    f32 = jnp.float32
    nrm = lambda k, shape, fan: jax.random.normal(k, shape, f32) * (fan ** -0.5)
    dt0 = jnp.exp(jax.random.uniform(ks[9], (SSM_HEADS,), f32, math.log(1e-3), math.log(1e-1)))
    return {
        "x": jax.random.normal(ks[0], (BATCH, SEQ, D_MODEL), f32),
        "norm_mix": 1.0 + 0.02 * jax.random.normal(ks[1], (D_MODEL,), f32),
        "w_in": nrm(ks[2], (D_MODEL, D_IN_PROJ), D_MODEL),
        "b_gate": 0.02 * jax.random.normal(ks[3], (COL_GATE,), f32),
        "sc_conv_w": nrm(ks[4], (SC_KERNEL, SC_WIDTH), SC_KERNEL),
        "ssm_conv_w": nrm(ks[5], (SSM_CONV, SSM_CONV_DIM), SSM_CONV),
        "ssm_conv_b": 0.02 * jax.random.normal(ks[6], (SSM_CONV_DIM,), f32),
        "dt_bias": dt0 + jnp.log(-jnp.expm1(-dt0)),
        "A_log": jnp.log(jax.random.uniform(ks[7], (SSM_HEADS,), f32, 1.0, 16.0)),
        "D_skip": 1.0 + 0.02 * jax.random.normal(ks[8], (SSM_HEADS,), f32),
        "ssm_norm_w": 1.0 + 0.02 * jax.random.normal(ks[10], (SSM_INNER,), f32),
        "w_branch_sc": nrm(ks[11], (SC_WIDTH, D_MODEL), SC_WIDTH),
        "w_branch_ssm": nrm(ks[12], (SSM_INNER, D_MODEL), SSM_INNER),
        "w_out": nrm(ks[13], (D_MODEL, D_MODEL), D_MODEL),
        "norm_mlp": 1.0 + 0.02 * jax.random.normal(ks[14], (D_MODEL,), f32),
        "w_mlp1": nrm(ks[15], (D_MODEL, D_FF), D_MODEL),
        "w_mlp2": nrm(ks[16], (D_FF, D_MODEL), D_FF),
        "norm_final": 1.0 + 0.02 * jax.random.normal(ks[17], (D_MODEL,), f32),
    }


def _fwd_reference(x, norm_mix, w_in, b_gate, sc_conv_w, ssm_conv_w, ssm_conv_b, dt_bias, A_log,
              D_skip, ssm_norm_w, w_branch_sc, w_branch_ssm, w_out, norm_mlp, w_mlp1, w_mlp2,
              norm_final):
    b, l, _ = x.shape
    for _layer in range(DEPTH):
        h = rmsnorm(x, norm_mix)
        proj = h @ w_in
        sc_part = proj[..., :COL_SC]
        ssm_part = proj[..., COL_SC:COL_SC + COL_SSM]
        gate_part = proj[..., COL_SC + COL_SSM:] + b_gate

        B_sc = sc_part[..., :SC_WIDTH]
        C_sc = sc_part[..., SC_WIDTH:2 * SC_WIDTH]
        x_sc = sc_part[..., 2 * SC_WIDTH:]
        y_a = B_sc * causal_depthwise_conv(C_sc * x_sc, sc_conv_w)
        br_a = y_a @ w_branch_sc

        z = ssm_part[..., :SSM_INNER]
        xBC = ssm_part[..., SSM_INNER:SSM_INNER + SSM_CONV_DIM]
        dt_raw = ssm_part[..., SSM_INNER + SSM_CONV_DIM:]
        xBC = jax.nn.silu(causal_depthwise_conv(xBC, ssm_conv_w) + ssm_conv_b)
        xs = xBC[..., :SSM_INNER].reshape(b, l, SSM_HEADS, SSM_HEADDIM)
        Bg = xBC[..., SSM_INNER:SSM_INNER + SSM_GROUPS * SSM_STATE].reshape(b, l, SSM_GROUPS, SSM_STATE)
        Cg = xBC[..., SSM_INNER + SSM_GROUPS * SSM_STATE:].reshape(b, l, SSM_GROUPS, SSM_STATE)
        dt = jax.nn.softplus(dt_raw.astype(jnp.float32) + dt_bias.astype(jnp.float32))
        A = -jnp.exp(A_log.astype(jnp.float32))
        y = ssd_chunked(xs, dt, A, Bg, Cg) + D_skip.astype(jnp.float32)[:, None] * xs.astype(jnp.float32)
        y = y.reshape(b, l, SSM_INNER)
        yz = (y * jax.nn.silu(z.astype(jnp.float32))).reshape(b, l, SSM_GROUPS, SSM_INNER // SSM_GROUPS)
        yz = yz * lax.rsqrt(jnp.mean(yz * yz, axis=-1, keepdims=True) + EPS)
        y_b = yz.reshape(b, l, SSM_INNER).astype(x.dtype) * ssm_norm_w
        br_b = y_b @ w_branch_ssm

        g = jax.nn.sigmoid(gate_part)
        merged = g[..., :D_MODEL] * br_a + g[..., D_MODEL:] * br_b
        x = x + merged @ w_out

        h2 = rmsnorm(x, norm_mlp)
        x = x + jnp.square(jax.nn.relu(h2 @ w_mlp1)) @ w_mlp2
    return rmsnorm(x, norm_final)


import jax as _jax
import jax.numpy as _jnp

TWIN_FORMAT = 'train_step'
FWD_PARAMS = ['x', 'norm_mix', 'w_in', 'b_gate', 'sc_conv_w', 'ssm_conv_w', 'ssm_conv_b', 'dt_bias', 'A_log', 'D_skip', 'ssm_norm_w', 'w_branch_sc', 'w_branch_ssm', 'w_out', 'norm_mlp', 'w_mlp1', 'w_mlp2', 'norm_final']
TWIN_WEIGHTS = ['norm_mix', 'w_in', 'b_gate', 'sc_conv_w', 'ssm_conv_w', 'ssm_conv_b', 'dt_bias', 'A_log', 'D_skip', 'ssm_norm_w', 'w_branch_sc', 'w_branch_ssm', 'w_out', 'norm_mlp', 'w_mlp1', 'w_mlp2', 'norm_final']
TWIN_DIFF_INPUT = 'x'
TWIN_INPUTS = ['x', 'norm_mix', 'w_in', 'b_gate', 'sc_conv_w', 'ssm_conv_w', 'ssm_conv_b', 'dt_bias', 'A_log', 'D_skip', 'ssm_norm_w', 'w_branch_sc', 'w_branch_ssm', 'w_out', 'norm_mlp', 'w_mlp1', 'w_mlp2', 'norm_final', 'loss_target', 'm_norm_mix', 'm_w_in', 'm_b_gate', 'm_sc_conv_w', 'm_ssm_conv_w', 'm_ssm_conv_b', 'm_dt_bias', 'm_A_log', 'm_D_skip', 'm_ssm_norm_w', 'm_w_branch_sc', 'm_w_branch_ssm', 'm_w_out', 'm_norm_mlp', 'm_w_mlp1', 'm_w_mlp2', 'm_norm_final', 'v_norm_mix', 'v_w_in', 'v_b_gate', 'v_sc_conv_w', 'v_ssm_conv_w', 'v_ssm_conv_b', 'v_dt_bias', 'v_A_log', 'v_D_skip', 'v_ssm_norm_w', 'v_w_branch_sc', 'v_w_branch_ssm', 'v_w_out', 'v_norm_mlp', 'v_w_mlp1', 'v_w_mlp2', 'v_norm_final']
TWIN_OUTPUTS = ['loss', 'grad_x', 'grad_norm_mix', 'grad_w_in', 'grad_b_gate', 'grad_sc_conv_w', 'grad_ssm_conv_w', 'grad_ssm_conv_b', 'grad_dt_bias', 'grad_A_log', 'grad_D_skip', 'grad_ssm_norm_w', 'grad_w_branch_sc', 'grad_w_branch_ssm', 'grad_w_out', 'grad_norm_mlp', 'grad_w_mlp1', 'grad_w_mlp2', 'grad_norm_final', 'delta_norm_mix', 'delta_w_in', 'delta_b_gate', 'delta_sc_conv_w', 'delta_ssm_conv_w', 'delta_ssm_conv_b', 'delta_dt_bias', 'delta_A_log', 'delta_D_skip', 'delta_ssm_norm_w', 'delta_w_branch_sc', 'delta_w_branch_ssm', 'delta_w_out', 'delta_norm_mlp', 'delta_w_mlp1', 'delta_w_mlp2', 'delta_norm_final', 'new_m_norm_mix', 'new_m_w_in', 'new_m_b_gate', 'new_m_sc_conv_w', 'new_m_ssm_conv_w', 'new_m_ssm_conv_b', 'new_m_dt_bias', 'new_m_A_log', 'new_m_D_skip', 'new_m_ssm_norm_w', 'new_m_w_branch_sc', 'new_m_w_branch_ssm', 'new_m_w_out', 'new_m_norm_mlp', 'new_m_w_mlp1', 'new_m_w_mlp2', 'new_m_norm_final', 'new_v_norm_mix', 'new_v_w_in', 'new_v_b_gate', 'new_v_sc_conv_w', 'new_v_ssm_conv_w', 'new_v_ssm_conv_b', 'new_v_dt_bias', 'new_v_A_log', 'new_v_D_skip', 'new_v_ssm_norm_w', 'new_v_w_branch_sc', 'new_v_w_branch_ssm', 'new_v_w_out', 'new_v_norm_mlp', 'new_v_w_mlp1', 'new_v_w_mlp2', 'new_v_norm_final']
TWIN_LEAF_KINDS = {'loss': 'loss', 'grad_x': 'grad_x', 'grad_norm_mix': 'grad_w', 'grad_w_in': 'grad_w', 'grad_b_gate': 'grad_w', 'grad_sc_conv_w': 'grad_w', 'grad_ssm_conv_w': 'grad_w', 'grad_ssm_conv_b': 'grad_w', 'grad_dt_bias': 'grad_w', 'grad_A_log': 'grad_w', 'grad_D_skip': 'grad_w', 'grad_ssm_norm_w': 'grad_w', 'grad_w_branch_sc': 'grad_w', 'grad_w_branch_ssm': 'grad_w', 'grad_w_out': 'grad_w', 'grad_norm_mlp': 'grad_w', 'grad_w_mlp1': 'grad_w', 'grad_w_mlp2': 'grad_w', 'grad_norm_final': 'grad_w', 'delta_norm_mix': 'delta_w', 'delta_w_in': 'delta_w', 'delta_b_gate': 'delta_w', 'delta_sc_conv_w': 'delta_w', 'delta_ssm_conv_w': 'delta_w', 'delta_ssm_conv_b': 'delta_w', 'delta_dt_bias': 'delta_w', 'delta_A_log': 'delta_w', 'delta_D_skip': 'delta_w', 'delta_ssm_norm_w': 'delta_w', 'delta_w_branch_sc': 'delta_w', 'delta_w_branch_ssm': 'delta_w', 'delta_w_out': 'delta_w', 'delta_norm_mlp': 'delta_w', 'delta_w_mlp1': 'delta_w', 'delta_w_mlp2': 'delta_w', 'delta_norm_final': 'delta_w', 'new_m_norm_mix': 'new_m', 'new_m_w_in': 'new_m', 'new_m_b_gate': 'new_m', 'new_m_sc_conv_w': 'new_m', 'new_m_ssm_conv_w': 'new_m', 'new_m_ssm_conv_b': 'new_m', 'new_m_dt_bias': 'new_m', 'new_m_A_log': 'new_m', 'new_m_D_skip': 'new_m', 'new_m_ssm_norm_w': 'new_m', 'new_m_w_branch_sc': 'new_m', 'new_m_w_branch_ssm': 'new_m', 'new_m_w_out': 'new_m', 'new_m_norm_mlp': 'new_m', 'new_m_w_mlp1': 'new_m', 'new_m_w_mlp2': 'new_m', 'new_m_norm_final': 'new_m', 'new_v_norm_mix': 'new_v', 'new_v_w_in': 'new_v', 'new_v_b_gate': 'new_v', 'new_v_sc_conv_w': 'new_v', 'new_v_ssm_conv_w': 'new_v', 'new_v_ssm_conv_b': 'new_v', 'new_v_dt_bias': 'new_v', 'new_v_A_log': 'new_v', 'new_v_D_skip': 'new_v', 'new_v_ssm_norm_w': 'new_v', 'new_v_w_branch_sc': 'new_v', 'new_v_w_branch_ssm': 'new_v', 'new_v_w_out': 'new_v', 'new_v_norm_mlp': 'new_v', 'new_v_w_mlp1': 'new_v', 'new_v_w_mlp2': 'new_v', 'new_v_norm_final': 'new_v'}


def _forward(args):
    return _fwd_reference(*[args[k] for k in FWD_PARAMS])


def _output_shape():
    out = _jax.eval_shape(lambda: _forward(_fwd_setup_inputs(0)))
    return out.shape, out.dtype

N_MICROBATCH = 1
ADAM_LR = 0.001
ADAM_B1 = 0.9
ADAM_B2 = 0.999
ADAM_EPS = 1e-08
ADAM_WD = 0.01
ADAM_STEP = 10
PER_EXAMPLE_BATCH_AXIS = {'x': 0, 'loss_target': 0}
SHARED_INPUTS = []
_WEIGHT_DTYPES = {'norm_mix': _jnp.float32, 'w_in': _jnp.float32, 'b_gate': _jnp.float32, 'sc_conv_w': _jnp.float32, 'ssm_conv_w': _jnp.float32, 'ssm_conv_b': _jnp.float32, 'dt_bias': _jnp.float32, 'A_log': _jnp.float32, 'D_skip': _jnp.float32, 'ssm_norm_w': _jnp.float32, 'w_branch_sc': _jnp.float32, 'w_branch_ssm': _jnp.float32, 'w_out': _jnp.float32, 'norm_mlp': _jnp.float32, 'w_mlp1': _jnp.float32, 'w_mlp2': _jnp.float32, 'norm_final': _jnp.float32}
MOMENT_SCALE = {'norm_mix': 1.379198e-01, 'w_in': 4.125096e-02, 'b_gate': 2.350433e-02, 'sc_conv_w': 5.879504e-02, 'ssm_conv_w': 3.118424e-02, 'ssm_conv_b': 4.572778e-02, 'dt_bias': 1.094738e-01, 'A_log': 2.255218e-01, 'D_skip': 2.154611e-01, 'ssm_norm_w': 4.275816e-02, 'w_branch_sc': 5.800095e-02, 'w_branch_ssm': 5.846757e-02, 'w_out': 8.246393e-02, 'norm_mlp': 1.045239e-01, 'w_mlp1': 5.042800e-02, 'w_mlp2': 9.472441e-02, 'norm_final': 1.613928e+01}


def _to_microbatches(a, axis):
    t = _jnp.moveaxis(a, axis, 0)
    t = t.reshape((N_MICROBATCH, t.shape[0] // N_MICROBATCH) + t.shape[1:])
    return _jnp.moveaxis(t, 1, axis + 1)


def setup_inputs(seed: int = 0) -> dict:
    inp = _fwd_setup_inputs(seed)
    key = _jax.random.fold_in(_jax.random.key(seed), 7919)
    shape, _ = _output_shape()
    out = dict(inp)
    out["loss_target"] = _jax.random.normal(_jax.random.fold_in(key, 0), shape, _jnp.float32)
    for i, name in enumerate(TWIN_WEIGHTS):
        w = inp[name].astype(_jnp.float32)
        if MOMENT_SCALE is None:
            s = _jnp.sqrt(_jnp.mean(_jnp.square(w)) + 1e-30)
        else:
            s = MOMENT_SCALE[name]
        km, kv = _jax.random.split(_jax.random.fold_in(key, i + 1))
        out[name] = w
        out["m_" + name] = s * _jax.random.normal(km, w.shape, _jnp.float32)
        out["v_" + name] = (s * s) * _jax.random.uniform(kv, w.shape, _jnp.float32, 0.5, 1.5)
    if N_MICROBATCH > 1:
        for name, axis in PER_EXAMPLE_BATCH_AXIS.items():
            out[name] = _to_microbatches(out[name], axis)
    return {'x': out['x'], 'norm_mix': out['norm_mix'], 'w_in': out['w_in'], 'b_gate': out['b_gate'], 'sc_conv_w': out['sc_conv_w'], 'ssm_conv_w': out['ssm_conv_w'], 'ssm_conv_b': out['ssm_conv_b'], 'dt_bias': out['dt_bias'], 'A_log': out['A_log'], 'D_skip': out['D_skip'], 'ssm_norm_w': out['ssm_norm_w'], 'w_branch_sc': out['w_branch_sc'], 'w_branch_ssm': out['w_branch_ssm'], 'w_out': out['w_out'], 'norm_mlp': out['norm_mlp'], 'w_mlp1': out['w_mlp1'], 'w_mlp2': out['w_mlp2'], 'norm_final': out['norm_final'], 'loss_target': out['loss_target'], 'm_norm_mix': out['m_norm_mix'], 'm_w_in': out['m_w_in'], 'm_b_gate': out['m_b_gate'], 'm_sc_conv_w': out['m_sc_conv_w'], 'm_ssm_conv_w': out['m_ssm_conv_w'], 'm_ssm_conv_b': out['m_ssm_conv_b'], 'm_dt_bias': out['m_dt_bias'], 'm_A_log': out['m_A_log'], 'm_D_skip': out['m_D_skip'], 'm_ssm_norm_w': out['m_ssm_norm_w'], 'm_w_branch_sc': out['m_w_branch_sc'], 'm_w_branch_ssm': out['m_w_branch_ssm'], 'm_w_out': out['m_w_out'], 'm_norm_mlp': out['m_norm_mlp'], 'm_w_mlp1': out['m_w_mlp1'], 'm_w_mlp2': out['m_w_mlp2'], 'm_norm_final': out['m_norm_final'], 'v_norm_mix': out['v_norm_mix'], 'v_w_in': out['v_w_in'], 'v_b_gate': out['v_b_gate'], 'v_sc_conv_w': out['v_sc_conv_w'], 'v_ssm_conv_w': out['v_ssm_conv_w'], 'v_ssm_conv_b': out['v_ssm_conv_b'], 'v_dt_bias': out['v_dt_bias'], 'v_A_log': out['v_A_log'], 'v_D_skip': out['v_D_skip'], 'v_ssm_norm_w': out['v_ssm_norm_w'], 'v_w_branch_sc': out['v_w_branch_sc'], 'v_w_branch_ssm': out['v_w_branch_ssm'], 'v_w_out': out['v_w_out'], 'v_norm_mlp': out['v_norm_mlp'], 'v_w_mlp1': out['v_w_mlp1'], 'v_w_mlp2': out['v_w_mlp2'], 'v_norm_final': out['v_norm_final']}


def _loss(weights, diff, rest, loss_target):
    with _jax.named_scope("forward"):
        args = {**rest, TWIN_DIFF_INPUT: diff, **{k: w.astype(_WEIGHT_DTYPES[k]) for k, w in weights.items()}}
        y = _forward(args)
    with _jax.named_scope("loss_head"):
        err = _jnp.square(y.astype(_jnp.float32) - loss_target)
        return 0.5 * _jnp.sum(_jnp.mean(err, axis=-1)) if err.ndim else 0.5 * err


def _adamw(w, g, m, v):
    m = ADAM_B1 * m + (1.0 - ADAM_B1) * g
    v = ADAM_B2 * v + (1.0 - ADAM_B2) * _jnp.square(g)
    m_hat = m / (1.0 - ADAM_B1 ** ADAM_STEP)
    v_hat = v / (1.0 - ADAM_B2 ** ADAM_STEP)
    delta = -ADAM_LR * (m_hat / (_jnp.sqrt(v_hat) + ADAM_EPS) + ADAM_WD * w)
    return delta, m, v


def reference(x, norm_mix, w_in, b_gate, sc_conv_w, ssm_conv_w, ssm_conv_b, dt_bias, A_log, D_skip, ssm_norm_w, w_branch_sc, w_branch_ssm, w_out, norm_mlp, w_mlp1, w_mlp2, norm_final, loss_target, m_norm_mix, m_w_in, m_b_gate, m_sc_conv_w, m_ssm_conv_w, m_ssm_conv_b, m_dt_bias, m_A_log, m_D_skip, m_ssm_norm_w, m_w_branch_sc, m_w_branch_ssm, m_w_out, m_norm_mlp, m_w_mlp1, m_w_mlp2, m_norm_final, v_norm_mix, v_w_in, v_b_gate, v_sc_conv_w, v_ssm_conv_w, v_ssm_conv_b, v_dt_bias, v_A_log, v_D_skip, v_ssm_norm_w, v_w_branch_sc, v_w_branch_ssm, v_w_out, v_norm_mlp, v_w_mlp1, v_w_mlp2, v_norm_final):
    given = dict(x=x, norm_mix=norm_mix, w_in=w_in, b_gate=b_gate, sc_conv_w=sc_conv_w, ssm_conv_w=ssm_conv_w, ssm_conv_b=ssm_conv_b, dt_bias=dt_bias, A_log=A_log, D_skip=D_skip, ssm_norm_w=ssm_norm_w, w_branch_sc=w_branch_sc, w_branch_ssm=w_branch_ssm, w_out=w_out, norm_mlp=norm_mlp, w_mlp1=w_mlp1, w_mlp2=w_mlp2, norm_final=norm_final, loss_target=loss_target, m_norm_mix=m_norm_mix, m_w_in=m_w_in, m_b_gate=m_b_gate, m_sc_conv_w=m_sc_conv_w, m_ssm_conv_w=m_ssm_conv_w, m_ssm_conv_b=m_ssm_conv_b, m_dt_bias=m_dt_bias, m_A_log=m_A_log, m_D_skip=m_D_skip, m_ssm_norm_w=m_ssm_norm_w, m_w_branch_sc=m_w_branch_sc, m_w_branch_ssm=m_w_branch_ssm, m_w_out=m_w_out, m_norm_mlp=m_norm_mlp, m_w_mlp1=m_w_mlp1, m_w_mlp2=m_w_mlp2, m_norm_final=m_norm_final, v_norm_mix=v_norm_mix, v_w_in=v_w_in, v_b_gate=v_b_gate, v_sc_conv_w=v_sc_conv_w, v_ssm_conv_w=v_ssm_conv_w, v_ssm_conv_b=v_ssm_conv_b, v_dt_bias=v_dt_bias, v_A_log=v_A_log, v_D_skip=v_D_skip, v_ssm_norm_w=v_ssm_norm_w, v_w_branch_sc=v_w_branch_sc, v_w_branch_ssm=v_w_branch_ssm, v_w_out=v_w_out, v_norm_mlp=v_norm_mlp, v_w_mlp1=v_w_mlp1, v_w_mlp2=v_w_mlp2, v_norm_final=v_norm_final)
    weights = {n: given[n] for n in TWIN_WEIGHTS}
    shared = {n: given[n] for n in SHARED_INPUTS}
    per_example = {n: given[n] for n in ['x']}
    grad_fn = _jax.value_and_grad(_loss, argnums=(0, 1))

    def one_microbatch(ex, loss_target):
        ex = dict(ex)
        diff = ex.pop(TWIN_DIFF_INPUT)
        return grad_fn(weights, diff, {**shared, **ex}, loss_target)

    if N_MICROBATCH == 1:
        loss, (grad_w, grad_x) = one_microbatch(per_example, given["loss_target"])
    else:
        def body(carry, xs):
            loss_sum, grad_sum = carry
            l_k, (gw_k, gx_k) = one_microbatch(xs[0], xs[1])
            with _jax.named_scope("update"):
                return (loss_sum + l_k, _jax.tree.map(_jnp.add, grad_sum, gw_k)), gx_k

        init = (_jnp.zeros((), _jnp.float32), _jax.tree.map(_jnp.zeros_like, weights))
        (loss, grad_w), grad_x = _jax.lax.scan(body, init, (per_example, given["loss_target"]))
    with _jax.named_scope("update"):
        delta_w, new_m, new_v = {}, {}, {}
        for n in TWIN_WEIGHTS:
            delta_w[n], new_m[n], new_v[n] = _adamw(weights[n], grad_w[n], given["m_" + n], given["v_" + n])
    return (loss, grad_x, *[grad_w[n] for n in TWIN_WEIGHTS], *[delta_w[n] for n in TWIN_WEIGHTS],
            *[new_m[n] for n in TWIN_WEIGHTS], *[new_v[n] for n in TWIN_WEIGHTS])
```

```python
import functools

import jax
import jax.numpy as jnp
from jax import lax
from jax.experimental import pallas as pl
from jax.experimental.pallas import tpu as pltpu

F32 = jnp.float32
BF16 = jnp.bfloat16
MESH = pl.DeviceIdType.MESH
HBM = pltpu.HBM

D = 1024
INNER = 2048
HD = 64
NH = 32
NG = 8
NS = 128
Q = 128
XBC = 4096
DFF = 4096
EPS = 1e-6
W_SHARD = 2824
NCW = 11520
PIECE = 3072
PMAIN = 2816
C_Z, C_XBC, C_GATE, C_DT = 3072, 5120, 9216, 11264
SMALL_ROWS = 256
VMEM_LIMIT = 56 * 1024 * 1024

ADAM_LR, ADAM_B1, ADAM_B2, ADAM_EPS, ADAM_WD, ADAM_STEP = 0.001, 0.9, 0.999, 1e-08, 0.01, 10


def _cp(sem=None, vmem=VMEM_LIMIT):
    return pltpu.CompilerParams(dimension_semantics=sem, vmem_limit_bytes=vmem)


def _sigmoid(v):
    return 1.0 / (1.0 + jnp.exp(-v))


_DIMS = {"nn": (((1,), (0,)), ((), ())), "nt": (((1,), (1,)), ((), ())), "tn": (((0,), (0,)), ((), ()))}


def _matmul(a, b, mode, out_dtype, tm, tn, tk, name, epi=None, extra=None, n_outer=False):
    if mode == "tn":
        K, M = a.shape
    else:
        M, K = a.shape
    N = b.shape[0] if mode == "nt" else b.shape[1]
    tm, tn, tk = min(tm, M), min(tn, N), min(tk, K)
    assert M % tm == 0 and N % tn == 0 and K % tk == 0, (name, M, N, K, tm, tn, tk)
    nm, nn, nk = M // tm, N // tn, K // tk
    dims = _DIMS[mode]

    def ij(p0, p1):
        return (p1, p0) if n_outer else (p0, p1)

    if mode == "tn":
        a_spec = pl.BlockSpec((tk, tm), lambda p0, p1, k: (k, ij(p0, p1)[0]))
    else:
        a_spec = pl.BlockSpec((tm, tk), lambda p0, p1, k: (ij(p0, p1)[0], k))
    if mode == "nt":
        b_spec = pl.BlockSpec((tn, tk), lambda p0, p1, k: (ij(p0, p1)[1], k))
    else:
        b_spec = pl.BlockSpec((tk, tn), lambda p0, p1, k: (k, ij(p0, p1)[1]))
    o_spec = pl.BlockSpec((tm, tn), lambda p0, p1, k: ij(p0, p1))
    in_specs = [a_spec, b_spec]
    args = [a, b]
    if epi in ("res", "drelu"):
        in_specs.append(o_spec)
        args.append(extra)
    if epi == "relu2":
        out_shape = (jax.ShapeDtypeStruct((M, N), F32), jax.ShapeDtypeStruct((M, N), BF16))
        out_specs = (o_spec, o_spec)
    else:
        out_shape = jax.ShapeDtypeStruct((M, N), out_dtype)
        out_specs = o_spec

    def kern(*refs):
        a_ref, b_ref = refs[0], refs[1]
        e_ref = refs[2] if epi in ("res", "drelu") else None
        acc = refs[-1]
        outs = refs[(3 if e_ref is not None else 2):-1]
        k = pl.program_id(2)

        @pl.when(k == 0)
        def _():
            acc[...] = jnp.zeros_like(acc)

        acc[...] += lax.dot_general(a_ref[...].astype(BF16), b_ref[...].astype(BF16), dims,
                                    preferred_element_type=F32)

        @pl.when(k == nk - 1)
        def _():
            r = acc[...]
            if epi is None:
                outs[0][...] = r.astype(out_dtype)
            elif epi == "res":
                outs[0][...] = (r + e_ref[...]).astype(out_dtype)
            elif epi == "relu2":
                outs[0][...] = r
                t = jnp.maximum(r, 0.0)
                outs[1][...] = (t * t).astype(BF16)
            else:
                outs[0][...] = (r * (2.0 * jnp.maximum(e_ref[...], 0.0))).astype(out_dtype)

    grid = (nn, nm, nk) if n_outer else (nm, nn, nk)
    return pl.pallas_call(
        kern, grid=grid, in_specs=in_specs, out_specs=out_specs, out_shape=out_shape,
        scratch_shapes=[pltpu.VMEM((tm, tn), F32)], name=name,
        compiler_params=_cp(("parallel", "parallel", "arbitrary")),
    )(*args)


def _rms_fwd(x, w, name, tl=256):
    L = x.shape[0]

    def kern(x_ref, w_ref, o_ref):
        xv = x_ref[...]
        r = lax.rsqrt(jnp.mean(xv * xv, axis=-1, keepdims=True) + EPS)
        o_ref[...] = ((xv * r) * w_ref[...]).astype(BF16)

    row = pl.BlockSpec((tl, D), lambda i: (i, 0))
    return pl.pallas_call(
        kern, grid=(L // tl,), in_specs=[row, pl.BlockSpec((1, D), lambda i: (0, 0))], out_specs=row,
        out_shape=jax.ShapeDtypeStruct((L, D), BF16), name=name, compiler_params=_cp(("parallel",)),
    )(x, w.reshape(1, D))


def _rms_bwd(dy, x, w, res, name, tl=256):
    L = x.shape[0]

    def kern(dy_ref, x_ref, w_ref, res_ref, dx_ref, gw_ref):
        @pl.when(pl.program_id(0) == 0)
        def _():
            gw_ref[...] = jnp.zeros_like(gw_ref)

        xv = x_ref[...]
        dyv = dy_ref[...]
        r = lax.rsqrt(jnp.mean(xv * xv, axis=-1, keepdims=True) + EPS)
        xn = xv * r
        gw_ref[...] += jnp.broadcast_to(jnp.sum(dyv * xn, axis=0, keepdims=True), (8, D))
        dxn = dyv * w_ref[...]
        dx_ref[...] = res_ref[...] + r * (dxn - xn * jnp.mean(dxn * xn, axis=-1, keepdims=True))

    row = pl.BlockSpec((tl, D), lambda i: (i, 0))
    return pl.pallas_call(
        kern, grid=(L // tl,), in_specs=[row, row, pl.BlockSpec((1, D), lambda i: (0, 0)), row],
        out_specs=(row, pl.BlockSpec((8, D), lambda i: (0, 0))),
        out_shape=(jax.ShapeDtypeStruct((L, D), F32), jax.ShapeDtypeStruct((8, D), F32)),
        name=name, compiler_params=_cp(("arbitrary",)),
    )(dy, x, w.reshape(1, D), res)


def _final(x2, w, tgt, tl=256):
    L = x2.shape[0]

    def kern(x_ref, w_ref, t_ref, dx_ref, gw_ref, loss_ref):
        @pl.when(pl.program_id(0) == 0)
        def _():
            gw_ref[...] = jnp.zeros_like(gw_ref)
            loss_ref[...] = jnp.zeros_like(loss_ref)

        xv = x_ref[...]
        r = lax.rsqrt(jnp.mean(xv * xv, axis=-1, keepdims=True) + EPS)
        xn = xv * r
        e = xn * w_ref[...] - t_ref[...]
        per_tok = jnp.mean(e * e, axis=-1, keepdims=True)
        loss_ref[...] += 0.5 * jnp.sum(per_tok)
        dyv = e * (1.0 / D)
        gw_ref[...] += jnp.broadcast_to(jnp.sum(dyv * xn, axis=0, keepdims=True), (8, D))
        dxn = dyv * w_ref[...]
        dx_ref[...] = r * (dxn - xn * jnp.mean(dxn * xn, axis=-1, keepdims=True))

    row = pl.BlockSpec((tl, D), lambda i: (i, 0))
    return pl.pallas_call(
        kern, grid=(L // tl,), in_specs=[row, pl.BlockSpec((1, D), lambda i: (0, 0)), row],
        out_specs=(row, pl.BlockSpec((8, D), lambda i: (0, 0)), pl.BlockSpec((8, 128), lambda i: (0, 0))),
        out_shape=(jax.ShapeDtypeStruct((L, D), F32), jax.ShapeDtypeStruct((8, D), F32),
                   jax.ShapeDtypeStruct((8, 128), F32)),
        name="final_norm_loss", compiler_params=_cp(("arbitrary",)),
    )(x2, w.reshape(1, D), tgt)


def _down(v, k):
    if k == 0:
        return v
    t = lax.broadcasted_iota(jnp.int32, v.shape, 0)
    return jnp.where(t >= k, pltpu.roll(v, k, axis=0), 0.0)


def _up(v, k):
    if k == 0:
        return v
    n = v.shape[0]
    t = lax.broadcasted_iota(jnp.int32, v.shape, 0)
    return jnp.where(t < n - k, pltpu.roll(v, n - k, axis=0), 0.0)


TW = 256


def _sc_fwd(proj, cw):
    L = proj.shape[0]
    nb = D // TW

    def kern(b_ref, c_ref, x_ref, w_ref, o_ref):
        u = c_ref[...] * x_ref[...]
        w = w_ref[...]
        cv = w[0:1] * _down(u, 2) + w[1:2] * _down(u, 1) + w[2:3] * u
        o_ref[...] = (b_ref[...] * cv).astype(BF16)

    col = lambda off: pl.BlockSpec((L, TW), lambda j: (0, off + j))
    return pl.pallas_call(
        kern, grid=(nb,), in_specs=[col(0), col(nb), col(2 * nb), pl.BlockSpec((8, TW), lambda j: (0, j))],
        out_specs=pl.BlockSpec((L, TW), lambda j: (0, j)), out_shape=jax.ShapeDtypeStruct((L, D), BF16),
        name="sc_fwd", compiler_params=_cp(("parallel",)),
    )(proj, proj, proj, cw)


def _sc_bwd(dya, proj, cw, dproj):
    L = proj.shape[0]
    nb = D // TW

    def kern(d_ref, b_ref, c_ref, x_ref, w_ref, _, dp_ref, gw_ref):
        sec = pl.program_id(1)
        cs, xs, dyv = c_ref[...], x_ref[...], d_ref[...]
        w = w_ref[...]
        u = cs * xs
        u1, u2 = _down(u, 1), _down(u, 2)
        cv = w[0:1] * u2 + w[1:2] * u1 + w[2:3] * u
        dcv = dyv * b_ref[...]
        du = w[2:3] * dcv + w[1:2] * _up(dcv, 1) + w[0:1] * _up(dcv, 2)
        g0 = jnp.sum(dcv * u2, axis=0, keepdims=True)
        g1 = jnp.sum(dcv * u1, axis=0, keepdims=True)
        g2 = jnp.sum(dcv * u, axis=0, keepdims=True)
        row = lax.broadcasted_iota(jnp.int32, (8, TW), 0)
        gw_ref[...] = jnp.where(row == 0, g0, jnp.where(row == 1, g1, jnp.where(row == 2, g2, 0.0)))
        out = jnp.where(sec == 0, dyv * cv, jnp.where(sec == 1, du * xs, du * cs))
        dp_ref[...] = out.astype(BF16)

    col = lambda off: pl.BlockSpec((L, TW), lambda j, s: (0, off + j))
    return pl.pallas_call(
        kern, grid=(nb, 3),
        in_specs=[col(0), col(0), col(nb), col(2 * nb), pl.BlockSpec((8, TW), lambda j, s: (0, j)),
                  pl.BlockSpec(memory_space=pl.ANY)],
        out_specs=(pl.BlockSpec((L, TW), lambda j, s: (0, s * nb + j)), pl.BlockSpec((8, TW), lambda j, s: (0, j))),
        out_shape=(jax.ShapeDtypeStruct(dproj.shape, BF16), jax.ShapeDtypeStruct((8, D), F32)),
        input_output_aliases={5: 0}, name="sc_bwd", compiler_params=_cp(("parallel", "arbitrary")),
    )(dya, proj, proj, proj, cw, dproj)


def _ssm_conv_fwd(proj, cw4):
    L = proj.shape[0]
    off = C_XBC // TW

    def kern(r_ref, w_ref, o_ref):
        raw = r_ref[...]
        w = w_ref[...]
        c4 = w[0:1] * _down(raw, 3) + w[1:2] * _down(raw, 2) + w[2:3] * _down(raw, 1) + w[3:4] * raw + w[4:5]
        o_ref[...] = c4 * _sigmoid(c4)

    return pl.pallas_call(
        kern, grid=(XBC // TW,),
        in_specs=[pl.BlockSpec((L, TW), lambda j: (0, off + j)), pl.BlockSpec((8, TW), lambda j: (0, j))],
        out_specs=pl.BlockSpec((L, TW), lambda j: (0, j)), out_shape=jax.ShapeDtypeStruct((L, XBC), F32),
        name="ssm_conv_fwd", compiler_params=_cp(("parallel",)),
    )(proj, cw4)


def _ssm_conv_bwd(dx, proj, cw4, dproj, col0, name):
    L, width = dx.shape
    off_p = (C_XBC + col0) // TW
    off_w = col0 // TW

    def kern(d_ref, r_ref, w_ref, _, dp_ref, gw_ref):
        raw = r_ref[...]
        w = w_ref[...]
        r1, r2, r3 = _down(raw, 1), _down(raw, 2), _down(raw, 3)
        c4 = w[0:1] * r3 + w[1:2] * r2 + w[2:3] * r1 + w[3:4] * raw + w[4:5]
        sg = _sigmoid(c4)
        dc4 = d_ref[...] * (sg * (1.0 + c4 * (1.0 - sg)))
        draw = w[3:4] * dc4 + w[2:3] * _up(dc4, 1) + w[1:2] * _up(dc4, 2) + w[0:1] * _up(dc4, 3)
        dp_ref[...] = draw.astype(BF16)
        gs = [jnp.sum(dc4 * r3, axis=0, keepdims=True), jnp.sum(dc4 * r2, axis=0, keepdims=True),
              jnp.sum(dc4 * r1, axis=0, keepdims=True), jnp.sum(dc4 * raw, axis=0, keepdims=True),
              jnp.sum(dc4, axis=0, keepdims=True)]
        row = lax.broadcasted_iota(jnp.int32, (8, TW), 0)
        acc = jnp.zeros((8, TW), F32)
        for k, gk in enumerate(gs):
            acc = jnp.where(row == k, gk, acc)
        gw_ref[...] = acc

    return pl.pallas_call(
        kern, grid=(width // TW,),
        in_specs=[pl.BlockSpec((L, TW), lambda j: (0, j)), pl.BlockSpec((L, TW), lambda j: (0, off_p + j)),
                  pl.BlockSpec((8, TW), lambda j: (0, off_w + j)), pl.BlockSpec(memory_space=pl.ANY)],
        out_specs=(pl.BlockSpec((L, TW), lambda j: (0, off_p + j)), pl.BlockSpec((8, TW), lambda j: (0, j))),
        out_shape=(jax.ShapeDtypeStruct(dproj.shape, BF16), jax.ShapeDtypeStruct((8, width), F32)),
        input_output_aliases={3: 0}, name=name, compiler_params=_cp(("arbitrary",)),
    )(dx, proj, cw4, dproj)


def _split3(v):
    h1 = v.astype(BF16)
    r1 = v - h1.astype(F32)
    h2 = r1.astype(BF16)
    h3 = (r1 - h2.astype(F32)).astype(BF16)
    return h1, h2, h3


def _dot01(m01, v, dims=_DIMS["nn"], m_left=True):
    out = None
    for part in _split3(v):
        ops = (m01, part) if m_left else (part, m01)
        t = lax.dot_general(ops[0], ops[1], dims, preferred_element_type=F32)
        out = t if out is None else out + t
    return out


def _bdot(a, b, mode="nn"):
    return lax.dot_general(a.astype(BF16), b.astype(BF16), _DIMS[mode], preferred_element_type=F32)


def _softplus(v):
    return jnp.maximum(v, 0.0) + jnp.log1p(jnp.exp(-jnp.abs(v)))


def _dt_prep(proj, vec):
    L = proj.shape[0]

    def kern(p_ref, v_ref, dt_ref, cs_ref, sg_ref):
        v = v_ref[...]
        pre = p_ref[:, 0:128] + v[0:1]
        dt = _softplus(pre)
        sg_ref[...] = _sigmoid(pre)
        da = dt * (-jnp.exp(v[1:2]))
        ii = lax.broadcasted_iota(jnp.int32, (Q, Q), 0)
        jj = lax.broadcasted_iota(jnp.int32, (Q, Q), 1)
        ltri = (jj <= ii).astype(BF16)
        dt_ref[...] = dt
        cs_ref[...] = _dot01(ltri, da)

    blk = pl.BlockSpec((Q, 128), lambda c: (c, 0))
    return pl.pallas_call(
        kern, grid=(L // Q,),
        in_specs=[pl.BlockSpec((Q, 256), lambda c: (c, C_DT // 256)), pl.BlockSpec((8, 128), lambda c: (0, 0))],
        out_specs=(blk, blk, blk),
        out_shape=(jax.ShapeDtypeStruct((L, 128), F32),) * 3,
        name="dt_prep", compiler_params=_cp(("parallel",)),
    )(proj, vec)


def _head_masks():
    lane = lax.broadcasted_iota(jnp.int32, (1, 4 * HD), 1)
    return [((lane >= HD * j) & (lane < HD * (j + 1))) for j in range(4)]


def _expand4(v4, masks):
    R = v4.shape[0]
    out = jnp.zeros((R, 4 * HD), F32)
    for j in range(4):
        out = jnp.where(masks[j], jnp.broadcast_to(v4[:, j:j + 1], (R, 4 * HD)), out)
    return out


def _decay_matrix(cs_col, tri):
    colb = jnp.broadcast_to(cs_col, (Q, Q))
    return jnp.exp(jnp.where(tri, colb - colb.T, -jnp.inf))


def _ssd_fwd(xbc, dt4, cs4, vecg):
    L = xbc.shape[0]
    nc = L // Q

    def kern(x_ref, b_ref, c_ref, dt_ref, cs_ref, v_ref, y_ref, s_ref, S):
        c = pl.program_id(1)

        @pl.when(c == 0)
        def _():
            S[...] = jnp.zeros_like(S)

        masks = _head_masks()
        ii = lax.broadcasted_iota(jnp.int32, (Q, Q), 0)
        jj = lax.broadcasted_iota(jnp.int32, (Q, Q), 1)
        tri = jj <= ii
        dt4v, cs4v = dt_ref[0], cs_ref[0]
        dt_b, cs_b = _expand4(dt4v, masks), _expand4(cs4v, masks)
        d_b = _expand4(v_ref[0], masks)[1:2]
        cs_last = cs_b[Q - 1:Q, :]
        x4, bm, cm = x_ref[...], b_ref[...], c_ref[...]
        xdt = x4 * dt_b
        gm = _bdot(cm, bm, "nt")
        s4 = S[...]
        s_ref[0, 0] = s4
        y = _bdot(cm, s4) * jnp.exp(cs_b) + d_b * x4
        for j in range(4):
            mh = gm * _decay_matrix(cs4v[:, j:j + 1], tri)
            y = y + _bdot(mh, jnp.where(masks[j], xdt, 0.0))
        y_ref[...] = y
        S[...] = jnp.exp(cs_last) * s4 + _bdot(bm, xdt * jnp.exp(cs_last - cs_b), "tn")

    sc = pl.BlockSpec((1, Q, 128), lambda g, c: (g, c, 0))
    return pl.pallas_call(
        kern, grid=(NG, nc),
        in_specs=[pl.BlockSpec((Q, 256), lambda g, c: (c, g)),
                  pl.BlockSpec((Q, 128), lambda g, c: (c, INNER // 128 + g)),
                  pl.BlockSpec((Q, 128), lambda g, c: (c, (INNER + NG * NS) // 128 + g)),
                  sc, sc, pl.BlockSpec((1, 8, 128), lambda g, c: (g, 0, 0))],
        out_specs=(pl.BlockSpec((Q, 256), lambda g, c: (c, g)),
                   pl.BlockSpec((1, 1, NS, 256), lambda g, c: (g, c, 0, 0))),
        out_shape=(jax.ShapeDtypeStruct((L, INNER), F32), jax.ShapeDtypeStruct((NG, nc, NS, 256), F32)),
        scratch_shapes=[pltpu.VMEM((NS, 256), F32)], name="ssd_fwd",
        compiler_params=_cp(("parallel", "arbitrary")),
    )(xbc, xbc, xbc, dt4, cs4, vecg)


def _ssd_bwd(xbc, dt4, cs4, sg4, vecg, s_all, dy):
    L = xbc.shape[0]
    nc = L // Q

    def kern(x_ref, b_ref, c_ref, dt_ref, cs_ref, sg_ref, v_ref, s_ref, dy_ref,
             dx_ref, db_ref, dc_ref, ddt_ref, st_ref, dS):
        cc = pl.program_id(1)

        @pl.when(cc == 0)
        def _():
            dS[...] = jnp.zeros_like(dS)
            st_ref[...] = jnp.zeros_like(st_ref)

        masks = _head_masks()
        ii = lax.broadcasted_iota(jnp.int32, (Q, Q), 0)
        jj = lax.broadcasted_iota(jnp.int32, (Q, Q), 1)
        tri = jj <= ii
        utri = (jj >= ii).astype(BF16)
        li = lax.broadcasted_iota(jnp.int32, (4 * HD, 4 * HD), 0)
        lj = lax.broadcasted_iota(jnp.int32, (4 * HD, 4 * HD), 1)
        eblk = ((li // HD) == (lj // HD)).astype(BF16)
        lane128 = lax.broadcasted_iota(jnp.int32, (Q, 128), 1)

        dt4v, cs4v, sg4v = dt_ref[0], cs_ref[0], sg_ref[0]
        dt_b, cs_b = _expand4(dt4v, masks), _expand4(cs4v, masks)
        vv = _expand4(v_ref[0], masks)
        a_b = -jnp.exp(vv[0:1])
        d_b = vv[1:2]
        a4 = -jnp.exp(v_ref[0][0:1, :])
        cs_last = cs_b[Q - 1:Q, :]
        ecs = jnp.exp(cs_b)
        decay = jnp.exp(cs_last - cs_b)
        elast = jnp.exp(cs_last)
        x4, bm, cm, dyv = x_ref[...], b_ref[...], c_ref[...], dy_ref[...]
        s4 = s_ref[0, 0]
        dsn = dS[...]
        xdt = x4 * dt_b
        gm = _bdot(cm, bm, "nt")
        gmt = gm.T
        dye = dyv * ecs
        yoff = ecs * _bdot(cm, s4)
        t4 = _bdot(bm, dsn) * decay
        dxdt = t4
        dg = jnp.zeros((Q, Q), F32)
        rc = jnp.zeros((Q, 4 * HD), F32)
        for j in range(4):
            colb = jnp.broadcast_to(cs4v[:, j:j + 1], (Q, Q))
            seg = colb - colb.T
            lm = jnp.exp(jnp.where(tri, seg, -jnp.inf))
            lmt = jnp.exp(jnp.where(jj >= ii, -seg, -jnp.inf))
            mh = gm * lm
            mht = gmt * lmt
            dyh = jnp.where(masks[j], dyv, 0.0).astype(BF16)
            xh = jnp.where(masks[j], xdt, 0.0).astype(BF16)
            dxdt = dxdt + _bdot(mh, dyh, "tn")
            dmh = _bdot(dyh, xh, "nt")
            dmht = _bdot(xh, dyh, "nt")
            dg = dg + dmh * lm
            rs = jnp.sum(dmh * mh, axis=1, keepdims=True) - jnp.sum(dmht * mht, axis=1, keepdims=True)
            rc = jnp.where(masks[j], jnp.broadcast_to(rs, (Q, 4 * HD)), rc)
        xt = xdt * t4
        da_b = _dot01(eblk, dyv * yoff - xt, m_left=False) + rc
        tail = jnp.sum(xt, axis=0, keepdims=True) + elast * jnp.sum(s4 * dsn, axis=0, keepdims=True)
        tail = _dot01(eblk, jnp.broadcast_to(tail, (8, 4 * HD)), m_left=False)[0:1]
        dda_b = _dot01(utri, da_b) + tail
        ddt_b = dda_b * a_b + _dot01(eblk, dxdt * x4, m_left=False)
        ddt4 = jnp.zeros((Q, 128), F32)
        dda4 = jnp.zeros((Q, 128), F32)
        for j in range(4):
            ddt4 = jnp.where(lane128 == j, jnp.broadcast_to(ddt_b[:, HD * j:HD * j + 1], (Q, 128)), ddt4)
            dda4 = jnp.where(lane128 == j, jnp.broadcast_to(dda_b[:, HD * j:HD * j + 1], (Q, 128)), dda4)
        ddt_ref[0] = ddt4 * sg4v
        ga = jnp.sum(dda4 * dt4v * a4, axis=0, keepdims=True)
        gd_b = _dot01(eblk, jnp.broadcast_to(jnp.sum(dyv * x4, axis=0, keepdims=True), (8, 4 * HD)), m_left=False)[0:1]
        gd = jnp.zeros((1, 128), F32)
        for j in range(4):
            gd = jnp.where(lane128[0:1] == j, jnp.broadcast_to(gd_b[:, HD * j:HD * j + 1], (1, 128)), gd)
        row = lax.broadcasted_iota(jnp.int32, (8, 128), 0)
        st_ref[0] += jnp.where(row == 0, ga, jnp.where(row == 1, gd, 0.0))
        dx_ref[...] = d_b * dyv + dxdt * dt_b
        dc_ref[...] = _bdot(dg, bm) + _bdot(dye, s4, "nt")
        db_ref[...] = _bdot(dg, cm, "tn") + _bdot(xdt * decay, dsn, "nt")
        dS[...] = elast * dsn + _bdot(cm, dye, "tn")

    rv = lambda c: nc - 1 - c
    sc = pl.BlockSpec((1, Q, 128), lambda g, c: (g, rv(c), 0))
    return pl.pallas_call(
        kern, grid=(NG, nc),
        in_specs=[pl.BlockSpec((Q, 256), lambda g, c: (rv(c), g)),
                  pl.BlockSpec((Q, 128), lambda g, c: (rv(c), INNER // 128 + g)),
                  pl.BlockSpec((Q, 128), lambda g, c: (rv(c), (INNER + NG * NS) // 128 + g)),
                  sc, sc, sc, pl.BlockSpec((1, 8, 128), lambda g, c: (g, 0, 0)),
                  pl.BlockSpec((1, 1, NS, 256), lambda g, c: (g, rv(c), 0, 0)),
                  pl.BlockSpec((Q, 256), lambda g, c: (rv(c), g))],
        out_specs=(pl.BlockSpec((Q, 256), lambda g, c: (rv(c), g)),
                   pl.BlockSpec((Q, 128), lambda g, c: (rv(c), g)),
                   pl.BlockSpec((Q, 128), lambda g, c: (rv(c), g)),
                   pl.BlockSpec((1, Q, 128), lambda g, c: (g, rv(c), 0)),
                   pl.BlockSpec((1, 8, 128), lambda g, c: (g, 0, 0))),
        out_shape=(jax.ShapeDtypeStruct((L, INNER), F32), jax.ShapeDtypeStruct((L, NG * NS), F32),
                   jax.ShapeDtypeStruct((L, NG * NS), F32), jax.ShapeDtypeStruct((NG, L, 128), F32),
                   jax.ShapeDtypeStruct((NG, 8, 128), F32)),
        scratch_shapes=[pltpu.VMEM((NS, 256), F32)], name="ssd_bwd",
        compiler_params=_cp(("parallel", "arbitrary")),
    )(xbc, xbc, xbc, dt4, cs4, sg4, vecg, s_all, dy)


def _dt_bwd(ddt, dproj, tl=256):
    L = ddt.shape[0]

    def kern(d_ref, _, dp_ref, gs_ref):
        @pl.when(pl.program_id(0) == 0)
        def _():
            gs_ref[...] = jnp.zeros_like(gs_ref)

        d = d_ref[...]
        gs_ref[...] += jnp.broadcast_to(jnp.sum(d, axis=0, keepdims=True), (8, 128))
        dp_ref[...] = jnp.concatenate([d, jnp.zeros_like(d)], axis=1).astype(BF16)

    return pl.pallas_call(
        kern, grid=(L // tl,),
        in_specs=[pl.BlockSpec((tl, 128), lambda i: (i, 0)), pl.BlockSpec(memory_space=pl.ANY)],
        out_specs=(pl.BlockSpec((tl, 256), lambda i: (i, C_DT // 256)), pl.BlockSpec((8, 128), lambda i: (0, 0))),
        out_shape=(jax.ShapeDtypeStruct(dproj.shape, BF16), jax.ShapeDtypeStruct((8, 128), F32)),
        input_output_aliases={1: 0}, name="dt_bwd", compiler_params=_cp(("arbitrary",)),
    )(ddt, dproj)


GW = INNER // NG


def _gnorm_fwd(y, proj, w, tl=256):
    L = y.shape[0]
    zoff = C_Z // 1024

    def kern(y_ref, z_ref, w_ref, o_ref):
        z = z_ref[...]
        yz = y_ref[...] * (z * _sigmoid(z))
        wv = w_ref[...]
        for k in range(1024 // GW):
            sl = slice(GW * k, GW * (k + 1))
            v = yz[:, sl]
            rg = lax.rsqrt(jnp.mean(v * v, axis=-1, keepdims=True) + EPS)
            o_ref[:, sl] = ((v * rg) * wv[:, sl]).astype(BF16)

    blk = pl.BlockSpec((tl, 1024), lambda i, j: (i, j))
    return pl.pallas_call(
        kern, grid=(L // tl, 2),
        in_specs=[blk, pl.BlockSpec((tl, 1024), lambda i, j: (i, zoff + j)), pl.BlockSpec((1, 1024), lambda i, j: (0, j))],
        out_specs=blk, out_shape=jax.ShapeDtypeStruct((L, INNER), BF16), name="gnorm_fwd",
        compiler_params=_cp(("parallel", "parallel")),
    )(y, proj, w.reshape(1, INNER))


def _gnorm_bwd(dyb, y, proj, w, dproj, tl=256):
    L = y.shape[0]
    zoff = C_Z // 1024

    def kern(d_ref, y_ref, z_ref, w_ref, _, dy_ref, dp_ref, gw_ref):
        @pl.when(pl.program_id(1) == 0)
        def _():
            gw_ref[...] = jnp.zeros_like(gw_ref)

        z = z_ref[...]
        sg = _sigmoid(z)
        sz = z * sg
        yv = y_ref[...]
        yz = yv * sz
        dv = d_ref[...]
        wv = w_ref[...]
        for k in range(1024 // GW):
            sl = slice(GW * k, GW * (k + 1))
            v = yz[:, sl]
            rg = lax.rsqrt(jnp.mean(v * v, axis=-1, keepdims=True) + EPS)
            vn = v * rg
            dk = dv[:, sl]
            gw_ref[:, sl] += jnp.broadcast_to(jnp.sum(dk * vn, axis=0, keepdims=True), (8, GW))
            dvn = dk * wv[:, sl]
            dyz = rg * (dvn - vn * jnp.mean(dvn * vn, axis=-1, keepdims=True))
            dy_ref[:, sl] = dyz * sz[:, sl]
            dp_ref[:, sl] = (dyz * yv[:, sl] * (sg[:, sl] * (1.0 + z[:, sl] * (1.0 - sg[:, sl])))).astype(BF16)

    blk = pl.BlockSpec((tl, 1024), lambda j, i: (i, j))
    zblk = pl.BlockSpec((tl, 1024), lambda j, i: (i, zoff + j))
    return pl.pallas_call(
        kern, grid=(2, L // tl),
        in_specs=[blk, blk, zblk, pl.BlockSpec((1, 1024), lambda j, i: (0, j)), pl.BlockSpec(memory_space=pl.ANY)],
        out_specs=(blk, zblk, pl.BlockSpec((8, 1024), lambda j, i: (0, j))),
        out_shape=(jax.ShapeDtypeStruct((L, INNER), F32), jax.ShapeDtypeStruct(dproj.shape, BF16),
                   jax.ShapeDtypeStruct((8, INNER), F32)),
        input_output_aliases={4: 1}, name="gnorm_bwd", compiler_params=_cp(("parallel", "arbitrary")),
    )(dyb, y, proj, w.reshape(1, INNER), dproj)


def _merge_fwd(proj, bg, br_a, br_b, tl=256):
    L = proj.shape[0]
    goff = C_GATE // 1024

    def kern(g1_ref, g2_ref, b1_ref, b2_ref, a_ref, b_ref, o_ref):
        g1 = _sigmoid(g1_ref[...] + b1_ref[...])
        g2 = _sigmoid(g2_ref[...] + b2_ref[...])
        o_ref[...] = (g1 * a_ref[...] + g2 * b_ref[...]).astype(BF16)

    row = pl.BlockSpec((tl, 1024), lambda i: (i, 0))
    bg2 = bg.reshape(1, 2 * D)
    return pl.pallas_call(
        kern, grid=(L // tl,),
        in_specs=[pl.BlockSpec((tl, 1024), lambda i: (i, goff)), pl.BlockSpec((tl, 1024), lambda i: (i, goff + 1)),
                  pl.BlockSpec((1, 1024), lambda i: (0, 0)), pl.BlockSpec((1, 1024), lambda i: (0, 1)), row, row],
        out_specs=row, out_shape=jax.ShapeDtypeStruct((L, D), BF16), name="merge_fwd",
        compiler_params=_cp(("parallel",)),
    )(proj, proj, bg2, bg2, br_a, br_b)


def _merge_bwd(dm, proj, bg, br_a, br_b, dproj, tl=256):
    L = proj.shape[0]
    goff = C_GATE // 1024

    def kern(dm_ref, g_ref, b_ref, a_ref, bb_ref, _, dbr_ref, dp_ref, gb_ref):
        j = pl.program_id(0)

        @pl.when(pl.program_id(1) == 0)
        def _():
            gb_ref[...] = jnp.zeros_like(gb_ref)

        g = _sigmoid(g_ref[...] + b_ref[...])
        br = jnp.where(j == 0, a_ref[...], bb_ref[...])
        dmv = dm_ref[...]
        dbr_ref[0] = (dmv * g).astype(BF16)
        dgate = dmv * br * g * (1.0 - g)
        gb_ref[...] += jnp.broadcast_to(jnp.sum(dgate, axis=0, keepdims=True), (8, 1024))
        dp_ref[...] = dgate.astype(BF16)

    row = pl.BlockSpec((tl, 1024), lambda j, i: (i, 0))
    gblk = pl.BlockSpec((tl, 1024), lambda j, i: (i, goff + j))
    return pl.pallas_call(
        kern, grid=(2, L // tl),
        in_specs=[row, gblk, pl.BlockSpec((1, 1024), lambda j, i: (0, j)), row, row, pl.BlockSpec(memory_space=pl.ANY)],
        out_specs=(pl.BlockSpec((1, tl, 1024), lambda j, i: (j, i, 0)), gblk, pl.BlockSpec((8, 1024), lambda j, i: (0, j))),
        out_shape=(jax.ShapeDtypeStruct((2, L, D), BF16), jax.ShapeDtypeStruct(dproj.shape, BF16),
                   jax.ShapeDtypeStruct((8, 2 * D), F32)),
        input_output_aliases={5: 1}, name="merge_bwd", compiler_params=_cp(("parallel", "arbitrary")),
    )(dm, proj, bg.reshape(1, 2 * D), br_a, br_b, dproj)


def _coords():
    return lax.axis_index("x"), lax.axis_index("y"), lax.axis_index("c")


def _other_chips(sk):
    xk, yk = sk // 2, sk % 2
    return [((1 - xk, yk), 2 * (1 - xk) + yk), ((xk, 1 - yk), 2 * xk + 1 - yk), ((1 - xk, 1 - yk), 2 * (1 - xk) + 1 - yk)]


def _rows(start, size):
    return pl.ds(pl.multiple_of(start, size), size)


def _per_chip(fn):
    x, y, _ = _coords()
    s = 2 * x + y
    for sk in range(4):
        pl.when(s == sk)(functools.partial(fn, sk))


def _weight_windows(refs_in, refs_out):
    piece_r, w1_r, w2_r, wa_r, wb_r, wo_r, cw_r = refs_in
    wc_o, xt_o, w1_o, w2_o, wa_o, wb_o, wo_o, cw_o = refs_out
    return [
        (True, lambda hc: piece_r.at[_rows(512 * hc, 512), pl.ds(0, PMAIN)],
         lambda sc, hc: wc_o.at[_rows(512 * hc, 512), pl.ds(PMAIN * sc, PMAIN)]),
        (True, lambda hc: piece_r.at[_rows(512 * hc, 512), pl.ds(PMAIN, PIECE - PMAIN)],
         lambda sc, hc: xt_o.at[sc, _rows(512 * hc, 512), :]),
        (True, lambda hc: w1_r.at[_rows(512 * hc, 512), :],
         lambda sc, hc: w1_o.at[_rows(512 * hc, 512), pl.ds(1024 * sc, 1024)]),
        (True, lambda hc: w2_r.at[_rows(512 * hc, 512), :],
         lambda sc, hc: w2_o.at[_rows(1024 * sc + 512 * hc, 512), :]),
        (True, lambda hc: wa_r.at[_rows(128 * hc, 128), :],
         lambda sc, hc: wa_o.at[_rows(256 * sc + 128 * hc, 128), :]),
        (True, lambda hc: wb_r.at[_rows(256 * hc, 256), :],
         lambda sc, hc: wb_o.at[_rows(512 * sc + 256 * hc, 256), :]),
        (True, lambda hc: wo_r.at[_rows(128 * hc, 128), :],
         lambda sc, hc: wo_o.at[_rows(256 * sc + 128 * hc, 128), :]),
        (False, lambda hc: cw_r, lambda sc, hc: cw_o.at[sc]),
    ]


def _all_gather_weights(piece, w1s, w2s, was, wbs, wos, cws):
    n_t = 8

    def body(*refs):
        ins, outs = refs[:7], refs[7:15]
        send_sems, recv_sems, loc_sems = refs[15:]
        x, y, c = _coords()
        sib = (x, y, 1 - c)

        def run(sk):
            tens = _weight_windows(ins, outs)
            peers = _other_chips(sk)

            def rcopy(src, dst, k, dev):
                return pltpu.make_async_remote_copy(src_ref=src, dst_ref=dst, send_sem=send_sems.at[k],
                                                    recv_sem=recv_sems.at[k], device_id=dev, device_id_type=MESH)

            sends, locs = [], []
            for t, (split, src, dst) in enumerate(tens):
                lc = pltpu.make_async_copy(src(c), dst(sk, c), loc_sems.at[t])
                lc.start()
                locs.append(lc)
                for j, ((px, py), _) in enumerate(peers):
                    cp = rcopy(src(c), dst(sk, c), 7 * t + j, (px, py, c))
                    cp.start()
                    sends.append(cp)
                if split:
                    cp = rcopy(src(c), dst(sk, c), 7 * t + 3, sib)
                    cp.start()
                    sends.append(cp)
            for t, (split, src, dst) in enumerate(tens):
                for j, ((px, py), ps) in enumerate(peers):
                    got = dst(ps, c)
                    rcopy(got, got, 7 * t + j, (px, py, c)).wait_recv()
                    if split:
                        cp = rcopy(got, got, 7 * t + 4 + j, sib)
                        cp.start()
                        sends.append(cp)
            for t, (split, src, dst) in enumerate(tens):
                if split:
                    w = dst(sk, 1 - c)
                    rcopy(w, w, 7 * t + 3, sib).wait_recv()
                    for j, (_, ps) in enumerate(peers):
                        w = dst(ps, 1 - c)
                        rcopy(w, w, 7 * t + 4 + j, sib).wait_recv()
            for cp in sends:
                cp.wait_send()
            for lc in locs:
                lc.wait()

        _per_chip(run)

    L4 = lambda *shape: jax.ShapeDtypeStruct(shape, BF16)
    out_shape = (L4(D, NCW), L4(4, D, PIECE - PMAIN), L4(D, DFF), L4(DFF, D), L4(D, D), L4(INNER, D), L4(D, D),
                 jax.ShapeDtypeStruct((4,) + cws.shape, F32))
    hbm = pl.BlockSpec(memory_space=HBM)
    return pl.pallas_call(
        body, out_shape=out_shape, in_specs=[hbm] * 7, out_specs=(hbm,) * 8,
        scratch_shapes=[pltpu.SemaphoreType.DMA((7 * n_t,)), pltpu.SemaphoreType.DMA((7 * n_t,)),
                        pltpu.SemaphoreType.DMA((n_t,))],
        name="all_gather_weights", compiler_params=pltpu.CompilerParams(has_side_effects=True),
    )(piece, w1s, w2s, was, wbs, wos, cws)


def _fix_wc(wc, xt):
    def kern(_, x_ref, o_ref, buf, sem):
        j = pl.program_id(0)
        for k in range(3):
            @pl.when(j == k)
            def _(k=k):
                win = o_ref.at[:, pl.ds(PMAIN * (k + 1), 128)]
                cp = pltpu.make_async_copy(win, buf, sem)
                cp.start()
                cp.wait()
                buf[...] = (buf[...].astype(F32) + x_ref[0][:, 0:128].astype(F32)).astype(BF16)
                cp = pltpu.make_async_copy(buf, win, sem)
                cp.start()
                cp.wait()

        @pl.when(j == 3)
        def _():
            win = o_ref.at[:, pl.ds(C_DT, 128)]
            buf[...] = x_ref[0][:, 0:128]
            cp = pltpu.make_async_copy(buf, win, sem)
            cp.start()
            cp.wait()
            win = o_ref.at[:, pl.ds(C_DT + 128, 128)]
            buf[...] = x_ref[0][:, 128:256]
            cp = pltpu.make_async_copy(buf, win, sem)
            cp.start()
            cp.wait()

    return pl.pallas_call(
        kern, grid=(4,),
        in_specs=[pl.BlockSpec(memory_space=pl.ANY), pl.BlockSpec((1, D, 256), lambda j: (j, 0, 0))],
        out_specs=pl.BlockSpec(memory_space=pl.ANY), out_shape=jax.ShapeDtypeStruct(wc.shape, BF16),
        scratch_shapes=[pltpu.VMEM((D, 128), BF16), pltpu.SemaphoreType.DMA(())],
        input_output_aliases={0: 0}, name="fix_wc", compiler_params=_cp(("arbitrary",)),
    )(wc, xt)


def _grad_windows(g_refs):
    gc_r, g1_r, g2_r, ga_r, gb_r, go_r = g_refs
    return [
        lambda sc, hc: gc_r.at[_rows(512 * hc, 512), pl.ds(PMAIN * sc, PIECE)],
        lambda sc, hc: g1_r.at[_rows(512 * hc, 512), pl.ds(1024 * sc, 1024)],
        lambda sc, hc: g2_r.at[_rows(1024 * sc + 512 * hc, 512), :],
        lambda sc, hc: ga_r.at[_rows(256 * sc + 128 * hc, 128), :],
        lambda sc, hc: gb_r.at[_rows(512 * sc + 256 * hc, 256), :],
        lambda sc, hc: go_r.at[_rows(256 * sc + 128 * hc, 128), :],
    ]


HALF_SHAPES = [(512, PIECE), (512, 1024), (512, 1024), (128, 1024), (256, 1024), (128, 1024)]


def _rs_sibling(grads):
    def body(*refs):
        g_refs, ra_refs = refs[:6], refs[6:12]
        send_sems, recv_sems = refs[12:]
        x, y, c = _coords()
        sib = (x, y, 1 - c)
        wins = _grad_windows(g_refs)
        cps = []
        for t in range(6):
            for sc in range(4):
                cp = pltpu.make_async_remote_copy(
                    src_ref=wins[t](sc, 1 - c), dst_ref=ra_refs[t].at[sc], send_sem=send_sems.at[4 * t + sc],
                    recv_sem=recv_sems.at[4 * t + sc], device_id=sib, device_id_type=MESH)
                cp.start()
                cps.append(cp)
        for cp in cps:
            cp.wait()

    hbm = pl.BlockSpec(memory_space=HBM)
    return pl.pallas_call(
        body, out_shape=tuple(jax.ShapeDtypeStruct((4,) + s, BF16) for s in HALF_SHAPES),
        in_specs=[hbm] * 6, out_specs=(hbm,) * 6,
        scratch_shapes=[pltpu.SemaphoreType.DMA((24,)), pltpu.SemaphoreType.DMA((24,))],
        name="rs_sibling", compiler_params=pltpu.CompilerParams(has_side_effects=True),
    )(*grads)


def _chip_sum(g, ra, t, idx, name):
    rows, cols = HALF_SHAPES[t]
    tw = 256 if t < 2 else cols
    ncol = cols // tw

    def kern(idx_ref, g_ref, r_ref, hb_ref, hf_ref):
        v = g_ref[...].astype(F32) + r_ref[0].astype(F32)
        hb_ref[0] = v.astype(BF16)
        hf_ref[0] = v

    if t == 0:
        gmap = lambda sc, j, idx_ref: (idx_ref[1], (PMAIN // tw) * sc + j)
    elif t == 1:
        gmap = lambda sc, j, idx_ref: (idx_ref[1], (1024 // tw) * sc + j)
    else:
        gmap = lambda sc, j, idx_ref: (2 * sc + idx_ref[1], 0)
    omap = lambda sc, j, idx_ref: (sc, 0, j)
    grid_spec = pltpu.PrefetchScalarGridSpec(
        num_scalar_prefetch=1, grid=(4, ncol),
        in_specs=[pl.BlockSpec((rows, tw), gmap), pl.BlockSpec((1, rows, tw), omap)],
        out_specs=(pl.BlockSpec((1, rows, tw), omap), pl.BlockSpec((1, rows, tw), omap)))
    return pl.pallas_call(
        kern, grid_spec=grid_spec,
        out_shape=(jax.ShapeDtypeStruct((4, rows, cols), BF16), jax.ShapeDtypeStruct((4, rows, cols), F32)),
        name=name, compiler_params=_cp(("parallel", "parallel")),
    )(idx, g, ra)


def _rs_chips(hbs):
    def body(*refs):
        h_refs, rb_refs = refs[:6], refs[6:12]
        send_sems, recv_sems = refs[12:]
        _, _, c = _coords()

        def run(sk):
            cps = []
            for t in range(6):
                for j, ((px, py), ps) in enumerate(_other_chips(sk)):
                    cp = pltpu.make_async_remote_copy(
                        src_ref=h_refs[t].at[ps], dst_ref=rb_refs[t].at[j], send_sem=send_sems.at[3 * t + j],
                        recv_sem=recv_sems.at[3 * t + j], device_id=(px, py, c), device_id_type=MESH)
                    cp.start()
                    cps.append(cp)
            for cp in cps:
                cp.wait()

        _per_chip(run)

    hbm = pl.BlockSpec(memory_space=HBM)
    return pl.pallas_call(
        body, out_shape=tuple(jax.ShapeDtypeStruct((3,) + s, BF16) for s in HALF_SHAPES),
        in_specs=[hbm] * 6, out_specs=(hbm,) * 6,
        scratch_shapes=[pltpu.SemaphoreType.DMA((18,)), pltpu.SemaphoreType.DMA((18,))],
        name="rs_chips", compiler_params=pltpu.CompilerParams(has_side_effects=True),
    )(*hbs)


def _final_sum(hf, rb, t, idx, name):
    rows, cols = HALF_SHAPES[t]
    tw = 256 if t < 2 else cols

    def kern(idx_ref, h_ref, r_ref, o_ref):
        o_ref[...] = ((h_ref[0] + r_ref[0].astype(F32)) + r_ref[1].astype(F32)) + r_ref[2].astype(F32)

    grid_spec = pltpu.PrefetchScalarGridSpec(
        num_scalar_prefetch=1, grid=(cols // tw,),
        in_specs=[pl.BlockSpec((1, rows, tw), lambda j, idx_ref: (idx_ref[0], 0, j)),
                  pl.BlockSpec((3, rows, tw), lambda j, idx_ref: (0, 0, j))],
        out_specs=pl.BlockSpec((rows, tw), lambda j, idx_ref: (0, j)))
    return pl.pallas_call(
        kern, grid_spec=grid_spec, out_shape=jax.ShapeDtypeStruct((rows, cols), F32),
        name=name, compiler_params=_cp(("parallel",)),
    )(idx, hf, rb)


def _rs_share(fs):
    def body(*refs):
        f_refs, o_refs = refs[:6], refs[6:12]
        send_sems, recv_sems, loc_sems = refs[12:]
        x, y, c = _coords()
        sib = (x, y, 1 - c)
        cps, locs = [], []
        for t in range(6):
            rows = HALF_SHAPES[t][0]
            mine = o_refs[t].at[_rows(rows * c, rows), :]
            lc = pltpu.make_async_copy(f_refs[t], mine, loc_sems.at[t])
            lc.start()
            locs.append(lc)
            cp = pltpu.make_async_remote_copy(src_ref=f_refs[t], dst_ref=mine, send_sem=send_sems.at[t],
                                              recv_sem=recv_sems.at[t], device_id=sib, device_id_type=MESH)
            cp.start()
            cps.append(cp)
        for t in range(6):
            rows = HALF_SHAPES[t][0]
            theirs = o_refs[t].at[_rows(rows * (1 - c), rows), :]
            pltpu.make_async_remote_copy(src_ref=theirs, dst_ref=theirs, send_sem=send_sems.at[t],
                                         recv_sem=recv_sems.at[t], device_id=sib, device_id_type=MESH).wait_recv()
        for cp in cps:
            cp.wait_send()
        for lc in locs:
            lc.wait()

    hbm = pl.BlockSpec(memory_space=HBM)
    return pl.pallas_call(
        body, out_shape=tuple(jax.ShapeDtypeStruct((2 * r, cdim), F32) for r, cdim in HALF_SHAPES),
        in_specs=[hbm] * 6, out_specs=(hbm,) * 6,
        scratch_shapes=[pltpu.SemaphoreType.DMA((6,)), pltpu.SemaphoreType.DMA((6,)), pltpu.SemaphoreType.DMA((6,))],
        name="rs_share", compiler_params=pltpu.CompilerParams(has_side_effects=True),
    )(*fs)


def _small_all_gather(v):
    def body(v_ref, o_ref, send_sems, recv_sems, loc_sem):
        x, y, c = _coords()
        me = 4 * x + 2 * y + c
        lc = pltpu.make_async_copy(v_ref, o_ref.at[me], loc_sem)
        lc.start()
        cps = []
        for k in range(1, 8):
            fx, fy, fc = (k >> 2) & 1, (k >> 1) & 1, k & 1
            dev = ((1 - x) if fx else x, (1 - y) if fy else y, (1 - c) if fc else c)
            cp = pltpu.make_async_remote_copy(src_ref=v_ref, dst_ref=o_ref.at[me], send_sem=send_sems.at[k - 1],
                                              recv_sem=recv_sems.at[k - 1], device_id=dev, device_id_type=MESH)
            cp.start()
            cps.append((cp, 4 * dev[0] + 2 * dev[1] + dev[2]))
        for k, (cp, frm) in enumerate(cps):
            got = o_ref.at[frm]
            pltpu.make_async_remote_copy(src_ref=got, dst_ref=got, send_sem=send_sems.at[k], recv_sem=recv_sems.at[k],
                                         device_id=(x, y, c), device_id_type=MESH).wait_recv()
        for cp, _ in cps:
            cp.wait_send()
        lc.wait()

    hbm = pl.BlockSpec(memory_space=HBM)
    return pl.pallas_call(
        body, out_shape=jax.ShapeDtypeStruct((8,) + v.shape, F32), in_specs=[hbm], out_specs=hbm,
        scratch_shapes=[pltpu.SemaphoreType.DMA((7,)), pltpu.SemaphoreType.DMA((7,)), pltpu.SemaphoreType.DMA(())],
        name="small_all_gather", compiler_params=pltpu.CompilerParams(has_side_effects=True),
    )(v)


def _sum8(v):
    def kern(v_ref, o_ref):
        acc = v_ref[0]
        for k in range(1, 8):
            acc = acc + v_ref[k]
        o_ref[...] = acc

    return pl.pallas_call(kern, out_shape=jax.ShapeDtypeStruct(v.shape[1:], F32), name="small_sum")(v)


def _adamw(w, g, m, v, name, tr=128):
    R, C = w.shape
    tr = min(tr, R)
    assert R % tr == 0

    def kern(w_ref, g_ref, m_ref, v_ref, d_ref, mo_ref, vo_ref):
        gv = g_ref[...]
        mn = ADAM_B1 * m_ref[...] + (1.0 - ADAM_B1) * gv
        vn = ADAM_B2 * v_ref[...] + (1.0 - ADAM_B2) * (gv * gv)
        m_hat = mn / (1.0 - ADAM_B1 ** ADAM_STEP)
        v_hat = vn / (1.0 - ADAM_B2 ** ADAM_STEP)
        d_ref[...] = -ADAM_LR * (m_hat / (jnp.sqrt(v_hat) + ADAM_EPS) + ADAM_WD * w_ref[...])
        mo_ref[...] = mn
        vo_ref[...] = vn

    blk = pl.BlockSpec((tr, C), lambda i: (i, 0))
    sd = jax.ShapeDtypeStruct((R, C), F32)
    return pl.pallas_call(kern, grid=(R // tr,), in_specs=[blk] * 4, out_specs=(blk,) * 3, out_shape=(sd,) * 3,
                          name=name, compiler_params=_cp(("parallel",)))(w, g, m, v)


def _to_piece(w, s):
    gen = lax.dynamic_update_slice(jnp.zeros((D, PIECE), w.dtype), w, (0, 8 * s))
    z = lambda n: jnp.zeros((D, n), w.dtype)
    last = jnp.concatenate([z(24), w[:, :744], w[:, 776:], w[:, 744:776], z(PIECE - 24 - W_SHARD)], axis=1)
    return jnp.where(s == 3, last, gen)


def _from_piece(p, s):
    gen = lax.dynamic_slice(p, (0, 8 * s), (D, W_SHARD))
    last = jnp.concatenate([p[:, 24:768], p[:, 2816:2848], p[:, 768:2816]], axis=1)
    return jnp.where(s == 3, last, gen)


_SMALL = [("norm_mix", 1024), ("b_gate", 2048), ("ssm_conv_b", 4096), ("dt_bias", 32), ("A_log", 32), ("D_skip", 32),
          ("ssm_norm_w", 2048), ("norm_mlp", 1024), ("norm_final", 1024), ("sc_conv_w", 3072), ("ssm_conv_w", 16384),
          ("loss", 1)]


def _pack(vals, table, rows):
    parts = []
    for name, n in table:
        v = vals[name].reshape(-1).astype(F32)
        pad = (-n) % 128
        parts.append(jnp.pad(v, (0, pad)) if pad else v)
    flat = jnp.concatenate(parts)
    return jnp.pad(flat, (0, rows * 128 - flat.shape[0])).reshape(rows, 128)


def _unpack(arr, table):
    flat = arr.reshape(-1)
    out, off = {}, 0
    for name, n in table:
        out[name] = flat[off:off + n]
        off += n + ((-n) % 128)
    return out


def kernel(x, norm_mix, w_in, b_gate, sc_conv_w, ssm_conv_w, ssm_conv_b, dt_bias, A_log, D_skip, ssm_norm_w, w_branch_sc, w_branch_ssm, w_out, norm_mlp, w_mlp1, w_mlp2, norm_final, loss_target, m_norm_mix, m_w_in, m_b_gate, m_sc_conv_w, m_ssm_conv_w, m_ssm_conv_b, m_dt_bias, m_A_log, m_D_skip, m_ssm_norm_w, m_w_branch_sc, m_w_branch_ssm, m_w_out, m_norm_mlp, m_w_mlp1, m_w_mlp2, m_norm_final, v_norm_mix, v_w_in, v_b_gate, v_sc_conv_w, v_ssm_conv_w, v_ssm_conv_b, v_dt_bias, v_A_log, v_D_skip, v_ssm_norm_w, v_w_branch_sc, v_w_branch_ssm, v_w_out, v_norm_mlp, v_w_mlp1, v_w_mlp2, v_norm_final):
    L = x.shape[1]
    nc = L // Q
    xi, yi, ci = lax.axis_index("x"), lax.axis_index("y"), lax.axis_index("c")
    s = 2 * xi + yi
    idx = jnp.stack([s, ci]).astype(jnp.int32)
    x0 = x.reshape(L, D)
    tgt = loss_target.reshape(L, D)

    piece = _to_piece(w_in, s).astype(BF16)
    cws = jnp.zeros((8, 1280), F32)
    cws = cws.at[0:3, 0:256].set(sc_conv_w).at[0:4, 256:1280].set(ssm_conv_w)
    wc, xt, w1, w2, wa, wb, wo, cw_all = _all_gather_weights(
        piece, w_mlp1.astype(BF16), w_mlp2.astype(BF16), w_branch_sc.astype(BF16), w_branch_ssm.astype(BF16),
        w_out.astype(BF16), cws)
    wc = _fix_wc(wc, xt)
    sc_w_full = jnp.concatenate([cw_all[k, :, 0:256] for k in range(4)], axis=1)
    ssm_w_full = jnp.concatenate([cw_all[k, :, 256:1280] for k in range(4)], axis=1)
    cw4 = ssm_w_full.at[4].set(ssm_conv_b)
    vec = jnp.zeros((8, 128), F32).at[0, :NH].set(dt_bias).at[1, :NH].set(A_log)
    vecg = jnp.zeros((NG, 8, 128), F32).at[:, 0, :4].set(A_log.reshape(NG, 4)).at[:, 1, :4].set(D_skip.reshape(NG, 4))

    h = _rms_fwd(x0, norm_mix, "rms_mix")
    proj = _matmul(h, wc, "nn", F32, 512, 1280, 1024, "in_proj", n_outer=True)
    ya = _sc_fwd(proj, sc_w_full)
    xbc = _ssm_conv_fwd(proj, cw4)
    dt, cs, sg = _dt_prep(proj, vec)
    per_group = lambda a: jnp.pad(a[:, :NH].reshape(L, NG, 4).transpose(1, 0, 2), ((0, 0), (0, 0), (0, 124)))
    dt4, cs4, sg4 = per_group(dt), per_group(cs), per_group(sg)
    y, s_all = _ssd_fwd(xbc, dt4, cs4, vecg)
    yb = _gnorm_fwd(y, proj, ssm_norm_w)
    br_a = _matmul(ya, wa, "nn", F32, 512, 1024, 1024, "branch_sc")
    br_b = _matmul(yb, wb, "nn", F32, 512, 1024, 2048, "branch_ssm")
    merged = _merge_fwd(proj, b_gate, br_a, br_b)
    x1 = _matmul(merged, wo, "nn", F32, 512, 1024, 1024, "out_proj", epi="res", extra=x0)
    h2 = _rms_fwd(x1, norm_mlp, "rms_mlp")
    a1, rl = _matmul(h2, w1, "nn", F32, 512, 1024, 1024, "mlp1", epi="relu2", n_outer=True)
    x2 = _matmul(rl, w2, "nn", F32, 512, 1024, 2048, "mlp2", epi="res", extra=x1)
    dx2, g_nf, loss8 = _final(x2, norm_final, tgt)

    da = _matmul(dx2, w2, "nt", BF16, 512, 1024, 1024, "mlp2_dx", epi="drelu", extra=a1, n_outer=True)
    g_w2 = _matmul(rl, dx2, "tn", BF16, 1024, 1024, 512, "mlp2_dw")
    g_w1 = _matmul(h2, da, "tn", BF16, 1024, 1024, 512, "mlp1_dw")
    dh2 = _matmul(da, w1, "nt", F32, 512, 1024, 2048, "mlp1_dx")
    dx1, g_nmlp = _rms_bwd(dh2, x1, norm_mlp, dx2, "rms_mlp_bwd")
    dmerged = _matmul(dx1, wo, "nt", F32, 512, 1024, 1024, "out_proj_dx")
    g_wo = _matmul(merged, dx1, "tn", BF16, 1024, 1024, 512, "out_proj_dw")
    dproj = jnp.zeros((L, NCW), BF16)
    dbr, dproj, g_bg = _merge_bwd(dmerged, proj, b_gate, br_a, br_b, dproj)
    dya = _matmul(dbr[0], wa, "nt", F32, 512, 1024, 1024, "branch_sc_dx")
    g_wa = _matmul(ya, dbr[0], "tn", BF16, 1024, 1024, 512, "branch_sc_dw")
    dproj, g_scw = _sc_bwd(dya, proj, sc_w_full, dproj)
    dyb = _matmul(dbr[1], wb, "nt", F32, 512, 1024, 1024, "branch_ssm_dx", n_outer=True)
    g_wb = _matmul(yb, dbr[1], "tn", BF16, 1024, 1024, 512, "branch_ssm_dw")
    dy, dproj, g_snw = _gnorm_bwd(dyb, y, proj, ssm_norm_w, dproj)
    dxs, dbm, dcm, ddt_g, st = _ssd_bwd(xbc, dt4, cs4, sg4, vecg, s_all, dy)
    dproj, gx1 = _ssm_conv_bwd(dxs, proj, cw4, dproj, 0, "ssm_conv_bwd_x")
    dproj, gx2 = _ssm_conv_bwd(dbm, proj, cw4, dproj, INNER, "ssm_conv_bwd_b")
    dproj, gx3 = _ssm_conv_bwd(dcm, proj, cw4, dproj, INNER + NG * NS, "ssm_conv_bwd_c")
    g_cw4 = jnp.concatenate([gx1, gx2, gx3], axis=1)
    ddt = jnp.pad(ddt_g[:, :, :4].transpose(1, 0, 2).reshape(L, NH), ((0, 0), (0, 128 - NH)))
    dproj, g_dtb = _dt_bwd(ddt, dproj)
    g_wc = _matmul(h, dproj, "tn", BF16, 1024, 1280, 512, "in_proj_dw")
    dh = _matmul(dproj, wc, "nt", F32, 512, 1024, 2304, "in_proj_dx")
    grad_x, g_nm = _rms_bwd(dh, x0, norm_mix, dx1, "rms_mix_bwd")

    small = {"norm_mix": g_nm[0], "b_gate": g_bg[0], "ssm_conv_b": g_cw4[4], "dt_bias": g_dtb[0, :NH],
             "A_log": st[:, 0, :4], "D_skip": st[:, 1, :4], "ssm_norm_w": g_snw[0], "norm_mlp": g_nmlp[0],
             "norm_final": g_nf[0], "sc_conv_w": g_scw[0:3], "ssm_conv_w": g_cw4[0:4], "loss": loss8[0, 0:1]}
    gs = _unpack(_sum8(_small_all_gather(_pack(small, _SMALL, SMALL_ROWS))), _SMALL)

    grads = [g_wc, g_w1, g_w2, g_wa, g_wb, g_wo]
    ras = _rs_sibling(grads)
    sums = [_chip_sum(grads[t], ras[t], t, idx, "chip_sum_%d" % t) for t in range(6)]
    rbs = _rs_chips([hb for hb, _ in sums])
    fs = [_final_sum(sums[t][1], rbs[t], t, idx, "final_sum_%d" % t) for t in range(6)]
    gp, g1f, g2f, gaf, gbf, gof = _rs_share(fs)
    big = {"w_in": _from_piece(gp, s), "w_mlp1": g1f, "w_mlp2": g2f, "w_branch_sc": gaf, "w_branch_ssm": gbf, "w_out": gof}

    given = dict(norm_mix=norm_mix, w_in=w_in, b_gate=b_gate, sc_conv_w=sc_conv_w, ssm_conv_w=ssm_conv_w, ssm_conv_b=ssm_conv_b, dt_bias=dt_bias, A_log=A_log, D_skip=D_skip, ssm_norm_w=ssm_norm_w, w_branch_sc=w_branch_sc, w_branch_ssm=w_branch_ssm, w_out=w_out, norm_mlp=norm_mlp, w_mlp1=w_mlp1, w_mlp2=w_mlp2, norm_final=norm_final,
                 m_norm_mix=m_norm_mix, m_w_in=m_w_in, m_b_gate=m_b_gate, m_sc_conv_w=m_sc_conv_w, m_ssm_conv_w=m_ssm_conv_w, m_ssm_conv_b=m_ssm_conv_b, m_dt_bias=m_dt_bias, m_A_log=m_A_log, m_D_skip=m_D_skip, m_ssm_norm_w=m_ssm_norm_w, m_w_branch_sc=m_w_branch_sc, m_w_branch_ssm=m_w_branch_ssm, m_w_out=m_w_out, m_norm_mlp=m_norm_mlp, m_w_mlp1=m_w_mlp1, m_w_mlp2=m_w_mlp2, m_norm_final=m_norm_final,
                 v_norm_mix=v_norm_mix, v_w_in=v_w_in, v_b_gate=v_b_gate, v_sc_conv_w=v_sc_conv_w, v_ssm_conv_w=v_ssm_conv_w, v_ssm_conv_b=v_ssm_conv_b, v_dt_bias=v_dt_bias, v_A_log=v_A_log, v_D_skip=v_D_skip, v_ssm_norm_w=v_ssm_norm_w, v_w_branch_sc=v_w_branch_sc, v_w_branch_ssm=v_w_branch_ssm, v_w_out=v_w_out, v_norm_mlp=v_norm_mlp, v_w_mlp1=v_w_mlp1, v_w_mlp2=v_w_mlp2, v_norm_final=v_norm_final)
    order = ["norm_mix", "w_in", "b_gate", "sc_conv_w", "ssm_conv_w", "ssm_conv_b", "dt_bias", "A_log", "D_skip",
             "ssm_norm_w", "w_branch_sc", "w_branch_ssm", "w_out", "norm_mlp", "w_mlp1", "w_mlp2", "norm_final"]
    grad, delta, new_m, new_v = {}, {}, {}, {}
    for n in big:
        grad[n] = big[n]
        delta[n], new_m[n], new_v[n] = _adamw(given[n], big[n], given["m_" + n], given["v_" + n], "adamw_" + n)
    grad_small = {n: gs[n].reshape(given[n].shape) for n in order if n not in big and n not in ("sc_conv_w", "ssm_conv_w")}
    grad_small["sc_conv_w"] = lax.dynamic_slice(gs["sc_conv_w"].reshape(3, D), (0, 256 * s), (3, 256))
    grad_small["ssm_conv_w"] = lax.dynamic_slice(gs["ssm_conv_w"].reshape(4, XBC), (0, 1024 * s), (4, 1024))
    table = [(n, int(grad_small[n].size)) for n in grad_small]
    rows = 136
    pk = lambda d: _pack(d, table, rows)
    ds_, ms_, vs_ = _adamw(pk({n: given[n] for n in grad_small}), pk(grad_small), pk({n: given["m_" + n] for n in grad_small}),
                           pk({n: given["v_" + n] for n in grad_small}), "adamw_small", tr=rows)
    ds_, ms_, vs_ = _unpack(ds_, table), _unpack(ms_, table), _unpack(vs_, table)
    for n in grad_small:
        shp = given[n].shape
        grad[n] = grad_small[n]
        delta[n], new_m[n], new_v[n] = ds_[n].reshape(shp), ms_[n].reshape(shp), vs_[n].reshape(shp)

    loss = gs["loss"].reshape(())
    return (loss, grad_x.reshape(1, L, D), *[grad[n] for n in order], *[delta[n] for n in order],
            *[new_m[n] for n in order], *[new_v[n] for n in order])
```

```python
import functools

import jax
import jax.numpy as jnp
from jax import lax
from jax.experimental import pallas as pl
from jax.experimental.pallas import tpu as pltpu

F32 = jnp.float32
BF16 = jnp.bfloat16
MESH = pl.DeviceIdType.MESH
HBM = pltpu.HBM

D = 1024
INNER = 2048
HD = 64
NH = 32
NG = 8
NS = 128
Q = 128
XBC = 4096
DFF = 4096
EPS = 1e-6
W_SHARD = 2824
NCW = 11520
PIECE = 3072
PMAIN = 2816
C_Z, C_XBC, C_GATE, C_DT = 3072, 5120, 9216, 11264
SMALL_ROWS = 256
VMEM_LIMIT = 56 * 1024 * 1024

ADAM_LR, ADAM_B1, ADAM_B2, ADAM_EPS, ADAM_WD, ADAM_STEP = 0.001, 0.9, 0.999, 1e-08, 0.01, 10


def _cp(sem=None, vmem=VMEM_LIMIT):
    return pltpu.CompilerParams(dimension_semantics=sem, vmem_limit_bytes=vmem)


def _sigmoid(v):
    return 1.0 / (1.0 + jnp.exp(-v))


_DIMS = {"nn": (((1,), (0,)), ((), ())), "nt": (((1,), (1,)), ((), ())), "tn": (((0,), (0,)), ((), ()))}


def _matmul(a, b, mode, out_dtype, tm, tn, tk, name, epi=None, extra=None, n_outer=False):
    if mode == "tn":
        K, M = a.shape
    else:
        M, K = a.shape
    N = b.shape[0] if mode == "nt" else b.shape[1]
    tm, tn, tk = min(tm, M), min(tn, N), min(tk, K)
    assert M % tm == 0 and N % tn == 0 and K % tk == 0, (name, M, N, K, tm, tn, tk)
    nm, nn, nk = M // tm, N // tn, K // tk
    dims = _DIMS[mode]

    def ij(p0, p1):
        return (p1, p0) if n_outer else (p0, p1)

    if mode == "tn":
        a_spec = pl.BlockSpec((tk, tm), lambda p0, p1, k: (k, ij(p0, p1)[0]))
    else:
        a_spec = pl.BlockSpec((tm, tk), lambda p0, p1, k: (ij(p0, p1)[0], k))
    if mode == "nt":
        b_spec = pl.BlockSpec((tn, tk), lambda p0, p1, k: (ij(p0, p1)[1], k))
    else:
        b_spec = pl.BlockSpec((tk, tn), lambda p0, p1, k: (k, ij(p0, p1)[1]))
    o_spec = pl.BlockSpec((tm, tn), lambda p0, p1, k: ij(p0, p1))
    in_specs = [a_spec, b_spec]
    args = [a, b]
    if epi in ("res", "drelu"):
        in_specs.append(o_spec)
        args.append(extra)
    if epi == "relu2":
        out_shape = (jax.ShapeDtypeStruct((M, N), F32), jax.ShapeDtypeStruct((M, N), BF16))
        out_specs = (o_spec, o_spec)
    else:
        out_shape = jax.ShapeDtypeStruct((M, N), out_dtype)
        out_specs = o_spec

    def kern(*refs):
        a_ref, b_ref = refs[0], refs[1]
        e_ref = refs[2] if epi in ("res", "drelu") else None
        acc = refs[-1]
        outs = refs[(3 if e_ref is not None else 2):-1]
        k = pl.program_id(2)

        @pl.when(k == 0)
        def _():
            acc[...] = jnp.zeros_like(acc)

        acc[...] += lax.dot_general(a_ref[...].astype(BF16), b_ref[...].astype(BF16), dims,
                                    preferred_element_type=F32)

        @pl.when(k == nk - 1)
        def _():
            r = acc[...]
            if epi is None:
                outs[0][...] = r.astype(out_dtype)
            elif epi == "res":
                outs[0][...] = (r + e_ref[...]).astype(out_dtype)
            elif epi == "relu2":
                outs[0][...] = r
                t = jnp.maximum(r, 0.0)
                outs[1][...] = (t * t).astype(BF16)
            else:
                outs[0][...] = (r * (2.0 * jnp.maximum(e_ref[...], 0.0))).astype(out_dtype)

    grid = (nn, nm, nk) if n_outer else (nm, nn, nk)
    return pl.pallas_call(
        kern, grid=grid, in_specs=in_specs, out_specs=out_specs, out_shape=out_shape,
        scratch_shapes=[pltpu.VMEM((tm, tn), F32)], name=name,
        compiler_params=_cp(("parallel", "parallel", "arbitrary")),
    )(*args)


def _rms_fwd(x, w, name, tl=256):
    L = x.shape[0]

    def kern(x_ref, w_ref, o_ref):
        xv = x_ref[...]
        r = lax.rsqrt(jnp.mean(xv * xv, axis=-1, keepdims=True) + EPS)
        o_ref[...] = ((xv * r) * w_ref[...]).astype(BF16)

    row = pl.BlockSpec((tl, D), lambda i: (i, 0))
    return pl.pallas_call(
        kern, grid=(L // tl,), in_specs=[row, pl.BlockSpec((1, D), lambda i: (0, 0))], out_specs=row,
        out_shape=jax.ShapeDtypeStruct((L, D), BF16), name=name, compiler_params=_cp(("parallel",)),
    )(x, w.reshape(1, D))


def _rms_bwd(dy, x, w, res, name, tl=256):
    L = x.shape[0]

    def kern(dy_ref, x_ref, w_ref, res_ref, dx_ref, gw_ref):
        @pl.when(pl.program_id(0) == 0)
        def _():
            gw_ref[...] = jnp.zeros_like(gw_ref)

        xv = x_ref[...]
        dyv = dy_ref[...]
        r = lax.rsqrt(jnp.mean(xv * xv, axis=-1, keepdims=True) + EPS)
        xn = xv * r
        gw_ref[...] += jnp.broadcast_to(jnp.sum(dyv * xn, axis=0, keepdims=True), (8, D))
        dxn = dyv * w_ref[...]
        dx_ref[...] = res_ref[...] + r * (dxn - xn * jnp.mean(dxn * xn, axis=-1, keepdims=True))

    row = pl.BlockSpec((tl, D), lambda i: (i, 0))
    return pl.pallas_call(
        kern, grid=(L // tl,), in_specs=[row, row, pl.BlockSpec((1, D), lambda i: (0, 0)), row],
        out_specs=(row, pl.BlockSpec((8, D), lambda i: (0, 0))),
        out_shape=(jax.ShapeDtypeStruct((L, D), F32), jax.ShapeDtypeStruct((8, D), F32)),
        name=name, compiler_params=_cp(("arbitrary",)),
    )(dy, x, w.reshape(1, D), res)


def _final(x2, w, tgt, tl=256):
    L = x2.shape[0]

    def kern(x_ref, w_ref, t_ref, dx_ref, gw_ref, loss_ref):
        @pl.when(pl.program_id(0) == 0)
        def _():
            gw_ref[...] = jnp.zeros_like(gw_ref)
            loss_ref[...] = jnp.zeros_like(loss_ref)

        xv = x_ref[...]
        r = lax.rsqrt(jnp.mean(xv * xv, axis=-1, keepdims=True) + EPS)
        xn = xv * r
        e = xn * w_ref[...] - t_ref[...]
        per_tok = jnp.mean(e * e, axis=-1, keepdims=True)
        loss_ref[...] += 0.5 * jnp.sum(per_tok)
        dyv = e * (1.0 / D)
        gw_ref[...] += jnp.broadcast_to(jnp.sum(dyv * xn, axis=0, keepdims=True), (8, D))
        dxn = dyv * w_ref[...]
        dx_ref[...] = r * (dxn - xn * jnp.mean(dxn * xn, axis=-1, keepdims=True))

    row = pl.BlockSpec((tl, D), lambda i: (i, 0))
    return pl.pallas_call(
        kern, grid=(L // tl,), in_specs=[row, pl.BlockSpec((1, D), lambda i: (0, 0)), row],
        out_specs=(row, pl.BlockSpec((8, D), lambda i: (0, 0)), pl.BlockSpec((8, 128), lambda i: (0, 0))),
        out_shape=(jax.ShapeDtypeStruct((L, D), F32), jax.ShapeDtypeStruct((8, D), F32),
                   jax.ShapeDtypeStruct((8, 128), F32)),
        name="final_norm_loss", compiler_params=_cp(("arbitrary",)),
    )(x2, w.reshape(1, D), tgt)


def _down(v, k):
    if k == 0:
        return v
    t = lax.broadcasted_iota(jnp.int32, v.shape, 0)
    return jnp.where(t >= k, pltpu.roll(v, k, axis=0), 0.0)


def _up(v, k):
    if k == 0:
        return v
    n = v.shape[0]
    t = lax.broadcasted_iota(jnp.int32, v.shape, 0)
    return jnp.where(t < n - k, pltpu.roll(v, n - k, axis=0), 0.0)


TW = 256


def _sc_fwd(proj, cw):
    L = proj.shape[0]
    nb = D // TW

    def kern(b_ref, c_ref, x_ref, w_ref, o_ref):
        u = c_ref[...] * x_ref[...]
        w = w_ref[...]
        cv = w[0:1] * _down(u, 2) + w[1:2] * _down(u, 1) + w[2:3] * u
        o_ref[...] = (b_ref[...] * cv).astype(BF16)

    col = lambda off: pl.BlockSpec((L, TW), lambda j: (0, off + j))
    return pl.pallas_call(
        kern, grid=(nb,), in_specs=[col(0), col(nb), col(2 * nb), pl.BlockSpec((8, TW), lambda j: (0, j))],
        out_specs=pl.BlockSpec((L, TW), lambda j: (0, j)), out_shape=jax.ShapeDtypeStruct((L, D), BF16),
        name="sc_fwd", compiler_params=_cp(("parallel",)),
    )(proj, proj, proj, cw)


def _sc_bwd(dya, proj, cw, dproj):
    L = proj.shape[0]
    nb = D // TW

    def kern(d_ref, b_ref, c_ref, x_ref, w_ref, _, dp_ref, gw_ref):
        sec = pl.program_id(1)
        cs, xs, dyv = c_ref[...], x_ref[...], d_ref[...]
        w = w_ref[...]
        u = cs * xs
        u1, u2 = _down(u, 1), _down(u, 2)
        cv = w[0:1] * u2 + w[1:2] * u1 + w[2:3] * u
        dcv = dyv * b_ref[...]
        du = w[2:3] * dcv + w[1:2] * _up(dcv, 1) + w[0:1] * _up(dcv, 2)
        g0 = jnp.sum(dcv * u2, axis=0, keepdims=True)
        g1 = jnp.sum(dcv * u1, axis=0, keepdims=True)
        g2 = jnp.sum(dcv * u, axis=0, keepdims=True)
        row = lax.broadcasted_iota(jnp.int32, (8, TW), 0)
        gw_ref[...] = jnp.where(row == 0, g0, jnp.where(row == 1, g1, jnp.where(row == 2, g2, 0.0)))
        out = jnp.where(sec == 0, dyv * cv, jnp.where(sec == 1, du * xs, du * cs))
        dp_ref[...] = out.astype(BF16)

    col = lambda off: pl.BlockSpec((L, TW), lambda j, s: (0, off + j))
    return pl.pallas_call(
        kern, grid=(nb, 3),
        in_specs=[col(0), col(0), col(nb), col(2 * nb), pl.BlockSpec((8, TW), lambda j, s: (0, j)),
                  pl.BlockSpec(memory_space=pl.ANY)],
        out_specs=(pl.BlockSpec((L, TW), lambda j, s: (0, s * nb + j)), pl.BlockSpec((8, TW), lambda j, s: (0, j))),
        out_shape=(jax.ShapeDtypeStruct(dproj.shape, BF16), jax.ShapeDtypeStruct((8, D), F32)),
        input_output_aliases={5: 0}, name="sc_bwd", compiler_params=_cp(("parallel", "arbitrary")),
    )(dya, proj, proj, proj, cw, dproj)


def _ssm_conv_fwd(proj, cw4):
    L = proj.shape[0]
    off = C_XBC // TW

    def kern(r_ref, w_ref, o_ref):
        raw = r_ref[...]
        w = w_ref[...]
        c4 = w[0:1] * _down(raw, 3) + w[1:2] * _down(raw, 2) + w[2:3] * _down(raw, 1) + w[3:4] * raw + w[4:5]
        o_ref[...] = c4 * _sigmoid(c4)

    return pl.pallas_call(
        kern, grid=(XBC // TW,),
        in_specs=[pl.BlockSpec((L, TW), lambda j: (0, off + j)), pl.BlockSpec((8, TW), lambda j: (0, j))],
        out_specs=pl.BlockSpec((L, TW), lambda j: (0, j)), out_shape=jax.ShapeDtypeStruct((L, XBC), F32),
        name="ssm_conv_fwd", compiler_params=_cp(("parallel",)),
    )(proj, cw4)


def _ssm_conv_bwd(dx, proj, cw4, dproj, col0, name):
    L, width = dx.shape
    off_p = (C_XBC + col0) // TW
    off_w = col0 // TW

    def kern(d_ref, r_ref, w_ref, _, dp_ref, gw_ref):
        raw = r_ref[...]
        w = w_ref[...]
        r1, r2, r3 = _down(raw, 1), _down(raw, 2), _down(raw, 3)
        c4 = w[0:1] * r3 + w[1:2] * r2 + w[2:3] * r1 + w[3:4] * raw + w[4:5]
        sg = _sigmoid(c4)
        dc4 = d_ref[...] * (sg * (1.0 + c4 * (1.0 - sg)))
        draw = w[3:4] * dc4 + w[2:3] * _up(dc4, 1) + w[1:2] * _up(dc4, 2) + w[0:1] * _up(dc4, 3)
        dp_ref[...] = draw.astype(BF16)
        gs = [jnp.sum(dc4 * r3, axis=0, keepdims=True), jnp.sum(dc4 * r2, axis=0, keepdims=True),
              jnp.sum(dc4 * r1, axis=0, keepdims=True), jnp.sum(dc4 * raw, axis=0, keepdims=True),
              jnp.sum(dc4, axis=0, keepdims=True)]
        row = lax.broadcasted_iota(jnp.int32, (8, TW), 0)
        acc = jnp.zeros((8, TW), F32)
        for k, gk in enumerate(gs):
            acc = jnp.where(row == k, gk, acc)
        gw_ref[...] = acc

    return pl.pallas_call(
        kern, grid=(width // TW,),
        in_specs=[pl.BlockSpec((L, TW), lambda j: (0, j)), pl.BlockSpec((L, TW), lambda j: (0, off_p + j)),
                  pl.BlockSpec((8, TW), lambda j: (0, off_w + j)), pl.BlockSpec(memory_space=pl.ANY)],
        out_specs=(pl.BlockSpec((L, TW), lambda j: (0, off_p + j)), pl.BlockSpec((8, TW), lambda j: (0, j))),
        out_shape=(jax.ShapeDtypeStruct(dproj.shape, BF16), jax.ShapeDtypeStruct((8, width), F32)),
        input_output_aliases={3: 0}, name=name, compiler_params=_cp(("arbitrary",)),
    )(dx, proj, cw4, dproj)


def _split3(v):
    h1 = v.astype(BF16)
    r1 = v - h1.astype(F32)
    h2 = r1.astype(BF16)
    h3 = (r1 - h2.astype(F32)).astype(BF16)
    return h1, h2, h3


def _dot01(m01, v, dims=_DIMS["nn"], m_left=True):
    out = None
    for part in _split3(v):
        ops = (m01, part) if m_left else (part, m01)
        t = lax.dot_general(ops[0], ops[1], dims, preferred_element_type=F32)
        out = t if out is None else out + t
    return out


def _bdot(a, b, mode="nn"):
    return lax.dot_general(a.astype(BF16), b.astype(BF16), _DIMS[mode], preferred_element_type=F32)


def _softplus(v):
    return jnp.maximum(v, 0.0) + jnp.log1p(jnp.exp(-jnp.abs(v)))


def _dt_prep(proj, vec):
    L = proj.shape[0]

    def kern(p_ref, v_ref, dt_ref, cs_ref, sg_ref):
        v = v_ref[...]
        pre = p_ref[:, 0:128] + v[0:1]
        dt = _softplus(pre)
        sg_ref[...] = _sigmoid(pre)
        da = dt * (-jnp.exp(v[1:2]))
        ii = lax.broadcasted_iota(jnp.int32, (Q, Q), 0)
        jj = lax.broadcasted_iota(jnp.int32, (Q, Q), 1)
        ltri = (jj <= ii).astype(BF16)
        dt_ref[...] = dt
        cs_ref[...] = _dot01(ltri, da)

    blk = pl.BlockSpec((Q, 128), lambda c: (c, 0))
    return pl.pallas_call(
        kern, grid=(L // Q,),
        in_specs=[pl.BlockSpec((Q, 256), lambda c: (c, C_DT // 256)), pl.BlockSpec((8, 128), lambda c: (0, 0))],
        out_specs=(blk, blk, blk),
        out_shape=(jax.ShapeDtypeStruct((L, 128), F32),) * 3,
        name="dt_prep", compiler_params=_cp(("parallel",)),
    )(proj, vec)


def _head_masks():
    lane = lax.broadcasted_iota(jnp.int32, (1, 4 * HD), 1)
    return [((lane >= HD * j) & (lane < HD * (j + 1))) for j in range(4)]


def _expand4(v4, masks):
    R = v4.shape[0]
    out = jnp.zeros((R, 4 * HD), F32)
    for j in range(4):
        out = jnp.where(masks[j], jnp.broadcast_to(v4[:, j:j + 1], (R, 4 * HD)), out)
    return out


def _decay_matrix(cs_col, tri):
    colb = jnp.broadcast_to(cs_col, (Q, Q))
    return jnp.exp(jnp.where(tri, colb - colb.T, -jnp.inf))


def _ssd_fwd(xbc, dt4, cs4, vecg):
    L = xbc.shape[0]
    nc = L // Q

    def kern(x_ref, b_ref, c_ref, dt_ref, cs_ref, v_ref, y_ref, s_ref, S):
        c = pl.program_id(1)

        @pl.when(c == 0)
        def _():
            S[...] = jnp.zeros_like(S)

        masks = _head_masks()
        ii = lax.broadcasted_iota(jnp.int32, (Q, Q), 0)
        jj = lax.broadcasted_iota(jnp.int32, (Q, Q), 1)
        tri = jj <= ii
        dt4v, cs4v = dt_ref[0], cs_ref[0]
        dt_b, cs_b = _expand4(dt4v, masks), _expand4(cs4v, masks)
        d_b = _expand4(v_ref[0], masks)[1:2]
        cs_last = cs_b[Q - 1:Q, :]
        x4, bm, cm = x_ref[...], b_ref[...], c_ref[...]
        xdt = x4 * dt_b
        gm = _bdot(cm, bm, "nt")
        s4 = S[...]
        s_ref[0, 0] = s4
        y = _bdot(cm, s4) * jnp.exp(cs_b) + d_b * x4
        for j in range(4):
            mh = gm * _decay_matrix(cs4v[:, j:j + 1], tri)
            y = y + _bdot(mh, jnp.where(masks[j], xdt, 0.0))
        y_ref[...] = y
        S[...] = jnp.exp(cs_last) * s4 + _bdot(bm, xdt * jnp.exp(cs_last - cs_b), "tn")

    sc = pl.BlockSpec((1, Q, 128), lambda g, c: (g, c, 0))
    return pl.pallas_call(
        kern, grid=(NG, nc),
        in_specs=[pl.BlockSpec((Q, 256), lambda g, c: (c, g)),
                  pl.BlockSpec((Q, 128), lambda g, c: (c, INNER // 128 + g)),
                  pl.BlockSpec((Q, 128), lambda g, c: (c, (INNER + NG * NS) // 128 + g)),
                  sc, sc, pl.BlockSpec((1, 8, 128), lambda g, c: (g, 0, 0))],
        out_specs=(pl.BlockSpec((Q, 256), lambda g, c: (c, g)),
                   pl.BlockSpec((1, 1, NS, 256), lambda g, c: (g, c, 0, 0))),
        out_shape=(jax.ShapeDtypeStruct((L, INNER), F32), jax.ShapeDtypeStruct((NG, nc, NS, 256), F32)),
        scratch_shapes=[pltpu.VMEM((NS, 256), F32)], name="ssd_fwd",
        compiler_params=_cp(("parallel", "arbitrary")),
    )(xbc, xbc, xbc, dt4, cs4, vecg)


def _ssd_bwd(xbc, dt4, cs4, sg4, vecg, s_all, dy):
    L = xbc.shape[0]
    nc = L // Q

    def kern(x_ref, b_ref, c_ref, dt_ref, cs_ref, sg_ref, v_ref, s_ref, dy_ref,
             dx_ref, db_ref, dc_ref, ddt_ref, st_ref, dS):
        cc = pl.program_id(1)

        @pl.when(cc == 0)
        def _():
            dS[...] = jnp.zeros_like(dS)
            st_ref[...] = jnp.zeros_like(st_ref)

        masks = _head_masks()
        ii = lax.broadcasted_iota(jnp.int32, (Q, Q), 0)
        jj = lax.broadcasted_iota(jnp.int32, (Q, Q), 1)
        tri = jj <= ii
        utri = (jj >= ii).astype(BF16)
        li = lax.broadcasted_iota(jnp.int32, (4 * HD, 4 * HD), 0)
        lj = lax.broadcasted_iota(jnp.int32, (4 * HD, 4 * HD), 1)
        eblk = ((li // HD) == (lj // HD)).astype(BF16)
        lane128 = lax.broadcasted_iota(jnp.int32, (Q, 128), 1)

        dt4v, cs4v, sg4v = dt_ref[0], cs_ref[0], sg_ref[0]
        dt_b, cs_b = _expand4(dt4v, masks), _expand4(cs4v, masks)
        vv = _expand4(v_ref[0], masks)
        a_b = -jnp.exp(vv[0:1])
        d_b = vv[1:2]
        a4 = -jnp.exp(v_ref[0][0:1, :])
        cs_last = cs_b[Q - 1:Q, :]
        ecs = jnp.exp(cs_b)
        decay = jnp.exp(cs_last - cs_b)
        elast = jnp.exp(cs_last)
        x4, bm, cm, dyv = x_ref[...], b_ref[...], c_ref[...], dy_ref[...]
        s4 = s_ref[0, 0]
        dsn = dS[...]
        xdt = x4 * dt_b
        gm = _bdot(cm, bm, "nt")
        gmt = gm.T
        dye = dyv * ecs
        yoff = ecs * _bdot(cm, s4)
        t4 = _bdot(bm, dsn) * decay
        dxdt = t4
        dg = jnp.zeros((Q, Q), F32)
        rc = jnp.zeros((Q, 4 * HD), F32)
        for j in range(4):
            colb = jnp.broadcast_to(cs4v[:, j:j + 1], (Q, Q))
            seg = colb - colb.T
            lm = jnp.exp(jnp.where(tri, seg, -jnp.inf))
            lmt = jnp.exp(jnp.where(jj >= ii, -seg, -jnp.inf))
            mh = gm * lm
            mht = gmt * lmt
            dyh = jnp.where(masks[j], dyv, 0.0).astype(BF16)
            xh = jnp.where(masks[j], xdt, 0.0).astype(BF16)
            dxdt = dxdt + _bdot(mh, dyh, "tn")
            dmh = _bdot(dyh, xh, "nt")
            dmht = _bdot(xh, dyh, "nt")
            dg = dg + dmh * lm
            rs = jnp.sum(dmh * mh, axis=1, keepdims=True) - jnp.sum(dmht * mht, axis=1, keepdims=True)
            rc = jnp.where(masks[j], jnp.broadcast_to(rs, (Q, 4 * HD)), rc)
        xt = xdt * t4
        da_b = _dot01(eblk, dyv * yoff - xt, m_left=False) + rc
        tail = jnp.sum(xt, axis=0, keepdims=True) + elast * jnp.sum(s4 * dsn, axis=0, keepdims=True)
        tail = _dot01(eblk, jnp.broadcast_to(tail, (8, 4 * HD)), m_left=False)[0:1]
        dda_b = _dot01(utri, da_b) + tail
        ddt_b = dda_b * a_b + _dot01(eblk, dxdt * x4, m_left=False)
        ddt4 = jnp.zeros((Q, 128), F32)
        dda4 = jnp.zeros((Q, 128), F32)
        for j in range(4):
            ddt4 = jnp.where(lane128 == j, jnp.broadcast_to(ddt_b[:, HD * j:HD * j + 1], (Q, 128)), ddt4)
            dda4 = jnp.where(lane128 == j, jnp.broadcast_to(dda_b[:, HD * j:HD * j + 1], (Q, 128)), dda4)
        ddt_ref[0] = ddt4 * sg4v
        ga = jnp.sum(dda4 * dt4v * a4, axis=0, keepdims=True)
        gd_b = _dot01(eblk, jnp.broadcast_to(jnp.sum(dyv * x4, axis=0, keepdims=True), (8, 4 * HD)), m_left=False)[0:1]
        gd = jnp.zeros((1, 128), F32)
        for j in range(4):
            gd = jnp.where(lane128[0:1] == j, jnp.broadcast_to(gd_b[:, HD * j:HD * j + 1], (1, 128)), gd)
        row = lax.broadcasted_iota(jnp.int32, (8, 128), 0)
        st_ref[0] += jnp.where(row == 0, ga, jnp.where(row == 1, gd, 0.0))
        dx_ref[...] = d_b * dyv + dxdt * dt_b
        dc_ref[...] = _bdot(dg, bm) + _bdot(dye, s4, "nt")
        db_ref[...] = _bdot(dg, cm, "tn") + _bdot(xdt * decay, dsn, "nt")
        dS[...] = elast * dsn + _bdot(cm, dye, "tn")

    rv = lambda c: nc - 1 - c
    sc = pl.BlockSpec((1, Q, 128), lambda g, c: (g, rv(c), 0))
    return pl.pallas_call(
        kern, grid=(NG, nc),
        in_specs=[pl.BlockSpec((Q, 256), lambda g, c: (rv(c), g)),
                  pl.BlockSpec((Q, 128), lambda g, c: (rv(c), INNER // 128 + g)),
                  pl.BlockSpec((Q, 128), lambda g, c: (rv(c), (INNER + NG * NS) // 128 + g)),
                  sc, sc, sc, pl.BlockSpec((1, 8, 128), lambda g, c: (g, 0, 0)),
                  pl.BlockSpec((1, 1, NS, 256), lambda g, c: (g, rv(c), 0, 0)),
                  pl.BlockSpec((Q, 256), lambda g, c: (rv(c), g))],
        out_specs=(pl.BlockSpec((Q, 256), lambda g, c: (rv(c), g)),
                   pl.BlockSpec((Q, 128), lambda g, c: (rv(c), g)),
                   pl.BlockSpec((Q, 128), lambda g, c: (rv(c), g)),
                   pl.BlockSpec((1, Q, 128), lambda g, c: (g, rv(c), 0)),
                   pl.BlockSpec((1, 8, 128), lambda g, c: (g, 0, 0))),
        out_shape=(jax.ShapeDtypeStruct((L, INNER), F32), jax.ShapeDtypeStruct((L, NG * NS), F32),
                   jax.ShapeDtypeStruct((L, NG * NS), F32), jax.ShapeDtypeStruct((NG, L, 128), F32),
                   jax.ShapeDtypeStruct((NG, 8, 128), F32)),
        scratch_shapes=[pltpu.VMEM((NS, 256), F32)], name="ssd_bwd",
        compiler_params=_cp(("parallel", "arbitrary")),
    )(xbc, xbc, xbc, dt4, cs4, sg4, vecg, s_all, dy)


def _dt_bwd(ddt, dproj, tl=256):
    L = ddt.shape[0]

    def kern(d_ref, _, dp_ref, gs_ref):
        @pl.when(pl.program_id(0) == 0)
        def _():
            gs_ref[...] = jnp.zeros_like(gs_ref)

        d = d_ref[...]
        gs_ref[...] += jnp.broadcast_to(jnp.sum(d, axis=0, keepdims=True), (8, 128))
        dp_ref[...] = jnp.concatenate([d, jnp.zeros_like(d)], axis=1).astype(BF16)

    return pl.pallas_call(
        kern, grid=(L // tl,),
        in_specs=[pl.BlockSpec((tl, 128), lambda i: (i, 0)), pl.BlockSpec(memory_space=pl.ANY)],
        out_specs=(pl.BlockSpec((tl, 256), lambda i: (i, C_DT // 256)), pl.BlockSpec((8, 128), lambda i: (0, 0))),
        out_shape=(jax.ShapeDtypeStruct(dproj.shape, BF16), jax.ShapeDtypeStruct((8, 128), F32)),
        input_output_aliases={1: 0}, name="dt_bwd", compiler_params=_cp(("arbitrary",)),
    )(ddt, dproj)


GW = INNER // NG


def _gnorm_fwd(y, proj, w, tl=256):
    L = y.shape[0]
    zoff = C_Z // 1024

    def kern(y_ref, z_ref, w_ref, o_ref):
        z = z_ref[...]
        yz = y_ref[...] * (z * _sigmoid(z))
        wv = w_ref[...]
        for k in range(1024 // GW):
            sl = slice(GW * k, GW * (k + 1))
            v = yz[:, sl]
            rg = lax.rsqrt(jnp.mean(v * v, axis=-1, keepdims=True) + EPS)
            o_ref[:, sl] = ((v * rg) * wv[:, sl]).astype(BF16)

    blk = pl.BlockSpec((tl, 1024), lambda i, j: (i, j))
    return pl.pallas_call(
        kern, grid=(L // tl, 2),
        in_specs=[blk, pl.BlockSpec((tl, 1024), lambda i, j: (i, zoff + j)), pl.BlockSpec((1, 1024), lambda i, j: (0, j))],
        out_specs=blk, out_shape=jax.ShapeDtypeStruct((L, INNER), BF16), name="gnorm_fwd",
        compiler_params=_cp(("parallel", "parallel")),
    )(y, proj, w.reshape(1, INNER))


def _gnorm_bwd(dyb, y, proj, w, dproj, tl=256):
    L = y.shape[0]
    zoff = C_Z // 1024

    def kern(d_ref, y_ref, z_ref, w_ref, _, dy_ref, dp_ref, gw_ref):
        @pl.when(pl.program_id(1) == 0)
        def _():
            gw_ref[...] = jnp.zeros_like(gw_ref)

        z = z_ref[...]
        sg = _sigmoid(z)
        sz = z * sg
        yv = y_ref[...]
        yz = yv * sz
        dv = d_ref[...]
        wv = w_ref[...]
        for k in range(1024 // GW):
            sl = slice(GW * k, GW * (k + 1))
            v = yz[:, sl]
            rg = lax.rsqrt(jnp.mean(v * v, axis=-1, keepdims=True) + EPS)
            vn = v * rg
            dk = dv[:, sl]
            gw_ref[:, sl] += jnp.broadcast_to(jnp.sum(dk * vn, axis=0, keepdims=True), (8, GW))
            dvn = dk * wv[:, sl]
            dyz = rg * (dvn - vn * jnp.mean(dvn * vn, axis=-1, keepdims=True))
            dy_ref[:, sl] = dyz * sz[:, sl]
            dp_ref[:, sl] = (dyz * yv[:, sl] * (sg[:, sl] * (1.0 + z[:, sl] * (1.0 - sg[:, sl])))).astype(BF16)

    blk = pl.BlockSpec((tl, 1024), lambda j, i: (i, j))
    zblk = pl.BlockSpec((tl, 1024), lambda j, i: (i, zoff + j))
    return pl.pallas_call(
        kern, grid=(2, L // tl),
        in_specs=[blk, blk, zblk, pl.BlockSpec((1, 1024), lambda j, i: (0, j)), pl.BlockSpec(memory_space=pl.ANY)],
        out_specs=(blk, zblk, pl.BlockSpec((8, 1024), lambda j, i: (0, j))),
        out_shape=(jax.ShapeDtypeStruct((L, INNER), F32), jax.ShapeDtypeStruct(dproj.shape, BF16),
                   jax.ShapeDtypeStruct((8, INNER), F32)),
        input_output_aliases={4: 1}, name="gnorm_bwd", compiler_params=_cp(("parallel", "arbitrary")),
    )(dyb, y, proj, w.reshape(1, INNER), dproj)


def _merge_fwd(proj, bg, br_a, br_b, tl=256):
    L = proj.shape[0]
    goff = C_GATE // 1024

    def kern(g1_ref, g2_ref, b1_ref, b2_ref, a_ref, b_ref, o_ref):
        g1 = _sigmoid(g1_ref[...] + b1_ref[...])
        g2 = _sigmoid(g2_ref[...] + b2_ref[...])
        o_ref[...] = (g1 * a_ref[...] + g2 * b_ref[...]).astype(BF16)

    row = pl.BlockSpec((tl, 1024), lambda i: (i, 0))
    bg2 = bg.reshape(1, 2 * D)
    return pl.pallas_call(
        kern, grid=(L // tl,),
        in_specs=[pl.BlockSpec((tl, 1024), lambda i: (i, goff)), pl.BlockSpec((tl, 1024), lambda i: (i, goff + 1)),
                  pl.BlockSpec((1, 1024), lambda i: (0, 0)), pl.BlockSpec((1, 1024), lambda i: (0, 1)), row, row],
        out_specs=row, out_shape=jax.ShapeDtypeStruct((L, D), BF16), name="merge_fwd",
        compiler_params=_cp(("parallel",)),
    )(proj, proj, bg2, bg2, br_a, br_b)


def _merge_bwd(dm, proj, bg, br_a, br_b, dproj, tl=256):
    L = proj.shape[0]
    goff = C_GATE // 1024

    def kern(dm_ref, g_ref, b_ref, a_ref, bb_ref, _, dbr_ref, dp_ref, gb_ref):
        j = pl.program_id(0)

        @pl.when(pl.program_id(1) == 0)
        def _():
            gb_ref[...] = jnp.zeros_like(gb_ref)

        g = _sigmoid(g_ref[...] + b_ref[...])
        br = jnp.where(j == 0, a_ref[...], bb_ref[...])
        dmv = dm_ref[...]
        dbr_ref[0] = (dmv * g).astype(BF16)
        dgate = dmv * br * g * (1.0 - g)
        gb_ref[...] += jnp.broadcast_to(jnp.sum(dgate, axis=0, keepdims=True), (8, 1024))
        dp_ref[...] = dgate.astype(BF16)

    row = pl.BlockSpec((tl, 1024), lambda j, i: (i, 0))
    gblk = pl.BlockSpec((tl, 1024), lambda j, i: (i, goff + j))
    return pl.pallas_call(
        kern, grid=(2, L // tl),
        in_specs=[row, gblk, pl.BlockSpec((1, 1024), lambda j, i: (0, j)), row, row, pl.BlockSpec(memory_space=pl.ANY)],
        out_specs=(pl.BlockSpec((1, tl, 1024), lambda j, i: (j, i, 0)), gblk, pl.BlockSpec((8, 1024), lambda j, i: (0, j))),
        out_shape=(jax.ShapeDtypeStruct((2, L, D), BF16), jax.ShapeDtypeStruct(dproj.shape, BF16),
                   jax.ShapeDtypeStruct((8, 2 * D), F32)),
        input_output_aliases={5: 1}, name="merge_bwd", compiler_params=_cp(("parallel", "arbitrary")),
    )(dm, proj, bg.reshape(1, 2 * D), br_a, br_b, dproj)


def _coords():
    return lax.axis_index("x"), lax.axis_index("y"), lax.axis_index("c")


def _other_chips(sk):
    xk, yk = sk // 2, sk % 2
    return [((1 - xk, yk), 2 * (1 - xk) + yk), ((xk, 1 - yk), 2 * xk + 1 - yk), ((1 - xk, 1 - yk), 2 * (1 - xk) + 1 - yk)]


def _rows(start, size):
    assert size % 128 == 0
    return pl.ds(pl.multiple_of(start, 128), size)


def _per_chip(fn):
    x, y, _ = _coords()
    s = 2 * x + y
    for sk in range(4):
        pl.when(s == sk)(functools.partial(fn, sk))


XTRA = PIECE - PMAIN


def _place(shard, full_shape, block, index_map, idx, name, blk0=0, nblk=None):
    in_block = block[-2:]
    if nblk is None:
        nblk = shard.shape[0] // in_block[0]

    def kern(idx_ref, s_ref, o_ref):
        o_ref[...] = s_ref[...].astype(BF16).reshape(o_ref.shape)

    grid_spec = pltpu.PrefetchScalarGridSpec(
        num_scalar_prefetch=1, grid=(nblk,),
        in_specs=[pl.BlockSpec(in_block, lambda i, idx_ref: (blk0 + i, 0))],
        out_specs=pl.BlockSpec(block, index_map))
    return pl.pallas_call(kern, grid_spec=grid_spec, out_shape=jax.ShapeDtypeStruct(full_shape, BF16), name=name,
                          compiler_params=_cp(("arbitrary",)))(idx, shard)


def _weight_windows(refs):
    wct, xt, w1, w2, wa, wb, wo, cw = refs
    hm = PMAIN // 2
    hx = XTRA // 2
    return [
        (True, lambda sc, hc: wct.at[_rows(PMAIN * sc + hm * hc, hm), :]),
        (True, lambda sc, hc: xt.at[sc, _rows(hx * hc, hx), :]),
        (True, lambda sc, hc: w1.at[_rows(512 * hc, 512), pl.ds(1024 * sc, 1024)]),
        (True, lambda sc, hc: w2.at[_rows(1024 * sc + 512 * hc, 512), :]),
        (True, lambda sc, hc: wa.at[_rows(256 * sc + 128 * hc, 128), :]),
        (True, lambda sc, hc: wb.at[_rows(512 * sc + 256 * hc, 256), :]),
        (True, lambda sc, hc: wo.at[_rows(256 * sc + 128 * hc, 128), :]),
        (False, lambda sc, hc: cw.at[sc]),
    ]


def _all_gather_weights(arrays):
    n_t = 8

    def body(*refs):
        outs = refs[n_t:2 * n_t]
        send_sems, recv_sems = refs[2 * n_t:]
        x, y, c = _coords()
        sib = (x, y, 1 - c)

        def run(sk):
            tens = _weight_windows(outs)
            peers = _other_chips(sk)

            def rcopy(win, k, dev):
                return pltpu.make_async_remote_copy(src_ref=win, dst_ref=win, send_sem=send_sems.at[k],
                                                    recv_sem=recv_sems.at[k], device_id=dev, device_id_type=MESH)

            sends = []
            for t, (split, win) in enumerate(tens):
                for j, ((px, py), _) in enumerate(peers):
                    cp = rcopy(win(sk, c), 6 * t + j, (px, py, c))
                    cp.start()
                    sends.append(cp)
            for t, (split, win) in enumerate(tens):
                for j, ((px, py), ps) in enumerate(peers):
                    rcopy(win(ps, c), 6 * t + j, (px, py, c)).wait_recv()
                    if split:
                        cp = rcopy(win(ps, c), 6 * t + 3 + j, sib)
                        cp.start()
                        sends.append(cp)
            for t, (split, win) in enumerate(tens):
                if split:
                    for j, (_, ps) in enumerate(peers):
                        rcopy(win(ps, 1 - c), 6 * t + 3 + j, sib).wait_recv()
            for cp in sends:
                cp.wait_send()

        _per_chip(run)

    hbm = pl.BlockSpec(memory_space=HBM)
    return pl.pallas_call(
        body, out_shape=tuple(jax.ShapeDtypeStruct(a.shape, a.dtype) for a in arrays),
        in_specs=[hbm] * n_t, out_specs=(hbm,) * n_t,
        scratch_shapes=[pltpu.SemaphoreType.DMA((6 * n_t,)), pltpu.SemaphoreType.DMA((6 * n_t,))],
        input_output_aliases={t: t for t in range(n_t)},
        name="all_gather_weights", compiler_params=pltpu.CompilerParams(has_side_effects=True),
    )(*arrays)


def _fix_wct(wct, xt):
    nb = PMAIN // XTRA

    def kern(w_ref, x_ref, o_ref):
        k = pl.program_id(0)
        xv = x_ref[0]
        o_ref[...] = jnp.where(k < 3, (w_ref[...].astype(F32) + xv.astype(F32)).astype(BF16), xv)

    blk = pl.BlockSpec((XTRA, D), lambda k: (nb * (k + 1), 0))
    rblk = pl.BlockSpec((XTRA, D), lambda k: (jnp.where(k < 3, nb * (k + 1), 0), 0))
    return pl.pallas_call(
        kern, grid=(4,), in_specs=[rblk, pl.BlockSpec((1, XTRA, D), lambda k: (k, 0, 0))], out_specs=blk,
        out_shape=jax.ShapeDtypeStruct(wct.shape, BF16), input_output_aliases={0: 0}, name="fix_wct",
        compiler_params=_cp(("arbitrary",)),
    )(wct, xt)


def _grad_windows(g_refs):
    gc_r, g1_r, g2_r, ga_r, gb_r, go_r = g_refs
    hp = PIECE // 2
    return [
        lambda sc, hc: gc_r.at[_rows(PMAIN * sc + hp * hc, hp), :],
        lambda sc, hc: g1_r.at[_rows(512 * hc, 512), pl.ds(1024 * sc, 1024)],
        lambda sc, hc: g2_r.at[_rows(1024 * sc + 512 * hc, 512), :],
        lambda sc, hc: ga_r.at[_rows(256 * sc + 128 * hc, 128), :],
        lambda sc, hc: gb_r.at[_rows(512 * sc + 256 * hc, 256), :],
        lambda sc, hc: go_r.at[_rows(256 * sc + 128 * hc, 128), :],
    ]


HALF_SHAPES = [(PIECE // 2, D), (512, 1024), (512, 1024), (128, 1024), (256, 1024), (128, 1024)]


def _half_tiling(t):
    rows, cols = HALF_SHAPES[t]
    if t == 0:
        return (256, cols), rows // 256, lambda i: (i, 0)
    if t == 1:
        return (rows, 256), cols // 256, lambda i: (0, i)
    return (rows, cols), 1, lambda i: (0, 0)


def _window_block(t, sc, hc, i):
    if t == 0:
        return (PMAIN // 256) * sc + (PIECE // 512) * hc + i, 0
    if t == 1:
        return hc, 4 * sc + i
    return 2 * sc + hc, 0


def _rs_sibling(grads):
    def body(*refs):
        g_refs, ra_refs = refs[:6], refs[6:12]
        send_sems, recv_sems = refs[12:]
        x, y, c = _coords()
        sib = (x, y, 1 - c)
        wins = _grad_windows(g_refs)
        cps = []
        for t in range(6):
            for sc in range(4):
                cp = pltpu.make_async_remote_copy(
                    src_ref=wins[t](sc, 1 - c), dst_ref=ra_refs[t].at[sc], send_sem=send_sems.at[4 * t + sc],
                    recv_sem=recv_sems.at[4 * t + sc], device_id=sib, device_id_type=MESH)
                cp.start()
                cps.append(cp)
        for cp in cps:
            cp.wait()

    hbm = pl.BlockSpec(memory_space=HBM)
    return pl.pallas_call(
        body, out_shape=tuple(jax.ShapeDtypeStruct((4,) + s, BF16) for s in HALF_SHAPES),
        in_specs=[hbm] * 6, out_specs=(hbm,) * 6,
        scratch_shapes=[pltpu.SemaphoreType.DMA((24,)), pltpu.SemaphoreType.DMA((24,))],
        name="rs_sibling", compiler_params=pltpu.CompilerParams(has_side_effects=True),
    )(*grads)


def _chip_sum(g, ra, t, idx, name):
    rows, cols = HALF_SHAPES[t]
    blk, nblk, inner = _half_tiling(t)

    def kern(idx_ref, g_ref, r_ref, hb_ref, hf_ref):
        v = g_ref[...].astype(F32) + r_ref[0].astype(F32)
        hb_ref[0] = v.astype(BF16)
        hf_ref[0] = v

    gmap = lambda sc, i, idx_ref: _window_block(t, sc, idx_ref[1], i)
    omap = lambda sc, i, idx_ref: (sc,) + inner(i)
    grid_spec = pltpu.PrefetchScalarGridSpec(
        num_scalar_prefetch=1, grid=(4, nblk),
        in_specs=[pl.BlockSpec(blk, gmap), pl.BlockSpec((1,) + blk, omap)],
        out_specs=(pl.BlockSpec((1,) + blk, omap), pl.BlockSpec((1,) + blk, omap)))
    return pl.pallas_call(
        kern, grid_spec=grid_spec,
        out_shape=(jax.ShapeDtypeStruct((4, rows, cols), BF16), jax.ShapeDtypeStruct((4, rows, cols), F32)),
        name=name, compiler_params=_cp(("parallel", "parallel")),
    )(idx, g, ra)


def _rs_chips(hbs):
    def body(*refs):
        h_refs, rb_refs = refs[:6], refs[6:12]
        send_sems, recv_sems = refs[12:]
        _, _, c = _coords()

        def run(sk):
            cps = []
            for t in range(6):
                for j, ((px, py), ps) in enumerate(_other_chips(sk)):
                    cp = pltpu.make_async_remote_copy(
                        src_ref=h_refs[t].at[ps], dst_ref=rb_refs[t].at[j], send_sem=send_sems.at[3 * t + j],
                        recv_sem=recv_sems.at[3 * t + j], device_id=(px, py, c), device_id_type=MESH)
                    cp.start()
                    cps.append(cp)
            for cp in cps:
                cp.wait()

        _per_chip(run)

    hbm = pl.BlockSpec(memory_space=HBM)
    return pl.pallas_call(
        body, out_shape=tuple(jax.ShapeDtypeStruct((3,) + s, BF16) for s in HALF_SHAPES),
        in_specs=[hbm] * 6, out_specs=(hbm,) * 6,
        scratch_shapes=[pltpu.SemaphoreType.DMA((18,)), pltpu.SemaphoreType.DMA((18,))],
        name="rs_chips", compiler_params=pltpu.CompilerParams(has_side_effects=True),
    )(*hbs)


def _final_sum(hf, rb, t, idx, name):
    rows, cols = HALF_SHAPES[t]
    blk, nblk, inner = _half_tiling(t)
    nbr = rows // blk[0]

    def kern(idx_ref, h_ref, r_ref, o_ref):
        o_ref[...] = ((h_ref[0] + r_ref[0].astype(F32)) + r_ref[1].astype(F32)) + r_ref[2].astype(F32)

    def omap(i, idx_ref):
        r, cidx = inner(i)
        return nbr * idx_ref[1] + r, cidx

    grid_spec = pltpu.PrefetchScalarGridSpec(
        num_scalar_prefetch=1, grid=(nblk,),
        in_specs=[pl.BlockSpec((1,) + blk, lambda i, idx_ref: (idx_ref[0],) + inner(i)),
                  pl.BlockSpec((3,) + blk, lambda i, idx_ref: (0,) + inner(i))],
        out_specs=pl.BlockSpec(blk, omap))
    return pl.pallas_call(
        kern, grid_spec=grid_spec, out_shape=jax.ShapeDtypeStruct((2 * rows, cols), F32),
        name=name, compiler_params=_cp(("parallel",)),
    )(idx, hf, rb)


def _rs_share(fs):
    def body(*refs):
        o_refs = refs[6:12]
        send_sems, recv_sems = refs[12:]
        x, y, c = _coords()
        sib = (x, y, 1 - c)
        cps = []
        for t in range(6):
            rows = HALF_SHAPES[t][0]
            mine = o_refs[t].at[_rows(rows * c, rows), :]
            cp = pltpu.make_async_remote_copy(src_ref=mine, dst_ref=mine, send_sem=send_sems.at[t],
                                              recv_sem=recv_sems.at[t], device_id=sib, device_id_type=MESH)
            cp.start()
            cps.append(cp)
        for t in range(6):
            rows = HALF_SHAPES[t][0]
            theirs = o_refs[t].at[_rows(rows * (1 - c), rows), :]
            pltpu.make_async_remote_copy(src_ref=theirs, dst_ref=theirs, send_sem=send_sems.at[t],
                                         recv_sem=recv_sems.at[t], device_id=sib, device_id_type=MESH).wait_recv()
        for cp in cps:
            cp.wait_send()

    hbm = pl.BlockSpec(memory_space=HBM)
    return pl.pallas_call(
        body, out_shape=tuple(jax.ShapeDtypeStruct(f.shape, F32) for f in fs),
        in_specs=[hbm] * 6, out_specs=(hbm,) * 6,
        scratch_shapes=[pltpu.SemaphoreType.DMA((6,)), pltpu.SemaphoreType.DMA((6,))],
        input_output_aliases={t: t for t in range(6)},
        name="rs_share", compiler_params=pltpu.CompilerParams(has_side_effects=True),
    )(*fs)


def _small_all_gather(v):
    def body(v_ref, o_ref, send_sems, recv_sems, loc_sem):
        x, y, c = _coords()
        me = 4 * x + 2 * y + c
        lc = pltpu.make_async_copy(v_ref, o_ref.at[me], loc_sem)
        lc.start()
        cps = []
        for k in range(1, 8):
            fx, fy, fc = (k >> 2) & 1, (k >> 1) & 1, k & 1
            dev = ((1 - x) if fx else x, (1 - y) if fy else y, (1 - c) if fc else c)
            cp = pltpu.make_async_remote_copy(src_ref=v_ref, dst_ref=o_ref.at[me], send_sem=send_sems.at[k - 1],
                                              recv_sem=recv_sems.at[k - 1], device_id=dev, device_id_type=MESH)
            cp.start()
            cps.append((cp, 4 * dev[0] + 2 * dev[1] + dev[2]))
        for k, (cp, frm) in enumerate(cps):
            got = o_ref.at[frm]
            pltpu.make_async_remote_copy(src_ref=got, dst_ref=got, send_sem=send_sems.at[k], recv_sem=recv_sems.at[k],
                                         device_id=(x, y, c), device_id_type=MESH).wait_recv()
        for cp, _ in cps:
            cp.wait_send()
        lc.wait()

    hbm = pl.BlockSpec(memory_space=HBM)
    return pl.pallas_call(
        body, out_shape=jax.ShapeDtypeStruct((8,) + v.shape, F32), in_specs=[hbm], out_specs=hbm,
        scratch_shapes=[pltpu.SemaphoreType.DMA((7,)), pltpu.SemaphoreType.DMA((7,)), pltpu.SemaphoreType.DMA(())],
        name="small_all_gather", compiler_params=pltpu.CompilerParams(has_side_effects=True),
    )(v)


def _sum8(v):
    def kern(v_ref, o_ref):
        acc = v_ref[0]
        for k in range(1, 8):
            acc = acc + v_ref[k]
        o_ref[...] = acc

    return pl.pallas_call(kern, out_shape=jax.ShapeDtypeStruct(v.shape[1:], F32), name="small_sum")(v)


def _adamw(w, g, m, v, name, tr=128, blk0=0, nblk=None, into=None, copy_g=False):
    R, C = w.shape
    tr = min(tr, R)
    if nblk is None:
        assert R % tr == 0 and blk0 == 0
        nblk = R // tr
    n_out = 4 if copy_g else 3

    def kern(*refs):
        w_ref, g_ref, m_ref, v_ref = refs[:4]
        d_ref, mo_ref, vo_ref = refs[-n_out:][:3]
        gv = g_ref[...]
        mn = ADAM_B1 * m_ref[...] + (1.0 - ADAM_B1) * gv
        vn = ADAM_B2 * v_ref[...] + (1.0 - ADAM_B2) * (gv * gv)
        m_hat = mn / (1.0 - ADAM_B1 ** ADAM_STEP)
        v_hat = vn / (1.0 - ADAM_B2 ** ADAM_STEP)
        d_ref[...] = -ADAM_LR * (m_hat / (jnp.sqrt(v_hat) + ADAM_EPS) + ADAM_WD * w_ref[...])
        mo_ref[...] = mn
        vo_ref[...] = vn
        if copy_g:
            refs[-1][...] = gv

    blk = pl.BlockSpec((tr, C), lambda i: (blk0 + i, 0))
    sd = jax.ShapeDtypeStruct((R, C), F32)
    in_specs, args, aliases = [blk] * 4, [w, g, m, v], {}
    if into is not None:
        in_specs += [pl.BlockSpec(memory_space=pl.ANY)] * 3
        args += list(into)
        aliases = {4: 0, 5: 1, 6: 2}
    return pl.pallas_call(kern, grid=(nblk,), in_specs=in_specs, out_specs=(blk,) * n_out, out_shape=(sd,) * n_out,
                          input_output_aliases=aliases, name=name, compiler_params=_cp(("parallel",)))(*args)


def _to_piece(wt, s):
    gen = lax.dynamic_update_slice(jnp.zeros((PIECE, D), wt.dtype), wt, (8 * s, 0))
    z = lambda n: jnp.zeros((n, D), wt.dtype)
    last = jnp.concatenate([z(24), wt[:744], wt[776:], wt[744:776], z(PIECE - 24 - W_SHARD)], axis=0)
    return jnp.where(s == 3, last, gen)


def _from_piece(p, s):
    gen = lax.dynamic_slice(p, (8 * s, 0), (W_SHARD, D))
    last = jnp.concatenate([p[24:768], p[2816:2848], p[768:2816]], axis=0)
    return jnp.where(s == 3, last, gen)


_SMALL = [("norm_mix", 1024), ("b_gate", 2048), ("ssm_conv_b", 4096), ("dt_bias", 32), ("A_log", 32), ("D_skip", 32),
          ("ssm_norm_w", 2048), ("norm_mlp", 1024), ("norm_final", 1024), ("sc_conv_w", 3072), ("ssm_conv_w", 16384),
          ("loss", 1)]


def _pack(vals, table, rows):
    parts = []
    for name, n in table:
        v = vals[name].reshape(-1).astype(F32)
        pad = (-n) % 128
        parts.append(jnp.pad(v, (0, pad)) if pad else v)
    flat = jnp.concatenate(parts)
    return jnp.pad(flat, (0, rows * 128 - flat.shape[0])).reshape(rows, 128)


def _unpack(arr, table):
    flat = arr.reshape(-1)
    out, off = {}, 0
    for name, n in table:
        out[name] = flat[off:off + n]
        off += n + ((-n) % 128)
    return out


def kernel(x, norm_mix, w_in, b_gate, sc_conv_w, ssm_conv_w, ssm_conv_b, dt_bias, A_log, D_skip, ssm_norm_w, w_branch_sc, w_branch_ssm, w_out, norm_mlp, w_mlp1, w_mlp2, norm_final, loss_target, m_norm_mix, m_w_in, m_b_gate, m_sc_conv_w, m_ssm_conv_w, m_ssm_conv_b, m_dt_bias, m_A_log, m_D_skip, m_ssm_norm_w, m_w_branch_sc, m_w_branch_ssm, m_w_out, m_norm_mlp, m_w_mlp1, m_w_mlp2, m_norm_final, v_norm_mix, v_w_in, v_b_gate, v_sc_conv_w, v_ssm_conv_w, v_ssm_conv_b, v_dt_bias, v_A_log, v_D_skip, v_ssm_norm_w, v_w_branch_sc, v_w_branch_ssm, v_w_out, v_norm_mlp, v_w_mlp1, v_w_mlp2, v_norm_final):
    L = x.shape[1]
    nc = L // Q
    xi, yi, ci = lax.axis_index("x"), lax.axis_index("y"), lax.axis_index("c")
    s = 2 * xi + yi
    idx = jnp.stack([s, ci]).astype(jnp.int32)
    x0 = x.reshape(L, D)
    tgt = loss_target.reshape(L, D)

    piece = _to_piece(w_in.T, s)
    nb = PMAIN // XTRA
    wct0 = _place(piece, (NCW, D), (XTRA, D), lambda i, r: (nb * r[0] + i, 0), idx, "place_wct", nblk=nb)
    xt0 = _place(piece, (4, XTRA, D), (1, XTRA, D), lambda i, r: (r[0], 0, 0), idx, "place_xt", blk0=nb, nblk=1)
    w10 = _place(w_mlp1, (D, DFF), (256, 1024), lambda i, r: (i, r[0]), idx, "place_w1")
    w20 = _place(w_mlp2, (DFF, D), (256, 1024), lambda i, r: (4 * r[0] + i, 0), idx, "place_w2")
    wa0 = _place(w_branch_sc, (D, D), (256, 1024), lambda i, r: (r[0], 0), idx, "place_wa")
    wb0 = _place(w_branch_ssm, (INNER, D), (512, 1024), lambda i, r: (r[0], 0), idx, "place_wb")
    wo0 = _place(w_out, (D, D), (256, 1024), lambda i, r: (r[0], 0), idx, "place_wo")
    cws = jnp.zeros((8, 1280), F32)
    cws = cws.at[0:3, 0:256].set(sc_conv_w).at[0:4, 256:1280].set(ssm_conv_w)
    cw0 = lax.dynamic_update_slice(jnp.zeros((4, 8, 1280), F32), cws[None], (s, 0, 0))
    wc, xt, w1, w2, wa, wb, wo, cw_all = _all_gather_weights([wct0, xt0, w10, w20, wa0, wb0, wo0, cw0])
    wc = _fix_wct(wc, xt)
    sc_w_full = jnp.concatenate([cw_all[k, :, 0:256] for k in range(4)], axis=1)
    ssm_w_full = jnp.concatenate([cw_all[k, :, 256:1280] for k in range(4)], axis=1)
    cw4 = ssm_w_full.at[4].set(ssm_conv_b)
    vec = jnp.zeros((8, 128), F32).at[0, :NH].set(dt_bias).at[1, :NH].set(A_log)
    vecg = jnp.zeros((NG, 8, 128), F32).at[:, 0, :4].set(A_log.reshape(NG, 4)).at[:, 1, :4].set(D_skip.reshape(NG, 4))

    h = _rms_fwd(x0, norm_mix, "rms_mix")
    proj = _matmul(h, wc, "nt", F32, 512, 1280, 1024, "in_proj", n_outer=True)
    ya = _sc_fwd(proj, sc_w_full)
    xbc = _ssm_conv_fwd(proj, cw4)
    dt, cs, sg = _dt_prep(proj, vec)
    per_group = lambda a: jnp.pad(a[:, :NH].reshape(L, NG, 4).transpose(1, 0, 2), ((0, 0), (0, 0), (0, 124)))
    dt4, cs4, sg4 = per_group(dt), per_group(cs), per_group(sg)
    y, s_all = _ssd_fwd(xbc, dt4, cs4, vecg)
    yb = _gnorm_fwd(y, proj, ssm_norm_w)
    br_a = _matmul(ya, wa, "nn", F32, 512, 1024, 1024, "branch_sc")
    br_b = _matmul(yb, wb, "nn", F32, 512, 1024, 2048, "branch_ssm")
    merged = _merge_fwd(proj, b_gate, br_a, br_b)
    x1 = _matmul(merged, wo, "nn", F32, 512, 1024, 1024, "out_proj", epi="res", extra=x0)
    h2 = _rms_fwd(x1, norm_mlp, "rms_mlp")
    a1, rl = _matmul(h2, w1, "nn", F32, 512, 1024, 1024, "mlp1", epi="relu2", n_outer=True)
    x2 = _matmul(rl, w2, "nn", F32, 512, 1024, 2048, "mlp2", epi="res", extra=x1)
    dx2, g_nf, loss8 = _final(x2, norm_final, tgt)

    da = _matmul(dx2, w2, "nt", BF16, 512, 1024, 1024, "mlp2_dx", epi="drelu", extra=a1, n_outer=True)
    g_w2 = _matmul(rl, dx2, "tn", BF16, 1024, 1024, 512, "mlp2_dw")
    g_w1 = _matmul(h2, da, "tn", BF16, 1024, 1024, 512, "mlp1_dw")
    dh2 = _matmul(da, w1, "nt", F32, 512, 1024, 2048, "mlp1_dx")
    dx1, g_nmlp = _rms_bwd(dh2, x1, norm_mlp, dx2, "rms_mlp_bwd")
    dmerged = _matmul(dx1, wo, "nt", F32, 512, 1024, 1024, "out_proj_dx")
    g_wo = _matmul(merged, dx1, "tn", BF16, 1024, 1024, 512, "out_proj_dw")
    dproj = lax.empty((L, NCW), BF16)
    dbr, dproj, g_bg = _merge_bwd(dmerged, proj, b_gate, br_a, br_b, dproj)
    dya = _matmul(dbr[0], wa, "nt", F32, 512, 1024, 1024, "branch_sc_dx")
    g_wa = _matmul(ya, dbr[0], "tn", BF16, 1024, 1024, 512, "branch_sc_dw")
    dproj, g_scw = _sc_bwd(dya, proj, sc_w_full, dproj)
    dyb = _matmul(dbr[1], wb, "nt", F32, 512, 1024, 1024, "branch_ssm_dx", n_outer=True)
    g_wb = _matmul(yb, dbr[1], "tn", BF16, 1024, 1024, 512, "branch_ssm_dw")
    dy, dproj, g_snw = _gnorm_bwd(dyb, y, proj, ssm_norm_w, dproj)
    dxs, dbm, dcm, ddt_g, st = _ssd_bwd(xbc, dt4, cs4, sg4, vecg, s_all, dy)
    dproj, gx1 = _ssm_conv_bwd(dxs, proj, cw4, dproj, 0, "ssm_conv_bwd_x")
    dproj, gx2 = _ssm_conv_bwd(dbm, proj, cw4, dproj, INNER, "ssm_conv_bwd_b")
    dproj, gx3 = _ssm_conv_bwd(dcm, proj, cw4, dproj, INNER + NG * NS, "ssm_conv_bwd_c")
    g_cw4 = jnp.concatenate([gx1, gx2, gx3], axis=1)
    ddt = jnp.pad(ddt_g[:, :, :4].transpose(1, 0, 2).reshape(L, NH), ((0, 0), (0, 128 - NH)))
    dproj, g_dtb = _dt_bwd(ddt, dproj)
    g_wc = _matmul(dproj, h, "tn", BF16, 1280, 1024, 512, "in_proj_dw")
    dh = _matmul(dproj, wc, "nn", F32, 512, 1024, 2304, "in_proj_dx")
    grad_x, g_nm = _rms_bwd(dh, x0, norm_mix, dx1, "rms_mix_bwd")

    small = {"norm_mix": g_nm[0], "b_gate": g_bg[0], "ssm_conv_b": g_cw4[4], "dt_bias": g_dtb[0, :NH],
             "A_log": st[:, 0, :4], "D_skip": st[:, 1, :4], "ssm_norm_w": g_snw[0], "norm_mlp": g_nmlp[0],
             "norm_final": g_nf[0], "sc_conv_w": g_scw[0:3], "ssm_conv_w": g_cw4[0:4], "loss": loss8[0, 0:1]}
    gs = _unpack(_sum8(_small_all_gather(_pack(small, _SMALL, SMALL_ROWS))), _SMALL)

    grads = [g_wc, g_w1, g_w2, g_wa, g_wb, g_wo]
    ras = _rs_sibling(grads)
    sums = [_chip_sum(grads[t], ras[t], t, idx, "chip_sum_%d" % t) for t in range(6)]
    rbs = _rs_chips([hb for hb, _ in sums])
    fs = [_final_sum(sums[t][1], rbs[t], t, idx, "final_sum_%d" % t) for t in range(6)]
    gp, g1f, g2f, gaf, gbf, gof = _rs_share(fs)
    big = {"w_mlp1": g1f, "w_mlp2": g2f, "w_branch_sc": gaf, "w_branch_ssm": gbf, "w_out": gof}

    given = dict(norm_mix=norm_mix, w_in=w_in, b_gate=b_gate, sc_conv_w=sc_conv_w, ssm_conv_w=ssm_conv_w, ssm_conv_b=ssm_conv_b, dt_bias=dt_bias, A_log=A_log, D_skip=D_skip, ssm_norm_w=ssm_norm_w, w_branch_sc=w_branch_sc, w_branch_ssm=w_branch_ssm, w_out=w_out, norm_mlp=norm_mlp, w_mlp1=w_mlp1, w_mlp2=w_mlp2, norm_final=norm_final,
                 m_norm_mix=m_norm_mix, m_w_in=m_w_in, m_b_gate=m_b_gate, m_sc_conv_w=m_sc_conv_w, m_ssm_conv_w=m_ssm_conv_w, m_ssm_conv_b=m_ssm_conv_b, m_dt_bias=m_dt_bias, m_A_log=m_A_log, m_D_skip=m_D_skip, m_ssm_norm_w=m_ssm_norm_w, m_w_branch_sc=m_w_branch_sc, m_w_branch_ssm=m_w_branch_ssm, m_w_out=m_w_out, m_norm_mlp=m_norm_mlp, m_w_mlp1=m_w_mlp1, m_w_mlp2=m_w_mlp2, m_norm_final=m_norm_final,
                 v_norm_mix=v_norm_mix, v_w_in=v_w_in, v_b_gate=v_b_gate, v_sc_conv_w=v_sc_conv_w, v_ssm_conv_w=v_ssm_conv_w, v_ssm_conv_b=v_ssm_conv_b, v_dt_bias=v_dt_bias, v_A_log=v_A_log, v_D_skip=v_D_skip, v_ssm_norm_w=v_ssm_norm_w, v_w_branch_sc=v_w_branch_sc, v_w_branch_ssm=v_w_branch_ssm, v_w_out=v_w_out, v_norm_mlp=v_norm_mlp, v_w_mlp1=v_w_mlp1, v_w_mlp2=v_w_mlp2, v_norm_final=v_norm_final)
    order = ["norm_mix", "w_in", "b_gate", "sc_conv_w", "ssm_conv_w", "ssm_conv_b", "dt_bias", "A_log", "D_skip",
             "ssm_norm_w", "w_branch_sc", "w_branch_ssm", "w_out", "norm_mlp", "w_mlp1", "w_mlp2", "norm_final"]
    grad, delta, new_m, new_v = {}, {}, {}, {}
    for n in big:
        delta[n], new_m[n], new_v[n], grad[n] = _adamw(given[n], big[n], given["m_" + n], given["v_" + n],
                                                       "adamw_" + n, copy_g=True)
    gwt = _from_piece(gp, s)
    wt_args = (w_in.T, gwt, m_w_in.T, v_w_in.T)
    head = _adamw(*wt_args, "adamw_w_in", tr=256, nblk=W_SHARD // 256)
    dt_, mt_, vt_ = _adamw(*wt_args, "adamw_w_in_tail", tr=8, blk0=(W_SHARD // 256) * 32, nblk=1, into=head)
    grad["w_in"], delta["w_in"], new_m["w_in"], new_v["w_in"] = gwt.T, dt_.T, mt_.T, vt_.T
    big["w_in"] = None
    grad_small = {n: gs[n].reshape(given[n].shape) for n in order if n not in big and n not in ("sc_conv_w", "ssm_conv_w")}
    grad_small["sc_conv_w"] = lax.dynamic_slice(gs["sc_conv_w"].reshape(3, D), (0, 256 * s), (3, 256))
    grad_small["ssm_conv_w"] = lax.dynamic_slice(gs["ssm_conv_w"].reshape(4, XBC), (0, 1024 * s), (4, 1024))
    table = [(n, int(grad_small[n].size)) for n in grad_small]
    rows = 136
    pk = lambda d: _pack(d, table, rows)
    ds_, ms_, vs_ = _adamw(pk({n: given[n] for n in grad_small}), pk(grad_small), pk({n: given["m_" + n] for n in grad_small}),
                           pk({n: given["v_" + n] for n in grad_small}), "adamw_small", tr=rows)
    ds_, ms_, vs_ = _unpack(ds_, table), _unpack(ms_, table), _unpack(vs_, table)
    for n in grad_small:
        shp = given[n].shape
        grad[n] = grad_small[n]
        delta[n], new_m[n], new_v[n] = ds_[n].reshape(shp), ms_[n].reshape(shp), vs_[n].reshape(shp)

    loss = gs["loss"].reshape(())
    return (loss, grad_x.reshape(1, L, D), *[grad[n] for n in order], *[delta[n] for n in order],
            *[new_m[n] for n in order], *[new_v[n] for n in order])
```

```python
import functools

import jax
import jax.numpy as jnp
from jax import lax
from jax.experimental import pallas as pl
from jax.experimental.pallas import tpu as pltpu

F32 = jnp.float32
BF16 = jnp.bfloat16
MESH = pl.DeviceIdType.MESH
HBM = pltpu.HBM

D = 1024
INNER = 2048
HD = 64
NH = 32
NG = 8
NS = 128
Q = 128
XBC = 4096
DFF = 4096
EPS = 1e-6
W_SHARD = 2824
NCW = 11520
PIECE = 3072
PMAIN = 2816
C_Z, C_XBC, C_GATE, C_DT = 3072, 5120, 9216, 11264
SMALL_ROWS = 256
VMEM_LIMIT = 56 * 1024 * 1024

ADAM_LR, ADAM_B1, ADAM_B2, ADAM_EPS, ADAM_WD, ADAM_STEP = 0.001, 0.9, 0.999, 1e-08, 0.01, 10


def _cp(sem=None, vmem=VMEM_LIMIT):
    return pltpu.CompilerParams(dimension_semantics=sem, vmem_limit_bytes=vmem)


def _sigmoid(v):
    return 1.0 / (1.0 + jnp.exp(-v))


_DIMS = {"nn": (((1,), (0,)), ((), ())), "nt": (((1,), (1,)), ((), ())), "tn": (((0,), (0,)), ((), ()))}


def _matmul(a, b, mode, out_dtype, tm, tn, tk, name, epi=None, extra=None, n_outer=False):
    if mode == "tn":
        K, M = a.shape
    else:
        M, K = a.shape
    N = b.shape[0] if mode == "nt" else b.shape[1]
    tm, tn, tk = min(tm, M), min(tn, N), min(tk, K)
    assert M % tm == 0 and N % tn == 0 and K % tk == 0, (name, M, N, K, tm, tn, tk)
    nm, nn, nk = M // tm, N // tn, K // tk
    dims = _DIMS[mode]

    def ij(p0, p1):
        return (p1, p0) if n_outer else (p0, p1)

    if mode == "tn":
        a_spec = pl.BlockSpec((tk, tm), lambda p0, p1, k: (k, ij(p0, p1)[0]))
    else:
        a_spec = pl.BlockSpec((tm, tk), lambda p0, p1, k: (ij(p0, p1)[0], k))
    if mode == "nt":
        b_spec = pl.BlockSpec((tn, tk), lambda p0, p1, k: (ij(p0, p1)[1], k))
    else:
        b_spec = pl.BlockSpec((tk, tn), lambda p0, p1, k: (k, ij(p0, p1)[1]))
    o_spec = pl.BlockSpec((tm, tn), lambda p0, p1, k: ij(p0, p1))
    in_specs = [a_spec, b_spec]
    args = [a, b]
    if epi in ("res", "drelu"):
        in_specs.append(o_spec)
        args.append(extra)
    if epi == "relu2":
        out_shape = (jax.ShapeDtypeStruct((M, N), F32), jax.ShapeDtypeStruct((M, N), BF16))
        out_specs = (o_spec, o_spec)
    else:
        out_shape = jax.ShapeDtypeStruct((M, N), out_dtype)
        out_specs = o_spec

    def kern(*refs):
        a_ref, b_ref = refs[0], refs[1]
        e_ref = refs[2] if epi in ("res", "drelu") else None
        acc = refs[-1]
        outs = refs[(3 if e_ref is not None else 2):-1]
        k = pl.program_id(2)

        @pl.when(k == 0)
        def _():
            acc[...] = jnp.zeros_like(acc)

        acc[...] += lax.dot_general(a_ref[...].astype(BF16), b_ref[...].astype(BF16), dims,
                                    preferred_element_type=F32)

        @pl.when(k == nk - 1)
        def _():
            r = acc[...]
            if epi is None:
                outs[0][...] = r.astype(out_dtype)
            elif epi == "res":
                outs[0][...] = (r + e_ref[...]).astype(out_dtype)
            elif epi == "relu2":
                outs[0][...] = r
                t = jnp.maximum(r, 0.0)
                outs[1][...] = (t * t).astype(BF16)
            else:
                outs[0][...] = (r * (2.0 * jnp.maximum(e_ref[...], 0.0))).astype(out_dtype)

    grid = (nn, nm, nk) if n_outer else (nm, nn, nk)
    return pl.pallas_call(
        kern, grid=grid, in_specs=in_specs, out_specs=out_specs, out_shape=out_shape,
        scratch_shapes=[pltpu.VMEM((tm, tn), F32)], name=name,
        compiler_params=_cp(("parallel", "parallel", "arbitrary")),
    )(*args)


def _rms_fwd(x, w, name, tl=256):
    L = x.shape[0]

    def kern(x_ref, w_ref, o_ref):
        xv = x_ref[...]
        r = lax.rsqrt(jnp.mean(xv * xv, axis=-1, keepdims=True) + EPS)
        o_ref[...] = ((xv * r) * w_ref[...]).astype(BF16)

    row = pl.BlockSpec((tl, D), lambda i: (i, 0))
    return pl.pallas_call(
        kern, grid=(L // tl,), in_specs=[row, pl.BlockSpec((1, D), lambda i: (0, 0))], out_specs=row,
        out_shape=jax.ShapeDtypeStruct((L, D), BF16), name=name, compiler_params=_cp(("parallel",)),
    )(x, w.reshape(1, D))


def _rms_bwd(dy, x, w, res, name, tl=256):
    L = x.shape[0]

    def kern(dy_ref, x_ref, w_ref, res_ref, dx_ref, gw_ref):
        @pl.when(pl.program_id(0) == 0)
        def _():
            gw_ref[...] = jnp.zeros_like(gw_ref)

        xv = x_ref[...]
        dyv = dy_ref[...]
        r = lax.rsqrt(jnp.mean(xv * xv, axis=-1, keepdims=True) + EPS)
        xn = xv * r
        gw_ref[...] += jnp.broadcast_to(jnp.sum(dyv * xn, axis=0, keepdims=True), (8, D))
        dxn = dyv * w_ref[...]
        dx_ref[...] = res_ref[...] + r * (dxn - xn * jnp.mean(dxn * xn, axis=-1, keepdims=True))

    row = pl.BlockSpec((tl, D), lambda i: (i, 0))
    return pl.pallas_call(
        kern, grid=(L // tl,), in_specs=[row, row, pl.BlockSpec((1, D), lambda i: (0, 0)), row],
        out_specs=(row, pl.BlockSpec((8, D), lambda i: (0, 0))),
        out_shape=(jax.ShapeDtypeStruct((L, D), F32), jax.ShapeDtypeStruct((8, D), F32)),
        name=name, compiler_params=_cp(("arbitrary",)),
    )(dy, x, w.reshape(1, D), res)


def _final(x2, w, tgt, tl=256):
    L = x2.shape[0]

    def kern(x_ref, w_ref, t_ref, dx_ref, gw_ref, loss_ref):
        @pl.when(pl.program_id(0) == 0)
        def _():
            gw_ref[...] = jnp.zeros_like(gw_ref)
            loss_ref[...] = jnp.zeros_like(loss_ref)

        xv = x_ref[...]
        r = lax.rsqrt(jnp.mean(xv * xv, axis=-1, keepdims=True) + EPS)
        xn = xv * r
        e = xn * w_ref[...] - t_ref[...]
        per_tok = jnp.mean(e * e, axis=-1, keepdims=True)
        loss_ref[...] += 0.5 * jnp.sum(per_tok)
        dyv = e * (1.0 / D)
        gw_ref[...] += jnp.broadcast_to(jnp.sum(dyv * xn, axis=0, keepdims=True), (8, D))
        dxn = dyv * w_ref[...]
        dx_ref[...] = r * (dxn - xn * jnp.mean(dxn * xn, axis=-1, keepdims=True))

    row = pl.BlockSpec((tl, D), lambda i: (i, 0))
    return pl.pallas_call(
        kern, grid=(L // tl,), in_specs=[row, pl.BlockSpec((1, D), lambda i: (0, 0)), row],
        out_specs=(row, pl.BlockSpec((8, D), lambda i: (0, 0)), pl.BlockSpec((8, 128), lambda i: (0, 0))),
        out_shape=(jax.ShapeDtypeStruct((L, D), F32), jax.ShapeDtypeStruct((8, D), F32),
                   jax.ShapeDtypeStruct((8, 128), F32)),
        name="final_norm_loss", compiler_params=_cp(("arbitrary",)),
    )(x2, w.reshape(1, D), tgt)


def _down(v, k):
    if k == 0:
        return v
    t = lax.broadcasted_iota(jnp.int32, v.shape, 0)
    return jnp.where(t >= k, pltpu.roll(v, k, axis=0), 0.0)


def _up(v, k):
    if k == 0:
        return v
    n = v.shape[0]
    t = lax.broadcasted_iota(jnp.int32, v.shape, 0)
    return jnp.where(t < n - k, pltpu.roll(v, n - k, axis=0), 0.0)


TW = 256


def _sc_fwd(proj, cw):
    L = proj.shape[0]
    nb = D // TW

    def kern(b_ref, c_ref, x_ref, w_ref, o_ref):
        u = c_ref[...] * x_ref[...]
        w = w_ref[...]
        cv = w[0:1] * _down(u, 2) + w[1:2] * _down(u, 1) + w[2:3] * u
        o_ref[...] = (b_ref[...] * cv).astype(BF16)

    col = lambda off: pl.BlockSpec((L, TW), lambda j: (0, off + j))
    return pl.pallas_call(
        kern, grid=(nb,), in_specs=[col(0), col(nb), col(2 * nb), pl.BlockSpec((8, TW), lambda j: (0, j))],
        out_specs=pl.BlockSpec((L, TW), lambda j: (0, j)), out_shape=jax.ShapeDtypeStruct((L, D), BF16),
        name="sc_fwd", compiler_params=_cp(("parallel",)),
    )(proj, proj, proj, cw)


def _sc_bwd(dya, proj, cw, dproj):
    L = proj.shape[0]
    nb = D // TW

    def kern(d_ref, b_ref, c_ref, x_ref, w_ref, _, dp_ref, gw_ref):
        sec = pl.program_id(1)
        cs, xs, dyv = c_ref[...], x_ref[...], d_ref[...]
        w = w_ref[...]
        u = cs * xs
        u1, u2 = _down(u, 1), _down(u, 2)
        cv = w[0:1] * u2 + w[1:2] * u1 + w[2:3] * u
        dcv = dyv * b_ref[...]
        du = w[2:3] * dcv + w[1:2] * _up(dcv, 1) + w[0:1] * _up(dcv, 2)
        g0 = jnp.sum(dcv * u2, axis=0, keepdims=True)
        g1 = jnp.sum(dcv * u1, axis=0, keepdims=True)
        g2 = jnp.sum(dcv * u, axis=0, keepdims=True)
        row = lax.broadcasted_iota(jnp.int32, (8, TW), 0)
        gw_ref[...] = jnp.where(row == 0, g0, jnp.where(row == 1, g1, jnp.where(row == 2, g2, 0.0)))
        out = jnp.where(sec == 0, dyv * cv, jnp.where(sec == 1, du * xs, du * cs))
        dp_ref[...] = out.astype(BF16)

    col = lambda off: pl.BlockSpec((L, TW), lambda j, s: (0, off + j))
    return pl.pallas_call(
        kern, grid=(nb, 3),
        in_specs=[col(0), col(0), col(nb), col(2 * nb), pl.BlockSpec((8, TW), lambda j, s: (0, j)),
                  pl.BlockSpec(memory_space=pl.ANY)],
        out_specs=(pl.BlockSpec((L, TW), lambda j, s: (0, s * nb + j)), pl.BlockSpec((8, TW), lambda j, s: (0, j))),
        out_shape=(jax.ShapeDtypeStruct(dproj.shape, BF16), jax.ShapeDtypeStruct((8, D), F32)),
        input_output_aliases={5: 0}, name="sc_bwd", compiler_params=_cp(("parallel", "arbitrary")),
    )(dya, proj, proj, proj, cw, dproj)


def _ssm_conv_fwd(proj, cw4):
    L = proj.shape[0]
    off = C_XBC // TW

    def kern(r_ref, w_ref, o_ref):
        raw = r_ref[...]
        w = w_ref[...]
        c4 = w[0:1] * _down(raw, 3) + w[1:2] * _down(raw, 2) + w[2:3] * _down(raw, 1) + w[3:4] * raw + w[4:5]
        o_ref[...] = c4 * _sigmoid(c4)

    return pl.pallas_call(
        kern, grid=(XBC // TW,),
        in_specs=[pl.BlockSpec((L, TW), lambda j: (0, off + j)), pl.BlockSpec((8, TW), lambda j: (0, j))],
        out_specs=pl.BlockSpec((L, TW), lambda j: (0, j)), out_shape=jax.ShapeDtypeStruct((L, XBC), F32),
        name="ssm_conv_fwd", compiler_params=_cp(("parallel",)),
    )(proj, cw4)


def _ssm_conv_bwd(dx, proj, cw4, dproj, col0, name):
    L, width = dx.shape
    off_p = (C_XBC + col0) // TW
    off_w = col0 // TW

    def kern(d_ref, r_ref, w_ref, _, dp_ref, gw_ref):
        raw = r_ref[...]
        w = w_ref[...]
        r1, r2, r3 = _down(raw, 1), _down(raw, 2), _down(raw, 3)
        c4 = w[0:1] * r3 + w[1:2] * r2 + w[2:3] * r1 + w[3:4] * raw + w[4:5]
        sg = _sigmoid(c4)
        dc4 = d_ref[...] * (sg * (1.0 + c4 * (1.0 - sg)))
        draw = w[3:4] * dc4 + w[2:3] * _up(dc4, 1) + w[1:2] * _up(dc4, 2) + w[0:1] * _up(dc4, 3)
        dp_ref[...] = draw.astype(BF16)
        gs = [jnp.sum(dc4 * r3, axis=0, keepdims=True), jnp.sum(dc4 * r2, axis=0, keepdims=True),
              jnp.sum(dc4 * r1, axis=0, keepdims=True), jnp.sum(dc4 * raw, axis=0, keepdims=True),
              jnp.sum(dc4, axis=0, keepdims=True)]
        row = lax.broadcasted_iota(jnp.int32, (8, TW), 0)
        acc = jnp.zeros((8, TW), F32)
        for k, gk in enumerate(gs):
            acc = jnp.where(row == k, gk, acc)
        gw_ref[...] = acc

    return pl.pallas_call(
        kern, grid=(width // TW,),
        in_specs=[pl.BlockSpec((L, TW), lambda j: (0, j)), pl.BlockSpec((L, TW), lambda j: (0, off_p + j)),
                  pl.BlockSpec((8, TW), lambda j: (0, off_w + j)), pl.BlockSpec(memory_space=pl.ANY)],
        out_specs=(pl.BlockSpec((L, TW), lambda j: (0, off_p + j)), pl.BlockSpec((8, TW), lambda j: (0, j))),
        out_shape=(jax.ShapeDtypeStruct(dproj.shape, BF16), jax.ShapeDtypeStruct((8, width), F32)),
        input_output_aliases={3: 0}, name=name, compiler_params=_cp(("arbitrary",)),
    )(dx, proj, cw4, dproj)


def _split3(v):
    h1 = v.astype(BF16)
    r1 = v - h1.astype(F32)
    h2 = r1.astype(BF16)
    h3 = (r1 - h2.astype(F32)).astype(BF16)
    return h1, h2, h3


def _dot01(m01, v, dims=_DIMS["nn"], m_left=True):
    out = None
    for part in _split3(v):
        ops = (m01, part) if m_left else (part, m01)
        t = lax.dot_general(ops[0], ops[1], dims, preferred_element_type=F32)
        out = t if out is None else out + t
    return out


def _bdot(a, b, mode="nn"):
    return lax.dot_general(a.astype(BF16), b.astype(BF16), _DIMS[mode], preferred_element_type=F32)


def _softplus(v):
    return jnp.maximum(v, 0.0) + jnp.log1p(jnp.exp(-jnp.abs(v)))


def _dt_prep(proj, vec):
    L = proj.shape[0]

    def kern(p_ref, v_ref, dt_ref, cs_ref, sg_ref):
        v = v_ref[...]
        pre = p_ref[:, 0:128] + v[0:1]
        dt = _softplus(pre)
        sg_ref[...] = _sigmoid(pre)
        da = dt * (-jnp.exp(v[1:2]))
        ii = lax.broadcasted_iota(jnp.int32, (Q, Q), 0)
        jj = lax.broadcasted_iota(jnp.int32, (Q, Q), 1)
        ltri = (jj <= ii).astype(BF16)
        dt_ref[...] = dt
        cs_ref[...] = _dot01(ltri, da)

    blk = pl.BlockSpec((Q, 128), lambda c: (c, 0))
    return pl.pallas_call(
        kern, grid=(L // Q,),
        in_specs=[pl.BlockSpec((Q, 256), lambda c: (c, C_DT // 256)), pl.BlockSpec((8, 128), lambda c: (0, 0))],
        out_specs=(blk, blk, blk),
        out_shape=(jax.ShapeDtypeStruct((L, 128), F32),) * 3,
        name="dt_prep", compiler_params=_cp(("parallel",)),
    )(proj, vec)


def _head_masks():
    lane = lax.broadcasted_iota(jnp.int32, (1, 4 * HD), 1)
    return [((lane >= HD * j) & (lane < HD * (j + 1))) for j in range(4)]


def _expand4(v4, masks):
    R = v4.shape[0]
    out = jnp.zeros((R, 4 * HD), F32)
    for j in range(4):
        out = jnp.where(masks[j], jnp.broadcast_to(v4[:, j:j + 1], (R, 4 * HD)), out)
    return out


def _decay_matrix(cs_col, tri):
    colb = jnp.broadcast_to(cs_col, (Q, Q))
    return jnp.exp(jnp.where(tri, colb - colb.T, -jnp.inf))


def _ssd_fwd(xbc, dt4, cs4, vecg):
    L = xbc.shape[0]
    nc = L // Q

    def kern(x_ref, b_ref, c_ref, dt_ref, cs_ref, v_ref, y_ref, s_ref, S):
        c = pl.program_id(1)

        @pl.when(c == 0)
        def _():
            S[...] = jnp.zeros_like(S)

        masks = _head_masks()
        ii = lax.broadcasted_iota(jnp.int32, (Q, Q), 0)
        jj = lax.broadcasted_iota(jnp.int32, (Q, Q), 1)
        tri = jj <= ii
        dt4v, cs4v = dt_ref[0], cs_ref[0]
        dt_b, cs_b = _expand4(dt4v, masks), _expand4(cs4v, masks)
        d_b = _expand4(v_ref[0], masks)[1:2]
        cs_last = cs_b[Q - 1:Q, :]
        x4, bm, cm = x_ref[...], b_ref[...], c_ref[...]
        xdt = x4 * dt_b
        gm = _bdot(cm, bm, "nt")
        s4 = S[...]
        s_ref[0, 0] = s4
        y = _bdot(cm, s4) * jnp.exp(cs_b) + d_b * x4
        for j in range(4):
            mh = gm * _decay_matrix(cs4v[:, j:j + 1], tri)
            y = y + _bdot(mh, jnp.where(masks[j], xdt, 0.0))
        y_ref[...] = y
        S[...] = jnp.exp(cs_last) * s4 + _bdot(bm, xdt * jnp.exp(cs_last - cs_b), "tn")

    sc = pl.BlockSpec((1, Q, 128), lambda g, c: (g, c, 0))
    return pl.pallas_call(
        kern, grid=(NG, nc),
        in_specs=[pl.BlockSpec((Q, 256), lambda g, c: (c, g)),
                  pl.BlockSpec((Q, 128), lambda g, c: (c, INNER // 128 + g)),
                  pl.BlockSpec((Q, 128), lambda g, c: (c, (INNER + NG * NS) // 128 + g)),
                  sc, sc, pl.BlockSpec((1, 8, 128), lambda g, c: (g, 0, 0))],
        out_specs=(pl.BlockSpec((Q, 256), lambda g, c: (c, g)),
                   pl.BlockSpec((1, 1, NS, 256), lambda g, c: (g, c, 0, 0))),
        out_shape=(jax.ShapeDtypeStruct((L, INNER), F32), jax.ShapeDtypeStruct((NG, nc, NS, 256), F32)),
        scratch_shapes=[pltpu.VMEM((NS, 256), F32)], name="ssd_fwd",
        compiler_params=_cp(("parallel", "arbitrary")),
    )(xbc, xbc, xbc, dt4, cs4, vecg)


def _ssd_bwd(xbc, dt4, cs4, sg4, vecg, s_all, dy):
    L = xbc.shape[0]
    nc = L // Q

    def kern(x_ref, b_ref, c_ref, dt_ref, cs_ref, sg_ref, v_ref, s_ref, dy_ref,
             dx_ref, db_ref, dc_ref, ddt_ref, st_ref, dS):
        cc = pl.program_id(1)

        @pl.when(cc == 0)
        def _():
            dS[...] = jnp.zeros_like(dS)
            st_ref[...] = jnp.zeros_like(st_ref)

        masks = _head_masks()
        ii = lax.broadcasted_iota(jnp.int32, (Q, Q), 0)
        jj = lax.broadcasted_iota(jnp.int32, (Q, Q), 1)
        tri = jj <= ii
        utri = (jj >= ii).astype(BF16)
        li = lax.broadcasted_iota(jnp.int32, (4 * HD, 4 * HD), 0)
        lj = lax.broadcasted_iota(jnp.int32, (4 * HD, 4 * HD), 1)
        eblk = ((li // HD) == (lj // HD)).astype(BF16)
        lane128 = lax.broadcasted_iota(jnp.int32, (Q, 128), 1)

        dt4v, cs4v, sg4v = dt_ref[0], cs_ref[0], sg_ref[0]
        dt_b, cs_b = _expand4(dt4v, masks), _expand4(cs4v, masks)
        vv = _expand4(v_ref[0], masks)
        a_b = -jnp.exp(vv[0:1])
        d_b = vv[1:2]
        a4 = -jnp.exp(v_ref[0][0:1, :])
        cs_last = cs_b[Q - 1:Q, :]
        ecs = jnp.exp(cs_b)
        decay = jnp.exp(cs_last - cs_b)
        elast = jnp.exp(cs_last)
        x4, bm, cm, dyv = x_ref[...], b_ref[...], c_ref[...], dy_ref[...]
        s4 = s_ref[0, 0]
        dsn = dS[...]
        xdt = x4 * dt_b
        gm = _bdot(cm, bm, "nt")
        gmt = gm.T
        dye = dyv * ecs
        yoff = ecs * _bdot(cm, s4)
        t4 = _bdot(bm, dsn) * decay
        dxdt = t4
        dg = jnp.zeros((Q, Q), F32)
        rc = jnp.zeros((Q, 4 * HD), F32)
        for j in range(4):
            colb = jnp.broadcast_to(cs4v[:, j:j + 1], (Q, Q))
            seg = colb - colb.T
            lm = jnp.exp(jnp.where(tri, seg, -jnp.inf))
            lmt = jnp.exp(jnp.where(jj >= ii, -seg, -jnp.inf))
            mh = gm * lm
            mht = gmt * lmt
            dyh = jnp.where(masks[j], dyv, 0.0).astype(BF16)
            xh = jnp.where(masks[j], xdt, 0.0).astype(BF16)
            dxdt = dxdt + _bdot(mh, dyh, "tn")
            dmh = _bdot(dyh, xh, "nt")
            dmht = _bdot(xh, dyh, "nt")
            dg = dg + dmh * lm
            rs = jnp.sum(dmh * mh, axis=1, keepdims=True) - jnp.sum(dmht * mht, axis=1, keepdims=True)
            rc = jnp.where(masks[j], jnp.broadcast_to(rs, (Q, 4 * HD)), rc)
        xt = xdt * t4
        da_b = _dot01(eblk, dyv * yoff - xt, m_left=False) + rc
        tail = jnp.sum(xt, axis=0, keepdims=True) + elast * jnp.sum(s4 * dsn, axis=0, keepdims=True)
        tail = _dot01(eblk, jnp.broadcast_to(tail, (8, 4 * HD)), m_left=False)[0:1]
        dda_b = _dot01(utri, da_b) + tail
        ddt_b = dda_b * a_b + _dot01(eblk, dxdt * x4, m_left=False)
        ddt4 = jnp.zeros((Q, 128), F32)
        dda4 = jnp.zeros((Q, 128), F32)
        for j in range(4):
            ddt4 = jnp.where(lane128 == j, jnp.broadcast_to(ddt_b[:, HD * j:HD * j + 1], (Q, 128)), ddt4)
            dda4 = jnp.where(lane128 == j, jnp.broadcast_to(dda_b[:, HD * j:HD * j + 1], (Q, 128)), dda4)
        ddt_ref[0] = ddt4 * sg4v
        ga = jnp.sum(dda4 * dt4v * a4, axis=0, keepdims=True)
        gd_b = _dot01(eblk, jnp.broadcast_to(jnp.sum(dyv * x4, axis=0, keepdims=True), (8, 4 * HD)), m_left=False)[0:1]
        gd = jnp.zeros((1, 128), F32)
        for j in range(4):
            gd = jnp.where(lane128[0:1] == j, jnp.broadcast_to(gd_b[:, HD * j:HD * j + 1], (1, 128)), gd)
        row = lax.broadcasted_iota(jnp.int32, (8, 128), 0)
        st_ref[0] += jnp.where(row == 0, ga, jnp.where(row == 1, gd, 0.0))
        dx_ref[...] = d_b * dyv + dxdt * dt_b
        dc_ref[...] = _bdot(dg, bm) + _bdot(dye, s4, "nt")
        db_ref[...] = _bdot(dg, cm, "tn") + _bdot(xdt * decay, dsn, "nt")
        dS[...] = elast * dsn + _bdot(cm, dye, "tn")

    rv = lambda c: nc - 1 - c
    sc = pl.BlockSpec((1, Q, 128), lambda g, c: (g, rv(c), 0))
    return pl.pallas_call(
        kern, grid=(NG, nc),
        in_specs=[pl.BlockSpec((Q, 256), lambda g, c: (rv(c), g)),
                  pl.BlockSpec((Q, 128), lambda g, c: (rv(c), INNER // 128 + g)),
                  pl.BlockSpec((Q, 128), lambda g, c: (rv(c), (INNER + NG * NS) // 128 + g)),
                  sc, sc, sc, pl.BlockSpec((1, 8, 128), lambda g, c: (g, 0, 0)),
                  pl.BlockSpec((1, 1, NS, 256), lambda g, c: (g, rv(c), 0, 0)),
                  pl.BlockSpec((Q, 256), lambda g, c: (rv(c), g))],
        out_specs=(pl.BlockSpec((Q, 256), lambda g, c: (rv(c), g)),
                   pl.BlockSpec((Q, 128), lambda g, c: (rv(c), g)),
                   pl.BlockSpec((Q, 128), lambda g, c: (rv(c), g)),
                   pl.BlockSpec((1, Q, 128), lambda g, c: (g, rv(c), 0)),
                   pl.BlockSpec((1, 8, 128), lambda g, c: (g, 0, 0))),
        out_shape=(jax.ShapeDtypeStruct((L, INNER), F32), jax.ShapeDtypeStruct((L, NG * NS), F32),
                   jax.ShapeDtypeStruct((L, NG * NS), F32), jax.ShapeDtypeStruct((NG, L, 128), F32),
                   jax.ShapeDtypeStruct((NG, 8, 128), F32)),
        scratch_shapes=[pltpu.VMEM((NS, 256), F32)], name="ssd_bwd",
        compiler_params=_cp(("parallel", "arbitrary")),
    )(xbc, xbc, xbc, dt4, cs4, sg4, vecg, s_all, dy)


def _dt_bwd(ddt, dproj, tl=256):
    L = ddt.shape[0]

    def kern(d_ref, _, dp_ref, gs_ref):
        @pl.when(pl.program_id(0) == 0)
        def _():
            gs_ref[...] = jnp.zeros_like(gs_ref)

        d = d_ref[...]
        gs_ref[...] += jnp.broadcast_to(jnp.sum(d, axis=0, keepdims=True), (8, 128))
        dp_ref[...] = jnp.concatenate([d, jnp.zeros_like(d)], axis=1).astype(BF16)

    return pl.pallas_call(
        kern, grid=(L // tl,),
        in_specs=[pl.BlockSpec((tl, 128), lambda i: (i, 0)), pl.BlockSpec(memory_space=pl.ANY)],
        out_specs=(pl.BlockSpec((tl, 256), lambda i: (i, C_DT // 256)), pl.BlockSpec((8, 128), lambda i: (0, 0))),
        out_shape=(jax.ShapeDtypeStruct(dproj.shape, BF16), jax.ShapeDtypeStruct((8, 128), F32)),
        input_output_aliases={1: 0}, name="dt_bwd", compiler_params=_cp(("arbitrary",)),
    )(ddt, dproj)


GW = INNER // NG


def _gnorm_fwd(y, proj, w, tl=256):
    L = y.shape[0]
    zoff = C_Z // 1024

    def kern(y_ref, z_ref, w_ref, o_ref):
        z = z_ref[...]
        yz = y_ref[...] * (z * _sigmoid(z))
        wv = w_ref[...]
        for k in range(1024 // GW):
            sl = slice(GW * k, GW * (k + 1))
            v = yz[:, sl]
            rg = lax.rsqrt(jnp.mean(v * v, axis=-1, keepdims=True) + EPS)
            o_ref[:, sl] = ((v * rg) * wv[:, sl]).astype(BF16)

    blk = pl.BlockSpec((tl, 1024), lambda i, j: (i, j))
    return pl.pallas_call(
        kern, grid=(L // tl, 2),
        in_specs=[blk, pl.BlockSpec((tl, 1024), lambda i, j: (i, zoff + j)), pl.BlockSpec((1, 1024), lambda i, j: (0, j))],
        out_specs=blk, out_shape=jax.ShapeDtypeStruct((L, INNER), BF16), name="gnorm_fwd",
        compiler_params=_cp(("parallel", "parallel")),
    )(y, proj, w.reshape(1, INNER))


def _gnorm_bwd(dyb, y, proj, w, dproj, tl=256):
    L = y.shape[0]
    zoff = C_Z // 1024

    def kern(d_ref, y_ref, z_ref, w_ref, _, dy_ref, dp_ref, gw_ref):
        @pl.when(pl.program_id(1) == 0)
        def _():
            gw_ref[...] = jnp.zeros_like(gw_ref)

        z = z_ref[...]
        sg = _sigmoid(z)
        sz = z * sg
        yv = y_ref[...]
        yz = yv * sz
        dv = d_ref[...]
        wv = w_ref[...]
        for k in range(1024 // GW):
            sl = slice(GW * k, GW * (k + 1))
            v = yz[:, sl]
            rg = lax.rsqrt(jnp.mean(v * v, axis=-1, keepdims=True) + EPS)
            vn = v * rg
            dk = dv[:, sl]
            gw_ref[:, sl] += jnp.broadcast_to(jnp.sum(dk * vn, axis=0, keepdims=True), (8, GW))
            dvn = dk * wv[:, sl]
            dyz = rg * (dvn - vn * jnp.mean(dvn * vn, axis=-1, keepdims=True))
            dy_ref[:, sl] = dyz * sz[:, sl]
            dp_ref[:, sl] = (dyz * yv[:, sl] * (sg[:, sl] * (1.0 + z[:, sl] * (1.0 - sg[:, sl])))).astype(BF16)

    blk = pl.BlockSpec((tl, 1024), lambda j, i: (i, j))
    zblk = pl.BlockSpec((tl, 1024), lambda j, i: (i, zoff + j))
    return pl.pallas_call(
        kern, grid=(2, L // tl),
        in_specs=[blk, blk, zblk, pl.BlockSpec((1, 1024), lambda j, i: (0, j)), pl.BlockSpec(memory_space=pl.ANY)],
        out_specs=(blk, zblk, pl.BlockSpec((8, 1024), lambda j, i: (0, j))),
        out_shape=(jax.ShapeDtypeStruct((L, INNER), F32), jax.ShapeDtypeStruct(dproj.shape, BF16),
                   jax.ShapeDtypeStruct((8, INNER), F32)),
        input_output_aliases={4: 1}, name="gnorm_bwd", compiler_params=_cp(("parallel", "arbitrary")),
    )(dyb, y, proj, w.reshape(1, INNER), dproj)


def _merge_fwd(proj, bg, br_a, br_b, tl=256):
    L = proj.shape[0]
    goff = C_GATE // 1024

    def kern(g1_ref, g2_ref, b1_ref, b2_ref, a_ref, b_ref, o_ref):
        g1 = _sigmoid(g1_ref[...] + b1_ref[...])
        g2 = _sigmoid(g2_ref[...] + b2_ref[...])
        o_ref[...] = (g1 * a_ref[...] + g2 * b_ref[...]).astype(BF16)

    row = pl.BlockSpec((tl, 1024), lambda i: (i, 0))
    bg2 = bg.reshape(1, 2 * D)
    return pl.pallas_call(
        kern, grid=(L // tl,),
        in_specs=[pl.BlockSpec((tl, 1024), lambda i: (i, goff)), pl.BlockSpec((tl, 1024), lambda i: (i, goff + 1)),
                  pl.BlockSpec((1, 1024), lambda i: (0, 0)), pl.BlockSpec((1, 1024), lambda i: (0, 1)), row, row],
        out_specs=row, out_shape=jax.ShapeDtypeStruct((L, D), BF16), name="merge_fwd",
        compiler_params=_cp(("parallel",)),
    )(proj, proj, bg2, bg2, br_a, br_b)


def _merge_bwd(dm, proj, bg, br_a, br_b, dproj, tl=256):
    L = proj.shape[0]
    goff = C_GATE // 1024

    def kern(dm_ref, g_ref, b_ref, a_ref, bb_ref, _, dbr_ref, dp_ref, gb_ref):
        j = pl.program_id(0)

        @pl.when(pl.program_id(1) == 0)
        def _():
            gb_ref[...] = jnp.zeros_like(gb_ref)

        g = _sigmoid(g_ref[...] + b_ref[...])
        br = jnp.where(j == 0, a_ref[...], bb_ref[...])
        dmv = dm_ref[...]
        dbr_ref[0] = (dmv * g).astype(BF16)
        dgate = dmv * br * g * (1.0 - g)
        gb_ref[...] += jnp.broadcast_to(jnp.sum(dgate, axis=0, keepdims=True), (8, 1024))
        dp_ref[...] = dgate.astype(BF16)

    row = pl.BlockSpec((tl, 1024), lambda j, i: (i, 0))
    gblk = pl.BlockSpec((tl, 1024), lambda j, i: (i, goff + j))
    return pl.pallas_call(
        kern, grid=(2, L // tl),
        in_specs=[row, gblk, pl.BlockSpec((1, 1024), lambda j, i: (0, j)), row, row, pl.BlockSpec(memory_space=pl.ANY)],
        out_specs=(pl.BlockSpec((1, tl, 1024), lambda j, i: (j, i, 0)), gblk, pl.BlockSpec((8, 1024), lambda j, i: (0, j))),
        out_shape=(jax.ShapeDtypeStruct((2, L, D), BF16), jax.ShapeDtypeStruct(dproj.shape, BF16),
                   jax.ShapeDtypeStruct((8, 2 * D), F32)),
        input_output_aliases={5: 1}, name="merge_bwd", compiler_params=_cp(("parallel", "arbitrary")),
    )(dm, proj, bg.reshape(1, 2 * D), br_a, br_b, dproj)


def _coords():
    return lax.axis_index("x"), lax.axis_index("y"), lax.axis_index("c")


def _other_chips(sk):
    xk, yk = sk // 2, sk % 2
    return [((1 - xk, yk), 2 * (1 - xk) + yk), ((xk, 1 - yk), 2 * xk + 1 - yk), ((1 - xk, 1 - yk), 2 * (1 - xk) + 1 - yk)]


def _rows(start, size):
    assert size % 128 == 0
    return pl.ds(pl.multiple_of(start, 128), size)


def _per_chip(fn):
    x, y, _ = _coords()
    s = 2 * x + y
    for sk in range(4):
        pl.when(s == sk)(functools.partial(fn, sk))


XTRA = PIECE - PMAIN


def _place(shard, full_shape, block, index_map, idx, name, blk0=0, nblk=None):
    in_block = block[-2:]
    if nblk is None:
        nblk = shard.shape[0] // in_block[0]

    def kern(idx_ref, s_ref, o_ref):
        o_ref[...] = s_ref[...].astype(BF16).reshape(o_ref.shape)

    grid_spec = pltpu.PrefetchScalarGridSpec(
        num_scalar_prefetch=1, grid=(nblk,),
        in_specs=[pl.BlockSpec(in_block, lambda i, idx_ref: (blk0 + i, 0))],
        out_specs=pl.BlockSpec(block, index_map))
    return pl.pallas_call(kern, grid_spec=grid_spec, out_shape=jax.ShapeDtypeStruct(full_shape, BF16), name=name,
                          compiler_params=_cp(("arbitrary",)))(idx, shard)


_SEM = pl.BlockSpec(memory_space=pltpu.SEMAPHORE)
_EFFECT = pltpu.SideEffectType.DATAFLOW_SIDE_EFFECTING


def _after(v, *deps):
    return lax.optimization_barrier((v,) + tuple(deps))[0]


def _split_call(name, arrays, start=None, wait=None, wait_sems=None, after=None):
    keys = list(arrays)
    n = len(keys)
    n_start = start.n if start is not None else 0

    def body(*refs):
        pos = n
        if wait is not None:
            wss, wrs = refs[pos], refs[pos + 1]
            pos += 2
        if after is not None:
            pos += 1
        if start is not None:
            nss, nrs = refs[pos], refs[pos + 1]
            pos += 2
        R = dict(zip(keys, refs[pos:pos + n]))
        token = refs[pos + n]
        x, y, c = _coords()

        def desc(src, dst, dev, ss, rs, k):
            return pltpu.make_async_remote_copy(src_ref=src, dst_ref=dst, send_sem=ss.at[k], recv_sem=rs.at[k],
                                                device_id=dev, device_id_type=MESH)

        def run(sk):
            if wait is not None:
                for k, (snd, land) in enumerate(wait.copies(sk, R)):
                    if snd is not None:
                        desc(snd[0], snd[1], snd[2], wss, wrs, k).wait_send()
                    if land is not None:
                        desc(land, land, (x, y, c), wss, wrs, k).wait_recv()
            if start is not None:
                for k, (snd, land) in enumerate(start.copies(sk, R)):
                    if snd is not None:
                        desc(snd[0], snd[1], snd[2], nss, nrs, k).start()

        _per_chip(run)
        token[...] = jnp.zeros_like(token)

    hbm = pl.BlockSpec(memory_space=HBM)
    vals = [arrays[k] for k in keys]
    ins, in_specs = list(vals), [hbm] * n
    if wait is not None:
        ins += list(wait_sems)
        in_specs += [_SEM, _SEM]
    if after is not None:
        ins.append(after)
        in_specs.append(pl.BlockSpec(memory_space=pl.ANY))
    out_shape, out_specs = [], []
    if start is not None:
        out_shape += [pltpu.SemaphoreType.DMA((n_start,)), pltpu.SemaphoreType.DMA((n_start,))]
        out_specs += [_SEM, _SEM]
    first = len(out_shape)
    out_shape += [jax.ShapeDtypeStruct(v.shape, v.dtype) for v in vals] + [jax.ShapeDtypeStruct((8, 128), F32)]
    out_specs += [hbm] * n + [pl.BlockSpec(memory_space=pltpu.VMEM)]
    res = pl.pallas_call(
        body, out_shape=tuple(out_shape), in_specs=in_specs, out_specs=tuple(out_specs),
        input_output_aliases={i: first + i for i in range(n)}, name=name,
        compiler_params=pltpu.CompilerParams(has_side_effects=_EFFECT),
    )(*ins)
    sems = (res[0], res[1]) if start is not None else None
    return dict(zip(keys, res[first:first + n])), sems, res[-1]


class _Plan:
    def __init__(self, n, copies):
        self.n, self.copies = n, copies


_HM, _HX = PMAIN // 2, XTRA // 2
_WIN = {
    "wct": (True, lambda r, sc, hc: r.at[_rows(PMAIN * sc + _HM * hc, _HM), :]),
    "xt": (True, lambda r, sc, hc: r.at[sc, _rows(_HX * hc, _HX), :]),
    "w1": (True, lambda r, sc, hc: r.at[_rows(512 * hc, 512), pl.ds(1024 * sc, 1024)]),
    "w2": (True, lambda r, sc, hc: r.at[_rows(1024 * sc + 512 * hc, 512), :]),
    "wa": (True, lambda r, sc, hc: r.at[_rows(256 * sc + 128 * hc, 128), :]),
    "wb": (True, lambda r, sc, hc: r.at[_rows(512 * sc + 256 * hc, 256), :]),
    "wo": (True, lambda r, sc, hc: r.at[_rows(256 * sc + 128 * hc, 128), :]),
    "cw": (False, lambda r, sc, hc: r.at[sc]),
}


def _ag_chips_plan(keys):
    def copies(sk, R):
        _, _, c = _coords()
        out = []
        for key in keys:
            win = _WIN[key][1]
            for (px, py), ps in _other_chips(sk):
                w = win(R[key], sk, c)
                out.append(((w, w, (px, py, c)), win(R[key], ps, c)))
        return out
    return _Plan(3 * len(keys), copies)


def _ag_sibling_plan(keys):
    keys = [k for k in keys if _WIN[k][0]]

    def copies(sk, R):
        x, y, c = _coords()
        out = []
        for key in keys:
            win = _WIN[key][1]
            for _, ps in _other_chips(sk):
                w = win(R[key], ps, c)
                out.append(((w, w, (x, y, 1 - c)), win(R[key], ps, 1 - c)))
        return out
    return _Plan(3 * len(keys), copies)


def _fix_wct(wct, xt):
    nb = PMAIN // XTRA

    def kern(w_ref, x_ref, o_ref):
        k = pl.program_id(0)
        xv = x_ref[0]
        o_ref[...] = jnp.where(k < 3, (w_ref[...].astype(F32) + xv.astype(F32)).astype(BF16), xv)

    blk = pl.BlockSpec((XTRA, D), lambda k: (nb * (k + 1), 0))
    rblk = pl.BlockSpec((XTRA, D), lambda k: (jnp.where(k < 3, nb * (k + 1), 0), 0))
    return pl.pallas_call(
        kern, grid=(4,), in_specs=[rblk, pl.BlockSpec((1, XTRA, D), lambda k: (k, 0, 0))], out_specs=blk,
        out_shape=jax.ShapeDtypeStruct(wct.shape, BF16), input_output_aliases={0: 0}, name="fix_wct",
        compiler_params=_cp(("arbitrary",)),
    )(wct, xt)


_HP = PIECE // 2
_GWIN = [
    lambda r, sc, hc: r.at[_rows(PMAIN * sc + _HP * hc, _HP), :],
    lambda r, sc, hc: r.at[_rows(512 * hc, 512), pl.ds(1024 * sc, 1024)],
    lambda r, sc, hc: r.at[_rows(1024 * sc + 512 * hc, 512), :],
    lambda r, sc, hc: r.at[_rows(256 * sc + 128 * hc, 128), :],
    lambda r, sc, hc: r.at[_rows(512 * sc + 256 * hc, 256), :],
    lambda r, sc, hc: r.at[_rows(256 * sc + 128 * hc, 128), :],
]
HALF_SHAPES = [(PIECE // 2, D), (512, 1024), (512, 1024), (128, 1024), (256, 1024), (128, 1024)]


def _rs_sibling_plan(ts):
    def copies(sk, R):
        x, y, c = _coords()
        out = []
        for t in ts:
            for sc in range(4):
                land = R["ra%d" % t].at[sc]
                out.append(((_GWIN[t](R["g%d" % t], sc, 1 - c), land, (x, y, 1 - c)), land))
        return out
    return _Plan(4 * len(ts), copies)


def _rs_chips_plan(ts):
    def copies(sk, R):
        _, _, c = _coords()
        out = []
        for t in ts:
            for j, ((px, py), ps) in enumerate(_other_chips(sk)):
                land = R["rb%d" % t].at[j]
                out.append(((R["hb%d" % t].at[ps], land, (px, py, c)), land))
        return out
    return _Plan(3 * len(ts), copies)


def _rs_share_plan(ts):
    def copies(sk, R):
        x, y, c = _coords()
        out = []
        for t in ts:
            rows = HALF_SHAPES[t][0]
            mine = R["f%d" % t].at[_rows(rows * c, rows), :]
            out.append(((mine, mine, (x, y, 1 - c)), R["f%d" % t].at[_rows(rows * (1 - c), rows), :]))
        return out
    return _Plan(len(ts), copies)


def _half_tiling(t):
    rows, cols = HALF_SHAPES[t]
    if t == 0:
        return (256, cols), rows // 256, lambda i: (i, 0)
    if t == 1:
        return (rows, 256), cols // 256, lambda i: (0, i)
    return (rows, cols), 1, lambda i: (0, 0)


def _window_block(t, sc, hc, i):
    if t == 0:
        return (PMAIN // 256) * sc + (PIECE // 512) * hc + i, 0
    if t == 1:
        return hc, 4 * sc + i
    return 2 * sc + hc, 0


def _chip_sum(g, ra, t, idx, name):
    rows, cols = HALF_SHAPES[t]
    blk, nblk, inner = _half_tiling(t)

    def kern(idx_ref, g_ref, r_ref, hb_ref, hf_ref):
        v = g_ref[...].astype(F32) + r_ref[0].astype(F32)
        hb_ref[0] = v.astype(BF16)
        hf_ref[0] = v

    gmap = lambda sc, i, idx_ref: _window_block(t, sc, idx_ref[1], i)
    omap = lambda sc, i, idx_ref: (sc,) + inner(i)
    grid_spec = pltpu.PrefetchScalarGridSpec(
        num_scalar_prefetch=1, grid=(4, nblk),
        in_specs=[pl.BlockSpec(blk, gmap), pl.BlockSpec((1,) + blk, omap)],
        out_specs=(pl.BlockSpec((1,) + blk, omap), pl.BlockSpec((1,) + blk, omap)))
    return pl.pallas_call(
        kern, grid_spec=grid_spec,
        out_shape=(jax.ShapeDtypeStruct((4, rows, cols), BF16), jax.ShapeDtypeStruct((4, rows, cols), F32)),
        name=name, compiler_params=_cp(("parallel", "parallel")),
    )(idx, g, ra)


def _final_sum(hf, rb, t, idx, name):
    rows, cols = HALF_SHAPES[t]
    blk, nblk, inner = _half_tiling(t)
    nbr = rows // blk[0]

    def kern(idx_ref, h_ref, r_ref, o_ref):
        o_ref[...] = ((h_ref[0] + r_ref[0].astype(F32)) + r_ref[1].astype(F32)) + r_ref[2].astype(F32)

    def omap(i, idx_ref):
        r, cidx = inner(i)
        return nbr * idx_ref[1] + r, cidx

    grid_spec = pltpu.PrefetchScalarGridSpec(
        num_scalar_prefetch=1, grid=(nblk,),
        in_specs=[pl.BlockSpec((1,) + blk, lambda i, idx_ref: (idx_ref[0],) + inner(i)),
                  pl.BlockSpec((3,) + blk, lambda i, idx_ref: (0,) + inner(i))],
        out_specs=pl.BlockSpec(blk, omap))
    return pl.pallas_call(
        kern, grid_spec=grid_spec, out_shape=jax.ShapeDtypeStruct((2 * rows, cols), F32),
        name=name, compiler_params=_cp(("parallel",)),
    )(idx, hf, rb)


class _ReduceScatter:
    def __init__(self, ts, grads, idx, tag):
        self.ts, self.idx, self.tag = ts, idx, tag
        arr = {}
        for t in ts:
            arr["g%d" % t] = grads[t]
            arr["ra%d" % t] = lax.empty((4,) + HALF_SHAPES[t], BF16)
        self.plan = _rs_sibling_plan(ts)
        self.arr, self.sems, self.token = _split_call("rs_sibling_start_" + tag, arr, start=self.plan)

    def chips(self, after):
        arr, _, _ = _split_call("rs_sibling_wait_" + self.tag, self.arr, wait=self.plan, wait_sems=self.sems, after=after)
        brr, self.hf = {}, {}
        for t in self.ts:
            hb, self.hf[t] = _chip_sum(arr["g%d" % t], arr["ra%d" % t], t, self.idx, "chip_sum_%d" % t)
            brr["hb%d" % t] = hb
            brr["rb%d" % t] = lax.empty((3,) + HALF_SHAPES[t], BF16)
        self.plan = _rs_chips_plan(self.ts)
        self.arr, self.sems, self.token = _split_call("rs_chips_start_" + self.tag, brr, start=self.plan)
        return self.token

    def share(self, after):
        brr, _, _ = _split_call("rs_chips_wait_" + self.tag, self.arr, wait=self.plan, wait_sems=self.sems, after=after)
        frr = {"f%d" % t: _final_sum(self.hf[t], brr["rb%d" % t], t, self.idx, "final_sum_%d" % t) for t in self.ts}
        self.plan = _rs_share_plan(self.ts)
        self.arr, self.sems, self.token = _split_call("rs_share_start_" + self.tag, frr, start=self.plan)
        return self.token

    def result(self, after):
        frr, _, _ = _split_call("rs_share_wait_" + self.tag, self.arr, wait=self.plan, wait_sems=self.sems, after=after)
        return {t: frr["f%d" % t] for t in self.ts}


def _small_all_gather(v):
    def body(v_ref, o_ref, send_sems, recv_sems, loc_sem):
        x, y, c = _coords()
        me = 4 * x + 2 * y + c
        lc = pltpu.make_async_copy(v_ref, o_ref.at[me], loc_sem)
        lc.start()
        cps = []
        for k in range(1, 8):
            fx, fy, fc = (k >> 2) & 1, (k >> 1) & 1, k & 1
            dev = ((1 - x) if fx else x, (1 - y) if fy else y, (1 - c) if fc else c)
            cp = pltpu.make_async_remote_copy(src_ref=v_ref, dst_ref=o_ref.at[me], send_sem=send_sems.at[k - 1],
                                              recv_sem=recv_sems.at[k - 1], device_id=dev, device_id_type=MESH)
            cp.start()
            cps.append((cp, 4 * dev[0] + 2 * dev[1] + dev[2]))
        for k, (cp, frm) in enumerate(cps):
            got = o_ref.at[frm]
            pltpu.make_async_remote_copy(src_ref=got, dst_ref=got, send_sem=send_sems.at[k], recv_sem=recv_sems.at[k],
                                         device_id=(x, y, c), device_id_type=MESH).wait_recv()
        for cp, _ in cps:
            cp.wait_send()
        lc.wait()

    hbm = pl.BlockSpec(memory_space=HBM)
    return pl.pallas_call(
        body, out_shape=jax.ShapeDtypeStruct((8,) + v.shape, F32), in_specs=[hbm], out_specs=hbm,
        scratch_shapes=[pltpu.SemaphoreType.DMA((7,)), pltpu.SemaphoreType.DMA((7,)), pltpu.SemaphoreType.DMA(())],
        name="small_all_gather", compiler_params=pltpu.CompilerParams(has_side_effects=True),
    )(v)


def _sum8(v):
    def kern(v_ref, o_ref):
        acc = v_ref[0]
        for k in range(1, 8):
            acc = acc + v_ref[k]
        o_ref[...] = acc

    return pl.pallas_call(kern, out_shape=jax.ShapeDtypeStruct(v.shape[1:], F32), name="small_sum")(v)


def _adamw(w, g, m, v, name, tr=128, blk0=0, nblk=None, into=None, copy_g=False):
    R, C = w.shape
    tr = min(tr, R)
    if nblk is None:
        assert R % tr == 0 and blk0 == 0
        nblk = R // tr
    n_out = 4 if copy_g else 3

    def kern(*refs):
        w_ref, g_ref, m_ref, v_ref = refs[:4]
        d_ref, mo_ref, vo_ref = refs[-n_out:][:3]
        gv = g_ref[...]
        mn = ADAM_B1 * m_ref[...] + (1.0 - ADAM_B1) * gv
        vn = ADAM_B2 * v_ref[...] + (1.0 - ADAM_B2) * (gv * gv)
        m_hat = mn / (1.0 - ADAM_B1 ** ADAM_STEP)
        v_hat = vn / (1.0 - ADAM_B2 ** ADAM_STEP)
        d_ref[...] = -ADAM_LR * (m_hat / (jnp.sqrt(v_hat) + ADAM_EPS) + ADAM_WD * w_ref[...])
        mo_ref[...] = mn
        vo_ref[...] = vn
        if copy_g:
            refs[-1][...] = gv

    blk = pl.BlockSpec((tr, C), lambda i: (blk0 + i, 0))
    sd = jax.ShapeDtypeStruct((R, C), F32)
    in_specs, args, aliases = [blk] * 4, [w, g, m, v], {}
    if into is not None:
        in_specs += [pl.BlockSpec(memory_space=pl.ANY)] * 3
        args += list(into)
        aliases = {4: 0, 5: 1, 6: 2}
    return pl.pallas_call(kern, grid=(nblk,), in_specs=in_specs, out_specs=(blk,) * n_out, out_shape=(sd,) * n_out,
                          input_output_aliases=aliases, name=name, compiler_params=_cp(("parallel",)))(*args)


def _to_piece(wt, s):
    gen = lax.dynamic_update_slice(jnp.zeros((PIECE, D), wt.dtype), wt, (8 * s, 0))
    z = lambda n: jnp.zeros((n, D), wt.dtype)
    last = jnp.concatenate([z(24), wt[:744], wt[776:], wt[744:776], z(PIECE - 24 - W_SHARD)], axis=0)
    return jnp.where(s == 3, last, gen)


def _from_piece(p, s):
    gen = lax.dynamic_slice(p, (8 * s, 0), (W_SHARD, D))
    last = jnp.concatenate([p[24:768], p[2816:2848], p[768:2816]], axis=0)
    return jnp.where(s == 3, last, gen)


_SMALL = [("norm_mix", 1024), ("b_gate", 2048), ("ssm_conv_b", 4096), ("dt_bias", 32), ("A_log", 32), ("D_skip", 32),
          ("ssm_norm_w", 2048), ("norm_mlp", 1024), ("norm_final", 1024), ("sc_conv_w", 3072), ("ssm_conv_w", 16384),
          ("loss", 1)]


def _pack(vals, table, rows):
    parts = []
    for name, n in table:
        v = vals[name].reshape(-1).astype(F32)
        pad = (-n) % 128
        parts.append(jnp.pad(v, (0, pad)) if pad else v)
    flat = jnp.concatenate(parts)
    return jnp.pad(flat, (0, rows * 128 - flat.shape[0])).reshape(rows, 128)


def _unpack(arr, table):
    flat = arr.reshape(-1)
    out, off = {}, 0
    for name, n in table:
        out[name] = flat[off:off + n]
        off += n + ((-n) % 128)
    return out


def kernel(x, norm_mix, w_in, b_gate, sc_conv_w, ssm_conv_w, ssm_conv_b, dt_bias, A_log, D_skip, ssm_norm_w, w_branch_sc, w_branch_ssm, w_out, norm_mlp, w_mlp1, w_mlp2, norm_final, loss_target, m_norm_mix, m_w_in, m_b_gate, m_sc_conv_w, m_ssm_conv_w, m_ssm_conv_b, m_dt_bias, m_A_log, m_D_skip, m_ssm_norm_w, m_w_branch_sc, m_w_branch_ssm, m_w_out, m_norm_mlp, m_w_mlp1, m_w_mlp2, m_norm_final, v_norm_mix, v_w_in, v_b_gate, v_sc_conv_w, v_ssm_conv_w, v_ssm_conv_b, v_dt_bias, v_A_log, v_D_skip, v_ssm_norm_w, v_w_branch_sc, v_w_branch_ssm, v_w_out, v_norm_mlp, v_w_mlp1, v_w_mlp2, v_norm_final):
    L = x.shape[1]
    nc = L // Q
    xi, yi, ci = lax.axis_index("x"), lax.axis_index("y"), lax.axis_index("c")
    s = 2 * xi + yi
    idx = jnp.stack([s, ci]).astype(jnp.int32)
    x0 = x.reshape(L, D)
    tgt = loss_target.reshape(L, D)

    piece = _to_piece(w_in.T, s)
    nb = PMAIN // XTRA
    wct0 = _place(piece, (NCW, D), (XTRA, D), lambda i, r: (nb * r[0] + i, 0), idx, "place_wct", nblk=nb)
    xt0 = _place(piece, (4, XTRA, D), (1, XTRA, D), lambda i, r: (r[0], 0, 0), idx, "place_xt", blk0=nb, nblk=1)
    cws = jnp.zeros((8, 1280), F32)
    cws = cws.at[0:3, 0:256].set(sc_conv_w).at[0:4, 256:1280].set(ssm_conv_w)
    cw0 = lax.dynamic_update_slice(jnp.zeros((4, 8, 1280), F32), cws[None], (s, 0, 0))
    win_keys, mid_keys, end_keys = ["wct", "xt", "cw"], ["wa", "wb", "wo", "w1"], ["w2"]
    gw, sems_w, tok = _split_call("ag_win_start", {"wct": wct0, "xt": xt0, "cw": cw0}, start=_ag_chips_plan(win_keys))
    wa0 = _place(_after(w_branch_sc, tok), (D, D), (256, 1024), lambda i, r: (r[0], 0), idx, "place_wa")
    wb0 = _place(_after(w_branch_ssm, tok), (INNER, D), (512, 1024), lambda i, r: (r[0], 0), idx, "place_wb")
    wo0 = _place(_after(w_out, tok), (D, D), (256, 1024), lambda i, r: (r[0], 0), idx, "place_wo")
    w10 = _place(_after(w_mlp1, tok), (D, DFF), (256, 1024), lambda i, r: (i, r[0]), idx, "place_w1")
    w20 = _place(_after(w_mlp2, tok), (DFF, D), (256, 1024), lambda i, r: (4 * r[0] + i, 0), idx, "place_w2")
    gm, sems_m, tok = _split_call("ag_mid_start", {"wa": wa0, "wb": wb0, "wo": wo0, "w1": w10},
                                  start=_ag_chips_plan(mid_keys))
    ge, sems_e, tok = _split_call("ag_end_start", {"w2": _after(w20, tok)}, start=_ag_chips_plan(end_keys))
    h = _rms_fwd(_after(x0, tok), norm_mix, "rms_mix")
    gw, sems_w, tok = _split_call("ag_win_pass", gw, wait=_ag_chips_plan(win_keys), wait_sems=sems_w,
                                  start=_ag_sibling_plan(win_keys), after=h)
    gw, _, _ = _split_call("ag_win_done", gw, wait=_ag_sibling_plan(win_keys), wait_sems=sems_w, after=tok)
    wc, cw_all = _fix_wct(gw["wct"], gw["xt"]), gw["cw"]
    sc_w_full = jnp.concatenate([cw_all[k, :, 0:256] for k in range(4)], axis=1)
    ssm_w_full = jnp.concatenate([cw_all[k, :, 256:1280] for k in range(4)], axis=1)
    cw4 = ssm_w_full.at[4].set(ssm_conv_b)
    vec = jnp.zeros((8, 128), F32).at[0, :NH].set(dt_bias).at[1, :NH].set(A_log)
    vecg = jnp.zeros((NG, 8, 128), F32).at[:, 0, :4].set(A_log.reshape(NG, 4)).at[:, 1, :4].set(D_skip.reshape(NG, 4))

    proj = _matmul(h, wc, "nt", F32, 512, 1280, 1024, "in_proj", n_outer=True)
    gm, sems_m, tok = _split_call("ag_mid_pass", gm, wait=_ag_chips_plan(mid_keys), wait_sems=sems_m,
                                  start=_ag_sibling_plan(mid_keys), after=proj)
    proj = _after(proj, tok)
    ya = _sc_fwd(proj, sc_w_full)
    xbc = _ssm_conv_fwd(proj, cw4)
    dt, cs, sg = _dt_prep(proj, vec)
    per_group = lambda a: jnp.pad(a[:, :NH].reshape(L, NG, 4).transpose(1, 0, 2), ((0, 0), (0, 0), (0, 124)))
    dt4, cs4, sg4 = per_group(dt), per_group(cs), per_group(sg)
    ge, sems_e, tok = _split_call("ag_end_pass", ge, wait=_ag_chips_plan(end_keys), wait_sems=sems_e,
                                  start=_ag_sibling_plan(end_keys), after=xbc)
    y, s_all = _ssd_fwd(_after(xbc, tok), dt4, cs4, vecg)
    yb = _gnorm_fwd(y, proj, ssm_norm_w)
    gm, _, _ = _split_call("ag_mid_done", gm, wait=_ag_sibling_plan(mid_keys), wait_sems=sems_m, after=yb)
    ge, _, _ = _split_call("ag_end_done", ge, wait=_ag_sibling_plan(end_keys), wait_sems=sems_e, after=yb)
    wa, wb, wo, w1, w2 = gm["wa"], gm["wb"], gm["wo"], gm["w1"], ge["w2"]
    br_a = _matmul(ya, wa, "nn", F32, 512, 1024, 1024, "branch_sc")
    br_b = _matmul(yb, wb, "nn", F32, 512, 1024, 2048, "branch_ssm")
    merged = _merge_fwd(proj, b_gate, br_a, br_b)
    x1 = _matmul(merged, wo, "nn", F32, 512, 1024, 1024, "out_proj", epi="res", extra=x0)
    h2 = _rms_fwd(x1, norm_mlp, "rms_mlp")
    a1, rl = _matmul(h2, w1, "nn", F32, 512, 1024, 1024, "mlp1", epi="relu2", n_outer=True)
    x2 = _matmul(rl, w2, "nn", F32, 512, 1024, 2048, "mlp2", epi="res", extra=x1)
    dx2, g_nf, loss8 = _final(x2, norm_final, tgt)

    da = _matmul(dx2, w2, "nt", BF16, 512, 1024, 1024, "mlp2_dx", epi="drelu", extra=a1, n_outer=True)
    g_w2 = _matmul(rl, dx2, "tn", BF16, 1024, 1024, 512, "mlp2_dw")
    g_w1 = _matmul(h2, da, "tn", BF16, 1024, 1024, 512, "mlp1_dw")
    dh2 = _matmul(da, w1, "nt", F32, 512, 1024, 2048, "mlp1_dx")
    dx1, g_nmlp = _rms_bwd(dh2, x1, norm_mlp, dx2, "rms_mlp_bwd")
    dmerged = _matmul(dx1, wo, "nt", F32, 512, 1024, 1024, "out_proj_dx")
    g_wo = _matmul(merged, dx1, "tn", BF16, 1024, 1024, 512, "out_proj_dw")
    dproj = lax.empty((L, NCW), BF16)
    dbr, dproj, g_bg = _merge_bwd(dmerged, proj, b_gate, br_a, br_b, dproj)
    dya = _matmul(dbr[0], wa, "nt", F32, 512, 1024, 1024, "branch_sc_dx")
    g_wa = _matmul(ya, dbr[0], "tn", BF16, 1024, 1024, 512, "branch_sc_dw")
    dproj, g_scw = _sc_bwd(dya, proj, sc_w_full, dproj)
    dyb = _matmul(dbr[1], wb, "nt", F32, 512, 1024, 1024, "branch_ssm_dx", n_outer=True)
    g_wb = _matmul(yb, dbr[1], "tn", BF16, 1024, 1024, 512, "branch_ssm_dw")
    rs_a = _ReduceScatter([1, 2, 3, 4, 5], {1: g_w1, 2: g_w2, 3: g_wa, 4: g_wb, 5: g_wo}, idx, "a")
    dy, dproj, g_snw = _gnorm_bwd(_after(dyb, rs_a.token), y, proj, ssm_norm_w, dproj)
    tok = rs_a.chips(after=dy)
    dxs, dbm, dcm, ddt_g, st = _ssd_bwd(xbc, dt4, cs4, sg4, vecg, s_all, _after(dy, tok))
    dproj, gx1 = _ssm_conv_bwd(dxs, proj, cw4, dproj, 0, "ssm_conv_bwd_x")
    dproj, gx2 = _ssm_conv_bwd(dbm, proj, cw4, dproj, INNER, "ssm_conv_bwd_b")
    dproj, gx3 = _ssm_conv_bwd(dcm, proj, cw4, dproj, INNER + NG * NS, "ssm_conv_bwd_c")
    g_cw4 = jnp.concatenate([gx1, gx2, gx3], axis=1)
    ddt = jnp.pad(ddt_g[:, :, :4].transpose(1, 0, 2).reshape(L, NH), ((0, 0), (0, 128 - NH)))
    dproj, g_dtb = _dt_bwd(ddt, dproj)
    g_wc = _matmul(dproj, h, "tn", BF16, 1280, 1024, 512, "in_proj_dw")
    rs_b = _ReduceScatter([0], {0: g_wc}, idx, "b")
    tok = rs_a.share(after=rs_b.token)
    dh = _matmul(_after(dproj, tok), wc, "nn", F32, 512, 1024, 2304, "in_proj_dx")
    tok = rs_b.chips(after=dh)
    grad_x, g_nm = _rms_bwd(_after(dh, tok), x0, norm_mix, dx1, "rms_mix_bwd")

    small = {"norm_mix": g_nm[0], "b_gate": g_bg[0], "ssm_conv_b": g_cw4[4], "dt_bias": g_dtb[0, :NH],
             "A_log": st[:, 0, :4], "D_skip": st[:, 1, :4], "ssm_norm_w": g_snw[0], "norm_mlp": g_nmlp[0],
             "norm_final": g_nf[0], "sc_conv_w": g_scw[0:3], "ssm_conv_w": g_cw4[0:4], "loss": loss8[0, 0:1]}
    small_sum = _sum8(_small_all_gather(_pack(small, _SMALL, SMALL_ROWS)))
    gs = _unpack(small_sum, _SMALL)
    red = rs_a.result(after=small_sum)
    big = {"w_mlp1": red[1], "w_mlp2": red[2], "w_branch_sc": red[3], "w_branch_ssm": red[4], "w_out": red[5]}

    given = dict(norm_mix=norm_mix, w_in=w_in, b_gate=b_gate, sc_conv_w=sc_conv_w, ssm_conv_w=ssm_conv_w, ssm_conv_b=ssm_conv_b, dt_bias=dt_bias, A_log=A_log, D_skip=D_skip, ssm_norm_w=ssm_norm_w, w_branch_sc=w_branch_sc, w_branch_ssm=w_branch_ssm, w_out=w_out, norm_mlp=norm_mlp, w_mlp1=w_mlp1, w_mlp2=w_mlp2, norm_final=norm_final,
                 m_norm_mix=m_norm_mix, m_w_in=m_w_in, m_b_gate=m_b_gate, m_sc_conv_w=m_sc_conv_w, m_ssm_conv_w=m_ssm_conv_w, m_ssm_conv_b=m_ssm_conv_b, m_dt_bias=m_dt_bias, m_A_log=m_A_log, m_D_skip=m_D_skip, m_ssm_norm_w=m_ssm_norm_w, m_w_branch_sc=m_w_branch_sc, m_w_branch_ssm=m_w_branch_ssm, m_w_out=m_w_out, m_norm_mlp=m_norm_mlp, m_w_mlp1=m_w_mlp1, m_w_mlp2=m_w_mlp2, m_norm_final=m_norm_final,
                 v_norm_mix=v_norm_mix, v_w_in=v_w_in, v_b_gate=v_b_gate, v_sc_conv_w=v_sc_conv_w, v_ssm_conv_w=v_ssm_conv_w, v_ssm_conv_b=v_ssm_conv_b, v_dt_bias=v_dt_bias, v_A_log=v_A_log, v_D_skip=v_D_skip, v_ssm_norm_w=v_ssm_norm_w, v_w_branch_sc=v_w_branch_sc, v_w_branch_ssm=v_w_branch_ssm, v_w_out=v_w_out, v_norm_mlp=v_norm_mlp, v_w_mlp1=v_w_mlp1, v_w_mlp2=v_w_mlp2, v_norm_final=v_norm_final)
    order = ["norm_mix", "w_in", "b_gate", "sc_conv_w", "ssm_conv_w", "ssm_conv_b", "dt_bias", "A_log", "D_skip",
             "ssm_norm_w", "w_branch_sc", "w_branch_ssm", "w_out", "norm_mlp", "w_mlp1", "w_mlp2", "norm_final"]
    grad, delta, new_m, new_v = {}, {}, {}, {}
    for n in big:
        delta[n], new_m[n], new_v[n], grad[n] = _adamw(given[n], big[n], given["m_" + n], given["v_" + n],
                                                       "adamw_" + n, copy_g=True)
    tok = rs_b.share(after=new_v["w_mlp2"])
    gp = rs_b.result(after=tok)[0]
    gwt = _from_piece(gp, s)
    wt_args = (w_in.T, gwt, m_w_in.T, v_w_in.T)
    head = _adamw(*wt_args, "adamw_w_in", tr=256, nblk=W_SHARD // 256)
    dt_, mt_, vt_ = _adamw(*wt_args, "adamw_w_in_tail", tr=8, blk0=(W_SHARD // 256) * 32, nblk=1, into=head)
    grad["w_in"], delta["w_in"], new_m["w_in"], new_v["w_in"] = gwt.T, dt_.T, mt_.T, vt_.T
    big["w_in"] = None
    grad_small = {n: gs[n].reshape(given[n].shape) for n in order if n not in big and n not in ("sc_conv_w", "ssm_conv_w")}
    grad_small["sc_conv_w"] = lax.dynamic_slice(gs["sc_conv_w"].reshape(3, D), (0, 256 * s), (3, 256))
    grad_small["ssm_conv_w"] = lax.dynamic_slice(gs["ssm_conv_w"].reshape(4, XBC), (0, 1024 * s), (4, 1024))
    table = [(n, int(grad_small[n].size)) for n in grad_small]
    rows = 136
    pk = lambda d: _pack(d, table, rows)
    ds_, ms_, vs_ = _adamw(pk({n: given[n] for n in grad_small}), pk(grad_small), pk({n: given["m_" + n] for n in grad_small}),
                           pk({n: given["v_" + n] for n in grad_small}), "adamw_small", tr=rows)
    ds_, ms_, vs_ = _unpack(ds_, table), _unpack(ms_, table), _unpack(vs_, table)
    for n in grad_small:
        shp = given[n].shape
        grad[n] = grad_small[n]
        delta[n], new_m[n], new_v[n] = ds_[n].reshape(shp), ms_[n].reshape(shp), vs_[n].reshape(shp)

    loss = gs["loss"].reshape(())
    return (loss, grad_x.reshape(1, L, D), *[grad[n] for n in order], *[delta[n] for n in order],
            *[new_m[n] for n in order], *[new_v[n] for n in order])
```

```python
import functools

import jax
import jax.numpy as jnp
from jax import lax
from jax.experimental import pallas as pl
from jax.experimental.pallas import tpu as pltpu

F32 = jnp.float32
BF16 = jnp.bfloat16
MESH = pl.DeviceIdType.MESH
HBM = pltpu.HBM

D = 1024
INNER = 2048
HD = 64
NH = 32
NG = 8
NS = 128
Q = 128
XBC = 4096
DFF = 4096
EPS = 1e-6
W_SHARD = 2824
NCW = 11520
PIECE = 3072
PMAIN = 2816
C_Z, C_XBC, C_GATE, C_DT = 3072, 5120, 9216, 11264
SMALL_ROWS = 256
VMEM_LIMIT = 56 * 1024 * 1024

ADAM_LR, ADAM_B1, ADAM_B2, ADAM_EPS, ADAM_WD, ADAM_STEP = 0.001, 0.9, 0.999, 1e-08, 0.01, 10


def _cp(sem=None, vmem=VMEM_LIMIT):
    return pltpu.CompilerParams(dimension_semantics=sem, vmem_limit_bytes=vmem)


def _sigmoid(v):
    return 1.0 / (1.0 + jnp.exp(-v))


_DIMS = {"nn": (((1,), (0,)), ((), ())), "nt": (((1,), (1,)), ((), ())), "tn": (((0,), (0,)), ((), ()))}


def _matmul(a, b, mode, out_dtype, tm, tn, tk, name, epi=None, extra=None, n_outer=False, dep=None):
    if mode == "tn":
        K, M = a.shape
    else:
        M, K = a.shape
    N = b.shape[0] if mode == "nt" else b.shape[1]
    tm, tn, tk = min(tm, M), min(tn, N), min(tk, K)
    assert M % tm == 0 and N % tn == 0 and K % tk == 0, (name, M, N, K, tm, tn, tk)
    nm, nn, nk = M // tm, N // tn, K // tk
    dims = _DIMS[mode]

    def ij(p0, p1):
        return (p1, p0) if n_outer else (p0, p1)

    if mode == "tn":
        a_spec = pl.BlockSpec((tk, tm), lambda p0, p1, k: (k, ij(p0, p1)[0]))
    else:
        a_spec = pl.BlockSpec((tm, tk), lambda p0, p1, k: (ij(p0, p1)[0], k))
    if mode == "nt":
        b_spec = pl.BlockSpec((tn, tk), lambda p0, p1, k: (ij(p0, p1)[1], k))
    else:
        b_spec = pl.BlockSpec((tk, tn), lambda p0, p1, k: (k, ij(p0, p1)[1]))
    o_spec = pl.BlockSpec((tm, tn), lambda p0, p1, k: ij(p0, p1))
    in_specs = [a_spec, b_spec]
    args = [a, b]
    if epi in ("res", "drelu"):
        in_specs.append(o_spec)
        args.append(extra)
    if dep is not None:
        in_specs.append(pl.BlockSpec(memory_space=pl.ANY))
        args.append(dep)
    n_in = len(args)
    if epi == "relu2":
        out_shape = (jax.ShapeDtypeStruct((M, N), F32), jax.ShapeDtypeStruct((M, N), BF16))
        out_specs = (o_spec, o_spec)
    else:
        out_shape = jax.ShapeDtypeStruct((M, N), out_dtype)
        out_specs = o_spec

    def kern(*refs):
        a_ref, b_ref = refs[0], refs[1]
        e_ref = refs[2] if epi in ("res", "drelu") else None
        acc = refs[-1]
        outs = refs[n_in:-1]
        k = pl.program_id(2)

        @pl.when(k == 0)
        def _():
            acc[...] = jnp.zeros_like(acc)

        acc[...] += lax.dot_general(a_ref[...].astype(BF16), b_ref[...].astype(BF16), dims,
                                    preferred_element_type=F32)

        @pl.when(k == nk - 1)
        def _():
            r = acc[...]
            if epi is None:
                outs[0][...] = r.astype(out_dtype)
            elif epi == "res":
                outs[0][...] = (r + e_ref[...]).astype(out_dtype)
            elif epi == "relu2":
                outs[0][...] = r
                t = jnp.maximum(r, 0.0)
                outs[1][...] = (t * t).astype(BF16)
            else:
                outs[0][...] = (r * (2.0 * jnp.maximum(e_ref[...], 0.0))).astype(out_dtype)

    grid = (nn, nm, nk) if n_outer else (nm, nn, nk)
    return pl.pallas_call(
        kern, grid=grid, in_specs=in_specs, out_specs=out_specs, out_shape=out_shape,
        scratch_shapes=[pltpu.VMEM((tm, tn), F32)], name=name,
        compiler_params=_cp(("parallel", "parallel", "arbitrary")),
    )(*args)


def _rms_fwd(x, w, name, tl=256, dep=None):
    L = x.shape[0]

    def kern(x_ref, w_ref, *rest):
        o_ref = rest[-1]
        xv = x_ref[...]
        r = lax.rsqrt(jnp.mean(xv * xv, axis=-1, keepdims=True) + EPS)
        o_ref[...] = ((xv * r) * w_ref[...]).astype(BF16)

    row = pl.BlockSpec((tl, D), lambda i: (i, 0))
    deps = [] if dep is None else [dep]
    return pl.pallas_call(
        kern, grid=(L // tl,),
        in_specs=[row, pl.BlockSpec((1, D), lambda i: (0, 0))] + [pl.BlockSpec(memory_space=pl.ANY)] * len(deps),
        out_specs=row, out_shape=jax.ShapeDtypeStruct((L, D), BF16), name=name, compiler_params=_cp(("parallel",)),
    )(x, w.reshape(1, D), *deps)


def _rms_bwd(dy, x, w, res, name, tl=256, dep=None):
    L = x.shape[0]
    deps = [] if dep is None else [dep]

    def kern(dy_ref, x_ref, w_ref, res_ref, *rest):
        dx_ref, gw_ref = rest[-2:]
        @pl.when(pl.program_id(0) == 0)
        def _():
            gw_ref[...] = jnp.zeros_like(gw_ref)

        xv = x_ref[...]
        dyv = dy_ref[...]
        r = lax.rsqrt(jnp.mean(xv * xv, axis=-1, keepdims=True) + EPS)
        xn = xv * r
        gw_ref[...] += jnp.broadcast_to(jnp.sum(dyv * xn, axis=0, keepdims=True), (8, D))
        dxn = dyv * w_ref[...]
        dx_ref[...] = res_ref[...] + r * (dxn - xn * jnp.mean(dxn * xn, axis=-1, keepdims=True))

    row = pl.BlockSpec((tl, D), lambda i: (i, 0))
    return pl.pallas_call(
        kern, grid=(L // tl,),
        in_specs=[row, row, pl.BlockSpec((1, D), lambda i: (0, 0)), row] + [pl.BlockSpec(memory_space=pl.ANY)] * len(deps),
        out_specs=(row, pl.BlockSpec((8, D), lambda i: (0, 0))),
        out_shape=(jax.ShapeDtypeStruct((L, D), F32), jax.ShapeDtypeStruct((8, D), F32)),
        name=name, compiler_params=_cp(("arbitrary",)),
    )(dy, x, w.reshape(1, D), res, *deps)


def _final(x2, w, tgt, tl=256):
    L = x2.shape[0]

    def kern(x_ref, w_ref, t_ref, dx_ref, gw_ref, loss_ref):
        @pl.when(pl.program_id(0) == 0)
        def _():
            gw_ref[...] = jnp.zeros_like(gw_ref)
            loss_ref[...] = jnp.zeros_like(loss_ref)

        xv = x_ref[...]
        r = lax.rsqrt(jnp.mean(xv * xv, axis=-1, keepdims=True) + EPS)
        xn = xv * r
        e = xn * w_ref[...] - t_ref[...]
        per_tok = jnp.mean(e * e, axis=-1, keepdims=True)
        loss_ref[...] += 0.5 * jnp.sum(per_tok)
        dyv = e * (1.0 / D)
        gw_ref[...] += jnp.broadcast_to(jnp.sum(dyv * xn, axis=0, keepdims=True), (8, D))
        dxn = dyv * w_ref[...]
        dx_ref[...] = r * (dxn - xn * jnp.mean(dxn * xn, axis=-1, keepdims=True))

    row = pl.BlockSpec((tl, D), lambda i: (i, 0))
    return pl.pallas_call(
        kern, grid=(L // tl,), in_specs=[row, pl.BlockSpec((1, D), lambda i: (0, 0)), row],
        out_specs=(row, pl.BlockSpec((8, D), lambda i: (0, 0)), pl.BlockSpec((8, 128), lambda i: (0, 0))),
        out_shape=(jax.ShapeDtypeStruct((L, D), F32), jax.ShapeDtypeStruct((8, D), F32),
                   jax.ShapeDtypeStruct((8, 128), F32)),
        name="final_norm_loss", compiler_params=_cp(("arbitrary",)),
    )(x2, w.reshape(1, D), tgt)


def _down(v, k):
    if k == 0:
        return v
    t = lax.broadcasted_iota(jnp.int32, v.shape, 0)
    return jnp.where(t >= k, pltpu.roll(v, k, axis=0), 0.0)


def _up(v, k):
    if k == 0:
        return v
    n = v.shape[0]
    t = lax.broadcasted_iota(jnp.int32, v.shape, 0)
    return jnp.where(t < n - k, pltpu.roll(v, n - k, axis=0), 0.0)


TW = 256


def _sc_fwd(proj, cw):
    L = proj.shape[0]
    nb = D // TW

    def kern(b_ref, c_ref, x_ref, w_ref, o_ref):
        u = c_ref[...] * x_ref[...]
        w = w_ref[...]
        cv = w[0:1] * _down(u, 2) + w[1:2] * _down(u, 1) + w[2:3] * u
        o_ref[...] = (b_ref[...] * cv).astype(BF16)

    col = lambda off: pl.BlockSpec((L, TW), lambda j: (0, off + j))
    return pl.pallas_call(
        kern, grid=(nb,), in_specs=[col(0), col(nb), col(2 * nb), pl.BlockSpec((8, TW), lambda j: (0, j))],
        out_specs=pl.BlockSpec((L, TW), lambda j: (0, j)), out_shape=jax.ShapeDtypeStruct((L, D), BF16),
        name="sc_fwd", compiler_params=_cp(("parallel",)),
    )(proj, proj, proj, cw)


def _sc_bwd(dya, proj, cw, dproj):
    L = proj.shape[0]
    nb = D // TW

    def kern(d_ref, b_ref, c_ref, x_ref, w_ref, _, dp_ref, gw_ref):
        sec = pl.program_id(1)
        cs, xs, dyv = c_ref[...], x_ref[...], d_ref[...]
        w = w_ref[...]
        u = cs * xs
        u1, u2 = _down(u, 1), _down(u, 2)
        cv = w[0:1] * u2 + w[1:2] * u1 + w[2:3] * u
        dcv = dyv * b_ref[...]
        du = w[2:3] * dcv + w[1:2] * _up(dcv, 1) + w[0:1] * _up(dcv, 2)
        g0 = jnp.sum(dcv * u2, axis=0, keepdims=True)
        g1 = jnp.sum(dcv * u1, axis=0, keepdims=True)
        g2 = jnp.sum(dcv * u, axis=0, keepdims=True)
        row = lax.broadcasted_iota(jnp.int32, (8, TW), 0)
        gw_ref[...] = jnp.where(row == 0, g0, jnp.where(row == 1, g1, jnp.where(row == 2, g2, 0.0)))
        out = jnp.where(sec == 0, dyv * cv, jnp.where(sec == 1, du * xs, du * cs))
        dp_ref[...] = out.astype(BF16)

    col = lambda off: pl.BlockSpec((L, TW), lambda j, s: (0, off + j))
    return pl.pallas_call(
        kern, grid=(nb, 3),
        in_specs=[col(0), col(0), col(nb), col(2 * nb), pl.BlockSpec((8, TW), lambda j, s: (0, j)),
                  pl.BlockSpec(memory_space=pl.ANY)],
        out_specs=(pl.BlockSpec((L, TW), lambda j, s: (0, s * nb + j)), pl.BlockSpec((8, TW), lambda j, s: (0, j))),
        out_shape=(jax.ShapeDtypeStruct(dproj.shape, BF16), jax.ShapeDtypeStruct((8, D), F32)),
        input_output_aliases={5: 0}, name="sc_bwd", compiler_params=_cp(("parallel", "arbitrary")),
    )(dya, proj, proj, proj, cw, dproj)


def _ssm_conv_fwd(proj, cw4):
    L = proj.shape[0]
    off = C_XBC // TW

    def kern(r_ref, w_ref, o_ref):
        raw = r_ref[...]
        w = w_ref[...]
        c4 = w[0:1] * _down(raw, 3) + w[1:2] * _down(raw, 2) + w[2:3] * _down(raw, 1) + w[3:4] * raw + w[4:5]
        o_ref[...] = c4 * _sigmoid(c4)

    return pl.pallas_call(
        kern, grid=(XBC // TW,),
        in_specs=[pl.BlockSpec((L, TW), lambda j: (0, off + j)), pl.BlockSpec((8, TW), lambda j: (0, j))],
        out_specs=pl.BlockSpec((L, TW), lambda j: (0, j)), out_shape=jax.ShapeDtypeStruct((L, XBC), F32),
        name="ssm_conv_fwd", compiler_params=_cp(("parallel",)),
    )(proj, cw4)


def _ssm_conv_bwd(dx, proj, cw4, dproj, col0, name):
    L, width = dx.shape
    off_p = (C_XBC + col0) // TW
    off_w = col0 // TW

    def kern(d_ref, r_ref, w_ref, _, dp_ref, gw_ref):
        raw = r_ref[...]
        w = w_ref[...]
        r1, r2, r3 = _down(raw, 1), _down(raw, 2), _down(raw, 3)
        c4 = w[0:1] * r3 + w[1:2] * r2 + w[2:3] * r1 + w[3:4] * raw + w[4:5]
        sg = _sigmoid(c4)
        dc4 = d_ref[...] * (sg * (1.0 + c4 * (1.0 - sg)))
        draw = w[3:4] * dc4 + w[2:3] * _up(dc4, 1) + w[1:2] * _up(dc4, 2) + w[0:1] * _up(dc4, 3)
        dp_ref[...] = draw.astype(BF16)
        gs = [jnp.sum(dc4 * r3, axis=0, keepdims=True), jnp.sum(dc4 * r2, axis=0, keepdims=True),
              jnp.sum(dc4 * r1, axis=0, keepdims=True), jnp.sum(dc4 * raw, axis=0, keepdims=True),
              jnp.sum(dc4, axis=0, keepdims=True)]
        row = lax.broadcasted_iota(jnp.int32, (8, TW), 0)
        acc = jnp.zeros((8, TW), F32)
        for k, gk in enumerate(gs):
            acc = jnp.where(row == k, gk, acc)
        gw_ref[...] = acc

    return pl.pallas_call(
        kern, grid=(width // TW,),
        in_specs=[pl.BlockSpec((L, TW), lambda j: (0, j)), pl.BlockSpec((L, TW), lambda j: (0, off_p + j)),
                  pl.BlockSpec((8, TW), lambda j: (0, off_w + j)), pl.BlockSpec(memory_space=pl.ANY)],
        out_specs=(pl.BlockSpec((L, TW), lambda j: (0, off_p + j)), pl.BlockSpec((8, TW), lambda j: (0, j))),
        out_shape=(jax.ShapeDtypeStruct(dproj.shape, BF16), jax.ShapeDtypeStruct((8, width), F32)),
        input_output_aliases={3: 0}, name=name, compiler_params=_cp(("arbitrary",)),
    )(dx, proj, cw4, dproj)


def _split3(v):
    h1 = v.astype(BF16)
    r1 = v - h1.astype(F32)
    h2 = r1.astype(BF16)
    h3 = (r1 - h2.astype(F32)).astype(BF16)
    return h1, h2, h3


def _dot01(m01, v, dims=_DIMS["nn"], m_left=True):
    out = None
    for part in _split3(v):
        ops = (m01, part) if m_left else (part, m01)
        t = lax.dot_general(ops[0], ops[1], dims, preferred_element_type=F32)
        out = t if out is None else out + t
    return out


def _bdot(a, b, mode="nn"):
    return lax.dot_general(a.astype(BF16), b.astype(BF16), _DIMS[mode], preferred_element_type=F32)


def _softplus(v):
    return jnp.maximum(v, 0.0) + jnp.log1p(jnp.exp(-jnp.abs(v)))


def _dt_prep(proj, vec):
    L = proj.shape[0]

    def kern(p_ref, v_ref, dt_ref, cs_ref, sg_ref):
        v = v_ref[...]
        pre = p_ref[:, 0:128] + v[0:1]
        dt = _softplus(pre)
        sg_ref[...] = _sigmoid(pre)
        da = dt * (-jnp.exp(v[1:2]))
        ii = lax.broadcasted_iota(jnp.int32, (Q, Q), 0)
        jj = lax.broadcasted_iota(jnp.int32, (Q, Q), 1)
        ltri = (jj <= ii).astype(BF16)
        dt_ref[...] = dt
        cs_ref[...] = _dot01(ltri, da)

    blk = pl.BlockSpec((Q, 128), lambda c: (c, 0))
    return pl.pallas_call(
        kern, grid=(L // Q,),
        in_specs=[pl.BlockSpec((Q, 256), lambda c: (c, C_DT // 256)), pl.BlockSpec((8, 128), lambda c: (0, 0))],
        out_specs=(blk, blk, blk),
        out_shape=(jax.ShapeDtypeStruct((L, 128), F32),) * 3,
        name="dt_prep", compiler_params=_cp(("parallel",)),
    )(proj, vec)


def _head_masks():
    lane = lax.broadcasted_iota(jnp.int32, (1, 4 * HD), 1)
    return [((lane >= HD * j) & (lane < HD * (j + 1))) for j in range(4)]


def _expand4(v4, masks):
    R = v4.shape[0]
    out = jnp.zeros((R, 4 * HD), F32)
    for j in range(4):
        out = jnp.where(masks[j], jnp.broadcast_to(v4[:, j:j + 1], (R, 4 * HD)), out)
    return out


def _decay_matrix(cs_col, tri):
    colb = jnp.broadcast_to(cs_col, (Q, Q))
    return jnp.exp(jnp.where(tri, colb - colb.T, -jnp.inf))


def _ssd_fwd(xbc, dt4, cs4, vecg):
    L = xbc.shape[0]
    nc = L // Q

    def kern(x_ref, b_ref, c_ref, dt_ref, cs_ref, v_ref, y_ref, s_ref, S):
        c = pl.program_id(1)

        @pl.when(c == 0)
        def _():
            S[...] = jnp.zeros_like(S)

        masks = _head_masks()
        ii = lax.broadcasted_iota(jnp.int32, (Q, Q), 0)
        jj = lax.broadcasted_iota(jnp.int32, (Q, Q), 1)
        tri = jj <= ii
        dt4v, cs4v = dt_ref[0], cs_ref[0]
        dt_b, cs_b = _expand4(dt4v, masks), _expand4(cs4v, masks)
        d_b = _expand4(v_ref[0], masks)[1:2]
        cs_last = cs_b[Q - 1:Q, :]
        x4, bm, cm = x_ref[...], b_ref[...], c_ref[...]
        xdt = x4 * dt_b
        gm = _bdot(cm, bm, "nt")
        s4 = S[...]
        s_ref[0, 0] = s4
        y = _bdot(cm, s4) * jnp.exp(cs_b) + d_b * x4
        for j in range(4):
            mh = gm * _decay_matrix(cs4v[:, j:j + 1], tri)
            y = y + _bdot(mh, jnp.where(masks[j], xdt, 0.0))
        y_ref[...] = y
        S[...] = jnp.exp(cs_last) * s4 + _bdot(bm, xdt * jnp.exp(cs_last - cs_b), "tn")

    sc = pl.BlockSpec((1, Q, 128), lambda g, c: (g, c, 0))
    return pl.pallas_call(
        kern, grid=(NG, nc),
        in_specs=[pl.BlockSpec((Q, 256), lambda g, c: (c, g)),
                  pl.BlockSpec((Q, 128), lambda g, c: (c, INNER // 128 + g)),
                  pl.BlockSpec((Q, 128), lambda g, c: (c, (INNER + NG * NS) // 128 + g)),
                  sc, sc, pl.BlockSpec((1, 8, 128), lambda g, c: (g, 0, 0))],
        out_specs=(pl.BlockSpec((Q, 256), lambda g, c: (c, g)),
                   pl.BlockSpec((1, 1, NS, 256), lambda g, c: (g, c, 0, 0))),
        out_shape=(jax.ShapeDtypeStruct((L, INNER), F32), jax.ShapeDtypeStruct((NG, nc, NS, 256), F32)),
        scratch_shapes=[pltpu.VMEM((NS, 256), F32)], name="ssd_fwd",
        compiler_params=_cp(("parallel", "arbitrary")),
    )(xbc, xbc, xbc, dt4, cs4, vecg)


def _ssd_bwd(xbc, dt4, cs4, sg4, vecg, s_all, dy):
    L = xbc.shape[0]
    nc = L // Q

    def kern(x_ref, b_ref, c_ref, dt_ref, cs_ref, sg_ref, v_ref, s_ref, dy_ref,
             dx_ref, db_ref, dc_ref, ddt_ref, st_ref, dS):
        cc = pl.program_id(1)

        @pl.when(cc == 0)
        def _():
            dS[...] = jnp.zeros_like(dS)
            st_ref[...] = jnp.zeros_like(st_ref)

        masks = _head_masks()
        ii = lax.broadcasted_iota(jnp.int32, (Q, Q), 0)
        jj = lax.broadcasted_iota(jnp.int32, (Q, Q), 1)
        tri = jj <= ii
        utri = (jj >= ii).astype(BF16)
        li = lax.broadcasted_iota(jnp.int32, (4 * HD, 4 * HD), 0)
        lj = lax.broadcasted_iota(jnp.int32, (4 * HD, 4 * HD), 1)
        eblk = ((li // HD) == (lj // HD)).astype(BF16)
        lane128 = lax.broadcasted_iota(jnp.int32, (Q, 128), 1)

        dt4v, cs4v, sg4v = dt_ref[0], cs_ref[0], sg_ref[0]
        dt_b, cs_b = _expand4(dt4v, masks), _expand4(cs4v, masks)
        vv = _expand4(v_ref[0], masks)
        a_b = -jnp.exp(vv[0:1])
        d_b = vv[1:2]
        a4 = -jnp.exp(v_ref[0][0:1, :])
        cs_last = cs_b[Q - 1:Q, :]
        ecs = jnp.exp(cs_b)
        decay = jnp.exp(cs_last - cs_b)
        elast = jnp.exp(cs_last)
        x4, bm, cm, dyv = x_ref[...], b_ref[...], c_ref[...], dy_ref[...]
        s4 = s_ref[0, 0]
        dsn = dS[...]
        xdt = x4 * dt_b
        gm = _bdot(cm, bm, "nt")
        gmt = gm.T
        dye = dyv * ecs
        yoff = ecs * _bdot(cm, s4)
        t4 = _bdot(bm, dsn) * decay
        dxdt = t4
        dg = jnp.zeros((Q, Q), F32)
        rc = jnp.zeros((Q, 4 * HD), F32)
        for j in range(4):
            colb = jnp.broadcast_to(cs4v[:, j:j + 1], (Q, Q))
            seg = colb - colb.T
            lm = jnp.exp(jnp.where(tri, seg, -jnp.inf))
            lmt = jnp.exp(jnp.where(jj >= ii, -seg, -jnp.inf))
            mh = gm * lm
            mht = gmt * lmt
            dyh = jnp.where(masks[j], dyv, 0.0).astype(BF16)
            xh = jnp.where(masks[j], xdt, 0.0).astype(BF16)
            dxdt = dxdt + _bdot(mh, dyh, "tn")
            dmh = _bdot(dyh, xh, "nt")
            dmht = _bdot(xh, dyh, "nt")
            dg = dg + dmh * lm
            rs = jnp.sum(dmh * mh, axis=1, keepdims=True) - jnp.sum(dmht * mht, axis=1, keepdims=True)
            rc = jnp.where(masks[j], jnp.broadcast_to(rs, (Q, 4 * HD)), rc)
        xt = xdt * t4
        da_b = _dot01(eblk, dyv * yoff - xt, m_left=False) + rc
        tail = jnp.sum(xt, axis=0, keepdims=True) + elast * jnp.sum(s4 * dsn, axis=0, keepdims=True)
        tail = _dot01(eblk, jnp.broadcast_to(tail, (8, 4 * HD)), m_left=False)[0:1]
        dda_b = _dot01(utri, da_b) + tail
        ddt_b = dda_b * a_b + _dot01(eblk, dxdt * x4, m_left=False)
        ddt4 = jnp.zeros((Q, 128), F32)
        dda4 = jnp.zeros((Q, 128), F32)
        for j in range(4):
            ddt4 = jnp.where(lane128 == j, jnp.broadcast_to(ddt_b[:, HD * j:HD * j + 1], (Q, 128)), ddt4)
            dda4 = jnp.where(lane128 == j, jnp.broadcast_to(dda_b[:, HD * j:HD * j + 1], (Q, 128)), dda4)
        ddt_ref[0] = ddt4 * sg4v
        ga = jnp.sum(dda4 * dt4v * a4, axis=0, keepdims=True)
        gd_b = _dot01(eblk, jnp.broadcast_to(jnp.sum(dyv * x4, axis=0, keepdims=True), (8, 4 * HD)), m_left=False)[0:1]
        gd = jnp.zeros((1, 128), F32)
        for j in range(4):
            gd = jnp.where(lane128[0:1] == j, jnp.broadcast_to(gd_b[:, HD * j:HD * j + 1], (1, 128)), gd)
        row = lax.broadcasted_iota(jnp.int32, (8, 128), 0)
        st_ref[0] += jnp.where(row == 0, ga, jnp.where(row == 1, gd, 0.0))
        dx_ref[...] = d_b * dyv + dxdt * dt_b
        dc_ref[...] = _bdot(dg, bm) + _bdot(dye, s4, "nt")
        db_ref[...] = _bdot(dg, cm, "tn") + _bdot(xdt * decay, dsn, "nt")
        dS[...] = elast * dsn + _bdot(cm, dye, "tn")

    rv = lambda c: nc - 1 - c
    sc = pl.BlockSpec((1, Q, 128), lambda g, c: (g, rv(c), 0))
    return pl.pallas_call(
        kern, grid=(NG, nc),
        in_specs=[pl.BlockSpec((Q, 256), lambda g, c: (rv(c), g)),
                  pl.BlockSpec((Q, 128), lambda g, c: (rv(c), INNER // 128 + g)),
                  pl.BlockSpec((Q, 128), lambda g, c: (rv(c), (INNER + NG * NS) // 128 + g)),
                  sc, sc, sc, pl.BlockSpec((1, 8, 128), lambda g, c: (g, 0, 0)),
                  pl.BlockSpec((1, 1, NS, 256), lambda g, c: (g, rv(c), 0, 0)),
                  pl.BlockSpec((Q, 256), lambda g, c: (rv(c), g))],
        out_specs=(pl.BlockSpec((Q, 256), lambda g, c: (rv(c), g)),
                   pl.BlockSpec((Q, 128), lambda g, c: (rv(c), g)),
                   pl.BlockSpec((Q, 128), lambda g, c: (rv(c), g)),
                   pl.BlockSpec((1, Q, 128), lambda g, c: (g, rv(c), 0)),
                   pl.BlockSpec((1, 8, 128), lambda g, c: (g, 0, 0))),
        out_shape=(jax.ShapeDtypeStruct((L, INNER), F32), jax.ShapeDtypeStruct((L, NG * NS), F32),
                   jax.ShapeDtypeStruct((L, NG * NS), F32), jax.ShapeDtypeStruct((NG, L, 128), F32),
                   jax.ShapeDtypeStruct((NG, 8, 128), F32)),
        scratch_shapes=[pltpu.VMEM((NS, 256), F32)], name="ssd_bwd",
        compiler_params=_cp(("parallel", "arbitrary")),
    )(xbc, xbc, xbc, dt4, cs4, sg4, vecg, s_all, dy)


def _dt_bwd(ddt, dproj, tl=256):
    L = ddt.shape[0]

    def kern(d_ref, _, dp_ref, gs_ref):
        @pl.when(pl.program_id(0) == 0)
        def _():
            gs_ref[...] = jnp.zeros_like(gs_ref)

        d = d_ref[...]
        gs_ref[...] += jnp.broadcast_to(jnp.sum(d, axis=0, keepdims=True), (8, 128))
        dp_ref[...] = jnp.concatenate([d, jnp.zeros_like(d)], axis=1).astype(BF16)

    return pl.pallas_call(
        kern, grid=(L // tl,),
        in_specs=[pl.BlockSpec((tl, 128), lambda i: (i, 0)), pl.BlockSpec(memory_space=pl.ANY)],
        out_specs=(pl.BlockSpec((tl, 256), lambda i: (i, C_DT // 256)), pl.BlockSpec((8, 128), lambda i: (0, 0))),
        out_shape=(jax.ShapeDtypeStruct(dproj.shape, BF16), jax.ShapeDtypeStruct((8, 128), F32)),
        input_output_aliases={1: 0}, name="dt_bwd", compiler_params=_cp(("arbitrary",)),
    )(ddt, dproj)


GW = INNER // NG


def _gnorm_fwd(y, proj, w, tl=256):
    L = y.shape[0]
    zoff = C_Z // 1024

    def kern(y_ref, z_ref, w_ref, o_ref):
        z = z_ref[...]
        yz = y_ref[...] * (z * _sigmoid(z))
        wv = w_ref[...]
        for k in range(1024 // GW):
            sl = slice(GW * k, GW * (k + 1))
            v = yz[:, sl]
            rg = lax.rsqrt(jnp.mean(v * v, axis=-1, keepdims=True) + EPS)
            o_ref[:, sl] = ((v * rg) * wv[:, sl]).astype(BF16)

    blk = pl.BlockSpec((tl, 1024), lambda i, j: (i, j))
    return pl.pallas_call(
        kern, grid=(L // tl, 2),
        in_specs=[blk, pl.BlockSpec((tl, 1024), lambda i, j: (i, zoff + j)), pl.BlockSpec((1, 1024), lambda i, j: (0, j))],
        out_specs=blk, out_shape=jax.ShapeDtypeStruct((L, INNER), BF16), name="gnorm_fwd",
        compiler_params=_cp(("parallel", "parallel")),
    )(y, proj, w.reshape(1, INNER))


def _gnorm_bwd(dyb, y, proj, w, dproj, tl=256):
    L = y.shape[0]
    zoff = C_Z // 1024

    def kern(d_ref, y_ref, z_ref, w_ref, _, dy_ref, dp_ref, gw_ref):
        @pl.when(pl.program_id(1) == 0)
        def _():
            gw_ref[...] = jnp.zeros_like(gw_ref)

        z = z_ref[...]
        sg = _sigmoid(z)
        sz = z * sg
        yv = y_ref[...]
        yz = yv * sz
        dv = d_ref[...]
        wv = w_ref[...]
        for k in range(1024 // GW):
            sl = slice(GW * k, GW * (k + 1))
            v = yz[:, sl]
            rg = lax.rsqrt(jnp.mean(v * v, axis=-1, keepdims=True) + EPS)
            vn = v * rg
            dk = dv[:, sl]
            gw_ref[:, sl] += jnp.broadcast_to(jnp.sum(dk * vn, axis=0, keepdims=True), (8, GW))
            dvn = dk * wv[:, sl]
            dyz = rg * (dvn - vn * jnp.mean(dvn * vn, axis=-1, keepdims=True))
            dy_ref[:, sl] = dyz * sz[:, sl]
            dp_ref[:, sl] = (dyz * yv[:, sl] * (sg[:, sl] * (1.0 + z[:, sl] * (1.0 - sg[:, sl])))).astype(BF16)

    blk = pl.BlockSpec((tl, 1024), lambda j, i: (i, j))
    zblk = pl.BlockSpec((tl, 1024), lambda j, i: (i, zoff + j))
    return pl.pallas_call(
        kern, grid=(2, L // tl),
        in_specs=[blk, blk, zblk, pl.BlockSpec((1, 1024), lambda j, i: (0, j)), pl.BlockSpec(memory_space=pl.ANY)],
        out_specs=(blk, zblk, pl.BlockSpec((8, 1024), lambda j, i: (0, j))),
        out_shape=(jax.ShapeDtypeStruct((L, INNER), F32), jax.ShapeDtypeStruct(dproj.shape, BF16),
                   jax.ShapeDtypeStruct((8, INNER), F32)),
        input_output_aliases={4: 1}, name="gnorm_bwd", compiler_params=_cp(("parallel", "arbitrary")),
    )(dyb, y, proj, w.reshape(1, INNER), dproj)


def _merge_fwd(proj, bg, br_a, br_b, tl=256):
    L = proj.shape[0]
    goff = C_GATE // 1024

    def kern(g1_ref, g2_ref, b1_ref, b2_ref, a_ref, b_ref, o_ref):
        g1 = _sigmoid(g1_ref[...] + b1_ref[...])
        g2 = _sigmoid(g2_ref[...] + b2_ref[...])
        o_ref[...] = (g1 * a_ref[...] + g2 * b_ref[...]).astype(BF16)

    row = pl.BlockSpec((tl, 1024), lambda i: (i, 0))
    bg2 = bg.reshape(1, 2 * D)
    return pl.pallas_call(
        kern, grid=(L // tl,),
        in_specs=[pl.BlockSpec((tl, 1024), lambda i: (i, goff)), pl.BlockSpec((tl, 1024), lambda i: (i, goff + 1)),
                  pl.BlockSpec((1, 1024), lambda i: (0, 0)), pl.BlockSpec((1, 1024), lambda i: (0, 1)), row, row],
        out_specs=row, out_shape=jax.ShapeDtypeStruct((L, D), BF16), name="merge_fwd",
        compiler_params=_cp(("parallel",)),
    )(proj, proj, bg2, bg2, br_a, br_b)


def _merge_bwd(dm, proj, bg, br_a, br_b, dproj, tl=256):
    L = proj.shape[0]
    goff = C_GATE // 1024

    def kern(dm_ref, g_ref, b_ref, a_ref, bb_ref, _, dbr_ref, dp_ref, gb_ref):
        j = pl.program_id(0)

        @pl.when(pl.program_id(1) == 0)
        def _():
            gb_ref[...] = jnp.zeros_like(gb_ref)

        g = _sigmoid(g_ref[...] + b_ref[...])
        br = jnp.where(j == 0, a_ref[...], bb_ref[...])
        dmv = dm_ref[...]
        dbr_ref[0] = (dmv * g).astype(BF16)
        dgate = dmv * br * g * (1.0 - g)
        gb_ref[...] += jnp.broadcast_to(jnp.sum(dgate, axis=0, keepdims=True), (8, 1024))
        dp_ref[...] = dgate.astype(BF16)

    row = pl.BlockSpec((tl, 1024), lambda j, i: (i, 0))
    gblk = pl.BlockSpec((tl, 1024), lambda j, i: (i, goff + j))
    return pl.pallas_call(
        kern, grid=(2, L // tl),
        in_specs=[row, gblk, pl.BlockSpec((1, 1024), lambda j, i: (0, j)), row, row, pl.BlockSpec(memory_space=pl.ANY)],
        out_specs=(pl.BlockSpec((1, tl, 1024), lambda j, i: (j, i, 0)), gblk, pl.BlockSpec((8, 1024), lambda j, i: (0, j))),
        out_shape=(jax.ShapeDtypeStruct((2, L, D), BF16), jax.ShapeDtypeStruct(dproj.shape, BF16),
                   jax.ShapeDtypeStruct((8, 2 * D), F32)),
        input_output_aliases={5: 1}, name="merge_bwd", compiler_params=_cp(("parallel", "arbitrary")),
    )(dm, proj, bg.reshape(1, 2 * D), br_a, br_b, dproj)


def _coords():
    return lax.axis_index("x"), lax.axis_index("y"), lax.axis_index("c")


def _other_chips(sk):
    xk, yk = sk // 2, sk % 2
    return [((1 - xk, yk), 2 * (1 - xk) + yk), ((xk, 1 - yk), 2 * xk + 1 - yk), ((1 - xk, 1 - yk), 2 * (1 - xk) + 1 - yk)]


def _rows(start, size):
    assert size % 128 == 0
    return pl.ds(pl.multiple_of(start, 128), size)


def _per_chip(fn):
    x, y, _ = _coords()
    s = 2 * x + y
    for sk in range(4):
        pl.when(s == sk)(functools.partial(fn, sk))


XTRA = PIECE - PMAIN


def _place(shard, full_shape, block, index_map, idx, name, blk0=0, nblk=None, dep=None):
    in_block = block[-2:]
    if nblk is None:
        nblk = shard.shape[0] // in_block[0]

    def kern(idx_ref, s_ref, *rest):
        o_ref = rest[-1]
        o_ref[...] = s_ref[...].astype(BF16).reshape(o_ref.shape)

    grid_spec = pltpu.PrefetchScalarGridSpec(
        num_scalar_prefetch=1, grid=(nblk,),
        in_specs=[pl.BlockSpec(in_block, lambda i, idx_ref: (blk0 + i, 0))] + ([_ANY] if dep is not None else []),
        out_specs=pl.BlockSpec(block, index_map))
    args = (idx, shard) + ((dep,) if dep is not None else ())
    return pl.pallas_call(kern, grid_spec=grid_spec, out_shape=jax.ShapeDtypeStruct(full_shape, BF16), name=name,
                          compiler_params=_cp(("arbitrary",)))(*args)


_SEM = pl.BlockSpec(memory_space=pltpu.SEMAPHORE)
_EFFECT = pltpu.SideEffectType.DATAFLOW_SIDE_EFFECTING


_ANY = pl.BlockSpec(memory_space=pl.ANY)


def _tie(v, dep, name):
    def body(v_ref, dep_ref, o_ref):
        del v_ref, dep_ref, o_ref

    return pl.pallas_call(body, out_shape=jax.ShapeDtypeStruct(v.shape, v.dtype), in_specs=[_ANY, _ANY],
                          out_specs=_ANY, input_output_aliases={0: 0}, name=name)(v, dep)


def _split_call(name, arrays, start=None, wait=None, wait_sems=None, after=None):
    keys = list(arrays)
    n = len(keys)
    n_start = start.n if start is not None else 0

    def body(*refs):
        pos = n
        if wait is not None:
            wss, wrs = refs[pos], refs[pos + 1]
            pos += 2
        if after is not None:
            pos += 1
        if start is not None:
            nss, nrs = refs[pos], refs[pos + 1]
            pos += 2
        R = dict(zip(keys, refs[pos:pos + n]))
        token = refs[pos + n]
        x, y, c = _coords()

        def desc(src, dst, dev, ss, rs, k):
            return pltpu.make_async_remote_copy(src_ref=src, dst_ref=dst, send_sem=ss.at[k], recv_sem=rs.at[k],
                                                device_id=dev, device_id_type=MESH)

        def run(sk):
            if wait is not None:
                for k, (snd, land) in enumerate(wait.copies(sk, R)):
                    if snd is not None:
                        desc(snd[0], snd[1], snd[2], wss, wrs, k).wait_send()
                    if land is not None:
                        desc(land, land, (x, y, c), wss, wrs, k).wait_recv()
            if start is not None:
                for k, (snd, land) in enumerate(start.copies(sk, R)):
                    if snd is not None:
                        desc(snd[0], snd[1], snd[2], nss, nrs, k).start()

        _per_chip(run)
        token[...] = jnp.zeros_like(token)

    hbm = pl.BlockSpec(memory_space=HBM)
    vals = [arrays[k] for k in keys]
    ins, in_specs = list(vals), [hbm] * n
    if wait is not None:
        ins += list(wait_sems)
        in_specs += [_SEM, _SEM]
    if after is not None:
        ins.append(after)
        in_specs.append(pl.BlockSpec(memory_space=pl.ANY))
    out_shape, out_specs = [], []
    if start is not None:
        out_shape += [pltpu.SemaphoreType.DMA((n_start,)), pltpu.SemaphoreType.DMA((n_start,))]
        out_specs += [_SEM, _SEM]
    first = len(out_shape)
    out_shape += [jax.ShapeDtypeStruct(v.shape, v.dtype) for v in vals] + [jax.ShapeDtypeStruct((8, 128), F32)]
    out_specs += [hbm] * n + [pl.BlockSpec(memory_space=pltpu.VMEM)]
    res = pl.pallas_call(
        body, out_shape=tuple(out_shape), in_specs=in_specs, out_specs=tuple(out_specs),
        input_output_aliases={i: first + i for i in range(n)}, name=name,
        compiler_params=pltpu.CompilerParams(has_side_effects=_EFFECT),
    )(*ins)
    sems = (res[0], res[1]) if start is not None else None
    return dict(zip(keys, res[first:first + n])), sems, res[-1]


class _Plan:
    def __init__(self, n, copies):
        self.n, self.copies = n, copies


_HM, _HX = PMAIN // 2, XTRA // 2
_WIN = {
    "wct": (True, lambda r, sc, hc: r.at[_rows(PMAIN * sc + _HM * hc, _HM), :]),
    "xt": (True, lambda r, sc, hc: r.at[sc, _rows(_HX * hc, _HX), :]),
    "w1": (True, lambda r, sc, hc: r.at[_rows(512 * hc, 512), pl.ds(1024 * sc, 1024)]),
    "w2": (True, lambda r, sc, hc: r.at[_rows(1024 * sc + 512 * hc, 512), :]),
    "wa": (True, lambda r, sc, hc: r.at[_rows(256 * sc + 128 * hc, 128), :]),
    "wb": (True, lambda r, sc, hc: r.at[_rows(512 * sc + 256 * hc, 256), :]),
    "wo": (True, lambda r, sc, hc: r.at[_rows(256 * sc + 128 * hc, 128), :]),
    "cw": (False, lambda r, sc, hc: r.at[sc]),
}


def _ag_chips_plan(keys):
    def copies(sk, R):
        _, _, c = _coords()
        out = []
        for key in keys:
            win = _WIN[key][1]
            for (px, py), ps in _other_chips(sk):
                w = win(R[key], sk, c)
                out.append(((w, w, (px, py, c)), win(R[key], ps, c)))
        return out
    return _Plan(3 * len(keys), copies)


def _ag_sibling_plan(keys):
    keys = [k for k in keys if _WIN[k][0]]

    def copies(sk, R):
        x, y, c = _coords()
        out = []
        for key in keys:
            win = _WIN[key][1]
            for _, ps in _other_chips(sk):
                w = win(R[key], ps, c)
                out.append(((w, w, (x, y, 1 - c)), win(R[key], ps, 1 - c)))
        return out
    return _Plan(3 * len(keys), copies)


def _fix_wct(wct, xt):
    nb = PMAIN // XTRA

    def kern(w_ref, x_ref, o_ref):
        k = pl.program_id(0)
        xv = x_ref[0]
        o_ref[...] = jnp.where(k < 3, (w_ref[...].astype(F32) + xv.astype(F32)).astype(BF16), xv)

    blk = pl.BlockSpec((XTRA, D), lambda k: (nb * (k + 1), 0))
    rblk = pl.BlockSpec((XTRA, D), lambda k: (jnp.where(k < 3, nb * (k + 1), 0), 0))
    return pl.pallas_call(
        kern, grid=(4,), in_specs=[rblk, pl.BlockSpec((1, XTRA, D), lambda k: (k, 0, 0))], out_specs=blk,
        out_shape=jax.ShapeDtypeStruct(wct.shape, BF16), input_output_aliases={0: 0}, name="fix_wct",
        compiler_params=_cp(("arbitrary",)),
    )(wct, xt)


_HP = PIECE // 2
_GWIN = [
    lambda r, sc, hc: r.at[_rows(PMAIN * sc + _HP * hc, _HP), :],
    lambda r, sc, hc: r.at[_rows(512 * hc, 512), pl.ds(1024 * sc, 1024)],
    lambda r, sc, hc: r.at[_rows(1024 * sc + 512 * hc, 512), :],
    lambda r, sc, hc: r.at[_rows(256 * sc + 128 * hc, 128), :],
    lambda r, sc, hc: r.at[_rows(512 * sc + 256 * hc, 256), :],
    lambda r, sc, hc: r.at[_rows(256 * sc + 128 * hc, 128), :],
]
HALF_SHAPES = [(PIECE // 2, D), (512, 1024), (512, 1024), (128, 1024), (256, 1024), (128, 1024)]


def _rs_sibling_plan(ts):
    def copies(sk, R):
        x, y, c = _coords()
        out = []
        for t in ts:
            for sc in range(4):
                land = R["ra%d" % t].at[sc]
                out.append(((_GWIN[t](R["g%d" % t], sc, 1 - c), land, (x, y, 1 - c)), land))
        return out
    return _Plan(4 * len(ts), copies)


def _rs_chips_plan(ts):
    def copies(sk, R):
        _, _, c = _coords()
        out = []
        for t in ts:
            for j, ((px, py), ps) in enumerate(_other_chips(sk)):
                land = R["rb%d" % t].at[j]
                out.append(((R["hb%d" % t].at[ps], land, (px, py, c)), land))
        return out
    return _Plan(3 * len(ts), copies)


def _rs_share_plan(ts):
    def copies(sk, R):
        x, y, c = _coords()
        out = []
        for t in ts:
            rows = HALF_SHAPES[t][0]
            mine = R["f%d" % t].at[_rows(rows * c, rows), :]
            out.append(((mine, mine, (x, y, 1 - c)), R["f%d" % t].at[_rows(rows * (1 - c), rows), :]))
        return out
    return _Plan(len(ts), copies)


def _half_tiling(t):
    rows, cols = HALF_SHAPES[t]
    if t == 0:
        return (256, cols), rows // 256, lambda i: (i, 0)
    if t == 1:
        return (rows, 256), cols // 256, lambda i: (0, i)
    return (rows, cols), 1, lambda i: (0, 0)


def _window_block(t, sc, hc, i):
    if t == 0:
        return (PMAIN // 256) * sc + (PIECE // 512) * hc + i, 0
    if t == 1:
        return hc, 4 * sc + i
    return 2 * sc + hc, 0


def _chip_sum(g, ra, t, idx, name):
    rows, cols = HALF_SHAPES[t]
    blk, nblk, inner = _half_tiling(t)

    def kern(idx_ref, g_ref, r_ref, hb_ref, hf_ref):
        v = g_ref[...].astype(F32) + r_ref[0].astype(F32)
        hb_ref[0] = v.astype(BF16)
        hf_ref[0] = v

    gmap = lambda sc, i, idx_ref: _window_block(t, sc, idx_ref[1], i)
    omap = lambda sc, i, idx_ref: (sc,) + inner(i)
    grid_spec = pltpu.PrefetchScalarGridSpec(
        num_scalar_prefetch=1, grid=(4, nblk),
        in_specs=[pl.BlockSpec(blk, gmap), pl.BlockSpec((1,) + blk, omap)],
        out_specs=(pl.BlockSpec((1,) + blk, omap), pl.BlockSpec((1,) + blk, omap)))
    return pl.pallas_call(
        kern, grid_spec=grid_spec,
        out_shape=(jax.ShapeDtypeStruct((4, rows, cols), BF16), jax.ShapeDtypeStruct((4, rows, cols), F32)),
        name=name, compiler_params=_cp(("parallel", "parallel")),
    )(idx, g, ra)


def _final_sum(hf, rb, t, idx, name):
    rows, cols = HALF_SHAPES[t]
    blk, nblk, inner = _half_tiling(t)
    nbr = rows // blk[0]

    def kern(idx_ref, h_ref, r_ref, o_ref):
        o_ref[...] = ((h_ref[0] + r_ref[0].astype(F32)) + r_ref[1].astype(F32)) + r_ref[2].astype(F32)

    def omap(i, idx_ref):
        r, cidx = inner(i)
        return nbr * idx_ref[1] + r, cidx

    grid_spec = pltpu.PrefetchScalarGridSpec(
        num_scalar_prefetch=1, grid=(nblk,),
        in_specs=[pl.BlockSpec((1,) + blk, lambda i, idx_ref: (idx_ref[0],) + inner(i)),
                  pl.BlockSpec((3,) + blk, lambda i, idx_ref: (0,) + inner(i))],
        out_specs=pl.BlockSpec(blk, omap))
    return pl.pallas_call(
        kern, grid_spec=grid_spec, out_shape=jax.ShapeDtypeStruct((2 * rows, cols), F32),
        name=name, compiler_params=_cp(("parallel",)),
    )(idx, hf, rb)


class _ReduceScatter:
    def __init__(self, ts, grads, idx, tag):
        self.ts, self.idx, self.tag = ts, idx, tag
        arr = {}
        for t in ts:
            arr["g%d" % t] = grads[t]
            arr["ra%d" % t] = lax.empty((4,) + HALF_SHAPES[t], BF16)
        self.plan = _rs_sibling_plan(ts)
        self.arr, self.sems, self.token = _split_call("rs_sibling_start_" + tag, arr, start=self.plan)

    def chips(self, after):
        arr, _, _ = _split_call("rs_sibling_wait_" + self.tag, self.arr, wait=self.plan, wait_sems=self.sems, after=after)
        brr, self.hf = {}, {}
        for t in self.ts:
            hb, self.hf[t] = _chip_sum(arr["g%d" % t], arr["ra%d" % t], t, self.idx, "chip_sum_%d" % t)
            brr["hb%d" % t] = hb
            brr["rb%d" % t] = lax.empty((3,) + HALF_SHAPES[t], BF16)
        self.plan = _rs_chips_plan(self.ts)
        self.arr, self.sems, self.token = _split_call("rs_chips_start_" + self.tag, brr, start=self.plan)
        return self.token

    def share(self, after):
        brr, _, _ = _split_call("rs_chips_wait_" + self.tag, self.arr, wait=self.plan, wait_sems=self.sems, after=after)
        frr = {"f%d" % t: _final_sum(self.hf[t], brr["rb%d" % t], t, self.idx, "final_sum_%d" % t) for t in self.ts}
        self.plan = _rs_share_plan(self.ts)
        self.arr, self.sems, self.token = _split_call("rs_share_start_" + self.tag, frr, start=self.plan)
        return self.token

    def result(self, after):
        frr, _, _ = _split_call("rs_share_wait_" + self.tag, self.arr, wait=self.plan, wait_sems=self.sems, after=after)
        return {t: frr["f%d" % t] for t in self.ts}


def _small_all_gather(v):
    def body(v_ref, o_ref, send_sems, recv_sems, loc_sem):
        x, y, c = _coords()
        me = 4 * x + 2 * y + c
        lc = pltpu.make_async_copy(v_ref, o_ref.at[me], loc_sem)
        lc.start()
        cps = []
        for k in range(1, 8):
            fx, fy, fc = (k >> 2) & 1, (k >> 1) & 1, k & 1
            dev = ((1 - x) if fx else x, (1 - y) if fy else y, (1 - c) if fc else c)
            cp = pltpu.make_async_remote_copy(src_ref=v_ref, dst_ref=o_ref.at[me], send_sem=send_sems.at[k - 1],
                                              recv_sem=recv_sems.at[k - 1], device_id=dev, device_id_type=MESH)
            cp.start()
            cps.append((cp, 4 * dev[0] + 2 * dev[1] + dev[2]))
        for k, (cp, frm) in enumerate(cps):
            got = o_ref.at[frm]
            pltpu.make_async_remote_copy(src_ref=got, dst_ref=got, send_sem=send_sems.at[k], recv_sem=recv_sems.at[k],
                                         device_id=(x, y, c), device_id_type=MESH).wait_recv()
        for cp, _ in cps:
            cp.wait_send()
        lc.wait()

    hbm = pl.BlockSpec(memory_space=HBM)
    return pl.pallas_call(
        body, out_shape=jax.ShapeDtypeStruct((8,) + v.shape, F32), in_specs=[hbm], out_specs=hbm,
        scratch_shapes=[pltpu.SemaphoreType.DMA((7,)), pltpu.SemaphoreType.DMA((7,)), pltpu.SemaphoreType.DMA(())],
        name="small_all_gather", compiler_params=pltpu.CompilerParams(has_side_effects=True),
    )(v)


def _sum8(v):
    def kern(v_ref, o_ref):
        acc = v_ref[0]
        for k in range(1, 8):
            acc = acc + v_ref[k]
        o_ref[...] = acc

    return pl.pallas_call(kern, out_shape=jax.ShapeDtypeStruct(v.shape[1:], F32), name="small_sum")(v)


def _adamw(w, g, m, v, name, tr=128, blk0=0, nblk=None, into=None, copy_g=False):
    R, C = w.shape
    tr = min(tr, R)
    if nblk is None:
        assert R % tr == 0 and blk0 == 0
        nblk = R // tr
    n_out = 4 if copy_g else 3

    def kern(*refs):
        w_ref, g_ref, m_ref, v_ref = refs[:4]
        d_ref, mo_ref, vo_ref = refs[-n_out:][:3]
        gv = g_ref[...]
        mn = ADAM_B1 * m_ref[...] + (1.0 - ADAM_B1) * gv
        vn = ADAM_B2 * v_ref[...] + (1.0 - ADAM_B2) * (gv * gv)
        m_hat = mn / (1.0 - ADAM_B1 ** ADAM_STEP)
        v_hat = vn / (1.0 - ADAM_B2 ** ADAM_STEP)
        d_ref[...] = -ADAM_LR * (m_hat / (jnp.sqrt(v_hat) + ADAM_EPS) + ADAM_WD * w_ref[...])
        mo_ref[...] = mn
        vo_ref[...] = vn
        if copy_g:
            refs[-1][...] = gv

    blk = pl.BlockSpec((tr, C), lambda i: (blk0 + i, 0))
    sd = jax.ShapeDtypeStruct((R, C), F32)
    in_specs, args, aliases = [blk] * 4, [w, g, m, v], {}
    if into is not None:
        in_specs += [pl.BlockSpec(memory_space=pl.ANY)] * 3
        args += list(into)
        aliases = {4: 0, 5: 1, 6: 2}
    return pl.pallas_call(kern, grid=(nblk,), in_specs=in_specs, out_specs=(blk,) * n_out, out_shape=(sd,) * n_out,
                          input_output_aliases=aliases, name=name, compiler_params=_cp(("parallel",)))(*args)


def _to_piece(wt, s):
    gen = lax.dynamic_update_slice(jnp.zeros((PIECE, D), wt.dtype), wt, (8 * s, 0))
    z = lambda n: jnp.zeros((n, D), wt.dtype)
    last = jnp.concatenate([z(24), wt[:744], wt[776:], wt[744:776], z(PIECE - 24 - W_SHARD)], axis=0)
    return jnp.where(s == 3, last, gen)


def _from_piece(p, s):
    gen = lax.dynamic_slice(p, (8 * s, 0), (W_SHARD, D))
    last = jnp.concatenate([p[24:768], p[2816:2848], p[768:2816]], axis=0)
    return jnp.where(s == 3, last, gen)


_SMALL = [("norm_mix", 1024), ("b_gate", 2048), ("ssm_conv_b", 4096), ("dt_bias", 32), ("A_log", 32), ("D_skip", 32),
          ("ssm_norm_w", 2048), ("norm_mlp", 1024), ("norm_final", 1024), ("sc_conv_w", 3072), ("ssm_conv_w", 16384),
          ("loss", 1)]


def _pack(vals, table, rows):
    parts = []
    for name, n in table:
        v = vals[name].reshape(-1).astype(F32)
        pad = (-n) % 128
        parts.append(jnp.pad(v, (0, pad)) if pad else v)
    flat = jnp.concatenate(parts)
    return jnp.pad(flat, (0, rows * 128 - flat.shape[0])).reshape(rows, 128)


def _unpack(arr, table):
    flat = arr.reshape(-1)
    out, off = {}, 0
    for name, n in table:
        out[name] = flat[off:off + n]
        off += n + ((-n) % 128)
    return out


def kernel(x, norm_mix, w_in, b_gate, sc_conv_w, ssm_conv_w, ssm_conv_b, dt_bias, A_log, D_skip, ssm_norm_w, w_branch_sc, w_branch_ssm, w_out, norm_mlp, w_mlp1, w_mlp2, norm_final, loss_target, m_norm_mix, m_w_in, m_b_gate, m_sc_conv_w, m_ssm_conv_w, m_ssm_conv_b, m_dt_bias, m_A_log, m_D_skip, m_ssm_norm_w, m_w_branch_sc, m_w_branch_ssm, m_w_out, m_norm_mlp, m_w_mlp1, m_w_mlp2, m_norm_final, v_norm_mix, v_w_in, v_b_gate, v_sc_conv_w, v_ssm_conv_w, v_ssm_conv_b, v_dt_bias, v_A_log, v_D_skip, v_ssm_norm_w, v_w_branch_sc, v_w_branch_ssm, v_w_out, v_norm_mlp, v_w_mlp1, v_w_mlp2, v_norm_final):
    L = x.shape[1]
    nc = L // Q
    xi, yi, ci = lax.axis_index("x"), lax.axis_index("y"), lax.axis_index("c")
    s = 2 * xi + yi
    idx = jnp.stack([s, ci]).astype(jnp.int32)
    x0 = x.reshape(L, D)
    tgt = loss_target.reshape(L, D)

    piece = _to_piece(w_in.T, s)
    nb = PMAIN // XTRA
    wct0 = _place(piece, (NCW, D), (XTRA, D), lambda i, r: (nb * r[0] + i, 0), idx, "place_wct", nblk=nb)
    xt0 = _place(piece, (4, XTRA, D), (1, XTRA, D), lambda i, r: (r[0], 0, 0), idx, "place_xt", blk0=nb, nblk=1)
    cws = jnp.zeros((8, 1280), F32)
    cws = cws.at[0:3, 0:256].set(sc_conv_w).at[0:4, 256:1280].set(ssm_conv_w)
    cw0 = lax.dynamic_update_slice(jnp.zeros((4, 8, 1280), F32), cws[None], (s, 0, 0))
    win_keys, mid_keys, end_keys = ["wct", "xt", "cw"], ["wa", "wb", "wo", "w1"], ["w2"]
    gw, sems_w, tok = _split_call("ag_win_start", {"wct": wct0, "xt": xt0, "cw": cw0}, start=_ag_chips_plan(win_keys))
    wa0 = _place(w_branch_sc, (D, D), (256, 1024), lambda i, r: (r[0], 0), idx, "place_wa", dep=tok)
    wb0 = _place(w_branch_ssm, (INNER, D), (512, 1024), lambda i, r: (r[0], 0), idx, "place_wb", dep=tok)
    wo0 = _place(w_out, (D, D), (256, 1024), lambda i, r: (r[0], 0), idx, "place_wo", dep=tok)
    w10 = _place(w_mlp1, (D, DFF), (256, 1024), lambda i, r: (i, r[0]), idx, "place_w1", dep=tok)
    gm, sems_m, tok = _split_call("ag_mid_start", {"wa": wa0, "wb": wb0, "wo": wo0, "w1": w10},
                                  start=_ag_chips_plan(mid_keys))
    w20 = _place(w_mlp2, (DFF, D), (256, 1024), lambda i, r: (4 * r[0] + i, 0), idx, "place_w2", dep=tok)
    ge, sems_e, tok = _split_call("ag_end_start", {"w2": w20}, start=_ag_chips_plan(end_keys))
    h = _rms_fwd(x0, norm_mix, "rms_mix", dep=tok)
    gw, sems_w, tok = _split_call("ag_win_pass", gw, wait=_ag_chips_plan(win_keys), wait_sems=sems_w,
                                  start=_ag_sibling_plan(win_keys), after=h)
    gw, _, _ = _split_call("ag_win_done", gw, wait=_ag_sibling_plan(win_keys), wait_sems=sems_w, after=tok)
    wc, cw_all = _fix_wct(gw["wct"], gw["xt"]), gw["cw"]
    sc_w_full = jnp.concatenate([cw_all[k, :, 0:256] for k in range(4)], axis=1)
    ssm_w_full = jnp.concatenate([cw_all[k, :, 256:1280] for k in range(4)], axis=1)
    cw4 = ssm_w_full.at[4].set(ssm_conv_b)
    vec = jnp.zeros((8, 128), F32).at[0, :NH].set(dt_bias).at[1, :NH].set(A_log)
    vecg = jnp.zeros((NG, 8, 128), F32).at[:, 0, :4].set(A_log.reshape(NG, 4)).at[:, 1, :4].set(D_skip.reshape(NG, 4))

    proj = _matmul(h, wc, "nt", F32, 512, 1280, 1024, "in_proj", n_outer=True)
    gm, sems_m, tok = _split_call("ag_mid_pass", gm, wait=_ag_chips_plan(mid_keys), wait_sems=sems_m,
                                  start=_ag_sibling_plan(mid_keys), after=proj)
    proj = _tie(proj, tok, "tie_proj")
    ya = _sc_fwd(proj, sc_w_full)
    xbc = _ssm_conv_fwd(proj, cw4)
    dt, cs, sg = _dt_prep(proj, vec)
    per_group = lambda a: jnp.pad(a[:, :NH].reshape(L, NG, 4).transpose(1, 0, 2), ((0, 0), (0, 0), (0, 124)))
    dt4, cs4, sg4 = per_group(dt), per_group(cs), per_group(sg)
    ge, sems_e, tok = _split_call("ag_end_pass", ge, wait=_ag_chips_plan(end_keys), wait_sems=sems_e,
                                  start=_ag_sibling_plan(end_keys), after=xbc)
    xbc = _tie(xbc, tok, "tie_xbc")
    y, s_all = _ssd_fwd(xbc, dt4, cs4, vecg)
    yb = _gnorm_fwd(y, proj, ssm_norm_w)
    gm, _, _ = _split_call("ag_mid_done", gm, wait=_ag_sibling_plan(mid_keys), wait_sems=sems_m, after=yb)
    ge, _, _ = _split_call("ag_end_done", ge, wait=_ag_sibling_plan(end_keys), wait_sems=sems_e, after=yb)
    wa, wb, wo, w1, w2 = gm["wa"], gm["wb"], gm["wo"], gm["w1"], ge["w2"]
    br_a = _matmul(ya, wa, "nn", F32, 512, 1024, 1024, "branch_sc")
    br_b = _matmul(yb, wb, "nn", F32, 512, 1024, 2048, "branch_ssm")
    merged = _merge_fwd(proj, b_gate, br_a, br_b)
    x1 = _matmul(merged, wo, "nn", F32, 512, 1024, 1024, "out_proj", epi="res", extra=x0)
    h2 = _rms_fwd(x1, norm_mlp, "rms_mlp")
    a1, rl = _matmul(h2, w1, "nn", F32, 512, 1024, 1024, "mlp1", epi="relu2", n_outer=True)
    x2 = _matmul(rl, w2, "nn", F32, 512, 1024, 2048, "mlp2", epi="res", extra=x1)
    dx2, g_nf, loss8 = _final(x2, norm_final, tgt)

    da = _matmul(dx2, w2, "nt", BF16, 512, 1024, 1024, "mlp2_dx", epi="drelu", extra=a1, n_outer=True)
    g_w2 = _matmul(rl, dx2, "tn", BF16, 1024, 1024, 512, "mlp2_dw")
    g_w1 = _matmul(h2, da, "tn", BF16, 1024, 1024, 512, "mlp1_dw")
    dh2 = _matmul(da, w1, "nt", F32, 512, 1024, 2048, "mlp1_dx")
    dx1, g_nmlp = _rms_bwd(dh2, x1, norm_mlp, dx2, "rms_mlp_bwd")
    dmerged = _matmul(dx1, wo, "nt", F32, 512, 1024, 1024, "out_proj_dx")
    g_wo = _matmul(merged, dx1, "tn", BF16, 1024, 1024, 512, "out_proj_dw")
    dproj = lax.empty((L, NCW), BF16)
    dbr, dproj, g_bg = _merge_bwd(dmerged, proj, b_gate, br_a, br_b, dproj)
    dya = _matmul(dbr[0], wa, "nt", F32, 512, 1024, 1024, "branch_sc_dx")
    g_wa = _matmul(ya, dbr[0], "tn", BF16, 1024, 1024, 512, "branch_sc_dw")
    dproj, g_scw = _sc_bwd(dya, proj, sc_w_full, dproj)
    dyb = _matmul(dbr[1], wb, "nt", F32, 512, 1024, 1024, "branch_ssm_dx", n_outer=True)
    g_wb = _matmul(yb, dbr[1], "tn", BF16, 1024, 1024, 512, "branch_ssm_dw")
    rs_a = _ReduceScatter([1, 2, 3, 4, 5], {1: g_w1, 2: g_w2, 3: g_wa, 4: g_wb, 5: g_wo}, idx, "a")
    dy, dproj, g_snw = _gnorm_bwd(_tie(dyb, rs_a.token, "tie_dyb"), y, proj, ssm_norm_w, dproj)
    tok = rs_a.chips(after=dy)
    dxs, dbm, dcm, ddt_g, st = _ssd_bwd(xbc, dt4, cs4, sg4, vecg, s_all, _tie(dy, tok, "tie_dy"))
    dproj, gx1 = _ssm_conv_bwd(dxs, proj, cw4, dproj, 0, "ssm_conv_bwd_x")
    dproj, gx2 = _ssm_conv_bwd(dbm, proj, cw4, dproj, INNER, "ssm_conv_bwd_b")
    dproj, gx3 = _ssm_conv_bwd(dcm, proj, cw4, dproj, INNER + NG * NS, "ssm_conv_bwd_c")
    g_cw4 = jnp.concatenate([gx1, gx2, gx3], axis=1)
    ddt = jnp.pad(ddt_g[:, :, :4].transpose(1, 0, 2).reshape(L, NH), ((0, 0), (0, 128 - NH)))
    dproj, g_dtb = _dt_bwd(ddt, dproj)
    g_wc = _matmul(dproj, h, "tn", BF16, 1280, 1024, 512, "in_proj_dw")
    rs_b = _ReduceScatter([0], {0: g_wc}, idx, "b")
    tok = rs_a.share(after=rs_b.token)
    dh = _matmul(dproj, wc, "nn", F32, 512, 1024, 2304, "in_proj_dx", dep=tok)
    tok = rs_b.chips(after=dh)
    grad_x, g_nm = _rms_bwd(dh, x0, norm_mix, dx1, "rms_mix_bwd", dep=tok)

    small = {"norm_mix": g_nm[0], "b_gate": g_bg[0], "ssm_conv_b": g_cw4[4], "dt_bias": g_dtb[0, :NH],
             "A_log": st[:, 0, :4], "D_skip": st[:, 1, :4], "ssm_norm_w": g_snw[0], "norm_mlp": g_nmlp[0],
             "norm_final": g_nf[0], "sc_conv_w": g_scw[0:3], "ssm_conv_w": g_cw4[0:4], "loss": loss8[0, 0:1]}
    small_sum = _sum8(_small_all_gather(_pack(small, _SMALL, SMALL_ROWS)))
    gs = _unpack(small_sum, _SMALL)
    red = rs_a.result(after=small_sum)
    big = {"w_mlp1": red[1], "w_mlp2": red[2], "w_branch_sc": red[3], "w_branch_ssm": red[4], "w_out": red[5]}

    given = dict(norm_mix=norm_mix, w_in=w_in, b_gate=b_gate, sc_conv_w=sc_conv_w, ssm_conv_w=ssm_conv_w, ssm_conv_b=ssm_conv_b, dt_bias=dt_bias, A_log=A_log, D_skip=D_skip, ssm_norm_w=ssm_norm_w, w_branch_sc=w_branch_sc, w_branch_ssm=w_branch_ssm, w_out=w_out, norm_mlp=norm_mlp, w_mlp1=w_mlp1, w_mlp2=w_mlp2, norm_final=norm_final,
                 m_norm_mix=m_norm_mix, m_w_in=m_w_in, m_b_gate=m_b_gate, m_sc_conv_w=m_sc_conv_w, m_ssm_conv_w=m_ssm_conv_w, m_ssm_conv_b=m_ssm_conv_b, m_dt_bias=m_dt_bias, m_A_log=m_A_log, m_D_skip=m_D_skip, m_ssm_norm_w=m_ssm_norm_w, m_w_branch_sc=m_w_branch_sc, m_w_branch_ssm=m_w_branch_ssm, m_w_out=m_w_out, m_norm_mlp=m_norm_mlp, m_w_mlp1=m_w_mlp1, m_w_mlp2=m_w_mlp2, m_norm_final=m_norm_final,
                 v_norm_mix=v_norm_mix, v_w_in=v_w_in, v_b_gate=v_b_gate, v_sc_conv_w=v_sc_conv_w, v_ssm_conv_w=v_ssm_conv_w, v_ssm_conv_b=v_ssm_conv_b, v_dt_bias=v_dt_bias, v_A_log=v_A_log, v_D_skip=v_D_skip, v_ssm_norm_w=v_ssm_norm_w, v_w_branch_sc=v_w_branch_sc, v_w_branch_ssm=v_w_branch_ssm, v_w_out=v_w_out, v_norm_mlp=v_norm_mlp, v_w_mlp1=v_w_mlp1, v_w_mlp2=v_w_mlp2, v_norm_final=v_norm_final)
    order = ["norm_mix", "w_in", "b_gate", "sc_conv_w", "ssm_conv_w", "ssm_conv_b", "dt_bias", "A_log", "D_skip",
             "ssm_norm_w", "w_branch_sc", "w_branch_ssm", "w_out", "norm_mlp", "w_mlp1", "w_mlp2", "norm_final"]
    grad, delta, new_m, new_v = {}, {}, {}, {}
    for n in big:
        delta[n], new_m[n], new_v[n], grad[n] = _adamw(given[n], big[n], given["m_" + n], given["v_" + n],
                                                       "adamw_" + n, copy_g=True)
    tok = rs_b.share(after=new_v["w_mlp2"])
    gp = rs_b.result(after=tok)[0]
    gwt = _from_piece(gp, s)
    wt_args = (w_in.T, gwt, m_w_in.T, v_w_in.T)
    head = _adamw(*wt_args, "adamw_w_in", tr=256, nblk=W_SHARD // 256)
    dt_, mt_, vt_ = _adamw(*wt_args, "adamw_w_in_tail", tr=8, blk0=(W_SHARD // 256) * 32, nblk=1, into=head)
    grad["w_in"], delta["w_in"], new_m["w_in"], new_v["w_in"] = gwt.T, dt_.T, mt_.T, vt_.T
    big["w_in"] = None
    grad_small = {n: gs[n].reshape(given[n].shape) for n in order if n not in big and n not in ("sc_conv_w", "ssm_conv_w")}
    grad_small["sc_conv_w"] = lax.dynamic_slice(gs["sc_conv_w"].reshape(3, D), (0, 256 * s), (3, 256))
    grad_small["ssm_conv_w"] = lax.dynamic_slice(gs["ssm_conv_w"].reshape(4, XBC), (0, 1024 * s), (4, 1024))
    table = [(n, int(grad_small[n].size)) for n in grad_small]
    rows = 136
    pk = lambda d: _pack(d, table, rows)
    ds_, ms_, vs_ = _adamw(pk({n: given[n] for n in grad_small}), pk(grad_small), pk({n: given["m_" + n] for n in grad_small}),
                           pk({n: given["v_" + n] for n in grad_small}), "adamw_small", tr=rows)
    ds_, ms_, vs_ = _unpack(ds_, table), _unpack(ms_, table), _unpack(vs_, table)
    for n in grad_small:
        shp = given[n].shape
        grad[n] = grad_small[n]
        delta[n], new_m[n], new_v[n] = ds_[n].reshape(shp), ms_[n].reshape(shp), vs_[n].reshape(shp)

    loss = gs["loss"].reshape(())
    return (loss, grad_x.reshape(1, L, D), *[grad[n] for n in order], *[delta[n] for n in order],
            *[new_m[n] for n in order], *[new_v[n] for n in order])
```

```python
import functools

import jax
import jax.numpy as jnp
from jax import lax
from jax.experimental import pallas as pl
from jax.experimental.pallas import tpu as pltpu

F32 = jnp.float32
BF16 = jnp.bfloat16
MESH = pl.DeviceIdType.MESH
HBM = pltpu.HBM

D = 1024
INNER = 2048
HD = 64
NH = 32
NG = 8
NS = 128
Q = 128
XBC = 4096
DFF = 4096
EPS = 1e-6
W_SHARD = 2824
NCW = 11520
PIECE = 3072
PMAIN = 2816
C_Z, C_XBC, C_GATE, C_DT = 3072, 5120, 9216, 11264
SMALL_ROWS = 256
VMEM_LIMIT = 56 * 1024 * 1024

ADAM_LR, ADAM_B1, ADAM_B2, ADAM_EPS, ADAM_WD, ADAM_STEP = 0.001, 0.9, 0.999, 1e-08, 0.01, 10


def _cp(sem=None, vmem=VMEM_LIMIT):
    return pltpu.CompilerParams(dimension_semantics=sem, vmem_limit_bytes=vmem)


def _sigmoid(v):
    return 1.0 / (1.0 + jnp.exp(-v))


_DIMS = {"nn": (((1,), (0,)), ((), ())), "nt": (((1,), (1,)), ((), ())), "tn": (((0,), (0,)), ((), ()))}


def _matmul(a, b, mode, out_dtype, tm, tn, tk, name, epi=None, extra=None, n_outer=False, dep=None):
    if mode == "tn":
        K, M = a.shape
    else:
        M, K = a.shape
    N = b.shape[0] if mode == "nt" else b.shape[1]
    tm, tn, tk = min(tm, M), min(tn, N), min(tk, K)
    assert M % tm == 0 and N % tn == 0 and K % tk == 0, (name, M, N, K, tm, tn, tk)
    nm, nn, nk = M // tm, N // tn, K // tk
    dims = _DIMS[mode]

    def ij(p0, p1):
        return (p1, p0) if n_outer else (p0, p1)

    if mode == "tn":
        a_spec = pl.BlockSpec((tk, tm), lambda p0, p1, k: (k, ij(p0, p1)[0]))
    else:
        a_spec = pl.BlockSpec((tm, tk), lambda p0, p1, k: (ij(p0, p1)[0], k))
    if mode == "nt":
        b_spec = pl.BlockSpec((tn, tk), lambda p0, p1, k: (ij(p0, p1)[1], k))
    else:
        b_spec = pl.BlockSpec((tk, tn), lambda p0, p1, k: (k, ij(p0, p1)[1]))
    o_spec = pl.BlockSpec((tm, tn), lambda p0, p1, k: ij(p0, p1))
    in_specs = [a_spec, b_spec]
    args = [a, b]
    if epi in ("res", "drelu"):
        in_specs.append(o_spec)
        args.append(extra)
    if dep is not None:
        in_specs.append(pl.BlockSpec(memory_space=pl.ANY))
        args.append(dep)
    n_in = len(args)
    if epi == "relu2":
        out_shape = (jax.ShapeDtypeStruct((M, N), F32), jax.ShapeDtypeStruct((M, N), BF16))
        out_specs = (o_spec, o_spec)
    else:
        out_shape = jax.ShapeDtypeStruct((M, N), out_dtype)
        out_specs = o_spec

    def kern(*refs):
        a_ref, b_ref = refs[0], refs[1]
        e_ref = refs[2] if epi in ("res", "drelu") else None
        acc = refs[-1]
        outs = refs[n_in:-1]
        k = pl.program_id(2)

        @pl.when(k == 0)
        def _():
            acc[...] = jnp.zeros_like(acc)

        acc[...] += lax.dot_general(a_ref[...].astype(BF16), b_ref[...].astype(BF16), dims,
                                    preferred_element_type=F32)

        @pl.when(k == nk - 1)
        def _():
            r = acc[...]
            if epi is None:
                outs[0][...] = r.astype(out_dtype)
            elif epi == "res":
                outs[0][...] = (r + e_ref[...]).astype(out_dtype)
            elif epi == "relu2":
                outs[0][...] = r
                t = jnp.maximum(r, 0.0)
                outs[1][...] = (t * t).astype(BF16)
            else:
                outs[0][...] = (r * (2.0 * jnp.maximum(e_ref[...], 0.0))).astype(out_dtype)

    grid = (nn, nm, nk) if n_outer else (nm, nn, nk)
    return pl.pallas_call(
        kern, grid=grid, in_specs=in_specs, out_specs=out_specs, out_shape=out_shape,
        scratch_shapes=[pltpu.VMEM((tm, tn), F32)], name=name,
        compiler_params=_cp(("parallel", "parallel", "arbitrary")),
    )(*args)


def _rms_fwd(x, w, name, tl=256, dep=None):
    L = x.shape[0]

    def kern(x_ref, w_ref, *rest):
        o_ref = rest[-1]
        xv = x_ref[...]
        r = lax.rsqrt(jnp.mean(xv * xv, axis=-1, keepdims=True) + EPS)
        o_ref[...] = ((xv * r) * w_ref[...]).astype(BF16)

    row = pl.BlockSpec((tl, D), lambda i: (i, 0))
    deps = [] if dep is None else [dep]
    return pl.pallas_call(
        kern, grid=(L // tl,),
        in_specs=[row, pl.BlockSpec((1, D), lambda i: (0, 0))] + [pl.BlockSpec(memory_space=pl.ANY)] * len(deps),
        out_specs=row, out_shape=jax.ShapeDtypeStruct((L, D), BF16), name=name, compiler_params=_cp(("parallel",)),
    )(x, w.reshape(1, D), *deps)


def _rms_bwd(dy, x, w, res, name, tl=256, dep=None):
    L = x.shape[0]
    deps = [] if dep is None else [dep]

    def kern(dy_ref, x_ref, w_ref, res_ref, *rest):
        dx_ref, gw_ref = rest[-2:]
        @pl.when(pl.program_id(0) == 0)
        def _():
            gw_ref[...] = jnp.zeros_like(gw_ref)

        xv = x_ref[...]
        dyv = dy_ref[...]
        r = lax.rsqrt(jnp.mean(xv * xv, axis=-1, keepdims=True) + EPS)
        xn = xv * r
        gw_ref[...] += jnp.broadcast_to(jnp.sum(dyv * xn, axis=0, keepdims=True), (8, D))
        dxn = dyv * w_ref[...]
        dx_ref[...] = res_ref[...] + r * (dxn - xn * jnp.mean(dxn * xn, axis=-1, keepdims=True))

    row = pl.BlockSpec((tl, D), lambda i: (i, 0))
    return pl.pallas_call(
        kern, grid=(L // tl,),
        in_specs=[row, row, pl.BlockSpec((1, D), lambda i: (0, 0)), row] + [pl.BlockSpec(memory_space=pl.ANY)] * len(deps),
        out_specs=(row, pl.BlockSpec((8, D), lambda i: (0, 0))),
        out_shape=(jax.ShapeDtypeStruct((L, D), F32), jax.ShapeDtypeStruct((8, D), F32)),
        name=name, compiler_params=_cp(("arbitrary",)),
    )(dy, x, w.reshape(1, D), res, *deps)


def _final(x2, w, tgt, tl=256):
    L = x2.shape[0]

    def kern(x_ref, w_ref, t_ref, dx_ref, gw_ref, loss_ref):
        @pl.when(pl.program_id(0) == 0)
        def _():
            gw_ref[...] = jnp.zeros_like(gw_ref)
            loss_ref[...] = jnp.zeros_like(loss_ref)

        xv = x_ref[...]
        r = lax.rsqrt(jnp.mean(xv * xv, axis=-1, keepdims=True) + EPS)
        xn = xv * r
        e = xn * w_ref[...] - t_ref[...]
        per_tok = jnp.mean(e * e, axis=-1, keepdims=True)
        loss_ref[...] += 0.5 * jnp.sum(per_tok)
        dyv = e * (1.0 / D)
        gw_ref[...] += jnp.broadcast_to(jnp.sum(dyv * xn, axis=0, keepdims=True), (8, D))
        dxn = dyv * w_ref[...]
        dx_ref[...] = r * (dxn - xn * jnp.mean(dxn * xn, axis=-1, keepdims=True))

    row = pl.BlockSpec((tl, D), lambda i: (i, 0))
    return pl.pallas_call(
        kern, grid=(L // tl,), in_specs=[row, pl.BlockSpec((1, D), lambda i: (0, 0)), row],
        out_specs=(row, pl.BlockSpec((8, D), lambda i: (0, 0)), pl.BlockSpec((8, 128), lambda i: (0, 0))),
        out_shape=(jax.ShapeDtypeStruct((L, D), F32), jax.ShapeDtypeStruct((8, D), F32),
                   jax.ShapeDtypeStruct((8, 128), F32)),
        name="final_norm_loss", compiler_params=_cp(("arbitrary",)),
    )(x2, w.reshape(1, D), tgt)


def _down(v, k):
    if k == 0:
        return v
    t = lax.broadcasted_iota(jnp.int32, v.shape, 0)
    return jnp.where(t >= k, pltpu.roll(v, k, axis=0), 0.0)


def _up(v, k):
    if k == 0:
        return v
    n = v.shape[0]
    t = lax.broadcasted_iota(jnp.int32, v.shape, 0)
    return jnp.where(t < n - k, pltpu.roll(v, n - k, axis=0), 0.0)


TW = 256


def _sc_fwd(proj, cw):
    L = proj.shape[0]
    nb = D // TW

    def kern(b_ref, c_ref, x_ref, w_ref, o_ref):
        u = c_ref[...] * x_ref[...]
        w = w_ref[...]
        cv = w[0:1] * _down(u, 2) + w[1:2] * _down(u, 1) + w[2:3] * u
        o_ref[...] = (b_ref[...] * cv).astype(BF16)

    col = lambda off: pl.BlockSpec((L, TW), lambda j: (0, off + j))
    return pl.pallas_call(
        kern, grid=(nb,), in_specs=[col(0), col(nb), col(2 * nb), pl.BlockSpec((8, TW), lambda j: (0, j))],
        out_specs=pl.BlockSpec((L, TW), lambda j: (0, j)), out_shape=jax.ShapeDtypeStruct((L, D), BF16),
        name="sc_fwd", compiler_params=_cp(("parallel",)),
    )(proj, proj, proj, cw)


def _sc_bwd(dya, proj, cw, dproj):
    L = proj.shape[0]
    nb = D // TW

    def kern(d_ref, b_ref, c_ref, x_ref, w_ref, _, dp_ref, gw_ref):
        sec = pl.program_id(1)
        cs, xs, dyv = c_ref[...], x_ref[...], d_ref[...]
        w = w_ref[...]
        u = cs * xs
        u1, u2 = _down(u, 1), _down(u, 2)
        cv = w[0:1] * u2 + w[1:2] * u1 + w[2:3] * u
        dcv = dyv * b_ref[...]
        du = w[2:3] * dcv + w[1:2] * _up(dcv, 1) + w[0:1] * _up(dcv, 2)
        g0 = jnp.sum(dcv * u2, axis=0, keepdims=True)
        g1 = jnp.sum(dcv * u1, axis=0, keepdims=True)
        g2 = jnp.sum(dcv * u, axis=0, keepdims=True)
        row = lax.broadcasted_iota(jnp.int32, (8, TW), 0)
        gw_ref[...] = jnp.where(row == 0, g0, jnp.where(row == 1, g1, jnp.where(row == 2, g2, 0.0)))
        out = jnp.where(sec == 0, dyv * cv, jnp.where(sec == 1, du * xs, du * cs))
        dp_ref[...] = out.astype(BF16)

    col = lambda off: pl.BlockSpec((L, TW), lambda j, s: (0, off + j))
    return pl.pallas_call(
        kern, grid=(nb, 3),
        in_specs=[col(0), col(0), col(nb), col(2 * nb), pl.BlockSpec((8, TW), lambda j, s: (0, j)),
                  pl.BlockSpec(memory_space=pl.ANY)],
        out_specs=(pl.BlockSpec((L, TW), lambda j, s: (0, s * nb + j)), pl.BlockSpec((8, TW), lambda j, s: (0, j))),
        out_shape=(jax.ShapeDtypeStruct(dproj.shape, BF16), jax.ShapeDtypeStruct((8, D), F32)),
        input_output_aliases={5: 0}, name="sc_bwd", compiler_params=_cp(("parallel", "arbitrary")),
    )(dya, proj, proj, proj, cw, dproj)


def _ssm_conv_fwd(proj, cw4):
    L = proj.shape[0]
    off = C_XBC // TW

    def kern(r_ref, w_ref, o_ref):
        raw = r_ref[...]
        w = w_ref[...]
        c4 = w[0:1] * _down(raw, 3) + w[1:2] * _down(raw, 2) + w[2:3] * _down(raw, 1) + w[3:4] * raw + w[4:5]
        o_ref[...] = c4 * _sigmoid(c4)

    return pl.pallas_call(
        kern, grid=(XBC // TW,),
        in_specs=[pl.BlockSpec((L, TW), lambda j: (0, off + j)), pl.BlockSpec((8, TW), lambda j: (0, j))],
        out_specs=pl.BlockSpec((L, TW), lambda j: (0, j)), out_shape=jax.ShapeDtypeStruct((L, XBC), F32),
        name="ssm_conv_fwd", compiler_params=_cp(("parallel",)),
    )(proj, cw4)


def _ssm_conv_bwd(dx, proj, cw4, dproj, col0, name):
    L, width = dx.shape
    off_p = (C_XBC + col0) // TW
    off_w = col0 // TW

    def kern(d_ref, r_ref, w_ref, _, dp_ref, gw_ref):
        raw = r_ref[...]
        w = w_ref[...]
        r1, r2, r3 = _down(raw, 1), _down(raw, 2), _down(raw, 3)
        c4 = w[0:1] * r3 + w[1:2] * r2 + w[2:3] * r1 + w[3:4] * raw + w[4:5]
        sg = _sigmoid(c4)
        dc4 = d_ref[...] * (sg * (1.0 + c4 * (1.0 - sg)))
        draw = w[3:4] * dc4 + w[2:3] * _up(dc4, 1) + w[1:2] * _up(dc4, 2) + w[0:1] * _up(dc4, 3)
        dp_ref[...] = draw.astype(BF16)
        gs = [jnp.sum(dc4 * r3, axis=0, keepdims=True), jnp.sum(dc4 * r2, axis=0, keepdims=True),
              jnp.sum(dc4 * r1, axis=0, keepdims=True), jnp.sum(dc4 * raw, axis=0, keepdims=True),
              jnp.sum(dc4, axis=0, keepdims=True)]
        row = lax.broadcasted_iota(jnp.int32, (8, TW), 0)
        acc = jnp.zeros((8, TW), F32)
        for k, gk in enumerate(gs):
            acc = jnp.where(row == k, gk, acc)
        gw_ref[...] = acc

    return pl.pallas_call(
        kern, grid=(width // TW,),
        in_specs=[pl.BlockSpec((L, TW), lambda j: (0, j)), pl.BlockSpec((L, TW), lambda j: (0, off_p + j)),
                  pl.BlockSpec((8, TW), lambda j: (0, off_w + j)), pl.BlockSpec(memory_space=pl.ANY)],
        out_specs=(pl.BlockSpec((L, TW), lambda j: (0, off_p + j)), pl.BlockSpec((8, TW), lambda j: (0, j))),
        out_shape=(jax.ShapeDtypeStruct(dproj.shape, BF16), jax.ShapeDtypeStruct((8, width), F32)),
        input_output_aliases={3: 0}, name=name, compiler_params=_cp(("arbitrary",)),
    )(dx, proj, cw4, dproj)


def _split3(v):
    h1 = v.astype(BF16)
    r1 = v - h1.astype(F32)
    h2 = r1.astype(BF16)
    h3 = (r1 - h2.astype(F32)).astype(BF16)
    return h1, h2, h3


def _dot01(m01, v, dims=_DIMS["nn"], m_left=True, terms=3):
    out = None
    for part in _split3(v)[:terms]:
        ops = (m01, part) if m_left else (part, m01)
        t = lax.dot_general(ops[0], ops[1], dims, preferred_element_type=F32)
        out = t if out is None else out + t
    return out


def _bdot(a, b, mode="nn"):
    return lax.dot_general(a.astype(BF16), b.astype(BF16), _DIMS[mode], preferred_element_type=F32)


def _softplus(v):
    return jnp.maximum(v, 0.0) + jnp.log1p(jnp.exp(-jnp.abs(v)))


def _dt_prep(proj, vec):
    L = proj.shape[0]

    def kern(p_ref, v_ref, dt_ref, cs_ref, sg_ref):
        v = v_ref[...]
        pre = p_ref[:, 0:128] + v[0:1]
        dt = _softplus(pre)
        da = dt * (-jnp.exp(v[1:2]))
        ii = lax.broadcasted_iota(jnp.int32, (Q, Q), 0)
        jj = lax.broadcasted_iota(jnp.int32, (Q, Q), 1)
        ltri = (jj <= ii).astype(BF16)
        lane = lax.broadcasted_iota(jnp.int32, (Q, 128), 1)
        for val, ref in ((dt, dt_ref), (_dot01(ltri, da), cs_ref), (_sigmoid(pre), sg_ref)):
            for g in range(NG):
                moved = val if g == 0 else pltpu.roll(val, 128 - 4 * g, axis=1)
                ref[g] = jnp.where(lane < 4, moved, 0.0)

    blk = pl.BlockSpec((NG, Q, 128), lambda c: (0, c, 0))
    return pl.pallas_call(
        kern, grid=(L // Q,),
        in_specs=[pl.BlockSpec((Q, 256), lambda c: (c, C_DT // 256)), pl.BlockSpec((8, 128), lambda c: (0, 0))],
        out_specs=(blk, blk, blk),
        out_shape=(jax.ShapeDtypeStruct((NG, L, 128), F32),) * 3,
        name="dt_prep", compiler_params=_cp(("parallel",)),
    )(proj, vec)


def _head_masks():
    lane = lax.broadcasted_iota(jnp.int32, (1, 4 * HD), 1)
    return [((lane >= HD * j) & (lane < HD * (j + 1))) for j in range(4)]


def _expand4(v4, masks):
    R = v4.shape[0]
    out = jnp.zeros((R, 4 * HD), F32)
    for j in range(4):
        out = jnp.where(masks[j], jnp.broadcast_to(v4[:, j:j + 1], (R, 4 * HD)), out)
    return out


def _decay_matrix(cs_col, tri):
    colb = jnp.broadcast_to(cs_col, (Q, Q))
    return jnp.exp(jnp.where(tri, colb - colb.T, -jnp.inf))


def _ssd_fwd(xbc, dt4, cs4, vecg):
    L = xbc.shape[0]
    nc = L // Q

    def kern(x_ref, b_ref, c_ref, dt_ref, cs_ref, v_ref, y_ref, s_ref, S):
        c = pl.program_id(1)

        @pl.when(c == 0)
        def _():
            S[...] = jnp.zeros_like(S)

        masks = _head_masks()
        ii = lax.broadcasted_iota(jnp.int32, (Q, Q), 0)
        jj = lax.broadcasted_iota(jnp.int32, (Q, Q), 1)
        tri = jj <= ii
        dt4v, cs4v = dt_ref[0], cs_ref[0]
        dt_b, cs_b = _expand4(dt4v, masks), _expand4(cs4v, masks)
        d_b = _expand4(v_ref[0], masks)[1:2]
        cs_last = cs_b[Q - 1:Q, :]
        x4, bm, cm = x_ref[...], b_ref[...], c_ref[...]
        xdt = x4 * dt_b
        gm = _bdot(cm, bm, "nt")
        s4 = S[...]
        s_ref[0, 0] = s4
        y = _bdot(cm, s4) * jnp.exp(cs_b) + d_b * x4
        for j in range(4):
            mh = gm * _decay_matrix(cs4v[:, j:j + 1], tri)
            y = y + _bdot(mh, jnp.where(masks[j], xdt, 0.0))
        y_ref[...] = y
        S[...] = jnp.exp(cs_last) * s4 + _bdot(bm, xdt * jnp.exp(cs_last - cs_b), "tn")

    sc = pl.BlockSpec((1, Q, 128), lambda g, c: (g, c, 0))
    return pl.pallas_call(
        kern, grid=(NG, nc),
        in_specs=[pl.BlockSpec((Q, 256), lambda g, c: (c, g)),
                  pl.BlockSpec((Q, 128), lambda g, c: (c, INNER // 128 + g)),
                  pl.BlockSpec((Q, 128), lambda g, c: (c, (INNER + NG * NS) // 128 + g)),
                  sc, sc, pl.BlockSpec((1, 8, 128), lambda g, c: (g, 0, 0))],
        out_specs=(pl.BlockSpec((Q, 256), lambda g, c: (c, g)),
                   pl.BlockSpec((1, 1, NS, 256), lambda g, c: (g, c, 0, 0))),
        out_shape=(jax.ShapeDtypeStruct((L, INNER), F32), jax.ShapeDtypeStruct((NG, nc, NS, 256), F32)),
        scratch_shapes=[pltpu.VMEM((NS, 256), F32)], name="ssd_fwd",
        compiler_params=_cp(("parallel", "arbitrary")),
    )(xbc, xbc, xbc, dt4, cs4, vecg)


def _ssd_bwd(xbc, dt4, cs4, sg4, vecg, s_all, dy):
    L = xbc.shape[0]
    nc = L // Q

    def kern(x_ref, b_ref, c_ref, dt_ref, cs_ref, sg_ref, v_ref, s_ref, dy_ref,
             dx_ref, db_ref, dc_ref, ddt_ref, st_ref, dS):
        cc = pl.program_id(1)

        @pl.when(cc == 0)
        def _():
            dS[...] = jnp.zeros_like(dS)
            st_ref[...] = jnp.zeros_like(st_ref)

        masks = _head_masks()
        ii = lax.broadcasted_iota(jnp.int32, (Q, Q), 0)
        jj = lax.broadcasted_iota(jnp.int32, (Q, Q), 1)
        tri = jj <= ii
        utri = (jj >= ii).astype(BF16)
        li = lax.broadcasted_iota(jnp.int32, (4 * HD, 4 * HD), 0)
        lj = lax.broadcasted_iota(jnp.int32, (4 * HD, 4 * HD), 1)
        eblk = ((li // HD) == (lj // HD)).astype(BF16)
        lane128 = lax.broadcasted_iota(jnp.int32, (Q, 128), 1)

        dt4v, cs4v, sg4v = dt_ref[0], cs_ref[0], sg_ref[0]
        dt_b, cs_b = _expand4(dt4v, masks), _expand4(cs4v, masks)
        vv = _expand4(v_ref[0], masks)
        a_b = -jnp.exp(vv[0:1])
        d_b = vv[1:2]
        a4 = -jnp.exp(v_ref[0][0:1, :])
        cs_last = cs_b[Q - 1:Q, :]
        ecs = jnp.exp(cs_b)
        decay = jnp.exp(cs_last - cs_b)
        elast = jnp.exp(cs_last)
        x4, bm, cm, dyv = x_ref[...], b_ref[...], c_ref[...], dy_ref[...]
        s4 = s_ref[0, 0]
        dsn = dS[...]
        xdt = x4 * dt_b
        gm = _bdot(cm, bm, "nt")
        gmt = gm.T
        dye = dyv * ecs
        yoff = ecs * _bdot(cm, s4)
        t4 = _bdot(bm, dsn) * decay
        dxdt = t4
        dg = jnp.zeros((Q, Q), F32)
        rc = jnp.zeros((Q, 4 * HD), F32)
        for j in range(4):
            colb = jnp.broadcast_to(cs4v[:, j:j + 1], (Q, Q))
            seg = colb - colb.T
            lm = jnp.exp(jnp.where(tri, seg, -jnp.inf))
            lmt = jnp.exp(jnp.where(jj >= ii, -seg, -jnp.inf))
            mh = gm * lm
            mht = gmt * lmt
            dyh = jnp.where(masks[j], dyv, 0.0).astype(BF16)
            xh = jnp.where(masks[j], xdt, 0.0).astype(BF16)
            dxdt = dxdt + _bdot(mh, dyh, "tn")
            dmh = _bdot(dyh, xh, "nt")
            dmht = _bdot(xh, dyh, "nt")
            dg = dg + dmh * lm
            rs = jnp.sum(dmh * mh, axis=1, keepdims=True) - jnp.sum(dmht * mht, axis=1, keepdims=True)
            rc = jnp.where(masks[j], jnp.broadcast_to(rs, (Q, 4 * HD)), rc)
        xt = xdt * t4
        tail = jnp.sum(xt, axis=0, keepdims=True) + elast * jnp.sum(s4 * dsn, axis=0, keepdims=True)
        gd_raw = jnp.sum(dyv * x4, axis=0, keepdims=True)
        stacked = jnp.concatenate([dyv * yoff - xt, dxdt * x4, jnp.broadcast_to(tail, (8, 4 * HD)),
                                   jnp.broadcast_to(gd_raw, (8, 4 * HD))], axis=0)
        seg = _dot01(eblk, stacked, m_left=False, terms=2)
        da_b = seg[0:Q] + rc
        dda_b = _dot01(utri, da_b, terms=2) + seg[2 * Q:2 * Q + 1]
        ddt_b = dda_b * a_b + seg[Q:2 * Q]
        gd_b = seg[2 * Q + 8:2 * Q + 9]
        ddt4 = jnp.zeros((Q, 128), F32)
        dda4 = jnp.zeros((Q, 128), F32)
        for j in range(4):
            ddt4 = jnp.where(lane128 == j, jnp.broadcast_to(ddt_b[:, HD * j:HD * j + 1], (Q, 128)), ddt4)
            dda4 = jnp.where(lane128 == j, jnp.broadcast_to(dda_b[:, HD * j:HD * j + 1], (Q, 128)), dda4)
        ddt_ref[0] = ddt4 * sg4v
        ga = jnp.sum(dda4 * dt4v * a4, axis=0, keepdims=True)
        gd = jnp.zeros((1, 128), F32)
        for j in range(4):
            gd = jnp.where(lane128[0:1] == j, jnp.broadcast_to(gd_b[:, HD * j:HD * j + 1], (1, 128)), gd)
        row = lax.broadcasted_iota(jnp.int32, (8, 128), 0)
        st_ref[0] += jnp.where(row == 0, ga, jnp.where(row == 1, gd, 0.0))
        dx_ref[...] = d_b * dyv + dxdt * dt_b
        dc_ref[...] = _bdot(dg, bm) + _bdot(dye, s4, "nt")
        db_ref[...] = _bdot(dg, cm, "tn") + _bdot(xdt * decay, dsn, "nt")
        dS[...] = elast * dsn + _bdot(cm, dye, "tn")

    rv = lambda c: nc - 1 - c
    sc = pl.BlockSpec((1, Q, 128), lambda g, c: (g, rv(c), 0))
    return pl.pallas_call(
        kern, grid=(NG, nc),
        in_specs=[pl.BlockSpec((Q, 256), lambda g, c: (rv(c), g)),
                  pl.BlockSpec((Q, 128), lambda g, c: (rv(c), INNER // 128 + g)),
                  pl.BlockSpec((Q, 128), lambda g, c: (rv(c), (INNER + NG * NS) // 128 + g)),
                  sc, sc, sc, pl.BlockSpec((1, 8, 128), lambda g, c: (g, 0, 0)),
                  pl.BlockSpec((1, 1, NS, 256), lambda g, c: (g, rv(c), 0, 0)),
                  pl.BlockSpec((Q, 256), lambda g, c: (rv(c), g))],
        out_specs=(pl.BlockSpec((Q, 256), lambda g, c: (rv(c), g)),
                   pl.BlockSpec((Q, 128), lambda g, c: (rv(c), g)),
                   pl.BlockSpec((Q, 128), lambda g, c: (rv(c), g)),
                   pl.BlockSpec((1, Q, 128), lambda g, c: (g, rv(c), 0)),
                   pl.BlockSpec((1, 8, 128), lambda g, c: (g, 0, 0))),
        out_shape=(jax.ShapeDtypeStruct((L, INNER), F32), jax.ShapeDtypeStruct((L, NG * NS), F32),
                   jax.ShapeDtypeStruct((L, NG * NS), F32), jax.ShapeDtypeStruct((NG, L, 128), F32),
                   jax.ShapeDtypeStruct((NG, 8, 128), F32)),
        scratch_shapes=[pltpu.VMEM((NS, 256), F32)], name="ssd_bwd",
        compiler_params=_cp(("parallel", "arbitrary")),
    )(xbc, xbc, xbc, dt4, cs4, sg4, vecg, s_all, dy)


def _dt_bwd(ddt, dproj, tl=256):
    L = ddt.shape[1]

    def kern(d_ref, _, dp_ref, gs_ref):
        @pl.when(pl.program_id(0) == 0)
        def _():
            gs_ref[...] = jnp.zeros_like(gs_ref)

        d = d_ref[0]
        for g in range(1, NG):
            d = d + pltpu.roll(d_ref[g], 4 * g, axis=1)
        gs_ref[...] += jnp.broadcast_to(jnp.sum(d, axis=0, keepdims=True), (8, 128))
        dp_ref[...] = jnp.concatenate([d, jnp.zeros_like(d)], axis=1).astype(BF16)

    return pl.pallas_call(
        kern, grid=(L // tl,),
        in_specs=[pl.BlockSpec((NG, tl, 128), lambda i: (0, i, 0)), pl.BlockSpec(memory_space=pl.ANY)],
        out_specs=(pl.BlockSpec((tl, 256), lambda i: (i, C_DT // 256)), pl.BlockSpec((8, 128), lambda i: (0, 0))),
        out_shape=(jax.ShapeDtypeStruct(dproj.shape, BF16), jax.ShapeDtypeStruct((8, 128), F32)),
        input_output_aliases={1: 0}, name="dt_bwd", compiler_params=_cp(("arbitrary",)),
    )(ddt, dproj)


GW = INNER // NG


def _gnorm_fwd(y, proj, w, tl=256):
    L = y.shape[0]
    zoff = C_Z // 1024

    def kern(y_ref, z_ref, w_ref, o_ref):
        z = z_ref[...]
        yz = y_ref[...] * (z * _sigmoid(z))
        wv = w_ref[...]
        for k in range(1024 // GW):
            sl = slice(GW * k, GW * (k + 1))
            v = yz[:, sl]
            rg = lax.rsqrt(jnp.mean(v * v, axis=-1, keepdims=True) + EPS)
            o_ref[:, sl] = ((v * rg) * wv[:, sl]).astype(BF16)

    blk = pl.BlockSpec((tl, 1024), lambda i, j: (i, j))
    return pl.pallas_call(
        kern, grid=(L // tl, 2),
        in_specs=[blk, pl.BlockSpec((tl, 1024), lambda i, j: (i, zoff + j)), pl.BlockSpec((1, 1024), lambda i, j: (0, j))],
        out_specs=blk, out_shape=jax.ShapeDtypeStruct((L, INNER), BF16), name="gnorm_fwd",
        compiler_params=_cp(("parallel", "parallel")),
    )(y, proj, w.reshape(1, INNER))


def _gnorm_bwd(dyb, y, proj, w, dproj, tl=256):
    L = y.shape[0]
    zoff = C_Z // 1024

    def kern(d_ref, y_ref, z_ref, w_ref, _, dy_ref, dp_ref, gw_ref):
        @pl.when(pl.program_id(1) == 0)
        def _():
            gw_ref[...] = jnp.zeros_like(gw_ref)

        z = z_ref[...]
        sg = _sigmoid(z)
        sz = z * sg
        yv = y_ref[...]
        yz = yv * sz
        dv = d_ref[...]
        wv = w_ref[...]
        for k in range(1024 // GW):
            sl = slice(GW * k, GW * (k + 1))
            v = yz[:, sl]
            rg = lax.rsqrt(jnp.mean(v * v, axis=-1, keepdims=True) + EPS)
            vn = v * rg
            dk = dv[:, sl]
            gw_ref[:, sl] += jnp.broadcast_to(jnp.sum(dk * vn, axis=0, keepdims=True), (8, GW))
            dvn = dk * wv[:, sl]
            dyz = rg * (dvn - vn * jnp.mean(dvn * vn, axis=-1, keepdims=True))
            dy_ref[:, sl] = dyz * sz[:, sl]
            dp_ref[:, sl] = (dyz * yv[:, sl] * (sg[:, sl] * (1.0 + z[:, sl] * (1.0 - sg[:, sl])))).astype(BF16)

    blk = pl.BlockSpec((tl, 1024), lambda j, i: (i, j))
    zblk = pl.BlockSpec((tl, 1024), lambda j, i: (i, zoff + j))
    return pl.pallas_call(
        kern, grid=(2, L // tl),
        in_specs=[blk, blk, zblk, pl.BlockSpec((1, 1024), lambda j, i: (0, j)), pl.BlockSpec(memory_space=pl.ANY)],
        out_specs=(blk, zblk, pl.BlockSpec((8, 1024), lambda j, i: (0, j))),
        out_shape=(jax.ShapeDtypeStruct((L, INNER), F32), jax.ShapeDtypeStruct(dproj.shape, BF16),
                   jax.ShapeDtypeStruct((8, INNER), F32)),
        input_output_aliases={4: 1}, name="gnorm_bwd", compiler_params=_cp(("parallel", "arbitrary")),
    )(dyb, y, proj, w.reshape(1, INNER), dproj)


def _merge_fwd(proj, bg, br_a, br_b, tl=256):
    L = proj.shape[0]
    goff = C_GATE // 1024

    def kern(g1_ref, g2_ref, b1_ref, b2_ref, a_ref, b_ref, o_ref):
        g1 = _sigmoid(g1_ref[...] + b1_ref[...])
        g2 = _sigmoid(g2_ref[...] + b2_ref[...])
        o_ref[...] = (g1 * a_ref[...] + g2 * b_ref[...]).astype(BF16)

    row = pl.BlockSpec((tl, 1024), lambda i: (i, 0))
    bg2 = bg.reshape(1, 2 * D)
    return pl.pallas_call(
        kern, grid=(L // tl,),
        in_specs=[pl.BlockSpec((tl, 1024), lambda i: (i, goff)), pl.BlockSpec((tl, 1024), lambda i: (i, goff + 1)),
                  pl.BlockSpec((1, 1024), lambda i: (0, 0)), pl.BlockSpec((1, 1024), lambda i: (0, 1)), row, row],
        out_specs=row, out_shape=jax.ShapeDtypeStruct((L, D), BF16), name="merge_fwd",
        compiler_params=_cp(("parallel",)),
    )(proj, proj, bg2, bg2, br_a, br_b)


def _merge_bwd(dm, proj, bg, br_a, br_b, dproj, tl=256):
    L = proj.shape[0]
    goff = C_GATE // 1024

    def kern(dm_ref, g_ref, b_ref, a_ref, bb_ref, _, dbr_ref, dp_ref, gb_ref):
        j = pl.program_id(0)

        @pl.when(pl.program_id(1) == 0)
        def _():
            gb_ref[...] = jnp.zeros_like(gb_ref)

        g = _sigmoid(g_ref[...] + b_ref[...])
        br = jnp.where(j == 0, a_ref[...], bb_ref[...])
        dmv = dm_ref[...]
        dbr_ref[0] = (dmv * g).astype(BF16)
        dgate = dmv * br * g * (1.0 - g)
        gb_ref[...] += jnp.broadcast_to(jnp.sum(dgate, axis=0, keepdims=True), (8, 1024))
        dp_ref[...] = dgate.astype(BF16)

    row = pl.BlockSpec((tl, 1024), lambda j, i: (i, 0))
    gblk = pl.BlockSpec((tl, 1024), lambda j, i: (i, goff + j))
    return pl.pallas_call(
        kern, grid=(2, L // tl),
        in_specs=[row, gblk, pl.BlockSpec((1, 1024), lambda j, i: (0, j)), row, row, pl.BlockSpec(memory_space=pl.ANY)],
        out_specs=(pl.BlockSpec((1, tl, 1024), lambda j, i: (j, i, 0)), gblk, pl.BlockSpec((8, 1024), lambda j, i: (0, j))),
        out_shape=(jax.ShapeDtypeStruct((2, L, D), BF16), jax.ShapeDtypeStruct(dproj.shape, BF16),
                   jax.ShapeDtypeStruct((8, 2 * D), F32)),
        input_output_aliases={5: 1}, name="merge_bwd", compiler_params=_cp(("parallel", "arbitrary")),
    )(dm, proj, bg.reshape(1, 2 * D), br_a, br_b, dproj)


def _coords():
    return lax.axis_index("x"), lax.axis_index("y"), lax.axis_index("c")


def _other_chips(sk):
    xk, yk = sk // 2, sk % 2
    return [((1 - xk, yk), 2 * (1 - xk) + yk), ((xk, 1 - yk), 2 * xk + 1 - yk), ((1 - xk, 1 - yk), 2 * (1 - xk) + 1 - yk)]


def _rows(start, size):
    assert size % 128 == 0
    return pl.ds(pl.multiple_of(start, 128), size)


def _per_chip(fn):
    x, y, _ = _coords()
    s = 2 * x + y
    for sk in range(4):
        pl.when(s == sk)(functools.partial(fn, sk))


XTRA = PIECE - PMAIN


def _place(shard, full_shape, block, index_map, idx, name, blk0=0, nblk=None, dep=None):
    in_block = block[-2:]
    if nblk is None:
        nblk = shard.shape[0] // in_block[0]

    def kern(idx_ref, s_ref, *rest):
        o_ref = rest[-1]
        o_ref[...] = s_ref[...].astype(BF16).reshape(o_ref.shape)

    grid_spec = pltpu.PrefetchScalarGridSpec(
        num_scalar_prefetch=1, grid=(nblk,),
        in_specs=[pl.BlockSpec(in_block, lambda i, idx_ref: (blk0 + i, 0))] + ([_ANY] if dep is not None else []),
        out_specs=pl.BlockSpec(block, index_map))
    args = (idx, shard) + ((dep,) if dep is not None else ())
    return pl.pallas_call(kern, grid_spec=grid_spec, out_shape=jax.ShapeDtypeStruct(full_shape, BF16), name=name,
                          compiler_params=_cp(("arbitrary",)))(*args)


_SEM = pl.BlockSpec(memory_space=pltpu.SEMAPHORE)
_EFFECT = pltpu.SideEffectType.DATAFLOW_SIDE_EFFECTING


_ANY = pl.BlockSpec(memory_space=pl.ANY)


def _tie(v, dep, name):
    def body(v_ref, dep_ref, o_ref):
        del v_ref, dep_ref, o_ref

    return pl.pallas_call(body, out_shape=jax.ShapeDtypeStruct(v.shape, v.dtype), in_specs=[_ANY, _ANY],
                          out_specs=_ANY, input_output_aliases={0: 0}, name=name)(v, dep)


def _split_call(name, arrays, start=None, wait=None, wait_sems=None, after=None):
    keys = list(arrays)
    n = len(keys)
    n_start = start.n if start is not None else 0

    def body(*refs):
        pos = n
        if wait is not None:
            wss, wrs = refs[pos], refs[pos + 1]
            pos += 2
        if after is not None:
            pos += 1
        if start is not None:
            nss, nrs = refs[pos], refs[pos + 1]
            pos += 2
        R = dict(zip(keys, refs[pos:pos + n]))
        token = refs[pos + n]
        x, y, c = _coords()

        def desc(src, dst, dev, ss, rs, k):
            return pltpu.make_async_remote_copy(src_ref=src, dst_ref=dst, send_sem=ss.at[k], recv_sem=rs.at[k],
                                                device_id=dev, device_id_type=MESH)

        def run(sk):
            if wait is not None:
                for k, (snd, land) in enumerate(wait.copies(sk, R)):
                    if snd is not None:
                        desc(snd[0], snd[1], snd[2], wss, wrs, k).wait_send()
                    if land is not None:
                        desc(land, land, (x, y, c), wss, wrs, k).wait_recv()
            if start is not None:
                for k, (snd, land) in enumerate(start.copies(sk, R)):
                    if snd is not None:
                        desc(snd[0], snd[1], snd[2], nss, nrs, k).start()

        _per_chip(run)
        token[...] = jnp.zeros_like(token)

    hbm = pl.BlockSpec(memory_space=HBM)
    vals = [arrays[k] for k in keys]
    ins, in_specs = list(vals), [hbm] * n
    if wait is not None:
        ins += list(wait_sems)
        in_specs += [_SEM, _SEM]
    if after is not None:
        ins.append(after)
        in_specs.append(pl.BlockSpec(memory_space=pl.ANY))
    out_shape, out_specs = [], []
    if start is not None:
        out_shape += [pltpu.SemaphoreType.DMA((n_start,)), pltpu.SemaphoreType.DMA((n_start,))]
        out_specs += [_SEM, _SEM]
    first = len(out_shape)
    out_shape += [jax.ShapeDtypeStruct(v.shape, v.dtype) for v in vals] + [jax.ShapeDtypeStruct((8, 128), F32)]
    out_specs += [hbm] * n + [pl.BlockSpec(memory_space=pltpu.VMEM)]
    res = pl.pallas_call(
        body, out_shape=tuple(out_shape), in_specs=in_specs, out_specs=tuple(out_specs),
        input_output_aliases={i: first + i for i in range(n)}, name=name,
        compiler_params=pltpu.CompilerParams(has_side_effects=_EFFECT),
    )(*ins)
    sems = (res[0], res[1]) if start is not None else None
    return dict(zip(keys, res[first:first + n])), sems, res[-1]


class _Plan:
    def __init__(self, n, copies):
        self.n, self.copies = n, copies


_HM, _HX = PMAIN // 2, XTRA // 2
_WIN = {
    "wct": (True, lambda r, sc, hc: r.at[_rows(PMAIN * sc + _HM * hc, _HM), :]),
    "xt": (True, lambda r, sc, hc: r.at[sc, _rows(_HX * hc, _HX), :]),
    "w1": (True, lambda r, sc, hc: r.at[_rows(512 * hc, 512), pl.ds(1024 * sc, 1024)]),
    "w2": (True, lambda r, sc, hc: r.at[_rows(1024 * sc + 512 * hc, 512), :]),
    "wa": (True, lambda r, sc, hc: r.at[_rows(256 * sc + 128 * hc, 128), :]),
    "wb": (True, lambda r, sc, hc: r.at[_rows(512 * sc + 256 * hc, 256), :]),
    "wo": (True, lambda r, sc, hc: r.at[_rows(256 * sc + 128 * hc, 128), :]),
    "cw": (False, lambda r, sc, hc: r.at[sc]),
}


def _ag_chips_plan(keys):
    def copies(sk, R):
        _, _, c = _coords()
        out = []
        for key in keys:
            win = _WIN[key][1]
            for (px, py), ps in _other_chips(sk):
                w = win(R[key], sk, c)
                out.append(((w, w, (px, py, c)), win(R[key], ps, c)))
        return out
    return _Plan(3 * len(keys), copies)


def _ag_sibling_plan(keys):
    keys = [k for k in keys if _WIN[k][0]]

    def copies(sk, R):
        x, y, c = _coords()
        out = []
        for key in keys:
            win = _WIN[key][1]
            for _, ps in _other_chips(sk):
                w = win(R[key], ps, c)
                out.append(((w, w, (x, y, 1 - c)), win(R[key], ps, 1 - c)))
        return out
    return _Plan(3 * len(keys), copies)


def _fix_wct(wct, xt):
    nb = PMAIN // XTRA

    def kern(w_ref, x_ref, o_ref):
        k = pl.program_id(0)
        xv = x_ref[0]
        o_ref[...] = jnp.where(k < 3, (w_ref[...].astype(F32) + xv.astype(F32)).astype(BF16), xv)

    blk = pl.BlockSpec((XTRA, D), lambda k: (nb * (k + 1), 0))
    rblk = pl.BlockSpec((XTRA, D), lambda k: (jnp.where(k < 3, nb * (k + 1), 0), 0))
    return pl.pallas_call(
        kern, grid=(4,), in_specs=[rblk, pl.BlockSpec((1, XTRA, D), lambda k: (k, 0, 0))], out_specs=blk,
        out_shape=jax.ShapeDtypeStruct(wct.shape, BF16), input_output_aliases={0: 0}, name="fix_wct",
        compiler_params=_cp(("arbitrary",)),
    )(wct, xt)


_HP = PIECE // 2
_GWIN = [
    lambda r, sc, hc: r.at[_rows(PMAIN * sc + _HP * hc, _HP), :],
    lambda r, sc, hc: r.at[_rows(512 * hc, 512), pl.ds(1024 * sc, 1024)],
    lambda r, sc, hc: r.at[_rows(1024 * sc + 512 * hc, 512), :],
    lambda r, sc, hc: r.at[_rows(256 * sc + 128 * hc, 128), :],
    lambda r, sc, hc: r.at[_rows(512 * sc + 256 * hc, 256), :],
    lambda r, sc, hc: r.at[_rows(256 * sc + 128 * hc, 128), :],
]
HALF_SHAPES = [(PIECE // 2, D), (512, 1024), (512, 1024), (128, 1024), (256, 1024), (128, 1024)]


def _rs_sibling_plan(ts):
    def copies(sk, R):
        x, y, c = _coords()
        out = []
        for t in ts:
            for sc in range(4):
                land = R["ra%d" % t].at[sc]
                out.append(((_GWIN[t](R["g%d" % t], sc, 1 - c), land, (x, y, 1 - c)), land))
        return out
    return _Plan(4 * len(ts), copies)


def _rs_chips_plan(ts):
    def copies(sk, R):
        _, _, c = _coords()
        out = []
        for t in ts:
            for j, ((px, py), ps) in enumerate(_other_chips(sk)):
                land = R["rb%d" % t].at[j]
                out.append(((R["hb%d" % t].at[ps], land, (px, py, c)), land))
        return out
    return _Plan(3 * len(ts), copies)


def _rs_share_plan(ts):
    def copies(sk, R):
        x, y, c = _coords()
        out = []
        for t in ts:
            rows = HALF_SHAPES[t][0]
            mine = R["f%d" % t].at[_rows(rows * c, rows), :]
            out.append(((mine, mine, (x, y, 1 - c)), R["f%d" % t].at[_rows(rows * (1 - c), rows), :]))
        return out
    return _Plan(len(ts), copies)


def _half_tiling(t):
    rows, cols = HALF_SHAPES[t]
    if t == 0:
        return (256, cols), rows // 256, lambda i: (i, 0)
    if t == 1:
        return (rows, 256), cols // 256, lambda i: (0, i)
    return (rows, cols), 1, lambda i: (0, 0)


def _window_block(t, sc, hc, i):
    if t == 0:
        return (PMAIN // 256) * sc + (PIECE // 512) * hc + i, 0
    if t == 1:
        return hc, 4 * sc + i
    return 2 * sc + hc, 0


def _chip_sum(g, ra, t, idx, name):
    rows, cols = HALF_SHAPES[t]
    blk, nblk, inner = _half_tiling(t)

    def kern(idx_ref, g_ref, r_ref, hb_ref, hf_ref):
        v = g_ref[...].astype(F32) + r_ref[0].astype(F32)
        hb_ref[0] = v.astype(BF16)
        hf_ref[0] = v

    gmap = lambda sc, i, idx_ref: _window_block(t, sc, idx_ref[1], i)
    omap = lambda sc, i, idx_ref: (sc,) + inner(i)
    grid_spec = pltpu.PrefetchScalarGridSpec(
        num_scalar_prefetch=1, grid=(4, nblk),
        in_specs=[pl.BlockSpec(blk, gmap), pl.BlockSpec((1,) + blk, omap)],
        out_specs=(pl.BlockSpec((1,) + blk, omap), pl.BlockSpec((1,) + blk, omap)))
    return pl.pallas_call(
        kern, grid_spec=grid_spec,
        out_shape=(jax.ShapeDtypeStruct((4, rows, cols), BF16), jax.ShapeDtypeStruct((4, rows, cols), F32)),
        name=name, compiler_params=_cp(("parallel", "parallel")),
    )(idx, g, ra)


def _final_sum(hf, rb, t, idx, name):
    rows, cols = HALF_SHAPES[t]
    blk, nblk, inner = _half_tiling(t)
    nbr = rows // blk[0]

    def kern(idx_ref, h_ref, r_ref, o_ref):
        o_ref[...] = ((h_ref[0] + r_ref[0].astype(F32)) + r_ref[1].astype(F32)) + r_ref[2].astype(F32)

    def omap(i, idx_ref):
        r, cidx = inner(i)
        return nbr * idx_ref[1] + r, cidx

    grid_spec = pltpu.PrefetchScalarGridSpec(
        num_scalar_prefetch=1, grid=(nblk,),
        in_specs=[pl.BlockSpec((1,) + blk, lambda i, idx_ref: (idx_ref[0],) + inner(i)),
                  pl.BlockSpec((3,) + blk, lambda i, idx_ref: (0,) + inner(i))],
        out_specs=pl.BlockSpec(blk, omap))
    return pl.pallas_call(
        kern, grid_spec=grid_spec, out_shape=jax.ShapeDtypeStruct((2 * rows, cols), F32),
        name=name, compiler_params=_cp(("parallel",)),
    )(idx, hf, rb)


class _ReduceScatter:
    def __init__(self, ts, grads, idx, tag):
        self.ts, self.idx, self.tag = ts, idx, tag
        arr = {}
        for t in ts:
            arr["g%d" % t] = grads[t]
            arr["ra%d" % t] = lax.empty((4,) + HALF_SHAPES[t], BF16)
        self.plan = _rs_sibling_plan(ts)
        self.arr, self.sems, self.token = _split_call("rs_sibling_start_" + tag, arr, start=self.plan)

    def chips(self, after):
        arr, _, _ = _split_call("rs_sibling_wait_" + self.tag, self.arr, wait=self.plan, wait_sems=self.sems, after=after)
        brr, self.hf = {}, {}
        for t in self.ts:
            hb, self.hf[t] = _chip_sum(arr["g%d" % t], arr["ra%d" % t], t, self.idx, "chip_sum_%d" % t)
            brr["hb%d" % t] = hb
            brr["rb%d" % t] = lax.empty((3,) + HALF_SHAPES[t], BF16)
        self.plan = _rs_chips_plan(self.ts)
        self.arr, self.sems, self.token = _split_call("rs_chips_start_" + self.tag, brr, start=self.plan)
        return self.token

    def share(self, after):
        brr, _, _ = _split_call("rs_chips_wait_" + self.tag, self.arr, wait=self.plan, wait_sems=self.sems, after=after)
        frr = {"f%d" % t: _final_sum(self.hf[t], brr["rb%d" % t], t, self.idx, "final_sum_%d" % t) for t in self.ts}
        self.plan = _rs_share_plan(self.ts)
        self.arr, self.sems, self.token = _split_call("rs_share_start_" + self.tag, frr, start=self.plan)
        return self.token

    def result(self, after):
        frr, _, _ = _split_call("rs_share_wait_" + self.tag, self.arr, wait=self.plan, wait_sems=self.sems, after=after)
        return {t: frr["f%d" % t] for t in self.ts}


def _small_all_gather(v):
    def body(v_ref, o_ref, send_sems, recv_sems, loc_sem):
        x, y, c = _coords()
        me = 4 * x + 2 * y + c
        lc = pltpu.make_async_copy(v_ref, o_ref.at[me], loc_sem)
        lc.start()
        cps = []
        for k in range(1, 8):
            fx, fy, fc = (k >> 2) & 1, (k >> 1) & 1, k & 1
            dev = ((1 - x) if fx else x, (1 - y) if fy else y, (1 - c) if fc else c)
            cp = pltpu.make_async_remote_copy(src_ref=v_ref, dst_ref=o_ref.at[me], send_sem=send_sems.at[k - 1],
                                              recv_sem=recv_sems.at[k - 1], device_id=dev, device_id_type=MESH)
            cp.start()
            cps.append((cp, 4 * dev[0] + 2 * dev[1] + dev[2]))
        for k, (cp, frm) in enumerate(cps):
            got = o_ref.at[frm]
            pltpu.make_async_remote_copy(src_ref=got, dst_ref=got, send_sem=send_sems.at[k], recv_sem=recv_sems.at[k],
                                         device_id=(x, y, c), device_id_type=MESH).wait_recv()
        for cp, _ in cps:
            cp.wait_send()
        lc.wait()

    hbm = pl.BlockSpec(memory_space=HBM)
    return pl.pallas_call(
        body, out_shape=jax.ShapeDtypeStruct((8,) + v.shape, F32), in_specs=[hbm], out_specs=hbm,
        scratch_shapes=[pltpu.SemaphoreType.DMA((7,)), pltpu.SemaphoreType.DMA((7,)), pltpu.SemaphoreType.DMA(())],
        name="small_all_gather", compiler_params=pltpu.CompilerParams(has_side_effects=True),
    )(v)


def _sum8(v):
    def kern(v_ref, o_ref):
        acc = v_ref[0]
        for k in range(1, 8):
            acc = acc + v_ref[k]
        o_ref[...] = acc

    return pl.pallas_call(kern, out_shape=jax.ShapeDtypeStruct(v.shape[1:], F32), name="small_sum")(v)


def _adamw(w, g, m, v, name, tr=128, blk0=0, nblk=None, into=None, copy_g=False):
    R, C = w.shape
    tr = min(tr, R)
    if nblk is None:
        assert R % tr == 0 and blk0 == 0
        nblk = R // tr
    n_out = 4 if copy_g else 3

    def kern(*refs):
        w_ref, g_ref, m_ref, v_ref = refs[:4]
        d_ref, mo_ref, vo_ref = refs[-n_out:][:3]
        gv = g_ref[...]
        mn = ADAM_B1 * m_ref[...] + (1.0 - ADAM_B1) * gv
        vn = ADAM_B2 * v_ref[...] + (1.0 - ADAM_B2) * (gv * gv)
        m_hat = mn / (1.0 - ADAM_B1 ** ADAM_STEP)
        v_hat = vn / (1.0 - ADAM_B2 ** ADAM_STEP)
        d_ref[...] = -ADAM_LR * (m_hat / (jnp.sqrt(v_hat) + ADAM_EPS) + ADAM_WD * w_ref[...])
        mo_ref[...] = mn
        vo_ref[...] = vn
        if copy_g:
            refs[-1][...] = gv

    blk = pl.BlockSpec((tr, C), lambda i: (blk0 + i, 0))
    sd = jax.ShapeDtypeStruct((R, C), F32)
    in_specs, args, aliases = [blk] * 4, [w, g, m, v], {}
    if into is not None:
        in_specs += [pl.BlockSpec(memory_space=pl.ANY)] * 3
        args += list(into)
        aliases = {4: 0, 5: 1, 6: 2}
    return pl.pallas_call(kern, grid=(nblk,), in_specs=in_specs, out_specs=(blk,) * n_out, out_shape=(sd,) * n_out,
                          input_output_aliases=aliases, name=name, compiler_params=_cp(("parallel",)))(*args)


def _to_piece(wt, s):
    z = lambda n: jnp.zeros((n, D), wt.dtype)
    pads = [functools.partial(lambda k, w: jnp.pad(w, ((8 * k, PIECE - W_SHARD - 8 * k), (0, 0))), k) for k in range(3)]
    last = lambda w: jnp.concatenate([z(24), w[:744], w[776:], w[744:776], z(PIECE - 24 - W_SHARD)], axis=0)
    return lax.switch(s, pads + [last], wt)


def _from_piece(p, s):
    cuts = [functools.partial(lambda k, q: q[8 * k:8 * k + W_SHARD], k) for k in range(3)]
    last = lambda q: jnp.concatenate([q[24:768], q[2816:2848], q[768:2816]], axis=0)
    return lax.switch(s, cuts + [last], p)


_SMALL = [("norm_mix", 1024), ("b_gate", 2048), ("ssm_conv_b", 4096), ("dt_bias", 32), ("A_log", 32), ("D_skip", 32),
          ("ssm_norm_w", 2048), ("norm_mlp", 1024), ("norm_final", 1024), ("sc_conv_w", 3072), ("ssm_conv_w", 16384),
          ("loss", 1)]


def _pack(vals, table, rows):
    parts = []
    for name, n in table:
        v = vals[name].reshape(-1).astype(F32)
        pad = (-n) % 128
        parts.append(jnp.pad(v, (0, pad)) if pad else v)
    flat = jnp.concatenate(parts)
    return jnp.pad(flat, (0, rows * 128 - flat.shape[0])).reshape(rows, 128)


def _unpack(arr, table):
    flat = arr.reshape(-1)
    out, off = {}, 0
    for name, n in table:
        out[name] = flat[off:off + n]
        off += n + ((-n) % 128)
    return out


def kernel(x, norm_mix, w_in, b_gate, sc_conv_w, ssm_conv_w, ssm_conv_b, dt_bias, A_log, D_skip, ssm_norm_w, w_branch_sc, w_branch_ssm, w_out, norm_mlp, w_mlp1, w_mlp2, norm_final, loss_target, m_norm_mix, m_w_in, m_b_gate, m_sc_conv_w, m_ssm_conv_w, m_ssm_conv_b, m_dt_bias, m_A_log, m_D_skip, m_ssm_norm_w, m_w_branch_sc, m_w_branch_ssm, m_w_out, m_norm_mlp, m_w_mlp1, m_w_mlp2, m_norm_final, v_norm_mix, v_w_in, v_b_gate, v_sc_conv_w, v_ssm_conv_w, v_ssm_conv_b, v_dt_bias, v_A_log, v_D_skip, v_ssm_norm_w, v_w_branch_sc, v_w_branch_ssm, v_w_out, v_norm_mlp, v_w_mlp1, v_w_mlp2, v_norm_final):
    L = x.shape[1]
    nc = L // Q
    xi, yi, ci = lax.axis_index("x"), lax.axis_index("y"), lax.axis_index("c")
    s = 2 * xi + yi
    idx = jnp.stack([s, ci]).astype(jnp.int32)
    x0 = x.reshape(L, D)
    tgt = loss_target.reshape(L, D)

    piece = _to_piece(w_in.T, s)
    nb = PMAIN // XTRA
    wct0 = _place(piece, (NCW, D), (XTRA, D), lambda i, r: (nb * r[0] + i, 0), idx, "place_wct", nblk=nb)
    xt0 = _place(piece, (4, XTRA, D), (1, XTRA, D), lambda i, r: (r[0], 0, 0), idx, "place_xt", blk0=nb, nblk=1)
    cws = jnp.zeros((8, 1280), F32)
    cws = cws.at[0:3, 0:256].set(sc_conv_w).at[0:4, 256:1280].set(ssm_conv_w)
    cw0 = lax.dynamic_update_slice(jnp.zeros((4, 8, 1280), F32), cws[None], (s, 0, 0))
    win_keys, mid_keys, end_keys = ["wct", "xt", "cw"], ["wa", "wb", "wo", "w1"], ["w2"]
    gw, sems_w, tok = _split_call("ag_win_start", {"wct": wct0, "xt": xt0, "cw": cw0}, start=_ag_chips_plan(win_keys))
    wa0 = _place(w_branch_sc, (D, D), (256, 1024), lambda i, r: (r[0], 0), idx, "place_wa", dep=tok)
    wb0 = _place(w_branch_ssm, (INNER, D), (512, 1024), lambda i, r: (r[0], 0), idx, "place_wb", dep=tok)
    wo0 = _place(w_out, (D, D), (256, 1024), lambda i, r: (r[0], 0), idx, "place_wo", dep=tok)
    w10 = _place(w_mlp1, (D, DFF), (256, 1024), lambda i, r: (i, r[0]), idx, "place_w1", dep=tok)
    gm, sems_m, tok = _split_call("ag_mid_start", {"wa": wa0, "wb": wb0, "wo": wo0, "w1": w10},
                                  start=_ag_chips_plan(mid_keys))
    w20 = _place(w_mlp2, (DFF, D), (256, 1024), lambda i, r: (4 * r[0] + i, 0), idx, "place_w2", dep=tok)
    ge, sems_e, tok = _split_call("ag_end_start", {"w2": w20}, start=_ag_chips_plan(end_keys))
    h = _rms_fwd(x0, norm_mix, "rms_mix", dep=tok)
    gw, sems_w, tok = _split_call("ag_win_pass", gw, wait=_ag_chips_plan(win_keys), wait_sems=sems_w,
                                  start=_ag_sibling_plan(win_keys), after=h)
    gw, _, _ = _split_call("ag_win_done", gw, wait=_ag_sibling_plan(win_keys), wait_sems=sems_w, after=tok)
    wc, cw_all = _fix_wct(gw["wct"], gw["xt"]), gw["cw"]
    sc_w_full = jnp.concatenate([cw_all[k, :, 0:256] for k in range(4)], axis=1)
    ssm_w_full = jnp.concatenate([cw_all[k, :, 256:1280] for k in range(4)], axis=1)
    cw4 = ssm_w_full.at[4].set(ssm_conv_b)
    vec = jnp.zeros((8, 128), F32).at[0, :NH].set(dt_bias).at[1, :NH].set(A_log)
    vecg = jnp.zeros((NG, 8, 128), F32).at[:, 0, :4].set(A_log.reshape(NG, 4)).at[:, 1, :4].set(D_skip.reshape(NG, 4))

    proj = _matmul(h, wc, "nt", F32, 512, 1280, 1024, "in_proj", n_outer=True)
    gm, sems_m, tok = _split_call("ag_mid_pass", gm, wait=_ag_chips_plan(mid_keys), wait_sems=sems_m,
                                  start=_ag_sibling_plan(mid_keys), after=proj)
    proj = _tie(proj, tok, "tie_proj")
    ya = _sc_fwd(proj, sc_w_full)
    xbc = _ssm_conv_fwd(proj, cw4)
    dt4, cs4, sg4 = _dt_prep(proj, vec)
    ge, sems_e, tok = _split_call("ag_end_pass", ge, wait=_ag_chips_plan(end_keys), wait_sems=sems_e,
                                  start=_ag_sibling_plan(end_keys), after=xbc)
    xbc = _tie(xbc, tok, "tie_xbc")
    y, s_all = _ssd_fwd(xbc, dt4, cs4, vecg)
    yb = _gnorm_fwd(y, proj, ssm_norm_w)
    gm, _, _ = _split_call("ag_mid_done", gm, wait=_ag_sibling_plan(mid_keys), wait_sems=sems_m, after=yb)
    ge, _, _ = _split_call("ag_end_done", ge, wait=_ag_sibling_plan(end_keys), wait_sems=sems_e, after=yb)
    wa, wb, wo, w1, w2 = gm["wa"], gm["wb"], gm["wo"], gm["w1"], ge["w2"]
    br_a = _matmul(ya, wa, "nn", F32, 512, 1024, 1024, "branch_sc")
    br_b = _matmul(yb, wb, "nn", F32, 512, 1024, 2048, "branch_ssm")
    merged = _merge_fwd(proj, b_gate, br_a, br_b)
    x1 = _matmul(merged, wo, "nn", F32, 512, 1024, 1024, "out_proj", epi="res", extra=x0)
    h2 = _rms_fwd(x1, norm_mlp, "rms_mlp")
    a1, rl = _matmul(h2, w1, "nn", F32, 512, 1024, 1024, "mlp1", epi="relu2", n_outer=True)
    x2 = _matmul(rl, w2, "nn", F32, 512, 1024, 2048, "mlp2", epi="res", extra=x1)
    dx2, g_nf, loss8 = _final(x2, norm_final, tgt)

    da = _matmul(dx2, w2, "nt", BF16, 512, 1024, 1024, "mlp2_dx", epi="drelu", extra=a1, n_outer=True)
    g_w2 = _matmul(rl, dx2, "tn", BF16, 1024, 1024, 512, "mlp2_dw")
    g_w1 = _matmul(h2, da, "tn", BF16, 1024, 1024, 512, "mlp1_dw")
    dh2 = _matmul(da, w1, "nt", F32, 512, 1024, 2048, "mlp1_dx")
    dx1, g_nmlp = _rms_bwd(dh2, x1, norm_mlp, dx2, "rms_mlp_bwd")
    dmerged = _matmul(dx1, wo, "nt", F32, 512, 1024, 1024, "out_proj_dx")
    g_wo = _matmul(merged, dx1, "tn", BF16, 1024, 1024, 512, "out_proj_dw")
    dproj = lax.empty((L, NCW), BF16)
    dbr, dproj, g_bg = _merge_bwd(dmerged, proj, b_gate, br_a, br_b, dproj)
    dya = _matmul(dbr[0], wa, "nt", F32, 512, 1024, 1024, "branch_sc_dx")
    g_wa = _matmul(ya, dbr[0], "tn", BF16, 1024, 1024, 512, "branch_sc_dw")
    dproj, g_scw = _sc_bwd(dya, proj, sc_w_full, dproj)
    dyb = _matmul(dbr[1], wb, "nt", F32, 512, 1024, 1024, "branch_ssm_dx", n_outer=True)
    g_wb = _matmul(yb, dbr[1], "tn", BF16, 1024, 1024, 512, "branch_ssm_dw")
    rs_a = _ReduceScatter([1, 2, 3, 4, 5], {1: g_w1, 2: g_w2, 3: g_wa, 4: g_wb, 5: g_wo}, idx, "a")
    dy, dproj, g_snw = _gnorm_bwd(_tie(dyb, rs_a.token, "tie_dyb"), y, proj, ssm_norm_w, dproj)
    tok = rs_a.chips(after=dy)
    dxs, dbm, dcm, ddt_g, st = _ssd_bwd(xbc, dt4, cs4, sg4, vecg, s_all, _tie(dy, tok, "tie_dy"))
    dproj, gx1 = _ssm_conv_bwd(dxs, proj, cw4, dproj, 0, "ssm_conv_bwd_x")
    dproj, gx2 = _ssm_conv_bwd(dbm, proj, cw4, dproj, INNER, "ssm_conv_bwd_b")
    dproj, gx3 = _ssm_conv_bwd(dcm, proj, cw4, dproj, INNER + NG * NS, "ssm_conv_bwd_c")
    g_cw4 = jnp.concatenate([gx1, gx2, gx3], axis=1)
    dproj, g_dtb = _dt_bwd(ddt_g, dproj)
    g_wc = _matmul(dproj, h, "tn", BF16, 1280, 1024, 512, "in_proj_dw")
    rs_b = _ReduceScatter([0], {0: g_wc}, idx, "b")
    tok = rs_a.share(after=rs_b.token)
    dh = _matmul(dproj, wc, "nn", F32, 512, 1024, 2304, "in_proj_dx", dep=tok)
    grad_x, g_nm = _rms_bwd(dh, x0, norm_mix, dx1, "rms_mix_bwd")

    small = {"norm_mix": g_nm[0], "b_gate": g_bg[0], "ssm_conv_b": g_cw4[4], "dt_bias": g_dtb[0, :NH],
             "A_log": st[:, 0, :4], "D_skip": st[:, 1, :4], "ssm_norm_w": g_snw[0], "norm_mlp": g_nmlp[0],
             "norm_final": g_nf[0], "sc_conv_w": g_scw[0:3], "ssm_conv_w": g_cw4[0:4], "loss": loss8[0, 0:1]}
    small_sum = _sum8(_small_all_gather(_pack(small, _SMALL, SMALL_ROWS)))
    gs = _unpack(small_sum, _SMALL)
    tok = rs_b.chips(after=small_sum)
    red = rs_a.result(after=tok)
    big = {"w_mlp1": red[1], "w_mlp2": red[2], "w_branch_sc": red[3], "w_branch_ssm": red[4], "w_out": red[5]}

    given = dict(norm_mix=norm_mix, w_in=w_in, b_gate=b_gate, sc_conv_w=sc_conv_w, ssm_conv_w=ssm_conv_w, ssm_conv_b=ssm_conv_b, dt_bias=dt_bias, A_log=A_log, D_skip=D_skip, ssm_norm_w=ssm_norm_w, w_branch_sc=w_branch_sc, w_branch_ssm=w_branch_ssm, w_out=w_out, norm_mlp=norm_mlp, w_mlp1=w_mlp1, w_mlp2=w_mlp2, norm_final=norm_final,
                 m_norm_mix=m_norm_mix, m_w_in=m_w_in, m_b_gate=m_b_gate, m_sc_conv_w=m_sc_conv_w, m_ssm_conv_w=m_ssm_conv_w, m_ssm_conv_b=m_ssm_conv_b, m_dt_bias=m_dt_bias, m_A_log=m_A_log, m_D_skip=m_D_skip, m_ssm_norm_w=m_ssm_norm_w, m_w_branch_sc=m_w_branch_sc, m_w_branch_ssm=m_w_branch_ssm, m_w_out=m_w_out, m_norm_mlp=m_norm_mlp, m_w_mlp1=m_w_mlp1, m_w_mlp2=m_w_mlp2, m_norm_final=m_norm_final,
                 v_norm_mix=v_norm_mix, v_w_in=v_w_in, v_b_gate=v_b_gate, v_sc_conv_w=v_sc_conv_w, v_ssm_conv_w=v_ssm_conv_w, v_ssm_conv_b=v_ssm_conv_b, v_dt_bias=v_dt_bias, v_A_log=v_A_log, v_D_skip=v_D_skip, v_ssm_norm_w=v_ssm_norm_w, v_w_branch_sc=v_w_branch_sc, v_w_branch_ssm=v_w_branch_ssm, v_w_out=v_w_out, v_norm_mlp=v_norm_mlp, v_w_mlp1=v_w_mlp1, v_w_mlp2=v_w_mlp2, v_norm_final=v_norm_final)
    order = ["norm_mix", "w_in", "b_gate", "sc_conv_w", "ssm_conv_w", "ssm_conv_b", "dt_bias", "A_log", "D_skip",
             "ssm_norm_w", "w_branch_sc", "w_branch_ssm", "w_out", "norm_mlp", "w_mlp1", "w_mlp2", "norm_final"]
    grad, delta, new_m, new_v = {}, {}, {}, {}
    for n in big:
        delta[n], new_m[n], new_v[n], grad[n] = _adamw(given[n], big[n], given["m_" + n], given["v_" + n],
                                                       "adamw_" + n, copy_g=True)
    tok = rs_b.share(after=new_v["w_mlp2"])
    gp = rs_b.result(after=tok)[0]
    gwt = _from_piece(gp, s)
    wt_args = (w_in.T, gwt, m_w_in.T, v_w_in.T)
    head = _adamw(*wt_args, "adamw_w_in", tr=256, nblk=W_SHARD // 256)
    dt_, mt_, vt_ = _adamw(*wt_args, "adamw_w_in_tail", tr=8, blk0=(W_SHARD // 256) * 32, nblk=1, into=head)
    grad["w_in"], delta["w_in"], new_m["w_in"], new_v["w_in"] = gwt.T, dt_.T, mt_.T, vt_.T
    big["w_in"] = None
    grad_small = {n: gs[n].reshape(given[n].shape) for n in order if n not in big and n not in ("sc_conv_w", "ssm_conv_w")}
    grad_small["sc_conv_w"] = lax.dynamic_slice(gs["sc_conv_w"].reshape(3, D), (0, 256 * s), (3, 256))
    grad_small["ssm_conv_w"] = lax.dynamic_slice(gs["ssm_conv_w"].reshape(4, XBC), (0, 1024 * s), (4, 1024))
    table = [(n, int(grad_small[n].size)) for n in grad_small]
    rows = 136
    pk = lambda d: _pack(d, table, rows)
    ds_, ms_, vs_ = _adamw(pk({n: given[n] for n in grad_small}), pk(grad_small), pk({n: given["m_" + n] for n in grad_small}),
                           pk({n: given["v_" + n] for n in grad_small}), "adamw_small", tr=rows)
    ds_, ms_, vs_ = _unpack(ds_, table), _unpack(ms_, table), _unpack(vs_, table)
    for n in grad_small:
        shp = given[n].shape
        grad[n] = grad_small[n]
        delta[n], new_m[n], new_v[n] = ds_[n].reshape(shp), ms_[n].reshape(shp), vs_[n].reshape(shp)

    loss = gs["loss"].reshape(())
    return (loss, grad_x.reshape(1, L, D), *[grad[n] for n in order], *[delta[n] for n in order],
            *[new_m[n] for n in order], *[new_v[n] for n in order])
```

```python
import functools

import jax
import jax.numpy as jnp
from jax import lax
from jax.experimental import pallas as pl
from jax.experimental.pallas import tpu as pltpu

F32 = jnp.float32
BF16 = jnp.bfloat16
MESH = pl.DeviceIdType.MESH
HBM = pltpu.HBM

D = 1024
INNER = 2048
HD = 64
NH = 32
NG = 8
NS = 128
Q = 128
XBC = 4096
DFF = 4096
EPS = 1e-6
W_SHARD = 2824
NCW = 11520
PIECE = 3072
PMAIN = 2816
C_Z, C_XBC, C_GATE, C_DT = 3072, 5120, 9216, 11264
SMALL_ROWS = 256
VMEM_LIMIT = 56 * 1024 * 1024

ADAM_LR, ADAM_B1, ADAM_B2, ADAM_EPS, ADAM_WD, ADAM_STEP = 0.001, 0.9, 0.999, 1e-08, 0.01, 10


def _cp(sem=None, vmem=VMEM_LIMIT):
    return pltpu.CompilerParams(dimension_semantics=sem, vmem_limit_bytes=vmem)


def _sigmoid(v):
    return 1.0 / (1.0 + jnp.exp(-v))


_DIMS = {"nn": (((1,), (0,)), ((), ())), "nt": (((1,), (1,)), ((), ())), "tn": (((0,), (0,)), ((), ()))}


def _matmul(a, b, mode, out_dtype, tm, tn, tk, name, epi=None, extra=None, n_outer=False, dep=None):
    if mode == "tn":
        K, M = a.shape
    else:
        M, K = a.shape
    N = b.shape[0] if mode == "nt" else b.shape[1]
    tm, tn, tk = min(tm, M), min(tn, N), min(tk, K)
    assert M % tm == 0 and N % tn == 0 and K % tk == 0, (name, M, N, K, tm, tn, tk)
    nm, nn, nk = M // tm, N // tn, K // tk
    dims = _DIMS[mode]

    def ij(p0, p1):
        return (p1, p0) if n_outer else (p0, p1)

    if mode == "tn":
        a_spec = pl.BlockSpec((tk, tm), lambda p0, p1, k: (k, ij(p0, p1)[0]))
    else:
        a_spec = pl.BlockSpec((tm, tk), lambda p0, p1, k: (ij(p0, p1)[0], k))
    if mode == "nt":
        b_spec = pl.BlockSpec((tn, tk), lambda p0, p1, k: (ij(p0, p1)[1], k))
    else:
        b_spec = pl.BlockSpec((tk, tn), lambda p0, p1, k: (k, ij(p0, p1)[1]))
    o_spec = pl.BlockSpec((tm, tn), lambda p0, p1, k: ij(p0, p1))
    in_specs = [a_spec, b_spec]
    args = [a, b]
    if epi in ("res", "drelu"):
        in_specs.append(o_spec)
        args.append(extra)
    if dep is not None:
        in_specs.append(pl.BlockSpec(memory_space=pl.ANY))
        args.append(dep)
    n_in = len(args)
    if epi == "relu2":
        out_shape = (jax.ShapeDtypeStruct((M, N), F32), jax.ShapeDtypeStruct((M, N), BF16))
        out_specs = (o_spec, o_spec)
    else:
        out_shape = jax.ShapeDtypeStruct((M, N), out_dtype)
        out_specs = o_spec

    def kern(*refs):
        a_ref, b_ref = refs[0], refs[1]
        e_ref = refs[2] if epi in ("res", "drelu") else None
        acc = refs[-1]
        outs = refs[n_in:-1]
        k = pl.program_id(2)

        @pl.when(k == 0)
        def _():
            acc[...] = jnp.zeros_like(acc)

        acc[...] += lax.dot_general(a_ref[...].astype(BF16), b_ref[...].astype(BF16), dims,
                                    preferred_element_type=F32)

        @pl.when(k == nk - 1)
        def _():
            r = acc[...]
            if epi is None:
                outs[0][...] = r.astype(out_dtype)
            elif epi == "res":
                outs[0][...] = (r + e_ref[...]).astype(out_dtype)
            elif epi == "relu2":
                outs[0][...] = r
                t = jnp.maximum(r, 0.0)
                outs[1][...] = (t * t).astype(BF16)
            else:
                outs[0][...] = (r * (2.0 * jnp.maximum(e_ref[...], 0.0))).astype(out_dtype)

    grid = (nn, nm, nk) if n_outer else (nm, nn, nk)
    return pl.pallas_call(
        kern, grid=grid, in_specs=in_specs, out_specs=out_specs, out_shape=out_shape,
        scratch_shapes=[pltpu.VMEM((tm, tn), F32)], name=name,
        compiler_params=_cp(("parallel", "parallel", "arbitrary")),
    )(*args)


def _rms_fwd(x, w, name, tl=256, dep=None):
    L = x.shape[0]

    def kern(x_ref, w_ref, *rest):
        o_ref = rest[-1]
        xv = x_ref[...]
        r = lax.rsqrt(jnp.mean(xv * xv, axis=-1, keepdims=True) + EPS)
        o_ref[...] = ((xv * r) * w_ref[...]).astype(BF16)

    row = pl.BlockSpec((tl, D), lambda i: (i, 0))
    deps = [] if dep is None else [dep]
    return pl.pallas_call(
        kern, grid=(L // tl,),
        in_specs=[row, pl.BlockSpec((1, D), lambda i: (0, 0))] + [pl.BlockSpec(memory_space=pl.ANY)] * len(deps),
        out_specs=row, out_shape=jax.ShapeDtypeStruct((L, D), BF16), name=name, compiler_params=_cp(("parallel",)),
    )(x, w.reshape(1, D), *deps)


def _rms_bwd(dy, x, w, res, name, tl=256, dep=None):
    L = x.shape[0]
    deps = [] if dep is None else [dep]

    def kern(dy_ref, x_ref, w_ref, res_ref, *rest):
        dx_ref, gw_ref = rest[-2:]
        @pl.when(pl.program_id(0) == 0)
        def _():
            gw_ref[...] = jnp.zeros_like(gw_ref)

        xv = x_ref[...]
        dyv = dy_ref[...]
        r = lax.rsqrt(jnp.mean(xv * xv, axis=-1, keepdims=True) + EPS)
        xn = xv * r
        gw_ref[...] += jnp.broadcast_to(jnp.sum(dyv * xn, axis=0, keepdims=True), (8, D))
        dxn = dyv * w_ref[...]
        dx_ref[...] = res_ref[...] + r * (dxn - xn * jnp.mean(dxn * xn, axis=-1, keepdims=True))

    row = pl.BlockSpec((tl, D), lambda i: (i, 0))
    return pl.pallas_call(
        kern, grid=(L // tl,),
        in_specs=[row, row, pl.BlockSpec((1, D), lambda i: (0, 0)), row] + [pl.BlockSpec(memory_space=pl.ANY)] * len(deps),
        out_specs=(row, pl.BlockSpec((8, D), lambda i: (0, 0))),
        out_shape=(jax.ShapeDtypeStruct((L, D), F32), jax.ShapeDtypeStruct((8, D), F32)),
        name=name, compiler_params=_cp(("arbitrary",)),
    )(dy, x, w.reshape(1, D), res, *deps)


def _final(x2, w, tgt, tl=256):
    L = x2.shape[0]

    def kern(x_ref, w_ref, t_ref, dx_ref, gw_ref, loss_ref):
        @pl.when(pl.program_id(0) == 0)
        def _():
            gw_ref[...] = jnp.zeros_like(gw_ref)
            loss_ref[...] = jnp.zeros_like(loss_ref)

        xv = x_ref[...]
        r = lax.rsqrt(jnp.mean(xv * xv, axis=-1, keepdims=True) + EPS)
        xn = xv * r
        e = xn * w_ref[...] - t_ref[...]
        per_tok = jnp.mean(e * e, axis=-1, keepdims=True)
        loss_ref[...] += 0.5 * jnp.sum(per_tok)
        dyv = e * (1.0 / D)
        gw_ref[...] += jnp.broadcast_to(jnp.sum(dyv * xn, axis=0, keepdims=True), (8, D))
        dxn = dyv * w_ref[...]
        dx_ref[...] = r * (dxn - xn * jnp.mean(dxn * xn, axis=-1, keepdims=True))

    row = pl.BlockSpec((tl, D), lambda i: (i, 0))
    return pl.pallas_call(
        kern, grid=(L // tl,), in_specs=[row, pl.BlockSpec((1, D), lambda i: (0, 0)), row],
        out_specs=(row, pl.BlockSpec((8, D), lambda i: (0, 0)), pl.BlockSpec((8, 128), lambda i: (0, 0))),
        out_shape=(jax.ShapeDtypeStruct((L, D), F32), jax.ShapeDtypeStruct((8, D), F32),
                   jax.ShapeDtypeStruct((8, 128), F32)),
        name="final_norm_loss", compiler_params=_cp(("arbitrary",)),
    )(x2, w.reshape(1, D), tgt)


def _down(v, k):
    if k == 0:
        return v
    t = lax.broadcasted_iota(jnp.int32, v.shape, 0)
    return jnp.where(t >= k, pltpu.roll(v, k, axis=0), 0.0)


def _up(v, k):
    if k == 0:
        return v
    n = v.shape[0]
    t = lax.broadcasted_iota(jnp.int32, v.shape, 0)
    return jnp.where(t < n - k, pltpu.roll(v, n - k, axis=0), 0.0)


TW = 256


def _sc_fwd(proj, cw):
    L = proj.shape[0]
    nb = D // TW

    def kern(b_ref, c_ref, x_ref, w_ref, o_ref):
        u = c_ref[...] * x_ref[...]
        w = w_ref[...]
        cv = w[0:1] * _down(u, 2) + w[1:2] * _down(u, 1) + w[2:3] * u
        o_ref[...] = (b_ref[...] * cv).astype(BF16)

    col = lambda off: pl.BlockSpec((L, TW), lambda j: (0, off + j))
    return pl.pallas_call(
        kern, grid=(nb,), in_specs=[col(0), col(nb), col(2 * nb), pl.BlockSpec((8, TW), lambda j: (0, j))],
        out_specs=pl.BlockSpec((L, TW), lambda j: (0, j)), out_shape=jax.ShapeDtypeStruct((L, D), BF16),
        name="sc_fwd", compiler_params=_cp(("parallel",)),
    )(proj, proj, proj, cw)


def _sc_bwd(dya, proj, cw, dproj):
    L = proj.shape[0]
    nb = D // TW

    def kern(d_ref, b_ref, c_ref, x_ref, w_ref, _, dp_ref, gw_ref):
        sec = pl.program_id(1)
        cs, xs, dyv = c_ref[...], x_ref[...], d_ref[...]
        w = w_ref[...]
        u = cs * xs
        u1, u2 = _down(u, 1), _down(u, 2)
        cv = w[0:1] * u2 + w[1:2] * u1 + w[2:3] * u
        dcv = dyv * b_ref[...]
        du = w[2:3] * dcv + w[1:2] * _up(dcv, 1) + w[0:1] * _up(dcv, 2)
        g0 = jnp.sum(dcv * u2, axis=0, keepdims=True)
        g1 = jnp.sum(dcv * u1, axis=0, keepdims=True)
        g2 = jnp.sum(dcv * u, axis=0, keepdims=True)
        row = lax.broadcasted_iota(jnp.int32, (8, TW), 0)
        gw_ref[...] = jnp.where(row == 0, g0, jnp.where(row == 1, g1, jnp.where(row == 2, g2, 0.0)))
        out = jnp.where(sec == 0, dyv * cv, jnp.where(sec == 1, du * xs, du * cs))
        dp_ref[...] = out.astype(BF16)

    col = lambda off: pl.BlockSpec((L, TW), lambda j, s: (0, off + j))
    return pl.pallas_call(
        kern, grid=(nb, 3),
        in_specs=[col(0), col(0), col(nb), col(2 * nb), pl.BlockSpec((8, TW), lambda j, s: (0, j)),
                  pl.BlockSpec(memory_space=pl.ANY)],
        out_specs=(pl.BlockSpec((L, TW), lambda j, s: (0, s * nb + j)), pl.BlockSpec((8, TW), lambda j, s: (0, j))),
        out_shape=(jax.ShapeDtypeStruct(dproj.shape, BF16), jax.ShapeDtypeStruct((8, D), F32)),
        input_output_aliases={5: 0}, name="sc_bwd", compiler_params=_cp(("parallel", "arbitrary")),
    )(dya, proj, proj, proj, cw, dproj)


def _ssm_conv_fwd(proj, cw4):
    L = proj.shape[0]
    off = C_XBC // TW

    def kern(r_ref, w_ref, o_ref):
        raw = r_ref[...]
        w = w_ref[...]
        c4 = w[0:1] * _down(raw, 3) + w[1:2] * _down(raw, 2) + w[2:3] * _down(raw, 1) + w[3:4] * raw + w[4:5]
        o_ref[...] = c4 * _sigmoid(c4)

    return pl.pallas_call(
        kern, grid=(XBC // TW,),
        in_specs=[pl.BlockSpec((L, TW), lambda j: (0, off + j)), pl.BlockSpec((8, TW), lambda j: (0, j))],
        out_specs=pl.BlockSpec((L, TW), lambda j: (0, j)), out_shape=jax.ShapeDtypeStruct((L, XBC), F32),
        name="ssm_conv_fwd", compiler_params=_cp(("parallel",)),
    )(proj, cw4)


def _ssm_conv_bwd(dx, proj, cw4, dproj, col0, name):
    L, width = dx.shape
    off_p = (C_XBC + col0) // TW
    off_w = col0 // TW

    def kern(d_ref, r_ref, w_ref, _, dp_ref, gw_ref):
        raw = r_ref[...]
        w = w_ref[...]
        r1, r2, r3 = _down(raw, 1), _down(raw, 2), _down(raw, 3)
        c4 = w[0:1] * r3 + w[1:2] * r2 + w[2:3] * r1 + w[3:4] * raw + w[4:5]
        sg = _sigmoid(c4)
        dc4 = d_ref[...] * (sg * (1.0 + c4 * (1.0 - sg)))
        draw = w[3:4] * dc4 + w[2:3] * _up(dc4, 1) + w[1:2] * _up(dc4, 2) + w[0:1] * _up(dc4, 3)
        dp_ref[...] = draw.astype(BF16)
        gs = [jnp.sum(dc4 * r3, axis=0, keepdims=True), jnp.sum(dc4 * r2, axis=0, keepdims=True),
              jnp.sum(dc4 * r1, axis=0, keepdims=True), jnp.sum(dc4 * raw, axis=0, keepdims=True),
              jnp.sum(dc4, axis=0, keepdims=True)]
        row = lax.broadcasted_iota(jnp.int32, (8, TW), 0)
        acc = jnp.zeros((8, TW), F32)
        for k, gk in enumerate(gs):
            acc = jnp.where(row == k, gk, acc)
        gw_ref[...] = acc

    return pl.pallas_call(
        kern, grid=(width // TW,),
        in_specs=[pl.BlockSpec((L, TW), lambda j: (0, j)), pl.BlockSpec((L, TW), lambda j: (0, off_p + j)),
                  pl.BlockSpec((8, TW), lambda j: (0, off_w + j)), pl.BlockSpec(memory_space=pl.ANY)],
        out_specs=(pl.BlockSpec((L, TW), lambda j: (0, off_p + j)), pl.BlockSpec((8, TW), lambda j: (0, j))),
        out_shape=(jax.ShapeDtypeStruct(dproj.shape, BF16), jax.ShapeDtypeStruct((8, width), F32)),
        input_output_aliases={3: 0}, name=name, compiler_params=_cp(("arbitrary",)),
    )(dx, proj, cw4, dproj)


def _split3(v):
    h1 = v.astype(BF16)
    r1 = v - h1.astype(F32)
    h2 = r1.astype(BF16)
    h3 = (r1 - h2.astype(F32)).astype(BF16)
    return h1, h2, h3


def _dot01(m01, v, dims=_DIMS["nn"], m_left=True, terms=3):
    out = None
    for part in _split3(v)[:terms]:
        ops = (m01, part) if m_left else (part, m01)
        t = lax.dot_general(ops[0], ops[1], dims, preferred_element_type=F32)
        out = t if out is None else out + t
    return out


def _bdot(a, b, mode="nn"):
    return lax.dot_general(a.astype(BF16), b.astype(BF16), _DIMS[mode], preferred_element_type=F32)


def _softplus(v):
    return jnp.maximum(v, 0.0) + jnp.log1p(jnp.exp(-jnp.abs(v)))


def _dt_prep(proj, vec):
    L = proj.shape[0]

    def kern(p_ref, v_ref, dt_ref, cs_ref, sg_ref):
        v = v_ref[...]
        pre = p_ref[:, 0:128] + v[0:1]
        dt = _softplus(pre)
        da = dt * (-jnp.exp(v[1:2]))
        ii = lax.broadcasted_iota(jnp.int32, (Q, Q), 0)
        jj = lax.broadcasted_iota(jnp.int32, (Q, Q), 1)
        ltri = (jj <= ii).astype(BF16)
        lane = lax.broadcasted_iota(jnp.int32, (Q, 128), 1)
        for val, ref in ((dt, dt_ref), (_dot01(ltri, da), cs_ref), (_sigmoid(pre), sg_ref)):
            for g in range(NG):
                moved = val if g == 0 else pltpu.roll(val, 128 - 4 * g, axis=1)
                ref[g] = jnp.where(lane < 4, moved, 0.0)

    blk = pl.BlockSpec((NG, Q, 128), lambda c: (0, c, 0))
    return pl.pallas_call(
        kern, grid=(L // Q,),
        in_specs=[pl.BlockSpec((Q, 256), lambda c: (c, C_DT // 256)), pl.BlockSpec((8, 128), lambda c: (0, 0))],
        out_specs=(blk, blk, blk),
        out_shape=(jax.ShapeDtypeStruct((NG, L, 128), F32),) * 3,
        name="dt_prep", compiler_params=_cp(("parallel",)),
    )(proj, vec)


def _head_masks():
    lane = lax.broadcasted_iota(jnp.int32, (1, 4 * HD), 1)
    return [((lane >= HD * j) & (lane < HD * (j + 1))) for j in range(4)]


def _expand4(v4, masks):
    R = v4.shape[0]
    out = jnp.zeros((R, 4 * HD), F32)
    for j in range(4):
        out = jnp.where(masks[j], jnp.broadcast_to(v4[:, j:j + 1], (R, 4 * HD)), out)
    return out


def _decay_matrix(cs_col, tri):
    colb = jnp.broadcast_to(cs_col, (Q, Q))
    return jnp.exp(jnp.where(tri, colb - colb.T, -jnp.inf))


def _ssd_fwd(xbc, dt4, cs4, vecg):
    L = xbc.shape[0]
    nc = L // Q

    def kern(x_ref, b_ref, c_ref, dt_ref, cs_ref, v_ref, y_ref, s_ref, S):
        c = pl.program_id(1)

        @pl.when(c == 0)
        def _():
            S[...] = jnp.zeros_like(S)

        masks = _head_masks()
        ii = lax.broadcasted_iota(jnp.int32, (Q, Q), 0)
        jj = lax.broadcasted_iota(jnp.int32, (Q, Q), 1)
        tri = jj <= ii
        dt4v, cs4v = dt_ref[0], cs_ref[0]
        dt_b, cs_b = _expand4(dt4v, masks), _expand4(cs4v, masks)
        d_b = _expand4(v_ref[0], masks)[1:2]
        cs_last = cs_b[Q - 1:Q, :]
        x4, bm, cm = x_ref[...], b_ref[...], c_ref[...]
        xdt = x4 * dt_b
        gm = _bdot(cm, bm, "nt")
        s4 = S[...]
        s_ref[0, 0] = s4
        y = _bdot(cm, s4) * jnp.exp(cs_b) + d_b * x4
        for j in range(4):
            mh = gm * _decay_matrix(cs4v[:, j:j + 1], tri)
            y = y + _bdot(mh, jnp.where(masks[j], xdt, 0.0))
        y_ref[...] = y
        S[...] = jnp.exp(cs_last) * s4 + _bdot(bm, xdt * jnp.exp(cs_last - cs_b), "tn")

    sc = pl.BlockSpec((1, Q, 128), lambda g, c: (g, c, 0))
    return pl.pallas_call(
        kern, grid=(NG, nc),
        in_specs=[pl.BlockSpec((Q, 256), lambda g, c: (c, g)),
                  pl.BlockSpec((Q, 128), lambda g, c: (c, INNER // 128 + g)),
                  pl.BlockSpec((Q, 128), lambda g, c: (c, (INNER + NG * NS) // 128 + g)),
                  sc, sc, pl.BlockSpec((1, 8, 128), lambda g, c: (g, 0, 0))],
        out_specs=(pl.BlockSpec((Q, 256), lambda g, c: (c, g)),
                   pl.BlockSpec((1, 1, NS, 256), lambda g, c: (g, c, 0, 0))),
        out_shape=(jax.ShapeDtypeStruct((L, INNER), F32), jax.ShapeDtypeStruct((NG, nc, NS, 256), F32)),
        scratch_shapes=[pltpu.VMEM((NS, 256), F32)], name="ssd_fwd",
        compiler_params=_cp(("parallel", "arbitrary")),
    )(xbc, xbc, xbc, dt4, cs4, vecg)


def _ssd_bwd(xbc, dt4, cs4, sg4, vecg, s_all, dy):
    L = xbc.shape[0]
    nc = L // Q

    def kern(x_ref, b_ref, c_ref, dt_ref, cs_ref, sg_ref, v_ref, s_ref, dy_ref,
             dx_ref, db_ref, dc_ref, ddt_ref, st_ref, dS):
        cc = pl.program_id(1)

        @pl.when(cc == 0)
        def _():
            dS[...] = jnp.zeros_like(dS)
            st_ref[...] = jnp.zeros_like(st_ref)

        masks = _head_masks()
        ii = lax.broadcasted_iota(jnp.int32, (Q, Q), 0)
        jj = lax.broadcasted_iota(jnp.int32, (Q, Q), 1)
        tri = jj <= ii
        utri = (jj >= ii).astype(BF16)
        li = lax.broadcasted_iota(jnp.int32, (4 * HD, 4 * HD), 0)
        lj = lax.broadcasted_iota(jnp.int32, (4 * HD, 4 * HD), 1)
        eblk = ((li // HD) == (lj // HD)).astype(BF16)
        lane128 = lax.broadcasted_iota(jnp.int32, (Q, 128), 1)

        dt4v, cs4v, sg4v = dt_ref[0], cs_ref[0], sg_ref[0]
        dt_b, cs_b = _expand4(dt4v, masks), _expand4(cs4v, masks)
        vv = _expand4(v_ref[0], masks)
        a_b = -jnp.exp(vv[0:1])
        d_b = vv[1:2]
        a4 = -jnp.exp(v_ref[0][0:1, :])
        cs_last = cs_b[Q - 1:Q, :]
        ecs = jnp.exp(cs_b)
        decay = jnp.exp(cs_last - cs_b)
        elast = jnp.exp(cs_last)
        x4, bm, cm, dyv = x_ref[...], b_ref[...], c_ref[...], dy_ref[...]
        s4 = s_ref[0, 0]
        dsn = dS[...]
        xdt = x4 * dt_b
        gm = _bdot(cm, bm, "nt")
        gmt = gm.T
        dye = dyv * ecs
        yoff = ecs * _bdot(cm, s4)
        t4 = _bdot(bm, dsn) * decay
        dxdt = t4
        dg = jnp.zeros((Q, Q), F32)
        rc = jnp.zeros((Q, 4 * HD), F32)
        for j in range(4):
            colb = jnp.broadcast_to(cs4v[:, j:j + 1], (Q, Q))
            seg = colb - colb.T
            lm = jnp.exp(jnp.where(tri, seg, -jnp.inf))
            lmt = jnp.exp(jnp.where(jj >= ii, -seg, -jnp.inf))
            mh = gm * lm
            mht = gmt * lmt
            dyh = jnp.where(masks[j], dyv, 0.0).astype(BF16)
            xh = jnp.where(masks[j], xdt, 0.0).astype(BF16)
            dxdt = dxdt + _bdot(mh, dyh, "tn")
            dmh = _bdot(dyh, xh, "nt")
            dmht = _bdot(xh, dyh, "nt")
            dg = dg + dmh * lm
            rs = jnp.sum(dmh * mh, axis=1, keepdims=True) - jnp.sum(dmht * mht, axis=1, keepdims=True)
            rc = jnp.where(masks[j], jnp.broadcast_to(rs, (Q, 4 * HD)), rc)
        xt = xdt * t4
        tail = jnp.sum(xt, axis=0, keepdims=True) + elast * jnp.sum(s4 * dsn, axis=0, keepdims=True)
        gd_raw = jnp.sum(dyv * x4, axis=0, keepdims=True)
        stacked = jnp.concatenate([dyv * yoff - xt, dxdt * x4, jnp.broadcast_to(tail, (8, 4 * HD)),
                                   jnp.broadcast_to(gd_raw, (8, 4 * HD))], axis=0)
        seg = _dot01(eblk, stacked, m_left=False, terms=2)
        da_b = seg[0:Q] + rc
        dda_b = _dot01(utri, da_b, terms=2) + seg[2 * Q:2 * Q + 1]
        ddt_b = dda_b * a_b + seg[Q:2 * Q]
        gd_b = seg[2 * Q + 8:2 * Q + 9]
        ddt4 = jnp.zeros((Q, 128), F32)
        dda4 = jnp.zeros((Q, 128), F32)
        for j in range(4):
            ddt4 = jnp.where(lane128 == j, jnp.broadcast_to(ddt_b[:, HD * j:HD * j + 1], (Q, 128)), ddt4)
            dda4 = jnp.where(lane128 == j, jnp.broadcast_to(dda_b[:, HD * j:HD * j + 1], (Q, 128)), dda4)
        ddt_ref[0] = ddt4 * sg4v
        ga = jnp.sum(dda4 * dt4v * a4, axis=0, keepdims=True)
        gd = jnp.zeros((1, 128), F32)
        for j in range(4):
            gd = jnp.where(lane128[0:1] == j, jnp.broadcast_to(gd_b[:, HD * j:HD * j + 1], (1, 128)), gd)
        row = lax.broadcasted_iota(jnp.int32, (8, 128), 0)
        st_ref[0] += jnp.where(row == 0, ga, jnp.where(row == 1, gd, 0.0))
        dx_ref[...] = d_b * dyv + dxdt * dt_b
        dc_ref[...] = _bdot(dg, bm) + _bdot(dye, s4, "nt")
        db_ref[...] = _bdot(dg, cm, "tn") + _bdot(xdt * decay, dsn, "nt")
        dS[...] = elast * dsn + _bdot(cm, dye, "tn")

    rv = lambda c: nc - 1 - c
    sc = pl.BlockSpec((1, Q, 128), lambda g, c: (g, rv(c), 0))
    return pl.pallas_call(
        kern, grid=(NG, nc),
        in_specs=[pl.BlockSpec((Q, 256), lambda g, c: (rv(c), g)),
                  pl.BlockSpec((Q, 128), lambda g, c: (rv(c), INNER // 128 + g)),
                  pl.BlockSpec((Q, 128), lambda g, c: (rv(c), (INNER + NG * NS) // 128 + g)),
                  sc, sc, sc, pl.BlockSpec((1, 8, 128), lambda g, c: (g, 0, 0)),
                  pl.BlockSpec((1, 1, NS, 256), lambda g, c: (g, rv(c), 0, 0)),
                  pl.BlockSpec((Q, 256), lambda g, c: (rv(c), g))],
        out_specs=(pl.BlockSpec((Q, 256), lambda g, c: (rv(c), g)),
                   pl.BlockSpec((Q, 128), lambda g, c: (rv(c), g)),
                   pl.BlockSpec((Q, 128), lambda g, c: (rv(c), g)),
                   pl.BlockSpec((1, Q, 128), lambda g, c: (g, rv(c), 0)),
                   pl.BlockSpec((1, 8, 128), lambda g, c: (g, 0, 0))),
        out_shape=(jax.ShapeDtypeStruct((L, INNER), F32), jax.ShapeDtypeStruct((L, NG * NS), F32),
                   jax.ShapeDtypeStruct((L, NG * NS), F32), jax.ShapeDtypeStruct((NG, L, 128), F32),
                   jax.ShapeDtypeStruct((NG, 8, 128), F32)),
        scratch_shapes=[pltpu.VMEM((NS, 256), F32)], name="ssd_bwd",
        compiler_params=_cp(("parallel", "arbitrary")),
    )(xbc, xbc, xbc, dt4, cs4, sg4, vecg, s_all, dy)


def _dt_bwd(ddt, dproj, tl=256):
    L = ddt.shape[1]

    def kern(d_ref, _, dp_ref, gs_ref):
        @pl.when(pl.program_id(0) == 0)
        def _():
            gs_ref[...] = jnp.zeros_like(gs_ref)

        d = d_ref[0]
        for g in range(1, NG):
            d = d + pltpu.roll(d_ref[g], 4 * g, axis=1)
        gs_ref[...] += jnp.broadcast_to(jnp.sum(d, axis=0, keepdims=True), (8, 128))
        dp_ref[...] = jnp.concatenate([d, jnp.zeros_like(d)], axis=1).astype(BF16)

    return pl.pallas_call(
        kern, grid=(L // tl,),
        in_specs=[pl.BlockSpec((NG, tl, 128), lambda i: (0, i, 0)), pl.BlockSpec(memory_space=pl.ANY)],
        out_specs=(pl.BlockSpec((tl, 256), lambda i: (i, C_DT // 256)), pl.BlockSpec((8, 128), lambda i: (0, 0))),
        out_shape=(jax.ShapeDtypeStruct(dproj.shape, BF16), jax.ShapeDtypeStruct((8, 128), F32)),
        input_output_aliases={1: 0}, name="dt_bwd", compiler_params=_cp(("arbitrary",)),
    )(ddt, dproj)


GW = INNER // NG


def _gnorm_fwd(y, proj, w, tl=256):
    L = y.shape[0]
    zoff = C_Z // 1024

    def kern(y_ref, z_ref, w_ref, o_ref):
        z = z_ref[...]
        yz = y_ref[...] * (z * _sigmoid(z))
        wv = w_ref[...]
        for k in range(1024 // GW):
            sl = slice(GW * k, GW * (k + 1))
            v = yz[:, sl]
            rg = lax.rsqrt(jnp.mean(v * v, axis=-1, keepdims=True) + EPS)
            o_ref[:, sl] = ((v * rg) * wv[:, sl]).astype(BF16)

    blk = pl.BlockSpec((tl, 1024), lambda i, j: (i, j))
    return pl.pallas_call(
        kern, grid=(L // tl, 2),
        in_specs=[blk, pl.BlockSpec((tl, 1024), lambda i, j: (i, zoff + j)), pl.BlockSpec((1, 1024), lambda i, j: (0, j))],
        out_specs=blk, out_shape=jax.ShapeDtypeStruct((L, INNER), BF16), name="gnorm_fwd",
        compiler_params=_cp(("parallel", "parallel")),
    )(y, proj, w.reshape(1, INNER))


def _gnorm_bwd(dyb, y, proj, w, dproj, tl=256):
    L = y.shape[0]
    zoff = C_Z // 1024

    def kern(d_ref, y_ref, z_ref, w_ref, _, dy_ref, dp_ref, gw_ref):
        @pl.when(pl.program_id(1) == 0)
        def _():
            gw_ref[...] = jnp.zeros_like(gw_ref)

        z = z_ref[...]
        sg = _sigmoid(z)
        sz = z * sg
        yv = y_ref[...]
        yz = yv * sz
        dv = d_ref[...]
        wv = w_ref[...]
        for k in range(1024 // GW):
            sl = slice(GW * k, GW * (k + 1))
            v = yz[:, sl]
            rg = lax.rsqrt(jnp.mean(v * v, axis=-1, keepdims=True) + EPS)
            vn = v * rg
            dk = dv[:, sl]
            gw_ref[:, sl] += jnp.broadcast_to(jnp.sum(dk * vn, axis=0, keepdims=True), (8, GW))
            dvn = dk * wv[:, sl]
            dyz = rg * (dvn - vn * jnp.mean(dvn * vn, axis=-1, keepdims=True))
            dy_ref[:, sl] = dyz * sz[:, sl]
            dp_ref[:, sl] = (dyz * yv[:, sl] * (sg[:, sl] * (1.0 + z[:, sl] * (1.0 - sg[:, sl])))).astype(BF16)

    blk = pl.BlockSpec((tl, 1024), lambda j, i: (i, j))
    zblk = pl.BlockSpec((tl, 1024), lambda j, i: (i, zoff + j))
    return pl.pallas_call(
        kern, grid=(2, L // tl),
        in_specs=[blk, blk, zblk, pl.BlockSpec((1, 1024), lambda j, i: (0, j)), pl.BlockSpec(memory_space=pl.ANY)],
        out_specs=(blk, zblk, pl.BlockSpec((8, 1024), lambda j, i: (0, j))),
        out_shape=(jax.ShapeDtypeStruct((L, INNER), F32), jax.ShapeDtypeStruct(dproj.shape, BF16),
                   jax.ShapeDtypeStruct((8, INNER), F32)),
        input_output_aliases={4: 1}, name="gnorm_bwd", compiler_params=_cp(("parallel", "arbitrary")),
    )(dyb, y, proj, w.reshape(1, INNER), dproj)


def _merge_fwd(proj, bg, br_a, br_b, tl=256):
    L = proj.shape[0]
    goff = C_GATE // 1024

    def kern(g1_ref, g2_ref, b1_ref, b2_ref, a_ref, b_ref, o_ref):
        g1 = _sigmoid(g1_ref[...] + b1_ref[...])
        g2 = _sigmoid(g2_ref[...] + b2_ref[...])
        o_ref[...] = (g1 * a_ref[...] + g2 * b_ref[...]).astype(BF16)

    row = pl.BlockSpec((tl, 1024), lambda i: (i, 0))
    bg2 = bg.reshape(1, 2 * D)
    return pl.pallas_call(
        kern, grid=(L // tl,),
        in_specs=[pl.BlockSpec((tl, 1024), lambda i: (i, goff)), pl.BlockSpec((tl, 1024), lambda i: (i, goff + 1)),
                  pl.BlockSpec((1, 1024), lambda i: (0, 0)), pl.BlockSpec((1, 1024), lambda i: (0, 1)), row, row],
        out_specs=row, out_shape=jax.ShapeDtypeStruct((L, D), BF16), name="merge_fwd",
        compiler_params=_cp(("parallel",)),
    )(proj, proj, bg2, bg2, br_a, br_b)


def _merge_bwd(dm, proj, bg, br_a, br_b, dproj, tl=256):
    L = proj.shape[0]
    goff = C_GATE // 1024

    def kern(dm_ref, g_ref, b_ref, a_ref, bb_ref, _, dbr_ref, dp_ref, gb_ref):
        j = pl.program_id(0)

        @pl.when(pl.program_id(1) == 0)
        def _():
            gb_ref[...] = jnp.zeros_like(gb_ref)

        g = _sigmoid(g_ref[...] + b_ref[...])
        br = jnp.where(j == 0, a_ref[...], bb_ref[...])
        dmv = dm_ref[...]
        dbr_ref[0] = (dmv * g).astype(BF16)
        dgate = dmv * br * g * (1.0 - g)
        gb_ref[...] += jnp.broadcast_to(jnp.sum(dgate, axis=0, keepdims=True), (8, 1024))
        dp_ref[...] = dgate.astype(BF16)

    row = pl.BlockSpec((tl, 1024), lambda j, i: (i, 0))
    gblk = pl.BlockSpec((tl, 1024), lambda j, i: (i, goff + j))
    return pl.pallas_call(
        kern, grid=(2, L // tl),
        in_specs=[row, gblk, pl.BlockSpec((1, 1024), lambda j, i: (0, j)), row, row, pl.BlockSpec(memory_space=pl.ANY)],
        out_specs=(pl.BlockSpec((1, tl, 1024), lambda j, i: (j, i, 0)), gblk, pl.BlockSpec((8, 1024), lambda j, i: (0, j))),
        out_shape=(jax.ShapeDtypeStruct((2, L, D), BF16), jax.ShapeDtypeStruct(dproj.shape, BF16),
                   jax.ShapeDtypeStruct((8, 2 * D), F32)),
        input_output_aliases={5: 1}, name="merge_bwd", compiler_params=_cp(("parallel", "arbitrary")),
    )(dm, proj, bg.reshape(1, 2 * D), br_a, br_b, dproj)


def _coords():
    return lax.axis_index("x"), lax.axis_index("y"), lax.axis_index("c")


def _other_chips(sk):
    xk, yk = sk // 2, sk % 2
    return [((1 - xk, yk), 2 * (1 - xk) + yk), ((xk, 1 - yk), 2 * xk + 1 - yk), ((1 - xk, 1 - yk), 2 * (1 - xk) + 1 - yk)]


def _rows(start, size):
    assert size % 128 == 0
    return pl.ds(pl.multiple_of(start, 128), size)


def _per_chip(fn):
    x, y, _ = _coords()
    s = 2 * x + y
    for sk in range(4):
        pl.when(s == sk)(functools.partial(fn, sk))


XTRA = PIECE - PMAIN


def _place(shard, full_shape, block, index_map, idx, name, blk0=0, nblk=None, dep=None):
    in_block = block[-2:]
    if nblk is None:
        nblk = shard.shape[0] // in_block[0]

    def kern(idx_ref, s_ref, *rest):
        o_ref = rest[-1]
        o_ref[...] = s_ref[...].astype(BF16).reshape(o_ref.shape)

    grid_spec = pltpu.PrefetchScalarGridSpec(
        num_scalar_prefetch=1, grid=(nblk,),
        in_specs=[pl.BlockSpec(in_block, lambda i, idx_ref: (blk0 + i, 0))] + ([_ANY] if dep is not None else []),
        out_specs=pl.BlockSpec(block, index_map))
    args = (idx, shard) + ((dep,) if dep is not None else ())
    return pl.pallas_call(kern, grid_spec=grid_spec, out_shape=jax.ShapeDtypeStruct(full_shape, BF16), name=name,
                          compiler_params=_cp(("arbitrary",)))(*args)


_SEM = pl.BlockSpec(memory_space=pltpu.SEMAPHORE)
_EFFECT = pltpu.SideEffectType.DATAFLOW_SIDE_EFFECTING


_ANY = pl.BlockSpec(memory_space=pl.ANY)


def _tie(v, dep, name):
    def body(v_ref, dep_ref, o_ref):
        del v_ref, dep_ref, o_ref

    return pl.pallas_call(body, out_shape=jax.ShapeDtypeStruct(v.shape, v.dtype), in_specs=[_ANY, _ANY],
                          out_specs=_ANY, input_output_aliases={0: 0}, name=name)(v, dep)


def _split_call(name, arrays, start=None, wait=None, wait_sems=None, after=None):
    keys = list(arrays)
    n = len(keys)
    n_start = start.n if start is not None else 0
    afters = [] if after is None else (list(after) if isinstance(after, (list, tuple)) else [after])

    def body(*refs):
        pos = n
        if wait is not None:
            wss, wrs = refs[pos], refs[pos + 1]
            pos += 2
        pos += len(afters)
        if start is not None:
            nss, nrs = refs[pos], refs[pos + 1]
            pos += 2
        R = dict(zip(keys, refs[pos:pos + n]))
        token = refs[pos + n]
        x, y, c = _coords()

        def desc(src, dst, dev, ss, rs, k):
            return pltpu.make_async_remote_copy(src_ref=src, dst_ref=dst, send_sem=ss.at[k], recv_sem=rs.at[k],
                                                device_id=dev, device_id_type=MESH)

        def run(sk):
            if wait is not None:
                for k, (snd, land) in enumerate(wait.copies(sk, R)):
                    if snd is not None:
                        desc(snd[0], snd[1], snd[2], wss, wrs, k).wait_send()
                    if land is not None:
                        desc(land, land, (x, y, c), wss, wrs, k).wait_recv()
            if start is not None:
                for k, (snd, land) in enumerate(start.copies(sk, R)):
                    if snd is not None:
                        desc(snd[0], snd[1], snd[2], nss, nrs, k).start()

        _per_chip(run)
        token[...] = jnp.zeros_like(token)

    hbm = pl.BlockSpec(memory_space=HBM)
    vals = [arrays[k] for k in keys]
    ins, in_specs = list(vals), [hbm] * n
    if wait is not None:
        ins += list(wait_sems)
        in_specs += [_SEM, _SEM]
    ins += afters
    in_specs += [pl.BlockSpec(memory_space=pl.ANY)] * len(afters)
    out_shape, out_specs = [], []
    if start is not None:
        out_shape += [pltpu.SemaphoreType.DMA((n_start,)), pltpu.SemaphoreType.DMA((n_start,))]
        out_specs += [_SEM, _SEM]
    first = len(out_shape)
    out_shape += [jax.ShapeDtypeStruct(v.shape, v.dtype) for v in vals] + [jax.ShapeDtypeStruct((8, 128), F32)]
    out_specs += [hbm] * n + [pl.BlockSpec(memory_space=pltpu.VMEM)]
    res = pl.pallas_call(
        body, out_shape=tuple(out_shape), in_specs=in_specs, out_specs=tuple(out_specs),
        input_output_aliases={i: first + i for i in range(n)}, name=name,
        compiler_params=pltpu.CompilerParams(has_side_effects=_EFFECT),
    )(*ins)
    sems = (res[0], res[1]) if start is not None else None
    return dict(zip(keys, res[first:first + n])), sems, res[-1]


class _Plan:
    def __init__(self, n, copies):
        self.n, self.copies = n, copies


_HM, _HX = PMAIN // 2, XTRA // 2
_WIN = {
    "wct": (True, lambda r, sc, hc: r.at[_rows(PMAIN * sc + _HM * hc, _HM), :]),
    "xt": (True, lambda r, sc, hc: r.at[sc, _rows(_HX * hc, _HX), :]),
    "w1": (True, lambda r, sc, hc: r.at[_rows(512 * hc, 512), pl.ds(1024 * sc, 1024)]),
    "w2": (True, lambda r, sc, hc: r.at[_rows(1024 * sc + 512 * hc, 512), :]),
    "wa": (True, lambda r, sc, hc: r.at[_rows(256 * sc + 128 * hc, 128), :]),
    "wb": (True, lambda r, sc, hc: r.at[_rows(512 * sc + 256 * hc, 256), :]),
    "wo": (True, lambda r, sc, hc: r.at[_rows(256 * sc + 128 * hc, 128), :]),
    "cw": (False, lambda r, sc, hc: r.at[sc]),
}


def _ag_chips_plan(keys):
    def copies(sk, R):
        _, _, c = _coords()
        out = []
        for key in keys:
            win = _WIN[key][1]
            for (px, py), ps in _other_chips(sk):
                w = win(R[key], sk, c)
                out.append(((w, w, (px, py, c)), win(R[key], ps, c)))
        return out
    return _Plan(3 * len(keys), copies)


def _ag_sibling_plan(keys):
    keys = [k for k in keys if _WIN[k][0]]

    def copies(sk, R):
        x, y, c = _coords()
        out = []
        for key in keys:
            win = _WIN[key][1]
            for _, ps in _other_chips(sk):
                w = win(R[key], ps, c)
                out.append(((w, w, (x, y, 1 - c)), win(R[key], ps, 1 - c)))
        return out
    return _Plan(3 * len(keys), copies)


def _fix_wct(wct, xt):
    nb = PMAIN // XTRA

    def kern(w_ref, x_ref, o_ref):
        k = pl.program_id(0)
        xv = x_ref[0]
        o_ref[...] = jnp.where(k < 3, (w_ref[...].astype(F32) + xv.astype(F32)).astype(BF16), xv)

    blk = pl.BlockSpec((XTRA, D), lambda k: (nb * (k + 1), 0))
    rblk = pl.BlockSpec((XTRA, D), lambda k: (jnp.where(k < 3, nb * (k + 1), 0), 0))
    return pl.pallas_call(
        kern, grid=(4,), in_specs=[rblk, pl.BlockSpec((1, XTRA, D), lambda k: (k, 0, 0))], out_specs=blk,
        out_shape=jax.ShapeDtypeStruct(wct.shape, BF16), input_output_aliases={0: 0}, name="fix_wct",
        compiler_params=_cp(("arbitrary",)),
    )(wct, xt)


_HP = PIECE // 2
_GWIN = [
    lambda r, sc, hc: r.at[_rows(PMAIN * sc + _HP * hc, _HP), :],
    lambda r, sc, hc: r.at[_rows(512 * hc, 512), pl.ds(1024 * sc, 1024)],
    lambda r, sc, hc: r.at[_rows(1024 * sc + 512 * hc, 512), :],
    lambda r, sc, hc: r.at[_rows(256 * sc + 128 * hc, 128), :],
    lambda r, sc, hc: r.at[_rows(512 * sc + 256 * hc, 256), :],
    lambda r, sc, hc: r.at[_rows(256 * sc + 128 * hc, 128), :],
]
HALF_SHAPES = [(PIECE // 2, D), (512, 1024), (512, 1024), (128, 1024), (256, 1024), (128, 1024)]


def _rs_sibling_plan(ts):
    def copies(sk, R):
        x, y, c = _coords()
        out = []
        for t in ts:
            for sc in range(4):
                land = R["ra%d" % t].at[sc]
                out.append(((_GWIN[t](R["g%d" % t], sc, 1 - c), land, (x, y, 1 - c)), land))
        return out
    return _Plan(4 * len(ts), copies)


def _rs_chips_plan(ts):
    def copies(sk, R):
        _, _, c = _coords()
        out = []
        for t in ts:
            for j, ((px, py), ps) in enumerate(_other_chips(sk)):
                land = R["rb%d" % t].at[j]
                out.append(((R["hb%d" % t].at[ps], land, (px, py, c)), land))
        return out
    return _Plan(3 * len(ts), copies)


def _rs_share_plan(ts):
    def copies(sk, R):
        x, y, c = _coords()
        out = []
        for t in ts:
            rows = HALF_SHAPES[t][0]
            mine = R["f%d" % t].at[_rows(rows * c, rows), :]
            out.append(((mine, mine, (x, y, 1 - c)), R["f%d" % t].at[_rows(rows * (1 - c), rows), :]))
        return out
    return _Plan(len(ts), copies)


def _half_tiling(t):
    rows, cols = HALF_SHAPES[t]
    if t == 0:
        return (256, cols), rows // 256, lambda i: (i, 0)
    if t == 1:
        return (rows, 256), cols // 256, lambda i: (0, i)
    return (rows, cols), 1, lambda i: (0, 0)


def _window_block(t, sc, hc, i):
    if t == 0:
        return (PMAIN // 256) * sc + (PIECE // 512) * hc + i, 0
    if t == 1:
        return hc, 4 * sc + i
    return 2 * sc + hc, 0


def _chip_sum(g, ra, t, idx, name):
    rows, cols = HALF_SHAPES[t]
    blk, nblk, inner = _half_tiling(t)

    def kern(idx_ref, g_ref, r_ref, hb_ref, hf_ref):
        v = g_ref[...].astype(F32) + r_ref[0].astype(F32)
        hb_ref[0] = v.astype(BF16)
        hf_ref[0] = v

    gmap = lambda sc, i, idx_ref: _window_block(t, sc, idx_ref[1], i)
    omap = lambda sc, i, idx_ref: (sc,) + inner(i)
    grid_spec = pltpu.PrefetchScalarGridSpec(
        num_scalar_prefetch=1, grid=(4, nblk),
        in_specs=[pl.BlockSpec(blk, gmap), pl.BlockSpec((1,) + blk, omap)],
        out_specs=(pl.BlockSpec((1,) + blk, omap), pl.BlockSpec((1,) + blk, omap)))
    return pl.pallas_call(
        kern, grid_spec=grid_spec,
        out_shape=(jax.ShapeDtypeStruct((4, rows, cols), BF16), jax.ShapeDtypeStruct((4, rows, cols), F32)),
        name=name, compiler_params=_cp(("parallel", "parallel")),
    )(idx, g, ra)


def _final_sum(hf, rb, t, idx, name):
    rows, cols = HALF_SHAPES[t]
    blk, nblk, inner = _half_tiling(t)
    nbr = rows // blk[0]

    def kern(idx_ref, h_ref, r_ref, o_ref):
        o_ref[...] = ((h_ref[0] + r_ref[0].astype(F32)) + r_ref[1].astype(F32)) + r_ref[2].astype(F32)

    def omap(i, idx_ref):
        r, cidx = inner(i)
        return nbr * idx_ref[1] + r, cidx

    grid_spec = pltpu.PrefetchScalarGridSpec(
        num_scalar_prefetch=1, grid=(nblk,),
        in_specs=[pl.BlockSpec((1,) + blk, lambda i, idx_ref: (idx_ref[0],) + inner(i)),
                  pl.BlockSpec((3,) + blk, lambda i, idx_ref: (0,) + inner(i))],
        out_specs=pl.BlockSpec(blk, omap))
    return pl.pallas_call(
        kern, grid_spec=grid_spec, out_shape=jax.ShapeDtypeStruct((2 * rows, cols), F32),
        name=name, compiler_params=_cp(("parallel",)),
    )(idx, hf, rb)


class _ReduceScatter:
    def __init__(self, ts, grads, idx, tag):
        self.ts, self.idx, self.tag = ts, idx, tag
        arr = {}
        for t in ts:
            arr["g%d" % t] = grads[t]
            arr["ra%d" % t] = lax.empty((4,) + HALF_SHAPES[t], BF16)
        self.plan = _rs_sibling_plan(ts)
        self.arr, self.sems, self.token = _split_call("rs_sibling_start_" + tag, arr, start=self.plan)

    def chips(self, after):
        arr, _, _ = _split_call("rs_sibling_wait_" + self.tag, self.arr, wait=self.plan, wait_sems=self.sems, after=after)
        brr, self.hf = {}, {}
        for t in self.ts:
            hb, self.hf[t] = _chip_sum(arr["g%d" % t], arr["ra%d" % t], t, self.idx, "chip_sum_%d" % t)
            brr["hb%d" % t] = hb
            brr["rb%d" % t] = lax.empty((3,) + HALF_SHAPES[t], BF16)
        self.plan = _rs_chips_plan(self.ts)
        self.arr, self.sems, self.token = _split_call("rs_chips_start_" + self.tag, brr, start=self.plan)
        return self.token

    def share(self, after):
        brr, _, _ = _split_call("rs_chips_wait_" + self.tag, self.arr, wait=self.plan, wait_sems=self.sems, after=after)
        frr = {"f%d" % t: _final_sum(self.hf[t], brr["rb%d" % t], t, self.idx, "final_sum_%d" % t) for t in self.ts}
        self.plan = _rs_share_plan(self.ts)
        self.arr, self.sems, self.token = _split_call("rs_share_start_" + self.tag, frr, start=self.plan)
        return self.token

    def result(self, after):
        frr, _, _ = _split_call("rs_share_wait_" + self.tag, self.arr, wait=self.plan, wait_sems=self.sems, after=after)
        return {t: frr["f%d" % t] for t in self.ts}


def _all8_plan(key):
    def copies(sk, R):
        x, y, c = _coords()
        own = R[key].at[4 * x + 2 * y + c]
        out = []
        for k in range(1, 8):
            dev = ((1 - x) if (k >> 2) & 1 else x, (1 - y) if (k >> 1) & 1 else y, (1 - c) if k & 1 else c)
            out.append(((own, own, dev), R[key].at[4 * dev[0] + 2 * dev[1] + dev[2]]))
        return out
    return _Plan(7, copies)


def _small_all_gather(v):
    def body(v_ref, o_ref, send_sems, recv_sems, loc_sem):
        x, y, c = _coords()
        me = 4 * x + 2 * y + c
        lc = pltpu.make_async_copy(v_ref, o_ref.at[me], loc_sem)
        lc.start()
        cps = []
        for k in range(1, 8):
            fx, fy, fc = (k >> 2) & 1, (k >> 1) & 1, k & 1
            dev = ((1 - x) if fx else x, (1 - y) if fy else y, (1 - c) if fc else c)
            cp = pltpu.make_async_remote_copy(src_ref=v_ref, dst_ref=o_ref.at[me], send_sem=send_sems.at[k - 1],
                                              recv_sem=recv_sems.at[k - 1], device_id=dev, device_id_type=MESH)
            cp.start()
            cps.append((cp, 4 * dev[0] + 2 * dev[1] + dev[2]))
        for k, (cp, frm) in enumerate(cps):
            got = o_ref.at[frm]
            pltpu.make_async_remote_copy(src_ref=got, dst_ref=got, send_sem=send_sems.at[k], recv_sem=recv_sems.at[k],
                                         device_id=(x, y, c), device_id_type=MESH).wait_recv()
        for cp, _ in cps:
            cp.wait_send()
        lc.wait()

    hbm = pl.BlockSpec(memory_space=HBM)
    return pl.pallas_call(
        body, out_shape=jax.ShapeDtypeStruct((8,) + v.shape, F32), in_specs=[hbm], out_specs=hbm,
        scratch_shapes=[pltpu.SemaphoreType.DMA((7,)), pltpu.SemaphoreType.DMA((7,)), pltpu.SemaphoreType.DMA(())],
        name="small_all_gather", compiler_params=pltpu.CompilerParams(has_side_effects=True),
    )(v)


def _sum8(v, name="small_sum"):
    def kern(v_ref, o_ref):
        acc = v_ref[0]
        for k in range(1, 8):
            acc = acc + v_ref[k]
        o_ref[...] = acc

    return pl.pallas_call(kern, out_shape=jax.ShapeDtypeStruct(v.shape[1:], F32), name=name)(v)


def _adamw(w, g, m, v, name, tr=128, blk0=0, nblk=None, into=None, copy_g=False):
    R, C = w.shape
    tr = min(tr, R)
    if nblk is None:
        assert R % tr == 0 and blk0 == 0
        nblk = R // tr
    n_out = 4 if copy_g else 3

    def kern(*refs):
        w_ref, g_ref, m_ref, v_ref = refs[:4]
        d_ref, mo_ref, vo_ref = refs[-n_out:][:3]
        gv = g_ref[...]
        mn = ADAM_B1 * m_ref[...] + (1.0 - ADAM_B1) * gv
        vn = ADAM_B2 * v_ref[...] + (1.0 - ADAM_B2) * (gv * gv)
        m_hat = mn / (1.0 - ADAM_B1 ** ADAM_STEP)
        v_hat = vn / (1.0 - ADAM_B2 ** ADAM_STEP)
        d_ref[...] = -ADAM_LR * (m_hat / (jnp.sqrt(v_hat) + ADAM_EPS) + ADAM_WD * w_ref[...])
        mo_ref[...] = mn
        vo_ref[...] = vn
        if copy_g:
            refs[-1][...] = gv

    blk = pl.BlockSpec((tr, C), lambda i: (blk0 + i, 0))
    sd = jax.ShapeDtypeStruct((R, C), F32)
    in_specs, args, aliases = [blk] * 4, [w, g, m, v], {}
    if into is not None:
        in_specs += [pl.BlockSpec(memory_space=pl.ANY)] * 3
        args += list(into)
        aliases = {4: 0, 5: 1, 6: 2}
    return pl.pallas_call(kern, grid=(nblk,), in_specs=in_specs, out_specs=(blk,) * n_out, out_shape=(sd,) * n_out,
                          input_output_aliases=aliases, name=name, compiler_params=_cp(("parallel",)))(*args)


def _to_piece(wt, s):
    z = lambda n: jnp.zeros((n, D), wt.dtype)
    pads = [functools.partial(lambda k, w: jnp.pad(w, ((8 * k, PIECE - W_SHARD - 8 * k), (0, 0))), k) for k in range(3)]
    last = lambda w: jnp.concatenate([z(24), w[:744], w[776:], w[744:776], z(PIECE - 24 - W_SHARD)], axis=0)
    return lax.switch(s, pads + [last], wt)


def _from_piece(p, s):
    cuts = [functools.partial(lambda k, q: q[8 * k:8 * k + W_SHARD], k) for k in range(3)]
    last = lambda q: jnp.concatenate([q[24:768], q[2816:2848], q[768:2816]], axis=0)
    return lax.switch(s, cuts + [last], p)


_SMALL = [("b_gate", 2048), ("ssm_conv_b", 4096), ("dt_bias", 32), ("A_log", 32), ("D_skip", 32),
          ("ssm_norm_w", 2048), ("norm_mlp", 1024), ("norm_final", 1024), ("sc_conv_w", 3072), ("ssm_conv_w", 16384),
          ("loss", 1)]


def _pack(vals, table, rows):
    parts = []
    for name, n in table:
        v = vals[name].reshape(-1).astype(F32)
        pad = (-n) % 128
        parts.append(jnp.pad(v, (0, pad)) if pad else v)
    flat = jnp.concatenate(parts)
    return jnp.pad(flat, (0, rows * 128 - flat.shape[0])).reshape(rows, 128)


def _unpack(arr, table):
    flat = arr.reshape(-1)
    out, off = {}, 0
    for name, n in table:
        out[name] = flat[off:off + n]
        off += n + ((-n) % 128)
    return out


def kernel(x, norm_mix, w_in, b_gate, sc_conv_w, ssm_conv_w, ssm_conv_b, dt_bias, A_log, D_skip, ssm_norm_w, w_branch_sc, w_branch_ssm, w_out, norm_mlp, w_mlp1, w_mlp2, norm_final, loss_target, m_norm_mix, m_w_in, m_b_gate, m_sc_conv_w, m_ssm_conv_w, m_ssm_conv_b, m_dt_bias, m_A_log, m_D_skip, m_ssm_norm_w, m_w_branch_sc, m_w_branch_ssm, m_w_out, m_norm_mlp, m_w_mlp1, m_w_mlp2, m_norm_final, v_norm_mix, v_w_in, v_b_gate, v_sc_conv_w, v_ssm_conv_w, v_ssm_conv_b, v_dt_bias, v_A_log, v_D_skip, v_ssm_norm_w, v_w_branch_sc, v_w_branch_ssm, v_w_out, v_norm_mlp, v_w_mlp1, v_w_mlp2, v_norm_final):
    L = x.shape[1]
    nc = L // Q
    xi, yi, ci = lax.axis_index("x"), lax.axis_index("y"), lax.axis_index("c")
    s = 2 * xi + yi
    idx = jnp.stack([s, ci]).astype(jnp.int32)
    x0 = x.reshape(L, D)
    tgt = loss_target.reshape(L, D)

    piece = _to_piece(w_in.T, s)
    nb = PMAIN // XTRA
    wct0 = _place(piece, (NCW, D), (XTRA, D), lambda i, r: (nb * r[0] + i, 0), idx, "place_wct", nblk=nb)
    xt0 = _place(piece, (4, XTRA, D), (1, XTRA, D), lambda i, r: (r[0], 0, 0), idx, "place_xt", blk0=nb, nblk=1)
    cws = jnp.zeros((8, 1280), F32)
    cws = cws.at[0:3, 0:256].set(sc_conv_w).at[0:4, 256:1280].set(ssm_conv_w)
    cw0 = lax.dynamic_update_slice(jnp.zeros((4, 8, 1280), F32), cws[None], (s, 0, 0))
    win_keys, mid_keys, end_keys = ["wct", "xt", "cw"], ["wa", "wb", "wo", "w1"], ["w2"]
    gw, sems_w, tok = _split_call("ag_win_start", {"wct": wct0, "xt": xt0, "cw": cw0}, start=_ag_chips_plan(win_keys))
    wa0 = _place(w_branch_sc, (D, D), (256, 1024), lambda i, r: (r[0], 0), idx, "place_wa", dep=tok)
    wb0 = _place(w_branch_ssm, (INNER, D), (512, 1024), lambda i, r: (r[0], 0), idx, "place_wb", dep=tok)
    wo0 = _place(w_out, (D, D), (256, 1024), lambda i, r: (r[0], 0), idx, "place_wo", dep=tok)
    w10 = _place(w_mlp1, (D, DFF), (256, 1024), lambda i, r: (i, r[0]), idx, "place_w1", dep=tok)
    gm, sems_m, tok = _split_call("ag_mid_start", {"wa": wa0, "wb": wb0, "wo": wo0, "w1": w10},
                                  start=_ag_chips_plan(mid_keys))
    w20 = _place(w_mlp2, (DFF, D), (256, 1024), lambda i, r: (4 * r[0] + i, 0), idx, "place_w2", dep=tok)
    ge, sems_e, tok = _split_call("ag_end_start", {"w2": w20}, start=_ag_chips_plan(end_keys))
    h = _rms_fwd(x0, norm_mix, "rms_mix", dep=tok)
    gw, sems_w, tok = _split_call("ag_win_pass", gw, wait=_ag_chips_plan(win_keys), wait_sems=sems_w,
                                  start=_ag_sibling_plan(win_keys), after=h)
    gw, _, _ = _split_call("ag_win_done", gw, wait=_ag_sibling_plan(win_keys), wait_sems=sems_w, after=tok)
    wc, cw_all = _fix_wct(gw["wct"], gw["xt"]), gw["cw"]
    sc_w_full = jnp.concatenate([cw_all[k, :, 0:256] for k in range(4)], axis=1)
    ssm_w_full = jnp.concatenate([cw_all[k, :, 256:1280] for k in range(4)], axis=1)
    cw4 = ssm_w_full.at[4].set(ssm_conv_b)
    vec = jnp.zeros((8, 128), F32).at[0, :NH].set(dt_bias).at[1, :NH].set(A_log)
    vecg = jnp.zeros((NG, 8, 128), F32).at[:, 0, :4].set(A_log.reshape(NG, 4)).at[:, 1, :4].set(D_skip.reshape(NG, 4))

    proj = _matmul(h, wc, "nt", F32, 512, 1280, 1024, "in_proj", n_outer=True)
    gm, sems_m, tok = _split_call("ag_mid_pass", gm, wait=_ag_chips_plan(mid_keys), wait_sems=sems_m,
                                  start=_ag_sibling_plan(mid_keys), after=proj)
    proj = _tie(proj, tok, "tie_proj")
    ya = _sc_fwd(proj, sc_w_full)
    xbc = _ssm_conv_fwd(proj, cw4)
    dt4, cs4, sg4 = _dt_prep(proj, vec)
    ge, sems_e, tok = _split_call("ag_end_pass", ge, wait=_ag_chips_plan(end_keys), wait_sems=sems_e,
                                  start=_ag_sibling_plan(end_keys), after=xbc)
    xbc = _tie(xbc, tok, "tie_xbc")
    y, s_all = _ssd_fwd(xbc, dt4, cs4, vecg)
    yb = _gnorm_fwd(y, proj, ssm_norm_w)
    gm, _, _ = _split_call("ag_mid_done", gm, wait=_ag_sibling_plan(mid_keys), wait_sems=sems_m, after=yb)
    ge, _, _ = _split_call("ag_end_done", ge, wait=_ag_sibling_plan(end_keys), wait_sems=sems_e, after=yb)
    wa, wb, wo, w1, w2 = gm["wa"], gm["wb"], gm["wo"], gm["w1"], ge["w2"]
    br_a = _matmul(ya, wa, "nn", F32, 512, 1024, 1024, "branch_sc")
    br_b = _matmul(yb, wb, "nn", F32, 512, 1024, 2048, "branch_ssm")
    merged = _merge_fwd(proj, b_gate, br_a, br_b)
    x1 = _matmul(merged, wo, "nn", F32, 512, 1024, 1024, "out_proj", epi="res", extra=x0)
    h2 = _rms_fwd(x1, norm_mlp, "rms_mlp")
    a1, rl = _matmul(h2, w1, "nn", F32, 512, 1024, 1024, "mlp1", epi="relu2", n_outer=True)
    x2 = _matmul(rl, w2, "nn", F32, 512, 1024, 2048, "mlp2", epi="res", extra=x1)
    dx2, g_nf, loss8 = _final(x2, norm_final, tgt)

    da = _matmul(dx2, w2, "nt", BF16, 512, 1024, 1024, "mlp2_dx", epi="drelu", extra=a1, n_outer=True)
    g_w2 = _matmul(rl, dx2, "tn", BF16, 1024, 1024, 512, "mlp2_dw")
    g_w1 = _matmul(h2, da, "tn", BF16, 1024, 1024, 512, "mlp1_dw")
    dh2 = _matmul(da, w1, "nt", F32, 512, 1024, 2048, "mlp1_dx")
    dx1, g_nmlp = _rms_bwd(dh2, x1, norm_mlp, dx2, "rms_mlp_bwd")
    dmerged = _matmul(dx1, wo, "nt", F32, 512, 1024, 1024, "out_proj_dx")
    g_wo = _matmul(merged, dx1, "tn", BF16, 1024, 1024, 512, "out_proj_dw")
    dproj = lax.empty((L, NCW), BF16)
    dbr, dproj, g_bg = _merge_bwd(dmerged, proj, b_gate, br_a, br_b, dproj)
    dya = _matmul(dbr[0], wa, "nt", F32, 512, 1024, 1024, "branch_sc_dx")
    g_wa = _matmul(ya, dbr[0], "tn", BF16, 1024, 1024, 512, "branch_sc_dw")
    dproj, g_scw = _sc_bwd(dya, proj, sc_w_full, dproj)
    dyb = _matmul(dbr[1], wb, "nt", F32, 512, 1024, 1024, "branch_ssm_dx", n_outer=True)
    g_wb = _matmul(yb, dbr[1], "tn", BF16, 1024, 1024, 512, "branch_ssm_dw")
    rs_a = _ReduceScatter([1, 2, 3, 4, 5], {1: g_w1, 2: g_w2, 3: g_wa, 4: g_wb, 5: g_wo}, idx, "a")
    dy, dproj, g_snw = _gnorm_bwd(_tie(dyb, rs_a.token, "tie_dyb"), y, proj, ssm_norm_w, dproj)
    tok = rs_a.chips(after=dy)
    dxs, dbm, dcm, ddt_g, st = _ssd_bwd(xbc, dt4, cs4, sg4, vecg, s_all, _tie(dy, tok, "tie_dy"))
    dproj, gx1 = _ssm_conv_bwd(dxs, proj, cw4, dproj, 0, "ssm_conv_bwd_x")
    dproj, gx2 = _ssm_conv_bwd(dbm, proj, cw4, dproj, INNER, "ssm_conv_bwd_b")
    dproj, gx3 = _ssm_conv_bwd(dcm, proj, cw4, dproj, INNER + NG * NS, "ssm_conv_bwd_c")
    g_cw4 = jnp.concatenate([gx1, gx2, gx3], axis=1)
    dproj, g_dtb = _dt_bwd(ddt_g, dproj)
    small = {"b_gate": g_bg[0], "ssm_conv_b": g_cw4[4], "dt_bias": g_dtb[0, :NH],
             "A_log": st[:, 0, :4], "D_skip": st[:, 1, :4], "ssm_norm_w": g_snw[0], "norm_mlp": g_nmlp[0],
             "norm_final": g_nf[0], "sc_conv_w": g_scw[0:3], "ssm_conv_w": g_cw4[0:4], "loss": loss8[0, 0:1]}
    small_sum = _sum8(_small_all_gather(_pack(small, _SMALL, SMALL_ROWS)))
    gs = _unpack(small_sum, _SMALL)
    g_wc = _matmul(dproj, h, "tn", BF16, 1280, 1024, 512, "in_proj_dw", dep=small_sum)
    rs_b = _ReduceScatter([0], {0: g_wc}, idx, "b")
    tok = rs_a.share(after=rs_b.token)
    tok = rs_b.chips(after=tok)
    dh = _matmul(dproj, wc, "nn", F32, 512, 1024, 2304, "in_proj_dx", dep=tok)
    grad_x, g_nm = _rms_bwd(dh, x0, norm_mix, dx1, "rms_mix_bwd")
    me = 4 * xi + 2 * yi + ci
    nm8 = lax.dynamic_update_slice(jnp.zeros((8, 8, 128), F32), g_nm[0].reshape(1, 8, 128), (me, 0, 0))
    nm_arr, nm_sems, tok = _split_call("norm_mix_start", {"nm": nm8}, start=_all8_plan("nm"))
    red = rs_a.result(after=tok)
    big = {"w_mlp1": red[1], "w_mlp2": red[2], "w_branch_sc": red[3], "w_branch_ssm": red[4], "w_out": red[5]}

    given = dict(norm_mix=norm_mix, w_in=w_in, b_gate=b_gate, sc_conv_w=sc_conv_w, ssm_conv_w=ssm_conv_w, ssm_conv_b=ssm_conv_b, dt_bias=dt_bias, A_log=A_log, D_skip=D_skip, ssm_norm_w=ssm_norm_w, w_branch_sc=w_branch_sc, w_branch_ssm=w_branch_ssm, w_out=w_out, norm_mlp=norm_mlp, w_mlp1=w_mlp1, w_mlp2=w_mlp2, norm_final=norm_final,
                 m_norm_mix=m_norm_mix, m_w_in=m_w_in, m_b_gate=m_b_gate, m_sc_conv_w=m_sc_conv_w, m_ssm_conv_w=m_ssm_conv_w, m_ssm_conv_b=m_ssm_conv_b, m_dt_bias=m_dt_bias, m_A_log=m_A_log, m_D_skip=m_D_skip, m_ssm_norm_w=m_ssm_norm_w, m_w_branch_sc=m_w_branch_sc, m_w_branch_ssm=m_w_branch_ssm, m_w_out=m_w_out, m_norm_mlp=m_norm_mlp, m_w_mlp1=m_w_mlp1, m_w_mlp2=m_w_mlp2, m_norm_final=m_norm_final,
                 v_norm_mix=v_norm_mix, v_w_in=v_w_in, v_b_gate=v_b_gate, v_sc_conv_w=v_sc_conv_w, v_ssm_conv_w=v_ssm_conv_w, v_ssm_conv_b=v_ssm_conv_b, v_dt_bias=v_dt_bias, v_A_log=v_A_log, v_D_skip=v_D_skip, v_ssm_norm_w=v_ssm_norm_w, v_w_branch_sc=v_w_branch_sc, v_w_branch_ssm=v_w_branch_ssm, v_w_out=v_w_out, v_norm_mlp=v_norm_mlp, v_w_mlp1=v_w_mlp1, v_w_mlp2=v_w_mlp2, v_norm_final=v_norm_final)
    order = ["norm_mix", "w_in", "b_gate", "sc_conv_w", "ssm_conv_w", "ssm_conv_b", "dt_bias", "A_log", "D_skip",
             "ssm_norm_w", "w_branch_sc", "w_branch_ssm", "w_out", "norm_mlp", "w_mlp1", "w_mlp2", "norm_final"]
    grad, delta, new_m, new_v = {}, {}, {}, {}
    for n in big:
        delta[n], new_m[n], new_v[n], grad[n] = _adamw(given[n], big[n], given["m_" + n], given["v_" + n],
                                                       "adamw_" + n, copy_g=True)
    big["w_in"] = None
    grad_small = {n: gs[n].reshape(given[n].shape) for n in order
                  if n not in big and n not in ("sc_conv_w", "ssm_conv_w", "norm_mix")}
    grad_small["sc_conv_w"] = lax.dynamic_slice(gs["sc_conv_w"].reshape(3, D), (0, 256 * s), (3, 256))
    grad_small["ssm_conv_w"] = lax.dynamic_slice(gs["ssm_conv_w"].reshape(4, XBC), (0, 1024 * s), (4, 1024))
    table = [(n, int(grad_small[n].size)) for n in grad_small]
    rows = 136
    pk = lambda d: _pack(d, table, rows)
    ds_, ms_, vs_ = _adamw(pk({n: given[n] for n in grad_small}), pk(grad_small), pk({n: given["m_" + n] for n in grad_small}),
                           pk({n: given["v_" + n] for n in grad_small}), "adamw_small", tr=rows)
    ds_, ms_, vs_ = _unpack(ds_, table), _unpack(ms_, table), _unpack(vs_, table)
    for n in grad_small:
        shp = given[n].shape
        grad[n] = grad_small[n]
        delta[n], new_m[n], new_v[n] = ds_[n].reshape(shp), ms_[n].reshape(shp), vs_[n].reshape(shp)

    done = [new_v[n] for n in ("w_mlp1", "w_mlp2", "w_branch_sc", "w_branch_ssm", "w_out")] + [vs_["b_gate"]]
    tok = rs_b.share(after=done)
    gp = rs_b.result(after=tok)[0]
    gwt = _from_piece(gp, s)
    wt_args = (w_in.T, gwt, m_w_in.T, v_w_in.T)
    head = _adamw(*wt_args, "adamw_w_in", tr=256, nblk=W_SHARD // 256)
    dt_, mt_, vt_ = _adamw(*wt_args, "adamw_w_in_tail", tr=8, blk0=(W_SHARD // 256) * 32, nblk=1, into=head)
    grad["w_in"], delta["w_in"], new_m["w_in"], new_v["w_in"] = gwt.T, dt_.T, mt_.T, vt_.T
    nm_arr, _, _ = _split_call("norm_mix_wait", nm_arr, wait=_all8_plan("nm"), wait_sems=nm_sems, after=tok)
    g8 = _sum8(nm_arr["nm"], "norm_mix_sum")
    r8 = lambda a: a.reshape(8, 128)
    d8, m8, v8 = _adamw(r8(norm_mix), g8, r8(m_norm_mix), r8(v_norm_mix), "adamw_norm_mix", tr=8)
    grad["norm_mix"], delta["norm_mix"] = g8.reshape(D), d8.reshape(D)
    new_m["norm_mix"], new_v["norm_mix"] = m8.reshape(D), v8.reshape(D)

    loss = gs["loss"].reshape(())
    return (loss, grad_x.reshape(1, L, D), *[grad[n] for n in order], *[delta[n] for n in order],
            *[new_m[n] for n in order], *[new_v[n] for n in order])
```

```python
import functools

import jax
import jax.numpy as jnp
from jax import lax
from jax.experimental import pallas as pl
from jax.experimental.pallas import tpu as pltpu

F32 = jnp.float32
BF16 = jnp.bfloat16
MESH = pl.DeviceIdType.MESH
HBM = pltpu.HBM

D = 1024
INNER = 2048
HD = 64
NH = 32
NG = 8
NS = 128
Q = 128
GPS = 2
XBC = 4096
DFF = 4096
EPS = 1e-6
W_SHARD = 2824
NCW = 11520
PIECE = 3072
PMAIN = 2816
C_Z, C_XBC, C_GATE, C_DT = 3072, 5120, 9216, 11264
SMALL_ROWS = 256
VMEM_LIMIT = 56 * 1024 * 1024

ADAM_LR, ADAM_B1, ADAM_B2, ADAM_EPS, ADAM_WD, ADAM_STEP = 0.001, 0.9, 0.999, 1e-08, 0.01, 10


def _cp(sem=None, vmem=VMEM_LIMIT):
    return pltpu.CompilerParams(dimension_semantics=sem, vmem_limit_bytes=vmem)


def _sigmoid(v):
    return 1.0 / (1.0 + jnp.exp(-v))


_DIMS = {"nn": (((1,), (0,)), ((), ())), "nt": (((1,), (1,)), ((), ())), "tn": (((0,), (0,)), ((), ()))}


def _matmul(a, b, mode, out_dtype, tm, tn, tk, name, epi=None, extra=None, n_outer=False, dep=None):
    if mode == "tn":
        K, M = a.shape
    else:
        M, K = a.shape
    N = b.shape[0] if mode == "nt" else b.shape[1]
    tm, tn, tk = min(tm, M), min(tn, N), min(tk, K)
    assert M % tm == 0 and N % tn == 0 and K % tk == 0, (name, M, N, K, tm, tn, tk)
    nm, nn, nk = M // tm, N // tn, K // tk
    dims = _DIMS[mode]

    def ij(p0, p1):
        return (p1, p0) if n_outer else (p0, p1)

    if mode == "tn":
        a_spec = pl.BlockSpec((tk, tm), lambda p0, p1, k: (k, ij(p0, p1)[0]))
    else:
        a_spec = pl.BlockSpec((tm, tk), lambda p0, p1, k: (ij(p0, p1)[0], k))
    if mode == "nt":
        b_spec = pl.BlockSpec((tn, tk), lambda p0, p1, k: (ij(p0, p1)[1], k))
    else:
        b_spec = pl.BlockSpec((tk, tn), lambda p0, p1, k: (k, ij(p0, p1)[1]))
    o_spec = pl.BlockSpec((tm, tn), lambda p0, p1, k: ij(p0, p1))
    in_specs = [a_spec, b_spec]
    args = [a, b]
    if epi in ("res", "drelu"):
        in_specs.append(o_spec)
        args.append(extra)
    if dep is not None:
        in_specs.append(pl.BlockSpec(memory_space=pl.ANY))
        args.append(dep)
    n_in = len(args)
    if epi == "relu2":
        out_shape = (jax.ShapeDtypeStruct((M, N), F32), jax.ShapeDtypeStruct((M, N), BF16))
        out_specs = (o_spec, o_spec)
    else:
        out_shape = jax.ShapeDtypeStruct((M, N), out_dtype)
        out_specs = o_spec

    def kern(*refs):
        a_ref, b_ref = refs[0], refs[1]
        e_ref = refs[2] if epi in ("res", "drelu") else None
        acc = refs[-1]
        outs = refs[n_in:-1]
        k = pl.program_id(2)

        @pl.when(k == 0)
        def _():
            acc[...] = jnp.zeros_like(acc)

        acc[...] += lax.dot_general(a_ref[...].astype(BF16), b_ref[...].astype(BF16), dims,
                                    preferred_element_type=F32)

        @pl.when(k == nk - 1)
        def _():
            r = acc[...]
            if epi is None:
                outs[0][...] = r.astype(out_dtype)
            elif epi == "res":
                outs[0][...] = (r + e_ref[...]).astype(out_dtype)
            elif epi == "relu2":
                outs[0][...] = r
                t = jnp.maximum(r, 0.0)
                outs[1][...] = (t * t).astype(BF16)
            else:
                outs[0][...] = (r * (2.0 * jnp.maximum(e_ref[...], 0.0))).astype(out_dtype)

    grid = (nn, nm, nk) if n_outer else (nm, nn, nk)
    return pl.pallas_call(
        kern, grid=grid, in_specs=in_specs, out_specs=out_specs, out_shape=out_shape,
        scratch_shapes=[pltpu.VMEM((tm, tn), F32)], name=name,
        compiler_params=_cp(("parallel", "parallel", "arbitrary")),
    )(*args)


def _rms_fwd(x, w, name, tl=256, dep=None):
    L = x.shape[0]

    def kern(x_ref, w_ref, *rest):
        o_ref = rest[-1]
        xv = x_ref[...]
        r = lax.rsqrt(jnp.mean(xv * xv, axis=-1, keepdims=True) + EPS)
        o_ref[...] = ((xv * r) * w_ref[...]).astype(BF16)

    row = pl.BlockSpec((tl, D), lambda i: (i, 0))
    deps = [] if dep is None else [dep]
    return pl.pallas_call(
        kern, grid=(L // tl,),
        in_specs=[row, pl.BlockSpec((1, D), lambda i: (0, 0))] + [pl.BlockSpec(memory_space=pl.ANY)] * len(deps),
        out_specs=row, out_shape=jax.ShapeDtypeStruct((L, D), BF16), name=name, compiler_params=_cp(("parallel",)),
    )(x, w.reshape(1, D), *deps)


def _rms_bwd(dy, x, w, res, name, tl=256, dep=None):
    L = x.shape[0]
    deps = [] if dep is None else [dep]

    def kern(dy_ref, x_ref, w_ref, res_ref, *rest):
        dx_ref, gw_ref = rest[-2:]
        @pl.when(pl.program_id(0) == 0)
        def _():
            gw_ref[...] = jnp.zeros_like(gw_ref)

        xv = x_ref[...]
        dyv = dy_ref[...]
        r = lax.rsqrt(jnp.mean(xv * xv, axis=-1, keepdims=True) + EPS)
        xn = xv * r
        gw_ref[...] += jnp.broadcast_to(jnp.sum(dyv * xn, axis=0, keepdims=True), (8, D))
        dxn = dyv * w_ref[...]
        dx_ref[...] = res_ref[...] + r * (dxn - xn * jnp.mean(dxn * xn, axis=-1, keepdims=True))

    row = pl.BlockSpec((tl, D), lambda i: (i, 0))
    return pl.pallas_call(
        kern, grid=(L // tl,),
        in_specs=[row, row, pl.BlockSpec((1, D), lambda i: (0, 0)), row] + [pl.BlockSpec(memory_space=pl.ANY)] * len(deps),
        out_specs=(row, pl.BlockSpec((8, D), lambda i: (0, 0))),
        out_shape=(jax.ShapeDtypeStruct((L, D), F32), jax.ShapeDtypeStruct((8, D), F32)),
        name=name, compiler_params=_cp(("arbitrary",)),
    )(dy, x, w.reshape(1, D), res, *deps)


def _final(x2, w, tgt, tl=256):
    L = x2.shape[0]

    def kern(x_ref, w_ref, t_ref, dx_ref, gw_ref, loss_ref):
        @pl.when(pl.program_id(0) == 0)
        def _():
            gw_ref[...] = jnp.zeros_like(gw_ref)
            loss_ref[...] = jnp.zeros_like(loss_ref)

        xv = x_ref[...]
        r = lax.rsqrt(jnp.mean(xv * xv, axis=-1, keepdims=True) + EPS)
        xn = xv * r
        e = xn * w_ref[...] - t_ref[...]
        per_tok = jnp.mean(e * e, axis=-1, keepdims=True)
        loss_ref[...] += 0.5 * jnp.sum(per_tok)
        dyv = e * (1.0 / D)
        gw_ref[...] += jnp.broadcast_to(jnp.sum(dyv * xn, axis=0, keepdims=True), (8, D))
        dxn = dyv * w_ref[...]
        dx_ref[...] = r * (dxn - xn * jnp.mean(dxn * xn, axis=-1, keepdims=True))

    row = pl.BlockSpec((tl, D), lambda i: (i, 0))
    return pl.pallas_call(
        kern, grid=(L // tl,), in_specs=[row, pl.BlockSpec((1, D), lambda i: (0, 0)), row],
        out_specs=(row, pl.BlockSpec((8, D), lambda i: (0, 0)), pl.BlockSpec((8, 128), lambda i: (0, 0))),
        out_shape=(jax.ShapeDtypeStruct((L, D), F32), jax.ShapeDtypeStruct((8, D), F32),
                   jax.ShapeDtypeStruct((8, 128), F32)),
        name="final_norm_loss", compiler_params=_cp(("arbitrary",)),
    )(x2, w.reshape(1, D), tgt)


def _down(v, k):
    if k == 0:
        return v
    t = lax.broadcasted_iota(jnp.int32, v.shape, 0)
    return jnp.where(t >= k, pltpu.roll(v, k, axis=0), 0.0)


def _up(v, k):
    if k == 0:
        return v
    n = v.shape[0]
    t = lax.broadcasted_iota(jnp.int32, v.shape, 0)
    return jnp.where(t < n - k, pltpu.roll(v, n - k, axis=0), 0.0)


TW = 256


def _sc_fwd(proj, cw):
    L = proj.shape[0]
    nb = D // TW

    def kern(b_ref, c_ref, x_ref, w_ref, o_ref):
        u = c_ref[...] * x_ref[...]
        w = w_ref[...]
        cv = w[0:1] * _down(u, 2) + w[1:2] * _down(u, 1) + w[2:3] * u
        o_ref[...] = (b_ref[...] * cv).astype(BF16)

    col = lambda off: pl.BlockSpec((L, TW), lambda j: (0, off + j))
    return pl.pallas_call(
        kern, grid=(nb,), in_specs=[col(0), col(nb), col(2 * nb), pl.BlockSpec((8, TW), lambda j: (0, j))],
        out_specs=pl.BlockSpec((L, TW), lambda j: (0, j)), out_shape=jax.ShapeDtypeStruct((L, D), BF16),
        name="sc_fwd", compiler_params=_cp(("parallel",)),
    )(proj, proj, proj, cw)


def _sc_bwd(dya, proj, cw, dproj):
    L = proj.shape[0]
    nb = D // TW

    def kern(d_ref, b_ref, c_ref, x_ref, w_ref, _, dp_ref, gw_ref, keep):
        sec = pl.program_id(1)

        @pl.when(sec == 0)
        def _():
            cs, xs, dyv = c_ref[...], x_ref[...], d_ref[...]
            w = w_ref[...]
            u = cs * xs
            u1, u2 = _down(u, 1), _down(u, 2)
            cv = w[0:1] * u2 + w[1:2] * u1 + w[2:3] * u
            dcv = dyv * b_ref[...]
            du = w[2:3] * dcv + w[1:2] * _up(dcv, 1) + w[0:1] * _up(dcv, 2)
            g0 = jnp.sum(dcv * u2, axis=0, keepdims=True)
            g1 = jnp.sum(dcv * u1, axis=0, keepdims=True)
            g2 = jnp.sum(dcv * u, axis=0, keepdims=True)
            row = lax.broadcasted_iota(jnp.int32, (8, TW), 0)
            gw_ref[...] = jnp.where(row == 0, g0, jnp.where(row == 1, g1, jnp.where(row == 2, g2, 0.0)))
            dp_ref[...] = (dyv * cv).astype(BF16)
            keep[0] = (du * xs).astype(BF16)
            keep[1] = (du * cs).astype(BF16)

        @pl.when(sec > 0)
        def _():
            dp_ref[...] = keep[sec - 1]

    col = lambda off: pl.BlockSpec((L, TW), lambda j, s: (0, off + j))
    return pl.pallas_call(
        kern, grid=(nb, 3),
        in_specs=[col(0), col(0), col(nb), col(2 * nb), pl.BlockSpec((8, TW), lambda j, s: (0, j)),
                  pl.BlockSpec(memory_space=pl.ANY)],
        out_specs=(pl.BlockSpec((L, TW), lambda j, s: (0, s * nb + j)), pl.BlockSpec((8, TW), lambda j, s: (0, j))),
        out_shape=(jax.ShapeDtypeStruct(dproj.shape, BF16), jax.ShapeDtypeStruct((8, D), F32)),
        scratch_shapes=[pltpu.VMEM((2, L, TW), BF16)],
        input_output_aliases={5: 0}, name="sc_bwd", compiler_params=_cp(("parallel", "arbitrary")),
    )(dya, proj, proj, proj, cw, dproj)


def _ssm_conv_fwd(proj, cw4):
    L = proj.shape[0]
    off = C_XBC // TW

    def kern(r_ref, w_ref, o_ref):
        raw = r_ref[...]
        w = w_ref[...]
        c4 = w[0:1] * _down(raw, 3) + w[1:2] * _down(raw, 2) + w[2:3] * _down(raw, 1) + w[3:4] * raw + w[4:5]
        o_ref[...] = c4 * _sigmoid(c4)

    return pl.pallas_call(
        kern, grid=(XBC // TW,),
        in_specs=[pl.BlockSpec((L, TW), lambda j: (0, off + j)), pl.BlockSpec((8, TW), lambda j: (0, j))],
        out_specs=pl.BlockSpec((L, TW), lambda j: (0, j)), out_shape=jax.ShapeDtypeStruct((L, XBC), F32),
        name="ssm_conv_fwd", compiler_params=_cp(("parallel",)),
    )(proj, cw4)


def _ssm_conv_bwd(dx, proj, cw4, dproj, col0, name):
    L, width = dx.shape
    off_p = (C_XBC + col0) // TW
    off_w = col0 // TW

    def kern(d_ref, r_ref, w_ref, _, dp_ref, gw_ref):
        raw = r_ref[...]
        w = w_ref[...]
        r1, r2, r3 = _down(raw, 1), _down(raw, 2), _down(raw, 3)
        c4 = w[0:1] * r3 + w[1:2] * r2 + w[2:3] * r1 + w[3:4] * raw + w[4:5]
        sg = _sigmoid(c4)
        dc4 = d_ref[...] * (sg * (1.0 + c4 * (1.0 - sg)))
        draw = w[3:4] * dc4 + w[2:3] * _up(dc4, 1) + w[1:2] * _up(dc4, 2) + w[0:1] * _up(dc4, 3)
        dp_ref[...] = draw.astype(BF16)
        gs = [jnp.sum(dc4 * r3, axis=0, keepdims=True), jnp.sum(dc4 * r2, axis=0, keepdims=True),
              jnp.sum(dc4 * r1, axis=0, keepdims=True), jnp.sum(dc4 * raw, axis=0, keepdims=True),
              jnp.sum(dc4, axis=0, keepdims=True)]
        row = lax.broadcasted_iota(jnp.int32, (8, TW), 0)
        acc = jnp.zeros((8, TW), F32)
        for k, gk in enumerate(gs):
            acc = jnp.where(row == k, gk, acc)
        gw_ref[...] = acc

    return pl.pallas_call(
        kern, grid=(width // TW,),
        in_specs=[pl.BlockSpec((L, TW), lambda j: (0, j)), pl.BlockSpec((L, TW), lambda j: (0, off_p + j)),
                  pl.BlockSpec((8, TW), lambda j: (0, off_w + j)), pl.BlockSpec(memory_space=pl.ANY)],
        out_specs=(pl.BlockSpec((L, TW), lambda j: (0, off_p + j)), pl.BlockSpec((8, TW), lambda j: (0, j))),
        out_shape=(jax.ShapeDtypeStruct(dproj.shape, BF16), jax.ShapeDtypeStruct((8, width), F32)),
        input_output_aliases={3: 0}, name=name, compiler_params=_cp(("arbitrary",)),
    )(dx, proj, cw4, dproj)


def _split3(v):
    h1 = v.astype(BF16)
    r1 = v - h1.astype(F32)
    h2 = r1.astype(BF16)
    h3 = (r1 - h2.astype(F32)).astype(BF16)
    return h1, h2, h3


def _dot01(m01, v, dims=_DIMS["nn"], m_left=True, terms=3):
    out = None
    for part in _split3(v)[:terms]:
        ops = (m01, part) if m_left else (part, m01)
        t = lax.dot_general(ops[0], ops[1], dims, preferred_element_type=F32)
        out = t if out is None else out + t
    return out


def _bdot(a, b, mode="nn"):
    return lax.dot_general(a.astype(BF16), b.astype(BF16), _DIMS[mode], preferred_element_type=F32)


def _softplus(v):
    return jnp.maximum(v, 0.0) + jnp.log1p(jnp.exp(-jnp.abs(v)))


def _dt_prep(proj, vec):
    L = proj.shape[0]

    def kern(p_ref, v_ref, dt_ref, cs_ref, sg_ref):
        v = v_ref[...]
        pre = p_ref[:, 0:128] + v[0:1]
        dt = _softplus(pre)
        da = dt * (-jnp.exp(v[1:2]))
        ii = lax.broadcasted_iota(jnp.int32, (Q, Q), 0)
        jj = lax.broadcasted_iota(jnp.int32, (Q, Q), 1)
        ltri = (jj <= ii).astype(BF16)
        lane = lax.broadcasted_iota(jnp.int32, (Q, 128), 1)
        for val, ref in ((dt, dt_ref), (_dot01(ltri, da), cs_ref), (_sigmoid(pre), sg_ref)):
            for g in range(NG):
                moved = val if g == 0 else pltpu.roll(val, 128 - 4 * g, axis=1)
                ref[g] = jnp.where(lane < 4, moved, 0.0)

    blk = pl.BlockSpec((NG, Q, 128), lambda c: (0, c, 0))
    return pl.pallas_call(
        kern, grid=(L // Q,),
        in_specs=[pl.BlockSpec((Q, 256), lambda c: (c, C_DT // 256)), pl.BlockSpec((8, 128), lambda c: (0, 0))],
        out_specs=(blk, blk, blk),
        out_shape=(jax.ShapeDtypeStruct((NG, L, 128), F32),) * 3,
        name="dt_prep", compiler_params=_cp(("parallel",)),
    )(proj, vec)


def _head_masks():
    lane = lax.broadcasted_iota(jnp.int32, (1, 4 * HD), 1)
    return [((lane >= HD * j) & (lane < HD * (j + 1))) for j in range(4)]


def _expand4(v4, masks):
    R = v4.shape[0]
    out = jnp.zeros((R, 4 * HD), F32)
    for j in range(4):
        out = jnp.where(masks[j], jnp.broadcast_to(v4[:, j:j + 1], (R, 4 * HD)), out)
    return out


def _decay_matrix(cs_col, tri):
    colb = jnp.broadcast_to(cs_col, (Q, Q))
    return jnp.exp(jnp.where(tri, colb - colb.T, -jnp.inf))


def _ssd_fwd(xbc, dt4, cs4, vecg):
    L = xbc.shape[0]
    nc = L // Q

    def kern(x_ref, b_ref, c_ref, dt_ref, cs_ref, v_ref, y_ref, s_ref, S):
        c = pl.program_id(1)

        @pl.when(c == 0)
        def _():
            S[...] = jnp.zeros_like(S)

        masks = _head_masks()
        ii = lax.broadcasted_iota(jnp.int32, (Q, Q), 0)
        jj = lax.broadcasted_iota(jnp.int32, (Q, Q), 1)
        tri = jj <= ii
        for gi in range(GPS):
            xs, ns = slice(256 * gi, 256 * (gi + 1)), slice(NS * gi, NS * (gi + 1))
            dt4v, cs4v = dt_ref[gi], cs_ref[gi]
            dt_b, cs_b = _expand4(dt4v, masks), _expand4(cs4v, masks)
            d_b = _expand4(v_ref[gi], masks)[1:2]
            cs_last = cs_b[Q - 1:Q, :]
            x4, bm, cm = x_ref[:, xs], b_ref[:, ns], c_ref[:, ns]
            xdt = x4 * dt_b
            gm = _bdot(cm, bm, "nt")
            s4 = S[gi]
            s_ref[gi, 0] = s4
            y = _bdot(cm, s4) * jnp.exp(cs_b) + d_b * x4
            m_all = jnp.concatenate([(gm * _decay_matrix(cs4v[:, j:j + 1], tri)).astype(BF16) for j in range(4)], axis=0)
            yd = _bdot(m_all, xdt)
            for j in range(4):
                y = y + jnp.where(masks[j], yd[Q * j:Q * (j + 1)], 0.0)
            y_ref[:, xs] = y
            S[gi] = jnp.exp(cs_last) * s4 + _bdot(bm, xdt * jnp.exp(cs_last - cs_b), "tn")

    sc = pl.BlockSpec((GPS, Q, 128), lambda g, c: (g, c, 0))
    bw = NS * GPS
    return pl.pallas_call(
        kern, grid=(NG // GPS, nc),
        in_specs=[pl.BlockSpec((Q, 256 * GPS), lambda g, c: (c, g)),
                  pl.BlockSpec((Q, bw), lambda g, c: (c, INNER // bw + g)),
                  pl.BlockSpec((Q, bw), lambda g, c: (c, (INNER + NG * NS) // bw + g)),
                  sc, sc, pl.BlockSpec((GPS, 8, 128), lambda g, c: (g, 0, 0))],
        out_specs=(pl.BlockSpec((Q, 256 * GPS), lambda g, c: (c, g)),
                   pl.BlockSpec((GPS, 1, NS, 256), lambda g, c: (g, c, 0, 0))),
        out_shape=(jax.ShapeDtypeStruct((L, INNER), F32), jax.ShapeDtypeStruct((NG, nc, NS, 256), F32)),
        scratch_shapes=[pltpu.VMEM((GPS, NS, 256), F32)], name="ssd_fwd",
        compiler_params=_cp(("parallel", "arbitrary")),
    )(xbc, xbc, xbc, dt4, cs4, vecg)


def _ssd_bwd(xbc, dt4, cs4, sg4, vecg, s_all, dy):
    L = xbc.shape[0]
    nc = L // Q

    def kern(x_ref, b_ref, c_ref, dt_ref, cs_ref, sg_ref, v_ref, s_ref, dy_ref,
             dx_ref, db_ref, dc_ref, ddt_ref, st_ref, dS):
        cc = pl.program_id(1)

        @pl.when(cc == 0)
        def _():
            dS[...] = jnp.zeros_like(dS)
            st_ref[...] = jnp.zeros_like(st_ref)

        masks = _head_masks()
        ii = lax.broadcasted_iota(jnp.int32, (Q, Q), 0)
        jj = lax.broadcasted_iota(jnp.int32, (Q, Q), 1)
        tri = jj <= ii
        utri = (jj >= ii).astype(BF16)
        li = lax.broadcasted_iota(jnp.int32, (4 * HD, 4 * HD), 0)
        lj = lax.broadcasted_iota(jnp.int32, (4 * HD, 4 * HD), 1)
        eblk = ((li // HD) == (lj // HD)).astype(BF16)
        lane128 = lax.broadcasted_iota(jnp.int32, (Q, 128), 1)

        for gi in range(GPS):
            xs, ns = slice(256 * gi, 256 * (gi + 1)), slice(NS * gi, NS * (gi + 1))
            dt4v, cs4v, sg4v = dt_ref[gi], cs_ref[gi], sg_ref[gi]
            dt_b, cs_b = _expand4(dt4v, masks), _expand4(cs4v, masks)
            vv = _expand4(v_ref[gi], masks)
            a_b = -jnp.exp(vv[0:1])
            d_b = vv[1:2]
            a4 = -jnp.exp(v_ref[gi][0:1, :])
            cs_last = cs_b[Q - 1:Q, :]
            ecs = jnp.exp(cs_b)
            decay = jnp.exp(cs_last - cs_b)
            elast = jnp.exp(cs_last)
            x4, bm, cm, dyv = x_ref[:, xs], b_ref[:, ns], c_ref[:, ns], dy_ref[:, xs]
            s4 = s_ref[gi, 0]
            dsn = dS[gi]
            xdt = x4 * dt_b
            gm = _bdot(cm, bm, "nt")
            gmt = gm.T
            dye = dyv * ecs
            yoff = ecs * _bdot(cm, s4)
            t4 = _bdot(bm, dsn) * decay
            lms, mhs, mhts = [], [], []
            for j in range(4):
                colb = jnp.broadcast_to(cs4v[:, j:j + 1], (Q, Q))
                seg = colb - colb.T
                lms.append(jnp.exp(jnp.where(tri, seg, -jnp.inf)))
                mhs.append(gm * lms[j])
                mhts.append(gmt * jnp.exp(jnp.where(jj >= ii, -seg, -jnp.inf)))
            m_all = jnp.concatenate([m.astype(BF16) for m in mhs], axis=0)
            dy_m = jnp.concatenate([jnp.where(masks[j], dyv, 0.0).astype(BF16) for j in range(4)], axis=0)
            x_m = jnp.concatenate([jnp.where(masks[j], xdt, 0.0).astype(BF16) for j in range(4)], axis=0)
            dxdt = t4 + _bdot(m_all, dy_m, "tn")
            dm_all = _bdot(dy_m, xdt, "nt")
            dmt_all = _bdot(x_m, dyv, "nt")
            dg = jnp.zeros((Q, Q), F32)
            rc = jnp.zeros((Q, 4 * HD), F32)
            for j in range(4):
                dmh = dm_all[Q * j:Q * (j + 1)]
                dg = dg + dmh * lms[j]
                rs = (jnp.sum(dmh * mhs[j], axis=1, keepdims=True)
                      - jnp.sum(dmt_all[Q * j:Q * (j + 1)] * mhts[j], axis=1, keepdims=True))
                rc = jnp.where(masks[j], jnp.broadcast_to(rs, (Q, 4 * HD)), rc)
            xt = xdt * t4
            tail = jnp.sum(xt, axis=0, keepdims=True) + elast * jnp.sum(s4 * dsn, axis=0, keepdims=True)
            gd_raw = jnp.sum(dyv * x4, axis=0, keepdims=True)
            stacked = jnp.concatenate([dyv * yoff - xt, dxdt * x4, jnp.broadcast_to(tail, (8, 4 * HD)),
                                       jnp.broadcast_to(gd_raw, (8, 4 * HD))], axis=0)
            seg = _dot01(eblk, stacked, m_left=False, terms=2)
            da_b = seg[0:Q] + rc
            dda_b = _dot01(utri, da_b, terms=2) + seg[2 * Q:2 * Q + 1]
            ddt_b = dda_b * a_b + seg[Q:2 * Q]
            gd_b = seg[2 * Q + 8:2 * Q + 9]
            ddt4 = jnp.zeros((Q, 128), F32)
            dda4 = jnp.zeros((Q, 128), F32)
            for j in range(4):
                ddt4 = jnp.where(lane128 == j, jnp.broadcast_to(ddt_b[:, HD * j:HD * j + 1], (Q, 128)), ddt4)
                dda4 = jnp.where(lane128 == j, jnp.broadcast_to(dda_b[:, HD * j:HD * j + 1], (Q, 128)), dda4)
            ddt_ref[gi] = ddt4 * sg4v
            ga = jnp.sum(dda4 * dt4v * a4, axis=0, keepdims=True)
            gd = jnp.zeros((1, 128), F32)
            for j in range(4):
                gd = jnp.where(lane128[0:1] == j, jnp.broadcast_to(gd_b[:, HD * j:HD * j + 1], (1, 128)), gd)
            row = lax.broadcasted_iota(jnp.int32, (8, 128), 0)
            st_ref[gi] += jnp.where(row == 0, ga, jnp.where(row == 1, gd, 0.0))
            dx_ref[:, xs] = d_b * dyv + dxdt * dt_b
            dc_ref[:, ns] = _bdot(dg, bm) + _bdot(dye, s4, "nt")
            db_ref[:, ns] = _bdot(dg, cm, "tn") + _bdot(xdt * decay, dsn, "nt")
            dS[gi] = elast * dsn + _bdot(cm, dye, "tn")

    rv = lambda c: nc - 1 - c
    sc = pl.BlockSpec((GPS, Q, 128), lambda g, c: (g, rv(c), 0))
    bw = NS * GPS
    return pl.pallas_call(
        kern, grid=(NG // GPS, nc),
        in_specs=[pl.BlockSpec((Q, 256 * GPS), lambda g, c: (rv(c), g)),
                  pl.BlockSpec((Q, bw), lambda g, c: (rv(c), INNER // bw + g)),
                  pl.BlockSpec((Q, bw), lambda g, c: (rv(c), (INNER + NG * NS) // bw + g)),
                  sc, sc, sc, pl.BlockSpec((GPS, 8, 128), lambda g, c: (g, 0, 0)),
                  pl.BlockSpec((GPS, 1, NS, 256), lambda g, c: (g, rv(c), 0, 0)),
                  pl.BlockSpec((Q, 256 * GPS), lambda g, c: (rv(c), g))],
        out_specs=(pl.BlockSpec((Q, 256 * GPS), lambda g, c: (rv(c), g)),
                   pl.BlockSpec((Q, bw), lambda g, c: (rv(c), g)),
                   pl.BlockSpec((Q, bw), lambda g, c: (rv(c), g)),
                   pl.BlockSpec((GPS, Q, 128), lambda g, c: (g, rv(c), 0)),
                   pl.BlockSpec((GPS, 8, 128), lambda g, c: (g, 0, 0))),
        out_shape=(jax.ShapeDtypeStruct((L, INNER), F32), jax.ShapeDtypeStruct((L, NG * NS), F32),
                   jax.ShapeDtypeStruct((L, NG * NS), F32), jax.ShapeDtypeStruct((NG, L, 128), F32),
                   jax.ShapeDtypeStruct((NG, 8, 128), F32)),
        scratch_shapes=[pltpu.VMEM((GPS, NS, 256), F32)], name="ssd_bwd",
        compiler_params=_cp(("parallel", "arbitrary")),
    )(xbc, xbc, xbc, dt4, cs4, sg4, vecg, s_all, dy)


def _dt_bwd(ddt, dproj, tl=256):
    L = ddt.shape[1]

    def kern(d_ref, _, dp_ref, gs_ref):
        @pl.when(pl.program_id(0) == 0)
        def _():
            gs_ref[...] = jnp.zeros_like(gs_ref)

        d = d_ref[0]
        for g in range(1, NG):
            d = d + pltpu.roll(d_ref[g], 4 * g, axis=1)
        gs_ref[...] += jnp.broadcast_to(jnp.sum(d, axis=0, keepdims=True), (8, 128))
        dp_ref[...] = jnp.concatenate([d, jnp.zeros_like(d)], axis=1).astype(BF16)

    return pl.pallas_call(
        kern, grid=(L // tl,),
        in_specs=[pl.BlockSpec((NG, tl, 128), lambda i: (0, i, 0)), pl.BlockSpec(memory_space=pl.ANY)],
        out_specs=(pl.BlockSpec((tl, 256), lambda i: (i, C_DT // 256)), pl.BlockSpec((8, 128), lambda i: (0, 0))),
        out_shape=(jax.ShapeDtypeStruct(dproj.shape, BF16), jax.ShapeDtypeStruct((8, 128), F32)),
        input_output_aliases={1: 0}, name="dt_bwd", compiler_params=_cp(("arbitrary",)),
    )(ddt, dproj)


GW = INNER // NG


def _gnorm_fwd(y, proj, w, tl=256):
    L = y.shape[0]
    zoff = C_Z // 1024

    def kern(y_ref, z_ref, w_ref, o_ref):
        z = z_ref[...]
        yz = y_ref[...] * (z * _sigmoid(z))
        wv = w_ref[...]
        for k in range(1024 // GW):
            sl = slice(GW * k, GW * (k + 1))
            v = yz[:, sl]
            rg = lax.rsqrt(jnp.mean(v * v, axis=-1, keepdims=True) + EPS)
            o_ref[:, sl] = ((v * rg) * wv[:, sl]).astype(BF16)

    blk = pl.BlockSpec((tl, 1024), lambda i, j: (i, j))
    return pl.pallas_call(
        kern, grid=(L // tl, 2),
        in_specs=[blk, pl.BlockSpec((tl, 1024), lambda i, j: (i, zoff + j)), pl.BlockSpec((1, 1024), lambda i, j: (0, j))],
        out_specs=blk, out_shape=jax.ShapeDtypeStruct((L, INNER), BF16), name="gnorm_fwd",
        compiler_params=_cp(("parallel", "parallel")),
    )(y, proj, w.reshape(1, INNER))


def _gnorm_bwd(dyb, y, proj, w, dproj, tl=256):
    L = y.shape[0]
    zoff = C_Z // 1024

    def kern(d_ref, y_ref, z_ref, w_ref, _, dy_ref, dp_ref, gw_ref):
        @pl.when(pl.program_id(1) == 0)
        def _():
            gw_ref[...] = jnp.zeros_like(gw_ref)

        z = z_ref[...]
        sg = _sigmoid(z)
        sz = z * sg
        yv = y_ref[...]
        yz = yv * sz
        dv = d_ref[...]
        wv = w_ref[...]
        for k in range(1024 // GW):
            sl = slice(GW * k, GW * (k + 1))
            v = yz[:, sl]
            rg = lax.rsqrt(jnp.mean(v * v, axis=-1, keepdims=True) + EPS)
            vn = v * rg
            dk = dv[:, sl]
            gw_ref[:, sl] += jnp.broadcast_to(jnp.sum(dk * vn, axis=0, keepdims=True), (8, GW))
            dvn = dk * wv[:, sl]
            dyz = rg * (dvn - vn * jnp.mean(dvn * vn, axis=-1, keepdims=True))
            dy_ref[:, sl] = dyz * sz[:, sl]
            dp_ref[:, sl] = (dyz * yv[:, sl] * (sg[:, sl] * (1.0 + z[:, sl] * (1.0 - sg[:, sl])))).astype(BF16)

    blk = pl.BlockSpec((tl, 1024), lambda j, i: (i, j))
    zblk = pl.BlockSpec((tl, 1024), lambda j, i: (i, zoff + j))
    return pl.pallas_call(
        kern, grid=(2, L // tl),
        in_specs=[blk, blk, zblk, pl.BlockSpec((1, 1024), lambda j, i: (0, j)), pl.BlockSpec(memory_space=pl.ANY)],
        out_specs=(blk, zblk, pl.BlockSpec((8, 1024), lambda j, i: (0, j))),
        out_shape=(jax.ShapeDtypeStruct((L, INNER), F32), jax.ShapeDtypeStruct(dproj.shape, BF16),
                   jax.ShapeDtypeStruct((8, INNER), F32)),
        input_output_aliases={4: 1}, name="gnorm_bwd", compiler_params=_cp(("parallel", "arbitrary")),
    )(dyb, y, proj, w.reshape(1, INNER), dproj)


def _merge_fwd(proj, bg, br_a, br_b, tl=256):
    L = proj.shape[0]
    goff = C_GATE // 1024

    def kern(g1_ref, g2_ref, b1_ref, b2_ref, a_ref, b_ref, o_ref):
        g1 = _sigmoid(g1_ref[...] + b1_ref[...])
        g2 = _sigmoid(g2_ref[...] + b2_ref[...])
        o_ref[...] = (g1 * a_ref[...] + g2 * b_ref[...]).astype(BF16)

    row = pl.BlockSpec((tl, 1024), lambda i: (i, 0))
    bg2 = bg.reshape(1, 2 * D)
    return pl.pallas_call(
        kern, grid=(L // tl,),
        in_specs=[pl.BlockSpec((tl, 1024), lambda i: (i, goff)), pl.BlockSpec((tl, 1024), lambda i: (i, goff + 1)),
                  pl.BlockSpec((1, 1024), lambda i: (0, 0)), pl.BlockSpec((1, 1024), lambda i: (0, 1)), row, row],
        out_specs=row, out_shape=jax.ShapeDtypeStruct((L, D), BF16), name="merge_fwd",
        compiler_params=_cp(("parallel",)),
    )(proj, proj, bg2, bg2, br_a, br_b)


def _merge_bwd(dm, proj, bg, br_a, br_b, dproj, tl=256):
    L = proj.shape[0]
    goff = C_GATE // 1024

    def kern(dm_ref, g_ref, b_ref, a_ref, bb_ref, _, dbr_ref, dp_ref, gb_ref):
        j = pl.program_id(0)

        @pl.when(pl.program_id(1) == 0)
        def _():
            gb_ref[...] = jnp.zeros_like(gb_ref)

        g = _sigmoid(g_ref[...] + b_ref[...])
        br = jnp.where(j == 0, a_ref[...], bb_ref[...])
        dmv = dm_ref[...]
        dbr_ref[0] = (dmv * g).astype(BF16)
        dgate = dmv * br * g * (1.0 - g)
        gb_ref[...] += jnp.broadcast_to(jnp.sum(dgate, axis=0, keepdims=True), (8, 1024))
        dp_ref[...] = dgate.astype(BF16)

    row = pl.BlockSpec((tl, 1024), lambda j, i: (i, 0))
    gblk = pl.BlockSpec((tl, 1024), lambda j, i: (i, goff + j))
    return pl.pallas_call(
        kern, grid=(2, L // tl),
        in_specs=[row, gblk, pl.BlockSpec((1, 1024), lambda j, i: (0, j)), row, row, pl.BlockSpec(memory_space=pl.ANY)],
        out_specs=(pl.BlockSpec((1, tl, 1024), lambda j, i: (j, i, 0)), gblk, pl.BlockSpec((8, 1024), lambda j, i: (0, j))),
        out_shape=(jax.ShapeDtypeStruct((2, L, D), BF16), jax.ShapeDtypeStruct(dproj.shape, BF16),
                   jax.ShapeDtypeStruct((8, 2 * D), F32)),
        input_output_aliases={5: 1}, name="merge_bwd", compiler_params=_cp(("parallel", "arbitrary")),
    )(dm, proj, bg.reshape(1, 2 * D), br_a, br_b, dproj)


def _coords():
    return lax.axis_index("x"), lax.axis_index("y"), lax.axis_index("c")


def _other_chips(sk):
    xk, yk = sk // 2, sk % 2
    return [((1 - xk, yk), 2 * (1 - xk) + yk), ((xk, 1 - yk), 2 * xk + 1 - yk), ((1 - xk, 1 - yk), 2 * (1 - xk) + 1 - yk)]


def _rows(start, size):
    assert size % 128 == 0
    return pl.ds(pl.multiple_of(start, 128), size)


def _per_chip(fn):
    x, y, _ = _coords()
    s = 2 * x + y
    for sk in range(4):
        pl.when(s == sk)(functools.partial(fn, sk))


XTRA = PIECE - PMAIN


def _place(shard, full_shape, block, index_map, idx, name, blk0=0, nblk=None, dep=None):
    in_block = block[-2:]
    if nblk is None:
        nblk = shard.shape[0] // in_block[0]

    def kern(idx_ref, s_ref, *rest):
        o_ref = rest[-1]
        o_ref[...] = s_ref[...].astype(BF16).reshape(o_ref.shape)

    grid_spec = pltpu.PrefetchScalarGridSpec(
        num_scalar_prefetch=1, grid=(nblk,),
        in_specs=[pl.BlockSpec(in_block, lambda i, idx_ref: (blk0 + i, 0))] + ([_ANY] if dep is not None else []),
        out_specs=pl.BlockSpec(block, index_map))
    args = (idx, shard) + ((dep,) if dep is not None else ())
    return pl.pallas_call(kern, grid_spec=grid_spec, out_shape=jax.ShapeDtypeStruct(full_shape, BF16), name=name,
                          compiler_params=_cp(("arbitrary",)))(*args)


_SEM = pl.BlockSpec(memory_space=pltpu.SEMAPHORE)
_EFFECT = pltpu.SideEffectType.DATAFLOW_SIDE_EFFECTING


_ANY = pl.BlockSpec(memory_space=pl.ANY)


def _tie(v, dep, name):
    def body(v_ref, dep_ref, o_ref):
        del v_ref, dep_ref, o_ref

    return pl.pallas_call(body, out_shape=jax.ShapeDtypeStruct(v.shape, v.dtype), in_specs=[_ANY, _ANY],
                          out_specs=_ANY, input_output_aliases={0: 0}, name=name)(v, dep)


def _split_call(name, arrays, start=None, wait=None, wait_sems=None, after=None):
    keys = list(arrays)
    n = len(keys)
    n_start = start.n if start is not None else 0
    afters = [] if after is None else (list(after) if isinstance(after, (list, tuple)) else [after])

    def body(*refs):
        pos = n
        if wait is not None:
            wss, wrs = refs[pos], refs[pos + 1]
            pos += 2
        pos += len(afters)
        if start is not None:
            nss, nrs = refs[pos], refs[pos + 1]
            pos += 2
        R = dict(zip(keys, refs[pos:pos + n]))
        token = refs[pos + n]
        x, y, c = _coords()

        def desc(src, dst, dev, ss, rs, k):
            return pltpu.make_async_remote_copy(src_ref=src, dst_ref=dst, send_sem=ss.at[k], recv_sem=rs.at[k],
                                                device_id=dev, device_id_type=MESH)

        def run(sk):
            if wait is not None:
                for k, (snd, land) in enumerate(wait.copies(sk, R)):
                    if snd is not None:
                        desc(snd[0], snd[1], snd[2], wss, wrs, k).wait_send()
                    if land is not None:
                        desc(land, land, (x, y, c), wss, wrs, k).wait_recv()
            if start is not None:
                for k, (snd, land) in enumerate(start.copies(sk, R)):
                    if snd is not None:
                        desc(snd[0], snd[1], snd[2], nss, nrs, k).start()

        _per_chip(run)
        token[...] = jnp.zeros_like(token)

    hbm = pl.BlockSpec(memory_space=HBM)
    vals = [arrays[k] for k in keys]
    ins, in_specs = list(vals), [hbm] * n
    if wait is not None:
        ins += list(wait_sems)
        in_specs += [_SEM, _SEM]
    ins += afters
    in_specs += [pl.BlockSpec(memory_space=pl.ANY)] * len(afters)
    out_shape, out_specs = [], []
    if start is not None:
        out_shape += [pltpu.SemaphoreType.DMA((n_start,)), pltpu.SemaphoreType.DMA((n_start,))]
        out_specs += [_SEM, _SEM]
    first = len(out_shape)
    out_shape += [jax.ShapeDtypeStruct(v.shape, v.dtype) for v in vals] + [jax.ShapeDtypeStruct((8, 128), F32)]
    out_specs += [hbm] * n + [pl.BlockSpec(memory_space=pltpu.VMEM)]
    res = pl.pallas_call(
        body, out_shape=tuple(out_shape), in_specs=in_specs, out_specs=tuple(out_specs),
        input_output_aliases={i: first + i for i in range(n)}, name=name,
        compiler_params=pltpu.CompilerParams(has_side_effects=_EFFECT),
    )(*ins)
    sems = (res[0], res[1]) if start is not None else None
    return dict(zip(keys, res[first:first + n])), sems, res[-1]


class _Plan:
    def __init__(self, n, copies):
        self.n, self.copies = n, copies


_HM, _HX = PMAIN // 2, XTRA // 2
_WIN = {
    "wct": (True, lambda r, sc, hc: r.at[_rows(PMAIN * sc + _HM * hc, _HM), :]),
    "xt": (True, lambda r, sc, hc: r.at[sc, _rows(_HX * hc, _HX), :]),
    "w1": (True, lambda r, sc, hc: r.at[_rows(512 * hc, 512), pl.ds(1024 * sc, 1024)]),
    "w2": (True, lambda r, sc, hc: r.at[_rows(1024 * sc + 512 * hc, 512), :]),
    "wa": (True, lambda r, sc, hc: r.at[_rows(256 * sc + 128 * hc, 128), :]),
    "wb": (True, lambda r, sc, hc: r.at[_rows(512 * sc + 256 * hc, 256), :]),
    "wo": (True, lambda r, sc, hc: r.at[_rows(256 * sc + 128 * hc, 128), :]),
    "cw": (False, lambda r, sc, hc: r.at[sc]),
}


def _ag_chips_plan(keys):
    def copies(sk, R):
        _, _, c = _coords()
        out = []
        for key in keys:
            win = _WIN[key][1]
            for (px, py), ps in _other_chips(sk):
                w = win(R[key], sk, c)
                out.append(((w, w, (px, py, c)), win(R[key], ps, c)))
        return out
    return _Plan(3 * len(keys), copies)


def _ag_sibling_plan(keys):
    keys = [k for k in keys if _WIN[k][0]]

    def copies(sk, R):
        x, y, c = _coords()
        out = []
        for key in keys:
            win = _WIN[key][1]
            for _, ps in _other_chips(sk):
                w = win(R[key], ps, c)
                out.append(((w, w, (x, y, 1 - c)), win(R[key], ps, 1 - c)))
        return out
    return _Plan(3 * len(keys), copies)


def _fix_wct(wct, xt):
    nb = PMAIN // XTRA

    def kern(w_ref, x_ref, o_ref):
        k = pl.program_id(0)
        xv = x_ref[0]
        o_ref[...] = jnp.where(k < 3, (w_ref[...].astype(F32) + xv.astype(F32)).astype(BF16), xv)

    blk = pl.BlockSpec((XTRA, D), lambda k: (nb * (k + 1), 0))
    rblk = pl.BlockSpec((XTRA, D), lambda k: (jnp.where(k < 3, nb * (k + 1), 0), 0))
    return pl.pallas_call(
        kern, grid=(4,), in_specs=[rblk, pl.BlockSpec((1, XTRA, D), lambda k: (k, 0, 0))], out_specs=blk,
        out_shape=jax.ShapeDtypeStruct(wct.shape, BF16), input_output_aliases={0: 0}, name="fix_wct",
        compiler_params=_cp(("arbitrary",)),
    )(wct, xt)


_HP = PIECE // 2
_GWIN = [
    lambda r, sc, hc: r.at[_rows(PMAIN * sc + _HP * hc, _HP), :],
    lambda r, sc, hc: r.at[_rows(512 * hc, 512), pl.ds(1024 * sc, 1024)],
    lambda r, sc, hc: r.at[_rows(1024 * sc + 512 * hc, 512), :],
    lambda r, sc, hc: r.at[_rows(256 * sc + 128 * hc, 128), :],
    lambda r, sc, hc: r.at[_rows(512 * sc + 256 * hc, 256), :],
    lambda r, sc, hc: r.at[_rows(256 * sc + 128 * hc, 128), :],
]
HALF_SHAPES = [(PIECE // 2, D), (512, 1024), (512, 1024), (128, 1024), (256, 1024), (128, 1024)]


def _rs_sibling_plan(ts):
    def copies(sk, R):
        x, y, c = _coords()
        out = []
        for t in ts:
            for sc in range(4):
                land = R["ra%d" % t].at[sc]
                out.append(((_GWIN[t](R["g%d" % t], sc, 1 - c), land, (x, y, 1 - c)), land))
        return out
    return _Plan(4 * len(ts), copies)


def _rs_chips_plan(ts):
    def copies(sk, R):
        _, _, c = _coords()
        out = []
        for t in ts:
            for j, ((px, py), ps) in enumerate(_other_chips(sk)):
                land = R["rb%d" % t].at[j]
                out.append(((R["hb%d" % t].at[ps], land, (px, py, c)), land))
        return out
    return _Plan(3 * len(ts), copies)


def _rs_share_plan(ts):
    def copies(sk, R):
        x, y, c = _coords()
        out = []
        for t in ts:
            rows = HALF_SHAPES[t][0]
            mine = R["f%d" % t].at[_rows(rows * c, rows), :]
            out.append(((mine, mine, (x, y, 1 - c)), R["f%d" % t].at[_rows(rows * (1 - c), rows), :]))
        return out
    return _Plan(len(ts), copies)


def _half_tiling(t):
    rows, cols = HALF_SHAPES[t]
    if t == 0:
        return (256, cols), rows // 256, lambda i: (i, 0)
    if t == 1:
        return (rows, 256), cols // 256, lambda i: (0, i)
    return (rows, cols), 1, lambda i: (0, 0)


def _window_block(t, sc, hc, i):
    if t == 0:
        return (PMAIN // 256) * sc + (PIECE // 512) * hc + i, 0
    if t == 1:
        return hc, 4 * sc + i
    return 2 * sc + hc, 0


def _chip_sum(g, ra, t, idx, name):
    rows, cols = HALF_SHAPES[t]
    blk, nblk, inner = _half_tiling(t)

    def kern(idx_ref, g_ref, r_ref, hb_ref, hf_ref):
        v = g_ref[...].astype(F32) + r_ref[0].astype(F32)
        hb_ref[0] = v.astype(BF16)
        hf_ref[0] = v

    gmap = lambda sc, i, idx_ref: _window_block(t, sc, idx_ref[1], i)
    omap = lambda sc, i, idx_ref: (sc,) + inner(i)
    grid_spec = pltpu.PrefetchScalarGridSpec(
        num_scalar_prefetch=1, grid=(4, nblk),
        in_specs=[pl.BlockSpec(blk, gmap), pl.BlockSpec((1,) + blk, omap)],
        out_specs=(pl.BlockSpec((1,) + blk, omap), pl.BlockSpec((1,) + blk, omap)))
    return pl.pallas_call(
        kern, grid_spec=grid_spec,
        out_shape=(jax.ShapeDtypeStruct((4, rows, cols), BF16), jax.ShapeDtypeStruct((4, rows, cols), F32)),
        name=name, compiler_params=_cp(("parallel", "parallel")),
    )(idx, g, ra)


def _final_sum(hf, rb, t, idx, name):
    rows, cols = HALF_SHAPES[t]
    blk, nblk, inner = _half_tiling(t)
    nbr = rows // blk[0]

    def kern(idx_ref, h_ref, r_ref, o_ref):
        o_ref[...] = ((h_ref[0] + r_ref[0].astype(F32)) + r_ref[1].astype(F32)) + r_ref[2].astype(F32)

    def omap(i, idx_ref):
        r, cidx = inner(i)
        return nbr * idx_ref[1] + r, cidx

    grid_spec = pltpu.PrefetchScalarGridSpec(
        num_scalar_prefetch=1, grid=(nblk,),
        in_specs=[pl.BlockSpec((1,) + blk, lambda i, idx_ref: (idx_ref[0],) + inner(i)),
                  pl.BlockSpec((3,) + blk, lambda i, idx_ref: (0,) + inner(i))],
        out_specs=pl.BlockSpec(blk, omap))
    return pl.pallas_call(
        kern, grid_spec=grid_spec, out_shape=jax.ShapeDtypeStruct((2 * rows, cols), F32),
        name=name, compiler_params=_cp(("parallel",)),
    )(idx, hf, rb)


class _ReduceScatter:
    def __init__(self, ts, grads, idx, tag):
        self.ts, self.idx, self.tag = ts, idx, tag
        arr = {}
        for t in ts:
            arr["g%d" % t] = grads[t]
            arr["ra%d" % t] = lax.empty((4,) + HALF_SHAPES[t], BF16)
        self.plan = _rs_sibling_plan(ts)
        self.arr, self.sems, self.token = _split_call("rs_sibling_start_" + tag, arr, start=self.plan)

    def chips(self, after):
        arr, _, _ = _split_call("rs_sibling_wait_" + self.tag, self.arr, wait=self.plan, wait_sems=self.sems, after=after)
        brr, self.hf = {}, {}
        for t in self.ts:
            hb, self.hf[t] = _chip_sum(arr["g%d" % t], arr["ra%d" % t], t, self.idx, "chip_sum_%d" % t)
            brr["hb%d" % t] = hb
            brr["rb%d" % t] = lax.empty((3,) + HALF_SHAPES[t], BF16)
        self.plan = _rs_chips_plan(self.ts)
        self.arr, self.sems, self.token = _split_call("rs_chips_start_" + self.tag, brr, start=self.plan)
        return self.token

    def share(self, after):
        brr, _, _ = _split_call("rs_chips_wait_" + self.tag, self.arr, wait=self.plan, wait_sems=self.sems, after=after)
        frr = {"f%d" % t: _final_sum(self.hf[t], brr["rb%d" % t], t, self.idx, "final_sum_%d" % t) for t in self.ts}
        self.plan = _rs_share_plan(self.ts)
        self.arr, self.sems, self.token = _split_call("rs_share_start_" + self.tag, frr, start=self.plan)
        return self.token

    def result(self, after):
        frr, _, _ = _split_call("rs_share_wait_" + self.tag, self.arr, wait=self.plan, wait_sems=self.sems, after=after)
        return {t: frr["f%d" % t] for t in self.ts}


def _all8_plan(key):
    def copies(sk, R):
        x, y, c = _coords()
        own = R[key].at[4 * x + 2 * y + c]
        out = []
        for k in range(1, 8):
            dev = ((1 - x) if (k >> 2) & 1 else x, (1 - y) if (k >> 1) & 1 else y, (1 - c) if k & 1 else c)
            out.append(((own, own, dev), R[key].at[4 * dev[0] + 2 * dev[1] + dev[2]]))
        return out
    return _Plan(7, copies)


def _small_all_gather(v):
    def body(v_ref, o_ref, send_sems, recv_sems, loc_sem):
        x, y, c = _coords()
        me = 4 * x + 2 * y + c
        lc = pltpu.make_async_copy(v_ref, o_ref.at[me], loc_sem)
        lc.start()
        cps = []
        for k in range(1, 8):
            fx, fy, fc = (k >> 2) & 1, (k >> 1) & 1, k & 1
            dev = ((1 - x) if fx else x, (1 - y) if fy else y, (1 - c) if fc else c)
            cp = pltpu.make_async_remote_copy(src_ref=v_ref, dst_ref=o_ref.at[me], send_sem=send_sems.at[k - 1],
                                              recv_sem=recv_sems.at[k - 1], device_id=dev, device_id_type=MESH)
            cp.start()
            cps.append((cp, 4 * dev[0] + 2 * dev[1] + dev[2]))
        for k, (cp, frm) in enumerate(cps):
            got = o_ref.at[frm]
            pltpu.make_async_remote_copy(src_ref=got, dst_ref=got, send_sem=send_sems.at[k], recv_sem=recv_sems.at[k],
                                         device_id=(x, y, c), device_id_type=MESH).wait_recv()
        for cp, _ in cps:
            cp.wait_send()
        lc.wait()

    hbm = pl.BlockSpec(memory_space=HBM)
    return pl.pallas_call(
        body, out_shape=jax.ShapeDtypeStruct((8,) + v.shape, F32), in_specs=[hbm], out_specs=hbm,
        scratch_shapes=[pltpu.SemaphoreType.DMA((7,)), pltpu.SemaphoreType.DMA((7,)), pltpu.SemaphoreType.DMA(())],
        name="small_all_gather", compiler_params=pltpu.CompilerParams(has_side_effects=True),
    )(v)


def _sum8(v, name="small_sum"):
    def kern(v_ref, o_ref):
        acc = v_ref[0]
        for k in range(1, 8):
            acc = acc + v_ref[k]
        o_ref[...] = acc

    return pl.pallas_call(kern, out_shape=jax.ShapeDtypeStruct(v.shape[1:], F32), name=name)(v)


def _adamw(w, g, m, v, name, tr=128, blk0=0, nblk=None, into=None, copy_g=False):
    R, C = w.shape
    tr = min(tr, R)
    if nblk is None:
        assert R % tr == 0 and blk0 == 0
        nblk = R // tr
    n_out = 4 if copy_g else 3

    def kern(*refs):
        w_ref, g_ref, m_ref, v_ref = refs[:4]
        d_ref, mo_ref, vo_ref = refs[-n_out:][:3]
        gv = g_ref[...]
        mn = ADAM_B1 * m_ref[...] + (1.0 - ADAM_B1) * gv
        vn = ADAM_B2 * v_ref[...] + (1.0 - ADAM_B2) * (gv * gv)
        m_hat = mn / (1.0 - ADAM_B1 ** ADAM_STEP)
        v_hat = vn / (1.0 - ADAM_B2 ** ADAM_STEP)
        d_ref[...] = -ADAM_LR * (m_hat / (jnp.sqrt(v_hat) + ADAM_EPS) + ADAM_WD * w_ref[...])
        mo_ref[...] = mn
        vo_ref[...] = vn
        if copy_g:
            refs[-1][...] = gv

    blk = pl.BlockSpec((tr, C), lambda i: (blk0 + i, 0))
    sd = jax.ShapeDtypeStruct((R, C), F32)
    in_specs, args, aliases = [blk] * 4, [w, g, m, v], {}
    if into is not None:
        in_specs += [pl.BlockSpec(memory_space=pl.ANY)] * 3
        args += list(into)
        aliases = {4: 0, 5: 1, 6: 2}
    return pl.pallas_call(kern, grid=(nblk,), in_specs=in_specs, out_specs=(blk,) * n_out, out_shape=(sd,) * n_out,
                          input_output_aliases=aliases, name=name, compiler_params=_cp(("parallel",)))(*args)


def _to_piece(wt, s):
    z = lambda n: jnp.zeros((n, D), wt.dtype)
    pads = [functools.partial(lambda k, w: jnp.pad(w, ((8 * k, PIECE - W_SHARD - 8 * k), (0, 0))), k) for k in range(3)]
    last = lambda w: jnp.concatenate([z(24), w[:744], w[776:], w[744:776], z(PIECE - 24 - W_SHARD)], axis=0)
    return lax.switch(s, pads + [last], wt)


def _from_piece(p, s):
    cuts = [functools.partial(lambda k, q: q[8 * k:8 * k + W_SHARD], k) for k in range(3)]
    last = lambda q: jnp.concatenate([q[24:768], q[2816:2848], q[768:2816]], axis=0)
    return lax.switch(s, cuts + [last], p)


_SMALL = [("b_gate", 2048), ("ssm_conv_b", 4096), ("dt_bias", 32), ("A_log", 32), ("D_skip", 32),
          ("ssm_norm_w", 2048), ("norm_mlp", 1024), ("norm_final", 1024), ("sc_conv_w", 3072), ("ssm_conv_w", 16384),
          ("loss", 1)]


def _pack(vals, table, rows):
    parts = []
    for name, n in table:
        v = vals[name].reshape(-1).astype(F32)
        pad = (-n) % 128
        parts.append(jnp.pad(v, (0, pad)) if pad else v)
    flat = jnp.concatenate(parts)
    return jnp.pad(flat, (0, rows * 128 - flat.shape[0])).reshape(rows, 128)


def _unpack(arr, table):
    flat = arr.reshape(-1)
    out, off = {}, 0
    for name, n in table:
        out[name] = flat[off:off + n]
        off += n + ((-n) % 128)
    return out


def kernel(x, norm_mix, w_in, b_gate, sc_conv_w, ssm_conv_w, ssm_conv_b, dt_bias, A_log, D_skip, ssm_norm_w, w_branch_sc, w_branch_ssm, w_out, norm_mlp, w_mlp1, w_mlp2, norm_final, loss_target, m_norm_mix, m_w_in, m_b_gate, m_sc_conv_w, m_ssm_conv_w, m_ssm_conv_b, m_dt_bias, m_A_log, m_D_skip, m_ssm_norm_w, m_w_branch_sc, m_w_branch_ssm, m_w_out, m_norm_mlp, m_w_mlp1, m_w_mlp2, m_norm_final, v_norm_mix, v_w_in, v_b_gate, v_sc_conv_w, v_ssm_conv_w, v_ssm_conv_b, v_dt_bias, v_A_log, v_D_skip, v_ssm_norm_w, v_w_branch_sc, v_w_branch_ssm, v_w_out, v_norm_mlp, v_w_mlp1, v_w_mlp2, v_norm_final):
    L = x.shape[1]
    nc = L // Q
    xi, yi, ci = lax.axis_index("x"), lax.axis_index("y"), lax.axis_index("c")
    s = 2 * xi + yi
    idx = jnp.stack([s, ci]).astype(jnp.int32)
    x0 = x.reshape(L, D)
    tgt = loss_target.reshape(L, D)

    piece = _to_piece(w_in.T, s)
    nb = PMAIN // XTRA
    wct0 = _place(piece, (NCW, D), (XTRA, D), lambda i, r: (nb * r[0] + i, 0), idx, "place_wct", nblk=nb)
    xt0 = _place(piece, (4, XTRA, D), (1, XTRA, D), lambda i, r: (r[0], 0, 0), idx, "place_xt", blk0=nb, nblk=1)
    cws = jnp.zeros((8, 1280), F32)
    cws = cws.at[0:3, 0:256].set(sc_conv_w).at[0:4, 256:1280].set(ssm_conv_w)
    cw0 = lax.dynamic_update_slice(jnp.zeros((4, 8, 1280), F32), cws[None], (s, 0, 0))
    win_keys, mid_keys, end_keys = ["wct", "xt", "cw"], ["wa", "wb", "wo", "w1"], ["w2"]
    gw, sems_w, tok = _split_call("ag_win_start", {"wct": wct0, "xt": xt0, "cw": cw0}, start=_ag_chips_plan(win_keys))
    wa0 = _place(w_branch_sc, (D, D), (256, 1024), lambda i, r: (r[0], 0), idx, "place_wa", dep=tok)
    wb0 = _place(w_branch_ssm, (INNER, D), (512, 1024), lambda i, r: (r[0], 0), idx, "place_wb", dep=tok)
    wo0 = _place(w_out, (D, D), (256, 1024), lambda i, r: (r[0], 0), idx, "place_wo", dep=tok)
    w10 = _place(w_mlp1, (D, DFF), (256, 1024), lambda i, r: (i, r[0]), idx, "place_w1", dep=tok)
    gm, sems_m, tok = _split_call("ag_mid_start", {"wa": wa0, "wb": wb0, "wo": wo0, "w1": w10},
                                  start=_ag_chips_plan(mid_keys))
    w20 = _place(w_mlp2, (DFF, D), (256, 1024), lambda i, r: (4 * r[0] + i, 0), idx, "place_w2", dep=tok)
    ge, sems_e, tok = _split_call("ag_end_start", {"w2": w20}, start=_ag_chips_plan(end_keys))
    h = _rms_fwd(x0, norm_mix, "rms_mix", dep=tok)
    gw, sems_w, tok = _split_call("ag_win_pass", gw, wait=_ag_chips_plan(win_keys), wait_sems=sems_w,
                                  start=_ag_sibling_plan(win_keys), after=h)
    gw, _, _ = _split_call("ag_win_done", gw, wait=_ag_sibling_plan(win_keys), wait_sems=sems_w, after=tok)
    wc, cw_all = _fix_wct(gw["wct"], gw["xt"]), gw["cw"]
    sc_w_full = jnp.concatenate([cw_all[k, :, 0:256] for k in range(4)], axis=1)
    ssm_w_full = jnp.concatenate([cw_all[k, :, 256:1280] for k in range(4)], axis=1)
    cw4 = ssm_w_full.at[4].set(ssm_conv_b)
    vec = jnp.zeros((8, 128), F32).at[0, :NH].set(dt_bias).at[1, :NH].set(A_log)
    vecg = jnp.zeros((NG, 8, 128), F32).at[:, 0, :4].set(A_log.reshape(NG, 4)).at[:, 1, :4].set(D_skip.reshape(NG, 4))

    proj = _matmul(h, wc, "nt", F32, 512, 1280, 1024, "in_proj", n_outer=True)
    gm, sems_m, tok = _split_call("ag_mid_pass", gm, wait=_ag_chips_plan(mid_keys), wait_sems=sems_m,
                                  start=_ag_sibling_plan(mid_keys), after=proj)
    proj = _tie(proj, tok, "tie_proj")
    ya = _sc_fwd(proj, sc_w_full)
    xbc = _ssm_conv_fwd(proj, cw4)
    dt4, cs4, sg4 = _dt_prep(proj, vec)
    ge, sems_e, tok = _split_call("ag_end_pass", ge, wait=_ag_chips_plan(end_keys), wait_sems=sems_e,
                                  start=_ag_sibling_plan(end_keys), after=xbc)
    xbc = _tie(xbc, tok, "tie_xbc")
    y, s_all = _ssd_fwd(xbc, dt4, cs4, vecg)
    yb = _gnorm_fwd(y, proj, ssm_norm_w)
    gm, _, _ = _split_call("ag_mid_done", gm, wait=_ag_sibling_plan(mid_keys), wait_sems=sems_m, after=yb)
    ge, _, _ = _split_call("ag_end_done", ge, wait=_ag_sibling_plan(end_keys), wait_sems=sems_e, after=yb)
    wa, wb, wo, w1, w2 = gm["wa"], gm["wb"], gm["wo"], gm["w1"], ge["w2"]
    br_a = _matmul(ya, wa, "nn", F32, 512, 1024, 1024, "branch_sc")
    br_b = _matmul(yb, wb, "nn", F32, 512, 1024, 2048, "branch_ssm")
    merged = _merge_fwd(proj, b_gate, br_a, br_b)
    x1 = _matmul(merged, wo, "nn", F32, 512, 1024, 1024, "out_proj", epi="res", extra=x0)
    h2 = _rms_fwd(x1, norm_mlp, "rms_mlp")
    a1, rl = _matmul(h2, w1, "nn", F32, 512, 1024, 1024, "mlp1", epi="relu2", n_outer=True)
    x2 = _matmul(rl, w2, "nn", F32, 512, 1024, 2048, "mlp2", epi="res", extra=x1)
    dx2, g_nf, loss8 = _final(x2, norm_final, tgt)

    da = _matmul(dx2, w2, "nt", BF16, 512, 1024, 1024, "mlp2_dx", epi="drelu", extra=a1, n_outer=True)
    g_w2 = _matmul(rl, dx2, "tn", BF16, 1024, 1024, 512, "mlp2_dw")
    g_w1 = _matmul(h2, da, "tn", BF16, 1024, 1024, 512, "mlp1_dw")
    dh2 = _matmul(da, w1, "nt", F32, 512, 1024, 2048, "mlp1_dx")
    dx1, g_nmlp = _rms_bwd(dh2, x1, norm_mlp, dx2, "rms_mlp_bwd")
    dmerged = _matmul(dx1, wo, "nt", F32, 512, 1024, 1024, "out_proj_dx")
    g_wo = _matmul(merged, dx1, "tn", BF16, 1024, 1024, 512, "out_proj_dw")
    dproj = lax.empty((L, NCW), BF16)
    dbr, dproj, g_bg = _merge_bwd(dmerged, proj, b_gate, br_a, br_b, dproj)
    dya = _matmul(dbr[0], wa, "nt", F32, 512, 1024, 1024, "branch_sc_dx")
    g_wa = _matmul(ya, dbr[0], "tn", BF16, 1024, 1024, 512, "branch_sc_dw")
    dproj, g_scw = _sc_bwd(dya, proj, sc_w_full, dproj)
    dyb = _matmul(dbr[1], wb, "nt", F32, 512, 1024, 1024, "branch_ssm_dx", n_outer=True)
    g_wb = _matmul(yb, dbr[1], "tn", BF16, 1024, 1024, 512, "branch_ssm_dw")
    rs_a = _ReduceScatter([1, 2, 3, 4, 5], {1: g_w1, 2: g_w2, 3: g_wa, 4: g_wb, 5: g_wo}, idx, "a")
    dy, dproj, g_snw = _gnorm_bwd(_tie(dyb, rs_a.token, "tie_dyb"), y, proj, ssm_norm_w, dproj)
    tok = rs_a.chips(after=dy)
    dxs, dbm, dcm, ddt_g, st = _ssd_bwd(xbc, dt4, cs4, sg4, vecg, s_all, _tie(dy, tok, "tie_dy"))
    dproj, gx1 = _ssm_conv_bwd(dxs, proj, cw4, dproj, 0, "ssm_conv_bwd_x")
    dproj, gx2 = _ssm_conv_bwd(dbm, proj, cw4, dproj, INNER, "ssm_conv_bwd_b")
    dproj, gx3 = _ssm_conv_bwd(dcm, proj, cw4, dproj, INNER + NG * NS, "ssm_conv_bwd_c")
    g_cw4 = jnp.concatenate([gx1, gx2, gx3], axis=1)
    dproj, g_dtb = _dt_bwd(ddt_g, dproj)
    small = {"b_gate": g_bg[0], "ssm_conv_b": g_cw4[4], "dt_bias": g_dtb[0, :NH],
             "A_log": st[:, 0, :4], "D_skip": st[:, 1, :4], "ssm_norm_w": g_snw[0], "norm_mlp": g_nmlp[0],
             "norm_final": g_nf[0], "sc_conv_w": g_scw[0:3], "ssm_conv_w": g_cw4[0:4], "loss": loss8[0, 0:1]}
    small_sum = _sum8(_small_all_gather(_pack(small, _SMALL, SMALL_ROWS)))
    gs = _unpack(small_sum, _SMALL)
    g_wc = _matmul(dproj, h, "tn", BF16, 1280, 1024, 512, "in_proj_dw", dep=small_sum)
    rs_b = _ReduceScatter([0], {0: g_wc}, idx, "b")
    tok = rs_a.share(after=rs_b.token)
    tok = rs_b.chips(after=tok)
    dh = _matmul(dproj, wc, "nn", F32, 512, 1024, 2304, "in_proj_dx", dep=tok)
    grad_x, g_nm = _rms_bwd(dh, x0, norm_mix, dx1, "rms_mix_bwd")
    me = 4 * xi + 2 * yi + ci
    nm8 = lax.dynamic_update_slice(jnp.zeros((8, 8, 128), F32), g_nm[0].reshape(1, 8, 128), (me, 0, 0))
    nm_arr, nm_sems, tok = _split_call("norm_mix_start", {"nm": nm8}, start=_all8_plan("nm"))
    red = rs_a.result(after=tok)
    big = {"w_mlp1": red[1], "w_mlp2": red[2], "w_branch_sc": red[3], "w_branch_ssm": red[4], "w_out": red[5]}

    given = dict(norm_mix=norm_mix, w_in=w_in, b_gate=b_gate, sc_conv_w=sc_conv_w, ssm_conv_w=ssm_conv_w, ssm_conv_b=ssm_conv_b, dt_bias=dt_bias, A_log=A_log, D_skip=D_skip, ssm_norm_w=ssm_norm_w, w_branch_sc=w_branch_sc, w_branch_ssm=w_branch_ssm, w_out=w_out, norm_mlp=norm_mlp, w_mlp1=w_mlp1, w_mlp2=w_mlp2, norm_final=norm_final,
                 m_norm_mix=m_norm_mix, m_w_in=m_w_in, m_b_gate=m_b_gate, m_sc_conv_w=m_sc_conv_w, m_ssm_conv_w=m_ssm_conv_w, m_ssm_conv_b=m_ssm_conv_b, m_dt_bias=m_dt_bias, m_A_log=m_A_log, m_D_skip=m_D_skip, m_ssm_norm_w=m_ssm_norm_w, m_w_branch_sc=m_w_branch_sc, m_w_branch_ssm=m_w_branch_ssm, m_w_out=m_w_out, m_norm_mlp=m_norm_mlp, m_w_mlp1=m_w_mlp1, m_w_mlp2=m_w_mlp2, m_norm_final=m_norm_final,
                 v_norm_mix=v_norm_mix, v_w_in=v_w_in, v_b_gate=v_b_gate, v_sc_conv_w=v_sc_conv_w, v_ssm_conv_w=v_ssm_conv_w, v_ssm_conv_b=v_ssm_conv_b, v_dt_bias=v_dt_bias, v_A_log=v_A_log, v_D_skip=v_D_skip, v_ssm_norm_w=v_ssm_norm_w, v_w_branch_sc=v_w_branch_sc, v_w_branch_ssm=v_w_branch_ssm, v_w_out=v_w_out, v_norm_mlp=v_norm_mlp, v_w_mlp1=v_w_mlp1, v_w_mlp2=v_w_mlp2, v_norm_final=v_norm_final)
    order = ["norm_mix", "w_in", "b_gate", "sc_conv_w", "ssm_conv_w", "ssm_conv_b", "dt_bias", "A_log", "D_skip",
             "ssm_norm_w", "w_branch_sc", "w_branch_ssm", "w_out", "norm_mlp", "w_mlp1", "w_mlp2", "norm_final"]
    grad, delta, new_m, new_v = {}, {}, {}, {}
    for n in big:
        delta[n], new_m[n], new_v[n], grad[n] = _adamw(given[n], big[n], given["m_" + n], given["v_" + n],
                                                       "adamw_" + n, copy_g=True)
    big["w_in"] = None
    grad_small = {n: gs[n].reshape(given[n].shape) for n in order
                  if n not in big and n not in ("sc_conv_w", "ssm_conv_w", "norm_mix")}
    grad_small["sc_conv_w"] = lax.dynamic_slice(gs["sc_conv_w"].reshape(3, D), (0, 256 * s), (3, 256))
    grad_small["ssm_conv_w"] = lax.dynamic_slice(gs["ssm_conv_w"].reshape(4, XBC), (0, 1024 * s), (4, 1024))
    table = [(n, int(grad_small[n].size)) for n in grad_small]
    rows = 136
    pk = lambda d: _pack(d, table, rows)
    ds_, ms_, vs_ = _adamw(pk({n: given[n] for n in grad_small}), pk(grad_small), pk({n: given["m_" + n] for n in grad_small}),
                           pk({n: given["v_" + n] for n in grad_small}), "adamw_small", tr=rows)
    ds_, ms_, vs_ = _unpack(ds_, table), _unpack(ms_, table), _unpack(vs_, table)
    for n in grad_small:
        shp = given[n].shape
        grad[n] = grad_small[n]
        delta[n], new_m[n], new_v[n] = ds_[n].reshape(shp), ms_[n].reshape(shp), vs_[n].reshape(shp)

    done = [new_v[n] for n in ("w_mlp1", "w_mlp2", "w_branch_sc", "w_branch_ssm", "w_out")] + [vs_["b_gate"]]
    tok = rs_b.share(after=done)
    gp = rs_b.result(after=tok)[0]
    gwt = _from_piece(gp, s)
    wt_args = (w_in.T, gwt, m_w_in.T, v_w_in.T)
    head = _adamw(*wt_args, "adamw_w_in", tr=256, nblk=W_SHARD // 256)
    dt_, mt_, vt_ = _adamw(*wt_args, "adamw_w_in_tail", tr=8, blk0=(W_SHARD // 256) * 32, nblk=1, into=head)
    grad["w_in"], delta["w_in"], new_m["w_in"], new_v["w_in"] = gwt.T, dt_.T, mt_.T, vt_.T
    nm_arr, _, _ = _split_call("norm_mix_wait", nm_arr, wait=_all8_plan("nm"), wait_sems=nm_sems, after=tok)
    g8 = _sum8(nm_arr["nm"], "norm_mix_sum")
    r8 = lambda a: a.reshape(8, 128)
    d8, m8, v8 = _adamw(r8(norm_mix), g8, r8(m_norm_mix), r8(v_norm_mix), "adamw_norm_mix", tr=8)
    grad["norm_mix"], delta["norm_mix"] = g8.reshape(D), d8.reshape(D)
    new_m["norm_mix"], new_v["norm_mix"] = m8.reshape(D), v8.reshape(D)

    loss = gs["loss"].reshape(())
    return (loss, grad_x.reshape(1, L, D), *[grad[n] for n in order], *[delta[n] for n in order],
            *[new_m[n] for n in order], *[new_v[n] for n in order])
```

```python
import functools

import jax
import jax.numpy as jnp
from jax import lax
from jax.experimental import pallas as pl
from jax.experimental.pallas import tpu as pltpu

F32 = jnp.float32
BF16 = jnp.bfloat16
MESH = pl.DeviceIdType.MESH
HBM = pltpu.HBM

D = 1024
INNER = 2048
HD = 64
NH = 32
NG = 8
NS = 128
Q = 128
GPS = 2
XBC = 4096
DFF = 4096
EPS = 1e-6
W_SHARD = 2824
NCW = 11520
PIECE = 3072
PMAIN = 2816
C_Z, C_XBC, C_GATE, C_DT = 3072, 5120, 9216, 11264
SMALL_ROWS = 256
VMEM_LIMIT = 56 * 1024 * 1024

ADAM_LR, ADAM_B1, ADAM_B2, ADAM_EPS, ADAM_WD, ADAM_STEP = 0.001, 0.9, 0.999, 1e-08, 0.01, 10


def _cp(sem=None, vmem=VMEM_LIMIT):
    return pltpu.CompilerParams(dimension_semantics=sem, vmem_limit_bytes=vmem)


def _sigmoid(v):
    return 1.0 / (1.0 + jnp.exp(-v))


_DIMS = {"nn": (((1,), (0,)), ((), ())), "nt": (((1,), (1,)), ((), ())), "tn": (((0,), (0,)), ((), ()))}


def _matmul(a, b, mode, out_dtype, tm, tn, tk, name, epi=None, extra=None, n_outer=False, dep=None):
    if mode == "tn":
        K, M = a.shape
    else:
        M, K = a.shape
    N = b.shape[0] if mode == "nt" else b.shape[1]
    tm, tn, tk = min(tm, M), min(tn, N), min(tk, K)
    assert M % tm == 0 and N % tn == 0 and K % tk == 0, (name, M, N, K, tm, tn, tk)
    nm, nn, nk = M // tm, N // tn, K // tk
    dims = _DIMS[mode]

    def ij(p0, p1):
        return (p1, p0) if n_outer else (p0, p1)

    if mode == "tn":
        a_spec = pl.BlockSpec((tk, tm), lambda p0, p1, k: (k, ij(p0, p1)[0]))
    else:
        a_spec = pl.BlockSpec((tm, tk), lambda p0, p1, k: (ij(p0, p1)[0], k))
    if mode == "nt":
        b_spec = pl.BlockSpec((tn, tk), lambda p0, p1, k: (ij(p0, p1)[1], k))
    else:
        b_spec = pl.BlockSpec((tk, tn), lambda p0, p1, k: (k, ij(p0, p1)[1]))
    o_spec = pl.BlockSpec((tm, tn), lambda p0, p1, k: ij(p0, p1))
    in_specs = [a_spec, b_spec]
    args = [a, b]
    if epi in ("res", "drelu"):
        in_specs.append(o_spec)
        args.append(extra)
    if dep is not None:
        in_specs.append(pl.BlockSpec(memory_space=pl.ANY))
        args.append(dep)
    n_in = len(args)
    if epi == "relu2":
        out_shape = (jax.ShapeDtypeStruct((M, N), out_dtype), jax.ShapeDtypeStruct((M, N), BF16))
        out_specs = (o_spec, o_spec)
    else:
        out_shape = jax.ShapeDtypeStruct((M, N), out_dtype)
        out_specs = o_spec

    def kern(*refs):
        a_ref, b_ref = refs[0], refs[1]
        e_ref = refs[2] if epi in ("res", "drelu") else None
        acc = refs[-1]
        outs = refs[n_in:-1] if nk > 1 else refs[n_in:]
        k = pl.program_id(2)
        prod = lax.dot_general(a_ref[...].astype(BF16), b_ref[...].astype(BF16), dims, preferred_element_type=F32)

        def finish(r):
            if epi is None:
                outs[0][...] = r.astype(out_dtype)
            elif epi == "res":
                outs[0][...] = (r + e_ref[...]).astype(out_dtype)
            elif epi == "relu2":
                outs[0][...] = r.astype(out_dtype)
                t = jnp.maximum(r, 0.0)
                outs[1][...] = (t * t).astype(BF16)
            else:
                outs[0][...] = (r * (2.0 * jnp.maximum(e_ref[...].astype(F32), 0.0))).astype(out_dtype)

        if nk == 1:
            finish(prod)
        else:
            @pl.when(k == 0)
            def _():
                acc[...] = prod

            @pl.when(k > 0)
            def _():
                acc[...] += prod

            @pl.when(k == nk - 1)
            def _():
                finish(acc[...])

    grid = (nn, nm, nk) if n_outer else (nm, nn, nk)
    return pl.pallas_call(
        kern, grid=grid, in_specs=in_specs, out_specs=out_specs, out_shape=out_shape,
        scratch_shapes=[pltpu.VMEM((tm, tn), F32)] if nk > 1 else [], name=name,
        compiler_params=_cp(("parallel", "parallel", "arbitrary")),
    )(*args)


def _rms_fwd(x, w, name, tl=256, dep=None):
    L = x.shape[0]

    def kern(x_ref, w_ref, *rest):
        o_ref = rest[-1]
        xv = x_ref[...]
        r = lax.rsqrt(jnp.mean(xv * xv, axis=-1, keepdims=True) + EPS)
        o_ref[...] = ((xv * r) * w_ref[...]).astype(BF16)

    row = pl.BlockSpec((tl, D), lambda i: (i, 0))
    deps = [] if dep is None else [dep]
    return pl.pallas_call(
        kern, grid=(L // tl,),
        in_specs=[row, pl.BlockSpec((1, D), lambda i: (0, 0))] + [pl.BlockSpec(memory_space=pl.ANY)] * len(deps),
        out_specs=row, out_shape=jax.ShapeDtypeStruct((L, D), BF16), name=name, compiler_params=_cp(("parallel",)),
    )(x, w.reshape(1, D), *deps)


def _rms_bwd(dy, x, w, res, name, tl=256, dep=None):
    L = x.shape[0]
    deps = [] if dep is None else [dep]

    def kern(dy_ref, x_ref, w_ref, res_ref, *rest):
        dx_ref, gw_ref = rest[-2:]
        @pl.when(pl.program_id(0) == 0)
        def _():
            gw_ref[...] = jnp.zeros_like(gw_ref)

        xv = x_ref[...]
        dyv = dy_ref[...]
        r = lax.rsqrt(jnp.mean(xv * xv, axis=-1, keepdims=True) + EPS)
        xn = xv * r
        gw_ref[...] += jnp.broadcast_to(jnp.sum(dyv * xn, axis=0, keepdims=True), (8, D))
        dxn = dyv * w_ref[...]
        dx_ref[...] = res_ref[...] + r * (dxn - xn * jnp.mean(dxn * xn, axis=-1, keepdims=True))

    row = pl.BlockSpec((tl, D), lambda i: (i, 0))
    return pl.pallas_call(
        kern, grid=(L // tl,),
        in_specs=[row, row, pl.BlockSpec((1, D), lambda i: (0, 0)), row] + [pl.BlockSpec(memory_space=pl.ANY)] * len(deps),
        out_specs=(row, pl.BlockSpec((8, D), lambda i: (0, 0))),
        out_shape=(jax.ShapeDtypeStruct((L, D), F32), jax.ShapeDtypeStruct((8, D), F32)),
        name=name, compiler_params=_cp(("arbitrary",)),
    )(dy, x, w.reshape(1, D), res, *deps)


def _final(x2, w, tgt, tl=256):
    L = x2.shape[0]

    def kern(x_ref, w_ref, t_ref, dx_ref, gw_ref, loss_ref):
        @pl.when(pl.program_id(0) == 0)
        def _():
            gw_ref[...] = jnp.zeros_like(gw_ref)
            loss_ref[...] = jnp.zeros_like(loss_ref)

        xv = x_ref[...]
        r = lax.rsqrt(jnp.mean(xv * xv, axis=-1, keepdims=True) + EPS)
        xn = xv * r
        e = xn * w_ref[...] - t_ref[...]
        per_tok = jnp.mean(e * e, axis=-1, keepdims=True)
        loss_ref[...] += 0.5 * jnp.sum(per_tok)
        dyv = e * (1.0 / D)
        gw_ref[...] += jnp.broadcast_to(jnp.sum(dyv * xn, axis=0, keepdims=True), (8, D))
        dxn = dyv * w_ref[...]
        dx_ref[...] = r * (dxn - xn * jnp.mean(dxn * xn, axis=-1, keepdims=True))

    row = pl.BlockSpec((tl, D), lambda i: (i, 0))
    return pl.pallas_call(
        kern, grid=(L // tl,), in_specs=[row, pl.BlockSpec((1, D), lambda i: (0, 0)), row],
        out_specs=(row, pl.BlockSpec((8, D), lambda i: (0, 0)), pl.BlockSpec((8, 128), lambda i: (0, 0))),
        out_shape=(jax.ShapeDtypeStruct((L, D), F32), jax.ShapeDtypeStruct((8, D), F32),
                   jax.ShapeDtypeStruct((8, 128), F32)),
        name="final_norm_loss", compiler_params=_cp(("arbitrary",)),
    )(x2, w.reshape(1, D), tgt)


def _down(v, k):
    if k == 0:
        return v
    t = lax.broadcasted_iota(jnp.int32, v.shape, 0)
    return jnp.where(t >= k, pltpu.roll(v, k, axis=0), 0.0)


def _up(v, k):
    if k == 0:
        return v
    n = v.shape[0]
    t = lax.broadcasted_iota(jnp.int32, v.shape, 0)
    return jnp.where(t < n - k, pltpu.roll(v, n - k, axis=0), 0.0)


TW = 256


def _sc_fwd(proj, cw):
    L = proj.shape[0]
    nb = D // TW

    def kern(b_ref, c_ref, x_ref, w_ref, o_ref):
        u = c_ref[...].astype(F32) * x_ref[...].astype(F32)
        w = w_ref[...]
        cv = w[0:1] * _down(u, 2) + w[1:2] * _down(u, 1) + w[2:3] * u
        o_ref[...] = (b_ref[...].astype(F32) * cv).astype(BF16)

    col = lambda off: pl.BlockSpec((L, TW), lambda j: (0, off + j))
    return pl.pallas_call(
        kern, grid=(nb,), in_specs=[col(0), col(nb), col(2 * nb), pl.BlockSpec((8, TW), lambda j: (0, j))],
        out_specs=pl.BlockSpec((L, TW), lambda j: (0, j)), out_shape=jax.ShapeDtypeStruct((L, D), BF16),
        name="sc_fwd", compiler_params=_cp(("parallel",)),
    )(proj, proj, proj, cw)


def _sc_bwd(dya, proj, cw, dproj):
    L = proj.shape[0]
    nb = D // TW

    def kern(d_ref, b_ref, c_ref, x_ref, w_ref, _, dp_ref, gw_ref, keep):
        sec = pl.program_id(1)

        @pl.when(sec == 0)
        def _():
            cs, xs, dyv = c_ref[...].astype(F32), x_ref[...].astype(F32), d_ref[...]
            w = w_ref[...]
            u = cs * xs
            u1, u2 = _down(u, 1), _down(u, 2)
            cv = w[0:1] * u2 + w[1:2] * u1 + w[2:3] * u
            dcv = dyv * b_ref[...].astype(F32)
            du = w[2:3] * dcv + w[1:2] * _up(dcv, 1) + w[0:1] * _up(dcv, 2)
            g0 = jnp.sum(dcv * u2, axis=0, keepdims=True)
            g1 = jnp.sum(dcv * u1, axis=0, keepdims=True)
            g2 = jnp.sum(dcv * u, axis=0, keepdims=True)
            row = lax.broadcasted_iota(jnp.int32, (8, TW), 0)
            gw_ref[...] = jnp.where(row == 0, g0, jnp.where(row == 1, g1, jnp.where(row == 2, g2, 0.0)))
            dp_ref[...] = (dyv * cv).astype(BF16)
            keep[0] = (du * xs).astype(BF16)
            keep[1] = (du * cs).astype(BF16)

        @pl.when(sec > 0)
        def _():
            dp_ref[...] = keep[sec - 1]

    col = lambda off: pl.BlockSpec((L, TW), lambda j, s: (0, off + j))
    return pl.pallas_call(
        kern, grid=(nb, 3),
        in_specs=[col(0), col(0), col(nb), col(2 * nb), pl.BlockSpec((8, TW), lambda j, s: (0, j)),
                  pl.BlockSpec(memory_space=pl.ANY)],
        out_specs=(pl.BlockSpec((L, TW), lambda j, s: (0, s * nb + j)), pl.BlockSpec((8, TW), lambda j, s: (0, j))),
        out_shape=(jax.ShapeDtypeStruct(dproj.shape, BF16), jax.ShapeDtypeStruct((8, D), F32)),
        scratch_shapes=[pltpu.VMEM((2, L, TW), BF16)],
        input_output_aliases={5: 0}, name="sc_bwd", compiler_params=_cp(("parallel", "arbitrary")),
    )(dya, proj, proj, proj, cw, dproj)


def _ssm_conv_fwd(proj, cw4):
    L = proj.shape[0]
    off = C_XBC // TW

    def kern(r_ref, w_ref, o_ref):
        raw = r_ref[...].astype(F32)
        w = w_ref[...]
        c4 = w[0:1] * _down(raw, 3) + w[1:2] * _down(raw, 2) + w[2:3] * _down(raw, 1) + w[3:4] * raw + w[4:5]
        o_ref[...] = c4 * _sigmoid(c4)

    return pl.pallas_call(
        kern, grid=(XBC // TW,),
        in_specs=[pl.BlockSpec((L, TW), lambda j: (0, off + j)), pl.BlockSpec((8, TW), lambda j: (0, j))],
        out_specs=pl.BlockSpec((L, TW), lambda j: (0, j)), out_shape=jax.ShapeDtypeStruct((L, XBC), F32),
        name="ssm_conv_fwd", compiler_params=_cp(("parallel",)),
    )(proj, cw4)


def _ssm_conv_bwd(dx, proj, cw4, dproj, col0, name):
    L, width = dx.shape
    off_p = (C_XBC + col0) // TW
    off_w = col0 // TW

    def kern(d_ref, r_ref, w_ref, _, dp_ref, gw_ref):
        raw = r_ref[...].astype(F32)
        w = w_ref[...]
        r1, r2, r3 = _down(raw, 1), _down(raw, 2), _down(raw, 3)
        c4 = w[0:1] * r3 + w[1:2] * r2 + w[2:3] * r1 + w[3:4] * raw + w[4:5]
        sg = _sigmoid(c4)
        dc4 = d_ref[...] * (sg * (1.0 + c4 * (1.0 - sg)))
        draw = w[3:4] * dc4 + w[2:3] * _up(dc4, 1) + w[1:2] * _up(dc4, 2) + w[0:1] * _up(dc4, 3)
        dp_ref[...] = draw.astype(BF16)
        gs = [jnp.sum(dc4 * r3, axis=0, keepdims=True), jnp.sum(dc4 * r2, axis=0, keepdims=True),
              jnp.sum(dc4 * r1, axis=0, keepdims=True), jnp.sum(dc4 * raw, axis=0, keepdims=True),
              jnp.sum(dc4, axis=0, keepdims=True)]
        row = lax.broadcasted_iota(jnp.int32, (8, TW), 0)
        acc = jnp.zeros((8, TW), F32)
        for k, gk in enumerate(gs):
            acc = jnp.where(row == k, gk, acc)
        gw_ref[...] = acc

    return pl.pallas_call(
        kern, grid=(width // TW,),
        in_specs=[pl.BlockSpec((L, TW), lambda j: (0, j)), pl.BlockSpec((L, TW), lambda j: (0, off_p + j)),
                  pl.BlockSpec((8, TW), lambda j: (0, off_w + j)), pl.BlockSpec(memory_space=pl.ANY)],
        out_specs=(pl.BlockSpec((L, TW), lambda j: (0, off_p + j)), pl.BlockSpec((8, TW), lambda j: (0, j))),
        out_shape=(jax.ShapeDtypeStruct(dproj.shape, BF16), jax.ShapeDtypeStruct((8, width), F32)),
        input_output_aliases={3: 0}, name=name, compiler_params=_cp(("arbitrary",)),
    )(dx, proj, cw4, dproj)


def _split3(v):
    h1 = v.astype(BF16)
    r1 = v - h1.astype(F32)
    h2 = r1.astype(BF16)
    h3 = (r1 - h2.astype(F32)).astype(BF16)
    return h1, h2, h3


def _dot01(m01, v, dims=_DIMS["nn"], m_left=True, terms=3):
    out = None
    for part in _split3(v)[:terms]:
        ops = (m01, part) if m_left else (part, m01)
        t = lax.dot_general(ops[0], ops[1], dims, preferred_element_type=F32)
        out = t if out is None else out + t
    return out


def _bdot(a, b, mode="nn"):
    return lax.dot_general(a.astype(BF16), b.astype(BF16), _DIMS[mode], preferred_element_type=F32)


def _softplus(v):
    return jnp.maximum(v, 0.0) + jnp.log1p(jnp.exp(-jnp.abs(v)))


def _dt_prep(proj, vec):
    L = proj.shape[0]

    def kern(p_ref, v_ref, dt_ref, cs_ref, sg_ref):
        v = v_ref[...]
        pre = p_ref[:, 0:128] + v[0:1]
        dt = _softplus(pre)
        da = dt * (-jnp.exp(v[1:2]))
        ii = lax.broadcasted_iota(jnp.int32, (Q, Q), 0)
        jj = lax.broadcasted_iota(jnp.int32, (Q, Q), 1)
        ltri = (jj <= ii).astype(BF16)
        lane = lax.broadcasted_iota(jnp.int32, (Q, 128), 1)
        for val, ref in ((dt, dt_ref), (_dot01(ltri, da), cs_ref), (_sigmoid(pre), sg_ref)):
            for g in range(NG):
                moved = val if g == 0 else pltpu.roll(val, 128 - 4 * g, axis=1)
                ref[g] = jnp.where(lane < 4, moved, 0.0)

    blk = pl.BlockSpec((NG, Q, 128), lambda c: (0, c, 0))
    return pl.pallas_call(
        kern, grid=(L // Q,),
        in_specs=[pl.BlockSpec((Q, 256), lambda c: (c, 0)), pl.BlockSpec((8, 128), lambda c: (0, 0))],
        out_specs=(blk, blk, blk),
        out_shape=(jax.ShapeDtypeStruct((NG, L, 128), F32),) * 3,
        name="dt_prep", compiler_params=_cp(("parallel",)),
    )(proj, vec)


def _head_masks():
    lane = lax.broadcasted_iota(jnp.int32, (1, 4 * HD), 1)
    return [((lane >= HD * j) & (lane < HD * (j + 1))) for j in range(4)]


def _expand4(v4, masks):
    R = v4.shape[0]
    out = jnp.zeros((R, 4 * HD), F32)
    for j in range(4):
        out = jnp.where(masks[j], jnp.broadcast_to(v4[:, j:j + 1], (R, 4 * HD)), out)
    return out


def _decay_matrix(cs_col, tri):
    colb = jnp.broadcast_to(cs_col, (Q, Q))
    return jnp.exp(jnp.where(tri, colb - colb.T, -jnp.inf))


def _ssd_fwd(xbc, dt4, cs4, vecg):
    L = xbc.shape[0]
    nc = L // Q

    def kern(x_ref, b_ref, c_ref, dt_ref, cs_ref, v_ref, y_ref, s_ref, S):
        c = pl.program_id(1)

        @pl.when(c == 0)
        def _():
            S[...] = jnp.zeros_like(S)

        masks = _head_masks()
        ii = lax.broadcasted_iota(jnp.int32, (Q, Q), 0)
        jj = lax.broadcasted_iota(jnp.int32, (Q, Q), 1)
        tri = jj <= ii
        for gi in range(GPS):
            xs, ns = slice(256 * gi, 256 * (gi + 1)), slice(NS * gi, NS * (gi + 1))
            dt4v, cs4v = dt_ref[gi], cs_ref[gi]
            dt_b, cs_b = _expand4(dt4v, masks), _expand4(cs4v, masks)
            d_b = _expand4(v_ref[gi], masks)[1:2]
            cs_last = cs_b[Q - 1:Q, :]
            x4, bm, cm = x_ref[:, xs], b_ref[:, ns], c_ref[:, ns]
            xdt = x4 * dt_b
            gm = _bdot(cm, bm, "nt")
            s4 = S[gi]
            s_ref[gi, 0] = s4
            y = _bdot(cm, s4) * jnp.exp(cs_b) + d_b * x4
            m_all = jnp.concatenate([(gm * _decay_matrix(cs4v[:, j:j + 1], tri)).astype(BF16) for j in range(4)], axis=0)
            yd = _bdot(m_all, xdt)
            for j in range(4):
                y = y + jnp.where(masks[j], yd[Q * j:Q * (j + 1)], 0.0)
            y_ref[:, xs] = y
            S[gi] = jnp.exp(cs_last) * s4 + _bdot(bm, xdt * jnp.exp(cs_last - cs_b), "tn")

    sc = pl.BlockSpec((GPS, Q, 128), lambda g, c: (g, c, 0))
    bw = NS * GPS
    return pl.pallas_call(
        kern, grid=(NG // GPS, nc),
        in_specs=[pl.BlockSpec((Q, 256 * GPS), lambda g, c: (c, g)),
                  pl.BlockSpec((Q, bw), lambda g, c: (c, INNER // bw + g)),
                  pl.BlockSpec((Q, bw), lambda g, c: (c, (INNER + NG * NS) // bw + g)),
                  sc, sc, pl.BlockSpec((GPS, 8, 128), lambda g, c: (g, 0, 0))],
        out_specs=(pl.BlockSpec((Q, 256 * GPS), lambda g, c: (c, g)),
                   pl.BlockSpec((GPS, 1, NS, 256), lambda g, c: (g, c, 0, 0))),
        out_shape=(jax.ShapeDtypeStruct((L, INNER), F32), jax.ShapeDtypeStruct((NG, nc, NS, 256), F32)),
        scratch_shapes=[pltpu.VMEM((GPS, NS, 256), F32)], name="ssd_fwd",
        compiler_params=_cp(("parallel", "arbitrary")),
    )(xbc, xbc, xbc, dt4, cs4, vecg)


def _ssd_bwd(xbc, dt4, cs4, sg4, vecg, s_all, dy):
    L = xbc.shape[0]
    nc = L // Q

    def kern(x_ref, b_ref, c_ref, dt_ref, cs_ref, sg_ref, v_ref, s_ref, dy_ref,
             dx_ref, db_ref, dc_ref, ddt_ref, st_ref, dS):
        cc = pl.program_id(1)

        @pl.when(cc == 0)
        def _():
            dS[...] = jnp.zeros_like(dS)
            st_ref[...] = jnp.zeros_like(st_ref)

        masks = _head_masks()
        ii = lax.broadcasted_iota(jnp.int32, (Q, Q), 0)
        jj = lax.broadcasted_iota(jnp.int32, (Q, Q), 1)
        tri = jj <= ii
        utri = (jj >= ii).astype(BF16)
        li = lax.broadcasted_iota(jnp.int32, (4 * HD, 4 * HD), 0)
        lj = lax.broadcasted_iota(jnp.int32, (4 * HD, 4 * HD), 1)
        eblk = ((li // HD) == (lj // HD)).astype(BF16)
        lane128 = lax.broadcasted_iota(jnp.int32, (Q, 128), 1)

        for gi in range(GPS):
            xs, ns = slice(256 * gi, 256 * (gi + 1)), slice(NS * gi, NS * (gi + 1))
            dt4v, cs4v, sg4v = dt_ref[gi], cs_ref[gi], sg_ref[gi]
            dt_b, cs_b = _expand4(dt4v, masks), _expand4(cs4v, masks)
            vv = _expand4(v_ref[gi], masks)
            a_b = -jnp.exp(vv[0:1])
            d_b = vv[1:2]
            a4 = -jnp.exp(v_ref[gi][0:1, :])
            cs_last = cs_b[Q - 1:Q, :]
            ecs = jnp.exp(cs_b)
            decay = jnp.exp(cs_last - cs_b)
            elast = jnp.exp(cs_last)
            x4, bm, cm, dyv = x_ref[:, xs], b_ref[:, ns], c_ref[:, ns], dy_ref[:, xs]
            s4 = s_ref[gi, 0]
            dsn = dS[gi]
            xdt = x4 * dt_b
            gm = _bdot(cm, bm, "nt")
            gmt = gm.T
            dye = dyv * ecs
            yoff = ecs * _bdot(cm, s4)
            t4 = _bdot(bm, dsn) * decay
            lms, mhs, mhts = [], [], []
            for j in range(4):
                colb = jnp.broadcast_to(cs4v[:, j:j + 1], (Q, Q))
                seg = colb - colb.T
                lms.append(jnp.exp(jnp.where(tri, seg, -jnp.inf)))
                mhs.append(gm * lms[j])
                mhts.append(gmt * jnp.exp(jnp.where(jj >= ii, -seg, -jnp.inf)))
            m_all = jnp.concatenate([m.astype(BF16) for m in mhs], axis=0)
            dy_m = jnp.concatenate([jnp.where(masks[j], dyv, 0.0).astype(BF16) for j in range(4)], axis=0)
            x_m = jnp.concatenate([jnp.where(masks[j], xdt, 0.0).astype(BF16) for j in range(4)], axis=0)
            dxdt = t4 + _bdot(m_all, dy_m, "tn")
            dm_all = _bdot(dy_m, xdt, "nt")
            dmt_all = _bdot(x_m, dyv, "nt")
            dg = jnp.zeros((Q, Q), F32)
            rc = jnp.zeros((Q, 4 * HD), F32)
            for j in range(4):
                dmh = dm_all[Q * j:Q * (j + 1)]
                dg = dg + dmh * lms[j]
                rs = (jnp.sum(dmh * mhs[j], axis=1, keepdims=True)
                      - jnp.sum(dmt_all[Q * j:Q * (j + 1)] * mhts[j], axis=1, keepdims=True))
                rc = jnp.where(masks[j], jnp.broadcast_to(rs, (Q, 4 * HD)), rc)
            xt = xdt * t4
            tail = jnp.sum(xt, axis=0, keepdims=True) + elast * jnp.sum(s4 * dsn, axis=0, keepdims=True)
            gd_raw = jnp.sum(dyv * x4, axis=0, keepdims=True)
            stacked = jnp.concatenate([dyv * yoff - xt, dxdt * x4, jnp.broadcast_to(tail, (8, 4 * HD)),
                                       jnp.broadcast_to(gd_raw, (8, 4 * HD))], axis=0)
            seg = _dot01(eblk, stacked, m_left=False, terms=2)
            da_b = seg[0:Q] + rc
            dda_b = _dot01(utri, da_b, terms=2) + seg[2 * Q:2 * Q + 1]
            ddt_b = dda_b * a_b + seg[Q:2 * Q]
            gd_b = seg[2 * Q + 8:2 * Q + 9]
            ddt4 = jnp.zeros((Q, 128), F32)
            dda4 = jnp.zeros((Q, 128), F32)
            for j in range(4):
                ddt4 = jnp.where(lane128 == j, jnp.broadcast_to(ddt_b[:, HD * j:HD * j + 1], (Q, 128)), ddt4)
                dda4 = jnp.where(lane128 == j, jnp.broadcast_to(dda_b[:, HD * j:HD * j + 1], (Q, 128)), dda4)
            ddt_ref[gi] = ddt4 * sg4v
            ga = jnp.sum(dda4 * dt4v * a4, axis=0, keepdims=True)
            gd = jnp.zeros((1, 128), F32)
            for j in range(4):
                gd = jnp.where(lane128[0:1] == j, jnp.broadcast_to(gd_b[:, HD * j:HD * j + 1], (1, 128)), gd)
            row = lax.broadcasted_iota(jnp.int32, (8, 128), 0)
            st_ref[gi] += jnp.where(row == 0, ga, jnp.where(row == 1, gd, 0.0))
            dx_ref[:, xs] = d_b * dyv + dxdt * dt_b
            dc_ref[:, ns] = _bdot(dg, bm) + _bdot(dye, s4, "nt")
            db_ref[:, ns] = _bdot(dg, cm, "tn") + _bdot(xdt * decay, dsn, "nt")
            dS[gi] = elast * dsn + _bdot(cm, dye, "tn")

    rv = lambda c: nc - 1 - c
    sc = pl.BlockSpec((GPS, Q, 128), lambda g, c: (g, rv(c), 0))
    bw = NS * GPS
    return pl.pallas_call(
        kern, grid=(NG // GPS, nc),
        in_specs=[pl.BlockSpec((Q, 256 * GPS), lambda g, c: (rv(c), g)),
                  pl.BlockSpec((Q, bw), lambda g, c: (rv(c), INNER // bw + g)),
                  pl.BlockSpec((Q, bw), lambda g, c: (rv(c), (INNER + NG * NS) // bw + g)),
                  sc, sc, sc, pl.BlockSpec((GPS, 8, 128), lambda g, c: (g, 0, 0)),
                  pl.BlockSpec((GPS, 1, NS, 256), lambda g, c: (g, rv(c), 0, 0)),
                  pl.BlockSpec((Q, 256 * GPS), lambda g, c: (rv(c), g))],
        out_specs=(pl.BlockSpec((Q, 256 * GPS), lambda g, c: (rv(c), g)),
                   pl.BlockSpec((Q, bw), lambda g, c: (rv(c), g)),
                   pl.BlockSpec((Q, bw), lambda g, c: (rv(c), g)),
                   pl.BlockSpec((GPS, Q, 128), lambda g, c: (g, rv(c), 0)),
                   pl.BlockSpec((GPS, 8, 128), lambda g, c: (g, 0, 0))),
        out_shape=(jax.ShapeDtypeStruct((L, INNER), F32), jax.ShapeDtypeStruct((L, NG * NS), F32),
                   jax.ShapeDtypeStruct((L, NG * NS), F32), jax.ShapeDtypeStruct((NG, L, 128), F32),
                   jax.ShapeDtypeStruct((NG, 8, 128), F32)),
        scratch_shapes=[pltpu.VMEM((GPS, NS, 256), F32)], name="ssd_bwd",
        compiler_params=_cp(("parallel", "arbitrary")),
    )(xbc, xbc, xbc, dt4, cs4, sg4, vecg, s_all, dy)


def _dt_bwd(ddt, dproj, tl=256):
    L = ddt.shape[1]

    def kern(d_ref, _, dp_ref, gs_ref):
        @pl.when(pl.program_id(0) == 0)
        def _():
            gs_ref[...] = jnp.zeros_like(gs_ref)

        d = d_ref[0]
        for g in range(1, NG):
            d = d + pltpu.roll(d_ref[g], 4 * g, axis=1)
        gs_ref[...] += jnp.broadcast_to(jnp.sum(d, axis=0, keepdims=True), (8, 128))
        dp_ref[...] = jnp.concatenate([d, jnp.zeros_like(d)], axis=1).astype(BF16)

    return pl.pallas_call(
        kern, grid=(L // tl,),
        in_specs=[pl.BlockSpec((NG, tl, 128), lambda i: (0, i, 0)), pl.BlockSpec(memory_space=pl.ANY)],
        out_specs=(pl.BlockSpec((tl, 256), lambda i: (i, C_DT // 256)), pl.BlockSpec((8, 128), lambda i: (0, 0))),
        out_shape=(jax.ShapeDtypeStruct(dproj.shape, BF16), jax.ShapeDtypeStruct((8, 128), F32)),
        input_output_aliases={1: 0}, name="dt_bwd", compiler_params=_cp(("arbitrary",)),
    )(ddt, dproj)


GW = INNER // NG


def _gnorm_fwd(y, proj, w, tl=256):
    L = y.shape[0]
    zoff = C_Z // 1024

    def kern(y_ref, z_ref, w_ref, o_ref):
        z = z_ref[...].astype(F32)
        yz = y_ref[...] * (z * _sigmoid(z))
        wv = w_ref[...]
        for k in range(1024 // GW):
            sl = slice(GW * k, GW * (k + 1))
            v = yz[:, sl]
            rg = lax.rsqrt(jnp.mean(v * v, axis=-1, keepdims=True) + EPS)
            o_ref[:, sl] = ((v * rg) * wv[:, sl]).astype(BF16)

    blk = pl.BlockSpec((tl, 1024), lambda i, j: (i, j))
    return pl.pallas_call(
        kern, grid=(L // tl, 2),
        in_specs=[blk, pl.BlockSpec((tl, 1024), lambda i, j: (i, zoff + j)), pl.BlockSpec((1, 1024), lambda i, j: (0, j))],
        out_specs=blk, out_shape=jax.ShapeDtypeStruct((L, INNER), BF16), name="gnorm_fwd",
        compiler_params=_cp(("parallel", "parallel")),
    )(y, proj, w.reshape(1, INNER))


def _gnorm_bwd(dyb, y, proj, w, dproj, tl=256):
    L = y.shape[0]
    zoff = C_Z // 1024

    def kern(d_ref, y_ref, z_ref, w_ref, _, dy_ref, dp_ref, gw_ref):
        @pl.when(pl.program_id(1) == 0)
        def _():
            gw_ref[...] = jnp.zeros_like(gw_ref)

        z = z_ref[...].astype(F32)
        sg = _sigmoid(z)
        sz = z * sg
        yv = y_ref[...]
        yz = yv * sz
        dv = d_ref[...]
        wv = w_ref[...]
        for k in range(1024 // GW):
            sl = slice(GW * k, GW * (k + 1))
            v = yz[:, sl]
            rg = lax.rsqrt(jnp.mean(v * v, axis=-1, keepdims=True) + EPS)
            vn = v * rg
            dk = dv[:, sl]
            gw_ref[:, sl] += jnp.broadcast_to(jnp.sum(dk * vn, axis=0, keepdims=True), (8, GW))
            dvn = dk * wv[:, sl]
            dyz = rg * (dvn - vn * jnp.mean(dvn * vn, axis=-1, keepdims=True))
            dy_ref[:, sl] = dyz * sz[:, sl]
            dp_ref[:, sl] = (dyz * yv[:, sl] * (sg[:, sl] * (1.0 + z[:, sl] * (1.0 - sg[:, sl])))).astype(BF16)

    blk = pl.BlockSpec((tl, 1024), lambda j, i: (i, j))
    zblk = pl.BlockSpec((tl, 1024), lambda j, i: (i, zoff + j))
    return pl.pallas_call(
        kern, grid=(2, L // tl),
        in_specs=[blk, blk, zblk, pl.BlockSpec((1, 1024), lambda j, i: (0, j)), pl.BlockSpec(memory_space=pl.ANY)],
        out_specs=(blk, zblk, pl.BlockSpec((8, 1024), lambda j, i: (0, j))),
        out_shape=(jax.ShapeDtypeStruct((L, INNER), F32), jax.ShapeDtypeStruct(dproj.shape, BF16),
                   jax.ShapeDtypeStruct((8, INNER), F32)),
        input_output_aliases={4: 1}, name="gnorm_bwd", compiler_params=_cp(("parallel", "arbitrary")),
    )(dyb, y, proj, w.reshape(1, INNER), dproj)


def _merge_fwd(proj, bg, br_a, br_b, tl=256):
    L = proj.shape[0]
    goff = C_GATE // 1024

    def kern(g1_ref, g2_ref, b1_ref, b2_ref, a_ref, b_ref, o_ref):
        g1 = _sigmoid(g1_ref[...].astype(F32) + b1_ref[...])
        g2 = _sigmoid(g2_ref[...].astype(F32) + b2_ref[...])
        o_ref[...] = (g1 * a_ref[...] + g2 * b_ref[...]).astype(BF16)

    row = pl.BlockSpec((tl, 1024), lambda i: (i, 0))
    bg2 = bg.reshape(1, 2 * D)
    return pl.pallas_call(
        kern, grid=(L // tl,),
        in_specs=[pl.BlockSpec((tl, 1024), lambda i: (i, goff)), pl.BlockSpec((tl, 1024), lambda i: (i, goff + 1)),
                  pl.BlockSpec((1, 1024), lambda i: (0, 0)), pl.BlockSpec((1, 1024), lambda i: (0, 1)), row, row],
        out_specs=row, out_shape=jax.ShapeDtypeStruct((L, D), BF16), name="merge_fwd",
        compiler_params=_cp(("parallel",)),
    )(proj, proj, bg2, bg2, br_a, br_b)


def _merge_bwd(dm, proj, bg, br_a, br_b, dproj, tl=256):
    L = proj.shape[0]
    goff = C_GATE // 1024

    def kern(dm_ref, g_ref, b_ref, a_ref, bb_ref, _, dbr_ref, dp_ref, gb_ref):
        j = pl.program_id(0)

        @pl.when(pl.program_id(1) == 0)
        def _():
            gb_ref[...] = jnp.zeros_like(gb_ref)

        g = _sigmoid(g_ref[...].astype(F32) + b_ref[...])
        br = jnp.where(j == 0, a_ref[...], bb_ref[...])
        dmv = dm_ref[...]
        dbr_ref[0] = (dmv * g).astype(BF16)
        dgate = dmv * br * g * (1.0 - g)
        gb_ref[...] += jnp.broadcast_to(jnp.sum(dgate, axis=0, keepdims=True), (8, 1024))
        dp_ref[...] = dgate.astype(BF16)

    row = pl.BlockSpec((tl, 1024), lambda j, i: (i, 0))
    gblk = pl.BlockSpec((tl, 1024), lambda j, i: (i, goff + j))
    return pl.pallas_call(
        kern, grid=(2, L // tl),
        in_specs=[row, gblk, pl.BlockSpec((1, 1024), lambda j, i: (0, j)), row, row, pl.BlockSpec(memory_space=pl.ANY)],
        out_specs=(pl.BlockSpec((1, tl, 1024), lambda j, i: (j, i, 0)), gblk, pl.BlockSpec((8, 1024), lambda j, i: (0, j))),
        out_shape=(jax.ShapeDtypeStruct((2, L, D), BF16), jax.ShapeDtypeStruct(dproj.shape, BF16),
                   jax.ShapeDtypeStruct((8, 2 * D), F32)),
        input_output_aliases={5: 1}, name="merge_bwd", compiler_params=_cp(("parallel", "arbitrary")),
    )(dm, proj, bg.reshape(1, 2 * D), br_a, br_b, dproj)


def _coords():
    return lax.axis_index("x"), lax.axis_index("y"), lax.axis_index("c")


def _other_chips(sk):
    xk, yk = sk // 2, sk % 2
    return [((1 - xk, yk), 2 * (1 - xk) + yk), ((xk, 1 - yk), 2 * xk + 1 - yk), ((1 - xk, 1 - yk), 2 * (1 - xk) + 1 - yk)]


def _rows(start, size):
    assert size % 128 == 0
    return pl.ds(pl.multiple_of(start, 128), size)


def _per_chip(fn):
    x, y, _ = _coords()
    s = 2 * x + y
    for sk in range(4):
        pl.when(s == sk)(functools.partial(fn, sk))


XTRA = PIECE - PMAIN


def _place(shard, full_shape, block, index_map, idx, name, blk0=0, nblk=None, dep=None):
    in_block = block[-2:]
    if nblk is None:
        nblk = shard.shape[0] // in_block[0]

    def kern(idx_ref, s_ref, *rest):
        o_ref = rest[-1]
        o_ref[...] = s_ref[...].astype(BF16).reshape(o_ref.shape)

    grid_spec = pltpu.PrefetchScalarGridSpec(
        num_scalar_prefetch=1, grid=(nblk,),
        in_specs=[pl.BlockSpec(in_block, lambda i, idx_ref: (blk0 + i, 0))] + ([_ANY] if dep is not None else []),
        out_specs=pl.BlockSpec(block, index_map))
    args = (idx, shard) + ((dep,) if dep is not None else ())
    return pl.pallas_call(kern, grid_spec=grid_spec, out_shape=jax.ShapeDtypeStruct(full_shape, BF16), name=name,
                          compiler_params=_cp(("arbitrary",)))(*args)


_SEM = pl.BlockSpec(memory_space=pltpu.SEMAPHORE)
_EFFECT = pltpu.SideEffectType.DATAFLOW_SIDE_EFFECTING


_ANY = pl.BlockSpec(memory_space=pl.ANY)


def _tie(v, dep, name):
    def body(v_ref, dep_ref, o_ref):
        del v_ref, dep_ref, o_ref

    return pl.pallas_call(body, out_shape=jax.ShapeDtypeStruct(v.shape, v.dtype), in_specs=[_ANY, _ANY],
                          out_specs=_ANY, input_output_aliases={0: 0}, name=name)(v, dep)


def _split_call(name, arrays, start=None, wait=None, wait_sems=None, after=None):
    keys = list(arrays)
    n = len(keys)
    n_start = start.n if start is not None else 0
    afters = [] if after is None else (list(after) if isinstance(after, (list, tuple)) else [after])

    def body(*refs):
        pos = n
        if wait is not None:
            wss, wrs = refs[pos], refs[pos + 1]
            pos += 2
        pos += len(afters)
        if start is not None:
            nss, nrs = refs[pos], refs[pos + 1]
            pos += 2
        R = dict(zip(keys, refs[pos:pos + n]))
        token = refs[pos + n]
        x, y, c = _coords()

        def desc(src, dst, dev, ss, rs, k):
            return pltpu.make_async_remote_copy(src_ref=src, dst_ref=dst, send_sem=ss.at[k], recv_sem=rs.at[k],
                                                device_id=dev, device_id_type=MESH)

        def run(sk):
            if wait is not None:
                for k, (snd, land) in enumerate(wait.copies(sk, R)):
                    if snd is not None:
                        desc(snd[0], snd[1], snd[2], wss, wrs, k).wait_send()
                    if land is not None:
                        desc(land, land, (x, y, c), wss, wrs, k).wait_recv()
            if start is not None:
                for k, (snd, land) in enumerate(start.copies(sk, R)):
                    if snd is not None:
                        desc(snd[0], snd[1], snd[2], nss, nrs, k).start()

        _per_chip(run)
        token[...] = jnp.zeros_like(token)

    hbm = pl.BlockSpec(memory_space=HBM)
    vals = [arrays[k] for k in keys]
    ins, in_specs = list(vals), [hbm] * n
    if wait is not None:
        ins += list(wait_sems)
        in_specs += [_SEM, _SEM]
    ins += afters
    in_specs += [pl.BlockSpec(memory_space=pl.ANY)] * len(afters)
    out_shape, out_specs = [], []
    if start is not None:
        out_shape += [pltpu.SemaphoreType.DMA((n_start,)), pltpu.SemaphoreType.DMA((n_start,))]
        out_specs += [_SEM, _SEM]
    first = len(out_shape)
    out_shape += [jax.ShapeDtypeStruct(v.shape, v.dtype) for v in vals] + [jax.ShapeDtypeStruct((8, 128), F32)]
    out_specs += [hbm] * n + [pl.BlockSpec(memory_space=pltpu.VMEM)]
    res = pl.pallas_call(
        body, out_shape=tuple(out_shape), in_specs=in_specs, out_specs=tuple(out_specs),
        input_output_aliases={i: first + i for i in range(n)}, name=name,
        compiler_params=pltpu.CompilerParams(has_side_effects=_EFFECT),
    )(*ins)
    sems = (res[0], res[1]) if start is not None else None
    return dict(zip(keys, res[first:first + n])), sems, res[-1]


class _Plan:
    def __init__(self, n, copies):
        self.n, self.copies = n, copies


_HM, _HX = PMAIN // 2, XTRA // 2
_WIN = {
    "wct": (True, lambda r, sc, hc: r.at[_rows(PMAIN * sc + _HM * hc, _HM), :]),
    "xt": (True, lambda r, sc, hc: r.at[sc, _rows(_HX * hc, _HX), :]),
    "w1": (True, lambda r, sc, hc: r.at[_rows(512 * hc, 512), pl.ds(1024 * sc, 1024)]),
    "w2": (True, lambda r, sc, hc: r.at[_rows(1024 * sc + 512 * hc, 512), :]),
    "wa": (True, lambda r, sc, hc: r.at[_rows(256 * sc + 128 * hc, 128), :]),
    "wb": (True, lambda r, sc, hc: r.at[_rows(512 * sc + 256 * hc, 256), :]),
    "wo": (True, lambda r, sc, hc: r.at[_rows(256 * sc + 128 * hc, 128), :]),
    "cw": (False, lambda r, sc, hc: r.at[sc]),
}


def _ag_chips_plan(keys):
    def copies(sk, R):
        _, _, c = _coords()
        out = []
        for key in keys:
            win = _WIN[key][1]
            for (px, py), ps in _other_chips(sk):
                w = win(R[key], sk, c)
                out.append(((w, w, (px, py, c)), win(R[key], ps, c)))
        return out
    return _Plan(3 * len(keys), copies)


def _ag_sibling_plan(keys):
    keys = [k for k in keys if _WIN[k][0]]

    def copies(sk, R):
        x, y, c = _coords()
        out = []
        for key in keys:
            win = _WIN[key][1]
            for _, ps in _other_chips(sk):
                w = win(R[key], ps, c)
                out.append(((w, w, (x, y, 1 - c)), win(R[key], ps, 1 - c)))
        return out
    return _Plan(3 * len(keys), copies)


def _fix_wct(wct, xt):
    nb = PMAIN // XTRA

    def kern(w_ref, x_ref, o_ref):
        k = pl.program_id(0)
        xv = x_ref[0]
        o_ref[...] = jnp.where(k < 3, (w_ref[...].astype(F32) + xv.astype(F32)).astype(BF16), xv)

    blk = pl.BlockSpec((XTRA, D), lambda k: (nb * (k + 1), 0))
    rblk = pl.BlockSpec((XTRA, D), lambda k: (jnp.where(k < 3, nb * (k + 1), 0), 0))
    return pl.pallas_call(
        kern, grid=(4,), in_specs=[rblk, pl.BlockSpec((1, XTRA, D), lambda k: (k, 0, 0))], out_specs=blk,
        out_shape=jax.ShapeDtypeStruct(wct.shape, BF16), input_output_aliases={0: 0}, name="fix_wct",
        compiler_params=_cp(("arbitrary",)),
    )(wct, xt)


_HP = PIECE // 2
_GWIN = [
    lambda r, sc, hc: r.at[_rows(PMAIN * sc + _HP * hc, _HP), :],
    lambda r, sc, hc: r.at[_rows(512 * hc, 512), pl.ds(1024 * sc, 1024)],
    lambda r, sc, hc: r.at[_rows(1024 * sc + 512 * hc, 512), :],
    lambda r, sc, hc: r.at[_rows(256 * sc + 128 * hc, 128), :],
    lambda r, sc, hc: r.at[_rows(512 * sc + 256 * hc, 256), :],
    lambda r, sc, hc: r.at[_rows(256 * sc + 128 * hc, 128), :],
]
HALF_SHAPES = [(PIECE // 2, D), (512, 1024), (512, 1024), (128, 1024), (256, 1024), (128, 1024)]


def _rs_sibling_plan(ts):
    def copies(sk, R):
        x, y, c = _coords()
        out = []
        for t in ts:
            for sc in range(4):
                land = R["ra%d" % t].at[sc]
                out.append(((_GWIN[t](R["g%d" % t], sc, 1 - c), land, (x, y, 1 - c)), land))
        return out
    return _Plan(4 * len(ts), copies)


def _rs_chips_plan(ts):
    def copies(sk, R):
        _, _, c = _coords()
        out = []
        for t in ts:
            for j, ((px, py), ps) in enumerate(_other_chips(sk)):
                land = R["rb%d" % t].at[j]
                out.append(((R["hb%d" % t].at[ps], land, (px, py, c)), land))
        return out
    return _Plan(3 * len(ts), copies)


def _rs_share_plan(ts):
    def copies(sk, R):
        x, y, c = _coords()
        out = []
        for t in ts:
            rows = HALF_SHAPES[t][0]
            mine = R["f%d" % t].at[_rows(rows * c, rows), :]
            out.append(((mine, mine, (x, y, 1 - c)), R["f%d" % t].at[_rows(rows * (1 - c), rows), :]))
        return out
    return _Plan(len(ts), copies)


def _half_tiling(t):
    rows, cols = HALF_SHAPES[t]
    if t == 0:
        return (256, cols), rows // 256, lambda i: (i, 0)
    if t == 1:
        return (rows, 256), cols // 256, lambda i: (0, i)
    return (rows, cols), 1, lambda i: (0, 0)


def _window_block(t, sc, hc, i):
    if t == 0:
        return (PMAIN // 256) * sc + (PIECE // 512) * hc + i, 0
    if t == 1:
        return hc, 4 * sc + i
    return 2 * sc + hc, 0


def _chip_sum(g, ra, t, idx, name):
    rows, cols = HALF_SHAPES[t]
    blk, nblk, inner = _half_tiling(t)

    def kern(idx_ref, g_ref, r_ref, hb_ref, hf_ref):
        v = g_ref[...].astype(F32) + r_ref[0].astype(F32)
        hb_ref[0] = v.astype(BF16)

        @pl.when(pl.program_id(1) == idx_ref[0])
        def _():
            hf_ref[...] = v

    gmap = lambda i, sc, idx_ref: _window_block(t, sc, idx_ref[1], i)
    omap = lambda i, sc, idx_ref: (sc,) + inner(i)
    grid_spec = pltpu.PrefetchScalarGridSpec(
        num_scalar_prefetch=1, grid=(nblk, 4),
        in_specs=[pl.BlockSpec(blk, gmap), pl.BlockSpec((1,) + blk, omap)],
        out_specs=(pl.BlockSpec((1,) + blk, omap), pl.BlockSpec(blk, lambda i, sc, idx_ref: inner(i))))
    return pl.pallas_call(
        kern, grid_spec=grid_spec,
        out_shape=(jax.ShapeDtypeStruct((4, rows, cols), BF16), jax.ShapeDtypeStruct((rows, cols), F32)),
        name=name, compiler_params=_cp(("parallel", "arbitrary")),
    )(idx, g, ra)


def _final_sum(hf, rb, t, idx, name):
    rows, cols = HALF_SHAPES[t]
    blk, nblk, inner = _half_tiling(t)
    nbr = rows // blk[0]

    def kern(idx_ref, h_ref, r_ref, o_ref):
        o_ref[...] = ((h_ref[...] + r_ref[0].astype(F32)) + r_ref[1].astype(F32)) + r_ref[2].astype(F32)

    def omap(i, idx_ref):
        r, cidx = inner(i)
        return nbr * idx_ref[1] + r, cidx

    grid_spec = pltpu.PrefetchScalarGridSpec(
        num_scalar_prefetch=1, grid=(nblk,),
        in_specs=[pl.BlockSpec(blk, lambda i, idx_ref: inner(i)),
                  pl.BlockSpec((3,) + blk, lambda i, idx_ref: (0,) + inner(i))],
        out_specs=pl.BlockSpec(blk, omap))
    return pl.pallas_call(
        kern, grid_spec=grid_spec, out_shape=jax.ShapeDtypeStruct((2 * rows, cols), F32),
        name=name, compiler_params=_cp(("parallel",)),
    )(idx, hf, rb)


class _ReduceScatter:
    def __init__(self, ts, grads, idx, tag):
        self.ts, self.idx, self.tag = ts, idx, tag
        arr = {}
        for t in ts:
            arr["g%d" % t] = grads[t]
            arr["ra%d" % t] = lax.empty((4,) + HALF_SHAPES[t], BF16)
        self.plan = _rs_sibling_plan(ts)
        self.arr, self.sems, self.token = _split_call("rs_sibling_start_" + tag, arr, start=self.plan)

    def chips(self, after):
        arr, _, _ = _split_call("rs_sibling_wait_" + self.tag, self.arr, wait=self.plan, wait_sems=self.sems, after=after)
        brr, self.hf = {}, {}
        for t in self.ts:
            hb, self.hf[t] = _chip_sum(arr["g%d" % t], arr["ra%d" % t], t, self.idx, "chip_sum_%d" % t)
            brr["hb%d" % t] = hb
            brr["rb%d" % t] = lax.empty((3,) + HALF_SHAPES[t], BF16)
        self.plan = _rs_chips_plan(self.ts)
        self.arr, self.sems, self.token = _split_call("rs_chips_start_" + self.tag, brr, start=self.plan)
        return self.token

    def share(self, after):
        brr, _, _ = _split_call("rs_chips_wait_" + self.tag, self.arr, wait=self.plan, wait_sems=self.sems, after=after)
        frr = {"f%d" % t: _final_sum(self.hf[t], brr["rb%d" % t], t, self.idx, "final_sum_%d" % t) for t in self.ts}
        self.plan = _rs_share_plan(self.ts)
        self.arr, self.sems, self.token = _split_call("rs_share_start_" + self.tag, frr, start=self.plan)
        return self.token

    def result(self, after):
        frr, _, _ = _split_call("rs_share_wait_" + self.tag, self.arr, wait=self.plan, wait_sems=self.sems, after=after)
        return {t: frr["f%d" % t] for t in self.ts}


def _all8_plan(key):
    def copies(sk, R):
        x, y, c = _coords()
        own = R[key].at[4 * x + 2 * y + c]
        out = []
        for k in range(1, 8):
            dev = ((1 - x) if (k >> 2) & 1 else x, (1 - y) if (k >> 1) & 1 else y, (1 - c) if k & 1 else c)
            out.append(((own, own, dev), R[key].at[4 * dev[0] + 2 * dev[1] + dev[2]]))
        return out
    return _Plan(7, copies)


def _small_all_gather(v):
    def body(v_ref, o_ref, send_sems, recv_sems, loc_sem):
        x, y, c = _coords()
        me = 4 * x + 2 * y + c
        lc = pltpu.make_async_copy(v_ref, o_ref.at[me], loc_sem)
        lc.start()
        cps = []
        for k in range(1, 8):
            fx, fy, fc = (k >> 2) & 1, (k >> 1) & 1, k & 1
            dev = ((1 - x) if fx else x, (1 - y) if fy else y, (1 - c) if fc else c)
            cp = pltpu.make_async_remote_copy(src_ref=v_ref, dst_ref=o_ref.at[me], send_sem=send_sems.at[k - 1],
                                              recv_sem=recv_sems.at[k - 1], device_id=dev, device_id_type=MESH)
            cp.start()
            cps.append((cp, 4 * dev[0] + 2 * dev[1] + dev[2]))
        for k, (cp, frm) in enumerate(cps):
            got = o_ref.at[frm]
            pltpu.make_async_remote_copy(src_ref=got, dst_ref=got, send_sem=send_sems.at[k], recv_sem=recv_sems.at[k],
                                         device_id=(x, y, c), device_id_type=MESH).wait_recv()
        for cp, _ in cps:
            cp.wait_send()
        lc.wait()

    hbm = pl.BlockSpec(memory_space=HBM)
    return pl.pallas_call(
        body, out_shape=jax.ShapeDtypeStruct((8,) + v.shape, F32), in_specs=[hbm], out_specs=hbm,
        scratch_shapes=[pltpu.SemaphoreType.DMA((7,)), pltpu.SemaphoreType.DMA((7,)), pltpu.SemaphoreType.DMA(())],
        name="small_all_gather", compiler_params=pltpu.CompilerParams(has_side_effects=True),
    )(v)


def _sum8(v, name="small_sum"):
    def kern(v_ref, o_ref):
        acc = v_ref[0]
        for k in range(1, 8):
            acc = acc + v_ref[k]
        o_ref[...] = acc

    return pl.pallas_call(kern, out_shape=jax.ShapeDtypeStruct(v.shape[1:], F32), name=name)(v)


def _adamw(w, g, m, v, name, tr=128, blk0=0, nblk=None, into=None, copy_g=False):
    R, C = w.shape
    tr = min(tr, R)
    if nblk is None:
        assert R % tr == 0 and blk0 == 0
        nblk = R // tr
    n_out = 4 if copy_g else 3

    def kern(*refs):
        w_ref, g_ref, m_ref, v_ref = refs[:4]
        d_ref, mo_ref, vo_ref = refs[-n_out:][:3]
        gv = g_ref[...]
        mn = ADAM_B1 * m_ref[...] + (1.0 - ADAM_B1) * gv
        vn = ADAM_B2 * v_ref[...] + (1.0 - ADAM_B2) * (gv * gv)
        m_hat = mn / (1.0 - ADAM_B1 ** ADAM_STEP)
        v_hat = vn / (1.0 - ADAM_B2 ** ADAM_STEP)
        d_ref[...] = -ADAM_LR * (m_hat / (jnp.sqrt(v_hat) + ADAM_EPS) + ADAM_WD * w_ref[...])
        mo_ref[...] = mn
        vo_ref[...] = vn
        if copy_g:
            refs[-1][...] = gv

    blk = pl.BlockSpec((tr, C), lambda i: (blk0 + i, 0))
    sd = jax.ShapeDtypeStruct((R, C), F32)
    in_specs, args, aliases = [blk] * 4, [w, g, m, v], {}
    if into is not None:
        in_specs += [pl.BlockSpec(memory_space=pl.ANY)] * 3
        args += list(into)
        aliases = {4: 0, 5: 1, 6: 2}
    return pl.pallas_call(kern, grid=(nblk,), in_specs=in_specs, out_specs=(blk,) * n_out, out_shape=(sd,) * n_out,
                          input_output_aliases=aliases, name=name, compiler_params=_cp(("parallel",)))(*args)


def _to_piece(wt, s):
    z = lambda n: jnp.zeros((n, D), wt.dtype)
    pads = [functools.partial(lambda k, w: jnp.pad(w, ((8 * k, PIECE - W_SHARD - 8 * k), (0, 0))), k) for k in range(3)]
    last = lambda w: jnp.concatenate([z(24), w[:744], w[776:], w[744:776], z(PIECE - 24 - W_SHARD)], axis=0)
    return lax.switch(s, pads + [last], wt)


def _from_piece(p, s):
    cuts = [functools.partial(lambda k, q: q[8 * k:8 * k + W_SHARD], k) for k in range(3)]
    last = lambda q: jnp.concatenate([q[24:768], q[2816:2848], q[768:2816]], axis=0)
    return lax.switch(s, cuts + [last], p)


_SMALL = [("b_gate", 2048), ("ssm_conv_b", 4096), ("dt_bias", 32), ("A_log", 32), ("D_skip", 32),
          ("ssm_norm_w", 2048), ("norm_mlp", 1024), ("norm_final", 1024), ("sc_conv_w", 3072), ("ssm_conv_w", 16384),
          ("loss", 1)]


def _pack(vals, table, rows):
    parts = []
    for name, n in table:
        v = vals[name].reshape(-1).astype(F32)
        pad = (-n) % 128
        parts.append(jnp.pad(v, (0, pad)) if pad else v)
    flat = jnp.concatenate(parts)
    return jnp.pad(flat, (0, rows * 128 - flat.shape[0])).reshape(rows, 128)


def _unpack(arr, table):
    flat = arr.reshape(-1)
    out, off = {}, 0
    for name, n in table:
        out[name] = flat[off:off + n]
        off += n + ((-n) % 128)
    return out


def kernel(x, norm_mix, w_in, b_gate, sc_conv_w, ssm_conv_w, ssm_conv_b, dt_bias, A_log, D_skip, ssm_norm_w, w_branch_sc, w_branch_ssm, w_out, norm_mlp, w_mlp1, w_mlp2, norm_final, loss_target, m_norm_mix, m_w_in, m_b_gate, m_sc_conv_w, m_ssm_conv_w, m_ssm_conv_b, m_dt_bias, m_A_log, m_D_skip, m_ssm_norm_w, m_w_branch_sc, m_w_branch_ssm, m_w_out, m_norm_mlp, m_w_mlp1, m_w_mlp2, m_norm_final, v_norm_mix, v_w_in, v_b_gate, v_sc_conv_w, v_ssm_conv_w, v_ssm_conv_b, v_dt_bias, v_A_log, v_D_skip, v_ssm_norm_w, v_w_branch_sc, v_w_branch_ssm, v_w_out, v_norm_mlp, v_w_mlp1, v_w_mlp2, v_norm_final):
    L = x.shape[1]
    nc = L // Q
    xi, yi, ci = lax.axis_index("x"), lax.axis_index("y"), lax.axis_index("c")
    s = 2 * xi + yi
    idx = jnp.stack([s, ci]).astype(jnp.int32)
    x0 = x.reshape(L, D)
    tgt = loss_target.reshape(L, D)

    piece = _to_piece(w_in.T, s)
    nb = PMAIN // XTRA
    wct0 = _place(piece, (NCW, D), (XTRA, D), lambda i, r: (nb * r[0] + i, 0), idx, "place_wct", nblk=nb)
    xt0 = _place(piece, (4, XTRA, D), (1, XTRA, D), lambda i, r: (r[0], 0, 0), idx, "place_xt", blk0=nb, nblk=1)
    cws = jnp.zeros((8, 1280), F32)
    cws = cws.at[0:3, 0:256].set(sc_conv_w).at[0:4, 256:1280].set(ssm_conv_w)
    cw0 = lax.dynamic_update_slice(jnp.zeros((4, 8, 1280), F32), cws[None], (s, 0, 0))
    win_keys, mid_keys, end_keys = ["wct", "xt", "cw"], ["wa", "wb", "wo", "w1"], ["w2"]
    gw, sems_w, tok = _split_call("ag_win_start", {"wct": wct0, "xt": xt0, "cw": cw0}, start=_ag_chips_plan(win_keys))
    wa0 = _place(w_branch_sc, (D, D), (256, 1024), lambda i, r: (r[0], 0), idx, "place_wa", dep=tok)
    wb0 = _place(w_branch_ssm, (INNER, D), (512, 1024), lambda i, r: (r[0], 0), idx, "place_wb", dep=tok)
    wo0 = _place(w_out, (D, D), (256, 1024), lambda i, r: (r[0], 0), idx, "place_wo", dep=tok)
    w10 = _place(w_mlp1, (D, DFF), (256, 1024), lambda i, r: (i, r[0]), idx, "place_w1", dep=tok)
    gm, sems_m, tok = _split_call("ag_mid_start", {"wa": wa0, "wb": wb0, "wo": wo0, "w1": w10},
                                  start=_ag_chips_plan(mid_keys))
    w20 = _place(w_mlp2, (DFF, D), (256, 1024), lambda i, r: (4 * r[0] + i, 0), idx, "place_w2", dep=tok)
    ge, sems_e, tok = _split_call("ag_end_start", {"w2": w20}, start=_ag_chips_plan(end_keys))
    h = _rms_fwd(x0, norm_mix, "rms_mix", dep=tok)
    gw, sems_w, tok = _split_call("ag_win_pass", gw, wait=_ag_chips_plan(win_keys), wait_sems=sems_w,
                                  start=_ag_sibling_plan(win_keys), after=h)
    gw, _, _ = _split_call("ag_win_done", gw, wait=_ag_sibling_plan(win_keys), wait_sems=sems_w, after=tok)
    wc, cw_all = _fix_wct(gw["wct"], gw["xt"]), gw["cw"]
    sc_w_full = jnp.concatenate([cw_all[k, :, 0:256] for k in range(4)], axis=1)
    ssm_w_full = jnp.concatenate([cw_all[k, :, 256:1280] for k in range(4)], axis=1)
    cw4 = ssm_w_full.at[4].set(ssm_conv_b)
    vec = jnp.zeros((8, 128), F32).at[0, :NH].set(dt_bias).at[1, :NH].set(A_log)
    vecg = jnp.zeros((NG, 8, 128), F32).at[:, 0, :4].set(A_log.reshape(NG, 4)).at[:, 1, :4].set(D_skip.reshape(NG, 4))

    proj = _matmul(h, wc, "nt", BF16, 512, 1280, 1024, "in_proj", n_outer=True)
    dtraw = _matmul(h, wc[C_DT:], "nt", F32, 512, 256, 1024, "in_proj_dt")
    gm, sems_m, tok = _split_call("ag_mid_pass", gm, wait=_ag_chips_plan(mid_keys), wait_sems=sems_m,
                                  start=_ag_sibling_plan(mid_keys), after=proj)
    proj = _tie(proj, tok, "tie_proj")
    ya = _sc_fwd(proj, sc_w_full)
    xbc = _ssm_conv_fwd(proj, cw4)
    dt4, cs4, sg4 = _dt_prep(dtraw, vec)
    ge, sems_e, tok = _split_call("ag_end_pass", ge, wait=_ag_chips_plan(end_keys), wait_sems=sems_e,
                                  start=_ag_sibling_plan(end_keys), after=xbc)
    xbc = _tie(xbc, tok, "tie_xbc")
    y, s_all = _ssd_fwd(xbc, dt4, cs4, vecg)
    yb = _gnorm_fwd(y, proj, ssm_norm_w)
    gm, _, _ = _split_call("ag_mid_done", gm, wait=_ag_sibling_plan(mid_keys), wait_sems=sems_m, after=yb)
    ge, _, _ = _split_call("ag_end_done", ge, wait=_ag_sibling_plan(end_keys), wait_sems=sems_e, after=yb)
    wa, wb, wo, w1, w2 = gm["wa"], gm["wb"], gm["wo"], gm["w1"], ge["w2"]
    br_a = _matmul(ya, wa, "nn", F32, 512, 1024, 1024, "branch_sc")
    br_b = _matmul(yb, wb, "nn", F32, 512, 1024, 2048, "branch_ssm")
    merged = _merge_fwd(proj, b_gate, br_a, br_b)
    x1 = _matmul(merged, wo, "nn", F32, 512, 1024, 1024, "out_proj", epi="res", extra=x0)
    h2 = _rms_fwd(x1, norm_mlp, "rms_mlp")
    a1, rl = _matmul(h2, w1, "nn", BF16, 512, 1024, 1024, "mlp1", epi="relu2", n_outer=True)
    x2 = _matmul(rl, w2, "nn", F32, 512, 1024, 2048, "mlp2", epi="res", extra=x1)
    dx2, g_nf, loss8 = _final(x2, norm_final, tgt)

    da = _matmul(dx2, w2, "nt", BF16, 512, 1024, 1024, "mlp2_dx", epi="drelu", extra=a1, n_outer=True)
    g_w2 = _matmul(rl, dx2, "tn", BF16, 1024, 1024, 512, "mlp2_dw")
    g_w1 = _matmul(h2, da, "tn", BF16, 1024, 1024, 512, "mlp1_dw")
    dh2 = _matmul(da, w1, "nt", F32, 512, 1024, 2048, "mlp1_dx")
    dx1, g_nmlp = _rms_bwd(dh2, x1, norm_mlp, dx2, "rms_mlp_bwd")
    dmerged = _matmul(dx1, wo, "nt", F32, 512, 1024, 1024, "out_proj_dx")
    g_wo = _matmul(merged, dx1, "tn", BF16, 1024, 1024, 512, "out_proj_dw")
    dproj = lax.empty((L, NCW), BF16)
    dbr, dproj, g_bg = _merge_bwd(dmerged, proj, b_gate, br_a, br_b, dproj)
    dya = _matmul(dbr[0], wa, "nt", F32, 512, 1024, 1024, "branch_sc_dx")
    g_wa = _matmul(ya, dbr[0], "tn", BF16, 1024, 1024, 512, "branch_sc_dw")
    dproj, g_scw = _sc_bwd(dya, proj, sc_w_full, dproj)
    dyb = _matmul(dbr[1], wb, "nt", F32, 512, 1024, 1024, "branch_ssm_dx", n_outer=True)
    g_wb = _matmul(yb, dbr[1], "tn", BF16, 1024, 1024, 512, "branch_ssm_dw")
    rs_a = _ReduceScatter([1, 2, 3, 4, 5], {1: g_w1, 2: g_w2, 3: g_wa, 4: g_wb, 5: g_wo}, idx, "a")
    dy, dproj, g_snw = _gnorm_bwd(_tie(dyb, rs_a.token, "tie_dyb"), y, proj, ssm_norm_w, dproj)
    tok = rs_a.chips(after=dy)
    dxs, dbm, dcm, ddt_g, st = _ssd_bwd(xbc, dt4, cs4, sg4, vecg, s_all, _tie(dy, tok, "tie_dy"))
    dproj, gx1 = _ssm_conv_bwd(dxs, proj, cw4, dproj, 0, "ssm_conv_bwd_x")
    dproj, gx2 = _ssm_conv_bwd(dbm, proj, cw4, dproj, INNER, "ssm_conv_bwd_b")
    dproj, gx3 = _ssm_conv_bwd(dcm, proj, cw4, dproj, INNER + NG * NS, "ssm_conv_bwd_c")
    g_cw4 = jnp.concatenate([gx1, gx2, gx3], axis=1)
    dproj, g_dtb = _dt_bwd(ddt_g, dproj)
    small = {"b_gate": g_bg[0], "ssm_conv_b": g_cw4[4], "dt_bias": g_dtb[0, :NH],
             "A_log": st[:, 0, :4], "D_skip": st[:, 1, :4], "ssm_norm_w": g_snw[0], "norm_mlp": g_nmlp[0],
             "norm_final": g_nf[0], "sc_conv_w": g_scw[0:3], "ssm_conv_w": g_cw4[0:4], "loss": loss8[0, 0:1]}
    small_sum = _sum8(_small_all_gather(_pack(small, _SMALL, SMALL_ROWS)))
    gs = _unpack(small_sum, _SMALL)
    g_wc = _matmul(dproj, h, "tn", BF16, 1280, 1024, 512, "in_proj_dw", dep=small_sum)
    rs_b = _ReduceScatter([0], {0: g_wc}, idx, "b")
    tok = rs_a.share(after=rs_b.token)
    tok = rs_b.chips(after=tok)
    dh = _matmul(dproj, wc, "nn", F32, 512, 1024, 2304, "in_proj_dx", dep=tok)
    grad_x, g_nm = _rms_bwd(dh, x0, norm_mix, dx1, "rms_mix_bwd")
    me = 4 * xi + 2 * yi + ci
    nm8 = lax.dynamic_update_slice(jnp.zeros((8, 8, 128), F32), g_nm[0].reshape(1, 8, 128), (me, 0, 0))
    nm_arr, nm_sems, tok = _split_call("norm_mix_start", {"nm": nm8}, start=_all8_plan("nm"))
    red = rs_a.result(after=tok)
    big = {"w_mlp1": red[1], "w_mlp2": red[2], "w_branch_sc": red[3], "w_branch_ssm": red[4], "w_out": red[5]}

    given = dict(norm_mix=norm_mix, w_in=w_in, b_gate=b_gate, sc_conv_w=sc_conv_w, ssm_conv_w=ssm_conv_w, ssm_conv_b=ssm_conv_b, dt_bias=dt_bias, A_log=A_log, D_skip=D_skip, ssm_norm_w=ssm_norm_w, w_branch_sc=w_branch_sc, w_branch_ssm=w_branch_ssm, w_out=w_out, norm_mlp=norm_mlp, w_mlp1=w_mlp1, w_mlp2=w_mlp2, norm_final=norm_final,
                 m_norm_mix=m_norm_mix, m_w_in=m_w_in, m_b_gate=m_b_gate, m_sc_conv_w=m_sc_conv_w, m_ssm_conv_w=m_ssm_conv_w, m_ssm_conv_b=m_ssm_conv_b, m_dt_bias=m_dt_bias, m_A_log=m_A_log, m_D_skip=m_D_skip, m_ssm_norm_w=m_ssm_norm_w, m_w_branch_sc=m_w_branch_sc, m_w_branch_ssm=m_w_branch_ssm, m_w_out=m_w_out, m_norm_mlp=m_norm_mlp, m_w_mlp1=m_w_mlp1, m_w_mlp2=m_w_mlp2, m_norm_final=m_norm_final,
                 v_norm_mix=v_norm_mix, v_w_in=v_w_in, v_b_gate=v_b_gate, v_sc_conv_w=v_sc_conv_w, v_ssm_conv_w=v_ssm_conv_w, v_ssm_conv_b=v_ssm_conv_b, v_dt_bias=v_dt_bias, v_A_log=v_A_log, v_D_skip=v_D_skip, v_ssm_norm_w=v_ssm_norm_w, v_w_branch_sc=v_w_branch_sc, v_w_branch_ssm=v_w_branch_ssm, v_w_out=v_w_out, v_norm_mlp=v_norm_mlp, v_w_mlp1=v_w_mlp1, v_w_mlp2=v_w_mlp2, v_norm_final=v_norm_final)
    order = ["norm_mix", "w_in", "b_gate", "sc_conv_w", "ssm_conv_w", "ssm_conv_b", "dt_bias", "A_log", "D_skip",
             "ssm_norm_w", "w_branch_sc", "w_branch_ssm", "w_out", "norm_mlp", "w_mlp1", "w_mlp2", "norm_final"]
    grad, delta, new_m, new_v = {}, {}, {}, {}
    for n in big:
        delta[n], new_m[n], new_v[n], grad[n] = _adamw(given[n], big[n], given["m_" + n], given["v_" + n],
                                                       "adamw_" + n, copy_g=True)
    big["w_in"] = None
    grad_small = {n: gs[n].reshape(given[n].shape) for n in order
                  if n not in big and n not in ("sc_conv_w", "ssm_conv_w", "norm_mix")}
    grad_small["sc_conv_w"] = lax.dynamic_slice(gs["sc_conv_w"].reshape(3, D), (0, 256 * s), (3, 256))
    grad_small["ssm_conv_w"] = lax.dynamic_slice(gs["ssm_conv_w"].reshape(4, XBC), (0, 1024 * s), (4, 1024))
    table = [(n, int(grad_small[n].size)) for n in grad_small]
    rows = 136
    pk = lambda d: _pack(d, table, rows)
    ds_, ms_, vs_ = _adamw(pk({n: given[n] for n in grad_small}), pk(grad_small), pk({n: given["m_" + n] for n in grad_small}),
                           pk({n: given["v_" + n] for n in grad_small}), "adamw_small", tr=rows)
    ds_, ms_, vs_ = _unpack(ds_, table), _unpack(ms_, table), _unpack(vs_, table)
    for n in grad_small:
        shp = given[n].shape
        grad[n] = grad_small[n]
        delta[n], new_m[n], new_v[n] = ds_[n].reshape(shp), ms_[n].reshape(shp), vs_[n].reshape(shp)

    done = [new_v[n] for n in ("w_mlp1", "w_mlp2", "w_branch_sc", "w_branch_ssm", "w_out")] + [vs_["b_gate"]]
    tok = rs_b.share(after=done)
    gp = rs_b.result(after=tok)[0]
    gwt = _from_piece(gp, s)
    wt_args = (w_in.T, gwt, m_w_in.T, v_w_in.T)
    head = _adamw(*wt_args, "adamw_w_in", tr=256, nblk=W_SHARD // 256)
    dt_, mt_, vt_ = _adamw(*wt_args, "adamw_w_in_tail", tr=8, blk0=(W_SHARD // 256) * 32, nblk=1, into=head)
    grad["w_in"], delta["w_in"], new_m["w_in"], new_v["w_in"] = gwt.T, dt_.T, mt_.T, vt_.T
    nm_arr, _, _ = _split_call("norm_mix_wait", nm_arr, wait=_all8_plan("nm"), wait_sems=nm_sems, after=tok)
    g8 = _sum8(nm_arr["nm"], "norm_mix_sum")
    r8 = lambda a: a.reshape(8, 128)
    d8, m8, v8 = _adamw(r8(norm_mix), g8, r8(m_norm_mix), r8(v_norm_mix), "adamw_norm_mix", tr=8)
    grad["norm_mix"], delta["norm_mix"] = g8.reshape(D), d8.reshape(D)
    new_m["norm_mix"], new_v["norm_mix"] = m8.reshape(D), v8.reshape(D)

    loss = gs["loss"].reshape(())
    return (loss, grad_x.reshape(1, L, D), *[grad[n] for n in order], *[delta[n] for n in order],
            *[new_m[n] for n in order], *[new_v[n] for n in order])
```

```python
import functools

import jax
import jax.numpy as jnp
from jax import lax
from jax.experimental import pallas as pl
from jax.experimental.pallas import tpu as pltpu

F32 = jnp.float32
BF16 = jnp.bfloat16
MESH = pl.DeviceIdType.MESH
HBM = pltpu.HBM

D = 1024
INNER = 2048
HD = 64
NH = 32
NG = 8
NS = 128
Q = 128
GPS = 2
XBC = 4096
DFF = 4096
EPS = 1e-6
W_SHARD = 2824
NCW = 11520
PIECE = 3072
PMAIN = 2816
C_Z, C_XBC, C_GATE, C_DT = 3072, 5120, 9216, 11264
SMALL_ROWS = 256
VMEM_LIMIT = 56 * 1024 * 1024

ADAM_LR, ADAM_B1, ADAM_B2, ADAM_EPS, ADAM_WD, ADAM_STEP = 0.001, 0.9, 0.999, 1e-08, 0.01, 10


def _cp(sem=None, vmem=VMEM_LIMIT):
    return pltpu.CompilerParams(dimension_semantics=sem, vmem_limit_bytes=vmem)


def _sigmoid(v):
    return 1.0 / (1.0 + jnp.exp(-v))


_DIMS = {"nn": (((1,), (0,)), ((), ())), "nt": (((1,), (1,)), ((), ())), "tn": (((0,), (0,)), ((), ()))}


def _matmul(a, b, mode, out_dtype, tm, tn, tk, name, epi=None, extra=None, n_outer=False, dep=None):
    if mode == "tn":
        K, M = a.shape
    else:
        M, K = a.shape
    N = b.shape[0] if mode == "nt" else b.shape[1]
    tm, tn, tk = min(tm, M), min(tn, N), min(tk, K)
    assert M % tm == 0 and N % tn == 0 and K % tk == 0, (name, M, N, K, tm, tn, tk)
    nm, nn, nk = M // tm, N // tn, K // tk
    dims = _DIMS[mode]

    def ij(p0, p1):
        return (p1, p0) if n_outer else (p0, p1)

    if mode == "tn":
        a_spec = pl.BlockSpec((tk, tm), lambda p0, p1, k: (k, ij(p0, p1)[0]))
    else:
        a_spec = pl.BlockSpec((tm, tk), lambda p0, p1, k: (ij(p0, p1)[0], k))
    if mode == "nt":
        b_spec = pl.BlockSpec((tn, tk), lambda p0, p1, k: (ij(p0, p1)[1], k))
    else:
        b_spec = pl.BlockSpec((tk, tn), lambda p0, p1, k: (k, ij(p0, p1)[1]))
    o_spec = pl.BlockSpec((tm, tn), lambda p0, p1, k: ij(p0, p1))
    in_specs = [a_spec, b_spec]
    args = [a, b]
    if epi in ("res", "drelu"):
        in_specs.append(o_spec)
        args.append(extra)
    if dep is not None:
        in_specs.append(pl.BlockSpec(memory_space=pl.ANY))
        args.append(dep)
    n_in = len(args)
    if epi == "relu2":
        out_shape = (jax.ShapeDtypeStruct((M, N), out_dtype), jax.ShapeDtypeStruct((M, N), BF16))
        out_specs = (o_spec, o_spec)
    else:
        out_shape = jax.ShapeDtypeStruct((M, N), out_dtype)
        out_specs = o_spec

    def kern(*refs):
        a_ref, b_ref = refs[0], refs[1]
        e_ref = refs[2] if epi in ("res", "drelu") else None
        acc = refs[-1]
        outs = refs[n_in:-1] if nk > 1 else refs[n_in:]
        k = pl.program_id(2)

        def product():
            return lax.dot_general(a_ref[...].astype(BF16), b_ref[...].astype(BF16), dims, preferred_element_type=F32)

        def finish(r):
            if epi is None:
                outs[0][...] = r.astype(out_dtype)
            elif epi == "res":
                outs[0][...] = (r + e_ref[...]).astype(out_dtype)
            elif epi == "relu2":
                outs[0][...] = r.astype(out_dtype)
                t = jnp.maximum(r, 0.0)
                outs[1][...] = (t * t).astype(BF16)
            else:
                outs[0][...] = (r * (2.0 * jnp.maximum(e_ref[...].astype(F32), 0.0))).astype(out_dtype)

        if nk == 1:
            finish(product())
        else:
            @pl.when(k == 0)
            def _():
                acc[...] = jnp.zeros_like(acc)

            acc[...] += product()

            @pl.when(k == nk - 1)
            def _():
                finish(acc[...])

    grid = (nn, nm, nk) if n_outer else (nm, nn, nk)
    return pl.pallas_call(
        kern, grid=grid, in_specs=in_specs, out_specs=out_specs, out_shape=out_shape,
        scratch_shapes=[pltpu.VMEM((tm, tn), F32)] if nk > 1 else [], name=name,
        compiler_params=_cp(("parallel", "parallel", "arbitrary")),
    )(*args)


def _rms_fwd(x, w, name, tl=256, dep=None):
    L = x.shape[0]

    def kern(x_ref, w_ref, *rest):
        o_ref = rest[-1]
        xv = x_ref[...]
        r = lax.rsqrt(jnp.mean(xv * xv, axis=-1, keepdims=True) + EPS)
        o_ref[...] = ((xv * r) * w_ref[...]).astype(BF16)

    row = pl.BlockSpec((tl, D), lambda i: (i, 0))
    deps = [] if dep is None else [dep]
    return pl.pallas_call(
        kern, grid=(L // tl,),
        in_specs=[row, pl.BlockSpec((1, D), lambda i: (0, 0))] + [pl.BlockSpec(memory_space=pl.ANY)] * len(deps),
        out_specs=row, out_shape=jax.ShapeDtypeStruct((L, D), BF16), name=name, compiler_params=_cp(("parallel",)),
    )(x, w.reshape(1, D), *deps)


def _rms_bwd(dy, x, w, res, name, tl=256, dep=None):
    L = x.shape[0]
    deps = [] if dep is None else [dep]

    def kern(dy_ref, x_ref, w_ref, res_ref, *rest):
        dx_ref, gw_ref = rest[-2:]
        @pl.when(pl.program_id(0) == 0)
        def _():
            gw_ref[...] = jnp.zeros_like(gw_ref)

        xv = x_ref[...]
        dyv = dy_ref[...]
        r = lax.rsqrt(jnp.mean(xv * xv, axis=-1, keepdims=True) + EPS)
        xn = xv * r
        gw_ref[...] += jnp.broadcast_to(jnp.sum(dyv * xn, axis=0, keepdims=True), (8, D))
        dxn = dyv * w_ref[...]
        dx_ref[...] = res_ref[...] + r * (dxn - xn * jnp.mean(dxn * xn, axis=-1, keepdims=True))

    row = pl.BlockSpec((tl, D), lambda i: (i, 0))
    return pl.pallas_call(
        kern, grid=(L // tl,),
        in_specs=[row, row, pl.BlockSpec((1, D), lambda i: (0, 0)), row] + [pl.BlockSpec(memory_space=pl.ANY)] * len(deps),
        out_specs=(row, pl.BlockSpec((8, D), lambda i: (0, 0))),
        out_shape=(jax.ShapeDtypeStruct((L, D), F32), jax.ShapeDtypeStruct((8, D), F32)),
        name=name, compiler_params=_cp(("arbitrary",)),
    )(dy, x, w.reshape(1, D), res, *deps)


def _final(x2, w, tgt, tl=256):
    L = x2.shape[0]

    def kern(x_ref, w_ref, t_ref, dx_ref, gw_ref, loss_ref):
        @pl.when(pl.program_id(0) == 0)
        def _():
            gw_ref[...] = jnp.zeros_like(gw_ref)
            loss_ref[...] = jnp.zeros_like(loss_ref)

        xv = x_ref[...]
        r = lax.rsqrt(jnp.mean(xv * xv, axis=-1, keepdims=True) + EPS)
        xn = xv * r
        e = xn * w_ref[...] - t_ref[...]
        per_tok = jnp.mean(e * e, axis=-1, keepdims=True)
        loss_ref[...] += 0.5 * jnp.sum(per_tok)
        dyv = e * (1.0 / D)
        gw_ref[...] += jnp.broadcast_to(jnp.sum(dyv * xn, axis=0, keepdims=True), (8, D))
        dxn = dyv * w_ref[...]
        dx_ref[...] = r * (dxn - xn * jnp.mean(dxn * xn, axis=-1, keepdims=True))

    row = pl.BlockSpec((tl, D), lambda i: (i, 0))
    return pl.pallas_call(
        kern, grid=(L // tl,), in_specs=[row, pl.BlockSpec((1, D), lambda i: (0, 0)), row],
        out_specs=(row, pl.BlockSpec((8, D), lambda i: (0, 0)), pl.BlockSpec((8, 128), lambda i: (0, 0))),
        out_shape=(jax.ShapeDtypeStruct((L, D), F32), jax.ShapeDtypeStruct((8, D), F32),
                   jax.ShapeDtypeStruct((8, 128), F32)),
        name="final_norm_loss", compiler_params=_cp(("arbitrary",)),
    )(x2, w.reshape(1, D), tgt)


def _down(v, k):
    if k == 0:
        return v
    t = lax.broadcasted_iota(jnp.int32, v.shape, 0)
    return jnp.where(t >= k, pltpu.roll(v, k, axis=0), 0.0)


def _up(v, k):
    if k == 0:
        return v
    n = v.shape[0]
    t = lax.broadcasted_iota(jnp.int32, v.shape, 0)
    return jnp.where(t < n - k, pltpu.roll(v, n - k, axis=0), 0.0)


TW = 256


def _sc_fwd(proj, cw):
    L = proj.shape[0]
    nb = D // TW

    def kern(b_ref, c_ref, x_ref, w_ref, o_ref):
        u = c_ref[...].astype(F32) * x_ref[...].astype(F32)
        w = w_ref[...]
        cv = w[0:1] * _down(u, 2) + w[1:2] * _down(u, 1) + w[2:3] * u
        o_ref[...] = (b_ref[...].astype(F32) * cv).astype(BF16)

    col = lambda off: pl.BlockSpec((L, TW), lambda j: (0, off + j))
    return pl.pallas_call(
        kern, grid=(nb,), in_specs=[col(0), col(nb), col(2 * nb), pl.BlockSpec((8, TW), lambda j: (0, j))],
        out_specs=pl.BlockSpec((L, TW), lambda j: (0, j)), out_shape=jax.ShapeDtypeStruct((L, D), BF16),
        name="sc_fwd", compiler_params=_cp(("parallel",)),
    )(proj, proj, proj, cw)


def _sc_bwd(dya, proj, cw, dproj):
    L = proj.shape[0]
    nb = D // TW

    def kern(d_ref, b_ref, c_ref, x_ref, w_ref, _, dp_ref, gw_ref, keep):
        sec = pl.program_id(1)

        @pl.when(sec == 0)
        def _():
            cs, xs, dyv = c_ref[...].astype(F32), x_ref[...].astype(F32), d_ref[...]
            w = w_ref[...]
            u = cs * xs
            u1, u2 = _down(u, 1), _down(u, 2)
            cv = w[0:1] * u2 + w[1:2] * u1 + w[2:3] * u
            dcv = dyv * b_ref[...].astype(F32)
            du = w[2:3] * dcv + w[1:2] * _up(dcv, 1) + w[0:1] * _up(dcv, 2)
            g0 = jnp.sum(dcv * u2, axis=0, keepdims=True)
            g1 = jnp.sum(dcv * u1, axis=0, keepdims=True)
            g2 = jnp.sum(dcv * u, axis=0, keepdims=True)
            row = lax.broadcasted_iota(jnp.int32, (8, TW), 0)
            gw_ref[...] = jnp.where(row == 0, g0, jnp.where(row == 1, g1, jnp.where(row == 2, g2, 0.0)))
            dp_ref[...] = (dyv * cv).astype(BF16)
            keep[0] = (du * xs).astype(BF16)
            keep[1] = (du * cs).astype(BF16)

        @pl.when(sec > 0)
        def _():
            dp_ref[...] = keep[sec - 1]

    col = lambda off: pl.BlockSpec((L, TW), lambda j, s: (0, off + j))
    return pl.pallas_call(
        kern, grid=(nb, 3),
        in_specs=[col(0), col(0), col(nb), col(2 * nb), pl.BlockSpec((8, TW), lambda j, s: (0, j)),
                  pl.BlockSpec(memory_space=pl.ANY)],
        out_specs=(pl.BlockSpec((L, TW), lambda j, s: (0, s * nb + j)), pl.BlockSpec((8, TW), lambda j, s: (0, j))),
        out_shape=(jax.ShapeDtypeStruct(dproj.shape, BF16), jax.ShapeDtypeStruct((8, D), F32)),
        scratch_shapes=[pltpu.VMEM((2, L, TW), BF16)],
        input_output_aliases={5: 0}, name="sc_bwd", compiler_params=_cp(("parallel", "arbitrary")),
    )(dya, proj, proj, proj, cw, dproj)


def _ssm_conv_fwd(proj, cw4):
    L = proj.shape[0]
    off = C_XBC // TW

    def kern(r_ref, w_ref, o_ref):
        raw = r_ref[...].astype(F32)
        w = w_ref[...]
        c4 = w[0:1] * _down(raw, 3) + w[1:2] * _down(raw, 2) + w[2:3] * _down(raw, 1) + w[3:4] * raw + w[4:5]
        o_ref[...] = c4 * _sigmoid(c4)

    return pl.pallas_call(
        kern, grid=(XBC // TW,),
        in_specs=[pl.BlockSpec((L, TW), lambda j: (0, off + j)), pl.BlockSpec((8, TW), lambda j: (0, j))],
        out_specs=pl.BlockSpec((L, TW), lambda j: (0, j)), out_shape=jax.ShapeDtypeStruct((L, XBC), F32),
        name="ssm_conv_fwd", compiler_params=_cp(("parallel",)),
    )(proj, cw4)


def _ssm_conv_bwd(dx, proj, cw4, dproj, col0, name):
    L, width = dx.shape
    off_p = (C_XBC + col0) // TW
    off_w = col0 // TW

    def kern(d_ref, r_ref, w_ref, _, dp_ref, gw_ref):
        raw = r_ref[...].astype(F32)
        w = w_ref[...]
        r1, r2, r3 = _down(raw, 1), _down(raw, 2), _down(raw, 3)
        c4 = w[0:1] * r3 + w[1:2] * r2 + w[2:3] * r1 + w[3:4] * raw + w[4:5]
        sg = _sigmoid(c4)
        dc4 = d_ref[...] * (sg * (1.0 + c4 * (1.0 - sg)))
        draw = w[3:4] * dc4 + w[2:3] * _up(dc4, 1) + w[1:2] * _up(dc4, 2) + w[0:1] * _up(dc4, 3)
        dp_ref[...] = draw.astype(BF16)
        gs = [jnp.sum(dc4 * r3, axis=0, keepdims=True), jnp.sum(dc4 * r2, axis=0, keepdims=True),
              jnp.sum(dc4 * r1, axis=0, keepdims=True), jnp.sum(dc4 * raw, axis=0, keepdims=True),
              jnp.sum(dc4, axis=0, keepdims=True)]
        row = lax.broadcasted_iota(jnp.int32, (8, TW), 0)
        acc = jnp.zeros((8, TW), F32)
        for k, gk in enumerate(gs):
            acc = jnp.where(row == k, gk, acc)
        gw_ref[...] = acc

    return pl.pallas_call(
        kern, grid=(width // TW,),
        in_specs=[pl.BlockSpec((L, TW), lambda j: (0, j)), pl.BlockSpec((L, TW), lambda j: (0, off_p + j)),
                  pl.BlockSpec((8, TW), lambda j: (0, off_w + j)), pl.BlockSpec(memory_space=pl.ANY)],
        out_specs=(pl.BlockSpec((L, TW), lambda j: (0, off_p + j)), pl.BlockSpec((8, TW), lambda j: (0, j))),
        out_shape=(jax.ShapeDtypeStruct(dproj.shape, BF16), jax.ShapeDtypeStruct((8, width), F32)),
        input_output_aliases={3: 0}, name=name, compiler_params=_cp(("arbitrary",)),
    )(dx, proj, cw4, dproj)


def _split3(v):
    h1 = v.astype(BF16)
    r1 = v - h1.astype(F32)
    h2 = r1.astype(BF16)
    h3 = (r1 - h2.astype(F32)).astype(BF16)
    return h1, h2, h3


def _dot01(m01, v, dims=_DIMS["nn"], m_left=True, terms=3):
    out = None
    for part in _split3(v)[:terms]:
        ops = (m01, part) if m_left else (part, m01)
        t = lax.dot_general(ops[0], ops[1], dims, preferred_element_type=F32)
        out = t if out is None else out + t
    return out


def _bdot(a, b, mode="nn"):
    return lax.dot_general(a.astype(BF16), b.astype(BF16), _DIMS[mode], preferred_element_type=F32)


def _softplus(v):
    return jnp.maximum(v, 0.0) + jnp.log1p(jnp.exp(-jnp.abs(v)))


def _dt_prep(proj, vec):
    L = proj.shape[0]

    def kern(p_ref, v_ref, dt_ref, cs_ref, sg_ref):
        v = v_ref[...]
        pre = p_ref[:, 0:128] + v[0:1]
        dt = _softplus(pre)
        da = dt * (-jnp.exp(v[1:2]))
        ii = lax.broadcasted_iota(jnp.int32, (Q, Q), 0)
        jj = lax.broadcasted_iota(jnp.int32, (Q, Q), 1)
        ltri = (jj <= ii).astype(BF16)
        lane = lax.broadcasted_iota(jnp.int32, (Q, 128), 1)
        for val, ref in ((dt, dt_ref), (_dot01(ltri, da), cs_ref), (_sigmoid(pre), sg_ref)):
            for g in range(NG):
                moved = val if g == 0 else pltpu.roll(val, 128 - 4 * g, axis=1)
                ref[g] = jnp.where(lane < 4, moved, 0.0)

    blk = pl.BlockSpec((NG, Q, 128), lambda c: (0, c, 0))
    return pl.pallas_call(
        kern, grid=(L // Q,),
        in_specs=[pl.BlockSpec((Q, 256), lambda c: (c, 0)), pl.BlockSpec((8, 128), lambda c: (0, 0))],
        out_specs=(blk, blk, blk),
        out_shape=(jax.ShapeDtypeStruct((NG, L, 128), F32),) * 3,
        name="dt_prep", compiler_params=_cp(("parallel",)),
    )(proj, vec)


def _head_masks():
    lane = lax.broadcasted_iota(jnp.int32, (1, 4 * HD), 1)
    return [((lane >= HD * j) & (lane < HD * (j + 1))) for j in range(4)]


def _expand4(v4, masks):
    R = v4.shape[0]
    out = jnp.zeros((R, 4 * HD), F32)
    for j in range(4):
        out = jnp.where(masks[j], jnp.broadcast_to(v4[:, j:j + 1], (R, 4 * HD)), out)
    return out


def _decay_matrix(cs_col, tri):
    colb = jnp.broadcast_to(cs_col, (Q, Q))
    return jnp.exp(jnp.where(tri, colb - colb.T, -jnp.inf))


def _ssd_fwd(xbc, dt4, cs4, vecg):
    L = xbc.shape[0]
    nc = L // Q

    def kern(x_ref, b_ref, c_ref, dt_ref, cs_ref, v_ref, y_ref, s_ref, S):
        c = pl.program_id(1)

        @pl.when(c == 0)
        def _():
            S[...] = jnp.zeros_like(S)

        masks = _head_masks()
        ii = lax.broadcasted_iota(jnp.int32, (Q, Q), 0)
        jj = lax.broadcasted_iota(jnp.int32, (Q, Q), 1)
        tri = jj <= ii
        for gi in range(GPS):
            xs, ns = slice(256 * gi, 256 * (gi + 1)), slice(NS * gi, NS * (gi + 1))
            dt4v, cs4v = dt_ref[gi], cs_ref[gi]
            dt_b, cs_b = _expand4(dt4v, masks), _expand4(cs4v, masks)
            d_b = _expand4(v_ref[gi], masks)[1:2]
            cs_last = cs_b[Q - 1:Q, :]
            x4, bm, cm = x_ref[:, xs], b_ref[:, ns], c_ref[:, ns]
            xdt = x4 * dt_b
            gm = _bdot(cm, bm, "nt")
            s4 = S[gi]
            s_ref[gi, 0] = s4
            y = _bdot(cm, s4) * jnp.exp(cs_b) + d_b * x4
            m_all = jnp.concatenate([(gm * _decay_matrix(cs4v[:, j:j + 1], tri)).astype(BF16) for j in range(4)], axis=0)
            yd = _bdot(m_all, xdt)
            for j in range(4):
                y = y + jnp.where(masks[j], yd[Q * j:Q * (j + 1)], 0.0)
            y_ref[:, xs] = y
            S[gi] = jnp.exp(cs_last) * s4 + _bdot(bm, xdt * jnp.exp(cs_last - cs_b), "tn")

    sc = pl.BlockSpec((GPS, Q, 128), lambda g, c: (g, c, 0))
    bw = NS * GPS
    return pl.pallas_call(
        kern, grid=(NG // GPS, nc),
        in_specs=[pl.BlockSpec((Q, 256 * GPS), lambda g, c: (c, g)),
                  pl.BlockSpec((Q, bw), lambda g, c: (c, INNER // bw + g)),
                  pl.BlockSpec((Q, bw), lambda g, c: (c, (INNER + NG * NS) // bw + g)),
                  sc, sc, pl.BlockSpec((GPS, 8, 128), lambda g, c: (g, 0, 0))],
        out_specs=(pl.BlockSpec((Q, 256 * GPS), lambda g, c: (c, g)),
                   pl.BlockSpec((GPS, 1, NS, 256), lambda g, c: (g, c, 0, 0))),
        out_shape=(jax.ShapeDtypeStruct((L, INNER), F32), jax.ShapeDtypeStruct((NG, nc, NS, 256), F32)),
        scratch_shapes=[pltpu.VMEM((GPS, NS, 256), F32)], name="ssd_fwd",
        compiler_params=_cp(("parallel", "arbitrary")),
    )(xbc, xbc, xbc, dt4, cs4, vecg)


def _ssd_bwd(xbc, dt4, cs4, sg4, vecg, s_all, dy):
    L = xbc.shape[0]
    nc = L // Q

    def kern(x_ref, b_ref, c_ref, dt_ref, cs_ref, sg_ref, v_ref, s_ref, dy_ref,
             dx_ref, db_ref, dc_ref, ddt_ref, st_ref, dS):
        cc = pl.program_id(1)

        @pl.when(cc == 0)
        def _():
            dS[...] = jnp.zeros_like(dS)
            st_ref[...] = jnp.zeros_like(st_ref)

        masks = _head_masks()
        ii = lax.broadcasted_iota(jnp.int32, (Q, Q), 0)
        jj = lax.broadcasted_iota(jnp.int32, (Q, Q), 1)
        tri = jj <= ii
        utri = (jj >= ii).astype(BF16)
        li = lax.broadcasted_iota(jnp.int32, (4 * HD, 4 * HD), 0)
        lj = lax.broadcasted_iota(jnp.int32, (4 * HD, 4 * HD), 1)
        eblk = ((li // HD) == (lj // HD)).astype(BF16)
        lane128 = lax.broadcasted_iota(jnp.int32, (Q, 128), 1)

        for gi in range(GPS):
            xs, ns = slice(256 * gi, 256 * (gi + 1)), slice(NS * gi, NS * (gi + 1))
            dt4v, cs4v, sg4v = dt_ref[gi], cs_ref[gi], sg_ref[gi]
            dt_b, cs_b = _expand4(dt4v, masks), _expand4(cs4v, masks)
            vv = _expand4(v_ref[gi], masks)
            a_b = -jnp.exp(vv[0:1])
            d_b = vv[1:2]
            a4 = -jnp.exp(v_ref[gi][0:1, :])
            cs_last = cs_b[Q - 1:Q, :]
            ecs = jnp.exp(cs_b)
            decay = jnp.exp(cs_last - cs_b)
            elast = jnp.exp(cs_last)
            x4, bm, cm, dyv = x_ref[:, xs], b_ref[:, ns], c_ref[:, ns], dy_ref[:, xs]
            s4 = s_ref[gi, 0]
            dsn = dS[gi]
            xdt = x4 * dt_b
            gm = _bdot(cm, bm, "nt")
            gmt = gm.T
            dye = dyv * ecs
            yoff = ecs * _bdot(cm, s4)
            t4 = _bdot(bm, dsn) * decay
            lms, mhs, mhts = [], [], []
            for j in range(4):
                colb = jnp.broadcast_to(cs4v[:, j:j + 1], (Q, Q))
                seg = colb - colb.T
                lms.append(jnp.exp(jnp.where(tri, seg, -jnp.inf)))
                mhs.append(gm * lms[j])
                mhts.append(gmt * jnp.exp(jnp.where(jj >= ii, -seg, -jnp.inf)))
            m_all = jnp.concatenate([m.astype(BF16) for m in mhs], axis=0)
            dy_m = jnp.concatenate([jnp.where(masks[j], dyv, 0.0).astype(BF16) for j in range(4)], axis=0)
            x_m = jnp.concatenate([jnp.where(masks[j], xdt, 0.0).astype(BF16) for j in range(4)], axis=0)
            dxdt = t4 + _bdot(m_all, dy_m, "tn")
            dm_all = _bdot(dy_m, xdt, "nt")
            dmt_all = _bdot(x_m, dyv, "nt")
            dg = jnp.zeros((Q, Q), F32)
            rc = jnp.zeros((Q, 4 * HD), F32)
            for j in range(4):
                dmh = dm_all[Q * j:Q * (j + 1)]
                dg = dg + dmh * lms[j]
                rs = (jnp.sum(dmh * mhs[j], axis=1, keepdims=True)
                      - jnp.sum(dmt_all[Q * j:Q * (j + 1)] * mhts[j], axis=1, keepdims=True))
                rc = jnp.where(masks[j], jnp.broadcast_to(rs, (Q, 4 * HD)), rc)
            xt = xdt * t4
            tail = jnp.sum(xt, axis=0, keepdims=True) + elast * jnp.sum(s4 * dsn, axis=0, keepdims=True)
            gd_raw = jnp.sum(dyv * x4, axis=0, keepdims=True)
            stacked = jnp.concatenate([dyv * yoff - xt, dxdt * x4, jnp.broadcast_to(tail, (8, 4 * HD)),
                                       jnp.broadcast_to(gd_raw, (8, 4 * HD))], axis=0)
            seg = _dot01(eblk, stacked, m_left=False, terms=2)
            da_b = seg[0:Q] + rc
            dda_b = _dot01(utri, da_b, terms=2) + seg[2 * Q:2 * Q + 1]
            ddt_b = dda_b * a_b + seg[Q:2 * Q]
            gd_b = seg[2 * Q + 8:2 * Q + 9]
            ddt4 = jnp.zeros((Q, 128), F32)
            dda4 = jnp.zeros((Q, 128), F32)
            for j in range(4):
                ddt4 = jnp.where(lane128 == j, jnp.broadcast_to(ddt_b[:, HD * j:HD * j + 1], (Q, 128)), ddt4)
                dda4 = jnp.where(lane128 == j, jnp.broadcast_to(dda_b[:, HD * j:HD * j + 1], (Q, 128)), dda4)
            ddt_ref[gi] = ddt4 * sg4v
            ga = jnp.sum(dda4 * dt4v * a4, axis=0, keepdims=True)
            gd = jnp.zeros((1, 128), F32)
            for j in range(4):
                gd = jnp.where(lane128[0:1] == j, jnp.broadcast_to(gd_b[:, HD * j:HD * j + 1], (1, 128)), gd)
            row = lax.broadcasted_iota(jnp.int32, (8, 128), 0)
            st_ref[gi] += jnp.where(row == 0, ga, jnp.where(row == 1, gd, 0.0))
            dx_ref[:, xs] = d_b * dyv + dxdt * dt_b
            dc_ref[:, ns] = _bdot(dg, bm) + _bdot(dye, s4, "nt")
            db_ref[:, ns] = _bdot(dg, cm, "tn") + _bdot(xdt * decay, dsn, "nt")
            dS[gi] = elast * dsn + _bdot(cm, dye, "tn")

    rv = lambda c: nc - 1 - c
    sc = pl.BlockSpec((GPS, Q, 128), lambda g, c: (g, rv(c), 0))
    bw = NS * GPS
    return pl.pallas_call(
        kern, grid=(NG // GPS, nc),
        in_specs=[pl.BlockSpec((Q, 256 * GPS), lambda g, c: (rv(c), g)),
                  pl.BlockSpec((Q, bw), lambda g, c: (rv(c), INNER // bw + g)),
                  pl.BlockSpec((Q, bw), lambda g, c: (rv(c), (INNER + NG * NS) // bw + g)),
                  sc, sc, sc, pl.BlockSpec((GPS, 8, 128), lambda g, c: (g, 0, 0)),
                  pl.BlockSpec((GPS, 1, NS, 256), lambda g, c: (g, rv(c), 0, 0)),
                  pl.BlockSpec((Q, 256 * GPS), lambda g, c: (rv(c), g))],
        out_specs=(pl.BlockSpec((Q, 256 * GPS), lambda g, c: (rv(c), g)),
                   pl.BlockSpec((Q, bw), lambda g, c: (rv(c), g)),
                   pl.BlockSpec((Q, bw), lambda g, c: (rv(c), g)),
                   pl.BlockSpec((GPS, Q, 128), lambda g, c: (g, rv(c), 0)),
                   pl.BlockSpec((GPS, 8, 128), lambda g, c: (g, 0, 0))),
        out_shape=(jax.ShapeDtypeStruct((L, INNER), F32), jax.ShapeDtypeStruct((L, NG * NS), F32),
                   jax.ShapeDtypeStruct((L, NG * NS), F32), jax.ShapeDtypeStruct((NG, L, 128), F32),
                   jax.ShapeDtypeStruct((NG, 8, 128), F32)),
        scratch_shapes=[pltpu.VMEM((GPS, NS, 256), F32)], name="ssd_bwd",
        compiler_params=_cp(("parallel", "arbitrary")),
    )(xbc, xbc, xbc, dt4, cs4, sg4, vecg, s_all, dy)


def _dt_bwd(ddt, dproj, tl=256):
    L = ddt.shape[1]

    def kern(d_ref, _, dp_ref, gs_ref):
        @pl.when(pl.program_id(0) == 0)
        def _():
            gs_ref[...] = jnp.zeros_like(gs_ref)

        d = d_ref[0]
        for g in range(1, NG):
            d = d + pltpu.roll(d_ref[g], 4 * g, axis=1)
        gs_ref[...] += jnp.broadcast_to(jnp.sum(d, axis=0, keepdims=True), (8, 128))
        dp_ref[...] = jnp.concatenate([d, jnp.zeros_like(d)], axis=1).astype(BF16)

    return pl.pallas_call(
        kern, grid=(L // tl,),
        in_specs=[pl.BlockSpec((NG, tl, 128), lambda i: (0, i, 0)), pl.BlockSpec(memory_space=pl.ANY)],
        out_specs=(pl.BlockSpec((tl, 256), lambda i: (i, C_DT // 256)), pl.BlockSpec((8, 128), lambda i: (0, 0))),
        out_shape=(jax.ShapeDtypeStruct(dproj.shape, BF16), jax.ShapeDtypeStruct((8, 128), F32)),
        input_output_aliases={1: 0}, name="dt_bwd", compiler_params=_cp(("arbitrary",)),
    )(ddt, dproj)


GW = INNER // NG


def _gnorm_fwd(y, proj, w, tl=256):
    L = y.shape[0]
    zoff = C_Z // 1024

    def kern(y_ref, z_ref, w_ref, o_ref):
        z = z_ref[...].astype(F32)
        yz = y_ref[...] * (z * _sigmoid(z))
        wv = w_ref[...]
        for k in range(1024 // GW):
            sl = slice(GW * k, GW * (k + 1))
            v = yz[:, sl]
            rg = lax.rsqrt(jnp.mean(v * v, axis=-1, keepdims=True) + EPS)
            o_ref[:, sl] = ((v * rg) * wv[:, sl]).astype(BF16)

    blk = pl.BlockSpec((tl, 1024), lambda i, j: (i, j))
    return pl.pallas_call(
        kern, grid=(L // tl, 2),
        in_specs=[blk, pl.BlockSpec((tl, 1024), lambda i, j: (i, zoff + j)), pl.BlockSpec((1, 1024), lambda i, j: (0, j))],
        out_specs=blk, out_shape=jax.ShapeDtypeStruct((L, INNER), BF16), name="gnorm_fwd",
        compiler_params=_cp(("parallel", "parallel")),
    )(y, proj, w.reshape(1, INNER))


def _gnorm_bwd(dyb, y, proj, w, dproj, tl=256):
    L = y.shape[0]
    zoff = C_Z // 1024

    def kern(d_ref, y_ref, z_ref, w_ref, _, dy_ref, dp_ref, gw_ref):
        @pl.when(pl.program_id(1) == 0)
        def _():
            gw_ref[...] = jnp.zeros_like(gw_ref)

        z = z_ref[...].astype(F32)
        sg = _sigmoid(z)
        sz = z * sg
        yv = y_ref[...]
        yz = yv * sz
        dv = d_ref[...]
        wv = w_ref[...]
        for k in range(1024 // GW):
            sl = slice(GW * k, GW * (k + 1))
            v = yz[:, sl]
            rg = lax.rsqrt(jnp.mean(v * v, axis=-1, keepdims=True) + EPS)
            vn = v * rg
            dk = dv[:, sl]
            gw_ref[:, sl] += jnp.broadcast_to(jnp.sum(dk * vn, axis=0, keepdims=True), (8, GW))
            dvn = dk * wv[:, sl]
            dyz = rg * (dvn - vn * jnp.mean(dvn * vn, axis=-1, keepdims=True))
            dy_ref[:, sl] = dyz * sz[:, sl]
            dp_ref[:, sl] = (dyz * yv[:, sl] * (sg[:, sl] * (1.0 + z[:, sl] * (1.0 - sg[:, sl])))).astype(BF16)

    blk = pl.BlockSpec((tl, 1024), lambda j, i: (i, j))
    zblk = pl.BlockSpec((tl, 1024), lambda j, i: (i, zoff + j))
    return pl.pallas_call(
        kern, grid=(2, L // tl),
        in_specs=[blk, blk, zblk, pl.BlockSpec((1, 1024), lambda j, i: (0, j)), pl.BlockSpec(memory_space=pl.ANY)],
        out_specs=(blk, zblk, pl.BlockSpec((8, 1024), lambda j, i: (0, j))),
        out_shape=(jax.ShapeDtypeStruct((L, INNER), F32), jax.ShapeDtypeStruct(dproj.shape, BF16),
                   jax.ShapeDtypeStruct((8, INNER), F32)),
        input_output_aliases={4: 1}, name="gnorm_bwd", compiler_params=_cp(("parallel", "arbitrary")),
    )(dyb, y, proj, w.reshape(1, INNER), dproj)


def _merge_fwd(proj, bg, br_a, br_b, tl=256):
    L = proj.shape[0]
    goff = C_GATE // 1024

    def kern(g1_ref, g2_ref, b1_ref, b2_ref, a_ref, b_ref, o_ref):
        g1 = _sigmoid(g1_ref[...].astype(F32) + b1_ref[...])
        g2 = _sigmoid(g2_ref[...].astype(F32) + b2_ref[...])
        o_ref[...] = (g1 * a_ref[...] + g2 * b_ref[...]).astype(BF16)

    row = pl.BlockSpec((tl, 1024), lambda i: (i, 0))
    bg2 = bg.reshape(1, 2 * D)
    return pl.pallas_call(
        kern, grid=(L // tl,),
        in_specs=[pl.BlockSpec((tl, 1024), lambda i: (i, goff)), pl.BlockSpec((tl, 1024), lambda i: (i, goff + 1)),
                  pl.BlockSpec((1, 1024), lambda i: (0, 0)), pl.BlockSpec((1, 1024), lambda i: (0, 1)), row, row],
        out_specs=row, out_shape=jax.ShapeDtypeStruct((L, D), BF16), name="merge_fwd",
        compiler_params=_cp(("parallel",)),
    )(proj, proj, bg2, bg2, br_a, br_b)


def _merge_bwd(dm, proj, bg, br_a, br_b, dproj, tl=256):
    L = proj.shape[0]
    goff = C_GATE // 1024

    def kern(dm_ref, g_ref, b_ref, a_ref, bb_ref, _, dbr_ref, dp_ref, gb_ref):
        j = pl.program_id(0)

        @pl.when(pl.program_id(1) == 0)
        def _():
            gb_ref[...] = jnp.zeros_like(gb_ref)

        g = _sigmoid(g_ref[...].astype(F32) + b_ref[...])
        br = jnp.where(j == 0, a_ref[...], bb_ref[...])
        dmv = dm_ref[...]
        dbr_ref[0] = (dmv * g).astype(BF16)
        dgate = dmv * br * g * (1.0 - g)
        gb_ref[...] += jnp.broadcast_to(jnp.sum(dgate, axis=0, keepdims=True), (8, 1024))
        dp_ref[...] = dgate.astype(BF16)

    row = pl.BlockSpec((tl, 1024), lambda j, i: (i, 0))
    gblk = pl.BlockSpec((tl, 1024), lambda j, i: (i, goff + j))
    return pl.pallas_call(
        kern, grid=(2, L // tl),
        in_specs=[row, gblk, pl.BlockSpec((1, 1024), lambda j, i: (0, j)), row, row, pl.BlockSpec(memory_space=pl.ANY)],
        out_specs=(pl.BlockSpec((1, tl, 1024), lambda j, i: (j, i, 0)), gblk, pl.BlockSpec((8, 1024), lambda j, i: (0, j))),
        out_shape=(jax.ShapeDtypeStruct((2, L, D), BF16), jax.ShapeDtypeStruct(dproj.shape, BF16),
                   jax.ShapeDtypeStruct((8, 2 * D), F32)),
        input_output_aliases={5: 1}, name="merge_bwd", compiler_params=_cp(("parallel", "arbitrary")),
    )(dm, proj, bg.reshape(1, 2 * D), br_a, br_b, dproj)


def _coords():
    return lax.axis_index("x"), lax.axis_index("y"), lax.axis_index("c")


def _other_chips(sk):
    xk, yk = sk // 2, sk % 2
    return [((1 - xk, yk), 2 * (1 - xk) + yk), ((xk, 1 - yk), 2 * xk + 1 - yk), ((1 - xk, 1 - yk), 2 * (1 - xk) + 1 - yk)]


def _rows(start, size):
    assert size % 128 == 0
    return pl.ds(pl.multiple_of(start, 128), size)


def _per_chip(fn):
    x, y, _ = _coords()
    s = 2 * x + y
    for sk in range(4):
        pl.when(s == sk)(functools.partial(fn, sk))


XTRA = PIECE - PMAIN


def _place(shard, full_shape, block, index_map, idx, name, blk0=0, nblk=None, dep=None):
    in_block = block[-2:]
    if nblk is None:
        nblk = shard.shape[0] // in_block[0]

    def kern(idx_ref, s_ref, *rest):
        o_ref = rest[-1]
        o_ref[...] = s_ref[...].astype(BF16).reshape(o_ref.shape)

    grid_spec = pltpu.PrefetchScalarGridSpec(
        num_scalar_prefetch=1, grid=(nblk,),
        in_specs=[pl.BlockSpec(in_block, lambda i, idx_ref: (blk0 + i, 0))] + ([_ANY] if dep is not None else []),
        out_specs=pl.BlockSpec(block, index_map))
    args = (idx, shard) + ((dep,) if dep is not None else ())
    return pl.pallas_call(kern, grid_spec=grid_spec, out_shape=jax.ShapeDtypeStruct(full_shape, BF16), name=name,
                          compiler_params=_cp(("arbitrary",)))(*args)


_SEM = pl.BlockSpec(memory_space=pltpu.SEMAPHORE)
_EFFECT = pltpu.SideEffectType.DATAFLOW_SIDE_EFFECTING


_ANY = pl.BlockSpec(memory_space=pl.ANY)


def _tie(v, dep, name):
    def body(v_ref, dep_ref, o_ref):
        del v_ref, dep_ref, o_ref

    return pl.pallas_call(body, out_shape=jax.ShapeDtypeStruct(v.shape, v.dtype), in_specs=[_ANY, _ANY],
                          out_specs=_ANY, input_output_aliases={0: 0}, name=name)(v, dep)


def _split_call(name, arrays, start=None, wait=None, wait_sems=None, after=None):
    keys = list(arrays)
    n = len(keys)
    n_start = start.n if start is not None else 0
    afters = [] if after is None else (list(after) if isinstance(after, (list, tuple)) else [after])

    def body(*refs):
        pos = n
        if wait is not None:
            wss, wrs = refs[pos], refs[pos + 1]
            pos += 2
        pos += len(afters)
        if start is not None:
            nss, nrs = refs[pos], refs[pos + 1]
            pos += 2
        R = dict(zip(keys, refs[pos:pos + n]))
        token = refs[pos + n]
        x, y, c = _coords()

        def desc(src, dst, dev, ss, rs, k):
            return pltpu.make_async_remote_copy(src_ref=src, dst_ref=dst, send_sem=ss.at[k], recv_sem=rs.at[k],
                                                device_id=dev, device_id_type=MESH)

        def run(sk):
            if wait is not None:
                for k, (snd, land) in enumerate(wait.copies(sk, R)):
                    if snd is not None:
                        desc(snd[0], snd[1], snd[2], wss, wrs, k).wait_send()
                    if land is not None:
                        desc(land, land, (x, y, c), wss, wrs, k).wait_recv()
            if start is not None:
                for k, (snd, land) in enumerate(start.copies(sk, R)):
                    if snd is not None:
                        desc(snd[0], snd[1], snd[2], nss, nrs, k).start()

        _per_chip(run)
        token[...] = jnp.zeros_like(token)

    hbm = pl.BlockSpec(memory_space=HBM)
    vals = [arrays[k] for k in keys]
    ins, in_specs = list(vals), [hbm] * n
    if wait is not None:
        ins += list(wait_sems)
        in_specs += [_SEM, _SEM]
    ins += afters
    in_specs += [pl.BlockSpec(memory_space=pl.ANY)] * len(afters)
    out_shape, out_specs = [], []
    if start is not None:
        out_shape += [pltpu.SemaphoreType.DMA((n_start,)), pltpu.SemaphoreType.DMA((n_start,))]
        out_specs += [_SEM, _SEM]
    first = len(out_shape)
    out_shape += [jax.ShapeDtypeStruct(v.shape, v.dtype) for v in vals] + [jax.ShapeDtypeStruct((8, 128), F32)]
    out_specs += [hbm] * n + [pl.BlockSpec(memory_space=pltpu.VMEM)]
    res = pl.pallas_call(
        body, out_shape=tuple(out_shape), in_specs=in_specs, out_specs=tuple(out_specs),
        input_output_aliases={i: first + i for i in range(n)}, name=name,
        compiler_params=pltpu.CompilerParams(has_side_effects=_EFFECT),
    )(*ins)
    sems = (res[0], res[1]) if start is not None else None
    return dict(zip(keys, res[first:first + n])), sems, res[-1]


class _Plan:
    def __init__(self, n, copies):
        self.n, self.copies = n, copies


_HM, _HX = PMAIN // 2, XTRA // 2
_WIN = {
    "wct": (True, lambda r, sc, hc: r.at[_rows(PMAIN * sc + _HM * hc, _HM), :]),
    "xt": (True, lambda r, sc, hc: r.at[sc, _rows(_HX * hc, _HX), :]),
    "w1": (True, lambda r, sc, hc: r.at[_rows(512 * hc, 512), pl.ds(1024 * sc, 1024)]),
    "w2": (True, lambda r, sc, hc: r.at[_rows(1024 * sc + 512 * hc, 512), :]),
    "wa": (True, lambda r, sc, hc: r.at[_rows(256 * sc + 128 * hc, 128), :]),
    "wb": (True, lambda r, sc, hc: r.at[_rows(512 * sc + 256 * hc, 256), :]),
    "wo": (True, lambda r, sc, hc: r.at[_rows(256 * sc + 128 * hc, 128), :]),
    "cw": (False, lambda r, sc, hc: r.at[sc]),
}


def _ag_chips_plan(keys):
    def copies(sk, R):
        _, _, c = _coords()
        out = []
        for key in keys:
            win = _WIN[key][1]
            for (px, py), ps in _other_chips(sk):
                w = win(R[key], sk, c)
                out.append(((w, w, (px, py, c)), win(R[key], ps, c)))
        return out
    return _Plan(3 * len(keys), copies)


def _ag_sibling_plan(keys):
    keys = [k for k in keys if _WIN[k][0]]

    def copies(sk, R):
        x, y, c = _coords()
        out = []
        for key in keys:
            win = _WIN[key][1]
            for _, ps in _other_chips(sk):
                w = win(R[key], ps, c)
                out.append(((w, w, (x, y, 1 - c)), win(R[key], ps, 1 - c)))
        return out
    return _Plan(3 * len(keys), copies)


def _fix_wct(wct, xt):
    nb = PMAIN // XTRA

    def kern(w_ref, x_ref, o_ref):
        k = pl.program_id(0)
        xv = x_ref[0]
        o_ref[...] = jnp.where(k < 3, (w_ref[...].astype(F32) + xv.astype(F32)).astype(BF16), xv)

    blk = pl.BlockSpec((XTRA, D), lambda k: (nb * (k + 1), 0))
    rblk = pl.BlockSpec((XTRA, D), lambda k: (jnp.where(k < 3, nb * (k + 1), 0), 0))
    return pl.pallas_call(
        kern, grid=(4,), in_specs=[rblk, pl.BlockSpec((1, XTRA, D), lambda k: (k, 0, 0))], out_specs=blk,
        out_shape=jax.ShapeDtypeStruct(wct.shape, BF16), input_output_aliases={0: 0}, name="fix_wct",
        compiler_params=_cp(("arbitrary",)),
    )(wct, xt)


_HP = PIECE // 2
_GWIN = [
    lambda r, sc, hc: r.at[_rows(PMAIN * sc + _HP * hc, _HP), :],
    lambda r, sc, hc: r.at[_rows(512 * hc, 512), pl.ds(1024 * sc, 1024)],
    lambda r, sc, hc: r.at[_rows(1024 * sc + 512 * hc, 512), :],
    lambda r, sc, hc: r.at[_rows(256 * sc + 128 * hc, 128), :],
    lambda r, sc, hc: r.at[_rows(512 * sc + 256 * hc, 256), :],
    lambda r, sc, hc: r.at[_rows(256 * sc + 128 * hc, 128), :],
]
HALF_SHAPES = [(PIECE // 2, D), (512, 1024), (512, 1024), (128, 1024), (256, 1024), (128, 1024)]


def _rs_sibling_plan(ts):
    def copies(sk, R):
        x, y, c = _coords()
        out = []
        for t in ts:
            for sc in range(4):
                land = R["ra%d" % t].at[sc]
                out.append(((_GWIN[t](R["g%d" % t], sc, 1 - c), land, (x, y, 1 - c)), land))
        return out
    return _Plan(4 * len(ts), copies)


def _rs_chips_plan(ts):
    def copies(sk, R):
        _, _, c = _coords()
        out = []
        for t in ts:
            for j, ((px, py), ps) in enumerate(_other_chips(sk)):
                land = R["rb%d" % t].at[j]
                out.append(((R["hb%d" % t].at[ps], land, (px, py, c)), land))
        return out
    return _Plan(3 * len(ts), copies)


def _rs_share_plan(ts):
    def copies(sk, R):
        x, y, c = _coords()
        out = []
        for t in ts:
            rows = HALF_SHAPES[t][0]
            mine = R["f%d" % t].at[_rows(rows * c, rows), :]
            out.append(((mine, mine, (x, y, 1 - c)), R["f%d" % t].at[_rows(rows * (1 - c), rows), :]))
        return out
    return _Plan(len(ts), copies)


def _half_tiling(t):
    rows, cols = HALF_SHAPES[t]
    if t == 0:
        return (256, cols), rows // 256, lambda i: (i, 0)
    if t == 1:
        return (rows, 256), cols // 256, lambda i: (0, i)
    return (rows, cols), 1, lambda i: (0, 0)


def _window_block(t, sc, hc, i):
    if t == 0:
        return (PMAIN // 256) * sc + (PIECE // 512) * hc + i, 0
    if t == 1:
        return hc, 4 * sc + i
    return 2 * sc + hc, 0


def _chip_sum(g, ra, t, idx, name):
    rows, cols = HALF_SHAPES[t]
    blk, nblk, inner = _half_tiling(t)

    def kern(idx_ref, g_ref, r_ref, hb_ref, hf_ref):
        v = g_ref[...].astype(F32) + r_ref[0].astype(F32)
        hb_ref[0] = v.astype(BF16)

        @pl.when(pl.program_id(1) == idx_ref[0])
        def _():
            hf_ref[...] = v

    gmap = lambda i, sc, idx_ref: _window_block(t, sc, idx_ref[1], i)
    omap = lambda i, sc, idx_ref: (sc,) + inner(i)
    grid_spec = pltpu.PrefetchScalarGridSpec(
        num_scalar_prefetch=1, grid=(nblk, 4),
        in_specs=[pl.BlockSpec(blk, gmap), pl.BlockSpec((1,) + blk, omap)],
        out_specs=(pl.BlockSpec((1,) + blk, omap), pl.BlockSpec(blk, lambda i, sc, idx_ref: inner(i))))
    return pl.pallas_call(
        kern, grid_spec=grid_spec,
        out_shape=(jax.ShapeDtypeStruct((4, rows, cols), BF16), jax.ShapeDtypeStruct((rows, cols), F32)),
        name=name, compiler_params=_cp(("parallel", "arbitrary")),
    )(idx, g, ra)


def _final_sum(hf, rb, t, idx, name):
    rows, cols = HALF_SHAPES[t]
    blk, nblk, inner = _half_tiling(t)
    nbr = rows // blk[0]

    def kern(idx_ref, h_ref, r_ref, o_ref):
        o_ref[...] = ((h_ref[...] + r_ref[0].astype(F32)) + r_ref[1].astype(F32)) + r_ref[2].astype(F32)

    def omap(i, idx_ref):
        r, cidx = inner(i)
        return nbr * idx_ref[1] + r, cidx

    grid_spec = pltpu.PrefetchScalarGridSpec(
        num_scalar_prefetch=1, grid=(nblk,),
        in_specs=[pl.BlockSpec(blk, lambda i, idx_ref: inner(i)),
                  pl.BlockSpec((3,) + blk, lambda i, idx_ref: (0,) + inner(i))],
        out_specs=pl.BlockSpec(blk, omap))
    return pl.pallas_call(
        kern, grid_spec=grid_spec, out_shape=jax.ShapeDtypeStruct((2 * rows, cols), F32),
        name=name, compiler_params=_cp(("parallel",)),
    )(idx, hf, rb)


class _ReduceScatter:
    def __init__(self, ts, grads, idx, tag):
        self.ts, self.idx, self.tag = ts, idx, tag
        arr = {}
        for t in ts:
            arr["g%d" % t] = grads[t]
            arr["ra%d" % t] = lax.empty((4,) + HALF_SHAPES[t], BF16)
        self.plan = _rs_sibling_plan(ts)
        self.arr, self.sems, self.token = _split_call("rs_sibling_start_" + tag, arr, start=self.plan)

    def chips(self, after):
        arr, _, _ = _split_call("rs_sibling_wait_" + self.tag, self.arr, wait=self.plan, wait_sems=self.sems, after=after)
        brr, self.hf = {}, {}
        for t in self.ts:
            hb, self.hf[t] = _chip_sum(arr["g%d" % t], arr["ra%d" % t], t, self.idx, "chip_sum_%d" % t)
            brr["hb%d" % t] = hb
            brr["rb%d" % t] = lax.empty((3,) + HALF_SHAPES[t], BF16)
        self.plan = _rs_chips_plan(self.ts)
        self.arr, self.sems, self.token = _split_call("rs_chips_start_" + self.tag, brr, start=self.plan)
        return self.token

    def share(self, after):
        brr, _, _ = _split_call("rs_chips_wait_" + self.tag, self.arr, wait=self.plan, wait_sems=self.sems, after=after)
        frr = {"f%d" % t: _final_sum(self.hf[t], brr["rb%d" % t], t, self.idx, "final_sum_%d" % t) for t in self.ts}
        self.plan = _rs_share_plan(self.ts)
        self.arr, self.sems, self.token = _split_call("rs_share_start_" + self.tag, frr, start=self.plan)
        return self.token

    def result(self, after):
        frr, _, _ = _split_call("rs_share_wait_" + self.tag, self.arr, wait=self.plan, wait_sems=self.sems, after=after)
        return {t: frr["f%d" % t] for t in self.ts}


def _all8_plan(key):
    def copies(sk, R):
        x, y, c = _coords()
        own = R[key].at[4 * x + 2 * y + c]
        out = []
        for k in range(1, 8):
            dev = ((1 - x) if (k >> 2) & 1 else x, (1 - y) if (k >> 1) & 1 else y, (1 - c) if k & 1 else c)
            out.append(((own, own, dev), R[key].at[4 * dev[0] + 2 * dev[1] + dev[2]]))
        return out
    return _Plan(7, copies)


def _small_all_gather(v):
    def body(v_ref, o_ref, send_sems, recv_sems, loc_sem):
        x, y, c = _coords()
        me = 4 * x + 2 * y + c
        lc = pltpu.make_async_copy(v_ref, o_ref.at[me], loc_sem)
        lc.start()
        cps = []
        for k in range(1, 8):
            fx, fy, fc = (k >> 2) & 1, (k >> 1) & 1, k & 1
            dev = ((1 - x) if fx else x, (1 - y) if fy else y, (1 - c) if fc else c)
            cp = pltpu.make_async_remote_copy(src_ref=v_ref, dst_ref=o_ref.at[me], send_sem=send_sems.at[k - 1],
                                              recv_sem=recv_sems.at[k - 1], device_id=dev, device_id_type=MESH)
            cp.start()
            cps.append((cp, 4 * dev[0] + 2 * dev[1] + dev[2]))
        for k, (cp, frm) in enumerate(cps):
            got = o_ref.at[frm]
            pltpu.make_async_remote_copy(src_ref=got, dst_ref=got, send_sem=send_sems.at[k], recv_sem=recv_sems.at[k],
                                         device_id=(x, y, c), device_id_type=MESH).wait_recv()
        for cp, _ in cps:
            cp.wait_send()
        lc.wait()

    hbm = pl.BlockSpec(memory_space=HBM)
    return pl.pallas_call(
        body, out_shape=jax.ShapeDtypeStruct((8,) + v.shape, F32), in_specs=[hbm], out_specs=hbm,
        scratch_shapes=[pltpu.SemaphoreType.DMA((7,)), pltpu.SemaphoreType.DMA((7,)), pltpu.SemaphoreType.DMA(())],
        name="small_all_gather", compiler_params=pltpu.CompilerParams(has_side_effects=True),
    )(v)


def _sum8(v, name="small_sum"):
    def kern(v_ref, o_ref):
        acc = v_ref[0]
        for k in range(1, 8):
            acc = acc + v_ref[k]
        o_ref[...] = acc

    return pl.pallas_call(kern, out_shape=jax.ShapeDtypeStruct(v.shape[1:], F32), name=name)(v)


def _adamw(w, g, m, v, name, tr=128, blk0=0, nblk=None, into=None, copy_g=False):
    R, C = w.shape
    tr = min(tr, R)
    if nblk is None:
        assert R % tr == 0 and blk0 == 0
        nblk = R // tr
    n_out = 4 if copy_g else 3

    def kern(*refs):
        w_ref, g_ref, m_ref, v_ref = refs[:4]
        d_ref, mo_ref, vo_ref = refs[-n_out:][:3]
        gv = g_ref[...]
        mn = ADAM_B1 * m_ref[...] + (1.0 - ADAM_B1) * gv
        vn = ADAM_B2 * v_ref[...] + (1.0 - ADAM_B2) * (gv * gv)
        m_hat = mn / (1.0 - ADAM_B1 ** ADAM_STEP)
        v_hat = vn / (1.0 - ADAM_B2 ** ADAM_STEP)
        d_ref[...] = -ADAM_LR * (m_hat / (jnp.sqrt(v_hat) + ADAM_EPS) + ADAM_WD * w_ref[...])
        mo_ref[...] = mn
        vo_ref[...] = vn
        if copy_g:
            refs[-1][...] = gv

    blk = pl.BlockSpec((tr, C), lambda i: (blk0 + i, 0))
    sd = jax.ShapeDtypeStruct((R, C), F32)
    in_specs, args, aliases = [blk] * 4, [w, g, m, v], {}
    if into is not None:
        in_specs += [pl.BlockSpec(memory_space=pl.ANY)] * 3
        args += list(into)
        aliases = {4: 0, 5: 1, 6: 2}
    return pl.pallas_call(kern, grid=(nblk,), in_specs=in_specs, out_specs=(blk,) * n_out, out_shape=(sd,) * n_out,
                          input_output_aliases=aliases, name=name, compiler_params=_cp(("parallel",)))(*args)


def _to_piece(wt, s):
    z = lambda n: jnp.zeros((n, D), wt.dtype)
    pads = [functools.partial(lambda k, w: jnp.pad(w, ((8 * k, PIECE - W_SHARD - 8 * k), (0, 0))), k) for k in range(3)]
    last = lambda w: jnp.concatenate([z(24), w[:744], w[776:], w[744:776], z(PIECE - 24 - W_SHARD)], axis=0)
    return lax.switch(s, pads + [last], wt)


def _from_piece(p, s):
    cuts = [functools.partial(lambda k, q: q[8 * k:8 * k + W_SHARD], k) for k in range(3)]
    last = lambda q: jnp.concatenate([q[24:768], q[2816:2848], q[768:2816]], axis=0)
    return lax.switch(s, cuts + [last], p)


_SMALL = [("b_gate", 2048), ("ssm_conv_b", 4096), ("dt_bias", 32), ("A_log", 32), ("D_skip", 32),
          ("ssm_norm_w", 2048), ("norm_mlp", 1024), ("norm_final", 1024), ("sc_conv_w", 3072), ("ssm_conv_w", 16384),
          ("loss", 1)]


def _pack(vals, table, rows):
    parts = []
    for name, n in table:
        v = vals[name].reshape(-1).astype(F32)
        pad = (-n) % 128
        parts.append(jnp.pad(v, (0, pad)) if pad else v)
    flat = jnp.concatenate(parts)
    return jnp.pad(flat, (0, rows * 128 - flat.shape[0])).reshape(rows, 128)


def _unpack(arr, table):
    flat = arr.reshape(-1)
    out, off = {}, 0
    for name, n in table:
        out[name] = flat[off:off + n]
        off += n + ((-n) % 128)
    return out


def kernel(x, norm_mix, w_in, b_gate, sc_conv_w, ssm_conv_w, ssm_conv_b, dt_bias, A_log, D_skip, ssm_norm_w, w_branch_sc, w_branch_ssm, w_out, norm_mlp, w_mlp1, w_mlp2, norm_final, loss_target, m_norm_mix, m_w_in, m_b_gate, m_sc_conv_w, m_ssm_conv_w, m_ssm_conv_b, m_dt_bias, m_A_log, m_D_skip, m_ssm_norm_w, m_w_branch_sc, m_w_branch_ssm, m_w_out, m_norm_mlp, m_w_mlp1, m_w_mlp2, m_norm_final, v_norm_mix, v_w_in, v_b_gate, v_sc_conv_w, v_ssm_conv_w, v_ssm_conv_b, v_dt_bias, v_A_log, v_D_skip, v_ssm_norm_w, v_w_branch_sc, v_w_branch_ssm, v_w_out, v_norm_mlp, v_w_mlp1, v_w_mlp2, v_norm_final):
    L = x.shape[1]
    nc = L // Q
    xi, yi, ci = lax.axis_index("x"), lax.axis_index("y"), lax.axis_index("c")
    s = 2 * xi + yi
    idx = jnp.stack([s, ci]).astype(jnp.int32)
    x0 = x.reshape(L, D)
    tgt = loss_target.reshape(L, D)

    piece = _to_piece(w_in.T, s)
    nb = PMAIN // XTRA
    wct0 = _place(piece, (NCW, D), (XTRA, D), lambda i, r: (nb * r[0] + i, 0), idx, "place_wct", nblk=nb)
    xt0 = _place(piece, (4, XTRA, D), (1, XTRA, D), lambda i, r: (r[0], 0, 0), idx, "place_xt", blk0=nb, nblk=1)
    cws = jnp.zeros((8, 1280), F32)
    cws = cws.at[0:3, 0:256].set(sc_conv_w).at[0:4, 256:1280].set(ssm_conv_w)
    cw0 = lax.dynamic_update_slice(jnp.zeros((4, 8, 1280), F32), cws[None], (s, 0, 0))
    win_keys, mid_keys, end_keys = ["wct", "xt", "cw"], ["wa", "wb", "wo", "w1"], ["w2"]
    gw, sems_w, tok = _split_call("ag_win_start", {"wct": wct0, "xt": xt0, "cw": cw0}, start=_ag_chips_plan(win_keys))
    wa0 = _place(w_branch_sc, (D, D), (256, 1024), lambda i, r: (r[0], 0), idx, "place_wa", dep=tok)
    wb0 = _place(w_branch_ssm, (INNER, D), (512, 1024), lambda i, r: (r[0], 0), idx, "place_wb", dep=tok)
    wo0 = _place(w_out, (D, D), (256, 1024), lambda i, r: (r[0], 0), idx, "place_wo", dep=tok)
    w10 = _place(w_mlp1, (D, DFF), (256, 1024), lambda i, r: (i, r[0]), idx, "place_w1", dep=tok)
    gm, sems_m, tok = _split_call("ag_mid_start", {"wa": wa0, "wb": wb0, "wo": wo0, "w1": w10},
                                  start=_ag_chips_plan(mid_keys))
    w20 = _place(w_mlp2, (DFF, D), (256, 1024), lambda i, r: (4 * r[0] + i, 0), idx, "place_w2", dep=tok)
    ge, sems_e, tok = _split_call("ag_end_start", {"w2": w20}, start=_ag_chips_plan(end_keys))
    h = _rms_fwd(x0, norm_mix, "rms_mix", dep=tok)
    gw, sems_w, tok = _split_call("ag_win_pass", gw, wait=_ag_chips_plan(win_keys), wait_sems=sems_w,
                                  start=_ag_sibling_plan(win_keys), after=h)
    gw, _, _ = _split_call("ag_win_done", gw, wait=_ag_sibling_plan(win_keys), wait_sems=sems_w, after=tok)
    wc, cw_all = _fix_wct(gw["wct"], gw["xt"]), gw["cw"]
    sc_w_full = jnp.concatenate([cw_all[k, :, 0:256] for k in range(4)], axis=1)
    ssm_w_full = jnp.concatenate([cw_all[k, :, 256:1280] for k in range(4)], axis=1)
    cw4 = ssm_w_full.at[4].set(ssm_conv_b)
    vec = jnp.zeros((8, 128), F32).at[0, :NH].set(dt_bias).at[1, :NH].set(A_log)
    vecg = jnp.zeros((NG, 8, 128), F32).at[:, 0, :4].set(A_log.reshape(NG, 4)).at[:, 1, :4].set(D_skip.reshape(NG, 4))

    proj = _matmul(h, wc, "nt", BF16, 1024, 1280, 1024, "in_proj", n_outer=True)
    dtraw = _matmul(h, wc[C_DT:], "nt", F32, 512, 256, 1024, "in_proj_dt")
    gm, sems_m, tok = _split_call("ag_mid_pass", gm, wait=_ag_chips_plan(mid_keys), wait_sems=sems_m,
                                  start=_ag_sibling_plan(mid_keys), after=proj)
    proj = _tie(proj, tok, "tie_proj")
    ya = _sc_fwd(proj, sc_w_full)
    xbc = _ssm_conv_fwd(proj, cw4)
    dt4, cs4, sg4 = _dt_prep(dtraw, vec)
    ge, sems_e, tok = _split_call("ag_end_pass", ge, wait=_ag_chips_plan(end_keys), wait_sems=sems_e,
                                  start=_ag_sibling_plan(end_keys), after=xbc)
    xbc = _tie(xbc, tok, "tie_xbc")
    y, s_all = _ssd_fwd(xbc, dt4, cs4, vecg)
    yb = _gnorm_fwd(y, proj, ssm_norm_w)
    gm, _, _ = _split_call("ag_mid_done", gm, wait=_ag_sibling_plan(mid_keys), wait_sems=sems_m, after=yb)
    ge, _, _ = _split_call("ag_end_done", ge, wait=_ag_sibling_plan(end_keys), wait_sems=sems_e, after=yb)
    wa, wb, wo, w1, w2 = gm["wa"], gm["wb"], gm["wo"], gm["w1"], ge["w2"]
    br_a = _matmul(ya, wa, "nn", F32, 1024, 1024, 1024, "branch_sc")
    br_b = _matmul(yb, wb, "nn", F32, 1024, 1024, 2048, "branch_ssm")
    merged = _merge_fwd(proj, b_gate, br_a, br_b)
    x1 = _matmul(merged, wo, "nn", F32, 1024, 1024, 1024, "out_proj", epi="res", extra=x0)
    h2 = _rms_fwd(x1, norm_mlp, "rms_mlp")
    a1, rl = _matmul(h2, w1, "nn", BF16, 1024, 1024, 1024, "mlp1", epi="relu2", n_outer=True)
    x2 = _matmul(rl, w2, "nn", F32, 512, 1024, 4096, "mlp2", epi="res", extra=x1)
    dx2, g_nf, loss8 = _final(x2, norm_final, tgt)

    da = _matmul(dx2, w2, "nt", BF16, 1024, 1024, 1024, "mlp2_dx", epi="drelu", extra=a1, n_outer=True)
    g_w2 = _matmul(rl, dx2, "tn", BF16, 1024, 1024, 2048, "mlp2_dw")
    g_w1 = _matmul(h2, da, "tn", BF16, 1024, 1024, 2048, "mlp1_dw")
    dh2 = _matmul(da, w1, "nt", F32, 512, 1024, 4096, "mlp1_dx")
    dx1, g_nmlp = _rms_bwd(dh2, x1, norm_mlp, dx2, "rms_mlp_bwd")
    dmerged = _matmul(dx1, wo, "nt", F32, 1024, 1024, 1024, "out_proj_dx")
    g_wo = _matmul(merged, dx1, "tn", BF16, 1024, 1024, 2048, "out_proj_dw")
    dproj = lax.empty((L, NCW), BF16)
    dbr, dproj, g_bg = _merge_bwd(dmerged, proj, b_gate, br_a, br_b, dproj)
    dya = _matmul(dbr[0], wa, "nt", F32, 1024, 1024, 1024, "branch_sc_dx")
    g_wa = _matmul(ya, dbr[0], "tn", BF16, 1024, 1024, 2048, "branch_sc_dw")
    dproj, g_scw = _sc_bwd(dya, proj, sc_w_full, dproj)
    dyb = _matmul(dbr[1], wb, "nt", F32, 1024, 1024, 1024, "branch_ssm_dx", n_outer=True)
    g_wb = _matmul(yb, dbr[1], "tn", BF16, 1024, 1024, 2048, "branch_ssm_dw")
    rs_a = _ReduceScatter([1, 2, 3, 4, 5], {1: g_w1, 2: g_w2, 3: g_wa, 4: g_wb, 5: g_wo}, idx, "a")
    dy, dproj, g_snw = _gnorm_bwd(_tie(dyb, rs_a.token, "tie_dyb"), y, proj, ssm_norm_w, dproj)
    tok = rs_a.chips(after=dy)
    dxs, dbm, dcm, ddt_g, st = _ssd_bwd(xbc, dt4, cs4, sg4, vecg, s_all, _tie(dy, tok, "tie_dy"))
    dproj, gx1 = _ssm_conv_bwd(dxs, proj, cw4, dproj, 0, "ssm_conv_bwd_x")
    dproj, gx2 = _ssm_conv_bwd(dbm, proj, cw4, dproj, INNER, "ssm_conv_bwd_b")
    dproj, gx3 = _ssm_conv_bwd(dcm, proj, cw4, dproj, INNER + NG * NS, "ssm_conv_bwd_c")
    g_cw4 = jnp.concatenate([gx1, gx2, gx3], axis=1)
    dproj, g_dtb = _dt_bwd(ddt_g, dproj)
    small = {"b_gate": g_bg[0], "ssm_conv_b": g_cw4[4], "dt_bias": g_dtb[0, :NH],
             "A_log": st[:, 0, :4], "D_skip": st[:, 1, :4], "ssm_norm_w": g_snw[0], "norm_mlp": g_nmlp[0],
             "norm_final": g_nf[0], "sc_conv_w": g_scw[0:3], "ssm_conv_w": g_cw4[0:4], "loss": loss8[0, 0:1]}
    small_sum = _sum8(_small_all_gather(_pack(small, _SMALL, SMALL_ROWS)))
    gs = _unpack(small_sum, _SMALL)
    g_wc = _matmul(dproj, h, "tn", BF16, 1280, 1024, 2048, "in_proj_dw", dep=small_sum)
    rs_b = _ReduceScatter([0], {0: g_wc}, idx, "b")
    tok = rs_a.share(after=rs_b.token)
    tok = rs_b.chips(after=tok)
    dh = _matmul(dproj, wc, "nn", F32, 512, 1024, 3840, "in_proj_dx", dep=tok)
    grad_x, g_nm = _rms_bwd(dh, x0, norm_mix, dx1, "rms_mix_bwd")
    me = 4 * xi + 2 * yi + ci
    nm8 = lax.dynamic_update_slice(jnp.zeros((8, 8, 128), F32), g_nm[0].reshape(1, 8, 128), (me, 0, 0))
    nm_arr, nm_sems, tok = _split_call("norm_mix_start", {"nm": nm8}, start=_all8_plan("nm"))
    red = rs_a.result(after=tok)
    big = {"w_mlp1": red[1], "w_mlp2": red[2], "w_branch_sc": red[3], "w_branch_ssm": red[4], "w_out": red[5]}

    given = dict(norm_mix=norm_mix, w_in=w_in, b_gate=b_gate, sc_conv_w=sc_conv_w, ssm_conv_w=ssm_conv_w, ssm_conv_b=ssm_conv_b, dt_bias=dt_bias, A_log=A_log, D_skip=D_skip, ssm_norm_w=ssm_norm_w, w_branch_sc=w_branch_sc, w_branch_ssm=w_branch_ssm, w_out=w_out, norm_mlp=norm_mlp, w_mlp1=w_mlp1, w_mlp2=w_mlp2, norm_final=norm_final,
                 m_norm_mix=m_norm_mix, m_w_in=m_w_in, m_b_gate=m_b_gate, m_sc_conv_w=m_sc_conv_w, m_ssm_conv_w=m_ssm_conv_w, m_ssm_conv_b=m_ssm_conv_b, m_dt_bias=m_dt_bias, m_A_log=m_A_log, m_D_skip=m_D_skip, m_ssm_norm_w=m_ssm_norm_w, m_w_branch_sc=m_w_branch_sc, m_w_branch_ssm=m_w_branch_ssm, m_w_out=m_w_out, m_norm_mlp=m_norm_mlp, m_w_mlp1=m_w_mlp1, m_w_mlp2=m_w_mlp2, m_norm_final=m_norm_final,
                 v_norm_mix=v_norm_mix, v_w_in=v_w_in, v_b_gate=v_b_gate, v_sc_conv_w=v_sc_conv_w, v_ssm_conv_w=v_ssm_conv_w, v_ssm_conv_b=v_ssm_conv_b, v_dt_bias=v_dt_bias, v_A_log=v_A_log, v_D_skip=v_D_skip, v_ssm_norm_w=v_ssm_norm_w, v_w_branch_sc=v_w_branch_sc, v_w_branch_ssm=v_w_branch_ssm, v_w_out=v_w_out, v_norm_mlp=v_norm_mlp, v_w_mlp1=v_w_mlp1, v_w_mlp2=v_w_mlp2, v_norm_final=v_norm_final)
    order = ["norm_mix", "w_in", "b_gate", "sc_conv_w", "ssm_conv_w", "ssm_conv_b", "dt_bias", "A_log", "D_skip",
             "ssm_norm_w", "w_branch_sc", "w_branch_ssm", "w_out", "norm_mlp", "w_mlp1", "w_mlp2", "norm_final"]
    grad, delta, new_m, new_v = {}, {}, {}, {}
    for n in big:
        delta[n], new_m[n], new_v[n], grad[n] = _adamw(given[n], big[n], given["m_" + n], given["v_" + n],
                                                       "adamw_" + n, copy_g=True)
    big["w_in"] = None
    grad_small = {n: gs[n].reshape(given[n].shape) for n in order
                  if n not in big and n not in ("sc_conv_w", "ssm_conv_w", "norm_mix")}
    grad_small["sc_conv_w"] = lax.dynamic_slice(gs["sc_conv_w"].reshape(3, D), (0, 256 * s), (3, 256))
    grad_small["ssm_conv_w"] = lax.dynamic_slice(gs["ssm_conv_w"].reshape(4, XBC), (0, 1024 * s), (4, 1024))
    table = [(n, int(grad_small[n].size)) for n in grad_small]
    rows = 136
    pk = lambda d: _pack(d, table, rows)
    ds_, ms_, vs_ = _adamw(pk({n: given[n] for n in grad_small}), pk(grad_small), pk({n: given["m_" + n] for n in grad_small}),
                           pk({n: given["v_" + n] for n in grad_small}), "adamw_small", tr=rows)
    ds_, ms_, vs_ = _unpack(ds_, table), _unpack(ms_, table), _unpack(vs_, table)
    for n in grad_small:
        shp = given[n].shape
        grad[n] = grad_small[n]
        delta[n], new_m[n], new_v[n] = ds_[n].reshape(shp), ms_[n].reshape(shp), vs_[n].reshape(shp)

    done = [new_v[n] for n in ("w_mlp1", "w_mlp2", "w_branch_sc", "w_branch_ssm", "w_out")] + [vs_["b_gate"]]
    tok = rs_b.share(after=done)
    gp = rs_b.result(after=tok)[0]
    gwt = _from_piece(gp, s)
    wt_args = (w_in.T, gwt, m_w_in.T, v_w_in.T)
    head = _adamw(*wt_args, "adamw_w_in", tr=256, nblk=W_SHARD // 256)
    dt_, mt_, vt_ = _adamw(*wt_args, "adamw_w_in_tail", tr=8, blk0=(W_SHARD // 256) * 32, nblk=1, into=head)
    grad["w_in"], delta["w_in"], new_m["w_in"], new_v["w_in"] = gwt.T, dt_.T, mt_.T, vt_.T
    nm_arr, _, _ = _split_call("norm_mix_wait", nm_arr, wait=_all8_plan("nm"), wait_sems=nm_sems, after=tok)
    g8 = _sum8(nm_arr["nm"], "norm_mix_sum")
    r8 = lambda a: a.reshape(8, 128)
    d8, m8, v8 = _adamw(r8(norm_mix), g8, r8(m_norm_mix), r8(v_norm_mix), "adamw_norm_mix", tr=8)
    grad["norm_mix"], delta["norm_mix"] = g8.reshape(D), d8.reshape(D)
    new_m["norm_mix"], new_v["norm_mix"] = m8.reshape(D), v8.reshape(D)

    loss = gs["loss"].reshape(())
    return (loss, grad_x.reshape(1, L, D), *[grad[n] for n in order], *[delta[n] for n in order],
            *[new_m[n] for n in order], *[new_v[n] for n in order])
```

```python
import functools

import jax
import jax.numpy as jnp
from jax import lax
from jax.experimental import pallas as pl
from jax.experimental.pallas import tpu as pltpu

F32 = jnp.float32
BF16 = jnp.bfloat16
MESH = pl.DeviceIdType.MESH
HBM = pltpu.HBM

D = 1024
INNER = 2048
HD = 64
NH = 32
NG = 8
NS = 128
Q = 128
GPS = 2
XBC = 4096
DFF = 4096
EPS = 1e-6
W_SHARD = 2824
NCW = 11520
PIECE = 3072
PMAIN = 2816
C_Z, C_XBC, C_GATE, C_DT = 3072, 5120, 9216, 11264
SMALL_ROWS = 256
VMEM_LIMIT = 56 * 1024 * 1024

ADAM_LR, ADAM_B1, ADAM_B2, ADAM_EPS, ADAM_WD, ADAM_STEP = 0.001, 0.9, 0.999, 1e-08, 0.01, 10


def _cp(sem=None, vmem=VMEM_LIMIT):
    return pltpu.CompilerParams(dimension_semantics=sem, vmem_limit_bytes=vmem)


def _sigmoid(v):
    return 1.0 / (1.0 + jnp.exp(-v))


_DIMS = {"nn": (((1,), (0,)), ((), ())), "nt": (((1,), (1,)), ((), ())), "tn": (((0,), (0,)), ((), ()))}


def _matmul(a, b, mode, out_dtype, tm, tn, tk, name, epi=None, extra=None, n_outer=False, dep=None):
    if mode == "tn":
        K, M = a.shape
    else:
        M, K = a.shape
    N = b.shape[0] if mode == "nt" else b.shape[1]
    tm, tn, tk = min(tm, M), min(tn, N), min(tk, K)
    assert M % tm == 0 and N % tn == 0 and K % tk == 0, (name, M, N, K, tm, tn, tk)
    nm, nn, nk = M // tm, N // tn, K // tk
    dims = _DIMS[mode]

    def ij(p0, p1):
        return (p1, p0) if n_outer else (p0, p1)

    if mode == "tn":
        a_spec = pl.BlockSpec((tk, tm), lambda p0, p1, k: (k, ij(p0, p1)[0]))
    else:
        a_spec = pl.BlockSpec((tm, tk), lambda p0, p1, k: (ij(p0, p1)[0], k))
    if mode == "nt":
        b_spec = pl.BlockSpec((tn, tk), lambda p0, p1, k: (ij(p0, p1)[1], k))
    else:
        b_spec = pl.BlockSpec((tk, tn), lambda p0, p1, k: (k, ij(p0, p1)[1]))
    o_spec = pl.BlockSpec((tm, tn), lambda p0, p1, k: ij(p0, p1))
    in_specs = [a_spec, b_spec]
    args = [a, b]
    if epi in ("res", "drelu"):
        in_specs.append(o_spec)
        args.append(extra)
    if dep is not None:
        in_specs.append(pl.BlockSpec(memory_space=pl.ANY))
        args.append(dep)
    n_in = len(args)
    if epi == "relu2":
        out_shape = (jax.ShapeDtypeStruct((M, N), out_dtype), jax.ShapeDtypeStruct((M, N), BF16))
        out_specs = (o_spec, o_spec)
    else:
        out_shape = jax.ShapeDtypeStruct((M, N), out_dtype)
        out_specs = o_spec

    def kern(*refs):
        a_ref, b_ref = refs[0], refs[1]
        e_ref = refs[2] if epi in ("res", "drelu") else None
        acc = refs[-1]
        outs = refs[n_in:-1] if nk > 1 else refs[n_in:]
        k = pl.program_id(2)

        def product():
            return lax.dot_general(a_ref[...].astype(BF16), b_ref[...].astype(BF16), dims, preferred_element_type=F32)

        def finish(r):
            if epi is None:
                outs[0][...] = r.astype(out_dtype)
            elif epi == "res":
                outs[0][...] = (r + e_ref[...]).astype(out_dtype)
            elif epi == "relu2":
                outs[0][...] = r.astype(out_dtype)
                t = jnp.maximum(r, 0.0)
                outs[1][...] = (t * t).astype(BF16)
            else:
                outs[0][...] = (r * (2.0 * jnp.maximum(e_ref[...].astype(F32), 0.0))).astype(out_dtype)

        if nk == 1:
            finish(product())
        else:
            @pl.when(k == 0)
            def _():
                acc[...] = jnp.zeros_like(acc)

            acc[...] += product()

            @pl.when(k == nk - 1)
            def _():
                finish(acc[...])

    grid = (nn, nm, nk) if n_outer else (nm, nn, nk)
    return pl.pallas_call(
        kern, grid=grid, in_specs=in_specs, out_specs=out_specs, out_shape=out_shape,
        scratch_shapes=[pltpu.VMEM((tm, tn), F32)] if nk > 1 else [], name=name,
        compiler_params=_cp(("parallel", "parallel", "arbitrary")),
    )(*args)


def _rms_fwd(x, w, name, tl=256, dep=None):
    L = x.shape[0]

    def kern(x_ref, w_ref, *rest):
        o_ref = rest[-1]
        xv = x_ref[...]
        r = lax.rsqrt(jnp.mean(xv * xv, axis=-1, keepdims=True) + EPS)
        o_ref[...] = ((xv * r) * w_ref[...]).astype(BF16)

    row = pl.BlockSpec((tl, D), lambda i: (i, 0))
    deps = [] if dep is None else [dep]
    return pl.pallas_call(
        kern, grid=(L // tl,),
        in_specs=[row, pl.BlockSpec((1, D), lambda i: (0, 0))] + [pl.BlockSpec(memory_space=pl.ANY)] * len(deps),
        out_specs=row, out_shape=jax.ShapeDtypeStruct((L, D), BF16), name=name, compiler_params=_cp(("parallel",)),
    )(x, w.reshape(1, D), *deps)


def _rms_bwd(dy, x, w, res, name, tl=256, dep=None):
    L = x.shape[0]
    deps = [] if dep is None else [dep]

    def kern(dy_ref, x_ref, w_ref, res_ref, *rest):
        dx_ref, gw_ref = rest[-2:]
        @pl.when(pl.program_id(0) == 0)
        def _():
            gw_ref[...] = jnp.zeros_like(gw_ref)

        xv = x_ref[...]
        dyv = dy_ref[...]
        r = lax.rsqrt(jnp.mean(xv * xv, axis=-1, keepdims=True) + EPS)
        xn = xv * r
        gw_ref[...] += jnp.broadcast_to(jnp.sum(dyv * xn, axis=0, keepdims=True), (8, D))
        dxn = dyv * w_ref[...]
        dx_ref[...] = res_ref[...] + r * (dxn - xn * jnp.mean(dxn * xn, axis=-1, keepdims=True))

    row = pl.BlockSpec((tl, D), lambda i: (i, 0))
    return pl.pallas_call(
        kern, grid=(L // tl,),
        in_specs=[row, row, pl.BlockSpec((1, D), lambda i: (0, 0)), row] + [pl.BlockSpec(memory_space=pl.ANY)] * len(deps),
        out_specs=(row, pl.BlockSpec((8, D), lambda i: (0, 0))),
        out_shape=(jax.ShapeDtypeStruct((L, D), F32), jax.ShapeDtypeStruct((8, D), F32)),
        name=name, compiler_params=_cp(("arbitrary",)),
    )(dy, x, w.reshape(1, D), res, *deps)


def _final(x2, w, tgt, tl=256):
    L = x2.shape[0]

    def kern(x_ref, w_ref, t_ref, dx_ref, gw_ref, loss_ref):
        @pl.when(pl.program_id(0) == 0)
        def _():
            gw_ref[...] = jnp.zeros_like(gw_ref)
            loss_ref[...] = jnp.zeros_like(loss_ref)

        xv = x_ref[...]
        r = lax.rsqrt(jnp.mean(xv * xv, axis=-1, keepdims=True) + EPS)
        xn = xv * r
        e = xn * w_ref[...] - t_ref[...]
        per_tok = jnp.mean(e * e, axis=-1, keepdims=True)
        loss_ref[...] += 0.5 * jnp.sum(per_tok)
        dyv = e * (1.0 / D)
        gw_ref[...] += jnp.broadcast_to(jnp.sum(dyv * xn, axis=0, keepdims=True), (8, D))
        dxn = dyv * w_ref[...]
        dx_ref[...] = r * (dxn - xn * jnp.mean(dxn * xn, axis=-1, keepdims=True))

    row = pl.BlockSpec((tl, D), lambda i: (i, 0))
    return pl.pallas_call(
        kern, grid=(L // tl,), in_specs=[row, pl.BlockSpec((1, D), lambda i: (0, 0)), row],
        out_specs=(row, pl.BlockSpec((8, D), lambda i: (0, 0)), pl.BlockSpec((8, 128), lambda i: (0, 0))),
        out_shape=(jax.ShapeDtypeStruct((L, D), F32), jax.ShapeDtypeStruct((8, D), F32),
                   jax.ShapeDtypeStruct((8, 128), F32)),
        name="final_norm_loss", compiler_params=_cp(("arbitrary",)),
    )(x2, w.reshape(1, D), tgt)


def _down(v, k):
    if k == 0:
        return v
    t = lax.broadcasted_iota(jnp.int32, v.shape, 0)
    return jnp.where(t >= k, pltpu.roll(v, k, axis=0), 0.0)


def _up(v, k):
    if k == 0:
        return v
    n = v.shape[0]
    t = lax.broadcasted_iota(jnp.int32, v.shape, 0)
    return jnp.where(t < n - k, pltpu.roll(v, n - k, axis=0), 0.0)


TW = 256


def _sc_fwd(proj, cw):
    L = proj.shape[0]
    nb = D // TW

    def kern(b_ref, c_ref, x_ref, w_ref, o_ref):
        u = c_ref[...].astype(F32) * x_ref[...].astype(F32)
        w = w_ref[...]
        cv = w[0:1] * _down(u, 2) + w[1:2] * _down(u, 1) + w[2:3] * u
        o_ref[...] = (b_ref[...].astype(F32) * cv).astype(BF16)

    col = lambda off: pl.BlockSpec((L, TW), lambda j: (0, off + j))
    return pl.pallas_call(
        kern, grid=(nb,), in_specs=[col(0), col(nb), col(2 * nb), pl.BlockSpec((8, TW), lambda j: (0, j))],
        out_specs=pl.BlockSpec((L, TW), lambda j: (0, j)), out_shape=jax.ShapeDtypeStruct((L, D), BF16),
        name="sc_fwd", compiler_params=_cp(("parallel",)),
    )(proj, proj, proj, cw)


def _sc_bwd(dya, proj, cw, dproj):
    L = proj.shape[0]
    nb = D // TW

    def kern(d_ref, b_ref, c_ref, x_ref, w_ref, _, dp_ref, gw_ref, keep):
        sec = pl.program_id(1)

        @pl.when(sec == 0)
        def _():
            cs, xs, dyv = c_ref[...].astype(F32), x_ref[...].astype(F32), d_ref[...]
            w = w_ref[...]
            u = cs * xs
            u1, u2 = _down(u, 1), _down(u, 2)
            cv = w[0:1] * u2 + w[1:2] * u1 + w[2:3] * u
            dcv = dyv * b_ref[...].astype(F32)
            du = w[2:3] * dcv + w[1:2] * _up(dcv, 1) + w[0:1] * _up(dcv, 2)
            g0 = jnp.sum(dcv * u2, axis=0, keepdims=True)
            g1 = jnp.sum(dcv * u1, axis=0, keepdims=True)
            g2 = jnp.sum(dcv * u, axis=0, keepdims=True)
            row = lax.broadcasted_iota(jnp.int32, (8, TW), 0)
            gw_ref[...] = jnp.where(row == 0, g0, jnp.where(row == 1, g1, jnp.where(row == 2, g2, 0.0)))
            dp_ref[...] = (dyv * cv).astype(BF16)
            keep[0] = (du * xs).astype(BF16)
            keep[1] = (du * cs).astype(BF16)

        @pl.when(sec > 0)
        def _():
            dp_ref[...] = keep[sec - 1]

    col = lambda off: pl.BlockSpec((L, TW), lambda j, s: (0, off + j))
    return pl.pallas_call(
        kern, grid=(nb, 3),
        in_specs=[col(0), col(0), col(nb), col(2 * nb), pl.BlockSpec((8, TW), lambda j, s: (0, j)),
                  pl.BlockSpec(memory_space=pl.ANY)],
        out_specs=(pl.BlockSpec((L, TW), lambda j, s: (0, s * nb + j)), pl.BlockSpec((8, TW), lambda j, s: (0, j))),
        out_shape=(jax.ShapeDtypeStruct(dproj.shape, BF16), jax.ShapeDtypeStruct((8, D), F32)),
        scratch_shapes=[pltpu.VMEM((2, L, TW), BF16)],
        input_output_aliases={5: 0}, name="sc_bwd", compiler_params=_cp(("parallel", "arbitrary")),
    )(dya, proj, proj, proj, cw, dproj)


def _ssm_conv_fwd(proj, cw4):
    L = proj.shape[0]
    off = C_XBC // TW

    def kern(r_ref, w_ref, o_ref):
        raw = r_ref[...].astype(F32)
        w = w_ref[...]
        c4 = w[0:1] * _down(raw, 3) + w[1:2] * _down(raw, 2) + w[2:3] * _down(raw, 1) + w[3:4] * raw + w[4:5]
        o_ref[...] = c4 * _sigmoid(c4)

    return pl.pallas_call(
        kern, grid=(XBC // TW,),
        in_specs=[pl.BlockSpec((L, TW), lambda j: (0, off + j)), pl.BlockSpec((8, TW), lambda j: (0, j))],
        out_specs=pl.BlockSpec((L, TW), lambda j: (0, j)), out_shape=jax.ShapeDtypeStruct((L, XBC), F32),
        name="ssm_conv_fwd", compiler_params=_cp(("parallel",)),
    )(proj, cw4)


def _ssm_conv_bwd(dx, proj, cw4, dproj, col0, name):
    L, width = dx.shape
    off_p = (C_XBC + col0) // TW
    off_w = col0 // TW

    def kern(d_ref, r_ref, w_ref, _, dp_ref, gw_ref):
        raw = r_ref[...].astype(F32)
        w = w_ref[...]
        r1, r2, r3 = _down(raw, 1), _down(raw, 2), _down(raw, 3)
        c4 = w[0:1] * r3 + w[1:2] * r2 + w[2:3] * r1 + w[3:4] * raw + w[4:5]
        sg = _sigmoid(c4)
        dc4 = d_ref[...] * (sg * (1.0 + c4 * (1.0 - sg)))
        draw = w[3:4] * dc4 + w[2:3] * _up(dc4, 1) + w[1:2] * _up(dc4, 2) + w[0:1] * _up(dc4, 3)
        dp_ref[...] = draw.astype(BF16)
        gs = [jnp.sum(dc4 * r3, axis=0, keepdims=True), jnp.sum(dc4 * r2, axis=0, keepdims=True),
              jnp.sum(dc4 * r1, axis=0, keepdims=True), jnp.sum(dc4 * raw, axis=0, keepdims=True),
              jnp.sum(dc4, axis=0, keepdims=True)]
        row = lax.broadcasted_iota(jnp.int32, (8, TW), 0)
        acc = jnp.zeros((8, TW), F32)
        for k, gk in enumerate(gs):
            acc = jnp.where(row == k, gk, acc)
        gw_ref[...] = acc

    return pl.pallas_call(
        kern, grid=(width // TW,),
        in_specs=[pl.BlockSpec((L, TW), lambda j: (0, j)), pl.BlockSpec((L, TW), lambda j: (0, off_p + j)),
                  pl.BlockSpec((8, TW), lambda j: (0, off_w + j)), pl.BlockSpec(memory_space=pl.ANY)],
        out_specs=(pl.BlockSpec((L, TW), lambda j: (0, off_p + j)), pl.BlockSpec((8, TW), lambda j: (0, j))),
        out_shape=(jax.ShapeDtypeStruct(dproj.shape, BF16), jax.ShapeDtypeStruct((8, width), F32)),
        input_output_aliases={3: 0}, name=name, compiler_params=_cp(("arbitrary",)),
    )(dx, proj, cw4, dproj)


def _split3(v):
    h1 = v.astype(BF16)
    r1 = v - h1.astype(F32)
    h2 = r1.astype(BF16)
    h3 = (r1 - h2.astype(F32)).astype(BF16)
    return h1, h2, h3


def _dot01(m01, v, dims=_DIMS["nn"], m_left=True, terms=3):
    out = None
    for part in _split3(v)[:terms]:
        ops = (m01, part) if m_left else (part, m01)
        t = lax.dot_general(ops[0], ops[1], dims, preferred_element_type=F32)
        out = t if out is None else out + t
    return out


def _bdot(a, b, mode="nn"):
    return lax.dot_general(a.astype(BF16), b.astype(BF16), _DIMS[mode], preferred_element_type=F32)


def _softplus(v):
    return jnp.maximum(v, 0.0) + jnp.log1p(jnp.exp(-jnp.abs(v)))


def _dt_prep(proj, vec):
    L = proj.shape[0]

    def kern(p_ref, v_ref, dt_ref, cs_ref, sg_ref):
        v = v_ref[...]
        pre = p_ref[:, 0:128] + v[0:1]
        dt = _softplus(pre)
        da = dt * (-jnp.exp(v[1:2]))
        ii = lax.broadcasted_iota(jnp.int32, (Q, Q), 0)
        jj = lax.broadcasted_iota(jnp.int32, (Q, Q), 1)
        ltri = (jj <= ii).astype(BF16)
        lane = lax.broadcasted_iota(jnp.int32, (Q, 128), 1)
        for val, ref in ((dt, dt_ref), (_dot01(ltri, da), cs_ref), (_sigmoid(pre), sg_ref)):
            for g in range(NG):
                moved = val if g == 0 else pltpu.roll(val, 128 - 4 * g, axis=1)
                ref[g] = jnp.where(lane < 4, moved, 0.0)

    blk = pl.BlockSpec((NG, Q, 128), lambda c: (0, c, 0))
    return pl.pallas_call(
        kern, grid=(L // Q,),
        in_specs=[pl.BlockSpec((Q, 256), lambda c: (c, 0)), pl.BlockSpec((8, 128), lambda c: (0, 0))],
        out_specs=(blk, blk, blk),
        out_shape=(jax.ShapeDtypeStruct((NG, L, 128), F32),) * 3,
        name="dt_prep", compiler_params=_cp(("parallel",)),
    )(proj, vec)


def _head_masks():
    lane = lax.broadcasted_iota(jnp.int32, (1, 4 * HD), 1)
    return [((lane >= HD * j) & (lane < HD * (j + 1))) for j in range(4)]


def _expand4(v4, masks):
    R = v4.shape[0]
    out = jnp.zeros((R, 4 * HD), F32)
    for j in range(4):
        out = jnp.where(masks[j], jnp.broadcast_to(v4[:, j:j + 1], (R, 4 * HD)), out)
    return out


def _decay_matrix(cs_col, tri):
    colb = jnp.broadcast_to(cs_col, (Q, Q))
    return jnp.exp(jnp.where(tri, colb - colb.T, -jnp.inf))


def _ssd_fwd(xbc, dt4, cs4, vecg):
    L = xbc.shape[0]
    nc = L // Q

    def kern(x_ref, b_ref, c_ref, dt_ref, cs_ref, v_ref, y_ref, s_ref, S):
        c = pl.program_id(1)

        @pl.when(c == 0)
        def _():
            S[...] = jnp.zeros_like(S)

        masks = _head_masks()
        ii = lax.broadcasted_iota(jnp.int32, (Q, Q), 0)
        jj = lax.broadcasted_iota(jnp.int32, (Q, Q), 1)
        tri = jj <= ii
        for gi in range(GPS):
            xs, ns = slice(256 * gi, 256 * (gi + 1)), slice(NS * gi, NS * (gi + 1))
            dt4v, cs4v = dt_ref[gi], cs_ref[gi]
            dt_b, cs_b = _expand4(dt4v, masks), _expand4(cs4v, masks)
            d_b = _expand4(v_ref[gi], masks)[1:2]
            cs_last = cs_b[Q - 1:Q, :]
            x4, bm, cm = x_ref[:, xs], b_ref[:, ns], c_ref[:, ns]
            xdt = x4 * dt_b
            gm = _bdot(cm, bm, "nt")
            s4 = S[gi]
            s_ref[gi, 0] = s4
            y = _bdot(cm, s4) * jnp.exp(cs_b) + d_b * x4
            m_all = jnp.concatenate([(gm * _decay_matrix(cs4v[:, j:j + 1], tri)).astype(BF16) for j in range(4)], axis=0)
            yd = _bdot(m_all, xdt)
            for j in range(4):
                y = y + jnp.where(masks[j], yd[Q * j:Q * (j + 1)], 0.0)
            y_ref[:, xs] = y
            S[gi] = jnp.exp(cs_last) * s4 + _bdot(bm, xdt * jnp.exp(cs_last - cs_b), "tn")

    sc = pl.BlockSpec((GPS, Q, 128), lambda g, c: (g, c, 0))
    bw = NS * GPS
    return pl.pallas_call(
        kern, grid=(NG // GPS, nc),
        in_specs=[pl.BlockSpec((Q, 256 * GPS), lambda g, c: (c, g)),
                  pl.BlockSpec((Q, bw), lambda g, c: (c, INNER // bw + g)),
                  pl.BlockSpec((Q, bw), lambda g, c: (c, (INNER + NG * NS) // bw + g)),
                  sc, sc, pl.BlockSpec((GPS, 8, 128), lambda g, c: (g, 0, 0))],
        out_specs=(pl.BlockSpec((Q, 256 * GPS), lambda g, c: (c, g)),
                   pl.BlockSpec((GPS, 1, NS, 256), lambda g, c: (g, c, 0, 0))),
        out_shape=(jax.ShapeDtypeStruct((L, INNER), F32), jax.ShapeDtypeStruct((NG, nc, NS, 256), F32)),
        scratch_shapes=[pltpu.VMEM((GPS, NS, 256), F32)], name="ssd_fwd",
        compiler_params=_cp(("parallel", "arbitrary")),
    )(xbc, xbc, xbc, dt4, cs4, vecg)


def _ssd_bwd(xbc, dt4, cs4, sg4, vecg, s_all, dy):
    L = xbc.shape[0]
    nc = L // Q

    def kern(x_ref, b_ref, c_ref, dt_ref, cs_ref, sg_ref, v_ref, s_ref, dy_ref,
             dx_ref, db_ref, dc_ref, ddt_ref, st_ref, dS):
        cc = pl.program_id(1)

        @pl.when(cc == 0)
        def _():
            dS[...] = jnp.zeros_like(dS)
            st_ref[...] = jnp.zeros_like(st_ref)

        masks = _head_masks()
        ii = lax.broadcasted_iota(jnp.int32, (Q, Q), 0)
        jj = lax.broadcasted_iota(jnp.int32, (Q, Q), 1)
        tri = jj <= ii
        utri = (jj >= ii).astype(BF16)
        li = lax.broadcasted_iota(jnp.int32, (4 * HD, 4 * HD), 0)
        lj = lax.broadcasted_iota(jnp.int32, (4 * HD, 4 * HD), 1)
        eblk = ((li // HD) == (lj // HD)).astype(BF16)
        lane128 = lax.broadcasted_iota(jnp.int32, (Q, 128), 1)

        for gi in range(GPS):
            xs, ns = slice(256 * gi, 256 * (gi + 1)), slice(NS * gi, NS * (gi + 1))
            dt4v, cs4v, sg4v = dt_ref[gi], cs_ref[gi], sg_ref[gi]
            dt_b, cs_b = _expand4(dt4v, masks), _expand4(cs4v, masks)
            vv = _expand4(v_ref[gi], masks)
            a_b = -jnp.exp(vv[0:1])
            d_b = vv[1:2]
            a4 = -jnp.exp(v_ref[gi][0:1, :])
            cs_last = cs_b[Q - 1:Q, :]
            ecs = jnp.exp(cs_b)
            decay = jnp.exp(cs_last - cs_b)
            elast = jnp.exp(cs_last)
            x4, bm, cm, dyv = x_ref[:, xs], b_ref[:, ns], c_ref[:, ns], dy_ref[:, xs]
            s4 = s_ref[gi, 0]
            dsn = dS[gi]
            xdt = x4 * dt_b
            gm = _bdot(cm, bm, "nt")
            gmt = gm.T
            dye = dyv * ecs
            yoff = ecs * _bdot(cm, s4)
            t4 = _bdot(bm, dsn) * decay
            lms, mhs, mhts = [], [], []
            for j in range(4):
                colb = jnp.broadcast_to(cs4v[:, j:j + 1], (Q, Q))
                seg = colb - colb.T
                lms.append(jnp.exp(jnp.where(tri, seg, -jnp.inf)))
                mhs.append(gm * lms[j])
                mhts.append(gmt * jnp.exp(jnp.where(jj >= ii, -seg, -jnp.inf)))
            m_all = jnp.concatenate([m.astype(BF16) for m in mhs], axis=0)
            dy_m = jnp.concatenate([jnp.where(masks[j], dyv, 0.0).astype(BF16) for j in range(4)], axis=0)
            x_m = jnp.concatenate([jnp.where(masks[j], xdt, 0.0).astype(BF16) for j in range(4)], axis=0)
            dxdt = t4 + _bdot(m_all, dy_m, "tn")
            dm_all = _bdot(dy_m, xdt, "nt")
            dmt_all = _bdot(x_m, dyv, "nt")
            dg = jnp.zeros((Q, Q), F32)
            rc = jnp.zeros((Q, 4 * HD), F32)
            for j in range(4):
                dmh = dm_all[Q * j:Q * (j + 1)]
                dg = dg + dmh * lms[j]
                rs = (jnp.sum(dmh * mhs[j], axis=1, keepdims=True)
                      - jnp.sum(dmt_all[Q * j:Q * (j + 1)] * mhts[j], axis=1, keepdims=True))
                rc = jnp.where(masks[j], jnp.broadcast_to(rs, (Q, 4 * HD)), rc)
            xt = xdt * t4
            tail = jnp.sum(xt, axis=0, keepdims=True) + elast * jnp.sum(s4 * dsn, axis=0, keepdims=True)
            gd_raw = jnp.sum(dyv * x4, axis=0, keepdims=True)
            stacked = jnp.concatenate([dyv * yoff - xt, dxdt * x4, jnp.broadcast_to(tail, (8, 4 * HD)),
                                       jnp.broadcast_to(gd_raw, (8, 4 * HD))], axis=0)
            seg = _dot01(eblk, stacked, m_left=False, terms=2)
            da_b = seg[0:Q] + rc
            dda_b = _dot01(utri, da_b, terms=2) + seg[2 * Q:2 * Q + 1]
            ddt_b = dda_b * a_b + seg[Q:2 * Q]
            gd_b = seg[2 * Q + 8:2 * Q + 9]
            ddt4 = jnp.zeros((Q, 128), F32)
            dda4 = jnp.zeros((Q, 128), F32)
            for j in range(4):
                ddt4 = jnp.where(lane128 == j, jnp.broadcast_to(ddt_b[:, HD * j:HD * j + 1], (Q, 128)), ddt4)
                dda4 = jnp.where(lane128 == j, jnp.broadcast_to(dda_b[:, HD * j:HD * j + 1], (Q, 128)), dda4)
            ddt_ref[gi] = ddt4 * sg4v
            ga = jnp.sum(dda4 * dt4v * a4, axis=0, keepdims=True)
            gd = jnp.zeros((1, 128), F32)
            for j in range(4):
                gd = jnp.where(lane128[0:1] == j, jnp.broadcast_to(gd_b[:, HD * j:HD * j + 1], (1, 128)), gd)
            row = lax.broadcasted_iota(jnp.int32, (8, 128), 0)
            st_ref[gi] += jnp.where(row == 0, ga, jnp.where(row == 1, gd, 0.0))
            dx_ref[:, xs] = d_b * dyv + dxdt * dt_b
            dc_ref[:, ns] = _bdot(dg, bm) + _bdot(dye, s4, "nt")
            db_ref[:, ns] = _bdot(dg, cm, "tn") + _bdot(xdt * decay, dsn, "nt")
            dS[gi] = elast * dsn + _bdot(cm, dye, "tn")

    rv = lambda c: nc - 1 - c
    sc = pl.BlockSpec((GPS, Q, 128), lambda g, c: (g, rv(c), 0))
    bw = NS * GPS
    return pl.pallas_call(
        kern, grid=(NG // GPS, nc),
        in_specs=[pl.BlockSpec((Q, 256 * GPS), lambda g, c: (rv(c), g)),
                  pl.BlockSpec((Q, bw), lambda g, c: (rv(c), INNER // bw + g)),
                  pl.BlockSpec((Q, bw), lambda g, c: (rv(c), (INNER + NG * NS) // bw + g)),
                  sc, sc, sc, pl.BlockSpec((GPS, 8, 128), lambda g, c: (g, 0, 0)),
                  pl.BlockSpec((GPS, 1, NS, 256), lambda g, c: (g, rv(c), 0, 0)),
                  pl.BlockSpec((Q, 256 * GPS), lambda g, c: (rv(c), g))],
        out_specs=(pl.BlockSpec((Q, 256 * GPS), lambda g, c: (rv(c), g)),
                   pl.BlockSpec((Q, bw), lambda g, c: (rv(c), g)),
                   pl.BlockSpec((Q, bw), lambda g, c: (rv(c), g)),
                   pl.BlockSpec((GPS, Q, 128), lambda g, c: (g, rv(c), 0)),
                   pl.BlockSpec((GPS, 8, 128), lambda g, c: (g, 0, 0))),
        out_shape=(jax.ShapeDtypeStruct((L, INNER), F32), jax.ShapeDtypeStruct((L, NG * NS), F32),
                   jax.ShapeDtypeStruct((L, NG * NS), F32), jax.ShapeDtypeStruct((NG, L, 128), F32),
                   jax.ShapeDtypeStruct((NG, 8, 128), F32)),
        scratch_shapes=[pltpu.VMEM((GPS, NS, 256), F32)], name="ssd_bwd",
        compiler_params=_cp(("parallel", "arbitrary")),
    )(xbc, xbc, xbc, dt4, cs4, sg4, vecg, s_all, dy)


def _dt_bwd(ddt, dproj, tl=256):
    L = ddt.shape[1]

    def kern(d_ref, _, dp_ref, gs_ref):
        @pl.when(pl.program_id(0) == 0)
        def _():
            gs_ref[...] = jnp.zeros_like(gs_ref)

        d = d_ref[0]
        for g in range(1, NG):
            d = d + pltpu.roll(d_ref[g], 4 * g, axis=1)
        gs_ref[...] += jnp.broadcast_to(jnp.sum(d, axis=0, keepdims=True), (8, 128))
        dp_ref[...] = jnp.concatenate([d, jnp.zeros_like(d)], axis=1).astype(BF16)

    return pl.pallas_call(
        kern, grid=(L // tl,),
        in_specs=[pl.BlockSpec((NG, tl, 128), lambda i: (0, i, 0)), pl.BlockSpec(memory_space=pl.ANY)],
        out_specs=(pl.BlockSpec((tl, 256), lambda i: (i, C_DT // 256)), pl.BlockSpec((8, 128), lambda i: (0, 0))),
        out_shape=(jax.ShapeDtypeStruct(dproj.shape, BF16), jax.ShapeDtypeStruct((8, 128), F32)),
        input_output_aliases={1: 0}, name="dt_bwd", compiler_params=_cp(("arbitrary",)),
    )(ddt, dproj)


GW = INNER // NG


def _gnorm_fwd(y, proj, w, tl=256):
    L = y.shape[0]
    zoff = C_Z // 1024

    def kern(y_ref, z_ref, w_ref, o_ref):
        z = z_ref[...].astype(F32)
        yz = y_ref[...] * (z * _sigmoid(z))
        wv = w_ref[...]
        for k in range(1024 // GW):
            sl = slice(GW * k, GW * (k + 1))
            v = yz[:, sl]
            rg = lax.rsqrt(jnp.mean(v * v, axis=-1, keepdims=True) + EPS)
            o_ref[:, sl] = ((v * rg) * wv[:, sl]).astype(BF16)

    blk = pl.BlockSpec((tl, 1024), lambda i, j: (i, j))
    return pl.pallas_call(
        kern, grid=(L // tl, 2),
        in_specs=[blk, pl.BlockSpec((tl, 1024), lambda i, j: (i, zoff + j)), pl.BlockSpec((1, 1024), lambda i, j: (0, j))],
        out_specs=blk, out_shape=jax.ShapeDtypeStruct((L, INNER), BF16), name="gnorm_fwd",
        compiler_params=_cp(("parallel", "parallel")),
    )(y, proj, w.reshape(1, INNER))


def _gnorm_bwd(dyb, y, proj, w, dproj, tl=256):
    L = y.shape[0]
    zoff = C_Z // 1024

    def kern(d_ref, y_ref, z_ref, w_ref, _, dy_ref, dp_ref, gw_ref):
        @pl.when(pl.program_id(1) == 0)
        def _():
            gw_ref[...] = jnp.zeros_like(gw_ref)

        z = z_ref[...].astype(F32)
        sg = _sigmoid(z)
        sz = z * sg
        yv = y_ref[...]
        yz = yv * sz
        dv = d_ref[...]
        wv = w_ref[...]
        for k in range(1024 // GW):
            sl = slice(GW * k, GW * (k + 1))
            v = yz[:, sl]
            rg = lax.rsqrt(jnp.mean(v * v, axis=-1, keepdims=True) + EPS)
            vn = v * rg
            dk = dv[:, sl]
            gw_ref[:, sl] += jnp.broadcast_to(jnp.sum(dk * vn, axis=0, keepdims=True), (8, GW))
            dvn = dk * wv[:, sl]
            dyz = rg * (dvn - vn * jnp.mean(dvn * vn, axis=-1, keepdims=True))
            dy_ref[:, sl] = dyz * sz[:, sl]
            dp_ref[:, sl] = (dyz * yv[:, sl] * (sg[:, sl] * (1.0 + z[:, sl] * (1.0 - sg[:, sl])))).astype(BF16)

    blk = pl.BlockSpec((tl, 1024), lambda j, i: (i, j))
    zblk = pl.BlockSpec((tl, 1024), lambda j, i: (i, zoff + j))
    return pl.pallas_call(
        kern, grid=(2, L // tl),
        in_specs=[blk, blk, zblk, pl.BlockSpec((1, 1024), lambda j, i: (0, j)), pl.BlockSpec(memory_space=pl.ANY)],
        out_specs=(blk, zblk, pl.BlockSpec((8, 1024), lambda j, i: (0, j))),
        out_shape=(jax.ShapeDtypeStruct((L, INNER), F32), jax.ShapeDtypeStruct(dproj.shape, BF16),
                   jax.ShapeDtypeStruct((8, INNER), F32)),
        input_output_aliases={4: 1}, name="gnorm_bwd", compiler_params=_cp(("parallel", "arbitrary")),
    )(dyb, y, proj, w.reshape(1, INNER), dproj)


def _merge_fwd(proj, bg, br_a, br_b, tl=256):
    L = proj.shape[0]
    goff = C_GATE // 1024

    def kern(g1_ref, g2_ref, b1_ref, b2_ref, a_ref, b_ref, o_ref):
        g1 = _sigmoid(g1_ref[...].astype(F32) + b1_ref[...])
        g2 = _sigmoid(g2_ref[...].astype(F32) + b2_ref[...])
        o_ref[...] = (g1 * a_ref[...] + g2 * b_ref[...]).astype(BF16)

    row = pl.BlockSpec((tl, 1024), lambda i: (i, 0))
    bg2 = bg.reshape(1, 2 * D)
    return pl.pallas_call(
        kern, grid=(L // tl,),
        in_specs=[pl.BlockSpec((tl, 1024), lambda i: (i, goff)), pl.BlockSpec((tl, 1024), lambda i: (i, goff + 1)),
                  pl.BlockSpec((1, 1024), lambda i: (0, 0)), pl.BlockSpec((1, 1024), lambda i: (0, 1)), row, row],
        out_specs=row, out_shape=jax.ShapeDtypeStruct((L, D), BF16), name="merge_fwd",
        compiler_params=_cp(("parallel",)),
    )(proj, proj, bg2, bg2, br_a, br_b)


def _merge_bwd(dm, proj, bg, br_a, br_b, dproj, tl=256):
    L = proj.shape[0]
    goff = C_GATE // 1024

    def kern(dm_ref, g_ref, b_ref, a_ref, bb_ref, _, dbr_ref, dp_ref, gb_ref):
        j = pl.program_id(0)

        @pl.when(pl.program_id(1) == 0)
        def _():
            gb_ref[...] = jnp.zeros_like(gb_ref)

        g = _sigmoid(g_ref[...].astype(F32) + b_ref[...])
        br = jnp.where(j == 0, a_ref[...], bb_ref[...])
        dmv = dm_ref[...]
        dbr_ref[0] = (dmv * g).astype(BF16)
        dgate = dmv * br * g * (1.0 - g)
        gb_ref[...] += jnp.broadcast_to(jnp.sum(dgate, axis=0, keepdims=True), (8, 1024))
        dp_ref[...] = dgate.astype(BF16)

    row = pl.BlockSpec((tl, 1024), lambda j, i: (i, 0))
    gblk = pl.BlockSpec((tl, 1024), lambda j, i: (i, goff + j))
    return pl.pallas_call(
        kern, grid=(2, L // tl),
        in_specs=[row, gblk, pl.BlockSpec((1, 1024), lambda j, i: (0, j)), row, row, pl.BlockSpec(memory_space=pl.ANY)],
        out_specs=(pl.BlockSpec((1, tl, 1024), lambda j, i: (j, i, 0)), gblk, pl.BlockSpec((8, 1024), lambda j, i: (0, j))),
        out_shape=(jax.ShapeDtypeStruct((2, L, D), BF16), jax.ShapeDtypeStruct(dproj.shape, BF16),
                   jax.ShapeDtypeStruct((8, 2 * D), F32)),
        input_output_aliases={5: 1}, name="merge_bwd", compiler_params=_cp(("parallel", "arbitrary")),
    )(dm, proj, bg.reshape(1, 2 * D), br_a, br_b, dproj)


def _coords():
    return lax.axis_index("x"), lax.axis_index("y"), lax.axis_index("c")


def _other_chips(sk):
    xk, yk = sk // 2, sk % 2
    return [((1 - xk, yk), 2 * (1 - xk) + yk), ((xk, 1 - yk), 2 * xk + 1 - yk), ((1 - xk, 1 - yk), 2 * (1 - xk) + 1 - yk)]


def _rows(start, size):
    assert size % 128 == 0
    return pl.ds(pl.multiple_of(start, 128), size)


def _per_chip(fn):
    x, y, _ = _coords()
    s = 2 * x + y
    for sk in range(4):
        pl.when(s == sk)(functools.partial(fn, sk))


XTRA = PIECE - PMAIN


def _place(shard, full_shape, block, index_map, idx, name, blk0=0, nblk=None, dep=None):
    in_block = block[-2:]
    if nblk is None:
        nblk = shard.shape[0] // in_block[0]

    def kern(idx_ref, s_ref, *rest):
        o_ref = rest[-1]
        o_ref[...] = s_ref[...].astype(BF16).reshape(o_ref.shape)

    grid_spec = pltpu.PrefetchScalarGridSpec(
        num_scalar_prefetch=1, grid=(nblk,),
        in_specs=[pl.BlockSpec(in_block, lambda i, idx_ref: (blk0 + i, 0))] + ([_ANY] if dep is not None else []),
        out_specs=pl.BlockSpec(block, index_map))
    args = (idx, shard) + ((dep,) if dep is not None else ())
    return pl.pallas_call(kern, grid_spec=grid_spec, out_shape=jax.ShapeDtypeStruct(full_shape, BF16), name=name,
                          compiler_params=_cp(("arbitrary",)))(*args)


_SEM = pl.BlockSpec(memory_space=pltpu.SEMAPHORE)
_EFFECT = pltpu.SideEffectType.DATAFLOW_SIDE_EFFECTING


_ANY = pl.BlockSpec(memory_space=pl.ANY)


def _tie(v, dep, name):
    def body(v_ref, dep_ref, o_ref):
        del v_ref, dep_ref, o_ref

    return pl.pallas_call(body, out_shape=jax.ShapeDtypeStruct(v.shape, v.dtype), in_specs=[_ANY, _ANY],
                          out_specs=_ANY, input_output_aliases={0: 0}, name=name)(v, dep)


def _split_call(name, arrays, start=None, wait=None, wait_sems=None, after=None):
    keys = list(arrays)
    n = len(keys)
    n_start = start.n if start is not None else 0
    afters = [] if after is None else (list(after) if isinstance(after, (list, tuple)) else [after])

    def body(*refs):
        pos = n
        if wait is not None:
            wss, wrs = refs[pos], refs[pos + 1]
            pos += 2
        pos += len(afters)
        if start is not None:
            nss, nrs = refs[pos], refs[pos + 1]
            pos += 2
        R = dict(zip(keys, refs[pos:pos + n]))
        token = refs[pos + n]
        x, y, c = _coords()

        def desc(src, dst, dev, ss, rs, k):
            return pltpu.make_async_remote_copy(src_ref=src, dst_ref=dst, send_sem=ss.at[k], recv_sem=rs.at[k],
                                                device_id=dev, device_id_type=MESH)

        def run(sk):
            if wait is not None:
                for k, (snd, land) in enumerate(wait.copies(sk, R)):
                    if snd is not None:
                        desc(snd[0], snd[1], snd[2], wss, wrs, k).wait_send()
                    if land is not None:
                        desc(land, land, (x, y, c), wss, wrs, k).wait_recv()
            if start is not None:
                for k, (snd, land) in enumerate(start.copies(sk, R)):
                    if snd is not None:
                        desc(snd[0], snd[1], snd[2], nss, nrs, k).start()

        _per_chip(run)
        token[...] = jnp.zeros_like(token)

    hbm = pl.BlockSpec(memory_space=HBM)
    vals = [arrays[k] for k in keys]
    ins, in_specs = list(vals), [hbm] * n
    if wait is not None:
        ins += list(wait_sems)
        in_specs += [_SEM, _SEM]
    ins += afters
    in_specs += [pl.BlockSpec(memory_space=pl.ANY)] * len(afters)
    out_shape, out_specs = [], []
    if start is not None:
        out_shape += [pltpu.SemaphoreType.DMA((n_start,)), pltpu.SemaphoreType.DMA((n_start,))]
        out_specs += [_SEM, _SEM]
    first = len(out_shape)
    out_shape += [jax.ShapeDtypeStruct(v.shape, v.dtype) for v in vals] + [jax.ShapeDtypeStruct((8, 128), F32)]
    out_specs += [hbm] * n + [pl.BlockSpec(memory_space=pltpu.VMEM)]
    res = pl.pallas_call(
        body, out_shape=tuple(out_shape), in_specs=in_specs, out_specs=tuple(out_specs),
        input_output_aliases={i: first + i for i in range(n)}, name=name,
        compiler_params=pltpu.CompilerParams(has_side_effects=_EFFECT),
    )(*ins)
    sems = (res[0], res[1]) if start is not None else None
    return dict(zip(keys, res[first:first + n])), sems, res[-1]


class _Plan:
    def __init__(self, n, copies):
        self.n, self.copies = n, copies


_HM, _HX = PMAIN // 2, XTRA // 2
WAVE0 = 768
WAVES = ((0, WAVE0), (WAVE0, _HM - WAVE0))
_WIN = {
    "wq0": (True, "wct", lambda r, sc, hc: r.at[_rows(PMAIN * sc + _HM * hc + WAVES[0][0], WAVES[0][1]), :]),
    "wq1": (True, "wct", lambda r, sc, hc: r.at[_rows(PMAIN * sc + _HM * hc + WAVES[1][0], WAVES[1][1]), :]),
    "xt": (True, "xt", lambda r, sc, hc: r.at[sc, _rows(_HX * hc, _HX), :]),
    "w1": (True, "w1", lambda r, sc, hc: r.at[_rows(512 * hc, 512), pl.ds(1024 * sc, 1024)]),
    "w2": (True, "w2", lambda r, sc, hc: r.at[_rows(1024 * sc + 512 * hc, 512), :]),
    "wa": (True, "wa", lambda r, sc, hc: r.at[_rows(256 * sc + 128 * hc, 128), :]),
    "wb": (True, "wb", lambda r, sc, hc: r.at[_rows(512 * sc + 256 * hc, 256), :]),
    "wo": (True, "wo", lambda r, sc, hc: r.at[_rows(256 * sc + 128 * hc, 128), :]),
    "cw": (False, "cw", lambda r, sc, hc: r.at[sc]),
}


def _ag_chips_plan(keys):
    def copies(sk, R):
        _, _, c = _coords()
        out = []
        for key in keys:
            _, arr, win = _WIN[key]
            for (px, py), ps in _other_chips(sk):
                w = win(R[arr], sk, c)
                out.append(((w, w, (px, py, c)), win(R[arr], ps, c)))
        return out
    return _Plan(3 * len(keys), copies)


def _ag_sibling_plan(keys):
    keys = [k for k in keys if _WIN[k][0]]

    def copies(sk, R):
        x, y, c = _coords()
        out = []
        for key in keys:
            _, arr, win = _WIN[key]
            for _, ps in _other_chips(sk):
                w = win(R[arr], ps, c)
                out.append(((w, w, (x, y, 1 - c)), win(R[arr], ps, 1 - c)))
        return out
    return _Plan(3 * len(keys), copies)


def _in_proj_wave(h, wct, wave, proj=None, tm=1024):
    L = h.shape[0]
    tm = min(tm, L)
    off, size = WAVES[wave]
    start = lambda j: pl.multiple_of(_HM * j + off, 128)

    def kern(h_ref, w_ref, *rest):
        o_ref = rest[-1]
        o_ref[...] = lax.dot_general(h_ref[...], w_ref[...], _DIMS["nt"], preferred_element_type=F32).astype(BF16)

    in_specs = [pl.BlockSpec((tm, D), lambda j, i: (i, 0)),
                pl.BlockSpec((pl.Element(size), pl.Element(D)), lambda j, i: (start(j), 0))]
    args, aliases = [h, wct], {}
    if proj is not None:
        in_specs.append(pl.BlockSpec(memory_space=pl.ANY))
        args.append(proj)
        aliases = {2: 0}
    return pl.pallas_call(
        kern, grid=(8, L // tm), in_specs=in_specs,
        out_specs=pl.BlockSpec((pl.Element(tm), pl.Element(size)), lambda j, i: (i * tm, start(j))),
        out_shape=jax.ShapeDtypeStruct((L, NCW), BF16), input_output_aliases=aliases,
        name="in_proj_wave%d" % wave, compiler_params=_cp(("parallel", "parallel")),
    )(*args)


def _fix_wct(wct, xt):
    nb = PMAIN // XTRA

    def kern(w_ref, x_ref, o_ref):
        k = pl.program_id(0)
        xv = x_ref[0]
        o_ref[...] = jnp.where(k < 3, (w_ref[...].astype(F32) + xv.astype(F32)).astype(BF16), xv)

    blk = pl.BlockSpec((XTRA, D), lambda k: (nb * (k + 1), 0))
    rblk = pl.BlockSpec((XTRA, D), lambda k: (jnp.where(k < 3, nb * (k + 1), 0), 0))
    return pl.pallas_call(
        kern, grid=(4,), in_specs=[rblk, pl.BlockSpec((1, XTRA, D), lambda k: (k, 0, 0))], out_specs=blk,
        out_shape=jax.ShapeDtypeStruct(wct.shape, BF16), input_output_aliases={0: 0}, name="fix_wct",
        compiler_params=_cp(("arbitrary",)),
    )(wct, xt)


_HP = PIECE // 2
_GWIN = [
    lambda r, sc, hc: r.at[_rows(PMAIN * sc + _HP * hc, _HP), :],
    lambda r, sc, hc: r.at[_rows(512 * hc, 512), pl.ds(1024 * sc, 1024)],
    lambda r, sc, hc: r.at[_rows(1024 * sc + 512 * hc, 512), :],
    lambda r, sc, hc: r.at[_rows(256 * sc + 128 * hc, 128), :],
    lambda r, sc, hc: r.at[_rows(512 * sc + 256 * hc, 256), :],
    lambda r, sc, hc: r.at[_rows(256 * sc + 128 * hc, 128), :],
]
HALF_SHAPES = [(PIECE // 2, D), (512, 1024), (512, 1024), (128, 1024), (256, 1024), (128, 1024)]


def _rs_sibling_plan(ts):
    def copies(sk, R):
        x, y, c = _coords()
        out = []
        for t in ts:
            for sc in range(4):
                land = R["ra%d" % t].at[sc]
                out.append(((_GWIN[t](R["g%d" % t], sc, 1 - c), land, (x, y, 1 - c)), land))
        return out
    return _Plan(4 * len(ts), copies)


def _rs_chips_plan(ts):
    def copies(sk, R):
        _, _, c = _coords()
        out = []
        for t in ts:
            for j, ((px, py), ps) in enumerate(_other_chips(sk)):
                land = R["rb%d" % t].at[j]
                out.append(((R["hb%d" % t].at[ps], land, (px, py, c)), land))
        return out
    return _Plan(3 * len(ts), copies)


def _rs_share_plan(ts):
    def copies(sk, R):
        x, y, c = _coords()
        out = []
        for t in ts:
            rows = HALF_SHAPES[t][0]
            mine = R["f%d" % t].at[_rows(rows * c, rows), :]
            out.append(((mine, mine, (x, y, 1 - c)), R["f%d" % t].at[_rows(rows * (1 - c), rows), :]))
        return out
    return _Plan(len(ts), copies)


def _half_tiling(t):
    rows, cols = HALF_SHAPES[t]
    if t == 0:
        return (256, cols), rows // 256, lambda i: (i, 0)
    if t == 1:
        return (rows, 256), cols // 256, lambda i: (0, i)
    return (rows, cols), 1, lambda i: (0, 0)


def _window_block(t, sc, hc, i):
    if t == 0:
        return (PMAIN // 256) * sc + (PIECE // 512) * hc + i, 0
    if t == 1:
        return hc, 4 * sc + i
    return 2 * sc + hc, 0


def _chip_sum(g, ra, t, idx, name):
    rows, cols = HALF_SHAPES[t]
    blk, nblk, inner = _half_tiling(t)

    def kern(idx_ref, g_ref, r_ref, hb_ref, hf_ref):
        v = g_ref[...].astype(F32) + r_ref[0].astype(F32)
        hb_ref[0] = v.astype(BF16)

        @pl.when(pl.program_id(1) == idx_ref[0])
        def _():
            hf_ref[...] = v

    gmap = lambda i, sc, idx_ref: _window_block(t, sc, idx_ref[1], i)
    omap = lambda i, sc, idx_ref: (sc,) + inner(i)
    grid_spec = pltpu.PrefetchScalarGridSpec(
        num_scalar_prefetch=1, grid=(nblk, 4),
        in_specs=[pl.BlockSpec(blk, gmap), pl.BlockSpec((1,) + blk, omap)],
        out_specs=(pl.BlockSpec((1,) + blk, omap), pl.BlockSpec(blk, lambda i, sc, idx_ref: inner(i))))
    return pl.pallas_call(
        kern, grid_spec=grid_spec,
        out_shape=(jax.ShapeDtypeStruct((4, rows, cols), BF16), jax.ShapeDtypeStruct((rows, cols), F32)),
        name=name, compiler_params=_cp(("parallel", "arbitrary")),
    )(idx, g, ra)


def _final_sum(hf, rb, t, idx, name):
    rows, cols = HALF_SHAPES[t]
    blk, nblk, inner = _half_tiling(t)
    nbr = rows // blk[0]

    def kern(idx_ref, h_ref, r_ref, o_ref):
        o_ref[...] = ((h_ref[...] + r_ref[0].astype(F32)) + r_ref[1].astype(F32)) + r_ref[2].astype(F32)

    def omap(i, idx_ref):
        r, cidx = inner(i)
        return nbr * idx_ref[1] + r, cidx

    grid_spec = pltpu.PrefetchScalarGridSpec(
        num_scalar_prefetch=1, grid=(nblk,),
        in_specs=[pl.BlockSpec(blk, lambda i, idx_ref: inner(i)),
                  pl.BlockSpec((3,) + blk, lambda i, idx_ref: (0,) + inner(i))],
        out_specs=pl.BlockSpec(blk, omap))
    return pl.pallas_call(
        kern, grid_spec=grid_spec, out_shape=jax.ShapeDtypeStruct((2 * rows, cols), F32),
        name=name, compiler_params=_cp(("parallel",)),
    )(idx, hf, rb)


class _ReduceScatter:
    def __init__(self, ts, grads, idx, tag):
        self.ts, self.idx, self.tag = ts, idx, tag
        arr = {}
        for t in ts:
            arr["g%d" % t] = grads[t]
            arr["ra%d" % t] = lax.empty((4,) + HALF_SHAPES[t], BF16)
        self.plan = _rs_sibling_plan(ts)
        self.arr, self.sems, self.token = _split_call("rs_sibling_start_" + tag, arr, start=self.plan)

    def chips(self, after):
        arr, _, _ = _split_call("rs_sibling_wait_" + self.tag, self.arr, wait=self.plan, wait_sems=self.sems, after=after)
        brr, self.hf = {}, {}
        for t in self.ts:
            hb, self.hf[t] = _chip_sum(arr["g%d" % t], arr["ra%d" % t], t, self.idx, "chip_sum_%d" % t)
            brr["hb%d" % t] = hb
            brr["rb%d" % t] = lax.empty((3,) + HALF_SHAPES[t], BF16)
        self.plan = _rs_chips_plan(self.ts)
        self.arr, self.sems, self.token = _split_call("rs_chips_start_" + self.tag, brr, start=self.plan)
        return self.token

    def share(self, after):
        brr, _, _ = _split_call("rs_chips_wait_" + self.tag, self.arr, wait=self.plan, wait_sems=self.sems, after=after)
        frr = {"f%d" % t: _final_sum(self.hf[t], brr["rb%d" % t], t, self.idx, "final_sum_%d" % t) for t in self.ts}
        self.plan = _rs_share_plan(self.ts)
        self.arr, self.sems, self.token = _split_call("rs_share_start_" + self.tag, frr, start=self.plan)
        return self.token

    def result(self, after):
        frr, _, _ = _split_call("rs_share_wait_" + self.tag, self.arr, wait=self.plan, wait_sems=self.sems, after=after)
        return {t: frr["f%d" % t] for t in self.ts}


def _all8_plan(key):
    def copies(sk, R):
        x, y, c = _coords()
        own = R[key].at[4 * x + 2 * y + c]
        out = []
        for k in range(1, 8):
            dev = ((1 - x) if (k >> 2) & 1 else x, (1 - y) if (k >> 1) & 1 else y, (1 - c) if k & 1 else c)
            out.append(((own, own, dev), R[key].at[4 * dev[0] + 2 * dev[1] + dev[2]]))
        return out
    return _Plan(7, copies)


def _small_all_gather(v):
    def body(v_ref, o_ref, send_sems, recv_sems, loc_sem):
        x, y, c = _coords()
        me = 4 * x + 2 * y + c
        lc = pltpu.make_async_copy(v_ref, o_ref.at[me], loc_sem)
        lc.start()
        cps = []
        for k in range(1, 8):
            fx, fy, fc = (k >> 2) & 1, (k >> 1) & 1, k & 1
            dev = ((1 - x) if fx else x, (1 - y) if fy else y, (1 - c) if fc else c)
            cp = pltpu.make_async_remote_copy(src_ref=v_ref, dst_ref=o_ref.at[me], send_sem=send_sems.at[k - 1],
                                              recv_sem=recv_sems.at[k - 1], device_id=dev, device_id_type=MESH)
            cp.start()
            cps.append((cp, 4 * dev[0] + 2 * dev[1] + dev[2]))
        for k, (cp, frm) in enumerate(cps):
            got = o_ref.at[frm]
            pltpu.make_async_remote_copy(src_ref=got, dst_ref=got, send_sem=send_sems.at[k], recv_sem=recv_sems.at[k],
                                         device_id=(x, y, c), device_id_type=MESH).wait_recv()
        for cp, _ in cps:
            cp.wait_send()
        lc.wait()

    hbm = pl.BlockSpec(memory_space=HBM)
    return pl.pallas_call(
        body, out_shape=jax.ShapeDtypeStruct((8,) + v.shape, F32), in_specs=[hbm], out_specs=hbm,
        scratch_shapes=[pltpu.SemaphoreType.DMA((7,)), pltpu.SemaphoreType.DMA((7,)), pltpu.SemaphoreType.DMA(())],
        name="small_all_gather", compiler_params=pltpu.CompilerParams(has_side_effects=True),
    )(v)


def _sum8(v, name="small_sum"):
    def kern(v_ref, o_ref):
        acc = v_ref[0]
        for k in range(1, 8):
            acc = acc + v_ref[k]
        o_ref[...] = acc

    return pl.pallas_call(kern, out_shape=jax.ShapeDtypeStruct(v.shape[1:], F32), name=name)(v)


def _adamw(w, g, m, v, name, tr=128, blk0=0, nblk=None, into=None, copy_g=False):
    R, C = w.shape
    tr = min(tr, R)
    if nblk is None:
        assert R % tr == 0 and blk0 == 0
        nblk = R // tr
    n_out = 4 if copy_g else 3

    def kern(*refs):
        w_ref, g_ref, m_ref, v_ref = refs[:4]
        d_ref, mo_ref, vo_ref = refs[-n_out:][:3]
        gv = g_ref[...]
        mn = ADAM_B1 * m_ref[...] + (1.0 - ADAM_B1) * gv
        vn = ADAM_B2 * v_ref[...] + (1.0 - ADAM_B2) * (gv * gv)
        m_hat = mn / (1.0 - ADAM_B1 ** ADAM_STEP)
        v_hat = vn / (1.0 - ADAM_B2 ** ADAM_STEP)
        d_ref[...] = -ADAM_LR * (m_hat / (jnp.sqrt(v_hat) + ADAM_EPS) + ADAM_WD * w_ref[...])
        mo_ref[...] = mn
        vo_ref[...] = vn
        if copy_g:
            refs[-1][...] = gv

    blk = pl.BlockSpec((tr, C), lambda i: (blk0 + i, 0))
    sd = jax.ShapeDtypeStruct((R, C), F32)
    in_specs, args, aliases = [blk] * 4, [w, g, m, v], {}
    if into is not None:
        in_specs += [pl.BlockSpec(memory_space=pl.ANY)] * 3
        args += list(into)
        aliases = {4: 0, 5: 1, 6: 2}
    return pl.pallas_call(kern, grid=(nblk,), in_specs=in_specs, out_specs=(blk,) * n_out, out_shape=(sd,) * n_out,
                          input_output_aliases=aliases, name=name, compiler_params=_cp(("parallel",)))(*args)


def _to_piece(wt, s):
    z = lambda n: jnp.zeros((n, D), wt.dtype)
    pads = [functools.partial(lambda k, w: jnp.pad(w, ((8 * k, PIECE - W_SHARD - 8 * k), (0, 0))), k) for k in range(3)]
    last = lambda w: jnp.concatenate([z(24), w[:744], w[776:], w[744:776], z(PIECE - 24 - W_SHARD)], axis=0)
    return lax.switch(s, pads + [last], wt)


def _from_piece(p, s):
    cuts = [functools.partial(lambda k, q: q[8 * k:8 * k + W_SHARD], k) for k in range(3)]
    last = lambda q: jnp.concatenate([q[24:768], q[2816:2848], q[768:2816]], axis=0)
    return lax.switch(s, cuts + [last], p)


_SMALL = [("b_gate", 2048), ("ssm_conv_b", 4096), ("dt_bias", 32), ("A_log", 32), ("D_skip", 32),
          ("ssm_norm_w", 2048), ("norm_mlp", 1024), ("norm_final", 1024), ("sc_conv_w", 3072), ("ssm_conv_w", 16384),
          ("loss", 1)]


def _pack(vals, table, rows):
    parts = []
    for name, n in table:
        v = vals[name].reshape(-1).astype(F32)
        pad = (-n) % 128
        parts.append(jnp.pad(v, (0, pad)) if pad else v)
    flat = jnp.concatenate(parts)
    return jnp.pad(flat, (0, rows * 128 - flat.shape[0])).reshape(rows, 128)


def _unpack(arr, table):
    flat = arr.reshape(-1)
    out, off = {}, 0
    for name, n in table:
        out[name] = flat[off:off + n]
        off += n + ((-n) % 128)
    return out


def kernel(x, norm_mix, w_in, b_gate, sc_conv_w, ssm_conv_w, ssm_conv_b, dt_bias, A_log, D_skip, ssm_norm_w, w_branch_sc, w_branch_ssm, w_out, norm_mlp, w_mlp1, w_mlp2, norm_final, loss_target, m_norm_mix, m_w_in, m_b_gate, m_sc_conv_w, m_ssm_conv_w, m_ssm_conv_b, m_dt_bias, m_A_log, m_D_skip, m_ssm_norm_w, m_w_branch_sc, m_w_branch_ssm, m_w_out, m_norm_mlp, m_w_mlp1, m_w_mlp2, m_norm_final, v_norm_mix, v_w_in, v_b_gate, v_sc_conv_w, v_ssm_conv_w, v_ssm_conv_b, v_dt_bias, v_A_log, v_D_skip, v_ssm_norm_w, v_w_branch_sc, v_w_branch_ssm, v_w_out, v_norm_mlp, v_w_mlp1, v_w_mlp2, v_norm_final):
    L = x.shape[1]
    nc = L // Q
    xi, yi, ci = lax.axis_index("x"), lax.axis_index("y"), lax.axis_index("c")
    s = 2 * xi + yi
    idx = jnp.stack([s, ci]).astype(jnp.int32)
    x0 = x.reshape(L, D)
    tgt = loss_target.reshape(L, D)

    piece = _to_piece(w_in.T, s)
    nb = PMAIN // XTRA
    wct0 = _place(piece, (NCW, D), (XTRA, D), lambda i, r: (nb * r[0] + i, 0), idx, "place_wct", nblk=nb)
    xt0 = _place(piece, (4, XTRA, D), (1, XTRA, D), lambda i, r: (r[0], 0, 0), idx, "place_xt", blk0=nb, nblk=1)
    cws = jnp.zeros((8, 1280), F32)
    cws = cws.at[0:3, 0:256].set(sc_conv_w).at[0:4, 256:1280].set(ssm_conv_w)
    cw0 = lax.dynamic_update_slice(jnp.zeros((4, 8, 1280), F32), cws[None], (s, 0, 0))
    win_keys, win2_keys, mid_keys, end_keys = ["xt", "cw", "wq0"], ["wq1"], ["wa", "wb", "wo", "w1"], ["w2"]
    gw, sems_w, tok = _split_call("ag_win_start", {"wct": wct0, "xt": xt0, "cw": cw0}, start=_ag_chips_plan(win_keys))
    g2, sems_w2, tok = _split_call("ag_win2_start", {"wct": gw["wct"]}, start=_ag_chips_plan(win2_keys), after=tok)
    gw["wct"] = g2["wct"]
    wa0 = _place(w_branch_sc, (D, D), (256, 1024), lambda i, r: (r[0], 0), idx, "place_wa", dep=tok)
    wb0 = _place(w_branch_ssm, (INNER, D), (512, 1024), lambda i, r: (r[0], 0), idx, "place_wb", dep=tok)
    wo0 = _place(w_out, (D, D), (256, 1024), lambda i, r: (r[0], 0), idx, "place_wo", dep=tok)
    w10 = _place(w_mlp1, (D, DFF), (256, 1024), lambda i, r: (i, r[0]), idx, "place_w1", dep=tok)
    gm, sems_m, tok = _split_call("ag_mid_start", {"wa": wa0, "wb": wb0, "wo": wo0, "w1": w10},
                                  start=_ag_chips_plan(mid_keys))
    w20 = _place(w_mlp2, (DFF, D), (256, 1024), lambda i, r: (4 * r[0] + i, 0), idx, "place_w2", dep=tok)
    ge, sems_e, tok = _split_call("ag_end_start", {"w2": w20}, start=_ag_chips_plan(end_keys))
    h = _rms_fwd(x0, norm_mix, "rms_mix", dep=tok)
    gw, sems_w, tok = _split_call("ag_win_pass", gw, wait=_ag_chips_plan(win_keys), wait_sems=sems_w,
                                  start=_ag_sibling_plan(win_keys), after=h)
    gw, _, _ = _split_call("ag_win_done", gw, wait=_ag_sibling_plan(win_keys), wait_sems=sems_w, after=tok)
    wc, cw_all = _fix_wct(gw["wct"], gw["xt"]), gw["cw"]
    sc_w_full = jnp.concatenate([cw_all[k, :, 0:256] for k in range(4)], axis=1)
    ssm_w_full = jnp.concatenate([cw_all[k, :, 256:1280] for k in range(4)], axis=1)
    cw4 = ssm_w_full.at[4].set(ssm_conv_b)
    vec = jnp.zeros((8, 128), F32).at[0, :NH].set(dt_bias).at[1, :NH].set(A_log)
    vecg = jnp.zeros((NG, 8, 128), F32).at[:, 0, :4].set(A_log.reshape(NG, 4)).at[:, 1, :4].set(D_skip.reshape(NG, 4))

    dtraw = _matmul(h, wc[C_DT:], "nt", F32, 512, 256, 1024, "in_proj_dt")
    proj = _in_proj_wave(h, wc, 0)
    g2, sems_w2, tok = _split_call("ag_win2_pass", {"wct": wc}, wait=_ag_chips_plan(win2_keys), wait_sems=sems_w2,
                                   start=_ag_sibling_plan(win2_keys), after=[proj, dtraw])
    g2, _, _ = _split_call("ag_win2_done", g2, wait=_ag_sibling_plan(win2_keys), wait_sems=sems_w2, after=tok)
    wc = g2["wct"]
    proj = _in_proj_wave(h, wc, 1, proj=proj)
    gm, sems_m, tok = _split_call("ag_mid_pass", gm, wait=_ag_chips_plan(mid_keys), wait_sems=sems_m,
                                  start=_ag_sibling_plan(mid_keys), after=proj)
    proj = _tie(proj, tok, "tie_proj")
    ya = _sc_fwd(proj, sc_w_full)
    xbc = _ssm_conv_fwd(proj, cw4)
    dt4, cs4, sg4 = _dt_prep(dtraw, vec)
    ge, sems_e, tok = _split_call("ag_end_pass", ge, wait=_ag_chips_plan(end_keys), wait_sems=sems_e,
                                  start=_ag_sibling_plan(end_keys), after=xbc)
    xbc = _tie(xbc, tok, "tie_xbc")
    y, s_all = _ssd_fwd(xbc, dt4, cs4, vecg)
    yb = _gnorm_fwd(y, proj, ssm_norm_w)
    gm, _, _ = _split_call("ag_mid_done", gm, wait=_ag_sibling_plan(mid_keys), wait_sems=sems_m, after=yb)
    ge, _, _ = _split_call("ag_end_done", ge, wait=_ag_sibling_plan(end_keys), wait_sems=sems_e, after=yb)
    wa, wb, wo, w1, w2 = gm["wa"], gm["wb"], gm["wo"], gm["w1"], ge["w2"]
    br_a = _matmul(ya, wa, "nn", F32, 1024, 1024, 1024, "branch_sc")
    br_b = _matmul(yb, wb, "nn", F32, 1024, 1024, 2048, "branch_ssm")
    merged = _merge_fwd(proj, b_gate, br_a, br_b)
    x1 = _matmul(merged, wo, "nn", F32, 1024, 1024, 1024, "out_proj", epi="res", extra=x0)
    h2 = _rms_fwd(x1, norm_mlp, "rms_mlp")
    a1, rl = _matmul(h2, w1, "nn", BF16, 1024, 1024, 1024, "mlp1", epi="relu2", n_outer=True)
    x2 = _matmul(rl, w2, "nn", F32, 512, 1024, 4096, "mlp2", epi="res", extra=x1)
    dx2, g_nf, loss8 = _final(x2, norm_final, tgt)

    da = _matmul(dx2, w2, "nt", BF16, 1024, 1024, 1024, "mlp2_dx", epi="drelu", extra=a1, n_outer=True)
    g_w2 = _matmul(rl, dx2, "tn", BF16, 1024, 1024, 2048, "mlp2_dw")
    g_w1 = _matmul(h2, da, "tn", BF16, 1024, 1024, 2048, "mlp1_dw")
    dh2 = _matmul(da, w1, "nt", F32, 512, 1024, 4096, "mlp1_dx")
    dx1, g_nmlp = _rms_bwd(dh2, x1, norm_mlp, dx2, "rms_mlp_bwd")
    dmerged = _matmul(dx1, wo, "nt", F32, 1024, 1024, 1024, "out_proj_dx")
    g_wo = _matmul(merged, dx1, "tn", BF16, 1024, 1024, 2048, "out_proj_dw")
    dproj = lax.empty((L, NCW), BF16)
    dbr, dproj, g_bg = _merge_bwd(dmerged, proj, b_gate, br_a, br_b, dproj)
    dya = _matmul(dbr[0], wa, "nt", F32, 1024, 1024, 1024, "branch_sc_dx")
    g_wa = _matmul(ya, dbr[0], "tn", BF16, 1024, 1024, 2048, "branch_sc_dw")
    dproj, g_scw = _sc_bwd(dya, proj, sc_w_full, dproj)
    dyb = _matmul(dbr[1], wb, "nt", F32, 1024, 1024, 1024, "branch_ssm_dx", n_outer=True)
    g_wb = _matmul(yb, dbr[1], "tn", BF16, 1024, 1024, 2048, "branch_ssm_dw")
    rs_a = _ReduceScatter([1, 2, 3, 4, 5], {1: g_w1, 2: g_w2, 3: g_wa, 4: g_wb, 5: g_wo}, idx, "a")
    dy, dproj, g_snw = _gnorm_bwd(_tie(dyb, rs_a.token, "tie_dyb"), y, proj, ssm_norm_w, dproj)
    tok = rs_a.chips(after=dy)
    dxs, dbm, dcm, ddt_g, st = _ssd_bwd(xbc, dt4, cs4, sg4, vecg, s_all, _tie(dy, tok, "tie_dy"))
    dproj, gx1 = _ssm_conv_bwd(dxs, proj, cw4, dproj, 0, "ssm_conv_bwd_x")
    dproj, gx2 = _ssm_conv_bwd(dbm, proj, cw4, dproj, INNER, "ssm_conv_bwd_b")
    dproj, gx3 = _ssm_conv_bwd(dcm, proj, cw4, dproj, INNER + NG * NS, "ssm_conv_bwd_c")
    g_cw4 = jnp.concatenate([gx1, gx2, gx3], axis=1)
    dproj, g_dtb = _dt_bwd(ddt_g, dproj)
    small = {"b_gate": g_bg[0], "ssm_conv_b": g_cw4[4], "dt_bias": g_dtb[0, :NH],
             "A_log": st[:, 0, :4], "D_skip": st[:, 1, :4], "ssm_norm_w": g_snw[0], "norm_mlp": g_nmlp[0],
             "norm_final": g_nf[0], "sc_conv_w": g_scw[0:3], "ssm_conv_w": g_cw4[0:4], "loss": loss8[0, 0:1]}
    small_sum = _sum8(_small_all_gather(_pack(small, _SMALL, SMALL_ROWS)))
    gs = _unpack(small_sum, _SMALL)
    g_wc = _matmul(dproj, h, "tn", BF16, 1280, 1024, 2048, "in_proj_dw", dep=small_sum)
    rs_b = _ReduceScatter([0], {0: g_wc}, idx, "b")
    tok = rs_a.share(after=rs_b.token)
    tok = rs_b.chips(after=tok)
    dh = _matmul(dproj, wc, "nn", F32, 512, 1024, 3840, "in_proj_dx", dep=tok)
    grad_x, g_nm = _rms_bwd(dh, x0, norm_mix, dx1, "rms_mix_bwd")
    me = 4 * xi + 2 * yi + ci
    nm8 = lax.dynamic_update_slice(jnp.zeros((8, 8, 128), F32), g_nm[0].reshape(1, 8, 128), (me, 0, 0))
    nm_arr, nm_sems, tok = _split_call("norm_mix_start", {"nm": nm8}, start=_all8_plan("nm"))
    red = rs_a.result(after=tok)
    big = {"w_mlp1": red[1], "w_mlp2": red[2], "w_branch_sc": red[3], "w_branch_ssm": red[4], "w_out": red[5]}

    given = dict(norm_mix=norm_mix, w_in=w_in, b_gate=b_gate, sc_conv_w=sc_conv_w, ssm_conv_w=ssm_conv_w, ssm_conv_b=ssm_conv_b, dt_bias=dt_bias, A_log=A_log, D_skip=D_skip, ssm_norm_w=ssm_norm_w, w_branch_sc=w_branch_sc, w_branch_ssm=w_branch_ssm, w_out=w_out, norm_mlp=norm_mlp, w_mlp1=w_mlp1, w_mlp2=w_mlp2, norm_final=norm_final,
                 m_norm_mix=m_norm_mix, m_w_in=m_w_in, m_b_gate=m_b_gate, m_sc_conv_w=m_sc_conv_w, m_ssm_conv_w=m_ssm_conv_w, m_ssm_conv_b=m_ssm_conv_b, m_dt_bias=m_dt_bias, m_A_log=m_A_log, m_D_skip=m_D_skip, m_ssm_norm_w=m_ssm_norm_w, m_w_branch_sc=m_w_branch_sc, m_w_branch_ssm=m_w_branch_ssm, m_w_out=m_w_out, m_norm_mlp=m_norm_mlp, m_w_mlp1=m_w_mlp1, m_w_mlp2=m_w_mlp2, m_norm_final=m_norm_final,
                 v_norm_mix=v_norm_mix, v_w_in=v_w_in, v_b_gate=v_b_gate, v_sc_conv_w=v_sc_conv_w, v_ssm_conv_w=v_ssm_conv_w, v_ssm_conv_b=v_ssm_conv_b, v_dt_bias=v_dt_bias, v_A_log=v_A_log, v_D_skip=v_D_skip, v_ssm_norm_w=v_ssm_norm_w, v_w_branch_sc=v_w_branch_sc, v_w_branch_ssm=v_w_branch_ssm, v_w_out=v_w_out, v_norm_mlp=v_norm_mlp, v_w_mlp1=v_w_mlp1, v_w_mlp2=v_w_mlp2, v_norm_final=v_norm_final)
    order = ["norm_mix", "w_in", "b_gate", "sc_conv_w", "ssm_conv_w", "ssm_conv_b", "dt_bias", "A_log", "D_skip",
             "ssm_norm_w", "w_branch_sc", "w_branch_ssm", "w_out", "norm_mlp", "w_mlp1", "w_mlp2", "norm_final"]
    grad, delta, new_m, new_v = {}, {}, {}, {}
    for n in big:
        delta[n], new_m[n], new_v[n], grad[n] = _adamw(given[n], big[n], given["m_" + n], given["v_" + n],
                                                       "adamw_" + n, copy_g=True)
    big["w_in"] = None
    grad_small = {n: gs[n].reshape(given[n].shape) for n in order
                  if n not in big and n not in ("sc_conv_w", "ssm_conv_w", "norm_mix")}
    grad_small["sc_conv_w"] = lax.dynamic_slice(gs["sc_conv_w"].reshape(3, D), (0, 256 * s), (3, 256))
    grad_small["ssm_conv_w"] = lax.dynamic_slice(gs["ssm_conv_w"].reshape(4, XBC), (0, 1024 * s), (4, 1024))
    table = [(n, int(grad_small[n].size)) for n in grad_small]
    rows = 136
    pk = lambda d: _pack(d, table, rows)
    ds_, ms_, vs_ = _adamw(pk({n: given[n] for n in grad_small}), pk(grad_small), pk({n: given["m_" + n] for n in grad_small}),
                           pk({n: given["v_" + n] for n in grad_small}), "adamw_small", tr=rows)
    ds_, ms_, vs_ = _unpack(ds_, table), _unpack(ms_, table), _unpack(vs_, table)
    for n in grad_small:
        shp = given[n].shape
        grad[n] = grad_small[n]
        delta[n], new_m[n], new_v[n] = ds_[n].reshape(shp), ms_[n].reshape(shp), vs_[n].reshape(shp)

    done = [new_v[n] for n in ("w_mlp1", "w_mlp2", "w_branch_sc", "w_branch_ssm", "w_out")] + [vs_["b_gate"]]
    tok = rs_b.share(after=done)
    gp = rs_b.result(after=tok)[0]
    gwt = _from_piece(gp, s)
    wt_args = (w_in.T, gwt, m_w_in.T, v_w_in.T)
    head = _adamw(*wt_args, "adamw_w_in", tr=256, nblk=W_SHARD // 256)
    dt_, mt_, vt_ = _adamw(*wt_args, "adamw_w_in_tail", tr=8, blk0=(W_SHARD // 256) * 32, nblk=1, into=head)
    grad["w_in"], delta["w_in"], new_m["w_in"], new_v["w_in"] = gwt.T, dt_.T, mt_.T, vt_.T
    nm_arr, _, _ = _split_call("norm_mix_wait", nm_arr, wait=_all8_plan("nm"), wait_sems=nm_sems, after=tok)
    g8 = _sum8(nm_arr["nm"], "norm_mix_sum")
    r8 = lambda a: a.reshape(8, 128)
    d8, m8, v8 = _adamw(r8(norm_mix), g8, r8(m_norm_mix), r8(v_norm_mix), "adamw_norm_mix", tr=8)
    grad["norm_mix"], delta["norm_mix"] = g8.reshape(D), d8.reshape(D)
    new_m["norm_mix"], new_v["norm_mix"] = m8.reshape(D), v8.reshape(D)

    loss = gs["loss"].reshape(())
    return (loss, grad_x.reshape(1, L, D), *[grad[n] for n in order], *[delta[n] for n in order],
            *[new_m[n] for n in order], *[new_v[n] for n in order])
```

```python
import functools

import jax
import jax.numpy as jnp
from jax import lax
from jax.experimental import pallas as pl
from jax.experimental.pallas import tpu as pltpu

F32 = jnp.float32
BF16 = jnp.bfloat16
MESH = pl.DeviceIdType.MESH
HBM = pltpu.HBM

D = 1024
INNER = 2048
HD = 64
NH = 32
NG = 8
NS = 128
Q = 128
GPS = 2
XBC = 4096
DFF = 4096
EPS = 1e-6
W_SHARD = 2824
NCW = 11520
PIECE = 3072
PMAIN = 2816
C_Z, C_XBC, C_GATE, C_DT = 3072, 5120, 9216, 11264
SMALL_ROWS = 256
VMEM_LIMIT = 56 * 1024 * 1024

ADAM_LR, ADAM_B1, ADAM_B2, ADAM_EPS, ADAM_WD, ADAM_STEP = 0.001, 0.9, 0.999, 1e-08, 0.01, 10


def _cp(sem=None, vmem=VMEM_LIMIT):
    return pltpu.CompilerParams(dimension_semantics=sem, vmem_limit_bytes=vmem)


def _sigmoid(v):
    return 1.0 / (1.0 + jnp.exp(-v))


_DIMS = {"nn": (((1,), (0,)), ((), ())), "nt": (((1,), (1,)), ((), ())), "tn": (((0,), (0,)), ((), ()))}


def _matmul(a, b, mode, out_dtype, tm, tn, tk, name, epi=None, extra=None, n_outer=False, dep=None):
    if mode == "tn":
        K, M = a.shape
    else:
        M, K = a.shape
    N = b.shape[0] if mode == "nt" else b.shape[1]
    tm, tn, tk = min(tm, M), min(tn, N), min(tk, K)
    assert M % tm == 0 and N % tn == 0 and K % tk == 0, (name, M, N, K, tm, tn, tk)
    nm, nn, nk = M // tm, N // tn, K // tk
    dims = _DIMS[mode]

    def ij(p0, p1):
        return (p1, p0) if n_outer else (p0, p1)

    if mode == "tn":
        a_spec = pl.BlockSpec((tk, tm), lambda p0, p1, k: (k, ij(p0, p1)[0]))
    else:
        a_spec = pl.BlockSpec((tm, tk), lambda p0, p1, k: (ij(p0, p1)[0], k))
    if mode == "nt":
        b_spec = pl.BlockSpec((tn, tk), lambda p0, p1, k: (ij(p0, p1)[1], k))
    else:
        b_spec = pl.BlockSpec((tk, tn), lambda p0, p1, k: (k, ij(p0, p1)[1]))
    o_spec = pl.BlockSpec((tm, tn), lambda p0, p1, k: ij(p0, p1))
    in_specs = [a_spec, b_spec]
    args = [a, b]
    if epi in ("res", "drelu"):
        in_specs.append(o_spec)
        args.append(extra)
    if dep is not None:
        in_specs.append(pl.BlockSpec(memory_space=pl.ANY))
        args.append(dep)
    n_in = len(args)
    if epi == "relu2":
        out_shape = (jax.ShapeDtypeStruct((M, N), out_dtype), jax.ShapeDtypeStruct((M, N), BF16))
        out_specs = (o_spec, o_spec)
    else:
        out_shape = jax.ShapeDtypeStruct((M, N), out_dtype)
        out_specs = o_spec

    def kern(*refs):
        a_ref, b_ref = refs[0], refs[1]
        e_ref = refs[2] if epi in ("res", "drelu") else None
        acc = refs[-1]
        outs = refs[n_in:-1] if nk > 1 else refs[n_in:]
        k = pl.program_id(2)

        def product():
            return lax.dot_general(a_ref[...].astype(BF16), b_ref[...].astype(BF16), dims, preferred_element_type=F32)

        def finish(r):
            if epi is None:
                outs[0][...] = r.astype(out_dtype)
            elif epi == "res":
                outs[0][...] = (r + e_ref[...]).astype(out_dtype)
            elif epi == "relu2":
                outs[0][...] = r.astype(out_dtype)
                t = jnp.maximum(r, 0.0)
                outs[1][...] = (t * t).astype(BF16)
            else:
                outs[0][...] = (r * (2.0 * jnp.maximum(e_ref[...].astype(F32), 0.0))).astype(out_dtype)

        if nk == 1:
            finish(product())
        else:
            @pl.when(k == 0)
            def _():
                acc[...] = jnp.zeros_like(acc)

            acc[...] += product()

            @pl.when(k == nk - 1)
            def _():
                finish(acc[...])

    grid = (nn, nm, nk) if n_outer else (nm, nn, nk)
    return pl.pallas_call(
        kern, grid=grid, in_specs=in_specs, out_specs=out_specs, out_shape=out_shape,
        scratch_shapes=[pltpu.VMEM((tm, tn), F32)] if nk > 1 else [], name=name,
        compiler_params=_cp(("parallel", "parallel", "arbitrary")),
    )(*args)


def _rms_fwd(x, w, name, tl=256, dep=None):
    L = x.shape[0]

    def kern(x_ref, w_ref, *rest):
        o_ref = rest[-1]
        xv = x_ref[...]
        r = lax.rsqrt(jnp.mean(xv * xv, axis=-1, keepdims=True) + EPS)
        o_ref[...] = ((xv * r) * w_ref[...]).astype(BF16)

    row = pl.BlockSpec((tl, D), lambda i: (i, 0))
    deps = [] if dep is None else [dep]
    return pl.pallas_call(
        kern, grid=(L // tl,),
        in_specs=[row, pl.BlockSpec((1, D), lambda i: (0, 0))] + [pl.BlockSpec(memory_space=pl.ANY)] * len(deps),
        out_specs=row, out_shape=jax.ShapeDtypeStruct((L, D), BF16), name=name, compiler_params=_cp(("parallel",)),
    )(x, w.reshape(1, D), *deps)


def _rms_bwd(dy, x, w, res, name, tl=256, dep=None):
    L = x.shape[0]
    deps = [] if dep is None else [dep]

    def kern(dy_ref, x_ref, w_ref, res_ref, *rest):
        dx_ref, gw_ref = rest[-2:]
        @pl.when(pl.program_id(0) == 0)
        def _():
            gw_ref[...] = jnp.zeros_like(gw_ref)

        xv = x_ref[...]
        dyv = dy_ref[...]
        r = lax.rsqrt(jnp.mean(xv * xv, axis=-1, keepdims=True) + EPS)
        xn = xv * r
        gw_ref[...] += jnp.broadcast_to(jnp.sum(dyv * xn, axis=0, keepdims=True), (8, D))
        dxn = dyv * w_ref[...]
        dx_ref[...] = res_ref[...] + r * (dxn - xn * jnp.mean(dxn * xn, axis=-1, keepdims=True))

    row = pl.BlockSpec((tl, D), lambda i: (i, 0))
    return pl.pallas_call(
        kern, grid=(L // tl,),
        in_specs=[row, row, pl.BlockSpec((1, D), lambda i: (0, 0)), row] + [pl.BlockSpec(memory_space=pl.ANY)] * len(deps),
        out_specs=(row, pl.BlockSpec((8, D), lambda i: (0, 0))),
        out_shape=(jax.ShapeDtypeStruct((L, D), F32), jax.ShapeDtypeStruct((8, D), F32)),
        name=name, compiler_params=_cp(("arbitrary",)),
    )(dy, x, w.reshape(1, D), res, *deps)


def _final(x2, w, tgt, tl=256):
    L = x2.shape[0]

    def kern(x_ref, w_ref, t_ref, dx_ref, gw_ref, loss_ref):
        @pl.when(pl.program_id(0) == 0)
        def _():
            gw_ref[...] = jnp.zeros_like(gw_ref)
            loss_ref[...] = jnp.zeros_like(loss_ref)

        xv = x_ref[...]
        r = lax.rsqrt(jnp.mean(xv * xv, axis=-1, keepdims=True) + EPS)
        xn = xv * r
        e = xn * w_ref[...] - t_ref[...]
        per_tok = jnp.mean(e * e, axis=-1, keepdims=True)
        loss_ref[...] += 0.5 * jnp.sum(per_tok)
        dyv = e * (1.0 / D)
        gw_ref[...] += jnp.broadcast_to(jnp.sum(dyv * xn, axis=0, keepdims=True), (8, D))
        dxn = dyv * w_ref[...]
        dx_ref[...] = r * (dxn - xn * jnp.mean(dxn * xn, axis=-1, keepdims=True))

    row = pl.BlockSpec((tl, D), lambda i: (i, 0))
    return pl.pallas_call(
        kern, grid=(L // tl,), in_specs=[row, pl.BlockSpec((1, D), lambda i: (0, 0)), row],
        out_specs=(row, pl.BlockSpec((8, D), lambda i: (0, 0)), pl.BlockSpec((8, 128), lambda i: (0, 0))),
        out_shape=(jax.ShapeDtypeStruct((L, D), F32), jax.ShapeDtypeStruct((8, D), F32),
                   jax.ShapeDtypeStruct((8, 128), F32)),
        name="final_norm_loss", compiler_params=_cp(("arbitrary",)),
    )(x2, w.reshape(1, D), tgt)


def _down(v, k):
    if k == 0:
        return v
    t = lax.broadcasted_iota(jnp.int32, v.shape, 0)
    return jnp.where(t >= k, pltpu.roll(v, k, axis=0), 0.0)


def _up(v, k):
    if k == 0:
        return v
    n = v.shape[0]
    t = lax.broadcasted_iota(jnp.int32, v.shape, 0)
    return jnp.where(t < n - k, pltpu.roll(v, n - k, axis=0), 0.0)


TW = 256


def _sc_fwd(proj, cw):
    L = proj.shape[0]
    nb = D // TW

    def kern(b_ref, c_ref, x_ref, w_ref, o_ref):
        u = c_ref[...].astype(F32) * x_ref[...].astype(F32)
        w = w_ref[...]
        cv = w[0:1] * _down(u, 2) + w[1:2] * _down(u, 1) + w[2:3] * u
        o_ref[...] = (b_ref[...].astype(F32) * cv).astype(BF16)

    col = lambda off: pl.BlockSpec((L, TW), lambda j: (0, off + j))
    return pl.pallas_call(
        kern, grid=(nb,), in_specs=[col(0), col(nb), col(2 * nb), pl.BlockSpec((8, TW), lambda j: (0, j))],
        out_specs=pl.BlockSpec((L, TW), lambda j: (0, j)), out_shape=jax.ShapeDtypeStruct((L, D), BF16),
        name="sc_fwd", compiler_params=_cp(("parallel",)),
    )(proj, proj, proj, cw)


def _sc_bwd(dya, proj, cw, dproj):
    L = proj.shape[0]
    nb = D // TW

    def kern(d_ref, b_ref, c_ref, x_ref, w_ref, _, dp_ref, gw_ref, keep):
        sec = pl.program_id(1)

        @pl.when(sec == 0)
        def _():
            cs, xs, dyv = c_ref[...].astype(F32), x_ref[...].astype(F32), d_ref[...]
            w = w_ref[...]
            u = cs * xs
            u1, u2 = _down(u, 1), _down(u, 2)
            cv = w[0:1] * u2 + w[1:2] * u1 + w[2:3] * u
            dcv = dyv * b_ref[...].astype(F32)
            du = w[2:3] * dcv + w[1:2] * _up(dcv, 1) + w[0:1] * _up(dcv, 2)
            g0 = jnp.sum(dcv * u2, axis=0, keepdims=True)
            g1 = jnp.sum(dcv * u1, axis=0, keepdims=True)
            g2 = jnp.sum(dcv * u, axis=0, keepdims=True)
            row = lax.broadcasted_iota(jnp.int32, (8, TW), 0)
            gw_ref[...] = jnp.where(row == 0, g0, jnp.where(row == 1, g1, jnp.where(row == 2, g2, 0.0)))
            dp_ref[...] = (dyv * cv).astype(BF16)
            keep[0] = (du * xs).astype(BF16)
            keep[1] = (du * cs).astype(BF16)

        @pl.when(sec > 0)
        def _():
            dp_ref[...] = keep[sec - 1]

    col = lambda off: pl.BlockSpec((L, TW), lambda j, s: (0, off + j))
    return pl.pallas_call(
        kern, grid=(nb, 3),
        in_specs=[col(0), col(0), col(nb), col(2 * nb), pl.BlockSpec((8, TW), lambda j, s: (0, j)),
                  pl.BlockSpec(memory_space=pl.ANY)],
        out_specs=(pl.BlockSpec((L, TW), lambda j, s: (0, s * nb + j)), pl.BlockSpec((8, TW), lambda j, s: (0, j))),
        out_shape=(jax.ShapeDtypeStruct(dproj.shape, BF16), jax.ShapeDtypeStruct((8, D), F32)),
        scratch_shapes=[pltpu.VMEM((2, L, TW), BF16)],
        input_output_aliases={5: 0}, name="sc_bwd", compiler_params=_cp(("parallel", "arbitrary")),
    )(dya, proj, proj, proj, cw, dproj)


def _ssm_conv_fwd(proj, cw4):
    L = proj.shape[0]
    off = C_XBC // TW

    def kern(r_ref, w_ref, o_ref):
        raw = r_ref[...].astype(F32)
        w = w_ref[...]
        c4 = w[0:1] * _down(raw, 3) + w[1:2] * _down(raw, 2) + w[2:3] * _down(raw, 1) + w[3:4] * raw + w[4:5]
        o_ref[...] = c4 * _sigmoid(c4)

    return pl.pallas_call(
        kern, grid=(XBC // TW,),
        in_specs=[pl.BlockSpec((L, TW), lambda j: (0, off + j)), pl.BlockSpec((8, TW), lambda j: (0, j))],
        out_specs=pl.BlockSpec((L, TW), lambda j: (0, j)), out_shape=jax.ShapeDtypeStruct((L, XBC), F32),
        name="ssm_conv_fwd", compiler_params=_cp(("parallel",)),
    )(proj, cw4)


def _ssm_conv_bwd(dx, proj, cw4, dproj, col0, name):
    L, width = dx.shape
    off_p = (C_XBC + col0) // TW
    off_w = col0 // TW

    def kern(d_ref, r_ref, w_ref, _, dp_ref, gw_ref):
        raw = r_ref[...].astype(F32)
        w = w_ref[...]
        r1, r2, r3 = _down(raw, 1), _down(raw, 2), _down(raw, 3)
        c4 = w[0:1] * r3 + w[1:2] * r2 + w[2:3] * r1 + w[3:4] * raw + w[4:5]
        sg = _sigmoid(c4)
        dc4 = d_ref[...] * (sg * (1.0 + c4 * (1.0 - sg)))
        draw = w[3:4] * dc4 + w[2:3] * _up(dc4, 1) + w[1:2] * _up(dc4, 2) + w[0:1] * _up(dc4, 3)
        dp_ref[...] = draw.astype(BF16)
        gs = [jnp.sum(dc4 * r3, axis=0, keepdims=True), jnp.sum(dc4 * r2, axis=0, keepdims=True),
              jnp.sum(dc4 * r1, axis=0, keepdims=True), jnp.sum(dc4 * raw, axis=0, keepdims=True),
              jnp.sum(dc4, axis=0, keepdims=True)]
        row = lax.broadcasted_iota(jnp.int32, (8, TW), 0)
        acc = jnp.zeros((8, TW), F32)
        for k, gk in enumerate(gs):
            acc = jnp.where(row == k, gk, acc)
        gw_ref[...] = acc

    return pl.pallas_call(
        kern, grid=(width // TW,),
        in_specs=[pl.BlockSpec((L, TW), lambda j: (0, j)), pl.BlockSpec((L, TW), lambda j: (0, off_p + j)),
                  pl.BlockSpec((8, TW), lambda j: (0, off_w + j)), pl.BlockSpec(memory_space=pl.ANY)],
        out_specs=(pl.BlockSpec((L, TW), lambda j: (0, off_p + j)), pl.BlockSpec((8, TW), lambda j: (0, j))),
        out_shape=(jax.ShapeDtypeStruct(dproj.shape, BF16), jax.ShapeDtypeStruct((8, width), F32)),
        input_output_aliases={3: 0}, name=name, compiler_params=_cp(("arbitrary",)),
    )(dx, proj, cw4, dproj)


def _split3(v):
    h1 = v.astype(BF16)
    r1 = v - h1.astype(F32)
    h2 = r1.astype(BF16)
    h3 = (r1 - h2.astype(F32)).astype(BF16)
    return h1, h2, h3


def _dot01(m01, v, dims=_DIMS["nn"], m_left=True, terms=3):
    out = None
    for part in _split3(v)[:terms]:
        ops = (m01, part) if m_left else (part, m01)
        t = lax.dot_general(ops[0], ops[1], dims, preferred_element_type=F32)
        out = t if out is None else out + t
    return out


def _bdot(a, b, mode="nn"):
    return lax.dot_general(a.astype(BF16), b.astype(BF16), _DIMS[mode], preferred_element_type=F32)


def _softplus(v):
    return jnp.maximum(v, 0.0) + jnp.log1p(jnp.exp(-jnp.abs(v)))


def _dt_prep(proj, vec):
    L = proj.shape[0]

    def kern(p_ref, v_ref, dt_ref, cs_ref, sg_ref):
        v = v_ref[...]
        pre = p_ref[:, 0:128] + v[0:1]
        dt = _softplus(pre)
        da = dt * (-jnp.exp(v[1:2]))
        ii = lax.broadcasted_iota(jnp.int32, (Q, Q), 0)
        jj = lax.broadcasted_iota(jnp.int32, (Q, Q), 1)
        ltri = (jj <= ii).astype(BF16)
        lane = lax.broadcasted_iota(jnp.int32, (Q, 128), 1)
        for val, ref in ((dt, dt_ref), (_dot01(ltri, da), cs_ref), (_sigmoid(pre), sg_ref)):
            for g in range(NG):
                moved = val if g == 0 else pltpu.roll(val, 128 - 4 * g, axis=1)
                ref[g] = jnp.where(lane < 4, moved, 0.0)

    blk = pl.BlockSpec((NG, Q, 128), lambda c: (0, c, 0))
    return pl.pallas_call(
        kern, grid=(L // Q,),
        in_specs=[pl.BlockSpec((Q, 256), lambda c: (c, 0)), pl.BlockSpec((8, 128), lambda c: (0, 0))],
        out_specs=(blk, blk, blk),
        out_shape=(jax.ShapeDtypeStruct((NG, L, 128), F32),) * 3,
        name="dt_prep", compiler_params=_cp(("parallel",)),
    )(proj, vec)


def _head_masks():
    lane = lax.broadcasted_iota(jnp.int32, (1, 4 * HD), 1)
    return [((lane >= HD * j) & (lane < HD * (j + 1))) for j in range(4)]


def _expand4(v4, masks):
    R = v4.shape[0]
    out = jnp.zeros((R, 4 * HD), F32)
    for j in range(4):
        out = jnp.where(masks[j], jnp.broadcast_to(v4[:, j:j + 1], (R, 4 * HD)), out)
    return out


def _decay_matrix(cs_col, tri):
    colb = jnp.broadcast_to(cs_col, (Q, Q))
    return jnp.exp(jnp.where(tri, colb - colb.T, -jnp.inf))


def _ssd_fwd(xbc, dt4, cs4, vecg):
    L = xbc.shape[0]
    nc = L // Q

    def kern(x_ref, b_ref, c_ref, dt_ref, cs_ref, v_ref, y_ref, s_ref, S):
        c = pl.program_id(1)

        @pl.when(c == 0)
        def _():
            S[...] = jnp.zeros_like(S)

        masks = _head_masks()
        ii = lax.broadcasted_iota(jnp.int32, (Q, Q), 0)
        jj = lax.broadcasted_iota(jnp.int32, (Q, Q), 1)
        tri = jj <= ii
        for gi in range(GPS):
            xs, ns = slice(256 * gi, 256 * (gi + 1)), slice(NS * gi, NS * (gi + 1))
            dt4v, cs4v = dt_ref[gi], cs_ref[gi]
            dt_b, cs_b = _expand4(dt4v, masks), _expand4(cs4v, masks)
            d_b = _expand4(v_ref[gi], masks)[1:2]
            cs_last = cs_b[Q - 1:Q, :]
            x4, bm, cm = x_ref[:, xs], b_ref[:, ns], c_ref[:, ns]
            xdt = x4 * dt_b
            gm = _bdot(cm, bm, "nt")
            s4 = S[gi]
            s_ref[gi, 0] = s4
            y = _bdot(cm, s4) * jnp.exp(cs_b) + d_b * x4
            m_all = jnp.concatenate([(gm * _decay_matrix(cs4v[:, j:j + 1], tri)).astype(BF16) for j in range(4)], axis=0)
            yd = _bdot(m_all, xdt)
            for j in range(4):
                y = y + jnp.where(masks[j], yd[Q * j:Q * (j + 1)], 0.0)
            y_ref[:, xs] = y
            S[gi] = jnp.exp(cs_last) * s4 + _bdot(bm, xdt * jnp.exp(cs_last - cs_b), "tn")

    sc = pl.BlockSpec((GPS, Q, 128), lambda g, c: (g, c, 0))
    bw = NS * GPS
    return pl.pallas_call(
        kern, grid=(NG // GPS, nc),
        in_specs=[pl.BlockSpec((Q, 256 * GPS), lambda g, c: (c, g)),
                  pl.BlockSpec((Q, bw), lambda g, c: (c, INNER // bw + g)),
                  pl.BlockSpec((Q, bw), lambda g, c: (c, (INNER + NG * NS) // bw + g)),
                  sc, sc, pl.BlockSpec((GPS, 8, 128), lambda g, c: (g, 0, 0))],
        out_specs=(pl.BlockSpec((Q, 256 * GPS), lambda g, c: (c, g)),
                   pl.BlockSpec((GPS, 1, NS, 256), lambda g, c: (g, c, 0, 0))),
        out_shape=(jax.ShapeDtypeStruct((L, INNER), F32), jax.ShapeDtypeStruct((NG, nc, NS, 256), F32)),
        scratch_shapes=[pltpu.VMEM((GPS, NS, 256), F32)], name="ssd_fwd",
        compiler_params=_cp(("parallel", "arbitrary")),
    )(xbc, xbc, xbc, dt4, cs4, vecg)


def _ssd_bwd(xbc, dt4, cs4, sg4, vecg, s_all, dy):
    L = xbc.shape[0]
    nc = L // Q

    def kern(x_ref, b_ref, c_ref, dt_ref, cs_ref, sg_ref, v_ref, s_ref, dy_ref,
             dx_ref, db_ref, dc_ref, ddt_ref, st_ref, dS):
        cc = pl.program_id(1)

        @pl.when(cc == 0)
        def _():
            dS[...] = jnp.zeros_like(dS)
            st_ref[...] = jnp.zeros_like(st_ref)

        masks = _head_masks()
        ii = lax.broadcasted_iota(jnp.int32, (Q, Q), 0)
        jj = lax.broadcasted_iota(jnp.int32, (Q, Q), 1)
        tri = jj <= ii
        utri = (jj >= ii).astype(BF16)
        li = lax.broadcasted_iota(jnp.int32, (4 * HD, 4 * HD), 0)
        lj = lax.broadcasted_iota(jnp.int32, (4 * HD, 4 * HD), 1)
        eblk = ((li // HD) == (lj // HD)).astype(BF16)
        lane128 = lax.broadcasted_iota(jnp.int32, (Q, 128), 1)

        for gi in range(GPS):
            xs, ns = slice(256 * gi, 256 * (gi + 1)), slice(NS * gi, NS * (gi + 1))
            dt4v, cs4v, sg4v = dt_ref[gi], cs_ref[gi], sg_ref[gi]
            dt_b, cs_b = _expand4(dt4v, masks), _expand4(cs4v, masks)
            vv = _expand4(v_ref[gi], masks)
            a_b = -jnp.exp(vv[0:1])
            d_b = vv[1:2]
            a4 = -jnp.exp(v_ref[gi][0:1, :])
            cs_last = cs_b[Q - 1:Q, :]
            ecs = jnp.exp(cs_b)
            decay = jnp.exp(cs_last - cs_b)
            elast = jnp.exp(cs_last)
            x4, bm, cm, dyv = x_ref[:, xs], b_ref[:, ns], c_ref[:, ns], dy_ref[:, xs]
            s4 = s_ref[gi, 0]
            dsn = dS[gi]
            xdt = x4 * dt_b
            gm = _bdot(cm, bm, "nt")
            gmt = gm.T
            dye = dyv * ecs
            yoff = ecs * _bdot(cm, s4)
            t4 = _bdot(bm, dsn) * decay
            lms, mhs, mhts = [], [], []
            for j in range(4):
                colb = jnp.broadcast_to(cs4v[:, j:j + 1], (Q, Q))
                seg = colb - colb.T
                lms.append(jnp.exp(jnp.where(tri, seg, -jnp.inf)))
                mhs.append(gm * lms[j])
                mhts.append(gmt * jnp.exp(jnp.where(jj >= ii, -seg, -jnp.inf)))
            m_all = jnp.concatenate([m.astype(BF16) for m in mhs], axis=0)
            dy_m = jnp.concatenate([jnp.where(masks[j], dyv, 0.0).astype(BF16) for j in range(4)], axis=0)
            x_m = jnp.concatenate([jnp.where(masks[j], xdt, 0.0).astype(BF16) for j in range(4)], axis=0)
            dxdt = t4 + _bdot(m_all, dy_m, "tn")
            dm_all = _bdot(dy_m, xdt, "nt")
            dmt_all = _bdot(x_m, dyv, "nt")
            dg = jnp.zeros((Q, Q), F32)
            rc = jnp.zeros((Q, 4 * HD), F32)
            for j in range(4):
                dmh = dm_all[Q * j:Q * (j + 1)]
                dg = dg + dmh * lms[j]
                rs = (jnp.sum(dmh * mhs[j], axis=1, keepdims=True)
                      - jnp.sum(dmt_all[Q * j:Q * (j + 1)] * mhts[j], axis=1, keepdims=True))
                rc = jnp.where(masks[j], jnp.broadcast_to(rs, (Q, 4 * HD)), rc)
            xt = xdt * t4
            tail = jnp.sum(xt, axis=0, keepdims=True) + elast * jnp.sum(s4 * dsn, axis=0, keepdims=True)
            gd_raw = jnp.sum(dyv * x4, axis=0, keepdims=True)
            stacked = jnp.concatenate([dyv * yoff - xt, dxdt * x4, jnp.broadcast_to(tail, (8, 4 * HD)),
                                       jnp.broadcast_to(gd_raw, (8, 4 * HD))], axis=0)
            seg = _dot01(eblk, stacked, m_left=False, terms=2)
            da_b = seg[0:Q] + rc
            dda_b = _dot01(utri, da_b, terms=2) + seg[2 * Q:2 * Q + 1]
            ddt_b = dda_b * a_b + seg[Q:2 * Q]
            gd_b = seg[2 * Q + 8:2 * Q + 9]
            ddt4 = jnp.zeros((Q, 128), F32)
            dda4 = jnp.zeros((Q, 128), F32)
            for j in range(4):
                ddt4 = jnp.where(lane128 == j, jnp.broadcast_to(ddt_b[:, HD * j:HD * j + 1], (Q, 128)), ddt4)
                dda4 = jnp.where(lane128 == j, jnp.broadcast_to(dda_b[:, HD * j:HD * j + 1], (Q, 128)), dda4)
            ddt_ref[gi] = ddt4 * sg4v
            ga = jnp.sum(dda4 * dt4v * a4, axis=0, keepdims=True)
            gd = jnp.zeros((1, 128), F32)
            for j in range(4):
                gd = jnp.where(lane128[0:1] == j, jnp.broadcast_to(gd_b[:, HD * j:HD * j + 1], (1, 128)), gd)
            row = lax.broadcasted_iota(jnp.int32, (8, 128), 0)
            st_ref[gi] += jnp.where(row == 0, ga, jnp.where(row == 1, gd, 0.0))
            dx_ref[:, xs] = d_b * dyv + dxdt * dt_b
            dc_ref[:, ns] = _bdot(dg, bm) + _bdot(dye, s4, "nt")
            db_ref[:, ns] = _bdot(dg, cm, "tn") + _bdot(xdt * decay, dsn, "nt")
            dS[gi] = elast * dsn + _bdot(cm, dye, "tn")

    rv = lambda c: nc - 1 - c
    sc = pl.BlockSpec((GPS, Q, 128), lambda g, c: (g, rv(c), 0))
    bw = NS * GPS
    return pl.pallas_call(
        kern, grid=(NG // GPS, nc),
        in_specs=[pl.BlockSpec((Q, 256 * GPS), lambda g, c: (rv(c), g)),
                  pl.BlockSpec((Q, bw), lambda g, c: (rv(c), INNER // bw + g)),
                  pl.BlockSpec((Q, bw), lambda g, c: (rv(c), (INNER + NG * NS) // bw + g)),
                  sc, sc, sc, pl.BlockSpec((GPS, 8, 128), lambda g, c: (g, 0, 0)),
                  pl.BlockSpec((GPS, 1, NS, 256), lambda g, c: (g, rv(c), 0, 0)),
                  pl.BlockSpec((Q, 256 * GPS), lambda g, c: (rv(c), g))],
        out_specs=(pl.BlockSpec((Q, 256 * GPS), lambda g, c: (rv(c), g)),
                   pl.BlockSpec((Q, bw), lambda g, c: (rv(c), g)),
                   pl.BlockSpec((Q, bw), lambda g, c: (rv(c), g)),
                   pl.BlockSpec((GPS, Q, 128), lambda g, c: (g, rv(c), 0)),
                   pl.BlockSpec((GPS, 8, 128), lambda g, c: (g, 0, 0))),
        out_shape=(jax.ShapeDtypeStruct((L, INNER), F32), jax.ShapeDtypeStruct((L, NG * NS), F32),
                   jax.ShapeDtypeStruct((L, NG * NS), F32), jax.ShapeDtypeStruct((NG, L, 128), F32),
                   jax.ShapeDtypeStruct((NG, 8, 128), F32)),
        scratch_shapes=[pltpu.VMEM((GPS, NS, 256), F32)], name="ssd_bwd",
        compiler_params=_cp(("parallel", "arbitrary")),
    )(xbc, xbc, xbc, dt4, cs4, sg4, vecg, s_all, dy)


def _dt_bwd(ddt, dproj, tl=256):
    L = ddt.shape[1]

    def kern(d_ref, _, dp_ref, gs_ref):
        @pl.when(pl.program_id(0) == 0)
        def _():
            gs_ref[...] = jnp.zeros_like(gs_ref)

        d = d_ref[0]
        for g in range(1, NG):
            d = d + pltpu.roll(d_ref[g], 4 * g, axis=1)
        gs_ref[...] += jnp.broadcast_to(jnp.sum(d, axis=0, keepdims=True), (8, 128))
        dp_ref[...] = jnp.concatenate([d, jnp.zeros_like(d)], axis=1).astype(BF16)

    return pl.pallas_call(
        kern, grid=(L // tl,),
        in_specs=[pl.BlockSpec((NG, tl, 128), lambda i: (0, i, 0)), pl.BlockSpec(memory_space=pl.ANY)],
        out_specs=(pl.BlockSpec((tl, 256), lambda i: (i, C_DT // 256)), pl.BlockSpec((8, 128), lambda i: (0, 0))),
        out_shape=(jax.ShapeDtypeStruct(dproj.shape, BF16), jax.ShapeDtypeStruct((8, 128), F32)),
        input_output_aliases={1: 0}, name="dt_bwd", compiler_params=_cp(("arbitrary",)),
    )(ddt, dproj)


GW = INNER // NG


def _gnorm_fwd(y, proj, w, tl=256):
    L = y.shape[0]
    zoff = C_Z // 1024

    def kern(y_ref, z_ref, w_ref, o_ref):
        z = z_ref[...].astype(F32)
        yz = y_ref[...] * (z * _sigmoid(z))
        wv = w_ref[...]
        for k in range(1024 // GW):
            sl = slice(GW * k, GW * (k + 1))
            v = yz[:, sl]
            rg = lax.rsqrt(jnp.mean(v * v, axis=-1, keepdims=True) + EPS)
            o_ref[:, sl] = ((v * rg) * wv[:, sl]).astype(BF16)

    blk = pl.BlockSpec((tl, 1024), lambda i, j: (i, j))
    return pl.pallas_call(
        kern, grid=(L // tl, 2),
        in_specs=[blk, pl.BlockSpec((tl, 1024), lambda i, j: (i, zoff + j)), pl.BlockSpec((1, 1024), lambda i, j: (0, j))],
        out_specs=blk, out_shape=jax.ShapeDtypeStruct((L, INNER), BF16), name="gnorm_fwd",
        compiler_params=_cp(("parallel", "parallel")),
    )(y, proj, w.reshape(1, INNER))


def _gnorm_bwd(dyb, y, proj, w, dproj, tl=256):
    L = y.shape[0]
    zoff = C_Z // 1024

    def kern(d_ref, y_ref, z_ref, w_ref, _, dy_ref, dp_ref, gw_ref):
        @pl.when(pl.program_id(1) == 0)
        def _():
            gw_ref[...] = jnp.zeros_like(gw_ref)

        z = z_ref[...].astype(F32)
        sg = _sigmoid(z)
        sz = z * sg
        yv = y_ref[...]
        yz = yv * sz
        dv = d_ref[...]
        wv = w_ref[...]
        for k in range(1024 // GW):
            sl = slice(GW * k, GW * (k + 1))
            v = yz[:, sl]
            rg = lax.rsqrt(jnp.mean(v * v, axis=-1, keepdims=True) + EPS)
            vn = v * rg
            dk = dv[:, sl]
            gw_ref[:, sl] += jnp.broadcast_to(jnp.sum(dk * vn, axis=0, keepdims=True), (8, GW))
            dvn = dk * wv[:, sl]
            dyz = rg * (dvn - vn * jnp.mean(dvn * vn, axis=-1, keepdims=True))
            dy_ref[:, sl] = dyz * sz[:, sl]
            dp_ref[:, sl] = (dyz * yv[:, sl] * (sg[:, sl] * (1.0 + z[:, sl] * (1.0 - sg[:, sl])))).astype(BF16)

    blk = pl.BlockSpec((tl, 1024), lambda j, i: (i, j))
    zblk = pl.BlockSpec((tl, 1024), lambda j, i: (i, zoff + j))
    return pl.pallas_call(
        kern, grid=(2, L // tl),
        in_specs=[blk, blk, zblk, pl.BlockSpec((1, 1024), lambda j, i: (0, j)), pl.BlockSpec(memory_space=pl.ANY)],
        out_specs=(blk, zblk, pl.BlockSpec((8, 1024), lambda j, i: (0, j))),
        out_shape=(jax.ShapeDtypeStruct((L, INNER), F32), jax.ShapeDtypeStruct(dproj.shape, BF16),
                   jax.ShapeDtypeStruct((8, INNER), F32)),
        input_output_aliases={4: 1}, name="gnorm_bwd", compiler_params=_cp(("parallel", "arbitrary")),
    )(dyb, y, proj, w.reshape(1, INNER), dproj)


def _merge_fwd(proj, bg, br_a, br_b, tl=256):
    L = proj.shape[0]
    goff = C_GATE // 1024

    def kern(g1_ref, g2_ref, b1_ref, b2_ref, a_ref, b_ref, o_ref):
        g1 = _sigmoid(g1_ref[...].astype(F32) + b1_ref[...])
        g2 = _sigmoid(g2_ref[...].astype(F32) + b2_ref[...])
        o_ref[...] = (g1 * a_ref[...] + g2 * b_ref[...]).astype(BF16)

    row = pl.BlockSpec((tl, 1024), lambda i: (i, 0))
    bg2 = bg.reshape(1, 2 * D)
    return pl.pallas_call(
        kern, grid=(L // tl,),
        in_specs=[pl.BlockSpec((tl, 1024), lambda i: (i, goff)), pl.BlockSpec((tl, 1024), lambda i: (i, goff + 1)),
                  pl.BlockSpec((1, 1024), lambda i: (0, 0)), pl.BlockSpec((1, 1024), lambda i: (0, 1)), row, row],
        out_specs=row, out_shape=jax.ShapeDtypeStruct((L, D), BF16), name="merge_fwd",
        compiler_params=_cp(("parallel",)),
    )(proj, proj, bg2, bg2, br_a, br_b)


def _merge_bwd(dm, proj, bg, br_a, br_b, dproj, tl=256):
    L = proj.shape[0]
    goff = C_GATE // 1024

    def kern(dm_ref, g_ref, b_ref, a_ref, bb_ref, _, dbr_ref, dp_ref, gb_ref):
        j = pl.program_id(0)

        @pl.when(pl.program_id(1) == 0)
        def _():
            gb_ref[...] = jnp.zeros_like(gb_ref)

        g = _sigmoid(g_ref[...].astype(F32) + b_ref[...])
        br = jnp.where(j == 0, a_ref[...], bb_ref[...])
        dmv = dm_ref[...]
        dbr_ref[0] = (dmv * g).astype(BF16)
        dgate = dmv * br * g * (1.0 - g)
        gb_ref[...] += jnp.broadcast_to(jnp.sum(dgate, axis=0, keepdims=True), (8, 1024))
        dp_ref[...] = dgate.astype(BF16)

    row = pl.BlockSpec((tl, 1024), lambda j, i: (i, 0))
    gblk = pl.BlockSpec((tl, 1024), lambda j, i: (i, goff + j))
    return pl.pallas_call(
        kern, grid=(2, L // tl),
        in_specs=[row, gblk, pl.BlockSpec((1, 1024), lambda j, i: (0, j)), row, row, pl.BlockSpec(memory_space=pl.ANY)],
        out_specs=(pl.BlockSpec((1, tl, 1024), lambda j, i: (j, i, 0)), gblk, pl.BlockSpec((8, 1024), lambda j, i: (0, j))),
        out_shape=(jax.ShapeDtypeStruct((2, L, D), BF16), jax.ShapeDtypeStruct(dproj.shape, BF16),
                   jax.ShapeDtypeStruct((8, 2 * D), F32)),
        input_output_aliases={5: 1}, name="merge_bwd", compiler_params=_cp(("parallel", "arbitrary")),
    )(dm, proj, bg.reshape(1, 2 * D), br_a, br_b, dproj)


def _coords():
    return lax.axis_index("x"), lax.axis_index("y"), lax.axis_index("c")


def _other_chips(sk):
    xk, yk = sk // 2, sk % 2
    return [((1 - xk, yk), 2 * (1 - xk) + yk), ((xk, 1 - yk), 2 * xk + 1 - yk), ((1 - xk, 1 - yk), 2 * (1 - xk) + 1 - yk)]


def _rows(start, size):
    assert size % 128 == 0
    return pl.ds(pl.multiple_of(start, 128), size)


def _per_chip(fn):
    x, y, _ = _coords()
    s = 2 * x + y
    for sk in range(4):
        pl.when(s == sk)(functools.partial(fn, sk))


XTRA = PIECE - PMAIN


def _place(shard, full_shape, block, index_map, idx, name, blk0=0, nblk=None, dep=None):
    in_block = block[-2:]
    if nblk is None:
        nblk = shard.shape[0] // in_block[0]

    def kern(idx_ref, s_ref, *rest):
        o_ref = rest[-1]
        o_ref[...] = s_ref[...].astype(BF16).reshape(o_ref.shape)

    grid_spec = pltpu.PrefetchScalarGridSpec(
        num_scalar_prefetch=1, grid=(nblk,),
        in_specs=[pl.BlockSpec(in_block, lambda i, idx_ref: (blk0 + i, 0))] + ([_ANY] if dep is not None else []),
        out_specs=pl.BlockSpec(block, index_map))
    args = (idx, shard) + ((dep,) if dep is not None else ())
    return pl.pallas_call(kern, grid_spec=grid_spec, out_shape=jax.ShapeDtypeStruct(full_shape, BF16), name=name,
                          compiler_params=_cp(("arbitrary",)))(*args)


_SEM = pl.BlockSpec(memory_space=pltpu.SEMAPHORE)
_EFFECT = pltpu.SideEffectType.DATAFLOW_SIDE_EFFECTING


_ANY = pl.BlockSpec(memory_space=pl.ANY)


def _tie(v, dep, name):
    def body(v_ref, dep_ref, o_ref):
        del v_ref, dep_ref, o_ref

    return pl.pallas_call(body, out_shape=jax.ShapeDtypeStruct(v.shape, v.dtype), in_specs=[_ANY, _ANY],
                          out_specs=_ANY, input_output_aliases={0: 0}, name=name)(v, dep)


def _split_call(name, arrays, start=None, wait=None, wait_sems=None, after=None):
    keys = list(arrays)
    n = len(keys)
    n_start = start.n if start is not None else 0
    afters = [] if after is None else (list(after) if isinstance(after, (list, tuple)) else [after])

    def body(*refs):
        pos = n
        if wait is not None:
            wss, wrs = refs[pos], refs[pos + 1]
            pos += 2
        pos += len(afters)
        if start is not None:
            nss, nrs = refs[pos], refs[pos + 1]
            pos += 2
        R = dict(zip(keys, refs[pos:pos + n]))
        token = refs[pos + n]
        x, y, c = _coords()

        def desc(src, dst, dev, ss, rs, k):
            return pltpu.make_async_remote_copy(src_ref=src, dst_ref=dst, send_sem=ss.at[k], recv_sem=rs.at[k],
                                                device_id=dev, device_id_type=MESH)

        def run(sk):
            if wait is not None:
                for k, (snd, land) in enumerate(wait.copies(sk, R)):
                    if snd is not None:
                        desc(snd[0], snd[1], snd[2], wss, wrs, k).wait_send()
                    if land is not None:
                        desc(land, land, (x, y, c), wss, wrs, k).wait_recv()
            if start is not None:
                for k, (snd, land) in enumerate(start.copies(sk, R)):
                    if snd is not None:
                        desc(snd[0], snd[1], snd[2], nss, nrs, k).start()

        _per_chip(run)
        token[...] = jnp.zeros_like(token)

    hbm = pl.BlockSpec(memory_space=HBM)
    vals = [arrays[k] for k in keys]
    ins, in_specs = list(vals), [hbm] * n
    if wait is not None:
        ins += list(wait_sems)
        in_specs += [_SEM, _SEM]
    ins += afters
    in_specs += [pl.BlockSpec(memory_space=pl.ANY)] * len(afters)
    out_shape, out_specs = [], []
    if start is not None:
        out_shape += [pltpu.SemaphoreType.DMA((n_start,)), pltpu.SemaphoreType.DMA((n_start,))]
        out_specs += [_SEM, _SEM]
    first = len(out_shape)
    out_shape += [jax.ShapeDtypeStruct(v.shape, v.dtype) for v in vals] + [jax.ShapeDtypeStruct((8, 128), F32)]
    out_specs += [hbm] * n + [pl.BlockSpec(memory_space=pltpu.VMEM)]
    res = pl.pallas_call(
        body, out_shape=tuple(out_shape), in_specs=in_specs, out_specs=tuple(out_specs),
        input_output_aliases={i: first + i for i in range(n)}, name=name,
        compiler_params=pltpu.CompilerParams(has_side_effects=_EFFECT),
    )(*ins)
    sems = (res[0], res[1]) if start is not None else None
    return dict(zip(keys, res[first:first + n])), sems, res[-1]


class _Plan:
    def __init__(self, n, copies):
        self.n, self.copies = n, copies


_HM, _HX = PMAIN // 2, XTRA // 2
WAVE0 = 768
WAVES = ((0, WAVE0), (WAVE0, _HM - WAVE0))
_WIN = {
    "wq0": (True, "wct", lambda r, sc, hc: r.at[_rows(PMAIN * sc + _HM * hc + WAVES[0][0], WAVES[0][1]), :]),
    "wq1": (True, "wct", lambda r, sc, hc: r.at[_rows(PMAIN * sc + _HM * hc + WAVES[1][0], WAVES[1][1]), :]),
    "xt": (True, "xt", lambda r, sc, hc: r.at[sc, _rows(_HX * hc, _HX), :]),
    "w1": (True, "w1", lambda r, sc, hc: r.at[_rows(512 * hc, 512), pl.ds(1024 * sc, 1024)]),
    "w2": (True, "w2", lambda r, sc, hc: r.at[_rows(1024 * sc + 512 * hc, 512), :]),
    "wa": (True, "wa", lambda r, sc, hc: r.at[_rows(256 * sc + 128 * hc, 128), :]),
    "wb": (True, "wb", lambda r, sc, hc: r.at[_rows(512 * sc + 256 * hc, 256), :]),
    "wo": (True, "wo", lambda r, sc, hc: r.at[_rows(256 * sc + 128 * hc, 128), :]),
    "cw": (False, "cw", lambda r, sc, hc: r.at[sc]),
}


def _ag_chips_plan(keys):
    def copies(sk, R):
        _, _, c = _coords()
        out = []
        for key in keys:
            _, arr, win = _WIN[key]
            for (px, py), ps in _other_chips(sk):
                w = win(R[arr], sk, c)
                out.append(((w, w, (px, py, c)), win(R[arr], ps, c)))
        return out
    return _Plan(3 * len(keys), copies)


def _ag_sibling_plan(keys):
    keys = [k for k in keys if _WIN[k][0]]

    def copies(sk, R):
        x, y, c = _coords()
        out = []
        for key in keys:
            _, arr, win = _WIN[key]
            for _, ps in _other_chips(sk):
                w = win(R[arr], ps, c)
                out.append(((w, w, (x, y, 1 - c)), win(R[arr], ps, 1 - c)))
        return out
    return _Plan(3 * len(keys), copies)


def _in_proj_wave(h, wct, wave, proj=None, tm=1024):
    L = h.shape[0]
    tm = min(tm, L)
    off, size = WAVES[wave]
    start = lambda j: pl.multiple_of(_HM * j + off, 128)

    def kern(h_ref, w_ref, *rest):
        o_ref = rest[-1]
        o_ref[...] = lax.dot_general(h_ref[...], w_ref[...], _DIMS["nt"], preferred_element_type=F32).astype(BF16)

    in_specs = [pl.BlockSpec((tm, D), lambda j, i: (i, 0)),
                pl.BlockSpec((pl.Element(size), pl.Element(D)), lambda j, i: (start(j), 0))]
    args, aliases = [h, wct], {}
    if proj is not None:
        in_specs.append(pl.BlockSpec(memory_space=pl.ANY))
        args.append(proj)
        aliases = {2: 0}
    return pl.pallas_call(
        kern, grid=(8, L // tm), in_specs=in_specs,
        out_specs=pl.BlockSpec((pl.Element(tm), pl.Element(size)), lambda j, i: (i * tm, start(j))),
        out_shape=jax.ShapeDtypeStruct((L, NCW), BF16), input_output_aliases=aliases,
        name="in_proj_wave%d" % wave, compiler_params=_cp(("parallel", "parallel")),
    )(*args)


def _fix_wct(wct, xt):
    nb = PMAIN // XTRA

    def kern(w_ref, x_ref, o_ref):
        k = pl.program_id(0)
        xv = x_ref[0]
        o_ref[...] = jnp.where(k < 3, (w_ref[...].astype(F32) + xv.astype(F32)).astype(BF16), xv)

    blk = pl.BlockSpec((XTRA, D), lambda k: (nb * (k + 1), 0))
    rblk = pl.BlockSpec((XTRA, D), lambda k: (jnp.where(k < 3, nb * (k + 1), 0), 0))
    return pl.pallas_call(
        kern, grid=(4,), in_specs=[rblk, pl.BlockSpec((1, XTRA, D), lambda k: (k, 0, 0))], out_specs=blk,
        out_shape=jax.ShapeDtypeStruct(wct.shape, BF16), input_output_aliases={0: 0}, name="fix_wct",
        compiler_params=_cp(("arbitrary",)),
    )(wct, xt)


_HP = PIECE // 2
_GWIN = [
    lambda r, sc, hc: r.at[_rows(PMAIN * sc + _HP * hc, _HP), :],
    lambda r, sc, hc: r.at[_rows(512 * hc, 512), pl.ds(1024 * sc, 1024)],
    lambda r, sc, hc: r.at[_rows(1024 * sc + 512 * hc, 512), :],
    lambda r, sc, hc: r.at[_rows(256 * sc + 128 * hc, 128), :],
    lambda r, sc, hc: r.at[_rows(512 * sc + 256 * hc, 256), :],
    lambda r, sc, hc: r.at[_rows(256 * sc + 128 * hc, 128), :],
]
HALF_SHAPES = [(PIECE // 2, D), (512, 1024), (512, 1024), (128, 1024), (256, 1024), (128, 1024)]


def _rs_sibling_plan(ts):
    def copies(sk, R):
        x, y, c = _coords()
        out = []
        for t in ts:
            for sc in range(4):
                land = R["ra%d" % t].at[sc]
                out.append(((_GWIN[t](R["g%d" % t], sc, 1 - c), land, (x, y, 1 - c)), land))
        return out
    return _Plan(4 * len(ts), copies)


def _rs_chips_plan(ts):
    def copies(sk, R):
        _, _, c = _coords()
        out = []
        for t in ts:
            for j, ((px, py), ps) in enumerate(_other_chips(sk)):
                land = R["rb%d" % t].at[j]
                out.append(((R["hb%d" % t].at[ps], land, (px, py, c)), land))
        return out
    return _Plan(3 * len(ts), copies)


def _rs_share_plan(ts):
    def copies(sk, R):
        x, y, c = _coords()
        out = []
        for t in ts:
            rows = HALF_SHAPES[t][0]
            mine = R["f%d" % t].at[_rows(rows * c, rows), :]
            out.append(((mine, mine, (x, y, 1 - c)), R["f%d" % t].at[_rows(rows * (1 - c), rows), :]))
        return out
    return _Plan(len(ts), copies)


def _half_tiling(t):
    rows, cols = HALF_SHAPES[t]
    if t == 0:
        return (256, cols), rows // 256, lambda i: (i, 0)
    if t == 1:
        return (rows, 256), cols // 256, lambda i: (0, i)
    return (rows, cols), 1, lambda i: (0, 0)


def _window_block(t, sc, hc, i):
    if t == 0:
        return (PMAIN // 256) * sc + (PIECE // 512) * hc + i, 0
    if t == 1:
        return hc, 4 * sc + i
    return 2 * sc + hc, 0


def _chip_sum(g, ra, t, idx, name):
    rows, cols = HALF_SHAPES[t]
    blk, nblk, inner = _half_tiling(t)

    def kern(idx_ref, g_ref, r_ref, hb_ref, hf_ref):
        v = g_ref[...].astype(F32) + r_ref[0].astype(F32)
        hb_ref[0] = v.astype(BF16)

        @pl.when(pl.program_id(1) == idx_ref[0])
        def _():
            hf_ref[...] = v

    gmap = lambda i, sc, idx_ref: _window_block(t, sc, idx_ref[1], i)
    omap = lambda i, sc, idx_ref: (sc,) + inner(i)
    grid_spec = pltpu.PrefetchScalarGridSpec(
        num_scalar_prefetch=1, grid=(nblk, 4),
        in_specs=[pl.BlockSpec(blk, gmap), pl.BlockSpec((1,) + blk, omap)],
        out_specs=(pl.BlockSpec((1,) + blk, omap), pl.BlockSpec(blk, lambda i, sc, idx_ref: inner(i))))
    return pl.pallas_call(
        kern, grid_spec=grid_spec,
        out_shape=(jax.ShapeDtypeStruct((4, rows, cols), BF16), jax.ShapeDtypeStruct((rows, cols), F32)),
        name=name, compiler_params=_cp(("parallel", "arbitrary")),
    )(idx, g, ra)


def _final_sum(hf, rb, t, idx, name):
    rows, cols = HALF_SHAPES[t]
    blk, nblk, inner = _half_tiling(t)
    nbr = rows // blk[0]

    def kern(idx_ref, h_ref, r_ref, o_ref):
        o_ref[...] = ((h_ref[...] + r_ref[0].astype(F32)) + r_ref[1].astype(F32)) + r_ref[2].astype(F32)

    def omap(i, idx_ref):
        r, cidx = inner(i)
        return nbr * idx_ref[1] + r, cidx

    grid_spec = pltpu.PrefetchScalarGridSpec(
        num_scalar_prefetch=1, grid=(nblk,),
        in_specs=[pl.BlockSpec(blk, lambda i, idx_ref: inner(i)),
                  pl.BlockSpec((3,) + blk, lambda i, idx_ref: (0,) + inner(i))],
        out_specs=pl.BlockSpec(blk, omap))
    return pl.pallas_call(
        kern, grid_spec=grid_spec, out_shape=jax.ShapeDtypeStruct((2 * rows, cols), F32),
        name=name, compiler_params=_cp(("parallel",)),
    )(idx, hf, rb)


class _ReduceScatter:
    def __init__(self, ts, grads, idx, tag):
        self.ts, self.idx, self.tag = ts, idx, tag
        arr = {}
        for t in ts:
            arr["g%d" % t] = grads[t]
            arr["ra%d" % t] = lax.empty((4,) + HALF_SHAPES[t], BF16)
        self.plan = _rs_sibling_plan(ts)
        self.arr, self.sems, self.token = _split_call("rs_sibling_start_" + tag, arr, start=self.plan)

    def chips(self, after):
        arr, _, _ = _split_call("rs_sibling_wait_" + self.tag, self.arr, wait=self.plan, wait_sems=self.sems, after=after)
        brr, self.hf = {}, {}
        for t in self.ts:
            hb, self.hf[t] = _chip_sum(arr["g%d" % t], arr["ra%d" % t], t, self.idx, "chip_sum_%d" % t)
            brr["hb%d" % t] = hb
            brr["rb%d" % t] = lax.empty((3,) + HALF_SHAPES[t], BF16)
        self.plan = _rs_chips_plan(self.ts)
        self.arr, self.sems, self.token = _split_call("rs_chips_start_" + self.tag, brr, start=self.plan)
        return self.token

    def share(self, after):
        brr, _, _ = _split_call("rs_chips_wait_" + self.tag, self.arr, wait=self.plan, wait_sems=self.sems, after=after)
        frr = {"f%d" % t: _final_sum(self.hf[t], brr["rb%d" % t], t, self.idx, "final_sum_%d" % t) for t in self.ts}
        self.plan = _rs_share_plan(self.ts)
        self.arr, self.sems, self.token = _split_call("rs_share_start_" + self.tag, frr, start=self.plan)
        return self.token

    def result(self, after):
        frr, _, _ = _split_call("rs_share_wait_" + self.tag, self.arr, wait=self.plan, wait_sems=self.sems, after=after)
        return {t: frr["f%d" % t] for t in self.ts}


def _all8_plan(key):
    def copies(sk, R):
        x, y, c = _coords()
        own = R[key].at[4 * x + 2 * y + c]
        out = []
        for k in range(1, 8):
            dev = ((1 - x) if (k >> 2) & 1 else x, (1 - y) if (k >> 1) & 1 else y, (1 - c) if k & 1 else c)
            out.append(((own, own, dev), R[key].at[4 * dev[0] + 2 * dev[1] + dev[2]]))
        return out
    return _Plan(7, copies)


def _small_all_gather(v):
    def body(v_ref, o_ref, send_sems, recv_sems, loc_sem):
        x, y, c = _coords()
        me = 4 * x + 2 * y + c
        lc = pltpu.make_async_copy(v_ref, o_ref.at[me], loc_sem)
        lc.start()
        cps = []
        for k in range(1, 8):
            fx, fy, fc = (k >> 2) & 1, (k >> 1) & 1, k & 1
            dev = ((1 - x) if fx else x, (1 - y) if fy else y, (1 - c) if fc else c)
            cp = pltpu.make_async_remote_copy(src_ref=v_ref, dst_ref=o_ref.at[me], send_sem=send_sems.at[k - 1],
                                              recv_sem=recv_sems.at[k - 1], device_id=dev, device_id_type=MESH)
            cp.start()
            cps.append((cp, 4 * dev[0] + 2 * dev[1] + dev[2]))
        for k, (cp, frm) in enumerate(cps):
            got = o_ref.at[frm]
            pltpu.make_async_remote_copy(src_ref=got, dst_ref=got, send_sem=send_sems.at[k], recv_sem=recv_sems.at[k],
                                         device_id=(x, y, c), device_id_type=MESH).wait_recv()
        for cp, _ in cps:
            cp.wait_send()
        lc.wait()

    hbm = pl.BlockSpec(memory_space=HBM)
    return pl.pallas_call(
        body, out_shape=jax.ShapeDtypeStruct((8,) + v.shape, F32), in_specs=[hbm], out_specs=hbm,
        scratch_shapes=[pltpu.SemaphoreType.DMA((7,)), pltpu.SemaphoreType.DMA((7,)), pltpu.SemaphoreType.DMA(())],
        name="small_all_gather", compiler_params=pltpu.CompilerParams(has_side_effects=True),
    )(v)


def _sum8(v, name="small_sum"):
    def kern(v_ref, o_ref):
        acc = v_ref[0]
        for k in range(1, 8):
            acc = acc + v_ref[k]
        o_ref[...] = acc

    return pl.pallas_call(kern, out_shape=jax.ShapeDtypeStruct(v.shape[1:], F32), name=name)(v)


def _adamw(w, g, m, v, name, tr=128, blk0=0, nblk=None, into=None, copy_g=False):
    R, C = w.shape
    tr = min(tr, R)
    if nblk is None:
        assert R % tr == 0 and blk0 == 0
        nblk = R // tr
    n_out = 4 if copy_g else 3

    def kern(*refs):
        w_ref, g_ref, m_ref, v_ref = refs[:4]
        d_ref, mo_ref, vo_ref = refs[-n_out:][:3]
        gv = g_ref[...]
        mn = ADAM_B1 * m_ref[...] + (1.0 - ADAM_B1) * gv
        vn = ADAM_B2 * v_ref[...] + (1.0 - ADAM_B2) * (gv * gv)
        m_hat = mn / (1.0 - ADAM_B1 ** ADAM_STEP)
        v_hat = vn / (1.0 - ADAM_B2 ** ADAM_STEP)
        d_ref[...] = -ADAM_LR * (m_hat / (jnp.sqrt(v_hat) + ADAM_EPS) + ADAM_WD * w_ref[...])
        mo_ref[...] = mn
        vo_ref[...] = vn
        if copy_g:
            refs[-1][...] = gv

    blk = pl.BlockSpec((tr, C), lambda i: (blk0 + i, 0))
    sd = jax.ShapeDtypeStruct((R, C), F32)
    in_specs, args, aliases = [blk] * 4, [w, g, m, v], {}
    if into is not None:
        in_specs += [pl.BlockSpec(memory_space=pl.ANY)] * 3
        args += list(into)
        aliases = {4: 0, 5: 1, 6: 2}
    return pl.pallas_call(kern, grid=(nblk,), in_specs=in_specs, out_specs=(blk,) * n_out, out_shape=(sd,) * n_out,
                          input_output_aliases=aliases, name=name, compiler_params=_cp(("parallel",)))(*args)


def _to_piece(wt, s):
    z = lambda n: jnp.zeros((n, D), wt.dtype)
    pads = [functools.partial(lambda k, w: jnp.pad(w, ((8 * k, PIECE - W_SHARD - 8 * k), (0, 0))), k) for k in range(3)]
    last = lambda w: jnp.concatenate([z(24), w[:744], w[776:], w[744:776], z(PIECE - 24 - W_SHARD)], axis=0)
    return lax.switch(s, pads + [last], wt)


def _from_piece(p, s):
    cuts = [functools.partial(lambda k, q: q[8 * k:8 * k + W_SHARD], k) for k in range(3)]
    last = lambda q: jnp.concatenate([q[24:768], q[2816:2848], q[768:2816]], axis=0)
    return lax.switch(s, cuts + [last], p)


_SMALL = [("b_gate", 2048), ("ssm_conv_b", 4096), ("dt_bias", 32), ("A_log", 32), ("D_skip", 32),
          ("ssm_norm_w", 2048), ("norm_mlp", 1024), ("norm_final", 1024), ("sc_conv_w", 3072), ("ssm_conv_w", 16384),
          ("loss", 1)]


def _pack(vals, table, rows):
    parts = []
    for name, n in table:
        v = vals[name].reshape(-1).astype(F32)
        pad = (-n) % 128
        parts.append(jnp.pad(v, (0, pad)) if pad else v)
    flat = jnp.concatenate(parts)
    return jnp.pad(flat, (0, rows * 128 - flat.shape[0])).reshape(rows, 128)


def _unpack(arr, table):
    flat = arr.reshape(-1)
    out, off = {}, 0
    for name, n in table:
        out[name] = flat[off:off + n]
        off += n + ((-n) % 128)
    return out


def kernel(x, norm_mix, w_in, b_gate, sc_conv_w, ssm_conv_w, ssm_conv_b, dt_bias, A_log, D_skip, ssm_norm_w, w_branch_sc, w_branch_ssm, w_out, norm_mlp, w_mlp1, w_mlp2, norm_final, loss_target, m_norm_mix, m_w_in, m_b_gate, m_sc_conv_w, m_ssm_conv_w, m_ssm_conv_b, m_dt_bias, m_A_log, m_D_skip, m_ssm_norm_w, m_w_branch_sc, m_w_branch_ssm, m_w_out, m_norm_mlp, m_w_mlp1, m_w_mlp2, m_norm_final, v_norm_mix, v_w_in, v_b_gate, v_sc_conv_w, v_ssm_conv_w, v_ssm_conv_b, v_dt_bias, v_A_log, v_D_skip, v_ssm_norm_w, v_w_branch_sc, v_w_branch_ssm, v_w_out, v_norm_mlp, v_w_mlp1, v_w_mlp2, v_norm_final):
    L = x.shape[1]
    nc = L // Q
    xi, yi, ci = lax.axis_index("x"), lax.axis_index("y"), lax.axis_index("c")
    s = 2 * xi + yi
    idx = jnp.stack([s, ci]).astype(jnp.int32)
    x0 = x.reshape(L, D)
    tgt = loss_target.reshape(L, D)

    piece = _to_piece(w_in.T, s)
    nb = PMAIN // XTRA
    wct0 = _place(piece, (NCW, D), (XTRA, D), lambda i, r: (nb * r[0] + i, 0), idx, "place_wct", nblk=nb)
    xt0 = _place(piece, (4, XTRA, D), (1, XTRA, D), lambda i, r: (r[0], 0, 0), idx, "place_xt", blk0=nb, nblk=1)
    cws = jnp.zeros((8, 1280), F32)
    cws = cws.at[0:3, 0:256].set(sc_conv_w).at[0:4, 256:1280].set(ssm_conv_w)
    cw0 = lax.dynamic_update_slice(jnp.zeros((4, 8, 1280), F32), cws[None], (s, 0, 0))
    win_keys, win2_keys, mid_keys, end_keys = ["xt", "cw", "wq0"], ["wq1"], ["wa", "wb", "wo", "w1"], ["w2"]
    gw, sems_w, tok = _split_call("ag_win_start", {"wct": wct0, "xt": xt0, "cw": cw0}, start=_ag_chips_plan(win_keys))
    g2, sems_w2, tok = _split_call("ag_win2_start", {"wct": gw["wct"]}, start=_ag_chips_plan(win2_keys), after=tok)
    gw["wct"] = g2["wct"]
    wa0 = _place(w_branch_sc, (D, D), (256, 1024), lambda i, r: (r[0], 0), idx, "place_wa", dep=tok)
    wb0 = _place(w_branch_ssm, (INNER, D), (512, 1024), lambda i, r: (r[0], 0), idx, "place_wb", dep=tok)
    wo0 = _place(w_out, (D, D), (256, 1024), lambda i, r: (r[0], 0), idx, "place_wo", dep=tok)
    w10 = _place(w_mlp1, (D, DFF), (256, 1024), lambda i, r: (i, r[0]), idx, "place_w1", dep=tok)
    gm, sems_m, tok = _split_call("ag_mid_start", {"wa": wa0, "wb": wb0, "wo": wo0, "w1": w10},
                                  start=_ag_chips_plan(mid_keys))
    w20 = _place(w_mlp2, (DFF, D), (256, 1024), lambda i, r: (4 * r[0] + i, 0), idx, "place_w2", dep=tok)
    ge, sems_e, tok = _split_call("ag_end_start", {"w2": w20}, start=_ag_chips_plan(end_keys))
    h = _rms_fwd(x0, norm_mix, "rms_mix", dep=tok)
    gw, sems_w, tok = _split_call("ag_win_pass", gw, wait=_ag_chips_plan(win_keys), wait_sems=sems_w,
                                  start=_ag_sibling_plan(win_keys), after=h)
    gw, _, _ = _split_call("ag_win_done", gw, wait=_ag_sibling_plan(win_keys), wait_sems=sems_w, after=tok)
    wc, cw_all = _fix_wct(gw["wct"], gw["xt"]), gw["cw"]
    sc_w_full = jnp.concatenate([cw_all[k, :, 0:256] for k in range(4)], axis=1)
    ssm_w_full = jnp.concatenate([cw_all[k, :, 256:1280] for k in range(4)], axis=1)
    cw4 = ssm_w_full.at[4].set(ssm_conv_b)
    vec = jnp.zeros((8, 128), F32).at[0, :NH].set(dt_bias).at[1, :NH].set(A_log)
    vecg = jnp.zeros((NG, 8, 128), F32).at[:, 0, :4].set(A_log.reshape(NG, 4)).at[:, 1, :4].set(D_skip.reshape(NG, 4))

    dtraw = _matmul(h, wc[C_DT:], "nt", F32, 512, 256, 1024, "in_proj_dt")
    proj = _in_proj_wave(h, wc, 0)
    g2, sems_w2, tok = _split_call("ag_win2_pass", {"wct": wc}, wait=_ag_chips_plan(win2_keys), wait_sems=sems_w2,
                                   start=_ag_sibling_plan(win2_keys), after=[proj, dtraw])
    g2, _, _ = _split_call("ag_win2_done", g2, wait=_ag_sibling_plan(win2_keys), wait_sems=sems_w2, after=tok)
    wc = g2["wct"]
    proj = _in_proj_wave(h, wc, 1, proj=proj)
    ya = _sc_fwd(proj, sc_w_full)
    xbc = _ssm_conv_fwd(proj, cw4)
    dt4, cs4, sg4 = _dt_prep(dtraw, vec)
    gm, sems_m, tok = _split_call("ag_mid_pass", gm, wait=_ag_chips_plan(mid_keys), wait_sems=sems_m,
                                  start=_ag_sibling_plan(mid_keys), after=[xbc, ya, dt4])
    xbc = _tie(xbc, tok, "tie_xbc")
    y, s_all = _ssd_fwd(xbc, dt4, cs4, vecg)
    yb = _gnorm_fwd(y, proj, ssm_norm_w)
    gm, _, _ = _split_call("ag_mid_done", gm, wait=_ag_sibling_plan(mid_keys), wait_sems=sems_m, after=yb)
    wa, wb, wo, w1 = gm["wa"], gm["wb"], gm["wo"], gm["w1"]
    ge, sems_e, tok = _split_call("ag_end_pass", ge, wait=_ag_chips_plan(end_keys), wait_sems=sems_e,
                                  start=_ag_sibling_plan(end_keys), after=yb)
    br_a = _matmul(ya, wa, "nn", F32, 1024, 1024, 1024, "branch_sc", dep=tok)
    br_b = _matmul(yb, wb, "nn", F32, 1024, 1024, 2048, "branch_ssm")
    merged = _merge_fwd(proj, b_gate, br_a, br_b)
    x1 = _matmul(merged, wo, "nn", F32, 1024, 1024, 1024, "out_proj", epi="res", extra=x0)
    h2 = _rms_fwd(x1, norm_mlp, "rms_mlp")
    a1, rl = _matmul(h2, w1, "nn", BF16, 1024, 1024, 1024, "mlp1", epi="relu2", n_outer=True)
    ge, _, _ = _split_call("ag_end_done", ge, wait=_ag_sibling_plan(end_keys), wait_sems=sems_e, after=a1)
    w2 = ge["w2"]
    x2 = _matmul(rl, w2, "nn", F32, 512, 1024, 4096, "mlp2", epi="res", extra=x1)
    dx2, g_nf, loss8 = _final(x2, norm_final, tgt)

    da = _matmul(dx2, w2, "nt", BF16, 1024, 1024, 1024, "mlp2_dx", epi="drelu", extra=a1, n_outer=True)
    g_w2 = _matmul(rl, dx2, "tn", BF16, 1024, 1024, 2048, "mlp2_dw")
    g_w1 = _matmul(h2, da, "tn", BF16, 1024, 1024, 2048, "mlp1_dw")
    dh2 = _matmul(da, w1, "nt", F32, 512, 1024, 4096, "mlp1_dx")
    dx1, g_nmlp = _rms_bwd(dh2, x1, norm_mlp, dx2, "rms_mlp_bwd")
    dmerged = _matmul(dx1, wo, "nt", F32, 1024, 1024, 1024, "out_proj_dx")
    g_wo = _matmul(merged, dx1, "tn", BF16, 1024, 1024, 2048, "out_proj_dw")
    dproj = lax.empty((L, NCW), BF16)
    dbr, dproj, g_bg = _merge_bwd(dmerged, proj, b_gate, br_a, br_b, dproj)
    dya = _matmul(dbr[0], wa, "nt", F32, 1024, 1024, 1024, "branch_sc_dx")
    g_wa = _matmul(ya, dbr[0], "tn", BF16, 1024, 1024, 2048, "branch_sc_dw")
    dproj, g_scw = _sc_bwd(dya, proj, sc_w_full, dproj)
    dyb = _matmul(dbr[1], wb, "nt", F32, 1024, 1024, 1024, "branch_ssm_dx", n_outer=True)
    g_wb = _matmul(yb, dbr[1], "tn", BF16, 1024, 1024, 2048, "branch_ssm_dw")
    rs_a = _ReduceScatter([1, 2, 3, 4, 5], {1: g_w1, 2: g_w2, 3: g_wa, 4: g_wb, 5: g_wo}, idx, "a")
    dy, dproj, g_snw = _gnorm_bwd(_tie(dyb, rs_a.token, "tie_dyb"), y, proj, ssm_norm_w, dproj)
    tok = rs_a.chips(after=dy)
    dxs, dbm, dcm, ddt_g, st = _ssd_bwd(xbc, dt4, cs4, sg4, vecg, s_all, _tie(dy, tok, "tie_dy"))
    dproj, gx1 = _ssm_conv_bwd(dxs, proj, cw4, dproj, 0, "ssm_conv_bwd_x")
    dproj, gx2 = _ssm_conv_bwd(dbm, proj, cw4, dproj, INNER, "ssm_conv_bwd_b")
    dproj, gx3 = _ssm_conv_bwd(dcm, proj, cw4, dproj, INNER + NG * NS, "ssm_conv_bwd_c")
    g_cw4 = jnp.concatenate([gx1, gx2, gx3], axis=1)
    dproj, g_dtb = _dt_bwd(ddt_g, dproj)
    small = {"b_gate": g_bg[0], "ssm_conv_b": g_cw4[4], "dt_bias": g_dtb[0, :NH],
             "A_log": st[:, 0, :4], "D_skip": st[:, 1, :4], "ssm_norm_w": g_snw[0], "norm_mlp": g_nmlp[0],
             "norm_final": g_nf[0], "sc_conv_w": g_scw[0:3], "ssm_conv_w": g_cw4[0:4], "loss": loss8[0, 0:1]}
    small_sum = _sum8(_small_all_gather(_pack(small, _SMALL, SMALL_ROWS)))
    gs = _unpack(small_sum, _SMALL)
    g_wc = _matmul(dproj, h, "tn", BF16, 1280, 1024, 2048, "in_proj_dw", dep=small_sum)
    rs_b = _ReduceScatter([0], {0: g_wc}, idx, "b")
    tok = rs_a.share(after=rs_b.token)
    tok = rs_b.chips(after=tok)
    dh = _matmul(dproj, wc, "nn", F32, 512, 1024, 5760, "in_proj_dx", dep=tok)
    grad_x, g_nm = _rms_bwd(dh, x0, norm_mix, dx1, "rms_mix_bwd")
    me = 4 * xi + 2 * yi + ci
    nm8 = lax.dynamic_update_slice(jnp.zeros((8, 8, 128), F32), g_nm[0].reshape(1, 8, 128), (me, 0, 0))
    nm_arr, nm_sems, tok = _split_call("norm_mix_start", {"nm": nm8}, start=_all8_plan("nm"))
    red = rs_a.result(after=tok)
    big = {"w_mlp1": red[1], "w_mlp2": red[2], "w_branch_sc": red[3], "w_branch_ssm": red[4], "w_out": red[5]}

    given = dict(norm_mix=norm_mix, w_in=w_in, b_gate=b_gate, sc_conv_w=sc_conv_w, ssm_conv_w=ssm_conv_w, ssm_conv_b=ssm_conv_b, dt_bias=dt_bias, A_log=A_log, D_skip=D_skip, ssm_norm_w=ssm_norm_w, w_branch_sc=w_branch_sc, w_branch_ssm=w_branch_ssm, w_out=w_out, norm_mlp=norm_mlp, w_mlp1=w_mlp1, w_mlp2=w_mlp2, norm_final=norm_final,
                 m_norm_mix=m_norm_mix, m_w_in=m_w_in, m_b_gate=m_b_gate, m_sc_conv_w=m_sc_conv_w, m_ssm_conv_w=m_ssm_conv_w, m_ssm_conv_b=m_ssm_conv_b, m_dt_bias=m_dt_bias, m_A_log=m_A_log, m_D_skip=m_D_skip, m_ssm_norm_w=m_ssm_norm_w, m_w_branch_sc=m_w_branch_sc, m_w_branch_ssm=m_w_branch_ssm, m_w_out=m_w_out, m_norm_mlp=m_norm_mlp, m_w_mlp1=m_w_mlp1, m_w_mlp2=m_w_mlp2, m_norm_final=m_norm_final,
                 v_norm_mix=v_norm_mix, v_w_in=v_w_in, v_b_gate=v_b_gate, v_sc_conv_w=v_sc_conv_w, v_ssm_conv_w=v_ssm_conv_w, v_ssm_conv_b=v_ssm_conv_b, v_dt_bias=v_dt_bias, v_A_log=v_A_log, v_D_skip=v_D_skip, v_ssm_norm_w=v_ssm_norm_w, v_w_branch_sc=v_w_branch_sc, v_w_branch_ssm=v_w_branch_ssm, v_w_out=v_w_out, v_norm_mlp=v_norm_mlp, v_w_mlp1=v_w_mlp1, v_w_mlp2=v_w_mlp2, v_norm_final=v_norm_final)
    order = ["norm_mix", "w_in", "b_gate", "sc_conv_w", "ssm_conv_w", "ssm_conv_b", "dt_bias", "A_log", "D_skip",
             "ssm_norm_w", "w_branch_sc", "w_branch_ssm", "w_out", "norm_mlp", "w_mlp1", "w_mlp2", "norm_final"]
    grad, delta, new_m, new_v = {}, {}, {}, {}
    for n in big:
        delta[n], new_m[n], new_v[n], grad[n] = _adamw(given[n], big[n], given["m_" + n], given["v_" + n],
                                                       "adamw_" + n, copy_g=True)
    big["w_in"] = None
    grad_small = {n: gs[n].reshape(given[n].shape) for n in order
                  if n not in big and n not in ("sc_conv_w", "ssm_conv_w", "norm_mix")}
    grad_small["sc_conv_w"] = lax.dynamic_slice(gs["sc_conv_w"].reshape(3, D), (0, 256 * s), (3, 256))
    grad_small["ssm_conv_w"] = lax.dynamic_slice(gs["ssm_conv_w"].reshape(4, XBC), (0, 1024 * s), (4, 1024))
    table = [(n, int(grad_small[n].size)) for n in grad_small]
    rows = 136
    pk = lambda d: _pack(d, table, rows)
    ds_, ms_, vs_ = _adamw(pk({n: given[n] for n in grad_small}), pk(grad_small), pk({n: given["m_" + n] for n in grad_small}),
                           pk({n: given["v_" + n] for n in grad_small}), "adamw_small", tr=rows)
    ds_, ms_, vs_ = _unpack(ds_, table), _unpack(ms_, table), _unpack(vs_, table)
    for n in grad_small:
        shp = given[n].shape
        grad[n] = grad_small[n]
        delta[n], new_m[n], new_v[n] = ds_[n].reshape(shp), ms_[n].reshape(shp), vs_[n].reshape(shp)

    done = [new_v[n] for n in ("w_mlp1", "w_mlp2", "w_branch_sc", "w_branch_ssm", "w_out")] + [vs_["b_gate"]]
    tok = rs_b.share(after=done)
    gp = rs_b.result(after=tok)[0]
    gwt = _from_piece(gp, s)
    wt_args = (w_in.T, gwt, m_w_in.T, v_w_in.T)
    head = _adamw(*wt_args, "adamw_w_in", tr=256, nblk=W_SHARD // 256)
    dt_, mt_, vt_ = _adamw(*wt_args, "adamw_w_in_tail", tr=8, blk0=(W_SHARD // 256) * 32, nblk=1, into=head)
    grad["w_in"], delta["w_in"], new_m["w_in"], new_v["w_in"] = gwt.T, dt_.T, mt_.T, vt_.T
    nm_arr, _, _ = _split_call("norm_mix_wait", nm_arr, wait=_all8_plan("nm"), wait_sems=nm_sems, after=tok)
    g8 = _sum8(nm_arr["nm"], "norm_mix_sum")
    r8 = lambda a: a.reshape(8, 128)
    d8, m8, v8 = _adamw(r8(norm_mix), g8, r8(m_norm_mix), r8(v_norm_mix), "adamw_norm_mix", tr=8)
    grad["norm_mix"], delta["norm_mix"] = g8.reshape(D), d8.reshape(D)
    new_m["norm_mix"], new_v["norm_mix"] = m8.reshape(D), v8.reshape(D)

    loss = gs["loss"].reshape(())
    return (loss, grad_x.reshape(1, L, D), *[grad[n] for n in order], *[delta[n] for n in order],
            *[new_m[n] for n in order], *[new_v[n] for n in order])
```

```python
import functools

import jax
import jax.numpy as jnp
from jax import lax
from jax.experimental import pallas as pl
from jax.experimental.pallas import tpu as pltpu

F32 = jnp.float32
BF16 = jnp.bfloat16
MESH = pl.DeviceIdType.MESH
HBM = pltpu.HBM

D = 1024
INNER = 2048
HD = 64
NH = 32
NG = 8
NS = 128
Q = 128
GPS = 4
XBC = 4096
DFF = 4096
EPS = 1e-6
W_SHARD = 2824
NCW = 11520
PIECE = 3072
PMAIN = 2816
C_Z, C_XBC, C_GATE, C_DT = 3072, 5120, 9216, 11264
SMALL_ROWS = 256
VMEM_LIMIT = 56 * 1024 * 1024

ADAM_LR, ADAM_B1, ADAM_B2, ADAM_EPS, ADAM_WD, ADAM_STEP = 0.001, 0.9, 0.999, 1e-08, 0.01, 10


def _cp(sem=None, vmem=VMEM_LIMIT):
    return pltpu.CompilerParams(dimension_semantics=sem, vmem_limit_bytes=vmem)


def _sigmoid(v):
    return 1.0 / (1.0 + jnp.exp(-v))


_DIMS = {"nn": (((1,), (0,)), ((), ())), "nt": (((1,), (1,)), ((), ())), "tn": (((0,), (0,)), ((), ()))}


def _matmul(a, b, mode, out_dtype, tm, tn, tk, name, epi=None, extra=None, n_outer=False, dep=None):
    if mode == "tn":
        K, M = a.shape
    else:
        M, K = a.shape
    N = b.shape[0] if mode == "nt" else b.shape[1]
    tm, tn, tk = min(tm, M), min(tn, N), min(tk, K)
    assert M % tm == 0 and N % tn == 0 and K % tk == 0, (name, M, N, K, tm, tn, tk)
    nm, nn, nk = M // tm, N // tn, K // tk
    dims = _DIMS[mode]

    def ij(p0, p1):
        return (p1, p0) if n_outer else (p0, p1)

    if mode == "tn":
        a_spec = pl.BlockSpec((tk, tm), lambda p0, p1, k: (k, ij(p0, p1)[0]))
    else:
        a_spec = pl.BlockSpec((tm, tk), lambda p0, p1, k: (ij(p0, p1)[0], k))
    if mode == "nt":
        b_spec = pl.BlockSpec((tn, tk), lambda p0, p1, k: (ij(p0, p1)[1], k))
    else:
        b_spec = pl.BlockSpec((tk, tn), lambda p0, p1, k: (k, ij(p0, p1)[1]))
    o_spec = pl.BlockSpec((tm, tn), lambda p0, p1, k: ij(p0, p1))
    in_specs = [a_spec, b_spec]
    args = [a, b]
    if epi in ("res", "drelu"):
        in_specs.append(o_spec)
        args.append(extra)
    if dep is not None:
        in_specs.append(pl.BlockSpec(memory_space=pl.ANY))
        args.append(dep)
    n_in = len(args)
    if epi == "relu2":
        out_shape = (jax.ShapeDtypeStruct((M, N), out_dtype), jax.ShapeDtypeStruct((M, N), BF16))
        out_specs = (o_spec, o_spec)
    else:
        out_shape = jax.ShapeDtypeStruct((M, N), out_dtype)
        out_specs = o_spec

    def kern(*refs):
        a_ref, b_ref = refs[0], refs[1]
        e_ref = refs[2] if epi in ("res", "drelu") else None
        acc = refs[-1]
        outs = refs[n_in:-1] if nk > 1 else refs[n_in:]
        k = pl.program_id(2)

        def product():
            return lax.dot_general(a_ref[...].astype(BF16), b_ref[...].astype(BF16), dims, preferred_element_type=F32)

        def finish(r):
            if epi is None:
                outs[0][...] = r.astype(out_dtype)
            elif epi == "res":
                outs[0][...] = (r + e_ref[...]).astype(out_dtype)
            elif epi == "relu2":
                outs[0][...] = r.astype(out_dtype)
                t = jnp.maximum(r, 0.0)
                outs[1][...] = (t * t).astype(BF16)
            else:
                outs[0][...] = (r * (2.0 * jnp.maximum(e_ref[...].astype(F32), 0.0))).astype(out_dtype)

        if nk == 1:
            finish(product())
        else:
            @pl.when(k == 0)
            def _():
                acc[...] = jnp.zeros_like(acc)

            acc[...] += product()

            @pl.when(k == nk - 1)
            def _():
                finish(acc[...])

    grid = (nn, nm, nk) if n_outer else (nm, nn, nk)
    return pl.pallas_call(
        kern, grid=grid, in_specs=in_specs, out_specs=out_specs, out_shape=out_shape,
        scratch_shapes=[pltpu.VMEM((tm, tn), F32)] if nk > 1 else [], name=name,
        compiler_params=_cp(("parallel", "parallel", "arbitrary")),
    )(*args)


def _rms_fwd(x, w, name, tl=256, dep=None):
    L = x.shape[0]

    def kern(x_ref, w_ref, *rest):
        o_ref = rest[-1]
        xv = x_ref[...]
        r = lax.rsqrt(jnp.mean(xv * xv, axis=-1, keepdims=True) + EPS)
        o_ref[...] = ((xv * r) * w_ref[...]).astype(BF16)

    row = pl.BlockSpec((tl, D), lambda i: (i, 0))
    deps = [] if dep is None else [dep]
    return pl.pallas_call(
        kern, grid=(L // tl,),
        in_specs=[row, pl.BlockSpec((1, D), lambda i: (0, 0))] + [pl.BlockSpec(memory_space=pl.ANY)] * len(deps),
        out_specs=row, out_shape=jax.ShapeDtypeStruct((L, D), BF16), name=name, compiler_params=_cp(("parallel",)),
    )(x, w.reshape(1, D), *deps)


def _rms_bwd(dy, x, w, res, name, tl=256, dep=None):
    L = x.shape[0]
    deps = [] if dep is None else [dep]

    def kern(dy_ref, x_ref, w_ref, res_ref, *rest):
        dx_ref, gw_ref = rest[-2:]
        @pl.when(pl.program_id(0) == 0)
        def _():
            gw_ref[...] = jnp.zeros_like(gw_ref)

        xv = x_ref[...]
        dyv = dy_ref[...]
        r = lax.rsqrt(jnp.mean(xv * xv, axis=-1, keepdims=True) + EPS)
        xn = xv * r
        gw_ref[...] += jnp.broadcast_to(jnp.sum(dyv * xn, axis=0, keepdims=True), (8, D))
        dxn = dyv * w_ref[...]
        dx_ref[...] = res_ref[...] + r * (dxn - xn * jnp.mean(dxn * xn, axis=-1, keepdims=True))

    row = pl.BlockSpec((tl, D), lambda i: (i, 0))
    return pl.pallas_call(
        kern, grid=(L // tl,),
        in_specs=[row, row, pl.BlockSpec((1, D), lambda i: (0, 0)), row] + [pl.BlockSpec(memory_space=pl.ANY)] * len(deps),
        out_specs=(row, pl.BlockSpec((8, D), lambda i: (0, 0))),
        out_shape=(jax.ShapeDtypeStruct((L, D), F32), jax.ShapeDtypeStruct((8, D), F32)),
        name=name, compiler_params=_cp(("arbitrary",)),
    )(dy, x, w.reshape(1, D), res, *deps)


def _final(x2, w, tgt, tl=256):
    L = x2.shape[0]

    def kern(x_ref, w_ref, t_ref, dx_ref, gw_ref, loss_ref):
        @pl.when(pl.program_id(0) == 0)
        def _():
            gw_ref[...] = jnp.zeros_like(gw_ref)
            loss_ref[...] = jnp.zeros_like(loss_ref)

        xv = x_ref[...]
        r = lax.rsqrt(jnp.mean(xv * xv, axis=-1, keepdims=True) + EPS)
        xn = xv * r
        e = xn * w_ref[...] - t_ref[...]
        per_tok = jnp.mean(e * e, axis=-1, keepdims=True)
        loss_ref[...] += 0.5 * jnp.sum(per_tok)
        dyv = e * (1.0 / D)
        gw_ref[...] += jnp.broadcast_to(jnp.sum(dyv * xn, axis=0, keepdims=True), (8, D))
        dxn = dyv * w_ref[...]
        dx_ref[...] = r * (dxn - xn * jnp.mean(dxn * xn, axis=-1, keepdims=True))

    row = pl.BlockSpec((tl, D), lambda i: (i, 0))
    return pl.pallas_call(
        kern, grid=(L // tl,), in_specs=[row, pl.BlockSpec((1, D), lambda i: (0, 0)), row],
        out_specs=(row, pl.BlockSpec((8, D), lambda i: (0, 0)), pl.BlockSpec((8, 128), lambda i: (0, 0))),
        out_shape=(jax.ShapeDtypeStruct((L, D), F32), jax.ShapeDtypeStruct((8, D), F32),
                   jax.ShapeDtypeStruct((8, 128), F32)),
        name="final_norm_loss", compiler_params=_cp(("arbitrary",)),
    )(x2, w.reshape(1, D), tgt)


def _down(v, k):
    if k == 0:
        return v
    t = lax.broadcasted_iota(jnp.int32, v.shape, 0)
    return jnp.where(t >= k, pltpu.roll(v, k, axis=0), 0.0)


def _up(v, k):
    if k == 0:
        return v
    n = v.shape[0]
    t = lax.broadcasted_iota(jnp.int32, v.shape, 0)
    return jnp.where(t < n - k, pltpu.roll(v, n - k, axis=0), 0.0)


TW = 256


def _sc_fwd(proj, cw):
    L = proj.shape[0]
    nb = D // TW

    def kern(b_ref, c_ref, x_ref, w_ref, o_ref):
        u = c_ref[...].astype(F32) * x_ref[...].astype(F32)
        w = w_ref[...]
        cv = w[0:1] * _down(u, 2) + w[1:2] * _down(u, 1) + w[2:3] * u
        o_ref[...] = (b_ref[...].astype(F32) * cv).astype(BF16)

    col = lambda off: pl.BlockSpec((L, TW), lambda j: (0, off + j))
    return pl.pallas_call(
        kern, grid=(nb,), in_specs=[col(0), col(nb), col(2 * nb), pl.BlockSpec((8, TW), lambda j: (0, j))],
        out_specs=pl.BlockSpec((L, TW), lambda j: (0, j)), out_shape=jax.ShapeDtypeStruct((L, D), BF16),
        name="sc_fwd", compiler_params=_cp(("parallel",)),
    )(proj, proj, proj, cw)


def _sc_bwd(dya, proj, cw, dproj):
    L = proj.shape[0]
    nb = D // TW

    def kern(d_ref, b_ref, c_ref, x_ref, w_ref, _, dp_ref, gw_ref, keep):
        sec = pl.program_id(1)

        @pl.when(sec == 0)
        def _():
            cs, xs, dyv = c_ref[...].astype(F32), x_ref[...].astype(F32), d_ref[...]
            w = w_ref[...]
            u = cs * xs
            u1, u2 = _down(u, 1), _down(u, 2)
            cv = w[0:1] * u2 + w[1:2] * u1 + w[2:3] * u
            dcv = dyv * b_ref[...].astype(F32)
            du = w[2:3] * dcv + w[1:2] * _up(dcv, 1) + w[0:1] * _up(dcv, 2)
            g0 = jnp.sum(dcv * u2, axis=0, keepdims=True)
            g1 = jnp.sum(dcv * u1, axis=0, keepdims=True)
            g2 = jnp.sum(dcv * u, axis=0, keepdims=True)
            row = lax.broadcasted_iota(jnp.int32, (8, TW), 0)
            gw_ref[...] = jnp.where(row == 0, g0, jnp.where(row == 1, g1, jnp.where(row == 2, g2, 0.0)))
            dp_ref[...] = (dyv * cv).astype(BF16)
            keep[0] = (du * xs).astype(BF16)
            keep[1] = (du * cs).astype(BF16)

        @pl.when(sec > 0)
        def _():
            dp_ref[...] = keep[sec - 1]

    col = lambda off: pl.BlockSpec((L, TW), lambda j, s: (0, off + j))
    return pl.pallas_call(
        kern, grid=(nb, 3),
        in_specs=[col(0), col(0), col(nb), col(2 * nb), pl.BlockSpec((8, TW), lambda j, s: (0, j)),
                  pl.BlockSpec(memory_space=pl.ANY)],
        out_specs=(pl.BlockSpec((L, TW), lambda j, s: (0, s * nb + j)), pl.BlockSpec((8, TW), lambda j, s: (0, j))),
        out_shape=(jax.ShapeDtypeStruct(dproj.shape, BF16), jax.ShapeDtypeStruct((8, D), F32)),
        scratch_shapes=[pltpu.VMEM((2, L, TW), BF16)],
        input_output_aliases={5: 0}, name="sc_bwd", compiler_params=_cp(("parallel", "arbitrary")),
    )(dya, proj, proj, proj, cw, dproj)


def _ssm_conv_fwd(proj, cw4):
    L = proj.shape[0]
    off = C_XBC // TW

    def kern(r_ref, w_ref, o_ref):
        raw = r_ref[...].astype(F32)
        w = w_ref[...]
        c4 = w[0:1] * _down(raw, 3) + w[1:2] * _down(raw, 2) + w[2:3] * _down(raw, 1) + w[3:4] * raw + w[4:5]
        o_ref[...] = c4 * _sigmoid(c4)

    return pl.pallas_call(
        kern, grid=(XBC // TW,),
        in_specs=[pl.BlockSpec((L, TW), lambda j: (0, off + j)), pl.BlockSpec((8, TW), lambda j: (0, j))],
        out_specs=pl.BlockSpec((L, TW), lambda j: (0, j)), out_shape=jax.ShapeDtypeStruct((L, XBC), F32),
        name="ssm_conv_fwd", compiler_params=_cp(("parallel",)),
    )(proj, cw4)


def _ssm_conv_bwd(dx, proj, cw4, dproj, col0, name):
    L, width = dx.shape
    off_p = (C_XBC + col0) // TW
    off_w = col0 // TW

    def kern(d_ref, r_ref, w_ref, _, dp_ref, gw_ref):
        raw = r_ref[...].astype(F32)
        w = w_ref[...]
        r1, r2, r3 = _down(raw, 1), _down(raw, 2), _down(raw, 3)
        c4 = w[0:1] * r3 + w[1:2] * r2 + w[2:3] * r1 + w[3:4] * raw + w[4:5]
        sg = _sigmoid(c4)
        dc4 = d_ref[...] * (sg * (1.0 + c4 * (1.0 - sg)))
        draw = w[3:4] * dc4 + w[2:3] * _up(dc4, 1) + w[1:2] * _up(dc4, 2) + w[0:1] * _up(dc4, 3)
        dp_ref[...] = draw.astype(BF16)
        gs = [jnp.sum(dc4 * r3, axis=0, keepdims=True), jnp.sum(dc4 * r2, axis=0, keepdims=True),
              jnp.sum(dc4 * r1, axis=0, keepdims=True), jnp.sum(dc4 * raw, axis=0, keepdims=True),
              jnp.sum(dc4, axis=0, keepdims=True)]
        row = lax.broadcasted_iota(jnp.int32, (8, TW), 0)
        acc = jnp.zeros((8, TW), F32)
        for k, gk in enumerate(gs):
            acc = jnp.where(row == k, gk, acc)
        gw_ref[...] = acc

    return pl.pallas_call(
        kern, grid=(width // TW,),
        in_specs=[pl.BlockSpec((L, TW), lambda j: (0, j)), pl.BlockSpec((L, TW), lambda j: (0, off_p + j)),
                  pl.BlockSpec((8, TW), lambda j: (0, off_w + j)), pl.BlockSpec(memory_space=pl.ANY)],
        out_specs=(pl.BlockSpec((L, TW), lambda j: (0, off_p + j)), pl.BlockSpec((8, TW), lambda j: (0, j))),
        out_shape=(jax.ShapeDtypeStruct(dproj.shape, BF16), jax.ShapeDtypeStruct((8, width), F32)),
        input_output_aliases={3: 0}, name=name, compiler_params=_cp(("arbitrary",)),
    )(dx, proj, cw4, dproj)


def _split3(v):
    h1 = v.astype(BF16)
    r1 = v - h1.astype(F32)
    h2 = r1.astype(BF16)
    h3 = (r1 - h2.astype(F32)).astype(BF16)
    return h1, h2, h3


def _dot01(m01, v, dims=_DIMS["nn"], m_left=True, terms=3):
    out = None
    for part in _split3(v)[:terms]:
        ops = (m01, part) if m_left else (part, m01)
        t = lax.dot_general(ops[0], ops[1], dims, preferred_element_type=F32)
        out = t if out is None else out + t
    return out


def _bdot(a, b, mode="nn"):
    return lax.dot_general(a.astype(BF16), b.astype(BF16), _DIMS[mode], preferred_element_type=F32)


def _softplus(v):
    return jnp.maximum(v, 0.0) + jnp.log1p(jnp.exp(-jnp.abs(v)))


def _dt_prep(proj, vec):
    L = proj.shape[0]

    def kern(p_ref, v_ref, dt_ref, cs_ref, sg_ref):
        v = v_ref[...]
        pre = p_ref[:, 0:128] + v[0:1]
        dt = _softplus(pre)
        da = dt * (-jnp.exp(v[1:2]))
        ii = lax.broadcasted_iota(jnp.int32, (Q, Q), 0)
        jj = lax.broadcasted_iota(jnp.int32, (Q, Q), 1)
        ltri = (jj <= ii).astype(BF16)
        lane = lax.broadcasted_iota(jnp.int32, (Q, 128), 1)
        for val, ref in ((dt, dt_ref), (_dot01(ltri, da), cs_ref), (_sigmoid(pre), sg_ref)):
            for g in range(NG):
                moved = val if g == 0 else pltpu.roll(val, 128 - 4 * g, axis=1)
                ref[g] = jnp.where(lane < 4, moved, 0.0)

    blk = pl.BlockSpec((NG, Q, 128), lambda c: (0, c, 0))
    return pl.pallas_call(
        kern, grid=(L // Q,),
        in_specs=[pl.BlockSpec((Q, 256), lambda c: (c, 0)), pl.BlockSpec((8, 128), lambda c: (0, 0))],
        out_specs=(blk, blk, blk),
        out_shape=(jax.ShapeDtypeStruct((NG, L, 128), F32),) * 3,
        name="dt_prep", compiler_params=_cp(("parallel",)),
    )(proj, vec)


def _head_masks():
    lane = lax.broadcasted_iota(jnp.int32, (1, 4 * HD), 1)
    return [((lane >= HD * j) & (lane < HD * (j + 1))) for j in range(4)]


def _expand4(v4, masks):
    R = v4.shape[0]
    out = jnp.zeros((R, 4 * HD), F32)
    for j in range(4):
        out = jnp.where(masks[j], jnp.broadcast_to(v4[:, j:j + 1], (R, 4 * HD)), out)
    return out


def _decay_matrix(cs_col, tri):
    colb = jnp.broadcast_to(cs_col, (Q, Q))
    return jnp.exp(jnp.where(tri, colb - colb.T, -jnp.inf))


def _ssd_fwd(xbc, dt4, cs4, vecg):
    L = xbc.shape[0]
    nc = L // Q

    def kern(x_ref, b_ref, c_ref, dt_ref, cs_ref, v_ref, y_ref, s_ref, S):
        c = pl.program_id(1)

        @pl.when(c == 0)
        def _():
            S[...] = jnp.zeros_like(S)

        masks = _head_masks()
        ii = lax.broadcasted_iota(jnp.int32, (Q, Q), 0)
        jj = lax.broadcasted_iota(jnp.int32, (Q, Q), 1)
        tri = jj <= ii
        for gi in range(GPS):
            xs, ns = slice(256 * gi, 256 * (gi + 1)), slice(NS * gi, NS * (gi + 1))
            dt4v, cs4v = dt_ref[gi], cs_ref[gi]
            dt_b, cs_b = _expand4(dt4v, masks), _expand4(cs4v, masks)
            d_b = _expand4(v_ref[gi], masks)[1:2]
            cs_last = cs_b[Q - 1:Q, :]
            x4, bm, cm = x_ref[:, xs], b_ref[:, ns], c_ref[:, ns]
            xdt = x4 * dt_b
            gm = _bdot(cm, bm, "nt")
            s4 = S[gi]
            s_ref[gi, 0] = s4
            y = _bdot(cm, s4) * jnp.exp(cs_b) + d_b * x4
            m_all = jnp.concatenate([(gm * _decay_matrix(cs4v[:, j:j + 1], tri)).astype(BF16) for j in range(4)], axis=0)
            yd = _bdot(m_all, xdt)
            for j in range(4):
                y = y + jnp.where(masks[j], yd[Q * j:Q * (j + 1)], 0.0)
            y_ref[:, xs] = y
            S[gi] = jnp.exp(cs_last) * s4 + _bdot(bm, xdt * jnp.exp(cs_last - cs_b), "tn")

    sc = pl.BlockSpec((GPS, Q, 128), lambda g, c: (g, c, 0))
    bw = NS * GPS
    return pl.pallas_call(
        kern, grid=(NG // GPS, nc),
        in_specs=[pl.BlockSpec((Q, 256 * GPS), lambda g, c: (c, g)),
                  pl.BlockSpec((Q, bw), lambda g, c: (c, INNER // bw + g)),
                  pl.BlockSpec((Q, bw), lambda g, c: (c, (INNER + NG * NS) // bw + g)),
                  sc, sc, pl.BlockSpec((GPS, 8, 128), lambda g, c: (g, 0, 0))],
        out_specs=(pl.BlockSpec((Q, 256 * GPS), lambda g, c: (c, g)),
                   pl.BlockSpec((GPS, 1, NS, 256), lambda g, c: (g, c, 0, 0))),
        out_shape=(jax.ShapeDtypeStruct((L, INNER), F32), jax.ShapeDtypeStruct((NG, nc, NS, 256), F32)),
        scratch_shapes=[pltpu.VMEM((GPS, NS, 256), F32)], name="ssd_fwd",
        compiler_params=_cp(("parallel", "arbitrary")),
    )(xbc, xbc, xbc, dt4, cs4, vecg)


def _ssd_bwd(xbc, dt4, cs4, sg4, vecg, s_all, dy):
    L = xbc.shape[0]
    nc = L // Q

    def kern(x_ref, b_ref, c_ref, dt_ref, cs_ref, sg_ref, v_ref, s_ref, dy_ref,
             dx_ref, db_ref, dc_ref, ddt_ref, st_ref, dS):
        cc = pl.program_id(1)

        @pl.when(cc == 0)
        def _():
            dS[...] = jnp.zeros_like(dS)
            st_ref[...] = jnp.zeros_like(st_ref)

        masks = _head_masks()
        ii = lax.broadcasted_iota(jnp.int32, (Q, Q), 0)
        jj = lax.broadcasted_iota(jnp.int32, (Q, Q), 1)
        tri = jj <= ii
        utri = (jj >= ii).astype(BF16)
        li = lax.broadcasted_iota(jnp.int32, (4 * HD, 4 * HD), 0)
        lj = lax.broadcasted_iota(jnp.int32, (4 * HD, 4 * HD), 1)
        eblk = ((li // HD) == (lj // HD)).astype(BF16)
        lane128 = lax.broadcasted_iota(jnp.int32, (Q, 128), 1)

        for gi in range(GPS):
            xs, ns = slice(256 * gi, 256 * (gi + 1)), slice(NS * gi, NS * (gi + 1))
            dt4v, cs4v, sg4v = dt_ref[gi], cs_ref[gi], sg_ref[gi]
            dt_b, cs_b = _expand4(dt4v, masks), _expand4(cs4v, masks)
            vv = _expand4(v_ref[gi], masks)
            a_b = -jnp.exp(vv[0:1])
            d_b = vv[1:2]
            a4 = -jnp.exp(v_ref[gi][0:1, :])
            cs_last = cs_b[Q - 1:Q, :]
            ecs = jnp.exp(cs_b)
            decay = jnp.exp(cs_last - cs_b)
            elast = jnp.exp(cs_last)
            x4, bm, cm, dyv = x_ref[:, xs], b_ref[:, ns], c_ref[:, ns], dy_ref[:, xs]
            s4 = s_ref[gi, 0]
            dsn = dS[gi]
            xdt = x4 * dt_b
            gm = _bdot(cm, bm, "nt")
            gmt = gm.T
            dye = dyv * ecs
            yoff = ecs * _bdot(cm, s4)
            t4 = _bdot(bm, dsn) * decay
            lms, mhs, mhts = [], [], []
            for j in range(4):
                colb = jnp.broadcast_to(cs4v[:, j:j + 1], (Q, Q))
                seg = colb - colb.T
                lms.append(jnp.exp(jnp.where(tri, seg, -jnp.inf)))
                mhs.append(gm * lms[j])
                mhts.append(gmt * jnp.exp(jnp.where(jj >= ii, -seg, -jnp.inf)))
            m_all = jnp.concatenate([m.astype(BF16) for m in mhs], axis=0)
            dy_m = jnp.concatenate([jnp.where(masks[j], dyv, 0.0).astype(BF16) for j in range(4)], axis=0)
            x_m = jnp.concatenate([jnp.where(masks[j], xdt, 0.0).astype(BF16) for j in range(4)], axis=0)
            dxdt = t4 + _bdot(m_all, dy_m, "tn")
            dm_all = _bdot(dy_m, xdt, "nt")
            dmt_all = _bdot(x_m, dyv, "nt")
            dg = jnp.zeros((Q, Q), F32)
            rc = jnp.zeros((Q, 4 * HD), F32)
            for j in range(4):
                dmh = dm_all[Q * j:Q * (j + 1)]
                dg = dg + dmh * lms[j]
                rs = (jnp.sum(dmh * mhs[j], axis=1, keepdims=True)
                      - jnp.sum(dmt_all[Q * j:Q * (j + 1)] * mhts[j], axis=1, keepdims=True))
                rc = jnp.where(masks[j], jnp.broadcast_to(rs, (Q, 4 * HD)), rc)
            xt = xdt * t4
            tail = jnp.sum(xt, axis=0, keepdims=True) + elast * jnp.sum(s4 * dsn, axis=0, keepdims=True)
            gd_raw = jnp.sum(dyv * x4, axis=0, keepdims=True)
            stacked = jnp.concatenate([dyv * yoff - xt, dxdt * x4, jnp.broadcast_to(tail, (8, 4 * HD)),
                                       jnp.broadcast_to(gd_raw, (8, 4 * HD))], axis=0)
            seg = _dot01(eblk, stacked, m_left=False, terms=2)
            da_b = seg[0:Q] + rc
            dda_b = _dot01(utri, da_b, terms=2) + seg[2 * Q:2 * Q + 1]
            ddt_b = dda_b * a_b + seg[Q:2 * Q]
            gd_b = seg[2 * Q + 8:2 * Q + 9]
            ddt4 = jnp.zeros((Q, 128), F32)
            dda4 = jnp.zeros((Q, 128), F32)
            for j in range(4):
                ddt4 = jnp.where(lane128 == j, jnp.broadcast_to(ddt_b[:, HD * j:HD * j + 1], (Q, 128)), ddt4)
                dda4 = jnp.where(lane128 == j, jnp.broadcast_to(dda_b[:, HD * j:HD * j + 1], (Q, 128)), dda4)
            ddt_ref[gi] = ddt4 * sg4v
            ga = jnp.sum(dda4 * dt4v * a4, axis=0, keepdims=True)
            gd = jnp.zeros((1, 128), F32)
            for j in range(4):
                gd = jnp.where(lane128[0:1] == j, jnp.broadcast_to(gd_b[:, HD * j:HD * j + 1], (1, 128)), gd)
            row = lax.broadcasted_iota(jnp.int32, (8, 128), 0)
            st_ref[gi] += jnp.where(row == 0, ga, jnp.where(row == 1, gd, 0.0))
            dx_ref[:, xs] = d_b * dyv + dxdt * dt_b
            dc_ref[:, ns] = _bdot(dg, bm) + _bdot(dye, s4, "nt")
            db_ref[:, ns] = _bdot(dg, cm, "tn") + _bdot(xdt * decay, dsn, "nt")
            dS[gi] = elast * dsn + _bdot(cm, dye, "tn")

    rv = lambda c: nc - 1 - c
    sc = pl.BlockSpec((GPS, Q, 128), lambda g, c: (g, rv(c), 0))
    bw = NS * GPS
    return pl.pallas_call(
        kern, grid=(NG // GPS, nc),
        in_specs=[pl.BlockSpec((Q, 256 * GPS), lambda g, c: (rv(c), g)),
                  pl.BlockSpec((Q, bw), lambda g, c: (rv(c), INNER // bw + g)),
                  pl.BlockSpec((Q, bw), lambda g, c: (rv(c), (INNER + NG * NS) // bw + g)),
                  sc, sc, sc, pl.BlockSpec((GPS, 8, 128), lambda g, c: (g, 0, 0)),
                  pl.BlockSpec((GPS, 1, NS, 256), lambda g, c: (g, rv(c), 0, 0)),
                  pl.BlockSpec((Q, 256 * GPS), lambda g, c: (rv(c), g))],
        out_specs=(pl.BlockSpec((Q, 256 * GPS), lambda g, c: (rv(c), g)),
                   pl.BlockSpec((Q, bw), lambda g, c: (rv(c), g)),
                   pl.BlockSpec((Q, bw), lambda g, c: (rv(c), g)),
                   pl.BlockSpec((GPS, Q, 128), lambda g, c: (g, rv(c), 0)),
                   pl.BlockSpec((GPS, 8, 128), lambda g, c: (g, 0, 0))),
        out_shape=(jax.ShapeDtypeStruct((L, INNER), F32), jax.ShapeDtypeStruct((L, NG * NS), F32),
                   jax.ShapeDtypeStruct((L, NG * NS), F32), jax.ShapeDtypeStruct((NG, L, 128), F32),
                   jax.ShapeDtypeStruct((NG, 8, 128), F32)),
        scratch_shapes=[pltpu.VMEM((GPS, NS, 256), F32)], name="ssd_bwd",
        compiler_params=_cp(("parallel", "arbitrary")),
    )(xbc, xbc, xbc, dt4, cs4, sg4, vecg, s_all, dy)


def _dt_bwd(ddt, dproj, tl=256):
    L = ddt.shape[1]

    def kern(d_ref, _, dp_ref, gs_ref):
        @pl.when(pl.program_id(0) == 0)
        def _():
            gs_ref[...] = jnp.zeros_like(gs_ref)

        d = d_ref[0]
        for g in range(1, NG):
            d = d + pltpu.roll(d_ref[g], 4 * g, axis=1)
        gs_ref[...] += jnp.broadcast_to(jnp.sum(d, axis=0, keepdims=True), (8, 128))
        dp_ref[...] = jnp.concatenate([d, jnp.zeros_like(d)], axis=1).astype(BF16)

    return pl.pallas_call(
        kern, grid=(L // tl,),
        in_specs=[pl.BlockSpec((NG, tl, 128), lambda i: (0, i, 0)), pl.BlockSpec(memory_space=pl.ANY)],
        out_specs=(pl.BlockSpec((tl, 256), lambda i: (i, C_DT // 256)), pl.BlockSpec((8, 128), lambda i: (0, 0))),
        out_shape=(jax.ShapeDtypeStruct(dproj.shape, BF16), jax.ShapeDtypeStruct((8, 128), F32)),
        input_output_aliases={1: 0}, name="dt_bwd", compiler_params=_cp(("arbitrary",)),
    )(ddt, dproj)


GW = INNER // NG


def _gnorm_fwd(y, proj, w, tl=256):
    L = y.shape[0]
    zoff = C_Z // 1024

    def kern(y_ref, z_ref, w_ref, o_ref):
        z = z_ref[...].astype(F32)
        yz = y_ref[...] * (z * _sigmoid(z))
        wv = w_ref[...]
        for k in range(1024 // GW):
            sl = slice(GW * k, GW * (k + 1))
            v = yz[:, sl]
            rg = lax.rsqrt(jnp.mean(v * v, axis=-1, keepdims=True) + EPS)
            o_ref[:, sl] = ((v * rg) * wv[:, sl]).astype(BF16)

    blk = pl.BlockSpec((tl, 1024), lambda i, j: (i, j))
    return pl.pallas_call(
        kern, grid=(L // tl, 2),
        in_specs=[blk, pl.BlockSpec((tl, 1024), lambda i, j: (i, zoff + j)), pl.BlockSpec((1, 1024), lambda i, j: (0, j))],
        out_specs=blk, out_shape=jax.ShapeDtypeStruct((L, INNER), BF16), name="gnorm_fwd",
        compiler_params=_cp(("parallel", "parallel")),
    )(y, proj, w.reshape(1, INNER))


def _gnorm_bwd(dyb, y, proj, w, dproj, tl=256):
    L = y.shape[0]
    zoff = C_Z // 1024

    def kern(d_ref, y_ref, z_ref, w_ref, _, dy_ref, dp_ref, gw_ref):
        @pl.when(pl.program_id(1) == 0)
        def _():
            gw_ref[...] = jnp.zeros_like(gw_ref)

        z = z_ref[...].astype(F32)
        sg = _sigmoid(z)
        sz = z * sg
        yv = y_ref[...]
        yz = yv * sz
        dv = d_ref[...]
        wv = w_ref[...]
        for k in range(1024 // GW):
            sl = slice(GW * k, GW * (k + 1))
            v = yz[:, sl]
            rg = lax.rsqrt(jnp.mean(v * v, axis=-1, keepdims=True) + EPS)
            vn = v * rg
            dk = dv[:, sl]
            gw_ref[:, sl] += jnp.broadcast_to(jnp.sum(dk * vn, axis=0, keepdims=True), (8, GW))
            dvn = dk * wv[:, sl]
            dyz = rg * (dvn - vn * jnp.mean(dvn * vn, axis=-1, keepdims=True))
            dy_ref[:, sl] = dyz * sz[:, sl]
            dp_ref[:, sl] = (dyz * yv[:, sl] * (sg[:, sl] * (1.0 + z[:, sl] * (1.0 - sg[:, sl])))).astype(BF16)

    blk = pl.BlockSpec((tl, 1024), lambda j, i: (i, j))
    zblk = pl.BlockSpec((tl, 1024), lambda j, i: (i, zoff + j))
    return pl.pallas_call(
        kern, grid=(2, L // tl),
        in_specs=[blk, blk, zblk, pl.BlockSpec((1, 1024), lambda j, i: (0, j)), pl.BlockSpec(memory_space=pl.ANY)],
        out_specs=(blk, zblk, pl.BlockSpec((8, 1024), lambda j, i: (0, j))),
        out_shape=(jax.ShapeDtypeStruct((L, INNER), F32), jax.ShapeDtypeStruct(dproj.shape, BF16),
                   jax.ShapeDtypeStruct((8, INNER), F32)),
        input_output_aliases={4: 1}, name="gnorm_bwd", compiler_params=_cp(("parallel", "arbitrary")),
    )(dyb, y, proj, w.reshape(1, INNER), dproj)


def _merge_fwd(proj, bg, br_a, br_b, tl=256):
    L = proj.shape[0]
    goff = C_GATE // 1024

    def kern(g1_ref, g2_ref, b1_ref, b2_ref, a_ref, b_ref, o_ref):
        g1 = _sigmoid(g1_ref[...].astype(F32) + b1_ref[...])
        g2 = _sigmoid(g2_ref[...].astype(F32) + b2_ref[...])
        o_ref[...] = (g1 * a_ref[...] + g2 * b_ref[...]).astype(BF16)

    row = pl.BlockSpec((tl, 1024), lambda i: (i, 0))
    bg2 = bg.reshape(1, 2 * D)
    return pl.pallas_call(
        kern, grid=(L // tl,),
        in_specs=[pl.BlockSpec((tl, 1024), lambda i: (i, goff)), pl.BlockSpec((tl, 1024), lambda i: (i, goff + 1)),
                  pl.BlockSpec((1, 1024), lambda i: (0, 0)), pl.BlockSpec((1, 1024), lambda i: (0, 1)), row, row],
        out_specs=row, out_shape=jax.ShapeDtypeStruct((L, D), BF16), name="merge_fwd",
        compiler_params=_cp(("parallel",)),
    )(proj, proj, bg2, bg2, br_a, br_b)


def _merge_bwd(dm, proj, bg, br_a, br_b, dproj, tl=256):
    L = proj.shape[0]
    goff = C_GATE // 1024

    def kern(dm_ref, g_ref, b_ref, a_ref, bb_ref, _, dbr_ref, dp_ref, gb_ref):
        j = pl.program_id(0)

        @pl.when(pl.program_id(1) == 0)
        def _():
            gb_ref[...] = jnp.zeros_like(gb_ref)

        g = _sigmoid(g_ref[...].astype(F32) + b_ref[...])
        br = jnp.where(j == 0, a_ref[...], bb_ref[...])
        dmv = dm_ref[...]
        dbr_ref[0] = (dmv * g).astype(BF16)
        dgate = dmv * br * g * (1.0 - g)
        gb_ref[...] += jnp.broadcast_to(jnp.sum(dgate, axis=0, keepdims=True), (8, 1024))
        dp_ref[...] = dgate.astype(BF16)

    row = pl.BlockSpec((tl, 1024), lambda j, i: (i, 0))
    gblk = pl.BlockSpec((tl, 1024), lambda j, i: (i, goff + j))
    return pl.pallas_call(
        kern, grid=(2, L // tl),
        in_specs=[row, gblk, pl.BlockSpec((1, 1024), lambda j, i: (0, j)), row, row, pl.BlockSpec(memory_space=pl.ANY)],
        out_specs=(pl.BlockSpec((1, tl, 1024), lambda j, i: (j, i, 0)), gblk, pl.BlockSpec((8, 1024), lambda j, i: (0, j))),
        out_shape=(jax.ShapeDtypeStruct((2, L, D), BF16), jax.ShapeDtypeStruct(dproj.shape, BF16),
                   jax.ShapeDtypeStruct((8, 2 * D), F32)),
        input_output_aliases={5: 1}, name="merge_bwd", compiler_params=_cp(("parallel", "arbitrary")),
    )(dm, proj, bg.reshape(1, 2 * D), br_a, br_b, dproj)


def _coords():
    return lax.axis_index("x"), lax.axis_index("y"), lax.axis_index("c")


def _other_chips(sk):
    xk, yk = sk // 2, sk % 2
    return [((1 - xk, yk), 2 * (1 - xk) + yk), ((xk, 1 - yk), 2 * xk + 1 - yk), ((1 - xk, 1 - yk), 2 * (1 - xk) + 1 - yk)]


def _rows(start, size):
    assert size % 128 == 0
    return pl.ds(pl.multiple_of(start, 128), size)


def _per_chip(fn):
    x, y, _ = _coords()
    s = 2 * x + y
    for sk in range(4):
        pl.when(s == sk)(functools.partial(fn, sk))


XTRA = PIECE - PMAIN


def _place(shard, full_shape, block, index_map, idx, name, blk0=0, nblk=None, dep=None):
    in_block = block[-2:]
    if nblk is None:
        nblk = shard.shape[0] // in_block[0]

    def kern(idx_ref, s_ref, *rest):
        o_ref = rest[-1]
        o_ref[...] = s_ref[...].astype(BF16).reshape(o_ref.shape)

    grid_spec = pltpu.PrefetchScalarGridSpec(
        num_scalar_prefetch=1, grid=(nblk,),
        in_specs=[pl.BlockSpec(in_block, lambda i, idx_ref: (blk0 + i, 0))] + ([_ANY] if dep is not None else []),
        out_specs=pl.BlockSpec(block, index_map))
    args = (idx, shard) + ((dep,) if dep is not None else ())
    return pl.pallas_call(kern, grid_spec=grid_spec, out_shape=jax.ShapeDtypeStruct(full_shape, BF16), name=name,
                          compiler_params=_cp(("arbitrary",)))(*args)


_SEM = pl.BlockSpec(memory_space=pltpu.SEMAPHORE)
_EFFECT = pltpu.SideEffectType.DATAFLOW_SIDE_EFFECTING


_ANY = pl.BlockSpec(memory_space=pl.ANY)


def _tie(v, dep, name):
    def body(v_ref, dep_ref, o_ref):
        del v_ref, dep_ref, o_ref

    return pl.pallas_call(body, out_shape=jax.ShapeDtypeStruct(v.shape, v.dtype), in_specs=[_ANY, _ANY],
                          out_specs=_ANY, input_output_aliases={0: 0}, name=name)(v, dep)


def _split_call(name, arrays, start=None, wait=None, wait_sems=None, after=None):
    keys = list(arrays)
    n = len(keys)
    n_start = start.n if start is not None else 0
    afters = [] if after is None else (list(after) if isinstance(after, (list, tuple)) else [after])

    def body(*refs):
        pos = n
        if wait is not None:
            wss, wrs = refs[pos], refs[pos + 1]
            pos += 2
        pos += len(afters)
        if start is not None:
            nss, nrs = refs[pos], refs[pos + 1]
            pos += 2
        R = dict(zip(keys, refs[pos:pos + n]))
        token = refs[pos + n]
        x, y, c = _coords()

        def desc(src, dst, dev, ss, rs, k):
            return pltpu.make_async_remote_copy(src_ref=src, dst_ref=dst, send_sem=ss.at[k], recv_sem=rs.at[k],
                                                device_id=dev, device_id_type=MESH)

        def run(sk):
            if wait is not None:
                for k, (snd, land) in enumerate(wait.copies(sk, R)):
                    if snd is not None:
                        desc(snd[0], snd[1], snd[2], wss, wrs, k).wait_send()
                    if land is not None:
                        desc(land, land, (x, y, c), wss, wrs, k).wait_recv()
            if start is not None:
                for k, (snd, land) in enumerate(start.copies(sk, R)):
                    if snd is not None:
                        desc(snd[0], snd[1], snd[2], nss, nrs, k).start()

        _per_chip(run)
        token[...] = jnp.zeros_like(token)

    hbm = pl.BlockSpec(memory_space=HBM)
    vals = [arrays[k] for k in keys]
    ins, in_specs = list(vals), [hbm] * n
    if wait is not None:
        ins += list(wait_sems)
        in_specs += [_SEM, _SEM]
    ins += afters
    in_specs += [pl.BlockSpec(memory_space=pl.ANY)] * len(afters)
    out_shape, out_specs = [], []
    if start is not None:
        out_shape += [pltpu.SemaphoreType.DMA((n_start,)), pltpu.SemaphoreType.DMA((n_start,))]
        out_specs += [_SEM, _SEM]
    first = len(out_shape)
    out_shape += [jax.ShapeDtypeStruct(v.shape, v.dtype) for v in vals] + [jax.ShapeDtypeStruct((8, 128), F32)]
    out_specs += [hbm] * n + [pl.BlockSpec(memory_space=pltpu.VMEM)]
    res = pl.pallas_call(
        body, out_shape=tuple(out_shape), in_specs=in_specs, out_specs=tuple(out_specs),
        input_output_aliases={i: first + i for i in range(n)}, name=name,
        compiler_params=pltpu.CompilerParams(has_side_effects=_EFFECT),
    )(*ins)
    sems = (res[0], res[1]) if start is not None else None
    return dict(zip(keys, res[first:first + n])), sems, res[-1]


class _Plan:
    def __init__(self, n, copies):
        self.n, self.copies = n, copies


_HM, _HX = PMAIN // 2, XTRA // 2
WAVE0 = 768
WAVES = ((0, WAVE0), (WAVE0, _HM - WAVE0))
_WIN = {
    "wq0": (True, "wct", lambda r, sc, hc: r.at[_rows(PMAIN * sc + _HM * hc + WAVES[0][0], WAVES[0][1]), :]),
    "wq1": (True, "wct", lambda r, sc, hc: r.at[_rows(PMAIN * sc + _HM * hc + WAVES[1][0], WAVES[1][1]), :]),
    "xt": (True, "xt", lambda r, sc, hc: r.at[sc, _rows(_HX * hc, _HX), :]),
    "w1": (True, "w1", lambda r, sc, hc: r.at[_rows(512 * hc, 512), pl.ds(1024 * sc, 1024)]),
    "w2": (True, "w2", lambda r, sc, hc: r.at[_rows(1024 * sc + 512 * hc, 512), :]),
    "wa": (True, "wa", lambda r, sc, hc: r.at[_rows(256 * sc + 128 * hc, 128), :]),
    "wb": (True, "wb", lambda r, sc, hc: r.at[_rows(512 * sc + 256 * hc, 256), :]),
    "wo": (True, "wo", lambda r, sc, hc: r.at[_rows(256 * sc + 128 * hc, 128), :]),
    "cw": (False, "cw", lambda r, sc, hc: r.at[sc]),
}


def _ag_chips_plan(keys):
    def copies(sk, R):
        _, _, c = _coords()
        out = []
        for key in keys:
            _, arr, win = _WIN[key]
            for (px, py), ps in _other_chips(sk):
                w = win(R[arr], sk, c)
                out.append(((w, w, (px, py, c)), win(R[arr], ps, c)))
        return out
    return _Plan(3 * len(keys), copies)


def _ag_sibling_plan(keys):
    keys = [k for k in keys if _WIN[k][0]]

    def copies(sk, R):
        x, y, c = _coords()
        out = []
        for key in keys:
            _, arr, win = _WIN[key]
            for _, ps in _other_chips(sk):
                w = win(R[arr], ps, c)
                out.append(((w, w, (x, y, 1 - c)), win(R[arr], ps, 1 - c)))
        return out
    return _Plan(3 * len(keys), copies)


def _in_proj_wave(h, wct, wave, proj=None, tm=1024):
    L = h.shape[0]
    tm = min(tm, L)
    off, size = WAVES[wave]
    start = lambda j: pl.multiple_of(_HM * j + off, 128)

    def kern(h_ref, w_ref, *rest):
        o_ref = rest[-1]
        o_ref[...] = lax.dot_general(h_ref[...], w_ref[...], _DIMS["nt"], preferred_element_type=F32).astype(BF16)

    in_specs = [pl.BlockSpec((tm, D), lambda j, i: (i, 0)),
                pl.BlockSpec((pl.Element(size), pl.Element(D)), lambda j, i: (start(j), 0))]
    args, aliases = [h, wct], {}
    if proj is not None:
        in_specs.append(pl.BlockSpec(memory_space=pl.ANY))
        args.append(proj)
        aliases = {2: 0}
    return pl.pallas_call(
        kern, grid=(8, L // tm), in_specs=in_specs,
        out_specs=pl.BlockSpec((pl.Element(tm), pl.Element(size)), lambda j, i: (i * tm, start(j))),
        out_shape=jax.ShapeDtypeStruct((L, NCW), BF16), input_output_aliases=aliases,
        name="in_proj_wave%d" % wave, compiler_params=_cp(("parallel", "parallel")),
    )(*args)


def _fix_wct(wct, xt):
    nb = PMAIN // XTRA

    def kern(w_ref, x_ref, o_ref):
        k = pl.program_id(0)
        xv = x_ref[0]
        o_ref[...] = jnp.where(k < 3, (w_ref[...].astype(F32) + xv.astype(F32)).astype(BF16), xv)

    blk = pl.BlockSpec((XTRA, D), lambda k: (nb * (k + 1), 0))
    rblk = pl.BlockSpec((XTRA, D), lambda k: (jnp.where(k < 3, nb * (k + 1), 0), 0))
    return pl.pallas_call(
        kern, grid=(4,), in_specs=[rblk, pl.BlockSpec((1, XTRA, D), lambda k: (k, 0, 0))], out_specs=blk,
        out_shape=jax.ShapeDtypeStruct(wct.shape, BF16), input_output_aliases={0: 0}, name="fix_wct",
        compiler_params=_cp(("arbitrary",)),
    )(wct, xt)


_HP = PIECE // 2
_GWIN = [
    lambda r, sc, hc: r.at[_rows(PMAIN * sc + _HP * hc, _HP), :],
    lambda r, sc, hc: r.at[_rows(512 * hc, 512), pl.ds(1024 * sc, 1024)],
    lambda r, sc, hc: r.at[_rows(1024 * sc + 512 * hc, 512), :],
    lambda r, sc, hc: r.at[_rows(256 * sc + 128 * hc, 128), :],
    lambda r, sc, hc: r.at[_rows(512 * sc + 256 * hc, 256), :],
    lambda r, sc, hc: r.at[_rows(256 * sc + 128 * hc, 128), :],
]
HALF_SHAPES = [(PIECE // 2, D), (512, 1024), (512, 1024), (128, 1024), (256, 1024), (128, 1024)]


def _rs_sibling_plan(ts):
    def copies(sk, R):
        x, y, c = _coords()
        out = []
        for t in ts:
            for sc in range(4):
                land = R["ra%d" % t].at[sc]
                out.append(((_GWIN[t](R["g%d" % t], sc, 1 - c), land, (x, y, 1 - c)), land))
        return out
    return _Plan(4 * len(ts), copies)


def _rs_chips_plan(ts):
    def copies(sk, R):
        _, _, c = _coords()
        out = []
        for t in ts:
            for j, ((px, py), ps) in enumerate(_other_chips(sk)):
                land = R["rb%d" % t].at[j]
                out.append(((R["hb%d" % t].at[ps], land, (px, py, c)), land))
        return out
    return _Plan(3 * len(ts), copies)


def _rs_share_plan(ts):
    def copies(sk, R):
        x, y, c = _coords()
        out = []
        for t in ts:
            rows = HALF_SHAPES[t][0]
            mine = R["f%d" % t].at[_rows(rows * c, rows), :]
            out.append(((mine, mine, (x, y, 1 - c)), R["f%d" % t].at[_rows(rows * (1 - c), rows), :]))
        return out
    return _Plan(len(ts), copies)


def _half_tiling(t):
    rows, cols = HALF_SHAPES[t]
    if t == 0:
        return (256, cols), rows // 256, lambda i: (i, 0)
    if t == 1:
        return (rows, 256), cols // 256, lambda i: (0, i)
    return (rows, cols), 1, lambda i: (0, 0)


def _window_block(t, sc, hc, i):
    if t == 0:
        return (PMAIN // 256) * sc + (PIECE // 512) * hc + i, 0
    if t == 1:
        return hc, 4 * sc + i
    return 2 * sc + hc, 0


def _chip_sum(g, ra, t, idx, name):
    rows, cols = HALF_SHAPES[t]
    blk, nblk, inner = _half_tiling(t)

    def kern(idx_ref, g_ref, r_ref, hb_ref, hf_ref):
        v = g_ref[...].astype(F32) + r_ref[0].astype(F32)
        hb_ref[0] = v.astype(BF16)

        @pl.when(pl.program_id(1) == idx_ref[0])
        def _():
            hf_ref[...] = v

    gmap = lambda i, sc, idx_ref: _window_block(t, sc, idx_ref[1], i)
    omap = lambda i, sc, idx_ref: (sc,) + inner(i)
    grid_spec = pltpu.PrefetchScalarGridSpec(
        num_scalar_prefetch=1, grid=(nblk, 4),
        in_specs=[pl.BlockSpec(blk, gmap), pl.BlockSpec((1,) + blk, omap)],
        out_specs=(pl.BlockSpec((1,) + blk, omap), pl.BlockSpec(blk, lambda i, sc, idx_ref: inner(i))))
    return pl.pallas_call(
        kern, grid_spec=grid_spec,
        out_shape=(jax.ShapeDtypeStruct((4, rows, cols), BF16), jax.ShapeDtypeStruct((rows, cols), F32)),
        name=name, compiler_params=_cp(("parallel", "arbitrary")),
    )(idx, g, ra)


def _final_sum(hf, rb, t, idx, name):
    rows, cols = HALF_SHAPES[t]
    blk, nblk, inner = _half_tiling(t)
    nbr = rows // blk[0]

    def kern(idx_ref, h_ref, r_ref, o_ref):
        o_ref[...] = ((h_ref[...] + r_ref[0].astype(F32)) + r_ref[1].astype(F32)) + r_ref[2].astype(F32)

    def omap(i, idx_ref):
        r, cidx = inner(i)
        return nbr * idx_ref[1] + r, cidx

    grid_spec = pltpu.PrefetchScalarGridSpec(
        num_scalar_prefetch=1, grid=(nblk,),
        in_specs=[pl.BlockSpec(blk, lambda i, idx_ref: inner(i)),
                  pl.BlockSpec((3,) + blk, lambda i, idx_ref: (0,) + inner(i))],
        out_specs=pl.BlockSpec(blk, omap))
    return pl.pallas_call(
        kern, grid_spec=grid_spec, out_shape=jax.ShapeDtypeStruct((2 * rows, cols), F32),
        name=name, compiler_params=_cp(("parallel",)),
    )(idx, hf, rb)


class _ReduceScatter:
    def __init__(self, ts, grads, idx, tag):
        self.ts, self.idx, self.tag = ts, idx, tag
        arr = {}
        for t in ts:
            arr["g%d" % t] = grads[t]
            arr["ra%d" % t] = lax.empty((4,) + HALF_SHAPES[t], BF16)
        self.plan = _rs_sibling_plan(ts)
        self.arr, self.sems, self.token = _split_call("rs_sibling_start_" + tag, arr, start=self.plan)

    def chips(self, after):
        arr, _, _ = _split_call("rs_sibling_wait_" + self.tag, self.arr, wait=self.plan, wait_sems=self.sems, after=after)
        brr, self.hf = {}, {}
        for t in self.ts:
            hb, self.hf[t] = _chip_sum(arr["g%d" % t], arr["ra%d" % t], t, self.idx, "chip_sum_%d" % t)
            brr["hb%d" % t] = hb
            brr["rb%d" % t] = lax.empty((3,) + HALF_SHAPES[t], BF16)
        self.plan = _rs_chips_plan(self.ts)
        self.arr, self.sems, self.token = _split_call("rs_chips_start_" + self.tag, brr, start=self.plan)
        return self.token

    def share(self, after):
        brr, _, _ = _split_call("rs_chips_wait_" + self.tag, self.arr, wait=self.plan, wait_sems=self.sems, after=after)
        frr = {"f%d" % t: _final_sum(self.hf[t], brr["rb%d" % t], t, self.idx, "final_sum_%d" % t) for t in self.ts}
        self.plan = _rs_share_plan(self.ts)
        self.arr, self.sems, self.token = _split_call("rs_share_start_" + self.tag, frr, start=self.plan)
        return self.token

    def result(self, after):
        frr, _, _ = _split_call("rs_share_wait_" + self.tag, self.arr, wait=self.plan, wait_sems=self.sems, after=after)
        return {t: frr["f%d" % t] for t in self.ts}


def _all8_plan(key):
    def copies(sk, R):
        x, y, c = _coords()
        own = R[key].at[4 * x + 2 * y + c]
        out = []
        for k in range(1, 8):
            dev = ((1 - x) if (k >> 2) & 1 else x, (1 - y) if (k >> 1) & 1 else y, (1 - c) if k & 1 else c)
            out.append(((own, own, dev), R[key].at[4 * dev[0] + 2 * dev[1] + dev[2]]))
        return out
    return _Plan(7, copies)


def _small_all_gather(v):
    def body(v_ref, o_ref, send_sems, recv_sems, loc_sem):
        x, y, c = _coords()
        me = 4 * x + 2 * y + c
        lc = pltpu.make_async_copy(v_ref, o_ref.at[me], loc_sem)
        lc.start()
        cps = []
        for k in range(1, 8):
            fx, fy, fc = (k >> 2) & 1, (k >> 1) & 1, k & 1
            dev = ((1 - x) if fx else x, (1 - y) if fy else y, (1 - c) if fc else c)
            cp = pltpu.make_async_remote_copy(src_ref=v_ref, dst_ref=o_ref.at[me], send_sem=send_sems.at[k - 1],
                                              recv_sem=recv_sems.at[k - 1], device_id=dev, device_id_type=MESH)
            cp.start()
            cps.append((cp, 4 * dev[0] + 2 * dev[1] + dev[2]))
        for k, (cp, frm) in enumerate(cps):
            got = o_ref.at[frm]
            pltpu.make_async_remote_copy(src_ref=got, dst_ref=got, send_sem=send_sems.at[k], recv_sem=recv_sems.at[k],
                                         device_id=(x, y, c), device_id_type=MESH).wait_recv()
        for cp, _ in cps:
            cp.wait_send()
        lc.wait()

    hbm = pl.BlockSpec(memory_space=HBM)
    return pl.pallas_call(
        body, out_shape=jax.ShapeDtypeStruct((8,) + v.shape, F32), in_specs=[hbm], out_specs=hbm,
        scratch_shapes=[pltpu.SemaphoreType.DMA((7,)), pltpu.SemaphoreType.DMA((7,)), pltpu.SemaphoreType.DMA(())],
        name="small_all_gather", compiler_params=pltpu.CompilerParams(has_side_effects=True),
    )(v)


def _sum8(v, name="small_sum"):
    def kern(v_ref, o_ref):
        acc = v_ref[0]
        for k in range(1, 8):
            acc = acc + v_ref[k]
        o_ref[...] = acc

    return pl.pallas_call(kern, out_shape=jax.ShapeDtypeStruct(v.shape[1:], F32), name=name)(v)


def _adamw(w, g, m, v, name, tr=128, blk0=0, nblk=None, into=None, copy_g=False):
    R, C = w.shape
    tr = min(tr, R)
    if nblk is None:
        assert R % tr == 0 and blk0 == 0
        nblk = R // tr
    n_out = 4 if copy_g else 3

    def kern(*refs):
        w_ref, g_ref, m_ref, v_ref = refs[:4]
        d_ref, mo_ref, vo_ref = refs[-n_out:][:3]
        gv = g_ref[...]
        mn = ADAM_B1 * m_ref[...] + (1.0 - ADAM_B1) * gv
        vn = ADAM_B2 * v_ref[...] + (1.0 - ADAM_B2) * (gv * gv)
        m_hat = mn / (1.0 - ADAM_B1 ** ADAM_STEP)
        v_hat = vn / (1.0 - ADAM_B2 ** ADAM_STEP)
        d_ref[...] = -ADAM_LR * (m_hat / (jnp.sqrt(v_hat) + ADAM_EPS) + ADAM_WD * w_ref[...])
        mo_ref[...] = mn
        vo_ref[...] = vn
        if copy_g:
            refs[-1][...] = gv

    blk = pl.BlockSpec((tr, C), lambda i: (blk0 + i, 0))
    sd = jax.ShapeDtypeStruct((R, C), F32)
    in_specs, args, aliases = [blk] * 4, [w, g, m, v], {}
    if into is not None:
        in_specs += [pl.BlockSpec(memory_space=pl.ANY)] * 3
        args += list(into)
        aliases = {4: 0, 5: 1, 6: 2}
    return pl.pallas_call(kern, grid=(nblk,), in_specs=in_specs, out_specs=(blk,) * n_out, out_shape=(sd,) * n_out,
                          input_output_aliases=aliases, name=name, compiler_params=_cp(("parallel",)))(*args)


def _to_piece(wt, s):
    z = lambda n: jnp.zeros((n, D), wt.dtype)
    pads = [functools.partial(lambda k, w: jnp.pad(w, ((8 * k, PIECE - W_SHARD - 8 * k), (0, 0))), k) for k in range(3)]
    last = lambda w: jnp.concatenate([z(24), w[:744], w[776:], w[744:776], z(PIECE - 24 - W_SHARD)], axis=0)
    return lax.switch(s, pads + [last], wt)


def _from_piece(p, s):
    cuts = [functools.partial(lambda k, q: q[8 * k:8 * k + W_SHARD], k) for k in range(3)]
    last = lambda q: jnp.concatenate([q[24:768], q[2816:2848], q[768:2816]], axis=0)
    return lax.switch(s, cuts + [last], p)


_SMALL = [("b_gate", 2048), ("ssm_conv_b", 4096), ("dt_bias", 32), ("A_log", 32), ("D_skip", 32),
          ("ssm_norm_w", 2048), ("norm_mlp", 1024), ("norm_final", 1024), ("sc_conv_w", 3072), ("ssm_conv_w", 16384),
          ("loss", 1)]


def _pack(vals, table, rows):
    parts = []
    for name, n in table:
        v = vals[name].reshape(-1).astype(F32)
        pad = (-n) % 128
        parts.append(jnp.pad(v, (0, pad)) if pad else v)
    flat = jnp.concatenate(parts)
    return jnp.pad(flat, (0, rows * 128 - flat.shape[0])).reshape(rows, 128)


def _unpack(arr, table):
    flat = arr.reshape(-1)
    out, off = {}, 0
    for name, n in table:
        out[name] = flat[off:off + n]
        off += n + ((-n) % 128)
    return out


def kernel(x, norm_mix, w_in, b_gate, sc_conv_w, ssm_conv_w, ssm_conv_b, dt_bias, A_log, D_skip, ssm_norm_w, w_branch_sc, w_branch_ssm, w_out, norm_mlp, w_mlp1, w_mlp2, norm_final, loss_target, m_norm_mix, m_w_in, m_b_gate, m_sc_conv_w, m_ssm_conv_w, m_ssm_conv_b, m_dt_bias, m_A_log, m_D_skip, m_ssm_norm_w, m_w_branch_sc, m_w_branch_ssm, m_w_out, m_norm_mlp, m_w_mlp1, m_w_mlp2, m_norm_final, v_norm_mix, v_w_in, v_b_gate, v_sc_conv_w, v_ssm_conv_w, v_ssm_conv_b, v_dt_bias, v_A_log, v_D_skip, v_ssm_norm_w, v_w_branch_sc, v_w_branch_ssm, v_w_out, v_norm_mlp, v_w_mlp1, v_w_mlp2, v_norm_final):
    L = x.shape[1]
    nc = L // Q
    xi, yi, ci = lax.axis_index("x"), lax.axis_index("y"), lax.axis_index("c")
    s = 2 * xi + yi
    idx = jnp.stack([s, ci]).astype(jnp.int32)
    x0 = x.reshape(L, D)
    tgt = loss_target.reshape(L, D)

    piece = _to_piece(w_in.T, s)
    nb = PMAIN // XTRA
    wct0 = _place(piece, (NCW, D), (XTRA, D), lambda i, r: (nb * r[0] + i, 0), idx, "place_wct", nblk=nb)
    xt0 = _place(piece, (4, XTRA, D), (1, XTRA, D), lambda i, r: (r[0], 0, 0), idx, "place_xt", blk0=nb, nblk=1)
    cws = jnp.zeros((8, 1280), F32)
    cws = cws.at[0:3, 0:256].set(sc_conv_w).at[0:4, 256:1280].set(ssm_conv_w)
    cw0 = lax.dynamic_update_slice(jnp.zeros((4, 8, 1280), F32), cws[None], (s, 0, 0))
    win_keys, win2_keys, mid_keys, end_keys = ["xt", "cw", "wq0"], ["wq1"], ["wa", "wb", "wo", "w1"], ["w2"]
    gw, sems_w, tok = _split_call("ag_win_start", {"wct": wct0, "xt": xt0, "cw": cw0}, start=_ag_chips_plan(win_keys))
    g2, sems_w2, tok = _split_call("ag_win2_start", {"wct": gw["wct"]}, start=_ag_chips_plan(win2_keys), after=tok)
    gw["wct"] = g2["wct"]
    wa0 = _place(w_branch_sc, (D, D), (256, 1024), lambda i, r: (r[0], 0), idx, "place_wa", dep=tok)
    wb0 = _place(w_branch_ssm, (INNER, D), (512, 1024), lambda i, r: (r[0], 0), idx, "place_wb", dep=tok)
    wo0 = _place(w_out, (D, D), (256, 1024), lambda i, r: (r[0], 0), idx, "place_wo", dep=tok)
    w10 = _place(w_mlp1, (D, DFF), (256, 1024), lambda i, r: (i, r[0]), idx, "place_w1", dep=tok)
    gm, sems_m, tok = _split_call("ag_mid_start", {"wa": wa0, "wb": wb0, "wo": wo0, "w1": w10},
                                  start=_ag_chips_plan(mid_keys))
    w20 = _place(w_mlp2, (DFF, D), (256, 1024), lambda i, r: (4 * r[0] + i, 0), idx, "place_w2", dep=tok)
    ge, sems_e, tok = _split_call("ag_end_start", {"w2": w20}, start=_ag_chips_plan(end_keys))
    h = _rms_fwd(x0, norm_mix, "rms_mix", dep=tok)
    gw, sems_w, tok = _split_call("ag_win_pass", gw, wait=_ag_chips_plan(win_keys), wait_sems=sems_w,
                                  start=_ag_sibling_plan(win_keys), after=h)
    gw, _, _ = _split_call("ag_win_done", gw, wait=_ag_sibling_plan(win_keys), wait_sems=sems_w, after=tok)
    wc, cw_all = _fix_wct(gw["wct"], gw["xt"]), gw["cw"]
    sc_w_full = jnp.concatenate([cw_all[k, :, 0:256] for k in range(4)], axis=1)
    ssm_w_full = jnp.concatenate([cw_all[k, :, 256:1280] for k in range(4)], axis=1)
    cw4 = ssm_w_full.at[4].set(ssm_conv_b)
    vec = jnp.zeros((8, 128), F32).at[0, :NH].set(dt_bias).at[1, :NH].set(A_log)
    vecg = jnp.zeros((NG, 8, 128), F32).at[:, 0, :4].set(A_log.reshape(NG, 4)).at[:, 1, :4].set(D_skip.reshape(NG, 4))

    dtraw = _matmul(h, wc[C_DT:], "nt", F32, 512, 256, 1024, "in_proj_dt")
    proj = _in_proj_wave(h, wc, 0)
    g2, sems_w2, tok = _split_call("ag_win2_pass", {"wct": wc}, wait=_ag_chips_plan(win2_keys), wait_sems=sems_w2,
                                   start=_ag_sibling_plan(win2_keys), after=[proj, dtraw])
    g2, _, _ = _split_call("ag_win2_done", g2, wait=_ag_sibling_plan(win2_keys), wait_sems=sems_w2, after=tok)
    wc = g2["wct"]
    proj = _in_proj_wave(h, wc, 1, proj=proj)
    ya = _sc_fwd(proj, sc_w_full)
    xbc = _ssm_conv_fwd(proj, cw4)
    dt4, cs4, sg4 = _dt_prep(dtraw, vec)
    gm, sems_m, tok = _split_call("ag_mid_pass", gm, wait=_ag_chips_plan(mid_keys), wait_sems=sems_m,
                                  start=_ag_sibling_plan(mid_keys), after=[xbc, ya, dt4])
    xbc = _tie(xbc, tok, "tie_xbc")
    y, s_all = _ssd_fwd(xbc, dt4, cs4, vecg)
    yb = _gnorm_fwd(y, proj, ssm_norm_w)
    gm, _, _ = _split_call("ag_mid_done", gm, wait=_ag_sibling_plan(mid_keys), wait_sems=sems_m, after=yb)
    wa, wb, wo, w1 = gm["wa"], gm["wb"], gm["wo"], gm["w1"]
    ge, sems_e, tok = _split_call("ag_end_pass", ge, wait=_ag_chips_plan(end_keys), wait_sems=sems_e,
                                  start=_ag_sibling_plan(end_keys), after=yb)
    br_a = _matmul(ya, wa, "nn", F32, 1024, 1024, 1024, "branch_sc", dep=tok)
    br_b = _matmul(yb, wb, "nn", F32, 1024, 1024, 2048, "branch_ssm")
    merged = _merge_fwd(proj, b_gate, br_a, br_b)
    x1 = _matmul(merged, wo, "nn", F32, 1024, 1024, 1024, "out_proj", epi="res", extra=x0)
    h2 = _rms_fwd(x1, norm_mlp, "rms_mlp")
    a1, rl = _matmul(h2, w1, "nn", BF16, 1024, 1024, 1024, "mlp1", epi="relu2", n_outer=True)
    ge, _, _ = _split_call("ag_end_done", ge, wait=_ag_sibling_plan(end_keys), wait_sems=sems_e, after=a1)
    w2 = ge["w2"]
    x2 = _matmul(rl, w2, "nn", F32, 512, 1024, 4096, "mlp2", epi="res", extra=x1)
    dx2, g_nf, loss8 = _final(x2, norm_final, tgt)

    da = _matmul(dx2, w2, "nt", BF16, 1024, 1024, 1024, "mlp2_dx", epi="drelu", extra=a1, n_outer=True)
    g_w2 = _matmul(rl, dx2, "tn", BF16, 1024, 1024, 2048, "mlp2_dw")
    g_w1 = _matmul(h2, da, "tn", BF16, 1024, 1024, 2048, "mlp1_dw")
    dh2 = _matmul(da, w1, "nt", F32, 512, 1024, 4096, "mlp1_dx")
    dx1, g_nmlp = _rms_bwd(dh2, x1, norm_mlp, dx2, "rms_mlp_bwd")
    dmerged = _matmul(dx1, wo, "nt", F32, 1024, 1024, 1024, "out_proj_dx")
    g_wo = _matmul(merged, dx1, "tn", BF16, 1024, 1024, 2048, "out_proj_dw")
    dproj = lax.empty((L, NCW), BF16)
    dbr, dproj, g_bg = _merge_bwd(dmerged, proj, b_gate, br_a, br_b, dproj)
    dya = _matmul(dbr[0], wa, "nt", F32, 1024, 1024, 1024, "branch_sc_dx")
    g_wa = _matmul(ya, dbr[0], "tn", BF16, 1024, 1024, 2048, "branch_sc_dw")
    dproj, g_scw = _sc_bwd(dya, proj, sc_w_full, dproj)
    dyb = _matmul(dbr[1], wb, "nt", F32, 1024, 1024, 1024, "branch_ssm_dx", n_outer=True)
    g_wb = _matmul(yb, dbr[1], "tn", BF16, 1024, 1024, 2048, "branch_ssm_dw")
    rs_a = _ReduceScatter([1, 2, 3, 4, 5], {1: g_w1, 2: g_w2, 3: g_wa, 4: g_wb, 5: g_wo}, idx, "a")
    dy, dproj, g_snw = _gnorm_bwd(_tie(dyb, rs_a.token, "tie_dyb"), y, proj, ssm_norm_w, dproj)
    tok = rs_a.chips(after=dy)
    dxs, dbm, dcm, ddt_g, st = _ssd_bwd(xbc, dt4, cs4, sg4, vecg, s_all, _tie(dy, tok, "tie_dy"))
    dproj, gx1 = _ssm_conv_bwd(dxs, proj, cw4, dproj, 0, "ssm_conv_bwd_x")
    dproj, gx2 = _ssm_conv_bwd(dbm, proj, cw4, dproj, INNER, "ssm_conv_bwd_b")
    dproj, gx3 = _ssm_conv_bwd(dcm, proj, cw4, dproj, INNER + NG * NS, "ssm_conv_bwd_c")
    g_cw4 = jnp.concatenate([gx1, gx2, gx3], axis=1)
    dproj, g_dtb = _dt_bwd(ddt_g, dproj)
    small = {"b_gate": g_bg[0], "ssm_conv_b": g_cw4[4], "dt_bias": g_dtb[0, :NH],
             "A_log": st[:, 0, :4], "D_skip": st[:, 1, :4], "ssm_norm_w": g_snw[0], "norm_mlp": g_nmlp[0],
             "norm_final": g_nf[0], "sc_conv_w": g_scw[0:3], "ssm_conv_w": g_cw4[0:4], "loss": loss8[0, 0:1]}
    small_sum = _sum8(_small_all_gather(_pack(small, _SMALL, SMALL_ROWS)))
    gs = _unpack(small_sum, _SMALL)
    g_wc = _matmul(dproj, h, "tn", BF16, 1280, 1024, 2048, "in_proj_dw", dep=small_sum)
    rs_b = _ReduceScatter([0], {0: g_wc}, idx, "b")
    tok = rs_a.share(after=rs_b.token)
    tok = rs_b.chips(after=tok)
    dh = _matmul(dproj, wc, "nn", F32, 512, 1024, 5760, "in_proj_dx", dep=tok)
    grad_x, g_nm = _rms_bwd(dh, x0, norm_mix, dx1, "rms_mix_bwd")
    me = 4 * xi + 2 * yi + ci
    nm8 = lax.dynamic_update_slice(jnp.zeros((8, 8, 128), F32), g_nm[0].reshape(1, 8, 128), (me, 0, 0))
    nm_arr, nm_sems, tok = _split_call("norm_mix_start", {"nm": nm8}, start=_all8_plan("nm"))
    red = rs_a.result(after=tok)
    big = {"w_mlp1": red[1], "w_mlp2": red[2], "w_branch_sc": red[3], "w_branch_ssm": red[4], "w_out": red[5]}

    given = dict(norm_mix=norm_mix, w_in=w_in, b_gate=b_gate, sc_conv_w=sc_conv_w, ssm_conv_w=ssm_conv_w, ssm_conv_b=ssm_conv_b, dt_bias=dt_bias, A_log=A_log, D_skip=D_skip, ssm_norm_w=ssm_norm_w, w_branch_sc=w_branch_sc, w_branch_ssm=w_branch_ssm, w_out=w_out, norm_mlp=norm_mlp, w_mlp1=w_mlp1, w_mlp2=w_mlp2, norm_final=norm_final,
                 m_norm_mix=m_norm_mix, m_w_in=m_w_in, m_b_gate=m_b_gate, m_sc_conv_w=m_sc_conv_w, m_ssm_conv_w=m_ssm_conv_w, m_ssm_conv_b=m_ssm_conv_b, m_dt_bias=m_dt_bias, m_A_log=m_A_log, m_D_skip=m_D_skip, m_ssm_norm_w=m_ssm_norm_w, m_w_branch_sc=m_w_branch_sc, m_w_branch_ssm=m_w_branch_ssm, m_w_out=m_w_out, m_norm_mlp=m_norm_mlp, m_w_mlp1=m_w_mlp1, m_w_mlp2=m_w_mlp2, m_norm_final=m_norm_final,
                 v_norm_mix=v_norm_mix, v_w_in=v_w_in, v_b_gate=v_b_gate, v_sc_conv_w=v_sc_conv_w, v_ssm_conv_w=v_ssm_conv_w, v_ssm_conv_b=v_ssm_conv_b, v_dt_bias=v_dt_bias, v_A_log=v_A_log, v_D_skip=v_D_skip, v_ssm_norm_w=v_ssm_norm_w, v_w_branch_sc=v_w_branch_sc, v_w_branch_ssm=v_w_branch_ssm, v_w_out=v_w_out, v_norm_mlp=v_norm_mlp, v_w_mlp1=v_w_mlp1, v_w_mlp2=v_w_mlp2, v_norm_final=v_norm_final)
    order = ["norm_mix", "w_in", "b_gate", "sc_conv_w", "ssm_conv_w", "ssm_conv_b", "dt_bias", "A_log", "D_skip",
             "ssm_norm_w", "w_branch_sc", "w_branch_ssm", "w_out", "norm_mlp", "w_mlp1", "w_mlp2", "norm_final"]
    grad, delta, new_m, new_v = {}, {}, {}, {}
    for n in big:
        delta[n], new_m[n], new_v[n], grad[n] = _adamw(given[n], big[n], given["m_" + n], given["v_" + n],
                                                       "adamw_" + n, copy_g=True)
    big["w_in"] = None
    grad_small = {n: gs[n].reshape(given[n].shape) for n in order
                  if n not in big and n not in ("sc_conv_w", "ssm_conv_w", "norm_mix")}
    grad_small["sc_conv_w"] = lax.dynamic_slice(gs["sc_conv_w"].reshape(3, D), (0, 256 * s), (3, 256))
    grad_small["ssm_conv_w"] = lax.dynamic_slice(gs["ssm_conv_w"].reshape(4, XBC), (0, 1024 * s), (4, 1024))
    table = [(n, int(grad_small[n].size)) for n in grad_small]
    rows = 136
    pk = lambda d: _pack(d, table, rows)
    ds_, ms_, vs_ = _adamw(pk({n: given[n] for n in grad_small}), pk(grad_small), pk({n: given["m_" + n] for n in grad_small}),
                           pk({n: given["v_" + n] for n in grad_small}), "adamw_small", tr=rows)
    ds_, ms_, vs_ = _unpack(ds_, table), _unpack(ms_, table), _unpack(vs_, table)
    for n in grad_small:
        shp = given[n].shape
        grad[n] = grad_small[n]
        delta[n], new_m[n], new_v[n] = ds_[n].reshape(shp), ms_[n].reshape(shp), vs_[n].reshape(shp)

    done = [new_v[n] for n in ("w_mlp1", "w_mlp2", "w_branch_sc", "w_branch_ssm", "w_out")] + [vs_["b_gate"]]
    tok = rs_b.share(after=done)
    gp = rs_b.result(after=tok)[0]
    gwt = _from_piece(gp, s)
    wt_args = (w_in.T, gwt, m_w_in.T, v_w_in.T)
    head = _adamw(*wt_args, "adamw_w_in", tr=256, nblk=W_SHARD // 256)
    dt_, mt_, vt_ = _adamw(*wt_args, "adamw_w_in_tail", tr=8, blk0=(W_SHARD // 256) * 32, nblk=1, into=head)
    grad["w_in"], delta["w_in"], new_m["w_in"], new_v["w_in"] = gwt.T, dt_.T, mt_.T, vt_.T
    nm_arr, _, _ = _split_call("norm_mix_wait", nm_arr, wait=_all8_plan("nm"), wait_sems=nm_sems, after=tok)
    g8 = _sum8(nm_arr["nm"], "norm_mix_sum")
    r8 = lambda a: a.reshape(8, 128)
    d8, m8, v8 = _adamw(r8(norm_mix), g8, r8(m_norm_mix), r8(v_norm_mix), "adamw_norm_mix", tr=8)
    grad["norm_mix"], delta["norm_mix"] = g8.reshape(D), d8.reshape(D)
    new_m["norm_mix"], new_v["norm_mix"] = m8.reshape(D), v8.reshape(D)

    loss = gs["loss"].reshape(())
    return (loss, grad_x.reshape(1, L, D), *[grad[n] for n in order], *[delta[n] for n in order],
            *[new_m[n] for n in order], *[new_v[n] for n in order])
```

```python
import functools

import jax
import jax.numpy as jnp
from jax import lax
from jax.experimental import pallas as pl
from jax.experimental.pallas import tpu as pltpu

F32 = jnp.float32
BF16 = jnp.bfloat16
MESH = pl.DeviceIdType.MESH
HBM = pltpu.HBM

D = 1024
INNER = 2048
HD = 64
NH = 32
NG = 8
NS = 128
Q = 128
GPS = 4
XBC = 4096
DFF = 4096
EPS = 1e-6
W_SHARD = 2824
NCW = 11520
PIECE = 3072
PMAIN = 2816
C_Z, C_XBC, C_GATE, C_DT = 3072, 5120, 9216, 11264
SMALL_ROWS = 256
VMEM_LIMIT = 56 * 1024 * 1024

ADAM_LR, ADAM_B1, ADAM_B2, ADAM_EPS, ADAM_WD, ADAM_STEP = 0.001, 0.9, 0.999, 1e-08, 0.01, 10


def _cp(sem=None, vmem=VMEM_LIMIT):
    return pltpu.CompilerParams(dimension_semantics=sem, vmem_limit_bytes=vmem)


def _sigmoid(v):
    return 1.0 / (1.0 + jnp.exp(-v))


_DIMS = {"nn": (((1,), (0,)), ((), ())), "nt": (((1,), (1,)), ((), ())), "tn": (((0,), (0,)), ((), ()))}


def _matmul(a, b, mode, out_dtype, tm, tn, tk, name, epi=None, extra=None, n_outer=False, dep=None):
    if mode == "tn":
        K, M = a.shape
    else:
        M, K = a.shape
    N = b.shape[0] if mode == "nt" else b.shape[1]
    tm, tn, tk = min(tm, M), min(tn, N), min(tk, K)
    assert M % tm == 0 and N % tn == 0 and K % tk == 0, (name, M, N, K, tm, tn, tk)
    nm, nn, nk = M // tm, N // tn, K // tk
    dims = _DIMS[mode]

    def ij(p0, p1):
        return (p1, p0) if n_outer else (p0, p1)

    if mode == "tn":
        a_spec = pl.BlockSpec((tk, tm), lambda p0, p1, k: (k, ij(p0, p1)[0]))
    else:
        a_spec = pl.BlockSpec((tm, tk), lambda p0, p1, k: (ij(p0, p1)[0], k))
    if mode == "nt":
        b_spec = pl.BlockSpec((tn, tk), lambda p0, p1, k: (ij(p0, p1)[1], k))
    else:
        b_spec = pl.BlockSpec((tk, tn), lambda p0, p1, k: (k, ij(p0, p1)[1]))
    o_spec = pl.BlockSpec((tm, tn), lambda p0, p1, k: ij(p0, p1))
    in_specs = [a_spec, b_spec]
    args = [a, b]
    if epi in ("res", "drelu"):
        in_specs.append(o_spec)
        args.append(extra)
    if dep is not None:
        in_specs.append(pl.BlockSpec(memory_space=pl.ANY))
        args.append(dep)
    n_in = len(args)
    if epi == "relu2":
        out_shape = (jax.ShapeDtypeStruct((M, N), out_dtype), jax.ShapeDtypeStruct((M, N), BF16))
        out_specs = (o_spec, o_spec)
    else:
        out_shape = jax.ShapeDtypeStruct((M, N), out_dtype)
        out_specs = o_spec

    def kern(*refs):
        a_ref, b_ref = refs[0], refs[1]
        e_ref = refs[2] if epi in ("res", "drelu") else None
        acc = refs[-1]
        outs = refs[n_in:-1] if nk > 1 else refs[n_in:]
        k = pl.program_id(2)

        def product():
            return lax.dot_general(a_ref[...].astype(BF16), b_ref[...].astype(BF16), dims, preferred_element_type=F32)

        def finish(r):
            if epi is None:
                outs[0][...] = r.astype(out_dtype)
            elif epi == "res":
                outs[0][...] = (r + e_ref[...]).astype(out_dtype)
            elif epi == "relu2":
                outs[0][...] = r.astype(out_dtype)
                t = jnp.maximum(r, 0.0)
                outs[1][...] = (t * t).astype(BF16)
            else:
                outs[0][...] = (r * (2.0 * jnp.maximum(e_ref[...].astype(F32), 0.0))).astype(out_dtype)

        if nk == 1:
            finish(product())
        else:
            @pl.when(k == 0)
            def _():
                acc[...] = jnp.zeros_like(acc)

            acc[...] += product()

            @pl.when(k == nk - 1)
            def _():
                finish(acc[...])

    grid = (nn, nm, nk) if n_outer else (nm, nn, nk)
    return pl.pallas_call(
        kern, grid=grid, in_specs=in_specs, out_specs=out_specs, out_shape=out_shape,
        scratch_shapes=[pltpu.VMEM((tm, tn), F32)] if nk > 1 else [], name=name,
        compiler_params=_cp(("parallel", "parallel", "arbitrary")),
    )(*args)


def _rms_fwd(x, w, name, tl=256, dep=None):
    L = x.shape[0]

    def kern(x_ref, w_ref, *rest):
        o_ref = rest[-1]
        xv = x_ref[...]
        r = lax.rsqrt(jnp.mean(xv * xv, axis=-1, keepdims=True) + EPS)
        o_ref[...] = ((xv * r) * w_ref[...]).astype(BF16)

    row = pl.BlockSpec((tl, D), lambda i: (i, 0))
    deps = [] if dep is None else [dep]
    return pl.pallas_call(
        kern, grid=(L // tl,),
        in_specs=[row, pl.BlockSpec((1, D), lambda i: (0, 0))] + [pl.BlockSpec(memory_space=pl.ANY)] * len(deps),
        out_specs=row, out_shape=jax.ShapeDtypeStruct((L, D), BF16), name=name, compiler_params=_cp(("parallel",)),
    )(x, w.reshape(1, D), *deps)


def _rms_bwd(dy, x, w, res, name, tl=256, dep=None):
    L = x.shape[0]
    deps = [] if dep is None else [dep]

    def kern(dy_ref, x_ref, w_ref, res_ref, *rest):
        dx_ref, gw_ref = rest[-2:]
        @pl.when(pl.program_id(0) == 0)
        def _():
            gw_ref[...] = jnp.zeros_like(gw_ref)

        xv = x_ref[...]
        dyv = dy_ref[...]
        r = lax.rsqrt(jnp.mean(xv * xv, axis=-1, keepdims=True) + EPS)
        xn = xv * r
        gw_ref[...] += jnp.broadcast_to(jnp.sum(dyv * xn, axis=0, keepdims=True), (8, D))
        dxn = dyv * w_ref[...]
        dx_ref[...] = res_ref[...] + r * (dxn - xn * jnp.mean(dxn * xn, axis=-1, keepdims=True))

    row = pl.BlockSpec((tl, D), lambda i: (i, 0))
    return pl.pallas_call(
        kern, grid=(L // tl,),
        in_specs=[row, row, pl.BlockSpec((1, D), lambda i: (0, 0)), row] + [pl.BlockSpec(memory_space=pl.ANY)] * len(deps),
        out_specs=(row, pl.BlockSpec((8, D), lambda i: (0, 0))),
        out_shape=(jax.ShapeDtypeStruct((L, D), F32), jax.ShapeDtypeStruct((8, D), F32)),
        name=name, compiler_params=_cp(("arbitrary",)),
    )(dy, x, w.reshape(1, D), res, *deps)


def _final(x2, w, tgt, tl=256):
    L = x2.shape[0]

    def kern(x_ref, w_ref, t_ref, dx_ref, gw_ref, loss_ref):
        @pl.when(pl.program_id(0) == 0)
        def _():
            gw_ref[...] = jnp.zeros_like(gw_ref)
            loss_ref[...] = jnp.zeros_like(loss_ref)

        xv = x_ref[...]
        r = lax.rsqrt(jnp.mean(xv * xv, axis=-1, keepdims=True) + EPS)
        xn = xv * r
        e = xn * w_ref[...] - t_ref[...]
        per_tok = jnp.mean(e * e, axis=-1, keepdims=True)
        loss_ref[...] += 0.5 * jnp.sum(per_tok)
        dyv = e * (1.0 / D)
        gw_ref[...] += jnp.broadcast_to(jnp.sum(dyv * xn, axis=0, keepdims=True), (8, D))
        dxn = dyv * w_ref[...]
        dx_ref[...] = r * (dxn - xn * jnp.mean(dxn * xn, axis=-1, keepdims=True))

    row = pl.BlockSpec((tl, D), lambda i: (i, 0))
    return pl.pallas_call(
        kern, grid=(L // tl,), in_specs=[row, pl.BlockSpec((1, D), lambda i: (0, 0)), row],
        out_specs=(row, pl.BlockSpec((8, D), lambda i: (0, 0)), pl.BlockSpec((8, 128), lambda i: (0, 0))),
        out_shape=(jax.ShapeDtypeStruct((L, D), F32), jax.ShapeDtypeStruct((8, D), F32),
                   jax.ShapeDtypeStruct((8, 128), F32)),
        name="final_norm_loss", compiler_params=_cp(("arbitrary",)),
    )(x2, w.reshape(1, D), tgt)


def _down(v, k):
    if k == 0:
        return v
    t = lax.broadcasted_iota(jnp.int32, v.shape, 0)
    return jnp.where(t >= k, pltpu.roll(v, k, axis=0), 0.0)


def _up(v, k):
    if k == 0:
        return v
    n = v.shape[0]
    t = lax.broadcasted_iota(jnp.int32, v.shape, 0)
    return jnp.where(t < n - k, pltpu.roll(v, n - k, axis=0), 0.0)


TW = 256


def _sc_fwd(proj, cw):
    L = proj.shape[0]
    nb = D // TW

    def kern(b_ref, c_ref, x_ref, w_ref, o_ref):
        u = c_ref[...].astype(F32) * x_ref[...].astype(F32)
        w = w_ref[...]
        cv = w[0:1] * _down(u, 2) + w[1:2] * _down(u, 1) + w[2:3] * u
        o_ref[...] = (b_ref[...].astype(F32) * cv).astype(BF16)

    col = lambda off: pl.BlockSpec((L, TW), lambda j: (0, off + j))
    return pl.pallas_call(
        kern, grid=(nb,), in_specs=[col(0), col(nb), col(2 * nb), pl.BlockSpec((8, TW), lambda j: (0, j))],
        out_specs=pl.BlockSpec((L, TW), lambda j: (0, j)), out_shape=jax.ShapeDtypeStruct((L, D), BF16),
        name="sc_fwd", compiler_params=_cp(("parallel",)),
    )(proj, proj, proj, cw)


def _sc_bwd(dya, proj, cw, dproj):
    L = proj.shape[0]
    nb = D // TW

    def kern(d_ref, b_ref, c_ref, x_ref, w_ref, _, dp_ref, gw_ref, keep):
        sec = pl.program_id(1)

        @pl.when(sec == 0)
        def _():
            cs, xs, dyv = c_ref[...].astype(F32), x_ref[...].astype(F32), d_ref[...]
            w = w_ref[...]
            u = cs * xs
            u1, u2 = _down(u, 1), _down(u, 2)
            cv = w[0:1] * u2 + w[1:2] * u1 + w[2:3] * u
            dcv = dyv * b_ref[...].astype(F32)
            du = w[2:3] * dcv + w[1:2] * _up(dcv, 1) + w[0:1] * _up(dcv, 2)
            g0 = jnp.sum(dcv * u2, axis=0, keepdims=True)
            g1 = jnp.sum(dcv * u1, axis=0, keepdims=True)
            g2 = jnp.sum(dcv * u, axis=0, keepdims=True)
            row = lax.broadcasted_iota(jnp.int32, (8, TW), 0)
            gw_ref[...] = jnp.where(row == 0, g0, jnp.where(row == 1, g1, jnp.where(row == 2, g2, 0.0)))
            dp_ref[...] = (dyv * cv).astype(BF16)
            keep[0] = (du * xs).astype(BF16)
            keep[1] = (du * cs).astype(BF16)

        @pl.when(sec > 0)
        def _():
            dp_ref[...] = keep[sec - 1]

    col = lambda off: pl.BlockSpec((L, TW), lambda j, s: (0, off + j))
    return pl.pallas_call(
        kern, grid=(nb, 3),
        in_specs=[col(0), col(0), col(nb), col(2 * nb), pl.BlockSpec((8, TW), lambda j, s: (0, j)),
                  pl.BlockSpec(memory_space=pl.ANY)],
        out_specs=(pl.BlockSpec((L, TW), lambda j, s: (0, s * nb + j)), pl.BlockSpec((8, TW), lambda j, s: (0, j))),
        out_shape=(jax.ShapeDtypeStruct(dproj.shape, BF16), jax.ShapeDtypeStruct((8, D), F32)),
        scratch_shapes=[pltpu.VMEM((2, L, TW), BF16)],
        input_output_aliases={5: 0}, name="sc_bwd", compiler_params=_cp(("parallel", "arbitrary")),
    )(dya, proj, proj, proj, cw, dproj)


def _ssm_conv_fwd(proj, cw4):
    L = proj.shape[0]
    off = C_XBC // TW

    def kern(r_ref, w_ref, o_ref):
        raw = r_ref[...].astype(F32)
        w = w_ref[...]
        c4 = w[0:1] * _down(raw, 3) + w[1:2] * _down(raw, 2) + w[2:3] * _down(raw, 1) + w[3:4] * raw + w[4:5]
        o_ref[...] = c4 * _sigmoid(c4)

    return pl.pallas_call(
        kern, grid=(XBC // TW,),
        in_specs=[pl.BlockSpec((L, TW), lambda j: (0, off + j)), pl.BlockSpec((8, TW), lambda j: (0, j))],
        out_specs=pl.BlockSpec((L, TW), lambda j: (0, j)), out_shape=jax.ShapeDtypeStruct((L, XBC), F32),
        name="ssm_conv_fwd", compiler_params=_cp(("parallel",)),
    )(proj, cw4)


def _ssm_conv_bwd(dx, proj, cw4, dproj, col0, name):
    L, width = dx.shape
    off_p = (C_XBC + col0) // TW
    off_w = col0 // TW

    def kern(d_ref, r_ref, w_ref, _, dp_ref, gw_ref):
        raw = r_ref[...].astype(F32)
        w = w_ref[...]
        r1, r2, r3 = _down(raw, 1), _down(raw, 2), _down(raw, 3)
        c4 = w[0:1] * r3 + w[1:2] * r2 + w[2:3] * r1 + w[3:4] * raw + w[4:5]
        sg = _sigmoid(c4)
        dc4 = d_ref[...] * (sg * (1.0 + c4 * (1.0 - sg)))
        draw = w[3:4] * dc4 + w[2:3] * _up(dc4, 1) + w[1:2] * _up(dc4, 2) + w[0:1] * _up(dc4, 3)
        dp_ref[...] = draw.astype(BF16)
        gs = [jnp.sum(dc4 * r3, axis=0, keepdims=True), jnp.sum(dc4 * r2, axis=0, keepdims=True),
              jnp.sum(dc4 * r1, axis=0, keepdims=True), jnp.sum(dc4 * raw, axis=0, keepdims=True),
              jnp.sum(dc4, axis=0, keepdims=True)]
        row = lax.broadcasted_iota(jnp.int32, (8, TW), 0)
        acc = jnp.zeros((8, TW), F32)
        for k, gk in enumerate(gs):
            acc = jnp.where(row == k, gk, acc)
        gw_ref[...] = acc

    return pl.pallas_call(
        kern, grid=(width // TW,),
        in_specs=[pl.BlockSpec((L, TW), lambda j: (0, j)), pl.BlockSpec((L, TW), lambda j: (0, off_p + j)),
                  pl.BlockSpec((8, TW), lambda j: (0, off_w + j)), pl.BlockSpec(memory_space=pl.ANY)],
        out_specs=(pl.BlockSpec((L, TW), lambda j: (0, off_p + j)), pl.BlockSpec((8, TW), lambda j: (0, j))),
        out_shape=(jax.ShapeDtypeStruct(dproj.shape, BF16), jax.ShapeDtypeStruct((8, width), F32)),
        input_output_aliases={3: 0}, name=name, compiler_params=_cp(("arbitrary",)),
    )(dx, proj, cw4, dproj)


def _split3(v):
    h1 = v.astype(BF16)
    r1 = v - h1.astype(F32)
    h2 = r1.astype(BF16)
    h3 = (r1 - h2.astype(F32)).astype(BF16)
    return h1, h2, h3


def _dot01(m01, v, dims=_DIMS["nn"], m_left=True, terms=3):
    out = None
    for part in _split3(v)[:terms]:
        ops = (m01, part) if m_left else (part, m01)
        t = lax.dot_general(ops[0], ops[1], dims, preferred_element_type=F32)
        out = t if out is None else out + t
    return out


def _bdot(a, b, mode="nn"):
    return lax.dot_general(a.astype(BF16), b.astype(BF16), _DIMS[mode], preferred_element_type=F32)


def _softplus(v):
    return jnp.maximum(v, 0.0) + jnp.log1p(jnp.exp(-jnp.abs(v)))


def _dt_prep(proj, vec):
    L = proj.shape[0]

    def kern(p_ref, v_ref, dt_ref, cs_ref, sg_ref):
        v = v_ref[...]
        pre = p_ref[:, 0:128] + v[0:1]
        dt = _softplus(pre)
        da = dt * (-jnp.exp(v[1:2]))
        ii = lax.broadcasted_iota(jnp.int32, (Q, Q), 0)
        jj = lax.broadcasted_iota(jnp.int32, (Q, Q), 1)
        ltri = (jj <= ii).astype(BF16)
        lane = lax.broadcasted_iota(jnp.int32, (Q, 128), 1)
        for val, ref in ((dt, dt_ref), (_dot01(ltri, da), cs_ref), (_sigmoid(pre), sg_ref)):
            for g in range(NG):
                moved = val if g == 0 else pltpu.roll(val, 128 - 4 * g, axis=1)
                ref[g] = jnp.where(lane < 4, moved, 0.0)

    blk = pl.BlockSpec((NG, Q, 128), lambda c: (0, c, 0))
    return pl.pallas_call(
        kern, grid=(L // Q,),
        in_specs=[pl.BlockSpec((Q, 256), lambda c: (c, 0)), pl.BlockSpec((8, 128), lambda c: (0, 0))],
        out_specs=(blk, blk, blk),
        out_shape=(jax.ShapeDtypeStruct((NG, L, 128), F32),) * 3,
        name="dt_prep", compiler_params=_cp(("parallel",)),
    )(proj, vec)


def _head_masks():
    lane = lax.broadcasted_iota(jnp.int32, (1, 4 * HD), 1)
    return [((lane >= HD * j) & (lane < HD * (j + 1))) for j in range(4)]


def _expand4(v4, masks):
    R = v4.shape[0]
    out = jnp.zeros((R, 4 * HD), F32)
    for j in range(4):
        out = jnp.where(masks[j], jnp.broadcast_to(v4[:, j:j + 1], (R, 4 * HD)), out)
    return out


def _decay_matrix(cs_col, tri):
    colb = jnp.broadcast_to(cs_col, (Q, Q))
    return jnp.exp(jnp.where(tri, colb - colb.T, -jnp.inf))


def _ssd_fwd(xbc, dt4, cs4, vecg):
    L = xbc.shape[0]
    nc = L // Q

    def kern(x_ref, b_ref, c_ref, dt_ref, cs_ref, v_ref, y_ref, s_ref, S):
        c = pl.program_id(1)

        @pl.when(c == 0)
        def _():
            S[...] = jnp.zeros_like(S)

        masks = _head_masks()
        ii = lax.broadcasted_iota(jnp.int32, (Q, Q), 0)
        jj = lax.broadcasted_iota(jnp.int32, (Q, Q), 1)
        tri = jj <= ii
        for gi in range(GPS):
            xs, ns = slice(256 * gi, 256 * (gi + 1)), slice(NS * gi, NS * (gi + 1))
            dt4v, cs4v = dt_ref[gi], cs_ref[gi]
            dt_b, cs_b = _expand4(dt4v, masks), _expand4(cs4v, masks)
            d_b = _expand4(v_ref[gi], masks)[1:2]
            cs_last = cs_b[Q - 1:Q, :]
            x4, bm, cm = x_ref[:, xs], b_ref[:, ns], c_ref[:, ns]
            xdt = x4 * dt_b
            gm = _bdot(cm, bm, "nt")
            s4 = S[gi]
            s_ref[gi, 0] = s4
            y = _bdot(cm, s4) * jnp.exp(cs_b) + d_b * x4
            m_all = jnp.concatenate([(gm * _decay_matrix(cs4v[:, j:j + 1], tri)).astype(BF16) for j in range(4)], axis=0)
            yd = _bdot(m_all, xdt)
            for j in range(4):
                y = y + jnp.where(masks[j], yd[Q * j:Q * (j + 1)], 0.0)
            y_ref[:, xs] = y
            S[gi] = jnp.exp(cs_last) * s4 + _bdot(bm, xdt * jnp.exp(cs_last - cs_b), "tn")

    sc = pl.BlockSpec((GPS, Q, 128), lambda g, c: (g, c, 0))
    bw = NS * GPS
    return pl.pallas_call(
        kern, grid=(NG // GPS, nc),
        in_specs=[pl.BlockSpec((Q, 256 * GPS), lambda g, c: (c, g)),
                  pl.BlockSpec((Q, bw), lambda g, c: (c, INNER // bw + g)),
                  pl.BlockSpec((Q, bw), lambda g, c: (c, (INNER + NG * NS) // bw + g)),
                  sc, sc, pl.BlockSpec((GPS, 8, 128), lambda g, c: (g, 0, 0))],
        out_specs=(pl.BlockSpec((Q, 256 * GPS), lambda g, c: (c, g)),
                   pl.BlockSpec((GPS, 1, NS, 256), lambda g, c: (g, c, 0, 0))),
        out_shape=(jax.ShapeDtypeStruct((L, INNER), F32), jax.ShapeDtypeStruct((NG, nc, NS, 256), F32)),
        scratch_shapes=[pltpu.VMEM((GPS, NS, 256), F32)], name="ssd_fwd",
        compiler_params=_cp(("parallel", "arbitrary")),
    )(xbc, xbc, xbc, dt4, cs4, vecg)


def _ssd_bwd(xbc, dt4, cs4, sg4, vecg, s_all, dy):
    L = xbc.shape[0]
    nc = L // Q

    def kern(x_ref, b_ref, c_ref, dt_ref, cs_ref, sg_ref, v_ref, s_ref, dy_ref,
             dx_ref, db_ref, dc_ref, ddt_ref, st_ref, dS):
        cc = pl.program_id(1)

        @pl.when(cc == 0)
        def _():
            dS[...] = jnp.zeros_like(dS)
            st_ref[...] = jnp.zeros_like(st_ref)

        masks = _head_masks()
        ii = lax.broadcasted_iota(jnp.int32, (Q, Q), 0)
        jj = lax.broadcasted_iota(jnp.int32, (Q, Q), 1)
        tri = jj <= ii
        utri = (jj >= ii).astype(BF16)
        li = lax.broadcasted_iota(jnp.int32, (4 * HD, 4 * HD), 0)
        lj = lax.broadcasted_iota(jnp.int32, (4 * HD, 4 * HD), 1)
        eblk = ((li // HD) == (lj // HD)).astype(BF16)
        lane128 = lax.broadcasted_iota(jnp.int32, (Q, 128), 1)

        for gi in range(GPS):
            xs, ns = slice(256 * gi, 256 * (gi + 1)), slice(NS * gi, NS * (gi + 1))
            dt4v, cs4v, sg4v = dt_ref[gi], cs_ref[gi], sg_ref[gi]
            dt_b, cs_b = _expand4(dt4v, masks), _expand4(cs4v, masks)
            vv = _expand4(v_ref[gi], masks)
            a_b = -jnp.exp(vv[0:1])
            d_b = vv[1:2]
            a4 = -jnp.exp(v_ref[gi][0:1, :])
            cs_last = cs_b[Q - 1:Q, :]
            ecs = jnp.exp(cs_b)
            decay = jnp.exp(cs_last - cs_b)
            elast = jnp.exp(cs_last)
            x4, bm, cm, dyv = x_ref[:, xs], b_ref[:, ns], c_ref[:, ns], dy_ref[:, xs]
            s4 = s_ref[gi, 0]
            dsn = dS[gi]
            xdt = x4 * dt_b
            gm = _bdot(cm, bm, "nt")
            gmt = gm.T
            dye = dyv * ecs
            yoff = ecs * _bdot(cm, s4)
            t4 = _bdot(bm, dsn) * decay
            lms, mhs, mhts = [], [], []
            for j in range(4):
                colb = jnp.broadcast_to(cs4v[:, j:j + 1], (Q, Q))
                seg = colb - colb.T
                lms.append(jnp.exp(jnp.where(tri, seg, -jnp.inf)))
                mhs.append(gm * lms[j])
                mhts.append(gmt * jnp.exp(jnp.where(jj >= ii, -seg, -jnp.inf)))
            m_all = jnp.concatenate([m.astype(BF16) for m in mhs], axis=0)
            dy_m = jnp.concatenate([jnp.where(masks[j], dyv, 0.0).astype(BF16) for j in range(4)], axis=0)
            x_m = jnp.concatenate([jnp.where(masks[j], xdt, 0.0).astype(BF16) for j in range(4)], axis=0)
            dxdt = t4 + _bdot(m_all, dy_m, "tn")
            dm_all = _bdot(dy_m, xdt, "nt")
            dmt_all = _bdot(x_m, dyv, "nt")
            dg = jnp.zeros((Q, Q), F32)
            rc = jnp.zeros((Q, 4 * HD), F32)
            for j in range(4):
                dmh = dm_all[Q * j:Q * (j + 1)]
                dg = dg + dmh * lms[j]
                rs = (jnp.sum(dmh * mhs[j], axis=1, keepdims=True)
                      - jnp.sum(dmt_all[Q * j:Q * (j + 1)] * mhts[j], axis=1, keepdims=True))
                rc = jnp.where(masks[j], jnp.broadcast_to(rs, (Q, 4 * HD)), rc)
            xt = xdt * t4
            tail = jnp.sum(xt, axis=0, keepdims=True) + elast * jnp.sum(s4 * dsn, axis=0, keepdims=True)
            gd_raw = jnp.sum(dyv * x4, axis=0, keepdims=True)
            stacked = jnp.concatenate([dyv * yoff - xt, dxdt * x4, jnp.broadcast_to(tail, (8, 4 * HD)),
                                       jnp.broadcast_to(gd_raw, (8, 4 * HD))], axis=0)
            seg = _dot01(eblk, stacked, m_left=False, terms=2)
            da_b = seg[0:Q] + rc
            dda_b = _dot01(utri, da_b, terms=2) + seg[2 * Q:2 * Q + 1]
            ddt_b = dda_b * a_b + seg[Q:2 * Q]
            gd_b = seg[2 * Q + 8:2 * Q + 9]
            ddt4 = jnp.zeros((Q, 128), F32)
            dda4 = jnp.zeros((Q, 128), F32)
            for j in range(4):
                ddt4 = jnp.where(lane128 == j, jnp.broadcast_to(ddt_b[:, HD * j:HD * j + 1], (Q, 128)), ddt4)
                dda4 = jnp.where(lane128 == j, jnp.broadcast_to(dda_b[:, HD * j:HD * j + 1], (Q, 128)), dda4)
            ddt_ref[gi] = ddt4 * sg4v
            ga = jnp.sum(dda4 * dt4v * a4, axis=0, keepdims=True)
            gd = jnp.zeros((1, 128), F32)
            for j in range(4):
                gd = jnp.where(lane128[0:1] == j, jnp.broadcast_to(gd_b[:, HD * j:HD * j + 1], (1, 128)), gd)
            row = lax.broadcasted_iota(jnp.int32, (8, 128), 0)
            st_ref[gi] += jnp.where(row == 0, ga, jnp.where(row == 1, gd, 0.0))
            dx_ref[:, xs] = d_b * dyv + dxdt * dt_b
            dc_ref[:, ns] = _bdot(dg, bm) + _bdot(dye, s4, "nt")
            db_ref[:, ns] = _bdot(dg, cm, "tn") + _bdot(xdt * decay, dsn, "nt")
            dS[gi] = elast * dsn + _bdot(cm, dye, "tn")

    rv = lambda c: nc - 1 - c
    sc = pl.BlockSpec((GPS, Q, 128), lambda g, c: (g, rv(c), 0))
    bw = NS * GPS
    return pl.pallas_call(
        kern, grid=(NG // GPS, nc),
        in_specs=[pl.BlockSpec((Q, 256 * GPS), lambda g, c: (rv(c), g)),
                  pl.BlockSpec((Q, bw), lambda g, c: (rv(c), INNER // bw + g)),
                  pl.BlockSpec((Q, bw), lambda g, c: (rv(c), (INNER + NG * NS) // bw + g)),
                  sc, sc, sc, pl.BlockSpec((GPS, 8, 128), lambda g, c: (g, 0, 0)),
                  pl.BlockSpec((GPS, 1, NS, 256), lambda g, c: (g, rv(c), 0, 0)),
                  pl.BlockSpec((Q, 256 * GPS), lambda g, c: (rv(c), g))],
        out_specs=(pl.BlockSpec((Q, 256 * GPS), lambda g, c: (rv(c), g)),
                   pl.BlockSpec((Q, bw), lambda g, c: (rv(c), g)),
                   pl.BlockSpec((Q, bw), lambda g, c: (rv(c), g)),
                   pl.BlockSpec((GPS, Q, 128), lambda g, c: (g, rv(c), 0)),
                   pl.BlockSpec((GPS, 8, 128), lambda g, c: (g, 0, 0))),
        out_shape=(jax.ShapeDtypeStruct((L, INNER), F32), jax.ShapeDtypeStruct((L, NG * NS), F32),
                   jax.ShapeDtypeStruct((L, NG * NS), F32), jax.ShapeDtypeStruct((NG, L, 128), F32),
                   jax.ShapeDtypeStruct((NG, 8, 128), F32)),
        scratch_shapes=[pltpu.VMEM((GPS, NS, 256), F32)], name="ssd_bwd",
        compiler_params=_cp(("parallel", "arbitrary")),
    )(xbc, xbc, xbc, dt4, cs4, sg4, vecg, s_all, dy)


def _dt_bwd(ddt, dproj, tl=256):
    L = ddt.shape[1]

    def kern(d_ref, _, dp_ref, gs_ref):
        @pl.when(pl.program_id(0) == 0)
        def _():
            gs_ref[...] = jnp.zeros_like(gs_ref)

        d = d_ref[0]
        for g in range(1, NG):
            d = d + pltpu.roll(d_ref[g], 4 * g, axis=1)
        gs_ref[...] += jnp.broadcast_to(jnp.sum(d, axis=0, keepdims=True), (8, 128))
        dp_ref[...] = jnp.concatenate([d, jnp.zeros_like(d)], axis=1).astype(BF16)

    return pl.pallas_call(
        kern, grid=(L // tl,),
        in_specs=[pl.BlockSpec((NG, tl, 128), lambda i: (0, i, 0)), pl.BlockSpec(memory_space=pl.ANY)],
        out_specs=(pl.BlockSpec((tl, 256), lambda i: (i, C_DT // 256)), pl.BlockSpec((8, 128), lambda i: (0, 0))),
        out_shape=(jax.ShapeDtypeStruct(dproj.shape, BF16), jax.ShapeDtypeStruct((8, 128), F32)),
        input_output_aliases={1: 0}, name="dt_bwd", compiler_params=_cp(("arbitrary",)),
    )(ddt, dproj)


GW = INNER // NG


def _gnorm_fwd(y, proj, w, tl=256):
    L = y.shape[0]
    zoff = C_Z // 1024

    def kern(y_ref, z_ref, w_ref, o_ref):
        z = z_ref[...].astype(F32)
        yz = y_ref[...] * (z * _sigmoid(z))
        wv = w_ref[...]
        for k in range(1024 // GW):
            sl = slice(GW * k, GW * (k + 1))
            v = yz[:, sl]
            rg = lax.rsqrt(jnp.mean(v * v, axis=-1, keepdims=True) + EPS)
            o_ref[:, sl] = ((v * rg) * wv[:, sl]).astype(BF16)

    blk = pl.BlockSpec((tl, 1024), lambda i, j: (i, j))
    return pl.pallas_call(
        kern, grid=(L // tl, 2),
        in_specs=[blk, pl.BlockSpec((tl, 1024), lambda i, j: (i, zoff + j)), pl.BlockSpec((1, 1024), lambda i, j: (0, j))],
        out_specs=blk, out_shape=jax.ShapeDtypeStruct((L, INNER), BF16), name="gnorm_fwd",
        compiler_params=_cp(("parallel", "parallel")),
    )(y, proj, w.reshape(1, INNER))


def _gnorm_bwd(dyb, y, proj, w, dproj, tl=256):
    L = y.shape[0]
    zoff = C_Z // 1024

    def kern(d_ref, y_ref, z_ref, w_ref, _, dy_ref, dp_ref, gw_ref):
        @pl.when(pl.program_id(1) == 0)
        def _():
            gw_ref[...] = jnp.zeros_like(gw_ref)

        z = z_ref[...].astype(F32)
        sg = _sigmoid(z)
        sz = z * sg
        yv = y_ref[...]
        yz = yv * sz
        dv = d_ref[...]
        wv = w_ref[...]
        for k in range(1024 // GW):
            sl = slice(GW * k, GW * (k + 1))
            v = yz[:, sl]
            rg = lax.rsqrt(jnp.mean(v * v, axis=-1, keepdims=True) + EPS)
            vn = v * rg
            dk = dv[:, sl]
            gw_ref[:, sl] += jnp.broadcast_to(jnp.sum(dk * vn, axis=0, keepdims=True), (8, GW))
            dvn = dk * wv[:, sl]
            dyz = rg * (dvn - vn * jnp.mean(dvn * vn, axis=-1, keepdims=True))
            dy_ref[:, sl] = dyz * sz[:, sl]
            dp_ref[:, sl] = (dyz * yv[:, sl] * (sg[:, sl] * (1.0 + z[:, sl] * (1.0 - sg[:, sl])))).astype(BF16)

    blk = pl.BlockSpec((tl, 1024), lambda j, i: (i, j))
    zblk = pl.BlockSpec((tl, 1024), lambda j, i: (i, zoff + j))
    return pl.pallas_call(
        kern, grid=(2, L // tl),
        in_specs=[blk, blk, zblk, pl.BlockSpec((1, 1024), lambda j, i: (0, j)), pl.BlockSpec(memory_space=pl.ANY)],
        out_specs=(blk, zblk, pl.BlockSpec((8, 1024), lambda j, i: (0, j))),
        out_shape=(jax.ShapeDtypeStruct((L, INNER), F32), jax.ShapeDtypeStruct(dproj.shape, BF16),
                   jax.ShapeDtypeStruct((8, INNER), F32)),
        input_output_aliases={4: 1}, name="gnorm_bwd", compiler_params=_cp(("parallel", "arbitrary")),
    )(dyb, y, proj, w.reshape(1, INNER), dproj)


def _merge_fwd(proj, bg, br_a, br_b, tl=256):
    L = proj.shape[0]
    goff = C_GATE // 1024

    def kern(g1_ref, g2_ref, b1_ref, b2_ref, a_ref, b_ref, o_ref):
        g1 = _sigmoid(g1_ref[...].astype(F32) + b1_ref[...])
        g2 = _sigmoid(g2_ref[...].astype(F32) + b2_ref[...])
        o_ref[...] = (g1 * a_ref[...] + g2 * b_ref[...]).astype(BF16)

    row = pl.BlockSpec((tl, 1024), lambda i: (i, 0))
    bg2 = bg.reshape(1, 2 * D)
    return pl.pallas_call(
        kern, grid=(L // tl,),
        in_specs=[pl.BlockSpec((tl, 1024), lambda i: (i, goff)), pl.BlockSpec((tl, 1024), lambda i: (i, goff + 1)),
                  pl.BlockSpec((1, 1024), lambda i: (0, 0)), pl.BlockSpec((1, 1024), lambda i: (0, 1)), row, row],
        out_specs=row, out_shape=jax.ShapeDtypeStruct((L, D), BF16), name="merge_fwd",
        compiler_params=_cp(("parallel",)),
    )(proj, proj, bg2, bg2, br_a, br_b)


def _merge_bwd(dm, proj, bg, br_a, br_b, dproj, tl=256):
    L = proj.shape[0]
    goff = C_GATE // 1024

    def kern(dm_ref, g_ref, b_ref, a_ref, bb_ref, _, dbr_ref, dp_ref, gb_ref):
        j = pl.program_id(0)

        @pl.when(pl.program_id(1) == 0)
        def _():
            gb_ref[...] = jnp.zeros_like(gb_ref)

        g = _sigmoid(g_ref[...].astype(F32) + b_ref[...])
        br = jnp.where(j == 0, a_ref[...], bb_ref[...])
        dmv = dm_ref[...]
        dbr_ref[0] = (dmv * g).astype(BF16)
        dgate = dmv * br * g * (1.0 - g)
        gb_ref[...] += jnp.broadcast_to(jnp.sum(dgate, axis=0, keepdims=True), (8, 1024))
        dp_ref[...] = dgate.astype(BF16)

    row = pl.BlockSpec((tl, 1024), lambda j, i: (i, 0))
    gblk = pl.BlockSpec((tl, 1024), lambda j, i: (i, goff + j))
    return pl.pallas_call(
        kern, grid=(2, L // tl),
        in_specs=[row, gblk, pl.BlockSpec((1, 1024), lambda j, i: (0, j)), row, row, pl.BlockSpec(memory_space=pl.ANY)],
        out_specs=(pl.BlockSpec((1, tl, 1024), lambda j, i: (j, i, 0)), gblk, pl.BlockSpec((8, 1024), lambda j, i: (0, j))),
        out_shape=(jax.ShapeDtypeStruct((2, L, D), BF16), jax.ShapeDtypeStruct(dproj.shape, BF16),
                   jax.ShapeDtypeStruct((8, 2 * D), F32)),
        input_output_aliases={5: 1}, name="merge_bwd", compiler_params=_cp(("parallel", "arbitrary")),
    )(dm, proj, bg.reshape(1, 2 * D), br_a, br_b, dproj)


def _coords():
    return lax.axis_index("x"), lax.axis_index("y"), lax.axis_index("c")


def _other_chips(sk):
    xk, yk = sk // 2, sk % 2
    return [((1 - xk, yk), 2 * (1 - xk) + yk), ((xk, 1 - yk), 2 * xk + 1 - yk), ((1 - xk, 1 - yk), 2 * (1 - xk) + 1 - yk)]


def _rows(start, size):
    assert size % 128 == 0
    return pl.ds(pl.multiple_of(start, 128), size)


def _per_chip(fn):
    x, y, _ = _coords()
    s = 2 * x + y
    for sk in range(4):
        pl.when(s == sk)(functools.partial(fn, sk))


XTRA = PIECE - PMAIN


def _place(shard, full_shape, block, index_map, idx, name, blk0=0, nblk=None, dep=None):
    in_block = block[-2:]
    if nblk is None:
        nblk = shard.shape[0] // in_block[0]

    def kern(idx_ref, s_ref, *rest):
        o_ref = rest[-1]
        o_ref[...] = s_ref[...].astype(BF16).reshape(o_ref.shape)

    grid_spec = pltpu.PrefetchScalarGridSpec(
        num_scalar_prefetch=1, grid=(nblk,),
        in_specs=[pl.BlockSpec(in_block, lambda i, idx_ref: (blk0 + i, 0))] + ([_ANY] if dep is not None else []),
        out_specs=pl.BlockSpec(block, index_map))
    args = (idx, shard) + ((dep,) if dep is not None else ())
    return pl.pallas_call(kern, grid_spec=grid_spec, out_shape=jax.ShapeDtypeStruct(full_shape, BF16), name=name,
                          compiler_params=_cp(("arbitrary",)))(*args)


_SEM = pl.BlockSpec(memory_space=pltpu.SEMAPHORE)
_EFFECT = pltpu.SideEffectType.DATAFLOW_SIDE_EFFECTING


_ANY = pl.BlockSpec(memory_space=pl.ANY)


def _tie(v, dep, name):
    def body(v_ref, dep_ref, o_ref):
        del v_ref, dep_ref, o_ref

    return pl.pallas_call(body, out_shape=jax.ShapeDtypeStruct(v.shape, v.dtype), in_specs=[_ANY, _ANY],
                          out_specs=_ANY, input_output_aliases={0: 0}, name=name)(v, dep)


def _split_call(name, arrays, start=None, wait=None, wait_sems=None, after=None):
    keys = list(arrays)
    n = len(keys)
    n_start = start.n if start is not None else 0
    afters = [] if after is None else (list(after) if isinstance(after, (list, tuple)) else [after])

    def body(*refs):
        pos = n
        if wait is not None:
            wss, wrs = refs[pos], refs[pos + 1]
            pos += 2
        pos += len(afters)
        if start is not None:
            nss, nrs = refs[pos], refs[pos + 1]
            pos += 2
        R = dict(zip(keys, refs[pos:pos + n]))
        token = refs[pos + n]
        x, y, c = _coords()

        def desc(src, dst, dev, ss, rs, k):
            return pltpu.make_async_remote_copy(src_ref=src, dst_ref=dst, send_sem=ss.at[k], recv_sem=rs.at[k],
                                                device_id=dev, device_id_type=MESH)

        def run(sk):
            if wait is not None:
                for k, (snd, land) in enumerate(wait.copies(sk, R)):
                    if snd is not None:
                        desc(snd[0], snd[1], snd[2], wss, wrs, k).wait_send()
                    if land is not None:
                        desc(land, land, (x, y, c), wss, wrs, k).wait_recv()
            if start is not None:
                for k, (snd, land) in enumerate(start.copies(sk, R)):
                    if snd is not None:
                        desc(snd[0], snd[1], snd[2], nss, nrs, k).start()

        _per_chip(run)
        token[...] = jnp.zeros_like(token)

    hbm = pl.BlockSpec(memory_space=HBM)
    vals = [arrays[k] for k in keys]
    ins, in_specs = list(vals), [hbm] * n
    if wait is not None:
        ins += list(wait_sems)
        in_specs += [_SEM, _SEM]
    ins += afters
    in_specs += [pl.BlockSpec(memory_space=pl.ANY)] * len(afters)
    out_shape, out_specs = [], []
    if start is not None:
        out_shape += [pltpu.SemaphoreType.DMA((n_start,)), pltpu.SemaphoreType.DMA((n_start,))]
        out_specs += [_SEM, _SEM]
    first = len(out_shape)
    out_shape += [jax.ShapeDtypeStruct(v.shape, v.dtype) for v in vals] + [jax.ShapeDtypeStruct((8, 128), F32)]
    out_specs += [hbm] * n + [pl.BlockSpec(memory_space=pltpu.VMEM)]
    res = pl.pallas_call(
        body, out_shape=tuple(out_shape), in_specs=in_specs, out_specs=tuple(out_specs),
        input_output_aliases={i: first + i for i in range(n)}, name=name,
        compiler_params=pltpu.CompilerParams(has_side_effects=_EFFECT),
    )(*ins)
    sems = (res[0], res[1]) if start is not None else None
    return dict(zip(keys, res[first:first + n])), sems, res[-1]


class _Plan:
    def __init__(self, n, copies):
        self.n, self.copies = n, copies


_HM, _HX = PMAIN // 2, XTRA // 2
WAVE0 = 768
WAVES = ((0, WAVE0), (WAVE0, _HM - WAVE0))
_WIN = {
    "wq0": (True, "wct", lambda r, sc, hc: r.at[_rows(PMAIN * sc + _HM * hc + WAVES[0][0], WAVES[0][1]), :]),
    "wq1": (True, "wct", lambda r, sc, hc: r.at[_rows(PMAIN * sc + _HM * hc + WAVES[1][0], WAVES[1][1]), :]),
    "xt": (True, "xt", lambda r, sc, hc: r.at[sc, _rows(_HX * hc, _HX), :]),
    "w1": (True, "w1", lambda r, sc, hc: r.at[_rows(512 * hc, 512), pl.ds(1024 * sc, 1024)]),
    "w2": (True, "w2", lambda r, sc, hc: r.at[_rows(1024 * sc + 512 * hc, 512), :]),
    "wa": (True, "wa", lambda r, sc, hc: r.at[_rows(256 * sc + 128 * hc, 128), :]),
    "wb": (True, "wb", lambda r, sc, hc: r.at[_rows(512 * sc + 256 * hc, 256), :]),
    "wo": (True, "wo", lambda r, sc, hc: r.at[_rows(256 * sc + 128 * hc, 128), :]),
    "cw": (False, "cw", lambda r, sc, hc: r.at[sc]),
}


def _ag_chips_plan(keys):
    def copies(sk, R):
        _, _, c = _coords()
        out = []
        for key in keys:
            _, arr, win = _WIN[key]
            for (px, py), ps in _other_chips(sk):
                w = win(R[arr], sk, c)
                out.append(((w, w, (px, py, c)), win(R[arr], ps, c)))
        return out
    return _Plan(3 * len(keys), copies)


def _ag_sibling_plan(keys):
    keys = [k for k in keys if _WIN[k][0]]

    def copies(sk, R):
        x, y, c = _coords()
        out = []
        for key in keys:
            _, arr, win = _WIN[key]
            for _, ps in _other_chips(sk):
                w = win(R[arr], ps, c)
                out.append(((w, w, (x, y, 1 - c)), win(R[arr], ps, 1 - c)))
        return out
    return _Plan(3 * len(keys), copies)


def _in_proj_wave(h, wct, wave, proj=None, tm=1024):
    L = h.shape[0]
    tm = min(tm, L)
    off, size = WAVES[wave]
    start = lambda j: pl.multiple_of(_HM * j + off, 128)

    def kern(h_ref, w_ref, *rest):
        o_ref = rest[-1]
        o_ref[...] = lax.dot_general(h_ref[...], w_ref[...], _DIMS["nt"], preferred_element_type=F32).astype(BF16)

    in_specs = [pl.BlockSpec((tm, D), lambda j, i: (i, 0)),
                pl.BlockSpec((pl.Element(size), pl.Element(D)), lambda j, i: (start(j), 0))]
    args, aliases = [h, wct], {}
    if proj is not None:
        in_specs.append(pl.BlockSpec(memory_space=pl.ANY))
        args.append(proj)
        aliases = {2: 0}
    return pl.pallas_call(
        kern, grid=(8, L // tm), in_specs=in_specs,
        out_specs=pl.BlockSpec((pl.Element(tm), pl.Element(size)), lambda j, i: (i * tm, start(j))),
        out_shape=jax.ShapeDtypeStruct((L, NCW), BF16), input_output_aliases=aliases,
        name="in_proj_wave%d" % wave, compiler_params=_cp(("parallel", "parallel")),
    )(*args)


def _fix_wct(wct, xt):
    nb = PMAIN // XTRA

    def kern(w_ref, x_ref, o_ref):
        k = pl.program_id(0)
        xv = x_ref[0]
        o_ref[...] = jnp.where(k < 3, (w_ref[...].astype(F32) + xv.astype(F32)).astype(BF16), xv)

    blk = pl.BlockSpec((XTRA, D), lambda k: (nb * (k + 1), 0))
    rblk = pl.BlockSpec((XTRA, D), lambda k: (jnp.where(k < 3, nb * (k + 1), 0), 0))
    return pl.pallas_call(
        kern, grid=(4,), in_specs=[rblk, pl.BlockSpec((1, XTRA, D), lambda k: (k, 0, 0))], out_specs=blk,
        out_shape=jax.ShapeDtypeStruct(wct.shape, BF16), input_output_aliases={0: 0}, name="fix_wct",
        compiler_params=_cp(("arbitrary",)),
    )(wct, xt)


_HP = PIECE // 2
_GWIN = [
    lambda r, sc, hc: r.at[_rows(PMAIN * sc + _HP * hc, _HP), :],
    lambda r, sc, hc: r.at[_rows(512 * hc, 512), pl.ds(1024 * sc, 1024)],
    lambda r, sc, hc: r.at[_rows(1024 * sc + 512 * hc, 512), :],
    lambda r, sc, hc: r.at[_rows(256 * sc + 128 * hc, 128), :],
    lambda r, sc, hc: r.at[_rows(512 * sc + 256 * hc, 256), :],
    lambda r, sc, hc: r.at[_rows(256 * sc + 128 * hc, 128), :],
]
HALF_SHAPES = [(PIECE // 2, D), (512, 1024), (512, 1024), (128, 1024), (256, 1024), (128, 1024)]


def _rs_sibling_plan(ts):
    def copies(sk, R):
        x, y, c = _coords()
        out = []
        for t in ts:
            for sc in range(4):
                land = R["ra%d" % t].at[sc]
                out.append(((_GWIN[t](R["g%d" % t], sc, 1 - c), land, (x, y, 1 - c)), land))
        return out
    return _Plan(4 * len(ts), copies)


def _rs_chips_plan(ts):
    def copies(sk, R):
        _, _, c = _coords()
        out = []
        for t in ts:
            for j, ((px, py), ps) in enumerate(_other_chips(sk)):
                land = R["rb%d" % t].at[j]
                out.append(((R["hb%d" % t].at[ps], land, (px, py, c)), land))
        return out
    return _Plan(3 * len(ts), copies)


def _rs_share_plan(ts):
    def copies(sk, R):
        x, y, c = _coords()
        out = []
        for t in ts:
            rows = HALF_SHAPES[t][0]
            mine = R["f%d" % t].at[_rows(rows * c, rows), :]
            out.append(((mine, mine, (x, y, 1 - c)), R["f%d" % t].at[_rows(rows * (1 - c), rows), :]))
        return out
    return _Plan(len(ts), copies)


def _half_tiling(t):
    rows, cols = HALF_SHAPES[t]
    if t == 0:
        return (rows // 2, cols), 2, lambda i: (i, 0)
    return (rows, cols), 1, lambda i: (0, 0)


def _window_spec(t, blk):
    if t == 0:
        return pl.BlockSpec((pl.Element(blk[0]), pl.Element(blk[1])), lambda i, sc, idx_ref: (
            pl.multiple_of(PMAIN * sc + _HP * idx_ref[1] + blk[0] * i, 128), 0))
    if t == 1:
        return pl.BlockSpec(blk, lambda i, sc, idx_ref: (idx_ref[1], sc))
    return pl.BlockSpec(blk, lambda i, sc, idx_ref: (2 * sc + idx_ref[1], 0))


def _chip_sum(g, ra, t, idx, name):
    rows, cols = HALF_SHAPES[t]
    blk, nblk, inner = _half_tiling(t)

    def kern(idx_ref, g_ref, r_ref, hb_ref, hf_ref):
        v = g_ref[...].astype(F32) + r_ref[0].astype(F32)
        hb_ref[0] = v.astype(BF16)

        @pl.when(pl.program_id(1) == idx_ref[0])
        def _():
            hf_ref[...] = v

    omap = lambda i, sc, idx_ref: (sc,) + inner(i)
    grid_spec = pltpu.PrefetchScalarGridSpec(
        num_scalar_prefetch=1, grid=(nblk, 4),
        in_specs=[_window_spec(t, blk), pl.BlockSpec((1,) + blk, omap)],
        out_specs=(pl.BlockSpec((1,) + blk, omap), pl.BlockSpec(blk, lambda i, sc, idx_ref: inner(i))))
    return pl.pallas_call(
        kern, grid_spec=grid_spec,
        out_shape=(jax.ShapeDtypeStruct((4, rows, cols), BF16), jax.ShapeDtypeStruct((rows, cols), F32)),
        name=name, compiler_params=_cp(("parallel", "arbitrary")),
    )(idx, g, ra)


def _final_sum(hf, rb, t, idx, name):
    rows, cols = HALF_SHAPES[t]
    blk, nblk, inner = _half_tiling(t)
    nbr = rows // blk[0]

    def kern(idx_ref, h_ref, r_ref, o_ref):
        o_ref[...] = ((h_ref[...] + r_ref[0].astype(F32)) + r_ref[1].astype(F32)) + r_ref[2].astype(F32)

    def omap(i, idx_ref):
        r, cidx = inner(i)
        return nbr * idx_ref[1] + r, cidx

    grid_spec = pltpu.PrefetchScalarGridSpec(
        num_scalar_prefetch=1, grid=(nblk,),
        in_specs=[pl.BlockSpec(blk, lambda i, idx_ref: inner(i)),
                  pl.BlockSpec((3,) + blk, lambda i, idx_ref: (0,) + inner(i))],
        out_specs=pl.BlockSpec(blk, omap))
    return pl.pallas_call(
        kern, grid_spec=grid_spec, out_shape=jax.ShapeDtypeStruct((2 * rows, cols), F32),
        name=name, compiler_params=_cp(("parallel",)),
    )(idx, hf, rb)


class _ReduceScatter:
    def __init__(self, ts, grads, idx, tag):
        self.ts, self.idx, self.tag = ts, idx, tag
        arr = {}
        for t in ts:
            arr["g%d" % t] = grads[t]
            arr["ra%d" % t] = lax.empty((4,) + HALF_SHAPES[t], BF16)
        self.plan = _rs_sibling_plan(ts)
        self.arr, self.sems, self.token = _split_call("rs_sibling_start_" + tag, arr, start=self.plan)

    def chips(self, after):
        arr, _, _ = _split_call("rs_sibling_wait_" + self.tag, self.arr, wait=self.plan, wait_sems=self.sems, after=after)
        brr, self.hf = {}, {}
        for t in self.ts:
            hb, self.hf[t] = _chip_sum(arr["g%d" % t], arr["ra%d" % t], t, self.idx, "chip_sum_%d" % t)
            brr["hb%d" % t] = hb
            brr["rb%d" % t] = lax.empty((3,) + HALF_SHAPES[t], BF16)
        self.plan = _rs_chips_plan(self.ts)
        self.arr, self.sems, self.token = _split_call("rs_chips_start_" + self.tag, brr, start=self.plan)
        return self.token

    def share(self, after):
        brr, _, _ = _split_call("rs_chips_wait_" + self.tag, self.arr, wait=self.plan, wait_sems=self.sems, after=after)
        frr = {"f%d" % t: _final_sum(self.hf[t], brr["rb%d" % t], t, self.idx, "final_sum_%d" % t) for t in self.ts}
        self.plan = _rs_share_plan(self.ts)
        self.arr, self.sems, self.token = _split_call("rs_share_start_" + self.tag, frr, start=self.plan)
        return self.token

    def result(self, after):
        frr, _, _ = _split_call("rs_share_wait_" + self.tag, self.arr, wait=self.plan, wait_sems=self.sems, after=after)
        return {t: frr["f%d" % t] for t in self.ts}


def _all8_plan(key):
    def copies(sk, R):
        x, y, c = _coords()
        own = R[key].at[4 * x + 2 * y + c]
        out = []
        for k in range(1, 8):
            dev = ((1 - x) if (k >> 2) & 1 else x, (1 - y) if (k >> 1) & 1 else y, (1 - c) if k & 1 else c)
            out.append(((own, own, dev), R[key].at[4 * dev[0] + 2 * dev[1] + dev[2]]))
        return out
    return _Plan(7, copies)


def _small_all_gather(v):
    def body(v_ref, o_ref, send_sems, recv_sems, loc_sem):
        x, y, c = _coords()
        me = 4 * x + 2 * y + c
        lc = pltpu.make_async_copy(v_ref, o_ref.at[me], loc_sem)
        lc.start()
        cps = []
        for k in range(1, 8):
            fx, fy, fc = (k >> 2) & 1, (k >> 1) & 1, k & 1
            dev = ((1 - x) if fx else x, (1 - y) if fy else y, (1 - c) if fc else c)
            cp = pltpu.make_async_remote_copy(src_ref=v_ref, dst_ref=o_ref.at[me], send_sem=send_sems.at[k - 1],
                                              recv_sem=recv_sems.at[k - 1], device_id=dev, device_id_type=MESH)
            cp.start()
            cps.append((cp, 4 * dev[0] + 2 * dev[1] + dev[2]))
        for k, (cp, frm) in enumerate(cps):
            got = o_ref.at[frm]
            pltpu.make_async_remote_copy(src_ref=got, dst_ref=got, send_sem=send_sems.at[k], recv_sem=recv_sems.at[k],
                                         device_id=(x, y, c), device_id_type=MESH).wait_recv()
        for cp, _ in cps:
            cp.wait_send()
        lc.wait()

    hbm = pl.BlockSpec(memory_space=HBM)
    return pl.pallas_call(
        body, out_shape=jax.ShapeDtypeStruct((8,) + v.shape, F32), in_specs=[hbm], out_specs=hbm,
        scratch_shapes=[pltpu.SemaphoreType.DMA((7,)), pltpu.SemaphoreType.DMA((7,)), pltpu.SemaphoreType.DMA(())],
        name="small_all_gather", compiler_params=pltpu.CompilerParams(has_side_effects=True),
    )(v)


def _sum8(v, name="small_sum"):
    def kern(v_ref, o_ref):
        acc = v_ref[0]
        for k in range(1, 8):
            acc = acc + v_ref[k]
        o_ref[...] = acc

    return pl.pallas_call(kern, out_shape=jax.ShapeDtypeStruct(v.shape[1:], F32), name=name)(v)


def _adamw(w, g, m, v, name, tr=128, blk0=0, nblk=None, into=None, copy_g=False):
    R, C = w.shape
    tr = min(tr, R)
    if nblk is None:
        assert R % tr == 0 and blk0 == 0
        nblk = R // tr
    n_out = 4 if copy_g else 3

    def kern(*refs):
        w_ref, g_ref, m_ref, v_ref = refs[:4]
        d_ref, mo_ref, vo_ref = refs[-n_out:][:3]
        gv = g_ref[...]
        mn = ADAM_B1 * m_ref[...] + (1.0 - ADAM_B1) * gv
        vn = ADAM_B2 * v_ref[...] + (1.0 - ADAM_B2) * (gv * gv)
        m_hat = mn / (1.0 - ADAM_B1 ** ADAM_STEP)
        v_hat = vn / (1.0 - ADAM_B2 ** ADAM_STEP)
        d_ref[...] = -ADAM_LR * (m_hat / (jnp.sqrt(v_hat) + ADAM_EPS) + ADAM_WD * w_ref[...])
        mo_ref[...] = mn
        vo_ref[...] = vn
        if copy_g:
            refs[-1][...] = gv

    blk = pl.BlockSpec((tr, C), lambda i: (blk0 + i, 0))
    sd = jax.ShapeDtypeStruct((R, C), F32)
    in_specs, args, aliases = [blk] * 4, [w, g, m, v], {}
    if into is not None:
        in_specs += [pl.BlockSpec(memory_space=pl.ANY)] * 3
        args += list(into)
        aliases = {4: 0, 5: 1, 6: 2}
    return pl.pallas_call(kern, grid=(nblk,), in_specs=in_specs, out_specs=(blk,) * n_out, out_shape=(sd,) * n_out,
                          input_output_aliases=aliases, name=name, compiler_params=_cp(("parallel",)))(*args)


def _adamw_w_in(wt, gp, mt, vt, offs, name, r0, tr, nblk, views, into=None):
    el = lambda n: (pl.Element(n), pl.Element(D))
    own = pl.BlockSpec(el(tr), lambda i, o: (pl.multiple_of(r0 + tr * i, 8), 0))

    def view(k):
        return pl.BlockSpec(el(tr), lambda i, o: (pl.multiple_of(jnp.maximum(r0 + tr * i + o[k], 0), 8), 0))

    def kern(o_ref, w_ref, m_ref, v_ref, *refs):
        g_refs, (d_ref, mo_ref, vo_ref, go_ref) = refs[:len(views)], refs[-4:]
        gv = g_refs[0][...]
        if len(views) == 2:
            row = r0 + tr * pl.program_id(0) + lax.broadcasted_iota(jnp.int32, (tr, D), 0)
            gv = jnp.where(row < o_ref[2], gv, g_refs[1][...])
        mn = ADAM_B1 * m_ref[...] + (1.0 - ADAM_B1) * gv
        vn = ADAM_B2 * v_ref[...] + (1.0 - ADAM_B2) * (gv * gv)
        m_hat = mn / (1.0 - ADAM_B1 ** ADAM_STEP)
        v_hat = vn / (1.0 - ADAM_B2 ** ADAM_STEP)
        d_ref[...] = -ADAM_LR * (m_hat / (jnp.sqrt(v_hat) + ADAM_EPS) + ADAM_WD * w_ref[...])
        mo_ref[...] = mn
        vo_ref[...] = vn
        go_ref[...] = gv

    in_specs = [own, own, own] + [view(k) for k in views]
    args = [wt, mt, vt] + [gp] * len(views)
    aliases = {}
    if into is not None:
        in_specs += [pl.BlockSpec(memory_space=pl.ANY)] * 4
        args += list(into)
        aliases = {1 + len(args) - 4 + j: j for j in range(4)}
    grid_spec = pltpu.PrefetchScalarGridSpec(num_scalar_prefetch=1, grid=(nblk,), in_specs=in_specs,
                                             out_specs=(own,) * 4)
    sd = jax.ShapeDtypeStruct(wt.shape, F32)
    return pl.pallas_call(kern, grid_spec=grid_spec, out_shape=(sd,) * 4, input_output_aliases=aliases, name=name,
                          compiler_params=_cp(("parallel",)))(offs, *args)


def _to_piece(wt, s):
    z = lambda n: jnp.zeros((n, D), wt.dtype)
    pads = [functools.partial(lambda k, w: jnp.pad(w, ((8 * k, PIECE - W_SHARD - 8 * k), (0, 0))), k) for k in range(3)]
    last = lambda w: jnp.concatenate([z(24), w[:744], w[776:], w[744:776], z(PIECE - 24 - W_SHARD)], axis=0)
    return lax.switch(s, pads + [last], wt)


def _from_piece(p, s):
    cuts = [functools.partial(lambda k, q: q[8 * k:8 * k + W_SHARD], k) for k in range(3)]
    last = lambda q: jnp.concatenate([q[24:768], q[2816:2848], q[768:2816]], axis=0)
    return lax.switch(s, cuts + [last], p)


_SMALL = [("b_gate", 2048), ("ssm_conv_b", 4096), ("dt_bias", 32), ("A_log", 32), ("D_skip", 32),
          ("ssm_norm_w", 2048), ("norm_mlp", 1024), ("norm_final", 1024), ("sc_conv_w", 3072), ("ssm_conv_w", 16384),
          ("loss", 1)]


def _pack(vals, table, rows):
    parts = []
    for name, n in table:
        v = vals[name].reshape(-1).astype(F32)
        pad = (-n) % 128
        parts.append(jnp.pad(v, (0, pad)) if pad else v)
    flat = jnp.concatenate(parts)
    return jnp.pad(flat, (0, rows * 128 - flat.shape[0])).reshape(rows, 128)


def _unpack(arr, table):
    flat = arr.reshape(-1)
    out, off = {}, 0
    for name, n in table:
        out[name] = flat[off:off + n]
        off += n + ((-n) % 128)
    return out


def kernel(x, norm_mix, w_in, b_gate, sc_conv_w, ssm_conv_w, ssm_conv_b, dt_bias, A_log, D_skip, ssm_norm_w, w_branch_sc, w_branch_ssm, w_out, norm_mlp, w_mlp1, w_mlp2, norm_final, loss_target, m_norm_mix, m_w_in, m_b_gate, m_sc_conv_w, m_ssm_conv_w, m_ssm_conv_b, m_dt_bias, m_A_log, m_D_skip, m_ssm_norm_w, m_w_branch_sc, m_w_branch_ssm, m_w_out, m_norm_mlp, m_w_mlp1, m_w_mlp2, m_norm_final, v_norm_mix, v_w_in, v_b_gate, v_sc_conv_w, v_ssm_conv_w, v_ssm_conv_b, v_dt_bias, v_A_log, v_D_skip, v_ssm_norm_w, v_w_branch_sc, v_w_branch_ssm, v_w_out, v_norm_mlp, v_w_mlp1, v_w_mlp2, v_norm_final):
    L = x.shape[1]
    nc = L // Q
    xi, yi, ci = lax.axis_index("x"), lax.axis_index("y"), lax.axis_index("c")
    s = 2 * xi + yi
    idx = jnp.stack([s, ci]).astype(jnp.int32)
    x0 = x.reshape(L, D)
    tgt = loss_target.reshape(L, D)

    piece = _to_piece(w_in.T, s)
    nb = PMAIN // XTRA
    wct0 = _place(piece, (NCW, D), (XTRA, D), lambda i, r: (nb * r[0] + i, 0), idx, "place_wct", nblk=nb)
    xt0 = _place(piece, (4, XTRA, D), (1, XTRA, D), lambda i, r: (r[0], 0, 0), idx, "place_xt", blk0=nb, nblk=1)
    cws = jnp.zeros((8, 1280), F32)
    cws = cws.at[0:3, 0:256].set(sc_conv_w).at[0:4, 256:1280].set(ssm_conv_w)
    cw0 = lax.dynamic_update_slice(jnp.zeros((4, 8, 1280), F32), cws[None], (s, 0, 0))
    win_keys, win2_keys, mid_keys, end_keys = ["xt", "cw", "wq0"], ["wq1"], ["wa", "wb", "wo", "w1"], ["w2"]
    gw, sems_w, tok = _split_call("ag_win_start", {"wct": wct0, "xt": xt0, "cw": cw0}, start=_ag_chips_plan(win_keys))
    g2, sems_w2, tok = _split_call("ag_win2_start", {"wct": gw["wct"]}, start=_ag_chips_plan(win2_keys), after=tok)
    gw["wct"] = g2["wct"]
    wa0 = _place(w_branch_sc, (D, D), (256, 1024), lambda i, r: (r[0], 0), idx, "place_wa", dep=tok)
    wb0 = _place(w_branch_ssm, (INNER, D), (512, 1024), lambda i, r: (r[0], 0), idx, "place_wb", dep=tok)
    wo0 = _place(w_out, (D, D), (256, 1024), lambda i, r: (r[0], 0), idx, "place_wo", dep=tok)
    w10 = _place(w_mlp1, (D, DFF), (256, 1024), lambda i, r: (i, r[0]), idx, "place_w1", dep=tok)
    gm, sems_m, tok = _split_call("ag_mid_start", {"wa": wa0, "wb": wb0, "wo": wo0, "w1": w10},
                                  start=_ag_chips_plan(mid_keys))
    w20 = _place(w_mlp2, (DFF, D), (256, 1024), lambda i, r: (4 * r[0] + i, 0), idx, "place_w2", dep=tok)
    ge, sems_e, tok = _split_call("ag_end_start", {"w2": w20}, start=_ag_chips_plan(end_keys))
    h = _rms_fwd(x0, norm_mix, "rms_mix", dep=tok)
    gw, sems_w, tok = _split_call("ag_win_pass", gw, wait=_ag_chips_plan(win_keys), wait_sems=sems_w,
                                  start=_ag_sibling_plan(win_keys), after=h)
    gw, _, _ = _split_call("ag_win_done", gw, wait=_ag_sibling_plan(win_keys), wait_sems=sems_w, after=tok)
    wc, cw_all = _fix_wct(gw["wct"], gw["xt"]), gw["cw"]
    sc_w_full = jnp.concatenate([cw_all[k, :, 0:256] for k in range(4)], axis=1)
    ssm_w_full = jnp.concatenate([cw_all[k, :, 256:1280] for k in range(4)], axis=1)
    cw4 = ssm_w_full.at[4].set(ssm_conv_b)
    vec = jnp.zeros((8, 128), F32).at[0, :NH].set(dt_bias).at[1, :NH].set(A_log)
    vecg = jnp.zeros((NG, 8, 128), F32).at[:, 0, :4].set(A_log.reshape(NG, 4)).at[:, 1, :4].set(D_skip.reshape(NG, 4))

    dtraw = _matmul(h, wc[C_DT:], "nt", F32, 512, 256, 1024, "in_proj_dt")
    proj = _in_proj_wave(h, wc, 0)
    g2, sems_w2, tok = _split_call("ag_win2_pass", {"wct": wc}, wait=_ag_chips_plan(win2_keys), wait_sems=sems_w2,
                                   start=_ag_sibling_plan(win2_keys), after=[proj, dtraw])
    g2, _, _ = _split_call("ag_win2_done", g2, wait=_ag_sibling_plan(win2_keys), wait_sems=sems_w2, after=tok)
    wc = g2["wct"]
    proj = _in_proj_wave(h, wc, 1, proj=proj)
    ya = _sc_fwd(proj, sc_w_full)
    xbc = _ssm_conv_fwd(proj, cw4)
    dt4, cs4, sg4 = _dt_prep(dtraw, vec)
    gm, sems_m, tok = _split_call("ag_mid_pass", gm, wait=_ag_chips_plan(mid_keys), wait_sems=sems_m,
                                  start=_ag_sibling_plan(mid_keys), after=[xbc, ya, dt4])
    xbc = _tie(xbc, tok, "tie_xbc")
    y, s_all = _ssd_fwd(xbc, dt4, cs4, vecg)
    yb = _gnorm_fwd(y, proj, ssm_norm_w)
    gm, _, _ = _split_call("ag_mid_done", gm, wait=_ag_sibling_plan(mid_keys), wait_sems=sems_m, after=yb)
    wa, wb, wo, w1 = gm["wa"], gm["wb"], gm["wo"], gm["w1"]
    ge, sems_e, tok = _split_call("ag_end_pass", ge, wait=_ag_chips_plan(end_keys), wait_sems=sems_e,
                                  start=_ag_sibling_plan(end_keys), after=yb)
    br_a = _matmul(ya, wa, "nn", F32, 1024, 1024, 1024, "branch_sc", dep=tok)
    br_b = _matmul(yb, wb, "nn", F32, 1024, 1024, 2048, "branch_ssm")
    merged = _merge_fwd(proj, b_gate, br_a, br_b)
    x1 = _matmul(merged, wo, "nn", F32, 1024, 1024, 1024, "out_proj", epi="res", extra=x0)
    h2 = _rms_fwd(x1, norm_mlp, "rms_mlp")
    a1, rl = _matmul(h2, w1, "nn", BF16, 1024, 1024, 1024, "mlp1", epi="relu2", n_outer=True)
    ge, _, _ = _split_call("ag_end_done", ge, wait=_ag_sibling_plan(end_keys), wait_sems=sems_e, after=a1)
    w2 = ge["w2"]
    x2 = _matmul(rl, w2, "nn", F32, 512, 1024, 4096, "mlp2", epi="res", extra=x1)
    dx2, g_nf, loss8 = _final(x2, norm_final, tgt)

    da = _matmul(dx2, w2, "nt", BF16, 1024, 1024, 1024, "mlp2_dx", epi="drelu", extra=a1, n_outer=True)
    g_w2 = _matmul(rl, dx2, "tn", BF16, 1024, 1024, 2048, "mlp2_dw")
    g_w1 = _matmul(h2, da, "tn", BF16, 1024, 1024, 2048, "mlp1_dw")
    dh2 = _matmul(da, w1, "nt", F32, 512, 1024, 4096, "mlp1_dx")
    dx1, g_nmlp = _rms_bwd(dh2, x1, norm_mlp, dx2, "rms_mlp_bwd")
    dmerged = _matmul(dx1, wo, "nt", F32, 1024, 1024, 1024, "out_proj_dx")
    g_wo = _matmul(merged, dx1, "tn", BF16, 1024, 1024, 2048, "out_proj_dw")
    dproj = lax.empty((L, NCW), BF16)
    dbr, dproj, g_bg = _merge_bwd(dmerged, proj, b_gate, br_a, br_b, dproj)
    dya = _matmul(dbr[0], wa, "nt", F32, 1024, 1024, 1024, "branch_sc_dx")
    g_wa = _matmul(ya, dbr[0], "tn", BF16, 1024, 1024, 2048, "branch_sc_dw")
    dproj, g_scw = _sc_bwd(dya, proj, sc_w_full, dproj)
    dyb = _matmul(dbr[1], wb, "nt", F32, 1024, 1024, 1024, "branch_ssm_dx", n_outer=True)
    g_wb = _matmul(yb, dbr[1], "tn", BF16, 1024, 1024, 2048, "branch_ssm_dw")
    rs_a = _ReduceScatter([1, 2, 3, 4, 5], {1: g_w1, 2: g_w2, 3: g_wa, 4: g_wb, 5: g_wo}, idx, "a")
    dy, dproj, g_snw = _gnorm_bwd(_tie(dyb, rs_a.token, "tie_dyb"), y, proj, ssm_norm_w, dproj)
    tok = rs_a.chips(after=dy)
    dxs, dbm, dcm, ddt_g, st = _ssd_bwd(xbc, dt4, cs4, sg4, vecg, s_all, _tie(dy, tok, "tie_dy"))
    dproj, gx1 = _ssm_conv_bwd(dxs, proj, cw4, dproj, 0, "ssm_conv_bwd_x")
    dproj, gx2 = _ssm_conv_bwd(dbm, proj, cw4, dproj, INNER, "ssm_conv_bwd_b")
    dproj, gx3 = _ssm_conv_bwd(dcm, proj, cw4, dproj, INNER + NG * NS, "ssm_conv_bwd_c")
    g_cw4 = jnp.concatenate([gx1, gx2, gx3], axis=1)
    dproj, g_dtb = _dt_bwd(ddt_g, dproj)
    small = {"b_gate": g_bg[0], "ssm_conv_b": g_cw4[4], "dt_bias": g_dtb[0, :NH],
             "A_log": st[:, 0, :4], "D_skip": st[:, 1, :4], "ssm_norm_w": g_snw[0], "norm_mlp": g_nmlp[0],
             "norm_final": g_nf[0], "sc_conv_w": g_scw[0:3], "ssm_conv_w": g_cw4[0:4], "loss": loss8[0, 0:1]}
    small_sum = _sum8(_small_all_gather(_pack(small, _SMALL, SMALL_ROWS)))
    gs = _unpack(small_sum, _SMALL)
    g_wc = _matmul(dproj, h, "tn", BF16, 1280, 1024, 2048, "in_proj_dw", dep=small_sum)
    rs_b = _ReduceScatter([0], {0: g_wc}, idx, "b")
    tok = rs_a.share(after=rs_b.token)
    tok = rs_b.chips(after=tok)
    dh = _matmul(dproj, wc, "nn", F32, 512, 1024, 5760, "in_proj_dx", dep=tok)
    grad_x, g_nm = _rms_bwd(dh, x0, norm_mix, dx1, "rms_mix_bwd")
    me = 4 * xi + 2 * yi + ci
    nm8 = lax.dynamic_update_slice(jnp.zeros((8, 8, 128), F32), g_nm[0].reshape(1, 8, 128), (me, 0, 0))
    nm_arr, nm_sems, tok = _split_call("norm_mix_start", {"nm": nm8}, start=_all8_plan("nm"))
    red = rs_a.result(after=tok)
    big = {"w_mlp1": red[1], "w_mlp2": red[2], "w_branch_sc": red[3], "w_branch_ssm": red[4], "w_out": red[5]}

    given = dict(norm_mix=norm_mix, w_in=w_in, b_gate=b_gate, sc_conv_w=sc_conv_w, ssm_conv_w=ssm_conv_w, ssm_conv_b=ssm_conv_b, dt_bias=dt_bias, A_log=A_log, D_skip=D_skip, ssm_norm_w=ssm_norm_w, w_branch_sc=w_branch_sc, w_branch_ssm=w_branch_ssm, w_out=w_out, norm_mlp=norm_mlp, w_mlp1=w_mlp1, w_mlp2=w_mlp2, norm_final=norm_final,
                 m_norm_mix=m_norm_mix, m_w_in=m_w_in, m_b_gate=m_b_gate, m_sc_conv_w=m_sc_conv_w, m_ssm_conv_w=m_ssm_conv_w, m_ssm_conv_b=m_ssm_conv_b, m_dt_bias=m_dt_bias, m_A_log=m_A_log, m_D_skip=m_D_skip, m_ssm_norm_w=m_ssm_norm_w, m_w_branch_sc=m_w_branch_sc, m_w_branch_ssm=m_w_branch_ssm, m_w_out=m_w_out, m_norm_mlp=m_norm_mlp, m_w_mlp1=m_w_mlp1, m_w_mlp2=m_w_mlp2, m_norm_final=m_norm_final,
                 v_norm_mix=v_norm_mix, v_w_in=v_w_in, v_b_gate=v_b_gate, v_sc_conv_w=v_sc_conv_w, v_ssm_conv_w=v_ssm_conv_w, v_ssm_conv_b=v_ssm_conv_b, v_dt_bias=v_dt_bias, v_A_log=v_A_log, v_D_skip=v_D_skip, v_ssm_norm_w=v_ssm_norm_w, v_w_branch_sc=v_w_branch_sc, v_w_branch_ssm=v_w_branch_ssm, v_w_out=v_w_out, v_norm_mlp=v_norm_mlp, v_w_mlp1=v_w_mlp1, v_w_mlp2=v_w_mlp2, v_norm_final=v_norm_final)
    order = ["norm_mix", "w_in", "b_gate", "sc_conv_w", "ssm_conv_w", "ssm_conv_b", "dt_bias", "A_log", "D_skip",
             "ssm_norm_w", "w_branch_sc", "w_branch_ssm", "w_out", "norm_mlp", "w_mlp1", "w_mlp2", "norm_final"]
    grad, delta, new_m, new_v = {}, {}, {}, {}
    for n in big:
        delta[n], new_m[n], new_v[n], grad[n] = _adamw(given[n], big[n], given["m_" + n], given["v_" + n],
                                                       "adamw_" + n, copy_g=True)
    big["w_in"] = None
    grad_small = {n: gs[n].reshape(given[n].shape) for n in order
                  if n not in big and n not in ("sc_conv_w", "ssm_conv_w", "norm_mix")}
    grad_small["sc_conv_w"] = lax.dynamic_slice(gs["sc_conv_w"].reshape(3, D), (0, 256 * s), (3, 256))
    grad_small["ssm_conv_w"] = lax.dynamic_slice(gs["ssm_conv_w"].reshape(4, XBC), (0, 1024 * s), (4, 1024))
    table = [(n, int(grad_small[n].size)) for n in grad_small]
    rows = 136
    pk = lambda d: _pack(d, table, rows)
    ds_, ms_, vs_ = _adamw(pk({n: given[n] for n in grad_small}), pk(grad_small), pk({n: given["m_" + n] for n in grad_small}),
                           pk({n: given["v_" + n] for n in grad_small}), "adamw_small", tr=rows)
    ds_, ms_, vs_ = _unpack(ds_, table), _unpack(ms_, table), _unpack(vs_, table)
    for n in grad_small:
        shp = given[n].shape
        grad[n] = grad_small[n]
        delta[n], new_m[n], new_v[n] = ds_[n].reshape(shp), ms_[n].reshape(shp), vs_[n].reshape(shp)

    done = [new_v[n] for n in ("w_mlp1", "w_mlp2", "w_branch_sc", "w_branch_ssm", "w_out")] + [vs_["b_gate"]]
    tok = rs_b.share(after=done)
    gp = rs_b.result(after=tok)[0]
    offs = jnp.where(s == 3, jnp.array([24, -8, 744, 2072, -8], jnp.int32),
                     jnp.stack([8 * s, 8 * s, 0 * s, 8 * s, 8 * s]).astype(jnp.int32))
    wt_args = (w_in.T, gp, m_w_in.T, v_w_in.T, offs)
    nmain = W_SHARD // 256
    res = _adamw_w_in(*wt_args, "adamw_w_in", 0, 256, nmain, (0, 1))
    res = _adamw_w_in(*wt_args, "adamw_w_in_dt", 744, 32, 1, (3,), into=res)
    dt_, mt_, vt_, gwt = _adamw_w_in(*wt_args, "adamw_w_in_tail", 256 * nmain, 8, 1, (4,), into=res)
    grad["w_in"], delta["w_in"], new_m["w_in"], new_v["w_in"] = gwt.T, dt_.T, mt_.T, vt_.T
    nm_arr, _, _ = _split_call("norm_mix_wait", nm_arr, wait=_all8_plan("nm"), wait_sems=nm_sems, after=tok)
    g8 = _sum8(nm_arr["nm"], "norm_mix_sum")
    r8 = lambda a: a.reshape(8, 128)
    d8, m8, v8 = _adamw(r8(norm_mix), g8, r8(m_norm_mix), r8(v_norm_mix), "adamw_norm_mix", tr=8)
    grad["norm_mix"], delta["norm_mix"] = g8.reshape(D), d8.reshape(D)
    new_m["norm_mix"], new_v["norm_mix"] = m8.reshape(D), v8.reshape(D)

    loss = gs["loss"].reshape(())
    return (loss, grad_x.reshape(1, L, D), *[grad[n] for n in order], *[delta[n] for n in order],
            *[new_m[n] for n in order], *[new_v[n] for n in order])
```

```python
import functools

import jax
import jax.numpy as jnp
from jax import lax
from jax.experimental import pallas as pl
from jax.experimental.pallas import tpu as pltpu

F32 = jnp.float32
BF16 = jnp.bfloat16
MESH = pl.DeviceIdType.MESH
HBM = pltpu.HBM

D = 1024
INNER = 2048
HD = 64
NH = 32
NG = 8
NS = 128
Q = 128
GPS = 4
XBC = 4096
DFF = 4096
EPS = 1e-6
W_SHARD = 2824
NCW = 11520
PIECE = 3072
PMAIN = 2816
C_Z, C_XBC, C_GATE, C_DT = 3072, 5120, 9216, 11264
SMALL_ROWS = 256
VMEM_LIMIT = 56 * 1024 * 1024

ADAM_LR, ADAM_B1, ADAM_B2, ADAM_EPS, ADAM_WD, ADAM_STEP = 0.001, 0.9, 0.999, 1e-08, 0.01, 10


def _cp(sem=None, vmem=VMEM_LIMIT):
    return pltpu.CompilerParams(dimension_semantics=sem, vmem_limit_bytes=vmem)


def _sigmoid(v):
    return 1.0 / (1.0 + jnp.exp(-v))


_DIMS = {"nn": (((1,), (0,)), ((), ())), "nt": (((1,), (1,)), ((), ())), "tn": (((0,), (0,)), ((), ()))}


def _matmul(a, b, mode, out_dtype, tm, tn, tk, name, epi=None, extra=None, n_outer=False, dep=None):
    if mode == "tn":
        K, M = a.shape
    else:
        M, K = a.shape
    N = b.shape[0] if mode == "nt" else b.shape[1]
    tm, tn, tk = min(tm, M), min(tn, N), min(tk, K)
    assert M % tm == 0 and N % tn == 0 and K % tk == 0, (name, M, N, K, tm, tn, tk)
    nm, nn, nk = M // tm, N // tn, K // tk
    dims = _DIMS[mode]

    def ij(p0, p1):
        return (p1, p0) if n_outer else (p0, p1)

    if mode == "tn":
        a_spec = pl.BlockSpec((tk, tm), lambda p0, p1, k: (k, ij(p0, p1)[0]))
    else:
        a_spec = pl.BlockSpec((tm, tk), lambda p0, p1, k: (ij(p0, p1)[0], k))
    if mode == "nt":
        b_spec = pl.BlockSpec((tn, tk), lambda p0, p1, k: (ij(p0, p1)[1], k))
    else:
        b_spec = pl.BlockSpec((tk, tn), lambda p0, p1, k: (k, ij(p0, p1)[1]))
    o_spec = pl.BlockSpec((tm, tn), lambda p0, p1, k: ij(p0, p1))
    in_specs = [a_spec, b_spec]
    args = [a, b]
    if epi in ("res", "drelu"):
        in_specs.append(o_spec)
        args.append(extra)
    if dep is not None:
        in_specs.append(pl.BlockSpec(memory_space=pl.ANY))
        args.append(dep)
    n_in = len(args)
    if epi == "relu2":
        out_shape = (jax.ShapeDtypeStruct((M, N), out_dtype), jax.ShapeDtypeStruct((M, N), BF16))
        out_specs = (o_spec, o_spec)
    else:
        out_shape = jax.ShapeDtypeStruct((M, N), out_dtype)
        out_specs = o_spec

    def kern(*refs):
        a_ref, b_ref = refs[0], refs[1]
        e_ref = refs[2] if epi in ("res", "drelu") else None
        acc = refs[-1]
        outs = refs[n_in:-1] if nk > 1 else refs[n_in:]
        k = pl.program_id(2)

        def product():
            return lax.dot_general(a_ref[...].astype(BF16), b_ref[...].astype(BF16), dims, preferred_element_type=F32)

        def finish(r):
            if epi is None:
                outs[0][...] = r.astype(out_dtype)
            elif epi == "res":
                outs[0][...] = (r + e_ref[...]).astype(out_dtype)
            elif epi == "relu2":
                outs[0][...] = r.astype(out_dtype)
                t = jnp.maximum(r, 0.0)
                outs[1][...] = (t * t).astype(BF16)
            else:
                outs[0][...] = (r * (2.0 * jnp.maximum(e_ref[...].astype(F32), 0.0))).astype(out_dtype)

        if nk == 1:
            finish(product())
        else:
            @pl.when(k == 0)
            def _():
                acc[...] = jnp.zeros_like(acc)

            acc[...] += product()

            @pl.when(k == nk - 1)
            def _():
                finish(acc[...])

    grid = (nn, nm, nk) if n_outer else (nm, nn, nk)
    return pl.pallas_call(
        kern, grid=grid, in_specs=in_specs, out_specs=out_specs, out_shape=out_shape,
        scratch_shapes=[pltpu.VMEM((tm, tn), F32)] if nk > 1 else [], name=name,
        compiler_params=_cp(("parallel", "parallel", "arbitrary")),
    )(*args)


def _rms_fwd(x, w, name, tl=256, dep=None):
    L = x.shape[0]

    def kern(x_ref, w_ref, *rest):
        o_ref = rest[-1]
        xv = x_ref[...]
        r = lax.rsqrt(jnp.mean(xv * xv, axis=-1, keepdims=True) + EPS)
        o_ref[...] = ((xv * r) * w_ref[...]).astype(BF16)

    row = pl.BlockSpec((tl, D), lambda i: (i, 0))
    deps = [] if dep is None else [dep]
    return pl.pallas_call(
        kern, grid=(L // tl,),
        in_specs=[row, pl.BlockSpec((1, D), lambda i: (0, 0))] + [pl.BlockSpec(memory_space=pl.ANY)] * len(deps),
        out_specs=row, out_shape=jax.ShapeDtypeStruct((L, D), BF16), name=name, compiler_params=_cp(("parallel",)),
    )(x, w.reshape(1, D), *deps)


def _rms_bwd(dy, x, w, res, name, tl=256, dep=None):
    L = x.shape[0]
    deps = [] if dep is None else [dep]

    def kern(dy_ref, x_ref, w_ref, res_ref, *rest):
        dx_ref, gw_ref = rest[-2:]
        @pl.when(pl.program_id(0) == 0)
        def _():
            gw_ref[...] = jnp.zeros_like(gw_ref)

        xv = x_ref[...]
        dyv = dy_ref[...]
        r = lax.rsqrt(jnp.mean(xv * xv, axis=-1, keepdims=True) + EPS)
        xn = xv * r
        gw_ref[...] += jnp.broadcast_to(jnp.sum(dyv * xn, axis=0, keepdims=True), (8, D))
        dxn = dyv * w_ref[...]
        dx_ref[...] = res_ref[...] + r * (dxn - xn * jnp.mean(dxn * xn, axis=-1, keepdims=True))

    row = pl.BlockSpec((tl, D), lambda i: (i, 0))
    return pl.pallas_call(
        kern, grid=(L // tl,),
        in_specs=[row, row, pl.BlockSpec((1, D), lambda i: (0, 0)), row] + [pl.BlockSpec(memory_space=pl.ANY)] * len(deps),
        out_specs=(row, pl.BlockSpec((8, D), lambda i: (0, 0))),
        out_shape=(jax.ShapeDtypeStruct((L, D), F32), jax.ShapeDtypeStruct((8, D), F32)),
        name=name, compiler_params=_cp(("arbitrary",)),
    )(dy, x, w.reshape(1, D), res, *deps)


def _final(x2, w, tgt, tl=256):
    L = x2.shape[0]

    def kern(x_ref, w_ref, t_ref, dx_ref, gw_ref, loss_ref):
        @pl.when(pl.program_id(0) == 0)
        def _():
            gw_ref[...] = jnp.zeros_like(gw_ref)
            loss_ref[...] = jnp.zeros_like(loss_ref)

        xv = x_ref[...]
        r = lax.rsqrt(jnp.mean(xv * xv, axis=-1, keepdims=True) + EPS)
        xn = xv * r
        e = xn * w_ref[...] - t_ref[...]
        per_tok = jnp.mean(e * e, axis=-1, keepdims=True)
        loss_ref[...] += 0.5 * jnp.sum(per_tok)
        dyv = e * (1.0 / D)
        gw_ref[...] += jnp.broadcast_to(jnp.sum(dyv * xn, axis=0, keepdims=True), (8, D))
        dxn = dyv * w_ref[...]
        dx_ref[...] = r * (dxn - xn * jnp.mean(dxn * xn, axis=-1, keepdims=True))

    row = pl.BlockSpec((tl, D), lambda i: (i, 0))
    return pl.pallas_call(
        kern, grid=(L // tl,), in_specs=[row, pl.BlockSpec((1, D), lambda i: (0, 0)), row],
        out_specs=(row, pl.BlockSpec((8, D), lambda i: (0, 0)), pl.BlockSpec((8, 128), lambda i: (0, 0))),
        out_shape=(jax.ShapeDtypeStruct((L, D), F32), jax.ShapeDtypeStruct((8, D), F32),
                   jax.ShapeDtypeStruct((8, 128), F32)),
        name="final_norm_loss", compiler_params=_cp(("arbitrary",)),
    )(x2, w.reshape(1, D), tgt)


def _down(v, k):
    if k == 0:
        return v
    t = lax.broadcasted_iota(jnp.int32, v.shape, 0)
    return jnp.where(t >= k, pltpu.roll(v, k, axis=0), 0.0)


def _up(v, k):
    if k == 0:
        return v
    n = v.shape[0]
    t = lax.broadcasted_iota(jnp.int32, v.shape, 0)
    return jnp.where(t < n - k, pltpu.roll(v, n - k, axis=0), 0.0)


TW = 256


def _sc_fwd(proj, cw):
    L = proj.shape[0]
    nb = D // TW

    def kern(b_ref, c_ref, x_ref, w_ref, o_ref):
        u = c_ref[...].astype(F32) * x_ref[...].astype(F32)
        w = w_ref[...]
        cv = w[0:1] * _down(u, 2) + w[1:2] * _down(u, 1) + w[2:3] * u
        o_ref[...] = (b_ref[...].astype(F32) * cv).astype(BF16)

    col = lambda off: pl.BlockSpec((L, TW), lambda j: (0, off + j))
    return pl.pallas_call(
        kern, grid=(nb,), in_specs=[col(0), col(nb), col(2 * nb), pl.BlockSpec((8, TW), lambda j: (0, j))],
        out_specs=pl.BlockSpec((L, TW), lambda j: (0, j)), out_shape=jax.ShapeDtypeStruct((L, D), BF16),
        name="sc_fwd", compiler_params=_cp(("parallel",)),
    )(proj, proj, proj, cw)


def _sc_bwd(dya, proj, cw, dproj):
    L = proj.shape[0]
    nb = D // TW

    def kern(d_ref, b_ref, c_ref, x_ref, w_ref, _, dp_ref, gw_ref, keep):
        sec = pl.program_id(1)

        @pl.when(sec == 0)
        def _():
            cs, xs, dyv = c_ref[...].astype(F32), x_ref[...].astype(F32), d_ref[...]
            w = w_ref[...]
            u = cs * xs
            u1, u2 = _down(u, 1), _down(u, 2)
            cv = w[0:1] * u2 + w[1:2] * u1 + w[2:3] * u
            dcv = dyv * b_ref[...].astype(F32)
            du = w[2:3] * dcv + w[1:2] * _up(dcv, 1) + w[0:1] * _up(dcv, 2)
            g0 = jnp.sum(dcv * u2, axis=0, keepdims=True)
            g1 = jnp.sum(dcv * u1, axis=0, keepdims=True)
            g2 = jnp.sum(dcv * u, axis=0, keepdims=True)
            row = lax.broadcasted_iota(jnp.int32, (8, TW), 0)
            gw_ref[...] = jnp.where(row == 0, g0, jnp.where(row == 1, g1, jnp.where(row == 2, g2, 0.0)))
            dp_ref[...] = (dyv * cv).astype(BF16)
            keep[0] = (du * xs).astype(BF16)
            keep[1] = (du * cs).astype(BF16)

        @pl.when(sec > 0)
        def _():
            dp_ref[...] = keep[sec - 1]

    col = lambda off: pl.BlockSpec((L, TW), lambda j, s: (0, off + j))
    return pl.pallas_call(
        kern, grid=(nb, 3),
        in_specs=[col(0), col(0), col(nb), col(2 * nb), pl.BlockSpec((8, TW), lambda j, s: (0, j)),
                  pl.BlockSpec(memory_space=pl.ANY)],
        out_specs=(pl.BlockSpec((L, TW), lambda j, s: (0, s * nb + j)), pl.BlockSpec((8, TW), lambda j, s: (0, j))),
        out_shape=(jax.ShapeDtypeStruct(dproj.shape, BF16), jax.ShapeDtypeStruct((8, D), F32)),
        scratch_shapes=[pltpu.VMEM((2, L, TW), BF16)],
        input_output_aliases={5: 0}, name="sc_bwd", compiler_params=_cp(("parallel", "arbitrary")),
    )(dya, proj, proj, proj, cw, dproj)


def _ssm_conv_fwd(proj, cw4):
    L = proj.shape[0]
    off = C_XBC // TW

    def kern(r_ref, w_ref, o_ref):
        raw = r_ref[...].astype(F32)
        w = w_ref[...]
        c4 = w[0:1] * _down(raw, 3) + w[1:2] * _down(raw, 2) + w[2:3] * _down(raw, 1) + w[3:4] * raw + w[4:5]
        o_ref[...] = c4 * _sigmoid(c4)

    return pl.pallas_call(
        kern, grid=(XBC // TW,),
        in_specs=[pl.BlockSpec((L, TW), lambda j: (0, off + j)), pl.BlockSpec((8, TW), lambda j: (0, j))],
        out_specs=pl.BlockSpec((L, TW), lambda j: (0, j)), out_shape=jax.ShapeDtypeStruct((L, XBC), F32),
        name="ssm_conv_fwd", compiler_params=_cp(("parallel",)),
    )(proj, cw4)


def _ssm_conv_bwd(dx, proj, cw4, dproj, col0, name):
    L, width = dx.shape
    off_p = (C_XBC + col0) // TW
    off_w = col0 // TW

    def kern(d_ref, r_ref, w_ref, _, dp_ref, gw_ref):
        raw = r_ref[...].astype(F32)
        w = w_ref[...]
        r1, r2, r3 = _down(raw, 1), _down(raw, 2), _down(raw, 3)
        c4 = w[0:1] * r3 + w[1:2] * r2 + w[2:3] * r1 + w[3:4] * raw + w[4:5]
        sg = _sigmoid(c4)
        dc4 = d_ref[...] * (sg * (1.0 + c4 * (1.0 - sg)))
        draw = w[3:4] * dc4 + w[2:3] * _up(dc4, 1) + w[1:2] * _up(dc4, 2) + w[0:1] * _up(dc4, 3)
        dp_ref[...] = draw.astype(BF16)
        gs = [jnp.sum(dc4 * r3, axis=0, keepdims=True), jnp.sum(dc4 * r2, axis=0, keepdims=True),
              jnp.sum(dc4 * r1, axis=0, keepdims=True), jnp.sum(dc4 * raw, axis=0, keepdims=True),
              jnp.sum(dc4, axis=0, keepdims=True)]
        row = lax.broadcasted_iota(jnp.int32, (8, TW), 0)
        acc = jnp.zeros((8, TW), F32)
        for k, gk in enumerate(gs):
            acc = jnp.where(row == k, gk, acc)
        gw_ref[...] = acc

    return pl.pallas_call(
        kern, grid=(width // TW,),
        in_specs=[pl.BlockSpec((L, TW), lambda j: (0, j)), pl.BlockSpec((L, TW), lambda j: (0, off_p + j)),
                  pl.BlockSpec((8, TW), lambda j: (0, off_w + j)), pl.BlockSpec(memory_space=pl.ANY)],
        out_specs=(pl.BlockSpec((L, TW), lambda j: (0, off_p + j)), pl.BlockSpec((8, TW), lambda j: (0, j))),
        out_shape=(jax.ShapeDtypeStruct(dproj.shape, BF16), jax.ShapeDtypeStruct((8, width), F32)),
        input_output_aliases={3: 0}, name=name, compiler_params=_cp(("arbitrary",)),
    )(dx, proj, cw4, dproj)


def _split3(v):
    h1 = v.astype(BF16)
    r1 = v - h1.astype(F32)
    h2 = r1.astype(BF16)
    h3 = (r1 - h2.astype(F32)).astype(BF16)
    return h1, h2, h3


def _dot01(m01, v, dims=_DIMS["nn"], m_left=True, terms=3):
    out = None
    for part in _split3(v)[:terms]:
        ops = (m01, part) if m_left else (part, m01)
        t = lax.dot_general(ops[0], ops[1], dims, preferred_element_type=F32)
        out = t if out is None else out + t
    return out


def _bdot(a, b, mode="nn"):
    return lax.dot_general(a.astype(BF16), b.astype(BF16), _DIMS[mode], preferred_element_type=F32)


def _softplus(v):
    return jnp.maximum(v, 0.0) + jnp.log1p(jnp.exp(-jnp.abs(v)))


def _dt_prep(proj, vec):
    L = proj.shape[0]

    def kern(p_ref, v_ref, dt_ref, cs_ref, sg_ref):
        v = v_ref[...]
        pre = p_ref[:, 0:128] + v[0:1]
        dt = _softplus(pre)
        da = dt * (-jnp.exp(v[1:2]))
        ii = lax.broadcasted_iota(jnp.int32, (Q, Q), 0)
        jj = lax.broadcasted_iota(jnp.int32, (Q, Q), 1)
        ltri = (jj <= ii).astype(BF16)
        lane = lax.broadcasted_iota(jnp.int32, (Q, 128), 1)
        for val, ref in ((dt, dt_ref), (_dot01(ltri, da), cs_ref), (_sigmoid(pre), sg_ref)):
            for g in range(NG):
                moved = val if g == 0 else pltpu.roll(val, 128 - 4 * g, axis=1)
                ref[g] = jnp.where(lane < 4, moved, 0.0)

    blk = pl.BlockSpec((NG, Q, 128), lambda c: (0, c, 0))
    return pl.pallas_call(
        kern, grid=(L // Q,),
        in_specs=[pl.BlockSpec((Q, 256), lambda c: (c, 0)), pl.BlockSpec((8, 128), lambda c: (0, 0))],
        out_specs=(blk, blk, blk),
        out_shape=(jax.ShapeDtypeStruct((NG, L, 128), F32),) * 3,
        name="dt_prep", compiler_params=_cp(("parallel",)),
    )(proj, vec)


def _head_masks():
    lane = lax.broadcasted_iota(jnp.int32, (1, 4 * HD), 1)
    return [((lane >= HD * j) & (lane < HD * (j + 1))) for j in range(4)]


def _expand4(v4, masks):
    R = v4.shape[0]
    out = jnp.zeros((R, 4 * HD), F32)
    for j in range(4):
        out = jnp.where(masks[j], jnp.broadcast_to(v4[:, j:j + 1], (R, 4 * HD)), out)
    return out


def _decay_matrix(cs_col, tri):
    colb = jnp.broadcast_to(cs_col, (Q, Q))
    return jnp.exp(jnp.where(tri, colb - colb.T, -jnp.inf))


def _ssd_fwd(xbc, dt4, cs4, vecg):
    L = xbc.shape[0]
    nc = L // Q

    def kern(x_ref, b_ref, c_ref, dt_ref, cs_ref, v_ref, y_ref, s_ref, S):
        c = pl.program_id(1)

        @pl.when(c == 0)
        def _():
            S[...] = jnp.zeros_like(S)

        masks = _head_masks()
        ii = lax.broadcasted_iota(jnp.int32, (Q, Q), 0)
        jj = lax.broadcasted_iota(jnp.int32, (Q, Q), 1)
        tri = jj <= ii
        for gi in range(GPS):
            xs, ns = slice(256 * gi, 256 * (gi + 1)), slice(NS * gi, NS * (gi + 1))
            dt4v, cs4v = dt_ref[gi], cs_ref[gi]
            dt_b, cs_b = _expand4(dt4v, masks), _expand4(cs4v, masks)
            d_b = _expand4(v_ref[gi], masks)[1:2]
            cs_last = cs_b[Q - 1:Q, :]
            x4, bm, cm = x_ref[:, xs], b_ref[:, ns], c_ref[:, ns]
            xdt = x4 * dt_b
            gm = _bdot(cm, bm, "nt")
            s4 = S[gi]
            s_ref[gi, 0] = s4
            y = _bdot(cm, s4) * jnp.exp(cs_b) + d_b * x4
            m_all = jnp.concatenate([(gm * _decay_matrix(cs4v[:, j:j + 1], tri)).astype(BF16) for j in range(4)], axis=0)
            yd = _bdot(m_all, xdt)
            for j in range(4):
                y = y + jnp.where(masks[j], yd[Q * j:Q * (j + 1)], 0.0)
            y_ref[:, xs] = y
            S[gi] = jnp.exp(cs_last) * s4 + _bdot(bm, xdt * jnp.exp(cs_last - cs_b), "tn")

    sc = pl.BlockSpec((GPS, Q, 128), lambda g, c: (g, c, 0))
    bw = NS * GPS
    return pl.pallas_call(
        kern, grid=(NG // GPS, nc),
        in_specs=[pl.BlockSpec((Q, 256 * GPS), lambda g, c: (c, g)),
                  pl.BlockSpec((Q, bw), lambda g, c: (c, INNER // bw + g)),
                  pl.BlockSpec((Q, bw), lambda g, c: (c, (INNER + NG * NS) // bw + g)),
                  sc, sc, pl.BlockSpec((GPS, 8, 128), lambda g, c: (g, 0, 0))],
        out_specs=(pl.BlockSpec((Q, 256 * GPS), lambda g, c: (c, g)),
                   pl.BlockSpec((GPS, 1, NS, 256), lambda g, c: (g, c, 0, 0))),
        out_shape=(jax.ShapeDtypeStruct((L, INNER), F32), jax.ShapeDtypeStruct((NG, nc, NS, 256), F32)),
        scratch_shapes=[pltpu.VMEM((GPS, NS, 256), F32)], name="ssd_fwd",
        compiler_params=_cp(("parallel", "arbitrary")),
    )(xbc, xbc, xbc, dt4, cs4, vecg)


def _ssd_bwd(xbc, dt4, cs4, sg4, vecg, s_all, dy):
    L = xbc.shape[0]
    nc = L // Q

    def kern(x_ref, b_ref, c_ref, dt_ref, cs_ref, sg_ref, v_ref, s_ref, dy_ref,
             dx_ref, db_ref, dc_ref, ddt_ref, st_ref, dS):
        cc = pl.program_id(1)

        @pl.when(cc == 0)
        def _():
            dS[...] = jnp.zeros_like(dS)
            st_ref[...] = jnp.zeros_like(st_ref)

        masks = _head_masks()
        ii = lax.broadcasted_iota(jnp.int32, (Q, Q), 0)
        jj = lax.broadcasted_iota(jnp.int32, (Q, Q), 1)
        tri = jj <= ii
        utri = (jj >= ii).astype(BF16)
        hsel = ((lax.broadcasted_iota(jnp.int32, (4 * HD, 128), 0) // HD)
                == lax.broadcasted_iota(jnp.int32, (4 * HD, 128), 1)).astype(BF16)
        hrow = ((lax.broadcasted_iota(jnp.int32, (4 * Q, 128), 0) // Q)
                == lax.broadcasted_iota(jnp.int32, (4 * Q, 128), 1)).astype(BF16)
        ones_q = jnp.ones((Q, 128), BF16)
        lane128 = lax.broadcasted_iota(jnp.int32, (Q, 128), 1)

        for gi in range(GPS):
            xs, ns = slice(256 * gi, 256 * (gi + 1)), slice(NS * gi, NS * (gi + 1))
            dt4v, cs4v, sg4v = dt_ref[gi], cs_ref[gi], sg_ref[gi]
            dt_b, cs_b = _expand4(dt4v, masks), _expand4(cs4v, masks)
            vv = _expand4(v_ref[gi], masks)
            a_b = -jnp.exp(vv[0:1])
            d_b = vv[1:2]
            a4 = -jnp.exp(v_ref[gi][0:1, :])
            cs_last = cs_b[Q - 1:Q, :]
            ecs = jnp.exp(cs_b)
            decay = jnp.exp(cs_last - cs_b)
            elast = jnp.exp(cs_last)
            x4, bm, cm, dyv = x_ref[:, xs], b_ref[:, ns], c_ref[:, ns], dy_ref[:, xs]
            s4 = s_ref[gi, 0]
            dsn = dS[gi]
            xdt = x4 * dt_b
            gm = _bdot(cm, bm, "nt")
            dye = dyv * ecs
            yoff = ecs * _bdot(cm, s4)
            t4 = _bdot(bm, dsn) * decay
            lms, mhs = [], []
            for j in range(4):
                colb = jnp.broadcast_to(cs4v[:, j:j + 1], (Q, Q))
                lms.append(jnp.exp(jnp.where(tri, colb - colb.T, -jnp.inf)))
                mhs.append(gm * lms[j])
            m_all = jnp.concatenate([m.astype(BF16) for m in mhs], axis=0)
            dy_m = jnp.concatenate([jnp.where(masks[j], dyv, 0.0).astype(BF16) for j in range(4)], axis=0)
            dxdt = t4 + _bdot(m_all, dy_m, "tn")
            dm_all = _bdot(dy_m, xdt, "nt")
            dg = jnp.zeros((Q, Q), F32)
            for j in range(4):
                dg = dg + dm_all[Q * j:Q * (j + 1)] * lms[j]
            e_all = dm_all * jnp.concatenate(mhs, axis=0)
            rsum = _dot01(ones_q, e_all, m_left=False, terms=2)
            da4 = -_dot01(hrow, e_all, _DIMS["tn"], m_left=False, terms=2)
            for j in range(4):
                da4 = da4 + jnp.where(lane128 == j, rsum[Q * j:Q * (j + 1)], 0.0)
            xt = xdt * t4
            tail = jnp.sum(xt, axis=0, keepdims=True) + elast * jnp.sum(s4 * dsn, axis=0, keepdims=True)
            gd_raw = jnp.sum(dyv * x4, axis=0, keepdims=True)
            stacked = jnp.concatenate([dyv * yoff - xt, dxdt * x4, jnp.broadcast_to(tail, (8, 4 * HD)),
                                       jnp.broadcast_to(gd_raw, (8, 4 * HD))], axis=0)
            seg = _dot01(hsel, stacked, m_left=False, terms=2)
            dda4 = _dot01(utri, da4 + seg[0:Q], terms=2) + seg[2 * Q:2 * Q + 1]
            ddt_ref[gi] = (dda4 * a4 + seg[Q:2 * Q]) * sg4v
            ga = jnp.sum(dda4 * dt4v * a4, axis=0, keepdims=True)
            row = lax.broadcasted_iota(jnp.int32, (8, 128), 0)
            st_ref[gi] += jnp.where(row == 0, ga, jnp.where(row == 1, seg[2 * Q + 8:2 * Q + 9], 0.0))
            dx_ref[:, xs] = d_b * dyv + dxdt * dt_b
            dc_ref[:, ns] = _bdot(dg, bm) + _bdot(dye, s4, "nt")
            db_ref[:, ns] = _bdot(dg, cm, "tn") + _bdot(xdt * decay, dsn, "nt")
            dS[gi] = elast * dsn + _bdot(cm, dye, "tn")

    rv = lambda c: nc - 1 - c
    sc = pl.BlockSpec((GPS, Q, 128), lambda g, c: (g, rv(c), 0))
    bw = NS * GPS
    return pl.pallas_call(
        kern, grid=(NG // GPS, nc),
        in_specs=[pl.BlockSpec((Q, 256 * GPS), lambda g, c: (rv(c), g)),
                  pl.BlockSpec((Q, bw), lambda g, c: (rv(c), INNER // bw + g)),
                  pl.BlockSpec((Q, bw), lambda g, c: (rv(c), (INNER + NG * NS) // bw + g)),
                  sc, sc, sc, pl.BlockSpec((GPS, 8, 128), lambda g, c: (g, 0, 0)),
                  pl.BlockSpec((GPS, 1, NS, 256), lambda g, c: (g, rv(c), 0, 0)),
                  pl.BlockSpec((Q, 256 * GPS), lambda g, c: (rv(c), g))],
        out_specs=(pl.BlockSpec((Q, 256 * GPS), lambda g, c: (rv(c), g)),
                   pl.BlockSpec((Q, bw), lambda g, c: (rv(c), g)),
                   pl.BlockSpec((Q, bw), lambda g, c: (rv(c), g)),
                   pl.BlockSpec((GPS, Q, 128), lambda g, c: (g, rv(c), 0)),
                   pl.BlockSpec((GPS, 8, 128), lambda g, c: (g, 0, 0))),
        out_shape=(jax.ShapeDtypeStruct((L, INNER), F32), jax.ShapeDtypeStruct((L, NG * NS), F32),
                   jax.ShapeDtypeStruct((L, NG * NS), F32), jax.ShapeDtypeStruct((NG, L, 128), F32),
                   jax.ShapeDtypeStruct((NG, 8, 128), F32)),
        scratch_shapes=[pltpu.VMEM((GPS, NS, 256), F32)], name="ssd_bwd",
        compiler_params=_cp(("parallel", "arbitrary")),
    )(xbc, xbc, xbc, dt4, cs4, sg4, vecg, s_all, dy)


def _dt_bwd(ddt, dproj, tl=256):
    L = ddt.shape[1]

    def kern(d_ref, _, dp_ref, gs_ref):
        @pl.when(pl.program_id(0) == 0)
        def _():
            gs_ref[...] = jnp.zeros_like(gs_ref)

        d = d_ref[0]
        for g in range(1, NG):
            d = d + pltpu.roll(d_ref[g], 4 * g, axis=1)
        gs_ref[...] += jnp.broadcast_to(jnp.sum(d, axis=0, keepdims=True), (8, 128))
        dp_ref[...] = jnp.concatenate([d, jnp.zeros_like(d)], axis=1).astype(BF16)

    return pl.pallas_call(
        kern, grid=(L // tl,),
        in_specs=[pl.BlockSpec((NG, tl, 128), lambda i: (0, i, 0)), pl.BlockSpec(memory_space=pl.ANY)],
        out_specs=(pl.BlockSpec((tl, 256), lambda i: (i, C_DT // 256)), pl.BlockSpec((8, 128), lambda i: (0, 0))),
        out_shape=(jax.ShapeDtypeStruct(dproj.shape, BF16), jax.ShapeDtypeStruct((8, 128), F32)),
        input_output_aliases={1: 0}, name="dt_bwd", compiler_params=_cp(("arbitrary",)),
    )(ddt, dproj)


GW = INNER // NG


def _gnorm_fwd(y, proj, w, tl=256):
    L = y.shape[0]
    zoff = C_Z // 1024

    def kern(y_ref, z_ref, w_ref, o_ref):
        z = z_ref[...].astype(F32)
        yz = y_ref[...] * (z * _sigmoid(z))
        wv = w_ref[...]
        for k in range(1024 // GW):
            sl = slice(GW * k, GW * (k + 1))
            v = yz[:, sl]
            rg = lax.rsqrt(jnp.mean(v * v, axis=-1, keepdims=True) + EPS)
            o_ref[:, sl] = ((v * rg) * wv[:, sl]).astype(BF16)

    blk = pl.BlockSpec((tl, 1024), lambda i, j: (i, j))
    return pl.pallas_call(
        kern, grid=(L // tl, 2),
        in_specs=[blk, pl.BlockSpec((tl, 1024), lambda i, j: (i, zoff + j)), pl.BlockSpec((1, 1024), lambda i, j: (0, j))],
        out_specs=blk, out_shape=jax.ShapeDtypeStruct((L, INNER), BF16), name="gnorm_fwd",
        compiler_params=_cp(("parallel", "parallel")),
    )(y, proj, w.reshape(1, INNER))


def _gnorm_bwd(dyb, y, proj, w, dproj, tl=256):
    L = y.shape[0]
    zoff = C_Z // 1024

    def kern(d_ref, y_ref, z_ref, w_ref, _, dy_ref, dp_ref, gw_ref):
        @pl.when(pl.program_id(1) == 0)
        def _():
            gw_ref[...] = jnp.zeros_like(gw_ref)

        z = z_ref[...].astype(F32)
        sg = _sigmoid(z)
        sz = z * sg
        yv = y_ref[...]
        yz = yv * sz
        dv = d_ref[...]
        wv = w_ref[...]
        for k in range(1024 // GW):
            sl = slice(GW * k, GW * (k + 1))
            v = yz[:, sl]
            rg = lax.rsqrt(jnp.mean(v * v, axis=-1, keepdims=True) + EPS)
            vn = v * rg
            dk = dv[:, sl]
            gw_ref[:, sl] += jnp.broadcast_to(jnp.sum(dk * vn, axis=0, keepdims=True), (8, GW))
            dvn = dk * wv[:, sl]
            dyz = rg * (dvn - vn * jnp.mean(dvn * vn, axis=-1, keepdims=True))
            dy_ref[:, sl] = dyz * sz[:, sl]
            dp_ref[:, sl] = (dyz * yv[:, sl] * (sg[:, sl] * (1.0 + z[:, sl] * (1.0 - sg[:, sl])))).astype(BF16)

    blk = pl.BlockSpec((tl, 1024), lambda j, i: (i, j))
    zblk = pl.BlockSpec((tl, 1024), lambda j, i: (i, zoff + j))
    return pl.pallas_call(
        kern, grid=(2, L // tl),
        in_specs=[blk, blk, zblk, pl.BlockSpec((1, 1024), lambda j, i: (0, j)), pl.BlockSpec(memory_space=pl.ANY)],
        out_specs=(blk, zblk, pl.BlockSpec((8, 1024), lambda j, i: (0, j))),
        out_shape=(jax.ShapeDtypeStruct((L, INNER), F32), jax.ShapeDtypeStruct(dproj.shape, BF16),
                   jax.ShapeDtypeStruct((8, INNER), F32)),
        input_output_aliases={4: 1}, name="gnorm_bwd", compiler_params=_cp(("parallel", "arbitrary")),
    )(dyb, y, proj, w.reshape(1, INNER), dproj)


def _merge_fwd(proj, bg, br_a, br_b, tl=256):
    L = proj.shape[0]
    goff = C_GATE // 1024

    def kern(g1_ref, g2_ref, b1_ref, b2_ref, a_ref, b_ref, o_ref):
        g1 = _sigmoid(g1_ref[...].astype(F32) + b1_ref[...])
        g2 = _sigmoid(g2_ref[...].astype(F32) + b2_ref[...])
        o_ref[...] = (g1 * a_ref[...] + g2 * b_ref[...]).astype(BF16)

    row = pl.BlockSpec((tl, 1024), lambda i: (i, 0))
    bg2 = bg.reshape(1, 2 * D)
    return pl.pallas_call(
        kern, grid=(L // tl,),
        in_specs=[pl.BlockSpec((tl, 1024), lambda i: (i, goff)), pl.BlockSpec((tl, 1024), lambda i: (i, goff + 1)),
                  pl.BlockSpec((1, 1024), lambda i: (0, 0)), pl.BlockSpec((1, 1024), lambda i: (0, 1)), row, row],
        out_specs=row, out_shape=jax.ShapeDtypeStruct((L, D), BF16), name="merge_fwd",
        compiler_params=_cp(("parallel",)),
    )(proj, proj, bg2, bg2, br_a, br_b)


def _merge_bwd(dm, proj, bg, br_a, br_b, dproj, tl=256):
    L = proj.shape[0]
    goff = C_GATE // 1024

    def kern(dm_ref, g_ref, b_ref, a_ref, bb_ref, _, dbr_ref, dp_ref, gb_ref):
        j = pl.program_id(0)

        @pl.when(pl.program_id(1) == 0)
        def _():
            gb_ref[...] = jnp.zeros_like(gb_ref)

        g = _sigmoid(g_ref[...].astype(F32) + b_ref[...])
        br = jnp.where(j == 0, a_ref[...], bb_ref[...])
        dmv = dm_ref[...]
        dbr_ref[0] = (dmv * g).astype(BF16)
        dgate = dmv * br * g * (1.0 - g)
        gb_ref[...] += jnp.broadcast_to(jnp.sum(dgate, axis=0, keepdims=True), (8, 1024))
        dp_ref[...] = dgate.astype(BF16)

    row = pl.BlockSpec((tl, 1024), lambda j, i: (i, 0))
    gblk = pl.BlockSpec((tl, 1024), lambda j, i: (i, goff + j))
    return pl.pallas_call(
        kern, grid=(2, L // tl),
        in_specs=[row, gblk, pl.BlockSpec((1, 1024), lambda j, i: (0, j)), row, row, pl.BlockSpec(memory_space=pl.ANY)],
        out_specs=(pl.BlockSpec((1, tl, 1024), lambda j, i: (j, i, 0)), gblk, pl.BlockSpec((8, 1024), lambda j, i: (0, j))),
        out_shape=(jax.ShapeDtypeStruct((2, L, D), BF16), jax.ShapeDtypeStruct(dproj.shape, BF16),
                   jax.ShapeDtypeStruct((8, 2 * D), F32)),
        input_output_aliases={5: 1}, name="merge_bwd", compiler_params=_cp(("parallel", "arbitrary")),
    )(dm, proj, bg.reshape(1, 2 * D), br_a, br_b, dproj)


def _coords():
    return lax.axis_index("x"), lax.axis_index("y"), lax.axis_index("c")


def _other_chips(sk):
    xk, yk = sk // 2, sk % 2
    return [((1 - xk, yk), 2 * (1 - xk) + yk), ((xk, 1 - yk), 2 * xk + 1 - yk), ((1 - xk, 1 - yk), 2 * (1 - xk) + 1 - yk)]


def _rows(start, size):
    assert size % 128 == 0
    return pl.ds(pl.multiple_of(start, 128), size)


def _per_chip(fn):
    x, y, _ = _coords()
    s = 2 * x + y
    for sk in range(4):
        pl.when(s == sk)(functools.partial(fn, sk))


XTRA = PIECE - PMAIN


def _place(shard, full_shape, block, index_map, idx, name, blk0=0, nblk=None, dep=None):
    in_block = block[-2:]
    if nblk is None:
        nblk = shard.shape[0] // in_block[0]

    def kern(idx_ref, s_ref, *rest):
        o_ref = rest[-1]
        o_ref[...] = s_ref[...].astype(BF16).reshape(o_ref.shape)

    grid_spec = pltpu.PrefetchScalarGridSpec(
        num_scalar_prefetch=1, grid=(nblk,),
        in_specs=[pl.BlockSpec(in_block, lambda i, idx_ref: (blk0 + i, 0))] + ([_ANY] if dep is not None else []),
        out_specs=pl.BlockSpec(block, index_map))
    args = (idx, shard) + ((dep,) if dep is not None else ())
    return pl.pallas_call(kern, grid_spec=grid_spec, out_shape=jax.ShapeDtypeStruct(full_shape, BF16), name=name,
                          compiler_params=_cp(("arbitrary",)))(*args)


_SEM = pl.BlockSpec(memory_space=pltpu.SEMAPHORE)
_EFFECT = pltpu.SideEffectType.DATAFLOW_SIDE_EFFECTING


_ANY = pl.BlockSpec(memory_space=pl.ANY)


def _tie(v, dep, name):
    def body(v_ref, dep_ref, o_ref):
        del v_ref, dep_ref, o_ref

    return pl.pallas_call(body, out_shape=jax.ShapeDtypeStruct(v.shape, v.dtype), in_specs=[_ANY, _ANY],
                          out_specs=_ANY, input_output_aliases={0: 0}, name=name)(v, dep)


def _split_call(name, arrays, start=None, wait=None, wait_sems=None, after=None):
    keys = list(arrays)
    n = len(keys)
    n_start = start.n if start is not None else 0
    afters = [] if after is None else (list(after) if isinstance(after, (list, tuple)) else [after])

    def body(*refs):
        pos = n
        if wait is not None:
            wss, wrs = refs[pos], refs[pos + 1]
            pos += 2
        pos += len(afters)
        if start is not None:
            nss, nrs = refs[pos], refs[pos + 1]
            pos += 2
        R = dict(zip(keys, refs[pos:pos + n]))
        token = refs[pos + n]
        x, y, c = _coords()

        def desc(src, dst, dev, ss, rs, k):
            return pltpu.make_async_remote_copy(src_ref=src, dst_ref=dst, send_sem=ss.at[k], recv_sem=rs.at[k],
                                                device_id=dev, device_id_type=MESH)

        def run(sk):
            if wait is not None:
                for k, (snd, land) in enumerate(wait.copies(sk, R)):
                    if snd is not None:
                        desc(snd[0], snd[1], snd[2], wss, wrs, k).wait_send()
                    if land is not None:
                        desc(land, land, (x, y, c), wss, wrs, k).wait_recv()
            if start is not None:
                for k, (snd, land) in enumerate(start.copies(sk, R)):
                    if snd is not None:
                        desc(snd[0], snd[1], snd[2], nss, nrs, k).start()

        _per_chip(run)
        token[...] = jnp.zeros_like(token)

    hbm = pl.BlockSpec(memory_space=HBM)
    vals = [arrays[k] for k in keys]
    ins, in_specs = list(vals), [hbm] * n
    if wait is not None:
        ins += list(wait_sems)
        in_specs += [_SEM, _SEM]
    ins += afters
    in_specs += [pl.BlockSpec(memory_space=pl.ANY)] * len(afters)
    out_shape, out_specs = [], []
    if start is not None:
        out_shape += [pltpu.SemaphoreType.DMA((n_start,)), pltpu.SemaphoreType.DMA((n_start,))]
        out_specs += [_SEM, _SEM]
    first = len(out_shape)
    out_shape += [jax.ShapeDtypeStruct(v.shape, v.dtype) for v in vals] + [jax.ShapeDtypeStruct((8, 128), F32)]
    out_specs += [hbm] * n + [pl.BlockSpec(memory_space=pltpu.VMEM)]
    res = pl.pallas_call(
        body, out_shape=tuple(out_shape), in_specs=in_specs, out_specs=tuple(out_specs),
        input_output_aliases={i: first + i for i in range(n)}, name=name,
        compiler_params=pltpu.CompilerParams(has_side_effects=_EFFECT),
    )(*ins)
    sems = (res[0], res[1]) if start is not None else None
    return dict(zip(keys, res[first:first + n])), sems, res[-1]


class _Plan:
    def __init__(self, n, copies):
        self.n, self.copies = n, copies


_HM, _HX = PMAIN // 2, XTRA // 2
WAVE0 = 768
WAVES = ((0, WAVE0), (WAVE0, _HM - WAVE0))
_WIN = {
    "wq0": (True, "wct", lambda r, sc, hc: r.at[_rows(PMAIN * sc + _HM * hc + WAVES[0][0], WAVES[0][1]), :]),
    "wq1": (True, "wct", lambda r, sc, hc: r.at[_rows(PMAIN * sc + _HM * hc + WAVES[1][0], WAVES[1][1]), :]),
    "xt": (True, "xt", lambda r, sc, hc: r.at[sc, _rows(_HX * hc, _HX), :]),
    "w1": (True, "w1", lambda r, sc, hc: r.at[_rows(512 * hc, 512), pl.ds(1024 * sc, 1024)]),
    "w2": (True, "w2", lambda r, sc, hc: r.at[_rows(1024 * sc + 512 * hc, 512), :]),
    "wa": (True, "wa", lambda r, sc, hc: r.at[_rows(256 * sc + 128 * hc, 128), :]),
    "wb": (True, "wb", lambda r, sc, hc: r.at[_rows(512 * sc + 256 * hc, 256), :]),
    "wo": (True, "wo", lambda r, sc, hc: r.at[_rows(256 * sc + 128 * hc, 128), :]),
    "cw": (False, "cw", lambda r, sc, hc: r.at[sc]),
}


def _ag_chips_plan(keys):
    def copies(sk, R):
        _, _, c = _coords()
        out = []
        for key in keys:
            _, arr, win = _WIN[key]
            for (px, py), ps in _other_chips(sk):
                w = win(R[arr], sk, c)
                out.append(((w, w, (px, py, c)), win(R[arr], ps, c)))
        return out
    return _Plan(3 * len(keys), copies)


def _ag_sibling_plan(keys):
    keys = [k for k in keys if _WIN[k][0]]

    def copies(sk, R):
        x, y, c = _coords()
        out = []
        for key in keys:
            _, arr, win = _WIN[key]
            for _, ps in _other_chips(sk):
                w = win(R[arr], ps, c)
                out.append(((w, w, (x, y, 1 - c)), win(R[arr], ps, 1 - c)))
        return out
    return _Plan(3 * len(keys), copies)


def _in_proj_wave(h, wct, wave, proj=None, tm=2048):
    L = h.shape[0]
    tm = min(tm, L)
    off, size = WAVES[wave]
    start = lambda j: pl.multiple_of(_HM * j + off, 128)

    def kern(h_ref, w_ref, *rest):
        o_ref = rest[-1]
        o_ref[...] = lax.dot_general(h_ref[...], w_ref[...], _DIMS["nt"], preferred_element_type=F32).astype(BF16)

    in_specs = [pl.BlockSpec((tm, D), lambda j, i: (i, 0)),
                pl.BlockSpec((pl.Element(size), pl.Element(D)), lambda j, i: (start(j), 0))]
    args, aliases = [h, wct], {}
    if proj is not None:
        in_specs.append(pl.BlockSpec(memory_space=pl.ANY))
        args.append(proj)
        aliases = {2: 0}
    return pl.pallas_call(
        kern, grid=(8, L // tm), in_specs=in_specs,
        out_specs=pl.BlockSpec((pl.Element(tm), pl.Element(size)), lambda j, i: (i * tm, start(j))),
        out_shape=jax.ShapeDtypeStruct((L, NCW), BF16), input_output_aliases=aliases,
        name="in_proj_wave%d" % wave, compiler_params=_cp(("parallel", "parallel")),
    )(*args)


def _fix_wct(wct, xt):
    nb = PMAIN // XTRA

    def kern(w_ref, x_ref, o_ref):
        k = pl.program_id(0)
        xv = x_ref[0]
        o_ref[...] = jnp.where(k < 3, (w_ref[...].astype(F32) + xv.astype(F32)).astype(BF16), xv)

    blk = pl.BlockSpec((XTRA, D), lambda k: (nb * (k + 1), 0))
    rblk = pl.BlockSpec((XTRA, D), lambda k: (jnp.where(k < 3, nb * (k + 1), 0), 0))
    return pl.pallas_call(
        kern, grid=(4,), in_specs=[rblk, pl.BlockSpec((1, XTRA, D), lambda k: (k, 0, 0))], out_specs=blk,
        out_shape=jax.ShapeDtypeStruct(wct.shape, BF16), input_output_aliases={0: 0}, name="fix_wct",
        compiler_params=_cp(("arbitrary",)),
    )(wct, xt)


_HP = PIECE // 2
_GWIN = [
    lambda r, sc, hc: r.at[_rows(PMAIN * sc + _HP * hc, _HP), :],
    lambda r, sc, hc: r.at[_rows(512 * hc, 512), pl.ds(1024 * sc, 1024)],
    lambda r, sc, hc: r.at[_rows(1024 * sc + 512 * hc, 512), :],
    lambda r, sc, hc: r.at[_rows(256 * sc + 128 * hc, 128), :],
    lambda r, sc, hc: r.at[_rows(512 * sc + 256 * hc, 256), :],
    lambda r, sc, hc: r.at[_rows(256 * sc + 128 * hc, 128), :],
]
HALF_SHAPES = [(PIECE // 2, D), (512, 1024), (512, 1024), (128, 1024), (256, 1024), (128, 1024)]


def _rs_sibling_plan(ts):
    def copies(sk, R):
        x, y, c = _coords()
        out = []
        for t in ts:
            for sc in range(4):
                land = R["ra%d" % t].at[sc]
                out.append(((_GWIN[t](R["g%d" % t], sc, 1 - c), land, (x, y, 1 - c)), land))
        return out
    return _Plan(4 * len(ts), copies)


def _rs_chips_plan(ts):
    def copies(sk, R):
        _, _, c = _coords()
        out = []
        for t in ts:
            for j, ((px, py), ps) in enumerate(_other_chips(sk)):
                land = R["rb%d" % t].at[j]
                out.append(((R["hb%d" % t].at[ps], land, (px, py, c)), land))
        return out
    return _Plan(3 * len(ts), copies)


def _rs_share_plan(ts):
    def copies(sk, R):
        x, y, c = _coords()
        out = []
        for t in ts:
            rows = HALF_SHAPES[t][0]
            mine = R["f%d" % t].at[_rows(rows * c, rows), :]
            out.append(((mine, mine, (x, y, 1 - c)), R["f%d" % t].at[_rows(rows * (1 - c), rows), :]))
        return out
    return _Plan(len(ts), copies)


def _half_tiling(t):
    rows, cols = HALF_SHAPES[t]
    if t == 0:
        return (rows // 2, cols), 2, lambda i: (i, 0)
    return (rows, cols), 1, lambda i: (0, 0)


def _window_spec(t, blk):
    if t == 0:
        return pl.BlockSpec((pl.Element(blk[0]), pl.Element(blk[1])), lambda i, sc, idx_ref: (
            pl.multiple_of(PMAIN * sc + _HP * idx_ref[1] + blk[0] * i, 128), 0))
    if t == 1:
        return pl.BlockSpec(blk, lambda i, sc, idx_ref: (idx_ref[1], sc))
    return pl.BlockSpec(blk, lambda i, sc, idx_ref: (2 * sc + idx_ref[1], 0))


def _chip_sum(g, ra, t, idx, name):
    rows, cols = HALF_SHAPES[t]
    blk, nblk, inner = _half_tiling(t)

    def kern(idx_ref, g_ref, r_ref, hb_ref, hf_ref):
        v = g_ref[...].astype(F32) + r_ref[0].astype(F32)
        hb_ref[0] = v.astype(BF16)

        @pl.when(pl.program_id(1) == idx_ref[0])
        def _():
            hf_ref[...] = v

    omap = lambda i, sc, idx_ref: (sc,) + inner(i)
    grid_spec = pltpu.PrefetchScalarGridSpec(
        num_scalar_prefetch=1, grid=(nblk, 4),
        in_specs=[_window_spec(t, blk), pl.BlockSpec((1,) + blk, omap)],
        out_specs=(pl.BlockSpec((1,) + blk, omap), pl.BlockSpec(blk, lambda i, sc, idx_ref: inner(i))))
    return pl.pallas_call(
        kern, grid_spec=grid_spec,
        out_shape=(jax.ShapeDtypeStruct((4, rows, cols), BF16), jax.ShapeDtypeStruct((rows, cols), F32)),
        name=name, compiler_params=_cp(("parallel", "arbitrary")),
    )(idx, g, ra)


def _final_sum(hf, rb, t, idx, name):
    rows, cols = HALF_SHAPES[t]
    blk, nblk, inner = _half_tiling(t)
    nbr = rows // blk[0]

    def kern(idx_ref, h_ref, r_ref, o_ref):
        o_ref[...] = ((h_ref[...] + r_ref[0].astype(F32)) + r_ref[1].astype(F32)) + r_ref[2].astype(F32)

    def omap(i, idx_ref):
        r, cidx = inner(i)
        return nbr * idx_ref[1] + r, cidx

    grid_spec = pltpu.PrefetchScalarGridSpec(
        num_scalar_prefetch=1, grid=(nblk,),
        in_specs=[pl.BlockSpec(blk, lambda i, idx_ref: inner(i)),
                  pl.BlockSpec((3,) + blk, lambda i, idx_ref: (0,) + inner(i))],
        out_specs=pl.BlockSpec(blk, omap))
    return pl.pallas_call(
        kern, grid_spec=grid_spec, out_shape=jax.ShapeDtypeStruct((2 * rows, cols), F32),
        name=name, compiler_params=_cp(("parallel",)),
    )(idx, hf, rb)


class _ReduceScatter:
    def __init__(self, ts, grads, idx, tag):
        self.ts, self.idx, self.tag = ts, idx, tag
        arr = {}
        for t in ts:
            arr["g%d" % t] = grads[t]
            arr["ra%d" % t] = lax.empty((4,) + HALF_SHAPES[t], BF16)
        self.plan = _rs_sibling_plan(ts)
        self.arr, self.sems, self.token = _split_call("rs_sibling_start_" + tag, arr, start=self.plan)

    def chips(self, after):
        arr, _, _ = _split_call("rs_sibling_wait_" + self.tag, self.arr, wait=self.plan, wait_sems=self.sems, after=after)
        brr, self.hf = {}, {}
        for t in self.ts:
            hb, self.hf[t] = _chip_sum(arr["g%d" % t], arr["ra%d" % t], t, self.idx, "chip_sum_%d" % t)
            brr["hb%d" % t] = hb
            brr["rb%d" % t] = lax.empty((3,) + HALF_SHAPES[t], BF16)
        self.plan = _rs_chips_plan(self.ts)
        self.arr, self.sems, self.token = _split_call("rs_chips_start_" + self.tag, brr, start=self.plan)
        return self.token

    def share(self, after):
        brr, _, _ = _split_call("rs_chips_wait_" + self.tag, self.arr, wait=self.plan, wait_sems=self.sems, after=after)
        frr = {"f%d" % t: _final_sum(self.hf[t], brr["rb%d" % t], t, self.idx, "final_sum_%d" % t) for t in self.ts}
        self.plan = _rs_share_plan(self.ts)
        self.arr, self.sems, self.token = _split_call("rs_share_start_" + self.tag, frr, start=self.plan)
        return self.token

    def result(self, after):
        frr, _, _ = _split_call("rs_share_wait_" + self.tag, self.arr, wait=self.plan, wait_sems=self.sems, after=after)
        return {t: frr["f%d" % t] for t in self.ts}


def _all8_plan(key):
    def copies(sk, R):
        x, y, c = _coords()
        own = R[key].at[4 * x + 2 * y + c]
        out = []
        for k in range(1, 8):
            dev = ((1 - x) if (k >> 2) & 1 else x, (1 - y) if (k >> 1) & 1 else y, (1 - c) if k & 1 else c)
            out.append(((own, own, dev), R[key].at[4 * dev[0] + 2 * dev[1] + dev[2]]))
        return out
    return _Plan(7, copies)


def _small_all_gather(v):
    def body(v_ref, o_ref, send_sems, recv_sems, loc_sem):
        x, y, c = _coords()
        me = 4 * x + 2 * y + c
        lc = pltpu.make_async_copy(v_ref, o_ref.at[me], loc_sem)
        lc.start()
        cps = []
        for k in range(1, 8):
            fx, fy, fc = (k >> 2) & 1, (k >> 1) & 1, k & 1
            dev = ((1 - x) if fx else x, (1 - y) if fy else y, (1 - c) if fc else c)
            cp = pltpu.make_async_remote_copy(src_ref=v_ref, dst_ref=o_ref.at[me], send_sem=send_sems.at[k - 1],
                                              recv_sem=recv_sems.at[k - 1], device_id=dev, device_id_type=MESH)
            cp.start()
            cps.append((cp, 4 * dev[0] + 2 * dev[1] + dev[2]))
        for k, (cp, frm) in enumerate(cps):
            got = o_ref.at[frm]
            pltpu.make_async_remote_copy(src_ref=got, dst_ref=got, send_sem=send_sems.at[k], recv_sem=recv_sems.at[k],
                                         device_id=(x, y, c), device_id_type=MESH).wait_recv()
        for cp, _ in cps:
            cp.wait_send()
        lc.wait()

    hbm = pl.BlockSpec(memory_space=HBM)
    return pl.pallas_call(
        body, out_shape=jax.ShapeDtypeStruct((8,) + v.shape, F32), in_specs=[hbm], out_specs=hbm,
        scratch_shapes=[pltpu.SemaphoreType.DMA((7,)), pltpu.SemaphoreType.DMA((7,)), pltpu.SemaphoreType.DMA(())],
        name="small_all_gather", compiler_params=pltpu.CompilerParams(has_side_effects=True),
    )(v)


def _sum8(v, name="small_sum"):
    def kern(v_ref, o_ref):
        acc = v_ref[0]
        for k in range(1, 8):
            acc = acc + v_ref[k]
        o_ref[...] = acc

    return pl.pallas_call(kern, out_shape=jax.ShapeDtypeStruct(v.shape[1:], F32), name=name)(v)


def _adamw(w, g, m, v, name, tr=128, blk0=0, nblk=None, into=None, copy_g=False):
    R, C = w.shape
    tr = min(tr, R)
    if nblk is None:
        assert R % tr == 0 and blk0 == 0
        nblk = R // tr
    n_out = 4 if copy_g else 3

    def kern(*refs):
        w_ref, g_ref, m_ref, v_ref = refs[:4]
        d_ref, mo_ref, vo_ref = refs[-n_out:][:3]
        gv = g_ref[...]
        mn = ADAM_B1 * m_ref[...] + (1.0 - ADAM_B1) * gv
        vn = ADAM_B2 * v_ref[...] + (1.0 - ADAM_B2) * (gv * gv)
        m_hat = mn / (1.0 - ADAM_B1 ** ADAM_STEP)
        v_hat = vn / (1.0 - ADAM_B2 ** ADAM_STEP)
        d_ref[...] = -ADAM_LR * (m_hat / (jnp.sqrt(v_hat) + ADAM_EPS) + ADAM_WD * w_ref[...])
        mo_ref[...] = mn
        vo_ref[...] = vn
        if copy_g:
            refs[-1][...] = gv

    blk = pl.BlockSpec((tr, C), lambda i: (blk0 + i, 0))
    sd = jax.ShapeDtypeStruct((R, C), F32)
    in_specs, args, aliases = [blk] * 4, [w, g, m, v], {}
    if into is not None:
        in_specs += [pl.BlockSpec(memory_space=pl.ANY)] * 3
        args += list(into)
        aliases = {4: 0, 5: 1, 6: 2}
    return pl.pallas_call(kern, grid=(nblk,), in_specs=in_specs, out_specs=(blk,) * n_out, out_shape=(sd,) * n_out,
                          input_output_aliases=aliases, name=name, compiler_params=_cp(("parallel",)))(*args)


def _adamw_w_in(wt, gp, mt, vt, offs, name, r0, tr, nblk, views, into=None):
    el = lambda n: (pl.Element(n), pl.Element(D))
    own = pl.BlockSpec(el(tr), lambda i, o: (pl.multiple_of(r0 + tr * i, 8), 0))

    def view(k):
        return pl.BlockSpec(el(tr), lambda i, o: (pl.multiple_of(jnp.maximum(r0 + tr * i + o[k], 0), 8), 0))

    def kern(o_ref, w_ref, m_ref, v_ref, *refs):
        g_refs, (d_ref, mo_ref, vo_ref, go_ref) = refs[:len(views)], refs[-4:]
        gv = g_refs[0][...]
        if len(views) == 2:
            row = r0 + tr * pl.program_id(0) + lax.broadcasted_iota(jnp.int32, (tr, D), 0)
            gv = jnp.where(row < o_ref[2], gv, g_refs[1][...])
        mn = ADAM_B1 * m_ref[...] + (1.0 - ADAM_B1) * gv
        vn = ADAM_B2 * v_ref[...] + (1.0 - ADAM_B2) * (gv * gv)
        m_hat = mn / (1.0 - ADAM_B1 ** ADAM_STEP)
        v_hat = vn / (1.0 - ADAM_B2 ** ADAM_STEP)
        d_ref[...] = -ADAM_LR * (m_hat / (jnp.sqrt(v_hat) + ADAM_EPS) + ADAM_WD * w_ref[...])
        mo_ref[...] = mn
        vo_ref[...] = vn
        go_ref[...] = gv

    in_specs = [own, own, own] + [view(k) for k in views]
    args = [wt, mt, vt] + [gp] * len(views)
    aliases = {}
    if into is not None:
        in_specs += [pl.BlockSpec(memory_space=pl.ANY)] * 4
        args += list(into)
        aliases = {1 + len(args) - 4 + j: j for j in range(4)}
    grid_spec = pltpu.PrefetchScalarGridSpec(num_scalar_prefetch=1, grid=(nblk,), in_specs=in_specs,
                                             out_specs=(own,) * 4)
    sd = jax.ShapeDtypeStruct(wt.shape, F32)
    return pl.pallas_call(kern, grid_spec=grid_spec, out_shape=(sd,) * 4, input_output_aliases=aliases, name=name,
                          compiler_params=_cp(("parallel",)))(offs, *args)


def _to_piece(wt, s):
    z = lambda n: jnp.zeros((n, D), wt.dtype)
    pads = [functools.partial(lambda k, w: jnp.pad(w, ((8 * k, PIECE - W_SHARD - 8 * k), (0, 0))), k) for k in range(3)]
    last = lambda w: jnp.concatenate([z(24), w[:744], w[776:], w[744:776], z(PIECE - 24 - W_SHARD)], axis=0)
    return lax.switch(s, pads + [last], wt)


def _from_piece(p, s):
    cuts = [functools.partial(lambda k, q: q[8 * k:8 * k + W_SHARD], k) for k in range(3)]
    last = lambda q: jnp.concatenate([q[24:768], q[2816:2848], q[768:2816]], axis=0)
    return lax.switch(s, cuts + [last], p)


_SMALL = [("b_gate", 2048), ("ssm_conv_b", 4096), ("dt_bias", 32), ("A_log", 32), ("D_skip", 32),
          ("ssm_norm_w", 2048), ("norm_mlp", 1024), ("norm_final", 1024), ("sc_conv_w", 3072), ("ssm_conv_w", 16384),
          ("loss", 1)]


def _pack(vals, table, rows):
    parts = []
    for name, n in table:
        v = vals[name].reshape(-1).astype(F32)
        pad = (-n) % 128
        parts.append(jnp.pad(v, (0, pad)) if pad else v)
    flat = jnp.concatenate(parts)
    return jnp.pad(flat, (0, rows * 128 - flat.shape[0])).reshape(rows, 128)


def _unpack(arr, table):
    flat = arr.reshape(-1)
    out, off = {}, 0
    for name, n in table:
        out[name] = flat[off:off + n]
        off += n + ((-n) % 128)
    return out


def kernel(x, norm_mix, w_in, b_gate, sc_conv_w, ssm_conv_w, ssm_conv_b, dt_bias, A_log, D_skip, ssm_norm_w, w_branch_sc, w_branch_ssm, w_out, norm_mlp, w_mlp1, w_mlp2, norm_final, loss_target, m_norm_mix, m_w_in, m_b_gate, m_sc_conv_w, m_ssm_conv_w, m_ssm_conv_b, m_dt_bias, m_A_log, m_D_skip, m_ssm_norm_w, m_w_branch_sc, m_w_branch_ssm, m_w_out, m_norm_mlp, m_w_mlp1, m_w_mlp2, m_norm_final, v_norm_mix, v_w_in, v_b_gate, v_sc_conv_w, v_ssm_conv_w, v_ssm_conv_b, v_dt_bias, v_A_log, v_D_skip, v_ssm_norm_w, v_w_branch_sc, v_w_branch_ssm, v_w_out, v_norm_mlp, v_w_mlp1, v_w_mlp2, v_norm_final):
    L = x.shape[1]
    nc = L // Q
    xi, yi, ci = lax.axis_index("x"), lax.axis_index("y"), lax.axis_index("c")
    s = 2 * xi + yi
    idx = jnp.stack([s, ci]).astype(jnp.int32)
    x0 = x.reshape(L, D)
    tgt = loss_target.reshape(L, D)

    piece = _to_piece(w_in.T, s)
    nb = PMAIN // XTRA
    wct0 = _place(piece, (NCW, D), (XTRA, D), lambda i, r: (nb * r[0] + i, 0), idx, "place_wct", nblk=nb)
    xt0 = _place(piece, (4, XTRA, D), (1, XTRA, D), lambda i, r: (r[0], 0, 0), idx, "place_xt", blk0=nb, nblk=1)
    cws = jnp.zeros((8, 1280), F32)
    cws = cws.at[0:3, 0:256].set(sc_conv_w).at[0:4, 256:1280].set(ssm_conv_w)
    cw0 = lax.dynamic_update_slice(jnp.zeros((4, 8, 1280), F32), cws[None], (s, 0, 0))
    win_keys, win2_keys, mid_keys, end_keys = ["xt", "cw", "wq0"], ["wq1"], ["wa", "wb", "wo", "w1"], ["w2"]
    gw, sems_w, tok = _split_call("ag_win_start", {"wct": wct0, "xt": xt0, "cw": cw0}, start=_ag_chips_plan(win_keys))
    g2, sems_w2, tok = _split_call("ag_win2_start", {"wct": gw["wct"]}, start=_ag_chips_plan(win2_keys), after=tok)
    gw["wct"] = g2["wct"]
    wa0 = _place(w_branch_sc, (D, D), (256, 1024), lambda i, r: (r[0], 0), idx, "place_wa", dep=tok)
    wb0 = _place(w_branch_ssm, (INNER, D), (512, 1024), lambda i, r: (r[0], 0), idx, "place_wb", dep=tok)
    wo0 = _place(w_out, (D, D), (256, 1024), lambda i, r: (r[0], 0), idx, "place_wo", dep=tok)
    w10 = _place(w_mlp1, (D, DFF), (256, 1024), lambda i, r: (i, r[0]), idx, "place_w1", dep=tok)
    gm, sems_m, tok = _split_call("ag_mid_start", {"wa": wa0, "wb": wb0, "wo": wo0, "w1": w10},
                                  start=_ag_chips_plan(mid_keys))
    w20 = _place(w_mlp2, (DFF, D), (256, 1024), lambda i, r: (4 * r[0] + i, 0), idx, "place_w2", dep=tok)
    ge, sems_e, tok = _split_call("ag_end_start", {"w2": w20}, start=_ag_chips_plan(end_keys))
    h = _rms_fwd(x0, norm_mix, "rms_mix", dep=tok)
    gw, sems_w, tok = _split_call("ag_win_pass", gw, wait=_ag_chips_plan(win_keys), wait_sems=sems_w,
                                  start=_ag_sibling_plan(win_keys), after=h)
    gw, _, _ = _split_call("ag_win_done", gw, wait=_ag_sibling_plan(win_keys), wait_sems=sems_w, after=tok)
    wc, cw_all = _fix_wct(gw["wct"], gw["xt"]), gw["cw"]
    sc_w_full = jnp.concatenate([cw_all[k, :, 0:256] for k in range(4)], axis=1)
    ssm_w_full = jnp.concatenate([cw_all[k, :, 256:1280] for k in range(4)], axis=1)
    cw4 = ssm_w_full.at[4].set(ssm_conv_b)
    vec = jnp.zeros((8, 128), F32).at[0, :NH].set(dt_bias).at[1, :NH].set(A_log)
    vecg = jnp.zeros((NG, 8, 128), F32).at[:, 0, :4].set(A_log.reshape(NG, 4)).at[:, 1, :4].set(D_skip.reshape(NG, 4))

    dtraw = _matmul(h, wc[C_DT:], "nt", F32, 512, 256, 1024, "in_proj_dt")
    proj = _in_proj_wave(h, wc, 0)
    g2, sems_w2, tok = _split_call("ag_win2_pass", {"wct": wc}, wait=_ag_chips_plan(win2_keys), wait_sems=sems_w2,
                                   start=_ag_sibling_plan(win2_keys), after=[proj, dtraw])
    g2, _, _ = _split_call("ag_win2_done", g2, wait=_ag_sibling_plan(win2_keys), wait_sems=sems_w2, after=tok)
    wc = g2["wct"]
    proj = _in_proj_wave(h, wc, 1, proj=proj)
    ya = _sc_fwd(proj, sc_w_full)
    xbc = _ssm_conv_fwd(proj, cw4)
    dt4, cs4, sg4 = _dt_prep(dtraw, vec)
    gm, sems_m, tok = _split_call("ag_mid_pass", gm, wait=_ag_chips_plan(mid_keys), wait_sems=sems_m,
                                  start=_ag_sibling_plan(mid_keys), after=[xbc, ya, dt4])
    xbc = _tie(xbc, tok, "tie_xbc")
    y, s_all = _ssd_fwd(xbc, dt4, cs4, vecg)
    yb = _gnorm_fwd(y, proj, ssm_norm_w)
    gm, _, _ = _split_call("ag_mid_done", gm, wait=_ag_sibling_plan(mid_keys), wait_sems=sems_m, after=yb)
    wa, wb, wo, w1 = gm["wa"], gm["wb"], gm["wo"], gm["w1"]
    ge, sems_e, tok = _split_call("ag_end_pass", ge, wait=_ag_chips_plan(end_keys), wait_sems=sems_e,
                                  start=_ag_sibling_plan(end_keys), after=yb)
    br_a = _matmul(ya, wa, "nn", F32, 1024, 1024, 1024, "branch_sc", dep=tok)
    br_b = _matmul(yb, wb, "nn", F32, 1024, 1024, 2048, "branch_ssm")
    merged = _merge_fwd(proj, b_gate, br_a, br_b)
    x1 = _matmul(merged, wo, "nn", F32, 1024, 1024, 1024, "out_proj", epi="res", extra=x0)
    h2 = _rms_fwd(x1, norm_mlp, "rms_mlp")
    a1, rl = _matmul(h2, w1, "nn", BF16, 1024, 1024, 1024, "mlp1", epi="relu2", n_outer=True)
    ge, _, _ = _split_call("ag_end_done", ge, wait=_ag_sibling_plan(end_keys), wait_sems=sems_e, after=a1)
    w2 = ge["w2"]
    x2 = _matmul(rl, w2, "nn", F32, 512, 1024, 4096, "mlp2", epi="res", extra=x1)
    dx2, g_nf, loss8 = _final(x2, norm_final, tgt)

    da = _matmul(dx2, w2, "nt", BF16, 1024, 1024, 1024, "mlp2_dx", epi="drelu", extra=a1, n_outer=True)
    g_w2 = _matmul(rl, dx2, "tn", BF16, 1024, 1024, 2048, "mlp2_dw")
    g_w1 = _matmul(h2, da, "tn", BF16, 1024, 1024, 2048, "mlp1_dw")
    dh2 = _matmul(da, w1, "nt", F32, 512, 1024, 4096, "mlp1_dx")
    dx1, g_nmlp = _rms_bwd(dh2, x1, norm_mlp, dx2, "rms_mlp_bwd")
    dmerged = _matmul(dx1, wo, "nt", F32, 1024, 1024, 1024, "out_proj_dx")
    g_wo = _matmul(merged, dx1, "tn", BF16, 1024, 1024, 2048, "out_proj_dw")
    dproj = lax.empty((L, NCW), BF16)
    dbr, dproj, g_bg = _merge_bwd(dmerged, proj, b_gate, br_a, br_b, dproj)
    dya = _matmul(dbr[0], wa, "nt", F32, 1024, 1024, 1024, "branch_sc_dx")
    g_wa = _matmul(ya, dbr[0], "tn", BF16, 1024, 1024, 2048, "branch_sc_dw")
    dproj, g_scw = _sc_bwd(dya, proj, sc_w_full, dproj)
    dyb = _matmul(dbr[1], wb, "nt", F32, 1024, 1024, 1024, "branch_ssm_dx", n_outer=True)
    g_wb = _matmul(yb, dbr[1], "tn", BF16, 1024, 1024, 2048, "branch_ssm_dw")
    rs_a = _ReduceScatter([1, 2, 3, 4, 5], {1: g_w1, 2: g_w2, 3: g_wa, 4: g_wb, 5: g_wo}, idx, "a")
    dy, dproj, g_snw = _gnorm_bwd(_tie(dyb, rs_a.token, "tie_dyb"), y, proj, ssm_norm_w, dproj)
    tok = rs_a.chips(after=dy)
    dxs, dbm, dcm, ddt_g, st = _ssd_bwd(xbc, dt4, cs4, sg4, vecg, s_all, _tie(dy, tok, "tie_dy"))
    dproj, gx1 = _ssm_conv_bwd(dxs, proj, cw4, dproj, 0, "ssm_conv_bwd_x")
    dproj, gx2 = _ssm_conv_bwd(dbm, proj, cw4, dproj, INNER, "ssm_conv_bwd_b")
    dproj, gx3 = _ssm_conv_bwd(dcm, proj, cw4, dproj, INNER + NG * NS, "ssm_conv_bwd_c")
    g_cw4 = jnp.concatenate([gx1, gx2, gx3], axis=1)
    dproj, g_dtb = _dt_bwd(ddt_g, dproj)
    small = {"b_gate": g_bg[0], "ssm_conv_b": g_cw4[4], "dt_bias": g_dtb[0, :NH],
             "A_log": st[:, 0, :4], "D_skip": st[:, 1, :4], "ssm_norm_w": g_snw[0], "norm_mlp": g_nmlp[0],
             "norm_final": g_nf[0], "sc_conv_w": g_scw[0:3], "ssm_conv_w": g_cw4[0:4], "loss": loss8[0, 0:1]}
    small_sum = _sum8(_small_all_gather(_pack(small, _SMALL, SMALL_ROWS)))
    gs = _unpack(small_sum, _SMALL)
    g_wc = _matmul(dproj, h, "tn", BF16, 1280, 1024, 2048, "in_proj_dw", dep=small_sum)
    rs_b = _ReduceScatter([0], {0: g_wc}, idx, "b")
    tok = rs_a.share(after=rs_b.token)
    tok = rs_b.chips(after=tok)
    dh = _matmul(dproj, wc, "nn", F32, 512, 1024, 5760, "in_proj_dx", dep=tok)
    grad_x, g_nm = _rms_bwd(dh, x0, norm_mix, dx1, "rms_mix_bwd")
    me = 4 * xi + 2 * yi + ci
    nm8 = lax.dynamic_update_slice(jnp.zeros((8, 8, 128), F32), g_nm[0].reshape(1, 8, 128), (me, 0, 0))
    nm_arr, nm_sems, tok = _split_call("norm_mix_start", {"nm": nm8}, start=_all8_plan("nm"))
    red = rs_a.result(after=tok)
    big = {"w_mlp1": red[1], "w_mlp2": red[2], "w_branch_sc": red[3], "w_branch_ssm": red[4], "w_out": red[5]}

    given = dict(norm_mix=norm_mix, w_in=w_in, b_gate=b_gate, sc_conv_w=sc_conv_w, ssm_conv_w=ssm_conv_w, ssm_conv_b=ssm_conv_b, dt_bias=dt_bias, A_log=A_log, D_skip=D_skip, ssm_norm_w=ssm_norm_w, w_branch_sc=w_branch_sc, w_branch_ssm=w_branch_ssm, w_out=w_out, norm_mlp=norm_mlp, w_mlp1=w_mlp1, w_mlp2=w_mlp2, norm_final=norm_final,
                 m_norm_mix=m_norm_mix, m_w_in=m_w_in, m_b_gate=m_b_gate, m_sc_conv_w=m_sc_conv_w, m_ssm_conv_w=m_ssm_conv_w, m_ssm_conv_b=m_ssm_conv_b, m_dt_bias=m_dt_bias, m_A_log=m_A_log, m_D_skip=m_D_skip, m_ssm_norm_w=m_ssm_norm_w, m_w_branch_sc=m_w_branch_sc, m_w_branch_ssm=m_w_branch_ssm, m_w_out=m_w_out, m_norm_mlp=m_norm_mlp, m_w_mlp1=m_w_mlp1, m_w_mlp2=m_w_mlp2, m_norm_final=m_norm_final,
                 v_norm_mix=v_norm_mix, v_w_in=v_w_in, v_b_gate=v_b_gate, v_sc_conv_w=v_sc_conv_w, v_ssm_conv_w=v_ssm_conv_w, v_ssm_conv_b=v_ssm_conv_b, v_dt_bias=v_dt_bias, v_A_log=v_A_log, v_D_skip=v_D_skip, v_ssm_norm_w=v_ssm_norm_w, v_w_branch_sc=v_w_branch_sc, v_w_branch_ssm=v_w_branch_ssm, v_w_out=v_w_out, v_norm_mlp=v_norm_mlp, v_w_mlp1=v_w_mlp1, v_w_mlp2=v_w_mlp2, v_norm_final=v_norm_final)
    order = ["norm_mix", "w_in", "b_gate", "sc_conv_w", "ssm_conv_w", "ssm_conv_b", "dt_bias", "A_log", "D_skip",
             "ssm_norm_w", "w_branch_sc", "w_branch_ssm", "w_out", "norm_mlp", "w_mlp1", "w_mlp2", "norm_final"]
    grad, delta, new_m, new_v = {}, {}, {}, {}
    for n in big:
        delta[n], new_m[n], new_v[n], grad[n] = _adamw(given[n], big[n], given["m_" + n], given["v_" + n],
                                                       "adamw_" + n, copy_g=True)
    big["w_in"] = None
    grad_small = {n: gs[n].reshape(given[n].shape) for n in order
                  if n not in big and n not in ("sc_conv_w", "ssm_conv_w", "norm_mix")}
    grad_small["sc_conv_w"] = lax.dynamic_slice(gs["sc_conv_w"].reshape(3, D), (0, 256 * s), (3, 256))
    grad_small["ssm_conv_w"] = lax.dynamic_slice(gs["ssm_conv_w"].reshape(4, XBC), (0, 1024 * s), (4, 1024))
    table = [(n, int(grad_small[n].size)) for n in grad_small]
    rows = 136
    pk = lambda d: _pack(d, table, rows)
    ds_, ms_, vs_ = _adamw(pk({n: given[n] for n in grad_small}), pk(grad_small), pk({n: given["m_" + n] for n in grad_small}),
                           pk({n: given["v_" + n] for n in grad_small}), "adamw_small", tr=rows)
    ds_, ms_, vs_ = _unpack(ds_, table), _unpack(ms_, table), _unpack(vs_, table)
    for n in grad_small:
        shp = given[n].shape
        grad[n] = grad_small[n]
        delta[n], new_m[n], new_v[n] = ds_[n].reshape(shp), ms_[n].reshape(shp), vs_[n].reshape(shp)

    done = [new_v[n] for n in ("w_mlp1", "w_mlp2", "w_branch_sc", "w_branch_ssm", "w_out")] + [vs_["b_gate"]]
    tok = rs_b.share(after=done)
    gp = rs_b.result(after=tok)[0]
    offs = jnp.where(s == 3, jnp.array([24, -8, 744, 2072, -8], jnp.int32),
                     jnp.stack([8 * s, 8 * s, 0 * s, 8 * s, 8 * s]).astype(jnp.int32))
    wt_args = (w_in.T, gp, m_w_in.T, v_w_in.T, offs)
    nmain = W_SHARD // 256
    res = _adamw_w_in(*wt_args, "adamw_w_in", 0, 256, nmain, (0, 1))
    res = _adamw_w_in(*wt_args, "adamw_w_in_dt", 744, 32, 1, (3,), into=res)
    dt_, mt_, vt_, gwt = _adamw_w_in(*wt_args, "adamw_w_in_tail", 256 * nmain, 8, 1, (4,), into=res)
    grad["w_in"], delta["w_in"], new_m["w_in"], new_v["w_in"] = gwt.T, dt_.T, mt_.T, vt_.T
    nm_arr, _, _ = _split_call("norm_mix_wait", nm_arr, wait=_all8_plan("nm"), wait_sems=nm_sems, after=tok)
    g8 = _sum8(nm_arr["nm"], "norm_mix_sum")
    r8 = lambda a: a.reshape(8, 128)
    d8, m8, v8 = _adamw(r8(norm_mix), g8, r8(m_norm_mix), r8(v_norm_mix), "adamw_norm_mix", tr=8)
    grad["norm_mix"], delta["norm_mix"] = g8.reshape(D), d8.reshape(D)
    new_m["norm_mix"], new_v["norm_mix"] = m8.reshape(D), v8.reshape(D)

    loss = gs["loss"].reshape(())
    return (loss, grad_x.reshape(1, L, D), *[grad[n] for n in order], *[delta[n] for n in order],
            *[new_m[n] for n in order], *[new_v[n] for n in order])
```

```python
import functools

import jax
import jax.numpy as jnp
from jax import lax
from jax.experimental import pallas as pl
from jax.experimental.pallas import tpu as pltpu

F32 = jnp.float32
BF16 = jnp.bfloat16
MESH = pl.DeviceIdType.MESH
HBM = pltpu.HBM

D = 1024
INNER = 2048
HD = 64
NH = 32
NG = 8
NS = 128
Q = 128
GPS = 4
XBC = 4096
DFF = 4096
EPS = 1e-6
W_SHARD = 2824
NCW = 11520
PIECE = 3072
PMAIN = 2816
C_Z, C_XBC, C_GATE, C_DT = 3072, 5120, 9216, 11264
SMALL_ROWS = 256
VMEM_LIMIT = 56 * 1024 * 1024

ADAM_LR, ADAM_B1, ADAM_B2, ADAM_EPS, ADAM_WD, ADAM_STEP = 0.001, 0.9, 0.999, 1e-08, 0.01, 10


def _cp(sem=None, vmem=VMEM_LIMIT):
    return pltpu.CompilerParams(dimension_semantics=sem, vmem_limit_bytes=vmem)


def _sigmoid(v):
    return 1.0 / (1.0 + jnp.exp(-v))


_DIMS = {"nn": (((1,), (0,)), ((), ())), "nt": (((1,), (1,)), ((), ())), "tn": (((0,), (0,)), ((), ()))}


def _matmul(a, b, mode, out_dtype, tm, tn, tk, name, epi=None, extra=None, n_outer=False, dep=None):
    if mode == "tn":
        K, M = a.shape
    else:
        M, K = a.shape
    N = b.shape[0] if mode == "nt" else b.shape[1]
    tm, tn, tk = min(tm, M), min(tn, N), min(tk, K)
    assert M % tm == 0 and N % tn == 0 and K % tk == 0, (name, M, N, K, tm, tn, tk)
    nm, nn, nk = M // tm, N // tn, K // tk
    dims = _DIMS[mode]

    def ij(p0, p1):
        return (p1, p0) if n_outer else (p0, p1)

    if mode == "tn":
        a_spec = pl.BlockSpec((tk, tm), lambda p0, p1, k: (k, ij(p0, p1)[0]))
    else:
        a_spec = pl.BlockSpec((tm, tk), lambda p0, p1, k: (ij(p0, p1)[0], k))
    if mode == "nt":
        b_spec = pl.BlockSpec((tn, tk), lambda p0, p1, k: (ij(p0, p1)[1], k))
    else:
        b_spec = pl.BlockSpec((tk, tn), lambda p0, p1, k: (k, ij(p0, p1)[1]))
    o_spec = pl.BlockSpec((tm, tn), lambda p0, p1, k: ij(p0, p1))
    in_specs = [a_spec, b_spec]
    args = [a, b]
    if epi in ("res", "drelu"):
        in_specs.append(o_spec)
        args.append(extra)
    if dep is not None:
        in_specs.append(pl.BlockSpec(memory_space=pl.ANY))
        args.append(dep)
    n_in = len(args)
    if epi == "relu2":
        out_shape = (jax.ShapeDtypeStruct((M, N), out_dtype), jax.ShapeDtypeStruct((M, N), BF16))
        out_specs = (o_spec, o_spec)
    else:
        out_shape = jax.ShapeDtypeStruct((M, N), out_dtype)
        out_specs = o_spec

    def kern(*refs):
        a_ref, b_ref = refs[0], refs[1]
        e_ref = refs[2] if epi in ("res", "drelu") else None
        acc = refs[-1]
        outs = refs[n_in:-1] if nk > 1 else refs[n_in:]
        k = pl.program_id(2)

        def product():
            return lax.dot_general(a_ref[...].astype(BF16), b_ref[...].astype(BF16), dims, preferred_element_type=F32)

        def finish(r):
            if epi is None:
                outs[0][...] = r.astype(out_dtype)
            elif epi == "res":
                outs[0][...] = (r + e_ref[...]).astype(out_dtype)
            elif epi == "relu2":
                outs[0][...] = r.astype(out_dtype)
                t = jnp.maximum(r, 0.0)
                outs[1][...] = (t * t).astype(BF16)
            else:
                outs[0][...] = (r * (2.0 * jnp.maximum(e_ref[...].astype(F32), 0.0))).astype(out_dtype)

        if nk == 1:
            finish(product())
        else:
            @pl.when(k == 0)
            def _():
                acc[...] = jnp.zeros_like(acc)

            acc[...] += product()

            @pl.when(k == nk - 1)
            def _():
                finish(acc[...])

    grid = (nn, nm, nk) if n_outer else (nm, nn, nk)
    return pl.pallas_call(
        kern, grid=grid, in_specs=in_specs, out_specs=out_specs, out_shape=out_shape,
        scratch_shapes=[pltpu.VMEM((tm, tn), F32)] if nk > 1 else [], name=name,
        compiler_params=_cp(("parallel", "parallel", "arbitrary")),
    )(*args)


def _rms_fwd(x, w, name, tl=256, dep=None):
    L = x.shape[0]

    def kern(x_ref, w_ref, *rest):
        o_ref = rest[-1]
        xv = x_ref[...]
        r = lax.rsqrt(jnp.mean(xv * xv, axis=-1, keepdims=True) + EPS)
        o_ref[...] = ((xv * r) * w_ref[...]).astype(BF16)

    row = pl.BlockSpec((tl, D), lambda i: (i, 0))
    deps = [] if dep is None else [dep]
    return pl.pallas_call(
        kern, grid=(L // tl,),
        in_specs=[row, pl.BlockSpec((1, D), lambda i: (0, 0))] + [pl.BlockSpec(memory_space=pl.ANY)] * len(deps),
        out_specs=row, out_shape=jax.ShapeDtypeStruct((L, D), BF16), name=name, compiler_params=_cp(("parallel",)),
    )(x, w.reshape(1, D), *deps)


def _rms_bwd(dy, x, w, res, name, tl=256, dep=None):
    L = x.shape[0]
    deps = [] if dep is None else [dep]

    def kern(dy_ref, x_ref, w_ref, res_ref, *rest):
        dx_ref, gw_ref = rest[-2:]
        @pl.when(pl.program_id(0) == 0)
        def _():
            gw_ref[...] = jnp.zeros_like(gw_ref)

        xv = x_ref[...]
        dyv = dy_ref[...]
        r = lax.rsqrt(jnp.mean(xv * xv, axis=-1, keepdims=True) + EPS)
        xn = xv * r
        gw_ref[...] += jnp.broadcast_to(jnp.sum(dyv * xn, axis=0, keepdims=True), (8, D))
        dxn = dyv * w_ref[...]
        dx_ref[...] = res_ref[...] + r * (dxn - xn * jnp.mean(dxn * xn, axis=-1, keepdims=True))

    row = pl.BlockSpec((tl, D), lambda i: (i, 0))
    return pl.pallas_call(
        kern, grid=(L // tl,),
        in_specs=[row, row, pl.BlockSpec((1, D), lambda i: (0, 0)), row] + [pl.BlockSpec(memory_space=pl.ANY)] * len(deps),
        out_specs=(row, pl.BlockSpec((8, D), lambda i: (0, 0))),
        out_shape=(jax.ShapeDtypeStruct((L, D), F32), jax.ShapeDtypeStruct((8, D), F32)),
        name=name, compiler_params=_cp(("arbitrary",)),
    )(dy, x, w.reshape(1, D), res, *deps)


def _final(x2, w, tgt, tl=256):
    L = x2.shape[0]

    def kern(x_ref, w_ref, t_ref, dx_ref, gw_ref, loss_ref):
        @pl.when(pl.program_id(0) == 0)
        def _():
            gw_ref[...] = jnp.zeros_like(gw_ref)
            loss_ref[...] = jnp.zeros_like(loss_ref)

        xv = x_ref[...]
        r = lax.rsqrt(jnp.mean(xv * xv, axis=-1, keepdims=True) + EPS)
        xn = xv * r
        e = xn * w_ref[...] - t_ref[...]
        per_tok = jnp.mean(e * e, axis=-1, keepdims=True)
        loss_ref[...] += 0.5 * jnp.sum(per_tok)
        dyv = e * (1.0 / D)
        gw_ref[...] += jnp.broadcast_to(jnp.sum(dyv * xn, axis=0, keepdims=True), (8, D))
        dxn = dyv * w_ref[...]
        dx_ref[...] = r * (dxn - xn * jnp.mean(dxn * xn, axis=-1, keepdims=True))

    row = pl.BlockSpec((tl, D), lambda i: (i, 0))
    return pl.pallas_call(
        kern, grid=(L // tl,), in_specs=[row, pl.BlockSpec((1, D), lambda i: (0, 0)), row],
        out_specs=(row, pl.BlockSpec((8, D), lambda i: (0, 0)), pl.BlockSpec((8, 128), lambda i: (0, 0))),
        out_shape=(jax.ShapeDtypeStruct((L, D), F32), jax.ShapeDtypeStruct((8, D), F32),
                   jax.ShapeDtypeStruct((8, 128), F32)),
        name="final_norm_loss", compiler_params=_cp(("arbitrary",)),
    )(x2, w.reshape(1, D), tgt)


def _down(v, k):
    if k == 0:
        return v
    t = lax.broadcasted_iota(jnp.int32, v.shape, 0)
    return jnp.where(t >= k, pltpu.roll(v, k, axis=0), 0.0)


def _up(v, k):
    if k == 0:
        return v
    n = v.shape[0]
    t = lax.broadcasted_iota(jnp.int32, v.shape, 0)
    return jnp.where(t < n - k, pltpu.roll(v, n - k, axis=0), 0.0)


TW = 256


def _sc_fwd(proj, cw):
    L = proj.shape[0]
    nb = D // TW

    def kern(b_ref, c_ref, x_ref, w_ref, o_ref):
        u = c_ref[...].astype(F32) * x_ref[...].astype(F32)
        w = w_ref[...]
        cv = w[0:1] * _down(u, 2) + w[1:2] * _down(u, 1) + w[2:3] * u
        o_ref[...] = (b_ref[...].astype(F32) * cv).astype(BF16)

    col = lambda off: pl.BlockSpec((L, TW), lambda j: (0, off + j))
    return pl.pallas_call(
        kern, grid=(nb,), in_specs=[col(0), col(nb), col(2 * nb), pl.BlockSpec((8, TW), lambda j: (0, j))],
        out_specs=pl.BlockSpec((L, TW), lambda j: (0, j)), out_shape=jax.ShapeDtypeStruct((L, D), BF16),
        name="sc_fwd", compiler_params=_cp(("parallel",)),
    )(proj, proj, proj, cw)


def _sc_bwd(dya, proj, cw, dproj):
    L = proj.shape[0]
    nb = D // TW

    def kern(d_ref, b_ref, c_ref, x_ref, w_ref, _, dp_ref, gw_ref, keep):
        sec = pl.program_id(1)

        @pl.when(sec == 0)
        def _():
            cs, xs, dyv = c_ref[...].astype(F32), x_ref[...].astype(F32), d_ref[...]
            w = w_ref[...]
            u = cs * xs
            u1, u2 = _down(u, 1), _down(u, 2)
            cv = w[0:1] * u2 + w[1:2] * u1 + w[2:3] * u
            dcv = dyv * b_ref[...].astype(F32)
            du = w[2:3] * dcv + w[1:2] * _up(dcv, 1) + w[0:1] * _up(dcv, 2)
            g0 = jnp.sum(dcv * u2, axis=0, keepdims=True)
            g1 = jnp.sum(dcv * u1, axis=0, keepdims=True)
            g2 = jnp.sum(dcv * u, axis=0, keepdims=True)
            row = lax.broadcasted_iota(jnp.int32, (8, TW), 0)
            gw_ref[...] = jnp.where(row == 0, g0, jnp.where(row == 1, g1, jnp.where(row == 2, g2, 0.0)))
            dp_ref[...] = (dyv * cv).astype(BF16)
            keep[0] = (du * xs).astype(BF16)
            keep[1] = (du * cs).astype(BF16)

        @pl.when(sec > 0)
        def _():
            dp_ref[...] = keep[sec - 1]

    col = lambda off: pl.BlockSpec((L, TW), lambda j, s: (0, off + j))
    return pl.pallas_call(
        kern, grid=(nb, 3),
        in_specs=[col(0), col(0), col(nb), col(2 * nb), pl.BlockSpec((8, TW), lambda j, s: (0, j)),
                  pl.BlockSpec(memory_space=pl.ANY)],
        out_specs=(pl.BlockSpec((L, TW), lambda j, s: (0, s * nb + j)), pl.BlockSpec((8, TW), lambda j, s: (0, j))),
        out_shape=(jax.ShapeDtypeStruct(dproj.shape, BF16), jax.ShapeDtypeStruct((8, D), F32)),
        scratch_shapes=[pltpu.VMEM((2, L, TW), BF16)],
        input_output_aliases={5: 0}, name="sc_bwd", compiler_params=_cp(("parallel", "arbitrary")),
    )(dya, proj, proj, proj, cw, dproj)


def _ssm_conv_fwd(proj, cw4):
    L = proj.shape[0]
    off = C_XBC // TW

    def kern(r_ref, w_ref, o_ref):
        raw = r_ref[...].astype(F32)
        w = w_ref[...]
        c4 = w[0:1] * _down(raw, 3) + w[1:2] * _down(raw, 2) + w[2:3] * _down(raw, 1) + w[3:4] * raw + w[4:5]
        o_ref[...] = c4 * _sigmoid(c4)

    return pl.pallas_call(
        kern, grid=(XBC // TW,),
        in_specs=[pl.BlockSpec((L, TW), lambda j: (0, off + j)), pl.BlockSpec((8, TW), lambda j: (0, j))],
        out_specs=pl.BlockSpec((L, TW), lambda j: (0, j)), out_shape=jax.ShapeDtypeStruct((L, XBC), F32),
        name="ssm_conv_fwd", compiler_params=_cp(("parallel",)),
    )(proj, cw4)


def _ssm_conv_bwd(dx, proj, cw4, dproj, col0, name):
    L, width = dx.shape
    off_p = (C_XBC + col0) // TW
    off_w = col0 // TW

    def kern(d_ref, r_ref, w_ref, _, dp_ref, gw_ref):
        raw = r_ref[...].astype(F32)
        w = w_ref[...]
        r1, r2, r3 = _down(raw, 1), _down(raw, 2), _down(raw, 3)
        c4 = w[0:1] * r3 + w[1:2] * r2 + w[2:3] * r1 + w[3:4] * raw + w[4:5]
        sg = _sigmoid(c4)
        dc4 = d_ref[...] * (sg * (1.0 + c4 * (1.0 - sg)))
        draw = w[3:4] * dc4 + w[2:3] * _up(dc4, 1) + w[1:2] * _up(dc4, 2) + w[0:1] * _up(dc4, 3)
        dp_ref[...] = draw.astype(BF16)
        gs = [jnp.sum(dc4 * r3, axis=0, keepdims=True), jnp.sum(dc4 * r2, axis=0, keepdims=True),
              jnp.sum(dc4 * r1, axis=0, keepdims=True), jnp.sum(dc4 * raw, axis=0, keepdims=True),
              jnp.sum(dc4, axis=0, keepdims=True)]
        row = lax.broadcasted_iota(jnp.int32, (8, TW), 0)
        acc = jnp.zeros((8, TW), F32)
        for k, gk in enumerate(gs):
            acc = jnp.where(row == k, gk, acc)
        gw_ref[...] = acc

    return pl.pallas_call(
        kern, grid=(width // TW,),
        in_specs=[pl.BlockSpec((L, TW), lambda j: (0, j)), pl.BlockSpec((L, TW), lambda j: (0, off_p + j)),
                  pl.BlockSpec((8, TW), lambda j: (0, off_w + j)), pl.BlockSpec(memory_space=pl.ANY)],
        out_specs=(pl.BlockSpec((L, TW), lambda j: (0, off_p + j)), pl.BlockSpec((8, TW), lambda j: (0, j))),
        out_shape=(jax.ShapeDtypeStruct(dproj.shape, BF16), jax.ShapeDtypeStruct((8, width), F32)),
        input_output_aliases={3: 0}, name=name, compiler_params=_cp(("arbitrary",)),
    )(dx, proj, cw4, dproj)


def _split3(v):
    h1 = v.astype(BF16)
    r1 = v - h1.astype(F32)
    h2 = r1.astype(BF16)
    h3 = (r1 - h2.astype(F32)).astype(BF16)
    return h1, h2, h3


def _dot01(m01, v, dims=_DIMS["nn"], m_left=True, terms=3):
    out = None
    for part in _split3(v)[:terms]:
        ops = (m01, part) if m_left else (part, m01)
        t = lax.dot_general(ops[0], ops[1], dims, preferred_element_type=F32)
        out = t if out is None else out + t
    return out


def _bdot(a, b, mode="nn"):
    return lax.dot_general(a.astype(BF16), b.astype(BF16), _DIMS[mode], preferred_element_type=F32)


def _softplus(v):
    return jnp.maximum(v, 0.0) + jnp.log1p(jnp.exp(-jnp.abs(v)))


def _dt_prep(proj, vec):
    L = proj.shape[0]

    def kern(p_ref, v_ref, dt_ref, cs_ref, sg_ref):
        v = v_ref[...]
        pre = p_ref[:, 0:128] + v[0:1]
        dt = _softplus(pre)
        da = dt * (-jnp.exp(v[1:2]))
        ii = lax.broadcasted_iota(jnp.int32, (Q, Q), 0)
        jj = lax.broadcasted_iota(jnp.int32, (Q, Q), 1)
        ltri = (jj <= ii).astype(BF16)
        lane = lax.broadcasted_iota(jnp.int32, (Q, 128), 1)
        for val, ref in ((dt, dt_ref), (_dot01(ltri, da), cs_ref), (_sigmoid(pre), sg_ref)):
            for g in range(NG):
                moved = val if g == 0 else pltpu.roll(val, 128 - 4 * g, axis=1)
                ref[g] = jnp.where(lane < 4, moved, 0.0)

    blk = pl.BlockSpec((NG, Q, 128), lambda c: (0, c, 0))
    return pl.pallas_call(
        kern, grid=(L // Q,),
        in_specs=[pl.BlockSpec((Q, 256), lambda c: (c, 0)), pl.BlockSpec((8, 128), lambda c: (0, 0))],
        out_specs=(blk, blk, blk),
        out_shape=(jax.ShapeDtypeStruct((NG, L, 128), F32),) * 3,
        name="dt_prep", compiler_params=_cp(("parallel",)),
    )(proj, vec)


def _head_masks():
    lane = lax.broadcasted_iota(jnp.int32, (1, 4 * HD), 1)
    return [((lane >= HD * j) & (lane < HD * (j + 1))) for j in range(4)]


def _expand4(v4, masks):
    R = v4.shape[0]
    out = jnp.zeros((R, 4 * HD), F32)
    for j in range(4):
        out = jnp.where(masks[j], jnp.broadcast_to(v4[:, j:j + 1], (R, 4 * HD)), out)
    return out


def _decay_matrix(cs_col, tri):
    colb = jnp.broadcast_to(cs_col, (Q, Q))
    return jnp.exp(jnp.where(tri, colb - colb.T, -jnp.inf))


def _ssd_fwd(xbc, dt4, cs4, vecg):
    L = xbc.shape[0]
    nc = L // Q

    def kern(x_ref, b_ref, c_ref, dt_ref, cs_ref, v_ref, y_ref, s_ref, S):
        c = pl.program_id(1)

        @pl.when(c == 0)
        def _():
            S[...] = jnp.zeros_like(S)

        masks = _head_masks()
        ii = lax.broadcasted_iota(jnp.int32, (Q, Q), 0)
        jj = lax.broadcasted_iota(jnp.int32, (Q, Q), 1)
        tri = jj <= ii
        for gi in range(GPS):
            xs, ns = slice(256 * gi, 256 * (gi + 1)), slice(NS * gi, NS * (gi + 1))
            dt4v, cs4v = dt_ref[gi], cs_ref[gi]
            dt_b, cs_b = _expand4(dt4v, masks), _expand4(cs4v, masks)
            d_b = _expand4(v_ref[gi], masks)[1:2]
            cs_last = cs_b[Q - 1:Q, :]
            x4, bm, cm = x_ref[:, xs], b_ref[:, ns], c_ref[:, ns]
            xdt = x4 * dt_b
            gm = _bdot(cm, bm, "nt")
            s4 = S[gi]
            s_ref[gi, 0] = s4
            y = _bdot(cm, s4) * jnp.exp(cs_b) + d_b * x4
            m_all = jnp.concatenate([(gm * _decay_matrix(cs4v[:, j:j + 1], tri)).astype(BF16) for j in range(4)], axis=0)
            yd = _bdot(m_all, xdt)
            for j in range(4):
                y = y + jnp.where(masks[j], yd[Q * j:Q * (j + 1)], 0.0)
            y_ref[:, xs] = y
            S[gi] = jnp.exp(cs_last) * s4 + _bdot(bm, xdt * jnp.exp(cs_last - cs_b), "tn")

    sc = pl.BlockSpec((GPS, Q, 128), lambda g, c: (g, c, 0))
    bw = NS * GPS
    return pl.pallas_call(
        kern, grid=(NG // GPS, nc),
        in_specs=[pl.BlockSpec((Q, 256 * GPS), lambda g, c: (c, g)),
                  pl.BlockSpec((Q, bw), lambda g, c: (c, INNER // bw + g)),
                  pl.BlockSpec((Q, bw), lambda g, c: (c, (INNER + NG * NS) // bw + g)),
                  sc, sc, pl.BlockSpec((GPS, 8, 128), lambda g, c: (g, 0, 0))],
        out_specs=(pl.BlockSpec((Q, 256 * GPS), lambda g, c: (c, g)),
                   pl.BlockSpec((GPS, 1, NS, 256), lambda g, c: (g, c, 0, 0))),
        out_shape=(jax.ShapeDtypeStruct((L, INNER), F32), jax.ShapeDtypeStruct((NG, nc, NS, 256), F32)),
        scratch_shapes=[pltpu.VMEM((GPS, NS, 256), F32)], name="ssd_fwd",
        compiler_params=_cp(("parallel", "arbitrary")),
    )(xbc, xbc, xbc, dt4, cs4, vecg)


def _ssd_bwd(xbc, dt4, cs4, sg4, vecg, s_all, dy):
    L = xbc.shape[0]
    nc = L // Q

    def kern(x_ref, b_ref, c_ref, dt_ref, cs_ref, sg_ref, v_ref, s_ref, dy_ref,
             dx_ref, db_ref, dc_ref, ddt_ref, st_ref, dS):
        cc = pl.program_id(1)

        @pl.when(cc == 0)
        def _():
            dS[...] = jnp.zeros_like(dS)
            st_ref[...] = jnp.zeros_like(st_ref)

        masks = _head_masks()
        ii = lax.broadcasted_iota(jnp.int32, (Q, Q), 0)
        jj = lax.broadcasted_iota(jnp.int32, (Q, Q), 1)
        tri = jj <= ii
        utri = (jj >= ii).astype(BF16)
        hsel = ((lax.broadcasted_iota(jnp.int32, (4 * HD, 128), 0) // HD)
                == lax.broadcasted_iota(jnp.int32, (4 * HD, 128), 1)).astype(BF16)
        hrow = ((lax.broadcasted_iota(jnp.int32, (4 * Q, 128), 0) // Q)
                == lax.broadcasted_iota(jnp.int32, (4 * Q, 128), 1)).astype(BF16)
        ones_q = jnp.ones((Q, 128), BF16)
        lane128 = lax.broadcasted_iota(jnp.int32, (Q, 128), 1)

        for gi in range(GPS):
            xs, ns = slice(256 * gi, 256 * (gi + 1)), slice(NS * gi, NS * (gi + 1))
            dt4v, cs4v, sg4v = dt_ref[gi], cs_ref[gi], sg_ref[gi]
            dt_b, cs_b = _expand4(dt4v, masks), _expand4(cs4v, masks)
            vv = _expand4(v_ref[gi], masks)
            a_b = -jnp.exp(vv[0:1])
            d_b = vv[1:2]
            a4 = -jnp.exp(v_ref[gi][0:1, :])
            cs_last = cs_b[Q - 1:Q, :]
            ecs = jnp.exp(cs_b)
            decay = jnp.exp(cs_last - cs_b)
            elast = jnp.exp(cs_last)
            x4, bm, cm, dyv = x_ref[:, xs], b_ref[:, ns], c_ref[:, ns], dy_ref[:, xs]
            s4 = s_ref[gi, 0]
            dsn = dS[gi]
            xdt = x4 * dt_b
            gm = _bdot(cm, bm, "nt")
            dye = dyv * ecs
            yoff = ecs * _bdot(cm, s4)
            t4 = _bdot(bm, dsn) * decay
            lms, mhs = [], []
            for j in range(4):
                colb = jnp.broadcast_to(cs4v[:, j:j + 1], (Q, Q))
                lms.append(jnp.exp(jnp.where(tri, colb - colb.T, -jnp.inf)))
                mhs.append(gm * lms[j])
            m_all = jnp.concatenate([m.astype(BF16) for m in mhs], axis=0)
            dy_m = jnp.concatenate([jnp.where(masks[j], dyv, 0.0).astype(BF16) for j in range(4)], axis=0)
            dxdt = t4 + _bdot(m_all, dy_m, "tn")
            dm_all = _bdot(dy_m, xdt, "nt")
            dg = jnp.zeros((Q, Q), F32)
            for j in range(4):
                dg = dg + dm_all[Q * j:Q * (j + 1)] * lms[j]
            e_all = dm_all * jnp.concatenate(mhs, axis=0)
            rsum = _dot01(ones_q, e_all, m_left=False, terms=2)
            da4 = -_dot01(hrow, e_all, _DIMS["tn"], m_left=False, terms=2)
            for j in range(4):
                da4 = da4 + jnp.where(lane128 == j, rsum[Q * j:Q * (j + 1)], 0.0)
            xt = xdt * t4
            tail = jnp.sum(xt, axis=0, keepdims=True) + elast * jnp.sum(s4 * dsn, axis=0, keepdims=True)
            gd_raw = jnp.sum(dyv * x4, axis=0, keepdims=True)
            stacked = jnp.concatenate([dyv * yoff - xt, dxdt * x4, jnp.broadcast_to(tail, (8, 4 * HD)),
                                       jnp.broadcast_to(gd_raw, (8, 4 * HD))], axis=0)
            seg = _dot01(hsel, stacked, m_left=False, terms=2)
            dda4 = _dot01(utri, da4 + seg[0:Q], terms=2) + seg[2 * Q:2 * Q + 1]
            ddt_ref[gi] = (dda4 * a4 + seg[Q:2 * Q]) * sg4v
            ga = jnp.sum(dda4 * dt4v * a4, axis=0, keepdims=True)
            row = lax.broadcasted_iota(jnp.int32, (8, 128), 0)
            st_ref[gi] += jnp.where(row == 0, ga, jnp.where(row == 1, seg[2 * Q + 8:2 * Q + 9], 0.0))
            dx_ref[:, xs] = d_b * dyv + dxdt * dt_b
            dc_ref[:, ns] = _bdot(dg, bm) + _bdot(dye, s4, "nt")
            db_ref[:, ns] = _bdot(dg, cm, "tn") + _bdot(xdt * decay, dsn, "nt")
            dS[gi] = elast * dsn + _bdot(cm, dye, "tn")

    rv = lambda c: nc - 1 - c
    sc = pl.BlockSpec((GPS, Q, 128), lambda g, c: (g, rv(c), 0))
    bw = NS * GPS
    return pl.pallas_call(
        kern, grid=(NG // GPS, nc),
        in_specs=[pl.BlockSpec((Q, 256 * GPS), lambda g, c: (rv(c), g)),
                  pl.BlockSpec((Q, bw), lambda g, c: (rv(c), INNER // bw + g)),
                  pl.BlockSpec((Q, bw), lambda g, c: (rv(c), (INNER + NG * NS) // bw + g)),
                  sc, sc, sc, pl.BlockSpec((GPS, 8, 128), lambda g, c: (g, 0, 0)),
                  pl.BlockSpec((GPS, 1, NS, 256), lambda g, c: (g, rv(c), 0, 0)),
                  pl.BlockSpec((Q, 256 * GPS), lambda g, c: (rv(c), g))],
        out_specs=(pl.BlockSpec((Q, 256 * GPS), lambda g, c: (rv(c), g)),
                   pl.BlockSpec((Q, bw), lambda g, c: (rv(c), g)),
                   pl.BlockSpec((Q, bw), lambda g, c: (rv(c), g)),
                   pl.BlockSpec((GPS, Q, 128), lambda g, c: (g, rv(c), 0)),
                   pl.BlockSpec((GPS, 8, 128), lambda g, c: (g, 0, 0))),
        out_shape=(jax.ShapeDtypeStruct((L, INNER), F32), jax.ShapeDtypeStruct((L, NG * NS), F32),
                   jax.ShapeDtypeStruct((L, NG * NS), F32), jax.ShapeDtypeStruct((NG, L, 128), F32),
                   jax.ShapeDtypeStruct((NG, 8, 128), F32)),
        scratch_shapes=[pltpu.VMEM((GPS, NS, 256), F32)], name="ssd_bwd",
        compiler_params=_cp(("parallel", "arbitrary")),
    )(xbc, xbc, xbc, dt4, cs4, sg4, vecg, s_all, dy)


def _dt_bwd(ddt, dproj, tl=256):
    L = ddt.shape[1]

    def kern(d_ref, _, dp_ref, gs_ref):
        @pl.when(pl.program_id(0) == 0)
        def _():
            gs_ref[...] = jnp.zeros_like(gs_ref)

        d = d_ref[0]
        for g in range(1, NG):
            d = d + pltpu.roll(d_ref[g], 4 * g, axis=1)
        gs_ref[...] += jnp.broadcast_to(jnp.sum(d, axis=0, keepdims=True), (8, 128))
        dp_ref[...] = jnp.concatenate([d, jnp.zeros_like(d)], axis=1).astype(BF16)

    return pl.pallas_call(
        kern, grid=(L // tl,),
        in_specs=[pl.BlockSpec((NG, tl, 128), lambda i: (0, i, 0)), pl.BlockSpec(memory_space=pl.ANY)],
        out_specs=(pl.BlockSpec((tl, 256), lambda i: (i, C_DT // 256)), pl.BlockSpec((8, 128), lambda i: (0, 0))),
        out_shape=(jax.ShapeDtypeStruct(dproj.shape, BF16), jax.ShapeDtypeStruct((8, 128), F32)),
        input_output_aliases={1: 0}, name="dt_bwd", compiler_params=_cp(("arbitrary",)),
    )(ddt, dproj)


GW = INNER // NG


def _gnorm_fwd(y, proj, w, tl=256):
    L = y.shape[0]
    zoff = C_Z // 1024

    def kern(y_ref, z_ref, w_ref, o_ref):
        z = z_ref[...].astype(F32)
        yz = y_ref[...] * (z * _sigmoid(z))
        wv = w_ref[...]
        for k in range(1024 // GW):
            sl = slice(GW * k, GW * (k + 1))
            v = yz[:, sl]
            rg = lax.rsqrt(jnp.mean(v * v, axis=-1, keepdims=True) + EPS)
            o_ref[:, sl] = ((v * rg) * wv[:, sl]).astype(BF16)

    blk = pl.BlockSpec((tl, 1024), lambda i, j: (i, j))
    return pl.pallas_call(
        kern, grid=(L // tl, 2),
        in_specs=[blk, pl.BlockSpec((tl, 1024), lambda i, j: (i, zoff + j)), pl.BlockSpec((1, 1024), lambda i, j: (0, j))],
        out_specs=blk, out_shape=jax.ShapeDtypeStruct((L, INNER), BF16), name="gnorm_fwd",
        compiler_params=_cp(("parallel", "parallel")),
    )(y, proj, w.reshape(1, INNER))


def _gnorm_bwd(dyb, y, proj, w, dproj, tl=256):
    L = y.shape[0]
    zoff = C_Z // 1024

    def kern(d_ref, y_ref, z_ref, w_ref, _, dy_ref, dp_ref, gw_ref):
        @pl.when(pl.program_id(1) == 0)
        def _():
            gw_ref[...] = jnp.zeros_like(gw_ref)

        z = z_ref[...].astype(F32)
        sg = _sigmoid(z)
        sz = z * sg
        yv = y_ref[...]
        yz = yv * sz
        dv = d_ref[...]
        wv = w_ref[...]
        for k in range(1024 // GW):
            sl = slice(GW * k, GW * (k + 1))
            v = yz[:, sl]
            rg = lax.rsqrt(jnp.mean(v * v, axis=-1, keepdims=True) + EPS)
            vn = v * rg
            dk = dv[:, sl]
            gw_ref[:, sl] += jnp.broadcast_to(jnp.sum(dk * vn, axis=0, keepdims=True), (8, GW))
            dvn = dk * wv[:, sl]
            dyz = rg * (dvn - vn * jnp.mean(dvn * vn, axis=-1, keepdims=True))
            dy_ref[:, sl] = dyz * sz[:, sl]
            dp_ref[:, sl] = (dyz * yv[:, sl] * (sg[:, sl] * (1.0 + z[:, sl] * (1.0 - sg[:, sl])))).astype(BF16)

    blk = pl.BlockSpec((tl, 1024), lambda j, i: (i, j))
    zblk = pl.BlockSpec((tl, 1024), lambda j, i: (i, zoff + j))
    return pl.pallas_call(
        kern, grid=(2, L // tl),
        in_specs=[blk, blk, zblk, pl.BlockSpec((1, 1024), lambda j, i: (0, j)), pl.BlockSpec(memory_space=pl.ANY)],
        out_specs=(blk, zblk, pl.BlockSpec((8, 1024), lambda j, i: (0, j))),
        out_shape=(jax.ShapeDtypeStruct((L, INNER), F32), jax.ShapeDtypeStruct(dproj.shape, BF16),
                   jax.ShapeDtypeStruct((8, INNER), F32)),
        input_output_aliases={4: 1}, name="gnorm_bwd", compiler_params=_cp(("parallel", "arbitrary")),
    )(dyb, y, proj, w.reshape(1, INNER), dproj)


def _merge_fwd(proj, bg, br_a, br_b, tl=256):
    L = proj.shape[0]
    goff = C_GATE // 1024

    def kern(g1_ref, g2_ref, b1_ref, b2_ref, a_ref, b_ref, o_ref):
        g1 = _sigmoid(g1_ref[...].astype(F32) + b1_ref[...])
        g2 = _sigmoid(g2_ref[...].astype(F32) + b2_ref[...])
        o_ref[...] = (g1 * a_ref[...] + g2 * b_ref[...]).astype(BF16)

    row = pl.BlockSpec((tl, 1024), lambda i: (i, 0))
    bg2 = bg.reshape(1, 2 * D)
    return pl.pallas_call(
        kern, grid=(L // tl,),
        in_specs=[pl.BlockSpec((tl, 1024), lambda i: (i, goff)), pl.BlockSpec((tl, 1024), lambda i: (i, goff + 1)),
                  pl.BlockSpec((1, 1024), lambda i: (0, 0)), pl.BlockSpec((1, 1024), lambda i: (0, 1)), row, row],
        out_specs=row, out_shape=jax.ShapeDtypeStruct((L, D), BF16), name="merge_fwd",
        compiler_params=_cp(("parallel",)),
    )(proj, proj, bg2, bg2, br_a, br_b)


def _merge_bwd(dm, proj, bg, br_a, br_b, dproj, tl=256):
    L = proj.shape[0]
    goff = C_GATE // 1024

    def kern(dm_ref, g_ref, b_ref, a_ref, bb_ref, _, dbr_ref, dp_ref, gb_ref):
        j = pl.program_id(0)

        @pl.when(pl.program_id(1) == 0)
        def _():
            gb_ref[...] = jnp.zeros_like(gb_ref)

        g = _sigmoid(g_ref[...].astype(F32) + b_ref[...])
        br = jnp.where(j == 0, a_ref[...], bb_ref[...])
        dmv = dm_ref[...]
        dbr_ref[0] = (dmv * g).astype(BF16)
        dgate = dmv * br * g * (1.0 - g)
        gb_ref[...] += jnp.broadcast_to(jnp.sum(dgate, axis=0, keepdims=True), (8, 1024))
        dp_ref[...] = dgate.astype(BF16)

    row = pl.BlockSpec((tl, 1024), lambda j, i: (i, 0))
    gblk = pl.BlockSpec((tl, 1024), lambda j, i: (i, goff + j))
    return pl.pallas_call(
        kern, grid=(2, L // tl),
        in_specs=[row, gblk, pl.BlockSpec((1, 1024), lambda j, i: (0, j)), row, row, pl.BlockSpec(memory_space=pl.ANY)],
        out_specs=(pl.BlockSpec((1, tl, 1024), lambda j, i: (j, i, 0)), gblk, pl.BlockSpec((8, 1024), lambda j, i: (0, j))),
        out_shape=(jax.ShapeDtypeStruct((2, L, D), BF16), jax.ShapeDtypeStruct(dproj.shape, BF16),
                   jax.ShapeDtypeStruct((8, 2 * D), F32)),
        input_output_aliases={5: 1}, name="merge_bwd", compiler_params=_cp(("parallel", "arbitrary")),
    )(dm, proj, bg.reshape(1, 2 * D), br_a, br_b, dproj)


def _coords():
    return lax.axis_index("x"), lax.axis_index("y"), lax.axis_index("c")


def _other_chips(sk):
    xk, yk = sk // 2, sk % 2
    return [((1 - xk, yk), 2 * (1 - xk) + yk), ((xk, 1 - yk), 2 * xk + 1 - yk), ((1 - xk, 1 - yk), 2 * (1 - xk) + 1 - yk)]


def _rows(start, size):
    assert size % 128 == 0
    return pl.ds(pl.multiple_of(start, 128), size)


def _per_chip(fn):
    x, y, _ = _coords()
    s = 2 * x + y
    for sk in range(4):
        pl.when(s == sk)(functools.partial(fn, sk))


XTRA = PIECE - PMAIN


def _place(shard, full_shape, block, index_map, idx, name, blk0=0, nblk=None, dep=None):
    in_block = block[-2:]
    if nblk is None:
        nblk = shard.shape[0] // in_block[0]

    def kern(idx_ref, s_ref, *rest):
        o_ref = rest[-1]
        o_ref[...] = s_ref[...].astype(BF16).reshape(o_ref.shape)

    grid_spec = pltpu.PrefetchScalarGridSpec(
        num_scalar_prefetch=1, grid=(nblk,),
        in_specs=[pl.BlockSpec(in_block, lambda i, idx_ref: (blk0 + i, 0))] + ([_ANY] if dep is not None else []),
        out_specs=pl.BlockSpec(block, index_map))
    args = (idx, shard) + ((dep,) if dep is not None else ())
    return pl.pallas_call(kern, grid_spec=grid_spec, out_shape=jax.ShapeDtypeStruct(full_shape, BF16), name=name,
                          compiler_params=_cp(("arbitrary",)))(*args)


_SEM = pl.BlockSpec(memory_space=pltpu.SEMAPHORE)
_EFFECT = pltpu.SideEffectType.DATAFLOW_SIDE_EFFECTING


_ANY = pl.BlockSpec(memory_space=pl.ANY)


def _tie(v, dep, name):
    def body(v_ref, dep_ref, o_ref):
        del v_ref, dep_ref, o_ref

    return pl.pallas_call(body, out_shape=jax.ShapeDtypeStruct(v.shape, v.dtype), in_specs=[_ANY, _ANY],
                          out_specs=_ANY, input_output_aliases={0: 0}, name=name)(v, dep)


def _split_call(name, arrays, start=None, wait=None, wait_sems=None, after=None):
    keys = list(arrays)
    n = len(keys)
    n_start = start.n if start is not None else 0
    afters = [] if after is None else (list(after) if isinstance(after, (list, tuple)) else [after])

    def body(*refs):
        pos = n
        if wait is not None:
            wss, wrs = refs[pos], refs[pos + 1]
            pos += 2
        pos += len(afters)
        if start is not None:
            nss, nrs = refs[pos], refs[pos + 1]
            pos += 2
        R = dict(zip(keys, refs[pos:pos + n]))
        token = refs[pos + n]
        x, y, c = _coords()

        def desc(src, dst, dev, ss, rs, k):
            return pltpu.make_async_remote_copy(src_ref=src, dst_ref=dst, send_sem=ss.at[k], recv_sem=rs.at[k],
                                                device_id=dev, device_id_type=MESH)

        def run(sk):
            if wait is not None:
                for k, (snd, land) in enumerate(wait.copies(sk, R)):
                    if snd is not None:
                        desc(snd[0], snd[1], snd[2], wss, wrs, k).wait_send()
                    if land is not None:
                        desc(land, land, (x, y, c), wss, wrs, k).wait_recv()
            if start is not None:
                for k, (snd, land) in enumerate(start.copies(sk, R)):
                    if snd is not None:
                        desc(snd[0], snd[1], snd[2], nss, nrs, k).start()

        _per_chip(run)
        token[...] = jnp.zeros_like(token)

    hbm = pl.BlockSpec(memory_space=HBM)
    vals = [arrays[k] for k in keys]
    ins, in_specs = list(vals), [hbm] * n
    if wait is not None:
        ins += list(wait_sems)
        in_specs += [_SEM, _SEM]
    ins += afters
    in_specs += [pl.BlockSpec(memory_space=pl.ANY)] * len(afters)
    out_shape, out_specs = [], []
    if start is not None:
        out_shape += [pltpu.SemaphoreType.DMA((n_start,)), pltpu.SemaphoreType.DMA((n_start,))]
        out_specs += [_SEM, _SEM]
    first = len(out_shape)
    out_shape += [jax.ShapeDtypeStruct(v.shape, v.dtype) for v in vals] + [jax.ShapeDtypeStruct((8, 128), F32)]
    out_specs += [hbm] * n + [pl.BlockSpec(memory_space=pltpu.VMEM)]
    res = pl.pallas_call(
        body, out_shape=tuple(out_shape), in_specs=in_specs, out_specs=tuple(out_specs),
        input_output_aliases={i: first + i for i in range(n)}, name=name,
        compiler_params=pltpu.CompilerParams(has_side_effects=_EFFECT),
    )(*ins)
    sems = (res[0], res[1]) if start is not None else None
    return dict(zip(keys, res[first:first + n])), sems, res[-1]


class _Plan:
    def __init__(self, n, copies):
        self.n, self.copies = n, copies


_HM, _HX = PMAIN // 2, XTRA // 2
WAVE0 = 768
WAVES = ((0, WAVE0), (WAVE0, _HM - WAVE0))
_WIN = {
    "wq0": (True, "wct", lambda r, sc, hc: r.at[_rows(PMAIN * sc + _HM * hc + WAVES[0][0], WAVES[0][1]), :]),
    "wq1": (True, "wct", lambda r, sc, hc: r.at[_rows(PMAIN * sc + _HM * hc + WAVES[1][0], WAVES[1][1]), :]),
    "xt": (True, "xt", lambda r, sc, hc: r.at[sc, _rows(_HX * hc, _HX), :]),
    "w1": (True, "w1", lambda r, sc, hc: r.at[_rows(512 * hc, 512), pl.ds(1024 * sc, 1024)]),
    "w2": (True, "w2", lambda r, sc, hc: r.at[_rows(1024 * sc + 512 * hc, 512), :]),
    "wa": (True, "wa", lambda r, sc, hc: r.at[_rows(256 * sc + 128 * hc, 128), :]),
    "wb": (True, "wb", lambda r, sc, hc: r.at[_rows(512 * sc + 256 * hc, 256), :]),
    "wo": (True, "wo", lambda r, sc, hc: r.at[_rows(256 * sc + 128 * hc, 128), :]),
    "cw": (False, "cw", lambda r, sc, hc: r.at[sc]),
}


def _ag_chips_plan(keys):
    def copies(sk, R):
        _, _, c = _coords()
        out = []
        for key in keys:
            _, arr, win = _WIN[key]
            for (px, py), ps in _other_chips(sk):
                w = win(R[arr], sk, c)
                out.append(((w, w, (px, py, c)), win(R[arr], ps, c)))
        return out
    return _Plan(3 * len(keys), copies)


def _ag_sibling_plan(keys):
    keys = [k for k in keys if _WIN[k][0]]

    def copies(sk, R):
        x, y, c = _coords()
        out = []
        for key in keys:
            _, arr, win = _WIN[key]
            for _, ps in _other_chips(sk):
                w = win(R[arr], ps, c)
                out.append(((w, w, (x, y, 1 - c)), win(R[arr], ps, 1 - c)))
        return out
    return _Plan(3 * len(keys), copies)


def _in_proj_wave(h, wct, wave, proj=None, tm=2048):
    L = h.shape[0]
    tm = min(tm, L)
    off, size = WAVES[wave]
    start = lambda j: pl.multiple_of(_HM * j + off, 128)

    def kern(h_ref, w_ref, *rest):
        o_ref = rest[-1]
        o_ref[...] = lax.dot_general(h_ref[...], w_ref[...], _DIMS["nt"], preferred_element_type=F32).astype(BF16)

    in_specs = [pl.BlockSpec((tm, D), lambda j, i: (i, 0)),
                pl.BlockSpec((pl.Element(size), pl.Element(D)), lambda j, i: (start(j), 0))]
    args, aliases = [h, wct], {}
    if proj is not None:
        in_specs.append(pl.BlockSpec(memory_space=pl.ANY))
        args.append(proj)
        aliases = {2: 0}
    return pl.pallas_call(
        kern, grid=(8, L // tm), in_specs=in_specs,
        out_specs=pl.BlockSpec((pl.Element(tm), pl.Element(size)), lambda j, i: (i * tm, start(j))),
        out_shape=jax.ShapeDtypeStruct((L, NCW), BF16), input_output_aliases=aliases,
        name="in_proj_wave%d" % wave, compiler_params=_cp(("parallel", "parallel")),
    )(*args)


def _fix_wct(wct, xt):
    nb = PMAIN // XTRA

    def kern(w_ref, x_ref, o_ref):
        k = pl.program_id(0)
        xv = x_ref[0]
        o_ref[...] = jnp.where(k < 3, (w_ref[...].astype(F32) + xv.astype(F32)).astype(BF16), xv)

    blk = pl.BlockSpec((XTRA, D), lambda k: (nb * (k + 1), 0))
    rblk = pl.BlockSpec((XTRA, D), lambda k: (jnp.where(k < 3, nb * (k + 1), 0), 0))
    return pl.pallas_call(
        kern, grid=(4,), in_specs=[rblk, pl.BlockSpec((1, XTRA, D), lambda k: (k, 0, 0))], out_specs=blk,
        out_shape=jax.ShapeDtypeStruct(wct.shape, BF16), input_output_aliases={0: 0}, name="fix_wct",
        compiler_params=_cp(("arbitrary",)),
    )(wct, xt)


_HP = PIECE // 2
_GWIN = [
    lambda r, sc, hc: r.at[_rows(PMAIN * sc + _HP * hc, _HP), :],
    lambda r, sc, hc: r.at[_rows(512 * hc, 512), pl.ds(1024 * sc, 1024)],
    lambda r, sc, hc: r.at[_rows(1024 * sc + 512 * hc, 512), :],
    lambda r, sc, hc: r.at[_rows(256 * sc + 128 * hc, 128), :],
    lambda r, sc, hc: r.at[_rows(512 * sc + 256 * hc, 256), :],
    lambda r, sc, hc: r.at[_rows(256 * sc + 128 * hc, 128), :],
]
HALF_SHAPES = [(PIECE // 2, D), (512, 1024), (512, 1024), (128, 1024), (256, 1024), (128, 1024)]


def _rs_sibling_plan(ts):
    def copies(sk, R):
        x, y, c = _coords()
        out = []
        for t in ts:
            for sc in range(4):
                land = R["ra%d" % t].at[sc]
                out.append(((_GWIN[t](R["g%d" % t], sc, 1 - c), land, (x, y, 1 - c)), land))
        return out
    return _Plan(4 * len(ts), copies)


def _rs_chips_plan(ts):
    def copies(sk, R):
        _, _, c = _coords()
        out = []
        for t in ts:
            for j, ((px, py), ps) in enumerate(_other_chips(sk)):
                land = R["rb%d" % t].at[j]
                out.append(((R["hb%d" % t].at[ps], land, (px, py, c)), land))
        return out
    return _Plan(3 * len(ts), copies)


def _rs_share_plan(ts):
    def copies(sk, R):
        x, y, c = _coords()
        out = []
        for t in ts:
            rows = HALF_SHAPES[t][0]
            mine = R["f%d" % t].at[_rows(rows * c, rows), :]
            out.append(((mine, mine, (x, y, 1 - c)), R["f%d" % t].at[_rows(rows * (1 - c), rows), :]))
        return out
    return _Plan(len(ts), copies)


def _half_tiling(t):
    rows, cols = HALF_SHAPES[t]
    if t == 0:
        return (rows // 2, cols), 2, lambda i: (i, 0)
    return (rows, cols), 1, lambda i: (0, 0)


def _window_spec(t, blk):
    if t == 0:
        return pl.BlockSpec((pl.Element(blk[0]), pl.Element(blk[1])), lambda i, sc, idx_ref: (
            pl.multiple_of(PMAIN * sc + _HP * idx_ref[1] + blk[0] * i, 128), 0))
    if t == 1:
        return pl.BlockSpec(blk, lambda i, sc, idx_ref: (idx_ref[1], sc))
    return pl.BlockSpec(blk, lambda i, sc, idx_ref: (2 * sc + idx_ref[1], 0))


def _chip_sum(g, ra, t, idx, name):
    rows, cols = HALF_SHAPES[t]
    blk, nblk, inner = _half_tiling(t)

    def kern(idx_ref, g_ref, r_ref, hb_ref, hf_ref):
        v = g_ref[...].astype(F32) + r_ref[0].astype(F32)
        hb_ref[0] = v.astype(BF16)

        @pl.when(pl.program_id(1) == idx_ref[0])
        def _():
            hf_ref[...] = v

    omap = lambda i, sc, idx_ref: (sc,) + inner(i)
    grid_spec = pltpu.PrefetchScalarGridSpec(
        num_scalar_prefetch=1, grid=(nblk, 4),
        in_specs=[_window_spec(t, blk), pl.BlockSpec((1,) + blk, omap)],
        out_specs=(pl.BlockSpec((1,) + blk, omap), pl.BlockSpec(blk, lambda i, sc, idx_ref: inner(i))))
    return pl.pallas_call(
        kern, grid_spec=grid_spec,
        out_shape=(jax.ShapeDtypeStruct((4, rows, cols), BF16), jax.ShapeDtypeStruct((rows, cols), F32)),
        name=name, compiler_params=_cp(("parallel", "arbitrary")),
    )(idx, g, ra)


def _final_sum(hf, rb, t, idx, name):
    rows, cols = HALF_SHAPES[t]
    blk, nblk, inner = _half_tiling(t)
    nbr = rows // blk[0]

    def kern(idx_ref, h_ref, r_ref, o_ref):
        o_ref[...] = ((h_ref[...] + r_ref[0].astype(F32)) + r_ref[1].astype(F32)) + r_ref[2].astype(F32)

    def omap(i, idx_ref):
        r, cidx = inner(i)
        return nbr * idx_ref[1] + r, cidx

    grid_spec = pltpu.PrefetchScalarGridSpec(
        num_scalar_prefetch=1, grid=(nblk,),
        in_specs=[pl.BlockSpec(blk, lambda i, idx_ref: inner(i)),
                  pl.BlockSpec((3,) + blk, lambda i, idx_ref: (0,) + inner(i))],
        out_specs=pl.BlockSpec(blk, omap))
    return pl.pallas_call(
        kern, grid_spec=grid_spec, out_shape=jax.ShapeDtypeStruct((2 * rows, cols), F32),
        name=name, compiler_params=_cp(("parallel",)),
    )(idx, hf, rb)


class _ReduceScatter:
    def __init__(self, ts, grads, idx, tag):
        self.ts, self.idx, self.tag = ts, idx, tag
        arr = {}
        for t in ts:
            arr["g%d" % t] = grads[t]
            arr["ra%d" % t] = lax.empty((4,) + HALF_SHAPES[t], BF16)
        self.plan = _rs_sibling_plan(ts)
        self.arr, self.sems, self.token = _split_call("rs_sibling_start_" + tag, arr, start=self.plan)

    def chips(self, after):
        arr, _, _ = _split_call("rs_sibling_wait_" + self.tag, self.arr, wait=self.plan, wait_sems=self.sems, after=after)
        brr, self.hf = {}, {}
        for t in self.ts:
            hb, self.hf[t] = _chip_sum(arr["g%d" % t], arr["ra%d" % t], t, self.idx, "chip_sum_%d" % t)
            brr["hb%d" % t] = hb
            brr["rb%d" % t] = lax.empty((3,) + HALF_SHAPES[t], BF16)
        self.plan = _rs_chips_plan(self.ts)
        self.arr, self.sems, self.token = _split_call("rs_chips_start_" + self.tag, brr, start=self.plan)
        return self.token

    def share(self, after):
        brr, _, _ = _split_call("rs_chips_wait_" + self.tag, self.arr, wait=self.plan, wait_sems=self.sems, after=after)
        frr = {"f%d" % t: _final_sum(self.hf[t], brr["rb%d" % t], t, self.idx, "final_sum_%d" % t) for t in self.ts}
        self.plan = _rs_share_plan(self.ts)
        self.arr, self.sems, self.token = _split_call("rs_share_start_" + self.tag, frr, start=self.plan)
        return self.token

    def result(self, after):
        frr, _, _ = _split_call("rs_share_wait_" + self.tag, self.arr, wait=self.plan, wait_sems=self.sems, after=after)
        return {t: frr["f%d" % t] for t in self.ts}


def _all8_plan(key):
    def copies(sk, R):
        x, y, c = _coords()
        own = R[key].at[4 * x + 2 * y + c]
        out = []
        for k in range(1, 8):
            dev = ((1 - x) if (k >> 2) & 1 else x, (1 - y) if (k >> 1) & 1 else y, (1 - c) if k & 1 else c)
            out.append(((own, own, dev), R[key].at[4 * dev[0] + 2 * dev[1] + dev[2]]))
        return out
    return _Plan(7, copies)


def _small_all_gather(v):
    def body(v_ref, o_ref, send_sems, recv_sems, loc_sem):
        x, y, c = _coords()
        me = 4 * x + 2 * y + c
        lc = pltpu.make_async_copy(v_ref, o_ref.at[me], loc_sem)
        lc.start()
        cps = []
        for k in range(1, 8):
            fx, fy, fc = (k >> 2) & 1, (k >> 1) & 1, k & 1
            dev = ((1 - x) if fx else x, (1 - y) if fy else y, (1 - c) if fc else c)
            cp = pltpu.make_async_remote_copy(src_ref=v_ref, dst_ref=o_ref.at[me], send_sem=send_sems.at[k - 1],
                                              recv_sem=recv_sems.at[k - 1], device_id=dev, device_id_type=MESH)
            cp.start()
            cps.append((cp, 4 * dev[0] + 2 * dev[1] + dev[2]))
        for k, (cp, frm) in enumerate(cps):
            got = o_ref.at[frm]
            pltpu.make_async_remote_copy(src_ref=got, dst_ref=got, send_sem=send_sems.at[k], recv_sem=recv_sems.at[k],
                                         device_id=(x, y, c), device_id_type=MESH).wait_recv()
        for cp, _ in cps:
            cp.wait_send()
        lc.wait()

    hbm = pl.BlockSpec(memory_space=HBM)
    return pl.pallas_call(
        body, out_shape=jax.ShapeDtypeStruct((8,) + v.shape, F32), in_specs=[hbm], out_specs=hbm,
        scratch_shapes=[pltpu.SemaphoreType.DMA((7,)), pltpu.SemaphoreType.DMA((7,)), pltpu.SemaphoreType.DMA(())],
        name="small_all_gather", compiler_params=pltpu.CompilerParams(has_side_effects=True),
    )(v)


def _sum8(v, name="small_sum"):
    def kern(v_ref, o_ref):
        acc = v_ref[0]
        for k in range(1, 8):
            acc = acc + v_ref[k]
        o_ref[...] = acc

    return pl.pallas_call(kern, out_shape=jax.ShapeDtypeStruct(v.shape[1:], F32), name=name)(v)


def _adamw(w, g, m, v, name, tr=128, blk0=0, nblk=None, into=None, copy_g=False):
    R, C = w.shape
    tr = min(tr, R)
    if nblk is None:
        assert R % tr == 0 and blk0 == 0
        nblk = R // tr
    n_out = 4 if copy_g else 3

    def kern(*refs):
        w_ref, g_ref, m_ref, v_ref = refs[:4]
        d_ref, mo_ref, vo_ref = refs[-n_out:][:3]
        gv = g_ref[...]
        mn = ADAM_B1 * m_ref[...] + (1.0 - ADAM_B1) * gv
        vn = ADAM_B2 * v_ref[...] + (1.0 - ADAM_B2) * (gv * gv)
        m_hat = mn / (1.0 - ADAM_B1 ** ADAM_STEP)
        v_hat = vn / (1.0 - ADAM_B2 ** ADAM_STEP)
        d_ref[...] = -ADAM_LR * (m_hat / (jnp.sqrt(v_hat) + ADAM_EPS) + ADAM_WD * w_ref[...])
        mo_ref[...] = mn
        vo_ref[...] = vn
        if copy_g:
            refs[-1][...] = gv

    blk = pl.BlockSpec((tr, C), lambda i: (blk0 + i, 0))
    sd = jax.ShapeDtypeStruct((R, C), F32)
    in_specs, args, aliases = [blk] * 4, [w, g, m, v], {}
    if into is not None:
        in_specs += [pl.BlockSpec(memory_space=pl.ANY)] * 3
        args += list(into)
        aliases = {4: 0, 5: 1, 6: 2}
    return pl.pallas_call(kern, grid=(nblk,), in_specs=in_specs, out_specs=(blk,) * n_out, out_shape=(sd,) * n_out,
                          input_output_aliases=aliases, name=name, compiler_params=_cp(("parallel",)))(*args)


def _adamw_w_in(wt, gp, mt, vt, offs, name, r0, tr, nblk, views, into=None):
    el = lambda n: (pl.Element(n), pl.Element(D))
    own = pl.BlockSpec(el(tr), lambda i, o: (pl.multiple_of(r0 + tr * i, 8), 0))

    def view(k):
        return pl.BlockSpec(el(tr), lambda i, o: (pl.multiple_of(jnp.maximum(r0 + tr * i + o[k], 0), 8), 0))

    def kern(o_ref, w_ref, m_ref, v_ref, *refs):
        g_refs, (d_ref, mo_ref, vo_ref, go_ref) = refs[:len(views)], refs[-4:]
        gv = g_refs[0][...]
        if len(views) == 2:
            row = r0 + tr * pl.program_id(0) + lax.broadcasted_iota(jnp.int32, (tr, D), 0)
            gv = jnp.where(row < o_ref[2], gv, g_refs[1][...])
        mn = ADAM_B1 * m_ref[...] + (1.0 - ADAM_B1) * gv
        vn = ADAM_B2 * v_ref[...] + (1.0 - ADAM_B2) * (gv * gv)
        m_hat = mn / (1.0 - ADAM_B1 ** ADAM_STEP)
        v_hat = vn / (1.0 - ADAM_B2 ** ADAM_STEP)
        d_ref[...] = -ADAM_LR * (m_hat / (jnp.sqrt(v_hat) + ADAM_EPS) + ADAM_WD * w_ref[...])
        mo_ref[...] = mn
        vo_ref[...] = vn
        go_ref[...] = gv

    in_specs = [own, own, own] + [view(k) for k in views]
    args = [wt, mt, vt] + [gp] * len(views)
    aliases = {}
    if into is not None:
        in_specs += [pl.BlockSpec(memory_space=pl.ANY)] * 4
        args += list(into)
        aliases = {1 + len(args) - 4 + j: j for j in range(4)}
    grid_spec = pltpu.PrefetchScalarGridSpec(num_scalar_prefetch=1, grid=(nblk,), in_specs=in_specs,
                                             out_specs=(own,) * 4)
    sd = jax.ShapeDtypeStruct(wt.shape, F32)
    return pl.pallas_call(kern, grid_spec=grid_spec, out_shape=(sd,) * 4, input_output_aliases=aliases, name=name,
                          compiler_params=_cp(("parallel",)))(offs, *args)


def _to_piece(wt, s):
    z = lambda n: jnp.zeros((n, D), wt.dtype)
    pads = [functools.partial(lambda k, w: jnp.pad(w, ((8 * k, PIECE - W_SHARD - 8 * k), (0, 0))), k) for k in range(3)]
    last = lambda w: jnp.concatenate([z(24), w[:744], w[776:], w[744:776], z(PIECE - 24 - W_SHARD)], axis=0)
    return lax.switch(s, pads + [last], wt)


def _from_piece(p, s):
    cuts = [functools.partial(lambda k, q: q[8 * k:8 * k + W_SHARD], k) for k in range(3)]
    last = lambda q: jnp.concatenate([q[24:768], q[2816:2848], q[768:2816]], axis=0)
    return lax.switch(s, cuts + [last], p)


_SMALL = [("b_gate", 2048), ("ssm_conv_b", 4096), ("dt_bias", 32), ("A_log", 32), ("D_skip", 32),
          ("ssm_norm_w", 2048), ("norm_mlp", 1024), ("norm_final", 1024), ("sc_conv_w", 3072), ("ssm_conv_w", 16384),
          ("loss", 1)]


def _pack(vals, table, rows):
    parts = []
    for name, n in table:
        v = vals[name].reshape(-1).astype(F32)
        pad = (-n) % 128
        parts.append(jnp.pad(v, (0, pad)) if pad else v)
    flat = jnp.concatenate(parts)
    return jnp.pad(flat, (0, rows * 128 - flat.shape[0])).reshape(rows, 128)


def _unpack(arr, table):
    flat = arr.reshape(-1)
    out, off = {}, 0
    for name, n in table:
        out[name] = flat[off:off + n]
        off += n + ((-n) % 128)
    return out


def kernel(x, norm_mix, w_in, b_gate, sc_conv_w, ssm_conv_w, ssm_conv_b, dt_bias, A_log, D_skip, ssm_norm_w, w_branch_sc, w_branch_ssm, w_out, norm_mlp, w_mlp1, w_mlp2, norm_final, loss_target, m_norm_mix, m_w_in, m_b_gate, m_sc_conv_w, m_ssm_conv_w, m_ssm_conv_b, m_dt_bias, m_A_log, m_D_skip, m_ssm_norm_w, m_w_branch_sc, m_w_branch_ssm, m_w_out, m_norm_mlp, m_w_mlp1, m_w_mlp2, m_norm_final, v_norm_mix, v_w_in, v_b_gate, v_sc_conv_w, v_ssm_conv_w, v_ssm_conv_b, v_dt_bias, v_A_log, v_D_skip, v_ssm_norm_w, v_w_branch_sc, v_w_branch_ssm, v_w_out, v_norm_mlp, v_w_mlp1, v_w_mlp2, v_norm_final):
    L = x.shape[1]
    nc = L // Q
    xi, yi, ci = lax.axis_index("x"), lax.axis_index("y"), lax.axis_index("c")
    s = 2 * xi + yi
    idx = jnp.stack([s, ci]).astype(jnp.int32)
    x0 = x.reshape(L, D)
    tgt = loss_target.reshape(L, D)

    piece = _to_piece(w_in.T, s)
    nb = PMAIN // XTRA
    wct0 = _place(piece, (NCW, D), (XTRA, D), lambda i, r: (nb * r[0] + i, 0), idx, "place_wct", nblk=nb)
    xt0 = _place(piece, (4, XTRA, D), (1, XTRA, D), lambda i, r: (r[0], 0, 0), idx, "place_xt", blk0=nb, nblk=1)
    cws = jnp.zeros((8, 1280), F32)
    cws = cws.at[0:3, 0:256].set(sc_conv_w).at[0:4, 256:1280].set(ssm_conv_w)
    cw0 = lax.dynamic_update_slice(jnp.zeros((4, 8, 1280), F32), cws[None], (s, 0, 0))
    win_keys, win2_keys, mid_keys, end_keys = ["xt", "cw", "wq0"], ["wq1"], ["wa", "wb", "wo", "w1"], ["w2"]
    gw, sems_w, tok = _split_call("ag_win_start", {"wct": wct0, "xt": xt0, "cw": cw0}, start=_ag_chips_plan(win_keys))
    g2, sems_w2, tok = _split_call("ag_win2_start", {"wct": gw["wct"]}, start=_ag_chips_plan(win2_keys), after=tok)
    gw["wct"] = g2["wct"]
    wa0 = _place(w_branch_sc, (D, D), (256, 1024), lambda i, r: (r[0], 0), idx, "place_wa", dep=tok)
    wb0 = _place(w_branch_ssm, (INNER, D), (512, 1024), lambda i, r: (r[0], 0), idx, "place_wb", dep=tok)
    wo0 = _place(w_out, (D, D), (256, 1024), lambda i, r: (r[0], 0), idx, "place_wo", dep=tok)
    w10 = _place(w_mlp1, (D, DFF), (256, 1024), lambda i, r: (i, r[0]), idx, "place_w1", dep=tok)
    gm, sems_m, tok = _split_call("ag_mid_start", {"wa": wa0, "wb": wb0, "wo": wo0, "w1": w10},
                                  start=_ag_chips_plan(mid_keys))
    w20 = _place(w_mlp2, (DFF, D), (256, 1024), lambda i, r: (4 * r[0] + i, 0), idx, "place_w2", dep=tok)
    ge, sems_e, tok = _split_call("ag_end_start", {"w2": w20}, start=_ag_chips_plan(end_keys))
    h = _rms_fwd(x0, norm_mix, "rms_mix", dep=tok)
    gw, sems_w, tok = _split_call("ag_win_pass", gw, wait=_ag_chips_plan(win_keys), wait_sems=sems_w,
                                  start=_ag_sibling_plan(win_keys), after=h)
    gw, _, _ = _split_call("ag_win_done", gw, wait=_ag_sibling_plan(win_keys), wait_sems=sems_w, after=tok)
    wc, cw_all = _fix_wct(gw["wct"], gw["xt"]), gw["cw"]
    sc_w_full = jnp.concatenate([cw_all[k, :, 0:256] for k in range(4)], axis=1)
    ssm_w_full = jnp.concatenate([cw_all[k, :, 256:1280] for k in range(4)], axis=1)
    cw4 = ssm_w_full.at[4].set(ssm_conv_b)
    vec = jnp.zeros((8, 128), F32).at[0, :NH].set(dt_bias).at[1, :NH].set(A_log)
    vecg = jnp.zeros((NG, 8, 128), F32).at[:, 0, :4].set(A_log.reshape(NG, 4)).at[:, 1, :4].set(D_skip.reshape(NG, 4))

    dtraw = _matmul(h, wc[C_DT:], "nt", F32, 512, 256, 1024, "in_proj_dt")
    proj = _in_proj_wave(h, wc, 0)
    g2, sems_w2, tok = _split_call("ag_win2_pass", {"wct": wc}, wait=_ag_chips_plan(win2_keys), wait_sems=sems_w2,
                                   start=_ag_sibling_plan(win2_keys), after=[proj, dtraw])
    g2, _, _ = _split_call("ag_win2_done", g2, wait=_ag_sibling_plan(win2_keys), wait_sems=sems_w2, after=tok)
    wc = g2["wct"]
    proj = _in_proj_wave(h, wc, 1, proj=proj)
    ya = _sc_fwd(proj, sc_w_full)
    xbc = _ssm_conv_fwd(proj, cw4)
    dt4, cs4, sg4 = _dt_prep(dtraw, vec)
    gm, sems_m, tok = _split_call("ag_mid_pass", gm, wait=_ag_chips_plan(mid_keys), wait_sems=sems_m,
                                  start=_ag_sibling_plan(mid_keys), after=[xbc, ya, dt4])
    xbc = _tie(xbc, tok, "tie_xbc")
    y, s_all = _ssd_fwd(xbc, dt4, cs4, vecg)
    yb = _gnorm_fwd(y, proj, ssm_norm_w)
    gm, _, _ = _split_call("ag_mid_done", gm, wait=_ag_sibling_plan(mid_keys), wait_sems=sems_m, after=yb)
    wa, wb, wo, w1 = gm["wa"], gm["wb"], gm["wo"], gm["w1"]
    ge, sems_e, tok = _split_call("ag_end_pass", ge, wait=_ag_chips_plan(end_keys), wait_sems=sems_e,
                                  start=_ag_sibling_plan(end_keys), after=yb)
    br_a = _matmul(ya, wa, "nn", F32, 1024, 1024, 1024, "branch_sc", dep=tok)
    br_b = _matmul(yb, wb, "nn", F32, 1024, 1024, 2048, "branch_ssm")
    merged = _merge_fwd(proj, b_gate, br_a, br_b)
    x1 = _matmul(merged, wo, "nn", F32, 1024, 1024, 1024, "out_proj", epi="res", extra=x0)
    h2 = _rms_fwd(x1, norm_mlp, "rms_mlp")
    a1, rl = _matmul(h2, w1, "nn", BF16, 1024, 1024, 1024, "mlp1", epi="relu2", n_outer=True)
    ge, _, _ = _split_call("ag_end_done", ge, wait=_ag_sibling_plan(end_keys), wait_sems=sems_e, after=a1)
    w2 = ge["w2"]
    x2 = _matmul(rl, w2, "nn", F32, 512, 1024, 4096, "mlp2", epi="res", extra=x1)
    dx2, g_nf, loss8 = _final(x2, norm_final, tgt)

    da = _matmul(dx2, w2, "nt", BF16, 1024, 1024, 1024, "mlp2_dx", epi="drelu", extra=a1, n_outer=True)
    g_w2 = _matmul(rl, dx2, "tn", BF16, 1024, 1024, 2048, "mlp2_dw")
    g_w1 = _matmul(h2, da, "tn", BF16, 1024, 1024, 2048, "mlp1_dw")
    dh2 = _matmul(da, w1, "nt", F32, 512, 1024, 4096, "mlp1_dx")
    dx1, g_nmlp = _rms_bwd(dh2, x1, norm_mlp, dx2, "rms_mlp_bwd")
    dmerged = _matmul(dx1, wo, "nt", F32, 1024, 1024, 1024, "out_proj_dx")
    g_wo = _matmul(merged, dx1, "tn", BF16, 1024, 1024, 2048, "out_proj_dw")
    dproj = lax.empty((L, NCW), BF16)
    dbr, dproj, g_bg = _merge_bwd(dmerged, proj, b_gate, br_a, br_b, dproj)
    dya = _matmul(dbr[0], wa, "nt", F32, 1024, 1024, 1024, "branch_sc_dx")
    g_wa = _matmul(ya, dbr[0], "tn", BF16, 1024, 1024, 2048, "branch_sc_dw")
    dproj, g_scw = _sc_bwd(dya, proj, sc_w_full, dproj)
    dyb = _matmul(dbr[1], wb, "nt", F32, 1024, 1024, 1024, "branch_ssm_dx", n_outer=True)
    g_wb = _matmul(yb, dbr[1], "tn", BF16, 1024, 1024, 2048, "branch_ssm_dw")
    rs_a = _ReduceScatter([1, 2, 3, 4, 5], {1: g_w1, 2: g_w2, 3: g_wa, 4: g_wb, 5: g_wo}, idx, "a")
    dy, dproj, g_snw = _gnorm_bwd(_tie(dyb, rs_a.token, "tie_dyb"), y, proj, ssm_norm_w, dproj)
    tok = rs_a.chips(after=dy)
    dxs, dbm, dcm, ddt_g, st = _ssd_bwd(xbc, dt4, cs4, sg4, vecg, s_all, _tie(dy, tok, "tie_dy"))
    dproj, gx1 = _ssm_conv_bwd(dxs, proj, cw4, dproj, 0, "ssm_conv_bwd_x")
    dproj, gx2 = _ssm_conv_bwd(dbm, proj, cw4, dproj, INNER, "ssm_conv_bwd_b")
    dproj, gx3 = _ssm_conv_bwd(dcm, proj, cw4, dproj, INNER + NG * NS, "ssm_conv_bwd_c")
    g_cw4 = jnp.concatenate([gx1, gx2, gx3], axis=1)
    dproj, g_dtb = _dt_bwd(ddt_g, dproj)
    small = {"b_gate": g_bg[0], "ssm_conv_b": g_cw4[4], "dt_bias": g_dtb[0, :NH],
             "A_log": st[:, 0, :4], "D_skip": st[:, 1, :4], "ssm_norm_w": g_snw[0], "norm_mlp": g_nmlp[0],
             "norm_final": g_nf[0], "sc_conv_w": g_scw[0:3], "ssm_conv_w": g_cw4[0:4], "loss": loss8[0, 0:1]}
    me = 4 * xi + 2 * yi + ci
    sm8 = lax.dynamic_update_slice(jnp.zeros((8, SMALL_ROWS, 128), F32), _pack(small, _SMALL, SMALL_ROWS)[None], (me, 0, 0))
    sm_arr, sm_sems, tok = _split_call("small_start", {"sm": sm8}, start=_all8_plan("sm"))
    g_wc = _matmul(dproj, h, "tn", BF16, 1280, 1024, 2048, "in_proj_dw", dep=tok)
    sm_arr, _, _ = _split_call("small_wait", sm_arr, wait=_all8_plan("sm"), wait_sems=sm_sems, after=g_wc)
    small_sum = _sum8(sm_arr["sm"])
    gs = _unpack(small_sum, _SMALL)
    rs_b = _ReduceScatter([0], {0: g_wc}, idx, "b")
    tok = rs_a.share(after=[rs_b.token, small_sum])
    tok = rs_b.chips(after=tok)
    dh = _matmul(dproj, wc, "nn", F32, 512, 1024, 5760, "in_proj_dx", dep=tok)
    grad_x, g_nm = _rms_bwd(dh, x0, norm_mix, dx1, "rms_mix_bwd")
    nm8 = lax.dynamic_update_slice(jnp.zeros((8, 8, 128), F32), g_nm[0].reshape(1, 8, 128), (me, 0, 0))
    nm_arr, nm_sems, tok = _split_call("norm_mix_start", {"nm": nm8}, start=_all8_plan("nm"))
    red = rs_a.result(after=tok)
    big = {"w_mlp1": red[1], "w_mlp2": red[2], "w_branch_sc": red[3], "w_branch_ssm": red[4], "w_out": red[5]}

    given = dict(norm_mix=norm_mix, w_in=w_in, b_gate=b_gate, sc_conv_w=sc_conv_w, ssm_conv_w=ssm_conv_w, ssm_conv_b=ssm_conv_b, dt_bias=dt_bias, A_log=A_log, D_skip=D_skip, ssm_norm_w=ssm_norm_w, w_branch_sc=w_branch_sc, w_branch_ssm=w_branch_ssm, w_out=w_out, norm_mlp=norm_mlp, w_mlp1=w_mlp1, w_mlp2=w_mlp2, norm_final=norm_final,
                 m_norm_mix=m_norm_mix, m_w_in=m_w_in, m_b_gate=m_b_gate, m_sc_conv_w=m_sc_conv_w, m_ssm_conv_w=m_ssm_conv_w, m_ssm_conv_b=m_ssm_conv_b, m_dt_bias=m_dt_bias, m_A_log=m_A_log, m_D_skip=m_D_skip, m_ssm_norm_w=m_ssm_norm_w, m_w_branch_sc=m_w_branch_sc, m_w_branch_ssm=m_w_branch_ssm, m_w_out=m_w_out, m_norm_mlp=m_norm_mlp, m_w_mlp1=m_w_mlp1, m_w_mlp2=m_w_mlp2, m_norm_final=m_norm_final,
                 v_norm_mix=v_norm_mix, v_w_in=v_w_in, v_b_gate=v_b_gate, v_sc_conv_w=v_sc_conv_w, v_ssm_conv_w=v_ssm_conv_w, v_ssm_conv_b=v_ssm_conv_b, v_dt_bias=v_dt_bias, v_A_log=v_A_log, v_D_skip=v_D_skip, v_ssm_norm_w=v_ssm_norm_w, v_w_branch_sc=v_w_branch_sc, v_w_branch_ssm=v_w_branch_ssm, v_w_out=v_w_out, v_norm_mlp=v_norm_mlp, v_w_mlp1=v_w_mlp1, v_w_mlp2=v_w_mlp2, v_norm_final=v_norm_final)
    order = ["norm_mix", "w_in", "b_gate", "sc_conv_w", "ssm_conv_w", "ssm_conv_b", "dt_bias", "A_log", "D_skip",
             "ssm_norm_w", "w_branch_sc", "w_branch_ssm", "w_out", "norm_mlp", "w_mlp1", "w_mlp2", "norm_final"]
    grad, delta, new_m, new_v = {}, {}, {}, {}
    for n in big:
        delta[n], new_m[n], new_v[n], grad[n] = _adamw(given[n], big[n], given["m_" + n], given["v_" + n],
                                                       "adamw_" + n, copy_g=True)
    big["w_in"] = None
    grad_small = {n: gs[n].reshape(given[n].shape) for n in order
                  if n not in big and n not in ("sc_conv_w", "ssm_conv_w", "norm_mix")}
    grad_small["sc_conv_w"] = lax.dynamic_slice(gs["sc_conv_w"].reshape(3, D), (0, 256 * s), (3, 256))
    grad_small["ssm_conv_w"] = lax.dynamic_slice(gs["ssm_conv_w"].reshape(4, XBC), (0, 1024 * s), (4, 1024))
    table = [(n, int(grad_small[n].size)) for n in grad_small]
    rows = 136
    pk = lambda d: _pack(d, table, rows)
    ds_, ms_, vs_ = _adamw(pk({n: given[n] for n in grad_small}), pk(grad_small), pk({n: given["m_" + n] for n in grad_small}),
                           pk({n: given["v_" + n] for n in grad_small}), "adamw_small", tr=rows)
    ds_, ms_, vs_ = _unpack(ds_, table), _unpack(ms_, table), _unpack(vs_, table)
    for n in grad_small:
        shp = given[n].shape
        grad[n] = grad_small[n]
        delta[n], new_m[n], new_v[n] = ds_[n].reshape(shp), ms_[n].reshape(shp), vs_[n].reshape(shp)

    done = [new_v[n] for n in ("w_mlp1", "w_mlp2", "w_branch_sc", "w_branch_ssm", "w_out")] + [vs_["b_gate"]]
    tok = rs_b.share(after=done)
    gp = rs_b.result(after=tok)[0]
    offs = jnp.where(s == 3, jnp.array([24, -8, 744, 2072, -8], jnp.int32),
                     jnp.stack([8 * s, 8 * s, 0 * s, 8 * s, 8 * s]).astype(jnp.int32))
    wt_args = (w_in.T, gp, m_w_in.T, v_w_in.T, offs)
    nmain = W_SHARD // 256
    res = _adamw_w_in(*wt_args, "adamw_w_in", 0, 256, nmain, (0, 1))
    res = _adamw_w_in(*wt_args, "adamw_w_in_dt", 744, 32, 1, (3,), into=res)
    dt_, mt_, vt_, gwt = _adamw_w_in(*wt_args, "adamw_w_in_tail", 256 * nmain, 8, 1, (4,), into=res)
    grad["w_in"], delta["w_in"], new_m["w_in"], new_v["w_in"] = gwt.T, dt_.T, mt_.T, vt_.T
    nm_arr, _, _ = _split_call("norm_mix_wait", nm_arr, wait=_all8_plan("nm"), wait_sems=nm_sems, after=tok)
    g8 = _sum8(nm_arr["nm"], "norm_mix_sum")
    r8 = lambda a: a.reshape(8, 128)
    d8, m8, v8 = _adamw(r8(norm_mix), g8, r8(m_norm_mix), r8(v_norm_mix), "adamw_norm_mix", tr=8)
    grad["norm_mix"], delta["norm_mix"] = g8.reshape(D), d8.reshape(D)
    new_m["norm_mix"], new_v["norm_mix"] = m8.reshape(D), v8.reshape(D)

    loss = gs["loss"].reshape(())
    return (loss, grad_x.reshape(1, L, D), *[grad[n] for n in order], *[delta[n] for n in order],
            *[new_m[n] for n in order], *[new_v[n] for n in order])
```

```python
import functools

import jax
import jax.numpy as jnp
from jax import lax
from jax.experimental import pallas as pl
from jax.experimental.pallas import tpu as pltpu

F32 = jnp.float32
BF16 = jnp.bfloat16
MESH = pl.DeviceIdType.MESH
HBM = pltpu.HBM

D = 1024
INNER = 2048
HD = 64
NH = 32
NG = 8
NS = 128
Q = 128
GPS = 4
XBC = 4096
DFF = 4096
EPS = 1e-6
W_SHARD = 2824
NCW = 11520
PIECE = 3072
PMAIN = 2816
C_Z, C_XBC, C_GATE, C_DT = 3072, 5120, 9216, 11264
SMALL_ROWS = 256
VMEM_LIMIT = 56 * 1024 * 1024

ADAM_LR, ADAM_B1, ADAM_B2, ADAM_EPS, ADAM_WD, ADAM_STEP = 0.001, 0.9, 0.999, 1e-08, 0.01, 10


def _cp(sem=None, vmem=VMEM_LIMIT):
    return pltpu.CompilerParams(dimension_semantics=sem, vmem_limit_bytes=vmem)


def _sigmoid(v):
    return 1.0 / (1.0 + jnp.exp(-v))


_DIMS = {"nn": (((1,), (0,)), ((), ())), "nt": (((1,), (1,)), ((), ())), "tn": (((0,), (0,)), ((), ()))}


def _matmul(a, b, mode, out_dtype, tm, tn, tk, name, epi=None, extra=None, n_outer=False, dep=None):
    if mode == "tn":
        K, M = a.shape
    else:
        M, K = a.shape
    N = b.shape[0] if mode == "nt" else b.shape[1]
    tm, tn, tk = min(tm, M), min(tn, N), min(tk, K)
    assert M % tm == 0 and N % tn == 0 and K % tk == 0, (name, M, N, K, tm, tn, tk)
    nm, nn, nk = M // tm, N // tn, K // tk
    dims = _DIMS[mode]

    def ij(p0, p1):
        return (p1, p0) if n_outer else (p0, p1)

    if mode == "tn":
        a_spec = pl.BlockSpec((tk, tm), lambda p0, p1, k: (k, ij(p0, p1)[0]))
    else:
        a_spec = pl.BlockSpec((tm, tk), lambda p0, p1, k: (ij(p0, p1)[0], k))
    if mode == "nt":
        b_spec = pl.BlockSpec((tn, tk), lambda p0, p1, k: (ij(p0, p1)[1], k))
    else:
        b_spec = pl.BlockSpec((tk, tn), lambda p0, p1, k: (k, ij(p0, p1)[1]))
    o_spec = pl.BlockSpec((tm, tn), lambda p0, p1, k: ij(p0, p1))
    in_specs = [a_spec, b_spec]
    args = [a, b]
    if epi in ("res", "drelu"):
        in_specs.append(o_spec)
        args.append(extra)
    if dep is not None:
        in_specs.append(pl.BlockSpec(memory_space=pl.ANY))
        args.append(dep)
    n_in = len(args)
    if epi == "relu2":
        out_shape = (jax.ShapeDtypeStruct((M, N), out_dtype), jax.ShapeDtypeStruct((M, N), BF16))
        out_specs = (o_spec, o_spec)
    else:
        out_shape = jax.ShapeDtypeStruct((M, N), out_dtype)
        out_specs = o_spec

    def kern(*refs):
        a_ref, b_ref = refs[0], refs[1]
        e_ref = refs[2] if epi in ("res", "drelu") else None
        acc = refs[-1]
        outs = refs[n_in:-1] if nk > 1 else refs[n_in:]
        k = pl.program_id(2)

        def product():
            return lax.dot_general(a_ref[...].astype(BF16), b_ref[...].astype(BF16), dims, preferred_element_type=F32)

        def finish(r):
            if epi is None:
                outs[0][...] = r.astype(out_dtype)
            elif epi == "res":
                outs[0][...] = (r + e_ref[...]).astype(out_dtype)
            elif epi == "relu2":
                outs[0][...] = r.astype(out_dtype)
                t = jnp.maximum(r, 0.0)
                outs[1][...] = (t * t).astype(BF16)
            else:
                outs[0][...] = (r * (2.0 * jnp.maximum(e_ref[...].astype(F32), 0.0))).astype(out_dtype)

        if nk == 1:
            finish(product())
        else:
            @pl.when(k == 0)
            def _():
                acc[...] = jnp.zeros_like(acc)

            acc[...] += product()

            @pl.when(k == nk - 1)
            def _():
                finish(acc[...])

    grid = (nn, nm, nk) if n_outer else (nm, nn, nk)
    return pl.pallas_call(
        kern, grid=grid, in_specs=in_specs, out_specs=out_specs, out_shape=out_shape,
        scratch_shapes=[pltpu.VMEM((tm, tn), F32)] if nk > 1 else [], name=name,
        compiler_params=_cp(("parallel", "parallel", "arbitrary")),
    )(*args)


def _rms_fwd(x, w, name, tl=256, dep=None):
    L = x.shape[0]

    def kern(x_ref, w_ref, *rest):
        o_ref = rest[-1]
        xv = x_ref[...]
        r = lax.rsqrt(jnp.mean(xv * xv, axis=-1, keepdims=True) + EPS)
        o_ref[...] = ((xv * r) * w_ref[...]).astype(BF16)

    row = pl.BlockSpec((tl, D), lambda i: (i, 0))
    deps = [] if dep is None else [dep]
    return pl.pallas_call(
        kern, grid=(L // tl,),
        in_specs=[row, pl.BlockSpec((1, D), lambda i: (0, 0))] + [pl.BlockSpec(memory_space=pl.ANY)] * len(deps),
        out_specs=row, out_shape=jax.ShapeDtypeStruct((L, D), BF16), name=name, compiler_params=_cp(("parallel",)),
    )(x, w.reshape(1, D), *deps)


def _rms_bwd(dy, x, w, res, name, tl=256, dep=None):
    L = x.shape[0]
    deps = [] if dep is None else [dep]

    def kern(dy_ref, x_ref, w_ref, res_ref, *rest):
        dx_ref, gw_ref = rest[-2:]
        @pl.when(pl.program_id(0) == 0)
        def _():
            gw_ref[...] = jnp.zeros_like(gw_ref)

        xv = x_ref[...]
        dyv = dy_ref[...]
        r = lax.rsqrt(jnp.mean(xv * xv, axis=-1, keepdims=True) + EPS)
        xn = xv * r
        gw_ref[...] += jnp.broadcast_to(jnp.sum(dyv * xn, axis=0, keepdims=True), (8, D))
        dxn = dyv * w_ref[...]
        dx_ref[...] = res_ref[...] + r * (dxn - xn * jnp.mean(dxn * xn, axis=-1, keepdims=True))

    row = pl.BlockSpec((tl, D), lambda i: (i, 0))
    return pl.pallas_call(
        kern, grid=(L // tl,),
        in_specs=[row, row, pl.BlockSpec((1, D), lambda i: (0, 0)), row] + [pl.BlockSpec(memory_space=pl.ANY)] * len(deps),
        out_specs=(row, pl.BlockSpec((8, D), lambda i: (0, 0))),
        out_shape=(jax.ShapeDtypeStruct((L, D), F32), jax.ShapeDtypeStruct((8, D), F32)),
        name=name, compiler_params=_cp(("arbitrary",)),
    )(dy, x, w.reshape(1, D), res, *deps)


def _final(x2, w, tgt, tl=256):
    L = x2.shape[0]

    def kern(x_ref, w_ref, t_ref, dx_ref, gw_ref, loss_ref):
        @pl.when(pl.program_id(0) == 0)
        def _():
            gw_ref[...] = jnp.zeros_like(gw_ref)
            loss_ref[...] = jnp.zeros_like(loss_ref)

        xv = x_ref[...]
        r = lax.rsqrt(jnp.mean(xv * xv, axis=-1, keepdims=True) + EPS)
        xn = xv * r
        e = xn * w_ref[...] - t_ref[...]
        per_tok = jnp.mean(e * e, axis=-1, keepdims=True)
        loss_ref[...] += 0.5 * jnp.sum(per_tok)
        dyv = e * (1.0 / D)
        gw_ref[...] += jnp.broadcast_to(jnp.sum(dyv * xn, axis=0, keepdims=True), (8, D))
        dxn = dyv * w_ref[...]
        dx_ref[...] = r * (dxn - xn * jnp.mean(dxn * xn, axis=-1, keepdims=True))

    row = pl.BlockSpec((tl, D), lambda i: (i, 0))
    return pl.pallas_call(
        kern, grid=(L // tl,), in_specs=[row, pl.BlockSpec((1, D), lambda i: (0, 0)), row],
        out_specs=(row, pl.BlockSpec((8, D), lambda i: (0, 0)), pl.BlockSpec((8, 128), lambda i: (0, 0))),
        out_shape=(jax.ShapeDtypeStruct((L, D), F32), jax.ShapeDtypeStruct((8, D), F32),
                   jax.ShapeDtypeStruct((8, 128), F32)),
        name="final_norm_loss", compiler_params=_cp(("arbitrary",)),
    )(x2, w.reshape(1, D), tgt)


def _down(v, k):
    if k == 0:
        return v
    t = lax.broadcasted_iota(jnp.int32, v.shape, 0)
    return jnp.where(t >= k, pltpu.roll(v, k, axis=0), 0.0)


def _up(v, k):
    if k == 0:
        return v
    n = v.shape[0]
    t = lax.broadcasted_iota(jnp.int32, v.shape, 0)
    return jnp.where(t < n - k, pltpu.roll(v, n - k, axis=0), 0.0)


TW = 256


def _sc_fwd(proj, cw):
    L = proj.shape[0]
    nb = D // TW

    def kern(b_ref, c_ref, x_ref, w_ref, o_ref):
        u = c_ref[...].astype(F32) * x_ref[...].astype(F32)
        w = w_ref[...]
        cv = w[0:1] * _down(u, 2) + w[1:2] * _down(u, 1) + w[2:3] * u
        o_ref[...] = (b_ref[...].astype(F32) * cv).astype(BF16)

    col = lambda off: pl.BlockSpec((L, TW), lambda j: (0, off + j))
    return pl.pallas_call(
        kern, grid=(nb,), in_specs=[col(0), col(nb), col(2 * nb), pl.BlockSpec((8, TW), lambda j: (0, j))],
        out_specs=pl.BlockSpec((L, TW), lambda j: (0, j)), out_shape=jax.ShapeDtypeStruct((L, D), BF16),
        name="sc_fwd", compiler_params=_cp(("parallel",)),
    )(proj, proj, proj, cw)


def _sc_bwd(dya, proj, cw, dproj):
    L = proj.shape[0]
    nb = D // TW

    def kern(d_ref, b_ref, c_ref, x_ref, w_ref, _, dp_ref, gw_ref, keep):
        sec = pl.program_id(1)

        @pl.when(sec == 0)
        def _():
            cs, xs, dyv = c_ref[...].astype(F32), x_ref[...].astype(F32), d_ref[...]
            w = w_ref[...]
            u = cs * xs
            u1, u2 = _down(u, 1), _down(u, 2)
            cv = w[0:1] * u2 + w[1:2] * u1 + w[2:3] * u
            dcv = dyv * b_ref[...].astype(F32)
            du = w[2:3] * dcv + w[1:2] * _up(dcv, 1) + w[0:1] * _up(dcv, 2)
            g0 = jnp.sum(dcv * u2, axis=0, keepdims=True)
            g1 = jnp.sum(dcv * u1, axis=0, keepdims=True)
            g2 = jnp.sum(dcv * u, axis=0, keepdims=True)
            row = lax.broadcasted_iota(jnp.int32, (8, TW), 0)
            gw_ref[...] = jnp.where(row == 0, g0, jnp.where(row == 1, g1, jnp.where(row == 2, g2, 0.0)))
            dp_ref[...] = (dyv * cv).astype(BF16)
            keep[0] = (du * xs).astype(BF16)
            keep[1] = (du * cs).astype(BF16)

        @pl.when(sec > 0)
        def _():
            dp_ref[...] = keep[sec - 1]

    col = lambda off: pl.BlockSpec((L, TW), lambda j, s: (0, off + j))
    return pl.pallas_call(
        kern, grid=(nb, 3),
        in_specs=[col(0), col(0), col(nb), col(2 * nb), pl.BlockSpec((8, TW), lambda j, s: (0, j)),
                  pl.BlockSpec(memory_space=pl.ANY)],
        out_specs=(pl.BlockSpec((L, TW), lambda j, s: (0, s * nb + j)), pl.BlockSpec((8, TW), lambda j, s: (0, j))),
        out_shape=(jax.ShapeDtypeStruct(dproj.shape, BF16), jax.ShapeDtypeStruct((8, D), F32)),
        scratch_shapes=[pltpu.VMEM((2, L, TW), BF16)],
        input_output_aliases={5: 0}, name="sc_bwd", compiler_params=_cp(("parallel", "arbitrary")),
    )(dya, proj, proj, proj, cw, dproj)


def _ssm_conv_fwd(proj, cw4):
    L = proj.shape[0]
    off = C_XBC // TW

    def kern(r_ref, w_ref, o_ref):
        raw = r_ref[...].astype(F32)
        w = w_ref[...]
        c4 = w[0:1] * _down(raw, 3) + w[1:2] * _down(raw, 2) + w[2:3] * _down(raw, 1) + w[3:4] * raw + w[4:5]
        o_ref[...] = c4 * _sigmoid(c4)

    return pl.pallas_call(
        kern, grid=(XBC // TW,),
        in_specs=[pl.BlockSpec((L, TW), lambda j: (0, off + j)), pl.BlockSpec((8, TW), lambda j: (0, j))],
        out_specs=pl.BlockSpec((L, TW), lambda j: (0, j)), out_shape=jax.ShapeDtypeStruct((L, XBC), F32),
        name="ssm_conv_fwd", compiler_params=_cp(("parallel",)),
    )(proj, cw4)


def _ssm_conv_bwd(dx, proj, cw4, dproj, col0, name):
    L, width = dx.shape
    off_p = (C_XBC + col0) // TW
    off_w = col0 // TW

    def kern(d_ref, r_ref, w_ref, _, dp_ref, gw_ref):
        raw = r_ref[...].astype(F32)
        w = w_ref[...]
        r1, r2, r3 = _down(raw, 1), _down(raw, 2), _down(raw, 3)
        c4 = w[0:1] * r3 + w[1:2] * r2 + w[2:3] * r1 + w[3:4] * raw + w[4:5]
        sg = _sigmoid(c4)
        dc4 = d_ref[...] * (sg * (1.0 + c4 * (1.0 - sg)))
        draw = w[3:4] * dc4 + w[2:3] * _up(dc4, 1) + w[1:2] * _up(dc4, 2) + w[0:1] * _up(dc4, 3)
        dp_ref[...] = draw.astype(BF16)
        gs = [jnp.sum(dc4 * r3, axis=0, keepdims=True), jnp.sum(dc4 * r2, axis=0, keepdims=True),
              jnp.sum(dc4 * r1, axis=0, keepdims=True), jnp.sum(dc4 * raw, axis=0, keepdims=True),
              jnp.sum(dc4, axis=0, keepdims=True)]
        row = lax.broadcasted_iota(jnp.int32, (8, TW), 0)
        acc = jnp.zeros((8, TW), F32)
        for k, gk in enumerate(gs):
            acc = jnp.where(row == k, gk, acc)
        gw_ref[...] = acc

    return pl.pallas_call(
        kern, grid=(width // TW,),
        in_specs=[pl.BlockSpec((L, TW), lambda j: (0, j)), pl.BlockSpec((L, TW), lambda j: (0, off_p + j)),
                  pl.BlockSpec((8, TW), lambda j: (0, off_w + j)), pl.BlockSpec(memory_space=pl.ANY)],
        out_specs=(pl.BlockSpec((L, TW), lambda j: (0, off_p + j)), pl.BlockSpec((8, TW), lambda j: (0, j))),
        out_shape=(jax.ShapeDtypeStruct(dproj.shape, BF16), jax.ShapeDtypeStruct((8, width), F32)),
        input_output_aliases={3: 0}, name=name, compiler_params=_cp(("arbitrary",)),
    )(dx, proj, cw4, dproj)


def _split3(v):
    h1 = v.astype(BF16)
    r1 = v - h1.astype(F32)
    h2 = r1.astype(BF16)
    h3 = (r1 - h2.astype(F32)).astype(BF16)
    return h1, h2, h3


def _dot01(m01, v, dims=_DIMS["nn"], m_left=True, terms=3):
    out = None
    for part in _split3(v)[:terms]:
        ops = (m01, part) if m_left else (part, m01)
        t = lax.dot_general(ops[0], ops[1], dims, preferred_element_type=F32)
        out = t if out is None else out + t
    return out


def _bdot(a, b, mode="nn"):
    return lax.dot_general(a.astype(BF16), b.astype(BF16), _DIMS[mode], preferred_element_type=F32)


def _softplus(v):
    return jnp.maximum(v, 0.0) + jnp.log1p(jnp.exp(-jnp.abs(v)))


def _dt_prep(proj, vec):
    L = proj.shape[0]

    def kern(p_ref, v_ref, dt_ref, cs_ref, sg_ref):
        v = v_ref[...]
        pre = p_ref[:, 0:128] + v[0:1]
        dt = _softplus(pre)
        da = dt * (-jnp.exp(v[1:2]))
        ii = lax.broadcasted_iota(jnp.int32, (Q, Q), 0)
        jj = lax.broadcasted_iota(jnp.int32, (Q, Q), 1)
        ltri = (jj <= ii).astype(BF16)
        lane = lax.broadcasted_iota(jnp.int32, (Q, 128), 1)
        for val, ref in ((dt, dt_ref), (_dot01(ltri, da), cs_ref), (_sigmoid(pre), sg_ref)):
            for g in range(NG):
                moved = val if g == 0 else pltpu.roll(val, 128 - 4 * g, axis=1)
                ref[g] = jnp.where(lane < 4, moved, 0.0)

    blk = pl.BlockSpec((NG, Q, 128), lambda c: (0, c, 0))
    return pl.pallas_call(
        kern, grid=(L // Q,),
        in_specs=[pl.BlockSpec((Q, 256), lambda c: (c, 0)), pl.BlockSpec((8, 128), lambda c: (0, 0))],
        out_specs=(blk, blk, blk),
        out_shape=(jax.ShapeDtypeStruct((NG, L, 128), F32),) * 3,
        name="dt_prep", compiler_params=_cp(("parallel",)),
    )(proj, vec)


def _head_masks():
    lane = lax.broadcasted_iota(jnp.int32, (1, 4 * HD), 1)
    return [((lane >= HD * j) & (lane < HD * (j + 1))) for j in range(4)]


def _expand4(v4, masks):
    R = v4.shape[0]
    out = jnp.zeros((R, 4 * HD), F32)
    for j in range(4):
        out = jnp.where(masks[j], jnp.broadcast_to(v4[:, j:j + 1], (R, 4 * HD)), out)
    return out


def _decay_matrix(cs_col, tri):
    colb = jnp.broadcast_to(cs_col, (Q, Q))
    return jnp.exp(jnp.where(tri, colb - colb.T, -jnp.inf))


def _ssd_fwd(xbc, dt4, cs4, vecg):
    L = xbc.shape[0]
    nc = L // Q

    def kern(x_ref, b_ref, c_ref, dt_ref, cs_ref, v_ref, y_ref, s_ref, S):
        c = pl.program_id(1)

        @pl.when(c == 0)
        def _():
            S[...] = jnp.zeros_like(S)

        masks = _head_masks()
        ii = lax.broadcasted_iota(jnp.int32, (Q, Q), 0)
        jj = lax.broadcasted_iota(jnp.int32, (Q, Q), 1)
        tri = jj <= ii
        for gi in range(GPS):
            xs, ns = slice(256 * gi, 256 * (gi + 1)), slice(NS * gi, NS * (gi + 1))
            dt4v, cs4v = dt_ref[gi], cs_ref[gi]
            dt_b, cs_b = _expand4(dt4v, masks), _expand4(cs4v, masks)
            d_b = _expand4(v_ref[gi], masks)[1:2]
            cs_last = cs_b[Q - 1:Q, :]
            x4, bm, cm = x_ref[:, xs], b_ref[:, ns], c_ref[:, ns]
            xdt = x4 * dt_b
            gm = _bdot(cm, bm, "nt")
            s4 = S[gi]
            s_ref[gi, 0] = s4
            y = _bdot(cm, s4) * jnp.exp(cs_b) + d_b * x4
            m_all = jnp.concatenate([(gm * _decay_matrix(cs4v[:, j:j + 1], tri)).astype(BF16) for j in range(4)], axis=0)
            yd = _bdot(m_all, xdt)
            for j in range(4):
                y = y + jnp.where(masks[j], yd[Q * j:Q * (j + 1)], 0.0)
            y_ref[:, xs] = y
            S[gi] = jnp.exp(cs_last) * s4 + _bdot(bm, xdt * jnp.exp(cs_last - cs_b), "tn")

    sc = pl.BlockSpec((GPS, Q, 128), lambda g, c: (g, c, 0))
    bw = NS * GPS
    return pl.pallas_call(
        kern, grid=(NG // GPS, nc),
        in_specs=[pl.BlockSpec((Q, 256 * GPS), lambda g, c: (c, g)),
                  pl.BlockSpec((Q, bw), lambda g, c: (c, INNER // bw + g)),
                  pl.BlockSpec((Q, bw), lambda g, c: (c, (INNER + NG * NS) // bw + g)),
                  sc, sc, pl.BlockSpec((GPS, 8, 128), lambda g, c: (g, 0, 0))],
        out_specs=(pl.BlockSpec((Q, 256 * GPS), lambda g, c: (c, g)),
                   pl.BlockSpec((GPS, 1, NS, 256), lambda g, c: (g, c, 0, 0))),
        out_shape=(jax.ShapeDtypeStruct((L, INNER), F32), jax.ShapeDtypeStruct((NG, nc, NS, 256), F32)),
        scratch_shapes=[pltpu.VMEM((GPS, NS, 256), F32)], name="ssd_fwd",
        compiler_params=_cp(("parallel", "arbitrary")),
    )(xbc, xbc, xbc, dt4, cs4, vecg)


def _ssd_bwd(xbc, dt4, cs4, sg4, vecg, s_all, dy):
    L = xbc.shape[0]
    nc = L // Q

    def kern(x_ref, b_ref, c_ref, dt_ref, cs_ref, sg_ref, v_ref, s_ref, dy_ref,
             dx_ref, db_ref, dc_ref, ddt_ref, st_ref, dS):
        cc = pl.program_id(1)

        @pl.when(cc == 0)
        def _():
            dS[...] = jnp.zeros_like(dS)
            st_ref[...] = jnp.zeros_like(st_ref)

        masks = _head_masks()
        ii = lax.broadcasted_iota(jnp.int32, (Q, Q), 0)
        jj = lax.broadcasted_iota(jnp.int32, (Q, Q), 1)
        tri = jj <= ii
        utri = (jj >= ii).astype(BF16)
        hsel = ((lax.broadcasted_iota(jnp.int32, (4 * HD, 128), 0) // HD)
                == lax.broadcasted_iota(jnp.int32, (4 * HD, 128), 1)).astype(BF16)
        hrow = ((lax.broadcasted_iota(jnp.int32, (4 * Q, 128), 0) // Q)
                == lax.broadcasted_iota(jnp.int32, (4 * Q, 128), 1)).astype(BF16)
        ones_q = jnp.ones((Q, 128), BF16)
        lane128 = lax.broadcasted_iota(jnp.int32, (Q, 128), 1)

        for gi in range(GPS):
            xs, ns = slice(256 * gi, 256 * (gi + 1)), slice(NS * gi, NS * (gi + 1))
            dt4v, cs4v, sg4v = dt_ref[gi], cs_ref[gi], sg_ref[gi]
            dt_b, cs_b = _expand4(dt4v, masks), _expand4(cs4v, masks)
            vv = _expand4(v_ref[gi], masks)
            a_b = -jnp.exp(vv[0:1])
            d_b = vv[1:2]
            a4 = -jnp.exp(v_ref[gi][0:1, :])
            cs_last = cs_b[Q - 1:Q, :]
            ecs = jnp.exp(cs_b)
            decay = jnp.exp(cs_last - cs_b)
            elast = jnp.exp(cs_last)
            x4, bm, cm, dyv = x_ref[:, xs], b_ref[:, ns], c_ref[:, ns], dy_ref[:, xs]
            s4 = s_ref[gi, 0]
            dsn = dS[gi]
            xdt = x4 * dt_b
            gm = _bdot(cm, bm, "nt")
            dye = dyv * ecs
            yoff = ecs * _bdot(cm, s4)
            t4 = _bdot(bm, dsn) * decay
            lms, mhs = [], []
            for j in range(4):
                colb = jnp.broadcast_to(cs4v[:, j:j + 1], (Q, Q))
                lms.append(jnp.exp(jnp.where(tri, colb - colb.T, -jnp.inf)))
                mhs.append(gm * lms[j])
            m_all = jnp.concatenate([m.astype(BF16) for m in mhs], axis=0)
            dy_m = jnp.concatenate([jnp.where(masks[j], dyv, 0.0).astype(BF16) for j in range(4)], axis=0)
            dxdt = t4 + _bdot(m_all, dy_m, "tn")
            dm_all = _bdot(dy_m, xdt, "nt")
            dg = jnp.zeros((Q, Q), F32)
            for j in range(4):
                dg = dg + dm_all[Q * j:Q * (j + 1)] * lms[j]
            e_all = dm_all * jnp.concatenate(mhs, axis=0)
            rsum = _dot01(ones_q, e_all, m_left=False, terms=2)
            da4 = -_dot01(hrow, e_all, _DIMS["tn"], m_left=False, terms=2)
            for j in range(4):
                da4 = da4 + jnp.where(lane128 == j, rsum[Q * j:Q * (j + 1)], 0.0)
            xt = xdt * t4
            tail = jnp.sum(xt, axis=0, keepdims=True) + elast * jnp.sum(s4 * dsn, axis=0, keepdims=True)
            gd_raw = jnp.sum(dyv * x4, axis=0, keepdims=True)
            stacked = jnp.concatenate([dyv * yoff - xt, dxdt * x4, jnp.broadcast_to(tail, (8, 4 * HD)),
                                       jnp.broadcast_to(gd_raw, (8, 4 * HD))], axis=0)
            seg = _dot01(hsel, stacked, m_left=False, terms=2)
            dda4 = _dot01(utri, da4 + seg[0:Q], terms=2) + seg[2 * Q:2 * Q + 1]
            ddt_ref[gi] = (dda4 * a4 + seg[Q:2 * Q]) * sg4v
            ga = jnp.sum(dda4 * dt4v * a4, axis=0, keepdims=True)
            row = lax.broadcasted_iota(jnp.int32, (8, 128), 0)
            st_ref[gi] += jnp.where(row == 0, ga, jnp.where(row == 1, seg[2 * Q + 8:2 * Q + 9], 0.0))
            dx_ref[:, xs] = d_b * dyv + dxdt * dt_b
            dc_ref[:, ns] = _bdot(dg, bm) + _bdot(dye, s4, "nt")
            db_ref[:, ns] = _bdot(dg, cm, "tn") + _bdot(xdt * decay, dsn, "nt")
            dS[gi] = elast * dsn + _bdot(cm, dye, "tn")

    rv = lambda c: nc - 1 - c
    sc = pl.BlockSpec((GPS, Q, 128), lambda g, c: (g, rv(c), 0))
    bw = NS * GPS
    return pl.pallas_call(
        kern, grid=(NG // GPS, nc),
        in_specs=[pl.BlockSpec((Q, 256 * GPS), lambda g, c: (rv(c), g)),
                  pl.BlockSpec((Q, bw), lambda g, c: (rv(c), INNER // bw + g)),
                  pl.BlockSpec((Q, bw), lambda g, c: (rv(c), (INNER + NG * NS) // bw + g)),
                  sc, sc, sc, pl.BlockSpec((GPS, 8, 128), lambda g, c: (g, 0, 0)),
                  pl.BlockSpec((GPS, 1, NS, 256), lambda g, c: (g, rv(c), 0, 0)),
                  pl.BlockSpec((Q, 256 * GPS), lambda g, c: (rv(c), g))],
        out_specs=(pl.BlockSpec((Q, 256 * GPS), lambda g, c: (rv(c), g)),
                   pl.BlockSpec((Q, bw), lambda g, c: (rv(c), g)),
                   pl.BlockSpec((Q, bw), lambda g, c: (rv(c), g)),
                   pl.BlockSpec((GPS, Q, 128), lambda g, c: (g, rv(c), 0)),
                   pl.BlockSpec((GPS, 8, 128), lambda g, c: (g, 0, 0))),
        out_shape=(jax.ShapeDtypeStruct((L, INNER), F32), jax.ShapeDtypeStruct((L, NG * NS), F32),
                   jax.ShapeDtypeStruct((L, NG * NS), F32), jax.ShapeDtypeStruct((NG, L, 128), F32),
                   jax.ShapeDtypeStruct((NG, 8, 128), F32)),
        scratch_shapes=[pltpu.VMEM((GPS, NS, 256), F32)], name="ssd_bwd",
        compiler_params=_cp(("parallel", "arbitrary")),
    )(xbc, xbc, xbc, dt4, cs4, sg4, vecg, s_all, dy)


def _dt_bwd(ddt, dproj, tl=256):
    L = ddt.shape[1]

    def kern(d_ref, _, dp_ref, gs_ref):
        @pl.when(pl.program_id(0) == 0)
        def _():
            gs_ref[...] = jnp.zeros_like(gs_ref)

        d = d_ref[0]
        for g in range(1, NG):
            d = d + pltpu.roll(d_ref[g], 4 * g, axis=1)
        gs_ref[...] += jnp.broadcast_to(jnp.sum(d, axis=0, keepdims=True), (8, 128))
        dp_ref[...] = jnp.concatenate([d, jnp.zeros_like(d)], axis=1).astype(BF16)

    return pl.pallas_call(
        kern, grid=(L // tl,),
        in_specs=[pl.BlockSpec((NG, tl, 128), lambda i: (0, i, 0)), pl.BlockSpec(memory_space=pl.ANY)],
        out_specs=(pl.BlockSpec((tl, 256), lambda i: (i, C_DT // 256)), pl.BlockSpec((8, 128), lambda i: (0, 0))),
        out_shape=(jax.ShapeDtypeStruct(dproj.shape, BF16), jax.ShapeDtypeStruct((8, 128), F32)),
        input_output_aliases={1: 0}, name="dt_bwd", compiler_params=_cp(("arbitrary",)),
    )(ddt, dproj)


GW = INNER // NG


def _gnorm_fwd(y, proj, w, tl=256):
    L = y.shape[0]
    zoff = C_Z // 1024

    def kern(y_ref, z_ref, w_ref, o_ref):
        z = z_ref[...].astype(F32)
        yz = y_ref[...] * (z * _sigmoid(z))
        wv = w_ref[...]
        for k in range(1024 // GW):
            sl = slice(GW * k, GW * (k + 1))
            v = yz[:, sl]
            rg = lax.rsqrt(jnp.mean(v * v, axis=-1, keepdims=True) + EPS)
            o_ref[:, sl] = ((v * rg) * wv[:, sl]).astype(BF16)

    blk = pl.BlockSpec((tl, 1024), lambda i, j: (i, j))
    return pl.pallas_call(
        kern, grid=(L // tl, 2),
        in_specs=[blk, pl.BlockSpec((tl, 1024), lambda i, j: (i, zoff + j)), pl.BlockSpec((1, 1024), lambda i, j: (0, j))],
        out_specs=blk, out_shape=jax.ShapeDtypeStruct((L, INNER), BF16), name="gnorm_fwd",
        compiler_params=_cp(("parallel", "parallel")),
    )(y, proj, w.reshape(1, INNER))


def _gnorm_bwd(dyb, y, proj, w, dproj, tl=256):
    L = y.shape[0]
    zoff = C_Z // 1024

    def kern(d_ref, y_ref, z_ref, w_ref, _, dy_ref, dp_ref, gw_ref):
        @pl.when(pl.program_id(1) == 0)
        def _():
            gw_ref[...] = jnp.zeros_like(gw_ref)

        z = z_ref[...].astype(F32)
        sg = _sigmoid(z)
        sz = z * sg
        yv = y_ref[...]
        yz = yv * sz
        dv = d_ref[...]
        wv = w_ref[...]
        for k in range(1024 // GW):
            sl = slice(GW * k, GW * (k + 1))
            v = yz[:, sl]
            rg = lax.rsqrt(jnp.mean(v * v, axis=-1, keepdims=True) + EPS)
            vn = v * rg
            dk = dv[:, sl]
            gw_ref[:, sl] += jnp.broadcast_to(jnp.sum(dk * vn, axis=0, keepdims=True), (8, GW))
            dvn = dk * wv[:, sl]
            dyz = rg * (dvn - vn * jnp.mean(dvn * vn, axis=-1, keepdims=True))
            dy_ref[:, sl] = dyz * sz[:, sl]
            dp_ref[:, sl] = (dyz * yv[:, sl] * (sg[:, sl] * (1.0 + z[:, sl] * (1.0 - sg[:, sl])))).astype(BF16)

    blk = pl.BlockSpec((tl, 1024), lambda j, i: (i, j))
    zblk = pl.BlockSpec((tl, 1024), lambda j, i: (i, zoff + j))
    return pl.pallas_call(
        kern, grid=(2, L // tl),
        in_specs=[blk, blk, zblk, pl.BlockSpec((1, 1024), lambda j, i: (0, j)), pl.BlockSpec(memory_space=pl.ANY)],
        out_specs=(blk, zblk, pl.BlockSpec((8, 1024), lambda j, i: (0, j))),
        out_shape=(jax.ShapeDtypeStruct((L, INNER), F32), jax.ShapeDtypeStruct(dproj.shape, BF16),
                   jax.ShapeDtypeStruct((8, INNER), F32)),
        input_output_aliases={4: 1}, name="gnorm_bwd", compiler_params=_cp(("parallel", "arbitrary")),
    )(dyb, y, proj, w.reshape(1, INNER), dproj)


def _merge_fwd(proj, bg, br_a, br_b, tl=256):
    L = proj.shape[0]
    goff = C_GATE // 1024

    def kern(g1_ref, g2_ref, b1_ref, b2_ref, a_ref, b_ref, o_ref):
        g1 = _sigmoid(g1_ref[...].astype(F32) + b1_ref[...])
        g2 = _sigmoid(g2_ref[...].astype(F32) + b2_ref[...])
        o_ref[...] = (g1 * a_ref[...] + g2 * b_ref[...]).astype(BF16)

    row = pl.BlockSpec((tl, 1024), lambda i: (i, 0))
    bg2 = bg.reshape(1, 2 * D)
    return pl.pallas_call(
        kern, grid=(L // tl,),
        in_specs=[pl.BlockSpec((tl, 1024), lambda i: (i, goff)), pl.BlockSpec((tl, 1024), lambda i: (i, goff + 1)),
                  pl.BlockSpec((1, 1024), lambda i: (0, 0)), pl.BlockSpec((1, 1024), lambda i: (0, 1)), row, row],
        out_specs=row, out_shape=jax.ShapeDtypeStruct((L, D), BF16), name="merge_fwd",
        compiler_params=_cp(("parallel",)),
    )(proj, proj, bg2, bg2, br_a, br_b)


def _merge_bwd(dm, proj, bg, br_a, br_b, dproj, tl=256):
    L = proj.shape[0]
    goff = C_GATE // 1024

    def kern(dm_ref, g_ref, b_ref, a_ref, bb_ref, _, dbr_ref, dp_ref, gb_ref):
        j = pl.program_id(0)

        @pl.when(pl.program_id(1) == 0)
        def _():
            gb_ref[...] = jnp.zeros_like(gb_ref)

        g = _sigmoid(g_ref[...].astype(F32) + b_ref[...])
        br = jnp.where(j == 0, a_ref[...], bb_ref[...])
        dmv = dm_ref[...]
        dbr_ref[0] = (dmv * g).astype(BF16)
        dgate = dmv * br * g * (1.0 - g)
        gb_ref[...] += jnp.broadcast_to(jnp.sum(dgate, axis=0, keepdims=True), (8, 1024))
        dp_ref[...] = dgate.astype(BF16)

    row = pl.BlockSpec((tl, 1024), lambda j, i: (i, 0))
    gblk = pl.BlockSpec((tl, 1024), lambda j, i: (i, goff + j))
    return pl.pallas_call(
        kern, grid=(2, L // tl),
        in_specs=[row, gblk, pl.BlockSpec((1, 1024), lambda j, i: (0, j)), row, row, pl.BlockSpec(memory_space=pl.ANY)],
        out_specs=(pl.BlockSpec((1, tl, 1024), lambda j, i: (j, i, 0)), gblk, pl.BlockSpec((8, 1024), lambda j, i: (0, j))),
        out_shape=(jax.ShapeDtypeStruct((2, L, D), BF16), jax.ShapeDtypeStruct(dproj.shape, BF16),
                   jax.ShapeDtypeStruct((8, 2 * D), F32)),
        input_output_aliases={5: 1}, name="merge_bwd", compiler_params=_cp(("parallel", "arbitrary")),
    )(dm, proj, bg.reshape(1, 2 * D), br_a, br_b, dproj)


def _coords():
    return lax.axis_index("x"), lax.axis_index("y"), lax.axis_index("c")


def _other_chips(sk):
    xk, yk = sk // 2, sk % 2
    return [((1 - xk, yk), 2 * (1 - xk) + yk), ((xk, 1 - yk), 2 * xk + 1 - yk), ((1 - xk, 1 - yk), 2 * (1 - xk) + 1 - yk)]


def _rows(start, size):
    assert size % 128 == 0
    return pl.ds(pl.multiple_of(start, 128), size)


def _per_chip(fn):
    x, y, _ = _coords()
    s = 2 * x + y
    for sk in range(4):
        pl.when(s == sk)(functools.partial(fn, sk))


XTRA = PIECE - PMAIN


def _place(shard, full_shape, block, index_map, idx, name, blk0=0, nblk=None, dep=None, into=None):
    in_block = block[-2:]
    if nblk is None:
        nblk = shard.shape[0] // in_block[0]

    def kern(idx_ref, s_ref, *rest):
        o_ref = rest[-1]
        o_ref[...] = s_ref[...].astype(BF16).reshape(o_ref.shape)

    extra = ([dep] if dep is not None else []) + ([into] if into is not None else [])
    grid_spec = pltpu.PrefetchScalarGridSpec(
        num_scalar_prefetch=1, grid=(nblk,),
        in_specs=[pl.BlockSpec(in_block, lambda i, idx_ref: (blk0 + i, 0))] + [_ANY] * len(extra),
        out_specs=pl.BlockSpec(block, index_map))
    aliases = {1 + len(extra): 0} if into is not None else {}
    return pl.pallas_call(kern, grid_spec=grid_spec, out_shape=jax.ShapeDtypeStruct(full_shape, BF16), name=name,
                          input_output_aliases=aliases, compiler_params=_cp(("arbitrary",)))(idx, shard, *extra)


_SEM = pl.BlockSpec(memory_space=pltpu.SEMAPHORE)
_EFFECT = pltpu.SideEffectType.DATAFLOW_SIDE_EFFECTING


_ANY = pl.BlockSpec(memory_space=pl.ANY)


def _tie(v, dep, name):
    def body(v_ref, dep_ref, o_ref):
        del v_ref, dep_ref, o_ref

    return pl.pallas_call(body, out_shape=jax.ShapeDtypeStruct(v.shape, v.dtype), in_specs=[_ANY, _ANY],
                          out_specs=_ANY, input_output_aliases={0: 0}, name=name)(v, dep)


def _split_call(name, arrays, start=None, wait=None, wait_sems=None, after=None):
    keys = list(arrays)
    n = len(keys)
    n_start = start.n if start is not None else 0
    afters = [] if after is None else (list(after) if isinstance(after, (list, tuple)) else [after])

    def body(*refs):
        pos = n
        if wait is not None:
            wss, wrs = refs[pos], refs[pos + 1]
            pos += 2
        pos += len(afters)
        if start is not None:
            nss, nrs = refs[pos], refs[pos + 1]
            pos += 2
        R = dict(zip(keys, refs[pos:pos + n]))
        token = refs[pos + n]
        x, y, c = _coords()

        def desc(src, dst, dev, ss, rs, k):
            return pltpu.make_async_remote_copy(src_ref=src, dst_ref=dst, send_sem=ss.at[k], recv_sem=rs.at[k],
                                                device_id=dev, device_id_type=MESH)

        def run(sk):
            if wait is not None:
                for k, (snd, land) in enumerate(wait.copies(sk, R)):
                    if snd is not None:
                        desc(snd[0], snd[1], snd[2], wss, wrs, k).wait_send()
                    if land is not None:
                        desc(land, land, (x, y, c), wss, wrs, k).wait_recv()
            if start is not None:
                for k, (snd, land) in enumerate(start.copies(sk, R)):
                    if snd is not None:
                        desc(snd[0], snd[1], snd[2], nss, nrs, k).start()

        _per_chip(run)
        token[...] = jnp.zeros_like(token)

    hbm = pl.BlockSpec(memory_space=HBM)
    vals = [arrays[k] for k in keys]
    ins, in_specs = list(vals), [hbm] * n
    if wait is not None:
        ins += list(wait_sems)
        in_specs += [_SEM, _SEM]
    ins += afters
    in_specs += [pl.BlockSpec(memory_space=pl.ANY)] * len(afters)
    out_shape, out_specs = [], []
    if start is not None:
        out_shape += [pltpu.SemaphoreType.DMA((n_start,)), pltpu.SemaphoreType.DMA((n_start,))]
        out_specs += [_SEM, _SEM]
    first = len(out_shape)
    out_shape += [jax.ShapeDtypeStruct(v.shape, v.dtype) for v in vals] + [jax.ShapeDtypeStruct((8, 128), F32)]
    out_specs += [hbm] * n + [pl.BlockSpec(memory_space=pltpu.VMEM)]
    res = pl.pallas_call(
        body, out_shape=tuple(out_shape), in_specs=in_specs, out_specs=tuple(out_specs),
        input_output_aliases={i: first + i for i in range(n)}, name=name,
        compiler_params=pltpu.CompilerParams(has_side_effects=_EFFECT),
    )(*ins)
    sems = (res[0], res[1]) if start is not None else None
    return dict(zip(keys, res[first:first + n])), sems, res[-1]


class _Plan:
    def __init__(self, n, copies):
        self.n, self.copies = n, copies


_HM, _HX = PMAIN // 2, XTRA // 2
WAVE0 = 768
WAVES = ((0, WAVE0), (WAVE0, _HM - WAVE0))
_WIN = {
    "wq0": (True, "wct", lambda r, sc, hc: r.at[_rows(PMAIN * sc + _HM * hc + WAVES[0][0], WAVES[0][1]), :]),
    "wq1": (True, "wct", lambda r, sc, hc: r.at[_rows(PMAIN * sc + _HM * hc + WAVES[1][0], WAVES[1][1]), :]),
    "xt": (True, "xt", lambda r, sc, hc: r.at[sc, _rows(_HX * hc, _HX), :]),
    "w1": (True, "w1", lambda r, sc, hc: r.at[_rows(512 * hc, 512), pl.ds(1024 * sc, 1024)]),
    "w2": (True, "w2", lambda r, sc, hc: r.at[_rows(1024 * sc + 512 * hc, 512), :]),
    "wa": (True, "wa", lambda r, sc, hc: r.at[_rows(256 * sc + 128 * hc, 128), :]),
    "wb": (True, "wb", lambda r, sc, hc: r.at[_rows(512 * sc + 256 * hc, 256), :]),
    "wo": (True, "wo", lambda r, sc, hc: r.at[_rows(256 * sc + 128 * hc, 128), :]),
    "cw": (False, "cw", lambda r, sc, hc: r.at[sc]),
}


_PIECE_SRC = {
    "wq0": lambda p, hc: p.at[_rows(_HM * hc + WAVES[0][0], WAVES[0][1]), :],
    "wq1": lambda p, hc: p.at[_rows(_HM * hc + WAVES[1][0], WAVES[1][1]), :],
    "xt": lambda p, hc: p.at[_rows(PMAIN + _HX * hc, _HX), :],
}


def _ag_chips_plan(keys):
    def copies(sk, R):
        _, _, c = _coords()
        out = []
        for key in keys:
            _, arr, win = _WIN[key]
            for (px, py), ps in _other_chips(sk):
                dst = win(R[arr], sk, c)
                src = _PIECE_SRC[key](R["piece"], c) if key in _PIECE_SRC else dst
                out.append(((src, dst, (px, py, c)), win(R[arr], ps, c)))
        return out
    return _Plan(3 * len(keys), copies)


def _ag_sibling_plan(keys):
    keys = [k for k in keys if _WIN[k][0]]

    def copies(sk, R):
        x, y, c = _coords()
        out = []
        for key in keys:
            _, arr, win = _WIN[key]
            for _, ps in _other_chips(sk):
                w = win(R[arr], ps, c)
                out.append(((w, w, (x, y, 1 - c)), win(R[arr], ps, 1 - c)))
        return out
    return _Plan(3 * len(keys), copies)


def _in_proj_wave(h, wct, wave, proj=None, tm=2048):
    L = h.shape[0]
    tm = min(tm, L)
    off, size = WAVES[wave]
    start = lambda j: pl.multiple_of(_HM * j + off, 128)

    def kern(h_ref, w_ref, *rest):
        o_ref = rest[-1]
        o_ref[...] = lax.dot_general(h_ref[...], w_ref[...], _DIMS["nt"], preferred_element_type=F32).astype(BF16)

    in_specs = [pl.BlockSpec((tm, D), lambda j, i: (i, 0)),
                pl.BlockSpec((pl.Element(size), pl.Element(D)), lambda j, i: (start(j), 0))]
    args, aliases = [h, wct], {}
    if proj is not None:
        in_specs.append(pl.BlockSpec(memory_space=pl.ANY))
        args.append(proj)
        aliases = {2: 0}
    return pl.pallas_call(
        kern, grid=(8, L // tm), in_specs=in_specs,
        out_specs=pl.BlockSpec((pl.Element(tm), pl.Element(size)), lambda j, i: (i * tm, start(j))),
        out_shape=jax.ShapeDtypeStruct((L, NCW), BF16), input_output_aliases=aliases,
        name="in_proj_wave%d" % wave, compiler_params=_cp(("parallel", "parallel")),
    )(*args)


def _fix_wct(wct, xt):
    nb = PMAIN // XTRA

    def kern(w_ref, x_ref, o_ref):
        k = pl.program_id(0)
        xv = x_ref[0]
        o_ref[...] = jnp.where(k < 3, (w_ref[...].astype(F32) + xv.astype(F32)).astype(BF16), xv)

    blk = pl.BlockSpec((XTRA, D), lambda k: (nb * (k + 1), 0))
    rblk = pl.BlockSpec((XTRA, D), lambda k: (jnp.where(k < 3, nb * (k + 1), 0), 0))
    return pl.pallas_call(
        kern, grid=(4,), in_specs=[rblk, pl.BlockSpec((1, XTRA, D), lambda k: (k, 0, 0))], out_specs=blk,
        out_shape=jax.ShapeDtypeStruct(wct.shape, BF16), input_output_aliases={0: 0}, name="fix_wct",
        compiler_params=_cp(("arbitrary",)),
    )(wct, xt)


_HP = PIECE // 2
_GWIN = [
    lambda r, sc, hc: r.at[_rows(PMAIN * sc + _HP * hc, _HP), :],
    lambda r, sc, hc: r.at[_rows(512 * hc, 512), pl.ds(1024 * sc, 1024)],
    lambda r, sc, hc: r.at[_rows(1024 * sc + 512 * hc, 512), :],
    lambda r, sc, hc: r.at[_rows(256 * sc + 128 * hc, 128), :],
    lambda r, sc, hc: r.at[_rows(512 * sc + 256 * hc, 256), :],
    lambda r, sc, hc: r.at[_rows(256 * sc + 128 * hc, 128), :],
]
HALF_SHAPES = [(PIECE // 2, D), (512, 1024), (512, 1024), (128, 1024), (256, 1024), (128, 1024)]


def _rs_sibling_plan(ts):
    def copies(sk, R):
        x, y, c = _coords()
        out = []
        for t in ts:
            for sc in range(4):
                land = R["ra%d" % t].at[sc]
                out.append(((_GWIN[t](R["g%d" % t], sc, 1 - c), land, (x, y, 1 - c)), land))
        return out
    return _Plan(4 * len(ts), copies)


def _rs_chips_plan(ts):
    def copies(sk, R):
        _, _, c = _coords()
        out = []
        for t in ts:
            for j, ((px, py), ps) in enumerate(_other_chips(sk)):
                land = R["rb%d" % t].at[j]
                out.append(((R["hb%d" % t].at[ps], land, (px, py, c)), land))
        return out
    return _Plan(3 * len(ts), copies)


def _rs_share_plan(ts):
    def copies(sk, R):
        x, y, c = _coords()
        out = []
        for t in ts:
            rows = HALF_SHAPES[t][0]
            mine = R["f%d" % t].at[_rows(rows * c, rows), :]
            out.append(((mine, mine, (x, y, 1 - c)), R["f%d" % t].at[_rows(rows * (1 - c), rows), :]))
        return out
    return _Plan(len(ts), copies)


def _half_tiling(t):
    rows, cols = HALF_SHAPES[t]
    if t == 0:
        return (rows // 2, cols), 2, lambda i: (i, 0)
    return (rows, cols), 1, lambda i: (0, 0)


def _window_spec(t, blk):
    if t == 0:
        return pl.BlockSpec((pl.Element(blk[0]), pl.Element(blk[1])), lambda i, sc, idx_ref: (
            pl.multiple_of(PMAIN * sc + _HP * idx_ref[1] + blk[0] * i, 128), 0))
    if t == 1:
        return pl.BlockSpec(blk, lambda i, sc, idx_ref: (idx_ref[1], sc))
    return pl.BlockSpec(blk, lambda i, sc, idx_ref: (2 * sc + idx_ref[1], 0))


def _chip_sum(g, ra, t, idx, name):
    rows, cols = HALF_SHAPES[t]
    blk, nblk, inner = _half_tiling(t)

    def kern(idx_ref, g_ref, r_ref, hb_ref, hf_ref):
        v = g_ref[...].astype(F32) + r_ref[0].astype(F32)
        hb_ref[0] = v.astype(BF16)

        @pl.when(pl.program_id(1) == idx_ref[0])
        def _():
            hf_ref[...] = v

    omap = lambda i, sc, idx_ref: (sc,) + inner(i)
    grid_spec = pltpu.PrefetchScalarGridSpec(
        num_scalar_prefetch=1, grid=(nblk, 4),
        in_specs=[_window_spec(t, blk), pl.BlockSpec((1,) + blk, omap)],
        out_specs=(pl.BlockSpec((1,) + blk, omap), pl.BlockSpec(blk, lambda i, sc, idx_ref: inner(i))))
    return pl.pallas_call(
        kern, grid_spec=grid_spec,
        out_shape=(jax.ShapeDtypeStruct((4, rows, cols), BF16), jax.ShapeDtypeStruct((rows, cols), F32)),
        name=name, compiler_params=_cp(("parallel", "arbitrary")),
    )(idx, g, ra)


def _final_sum(hf, rb, t, idx, name):
    rows, cols = HALF_SHAPES[t]
    blk, nblk, inner = _half_tiling(t)
    nbr = rows // blk[0]

    def kern(idx_ref, h_ref, r_ref, o_ref):
        o_ref[...] = ((h_ref[...] + r_ref[0].astype(F32)) + r_ref[1].astype(F32)) + r_ref[2].astype(F32)

    def omap(i, idx_ref):
        r, cidx = inner(i)
        return nbr * idx_ref[1] + r, cidx

    grid_spec = pltpu.PrefetchScalarGridSpec(
        num_scalar_prefetch=1, grid=(nblk,),
        in_specs=[pl.BlockSpec(blk, lambda i, idx_ref: inner(i)),
                  pl.BlockSpec((3,) + blk, lambda i, idx_ref: (0,) + inner(i))],
        out_specs=pl.BlockSpec(blk, omap))
    return pl.pallas_call(
        kern, grid_spec=grid_spec, out_shape=jax.ShapeDtypeStruct((2 * rows, cols), F32),
        name=name, compiler_params=_cp(("parallel",)),
    )(idx, hf, rb)


class _ReduceScatter:
    def __init__(self, ts, grads, idx, tag):
        self.ts, self.idx, self.tag = ts, idx, tag
        arr = {}
        for t in ts:
            arr["g%d" % t] = grads[t]
            arr["ra%d" % t] = lax.empty((4,) + HALF_SHAPES[t], BF16)
        self.plan = _rs_sibling_plan(ts)
        self.arr, self.sems, self.token = _split_call("rs_sibling_start_" + tag, arr, start=self.plan)

    def chips(self, after):
        arr, _, _ = _split_call("rs_sibling_wait_" + self.tag, self.arr, wait=self.plan, wait_sems=self.sems, after=after)
        brr, self.hf = {}, {}
        for t in self.ts:
            hb, self.hf[t] = _chip_sum(arr["g%d" % t], arr["ra%d" % t], t, self.idx, "chip_sum_%d" % t)
            brr["hb%d" % t] = hb
            brr["rb%d" % t] = lax.empty((3,) + HALF_SHAPES[t], BF16)
        self.plan = _rs_chips_plan(self.ts)
        self.arr, self.sems, self.token = _split_call("rs_chips_start_" + self.tag, brr, start=self.plan)
        return self.token

    def share(self, after):
        brr, _, _ = _split_call("rs_chips_wait_" + self.tag, self.arr, wait=self.plan, wait_sems=self.sems, after=after)
        frr = {"f%d" % t: _final_sum(self.hf[t], brr["rb%d" % t], t, self.idx, "final_sum_%d" % t) for t in self.ts}
        self.plan = _rs_share_plan(self.ts)
        self.arr, self.sems, self.token = _split_call("rs_share_start_" + self.tag, frr, start=self.plan)
        return self.token

    def result(self, after):
        frr, _, _ = _split_call("rs_share_wait_" + self.tag, self.arr, wait=self.plan, wait_sems=self.sems, after=after)
        return {t: frr["f%d" % t] for t in self.ts}


def _all8_plan(key):
    def copies(sk, R):
        x, y, c = _coords()
        own = R[key].at[4 * x + 2 * y + c]
        out = []
        for k in range(1, 8):
            dev = ((1 - x) if (k >> 2) & 1 else x, (1 - y) if (k >> 1) & 1 else y, (1 - c) if k & 1 else c)
            out.append(((own, own, dev), R[key].at[4 * dev[0] + 2 * dev[1] + dev[2]]))
        return out
    return _Plan(7, copies)


def _small_all_gather(v):
    def body(v_ref, o_ref, send_sems, recv_sems, loc_sem):
        x, y, c = _coords()
        me = 4 * x + 2 * y + c
        lc = pltpu.make_async_copy(v_ref, o_ref.at[me], loc_sem)
        lc.start()
        cps = []
        for k in range(1, 8):
            fx, fy, fc = (k >> 2) & 1, (k >> 1) & 1, k & 1
            dev = ((1 - x) if fx else x, (1 - y) if fy else y, (1 - c) if fc else c)
            cp = pltpu.make_async_remote_copy(src_ref=v_ref, dst_ref=o_ref.at[me], send_sem=send_sems.at[k - 1],
                                              recv_sem=recv_sems.at[k - 1], device_id=dev, device_id_type=MESH)
            cp.start()
            cps.append((cp, 4 * dev[0] + 2 * dev[1] + dev[2]))
        for k, (cp, frm) in enumerate(cps):
            got = o_ref.at[frm]
            pltpu.make_async_remote_copy(src_ref=got, dst_ref=got, send_sem=send_sems.at[k], recv_sem=recv_sems.at[k],
                                         device_id=(x, y, c), device_id_type=MESH).wait_recv()
        for cp, _ in cps:
            cp.wait_send()
        lc.wait()

    hbm = pl.BlockSpec(memory_space=HBM)
    return pl.pallas_call(
        body, out_shape=jax.ShapeDtypeStruct((8,) + v.shape, F32), in_specs=[hbm], out_specs=hbm,
        scratch_shapes=[pltpu.SemaphoreType.DMA((7,)), pltpu.SemaphoreType.DMA((7,)), pltpu.SemaphoreType.DMA(())],
        name="small_all_gather", compiler_params=pltpu.CompilerParams(has_side_effects=True),
    )(v)


def _sum8(v, name="small_sum"):
    def kern(v_ref, o_ref):
        acc = v_ref[0]
        for k in range(1, 8):
            acc = acc + v_ref[k]
        o_ref[...] = acc

    return pl.pallas_call(kern, out_shape=jax.ShapeDtypeStruct(v.shape[1:], F32), name=name)(v)


def _adamw(w, g, m, v, name, tr=128, blk0=0, nblk=None, into=None, copy_g=False):
    R, C = w.shape
    tr = min(tr, R)
    if nblk is None:
        assert R % tr == 0 and blk0 == 0
        nblk = R // tr
    n_out = 4 if copy_g else 3

    def kern(*refs):
        w_ref, g_ref, m_ref, v_ref = refs[:4]
        d_ref, mo_ref, vo_ref = refs[-n_out:][:3]
        gv = g_ref[...]
        mn = ADAM_B1 * m_ref[...] + (1.0 - ADAM_B1) * gv
        vn = ADAM_B2 * v_ref[...] + (1.0 - ADAM_B2) * (gv * gv)
        m_hat = mn / (1.0 - ADAM_B1 ** ADAM_STEP)
        v_hat = vn / (1.0 - ADAM_B2 ** ADAM_STEP)
        d_ref[...] = -ADAM_LR * (m_hat / (jnp.sqrt(v_hat) + ADAM_EPS) + ADAM_WD * w_ref[...])
        mo_ref[...] = mn
        vo_ref[...] = vn
        if copy_g:
            refs[-1][...] = gv

    blk = pl.BlockSpec((tr, C), lambda i: (blk0 + i, 0))
    sd = jax.ShapeDtypeStruct((R, C), F32)
    in_specs, args, aliases = [blk] * 4, [w, g, m, v], {}
    if into is not None:
        in_specs += [pl.BlockSpec(memory_space=pl.ANY)] * 3
        args += list(into)
        aliases = {4: 0, 5: 1, 6: 2}
    return pl.pallas_call(kern, grid=(nblk,), in_specs=in_specs, out_specs=(blk,) * n_out, out_shape=(sd,) * n_out,
                          input_output_aliases=aliases, name=name, compiler_params=_cp(("parallel",)))(*args)


def _adamw_w_in(wt, gp, mt, vt, offs, name, r0, tr, nblk, views, into=None):
    el = lambda n: (pl.Element(n), pl.Element(D))
    own = pl.BlockSpec(el(tr), lambda i, o: (pl.multiple_of(r0 + tr * i, 8), 0))

    def view(k):
        return pl.BlockSpec(el(tr), lambda i, o: (pl.multiple_of(jnp.maximum(r0 + tr * i + o[k], 0), 8), 0))

    def kern(o_ref, w_ref, m_ref, v_ref, *refs):
        g_refs, (d_ref, mo_ref, vo_ref, go_ref) = refs[:len(views)], refs[-4:]
        gv = g_refs[0][...]
        if len(views) == 2:
            row = r0 + tr * pl.program_id(0) + lax.broadcasted_iota(jnp.int32, (tr, D), 0)
            gv = jnp.where(row < o_ref[2], gv, g_refs[1][...])
        mn = ADAM_B1 * m_ref[...] + (1.0 - ADAM_B1) * gv
        vn = ADAM_B2 * v_ref[...] + (1.0 - ADAM_B2) * (gv * gv)
        m_hat = mn / (1.0 - ADAM_B1 ** ADAM_STEP)
        v_hat = vn / (1.0 - ADAM_B2 ** ADAM_STEP)
        d_ref[...] = -ADAM_LR * (m_hat / (jnp.sqrt(v_hat) + ADAM_EPS) + ADAM_WD * w_ref[...])
        mo_ref[...] = mn
        vo_ref[...] = vn
        go_ref[...] = gv

    in_specs = [own, own, own] + [view(k) for k in views]
    args = [wt, mt, vt] + [gp] * len(views)
    aliases = {}
    if into is not None:
        in_specs += [pl.BlockSpec(memory_space=pl.ANY)] * 4
        args += list(into)
        aliases = {1 + len(args) - 4 + j: j for j in range(4)}
    grid_spec = pltpu.PrefetchScalarGridSpec(num_scalar_prefetch=1, grid=(nblk,), in_specs=in_specs,
                                             out_specs=(own,) * 4)
    sd = jax.ShapeDtypeStruct(wt.shape, F32)
    return pl.pallas_call(kern, grid_spec=grid_spec, out_shape=(sd,) * 4, input_output_aliases=aliases, name=name,
                          compiler_params=_cp(("parallel",)))(offs, *args)


def _to_piece(wt, s):
    z = lambda n: jnp.zeros((n, D), wt.dtype)
    pads = [functools.partial(lambda k, w: jnp.pad(w, ((8 * k, PIECE - W_SHARD - 8 * k), (0, 0))).astype(BF16), k)
            for k in range(3)]
    last = lambda w: jnp.concatenate([z(24), w[:744], w[776:], w[744:776], z(PIECE - 24 - W_SHARD)], axis=0).astype(BF16)
    return lax.switch(s, pads + [last], wt)


def _from_piece(p, s):
    cuts = [functools.partial(lambda k, q: q[8 * k:8 * k + W_SHARD], k) for k in range(3)]
    last = lambda q: jnp.concatenate([q[24:768], q[2816:2848], q[768:2816]], axis=0)
    return lax.switch(s, cuts + [last], p)


_SMALL = [("b_gate", 2048), ("ssm_conv_b", 4096), ("dt_bias", 32), ("A_log", 32), ("D_skip", 32),
          ("ssm_norm_w", 2048), ("norm_mlp", 1024), ("norm_final", 1024), ("sc_conv_w", 3072), ("ssm_conv_w", 16384),
          ("loss", 1)]


def _pack(vals, table, rows):
    parts = []
    for name, n in table:
        v = vals[name].reshape(-1).astype(F32)
        pad = (-n) % 128
        parts.append(jnp.pad(v, (0, pad)) if pad else v)
    flat = jnp.concatenate(parts)
    return jnp.pad(flat, (0, rows * 128 - flat.shape[0])).reshape(rows, 128)


def _unpack(arr, table):
    flat = arr.reshape(-1)
    out, off = {}, 0
    for name, n in table:
        out[name] = flat[off:off + n]
        off += n + ((-n) % 128)
    return out


def kernel(x, norm_mix, w_in, b_gate, sc_conv_w, ssm_conv_w, ssm_conv_b, dt_bias, A_log, D_skip, ssm_norm_w, w_branch_sc, w_branch_ssm, w_out, norm_mlp, w_mlp1, w_mlp2, norm_final, loss_target, m_norm_mix, m_w_in, m_b_gate, m_sc_conv_w, m_ssm_conv_w, m_ssm_conv_b, m_dt_bias, m_A_log, m_D_skip, m_ssm_norm_w, m_w_branch_sc, m_w_branch_ssm, m_w_out, m_norm_mlp, m_w_mlp1, m_w_mlp2, m_norm_final, v_norm_mix, v_w_in, v_b_gate, v_sc_conv_w, v_ssm_conv_w, v_ssm_conv_b, v_dt_bias, v_A_log, v_D_skip, v_ssm_norm_w, v_w_branch_sc, v_w_branch_ssm, v_w_out, v_norm_mlp, v_w_mlp1, v_w_mlp2, v_norm_final):
    L = x.shape[1]
    nc = L // Q
    xi, yi, ci = lax.axis_index("x"), lax.axis_index("y"), lax.axis_index("c")
    s = 2 * xi + yi
    idx = jnp.stack([s, ci]).astype(jnp.int32)
    x0 = x.reshape(L, D)
    tgt = loss_target.reshape(L, D)

    piece = _to_piece(w_in.T, s)
    nb = PMAIN // XTRA
    cws = jnp.zeros((8, 1280), F32)
    cws = cws.at[0:3, 0:256].set(sc_conv_w).at[0:4, 256:1280].set(ssm_conv_w)
    cw0 = lax.dynamic_update_slice(jnp.zeros((4, 8, 1280), F32), cws[None], (s, 0, 0))
    win_keys, win2_keys, mid_keys, end_keys = ["xt", "cw", "wq0"], ["wq1"], ["wa", "wb", "wo", "w1"], ["w2"]
    gw, sems_w, tok = _split_call(
        "ag_win_start", {"wct": lax.empty((NCW, D), BF16), "xt": lax.empty((4, XTRA, D), BF16), "cw": cw0, "piece": piece},
        start=_ag_chips_plan(win_keys))
    g2, sems_w2, tok = _split_call("ag_win2_start", {"wct": gw["wct"], "piece": gw["piece"]},
                                   start=_ag_chips_plan(win2_keys), after=tok)
    piece = g2["piece"]
    gw["wct"] = _place(piece, (NCW, D), (XTRA, D), lambda i, r: (nb * r[0] + i, 0), idx, "place_wct", nblk=nb,
                       dep=tok, into=g2["wct"])
    gw["xt"] = _place(piece, (4, XTRA, D), (1, XTRA, D), lambda i, r: (r[0], 0, 0), idx, "place_xt", blk0=nb, nblk=1,
                      dep=tok, into=gw["xt"])
    gw["piece"] = piece
    wa0 = _place(w_branch_sc, (D, D), (256, 1024), lambda i, r: (r[0], 0), idx, "place_wa", dep=tok)
    wb0 = _place(w_branch_ssm, (INNER, D), (512, 1024), lambda i, r: (r[0], 0), idx, "place_wb", dep=tok)
    wo0 = _place(w_out, (D, D), (256, 1024), lambda i, r: (r[0], 0), idx, "place_wo", dep=tok)
    w10 = _place(w_mlp1, (D, DFF), (256, 1024), lambda i, r: (i, r[0]), idx, "place_w1", dep=tok)
    gm, sems_m, tok = _split_call("ag_mid_start", {"wa": wa0, "wb": wb0, "wo": wo0, "w1": w10},
                                  start=_ag_chips_plan(mid_keys))
    w20 = _place(w_mlp2, (DFF, D), (256, 1024), lambda i, r: (4 * r[0] + i, 0), idx, "place_w2", dep=tok)
    ge, sems_e, tok = _split_call("ag_end_start", {"w2": w20}, start=_ag_chips_plan(end_keys))
    h = _rms_fwd(x0, norm_mix, "rms_mix", dep=tok)
    gw, sems_w, tok = _split_call("ag_win_pass", gw, wait=_ag_chips_plan(win_keys), wait_sems=sems_w,
                                  start=_ag_sibling_plan(win_keys), after=h)
    gw, _, _ = _split_call("ag_win_done", gw, wait=_ag_sibling_plan(win_keys), wait_sems=sems_w, after=tok)
    wc, cw_all = _fix_wct(gw["wct"], gw["xt"]), gw["cw"]
    sc_w_full = jnp.concatenate([cw_all[k, :, 0:256] for k in range(4)], axis=1)
    ssm_w_full = jnp.concatenate([cw_all[k, :, 256:1280] for k in range(4)], axis=1)
    cw4 = ssm_w_full.at[4].set(ssm_conv_b)
    vec = jnp.zeros((8, 128), F32).at[0, :NH].set(dt_bias).at[1, :NH].set(A_log)
    vecg = jnp.zeros((NG, 8, 128), F32).at[:, 0, :4].set(A_log.reshape(NG, 4)).at[:, 1, :4].set(D_skip.reshape(NG, 4))

    dtraw = _matmul(h, wc[C_DT:], "nt", F32, 512, 256, 1024, "in_proj_dt")
    proj = _in_proj_wave(h, wc, 0)
    g2, sems_w2, tok = _split_call("ag_win2_pass", {"wct": wc, "piece": gw["piece"]},
                                   wait=_ag_chips_plan(win2_keys), wait_sems=sems_w2,
                                   start=_ag_sibling_plan(win2_keys), after=[proj, dtraw])
    g2, _, _ = _split_call("ag_win2_done", g2, wait=_ag_sibling_plan(win2_keys), wait_sems=sems_w2, after=tok)
    wc = g2["wct"]
    proj = _in_proj_wave(h, wc, 1, proj=proj)
    ya = _sc_fwd(proj, sc_w_full)
    xbc = _ssm_conv_fwd(proj, cw4)
    dt4, cs4, sg4 = _dt_prep(dtraw, vec)
    gm, sems_m, tok = _split_call("ag_mid_pass", gm, wait=_ag_chips_plan(mid_keys), wait_sems=sems_m,
                                  start=_ag_sibling_plan(mid_keys), after=[xbc, ya, dt4])
    xbc = _tie(xbc, tok, "tie_xbc")
    y, s_all = _ssd_fwd(xbc, dt4, cs4, vecg)
    yb = _gnorm_fwd(y, proj, ssm_norm_w)
    gm, _, _ = _split_call("ag_mid_done", gm, wait=_ag_sibling_plan(mid_keys), wait_sems=sems_m, after=yb)
    wa, wb, wo, w1 = gm["wa"], gm["wb"], gm["wo"], gm["w1"]
    ge, sems_e, tok = _split_call("ag_end_pass", ge, wait=_ag_chips_plan(end_keys), wait_sems=sems_e,
                                  start=_ag_sibling_plan(end_keys), after=yb)
    br_a = _matmul(ya, wa, "nn", F32, 1024, 1024, 1024, "branch_sc", dep=tok)
    br_b = _matmul(yb, wb, "nn", F32, 1024, 1024, 2048, "branch_ssm")
    merged = _merge_fwd(proj, b_gate, br_a, br_b)
    x1 = _matmul(merged, wo, "nn", F32, 1024, 1024, 1024, "out_proj", epi="res", extra=x0)
    h2 = _rms_fwd(x1, norm_mlp, "rms_mlp")
    a1, rl = _matmul(h2, w1, "nn", BF16, 1024, 1024, 1024, "mlp1", epi="relu2", n_outer=True)
    ge, _, _ = _split_call("ag_end_done", ge, wait=_ag_sibling_plan(end_keys), wait_sems=sems_e, after=a1)
    w2 = ge["w2"]
    x2 = _matmul(rl, w2, "nn", F32, 512, 1024, 4096, "mlp2", epi="res", extra=x1)
    dx2, g_nf, loss8 = _final(x2, norm_final, tgt)

    da = _matmul(dx2, w2, "nt", BF16, 1024, 1024, 1024, "mlp2_dx", epi="drelu", extra=a1, n_outer=True)
    g_w2 = _matmul(rl, dx2, "tn", BF16, 1024, 1024, 2048, "mlp2_dw")
    g_w1 = _matmul(h2, da, "tn", BF16, 1024, 1024, 2048, "mlp1_dw")
    dh2 = _matmul(da, w1, "nt", F32, 512, 1024, 4096, "mlp1_dx")
    dx1, g_nmlp = _rms_bwd(dh2, x1, norm_mlp, dx2, "rms_mlp_bwd")
    dmerged = _matmul(dx1, wo, "nt", F32, 1024, 1024, 1024, "out_proj_dx")
    g_wo = _matmul(merged, dx1, "tn", BF16, 1024, 1024, 2048, "out_proj_dw")
    dproj = lax.empty((L, NCW), BF16)
    dbr, dproj, g_bg = _merge_bwd(dmerged, proj, b_gate, br_a, br_b, dproj)
    dya = _matmul(dbr[0], wa, "nt", F32, 1024, 1024, 1024, "branch_sc_dx")
    g_wa = _matmul(ya, dbr[0], "tn", BF16, 1024, 1024, 2048, "branch_sc_dw")
    dproj, g_scw = _sc_bwd(dya, proj, sc_w_full, dproj)
    dyb = _matmul(dbr[1], wb, "nt", F32, 1024, 1024, 1024, "branch_ssm_dx", n_outer=True)
    g_wb = _matmul(yb, dbr[1], "tn", BF16, 1024, 1024, 2048, "branch_ssm_dw")
    rs_a = _ReduceScatter([1, 2, 3, 4, 5], {1: g_w1, 2: g_w2, 3: g_wa, 4: g_wb, 5: g_wo}, idx, "a")
    dy, dproj, g_snw = _gnorm_bwd(_tie(dyb, rs_a.token, "tie_dyb"), y, proj, ssm_norm_w, dproj)
    tok = rs_a.chips(after=dy)
    dxs, dbm, dcm, ddt_g, st = _ssd_bwd(xbc, dt4, cs4, sg4, vecg, s_all, _tie(dy, tok, "tie_dy"))
    dproj, gx1 = _ssm_conv_bwd(dxs, proj, cw4, dproj, 0, "ssm_conv_bwd_x")
    dproj, gx2 = _ssm_conv_bwd(dbm, proj, cw4, dproj, INNER, "ssm_conv_bwd_b")
    dproj, gx3 = _ssm_conv_bwd(dcm, proj, cw4, dproj, INNER + NG * NS, "ssm_conv_bwd_c")
    g_cw4 = jnp.concatenate([gx1, gx2, gx3], axis=1)
    dproj, g_dtb = _dt_bwd(ddt_g, dproj)
    small = {"b_gate": g_bg[0], "ssm_conv_b": g_cw4[4], "dt_bias": g_dtb[0, :NH],
             "A_log": st[:, 0, :4], "D_skip": st[:, 1, :4], "ssm_norm_w": g_snw[0], "norm_mlp": g_nmlp[0],
             "norm_final": g_nf[0], "sc_conv_w": g_scw[0:3], "ssm_conv_w": g_cw4[0:4], "loss": loss8[0, 0:1]}
    me = 4 * xi + 2 * yi + ci
    sm8 = lax.dynamic_update_slice(jnp.zeros((8, SMALL_ROWS, 128), F32), _pack(small, _SMALL, SMALL_ROWS)[None], (me, 0, 0))
    sm_arr, sm_sems, tok = _split_call("small_start", {"sm": sm8}, start=_all8_plan("sm"))
    g_wc = _matmul(dproj, h, "tn", BF16, 1280, 1024, 2048, "in_proj_dw", dep=tok)
    rs_b = _ReduceScatter([0], {0: g_wc}, idx, "b")
    tok = rs_a.share(after=rs_b.token)
    tok = rs_b.chips(after=tok)
    dh = _matmul(dproj, wc, "nn", F32, 512, 1024, 5760, "in_proj_dx", dep=tok)
    grad_x, g_nm = _rms_bwd(dh, x0, norm_mix, dx1, "rms_mix_bwd")
    nm8 = lax.dynamic_update_slice(jnp.zeros((8, 8, 128), F32), g_nm[0].reshape(1, 8, 128), (me, 0, 0))
    nm_arr, nm_sems, tok = _split_call("norm_mix_start", {"nm": nm8}, start=_all8_plan("nm"))
    sm_arr, _, _ = _split_call("small_wait", sm_arr, wait=_all8_plan("sm"), wait_sems=sm_sems, after=tok)
    small_sum = _sum8(sm_arr["sm"])
    gs = _unpack(small_sum, _SMALL)
    red = rs_a.result(after=tok)
    big = {"w_mlp1": red[1], "w_mlp2": red[2], "w_branch_sc": red[3], "w_branch_ssm": red[4], "w_out": red[5]}

    given = dict(norm_mix=norm_mix, w_in=w_in, b_gate=b_gate, sc_conv_w=sc_conv_w, ssm_conv_w=ssm_conv_w, ssm_conv_b=ssm_conv_b, dt_bias=dt_bias, A_log=A_log, D_skip=D_skip, ssm_norm_w=ssm_norm_w, w_branch_sc=w_branch_sc, w_branch_ssm=w_branch_ssm, w_out=w_out, norm_mlp=norm_mlp, w_mlp1=w_mlp1, w_mlp2=w_mlp2, norm_final=norm_final,
                 m_norm_mix=m_norm_mix, m_w_in=m_w_in, m_b_gate=m_b_gate, m_sc_conv_w=m_sc_conv_w, m_ssm_conv_w=m_ssm_conv_w, m_ssm_conv_b=m_ssm_conv_b, m_dt_bias=m_dt_bias, m_A_log=m_A_log, m_D_skip=m_D_skip, m_ssm_norm_w=m_ssm_norm_w, m_w_branch_sc=m_w_branch_sc, m_w_branch_ssm=m_w_branch_ssm, m_w_out=m_w_out, m_norm_mlp=m_norm_mlp, m_w_mlp1=m_w_mlp1, m_w_mlp2=m_w_mlp2, m_norm_final=m_norm_final,
                 v_norm_mix=v_norm_mix, v_w_in=v_w_in, v_b_gate=v_b_gate, v_sc_conv_w=v_sc_conv_w, v_ssm_conv_w=v_ssm_conv_w, v_ssm_conv_b=v_ssm_conv_b, v_dt_bias=v_dt_bias, v_A_log=v_A_log, v_D_skip=v_D_skip, v_ssm_norm_w=v_ssm_norm_w, v_w_branch_sc=v_w_branch_sc, v_w_branch_ssm=v_w_branch_ssm, v_w_out=v_w_out, v_norm_mlp=v_norm_mlp, v_w_mlp1=v_w_mlp1, v_w_mlp2=v_w_mlp2, v_norm_final=v_norm_final)
    order = ["norm_mix", "w_in", "b_gate", "sc_conv_w", "ssm_conv_w", "ssm_conv_b", "dt_bias", "A_log", "D_skip",
             "ssm_norm_w", "w_branch_sc", "w_branch_ssm", "w_out", "norm_mlp", "w_mlp1", "w_mlp2", "norm_final"]
    grad, delta, new_m, new_v = {}, {}, {}, {}
    for n in big:
        delta[n], new_m[n], new_v[n], grad[n] = _adamw(given[n], big[n], given["m_" + n], given["v_" + n],
                                                       "adamw_" + n, copy_g=True)
    big["w_in"] = None
    grad_small = {n: gs[n].reshape(given[n].shape) for n in order
                  if n not in big and n not in ("sc_conv_w", "ssm_conv_w", "norm_mix")}
    grad_small["sc_conv_w"] = lax.dynamic_slice(gs["sc_conv_w"].reshape(3, D), (0, 256 * s), (3, 256))
    grad_small["ssm_conv_w"] = lax.dynamic_slice(gs["ssm_conv_w"].reshape(4, XBC), (0, 1024 * s), (4, 1024))
    table = [(n, int(grad_small[n].size)) for n in grad_small]
    rows = 136
    pk = lambda d: _pack(d, table, rows)
    ds_, ms_, vs_ = _adamw(pk({n: given[n] for n in grad_small}), pk(grad_small), pk({n: given["m_" + n] for n in grad_small}),
                           pk({n: given["v_" + n] for n in grad_small}), "adamw_small", tr=rows)
    ds_, ms_, vs_ = _unpack(ds_, table), _unpack(ms_, table), _unpack(vs_, table)
    for n in grad_small:
        shp = given[n].shape
        grad[n] = grad_small[n]
        delta[n], new_m[n], new_v[n] = ds_[n].reshape(shp), ms_[n].reshape(shp), vs_[n].reshape(shp)

    done = [new_v[n] for n in ("w_mlp1", "w_mlp2", "w_branch_sc", "w_branch_ssm", "w_out")] + [vs_["b_gate"]]
    tok = rs_b.share(after=done)
    gp = rs_b.result(after=tok)[0]
    offs = jnp.where(s == 3, jnp.array([24, -8, 744, 2072, -8], jnp.int32),
                     jnp.stack([8 * s, 8 * s, 0 * s, 8 * s, 8 * s]).astype(jnp.int32))
    wt_args = (w_in.T, gp, m_w_in.T, v_w_in.T, offs)
    nmain = W_SHARD // 256
    res = _adamw_w_in(*wt_args, "adamw_w_in", 0, 256, nmain, (0, 1))
    res = _adamw_w_in(*wt_args, "adamw_w_in_dt", 744, 32, 1, (3,), into=res)
    dt_, mt_, vt_, gwt = _adamw_w_in(*wt_args, "adamw_w_in_tail", 256 * nmain, 8, 1, (4,), into=res)
    grad["w_in"], delta["w_in"], new_m["w_in"], new_v["w_in"] = gwt.T, dt_.T, mt_.T, vt_.T
    nm_arr, _, _ = _split_call("norm_mix_wait", nm_arr, wait=_all8_plan("nm"), wait_sems=nm_sems, after=tok)
    g8 = _sum8(nm_arr["nm"], "norm_mix_sum")
    r8 = lambda a: a.reshape(8, 128)
    d8, m8, v8 = _adamw(r8(norm_mix), g8, r8(m_norm_mix), r8(v_norm_mix), "adamw_norm_mix", tr=8)
    grad["norm_mix"], delta["norm_mix"] = g8.reshape(D), d8.reshape(D)
    new_m["norm_mix"], new_v["norm_mix"] = m8.reshape(D), v8.reshape(D)

    loss = gs["loss"].reshape(())
    return (loss, grad_x.reshape(1, L, D), *[grad[n] for n in order], *[delta[n] for n in order],
            *[new_m[n] for n in order], *[new_v[n] for n in order])
```

```python
import functools

import jax
import jax.numpy as jnp
from jax import lax
from jax.experimental import pallas as pl
from jax.experimental.pallas import tpu as pltpu

F32 = jnp.float32
BF16 = jnp.bfloat16
MESH = pl.DeviceIdType.MESH
HBM = pltpu.HBM

D = 1024
INNER = 2048
HD = 64
NH = 32
NG = 8
NS = 128
Q = 128
GPS = 4
XBC = 4096
DFF = 4096
EPS = 1e-6
W_SHARD = 2824
NCW = 11520
PIECE = 3072
PMAIN = 2816
C_Z, C_XBC, C_GATE, C_DT = 3072, 5120, 9216, 11264
SMALL_ROWS = 256
VMEM_LIMIT = 56 * 1024 * 1024

ADAM_LR, ADAM_B1, ADAM_B2, ADAM_EPS, ADAM_WD, ADAM_STEP = 0.001, 0.9, 0.999, 1e-08, 0.01, 10


def _cp(sem=None, vmem=VMEM_LIMIT):
    return pltpu.CompilerParams(dimension_semantics=sem, vmem_limit_bytes=vmem)


def _sigmoid(v):
    return 1.0 / (1.0 + jnp.exp(-v))


_DIMS = {"nn": (((1,), (0,)), ((), ())), "nt": (((1,), (1,)), ((), ())), "tn": (((0,), (0,)), ((), ()))}


def _matmul(a, b, mode, out_dtype, tm, tn, tk, name, epi=None, extra=None, n_outer=False, dep=None):
    if mode == "tn":
        K, M = a.shape
    else:
        M, K = a.shape
    N = b.shape[0] if mode == "nt" else b.shape[1]
    tm, tn, tk = min(tm, M), min(tn, N), min(tk, K)
    assert M % tm == 0 and N % tn == 0 and K % tk == 0, (name, M, N, K, tm, tn, tk)
    nm, nn, nk = M // tm, N // tn, K // tk
    dims = _DIMS[mode]

    def ij(p0, p1):
        return (p1, p0) if n_outer else (p0, p1)

    if mode == "tn":
        a_spec = pl.BlockSpec((tk, tm), lambda p0, p1, k: (k, ij(p0, p1)[0]))
    else:
        a_spec = pl.BlockSpec((tm, tk), lambda p0, p1, k: (ij(p0, p1)[0], k))
    if mode == "nt":
        b_spec = pl.BlockSpec((tn, tk), lambda p0, p1, k: (ij(p0, p1)[1], k))
    else:
        b_spec = pl.BlockSpec((tk, tn), lambda p0, p1, k: (k, ij(p0, p1)[1]))
    o_spec = pl.BlockSpec((tm, tn), lambda p0, p1, k: ij(p0, p1))
    in_specs = [a_spec, b_spec]
    args = [a, b]
    if epi in ("res", "drelu"):
        in_specs.append(o_spec)
        args.append(extra)
    if dep is not None:
        in_specs.append(pl.BlockSpec(memory_space=pl.ANY))
        args.append(dep)
    n_in = len(args)
    if epi == "relu2":
        out_shape = (jax.ShapeDtypeStruct((M, N), out_dtype), jax.ShapeDtypeStruct((M, N), BF16))
        out_specs = (o_spec, o_spec)
    else:
        out_shape = jax.ShapeDtypeStruct((M, N), out_dtype)
        out_specs = o_spec

    def kern(*refs):
        a_ref, b_ref = refs[0], refs[1]
        e_ref = refs[2] if epi in ("res", "drelu") else None
        acc = refs[-1]
        outs = refs[n_in:-1] if nk > 1 else refs[n_in:]
        k = pl.program_id(2)

        def product():
            return lax.dot_general(a_ref[...].astype(BF16), b_ref[...].astype(BF16), dims, preferred_element_type=F32)

        def finish(r):
            if epi is None:
                outs[0][...] = r.astype(out_dtype)
            elif epi == "res":
                outs[0][...] = (r + e_ref[...]).astype(out_dtype)
            elif epi == "relu2":
                outs[0][...] = r.astype(out_dtype)
                t = jnp.maximum(r, 0.0)
                outs[1][...] = (t * t).astype(BF16)
            else:
                outs[0][...] = (r * (2.0 * jnp.maximum(e_ref[...].astype(F32), 0.0))).astype(out_dtype)

        if nk == 1:
            finish(product())
        else:
            @pl.when(k == 0)
            def _():
                acc[...] = jnp.zeros_like(acc)

            acc[...] += product()

            @pl.when(k == nk - 1)
            def _():
                finish(acc[...])

    grid = (nn, nm, nk) if n_outer else (nm, nn, nk)
    return pl.pallas_call(
        kern, grid=grid, in_specs=in_specs, out_specs=out_specs, out_shape=out_shape,
        scratch_shapes=[pltpu.VMEM((tm, tn), F32)] if nk > 1 else [], name=name,
        compiler_params=_cp(("parallel", "parallel", "arbitrary")),
    )(*args)


def _rms_fwd(x, w, name, tl=256, dep=None):
    L = x.shape[0]

    def kern(x_ref, w_ref, *rest):
        o_ref = rest[-1]
        xv = x_ref[...]
        r = lax.rsqrt(jnp.mean(xv * xv, axis=-1, keepdims=True) + EPS)
        o_ref[...] = ((xv * r) * w_ref[...]).astype(BF16)

    row = pl.BlockSpec((tl, D), lambda i: (i, 0))
    deps = [] if dep is None else [dep]
    return pl.pallas_call(
        kern, grid=(L // tl,),
        in_specs=[row, pl.BlockSpec((1, D), lambda i: (0, 0))] + [pl.BlockSpec(memory_space=pl.ANY)] * len(deps),
        out_specs=row, out_shape=jax.ShapeDtypeStruct((L, D), BF16), name=name, compiler_params=_cp(("parallel",)),
    )(x, w.reshape(1, D), *deps)


def _rms_bwd(dy, x, w, res, name, tl=256, dep=None):
    L = x.shape[0]
    deps = [] if dep is None else [dep]

    def kern(dy_ref, x_ref, w_ref, res_ref, *rest):
        dx_ref, gw_ref = rest[-2:]
        @pl.when(pl.program_id(0) == 0)
        def _():
            gw_ref[...] = jnp.zeros_like(gw_ref)

        xv = x_ref[...]
        dyv = dy_ref[...]
        r = lax.rsqrt(jnp.mean(xv * xv, axis=-1, keepdims=True) + EPS)
        xn = xv * r
        gw_ref[...] += jnp.broadcast_to(jnp.sum(dyv * xn, axis=0, keepdims=True), (8, D))
        dxn = dyv * w_ref[...]
        dx_ref[...] = res_ref[...] + r * (dxn - xn * jnp.mean(dxn * xn, axis=-1, keepdims=True))

    row = pl.BlockSpec((tl, D), lambda i: (i, 0))
    return pl.pallas_call(
        kern, grid=(L // tl,),
        in_specs=[row, row, pl.BlockSpec((1, D), lambda i: (0, 0)), row] + [pl.BlockSpec(memory_space=pl.ANY)] * len(deps),
        out_specs=(row, pl.BlockSpec((8, D), lambda i: (0, 0))),
        out_shape=(jax.ShapeDtypeStruct((L, D), F32), jax.ShapeDtypeStruct((8, D), F32)),
        name=name, compiler_params=_cp(("arbitrary",)),
    )(dy, x, w.reshape(1, D), res, *deps)


def _final(x2, w, tgt, tl=256):
    L = x2.shape[0]

    def kern(x_ref, w_ref, t_ref, dx_ref, gw_ref, loss_ref):
        @pl.when(pl.program_id(0) == 0)
        def _():
            gw_ref[...] = jnp.zeros_like(gw_ref)
            loss_ref[...] = jnp.zeros_like(loss_ref)

        xv = x_ref[...]
        r = lax.rsqrt(jnp.mean(xv * xv, axis=-1, keepdims=True) + EPS)
        xn = xv * r
        e = xn * w_ref[...] - t_ref[...]
        per_tok = jnp.mean(e * e, axis=-1, keepdims=True)
        loss_ref[...] += 0.5 * jnp.sum(per_tok)
        dyv = e * (1.0 / D)
        gw_ref[...] += jnp.broadcast_to(jnp.sum(dyv * xn, axis=0, keepdims=True), (8, D))
        dxn = dyv * w_ref[...]
        dx_ref[...] = r * (dxn - xn * jnp.mean(dxn * xn, axis=-1, keepdims=True))

    row = pl.BlockSpec((tl, D), lambda i: (i, 0))
    return pl.pallas_call(
        kern, grid=(L // tl,), in_specs=[row, pl.BlockSpec((1, D), lambda i: (0, 0)), row],
        out_specs=(row, pl.BlockSpec((8, D), lambda i: (0, 0)), pl.BlockSpec((8, 128), lambda i: (0, 0))),
        out_shape=(jax.ShapeDtypeStruct((L, D), F32), jax.ShapeDtypeStruct((8, D), F32),
                   jax.ShapeDtypeStruct((8, 128), F32)),
        name="final_norm_loss", compiler_params=_cp(("arbitrary",)),
    )(x2, w.reshape(1, D), tgt)


def _down(v, k):
    if k == 0:
        return v
    t = lax.broadcasted_iota(jnp.int32, v.shape, 0)
    return jnp.where(t >= k, pltpu.roll(v, k, axis=0), 0.0)


def _up(v, k):
    if k == 0:
        return v
    n = v.shape[0]
    t = lax.broadcasted_iota(jnp.int32, v.shape, 0)
    return jnp.where(t < n - k, pltpu.roll(v, n - k, axis=0), 0.0)


TW = 256


def _sc_fwd(proj, cw):
    L = proj.shape[0]
    nb = D // TW

    def kern(b_ref, c_ref, x_ref, w_ref, o_ref):
        u = c_ref[...].astype(F32) * x_ref[...].astype(F32)
        w = w_ref[...]
        cv = w[0:1] * _down(u, 2) + w[1:2] * _down(u, 1) + w[2:3] * u
        o_ref[...] = (b_ref[...].astype(F32) * cv).astype(BF16)

    col = lambda off: pl.BlockSpec((L, TW), lambda j: (0, off + j))
    return pl.pallas_call(
        kern, grid=(nb,), in_specs=[col(0), col(nb), col(2 * nb), pl.BlockSpec((8, TW), lambda j: (0, j))],
        out_specs=pl.BlockSpec((L, TW), lambda j: (0, j)), out_shape=jax.ShapeDtypeStruct((L, D), BF16),
        name="sc_fwd", compiler_params=_cp(("parallel",)),
    )(proj, proj, proj, cw)


def _sc_bwd(dya, proj, cw, dproj):
    L = proj.shape[0]
    nb = D // TW

    def kern(d_ref, b_ref, c_ref, x_ref, w_ref, _, dp_ref, gw_ref, keep):
        sec = pl.program_id(1)

        @pl.when(sec == 0)
        def _():
            cs, xs, dyv = c_ref[...].astype(F32), x_ref[...].astype(F32), d_ref[...]
            w = w_ref[...]
            u = cs * xs
            u1, u2 = _down(u, 1), _down(u, 2)
            cv = w[0:1] * u2 + w[1:2] * u1 + w[2:3] * u
            dcv = dyv * b_ref[...].astype(F32)
            du = w[2:3] * dcv + w[1:2] * _up(dcv, 1) + w[0:1] * _up(dcv, 2)
            g0 = jnp.sum(dcv * u2, axis=0, keepdims=True)
            g1 = jnp.sum(dcv * u1, axis=0, keepdims=True)
            g2 = jnp.sum(dcv * u, axis=0, keepdims=True)
            row = lax.broadcasted_iota(jnp.int32, (8, TW), 0)
            gw_ref[...] = jnp.where(row == 0, g0, jnp.where(row == 1, g1, jnp.where(row == 2, g2, 0.0)))
            dp_ref[...] = (dyv * cv).astype(BF16)
            keep[0] = (du * xs).astype(BF16)
            keep[1] = (du * cs).astype(BF16)

        @pl.when(sec > 0)
        def _():
            dp_ref[...] = keep[sec - 1]

    col = lambda off: pl.BlockSpec((L, TW), lambda j, s: (0, off + j))
    return pl.pallas_call(
        kern, grid=(nb, 3),
        in_specs=[col(0), col(0), col(nb), col(2 * nb), pl.BlockSpec((8, TW), lambda j, s: (0, j)),
                  pl.BlockSpec(memory_space=pl.ANY)],
        out_specs=(pl.BlockSpec((L, TW), lambda j, s: (0, s * nb + j)), pl.BlockSpec((8, TW), lambda j, s: (0, j))),
        out_shape=(jax.ShapeDtypeStruct(dproj.shape, BF16), jax.ShapeDtypeStruct((8, D), F32)),
        scratch_shapes=[pltpu.VMEM((2, L, TW), BF16)],
        input_output_aliases={5: 0}, name="sc_bwd", compiler_params=_cp(("parallel", "arbitrary")),
    )(dya, proj, proj, proj, cw, dproj)


def _ssm_conv_fwd(proj, cw4):
    L = proj.shape[0]
    off = C_XBC // TW

    def kern(r_ref, w_ref, o_ref):
        raw = r_ref[...].astype(F32)
        w = w_ref[...]
        c4 = w[0:1] * _down(raw, 3) + w[1:2] * _down(raw, 2) + w[2:3] * _down(raw, 1) + w[3:4] * raw + w[4:5]
        o_ref[...] = c4 * _sigmoid(c4)

    return pl.pallas_call(
        kern, grid=(XBC // TW,),
        in_specs=[pl.BlockSpec((L, TW), lambda j: (0, off + j)), pl.BlockSpec((8, TW), lambda j: (0, j))],
        out_specs=pl.BlockSpec((L, TW), lambda j: (0, j)), out_shape=jax.ShapeDtypeStruct((L, XBC), F32),
        name="ssm_conv_fwd", compiler_params=_cp(("parallel",)),
    )(proj, cw4)


def _ssm_conv_bwd(dx, proj, cw4, dproj, col0, name):
    L, width = dx.shape
    off_p = (C_XBC + col0) // TW
    off_w = col0 // TW

    def kern(d_ref, r_ref, w_ref, _, dp_ref, gw_ref):
        raw = r_ref[...].astype(F32)
        w = w_ref[...]
        r1, r2, r3 = _down(raw, 1), _down(raw, 2), _down(raw, 3)
        c4 = w[0:1] * r3 + w[1:2] * r2 + w[2:3] * r1 + w[3:4] * raw + w[4:5]
        sg = _sigmoid(c4)
        dc4 = d_ref[...] * (sg * (1.0 + c4 * (1.0 - sg)))
        draw = w[3:4] * dc4 + w[2:3] * _up(dc4, 1) + w[1:2] * _up(dc4, 2) + w[0:1] * _up(dc4, 3)
        dp_ref[...] = draw.astype(BF16)
        gs = [jnp.sum(dc4 * r3, axis=0, keepdims=True), jnp.sum(dc4 * r2, axis=0, keepdims=True),
              jnp.sum(dc4 * r1, axis=0, keepdims=True), jnp.sum(dc4 * raw, axis=0, keepdims=True),
              jnp.sum(dc4, axis=0, keepdims=True)]
        row = lax.broadcasted_iota(jnp.int32, (8, TW), 0)
        acc = jnp.zeros((8, TW), F32)
        for k, gk in enumerate(gs):
            acc = jnp.where(row == k, gk, acc)
        gw_ref[...] = acc

    return pl.pallas_call(
        kern, grid=(width // TW,),
        in_specs=[pl.BlockSpec((L, TW), lambda j: (0, j)), pl.BlockSpec((L, TW), lambda j: (0, off_p + j)),
                  pl.BlockSpec((8, TW), lambda j: (0, off_w + j)), pl.BlockSpec(memory_space=pl.ANY)],
        out_specs=(pl.BlockSpec((L, TW), lambda j: (0, off_p + j)), pl.BlockSpec((8, TW), lambda j: (0, j))),
        out_shape=(jax.ShapeDtypeStruct(dproj.shape, BF16), jax.ShapeDtypeStruct((8, width), F32)),
        input_output_aliases={3: 0}, name=name, compiler_params=_cp(("arbitrary",)),
    )(dx, proj, cw4, dproj)


def _split3(v):
    h1 = v.astype(BF16)
    r1 = v - h1.astype(F32)
    h2 = r1.astype(BF16)
    h3 = (r1 - h2.astype(F32)).astype(BF16)
    return h1, h2, h3


def _dot01(m01, v, dims=_DIMS["nn"], m_left=True, terms=3):
    out = None
    for part in _split3(v)[:terms]:
        ops = (m01, part) if m_left else (part, m01)
        t = lax.dot_general(ops[0], ops[1], dims, preferred_element_type=F32)
        out = t if out is None else out + t
    return out


def _bdot(a, b, mode="nn"):
    return lax.dot_general(a.astype(BF16), b.astype(BF16), _DIMS[mode], preferred_element_type=F32)


def _softplus(v):
    return jnp.maximum(v, 0.0) + jnp.log1p(jnp.exp(-jnp.abs(v)))


def _dt_prep(proj, vec):
    L = proj.shape[0]

    def kern(p_ref, v_ref, dt_ref, cs_ref, sg_ref):
        v = v_ref[...]
        pre = p_ref[:, 0:128] + v[0:1]
        dt = _softplus(pre)
        da = dt * (-jnp.exp(v[1:2]))
        ii = lax.broadcasted_iota(jnp.int32, (Q, Q), 0)
        jj = lax.broadcasted_iota(jnp.int32, (Q, Q), 1)
        ltri = (jj <= ii).astype(BF16)
        lane = lax.broadcasted_iota(jnp.int32, (Q, 128), 1)
        for val, ref in ((dt, dt_ref), (_dot01(ltri, da), cs_ref), (_sigmoid(pre), sg_ref)):
            for g in range(NG):
                moved = val if g == 0 else pltpu.roll(val, 128 - 4 * g, axis=1)
                ref[g] = jnp.where(lane < 4, moved, 0.0)

    blk = pl.BlockSpec((NG, Q, 128), lambda c: (0, c, 0))
    return pl.pallas_call(
        kern, grid=(L // Q,),
        in_specs=[pl.BlockSpec((Q, 256), lambda c: (c, 0)), pl.BlockSpec((8, 128), lambda c: (0, 0))],
        out_specs=(blk, blk, blk),
        out_shape=(jax.ShapeDtypeStruct((NG, L, 128), F32),) * 3,
        name="dt_prep", compiler_params=_cp(("parallel",)),
    )(proj, vec)


def _head_masks():
    lane = lax.broadcasted_iota(jnp.int32, (1, 4 * HD), 1)
    return [((lane >= HD * j) & (lane < HD * (j + 1))) for j in range(4)]


def _expand4(v4, masks):
    R = v4.shape[0]
    out = jnp.zeros((R, 4 * HD), F32)
    for j in range(4):
        out = jnp.where(masks[j], jnp.broadcast_to(v4[:, j:j + 1], (R, 4 * HD)), out)
    return out


def _decay_matrix(cs_col, tri):
    colb = jnp.broadcast_to(cs_col, (Q, Q))
    return jnp.exp(jnp.where(tri, colb - colb.T, -jnp.inf))


def _ssd_fwd(xbc, dt4, cs4, vecg):
    L = xbc.shape[0]
    nc = L // Q

    def kern(x_ref, b_ref, c_ref, dt_ref, cs_ref, v_ref, y_ref, s_ref, S):
        c = pl.program_id(1)

        @pl.when(c == 0)
        def _():
            S[...] = jnp.zeros_like(S)

        masks = _head_masks()
        ii = lax.broadcasted_iota(jnp.int32, (Q, Q), 0)
        jj = lax.broadcasted_iota(jnp.int32, (Q, Q), 1)
        tri = jj <= ii
        for gi in range(GPS):
            xs, ns = slice(256 * gi, 256 * (gi + 1)), slice(NS * gi, NS * (gi + 1))
            dt4v, cs4v = dt_ref[gi], cs_ref[gi]
            dt_b, cs_b = _expand4(dt4v, masks), _expand4(cs4v, masks)
            d_b = _expand4(v_ref[gi], masks)[1:2]
            cs_last = cs_b[Q - 1:Q, :]
            x4, bm, cm = x_ref[:, xs], b_ref[:, ns], c_ref[:, ns]
            xdt = x4 * dt_b
            gm = _bdot(cm, bm, "nt")
            s4 = S[gi]
            s_ref[gi, 0] = s4
            y = _bdot(cm, s4) * jnp.exp(cs_b) + d_b * x4
            m_all = jnp.concatenate([(gm * _decay_matrix(cs4v[:, j:j + 1], tri)).astype(BF16) for j in range(4)], axis=0)
            yd = _bdot(m_all, xdt)
            for j in range(4):
                y = y + jnp.where(masks[j], yd[Q * j:Q * (j + 1)], 0.0)
            y_ref[:, xs] = y
            S[gi] = jnp.exp(cs_last) * s4 + _bdot(bm, xdt * jnp.exp(cs_last - cs_b), "tn")

    sc = pl.BlockSpec((GPS, Q, 128), lambda g, c: (g, c, 0))
    bw = NS * GPS
    return pl.pallas_call(
        kern, grid=(NG // GPS, nc),
        in_specs=[pl.BlockSpec((Q, 256 * GPS), lambda g, c: (c, g)),
                  pl.BlockSpec((Q, bw), lambda g, c: (c, INNER // bw + g)),
                  pl.BlockSpec((Q, bw), lambda g, c: (c, (INNER + NG * NS) // bw + g)),
                  sc, sc, pl.BlockSpec((GPS, 8, 128), lambda g, c: (g, 0, 0))],
        out_specs=(pl.BlockSpec((Q, 256 * GPS), lambda g, c: (c, g)),
                   pl.BlockSpec((GPS, 1, NS, 256), lambda g, c: (g, c, 0, 0))),
        out_shape=(jax.ShapeDtypeStruct((L, INNER), F32), jax.ShapeDtypeStruct((NG, nc, NS, 256), F32)),
        scratch_shapes=[pltpu.VMEM((GPS, NS, 256), F32)], name="ssd_fwd",
        compiler_params=_cp(("parallel", "arbitrary")),
    )(xbc, xbc, xbc, dt4, cs4, vecg)


def _ssd_bwd(xbc, dt4, cs4, sg4, vecg, s_all, dy):
    L = xbc.shape[0]
    nc = L // Q

    def kern(x_ref, b_ref, c_ref, dt_ref, cs_ref, sg_ref, v_ref, s_ref, dy_ref,
             dx_ref, db_ref, dc_ref, ddt_ref, st_ref, dS):
        cc = pl.program_id(1)

        @pl.when(cc == 0)
        def _():
            dS[...] = jnp.zeros_like(dS)
            st_ref[...] = jnp.zeros_like(st_ref)

        masks = _head_masks()
        ii = lax.broadcasted_iota(jnp.int32, (Q, Q), 0)
        jj = lax.broadcasted_iota(jnp.int32, (Q, Q), 1)
        tri = jj <= ii
        utri = (jj >= ii).astype(BF16)
        hsel = ((lax.broadcasted_iota(jnp.int32, (4 * HD, 128), 0) // HD)
                == lax.broadcasted_iota(jnp.int32, (4 * HD, 128), 1)).astype(BF16)
        hrow = ((lax.broadcasted_iota(jnp.int32, (4 * Q, 128), 0) // Q)
                == lax.broadcasted_iota(jnp.int32, (4 * Q, 128), 1)).astype(BF16)
        ones_q = jnp.ones((Q, 128), BF16)
        lane128 = lax.broadcasted_iota(jnp.int32, (Q, 128), 1)

        for gi in range(GPS):
            xs, ns = slice(256 * gi, 256 * (gi + 1)), slice(NS * gi, NS * (gi + 1))
            dt4v, cs4v, sg4v = dt_ref[gi], cs_ref[gi], sg_ref[gi]
            dt_b, cs_b = _expand4(dt4v, masks), _expand4(cs4v, masks)
            vv = _expand4(v_ref[gi], masks)
            a_b = -jnp.exp(vv[0:1])
            d_b = vv[1:2]
            a4 = -jnp.exp(v_ref[gi][0:1, :])
            cs_last = cs_b[Q - 1:Q, :]
            ecs = jnp.exp(cs_b)
            decay = jnp.exp(cs_last - cs_b)
            elast = jnp.exp(cs_last)
            x4, bm, cm, dyv = x_ref[:, xs], b_ref[:, ns], c_ref[:, ns], dy_ref[:, xs]
            s4 = s_ref[gi, 0]
            dsn = dS[gi]
            xdt = x4 * dt_b
            gm = _bdot(cm, bm, "nt")
            dye = dyv * ecs
            yoff = ecs * _bdot(cm, s4)
            t4 = _bdot(bm, dsn) * decay
            lms, mhs = [], []
            for j in range(4):
                colb = jnp.broadcast_to(cs4v[:, j:j + 1], (Q, Q))
                lms.append(jnp.exp(jnp.where(tri, colb - colb.T, -jnp.inf)))
                mhs.append(gm * lms[j])
            m_all = jnp.concatenate([m.astype(BF16) for m in mhs], axis=0)
            dy_m = jnp.concatenate([jnp.where(masks[j], dyv, 0.0).astype(BF16) for j in range(4)], axis=0)
            dxdt = t4 + _bdot(m_all, dy_m, "tn")
            dm_all = _bdot(dy_m, xdt, "nt")
            dg = jnp.zeros((Q, Q), F32)
            for j in range(4):
                dg = dg + dm_all[Q * j:Q * (j + 1)] * lms[j]
            e_all = dm_all * jnp.concatenate(mhs, axis=0)
            rsum = _dot01(ones_q, e_all, m_left=False, terms=2)
            da4 = -_dot01(hrow, e_all, _DIMS["tn"], m_left=False, terms=2)
            for j in range(4):
                da4 = da4 + jnp.where(lane128 == j, rsum[Q * j:Q * (j + 1)], 0.0)
            xt = xdt * t4
            tail = jnp.sum(xt, axis=0, keepdims=True) + elast * jnp.sum(s4 * dsn, axis=0, keepdims=True)
            gd_raw = jnp.sum(dyv * x4, axis=0, keepdims=True)
            stacked = jnp.concatenate([dyv * yoff - xt, dxdt * x4, jnp.broadcast_to(tail, (8, 4 * HD)),
                                       jnp.broadcast_to(gd_raw, (8, 4 * HD))], axis=0)
            seg = _dot01(hsel, stacked, m_left=False, terms=2)
            dda4 = _dot01(utri, da4 + seg[0:Q], terms=2) + seg[2 * Q:2 * Q + 1]
            ddt_ref[gi] = (dda4 * a4 + seg[Q:2 * Q]) * sg4v
            ga = jnp.sum(dda4 * dt4v * a4, axis=0, keepdims=True)
            row = lax.broadcasted_iota(jnp.int32, (8, 128), 0)
            st_ref[gi] += jnp.where(row == 0, ga, jnp.where(row == 1, seg[2 * Q + 8:2 * Q + 9], 0.0))
            dx_ref[:, xs] = d_b * dyv + dxdt * dt_b
            dc_ref[:, ns] = _bdot(dg, bm) + _bdot(dye, s4, "nt")
            db_ref[:, ns] = _bdot(dg, cm, "tn") + _bdot(xdt * decay, dsn, "nt")
            dS[gi] = elast * dsn + _bdot(cm, dye, "tn")

    rv = lambda c: nc - 1 - c
    sc = pl.BlockSpec((GPS, Q, 128), lambda g, c: (g, rv(c), 0))
    bw = NS * GPS
    return pl.pallas_call(
        kern, grid=(NG // GPS, nc),
        in_specs=[pl.BlockSpec((Q, 256 * GPS), lambda g, c: (rv(c), g)),
                  pl.BlockSpec((Q, bw), lambda g, c: (rv(c), INNER // bw + g)),
                  pl.BlockSpec((Q, bw), lambda g, c: (rv(c), (INNER + NG * NS) // bw + g)),
                  sc, sc, sc, pl.BlockSpec((GPS, 8, 128), lambda g, c: (g, 0, 0)),
                  pl.BlockSpec((GPS, 1, NS, 256), lambda g, c: (g, rv(c), 0, 0)),
                  pl.BlockSpec((Q, 256 * GPS), lambda g, c: (rv(c), g))],
        out_specs=(pl.BlockSpec((Q, 256 * GPS), lambda g, c: (rv(c), g)),
                   pl.BlockSpec((Q, bw), lambda g, c: (rv(c), g)),
                   pl.BlockSpec((Q, bw), lambda g, c: (rv(c), g)),
                   pl.BlockSpec((GPS, Q, 128), lambda g, c: (g, rv(c), 0)),
                   pl.BlockSpec((GPS, 8, 128), lambda g, c: (g, 0, 0))),
        out_shape=(jax.ShapeDtypeStruct((L, INNER), F32), jax.ShapeDtypeStruct((L, NG * NS), F32),
                   jax.ShapeDtypeStruct((L, NG * NS), F32), jax.ShapeDtypeStruct((NG, L, 128), F32),
                   jax.ShapeDtypeStruct((NG, 8, 128), F32)),
        scratch_shapes=[pltpu.VMEM((GPS, NS, 256), F32)], name="ssd_bwd",
        compiler_params=_cp(("parallel", "arbitrary")),
    )(xbc, xbc, xbc, dt4, cs4, sg4, vecg, s_all, dy)


def _dt_bwd(ddt, dproj, tl=256):
    L = ddt.shape[1]

    def kern(d_ref, _, dp_ref, gs_ref):
        @pl.when(pl.program_id(0) == 0)
        def _():
            gs_ref[...] = jnp.zeros_like(gs_ref)

        d = d_ref[0]
        for g in range(1, NG):
            d = d + pltpu.roll(d_ref[g], 4 * g, axis=1)
        gs_ref[...] += jnp.broadcast_to(jnp.sum(d, axis=0, keepdims=True), (8, 128))
        dp_ref[...] = jnp.concatenate([d, jnp.zeros_like(d)], axis=1).astype(BF16)

    return pl.pallas_call(
        kern, grid=(L // tl,),
        in_specs=[pl.BlockSpec((NG, tl, 128), lambda i: (0, i, 0)), pl.BlockSpec(memory_space=pl.ANY)],
        out_specs=(pl.BlockSpec((tl, 256), lambda i: (i, C_DT // 256)), pl.BlockSpec((8, 128), lambda i: (0, 0))),
        out_shape=(jax.ShapeDtypeStruct(dproj.shape, BF16), jax.ShapeDtypeStruct((8, 128), F32)),
        input_output_aliases={1: 0}, name="dt_bwd", compiler_params=_cp(("arbitrary",)),
    )(ddt, dproj)


GW = INNER // NG


def _gnorm_fwd(y, proj, w, tl=256):
    L = y.shape[0]
    zoff = C_Z // 1024

    def kern(y_ref, z_ref, w_ref, o_ref):
        z = z_ref[...].astype(F32)
        yz = y_ref[...] * (z * _sigmoid(z))
        wv = w_ref[...]
        for k in range(1024 // GW):
            sl = slice(GW * k, GW * (k + 1))
            v = yz[:, sl]
            rg = lax.rsqrt(jnp.mean(v * v, axis=-1, keepdims=True) + EPS)
            o_ref[:, sl] = ((v * rg) * wv[:, sl]).astype(BF16)

    blk = pl.BlockSpec((tl, 1024), lambda i, j: (i, j))
    return pl.pallas_call(
        kern, grid=(L // tl, 2),
        in_specs=[blk, pl.BlockSpec((tl, 1024), lambda i, j: (i, zoff + j)), pl.BlockSpec((1, 1024), lambda i, j: (0, j))],
        out_specs=blk, out_shape=jax.ShapeDtypeStruct((L, INNER), BF16), name="gnorm_fwd",
        compiler_params=_cp(("parallel", "parallel")),
    )(y, proj, w.reshape(1, INNER))


def _gnorm_bwd(dyb, y, proj, w, dproj, tl=256):
    L = y.shape[0]
    zoff = C_Z // 1024

    def kern(d_ref, y_ref, z_ref, w_ref, _, dy_ref, dp_ref, gw_ref):
        @pl.when(pl.program_id(1) == 0)
        def _():
            gw_ref[...] = jnp.zeros_like(gw_ref)

        z = z_ref[...].astype(F32)
        sg = _sigmoid(z)
        sz = z * sg
        yv = y_ref[...]
        yz = yv * sz
        dv = d_ref[...]
        wv = w_ref[...]
        for k in range(1024 // GW):
            sl = slice(GW * k, GW * (k + 1))
            v = yz[:, sl]
            rg = lax.rsqrt(jnp.mean(v * v, axis=-1, keepdims=True) + EPS)
            vn = v * rg
            dk = dv[:, sl]
            gw_ref[:, sl] += jnp.broadcast_to(jnp.sum(dk * vn, axis=0, keepdims=True), (8, GW))
            dvn = dk * wv[:, sl]
            dyz = rg * (dvn - vn * jnp.mean(dvn * vn, axis=-1, keepdims=True))
            dy_ref[:, sl] = dyz * sz[:, sl]
            dp_ref[:, sl] = (dyz * yv[:, sl] * (sg[:, sl] * (1.0 + z[:, sl] * (1.0 - sg[:, sl])))).astype(BF16)

    blk = pl.BlockSpec((tl, 1024), lambda j, i: (i, j))
    zblk = pl.BlockSpec((tl, 1024), lambda j, i: (i, zoff + j))
    return pl.pallas_call(
        kern, grid=(2, L // tl),
        in_specs=[blk, blk, zblk, pl.BlockSpec((1, 1024), lambda j, i: (0, j)), pl.BlockSpec(memory_space=pl.ANY)],
        out_specs=(blk, zblk, pl.BlockSpec((8, 1024), lambda j, i: (0, j))),
        out_shape=(jax.ShapeDtypeStruct((L, INNER), F32), jax.ShapeDtypeStruct(dproj.shape, BF16),
                   jax.ShapeDtypeStruct((8, INNER), F32)),
        input_output_aliases={4: 1}, name="gnorm_bwd", compiler_params=_cp(("parallel", "arbitrary")),
    )(dyb, y, proj, w.reshape(1, INNER), dproj)


def _merge_fwd(proj, bg, br_a, br_b, tl=256):
    L = proj.shape[0]
    goff = C_GATE // 1024

    def kern(g1_ref, g2_ref, b1_ref, b2_ref, a_ref, b_ref, o_ref):
        g1 = _sigmoid(g1_ref[...].astype(F32) + b1_ref[...])
        g2 = _sigmoid(g2_ref[...].astype(F32) + b2_ref[...])
        o_ref[...] = (g1 * a_ref[...] + g2 * b_ref[...]).astype(BF16)

    row = pl.BlockSpec((tl, 1024), lambda i: (i, 0))
    bg2 = bg.reshape(1, 2 * D)
    return pl.pallas_call(
        kern, grid=(L // tl,),
        in_specs=[pl.BlockSpec((tl, 1024), lambda i: (i, goff)), pl.BlockSpec((tl, 1024), lambda i: (i, goff + 1)),
                  pl.BlockSpec((1, 1024), lambda i: (0, 0)), pl.BlockSpec((1, 1024), lambda i: (0, 1)), row, row],
        out_specs=row, out_shape=jax.ShapeDtypeStruct((L, D), BF16), name="merge_fwd",
        compiler_params=_cp(("parallel",)),
    )(proj, proj, bg2, bg2, br_a, br_b)


def _merge_bwd(dm, proj, bg, br_a, br_b, dproj, tl=256):
    L = proj.shape[0]
    goff = C_GATE // 1024

    def kern(dm_ref, g_ref, b_ref, a_ref, bb_ref, _, dbr_ref, dp_ref, gb_ref):
        j = pl.program_id(0)

        @pl.when(pl.program_id(1) == 0)
        def _():
            gb_ref[...] = jnp.zeros_like(gb_ref)

        g = _sigmoid(g_ref[...].astype(F32) + b_ref[...])
        br = jnp.where(j == 0, a_ref[...], bb_ref[...])
        dmv = dm_ref[...]
        dbr_ref[0] = (dmv * g).astype(BF16)
        dgate = dmv * br * g * (1.0 - g)
        gb_ref[...] += jnp.broadcast_to(jnp.sum(dgate, axis=0, keepdims=True), (8, 1024))
        dp_ref[...] = dgate.astype(BF16)

    row = pl.BlockSpec((tl, 1024), lambda j, i: (i, 0))
    gblk = pl.BlockSpec((tl, 1024), lambda j, i: (i, goff + j))
    return pl.pallas_call(
        kern, grid=(2, L // tl),
        in_specs=[row, gblk, pl.BlockSpec((1, 1024), lambda j, i: (0, j)), row, row, pl.BlockSpec(memory_space=pl.ANY)],
        out_specs=(pl.BlockSpec((1, tl, 1024), lambda j, i: (j, i, 0)), gblk, pl.BlockSpec((8, 1024), lambda j, i: (0, j))),
        out_shape=(jax.ShapeDtypeStruct((2, L, D), BF16), jax.ShapeDtypeStruct(dproj.shape, BF16),
                   jax.ShapeDtypeStruct((8, 2 * D), F32)),
        input_output_aliases={5: 1}, name="merge_bwd", compiler_params=_cp(("parallel", "arbitrary")),
    )(dm, proj, bg.reshape(1, 2 * D), br_a, br_b, dproj)


def _coords():
    return lax.axis_index("x"), lax.axis_index("y"), lax.axis_index("c")


def _other_chips(sk):
    xk, yk = sk // 2, sk % 2
    return [((1 - xk, yk), 2 * (1 - xk) + yk), ((xk, 1 - yk), 2 * xk + 1 - yk), ((1 - xk, 1 - yk), 2 * (1 - xk) + 1 - yk)]


def _rows(start, size):
    assert size % 128 == 0
    return pl.ds(pl.multiple_of(start, 128), size)


def _per_chip(fn):
    x, y, _ = _coords()
    s = 2 * x + y
    for sk in range(4):
        pl.when(s == sk)(functools.partial(fn, sk))


XTRA = PIECE - PMAIN


def _place(shard, full_shape, block, index_map, idx, name, blk0=0, nblk=None, dep=None, into=None):
    in_block = block[-2:]
    if nblk is None:
        nblk = shard.shape[0] // in_block[0]

    def kern(idx_ref, s_ref, *rest):
        o_ref = rest[-1]
        o_ref[...] = s_ref[...].astype(BF16).reshape(o_ref.shape)

    extra = ([dep] if dep is not None else []) + ([into] if into is not None else [])
    grid_spec = pltpu.PrefetchScalarGridSpec(
        num_scalar_prefetch=1, grid=(nblk,),
        in_specs=[pl.BlockSpec(in_block, lambda i, idx_ref: (blk0 + i, 0))] + [_ANY] * len(extra),
        out_specs=pl.BlockSpec(block, index_map))
    aliases = {1 + len(extra): 0} if into is not None else {}
    return pl.pallas_call(kern, grid_spec=grid_spec, out_shape=jax.ShapeDtypeStruct(full_shape, BF16), name=name,
                          input_output_aliases=aliases, compiler_params=_cp(("arbitrary",)))(idx, shard, *extra)


_SEM = pl.BlockSpec(memory_space=pltpu.SEMAPHORE)
_EFFECT = pltpu.SideEffectType.DATAFLOW_SIDE_EFFECTING


_ANY = pl.BlockSpec(memory_space=pl.ANY)


def _tie(v, dep, name):
    def body(v_ref, dep_ref, o_ref):
        del v_ref, dep_ref, o_ref

    return pl.pallas_call(body, out_shape=jax.ShapeDtypeStruct(v.shape, v.dtype), in_specs=[_ANY, _ANY],
                          out_specs=_ANY, input_output_aliases={0: 0}, name=name)(v, dep)


def _split_call(name, arrays, start=None, wait=None, wait_sems=None, after=None):
    keys = list(arrays)
    n = len(keys)
    n_start = start.n if start is not None else 0
    afters = [] if after is None else (list(after) if isinstance(after, (list, tuple)) else [after])

    def body(*refs):
        pos = n
        if wait is not None:
            wss, wrs = refs[pos], refs[pos + 1]
            pos += 2
        pos += len(afters)
        if start is not None:
            nss, nrs = refs[pos], refs[pos + 1]
            pos += 2
        R = dict(zip(keys, refs[pos:pos + n]))
        token = refs[pos + n]
        x, y, c = _coords()

        def desc(src, dst, dev, ss, rs, k):
            return pltpu.make_async_remote_copy(src_ref=src, dst_ref=dst, send_sem=ss.at[k], recv_sem=rs.at[k],
                                                device_id=dev, device_id_type=MESH)

        def run(sk):
            if wait is not None:
                for k, (snd, land) in enumerate(wait.copies(sk, R)):
                    if snd is not None:
                        desc(snd[0], snd[1], snd[2], wss, wrs, k).wait_send()
                    if land is not None:
                        desc(land, land, (x, y, c), wss, wrs, k).wait_recv()
            if start is not None:
                for k, (snd, land) in enumerate(start.copies(sk, R)):
                    if snd is not None:
                        desc(snd[0], snd[1], snd[2], nss, nrs, k).start()

        _per_chip(run)
        token[...] = jnp.zeros_like(token)

    hbm = pl.BlockSpec(memory_space=HBM)
    vals = [arrays[k] for k in keys]
    ins, in_specs = list(vals), [hbm] * n
    if wait is not None:
        ins += list(wait_sems)
        in_specs += [_SEM, _SEM]
    ins += afters
    in_specs += [pl.BlockSpec(memory_space=pl.ANY)] * len(afters)
    out_shape, out_specs = [], []
    if start is not None:
        out_shape += [pltpu.SemaphoreType.DMA((n_start,)), pltpu.SemaphoreType.DMA((n_start,))]
        out_specs += [_SEM, _SEM]
    first = len(out_shape)
    out_shape += [jax.ShapeDtypeStruct(v.shape, v.dtype) for v in vals] + [jax.ShapeDtypeStruct((8, 128), F32)]
    out_specs += [hbm] * n + [pl.BlockSpec(memory_space=pltpu.VMEM)]
    res = pl.pallas_call(
        body, out_shape=tuple(out_shape), in_specs=in_specs, out_specs=tuple(out_specs),
        input_output_aliases={i: first + i for i in range(n)}, name=name,
        compiler_params=pltpu.CompilerParams(has_side_effects=_EFFECT),
    )(*ins)
    sems = (res[0], res[1]) if start is not None else None
    return dict(zip(keys, res[first:first + n])), sems, res[-1]


class _Plan:
    def __init__(self, n, copies):
        self.n, self.copies = n, copies


_HM, _HX = PMAIN // 2, XTRA // 2
WAVE0 = 768
WAVES = ((0, WAVE0), (WAVE0, _HM - WAVE0))
_WIN = {
    "wq0": (True, "wct", lambda r, sc, hc: r.at[_rows(PMAIN * sc + _HM * hc + WAVES[0][0], WAVES[0][1]), :]),
    "wq1": (True, "wct", lambda r, sc, hc: r.at[_rows(PMAIN * sc + _HM * hc + WAVES[1][0], WAVES[1][1]), :]),
    "xt": (True, "xt", lambda r, sc, hc: r.at[sc, _rows(_HX * hc, _HX), :]),
    "w1": (True, "w1", lambda r, sc, hc: r.at[_rows(512 * hc, 512), pl.ds(1024 * sc, 1024)]),
    "w2": (True, "w2", lambda r, sc, hc: r.at[_rows(1024 * sc + 512 * hc, 512), :]),
    "wa": (True, "wa", lambda r, sc, hc: r.at[_rows(256 * sc + 128 * hc, 128), :]),
    "wb": (True, "wb", lambda r, sc, hc: r.at[_rows(512 * sc + 256 * hc, 256), :]),
    "wo": (True, "wo", lambda r, sc, hc: r.at[_rows(256 * sc + 128 * hc, 128), :]),
    "cw": (False, "cw", lambda r, sc, hc: r.at[sc]),
}


_PIECE_SRC = {
    "wq0": lambda p, hc: p.at[_rows(_HM * hc + WAVES[0][0], WAVES[0][1]), :],
    "wq1": lambda p, hc: p.at[_rows(_HM * hc + WAVES[1][0], WAVES[1][1]), :],
    "xt": lambda p, hc: p.at[_rows(PMAIN + _HX * hc, _HX), :],
}


def _ag_chips_plan(keys):
    def copies(sk, R):
        _, _, c = _coords()
        out = []
        for key in keys:
            _, arr, win = _WIN[key]
            for (px, py), ps in _other_chips(sk):
                dst = win(R[arr], sk, c)
                src = _PIECE_SRC[key](R["piece"], c) if key in _PIECE_SRC else dst
                out.append(((src, dst, (px, py, c)), win(R[arr], ps, c)))
        return out
    return _Plan(3 * len(keys), copies)


def _ag_sibling_plan(keys):
    keys = [k for k in keys if _WIN[k][0]]

    def copies(sk, R):
        x, y, c = _coords()
        out = []
        for key in keys:
            _, arr, win = _WIN[key]
            for _, ps in _other_chips(sk):
                w = win(R[arr], ps, c)
                out.append(((w, w, (x, y, 1 - c)), win(R[arr], ps, 1 - c)))
        return out
    return _Plan(3 * len(keys), copies)


def _in_proj_wave(h, wct, wave, proj=None, tm=2048):
    L = h.shape[0]
    tm = min(tm, L)
    off, size = WAVES[wave]
    start = lambda j: pl.multiple_of(_HM * j + off, 128)

    def kern(h_ref, w_ref, *rest):
        o_ref = rest[-1]
        o_ref[...] = lax.dot_general(h_ref[...], w_ref[...], _DIMS["nt"], preferred_element_type=F32).astype(BF16)

    in_specs = [pl.BlockSpec((tm, D), lambda j, i: (i, 0)),
                pl.BlockSpec((pl.Element(size), pl.Element(D)), lambda j, i: (start(j), 0))]
    args, aliases = [h, wct], {}
    if proj is not None:
        in_specs.append(pl.BlockSpec(memory_space=pl.ANY))
        args.append(proj)
        aliases = {2: 0}
    return pl.pallas_call(
        kern, grid=(8, L // tm), in_specs=in_specs,
        out_specs=pl.BlockSpec((pl.Element(tm), pl.Element(size)), lambda j, i: (i * tm, start(j))),
        out_shape=jax.ShapeDtypeStruct((L, NCW), BF16), input_output_aliases=aliases,
        name="in_proj_wave%d" % wave, compiler_params=_cp(("parallel", "parallel")),
    )(*args)


def _fix_wct(wct, xt):
    nb = PMAIN // XTRA

    def kern(w_ref, x_ref, o_ref):
        k = pl.program_id(0)
        xv = x_ref[0]
        o_ref[...] = jnp.where(k < 3, (w_ref[...].astype(F32) + xv.astype(F32)).astype(BF16), xv)

    blk = pl.BlockSpec((XTRA, D), lambda k: (nb * (k + 1), 0))
    rblk = pl.BlockSpec((XTRA, D), lambda k: (jnp.where(k < 3, nb * (k + 1), 0), 0))
    return pl.pallas_call(
        kern, grid=(4,), in_specs=[rblk, pl.BlockSpec((1, XTRA, D), lambda k: (k, 0, 0))], out_specs=blk,
        out_shape=jax.ShapeDtypeStruct(wct.shape, BF16), input_output_aliases={0: 0}, name="fix_wct",
        compiler_params=_cp(("arbitrary",)),
    )(wct, xt)


_HP = PIECE // 2
_GWIN = [
    lambda r, sc, hc: r.at[_rows(PMAIN * sc + _HP * hc, _HP), :],
    lambda r, sc, hc: r.at[_rows(512 * hc, 512), pl.ds(1024 * sc, 1024)],
    lambda r, sc, hc: r.at[_rows(1024 * sc + 512 * hc, 512), :],
    lambda r, sc, hc: r.at[_rows(256 * sc + 128 * hc, 128), :],
    lambda r, sc, hc: r.at[_rows(512 * sc + 256 * hc, 256), :],
    lambda r, sc, hc: r.at[_rows(256 * sc + 128 * hc, 128), :],
]
HALF_SHAPES = [(PIECE // 2, D), (512, 1024), (512, 1024), (128, 1024), (256, 1024), (128, 1024)]


def _rs_sibling_plan(ts):
    def copies(sk, R):
        x, y, c = _coords()
        out = []
        for t in ts:
            for sc in range(4):
                land = R["ra%d" % t].at[sc]
                out.append(((_GWIN[t](R["g%d" % t], sc, 1 - c), land, (x, y, 1 - c)), land))
        return out
    return _Plan(4 * len(ts), copies)


def _rs_chips_plan(ts):
    def copies(sk, R):
        _, _, c = _coords()
        out = []
        for t in ts:
            for j, ((px, py), ps) in enumerate(_other_chips(sk)):
                land = R["rb%d" % t].at[j]
                out.append(((R["hb%d" % t].at[ps], land, (px, py, c)), land))
        return out
    return _Plan(3 * len(ts), copies)


def _rs_share_plan(ts):
    def copies(sk, R):
        x, y, c = _coords()
        out = []
        for t in ts:
            rows = HALF_SHAPES[t][0]
            mine = R["f%d" % t].at[_rows(rows * c, rows), :]
            out.append(((mine, mine, (x, y, 1 - c)), R["f%d" % t].at[_rows(rows * (1 - c), rows), :]))
        return out
    return _Plan(len(ts), copies)


def _half_tiling(t):
    rows, cols = HALF_SHAPES[t]
    if t == 0:
        return (rows // 2, cols), 2, lambda i: (i, 0)
    return (rows, cols), 1, lambda i: (0, 0)


def _window_spec(t, blk):
    if t == 0:
        return pl.BlockSpec((pl.Element(blk[0]), pl.Element(blk[1])), lambda i, sc, idx_ref: (
            pl.multiple_of(PMAIN * sc + _HP * idx_ref[1] + blk[0] * i, 128), 0))
    if t == 1:
        return pl.BlockSpec(blk, lambda i, sc, idx_ref: (idx_ref[1], sc))
    return pl.BlockSpec(blk, lambda i, sc, idx_ref: (2 * sc + idx_ref[1], 0))


def _chip_sum(g, ra, t, idx, name):
    rows, cols = HALF_SHAPES[t]
    blk, nblk, inner = _half_tiling(t)

    def kern(idx_ref, g_ref, r_ref, hb_ref, hf_ref):
        v = g_ref[...].astype(F32) + r_ref[0].astype(F32)
        hb_ref[0] = v.astype(BF16)

        @pl.when(pl.program_id(1) == idx_ref[0])
        def _():
            hf_ref[...] = v

    omap = lambda i, sc, idx_ref: (sc,) + inner(i)
    grid_spec = pltpu.PrefetchScalarGridSpec(
        num_scalar_prefetch=1, grid=(nblk, 4),
        in_specs=[_window_spec(t, blk), pl.BlockSpec((1,) + blk, omap)],
        out_specs=(pl.BlockSpec((1,) + blk, omap), pl.BlockSpec(blk, lambda i, sc, idx_ref: inner(i))))
    return pl.pallas_call(
        kern, grid_spec=grid_spec,
        out_shape=(jax.ShapeDtypeStruct((4, rows, cols), BF16), jax.ShapeDtypeStruct((rows, cols), F32)),
        name=name, compiler_params=_cp(("parallel", "arbitrary")),
    )(idx, g, ra)


def _final_sum(hf, rb, t, idx, name):
    rows, cols = HALF_SHAPES[t]
    blk, nblk, inner = _half_tiling(t)
    nbr = rows // blk[0]

    def kern(idx_ref, h_ref, r_ref, o_ref):
        o_ref[...] = ((h_ref[...] + r_ref[0].astype(F32)) + r_ref[1].astype(F32)) + r_ref[2].astype(F32)

    def omap(i, idx_ref):
        r, cidx = inner(i)
        return nbr * idx_ref[1] + r, cidx

    grid_spec = pltpu.PrefetchScalarGridSpec(
        num_scalar_prefetch=1, grid=(nblk,),
        in_specs=[pl.BlockSpec(blk, lambda i, idx_ref: inner(i)),
                  pl.BlockSpec((3,) + blk, lambda i, idx_ref: (0,) + inner(i))],
        out_specs=pl.BlockSpec(blk, omap))
    return pl.pallas_call(
        kern, grid_spec=grid_spec, out_shape=jax.ShapeDtypeStruct((2 * rows, cols), F32),
        name=name, compiler_params=_cp(("parallel",)),
    )(idx, hf, rb)


class _ReduceScatter:
    def __init__(self, ts, grads, idx, tag):
        self.ts, self.idx, self.tag = ts, idx, tag
        arr = {}
        for t in ts:
            arr["g%d" % t] = grads[t]
            arr["ra%d" % t] = lax.empty((4,) + HALF_SHAPES[t], BF16)
        self.plan = _rs_sibling_plan(ts)
        self.arr, self.sems, self.token = _split_call("rs_sibling_start_" + tag, arr, start=self.plan)

    def chips(self, after):
        arr, _, _ = _split_call("rs_sibling_wait_" + self.tag, self.arr, wait=self.plan, wait_sems=self.sems, after=after)
        brr, self.hf = {}, {}
        for t in self.ts:
            hb, self.hf[t] = _chip_sum(arr["g%d" % t], arr["ra%d" % t], t, self.idx, "chip_sum_%d" % t)
            brr["hb%d" % t] = hb
            brr["rb%d" % t] = lax.empty((3,) + HALF_SHAPES[t], BF16)
        self.plan = _rs_chips_plan(self.ts)
        self.arr, self.sems, self.token = _split_call("rs_chips_start_" + self.tag, brr, start=self.plan)
        return self.token

    def share(self, after):
        brr, _, _ = _split_call("rs_chips_wait_" + self.tag, self.arr, wait=self.plan, wait_sems=self.sems, after=after)
        frr = {"f%d" % t: _final_sum(self.hf[t], brr["rb%d" % t], t, self.idx, "final_sum_%d" % t) for t in self.ts}
        self.plan = _rs_share_plan(self.ts)
        self.arr, self.sems, self.token = _split_call("rs_share_start_" + self.tag, frr, start=self.plan)
        return self.token

    def result(self, after):
        frr, _, _ = _split_call("rs_share_wait_" + self.tag, self.arr, wait=self.plan, wait_sems=self.sems, after=after)
        return {t: frr["f%d" % t] for t in self.ts}


def _all8_plan(key):
    def copies(sk, R):
        x, y, c = _coords()
        own = R[key].at[4 * x + 2 * y + c]
        out = []
        for k in range(1, 8):
            dev = ((1 - x) if (k >> 2) & 1 else x, (1 - y) if (k >> 1) & 1 else y, (1 - c) if k & 1 else c)
            out.append(((own, own, dev), R[key].at[4 * dev[0] + 2 * dev[1] + dev[2]]))
        return out
    return _Plan(7, copies)


def _small_all_gather(v):
    def body(v_ref, o_ref, send_sems, recv_sems, loc_sem):
        x, y, c = _coords()
        me = 4 * x + 2 * y + c
        lc = pltpu.make_async_copy(v_ref, o_ref.at[me], loc_sem)
        lc.start()
        cps = []
        for k in range(1, 8):
            fx, fy, fc = (k >> 2) & 1, (k >> 1) & 1, k & 1
            dev = ((1 - x) if fx else x, (1 - y) if fy else y, (1 - c) if fc else c)
            cp = pltpu.make_async_remote_copy(src_ref=v_ref, dst_ref=o_ref.at[me], send_sem=send_sems.at[k - 1],
                                              recv_sem=recv_sems.at[k - 1], device_id=dev, device_id_type=MESH)
            cp.start()
            cps.append((cp, 4 * dev[0] + 2 * dev[1] + dev[2]))
        for k, (cp, frm) in enumerate(cps):
            got = o_ref.at[frm]
            pltpu.make_async_remote_copy(src_ref=got, dst_ref=got, send_sem=send_sems.at[k], recv_sem=recv_sems.at[k],
                                         device_id=(x, y, c), device_id_type=MESH).wait_recv()
        for cp, _ in cps:
            cp.wait_send()
        lc.wait()

    hbm = pl.BlockSpec(memory_space=HBM)
    return pl.pallas_call(
        body, out_shape=jax.ShapeDtypeStruct((8,) + v.shape, F32), in_specs=[hbm], out_specs=hbm,
        scratch_shapes=[pltpu.SemaphoreType.DMA((7,)), pltpu.SemaphoreType.DMA((7,)), pltpu.SemaphoreType.DMA(())],
        name="small_all_gather", compiler_params=pltpu.CompilerParams(has_side_effects=True),
    )(v)


def _sum8(v, name="small_sum"):
    def kern(v_ref, o_ref):
        acc = v_ref[0]
        for k in range(1, 8):
            acc = acc + v_ref[k]
        o_ref[...] = acc

    return pl.pallas_call(kern, out_shape=jax.ShapeDtypeStruct(v.shape[1:], F32), name=name)(v)


def _adamw(w, g, m, v, name, tr=128, blk0=0, nblk=None, into=None, copy_g=False):
    R, C = w.shape
    tr = min(tr, R)
    if nblk is None:
        assert R % tr == 0 and blk0 == 0
        nblk = R // tr
    n_out = 4 if copy_g else 3

    def kern(*refs):
        w_ref, g_ref, m_ref, v_ref = refs[:4]
        d_ref, mo_ref, vo_ref = refs[-n_out:][:3]
        gv = g_ref[...]
        mn = ADAM_B1 * m_ref[...] + (1.0 - ADAM_B1) * gv
        vn = ADAM_B2 * v_ref[...] + (1.0 - ADAM_B2) * (gv * gv)
        m_hat = mn / (1.0 - ADAM_B1 ** ADAM_STEP)
        v_hat = vn / (1.0 - ADAM_B2 ** ADAM_STEP)
        d_ref[...] = -ADAM_LR * (m_hat / (jnp.sqrt(v_hat) + ADAM_EPS) + ADAM_WD * w_ref[...])
        mo_ref[...] = mn
        vo_ref[...] = vn
        if copy_g:
            refs[-1][...] = gv

    blk = pl.BlockSpec((tr, C), lambda i: (blk0 + i, 0))
    sd = jax.ShapeDtypeStruct((R, C), F32)
    in_specs, args, aliases = [blk] * 4, [w, g, m, v], {}
    if into is not None:
        in_specs += [pl.BlockSpec(memory_space=pl.ANY)] * 3
        args += list(into)
        aliases = {4: 0, 5: 1, 6: 2}
    return pl.pallas_call(kern, grid=(nblk,), in_specs=in_specs, out_specs=(blk,) * n_out, out_shape=(sd,) * n_out,
                          input_output_aliases=aliases, name=name, compiler_params=_cp(("parallel",)))(*args)


def _adamw_w_in(wt, gp, mt, vt, offs, name, r0, tr, nblk, views, into=None):
    el = lambda n: (pl.Element(n), pl.Element(D))
    own = pl.BlockSpec(el(tr), lambda i, o: (pl.multiple_of(r0 + tr * i, 8), 0))

    def view(k):
        return pl.BlockSpec(el(tr), lambda i, o: (pl.multiple_of(jnp.maximum(r0 + tr * i + o[k], 0), 8), 0))

    def kern(o_ref, w_ref, m_ref, v_ref, *refs):
        g_refs, (d_ref, mo_ref, vo_ref, go_ref) = refs[:len(views)], refs[-4:]
        gv = g_refs[0][...]
        if len(views) == 2:
            row = r0 + tr * pl.program_id(0) + lax.broadcasted_iota(jnp.int32, (tr, D), 0)
            gv = jnp.where(row < o_ref[2], gv, g_refs[1][...])
        mn = ADAM_B1 * m_ref[...] + (1.0 - ADAM_B1) * gv
        vn = ADAM_B2 * v_ref[...] + (1.0 - ADAM_B2) * (gv * gv)
        m_hat = mn / (1.0 - ADAM_B1 ** ADAM_STEP)
        v_hat = vn / (1.0 - ADAM_B2 ** ADAM_STEP)
        d_ref[...] = -ADAM_LR * (m_hat / (jnp.sqrt(v_hat) + ADAM_EPS) + ADAM_WD * w_ref[...])
        mo_ref[...] = mn
        vo_ref[...] = vn
        go_ref[...] = gv

    in_specs = [own, own, own] + [view(k) for k in views]
    args = [wt, mt, vt] + [gp] * len(views)
    aliases = {}
    if into is not None:
        in_specs += [pl.BlockSpec(memory_space=pl.ANY)] * 4
        args += list(into)
        aliases = {1 + len(args) - 4 + j: j for j in range(4)}
    grid_spec = pltpu.PrefetchScalarGridSpec(num_scalar_prefetch=1, grid=(nblk,), in_specs=in_specs,
                                             out_specs=(own,) * 4)
    sd = jax.ShapeDtypeStruct(wt.shape, F32)
    return pl.pallas_call(kern, grid_spec=grid_spec, out_shape=(sd,) * 4, input_output_aliases=aliases, name=name,
                          compiler_params=_cp(("parallel",)))(offs, *args)


def _to_piece(wt, s):
    z = lambda n: jnp.zeros((n, D), wt.dtype)
    pads = [functools.partial(lambda k, w: jnp.pad(w, ((8 * k, PIECE - W_SHARD - 8 * k), (0, 0))).astype(BF16), k)
            for k in range(3)]
    last = lambda w: jnp.concatenate([z(24), w[:744], w[776:], w[744:776], z(PIECE - 24 - W_SHARD)], axis=0).astype(BF16)
    return lax.switch(s, pads + [last], wt)


def _from_piece(p, s):
    cuts = [functools.partial(lambda k, q: q[8 * k:8 * k + W_SHARD], k) for k in range(3)]
    last = lambda q: jnp.concatenate([q[24:768], q[2816:2848], q[768:2816]], axis=0)
    return lax.switch(s, cuts + [last], p)


_SMALL = [("b_gate", 2048), ("ssm_conv_b", 4096), ("dt_bias", 32), ("A_log", 32), ("D_skip", 32),
          ("ssm_norm_w", 2048), ("norm_mlp", 1024), ("norm_final", 1024), ("sc_conv_w", 3072), ("ssm_conv_w", 16384),
          ("loss", 1)]


def _pack(vals, table, rows):
    parts = []
    for name, n in table:
        v = vals[name].reshape(-1).astype(F32)
        pad = (-n) % 128
        parts.append(jnp.pad(v, (0, pad)) if pad else v)
    flat = jnp.concatenate(parts)
    return jnp.pad(flat, (0, rows * 128 - flat.shape[0])).reshape(rows, 128)


def _unpack(arr, table):
    flat = arr.reshape(-1)
    out, off = {}, 0
    for name, n in table:
        out[name] = flat[off:off + n]
        off += n + ((-n) % 128)
    return out


def kernel(x, norm_mix, w_in, b_gate, sc_conv_w, ssm_conv_w, ssm_conv_b, dt_bias, A_log, D_skip, ssm_norm_w, w_branch_sc, w_branch_ssm, w_out, norm_mlp, w_mlp1, w_mlp2, norm_final, loss_target, m_norm_mix, m_w_in, m_b_gate, m_sc_conv_w, m_ssm_conv_w, m_ssm_conv_b, m_dt_bias, m_A_log, m_D_skip, m_ssm_norm_w, m_w_branch_sc, m_w_branch_ssm, m_w_out, m_norm_mlp, m_w_mlp1, m_w_mlp2, m_norm_final, v_norm_mix, v_w_in, v_b_gate, v_sc_conv_w, v_ssm_conv_w, v_ssm_conv_b, v_dt_bias, v_A_log, v_D_skip, v_ssm_norm_w, v_w_branch_sc, v_w_branch_ssm, v_w_out, v_norm_mlp, v_w_mlp1, v_w_mlp2, v_norm_final):
    L = x.shape[1]
    nc = L // Q
    xi, yi, ci = lax.axis_index("x"), lax.axis_index("y"), lax.axis_index("c")
    s = 2 * xi + yi
    idx = jnp.stack([s, ci]).astype(jnp.int32)
    x0 = x.reshape(L, D)
    tgt = loss_target.reshape(L, D)

    piece = _to_piece(w_in.T, s)
    nb = PMAIN // XTRA
    cws = jnp.zeros((8, 1280), F32)
    cws = cws.at[0:3, 0:256].set(sc_conv_w).at[0:4, 256:1280].set(ssm_conv_w)
    cw0 = lax.dynamic_update_slice(jnp.zeros((4, 8, 1280), F32), cws[None], (s, 0, 0))
    win_keys, win2_keys, mid_keys, end_keys = ["xt", "cw", "wq0"], ["wq1"], ["wa", "wb", "wo", "w1"], ["w2"]
    gw, sems_w, tok = _split_call(
        "ag_win_start", {"wct": lax.empty((NCW, D), BF16), "xt": lax.empty((4, XTRA, D), BF16), "cw": cw0, "piece": piece},
        start=_ag_chips_plan(win_keys))
    g2, sems_w2, tok = _split_call("ag_win2_start", {"wct": gw["wct"], "piece": gw["piece"]},
                                   start=_ag_chips_plan(win2_keys), after=tok)
    piece = g2["piece"]
    gw["wct"] = _place(piece, (NCW, D), (XTRA, D), lambda i, r: (nb * r[0] + i, 0), idx, "place_wct", nblk=nb,
                       dep=tok, into=g2["wct"])
    gw["xt"] = _place(piece, (4, XTRA, D), (1, XTRA, D), lambda i, r: (r[0], 0, 0), idx, "place_xt", blk0=nb, nblk=1,
                      dep=tok, into=gw["xt"])
    gw["piece"] = piece
    wa0 = _place(w_branch_sc, (D, D), (256, 1024), lambda i, r: (r[0], 0), idx, "place_wa", dep=tok)
    wb0 = _place(w_branch_ssm, (INNER, D), (512, 1024), lambda i, r: (r[0], 0), idx, "place_wb", dep=tok)
    wo0 = _place(w_out, (D, D), (256, 1024), lambda i, r: (r[0], 0), idx, "place_wo", dep=tok)
    w10 = _place(w_mlp1, (D, DFF), (256, 1024), lambda i, r: (i, r[0]), idx, "place_w1", dep=tok)
    gm, sems_m, tok = _split_call("ag_mid_start", {"wa": wa0, "wb": wb0, "wo": wo0, "w1": w10},
                                  start=_ag_chips_plan(mid_keys))
    w20 = _place(w_mlp2, (DFF, D), (256, 1024), lambda i, r: (4 * r[0] + i, 0), idx, "place_w2", dep=tok)
    ge, sems_e, tok = _split_call("ag_end_start", {"w2": w20}, start=_ag_chips_plan(end_keys))
    h = _rms_fwd(x0, norm_mix, "rms_mix", dep=tok)
    gw, sems_w, tok = _split_call("ag_win_pass", gw, wait=_ag_chips_plan(win_keys), wait_sems=sems_w,
                                  start=_ag_sibling_plan(win_keys), after=h)
    gw, _, _ = _split_call("ag_win_done", gw, wait=_ag_sibling_plan(win_keys), wait_sems=sems_w, after=tok)
    wc, cw_all = _fix_wct(gw["wct"], gw["xt"]), gw["cw"]
    sc_w_full = jnp.concatenate([cw_all[k, :, 0:256] for k in range(4)], axis=1)
    ssm_w_full = jnp.concatenate([cw_all[k, :, 256:1280] for k in range(4)], axis=1)
    cw4 = ssm_w_full.at[4].set(ssm_conv_b)
    vec = jnp.zeros((8, 128), F32).at[0, :NH].set(dt_bias).at[1, :NH].set(A_log)
    vecg = jnp.zeros((NG, 8, 128), F32).at[:, 0, :4].set(A_log.reshape(NG, 4)).at[:, 1, :4].set(D_skip.reshape(NG, 4))

    dtraw = _matmul(h, wc[C_DT:], "nt", F32, 512, 256, 1024, "in_proj_dt")
    proj = _in_proj_wave(h, wc, 0)
    g2, sems_w2, tok = _split_call("ag_win2_pass", {"wct": wc, "piece": gw["piece"]},
                                   wait=_ag_chips_plan(win2_keys), wait_sems=sems_w2,
                                   start=_ag_sibling_plan(win2_keys), after=[proj, dtraw])
    g2, _, _ = _split_call("ag_win2_done", g2, wait=_ag_sibling_plan(win2_keys), wait_sems=sems_w2, after=tok)
    wc = g2["wct"]
    proj = _in_proj_wave(h, wc, 1, proj=proj)
    ya = _sc_fwd(proj, sc_w_full)
    xbc = _ssm_conv_fwd(proj, cw4)
    dt4, cs4, sg4 = _dt_prep(dtraw, vec)
    y, s_all = _ssd_fwd(xbc, dt4, cs4, vecg)
    gm, sems_m, tok = _split_call("ag_mid_pass", gm, wait=_ag_chips_plan(mid_keys), wait_sems=sems_m,
                                  start=_ag_sibling_plan(mid_keys), after=[y, ya])
    yb = _gnorm_fwd(_tie(y, tok, "tie_y"), proj, ssm_norm_w)
    gm, _, _ = _split_call("ag_mid_done", gm, wait=_ag_sibling_plan(mid_keys), wait_sems=sems_m, after=yb)
    wa, wb, wo, w1 = gm["wa"], gm["wb"], gm["wo"], gm["w1"]
    ge, sems_e, tok = _split_call("ag_end_pass", ge, wait=_ag_chips_plan(end_keys), wait_sems=sems_e,
                                  start=_ag_sibling_plan(end_keys), after=yb)
    br_a = _matmul(ya, wa, "nn", F32, 1024, 1024, 1024, "branch_sc", dep=tok)
    br_b = _matmul(yb, wb, "nn", F32, 1024, 1024, 2048, "branch_ssm")
    merged = _merge_fwd(proj, b_gate, br_a, br_b)
    x1 = _matmul(merged, wo, "nn", F32, 1024, 1024, 1024, "out_proj", epi="res", extra=x0)
    h2 = _rms_fwd(x1, norm_mlp, "rms_mlp")
    a1, rl = _matmul(h2, w1, "nn", BF16, 1024, 1024, 1024, "mlp1", epi="relu2", n_outer=True)
    ge, _, _ = _split_call("ag_end_done", ge, wait=_ag_sibling_plan(end_keys), wait_sems=sems_e, after=a1)
    w2 = ge["w2"]
    x2 = _matmul(rl, w2, "nn", F32, 512, 1024, 4096, "mlp2", epi="res", extra=x1)
    dx2, g_nf, loss8 = _final(x2, norm_final, tgt)

    da = _matmul(dx2, w2, "nt", BF16, 1024, 1024, 1024, "mlp2_dx", epi="drelu", extra=a1, n_outer=True)
    g_w2 = _matmul(rl, dx2, "tn", BF16, 1024, 1024, 2048, "mlp2_dw")
    g_w1 = _matmul(h2, da, "tn", BF16, 1024, 1024, 2048, "mlp1_dw")
    dh2 = _matmul(da, w1, "nt", F32, 512, 1024, 4096, "mlp1_dx")
    dx1, g_nmlp = _rms_bwd(dh2, x1, norm_mlp, dx2, "rms_mlp_bwd")
    dmerged = _matmul(dx1, wo, "nt", F32, 1024, 1024, 1024, "out_proj_dx")
    g_wo = _matmul(merged, dx1, "tn", BF16, 1024, 1024, 2048, "out_proj_dw")
    dproj = lax.empty((L, NCW), BF16)
    dbr, dproj, g_bg = _merge_bwd(dmerged, proj, b_gate, br_a, br_b, dproj)
    dya = _matmul(dbr[0], wa, "nt", F32, 1024, 1024, 1024, "branch_sc_dx")
    g_wa = _matmul(ya, dbr[0], "tn", BF16, 1024, 1024, 2048, "branch_sc_dw")
    dproj, g_scw = _sc_bwd(dya, proj, sc_w_full, dproj)
    dyb = _matmul(dbr[1], wb, "nt", F32, 1024, 1024, 1024, "branch_ssm_dx", n_outer=True)
    g_wb = _matmul(yb, dbr[1], "tn", BF16, 1024, 1024, 2048, "branch_ssm_dw")
    rs_a = _ReduceScatter([1, 2, 3, 4, 5], {1: g_w1, 2: g_w2, 3: g_wa, 4: g_wb, 5: g_wo}, idx, "a")
    dy, dproj, g_snw = _gnorm_bwd(_tie(dyb, rs_a.token, "tie_dyb"), y, proj, ssm_norm_w, dproj)
    tok = rs_a.chips(after=dy)
    dxs, dbm, dcm, ddt_g, st = _ssd_bwd(xbc, dt4, cs4, sg4, vecg, s_all, _tie(dy, tok, "tie_dy"))
    dproj, gx1 = _ssm_conv_bwd(dxs, proj, cw4, dproj, 0, "ssm_conv_bwd_x")
    dproj, gx2 = _ssm_conv_bwd(dbm, proj, cw4, dproj, INNER, "ssm_conv_bwd_b")
    dproj, gx3 = _ssm_conv_bwd(dcm, proj, cw4, dproj, INNER + NG * NS, "ssm_conv_bwd_c")
    g_cw4 = jnp.concatenate([gx1, gx2, gx3], axis=1)
    dproj, g_dtb = _dt_bwd(ddt_g, dproj)
    small = {"b_gate": g_bg[0], "ssm_conv_b": g_cw4[4], "dt_bias": g_dtb[0, :NH],
             "A_log": st[:, 0, :4], "D_skip": st[:, 1, :4], "ssm_norm_w": g_snw[0], "norm_mlp": g_nmlp[0],
             "norm_final": g_nf[0], "sc_conv_w": g_scw[0:3], "ssm_conv_w": g_cw4[0:4], "loss": loss8[0, 0:1]}
    me = 4 * xi + 2 * yi + ci
    sm8 = lax.dynamic_update_slice(jnp.zeros((8, SMALL_ROWS, 128), F32), _pack(small, _SMALL, SMALL_ROWS)[None], (me, 0, 0))
    sm_arr, sm_sems, tok = _split_call("small_start", {"sm": sm8}, start=_all8_plan("sm"))
    g_wc = _matmul(dproj, h, "tn", BF16, 1280, 1024, 2048, "in_proj_dw", dep=tok)
    rs_b = _ReduceScatter([0], {0: g_wc}, idx, "b")
    tok = rs_a.share(after=rs_b.token)
    tok = rs_b.chips(after=tok)
    dh = _matmul(dproj, wc, "nn", F32, 512, 1024, 5760, "in_proj_dx", dep=tok)
    grad_x, g_nm = _rms_bwd(dh, x0, norm_mix, dx1, "rms_mix_bwd")
    nm8 = lax.dynamic_update_slice(jnp.zeros((8, 8, 128), F32), g_nm[0].reshape(1, 8, 128), (me, 0, 0))
    nm_arr, nm_sems, tok = _split_call("norm_mix_start", {"nm": nm8}, start=_all8_plan("nm"))
    sm_arr, _, _ = _split_call("small_wait", sm_arr, wait=_all8_plan("sm"), wait_sems=sm_sems, after=tok)
    small_sum = _sum8(sm_arr["sm"])
    gs = _unpack(small_sum, _SMALL)
    red = rs_a.result(after=tok)
    big = {"w_mlp1": red[1], "w_mlp2": red[2], "w_branch_sc": red[3], "w_branch_ssm": red[4], "w_out": red[5]}

    given = dict(norm_mix=norm_mix, w_in=w_in, b_gate=b_gate, sc_conv_w=sc_conv_w, ssm_conv_w=ssm_conv_w, ssm_conv_b=ssm_conv_b, dt_bias=dt_bias, A_log=A_log, D_skip=D_skip, ssm_norm_w=ssm_norm_w, w_branch_sc=w_branch_sc, w_branch_ssm=w_branch_ssm, w_out=w_out, norm_mlp=norm_mlp, w_mlp1=w_mlp1, w_mlp2=w_mlp2, norm_final=norm_final,
                 m_norm_mix=m_norm_mix, m_w_in=m_w_in, m_b_gate=m_b_gate, m_sc_conv_w=m_sc_conv_w, m_ssm_conv_w=m_ssm_conv_w, m_ssm_conv_b=m_ssm_conv_b, m_dt_bias=m_dt_bias, m_A_log=m_A_log, m_D_skip=m_D_skip, m_ssm_norm_w=m_ssm_norm_w, m_w_branch_sc=m_w_branch_sc, m_w_branch_ssm=m_w_branch_ssm, m_w_out=m_w_out, m_norm_mlp=m_norm_mlp, m_w_mlp1=m_w_mlp1, m_w_mlp2=m_w_mlp2, m_norm_final=m_norm_final,
                 v_norm_mix=v_norm_mix, v_w_in=v_w_in, v_b_gate=v_b_gate, v_sc_conv_w=v_sc_conv_w, v_ssm_conv_w=v_ssm_conv_w, v_ssm_conv_b=v_ssm_conv_b, v_dt_bias=v_dt_bias, v_A_log=v_A_log, v_D_skip=v_D_skip, v_ssm_norm_w=v_ssm_norm_w, v_w_branch_sc=v_w_branch_sc, v_w_branch_ssm=v_w_branch_ssm, v_w_out=v_w_out, v_norm_mlp=v_norm_mlp, v_w_mlp1=v_w_mlp1, v_w_mlp2=v_w_mlp2, v_norm_final=v_norm_final)
    order = ["norm_mix", "w_in", "b_gate", "sc_conv_w", "ssm_conv_w", "ssm_conv_b", "dt_bias", "A_log", "D_skip",
             "ssm_norm_w", "w_branch_sc", "w_branch_ssm", "w_out", "norm_mlp", "w_mlp1", "w_mlp2", "norm_final"]
    grad, delta, new_m, new_v = {}, {}, {}, {}
    for n in big:
        delta[n], new_m[n], new_v[n], grad[n] = _adamw(given[n], big[n], given["m_" + n], given["v_" + n],
                                                       "adamw_" + n, copy_g=True)
    big["w_in"] = None
    grad_small = {n: gs[n].reshape(given[n].shape) for n in order
                  if n not in big and n not in ("sc_conv_w", "ssm_conv_w", "norm_mix")}
    grad_small["sc_conv_w"] = lax.dynamic_slice(gs["sc_conv_w"].reshape(3, D), (0, 256 * s), (3, 256))
    grad_small["ssm_conv_w"] = lax.dynamic_slice(gs["ssm_conv_w"].reshape(4, XBC), (0, 1024 * s), (4, 1024))
    table = [(n, int(grad_small[n].size)) for n in grad_small]
    rows = 136
    pk = lambda d: _pack(d, table, rows)
    ds_, ms_, vs_ = _adamw(pk({n: given[n] for n in grad_small}), pk(grad_small), pk({n: given["m_" + n] for n in grad_small}),
                           pk({n: given["v_" + n] for n in grad_small}), "adamw_small", tr=rows)
    ds_, ms_, vs_ = _unpack(ds_, table), _unpack(ms_, table), _unpack(vs_, table)
    for n in grad_small:
        shp = given[n].shape
        grad[n] = grad_small[n]
        delta[n], new_m[n], new_v[n] = ds_[n].reshape(shp), ms_[n].reshape(shp), vs_[n].reshape(shp)

    done = [new_v[n] for n in ("w_mlp1", "w_mlp2", "w_branch_sc", "w_branch_ssm", "w_out")] + [vs_["b_gate"]]
    tok = rs_b.share(after=done)
    gp = rs_b.result(after=tok)[0]
    offs = jnp.where(s == 3, jnp.array([24, -8, 744, 2072, -8], jnp.int32),
                     jnp.stack([8 * s, 8 * s, 0 * s, 8 * s, 8 * s]).astype(jnp.int32))
    wt_args = (w_in.T, gp, m_w_in.T, v_w_in.T, offs)
    nmain = W_SHARD // 256
    res = _adamw_w_in(*wt_args, "adamw_w_in", 0, 256, nmain, (0, 1))
    res = _adamw_w_in(*wt_args, "adamw_w_in_dt", 744, 32, 1, (3,), into=res)
    dt_, mt_, vt_, gwt = _adamw_w_in(*wt_args, "adamw_w_in_tail", 256 * nmain, 8, 1, (4,), into=res)
    grad["w_in"], delta["w_in"], new_m["w_in"], new_v["w_in"] = gwt.T, dt_.T, mt_.T, vt_.T
    nm_arr, _, _ = _split_call("norm_mix_wait", nm_arr, wait=_all8_plan("nm"), wait_sems=nm_sems, after=tok)
    g8 = _sum8(nm_arr["nm"], "norm_mix_sum")
    r8 = lambda a: a.reshape(8, 128)
    d8, m8, v8 = _adamw(r8(norm_mix), g8, r8(m_norm_mix), r8(v_norm_mix), "adamw_norm_mix", tr=8)
    grad["norm_mix"], delta["norm_mix"] = g8.reshape(D), d8.reshape(D)
    new_m["norm_mix"], new_v["norm_mix"] = m8.reshape(D), v8.reshape(D)

    loss = gs["loss"].reshape(())
    return (loss, grad_x.reshape(1, L, D), *[grad[n] for n in order], *[delta[n] for n in order],
            *[new_m[n] for n in order], *[new_v[n] for n in order])
```

```python
import functools

import jax
import jax.numpy as jnp
from jax import lax
from jax.experimental import pallas as pl
from jax.experimental.pallas import tpu as pltpu

F32 = jnp.float32
BF16 = jnp.bfloat16
MESH = pl.DeviceIdType.MESH
HBM = pltpu.HBM

D = 1024
INNER = 2048
HD = 64
NH = 32
NG = 8
NS = 128
Q = 128
GPS = 4
XBC = 4096
DFF = 4096
EPS = 1e-6
W_SHARD = 2824
NCW = 11520
PIECE = 3072
PMAIN = 2816
C_Z, C_XBC, C_GATE, C_DT = 3072, 5120, 9216, 11264
SMALL_ROWS = 256
VMEM_LIMIT = 56 * 1024 * 1024

ADAM_LR, ADAM_B1, ADAM_B2, ADAM_EPS, ADAM_WD, ADAM_STEP = 0.001, 0.9, 0.999, 1e-08, 0.01, 10


def _cp(sem=None, vmem=VMEM_LIMIT):
    return pltpu.CompilerParams(dimension_semantics=sem, vmem_limit_bytes=vmem)


def _sigmoid(v):
    return 1.0 / (1.0 + jnp.exp(-v))


_DIMS = {"nn": (((1,), (0,)), ((), ())), "nt": (((1,), (1,)), ((), ())), "tn": (((0,), (0,)), ((), ()))}


def _matmul(a, b, mode, out_dtype, tm, tn, tk, name, epi=None, extra=None, n_outer=False, dep=None):
    if mode == "tn":
        K, M = a.shape
    else:
        M, K = a.shape
    N = b.shape[0] if mode == "nt" else b.shape[1]
    tm, tn, tk = min(tm, M), min(tn, N), min(tk, K)
    assert M % tm == 0 and N % tn == 0 and K % tk == 0, (name, M, N, K, tm, tn, tk)
    nm, nn, nk = M // tm, N // tn, K // tk
    dims = _DIMS[mode]

    def ij(p0, p1):
        return (p1, p0) if n_outer else (p0, p1)

    if mode == "tn":
        a_spec = pl.BlockSpec((tk, tm), lambda p0, p1, k: (k, ij(p0, p1)[0]))
    else:
        a_spec = pl.BlockSpec((tm, tk), lambda p0, p1, k: (ij(p0, p1)[0], k))
    if mode == "nt":
        b_spec = pl.BlockSpec((tn, tk), lambda p0, p1, k: (ij(p0, p1)[1], k))
    else:
        b_spec = pl.BlockSpec((tk, tn), lambda p0, p1, k: (k, ij(p0, p1)[1]))
    o_spec = pl.BlockSpec((tm, tn), lambda p0, p1, k: ij(p0, p1))
    in_specs = [a_spec, b_spec]
    args = [a, b]
    if epi in ("res", "drelu"):
        in_specs.append(o_spec)
        args.append(extra)
    if dep is not None:
        in_specs.append(pl.BlockSpec(memory_space=pl.ANY))
        args.append(dep)
    n_in = len(args)
    if epi == "relu2":
        out_shape = (jax.ShapeDtypeStruct((M, N), out_dtype), jax.ShapeDtypeStruct((M, N), BF16))
        out_specs = (o_spec, o_spec)
    else:
        out_shape = jax.ShapeDtypeStruct((M, N), out_dtype)
        out_specs = o_spec

    def kern(*refs):
        a_ref, b_ref = refs[0], refs[1]
        e_ref = refs[2] if epi in ("res", "drelu") else None
        acc = refs[-1]
        outs = refs[n_in:-1] if nk > 1 else refs[n_in:]
        k = pl.program_id(2)

        def product():
            return lax.dot_general(a_ref[...].astype(BF16), b_ref[...].astype(BF16), dims, preferred_element_type=F32)

        def finish(r):
            if epi is None:
                outs[0][...] = r.astype(out_dtype)
            elif epi == "res":
                outs[0][...] = (r + e_ref[...]).astype(out_dtype)
            elif epi == "relu2":
                outs[0][...] = r.astype(out_dtype)
                t = jnp.maximum(r, 0.0)
                outs[1][...] = (t * t).astype(BF16)
            else:
                outs[0][...] = (r * (2.0 * jnp.maximum(e_ref[...].astype(F32), 0.0))).astype(out_dtype)

        if nk == 1:
            finish(product())
        else:
            @pl.when(k == 0)
            def _():
                acc[...] = jnp.zeros_like(acc)

            acc[...] += product()

            @pl.when(k == nk - 1)
            def _():
                finish(acc[...])

    grid = (nn, nm, nk) if n_outer else (nm, nn, nk)
    return pl.pallas_call(
        kern, grid=grid, in_specs=in_specs, out_specs=out_specs, out_shape=out_shape,
        scratch_shapes=[pltpu.VMEM((tm, tn), F32)] if nk > 1 else [], name=name,
        compiler_params=_cp(("parallel", "parallel", "arbitrary")),
    )(*args)


def _rms_fwd(x, w, name, tl=256, dep=None):
    L = x.shape[0]

    def kern(x_ref, w_ref, *rest):
        o_ref = rest[-1]
        xv = x_ref[...]
        r = lax.rsqrt(jnp.mean(xv * xv, axis=-1, keepdims=True) + EPS)
        o_ref[...] = ((xv * r) * w_ref[...]).astype(BF16)

    row = pl.BlockSpec((tl, D), lambda i: (i, 0))
    deps = [] if dep is None else [dep]
    return pl.pallas_call(
        kern, grid=(L // tl,),
        in_specs=[row, pl.BlockSpec((1, D), lambda i: (0, 0))] + [pl.BlockSpec(memory_space=pl.ANY)] * len(deps),
        out_specs=row, out_shape=jax.ShapeDtypeStruct((L, D), BF16), name=name, compiler_params=_cp(("parallel",)),
    )(x, w.reshape(1, D), *deps)


def _rms_bwd(dy, x, w, res, name, tl=256, dep=None):
    L = x.shape[0]
    deps = [] if dep is None else [dep]

    def kern(dy_ref, x_ref, w_ref, res_ref, *rest):
        dx_ref, gw_ref = rest[-2:]
        @pl.when(pl.program_id(0) == 0)
        def _():
            gw_ref[...] = jnp.zeros_like(gw_ref)

        xv = x_ref[...]
        dyv = dy_ref[...]
        r = lax.rsqrt(jnp.mean(xv * xv, axis=-1, keepdims=True) + EPS)
        xn = xv * r
        gw_ref[...] += jnp.broadcast_to(jnp.sum(dyv * xn, axis=0, keepdims=True), (8, D))
        dxn = dyv * w_ref[...]
        dx_ref[...] = res_ref[...] + r * (dxn - xn * jnp.mean(dxn * xn, axis=-1, keepdims=True))

    row = pl.BlockSpec((tl, D), lambda i: (i, 0))
    return pl.pallas_call(
        kern, grid=(L // tl,),
        in_specs=[row, row, pl.BlockSpec((1, D), lambda i: (0, 0)), row] + [pl.BlockSpec(memory_space=pl.ANY)] * len(deps),
        out_specs=(row, pl.BlockSpec((8, D), lambda i: (0, 0))),
        out_shape=(jax.ShapeDtypeStruct((L, D), F32), jax.ShapeDtypeStruct((8, D), F32)),
        name=name, compiler_params=_cp(("arbitrary",)),
    )(dy, x, w.reshape(1, D), res, *deps)


def _final(x2, w, tgt, tl=256):
    L = x2.shape[0]

    def kern(x_ref, w_ref, t_ref, dx_ref, gw_ref, loss_ref):
        @pl.when(pl.program_id(0) == 0)
        def _():
            gw_ref[...] = jnp.zeros_like(gw_ref)
            loss_ref[...] = jnp.zeros_like(loss_ref)

        xv = x_ref[...]
        r = lax.rsqrt(jnp.mean(xv * xv, axis=-1, keepdims=True) + EPS)
        xn = xv * r
        e = xn * w_ref[...] - t_ref[...]
        per_tok = jnp.mean(e * e, axis=-1, keepdims=True)
        loss_ref[...] += 0.5 * jnp.sum(per_tok)
        dyv = e * (1.0 / D)
        gw_ref[...] += jnp.broadcast_to(jnp.sum(dyv * xn, axis=0, keepdims=True), (8, D))
        dxn = dyv * w_ref[...]
        dx_ref[...] = r * (dxn - xn * jnp.mean(dxn * xn, axis=-1, keepdims=True))

    row = pl.BlockSpec((tl, D), lambda i: (i, 0))
    return pl.pallas_call(
        kern, grid=(L // tl,), in_specs=[row, pl.BlockSpec((1, D), lambda i: (0, 0)), row],
        out_specs=(row, pl.BlockSpec((8, D), lambda i: (0, 0)), pl.BlockSpec((8, 128), lambda i: (0, 0))),
        out_shape=(jax.ShapeDtypeStruct((L, D), F32), jax.ShapeDtypeStruct((8, D), F32),
                   jax.ShapeDtypeStruct((8, 128), F32)),
        name="final_norm_loss", compiler_params=_cp(("arbitrary",)),
    )(x2, w.reshape(1, D), tgt)


def _down(v, k):
    if k == 0:
        return v
    t = lax.broadcasted_iota(jnp.int32, v.shape, 0)
    return jnp.where(t >= k, pltpu.roll(v, k, axis=0), 0.0)


def _up(v, k):
    if k == 0:
        return v
    n = v.shape[0]
    t = lax.broadcasted_iota(jnp.int32, v.shape, 0)
    return jnp.where(t < n - k, pltpu.roll(v, n - k, axis=0), 0.0)


TW = 256


def _sc_fwd(proj, cw):
    L = proj.shape[0]
    nb = D // TW

    def kern(b_ref, c_ref, x_ref, w_ref, o_ref):
        u = c_ref[...].astype(F32) * x_ref[...].astype(F32)
        w = w_ref[...]
        cv = w[0:1] * _down(u, 2) + w[1:2] * _down(u, 1) + w[2:3] * u
        o_ref[...] = (b_ref[...].astype(F32) * cv).astype(BF16)

    col = lambda off: pl.BlockSpec((L, TW), lambda j: (0, off + j))
    return pl.pallas_call(
        kern, grid=(nb,), in_specs=[col(0), col(nb), col(2 * nb), pl.BlockSpec((8, TW), lambda j: (0, j))],
        out_specs=pl.BlockSpec((L, TW), lambda j: (0, j)), out_shape=jax.ShapeDtypeStruct((L, D), BF16),
        name="sc_fwd", compiler_params=_cp(("parallel",)),
    )(proj, proj, proj, cw)


def _sc_bwd(dya, proj, cw, dproj):
    L = proj.shape[0]
    nb = D // TW

    def kern(d_ref, b_ref, c_ref, x_ref, w_ref, _, dp_ref, gw_ref, keep):
        sec = pl.program_id(1)

        @pl.when(sec == 0)
        def _():
            cs, xs, dyv = c_ref[...].astype(F32), x_ref[...].astype(F32), d_ref[...]
            w = w_ref[...]
            u = cs * xs
            u1, u2 = _down(u, 1), _down(u, 2)
            cv = w[0:1] * u2 + w[1:2] * u1 + w[2:3] * u
            dcv = dyv * b_ref[...].astype(F32)
            du = w[2:3] * dcv + w[1:2] * _up(dcv, 1) + w[0:1] * _up(dcv, 2)
            g0 = jnp.sum(dcv * u2, axis=0, keepdims=True)
            g1 = jnp.sum(dcv * u1, axis=0, keepdims=True)
            g2 = jnp.sum(dcv * u, axis=0, keepdims=True)
            row = lax.broadcasted_iota(jnp.int32, (8, TW), 0)
            gw_ref[...] = jnp.where(row == 0, g0, jnp.where(row == 1, g1, jnp.where(row == 2, g2, 0.0)))
            dp_ref[...] = (dyv * cv).astype(BF16)
            keep[0] = (du * xs).astype(BF16)
            keep[1] = (du * cs).astype(BF16)

        @pl.when(sec > 0)
        def _():
            dp_ref[...] = keep[sec - 1]

    col = lambda off: pl.BlockSpec((L, TW), lambda j, s: (0, off + j))
    return pl.pallas_call(
        kern, grid=(nb, 3),
        in_specs=[col(0), col(0), col(nb), col(2 * nb), pl.BlockSpec((8, TW), lambda j, s: (0, j)),
                  pl.BlockSpec(memory_space=pl.ANY)],
        out_specs=(pl.BlockSpec((L, TW), lambda j, s: (0, s * nb + j)), pl.BlockSpec((8, TW), lambda j, s: (0, j))),
        out_shape=(jax.ShapeDtypeStruct(dproj.shape, BF16), jax.ShapeDtypeStruct((8, D), F32)),
        scratch_shapes=[pltpu.VMEM((2, L, TW), BF16)],
        input_output_aliases={5: 0}, name="sc_bwd", compiler_params=_cp(("parallel", "arbitrary")),
    )(dya, proj, proj, proj, cw, dproj)


def _ssm_conv_fwd(proj, cw4):
    L = proj.shape[0]
    off = C_XBC // TW

    def kern(r_ref, w_ref, o_ref):
        raw = r_ref[...].astype(F32)
        w = w_ref[...]
        c4 = w[0:1] * _down(raw, 3) + w[1:2] * _down(raw, 2) + w[2:3] * _down(raw, 1) + w[3:4] * raw + w[4:5]
        o_ref[...] = c4 * _sigmoid(c4)

    return pl.pallas_call(
        kern, grid=(XBC // TW,),
        in_specs=[pl.BlockSpec((L, TW), lambda j: (0, off + j)), pl.BlockSpec((8, TW), lambda j: (0, j))],
        out_specs=pl.BlockSpec((L, TW), lambda j: (0, j)), out_shape=jax.ShapeDtypeStruct((L, XBC), F32),
        name="ssm_conv_fwd", compiler_params=_cp(("parallel",)),
    )(proj, cw4)


def _ssm_conv_bwd(dx, proj, cw4, dproj, col0, name):
    L, width = dx.shape
    off_p = (C_XBC + col0) // TW
    off_w = col0 // TW

    def kern(d_ref, r_ref, w_ref, _, dp_ref, gw_ref):
        raw = r_ref[...].astype(F32)
        w = w_ref[...]
        r1, r2, r3 = _down(raw, 1), _down(raw, 2), _down(raw, 3)
        c4 = w[0:1] * r3 + w[1:2] * r2 + w[2:3] * r1 + w[3:4] * raw + w[4:5]
        sg = _sigmoid(c4)
        dc4 = d_ref[...] * (sg * (1.0 + c4 * (1.0 - sg)))
        draw = w[3:4] * dc4 + w[2:3] * _up(dc4, 1) + w[1:2] * _up(dc4, 2) + w[0:1] * _up(dc4, 3)
        dp_ref[...] = draw.astype(BF16)
        gs = [jnp.sum(dc4 * r3, axis=0, keepdims=True), jnp.sum(dc4 * r2, axis=0, keepdims=True),
              jnp.sum(dc4 * r1, axis=0, keepdims=True), jnp.sum(dc4 * raw, axis=0, keepdims=True),
              jnp.sum(dc4, axis=0, keepdims=True)]
        row = lax.broadcasted_iota(jnp.int32, (8, TW), 0)
        acc = jnp.zeros((8, TW), F32)
        for k, gk in enumerate(gs):
            acc = jnp.where(row == k, gk, acc)
        gw_ref[...] = acc

    return pl.pallas_call(
        kern, grid=(width // TW,),
        in_specs=[pl.BlockSpec((L, TW), lambda j: (0, j)), pl.BlockSpec((L, TW), lambda j: (0, off_p + j)),
                  pl.BlockSpec((8, TW), lambda j: (0, off_w + j)), pl.BlockSpec(memory_space=pl.ANY)],
        out_specs=(pl.BlockSpec((L, TW), lambda j: (0, off_p + j)), pl.BlockSpec((8, TW), lambda j: (0, j))),
        out_shape=(jax.ShapeDtypeStruct(dproj.shape, BF16), jax.ShapeDtypeStruct((8, width), F32)),
        input_output_aliases={3: 0}, name=name, compiler_params=_cp(("arbitrary",)),
    )(dx, proj, cw4, dproj)


def _split3(v):
    h1 = v.astype(BF16)
    r1 = v - h1.astype(F32)
    h2 = r1.astype(BF16)
    h3 = (r1 - h2.astype(F32)).astype(BF16)
    return h1, h2, h3


def _dot01(m01, v, dims=_DIMS["nn"], m_left=True, terms=3):
    out = None
    for part in _split3(v)[:terms]:
        ops = (m01, part) if m_left else (part, m01)
        t = lax.dot_general(ops[0], ops[1], dims, preferred_element_type=F32)
        out = t if out is None else out + t
    return out


def _bdot(a, b, mode="nn"):
    return lax.dot_general(a.astype(BF16), b.astype(BF16), _DIMS[mode], preferred_element_type=F32)


def _softplus(v):
    return jnp.maximum(v, 0.0) + jnp.log1p(jnp.exp(-jnp.abs(v)))


def _dt_prep(proj, vec):
    L = proj.shape[0]

    def kern(p_ref, v_ref, dt_ref, cs_ref, sg_ref):
        v = v_ref[...]
        pre = p_ref[:, 0:128] + v[0:1]
        dt = _softplus(pre)
        da = dt * (-jnp.exp(v[1:2]))
        ii = lax.broadcasted_iota(jnp.int32, (Q, Q), 0)
        jj = lax.broadcasted_iota(jnp.int32, (Q, Q), 1)
        ltri = (jj <= ii).astype(BF16)
        lane = lax.broadcasted_iota(jnp.int32, (Q, 128), 1)
        for val, ref in ((dt, dt_ref), (_dot01(ltri, da), cs_ref), (_sigmoid(pre), sg_ref)):
            for g in range(NG):
                moved = val if g == 0 else pltpu.roll(val, 128 - 4 * g, axis=1)
                ref[g] = jnp.where(lane < 4, moved, 0.0)

    blk = pl.BlockSpec((NG, Q, 128), lambda c: (0, c, 0))
    return pl.pallas_call(
        kern, grid=(L // Q,),
        in_specs=[pl.BlockSpec((Q, 256), lambda c: (c, 0)), pl.BlockSpec((8, 128), lambda c: (0, 0))],
        out_specs=(blk, blk, blk),
        out_shape=(jax.ShapeDtypeStruct((NG, L, 128), F32),) * 3,
        name="dt_prep", compiler_params=_cp(("parallel",)),
    )(proj, vec)


def _head_masks():
    lane = lax.broadcasted_iota(jnp.int32, (1, 4 * HD), 1)
    return [((lane >= HD * j) & (lane < HD * (j + 1))) for j in range(4)]


def _expand4(v4, masks):
    R = v4.shape[0]
    out = jnp.zeros((R, 4 * HD), F32)
    for j in range(4):
        out = jnp.where(masks[j], jnp.broadcast_to(v4[:, j:j + 1], (R, 4 * HD)), out)
    return out


def _decay_matrix(cs_col, tri):
    colb = jnp.broadcast_to(cs_col, (Q, Q))
    return jnp.exp(jnp.where(tri, colb - colb.T, -jnp.inf))


def _ssd_fwd(xbc, dt4, cs4, vecg):
    L = xbc.shape[0]
    nc = L // Q

    def kern(x_ref, b_ref, c_ref, dt_ref, cs_ref, v_ref, y_ref, s_ref, S):
        c = pl.program_id(1)

        @pl.when(c == 0)
        def _():
            S[...] = jnp.zeros_like(S)

        masks = _head_masks()
        ii = lax.broadcasted_iota(jnp.int32, (Q, Q), 0)
        jj = lax.broadcasted_iota(jnp.int32, (Q, Q), 1)
        tri = jj <= ii
        for gi in range(GPS):
            xs, ns = slice(256 * gi, 256 * (gi + 1)), slice(NS * gi, NS * (gi + 1))
            dt4v, cs4v = dt_ref[gi], cs_ref[gi]
            dt_b, cs_b = _expand4(dt4v, masks), _expand4(cs4v, masks)
            d_b = _expand4(v_ref[gi], masks)[1:2]
            cs_last = cs_b[Q - 1:Q, :]
            x4, bm, cm = x_ref[:, xs], b_ref[:, ns], c_ref[:, ns]
            xdt = x4 * dt_b
            gm = _bdot(cm, bm, "nt")
            s4 = S[gi]
            s_ref[gi, 0] = s4
            y = _bdot(cm, s4) * jnp.exp(cs_b) + d_b * x4
            m_all = jnp.concatenate([(gm * _decay_matrix(cs4v[:, j:j + 1], tri)).astype(BF16) for j in range(4)], axis=0)
            yd = _bdot(m_all, xdt)
            for j in range(4):
                y = y + jnp.where(masks[j], yd[Q * j:Q * (j + 1)], 0.0)
            y_ref[:, xs] = y
            S[gi] = jnp.exp(cs_last) * s4 + _bdot(bm, xdt * jnp.exp(cs_last - cs_b), "tn")

    sc = pl.BlockSpec((GPS, Q, 128), lambda g, c: (g, c, 0))
    bw = NS * GPS
    return pl.pallas_call(
        kern, grid=(NG // GPS, nc),
        in_specs=[pl.BlockSpec((Q, 256 * GPS), lambda g, c: (c, g)),
                  pl.BlockSpec((Q, bw), lambda g, c: (c, INNER // bw + g)),
                  pl.BlockSpec((Q, bw), lambda g, c: (c, (INNER + NG * NS) // bw + g)),
                  sc, sc, pl.BlockSpec((GPS, 8, 128), lambda g, c: (g, 0, 0))],
        out_specs=(pl.BlockSpec((Q, 256 * GPS), lambda g, c: (c, g)),
                   pl.BlockSpec((GPS, 1, NS, 256), lambda g, c: (g, c, 0, 0))),
        out_shape=(jax.ShapeDtypeStruct((L, INNER), F32), jax.ShapeDtypeStruct((NG, nc, NS, 256), F32)),
        scratch_shapes=[pltpu.VMEM((GPS, NS, 256), F32)], name="ssd_fwd",
        compiler_params=_cp(("parallel", "arbitrary")),
    )(xbc, xbc, xbc, dt4, cs4, vecg)


def _ssd_bwd(xbc, dt4, cs4, sg4, vecg, s_all, dy):
    L = xbc.shape[0]
    nc = L // Q

    def kern(x_ref, b_ref, c_ref, dt_ref, cs_ref, sg_ref, v_ref, s_ref, dy_ref,
             dx_ref, db_ref, dc_ref, ddt_ref, st_ref, dS):
        cc = pl.program_id(1)

        @pl.when(cc == 0)
        def _():
            dS[...] = jnp.zeros_like(dS)
            st_ref[...] = jnp.zeros_like(st_ref)

        masks = _head_masks()
        ii = lax.broadcasted_iota(jnp.int32, (Q, Q), 0)
        jj = lax.broadcasted_iota(jnp.int32, (Q, Q), 1)
        tri = jj <= ii
        utri = (jj >= ii).astype(BF16)
        hsel = ((lax.broadcasted_iota(jnp.int32, (4 * HD, 128), 0) // HD)
                == lax.broadcasted_iota(jnp.int32, (4 * HD, 128), 1)).astype(BF16)
        hrow = ((lax.broadcasted_iota(jnp.int32, (4 * Q, 128), 0) // Q)
                == lax.broadcasted_iota(jnp.int32, (4 * Q, 128), 1)).astype(BF16)
        ones_q = jnp.ones((Q, 128), BF16)
        lane128 = lax.broadcasted_iota(jnp.int32, (Q, 128), 1)

        for gi in range(GPS):
            xs, ns = slice(256 * gi, 256 * (gi + 1)), slice(NS * gi, NS * (gi + 1))
            dt4v, cs4v, sg4v = dt_ref[gi], cs_ref[gi], sg_ref[gi]
            dt_b, cs_b = _expand4(dt4v, masks), _expand4(cs4v, masks)
            vv = _expand4(v_ref[gi], masks)
            a_b = -jnp.exp(vv[0:1])
            d_b = vv[1:2]
            a4 = -jnp.exp(v_ref[gi][0:1, :])
            cs_last = cs_b[Q - 1:Q, :]
            ecs = jnp.exp(cs_b)
            decay = jnp.exp(cs_last - cs_b)
            elast = jnp.exp(cs_last)
            x4, bm, cm, dyv = x_ref[:, xs], b_ref[:, ns], c_ref[:, ns], dy_ref[:, xs]
            s4 = s_ref[gi, 0]
            dsn = dS[gi]
            xdt = x4 * dt_b
            gm = _bdot(cm, bm, "nt")
            dye = dyv * ecs
            yoff = ecs * _bdot(cm, s4)
            t4 = _bdot(bm, dsn) * decay
            lms, mhs = [], []
            for j in range(4):
                colb = jnp.broadcast_to(cs4v[:, j:j + 1], (Q, Q))
                lms.append(jnp.exp(jnp.where(tri, colb - colb.T, -jnp.inf)))
                mhs.append(gm * lms[j])
            m_all = jnp.concatenate([m.astype(BF16) for m in mhs], axis=0)
            dy_m = jnp.concatenate([jnp.where(masks[j], dyv, 0.0).astype(BF16) for j in range(4)], axis=0)
            dxdt = t4 + _bdot(m_all, dy_m, "tn")
            dm_all = _bdot(dy_m, xdt, "nt")
            dg = jnp.zeros((Q, Q), F32)
            for j in range(4):
                dg = dg + dm_all[Q * j:Q * (j + 1)] * lms[j]
            e_all = dm_all * jnp.concatenate(mhs, axis=0)
            rsum = _dot01(ones_q, e_all, m_left=False, terms=2)
            da4 = -_dot01(hrow, e_all, _DIMS["tn"], m_left=False, terms=2)
            for j in range(4):
                da4 = da4 + jnp.where(lane128 == j, rsum[Q * j:Q * (j + 1)], 0.0)
            xt = xdt * t4
            tail = jnp.sum(xt, axis=0, keepdims=True) + elast * jnp.sum(s4 * dsn, axis=0, keepdims=True)
            gd_raw = jnp.sum(dyv * x4, axis=0, keepdims=True)
            stacked = jnp.concatenate([dyv * yoff - xt, dxdt * x4, jnp.broadcast_to(tail, (8, 4 * HD)),
                                       jnp.broadcast_to(gd_raw, (8, 4 * HD))], axis=0)
            seg = _dot01(hsel, stacked, m_left=False, terms=2)
            dda4 = _dot01(utri, da4 + seg[0:Q], terms=2) + seg[2 * Q:2 * Q + 1]
            ddt_ref[gi] = (dda4 * a4 + seg[Q:2 * Q]) * sg4v
            ga = jnp.sum(dda4 * dt4v * a4, axis=0, keepdims=True)
            row = lax.broadcasted_iota(jnp.int32, (8, 128), 0)
            st_ref[gi] += jnp.where(row == 0, ga, jnp.where(row == 1, seg[2 * Q + 8:2 * Q + 9], 0.0))
            dx_ref[:, xs] = d_b * dyv + dxdt * dt_b
            dc_ref[:, ns] = _bdot(dg, bm) + _bdot(dye, s4, "nt")
            db_ref[:, ns] = _bdot(dg, cm, "tn") + _bdot(xdt * decay, dsn, "nt")
            dS[gi] = elast * dsn + _bdot(cm, dye, "tn")

    rv = lambda c: nc - 1 - c
    sc = pl.BlockSpec((GPS, Q, 128), lambda g, c: (g, rv(c), 0))
    bw = NS * GPS
    return pl.pallas_call(
        kern, grid=(NG // GPS, nc),
        in_specs=[pl.BlockSpec((Q, 256 * GPS), lambda g, c: (rv(c), g)),
                  pl.BlockSpec((Q, bw), lambda g, c: (rv(c), INNER // bw + g)),
                  pl.BlockSpec((Q, bw), lambda g, c: (rv(c), (INNER + NG * NS) // bw + g)),
                  sc, sc, sc, pl.BlockSpec((GPS, 8, 128), lambda g, c: (g, 0, 0)),
                  pl.BlockSpec((GPS, 1, NS, 256), lambda g, c: (g, rv(c), 0, 0)),
                  pl.BlockSpec((Q, 256 * GPS), lambda g, c: (rv(c), g))],
        out_specs=(pl.BlockSpec((Q, 256 * GPS), lambda g, c: (rv(c), g)),
                   pl.BlockSpec((Q, bw), lambda g, c: (rv(c), g)),
                   pl.BlockSpec((Q, bw), lambda g, c: (rv(c), g)),
                   pl.BlockSpec((GPS, Q, 128), lambda g, c: (g, rv(c), 0)),
                   pl.BlockSpec((GPS, 8, 128), lambda g, c: (g, 0, 0))),
        out_shape=(jax.ShapeDtypeStruct((L, INNER), F32), jax.ShapeDtypeStruct((L, NG * NS), F32),
                   jax.ShapeDtypeStruct((L, NG * NS), F32), jax.ShapeDtypeStruct((NG, L, 128), F32),
                   jax.ShapeDtypeStruct((NG, 8, 128), F32)),
        scratch_shapes=[pltpu.VMEM((GPS, NS, 256), F32)], name="ssd_bwd",
        compiler_params=_cp(("parallel", "arbitrary")),
    )(xbc, xbc, xbc, dt4, cs4, sg4, vecg, s_all, dy)


def _dt_bwd(ddt, dproj, tl=256):
    L = ddt.shape[1]

    def kern(d_ref, _, dp_ref, gs_ref):
        @pl.when(pl.program_id(0) == 0)
        def _():
            gs_ref[...] = jnp.zeros_like(gs_ref)

        d = d_ref[0]
        for g in range(1, NG):
            d = d + pltpu.roll(d_ref[g], 4 * g, axis=1)
        gs_ref[...] += jnp.broadcast_to(jnp.sum(d, axis=0, keepdims=True), (8, 128))
        dp_ref[...] = jnp.concatenate([d, jnp.zeros_like(d)], axis=1).astype(BF16)

    return pl.pallas_call(
        kern, grid=(L // tl,),
        in_specs=[pl.BlockSpec((NG, tl, 128), lambda i: (0, i, 0)), pl.BlockSpec(memory_space=pl.ANY)],
        out_specs=(pl.BlockSpec((tl, 256), lambda i: (i, C_DT // 256)), pl.BlockSpec((8, 128), lambda i: (0, 0))),
        out_shape=(jax.ShapeDtypeStruct(dproj.shape, BF16), jax.ShapeDtypeStruct((8, 128), F32)),
        input_output_aliases={1: 0}, name="dt_bwd", compiler_params=_cp(("arbitrary",)),
    )(ddt, dproj)


GW = INNER // NG


def _gnorm_fwd(y, proj, w, tl=256):
    L = y.shape[0]
    zoff = C_Z // 1024

    def kern(y_ref, z_ref, w_ref, o_ref):
        z = z_ref[...].astype(F32)
        yz = y_ref[...] * (z * _sigmoid(z))
        wv = w_ref[...]
        for k in range(1024 // GW):
            sl = slice(GW * k, GW * (k + 1))
            v = yz[:, sl]
            rg = lax.rsqrt(jnp.mean(v * v, axis=-1, keepdims=True) + EPS)
            o_ref[:, sl] = ((v * rg) * wv[:, sl]).astype(BF16)

    blk = pl.BlockSpec((tl, 1024), lambda i, j: (i, j))
    return pl.pallas_call(
        kern, grid=(L // tl, 2),
        in_specs=[blk, pl.BlockSpec((tl, 1024), lambda i, j: (i, zoff + j)), pl.BlockSpec((1, 1024), lambda i, j: (0, j))],
        out_specs=blk, out_shape=jax.ShapeDtypeStruct((L, INNER), BF16), name="gnorm_fwd",
        compiler_params=_cp(("parallel", "parallel")),
    )(y, proj, w.reshape(1, INNER))


def _gnorm_bwd(dyb, y, proj, w, dproj, tl=256):
    L = y.shape[0]
    zoff = C_Z // 1024

    def kern(d_ref, y_ref, z_ref, w_ref, _, dy_ref, dp_ref, gw_ref):
        @pl.when(pl.program_id(1) == 0)
        def _():
            gw_ref[...] = jnp.zeros_like(gw_ref)

        z = z_ref[...].astype(F32)
        sg = _sigmoid(z)
        sz = z * sg
        yv = y_ref[...]
        yz = yv * sz
        dv = d_ref[...]
        wv = w_ref[...]
        for k in range(1024 // GW):
            sl = slice(GW * k, GW * (k + 1))
            v = yz[:, sl]
            rg = lax.rsqrt(jnp.mean(v * v, axis=-1, keepdims=True) + EPS)
            vn = v * rg
            dk = dv[:, sl]
            gw_ref[:, sl] += jnp.broadcast_to(jnp.sum(dk * vn, axis=0, keepdims=True), (8, GW))
            dvn = dk * wv[:, sl]
            dyz = rg * (dvn - vn * jnp.mean(dvn * vn, axis=-1, keepdims=True))
            dy_ref[:, sl] = dyz * sz[:, sl]
            dp_ref[:, sl] = (dyz * yv[:, sl] * (sg[:, sl] * (1.0 + z[:, sl] * (1.0 - sg[:, sl])))).astype(BF16)

    blk = pl.BlockSpec((tl, 1024), lambda j, i: (i, j))
    zblk = pl.BlockSpec((tl, 1024), lambda j, i: (i, zoff + j))
    return pl.pallas_call(
        kern, grid=(2, L // tl),
        in_specs=[blk, blk, zblk, pl.BlockSpec((1, 1024), lambda j, i: (0, j)), pl.BlockSpec(memory_space=pl.ANY)],
        out_specs=(blk, zblk, pl.BlockSpec((8, 1024), lambda j, i: (0, j))),
        out_shape=(jax.ShapeDtypeStruct((L, INNER), F32), jax.ShapeDtypeStruct(dproj.shape, BF16),
                   jax.ShapeDtypeStruct((8, INNER), F32)),
        input_output_aliases={4: 1}, name="gnorm_bwd", compiler_params=_cp(("parallel", "arbitrary")),
    )(dyb, y, proj, w.reshape(1, INNER), dproj)


def _merge_fwd(proj, bg, br_a, br_b, tl=256):
    L = proj.shape[0]
    goff = C_GATE // 1024

    def kern(g1_ref, g2_ref, b1_ref, b2_ref, a_ref, b_ref, o_ref):
        g1 = _sigmoid(g1_ref[...].astype(F32) + b1_ref[...])
        g2 = _sigmoid(g2_ref[...].astype(F32) + b2_ref[...])
        o_ref[...] = (g1 * a_ref[...] + g2 * b_ref[...]).astype(BF16)

    row = pl.BlockSpec((tl, 1024), lambda i: (i, 0))
    bg2 = bg.reshape(1, 2 * D)
    return pl.pallas_call(
        kern, grid=(L // tl,),
        in_specs=[pl.BlockSpec((tl, 1024), lambda i: (i, goff)), pl.BlockSpec((tl, 1024), lambda i: (i, goff + 1)),
                  pl.BlockSpec((1, 1024), lambda i: (0, 0)), pl.BlockSpec((1, 1024), lambda i: (0, 1)), row, row],
        out_specs=row, out_shape=jax.ShapeDtypeStruct((L, D), BF16), name="merge_fwd",
        compiler_params=_cp(("parallel",)),
    )(proj, proj, bg2, bg2, br_a, br_b)


def _merge_bwd(dm, proj, bg, br_a, br_b, dproj, tl=256):
    L = proj.shape[0]
    goff = C_GATE // 1024

    def kern(dm_ref, g_ref, b_ref, a_ref, bb_ref, _, dbr_ref, dp_ref, gb_ref):
        j = pl.program_id(0)

        @pl.when(pl.program_id(1) == 0)
        def _():
            gb_ref[...] = jnp.zeros_like(gb_ref)

        g = _sigmoid(g_ref[...].astype(F32) + b_ref[...])
        br = jnp.where(j == 0, a_ref[...], bb_ref[...])
        dmv = dm_ref[...]
        dbr_ref[0] = (dmv * g).astype(BF16)
        dgate = dmv * br * g * (1.0 - g)
        gb_ref[...] += jnp.broadcast_to(jnp.sum(dgate, axis=0, keepdims=True), (8, 1024))
        dp_ref[...] = dgate.astype(BF16)

    row = pl.BlockSpec((tl, 1024), lambda j, i: (i, 0))
    gblk = pl.BlockSpec((tl, 1024), lambda j, i: (i, goff + j))
    return pl.pallas_call(
        kern, grid=(2, L // tl),
        in_specs=[row, gblk, pl.BlockSpec((1, 1024), lambda j, i: (0, j)), row, row, pl.BlockSpec(memory_space=pl.ANY)],
        out_specs=(pl.BlockSpec((1, tl, 1024), lambda j, i: (j, i, 0)), gblk, pl.BlockSpec((8, 1024), lambda j, i: (0, j))),
        out_shape=(jax.ShapeDtypeStruct((2, L, D), BF16), jax.ShapeDtypeStruct(dproj.shape, BF16),
                   jax.ShapeDtypeStruct((8, 2 * D), F32)),
        input_output_aliases={5: 1}, name="merge_bwd", compiler_params=_cp(("parallel", "arbitrary")),
    )(dm, proj, bg.reshape(1, 2 * D), br_a, br_b, dproj)


def _coords():
    return lax.axis_index("x"), lax.axis_index("y"), lax.axis_index("c")


def _other_chips(sk):
    xk, yk = sk // 2, sk % 2
    return [((1 - xk, yk), 2 * (1 - xk) + yk), ((xk, 1 - yk), 2 * xk + 1 - yk), ((1 - xk, 1 - yk), 2 * (1 - xk) + 1 - yk)]


def _rows(start, size):
    assert size % 128 == 0
    return pl.ds(pl.multiple_of(start, 128), size)


def _per_chip(fn):
    x, y, _ = _coords()
    s = 2 * x + y
    for sk in range(4):
        pl.when(s == sk)(functools.partial(fn, sk))


XTRA = PIECE - PMAIN


def _place(shard, full_shape, block, index_map, idx, name, blk0=0, nblk=None, dep=None, into=None):
    in_block = block[-2:]
    if nblk is None:
        nblk = shard.shape[0] // in_block[0]

    def kern(idx_ref, s_ref, *rest):
        o_ref = rest[-1]
        o_ref[...] = s_ref[...].astype(BF16).reshape(o_ref.shape)

    extra = ([dep] if dep is not None else []) + ([into] if into is not None else [])
    grid_spec = pltpu.PrefetchScalarGridSpec(
        num_scalar_prefetch=1, grid=(nblk,),
        in_specs=[pl.BlockSpec(in_block, lambda i, idx_ref: (blk0 + i, 0))] + [_ANY] * len(extra),
        out_specs=pl.BlockSpec(block, index_map))
    aliases = {1 + len(extra): 0} if into is not None else {}
    return pl.pallas_call(kern, grid_spec=grid_spec, out_shape=jax.ShapeDtypeStruct(full_shape, BF16), name=name,
                          input_output_aliases=aliases, compiler_params=_cp(("arbitrary",)))(idx, shard, *extra)


_SEM = pl.BlockSpec(memory_space=pltpu.SEMAPHORE)
_EFFECT = pltpu.SideEffectType.DATAFLOW_SIDE_EFFECTING


_ANY = pl.BlockSpec(memory_space=pl.ANY)


def _tie(v, dep, name):
    def body(v_ref, dep_ref, o_ref):
        del v_ref, dep_ref, o_ref

    return pl.pallas_call(body, out_shape=jax.ShapeDtypeStruct(v.shape, v.dtype), in_specs=[_ANY, _ANY],
                          out_specs=_ANY, input_output_aliases={0: 0}, name=name)(v, dep)


def _split_call(name, arrays, start=None, wait=None, wait_sems=None, after=None):
    keys = list(arrays)
    n = len(keys)
    n_start = start.n if start is not None else 0
    afters = [] if after is None else (list(after) if isinstance(after, (list, tuple)) else [after])

    def body(*refs):
        pos = n
        if wait is not None:
            wss, wrs = refs[pos], refs[pos + 1]
            pos += 2
        pos += len(afters)
        if start is not None:
            nss, nrs = refs[pos], refs[pos + 1]
            pos += 2
        R = dict(zip(keys, refs[pos:pos + n]))
        token = refs[pos + n]
        x, y, c = _coords()

        def desc(src, dst, dev, ss, rs, k):
            return pltpu.make_async_remote_copy(src_ref=src, dst_ref=dst, send_sem=ss.at[k], recv_sem=rs.at[k],
                                                device_id=dev, device_id_type=MESH)

        def run(sk):
            if wait is not None:
                for k, (snd, land) in enumerate(wait.copies(sk, R)):
                    if snd is not None:
                        desc(snd[0], snd[1], snd[2], wss, wrs, k).wait_send()
                    if land is not None:
                        desc(land, land, (x, y, c), wss, wrs, k).wait_recv()
            if start is not None:
                for k, (snd, land) in enumerate(start.copies(sk, R)):
                    if snd is not None:
                        desc(snd[0], snd[1], snd[2], nss, nrs, k).start()

        _per_chip(run)
        token[...] = jnp.zeros_like(token)

    hbm = pl.BlockSpec(memory_space=HBM)
    vals = [arrays[k] for k in keys]
    ins, in_specs = list(vals), [hbm] * n
    if wait is not None:
        ins += list(wait_sems)
        in_specs += [_SEM, _SEM]
    ins += afters
    in_specs += [pl.BlockSpec(memory_space=pl.ANY)] * len(afters)
    out_shape, out_specs = [], []
    if start is not None:
        out_shape += [pltpu.SemaphoreType.DMA((n_start,)), pltpu.SemaphoreType.DMA((n_start,))]
        out_specs += [_SEM, _SEM]
    first = len(out_shape)
    out_shape += [jax.ShapeDtypeStruct(v.shape, v.dtype) for v in vals] + [jax.ShapeDtypeStruct((8, 128), F32)]
    out_specs += [hbm] * n + [pl.BlockSpec(memory_space=pltpu.VMEM)]
    res = pl.pallas_call(
        body, out_shape=tuple(out_shape), in_specs=in_specs, out_specs=tuple(out_specs),
        input_output_aliases={i: first + i for i in range(n)}, name=name,
        compiler_params=pltpu.CompilerParams(has_side_effects=_EFFECT),
    )(*ins)
    sems = (res[0], res[1]) if start is not None else None
    return dict(zip(keys, res[first:first + n])), sems, res[-1]


class _Plan:
    def __init__(self, n, copies):
        self.n, self.copies = n, copies


_HM, _HX = PMAIN // 2, XTRA // 2
WAVE0 = 768
WAVES = ((0, WAVE0), (WAVE0, _HM - WAVE0))
_WIN = {
    "wq0": (True, "wct", lambda r, sc, hc: r.at[_rows(PMAIN * sc + _HM * hc + WAVES[0][0], WAVES[0][1]), :]),
    "wq1": (True, "wct", lambda r, sc, hc: r.at[_rows(PMAIN * sc + _HM * hc + WAVES[1][0], WAVES[1][1]), :]),
    "xt": (True, "xt", lambda r, sc, hc: r.at[sc, _rows(_HX * hc, _HX), :]),
    "w1": (True, "w1", lambda r, sc, hc: r.at[_rows(512 * hc, 512), pl.ds(1024 * sc, 1024)]),
    "w2": (True, "w2", lambda r, sc, hc: r.at[_rows(1024 * sc + 512 * hc, 512), :]),
    "wa": (True, "wa", lambda r, sc, hc: r.at[_rows(256 * sc + 128 * hc, 128), :]),
    "wb": (True, "wb", lambda r, sc, hc: r.at[_rows(512 * sc + 256 * hc, 256), :]),
    "wo": (True, "wo", lambda r, sc, hc: r.at[_rows(256 * sc + 128 * hc, 128), :]),
    "cw": (False, "cw", lambda r, sc, hc: r.at[sc]),
}


_PIECE_SRC = {
    "wq0": lambda p, hc: p.at[_rows(_HM * hc + WAVES[0][0], WAVES[0][1]), :],
    "wq1": lambda p, hc: p.at[_rows(_HM * hc + WAVES[1][0], WAVES[1][1]), :],
    "xt": lambda p, hc: p.at[_rows(PMAIN + _HX * hc, _HX), :],
}


def _ag_chips_plan(keys):
    def copies(sk, R):
        _, _, c = _coords()
        out = []
        for key in keys:
            _, arr, win = _WIN[key]
            for (px, py), ps in _other_chips(sk):
                dst = win(R[arr], sk, c)
                src = _PIECE_SRC[key](R["piece"], c) if key in _PIECE_SRC else dst
                out.append(((src, dst, (px, py, c)), win(R[arr], ps, c)))
        return out
    return _Plan(3 * len(keys), copies)


def _ag_sibling_plan(keys):
    keys = [k for k in keys if _WIN[k][0]]

    def copies(sk, R):
        x, y, c = _coords()
        out = []
        for key in keys:
            _, arr, win = _WIN[key]
            for _, ps in _other_chips(sk):
                w = win(R[arr], ps, c)
                out.append(((w, w, (x, y, 1 - c)), win(R[arr], ps, 1 - c)))
        return out
    return _Plan(3 * len(keys), copies)


def _in_proj_wave(h, wct, wave, proj=None, tm=2048):
    L = h.shape[0]
    tm = min(tm, L)
    off, size = WAVES[wave]
    start = lambda j: pl.multiple_of(_HM * j + off, 128)

    def kern(h_ref, w_ref, *rest):
        o_ref = rest[-1]
        o_ref[...] = lax.dot_general(h_ref[...], w_ref[...], _DIMS["nt"], preferred_element_type=F32).astype(BF16)

    in_specs = [pl.BlockSpec((tm, D), lambda j, i: (i, 0)),
                pl.BlockSpec((pl.Element(size), pl.Element(D)), lambda j, i: (start(j), 0))]
    args, aliases = [h, wct], {}
    if proj is not None:
        in_specs.append(pl.BlockSpec(memory_space=pl.ANY))
        args.append(proj)
        aliases = {2: 0}
    return pl.pallas_call(
        kern, grid=(8, L // tm), in_specs=in_specs,
        out_specs=pl.BlockSpec((pl.Element(tm), pl.Element(size)), lambda j, i: (i * tm, start(j))),
        out_shape=jax.ShapeDtypeStruct((L, NCW), BF16), input_output_aliases=aliases,
        name="in_proj_wave%d" % wave, compiler_params=_cp(("parallel", "parallel")),
    )(*args)


def _fix_wct(wct, xt):
    nb = PMAIN // XTRA

    def kern(w_ref, x_ref, o_ref):
        k = pl.program_id(0)
        xv = x_ref[0]
        o_ref[...] = jnp.where(k < 3, (w_ref[...].astype(F32) + xv.astype(F32)).astype(BF16), xv)

    blk = pl.BlockSpec((XTRA, D), lambda k: (nb * (k + 1), 0))
    rblk = pl.BlockSpec((XTRA, D), lambda k: (jnp.where(k < 3, nb * (k + 1), 0), 0))
    return pl.pallas_call(
        kern, grid=(4,), in_specs=[rblk, pl.BlockSpec((1, XTRA, D), lambda k: (k, 0, 0))], out_specs=blk,
        out_shape=jax.ShapeDtypeStruct(wct.shape, BF16), input_output_aliases={0: 0}, name="fix_wct",
        compiler_params=_cp(("arbitrary",)),
    )(wct, xt)


_HP = PIECE // 2
_GWIN = [
    lambda r, sc, hc: r.at[_rows(PMAIN * sc + _HP * hc, _HP), :],
    lambda r, sc, hc: r.at[_rows(512 * hc, 512), pl.ds(1024 * sc, 1024)],
    lambda r, sc, hc: r.at[_rows(1024 * sc + 512 * hc, 512), :],
    lambda r, sc, hc: r.at[_rows(256 * sc + 128 * hc, 128), :],
    lambda r, sc, hc: r.at[_rows(512 * sc + 256 * hc, 256), :],
    lambda r, sc, hc: r.at[_rows(256 * sc + 128 * hc, 128), :],
]
HALF_SHAPES = [(PIECE // 2, D), (512, 1024), (512, 1024), (128, 1024), (256, 1024), (128, 1024)]


def _rs_sibling_plan(ts):
    def copies(sk, R):
        x, y, c = _coords()
        out = []
        for t in ts:
            for sc in range(4):
                land = R["ra%d" % t].at[sc]
                out.append(((_GWIN[t](R["g%d" % t], sc, 1 - c), land, (x, y, 1 - c)), land))
        return out
    return _Plan(4 * len(ts), copies)


def _rs_chips_plan(ts):
    def copies(sk, R):
        _, _, c = _coords()
        out = []
        for t in ts:
            for j, ((px, py), ps) in enumerate(_other_chips(sk)):
                land = R["rb%d" % t].at[j]
                out.append(((R["hb%d" % t].at[ps], land, (px, py, c)), land))
        return out
    return _Plan(3 * len(ts), copies)


def _rs_share_plan(ts):
    def copies(sk, R):
        x, y, c = _coords()
        out = []
        for t in ts:
            rows = HALF_SHAPES[t][0]
            mine = R["f%d" % t].at[_rows(rows * c, rows), :]
            out.append(((mine, mine, (x, y, 1 - c)), R["f%d" % t].at[_rows(rows * (1 - c), rows), :]))
        return out
    return _Plan(len(ts), copies)


def _half_tiling(t):
    rows, cols = HALF_SHAPES[t]
    if t == 0:
        return (rows // 2, cols), 2, lambda i: (i, 0)
    return (rows, cols), 1, lambda i: (0, 0)


def _window_spec(t, blk):
    if t == 0:
        return pl.BlockSpec((pl.Element(blk[0]), pl.Element(blk[1])), lambda i, sc, idx_ref: (
            pl.multiple_of(PMAIN * sc + _HP * idx_ref[1] + blk[0] * i, 128), 0))
    if t == 1:
        return pl.BlockSpec(blk, lambda i, sc, idx_ref: (idx_ref[1], sc))
    return pl.BlockSpec(blk, lambda i, sc, idx_ref: (2 * sc + idx_ref[1], 0))


def _chip_sum(g, ra, t, idx, name):
    rows, cols = HALF_SHAPES[t]
    blk, nblk, inner = _half_tiling(t)

    def kern(idx_ref, g_ref, r_ref, hb_ref, hf_ref):
        v = g_ref[...].astype(F32) + r_ref[0].astype(F32)
        hb_ref[0] = v.astype(BF16)

        @pl.when(pl.program_id(1) == idx_ref[0])
        def _():
            hf_ref[...] = v

    omap = lambda i, sc, idx_ref: (sc,) + inner(i)
    grid_spec = pltpu.PrefetchScalarGridSpec(
        num_scalar_prefetch=1, grid=(nblk, 4),
        in_specs=[_window_spec(t, blk), pl.BlockSpec((1,) + blk, omap)],
        out_specs=(pl.BlockSpec((1,) + blk, omap), pl.BlockSpec(blk, lambda i, sc, idx_ref: inner(i))))
    return pl.pallas_call(
        kern, grid_spec=grid_spec,
        out_shape=(jax.ShapeDtypeStruct((4, rows, cols), BF16), jax.ShapeDtypeStruct((rows, cols), F32)),
        name=name, compiler_params=_cp(("parallel", "arbitrary")),
    )(idx, g, ra)


def _final_sum(hf, rb, t, idx, name):
    rows, cols = HALF_SHAPES[t]
    blk, nblk, inner = _half_tiling(t)
    nbr = rows // blk[0]

    def kern(idx_ref, h_ref, r_ref, o_ref):
        o_ref[...] = ((h_ref[...] + r_ref[0].astype(F32)) + r_ref[1].astype(F32)) + r_ref[2].astype(F32)

    def omap(i, idx_ref):
        r, cidx = inner(i)
        return nbr * idx_ref[1] + r, cidx

    grid_spec = pltpu.PrefetchScalarGridSpec(
        num_scalar_prefetch=1, grid=(nblk,),
        in_specs=[pl.BlockSpec(blk, lambda i, idx_ref: inner(i)),
                  pl.BlockSpec((3,) + blk, lambda i, idx_ref: (0,) + inner(i))],
        out_specs=pl.BlockSpec(blk, omap))
    return pl.pallas_call(
        kern, grid_spec=grid_spec, out_shape=jax.ShapeDtypeStruct((2 * rows, cols), F32),
        name=name, compiler_params=_cp(("parallel",)),
    )(idx, hf, rb)


class _ReduceScatter:
    def __init__(self, ts, grads, idx, tag):
        self.ts, self.idx, self.tag = ts, idx, tag
        arr = {}
        for t in ts:
            arr["g%d" % t] = grads[t]
            arr["ra%d" % t] = lax.empty((4,) + HALF_SHAPES[t], BF16)
        self.plan = _rs_sibling_plan(ts)
        self.arr, self.sems, self.token = _split_call("rs_sibling_start_" + tag, arr, start=self.plan)

    def chips(self, after):
        arr, _, _ = _split_call("rs_sibling_wait_" + self.tag, self.arr, wait=self.plan, wait_sems=self.sems, after=after)
        brr, self.hf = {}, {}
        for t in self.ts:
            hb, self.hf[t] = _chip_sum(arr["g%d" % t], arr["ra%d" % t], t, self.idx, "chip_sum_%d" % t)
            brr["hb%d" % t] = hb
            brr["rb%d" % t] = lax.empty((3,) + HALF_SHAPES[t], BF16)
        self.plan = _rs_chips_plan(self.ts)
        self.arr, self.sems, self.token = _split_call("rs_chips_start_" + self.tag, brr, start=self.plan)
        return self.token

    def share(self, after):
        brr, _, _ = _split_call("rs_chips_wait_" + self.tag, self.arr, wait=self.plan, wait_sems=self.sems, after=after)
        frr = {"f%d" % t: _final_sum(self.hf[t], brr["rb%d" % t], t, self.idx, "final_sum_%d" % t) for t in self.ts}
        self.plan = _rs_share_plan(self.ts)
        self.arr, self.sems, self.token = _split_call("rs_share_start_" + self.tag, frr, start=self.plan)
        return self.token

    def result(self, after):
        frr, _, _ = _split_call("rs_share_wait_" + self.tag, self.arr, wait=self.plan, wait_sems=self.sems, after=after)
        return {t: frr["f%d" % t] for t in self.ts}


def _all8_plan(key):
    def copies(sk, R):
        x, y, c = _coords()
        own = R[key].at[4 * x + 2 * y + c]
        out = []
        for k in range(1, 8):
            dev = ((1 - x) if (k >> 2) & 1 else x, (1 - y) if (k >> 1) & 1 else y, (1 - c) if k & 1 else c)
            out.append(((own, own, dev), R[key].at[4 * dev[0] + 2 * dev[1] + dev[2]]))
        return out
    return _Plan(7, copies)


def _small_all_gather(v):
    def body(v_ref, o_ref, send_sems, recv_sems, loc_sem):
        x, y, c = _coords()
        me = 4 * x + 2 * y + c
        lc = pltpu.make_async_copy(v_ref, o_ref.at[me], loc_sem)
        lc.start()
        cps = []
        for k in range(1, 8):
            fx, fy, fc = (k >> 2) & 1, (k >> 1) & 1, k & 1
            dev = ((1 - x) if fx else x, (1 - y) if fy else y, (1 - c) if fc else c)
            cp = pltpu.make_async_remote_copy(src_ref=v_ref, dst_ref=o_ref.at[me], send_sem=send_sems.at[k - 1],
                                              recv_sem=recv_sems.at[k - 1], device_id=dev, device_id_type=MESH)
            cp.start()
            cps.append((cp, 4 * dev[0] + 2 * dev[1] + dev[2]))
        for k, (cp, frm) in enumerate(cps):
            got = o_ref.at[frm]
            pltpu.make_async_remote_copy(src_ref=got, dst_ref=got, send_sem=send_sems.at[k], recv_sem=recv_sems.at[k],
                                         device_id=(x, y, c), device_id_type=MESH).wait_recv()
        for cp, _ in cps:
            cp.wait_send()
        lc.wait()

    hbm = pl.BlockSpec(memory_space=HBM)
    return pl.pallas_call(
        body, out_shape=jax.ShapeDtypeStruct((8,) + v.shape, F32), in_specs=[hbm], out_specs=hbm,
        scratch_shapes=[pltpu.SemaphoreType.DMA((7,)), pltpu.SemaphoreType.DMA((7,)), pltpu.SemaphoreType.DMA(())],
        name="small_all_gather", compiler_params=pltpu.CompilerParams(has_side_effects=True),
    )(v)


def _sum8(v, name="small_sum"):
    def kern(v_ref, o_ref):
        acc = v_ref[0]
        for k in range(1, 8):
            acc = acc + v_ref[k]
        o_ref[...] = acc

    return pl.pallas_call(kern, out_shape=jax.ShapeDtypeStruct(v.shape[1:], F32), name=name)(v)


def _adamw(w, g, m, v, name, tr=128, blk0=0, nblk=None, into=None, copy_g=False):
    R, C = w.shape
    tr = min(tr, R)
    if nblk is None:
        assert R % tr == 0 and blk0 == 0
        nblk = R // tr
    n_out = 4 if copy_g else 3

    def kern(*refs):
        w_ref, g_ref, m_ref, v_ref = refs[:4]
        d_ref, mo_ref, vo_ref = refs[-n_out:][:3]
        gv = g_ref[...]
        mn = ADAM_B1 * m_ref[...] + (1.0 - ADAM_B1) * gv
        vn = ADAM_B2 * v_ref[...] + (1.0 - ADAM_B2) * (gv * gv)
        m_hat = mn / (1.0 - ADAM_B1 ** ADAM_STEP)
        v_hat = vn / (1.0 - ADAM_B2 ** ADAM_STEP)
        d_ref[...] = -ADAM_LR * (m_hat / (jnp.sqrt(v_hat) + ADAM_EPS) + ADAM_WD * w_ref[...])
        mo_ref[...] = mn
        vo_ref[...] = vn
        if copy_g:
            refs[-1][...] = gv

    blk = pl.BlockSpec((tr, C), lambda i: (blk0 + i, 0))
    sd = jax.ShapeDtypeStruct((R, C), F32)
    in_specs, args, aliases = [blk] * 4, [w, g, m, v], {}
    if into is not None:
        in_specs += [pl.BlockSpec(memory_space=pl.ANY)] * 3
        args += list(into)
        aliases = {4: 0, 5: 1, 6: 2}
    return pl.pallas_call(kern, grid=(nblk,), in_specs=in_specs, out_specs=(blk,) * n_out, out_shape=(sd,) * n_out,
                          input_output_aliases=aliases, name=name, compiler_params=_cp(("parallel",)))(*args)


def _adamw_w_in(wt, gp, mt, vt, offs, name, r0, tr, nblk, views, into=None, blk_key=None):
    el = lambda n: (pl.Element(n), pl.Element(D))
    first = (lambda o: r0) if blk_key is None else (lambda o: tr * o[blk_key])
    own = pl.BlockSpec(el(tr), lambda i, o: (pl.multiple_of(first(o) + tr * i, 8), 0))

    def view(k):
        return pl.BlockSpec(el(tr), lambda i, o: (pl.multiple_of(jnp.maximum(first(o) + tr * i + o[k], 0), 8), 0))

    def kern(o_ref, w_ref, m_ref, v_ref, *refs):
        g_refs, (d_ref, mo_ref, vo_ref, go_ref) = refs[:len(views)], refs[-4:]
        gv = g_refs[0][...]
        if len(views) == 2:
            row = first(o_ref) + tr * pl.program_id(0) + lax.broadcasted_iota(jnp.int32, (tr, D), 0)
            gv = jnp.where(row < o_ref[2], gv, g_refs[1][...])
        mn = ADAM_B1 * m_ref[...] + (1.0 - ADAM_B1) * gv
        vn = ADAM_B2 * v_ref[...] + (1.0 - ADAM_B2) * (gv * gv)
        m_hat = mn / (1.0 - ADAM_B1 ** ADAM_STEP)
        v_hat = vn / (1.0 - ADAM_B2 ** ADAM_STEP)
        d_ref[...] = -ADAM_LR * (m_hat / (jnp.sqrt(v_hat) + ADAM_EPS) + ADAM_WD * w_ref[...])
        mo_ref[...] = mn
        vo_ref[...] = vn
        go_ref[...] = gv

    in_specs = [own, own, own] + [view(k) for k in views]
    args = [wt, mt, vt] + [gp] * len(views)
    aliases = {}
    if into is not None:
        in_specs += [pl.BlockSpec(memory_space=pl.ANY)] * 4
        args += list(into)
        aliases = {1 + len(args) - 4 + j: j for j in range(4)}
    grid_spec = pltpu.PrefetchScalarGridSpec(num_scalar_prefetch=1, grid=(nblk,), in_specs=in_specs,
                                             out_specs=(own,) * 4)
    sd = jax.ShapeDtypeStruct(wt.shape, F32)
    return pl.pallas_call(kern, grid_spec=grid_spec, out_shape=(sd,) * 4, input_output_aliases=aliases, name=name,
                          compiler_params=_cp(("parallel",)))(offs, *args)


def _to_piece(wt, s):
    z = lambda n: jnp.zeros((n, D), wt.dtype)
    pads = [functools.partial(lambda k, w: jnp.pad(w, ((8 * k, PIECE - W_SHARD - 8 * k), (0, 0))).astype(BF16), k)
            for k in range(3)]
    last = lambda w: jnp.concatenate([z(24), w[:744], w[776:], w[744:776], z(PIECE - 24 - W_SHARD)], axis=0).astype(BF16)
    return lax.switch(s, pads + [last], wt)


def _from_piece(p, s):
    cuts = [functools.partial(lambda k, q: q[8 * k:8 * k + W_SHARD], k) for k in range(3)]
    last = lambda q: jnp.concatenate([q[24:768], q[2816:2848], q[768:2816]], axis=0)
    return lax.switch(s, cuts + [last], p)


_SMALL = [("b_gate", 2048), ("ssm_conv_b", 4096), ("dt_bias", 32), ("A_log", 32), ("D_skip", 32),
          ("ssm_norm_w", 2048), ("norm_mlp", 1024), ("norm_final", 1024), ("sc_conv_w", 3072), ("ssm_conv_w", 16384),
          ("loss", 1)]


def _pack(vals, table, rows):
    parts = []
    for name, n in table:
        v = vals[name].reshape(-1).astype(F32)
        pad = (-n) % 128
        parts.append(jnp.pad(v, (0, pad)) if pad else v)
    flat = jnp.concatenate(parts)
    return jnp.pad(flat, (0, rows * 128 - flat.shape[0])).reshape(rows, 128)


def _unpack(arr, table):
    flat = arr.reshape(-1)
    out, off = {}, 0
    for name, n in table:
        out[name] = flat[off:off + n]
        off += n + ((-n) % 128)
    return out


def kernel(x, norm_mix, w_in, b_gate, sc_conv_w, ssm_conv_w, ssm_conv_b, dt_bias, A_log, D_skip, ssm_norm_w, w_branch_sc, w_branch_ssm, w_out, norm_mlp, w_mlp1, w_mlp2, norm_final, loss_target, m_norm_mix, m_w_in, m_b_gate, m_sc_conv_w, m_ssm_conv_w, m_ssm_conv_b, m_dt_bias, m_A_log, m_D_skip, m_ssm_norm_w, m_w_branch_sc, m_w_branch_ssm, m_w_out, m_norm_mlp, m_w_mlp1, m_w_mlp2, m_norm_final, v_norm_mix, v_w_in, v_b_gate, v_sc_conv_w, v_ssm_conv_w, v_ssm_conv_b, v_dt_bias, v_A_log, v_D_skip, v_ssm_norm_w, v_w_branch_sc, v_w_branch_ssm, v_w_out, v_norm_mlp, v_w_mlp1, v_w_mlp2, v_norm_final):
    L = x.shape[1]
    nc = L // Q
    xi, yi, ci = lax.axis_index("x"), lax.axis_index("y"), lax.axis_index("c")
    s = 2 * xi + yi
    idx = jnp.stack([s, ci]).astype(jnp.int32)
    x0 = x.reshape(L, D)
    tgt = loss_target.reshape(L, D)

    piece = _to_piece(w_in.T, s)
    nb = PMAIN // XTRA
    cws = jnp.zeros((8, 1280), F32)
    cws = cws.at[0:3, 0:256].set(sc_conv_w).at[0:4, 256:1280].set(ssm_conv_w)
    cw0 = lax.dynamic_update_slice(jnp.zeros((4, 8, 1280), F32), cws[None], (s, 0, 0))
    win_keys, win2_keys, mid_keys, end_keys = ["xt", "cw", "wq0"], ["wq1"], ["wa", "wb", "wo", "w1"], ["w2"]
    gw, sems_w, tok = _split_call(
        "ag_win_start", {"wct": lax.empty((NCW, D), BF16), "xt": lax.empty((4, XTRA, D), BF16), "cw": cw0, "piece": piece},
        start=_ag_chips_plan(win_keys))
    g2, sems_w2, tok = _split_call("ag_win2_start", {"wct": gw["wct"], "piece": gw["piece"]},
                                   start=_ag_chips_plan(win2_keys), after=tok)
    piece = g2["piece"]
    gw["wct"] = _place(piece, (NCW, D), (XTRA, D), lambda i, r: (nb * r[0] + i, 0), idx, "place_wct", nblk=nb,
                       dep=tok, into=g2["wct"])
    gw["xt"] = _place(piece, (4, XTRA, D), (1, XTRA, D), lambda i, r: (r[0], 0, 0), idx, "place_xt", blk0=nb, nblk=1,
                      dep=tok, into=gw["xt"])
    gw["piece"] = piece
    wa0 = _place(w_branch_sc, (D, D), (256, 1024), lambda i, r: (r[0], 0), idx, "place_wa", dep=tok)
    wb0 = _place(w_branch_ssm, (INNER, D), (512, 1024), lambda i, r: (r[0], 0), idx, "place_wb", dep=tok)
    wo0 = _place(w_out, (D, D), (256, 1024), lambda i, r: (r[0], 0), idx, "place_wo", dep=tok)
    w10 = _place(w_mlp1, (D, DFF), (256, 1024), lambda i, r: (i, r[0]), idx, "place_w1", dep=tok)
    gm, sems_m, tok = _split_call("ag_mid_start", {"wa": wa0, "wb": wb0, "wo": wo0, "w1": w10},
                                  start=_ag_chips_plan(mid_keys))
    w20 = _place(w_mlp2, (DFF, D), (256, 1024), lambda i, r: (4 * r[0] + i, 0), idx, "place_w2", dep=tok)
    ge, sems_e, tok = _split_call("ag_end_start", {"w2": w20}, start=_ag_chips_plan(end_keys))
    h = _rms_fwd(x0, norm_mix, "rms_mix", dep=tok)
    gw, sems_w, tok = _split_call("ag_win_pass", gw, wait=_ag_chips_plan(win_keys), wait_sems=sems_w,
                                  start=_ag_sibling_plan(win_keys), after=h)
    gw, _, _ = _split_call("ag_win_done", gw, wait=_ag_sibling_plan(win_keys), wait_sems=sems_w, after=tok)
    wc, cw_all = _fix_wct(gw["wct"], gw["xt"]), gw["cw"]
    sc_w_full = jnp.concatenate([cw_all[k, :, 0:256] for k in range(4)], axis=1)
    ssm_w_full = jnp.concatenate([cw_all[k, :, 256:1280] for k in range(4)], axis=1)
    cw4 = ssm_w_full.at[4].set(ssm_conv_b)
    vec = jnp.zeros((8, 128), F32).at[0, :NH].set(dt_bias).at[1, :NH].set(A_log)
    vecg = jnp.zeros((NG, 8, 128), F32).at[:, 0, :4].set(A_log.reshape(NG, 4)).at[:, 1, :4].set(D_skip.reshape(NG, 4))

    dtraw = _matmul(h, wc[C_DT:], "nt", F32, 512, 256, 1024, "in_proj_dt")
    proj = _in_proj_wave(h, wc, 0)
    g2, sems_w2, tok = _split_call("ag_win2_pass", {"wct": wc, "piece": gw["piece"]},
                                   wait=_ag_chips_plan(win2_keys), wait_sems=sems_w2,
                                   start=_ag_sibling_plan(win2_keys), after=[proj, dtraw])
    g2, _, _ = _split_call("ag_win2_done", g2, wait=_ag_sibling_plan(win2_keys), wait_sems=sems_w2, after=tok)
    wc = g2["wct"]
    proj = _in_proj_wave(h, wc, 1, proj=proj)
    ya = _sc_fwd(proj, sc_w_full)
    xbc = _ssm_conv_fwd(proj, cw4)
    dt4, cs4, sg4 = _dt_prep(dtraw, vec)
    y, s_all = _ssd_fwd(xbc, dt4, cs4, vecg)
    gm, sems_m, tok = _split_call("ag_mid_pass", gm, wait=_ag_chips_plan(mid_keys), wait_sems=sems_m,
                                  start=_ag_sibling_plan(mid_keys), after=[y, ya])
    y = _tie(y, tok, "tie_y")
    yb = _gnorm_fwd(y, proj, ssm_norm_w)
    gm, _, _ = _split_call("ag_mid_done", gm, wait=_ag_sibling_plan(mid_keys), wait_sems=sems_m, after=yb)
    wa, wb, wo, w1 = gm["wa"], gm["wb"], gm["wo"], gm["w1"]
    ge, sems_e, tok = _split_call("ag_end_pass", ge, wait=_ag_chips_plan(end_keys), wait_sems=sems_e,
                                  start=_ag_sibling_plan(end_keys), after=yb)
    br_a = _matmul(ya, wa, "nn", F32, 1024, 1024, 1024, "branch_sc", dep=tok)
    br_b = _matmul(yb, wb, "nn", F32, 1024, 1024, 2048, "branch_ssm")
    merged = _merge_fwd(proj, b_gate, br_a, br_b)
    x1 = _matmul(merged, wo, "nn", F32, 1024, 1024, 1024, "out_proj", epi="res", extra=x0)
    h2 = _rms_fwd(x1, norm_mlp, "rms_mlp")
    a1, rl = _matmul(h2, w1, "nn", BF16, 1024, 1024, 1024, "mlp1", epi="relu2", n_outer=True)
    ge, _, _ = _split_call("ag_end_done", ge, wait=_ag_sibling_plan(end_keys), wait_sems=sems_e, after=a1)
    w2 = ge["w2"]
    x2 = _matmul(rl, w2, "nn", F32, 512, 1024, 4096, "mlp2", epi="res", extra=x1)
    dx2, g_nf, loss8 = _final(x2, norm_final, tgt)

    da = _matmul(dx2, w2, "nt", BF16, 1024, 1024, 1024, "mlp2_dx", epi="drelu", extra=a1, n_outer=True)
    g_w2 = _matmul(rl, dx2, "tn", BF16, 1024, 1024, 2048, "mlp2_dw")
    g_w1 = _matmul(h2, da, "tn", BF16, 1024, 1024, 2048, "mlp1_dw")
    dh2 = _matmul(da, w1, "nt", F32, 512, 1024, 4096, "mlp1_dx")
    dx1, g_nmlp = _rms_bwd(dh2, x1, norm_mlp, dx2, "rms_mlp_bwd")
    dmerged = _matmul(dx1, wo, "nt", F32, 1024, 1024, 1024, "out_proj_dx")
    g_wo = _matmul(merged, dx1, "tn", BF16, 1024, 1024, 2048, "out_proj_dw")
    dproj = lax.empty((L, NCW), BF16)
    dbr, dproj, g_bg = _merge_bwd(dmerged, proj, b_gate, br_a, br_b, dproj)
    dya = _matmul(dbr[0], wa, "nt", F32, 1024, 1024, 1024, "branch_sc_dx")
    g_wa = _matmul(ya, dbr[0], "tn", BF16, 1024, 1024, 2048, "branch_sc_dw")
    dproj, g_scw = _sc_bwd(dya, proj, sc_w_full, dproj)
    dyb = _matmul(dbr[1], wb, "nt", F32, 1024, 1024, 1024, "branch_ssm_dx", n_outer=True)
    g_wb = _matmul(yb, dbr[1], "tn", BF16, 1024, 1024, 2048, "branch_ssm_dw")
    rs_a = _ReduceScatter([1, 2, 3, 4, 5], {1: g_w1, 2: g_w2, 3: g_wa, 4: g_wb, 5: g_wo}, idx, "a")
    dy, dproj, g_snw = _gnorm_bwd(_tie(dyb, rs_a.token, "tie_dyb"), y, proj, ssm_norm_w, dproj)
    tok = rs_a.chips(after=dy)
    dxs, dbm, dcm, ddt_g, st = _ssd_bwd(xbc, dt4, cs4, sg4, vecg, s_all, _tie(dy, tok, "tie_dy"))
    dproj, gx1 = _ssm_conv_bwd(dxs, proj, cw4, dproj, 0, "ssm_conv_bwd_x")
    dproj, gx2 = _ssm_conv_bwd(dbm, proj, cw4, dproj, INNER, "ssm_conv_bwd_b")
    dproj, gx3 = _ssm_conv_bwd(dcm, proj, cw4, dproj, INNER + NG * NS, "ssm_conv_bwd_c")
    g_cw4 = jnp.concatenate([gx1, gx2, gx3], axis=1)
    dproj, g_dtb = _dt_bwd(ddt_g, dproj)
    small = {"b_gate": g_bg[0], "ssm_conv_b": g_cw4[4], "dt_bias": g_dtb[0, :NH],
             "A_log": st[:, 0, :4], "D_skip": st[:, 1, :4], "ssm_norm_w": g_snw[0], "norm_mlp": g_nmlp[0],
             "norm_final": g_nf[0], "sc_conv_w": g_scw[0:3], "ssm_conv_w": g_cw4[0:4], "loss": loss8[0, 0:1]}
    me = 4 * xi + 2 * yi + ci
    sm8 = lax.dynamic_update_slice(jnp.zeros((8, SMALL_ROWS, 128), F32), _pack(small, _SMALL, SMALL_ROWS)[None], (me, 0, 0))
    sm_arr, sm_sems, tok = _split_call("small_start", {"sm": sm8}, start=_all8_plan("sm"))
    g_wc = _matmul(dproj, h, "tn", BF16, 1280, 1024, 2048, "in_proj_dw", dep=tok)
    rs_b = _ReduceScatter([0], {0: g_wc}, idx, "b")
    tok = rs_a.share(after=rs_b.token)
    tok = rs_b.chips(after=tok)
    dh = _matmul(dproj, wc, "nn", F32, 512, 1024, 5760, "in_proj_dx", dep=tok)
    grad_x, g_nm = _rms_bwd(dh, x0, norm_mix, dx1, "rms_mix_bwd")
    nm8 = lax.dynamic_update_slice(jnp.zeros((8, 8, 128), F32), g_nm[0].reshape(1, 8, 128), (me, 0, 0))
    nm_arr, nm_sems, tok = _split_call("norm_mix_start", {"nm": nm8}, start=_all8_plan("nm"))
    sm_arr, _, _ = _split_call("small_wait", sm_arr, wait=_all8_plan("sm"), wait_sems=sm_sems, after=tok)
    small_sum = _sum8(sm_arr["sm"])
    gs = _unpack(small_sum, _SMALL)
    red = rs_a.result(after=tok)
    big = {"w_mlp1": red[1], "w_mlp2": red[2], "w_branch_sc": red[3], "w_branch_ssm": red[4], "w_out": red[5]}

    given = dict(norm_mix=norm_mix, w_in=w_in, b_gate=b_gate, sc_conv_w=sc_conv_w, ssm_conv_w=ssm_conv_w, ssm_conv_b=ssm_conv_b, dt_bias=dt_bias, A_log=A_log, D_skip=D_skip, ssm_norm_w=ssm_norm_w, w_branch_sc=w_branch_sc, w_branch_ssm=w_branch_ssm, w_out=w_out, norm_mlp=norm_mlp, w_mlp1=w_mlp1, w_mlp2=w_mlp2, norm_final=norm_final,
                 m_norm_mix=m_norm_mix, m_w_in=m_w_in, m_b_gate=m_b_gate, m_sc_conv_w=m_sc_conv_w, m_ssm_conv_w=m_ssm_conv_w, m_ssm_conv_b=m_ssm_conv_b, m_dt_bias=m_dt_bias, m_A_log=m_A_log, m_D_skip=m_D_skip, m_ssm_norm_w=m_ssm_norm_w, m_w_branch_sc=m_w_branch_sc, m_w_branch_ssm=m_w_branch_ssm, m_w_out=m_w_out, m_norm_mlp=m_norm_mlp, m_w_mlp1=m_w_mlp1, m_w_mlp2=m_w_mlp2, m_norm_final=m_norm_final,
                 v_norm_mix=v_norm_mix, v_w_in=v_w_in, v_b_gate=v_b_gate, v_sc_conv_w=v_sc_conv_w, v_ssm_conv_w=v_ssm_conv_w, v_ssm_conv_b=v_ssm_conv_b, v_dt_bias=v_dt_bias, v_A_log=v_A_log, v_D_skip=v_D_skip, v_ssm_norm_w=v_ssm_norm_w, v_w_branch_sc=v_w_branch_sc, v_w_branch_ssm=v_w_branch_ssm, v_w_out=v_w_out, v_norm_mlp=v_norm_mlp, v_w_mlp1=v_w_mlp1, v_w_mlp2=v_w_mlp2, v_norm_final=v_norm_final)
    order = ["norm_mix", "w_in", "b_gate", "sc_conv_w", "ssm_conv_w", "ssm_conv_b", "dt_bias", "A_log", "D_skip",
             "ssm_norm_w", "w_branch_sc", "w_branch_ssm", "w_out", "norm_mlp", "w_mlp1", "w_mlp2", "norm_final"]
    grad, delta, new_m, new_v = {}, {}, {}, {}
    for n in big:
        delta[n], new_m[n], new_v[n], grad[n] = _adamw(given[n], big[n], given["m_" + n], given["v_" + n],
                                                       "adamw_" + n, copy_g=True)
    big["w_in"] = None
    grad_small = {n: gs[n].reshape(given[n].shape) for n in order
                  if n not in big and n not in ("sc_conv_w", "ssm_conv_w", "norm_mix")}
    grad_small["sc_conv_w"] = lax.dynamic_slice(gs["sc_conv_w"].reshape(3, D), (0, 256 * s), (3, 256))
    grad_small["ssm_conv_w"] = lax.dynamic_slice(gs["ssm_conv_w"].reshape(4, XBC), (0, 1024 * s), (4, 1024))
    table = [(n, int(grad_small[n].size)) for n in grad_small]
    rows = 136
    pk = lambda d: _pack(d, table, rows)
    ds_, ms_, vs_ = _adamw(pk({n: given[n] for n in grad_small}), pk(grad_small), pk({n: given["m_" + n] for n in grad_small}),
                           pk({n: given["v_" + n] for n in grad_small}), "adamw_small", tr=rows)
    ds_, ms_, vs_ = _unpack(ds_, table), _unpack(ms_, table), _unpack(vs_, table)
    for n in grad_small:
        shp = given[n].shape
        grad[n] = grad_small[n]
        delta[n], new_m[n], new_v[n] = ds_[n].reshape(shp), ms_[n].reshape(shp), vs_[n].reshape(shp)

    done = [new_v[n] for n in ("w_mlp1", "w_mlp2", "w_branch_sc", "w_branch_ssm", "w_out")] + [vs_["b_gate"]]
    tok = rs_b.share(after=done)
    offs = jnp.where(s == 3, jnp.array([24, -8, 744, 2072, -8], jnp.int32),
                     jnp.stack([8 * s, 8 * s, 0 * s, 8 * s, 8 * s]).astype(jnp.int32))
    offs = jnp.concatenate([offs, jnp.stack([7 * ci, 4 - 4 * ci]).astype(jnp.int32)])
    nmain = W_SHARD // 256
    wt_own = (w_in.T, rs_b.arr["f0"], m_w_in.T, v_w_in.T, offs)
    res = _adamw_w_in(*wt_own, "adamw_w_in_own", 0, 256, 4, (0, 1), blk_key=5)
    gp = rs_b.result(after=[tok, res[0]])[0]
    wt_args = (w_in.T, gp, m_w_in.T, v_w_in.T, offs)
    res = _adamw_w_in(*wt_args, "adamw_w_in", 0, 256, nmain - 4, (0, 1), into=res, blk_key=6)
    res = _adamw_w_in(*wt_args, "adamw_w_in_dt", 744, 32, 1, (3,), into=res)
    dt_, mt_, vt_, gwt = _adamw_w_in(*wt_args, "adamw_w_in_tail", 256 * nmain, 8, 1, (4,), into=res)
    grad["w_in"], delta["w_in"], new_m["w_in"], new_v["w_in"] = gwt.T, dt_.T, mt_.T, vt_.T
    nm_arr, _, _ = _split_call("norm_mix_wait", nm_arr, wait=_all8_plan("nm"), wait_sems=nm_sems, after=tok)
    g8 = _sum8(nm_arr["nm"], "norm_mix_sum")
    r8 = lambda a: a.reshape(8, 128)
    d8, m8, v8 = _adamw(r8(norm_mix), g8, r8(m_norm_mix), r8(v_norm_mix), "adamw_norm_mix", tr=8)
    grad["norm_mix"], delta["norm_mix"] = g8.reshape(D), d8.reshape(D)
    new_m["norm_mix"], new_v["norm_mix"] = m8.reshape(D), v8.reshape(D)

    loss = gs["loss"].reshape(())
    return (loss, grad_x.reshape(1, L, D), *[grad[n] for n in order], *[delta[n] for n in order],
            *[new_m[n] for n in order], *[new_v[n] for n in order])
```

```python
import functools

import jax
import jax.numpy as jnp
from jax import lax
from jax.experimental import pallas as pl
from jax.experimental.pallas import tpu as pltpu

F32 = jnp.float32
BF16 = jnp.bfloat16
MESH = pl.DeviceIdType.MESH
HBM = pltpu.HBM

D = 1024
INNER = 2048
HD = 64
NH = 32
NG = 8
NS = 128
Q = 128
GPS = 8
XBC = 4096
DFF = 4096
EPS = 1e-6
W_SHARD = 2824
NCW = 11520
PIECE = 3072
PMAIN = 2816
C_Z, C_XBC, C_GATE, C_DT = 3072, 5120, 9216, 11264
SMALL_ROWS = 256
VMEM_LIMIT = 56 * 1024 * 1024

ADAM_LR, ADAM_B1, ADAM_B2, ADAM_EPS, ADAM_WD, ADAM_STEP = 0.001, 0.9, 0.999, 1e-08, 0.01, 10


def _cp(sem=None, vmem=VMEM_LIMIT):
    return pltpu.CompilerParams(dimension_semantics=sem, vmem_limit_bytes=vmem)


def _sigmoid(v):
    return 1.0 / (1.0 + jnp.exp(-v))


_DIMS = {"nn": (((1,), (0,)), ((), ())), "nt": (((1,), (1,)), ((), ())), "tn": (((0,), (0,)), ((), ()))}


def _matmul(a, b, mode, out_dtype, tm, tn, tk, name, epi=None, extra=None, n_outer=False, dep=None):
    if mode == "tn":
        K, M = a.shape
    else:
        M, K = a.shape
    N = b.shape[0] if mode == "nt" else b.shape[1]
    tm, tn, tk = min(tm, M), min(tn, N), min(tk, K)
    assert M % tm == 0 and N % tn == 0 and K % tk == 0, (name, M, N, K, tm, tn, tk)
    nm, nn, nk = M // tm, N // tn, K // tk
    dims = _DIMS[mode]

    def ij(p0, p1):
        return (p1, p0) if n_outer else (p0, p1)

    if mode == "tn":
        a_spec = pl.BlockSpec((tk, tm), lambda p0, p1, k: (k, ij(p0, p1)[0]))
    else:
        a_spec = pl.BlockSpec((tm, tk), lambda p0, p1, k: (ij(p0, p1)[0], k))
    if mode == "nt":
        b_spec = pl.BlockSpec((tn, tk), lambda p0, p1, k: (ij(p0, p1)[1], k))
    else:
        b_spec = pl.BlockSpec((tk, tn), lambda p0, p1, k: (k, ij(p0, p1)[1]))
    o_spec = pl.BlockSpec((tm, tn), lambda p0, p1, k: ij(p0, p1))
    in_specs = [a_spec, b_spec]
    args = [a, b]
    if epi in ("res", "drelu"):
        in_specs.append(o_spec)
        args.append(extra)
    if dep is not None:
        in_specs.append(pl.BlockSpec(memory_space=pl.ANY))
        args.append(dep)
    n_in = len(args)
    if epi == "relu2":
        out_shape = (jax.ShapeDtypeStruct((M, N), out_dtype), jax.ShapeDtypeStruct((M, N), BF16))
        out_specs = (o_spec, o_spec)
    else:
        out_shape = jax.ShapeDtypeStruct((M, N), out_dtype)
        out_specs = o_spec

    def kern(*refs):
        a_ref, b_ref = refs[0], refs[1]
        e_ref = refs[2] if epi in ("res", "drelu") else None
        acc = refs[-1]
        outs = refs[n_in:-1] if nk > 1 else refs[n_in:]
        k = pl.program_id(2)

        def product():
            return lax.dot_general(a_ref[...].astype(BF16), b_ref[...].astype(BF16), dims, preferred_element_type=F32)

        def finish(r):
            if epi is None:
                outs[0][...] = r.astype(out_dtype)
            elif epi == "res":
                outs[0][...] = (r + e_ref[...]).astype(out_dtype)
            elif epi == "relu2":
                outs[0][...] = r.astype(out_dtype)
                t = jnp.maximum(r, 0.0)
                outs[1][...] = (t * t).astype(BF16)
            else:
                outs[0][...] = (r * (2.0 * jnp.maximum(e_ref[...].astype(F32), 0.0))).astype(out_dtype)

        if nk == 1:
            finish(product())
        else:
            @pl.when(k == 0)
            def _():
                acc[...] = jnp.zeros_like(acc)

            acc[...] += product()

            @pl.when(k == nk - 1)
            def _():
                finish(acc[...])

    grid = (nn, nm, nk) if n_outer else (nm, nn, nk)
    return pl.pallas_call(
        kern, grid=grid, in_specs=in_specs, out_specs=out_specs, out_shape=out_shape,
        scratch_shapes=[pltpu.VMEM((tm, tn), F32)] if nk > 1 else [], name=name,
        compiler_params=_cp(("parallel", "parallel", "arbitrary")),
    )(*args)


def _rms_fwd(x, w, name, tl=256, dep=None):
    L = x.shape[0]

    def kern(x_ref, w_ref, *rest):
        o_ref = rest[-1]
        xv = x_ref[...]
        r = lax.rsqrt(jnp.mean(xv * xv, axis=-1, keepdims=True) + EPS)
        o_ref[...] = ((xv * r) * w_ref[...]).astype(BF16)

    row = pl.BlockSpec((tl, D), lambda i: (i, 0))
    deps = [] if dep is None else [dep]
    return pl.pallas_call(
        kern, grid=(L // tl,),
        in_specs=[row, pl.BlockSpec((1, D), lambda i: (0, 0))] + [pl.BlockSpec(memory_space=pl.ANY)] * len(deps),
        out_specs=row, out_shape=jax.ShapeDtypeStruct((L, D), BF16), name=name, compiler_params=_cp(("parallel",)),
    )(x, w.reshape(1, D), *deps)


def _rms_bwd(dy, x, w, res, name, tl=256, dep=None):
    L = x.shape[0]
    deps = [] if dep is None else [dep]

    def kern(dy_ref, x_ref, w_ref, res_ref, *rest):
        dx_ref, gw_ref = rest[-2:]
        @pl.when(pl.program_id(0) == 0)
        def _():
            gw_ref[...] = jnp.zeros_like(gw_ref)

        xv = x_ref[...]
        dyv = dy_ref[...]
        r = lax.rsqrt(jnp.mean(xv * xv, axis=-1, keepdims=True) + EPS)
        xn = xv * r
        gw_ref[...] += jnp.broadcast_to(jnp.sum(dyv * xn, axis=0, keepdims=True), (8, D))
        dxn = dyv * w_ref[...]
        dx_ref[...] = res_ref[...] + r * (dxn - xn * jnp.mean(dxn * xn, axis=-1, keepdims=True))

    row = pl.BlockSpec((tl, D), lambda i: (i, 0))
    return pl.pallas_call(
        kern, grid=(L // tl,),
        in_specs=[row, row, pl.BlockSpec((1, D), lambda i: (0, 0)), row] + [pl.BlockSpec(memory_space=pl.ANY)] * len(deps),
        out_specs=(row, pl.BlockSpec((8, D), lambda i: (0, 0))),
        out_shape=(jax.ShapeDtypeStruct((L, D), F32), jax.ShapeDtypeStruct((8, D), F32)),
        name=name, compiler_params=_cp(("arbitrary",)),
    )(dy, x, w.reshape(1, D), res, *deps)


def _final(x2, w, tgt, tl=256):
    L = x2.shape[0]

    def kern(x_ref, w_ref, t_ref, dx_ref, gw_ref, loss_ref):
        @pl.when(pl.program_id(0) == 0)
        def _():
            gw_ref[...] = jnp.zeros_like(gw_ref)
            loss_ref[...] = jnp.zeros_like(loss_ref)

        xv = x_ref[...]
        r = lax.rsqrt(jnp.mean(xv * xv, axis=-1, keepdims=True) + EPS)
        xn = xv * r
        e = xn * w_ref[...] - t_ref[...]
        per_tok = jnp.mean(e * e, axis=-1, keepdims=True)
        loss_ref[...] += 0.5 * jnp.sum(per_tok)
        dyv = e * (1.0 / D)
        gw_ref[...] += jnp.broadcast_to(jnp.sum(dyv * xn, axis=0, keepdims=True), (8, D))
        dxn = dyv * w_ref[...]
        dx_ref[...] = r * (dxn - xn * jnp.mean(dxn * xn, axis=-1, keepdims=True))

    row = pl.BlockSpec((tl, D), lambda i: (i, 0))
    return pl.pallas_call(
        kern, grid=(L // tl,), in_specs=[row, pl.BlockSpec((1, D), lambda i: (0, 0)), row],
        out_specs=(row, pl.BlockSpec((8, D), lambda i: (0, 0)), pl.BlockSpec((8, 128), lambda i: (0, 0))),
        out_shape=(jax.ShapeDtypeStruct((L, D), F32), jax.ShapeDtypeStruct((8, D), F32),
                   jax.ShapeDtypeStruct((8, 128), F32)),
        name="final_norm_loss", compiler_params=_cp(("arbitrary",)),
    )(x2, w.reshape(1, D), tgt)


def _down(v, k):
    if k == 0:
        return v
    t = lax.broadcasted_iota(jnp.int32, v.shape, 0)
    return jnp.where(t >= k, pltpu.roll(v, k, axis=0), 0.0)


def _up(v, k):
    if k == 0:
        return v
    n = v.shape[0]
    t = lax.broadcasted_iota(jnp.int32, v.shape, 0)
    return jnp.where(t < n - k, pltpu.roll(v, n - k, axis=0), 0.0)


TW = 256


def _sc_fwd(proj, cw):
    L = proj.shape[0]
    nb = D // TW

    def kern(b_ref, c_ref, x_ref, w_ref, o_ref):
        u = c_ref[...].astype(F32) * x_ref[...].astype(F32)
        w = w_ref[...]
        cv = w[0:1] * _down(u, 2) + w[1:2] * _down(u, 1) + w[2:3] * u
        o_ref[...] = (b_ref[...].astype(F32) * cv).astype(BF16)

    col = lambda off: pl.BlockSpec((L, TW), lambda j: (0, off + j))
    return pl.pallas_call(
        kern, grid=(nb,), in_specs=[col(0), col(nb), col(2 * nb), pl.BlockSpec((8, TW), lambda j: (0, j))],
        out_specs=pl.BlockSpec((L, TW), lambda j: (0, j)), out_shape=jax.ShapeDtypeStruct((L, D), BF16),
        name="sc_fwd", compiler_params=_cp(("parallel",)),
    )(proj, proj, proj, cw)


def _sc_bwd(dya, proj, cw, dproj):
    L = proj.shape[0]
    nb = D // TW

    def kern(d_ref, b_ref, c_ref, x_ref, w_ref, _, dp_ref, gw_ref, keep):
        sec = pl.program_id(1)

        @pl.when(sec == 0)
        def _():
            cs, xs, dyv = c_ref[...].astype(F32), x_ref[...].astype(F32), d_ref[...]
            w = w_ref[...]
            u = cs * xs
            u1, u2 = _down(u, 1), _down(u, 2)
            cv = w[0:1] * u2 + w[1:2] * u1 + w[2:3] * u
            dcv = dyv * b_ref[...].astype(F32)
            du = w[2:3] * dcv + w[1:2] * _up(dcv, 1) + w[0:1] * _up(dcv, 2)
            g0 = jnp.sum(dcv * u2, axis=0, keepdims=True)
            g1 = jnp.sum(dcv * u1, axis=0, keepdims=True)
            g2 = jnp.sum(dcv * u, axis=0, keepdims=True)
            row = lax.broadcasted_iota(jnp.int32, (8, TW), 0)
            gw_ref[...] = jnp.where(row == 0, g0, jnp.where(row == 1, g1, jnp.where(row == 2, g2, 0.0)))
            dp_ref[...] = (dyv * cv).astype(BF16)
            keep[0] = (du * xs).astype(BF16)
            keep[1] = (du * cs).astype(BF16)

        @pl.when(sec > 0)
        def _():
            dp_ref[...] = keep[sec - 1]

    col = lambda off: pl.BlockSpec((L, TW), lambda j, s: (0, off + j))
    return pl.pallas_call(
        kern, grid=(nb, 3),
        in_specs=[col(0), col(0), col(nb), col(2 * nb), pl.BlockSpec((8, TW), lambda j, s: (0, j)),
                  pl.BlockSpec(memory_space=pl.ANY)],
        out_specs=(pl.BlockSpec((L, TW), lambda j, s: (0, s * nb + j)), pl.BlockSpec((8, TW), lambda j, s: (0, j))),
        out_shape=(jax.ShapeDtypeStruct(dproj.shape, BF16), jax.ShapeDtypeStruct((8, D), F32)),
        scratch_shapes=[pltpu.VMEM((2, L, TW), BF16)],
        input_output_aliases={5: 0}, name="sc_bwd", compiler_params=_cp(("parallel", "arbitrary")),
    )(dya, proj, proj, proj, cw, dproj)


def _ssm_conv_fwd(proj, cw4):
    L = proj.shape[0]
    off = C_XBC // TW

    def kern(r_ref, w_ref, o_ref):
        raw = r_ref[...].astype(F32)
        w = w_ref[...]
        c4 = w[0:1] * _down(raw, 3) + w[1:2] * _down(raw, 2) + w[2:3] * _down(raw, 1) + w[3:4] * raw + w[4:5]
        o_ref[...] = c4 * _sigmoid(c4)

    return pl.pallas_call(
        kern, grid=(XBC // TW,),
        in_specs=[pl.BlockSpec((L, TW), lambda j: (0, off + j)), pl.BlockSpec((8, TW), lambda j: (0, j))],
        out_specs=pl.BlockSpec((L, TW), lambda j: (0, j)), out_shape=jax.ShapeDtypeStruct((L, XBC), F32),
        name="ssm_conv_fwd", compiler_params=_cp(("parallel",)),
    )(proj, cw4)


def _ssm_conv_bwd(dx, proj, cw4, dproj, col0, name):
    L, width = dx.shape
    off_p = (C_XBC + col0) // TW
    off_w = col0 // TW

    def kern(d_ref, r_ref, w_ref, _, dp_ref, gw_ref):
        raw = r_ref[...].astype(F32)
        w = w_ref[...]
        r1, r2, r3 = _down(raw, 1), _down(raw, 2), _down(raw, 3)
        c4 = w[0:1] * r3 + w[1:2] * r2 + w[2:3] * r1 + w[3:4] * raw + w[4:5]
        sg = _sigmoid(c4)
        dc4 = d_ref[...] * (sg * (1.0 + c4 * (1.0 - sg)))
        draw = w[3:4] * dc4 + w[2:3] * _up(dc4, 1) + w[1:2] * _up(dc4, 2) + w[0:1] * _up(dc4, 3)
        dp_ref[...] = draw.astype(BF16)
        gs = [jnp.sum(dc4 * r3, axis=0, keepdims=True), jnp.sum(dc4 * r2, axis=0, keepdims=True),
              jnp.sum(dc4 * r1, axis=0, keepdims=True), jnp.sum(dc4 * raw, axis=0, keepdims=True),
              jnp.sum(dc4, axis=0, keepdims=True)]
        row = lax.broadcasted_iota(jnp.int32, (8, TW), 0)
        acc = jnp.zeros((8, TW), F32)
        for k, gk in enumerate(gs):
            acc = jnp.where(row == k, gk, acc)
        gw_ref[...] = acc

    return pl.pallas_call(
        kern, grid=(width // TW,),
        in_specs=[pl.BlockSpec((L, TW), lambda j: (0, j)), pl.BlockSpec((L, TW), lambda j: (0, off_p + j)),
                  pl.BlockSpec((8, TW), lambda j: (0, off_w + j)), pl.BlockSpec(memory_space=pl.ANY)],
        out_specs=(pl.BlockSpec((L, TW), lambda j: (0, off_p + j)), pl.BlockSpec((8, TW), lambda j: (0, j))),
        out_shape=(jax.ShapeDtypeStruct(dproj.shape, BF16), jax.ShapeDtypeStruct((8, width), F32)),
        input_output_aliases={3: 0}, name=name, compiler_params=_cp(("arbitrary",)),
    )(dx, proj, cw4, dproj)


def _split3(v):
    h1 = v.astype(BF16)
    r1 = v - h1.astype(F32)
    h2 = r1.astype(BF16)
    h3 = (r1 - h2.astype(F32)).astype(BF16)
    return h1, h2, h3


def _dot01(m01, v, dims=_DIMS["nn"], m_left=True, terms=3):
    out = None
    for part in _split3(v)[:terms]:
        ops = (m01, part) if m_left else (part, m01)
        t = lax.dot_general(ops[0], ops[1], dims, preferred_element_type=F32)
        out = t if out is None else out + t
    return out


def _bdot(a, b, mode="nn"):
    return lax.dot_general(a.astype(BF16), b.astype(BF16), _DIMS[mode], preferred_element_type=F32)


def _softplus(v):
    return jnp.maximum(v, 0.0) + jnp.log1p(jnp.exp(-jnp.abs(v)))


def _dt_prep(proj, vec):
    L = proj.shape[0]

    def kern(p_ref, v_ref, dt_ref, cs_ref, sg_ref):
        v = v_ref[...]
        pre = p_ref[:, 0:128] + v[0:1]
        dt = _softplus(pre)
        da = dt * (-jnp.exp(v[1:2]))
        ii = lax.broadcasted_iota(jnp.int32, (Q, Q), 0)
        jj = lax.broadcasted_iota(jnp.int32, (Q, Q), 1)
        ltri = (jj <= ii).astype(BF16)
        lane = lax.broadcasted_iota(jnp.int32, (Q, 128), 1)
        for val, ref in ((dt, dt_ref), (_dot01(ltri, da), cs_ref), (_sigmoid(pre), sg_ref)):
            for g in range(NG):
                moved = val if g == 0 else pltpu.roll(val, 128 - 4 * g, axis=1)
                ref[g] = jnp.where(lane < 4, moved, 0.0)

    blk = pl.BlockSpec((NG, Q, 128), lambda c: (0, c, 0))
    return pl.pallas_call(
        kern, grid=(L // Q,),
        in_specs=[pl.BlockSpec((Q, 256), lambda c: (c, 0)), pl.BlockSpec((8, 128), lambda c: (0, 0))],
        out_specs=(blk, blk, blk),
        out_shape=(jax.ShapeDtypeStruct((NG, L, 128), F32),) * 3,
        name="dt_prep", compiler_params=_cp(("parallel",)),
    )(proj, vec)


def _head_masks():
    lane = lax.broadcasted_iota(jnp.int32, (1, 4 * HD), 1)
    return [((lane >= HD * j) & (lane < HD * (j + 1))) for j in range(4)]


def _expand4(v4, masks):
    R = v4.shape[0]
    out = jnp.zeros((R, 4 * HD), F32)
    for j in range(4):
        out = jnp.where(masks[j], jnp.broadcast_to(v4[:, j:j + 1], (R, 4 * HD)), out)
    return out


def _decay_matrix(cs_col, tri):
    colb = jnp.broadcast_to(cs_col, (Q, Q))
    return jnp.exp(jnp.where(tri, colb - colb.T, -jnp.inf))


def _ssd_fwd(xbc, dt4, cs4, vecg):
    L = xbc.shape[0]
    nc = L // Q

    def kern(x_ref, b_ref, c_ref, dt_ref, cs_ref, v_ref, y_ref, s_ref, S):
        c = pl.program_id(1)

        @pl.when(c == 0)
        def _():
            S[...] = jnp.zeros_like(S)

        masks = _head_masks()
        ii = lax.broadcasted_iota(jnp.int32, (Q, Q), 0)
        jj = lax.broadcasted_iota(jnp.int32, (Q, Q), 1)
        tri = jj <= ii
        for gi in range(GPS):
            xs, ns = slice(256 * gi, 256 * (gi + 1)), slice(NS * gi, NS * (gi + 1))
            dt4v, cs4v = dt_ref[gi], cs_ref[gi]
            dt_b, cs_b = _expand4(dt4v, masks), _expand4(cs4v, masks)
            d_b = _expand4(v_ref[gi], masks)[1:2]
            cs_last = cs_b[Q - 1:Q, :]
            x4, bm, cm = x_ref[:, xs], b_ref[:, ns], c_ref[:, ns]
            xdt = x4 * dt_b
            gm = _bdot(cm, bm, "nt")
            s4 = S[gi]
            s_ref[gi, 0] = s4
            y = _bdot(cm, s4) * jnp.exp(cs_b) + d_b * x4
            m_all = jnp.concatenate([(gm * _decay_matrix(cs4v[:, j:j + 1], tri)).astype(BF16) for j in range(4)], axis=0)
            yd = _bdot(m_all, xdt)
            for j in range(4):
                y = y + jnp.where(masks[j], yd[Q * j:Q * (j + 1)], 0.0)
            y_ref[:, xs] = y
            S[gi] = jnp.exp(cs_last) * s4 + _bdot(bm, xdt * jnp.exp(cs_last - cs_b), "tn")

    sc = pl.BlockSpec((GPS, Q, 128), lambda g, c: (g, c, 0))
    bw = NS * GPS
    return pl.pallas_call(
        kern, grid=(NG // GPS, nc),
        in_specs=[pl.BlockSpec((Q, 256 * GPS), lambda g, c: (c, g)),
                  pl.BlockSpec((Q, bw), lambda g, c: (c, INNER // bw + g)),
                  pl.BlockSpec((Q, bw), lambda g, c: (c, (INNER + NG * NS) // bw + g)),
                  sc, sc, pl.BlockSpec((GPS, 8, 128), lambda g, c: (g, 0, 0))],
        out_specs=(pl.BlockSpec((Q, 256 * GPS), lambda g, c: (c, g)),
                   pl.BlockSpec((GPS, 1, NS, 256), lambda g, c: (g, c, 0, 0))),
        out_shape=(jax.ShapeDtypeStruct((L, INNER), F32), jax.ShapeDtypeStruct((NG, nc, NS, 256), F32)),
        scratch_shapes=[pltpu.VMEM((GPS, NS, 256), F32)], name="ssd_fwd",
        compiler_params=_cp(("parallel", "arbitrary")),
    )(xbc, xbc, xbc, dt4, cs4, vecg)


def _ssd_bwd(xbc, dt4, cs4, sg4, vecg, s_all, dy):
    L = xbc.shape[0]
    nc = L // Q

    def kern(x_ref, b_ref, c_ref, dt_ref, cs_ref, sg_ref, v_ref, s_ref, dy_ref,
             dx_ref, db_ref, dc_ref, ddt_ref, st_ref, dS):
        cc = pl.program_id(1)

        @pl.when(cc == 0)
        def _():
            dS[...] = jnp.zeros_like(dS)
            st_ref[...] = jnp.zeros_like(st_ref)

        masks = _head_masks()
        ii = lax.broadcasted_iota(jnp.int32, (Q, Q), 0)
        jj = lax.broadcasted_iota(jnp.int32, (Q, Q), 1)
        tri = jj <= ii
        utri = (jj >= ii).astype(BF16)
        hsel = ((lax.broadcasted_iota(jnp.int32, (4 * HD, 128), 0) // HD)
                == lax.broadcasted_iota(jnp.int32, (4 * HD, 128), 1)).astype(BF16)
        hrow = ((lax.broadcasted_iota(jnp.int32, (4 * Q, 128), 0) // Q)
                == lax.broadcasted_iota(jnp.int32, (4 * Q, 128), 1)).astype(BF16)
        ones_q = jnp.ones((Q, 128), BF16)
        lane128 = lax.broadcasted_iota(jnp.int32, (Q, 128), 1)

        for gi in range(GPS):
            xs, ns = slice(256 * gi, 256 * (gi + 1)), slice(NS * gi, NS * (gi + 1))
            dt4v, cs4v, sg4v = dt_ref[gi], cs_ref[gi], sg_ref[gi]
            dt_b, cs_b = _expand4(dt4v, masks), _expand4(cs4v, masks)
            vv = _expand4(v_ref[gi], masks)
            a_b = -jnp.exp(vv[0:1])
            d_b = vv[1:2]
            a4 = -jnp.exp(v_ref[gi][0:1, :])
            cs_last = cs_b[Q - 1:Q, :]
            ecs = jnp.exp(cs_b)
            decay = jnp.exp(cs_last - cs_b)
            elast = jnp.exp(cs_last)
            x4, bm, cm, dyv = x_ref[:, xs], b_ref[:, ns], c_ref[:, ns], dy_ref[:, xs]
            s4 = s_ref[gi, 0]
            dsn = dS[gi]
            xdt = x4 * dt_b
            gm = _bdot(cm, bm, "nt")
            dye = dyv * ecs
            yoff = ecs * _bdot(cm, s4)
            t4 = _bdot(bm, dsn) * decay
            lms, mhs = [], []
            for j in range(4):
                colb = jnp.broadcast_to(cs4v[:, j:j + 1], (Q, Q))
                lms.append(jnp.exp(jnp.where(tri, colb - colb.T, -jnp.inf)))
                mhs.append(gm * lms[j])
            m_all = jnp.concatenate([m.astype(BF16) for m in mhs], axis=0)
            dy_m = jnp.concatenate([jnp.where(masks[j], dyv, 0.0).astype(BF16) for j in range(4)], axis=0)
            dxdt = t4 + _bdot(m_all, dy_m, "tn")
            dm_all = _bdot(dy_m, xdt, "nt")
            dg = jnp.zeros((Q, Q), F32)
            for j in range(4):
                dg = dg + dm_all[Q * j:Q * (j + 1)] * lms[j]
            e_all = dm_all * jnp.concatenate(mhs, axis=0)
            rsum = _dot01(ones_q, e_all, m_left=False, terms=2)
            da4 = -_dot01(hrow, e_all, _DIMS["tn"], m_left=False, terms=2)
            for j in range(4):
                da4 = da4 + jnp.where(lane128 == j, rsum[Q * j:Q * (j + 1)], 0.0)
            xt = xdt * t4
            tail = jnp.sum(xt, axis=0, keepdims=True) + elast * jnp.sum(s4 * dsn, axis=0, keepdims=True)
            gd_raw = jnp.sum(dyv * x4, axis=0, keepdims=True)
            stacked = jnp.concatenate([dyv * yoff - xt, dxdt * x4, jnp.broadcast_to(tail, (8, 4 * HD)),
                                       jnp.broadcast_to(gd_raw, (8, 4 * HD))], axis=0)
            seg = _dot01(hsel, stacked, m_left=False, terms=2)
            dda4 = _dot01(utri, da4 + seg[0:Q], terms=2) + seg[2 * Q:2 * Q + 1]
            ddt_ref[gi] = (dda4 * a4 + seg[Q:2 * Q]) * sg4v
            ga = jnp.sum(dda4 * dt4v * a4, axis=0, keepdims=True)
            row = lax.broadcasted_iota(jnp.int32, (8, 128), 0)
            st_ref[gi] += jnp.where(row == 0, ga, jnp.where(row == 1, seg[2 * Q + 8:2 * Q + 9], 0.0))
            dx_ref[:, xs] = d_b * dyv + dxdt * dt_b
            dc_ref[:, ns] = _bdot(dg, bm) + _bdot(dye, s4, "nt")
            db_ref[:, ns] = _bdot(dg, cm, "tn") + _bdot(xdt * decay, dsn, "nt")
            dS[gi] = elast * dsn + _bdot(cm, dye, "tn")

    rv = lambda c: nc - 1 - c
    sc = pl.BlockSpec((GPS, Q, 128), lambda g, c: (g, rv(c), 0))
    bw = NS * GPS
    return pl.pallas_call(
        kern, grid=(NG // GPS, nc),
        in_specs=[pl.BlockSpec((Q, 256 * GPS), lambda g, c: (rv(c), g)),
                  pl.BlockSpec((Q, bw), lambda g, c: (rv(c), INNER // bw + g)),
                  pl.BlockSpec((Q, bw), lambda g, c: (rv(c), (INNER + NG * NS) // bw + g)),
                  sc, sc, sc, pl.BlockSpec((GPS, 8, 128), lambda g, c: (g, 0, 0)),
                  pl.BlockSpec((GPS, 1, NS, 256), lambda g, c: (g, rv(c), 0, 0)),
                  pl.BlockSpec((Q, 256 * GPS), lambda g, c: (rv(c), g))],
        out_specs=(pl.BlockSpec((Q, 256 * GPS), lambda g, c: (rv(c), g)),
                   pl.BlockSpec((Q, bw), lambda g, c: (rv(c), g)),
                   pl.BlockSpec((Q, bw), lambda g, c: (rv(c), g)),
                   pl.BlockSpec((GPS, Q, 128), lambda g, c: (g, rv(c), 0)),
                   pl.BlockSpec((GPS, 8, 128), lambda g, c: (g, 0, 0))),
        out_shape=(jax.ShapeDtypeStruct((L, INNER), F32), jax.ShapeDtypeStruct((L, NG * NS), F32),
                   jax.ShapeDtypeStruct((L, NG * NS), F32), jax.ShapeDtypeStruct((NG, L, 128), F32),
                   jax.ShapeDtypeStruct((NG, 8, 128), F32)),
        scratch_shapes=[pltpu.VMEM((GPS, NS, 256), F32)], name="ssd_bwd",
        compiler_params=_cp(("parallel", "arbitrary")),
    )(xbc, xbc, xbc, dt4, cs4, sg4, vecg, s_all, dy)


def _dt_bwd(ddt, dproj, tl=256):
    L = ddt.shape[1]

    def kern(d_ref, _, dp_ref, gs_ref):
        @pl.when(pl.program_id(0) == 0)
        def _():
            gs_ref[...] = jnp.zeros_like(gs_ref)

        d = d_ref[0]
        for g in range(1, NG):
            d = d + pltpu.roll(d_ref[g], 4 * g, axis=1)
        gs_ref[...] += jnp.broadcast_to(jnp.sum(d, axis=0, keepdims=True), (8, 128))
        dp_ref[...] = jnp.concatenate([d, jnp.zeros_like(d)], axis=1).astype(BF16)

    return pl.pallas_call(
        kern, grid=(L // tl,),
        in_specs=[pl.BlockSpec((NG, tl, 128), lambda i: (0, i, 0)), pl.BlockSpec(memory_space=pl.ANY)],
        out_specs=(pl.BlockSpec((tl, 256), lambda i: (i, C_DT // 256)), pl.BlockSpec((8, 128), lambda i: (0, 0))),
        out_shape=(jax.ShapeDtypeStruct(dproj.shape, BF16), jax.ShapeDtypeStruct((8, 128), F32)),
        input_output_aliases={1: 0}, name="dt_bwd", compiler_params=_cp(("arbitrary",)),
    )(ddt, dproj)


GW = INNER // NG


def _gnorm_fwd(y, proj, w, tl=256):
    L = y.shape[0]
    zoff = C_Z // 1024

    def kern(y_ref, z_ref, w_ref, o_ref):
        z = z_ref[...].astype(F32)
        yz = y_ref[...] * (z * _sigmoid(z))
        wv = w_ref[...]
        for k in range(1024 // GW):
            sl = slice(GW * k, GW * (k + 1))
            v = yz[:, sl]
            rg = lax.rsqrt(jnp.mean(v * v, axis=-1, keepdims=True) + EPS)
            o_ref[:, sl] = ((v * rg) * wv[:, sl]).astype(BF16)

    blk = pl.BlockSpec((tl, 1024), lambda i, j: (i, j))
    return pl.pallas_call(
        kern, grid=(L // tl, 2),
        in_specs=[blk, pl.BlockSpec((tl, 1024), lambda i, j: (i, zoff + j)), pl.BlockSpec((1, 1024), lambda i, j: (0, j))],
        out_specs=blk, out_shape=jax.ShapeDtypeStruct((L, INNER), BF16), name="gnorm_fwd",
        compiler_params=_cp(("parallel", "parallel")),
    )(y, proj, w.reshape(1, INNER))


def _gnorm_bwd(dyb, y, proj, w, dproj, tl=256):
    L = y.shape[0]
    zoff = C_Z // 1024

    def kern(d_ref, y_ref, z_ref, w_ref, _, dy_ref, dp_ref, gw_ref):
        @pl.when(pl.program_id(1) == 0)
        def _():
            gw_ref[...] = jnp.zeros_like(gw_ref)

        z = z_ref[...].astype(F32)
        sg = _sigmoid(z)
        sz = z * sg
        yv = y_ref[...]
        yz = yv * sz
        dv = d_ref[...]
        wv = w_ref[...]
        for k in range(1024 // GW):
            sl = slice(GW * k, GW * (k + 1))
            v = yz[:, sl]
            rg = lax.rsqrt(jnp.mean(v * v, axis=-1, keepdims=True) + EPS)
            vn = v * rg
            dk = dv[:, sl]
            gw_ref[:, sl] += jnp.broadcast_to(jnp.sum(dk * vn, axis=0, keepdims=True), (8, GW))
            dvn = dk * wv[:, sl]
            dyz = rg * (dvn - vn * jnp.mean(dvn * vn, axis=-1, keepdims=True))
            dy_ref[:, sl] = dyz * sz[:, sl]
            dp_ref[:, sl] = (dyz * yv[:, sl] * (sg[:, sl] * (1.0 + z[:, sl] * (1.0 - sg[:, sl])))).astype(BF16)

    blk = pl.BlockSpec((tl, 1024), lambda j, i: (i, j))
    zblk = pl.BlockSpec((tl, 1024), lambda j, i: (i, zoff + j))
    return pl.pallas_call(
        kern, grid=(2, L // tl),
        in_specs=[blk, blk, zblk, pl.BlockSpec((1, 1024), lambda j, i: (0, j)), pl.BlockSpec(memory_space=pl.ANY)],
        out_specs=(blk, zblk, pl.BlockSpec((8, 1024), lambda j, i: (0, j))),
        out_shape=(jax.ShapeDtypeStruct((L, INNER), F32), jax.ShapeDtypeStruct(dproj.shape, BF16),
                   jax.ShapeDtypeStruct((8, INNER), F32)),
        input_output_aliases={4: 1}, name="gnorm_bwd", compiler_params=_cp(("parallel", "arbitrary")),
    )(dyb, y, proj, w.reshape(1, INNER), dproj)


def _merge_fwd(proj, bg, br_a, br_b, tl=256):
    L = proj.shape[0]
    goff = C_GATE // 1024

    def kern(g1_ref, g2_ref, b1_ref, b2_ref, a_ref, b_ref, o_ref):
        g1 = _sigmoid(g1_ref[...].astype(F32) + b1_ref[...])
        g2 = _sigmoid(g2_ref[...].astype(F32) + b2_ref[...])
        o_ref[...] = (g1 * a_ref[...] + g2 * b_ref[...]).astype(BF16)

    row = pl.BlockSpec((tl, 1024), lambda i: (i, 0))
    bg2 = bg.reshape(1, 2 * D)
    return pl.pallas_call(
        kern, grid=(L // tl,),
        in_specs=[pl.BlockSpec((tl, 1024), lambda i: (i, goff)), pl.BlockSpec((tl, 1024), lambda i: (i, goff + 1)),
                  pl.BlockSpec((1, 1024), lambda i: (0, 0)), pl.BlockSpec((1, 1024), lambda i: (0, 1)), row, row],
        out_specs=row, out_shape=jax.ShapeDtypeStruct((L, D), BF16), name="merge_fwd",
        compiler_params=_cp(("parallel",)),
    )(proj, proj, bg2, bg2, br_a, br_b)


def _merge_bwd(dm, proj, bg, br_a, br_b, dproj, tl=256):
    L = proj.shape[0]
    goff = C_GATE // 1024

    def kern(dm_ref, g_ref, b_ref, a_ref, bb_ref, _, dbr_ref, dp_ref, gb_ref):
        j = pl.program_id(0)

        @pl.when(pl.program_id(1) == 0)
        def _():
            gb_ref[...] = jnp.zeros_like(gb_ref)

        g = _sigmoid(g_ref[...].astype(F32) + b_ref[...])
        br = jnp.where(j == 0, a_ref[...], bb_ref[...])
        dmv = dm_ref[...]
        dbr_ref[0] = (dmv * g).astype(BF16)
        dgate = dmv * br * g * (1.0 - g)
        gb_ref[...] += jnp.broadcast_to(jnp.sum(dgate, axis=0, keepdims=True), (8, 1024))
        dp_ref[...] = dgate.astype(BF16)

    row = pl.BlockSpec((tl, 1024), lambda j, i: (i, 0))
    gblk = pl.BlockSpec((tl, 1024), lambda j, i: (i, goff + j))
    return pl.pallas_call(
        kern, grid=(2, L // tl),
        in_specs=[row, gblk, pl.BlockSpec((1, 1024), lambda j, i: (0, j)), row, row, pl.BlockSpec(memory_space=pl.ANY)],
        out_specs=(pl.BlockSpec((1, tl, 1024), lambda j, i: (j, i, 0)), gblk, pl.BlockSpec((8, 1024), lambda j, i: (0, j))),
        out_shape=(jax.ShapeDtypeStruct((2, L, D), BF16), jax.ShapeDtypeStruct(dproj.shape, BF16),
                   jax.ShapeDtypeStruct((8, 2 * D), F32)),
        input_output_aliases={5: 1}, name="merge_bwd", compiler_params=_cp(("parallel", "arbitrary")),
    )(dm, proj, bg.reshape(1, 2 * D), br_a, br_b, dproj)


def _coords():
    return lax.axis_index("x"), lax.axis_index("y"), lax.axis_index("c")


def _other_chips(sk):
    xk, yk = sk // 2, sk % 2
    return [((1 - xk, yk), 2 * (1 - xk) + yk), ((xk, 1 - yk), 2 * xk + 1 - yk), ((1 - xk, 1 - yk), 2 * (1 - xk) + 1 - yk)]


def _rows(start, size):
    assert size % 128 == 0
    return pl.ds(pl.multiple_of(start, 128), size)


def _per_chip(fn):
    x, y, _ = _coords()
    s = 2 * x + y
    for sk in range(4):
        pl.when(s == sk)(functools.partial(fn, sk))


XTRA = PIECE - PMAIN


def _place(shard, full_shape, block, index_map, idx, name, blk0=0, nblk=None, dep=None, into=None):
    in_block = block[-2:]
    if nblk is None:
        nblk = shard.shape[0] // in_block[0]

    def kern(idx_ref, s_ref, *rest):
        o_ref = rest[-1]
        o_ref[...] = s_ref[...].astype(BF16).reshape(o_ref.shape)

    extra = ([dep] if dep is not None else []) + ([into] if into is not None else [])
    grid_spec = pltpu.PrefetchScalarGridSpec(
        num_scalar_prefetch=1, grid=(nblk,),
        in_specs=[pl.BlockSpec(in_block, lambda i, idx_ref: (blk0 + i, 0))] + [_ANY] * len(extra),
        out_specs=pl.BlockSpec(block, index_map))
    aliases = {1 + len(extra): 0} if into is not None else {}
    return pl.pallas_call(kern, grid_spec=grid_spec, out_shape=jax.ShapeDtypeStruct(full_shape, BF16), name=name,
                          input_output_aliases=aliases, compiler_params=_cp(("arbitrary",)))(idx, shard, *extra)


_SEM = pl.BlockSpec(memory_space=pltpu.SEMAPHORE)
_EFFECT = pltpu.SideEffectType.DATAFLOW_SIDE_EFFECTING


_ANY = pl.BlockSpec(memory_space=pl.ANY)


def _tie(v, dep, name):
    def body(v_ref, dep_ref, o_ref):
        del v_ref, dep_ref, o_ref

    return pl.pallas_call(body, out_shape=jax.ShapeDtypeStruct(v.shape, v.dtype), in_specs=[_ANY, _ANY],
                          out_specs=_ANY, input_output_aliases={0: 0}, name=name)(v, dep)


def _split_call(name, arrays, start=None, wait=None, wait_sems=None, after=None):
    keys = list(arrays)
    n = len(keys)
    n_start = start.n if start is not None else 0
    afters = [] if after is None else (list(after) if isinstance(after, (list, tuple)) else [after])

    def body(*refs):
        pos = n
        if wait is not None:
            wss, wrs = refs[pos], refs[pos + 1]
            pos += 2
        pos += len(afters)
        if start is not None:
            nss, nrs = refs[pos], refs[pos + 1]
            pos += 2
        R = dict(zip(keys, refs[pos:pos + n]))
        token = refs[pos + n]
        x, y, c = _coords()

        def desc(src, dst, dev, ss, rs, k):
            return pltpu.make_async_remote_copy(src_ref=src, dst_ref=dst, send_sem=ss.at[k], recv_sem=rs.at[k],
                                                device_id=dev, device_id_type=MESH)

        def run(sk):
            if wait is not None:
                for k, (snd, land) in enumerate(wait.copies(sk, R)):
                    if snd is not None:
                        desc(snd[0], snd[1], snd[2], wss, wrs, k).wait_send()
                    if land is not None:
                        desc(land, land, (x, y, c), wss, wrs, k).wait_recv()
            if start is not None:
                for k, (snd, land) in enumerate(start.copies(sk, R)):
                    if snd is not None:
                        desc(snd[0], snd[1], snd[2], nss, nrs, k).start()

        _per_chip(run)
        token[...] = jnp.zeros_like(token)

    hbm = pl.BlockSpec(memory_space=HBM)
    vals = [arrays[k] for k in keys]
    ins, in_specs = list(vals), [hbm] * n
    if wait is not None:
        ins += list(wait_sems)
        in_specs += [_SEM, _SEM]
    ins += afters
    in_specs += [pl.BlockSpec(memory_space=pl.ANY)] * len(afters)
    out_shape, out_specs = [], []
    if start is not None:
        out_shape += [pltpu.SemaphoreType.DMA((n_start,)), pltpu.SemaphoreType.DMA((n_start,))]
        out_specs += [_SEM, _SEM]
    first = len(out_shape)
    out_shape += [jax.ShapeDtypeStruct(v.shape, v.dtype) for v in vals] + [jax.ShapeDtypeStruct((8, 128), F32)]
    out_specs += [hbm] * n + [pl.BlockSpec(memory_space=pltpu.VMEM)]
    res = pl.pallas_call(
        body, out_shape=tuple(out_shape), in_specs=in_specs, out_specs=tuple(out_specs),
        input_output_aliases={i: first + i for i in range(n)}, name=name,
        compiler_params=pltpu.CompilerParams(has_side_effects=_EFFECT),
    )(*ins)
    sems = (res[0], res[1]) if start is not None else None
    return dict(zip(keys, res[first:first + n])), sems, res[-1]


class _Plan:
    def __init__(self, n, copies):
        self.n, self.copies = n, copies


_HM, _HX = PMAIN // 2, XTRA // 2
WAVE0 = 768
WAVES = ((0, WAVE0), (WAVE0, _HM - WAVE0))
_WIN = {
    "wq0": (True, "wct", lambda r, sc, hc: r.at[_rows(PMAIN * sc + _HM * hc + WAVES[0][0], WAVES[0][1]), :]),
    "wq1": (True, "wct", lambda r, sc, hc: r.at[_rows(PMAIN * sc + _HM * hc + WAVES[1][0], WAVES[1][1]), :]),
    "xt": (True, "xt", lambda r, sc, hc: r.at[sc, _rows(_HX * hc, _HX), :]),
    "w1": (True, "w1", lambda r, sc, hc: r.at[_rows(512 * hc, 512), pl.ds(1024 * sc, 1024)]),
    "w2": (True, "w2", lambda r, sc, hc: r.at[_rows(1024 * sc + 512 * hc, 512), :]),
    "wa": (True, "wa", lambda r, sc, hc: r.at[_rows(256 * sc + 128 * hc, 128), :]),
    "wb": (True, "wb", lambda r, sc, hc: r.at[_rows(512 * sc + 256 * hc, 256), :]),
    "wo": (True, "wo", lambda r, sc, hc: r.at[_rows(256 * sc + 128 * hc, 128), :]),
    "cw": (False, "cw", lambda r, sc, hc: r.at[sc]),
}


_PIECE_SRC = {
    "wq0": lambda p, hc: p.at[_rows(_HM * hc + WAVES[0][0], WAVES[0][1]), :],
    "wq1": lambda p, hc: p.at[_rows(_HM * hc + WAVES[1][0], WAVES[1][1]), :],
    "xt": lambda p, hc: p.at[_rows(PMAIN + _HX * hc, _HX), :],
}


def _ag_chips_plan(keys):
    def copies(sk, R):
        _, _, c = _coords()
        out = []
        for key in keys:
            _, arr, win = _WIN[key]
            for (px, py), ps in _other_chips(sk):
                dst = win(R[arr], sk, c)
                src = _PIECE_SRC[key](R["piece"], c) if key in _PIECE_SRC else dst
                out.append(((src, dst, (px, py, c)), win(R[arr], ps, c)))
        return out
    return _Plan(3 * len(keys), copies)


def _ag_sibling_plan(keys):
    keys = [k for k in keys if _WIN[k][0]]

    def copies(sk, R):
        x, y, c = _coords()
        out = []
        for key in keys:
            _, arr, win = _WIN[key]
            for _, ps in _other_chips(sk):
                w = win(R[arr], ps, c)
                out.append(((w, w, (x, y, 1 - c)), win(R[arr], ps, 1 - c)))
        return out
    return _Plan(3 * len(keys), copies)


def _in_proj_wave(h, wct, wave, proj=None, tm=2048):
    L = h.shape[0]
    tm = min(tm, L)
    off, size = WAVES[wave]
    start = lambda j: pl.multiple_of(_HM * j + off, 128)

    def kern(h_ref, w_ref, *rest):
        o_ref = rest[-1]
        o_ref[...] = lax.dot_general(h_ref[...], w_ref[...], _DIMS["nt"], preferred_element_type=F32).astype(BF16)

    in_specs = [pl.BlockSpec((tm, D), lambda j, i: (i, 0)),
                pl.BlockSpec((pl.Element(size), pl.Element(D)), lambda j, i: (start(j), 0))]
    args, aliases = [h, wct], {}
    if proj is not None:
        in_specs.append(pl.BlockSpec(memory_space=pl.ANY))
        args.append(proj)
        aliases = {2: 0}
    return pl.pallas_call(
        kern, grid=(8, L // tm), in_specs=in_specs,
        out_specs=pl.BlockSpec((pl.Element(tm), pl.Element(size)), lambda j, i: (i * tm, start(j))),
        out_shape=jax.ShapeDtypeStruct((L, NCW), BF16), input_output_aliases=aliases,
        name="in_proj_wave%d" % wave, compiler_params=_cp(("parallel", "parallel")),
    )(*args)


def _fix_wct(wct, xt):
    nb = PMAIN // XTRA

    def kern(w_ref, x_ref, o_ref):
        k = pl.program_id(0)
        xv = x_ref[0]
        o_ref[...] = jnp.where(k < 3, (w_ref[...].astype(F32) + xv.astype(F32)).astype(BF16), xv)

    blk = pl.BlockSpec((XTRA, D), lambda k: (nb * (k + 1), 0))
    rblk = pl.BlockSpec((XTRA, D), lambda k: (jnp.where(k < 3, nb * (k + 1), 0), 0))
    return pl.pallas_call(
        kern, grid=(4,), in_specs=[rblk, pl.BlockSpec((1, XTRA, D), lambda k: (k, 0, 0))], out_specs=blk,
        out_shape=jax.ShapeDtypeStruct(wct.shape, BF16), input_output_aliases={0: 0}, name="fix_wct",
        compiler_params=_cp(("arbitrary",)),
    )(wct, xt)


_HP = PIECE // 2
_GWIN = [
    lambda r, sc, hc: r.at[_rows(PMAIN * sc + _HP * hc, _HP), :],
    lambda r, sc, hc: r.at[_rows(512 * hc, 512), pl.ds(1024 * sc, 1024)],
    lambda r, sc, hc: r.at[_rows(1024 * sc + 512 * hc, 512), :],
    lambda r, sc, hc: r.at[_rows(256 * sc + 128 * hc, 128), :],
    lambda r, sc, hc: r.at[_rows(512 * sc + 256 * hc, 256), :],
    lambda r, sc, hc: r.at[_rows(256 * sc + 128 * hc, 128), :],
]
HALF_SHAPES = [(PIECE // 2, D), (512, 1024), (512, 1024), (128, 1024), (256, 1024), (128, 1024)]


def _rs_sibling_plan(ts):
    def copies(sk, R):
        x, y, c = _coords()
        out = []
        for t in ts:
            for sc in range(4):
                land = R["ra%d" % t].at[sc]
                out.append(((_GWIN[t](R["g%d" % t], sc, 1 - c), land, (x, y, 1 - c)), land))
        return out
    return _Plan(4 * len(ts), copies)


def _rs_chips_plan(ts):
    def copies(sk, R):
        _, _, c = _coords()
        out = []
        for t in ts:
            for j, ((px, py), ps) in enumerate(_other_chips(sk)):
                land = R["rb%d" % t].at[j]
                out.append(((R["hb%d" % t].at[ps], land, (px, py, c)), land))
        return out
    return _Plan(3 * len(ts), copies)


def _rs_share_plan(ts):
    def copies(sk, R):
        x, y, c = _coords()
        out = []
        for t in ts:
            rows = HALF_SHAPES[t][0]
            mine = R["f%d" % t].at[_rows(rows * c, rows), :]
            out.append(((mine, mine, (x, y, 1 - c)), R["f%d" % t].at[_rows(rows * (1 - c), rows), :]))
        return out
    return _Plan(len(ts), copies)


def _half_tiling(t):
    rows, cols = HALF_SHAPES[t]
    if t == 0:
        return (rows // 2, cols), 2, lambda i: (i, 0)
    return (rows, cols), 1, lambda i: (0, 0)


def _window_spec(t, blk):
    if t == 0:
        return pl.BlockSpec((pl.Element(blk[0]), pl.Element(blk[1])), lambda i, sc, idx_ref: (
            pl.multiple_of(PMAIN * sc + _HP * idx_ref[1] + blk[0] * i, 128), 0))
    if t == 1:
        return pl.BlockSpec(blk, lambda i, sc, idx_ref: (idx_ref[1], sc))
    return pl.BlockSpec(blk, lambda i, sc, idx_ref: (2 * sc + idx_ref[1], 0))


def _chip_sum(g, ra, t, idx, name):
    rows, cols = HALF_SHAPES[t]
    blk, nblk, inner = _half_tiling(t)

    def kern(idx_ref, g_ref, r_ref, hb_ref, hf_ref):
        v = g_ref[...].astype(F32) + r_ref[0].astype(F32)
        hb_ref[0] = v.astype(BF16)

        @pl.when(pl.program_id(1) == idx_ref[0])
        def _():
            hf_ref[...] = v

    omap = lambda i, sc, idx_ref: (sc,) + inner(i)
    grid_spec = pltpu.PrefetchScalarGridSpec(
        num_scalar_prefetch=1, grid=(nblk, 4),
        in_specs=[_window_spec(t, blk), pl.BlockSpec((1,) + blk, omap)],
        out_specs=(pl.BlockSpec((1,) + blk, omap), pl.BlockSpec(blk, lambda i, sc, idx_ref: inner(i))))
    return pl.pallas_call(
        kern, grid_spec=grid_spec,
        out_shape=(jax.ShapeDtypeStruct((4, rows, cols), BF16), jax.ShapeDtypeStruct((rows, cols), F32)),
        name=name, compiler_params=_cp(("parallel", "arbitrary")),
    )(idx, g, ra)


def _final_sum(hf, rb, t, idx, name):
    rows, cols = HALF_SHAPES[t]
    blk, nblk, inner = _half_tiling(t)
    nbr = rows // blk[0]

    def kern(idx_ref, h_ref, r_ref, o_ref):
        o_ref[...] = ((h_ref[...] + r_ref[0].astype(F32)) + r_ref[1].astype(F32)) + r_ref[2].astype(F32)

    def omap(i, idx_ref):
        r, cidx = inner(i)
        return nbr * idx_ref[1] + r, cidx

    grid_spec = pltpu.PrefetchScalarGridSpec(
        num_scalar_prefetch=1, grid=(nblk,),
        in_specs=[pl.BlockSpec(blk, lambda i, idx_ref: inner(i)),
                  pl.BlockSpec((3,) + blk, lambda i, idx_ref: (0,) + inner(i))],
        out_specs=pl.BlockSpec(blk, omap))
    return pl.pallas_call(
        kern, grid_spec=grid_spec, out_shape=jax.ShapeDtypeStruct((2 * rows, cols), F32),
        name=name, compiler_params=_cp(("parallel",)),
    )(idx, hf, rb)


class _ReduceScatter:
    def __init__(self, ts, grads, idx, tag):
        self.ts, self.idx, self.tag = ts, idx, tag
        arr = {}
        for t in ts:
            arr["g%d" % t] = grads[t]
            arr["ra%d" % t] = lax.empty((4,) + HALF_SHAPES[t], BF16)
        self.plan = _rs_sibling_plan(ts)
        self.arr, self.sems, self.token = _split_call("rs_sibling_start_" + tag, arr, start=self.plan)

    def chips(self, after):
        arr, _, _ = _split_call("rs_sibling_wait_" + self.tag, self.arr, wait=self.plan, wait_sems=self.sems, after=after)
        brr, self.hf = {}, {}
        for t in self.ts:
            hb, self.hf[t] = _chip_sum(arr["g%d" % t], arr["ra%d" % t], t, self.idx, "chip_sum_%d" % t)
            brr["hb%d" % t] = hb
            brr["rb%d" % t] = lax.empty((3,) + HALF_SHAPES[t], BF16)
        self.plan = _rs_chips_plan(self.ts)
        self.arr, self.sems, self.token = _split_call("rs_chips_start_" + self.tag, brr, start=self.plan)
        return self.token

    def share(self, after):
        brr, _, _ = _split_call("rs_chips_wait_" + self.tag, self.arr, wait=self.plan, wait_sems=self.sems, after=after)
        frr = {"f%d" % t: _final_sum(self.hf[t], brr["rb%d" % t], t, self.idx, "final_sum_%d" % t) for t in self.ts}
        self.plan = _rs_share_plan(self.ts)
        self.arr, self.sems, self.token = _split_call("rs_share_start_" + self.tag, frr, start=self.plan)
        return self.token

    def result(self, after):
        frr, _, _ = _split_call("rs_share_wait_" + self.tag, self.arr, wait=self.plan, wait_sems=self.sems, after=after)
        return {t: frr["f%d" % t] for t in self.ts}


def _all8_plan(key):
    def copies(sk, R):
        x, y, c = _coords()
        own = R[key].at[4 * x + 2 * y + c]
        out = []
        for k in range(1, 8):
            dev = ((1 - x) if (k >> 2) & 1 else x, (1 - y) if (k >> 1) & 1 else y, (1 - c) if k & 1 else c)
            out.append(((own, own, dev), R[key].at[4 * dev[0] + 2 * dev[1] + dev[2]]))
        return out
    return _Plan(7, copies)


def _small_all_gather(v):
    def body(v_ref, o_ref, send_sems, recv_sems, loc_sem):
        x, y, c = _coords()
        me = 4 * x + 2 * y + c
        lc = pltpu.make_async_copy(v_ref, o_ref.at[me], loc_sem)
        lc.start()
        cps = []
        for k in range(1, 8):
            fx, fy, fc = (k >> 2) & 1, (k >> 1) & 1, k & 1
            dev = ((1 - x) if fx else x, (1 - y) if fy else y, (1 - c) if fc else c)
            cp = pltpu.make_async_remote_copy(src_ref=v_ref, dst_ref=o_ref.at[me], send_sem=send_sems.at[k - 1],
                                              recv_sem=recv_sems.at[k - 1], device_id=dev, device_id_type=MESH)
            cp.start()
            cps.append((cp, 4 * dev[0] + 2 * dev[1] + dev[2]))
        for k, (cp, frm) in enumerate(cps):
            got = o_ref.at[frm]
            pltpu.make_async_remote_copy(src_ref=got, dst_ref=got, send_sem=send_sems.at[k], recv_sem=recv_sems.at[k],
                                         device_id=(x, y, c), device_id_type=MESH).wait_recv()
        for cp, _ in cps:
            cp.wait_send()
        lc.wait()

    hbm = pl.BlockSpec(memory_space=HBM)
    return pl.pallas_call(
        body, out_shape=jax.ShapeDtypeStruct((8,) + v.shape, F32), in_specs=[hbm], out_specs=hbm,
        scratch_shapes=[pltpu.SemaphoreType.DMA((7,)), pltpu.SemaphoreType.DMA((7,)), pltpu.SemaphoreType.DMA(())],
        name="small_all_gather", compiler_params=pltpu.CompilerParams(has_side_effects=True),
    )(v)


def _sum8(v, name="small_sum"):
    def kern(v_ref, o_ref):
        acc = v_ref[0]
        for k in range(1, 8):
            acc = acc + v_ref[k]
        o_ref[...] = acc

    return pl.pallas_call(kern, out_shape=jax.ShapeDtypeStruct(v.shape[1:], F32), name=name)(v)


def _adamw(w, g, m, v, name, tr=128, blk0=0, nblk=None, into=None, copy_g=False):
    R, C = w.shape
    tr = min(tr, R)
    if nblk is None:
        assert R % tr == 0 and blk0 == 0
        nblk = R // tr
    n_out = 4 if copy_g else 3

    def kern(*refs):
        w_ref, g_ref, m_ref, v_ref = refs[:4]
        d_ref, mo_ref, vo_ref = refs[-n_out:][:3]
        gv = g_ref[...]
        mn = ADAM_B1 * m_ref[...] + (1.0 - ADAM_B1) * gv
        vn = ADAM_B2 * v_ref[...] + (1.0 - ADAM_B2) * (gv * gv)
        m_hat = mn / (1.0 - ADAM_B1 ** ADAM_STEP)
        v_hat = vn / (1.0 - ADAM_B2 ** ADAM_STEP)
        d_ref[...] = -ADAM_LR * (m_hat / (jnp.sqrt(v_hat) + ADAM_EPS) + ADAM_WD * w_ref[...])
        mo_ref[...] = mn
        vo_ref[...] = vn
        if copy_g:
            refs[-1][...] = gv

    blk = pl.BlockSpec((tr, C), lambda i: (blk0 + i, 0))
    sd = jax.ShapeDtypeStruct((R, C), F32)
    in_specs, args, aliases = [blk] * 4, [w, g, m, v], {}
    if into is not None:
        in_specs += [pl.BlockSpec(memory_space=pl.ANY)] * 3
        args += list(into)
        aliases = {4: 0, 5: 1, 6: 2}
    return pl.pallas_call(kern, grid=(nblk,), in_specs=in_specs, out_specs=(blk,) * n_out, out_shape=(sd,) * n_out,
                          input_output_aliases=aliases, name=name, compiler_params=_cp(("parallel",)))(*args)


def _adamw_w_in(wt, gp, mt, vt, offs, name, r0, tr, nblk, views, into=None, blk_key=None):
    el = lambda n: (pl.Element(n), pl.Element(D))
    first = (lambda o: r0) if blk_key is None else (lambda o: tr * o[blk_key])
    own = pl.BlockSpec(el(tr), lambda i, o: (pl.multiple_of(first(o) + tr * i, 8), 0))

    def view(k):
        return pl.BlockSpec(el(tr), lambda i, o: (pl.multiple_of(jnp.maximum(first(o) + tr * i + o[k], 0), 8), 0))

    def kern(o_ref, w_ref, m_ref, v_ref, *refs):
        g_refs, (d_ref, mo_ref, vo_ref, go_ref) = refs[:len(views)], refs[-4:]
        gv = g_refs[0][...]
        if len(views) == 2:
            row = first(o_ref) + tr * pl.program_id(0) + lax.broadcasted_iota(jnp.int32, (tr, D), 0)
            gv = jnp.where(row < o_ref[2], gv, g_refs[1][...])
        mn = ADAM_B1 * m_ref[...] + (1.0 - ADAM_B1) * gv
        vn = ADAM_B2 * v_ref[...] + (1.0 - ADAM_B2) * (gv * gv)
        m_hat = mn / (1.0 - ADAM_B1 ** ADAM_STEP)
        v_hat = vn / (1.0 - ADAM_B2 ** ADAM_STEP)
        d_ref[...] = -ADAM_LR * (m_hat / (jnp.sqrt(v_hat) + ADAM_EPS) + ADAM_WD * w_ref[...])
        mo_ref[...] = mn
        vo_ref[...] = vn
        go_ref[...] = gv

    in_specs = [own, own, own] + [view(k) for k in views]
    args = [wt, mt, vt] + [gp] * len(views)
    aliases = {}
    if into is not None:
        in_specs += [pl.BlockSpec(memory_space=pl.ANY)] * 4
        args += list(into)
        aliases = {1 + len(args) - 4 + j: j for j in range(4)}
    grid_spec = pltpu.PrefetchScalarGridSpec(num_scalar_prefetch=1, grid=(nblk,), in_specs=in_specs,
                                             out_specs=(own,) * 4)
    sd = jax.ShapeDtypeStruct(wt.shape, F32)
    return pl.pallas_call(kern, grid_spec=grid_spec, out_shape=(sd,) * 4, input_output_aliases=aliases, name=name,
                          compiler_params=_cp(("parallel",)))(offs, *args)


def _to_piece(wt, s):
    z = lambda n: jnp.zeros((n, D), wt.dtype)
    pads = [functools.partial(lambda k, w: jnp.pad(w, ((8 * k, PIECE - W_SHARD - 8 * k), (0, 0))).astype(BF16), k)
            for k in range(3)]
    last = lambda w: jnp.concatenate([z(24), w[:744], w[776:], w[744:776], z(PIECE - 24 - W_SHARD)], axis=0).astype(BF16)
    return lax.switch(s, pads + [last], wt)


def _from_piece(p, s):
    cuts = [functools.partial(lambda k, q: q[8 * k:8 * k + W_SHARD], k) for k in range(3)]
    last = lambda q: jnp.concatenate([q[24:768], q[2816:2848], q[768:2816]], axis=0)
    return lax.switch(s, cuts + [last], p)


_SMALL = [("b_gate", 2048), ("ssm_conv_b", 4096), ("dt_bias", 32), ("A_log", 32), ("D_skip", 32),
          ("ssm_norm_w", 2048), ("norm_mlp", 1024), ("norm_final", 1024), ("sc_conv_w", 3072), ("ssm_conv_w", 16384),
          ("loss", 1)]


def _pack(vals, table, rows):
    parts = []
    for name, n in table:
        v = vals[name].reshape(-1).astype(F32)
        pad = (-n) % 128
        parts.append(jnp.pad(v, (0, pad)) if pad else v)
    flat = jnp.concatenate(parts)
    return jnp.pad(flat, (0, rows * 128 - flat.shape[0])).reshape(rows, 128)


def _unpack(arr, table):
    flat = arr.reshape(-1)
    out, off = {}, 0
    for name, n in table:
        out[name] = flat[off:off + n]
        off += n + ((-n) % 128)
    return out


def kernel(x, norm_mix, w_in, b_gate, sc_conv_w, ssm_conv_w, ssm_conv_b, dt_bias, A_log, D_skip, ssm_norm_w, w_branch_sc, w_branch_ssm, w_out, norm_mlp, w_mlp1, w_mlp2, norm_final, loss_target, m_norm_mix, m_w_in, m_b_gate, m_sc_conv_w, m_ssm_conv_w, m_ssm_conv_b, m_dt_bias, m_A_log, m_D_skip, m_ssm_norm_w, m_w_branch_sc, m_w_branch_ssm, m_w_out, m_norm_mlp, m_w_mlp1, m_w_mlp2, m_norm_final, v_norm_mix, v_w_in, v_b_gate, v_sc_conv_w, v_ssm_conv_w, v_ssm_conv_b, v_dt_bias, v_A_log, v_D_skip, v_ssm_norm_w, v_w_branch_sc, v_w_branch_ssm, v_w_out, v_norm_mlp, v_w_mlp1, v_w_mlp2, v_norm_final):
    L = x.shape[1]
    nc = L // Q
    xi, yi, ci = lax.axis_index("x"), lax.axis_index("y"), lax.axis_index("c")
    s = 2 * xi + yi
    idx = jnp.stack([s, ci]).astype(jnp.int32)
    x0 = x.reshape(L, D)
    tgt = loss_target.reshape(L, D)
    small_names = ["b_gate", "sc_conv_w", "ssm_conv_w", "ssm_conv_b", "dt_bias", "A_log", "D_skip", "ssm_norm_w",
                   "norm_mlp", "norm_final"]
    small_wmv = [dict(zip(small_names, vals)) for vals in (
        (b_gate, sc_conv_w, ssm_conv_w, ssm_conv_b, dt_bias, A_log, D_skip, ssm_norm_w, norm_mlp, norm_final),
        (m_b_gate, m_sc_conv_w, m_ssm_conv_w, m_ssm_conv_b, m_dt_bias, m_A_log, m_D_skip, m_ssm_norm_w, m_norm_mlp,
         m_norm_final),
        (v_b_gate, v_sc_conv_w, v_ssm_conv_w, v_ssm_conv_b, v_dt_bias, v_A_log, v_D_skip, v_ssm_norm_w, v_norm_mlp,
         v_norm_final))]
    small_table = [(n, int(small_wmv[0][n].size)) for n in small_names]
    small_rows = 136
    pk_w, pk_m, pk_v = [_pack(d, small_table, small_rows) for d in small_wmv]

    piece = _to_piece(w_in.T, s)
    nb = PMAIN // XTRA
    cws = jnp.zeros((8, 1280), F32)
    cws = cws.at[0:3, 0:256].set(sc_conv_w).at[0:4, 256:1280].set(ssm_conv_w)
    cw0 = lax.dynamic_update_slice(jnp.zeros((4, 8, 1280), F32), cws[None], (s, 0, 0))
    win_keys, win2_keys, mid_keys, end_keys = ["xt", "cw", "wq0"], ["wq1"], ["wa", "wb", "wo", "w1"], ["w2"]
    gw, sems_w, tok = _split_call(
        "ag_win_start", {"wct": lax.empty((NCW, D), BF16), "xt": lax.empty((4, XTRA, D), BF16), "cw": cw0, "piece": piece},
        start=_ag_chips_plan(win_keys))
    g2, sems_w2, tok = _split_call("ag_win2_start", {"wct": gw["wct"], "piece": gw["piece"]},
                                   start=_ag_chips_plan(win2_keys), after=tok)
    piece = g2["piece"]
    gw["wct"] = _place(piece, (NCW, D), (XTRA, D), lambda i, r: (nb * r[0] + i, 0), idx, "place_wct", nblk=nb,
                       dep=tok, into=g2["wct"])
    gw["xt"] = _place(piece, (4, XTRA, D), (1, XTRA, D), lambda i, r: (r[0], 0, 0), idx, "place_xt", blk0=nb, nblk=1,
                      dep=tok, into=gw["xt"])
    gw["piece"] = piece
    wa0 = _place(w_branch_sc, (D, D), (256, 1024), lambda i, r: (r[0], 0), idx, "place_wa", dep=tok)
    wb0 = _place(w_branch_ssm, (INNER, D), (512, 1024), lambda i, r: (r[0], 0), idx, "place_wb", dep=tok)
    wo0 = _place(w_out, (D, D), (256, 1024), lambda i, r: (r[0], 0), idx, "place_wo", dep=tok)
    w10 = _place(w_mlp1, (D, DFF), (256, 1024), lambda i, r: (i, r[0]), idx, "place_w1", dep=tok)
    gm, sems_m, tok = _split_call("ag_mid_start", {"wa": wa0, "wb": wb0, "wo": wo0, "w1": w10},
                                  start=_ag_chips_plan(mid_keys))
    w20 = _place(w_mlp2, (DFF, D), (256, 1024), lambda i, r: (4 * r[0] + i, 0), idx, "place_w2", dep=tok)
    ge, sems_e, tok = _split_call("ag_end_start", {"w2": w20}, start=_ag_chips_plan(end_keys))
    h = _rms_fwd(x0, norm_mix, "rms_mix", dep=tok)
    gw, sems_w, tok = _split_call("ag_win_pass", gw, wait=_ag_chips_plan(win_keys), wait_sems=sems_w,
                                  start=_ag_sibling_plan(win_keys), after=[h, pk_w, pk_m, pk_v])
    gw, _, _ = _split_call("ag_win_done", gw, wait=_ag_sibling_plan(win_keys), wait_sems=sems_w, after=tok)
    wc, cw_all = _fix_wct(gw["wct"], gw["xt"]), gw["cw"]
    sc_w_full = jnp.concatenate([cw_all[k, :, 0:256] for k in range(4)], axis=1)
    ssm_w_full = jnp.concatenate([cw_all[k, :, 256:1280] for k in range(4)], axis=1)
    cw4 = ssm_w_full.at[4].set(ssm_conv_b)
    vec = jnp.zeros((8, 128), F32).at[0, :NH].set(dt_bias).at[1, :NH].set(A_log)
    vecg = jnp.zeros((NG, 8, 128), F32).at[:, 0, :4].set(A_log.reshape(NG, 4)).at[:, 1, :4].set(D_skip.reshape(NG, 4))

    dtraw = _matmul(h, wc[C_DT:], "nt", F32, 512, 256, 1024, "in_proj_dt")
    proj = _in_proj_wave(h, wc, 0)
    g2, sems_w2, tok = _split_call("ag_win2_pass", {"wct": wc, "piece": gw["piece"]},
                                   wait=_ag_chips_plan(win2_keys), wait_sems=sems_w2,
                                   start=_ag_sibling_plan(win2_keys), after=[proj, dtraw])
    g2, _, _ = _split_call("ag_win2_done", g2, wait=_ag_sibling_plan(win2_keys), wait_sems=sems_w2, after=tok)
    wc = g2["wct"]
    proj = _in_proj_wave(h, wc, 1, proj=proj)
    ya = _sc_fwd(proj, sc_w_full)
    xbc = _ssm_conv_fwd(proj, cw4)
    dt4, cs4, sg4 = _dt_prep(dtraw, vec)
    y, s_all = _ssd_fwd(xbc, dt4, cs4, vecg)
    gm, sems_m, tok = _split_call("ag_mid_pass", gm, wait=_ag_chips_plan(mid_keys), wait_sems=sems_m,
                                  start=_ag_sibling_plan(mid_keys), after=[y, ya])
    y = _tie(y, tok, "tie_y")
    yb = _gnorm_fwd(y, proj, ssm_norm_w)
    gm, _, _ = _split_call("ag_mid_done", gm, wait=_ag_sibling_plan(mid_keys), wait_sems=sems_m, after=yb)
    wa, wb, wo, w1 = gm["wa"], gm["wb"], gm["wo"], gm["w1"]
    ge, sems_e, tok = _split_call("ag_end_pass", ge, wait=_ag_chips_plan(end_keys), wait_sems=sems_e,
                                  start=_ag_sibling_plan(end_keys), after=yb)
    br_a = _matmul(ya, wa, "nn", F32, 1024, 1024, 1024, "branch_sc", dep=tok)
    br_b = _matmul(yb, wb, "nn", F32, 1024, 1024, 2048, "branch_ssm")
    merged = _merge_fwd(proj, b_gate, br_a, br_b)
    x1 = _matmul(merged, wo, "nn", F32, 1024, 1024, 1024, "out_proj", epi="res", extra=x0)
    h2 = _rms_fwd(x1, norm_mlp, "rms_mlp")
    a1, rl = _matmul(h2, w1, "nn", BF16, 1024, 1024, 1024, "mlp1", epi="relu2", n_outer=True)
    ge, _, _ = _split_call("ag_end_done", ge, wait=_ag_sibling_plan(end_keys), wait_sems=sems_e, after=a1)
    w2 = ge["w2"]
    x2 = _matmul(rl, w2, "nn", F32, 512, 1024, 4096, "mlp2", epi="res", extra=x1)
    dx2, g_nf, loss8 = _final(x2, norm_final, tgt)

    da = _matmul(dx2, w2, "nt", BF16, 1024, 1024, 1024, "mlp2_dx", epi="drelu", extra=a1, n_outer=True)
    g_w2 = _matmul(rl, dx2, "tn", BF16, 1024, 1024, 2048, "mlp2_dw")
    g_w1 = _matmul(h2, da, "tn", BF16, 1024, 1024, 2048, "mlp1_dw")
    dh2 = _matmul(da, w1, "nt", F32, 512, 1024, 4096, "mlp1_dx")
    dx1, g_nmlp = _rms_bwd(dh2, x1, norm_mlp, dx2, "rms_mlp_bwd")
    dmerged = _matmul(dx1, wo, "nt", F32, 1024, 1024, 1024, "out_proj_dx")
    g_wo = _matmul(merged, dx1, "tn", BF16, 1024, 1024, 2048, "out_proj_dw")
    dproj = lax.empty((L, NCW), BF16)
    dbr, dproj, g_bg = _merge_bwd(dmerged, proj, b_gate, br_a, br_b, dproj)
    dya = _matmul(dbr[0], wa, "nt", F32, 1024, 1024, 1024, "branch_sc_dx")
    g_wa = _matmul(ya, dbr[0], "tn", BF16, 1024, 1024, 2048, "branch_sc_dw")
    dproj, g_scw = _sc_bwd(dya, proj, sc_w_full, dproj)
    dyb = _matmul(dbr[1], wb, "nt", F32, 1024, 1024, 1024, "branch_ssm_dx", n_outer=True)
    g_wb = _matmul(yb, dbr[1], "tn", BF16, 1024, 1024, 2048, "branch_ssm_dw")
    rs_a = _ReduceScatter([1, 2, 3, 4, 5], {1: g_w1, 2: g_w2, 3: g_wa, 4: g_wb, 5: g_wo}, idx, "a")
    dy, dproj, g_snw = _gnorm_bwd(_tie(dyb, rs_a.token, "tie_dyb"), y, proj, ssm_norm_w, dproj)
    tok = rs_a.chips(after=dy)
    dxs, dbm, dcm, ddt_g, st = _ssd_bwd(xbc, dt4, cs4, sg4, vecg, s_all, _tie(dy, tok, "tie_dy"))
    dproj, gx1 = _ssm_conv_bwd(dxs, proj, cw4, dproj, 0, "ssm_conv_bwd_x")
    dproj, gx2 = _ssm_conv_bwd(dbm, proj, cw4, dproj, INNER, "ssm_conv_bwd_b")
    dproj, gx3 = _ssm_conv_bwd(dcm, proj, cw4, dproj, INNER + NG * NS, "ssm_conv_bwd_c")
    g_cw4 = jnp.concatenate([gx1, gx2, gx3], axis=1)
    dproj, g_dtb = _dt_bwd(ddt_g, dproj)
    small = {"b_gate": g_bg[0], "ssm_conv_b": g_cw4[4], "dt_bias": g_dtb[0, :NH],
             "A_log": st[:, 0, :4], "D_skip": st[:, 1, :4], "ssm_norm_w": g_snw[0], "norm_mlp": g_nmlp[0],
             "norm_final": g_nf[0], "sc_conv_w": g_scw[0:3], "ssm_conv_w": g_cw4[0:4], "loss": loss8[0, 0:1]}
    me = 4 * xi + 2 * yi + ci
    sm8 = lax.dynamic_update_slice(jnp.zeros((8, SMALL_ROWS, 128), F32), _pack(small, _SMALL, SMALL_ROWS)[None], (me, 0, 0))
    sm_arr, sm_sems, tok = _split_call("small_start", {"sm": sm8}, start=_all8_plan("sm"))
    g_wc = _matmul(dproj, h, "tn", BF16, 1280, 1024, 2048, "in_proj_dw", dep=tok)
    rs_b = _ReduceScatter([0], {0: g_wc}, idx, "b")
    tok = rs_a.share(after=rs_b.token)
    tok = rs_b.chips(after=tok)
    dh = _matmul(dproj, wc, "nn", F32, 512, 1024, 5760, "in_proj_dx", dep=tok)
    grad_x, g_nm = _rms_bwd(dh, x0, norm_mix, dx1, "rms_mix_bwd")
    nm8 = lax.dynamic_update_slice(jnp.zeros((8, 8, 128), F32), g_nm[0].reshape(1, 8, 128), (me, 0, 0))
    nm_arr, nm_sems, tok = _split_call("norm_mix_start", {"nm": nm8}, start=_all8_plan("nm"))
    sm_arr, _, _ = _split_call("small_wait", sm_arr, wait=_all8_plan("sm"), wait_sems=sm_sems, after=tok)
    small_sum = _sum8(sm_arr["sm"])
    gs = _unpack(small_sum, _SMALL)
    red = rs_a.result(after=tok)
    big = {"w_mlp1": red[1], "w_mlp2": red[2], "w_branch_sc": red[3], "w_branch_ssm": red[4], "w_out": red[5]}

    given = dict(norm_mix=norm_mix, w_in=w_in, b_gate=b_gate, sc_conv_w=sc_conv_w, ssm_conv_w=ssm_conv_w, ssm_conv_b=ssm_conv_b, dt_bias=dt_bias, A_log=A_log, D_skip=D_skip, ssm_norm_w=ssm_norm_w, w_branch_sc=w_branch_sc, w_branch_ssm=w_branch_ssm, w_out=w_out, norm_mlp=norm_mlp, w_mlp1=w_mlp1, w_mlp2=w_mlp2, norm_final=norm_final,
                 m_norm_mix=m_norm_mix, m_w_in=m_w_in, m_b_gate=m_b_gate, m_sc_conv_w=m_sc_conv_w, m_ssm_conv_w=m_ssm_conv_w, m_ssm_conv_b=m_ssm_conv_b, m_dt_bias=m_dt_bias, m_A_log=m_A_log, m_D_skip=m_D_skip, m_ssm_norm_w=m_ssm_norm_w, m_w_branch_sc=m_w_branch_sc, m_w_branch_ssm=m_w_branch_ssm, m_w_out=m_w_out, m_norm_mlp=m_norm_mlp, m_w_mlp1=m_w_mlp1, m_w_mlp2=m_w_mlp2, m_norm_final=m_norm_final,
                 v_norm_mix=v_norm_mix, v_w_in=v_w_in, v_b_gate=v_b_gate, v_sc_conv_w=v_sc_conv_w, v_ssm_conv_w=v_ssm_conv_w, v_ssm_conv_b=v_ssm_conv_b, v_dt_bias=v_dt_bias, v_A_log=v_A_log, v_D_skip=v_D_skip, v_ssm_norm_w=v_ssm_norm_w, v_w_branch_sc=v_w_branch_sc, v_w_branch_ssm=v_w_branch_ssm, v_w_out=v_w_out, v_norm_mlp=v_norm_mlp, v_w_mlp1=v_w_mlp1, v_w_mlp2=v_w_mlp2, v_norm_final=v_norm_final)
    order = ["norm_mix", "w_in", "b_gate", "sc_conv_w", "ssm_conv_w", "ssm_conv_b", "dt_bias", "A_log", "D_skip",
             "ssm_norm_w", "w_branch_sc", "w_branch_ssm", "w_out", "norm_mlp", "w_mlp1", "w_mlp2", "norm_final"]
    grad, delta, new_m, new_v = {}, {}, {}, {}
    for n in big:
        delta[n], new_m[n], new_v[n], grad[n] = _adamw(given[n], big[n], given["m_" + n], given["v_" + n],
                                                       "adamw_" + n, copy_g=True)
    big["w_in"] = None
    grad_small = {n: gs[n].reshape(given[n].shape) for n in small_names if n not in ("sc_conv_w", "ssm_conv_w")}
    grad_small["sc_conv_w"] = lax.dynamic_slice(gs["sc_conv_w"].reshape(3, D), (0, 256 * s), (3, 256))
    grad_small["ssm_conv_w"] = lax.dynamic_slice(gs["ssm_conv_w"].reshape(4, XBC), (0, 1024 * s), (4, 1024))
    table = small_table
    ds_, ms_, vs_ = _adamw(pk_w, _pack(grad_small, table, small_rows), pk_m, pk_v, "adamw_small", tr=small_rows)
    ds_, ms_, vs_ = _unpack(ds_, table), _unpack(ms_, table), _unpack(vs_, table)
    for n in grad_small:
        shp = given[n].shape
        grad[n] = grad_small[n]
        delta[n], new_m[n], new_v[n] = ds_[n].reshape(shp), ms_[n].reshape(shp), vs_[n].reshape(shp)

    done = [new_v[n] for n in ("w_mlp1", "w_mlp2", "w_branch_sc", "w_branch_ssm", "w_out")] + [vs_["b_gate"]]
    tok = rs_b.share(after=done)
    offs = jnp.where(s == 3, jnp.array([24, -8, 744, 2072, -8], jnp.int32),
                     jnp.stack([8 * s, 8 * s, 0 * s, 8 * s, 8 * s]).astype(jnp.int32))
    offs = jnp.concatenate([offs, jnp.stack([7 * ci, 4 - 4 * ci]).astype(jnp.int32)])
    nmain = W_SHARD // 256
    wt_own = (w_in.T, rs_b.arr["f0"], m_w_in.T, v_w_in.T, offs)
    res = _adamw_w_in(*wt_own, "adamw_w_in_own", 0, 256, 4, (0, 1), blk_key=5)
    gp = rs_b.result(after=[tok, res[0]])[0]
    wt_args = (w_in.T, gp, m_w_in.T, v_w_in.T, offs)
    res = _adamw_w_in(*wt_args, "adamw_w_in", 0, 256, nmain - 4, (0, 1), into=res, blk_key=6)
    res = _adamw_w_in(*wt_args, "adamw_w_in_dt", 744, 32, 1, (3,), into=res)
    dt_, mt_, vt_, gwt = _adamw_w_in(*wt_args, "adamw_w_in_tail", 256 * nmain, 8, 1, (4,), into=res)
    grad["w_in"], delta["w_in"], new_m["w_in"], new_v["w_in"] = gwt.T, dt_.T, mt_.T, vt_.T
    nm_arr, _, _ = _split_call("norm_mix_wait", nm_arr, wait=_all8_plan("nm"), wait_sems=nm_sems, after=tok)
    g8 = _sum8(nm_arr["nm"], "norm_mix_sum")
    r8 = lambda a: a.reshape(8, 128)
    d8, m8, v8 = _adamw(r8(norm_mix), g8, r8(m_norm_mix), r8(v_norm_mix), "adamw_norm_mix", tr=8)
    grad["norm_mix"], delta["norm_mix"] = g8.reshape(D), d8.reshape(D)
    new_m["norm_mix"], new_v["norm_mix"] = m8.reshape(D), v8.reshape(D)

    loss = gs["loss"].reshape(())
    return (loss, grad_x.reshape(1, L, D), *[grad[n] for n in order], *[delta[n] for n in order],
            *[new_m[n] for n in order], *[new_v[n] for n in order])
```

```python
import functools

import jax
import jax.numpy as jnp
from jax import lax
from jax.experimental import pallas as pl
from jax.experimental.pallas import tpu as pltpu

F32 = jnp.float32
BF16 = jnp.bfloat16
MESH = pl.DeviceIdType.MESH
HBM = pltpu.HBM

D = 1024
INNER = 2048
HD = 64
NH = 32
NG = 8
NS = 128
Q = 128
GPS = 8
XBC = 4096
DFF = 4096
EPS = 1e-6
W_SHARD = 2824
NCW = 11520
PIECE = 3072
PMAIN = 2816
C_Z, C_XBC, C_GATE, C_DT = 3072, 5120, 9216, 11264
SMALL_ROWS = 256
VMEM_LIMIT = 56 * 1024 * 1024

ADAM_LR, ADAM_B1, ADAM_B2, ADAM_EPS, ADAM_WD, ADAM_STEP = 0.001, 0.9, 0.999, 1e-08, 0.01, 10


def _cp(sem=None, vmem=VMEM_LIMIT):
    return pltpu.CompilerParams(dimension_semantics=sem, vmem_limit_bytes=vmem)


def _sigmoid(v):
    return 1.0 / (1.0 + jnp.exp(-v))


_DIMS = {"nn": (((1,), (0,)), ((), ())), "nt": (((1,), (1,)), ((), ())), "tn": (((0,), (0,)), ((), ()))}


def _matmul(a, b, mode, out_dtype, tm, tn, tk, name, epi=None, extra=None, n_outer=False, dep=None):
    if mode == "tn":
        K, M = a.shape
    else:
        M, K = a.shape
    N = b.shape[0] if mode == "nt" else b.shape[1]
    tm, tn, tk = min(tm, M), min(tn, N), min(tk, K)
    assert M % tm == 0 and N % tn == 0 and K % tk == 0, (name, M, N, K, tm, tn, tk)
    nm, nn, nk = M // tm, N // tn, K // tk
    dims = _DIMS[mode]

    def ij(p0, p1):
        return (p1, p0) if n_outer else (p0, p1)

    if mode == "tn":
        a_spec = pl.BlockSpec((tk, tm), lambda p0, p1, k: (k, ij(p0, p1)[0]))
    else:
        a_spec = pl.BlockSpec((tm, tk), lambda p0, p1, k: (ij(p0, p1)[0], k))
    if mode == "nt":
        b_spec = pl.BlockSpec((tn, tk), lambda p0, p1, k: (ij(p0, p1)[1], k))
    else:
        b_spec = pl.BlockSpec((tk, tn), lambda p0, p1, k: (k, ij(p0, p1)[1]))
    o_spec = pl.BlockSpec((tm, tn), lambda p0, p1, k: ij(p0, p1))
    in_specs = [a_spec, b_spec]
    args = [a, b]
    if epi in ("res", "drelu"):
        in_specs.append(o_spec)
        args.append(extra)
    if dep is not None:
        in_specs.append(pl.BlockSpec(memory_space=pl.ANY))
        args.append(dep)
    n_in = len(args)
    if epi == "relu2":
        out_shape = (jax.ShapeDtypeStruct((M, N), out_dtype), jax.ShapeDtypeStruct((M, N), BF16))
        out_specs = (o_spec, o_spec)
    else:
        out_shape = jax.ShapeDtypeStruct((M, N), out_dtype)
        out_specs = o_spec

    def kern(*refs):
        a_ref, b_ref = refs[0], refs[1]
        e_ref = refs[2] if epi in ("res", "drelu") else None
        acc = refs[-1]
        outs = refs[n_in:-1] if nk > 1 else refs[n_in:]
        k = pl.program_id(2)

        def product():
            return lax.dot_general(a_ref[...].astype(BF16), b_ref[...].astype(BF16), dims, preferred_element_type=F32)

        def finish(r):
            if epi is None:
                outs[0][...] = r.astype(out_dtype)
            elif epi == "res":
                outs[0][...] = (r + e_ref[...]).astype(out_dtype)
            elif epi == "relu2":
                outs[0][...] = r.astype(out_dtype)
                t = jnp.maximum(r, 0.0)
                outs[1][...] = (t * t).astype(BF16)
            else:
                outs[0][...] = (r * (2.0 * jnp.maximum(e_ref[...].astype(F32), 0.0))).astype(out_dtype)

        if nk == 1:
            finish(product())
        else:
            @pl.when(k == 0)
            def _():
                acc[...] = jnp.zeros_like(acc)

            acc[...] += product()

            @pl.when(k == nk - 1)
            def _():
                finish(acc[...])

    grid = (nn, nm, nk) if n_outer else (nm, nn, nk)
    return pl.pallas_call(
        kern, grid=grid, in_specs=in_specs, out_specs=out_specs, out_shape=out_shape,
        scratch_shapes=[pltpu.VMEM((tm, tn), F32)] if nk > 1 else [], name=name,
        compiler_params=_cp(("parallel", "parallel", "arbitrary")),
    )(*args)


def _rms_fwd(x, w, name, tl=256, dep=None):
    L = x.shape[0]

    def kern(x_ref, w_ref, *rest):
        o_ref = rest[-1]
        xv = x_ref[...]
        r = lax.rsqrt(jnp.mean(xv * xv, axis=-1, keepdims=True) + EPS)
        o_ref[...] = ((xv * r) * w_ref[...]).astype(BF16)

    row = pl.BlockSpec((tl, D), lambda i: (i, 0))
    deps = [] if dep is None else [dep]
    return pl.pallas_call(
        kern, grid=(L // tl,),
        in_specs=[row, pl.BlockSpec((1, D), lambda i: (0, 0))] + [pl.BlockSpec(memory_space=pl.ANY)] * len(deps),
        out_specs=row, out_shape=jax.ShapeDtypeStruct((L, D), BF16), name=name, compiler_params=_cp(("parallel",)),
    )(x, w.reshape(1, D), *deps)


def _rms_bwd(dy, x, w, res, name, tl=256, dep=None):
    L = x.shape[0]
    deps = [] if dep is None else [dep]

    def kern(dy_ref, x_ref, w_ref, res_ref, *rest):
        dx_ref, gw_ref = rest[-2:]
        @pl.when(pl.program_id(0) == 0)
        def _():
            gw_ref[...] = jnp.zeros_like(gw_ref)

        xv = x_ref[...]
        dyv = dy_ref[...]
        r = lax.rsqrt(jnp.mean(xv * xv, axis=-1, keepdims=True) + EPS)
        xn = xv * r
        gw_ref[...] += jnp.broadcast_to(jnp.sum(dyv * xn, axis=0, keepdims=True), (8, D))
        dxn = dyv * w_ref[...]
        dx_ref[...] = res_ref[...] + r * (dxn - xn * jnp.mean(dxn * xn, axis=-1, keepdims=True))

    row = pl.BlockSpec((tl, D), lambda i: (i, 0))
    return pl.pallas_call(
        kern, grid=(L // tl,),
        in_specs=[row, row, pl.BlockSpec((1, D), lambda i: (0, 0)), row] + [pl.BlockSpec(memory_space=pl.ANY)] * len(deps),
        out_specs=(row, pl.BlockSpec((8, D), lambda i: (0, 0))),
        out_shape=(jax.ShapeDtypeStruct((L, D), F32), jax.ShapeDtypeStruct((8, D), F32)),
        name=name, compiler_params=_cp(("arbitrary",)),
    )(dy, x, w.reshape(1, D), res, *deps)


def _final(x2, w, tgt, tl=256):
    L = x2.shape[0]

    def kern(x_ref, w_ref, t_ref, dx_ref, gw_ref, loss_ref):
        @pl.when(pl.program_id(0) == 0)
        def _():
            gw_ref[...] = jnp.zeros_like(gw_ref)
            loss_ref[...] = jnp.zeros_like(loss_ref)

        xv = x_ref[...]
        r = lax.rsqrt(jnp.mean(xv * xv, axis=-1, keepdims=True) + EPS)
        xn = xv * r
        e = xn * w_ref[...] - t_ref[...]
        per_tok = jnp.mean(e * e, axis=-1, keepdims=True)
        loss_ref[...] += 0.5 * jnp.sum(per_tok)
        dyv = e * (1.0 / D)
        gw_ref[...] += jnp.broadcast_to(jnp.sum(dyv * xn, axis=0, keepdims=True), (8, D))
        dxn = dyv * w_ref[...]
        dx_ref[...] = r * (dxn - xn * jnp.mean(dxn * xn, axis=-1, keepdims=True))

    row = pl.BlockSpec((tl, D), lambda i: (i, 0))
    return pl.pallas_call(
        kern, grid=(L // tl,), in_specs=[row, pl.BlockSpec((1, D), lambda i: (0, 0)), row],
        out_specs=(row, pl.BlockSpec((8, D), lambda i: (0, 0)), pl.BlockSpec((8, 128), lambda i: (0, 0))),
        out_shape=(jax.ShapeDtypeStruct((L, D), F32), jax.ShapeDtypeStruct((8, D), F32),
                   jax.ShapeDtypeStruct((8, 128), F32)),
        name="final_norm_loss", compiler_params=_cp(("arbitrary",)),
    )(x2, w.reshape(1, D), tgt)


def _down(v, k):
    if k == 0:
        return v
    t = lax.broadcasted_iota(jnp.int32, v.shape, 0)
    return jnp.where(t >= k, pltpu.roll(v, k, axis=0), 0.0)


def _up(v, k):
    if k == 0:
        return v
    n = v.shape[0]
    t = lax.broadcasted_iota(jnp.int32, v.shape, 0)
    return jnp.where(t < n - k, pltpu.roll(v, n - k, axis=0), 0.0)


TW = 256


def _sc_fwd(proj, cw):
    L = proj.shape[0]
    nb = D // TW

    def kern(b_ref, c_ref, x_ref, w_ref, o_ref):
        u = c_ref[...].astype(F32) * x_ref[...].astype(F32)
        w = w_ref[...]
        cv = w[0:1] * _down(u, 2) + w[1:2] * _down(u, 1) + w[2:3] * u
        o_ref[...] = (b_ref[...].astype(F32) * cv).astype(BF16)

    col = lambda off: pl.BlockSpec((L, TW), lambda j: (0, off + j))
    return pl.pallas_call(
        kern, grid=(nb,), in_specs=[col(0), col(nb), col(2 * nb), pl.BlockSpec((8, TW), lambda j: (0, j))],
        out_specs=pl.BlockSpec((L, TW), lambda j: (0, j)), out_shape=jax.ShapeDtypeStruct((L, D), BF16),
        name="sc_fwd", compiler_params=_cp(("parallel",)),
    )(proj, proj, proj, cw)


def _sc_bwd(dya, proj, cw, dproj):
    L = proj.shape[0]
    nb = D // TW

    def kern(d_ref, b_ref, c_ref, x_ref, w_ref, _, dp_ref, gw_ref, keep):
        sec = pl.program_id(1)

        @pl.when(sec == 0)
        def _():
            cs, xs, dyv = c_ref[...].astype(F32), x_ref[...].astype(F32), d_ref[...]
            w = w_ref[...]
            u = cs * xs
            u1, u2 = _down(u, 1), _down(u, 2)
            cv = w[0:1] * u2 + w[1:2] * u1 + w[2:3] * u
            dcv = dyv * b_ref[...].astype(F32)
            du = w[2:3] * dcv + w[1:2] * _up(dcv, 1) + w[0:1] * _up(dcv, 2)
            g0 = jnp.sum(dcv * u2, axis=0, keepdims=True)
            g1 = jnp.sum(dcv * u1, axis=0, keepdims=True)
            g2 = jnp.sum(dcv * u, axis=0, keepdims=True)
            row = lax.broadcasted_iota(jnp.int32, (8, TW), 0)
            gw_ref[...] = jnp.where(row == 0, g0, jnp.where(row == 1, g1, jnp.where(row == 2, g2, 0.0)))
            dp_ref[...] = (dyv * cv).astype(BF16)
            keep[0] = (du * xs).astype(BF16)
            keep[1] = (du * cs).astype(BF16)

        @pl.when(sec > 0)
        def _():
            dp_ref[...] = keep[sec - 1]

    col = lambda off: pl.BlockSpec((L, TW), lambda j, s: (0, off + j))
    return pl.pallas_call(
        kern, grid=(nb, 3),
        in_specs=[col(0), col(0), col(nb), col(2 * nb), pl.BlockSpec((8, TW), lambda j, s: (0, j)),
                  pl.BlockSpec(memory_space=pl.ANY)],
        out_specs=(pl.BlockSpec((L, TW), lambda j, s: (0, s * nb + j)), pl.BlockSpec((8, TW), lambda j, s: (0, j))),
        out_shape=(jax.ShapeDtypeStruct(dproj.shape, BF16), jax.ShapeDtypeStruct((8, D), F32)),
        scratch_shapes=[pltpu.VMEM((2, L, TW), BF16)],
        input_output_aliases={5: 0}, name="sc_bwd", compiler_params=_cp(("parallel", "arbitrary")),
    )(dya, proj, proj, proj, cw, dproj)


def _ssm_conv_fwd(proj, cw4):
    L = proj.shape[0]
    off = C_XBC // TW

    def kern(r_ref, w_ref, o_ref):
        raw = r_ref[...].astype(F32)
        w = w_ref[...]
        c4 = w[0:1] * _down(raw, 3) + w[1:2] * _down(raw, 2) + w[2:3] * _down(raw, 1) + w[3:4] * raw + w[4:5]
        o_ref[...] = c4 * _sigmoid(c4)

    return pl.pallas_call(
        kern, grid=(XBC // TW,),
        in_specs=[pl.BlockSpec((L, TW), lambda j: (0, off + j)), pl.BlockSpec((8, TW), lambda j: (0, j))],
        out_specs=pl.BlockSpec((L, TW), lambda j: (0, j)), out_shape=jax.ShapeDtypeStruct((L, XBC), F32),
        name="ssm_conv_fwd", compiler_params=_cp(("parallel",)),
    )(proj, cw4)


def _ssm_conv_bwd(dx, proj, cw4, dproj, col0, name):
    L, width = dx.shape
    off_p = (C_XBC + col0) // TW
    off_w = col0 // TW

    def kern(d_ref, r_ref, w_ref, _, dp_ref, gw_ref):
        raw = r_ref[...].astype(F32)
        w = w_ref[...]
        r1, r2, r3 = _down(raw, 1), _down(raw, 2), _down(raw, 3)
        c4 = w[0:1] * r3 + w[1:2] * r2 + w[2:3] * r1 + w[3:4] * raw + w[4:5]
        sg = _sigmoid(c4)
        dc4 = d_ref[...] * (sg * (1.0 + c4 * (1.0 - sg)))
        draw = w[3:4] * dc4 + w[2:3] * _up(dc4, 1) + w[1:2] * _up(dc4, 2) + w[0:1] * _up(dc4, 3)
        dp_ref[...] = draw.astype(BF16)
        gs = [jnp.sum(dc4 * r3, axis=0, keepdims=True), jnp.sum(dc4 * r2, axis=0, keepdims=True),
              jnp.sum(dc4 * r1, axis=0, keepdims=True), jnp.sum(dc4 * raw, axis=0, keepdims=True),
              jnp.sum(dc4, axis=0, keepdims=True)]
        row = lax.broadcasted_iota(jnp.int32, (8, TW), 0)
        acc = jnp.zeros((8, TW), F32)
        for k, gk in enumerate(gs):
            acc = jnp.where(row == k, gk, acc)
        gw_ref[...] = acc

    return pl.pallas_call(
        kern, grid=(width // TW,),
        in_specs=[pl.BlockSpec((L, TW), lambda j: (0, j)), pl.BlockSpec((L, TW), lambda j: (0, off_p + j)),
                  pl.BlockSpec((8, TW), lambda j: (0, off_w + j)), pl.BlockSpec(memory_space=pl.ANY)],
        out_specs=(pl.BlockSpec((L, TW), lambda j: (0, off_p + j)), pl.BlockSpec((8, TW), lambda j: (0, j))),
        out_shape=(jax.ShapeDtypeStruct(dproj.shape, BF16), jax.ShapeDtypeStruct((8, width), F32)),
        input_output_aliases={3: 0}, name=name, compiler_params=_cp(("arbitrary",)),
    )(dx, proj, cw4, dproj)


def _split3(v):
    h1 = v.astype(BF16)
    r1 = v - h1.astype(F32)
    h2 = r1.astype(BF16)
    h3 = (r1 - h2.astype(F32)).astype(BF16)
    return h1, h2, h3


def _dot01(m01, v, dims=_DIMS["nn"], m_left=True, terms=3):
    out = None
    for part in _split3(v)[:terms]:
        ops = (m01, part) if m_left else (part, m01)
        t = lax.dot_general(ops[0], ops[1], dims, preferred_element_type=F32)
        out = t if out is None else out + t
    return out


def _bdot(a, b, mode="nn"):
    return lax.dot_general(a.astype(BF16), b.astype(BF16), _DIMS[mode], preferred_element_type=F32)


def _softplus(v):
    return jnp.maximum(v, 0.0) + jnp.log1p(jnp.exp(-jnp.abs(v)))


def _dt_prep(proj, vec):
    L = proj.shape[0]

    def kern(p_ref, v_ref, dt_ref, cs_ref, sg_ref):
        v = v_ref[...]
        pre = p_ref[:, 0:128] + v[0:1]
        dt = _softplus(pre)
        da = dt * (-jnp.exp(v[1:2]))
        ii = lax.broadcasted_iota(jnp.int32, (Q, Q), 0)
        jj = lax.broadcasted_iota(jnp.int32, (Q, Q), 1)
        ltri = (jj <= ii).astype(BF16)
        lane = lax.broadcasted_iota(jnp.int32, (Q, 128), 1)
        for val, ref in ((dt, dt_ref), (_dot01(ltri, da), cs_ref), (_sigmoid(pre), sg_ref)):
            for g in range(NG):
                moved = val if g == 0 else pltpu.roll(val, 128 - 4 * g, axis=1)
                ref[g] = jnp.where(lane < 4, moved, 0.0)

    blk = pl.BlockSpec((NG, Q, 128), lambda c: (0, c, 0))
    return pl.pallas_call(
        kern, grid=(L // Q,),
        in_specs=[pl.BlockSpec((Q, 256), lambda c: (c, 0)), pl.BlockSpec((8, 128), lambda c: (0, 0))],
        out_specs=(blk, blk, blk),
        out_shape=(jax.ShapeDtypeStruct((NG, L, 128), F32),) * 3,
        name="dt_prep", compiler_params=_cp(("parallel",)),
    )(proj, vec)


def _head_masks():
    lane = lax.broadcasted_iota(jnp.int32, (1, 4 * HD), 1)
    return [((lane >= HD * j) & (lane < HD * (j + 1))) for j in range(4)]


def _expand4(v4, masks):
    R = v4.shape[0]
    out = jnp.zeros((R, 4 * HD), F32)
    for j in range(4):
        out = jnp.where(masks[j], jnp.broadcast_to(v4[:, j:j + 1], (R, 4 * HD)), out)
    return out


def _decay_matrix(cs_col, tri):
    colb = jnp.broadcast_to(cs_col, (Q, Q))
    return jnp.exp(jnp.where(tri, colb - colb.T, -jnp.inf))


def _ssd_fwd(xbc, dt4, cs4, vecg):
    L = xbc.shape[0]
    nc = L // Q

    def kern(x_ref, b_ref, c_ref, dt_ref, cs_ref, v_ref, y_ref, s_ref, S):
        c = pl.program_id(1)

        @pl.when(c == 0)
        def _():
            S[...] = jnp.zeros_like(S)

        masks = _head_masks()
        ii = lax.broadcasted_iota(jnp.int32, (Q, Q), 0)
        jj = lax.broadcasted_iota(jnp.int32, (Q, Q), 1)
        tri = jj <= ii
        for gi in range(GPS):
            xs, ns = slice(256 * gi, 256 * (gi + 1)), slice(NS * gi, NS * (gi + 1))
            dt4v, cs4v = dt_ref[gi], cs_ref[gi]
            dt_b, cs_b = _expand4(dt4v, masks), _expand4(cs4v, masks)
            d_b = _expand4(v_ref[gi], masks)[1:2]
            cs_last = cs_b[Q - 1:Q, :]
            x4, bm, cm = x_ref[:, xs], b_ref[:, ns], c_ref[:, ns]
            xdt = x4 * dt_b
            gm = _bdot(cm, bm, "nt")
            s4 = S[gi]
            s_ref[gi, 0] = s4
            y = _bdot(cm, s4) * jnp.exp(cs_b) + d_b * x4
            m_all = jnp.concatenate([(gm * _decay_matrix(cs4v[:, j:j + 1], tri)).astype(BF16) for j in range(4)], axis=0)
            yd = _bdot(m_all, xdt)
            for j in range(4):
                y = y + jnp.where(masks[j], yd[Q * j:Q * (j + 1)], 0.0)
            y_ref[:, xs] = y
            S[gi] = jnp.exp(cs_last) * s4 + _bdot(bm, xdt * jnp.exp(cs_last - cs_b), "tn")

    sc = pl.BlockSpec((GPS, Q, 128), lambda g, c: (g, c, 0))
    bw = NS * GPS
    return pl.pallas_call(
        kern, grid=(NG // GPS, nc),
        in_specs=[pl.BlockSpec((Q, 256 * GPS), lambda g, c: (c, g)),
                  pl.BlockSpec((Q, bw), lambda g, c: (c, INNER // bw + g)),
                  pl.BlockSpec((Q, bw), lambda g, c: (c, (INNER + NG * NS) // bw + g)),
                  sc, sc, pl.BlockSpec((GPS, 8, 128), lambda g, c: (g, 0, 0))],
        out_specs=(pl.BlockSpec((Q, 256 * GPS), lambda g, c: (c, g)),
                   pl.BlockSpec((GPS, 1, NS, 256), lambda g, c: (g, c, 0, 0))),
        out_shape=(jax.ShapeDtypeStruct((L, INNER), F32), jax.ShapeDtypeStruct((NG, nc, NS, 256), F32)),
        scratch_shapes=[pltpu.VMEM((GPS, NS, 256), F32)], name="ssd_fwd",
        compiler_params=_cp(("parallel", "arbitrary")),
    )(xbc, xbc, xbc, dt4, cs4, vecg)


def _ssd_bwd(xbc, dt4, cs4, sg4, vecg, s_all, dy):
    L = xbc.shape[0]
    nc = L // Q

    def kern(x_ref, b_ref, c_ref, dt_ref, cs_ref, sg_ref, v_ref, s_ref, dy_ref,
             dx_ref, db_ref, dc_ref, ddt_ref, st_ref, dS):
        cc = pl.program_id(1)

        @pl.when(cc == 0)
        def _():
            dS[...] = jnp.zeros_like(dS)
            st_ref[...] = jnp.zeros_like(st_ref)

        masks = _head_masks()
        ii = lax.broadcasted_iota(jnp.int32, (Q, Q), 0)
        jj = lax.broadcasted_iota(jnp.int32, (Q, Q), 1)
        tri = jj <= ii
        utri = (jj >= ii).astype(BF16)
        hsel = ((lax.broadcasted_iota(jnp.int32, (4 * HD, 128), 0) // HD)
                == lax.broadcasted_iota(jnp.int32, (4 * HD, 128), 1)).astype(BF16)
        hrow = ((lax.broadcasted_iota(jnp.int32, (4 * Q, 128), 0) // Q)
                == lax.broadcasted_iota(jnp.int32, (4 * Q, 128), 1)).astype(BF16)
        ones_q = jnp.ones((Q, 128), BF16)
        lane128 = lax.broadcasted_iota(jnp.int32, (Q, 128), 1)

        for gi in range(GPS):
            xs, ns = slice(256 * gi, 256 * (gi + 1)), slice(NS * gi, NS * (gi + 1))
            dt4v, cs4v, sg4v = dt_ref[gi], cs_ref[gi], sg_ref[gi]
            dt_b, cs_b = _expand4(dt4v, masks), _expand4(cs4v, masks)
            vv = _expand4(v_ref[gi], masks)
            a_b = -jnp.exp(vv[0:1])
            d_b = vv[1:2]
            a4 = -jnp.exp(v_ref[gi][0:1, :])
            cs_last = cs_b[Q - 1:Q, :]
            ecs = jnp.exp(cs_b)
            decay = jnp.exp(cs_last - cs_b)
            elast = jnp.exp(cs_last)
            x4, bm, cm, dyv = x_ref[:, xs], b_ref[:, ns], c_ref[:, ns], dy_ref[:, xs]
            s4 = s_ref[gi, 0]
            dsn = dS[gi]
            xdt = x4 * dt_b
            gm = _bdot(cm, bm, "nt")
            dye = dyv * ecs
            yoff = ecs * _bdot(cm, s4)
            t4 = _bdot(bm, dsn) * decay
            lms, mhs = [], []
            for j in range(4):
                colb = jnp.broadcast_to(cs4v[:, j:j + 1], (Q, Q))
                lms.append(jnp.exp(jnp.where(tri, colb - colb.T, -jnp.inf)))
                mhs.append(gm * lms[j])
            m_all = jnp.concatenate([m.astype(BF16) for m in mhs], axis=0)
            dy_m = jnp.concatenate([jnp.where(masks[j], dyv, 0.0).astype(BF16) for j in range(4)], axis=0)
            dxdt = t4 + _bdot(m_all, dy_m, "tn")
            dm_all = _bdot(dy_m, xdt, "nt")
            dg = jnp.zeros((Q, Q), F32)
            for j in range(4):
                dg = dg + dm_all[Q * j:Q * (j + 1)] * lms[j]
            e_all = dm_all * jnp.concatenate(mhs, axis=0)
            rsum = _dot01(ones_q, e_all, m_left=False, terms=2)
            da4 = -_dot01(hrow, e_all, _DIMS["tn"], m_left=False, terms=2)
            for j in range(4):
                da4 = da4 + jnp.where(lane128 == j, rsum[Q * j:Q * (j + 1)], 0.0)
            xt = xdt * t4
            tail = jnp.sum(xt, axis=0, keepdims=True) + elast * jnp.sum(s4 * dsn, axis=0, keepdims=True)
            gd_raw = jnp.sum(dyv * x4, axis=0, keepdims=True)
            stacked = jnp.concatenate([dyv * yoff - xt, dxdt * x4, jnp.broadcast_to(tail, (8, 4 * HD)),
                                       jnp.broadcast_to(gd_raw, (8, 4 * HD))], axis=0)
            seg = _dot01(hsel, stacked, m_left=False, terms=2)
            dda4 = _dot01(utri, da4 + seg[0:Q], terms=2) + seg[2 * Q:2 * Q + 1]
            ddt_ref[gi] = (dda4 * a4 + seg[Q:2 * Q]) * sg4v
            ga = jnp.sum(dda4 * dt4v * a4, axis=0, keepdims=True)
            row = lax.broadcasted_iota(jnp.int32, (8, 128), 0)
            st_ref[gi] += jnp.where(row == 0, ga, jnp.where(row == 1, seg[2 * Q + 8:2 * Q + 9], 0.0))
            dx_ref[:, xs] = d_b * dyv + dxdt * dt_b
            dc_ref[:, ns] = _bdot(dg, bm) + _bdot(dye, s4, "nt")
            db_ref[:, ns] = _bdot(dg, cm, "tn") + _bdot(xdt * decay, dsn, "nt")
            dS[gi] = elast * dsn + _bdot(cm, dye, "tn")

    rv = lambda c: nc - 1 - c
    sc = pl.BlockSpec((GPS, Q, 128), lambda g, c: (g, rv(c), 0))
    bw = NS * GPS
    return pl.pallas_call(
        kern, grid=(NG // GPS, nc),
        in_specs=[pl.BlockSpec((Q, 256 * GPS), lambda g, c: (rv(c), g)),
                  pl.BlockSpec((Q, bw), lambda g, c: (rv(c), INNER // bw + g)),
                  pl.BlockSpec((Q, bw), lambda g, c: (rv(c), (INNER + NG * NS) // bw + g)),
                  sc, sc, sc, pl.BlockSpec((GPS, 8, 128), lambda g, c: (g, 0, 0)),
                  pl.BlockSpec((GPS, 1, NS, 256), lambda g, c: (g, rv(c), 0, 0)),
                  pl.BlockSpec((Q, 256 * GPS), lambda g, c: (rv(c), g))],
        out_specs=(pl.BlockSpec((Q, 256 * GPS), lambda g, c: (rv(c), g)),
                   pl.BlockSpec((Q, bw), lambda g, c: (rv(c), g)),
                   pl.BlockSpec((Q, bw), lambda g, c: (rv(c), g)),
                   pl.BlockSpec((GPS, Q, 128), lambda g, c: (g, rv(c), 0)),
                   pl.BlockSpec((GPS, 8, 128), lambda g, c: (g, 0, 0))),
        out_shape=(jax.ShapeDtypeStruct((L, INNER), F32), jax.ShapeDtypeStruct((L, NG * NS), F32),
                   jax.ShapeDtypeStruct((L, NG * NS), F32), jax.ShapeDtypeStruct((NG, L, 128), F32),
                   jax.ShapeDtypeStruct((NG, 8, 128), F32)),
        scratch_shapes=[pltpu.VMEM((GPS, NS, 256), F32)], name="ssd_bwd",
        compiler_params=_cp(("parallel", "arbitrary")),
    )(xbc, xbc, xbc, dt4, cs4, sg4, vecg, s_all, dy)


def _dt_bwd(ddt, dproj, tl=256):
    L = ddt.shape[1]

    def kern(d_ref, _, dp_ref, gs_ref):
        @pl.when(pl.program_id(0) == 0)
        def _():
            gs_ref[...] = jnp.zeros_like(gs_ref)

        d = d_ref[0]
        for g in range(1, NG):
            d = d + pltpu.roll(d_ref[g], 4 * g, axis=1)
        gs_ref[...] += jnp.broadcast_to(jnp.sum(d, axis=0, keepdims=True), (8, 128))
        dp_ref[...] = jnp.concatenate([d, jnp.zeros_like(d)], axis=1).astype(BF16)

    return pl.pallas_call(
        kern, grid=(L // tl,),
        in_specs=[pl.BlockSpec((NG, tl, 128), lambda i: (0, i, 0)), pl.BlockSpec(memory_space=pl.ANY)],
        out_specs=(pl.BlockSpec((tl, 256), lambda i: (i, C_DT // 256)), pl.BlockSpec((8, 128), lambda i: (0, 0))),
        out_shape=(jax.ShapeDtypeStruct(dproj.shape, BF16), jax.ShapeDtypeStruct((8, 128), F32)),
        input_output_aliases={1: 0}, name="dt_bwd", compiler_params=_cp(("arbitrary",)),
    )(ddt, dproj)


GW = INNER // NG


def _gnorm_fwd(y, proj, w, tl=256):
    L = y.shape[0]
    zoff = C_Z // 1024

    def kern(y_ref, z_ref, w_ref, o_ref):
        z = z_ref[...].astype(F32)
        yz = y_ref[...] * (z * _sigmoid(z))
        wv = w_ref[...]
        for k in range(1024 // GW):
            sl = slice(GW * k, GW * (k + 1))
            v = yz[:, sl]
            rg = lax.rsqrt(jnp.mean(v * v, axis=-1, keepdims=True) + EPS)
            o_ref[:, sl] = ((v * rg) * wv[:, sl]).astype(BF16)

    blk = pl.BlockSpec((tl, 1024), lambda i, j: (i, j))
    return pl.pallas_call(
        kern, grid=(L // tl, 2),
        in_specs=[blk, pl.BlockSpec((tl, 1024), lambda i, j: (i, zoff + j)), pl.BlockSpec((1, 1024), lambda i, j: (0, j))],
        out_specs=blk, out_shape=jax.ShapeDtypeStruct((L, INNER), BF16), name="gnorm_fwd",
        compiler_params=_cp(("parallel", "parallel")),
    )(y, proj, w.reshape(1, INNER))


def _gnorm_bwd(dyb, y, proj, w, dproj, tl=256):
    L = y.shape[0]
    zoff = C_Z // 1024

    def kern(d_ref, y_ref, z_ref, w_ref, _, dy_ref, dp_ref, gw_ref):
        @pl.when(pl.program_id(1) == 0)
        def _():
            gw_ref[...] = jnp.zeros_like(gw_ref)

        z = z_ref[...].astype(F32)
        sg = _sigmoid(z)
        sz = z * sg
        yv = y_ref[...]
        yz = yv * sz
        dv = d_ref[...]
        wv = w_ref[...]
        for k in range(1024 // GW):
            sl = slice(GW * k, GW * (k + 1))
            v = yz[:, sl]
            rg = lax.rsqrt(jnp.mean(v * v, axis=-1, keepdims=True) + EPS)
            vn = v * rg
            dk = dv[:, sl]
            gw_ref[:, sl] += jnp.broadcast_to(jnp.sum(dk * vn, axis=0, keepdims=True), (8, GW))
            dvn = dk * wv[:, sl]
            dyz = rg * (dvn - vn * jnp.mean(dvn * vn, axis=-1, keepdims=True))
            dy_ref[:, sl] = dyz * sz[:, sl]
            dp_ref[:, sl] = (dyz * yv[:, sl] * (sg[:, sl] * (1.0 + z[:, sl] * (1.0 - sg[:, sl])))).astype(BF16)

    blk = pl.BlockSpec((tl, 1024), lambda j, i: (i, j))
    zblk = pl.BlockSpec((tl, 1024), lambda j, i: (i, zoff + j))
    return pl.pallas_call(
        kern, grid=(2, L // tl),
        in_specs=[blk, blk, zblk, pl.BlockSpec((1, 1024), lambda j, i: (0, j)), pl.BlockSpec(memory_space=pl.ANY)],
        out_specs=(blk, zblk, pl.BlockSpec((8, 1024), lambda j, i: (0, j))),
        out_shape=(jax.ShapeDtypeStruct((L, INNER), F32), jax.ShapeDtypeStruct(dproj.shape, BF16),
                   jax.ShapeDtypeStruct((8, INNER), F32)),
        input_output_aliases={4: 1}, name="gnorm_bwd", compiler_params=_cp(("parallel", "arbitrary")),
    )(dyb, y, proj, w.reshape(1, INNER), dproj)


def _merge_fwd(proj, bg, br_a, br_b, tl=256):
    L = proj.shape[0]
    goff = C_GATE // 1024

    def kern(g1_ref, g2_ref, b1_ref, b2_ref, a_ref, b_ref, o_ref):
        g1 = _sigmoid(g1_ref[...].astype(F32) + b1_ref[...])
        g2 = _sigmoid(g2_ref[...].astype(F32) + b2_ref[...])
        o_ref[...] = (g1 * a_ref[...] + g2 * b_ref[...]).astype(BF16)

    row = pl.BlockSpec((tl, 1024), lambda i: (i, 0))
    bg2 = bg.reshape(1, 2 * D)
    return pl.pallas_call(
        kern, grid=(L // tl,),
        in_specs=[pl.BlockSpec((tl, 1024), lambda i: (i, goff)), pl.BlockSpec((tl, 1024), lambda i: (i, goff + 1)),
                  pl.BlockSpec((1, 1024), lambda i: (0, 0)), pl.BlockSpec((1, 1024), lambda i: (0, 1)), row, row],
        out_specs=row, out_shape=jax.ShapeDtypeStruct((L, D), BF16), name="merge_fwd",
        compiler_params=_cp(("parallel",)),
    )(proj, proj, bg2, bg2, br_a, br_b)


def _merge_bwd(dm, proj, bg, br_a, br_b, dproj, tl=256):
    L = proj.shape[0]
    goff = C_GATE // 1024

    def kern(dm_ref, g_ref, b_ref, a_ref, bb_ref, _, dbr_ref, dp_ref, gb_ref):
        j = pl.program_id(0)

        @pl.when(pl.program_id(1) == 0)
        def _():
            gb_ref[...] = jnp.zeros_like(gb_ref)

        g = _sigmoid(g_ref[...].astype(F32) + b_ref[...])
        br = jnp.where(j == 0, a_ref[...], bb_ref[...])
        dmv = dm_ref[...]
        dbr_ref[0] = (dmv * g).astype(BF16)
        dgate = dmv * br * g * (1.0 - g)
        gb_ref[...] += jnp.broadcast_to(jnp.sum(dgate, axis=0, keepdims=True), (8, 1024))
        dp_ref[...] = dgate.astype(BF16)

    row = pl.BlockSpec((tl, 1024), lambda j, i: (i, 0))
    gblk = pl.BlockSpec((tl, 1024), lambda j, i: (i, goff + j))
    return pl.pallas_call(
        kern, grid=(2, L // tl),
        in_specs=[row, gblk, pl.BlockSpec((1, 1024), lambda j, i: (0, j)), row, row, pl.BlockSpec(memory_space=pl.ANY)],
        out_specs=(pl.BlockSpec((1, tl, 1024), lambda j, i: (j, i, 0)), gblk, pl.BlockSpec((8, 1024), lambda j, i: (0, j))),
        out_shape=(jax.ShapeDtypeStruct((2, L, D), BF16), jax.ShapeDtypeStruct(dproj.shape, BF16),
                   jax.ShapeDtypeStruct((8, 2 * D), F32)),
        input_output_aliases={5: 1}, name="merge_bwd", compiler_params=_cp(("parallel", "arbitrary")),
    )(dm, proj, bg.reshape(1, 2 * D), br_a, br_b, dproj)


def _coords():
    return lax.axis_index("x"), lax.axis_index("y"), lax.axis_index("c")


def _other_chips(sk):
    xk, yk = sk // 2, sk % 2
    return [((1 - xk, yk), 2 * (1 - xk) + yk), ((xk, 1 - yk), 2 * xk + 1 - yk), ((1 - xk, 1 - yk), 2 * (1 - xk) + 1 - yk)]


def _rows(start, size):
    assert size % 128 == 0
    return pl.ds(pl.multiple_of(start, 128), size)


def _per_chip(fn):
    x, y, _ = _coords()
    s = 2 * x + y
    for sk in range(4):
        pl.when(s == sk)(functools.partial(fn, sk))


XTRA = PIECE - PMAIN


def _place(shard, full_shape, block, index_map, idx, name, blk0=0, nblk=None, dep=None, into=None):
    in_block = block[-2:]
    if nblk is None:
        nblk = shard.shape[0] // in_block[0]

    def kern(idx_ref, s_ref, *rest):
        o_ref = rest[-1]
        o_ref[...] = s_ref[...].astype(BF16).reshape(o_ref.shape)

    extra = ([dep] if dep is not None else []) + ([into] if into is not None else [])
    grid_spec = pltpu.PrefetchScalarGridSpec(
        num_scalar_prefetch=1, grid=(nblk,),
        in_specs=[pl.BlockSpec(in_block, lambda i, idx_ref: (blk0 + i, 0))] + [_ANY] * len(extra),
        out_specs=pl.BlockSpec(block, index_map))
    aliases = {1 + len(extra): 0} if into is not None else {}
    return pl.pallas_call(kern, grid_spec=grid_spec, out_shape=jax.ShapeDtypeStruct(full_shape, BF16), name=name,
                          input_output_aliases=aliases, compiler_params=_cp(("arbitrary",)))(idx, shard, *extra)


_SEM = pl.BlockSpec(memory_space=pltpu.SEMAPHORE)
_EFFECT = pltpu.SideEffectType.DATAFLOW_SIDE_EFFECTING


_ANY = pl.BlockSpec(memory_space=pl.ANY)


def _tie(v, dep, name):
    def body(v_ref, dep_ref, o_ref):
        del v_ref, dep_ref, o_ref

    return pl.pallas_call(body, out_shape=jax.ShapeDtypeStruct(v.shape, v.dtype), in_specs=[_ANY, _ANY],
                          out_specs=_ANY, input_output_aliases={0: 0}, name=name)(v, dep)


def _split_call(name, arrays, start=None, wait=None, wait_sems=None, after=None):
    keys = list(arrays)
    n = len(keys)
    n_start = start.n if start is not None else 0
    afters = [] if after is None else (list(after) if isinstance(after, (list, tuple)) else [after])

    def body(*refs):
        pos = n
        if wait is not None:
            wss, wrs = refs[pos], refs[pos + 1]
            pos += 2
        pos += len(afters)
        if start is not None:
            nss, nrs = refs[pos], refs[pos + 1]
            pos += 2
        R = dict(zip(keys, refs[pos:pos + n]))
        token = refs[pos + n]
        x, y, c = _coords()

        def desc(src, dst, dev, ss, rs, k):
            return pltpu.make_async_remote_copy(src_ref=src, dst_ref=dst, send_sem=ss.at[k], recv_sem=rs.at[k],
                                                device_id=dev, device_id_type=MESH)

        def run(sk):
            if wait is not None:
                for k, (snd, land) in enumerate(wait.copies(sk, R)):
                    if snd is not None:
                        desc(snd[0], snd[1], snd[2], wss, wrs, k).wait_send()
                    if land is not None:
                        desc(land, land, (x, y, c), wss, wrs, k).wait_recv()
            if start is not None:
                for k, (snd, land) in enumerate(start.copies(sk, R)):
                    if snd is not None:
                        desc(snd[0], snd[1], snd[2], nss, nrs, k).start()

        _per_chip(run)
        token[...] = jnp.zeros_like(token)

    hbm = pl.BlockSpec(memory_space=HBM)
    vals = [arrays[k] for k in keys]
    ins, in_specs = list(vals), [hbm] * n
    if wait is not None:
        ins += list(wait_sems)
        in_specs += [_SEM, _SEM]
    ins += afters
    in_specs += [pl.BlockSpec(memory_space=pl.ANY)] * len(afters)
    out_shape, out_specs = [], []
    if start is not None:
        out_shape += [pltpu.SemaphoreType.DMA((n_start,)), pltpu.SemaphoreType.DMA((n_start,))]
        out_specs += [_SEM, _SEM]
    first = len(out_shape)
    out_shape += [jax.ShapeDtypeStruct(v.shape, v.dtype) for v in vals] + [jax.ShapeDtypeStruct((8, 128), F32)]
    out_specs += [hbm] * n + [pl.BlockSpec(memory_space=pltpu.VMEM)]
    res = pl.pallas_call(
        body, out_shape=tuple(out_shape), in_specs=in_specs, out_specs=tuple(out_specs),
        input_output_aliases={i: first + i for i in range(n)}, name=name,
        compiler_params=pltpu.CompilerParams(has_side_effects=_EFFECT),
    )(*ins)
    sems = (res[0], res[1]) if start is not None else None
    return dict(zip(keys, res[first:first + n])), sems, res[-1]


class _Plan:
    def __init__(self, n, copies):
        self.n, self.copies = n, copies


_HM, _HX = PMAIN // 2, XTRA // 2
WAVE0 = 1024
WAVES = ((0, WAVE0), (WAVE0, _HM - WAVE0))
_WIN = {
    "wq0": (True, "wct", lambda r, sc, hc: r.at[_rows(PMAIN * sc + _HM * hc + WAVES[0][0], WAVES[0][1]), :]),
    "wq1": (True, "wct", lambda r, sc, hc: r.at[_rows(PMAIN * sc + _HM * hc + WAVES[1][0], WAVES[1][1]), :]),
    "xt": (True, "xt", lambda r, sc, hc: r.at[sc, _rows(_HX * hc, _HX), :]),
    "w1": (True, "w1", lambda r, sc, hc: r.at[_rows(512 * hc, 512), pl.ds(1024 * sc, 1024)]),
    "w2": (True, "w2", lambda r, sc, hc: r.at[_rows(1024 * sc + 512 * hc, 512), :]),
    "wa": (True, "wa", lambda r, sc, hc: r.at[_rows(256 * sc + 128 * hc, 128), :]),
    "wb": (True, "wb", lambda r, sc, hc: r.at[_rows(512 * sc + 256 * hc, 256), :]),
    "wo": (True, "wo", lambda r, sc, hc: r.at[_rows(256 * sc + 128 * hc, 128), :]),
    "cw": (False, "cw", lambda r, sc, hc: r.at[sc]),
}


_PIECE_SRC = {
    "wq0": lambda p, hc: p.at[_rows(_HM * hc + WAVES[0][0], WAVES[0][1]), :],
    "wq1": lambda p, hc: p.at[_rows(_HM * hc + WAVES[1][0], WAVES[1][1]), :],
    "xt": lambda p, hc: p.at[_rows(PMAIN + _HX * hc, _HX), :],
}


def _ag_chips_plan(keys):
    def copies(sk, R):
        _, _, c = _coords()
        out = []
        for key in keys:
            _, arr, win = _WIN[key]
            for (px, py), ps in _other_chips(sk):
                dst = win(R[arr], sk, c)
                src = _PIECE_SRC[key](R["piece"], c) if key in _PIECE_SRC else dst
                out.append(((src, dst, (px, py, c)), win(R[arr], ps, c)))
        return out
    return _Plan(3 * len(keys), copies)


def _ag_sibling_plan(keys):
    keys = [k for k in keys if _WIN[k][0]]

    def copies(sk, R):
        x, y, c = _coords()
        out = []
        for key in keys:
            _, arr, win = _WIN[key]
            for _, ps in _other_chips(sk):
                w = win(R[arr], ps, c)
                out.append(((w, w, (x, y, 1 - c)), win(R[arr], ps, 1 - c)))
        return out
    return _Plan(3 * len(keys), copies)


def _in_proj_wave(h, wct, wave, proj=None, tm=2048):
    L = h.shape[0]
    tm = min(tm, L)
    off, size = WAVES[wave]
    start = lambda j: pl.multiple_of(_HM * j + off, 128)

    def kern(h_ref, w_ref, *rest):
        o_ref = rest[-1]
        o_ref[...] = lax.dot_general(h_ref[...], w_ref[...], _DIMS["nt"], preferred_element_type=F32).astype(BF16)

    in_specs = [pl.BlockSpec((tm, D), lambda j, i: (i, 0)),
                pl.BlockSpec((pl.Element(size), pl.Element(D)), lambda j, i: (start(j), 0))]
    args, aliases = [h, wct], {}
    if proj is not None:
        in_specs.append(pl.BlockSpec(memory_space=pl.ANY))
        args.append(proj)
        aliases = {2: 0}
    return pl.pallas_call(
        kern, grid=(8, L // tm), in_specs=in_specs,
        out_specs=pl.BlockSpec((pl.Element(tm), pl.Element(size)), lambda j, i: (i * tm, start(j))),
        out_shape=jax.ShapeDtypeStruct((L, NCW), BF16), input_output_aliases=aliases,
        name="in_proj_wave%d" % wave, compiler_params=_cp(("parallel", "parallel")),
    )(*args)


def _fix_wct(wct, xt):
    nb = PMAIN // XTRA

    def kern(w_ref, x_ref, o_ref):
        k = pl.program_id(0)
        xv = x_ref[0]
        o_ref[...] = jnp.where(k < 3, (w_ref[...].astype(F32) + xv.astype(F32)).astype(BF16), xv)

    blk = pl.BlockSpec((XTRA, D), lambda k: (nb * (k + 1), 0))
    rblk = pl.BlockSpec((XTRA, D), lambda k: (jnp.where(k < 3, nb * (k + 1), 0), 0))
    return pl.pallas_call(
        kern, grid=(4,), in_specs=[rblk, pl.BlockSpec((1, XTRA, D), lambda k: (k, 0, 0))], out_specs=blk,
        out_shape=jax.ShapeDtypeStruct(wct.shape, BF16), input_output_aliases={0: 0}, name="fix_wct",
        compiler_params=_cp(("arbitrary",)),
    )(wct, xt)


_HP = PIECE // 2
_GWIN = [
    lambda r, sc, hc: r.at[_rows(PMAIN * sc + _HP * hc, _HP), :],
    lambda r, sc, hc: r.at[_rows(512 * hc, 512), pl.ds(1024 * sc, 1024)],
    lambda r, sc, hc: r.at[_rows(1024 * sc + 512 * hc, 512), :],
    lambda r, sc, hc: r.at[_rows(256 * sc + 128 * hc, 128), :],
    lambda r, sc, hc: r.at[_rows(512 * sc + 256 * hc, 256), :],
    lambda r, sc, hc: r.at[_rows(256 * sc + 128 * hc, 128), :],
]
HALF_SHAPES = [(PIECE // 2, D), (512, 1024), (512, 1024), (128, 1024), (256, 1024), (128, 1024)]


def _rs_sibling_plan(ts):
    def copies(sk, R):
        x, y, c = _coords()
        out = []
        for t in ts:
            for sc in range(4):
                land = R["ra%d" % t].at[sc]
                out.append(((_GWIN[t](R["g%d" % t], sc, 1 - c), land, (x, y, 1 - c)), land))
        return out
    return _Plan(4 * len(ts), copies)


def _rs_chips_plan(ts):
    def copies(sk, R):
        _, _, c = _coords()
        out = []
        for t in ts:
            for j, ((px, py), ps) in enumerate(_other_chips(sk)):
                land = R["rb%d" % t].at[j]
                out.append(((R["hb%d" % t].at[ps], land, (px, py, c)), land))
        return out
    return _Plan(3 * len(ts), copies)


def _rs_share_plan(ts):
    def copies(sk, R):
        x, y, c = _coords()
        out = []
        for t in ts:
            rows = HALF_SHAPES[t][0]
            mine = R["f%d" % t].at[_rows(rows * c, rows), :]
            out.append(((mine, mine, (x, y, 1 - c)), R["f%d" % t].at[_rows(rows * (1 - c), rows), :]))
        return out
    return _Plan(len(ts), copies)


def _half_tiling(t):
    rows, cols = HALF_SHAPES[t]
    if t == 0:
        return (rows // 2, cols), 2, lambda i: (i, 0)
    return (rows, cols), 1, lambda i: (0, 0)


def _window_spec(t, blk):
    if t == 0:
        return pl.BlockSpec((pl.Element(blk[0]), pl.Element(blk[1])), lambda i, sc, idx_ref: (
            pl.multiple_of(PMAIN * sc + _HP * idx_ref[1] + blk[0] * i, 128), 0))
    if t == 1:
        return pl.BlockSpec(blk, lambda i, sc, idx_ref: (idx_ref[1], sc))
    return pl.BlockSpec(blk, lambda i, sc, idx_ref: (2 * sc + idx_ref[1], 0))


def _chip_sum(g, ra, t, idx, name):
    rows, cols = HALF_SHAPES[t]
    blk, nblk, inner = _half_tiling(t)

    def kern(idx_ref, g_ref, r_ref, hb_ref, hf_ref):
        v = g_ref[...].astype(F32) + r_ref[0].astype(F32)
        hb_ref[0] = v.astype(BF16)

        @pl.when(pl.program_id(1) == idx_ref[0])
        def _():
            hf_ref[...] = v

    omap = lambda i, sc, idx_ref: (sc,) + inner(i)
    grid_spec = pltpu.PrefetchScalarGridSpec(
        num_scalar_prefetch=1, grid=(nblk, 4),
        in_specs=[_window_spec(t, blk), pl.BlockSpec((1,) + blk, omap)],
        out_specs=(pl.BlockSpec((1,) + blk, omap), pl.BlockSpec(blk, lambda i, sc, idx_ref: inner(i))))
    return pl.pallas_call(
        kern, grid_spec=grid_spec,
        out_shape=(jax.ShapeDtypeStruct((4, rows, cols), BF16), jax.ShapeDtypeStruct((rows, cols), F32)),
        name=name, compiler_params=_cp(("parallel", "arbitrary")),
    )(idx, g, ra)


def _final_sum(hf, rb, t, idx, name):
    rows, cols = HALF_SHAPES[t]
    blk, nblk, inner = _half_tiling(t)
    nbr = rows // blk[0]

    def kern(idx_ref, h_ref, r_ref, o_ref):
        o_ref[...] = ((h_ref[...] + r_ref[0].astype(F32)) + r_ref[1].astype(F32)) + r_ref[2].astype(F32)

    def omap(i, idx_ref):
        r, cidx = inner(i)
        return nbr * idx_ref[1] + r, cidx

    grid_spec = pltpu.PrefetchScalarGridSpec(
        num_scalar_prefetch=1, grid=(nblk,),
        in_specs=[pl.BlockSpec(blk, lambda i, idx_ref: inner(i)),
                  pl.BlockSpec((3,) + blk, lambda i, idx_ref: (0,) + inner(i))],
        out_specs=pl.BlockSpec(blk, omap))
    return pl.pallas_call(
        kern, grid_spec=grid_spec, out_shape=jax.ShapeDtypeStruct((2 * rows, cols), F32),
        name=name, compiler_params=_cp(("parallel",)),
    )(idx, hf, rb)


class _ReduceScatter:
    def __init__(self, ts, grads, idx, tag):
        self.ts, self.idx, self.tag = ts, idx, tag
        arr = {}
        for t in ts:
            arr["g%d" % t] = grads[t]
            arr["ra%d" % t] = lax.empty((4,) + HALF_SHAPES[t], BF16)
        self.plan = _rs_sibling_plan(ts)
        self.arr, self.sems, self.token = _split_call("rs_sibling_start_" + tag, arr, start=self.plan)

    def chips(self, after):
        arr, _, _ = _split_call("rs_sibling_wait_" + self.tag, self.arr, wait=self.plan, wait_sems=self.sems, after=after)
        brr, self.hf = {}, {}
        for t in self.ts:
            hb, self.hf[t] = _chip_sum(arr["g%d" % t], arr["ra%d" % t], t, self.idx, "chip_sum_%d" % t)
            brr["hb%d" % t] = hb
            brr["rb%d" % t] = lax.empty((3,) + HALF_SHAPES[t], BF16)
        self.plan = _rs_chips_plan(self.ts)
        self.arr, self.sems, self.token = _split_call("rs_chips_start_" + self.tag, brr, start=self.plan)
        return self.token

    def share(self, after):
        brr, _, _ = _split_call("rs_chips_wait_" + self.tag, self.arr, wait=self.plan, wait_sems=self.sems, after=after)
        frr = {"f%d" % t: _final_sum(self.hf[t], brr["rb%d" % t], t, self.idx, "final_sum_%d" % t) for t in self.ts}
        self.plan = _rs_share_plan(self.ts)
        self.arr, self.sems, self.token = _split_call("rs_share_start_" + self.tag, frr, start=self.plan)
        return self.token

    def result(self, after):
        frr, _, _ = _split_call("rs_share_wait_" + self.tag, self.arr, wait=self.plan, wait_sems=self.sems, after=after)
        return {t: frr["f%d" % t] for t in self.ts}


def _all8_plan(key):
    def copies(sk, R):
        x, y, c = _coords()
        own = R[key].at[4 * x + 2 * y + c]
        out = []
        for k in range(1, 8):
            dev = ((1 - x) if (k >> 2) & 1 else x, (1 - y) if (k >> 1) & 1 else y, (1 - c) if k & 1 else c)
            out.append(((own, own, dev), R[key].at[4 * dev[0] + 2 * dev[1] + dev[2]]))
        return out
    return _Plan(7, copies)


def _sum8(v, name="small_sum"):
    def kern(v_ref, o_ref):
        acc = v_ref[0]
        for k in range(1, 8):
            acc = acc + v_ref[k]
        o_ref[...] = acc

    return pl.pallas_call(kern, out_shape=jax.ShapeDtypeStruct(v.shape[1:], F32), name=name)(v)


def _adamw(w, g, m, v, name, tr=128, blk0=0, nblk=None, into=None, copy_g=False):
    R, C = w.shape
    tr = min(tr, R)
    if nblk is None:
        assert R % tr == 0 and blk0 == 0
        nblk = R // tr
    n_out = 4 if copy_g else 3

    def kern(*refs):
        w_ref, g_ref, m_ref, v_ref = refs[:4]
        d_ref, mo_ref, vo_ref = refs[-n_out:][:3]
        gv = g_ref[...]
        mn = ADAM_B1 * m_ref[...] + (1.0 - ADAM_B1) * gv
        vn = ADAM_B2 * v_ref[...] + (1.0 - ADAM_B2) * (gv * gv)
        m_hat = mn / (1.0 - ADAM_B1 ** ADAM_STEP)
        v_hat = vn / (1.0 - ADAM_B2 ** ADAM_STEP)
        d_ref[...] = -ADAM_LR * (m_hat / (jnp.sqrt(v_hat) + ADAM_EPS) + ADAM_WD * w_ref[...])
        mo_ref[...] = mn
        vo_ref[...] = vn
        if copy_g:
            refs[-1][...] = gv

    blk = pl.BlockSpec((tr, C), lambda i: (blk0 + i, 0))
    sd = jax.ShapeDtypeStruct((R, C), F32)
    in_specs, args, aliases = [blk] * 4, [w, g, m, v], {}
    if into is not None:
        in_specs += [pl.BlockSpec(memory_space=pl.ANY)] * 3
        args += list(into)
        aliases = {4: 0, 5: 1, 6: 2}
    return pl.pallas_call(kern, grid=(nblk,), in_specs=in_specs, out_specs=(blk,) * n_out, out_shape=(sd,) * n_out,
                          input_output_aliases=aliases, name=name, compiler_params=_cp(("parallel",)))(*args)


def _adamw_w_in(wt, gp, mt, vt, offs, name, r0, tr, nblk, views, into=None, blk_key=None):
    el = lambda n: (pl.Element(n), pl.Element(D))
    first = (lambda o: r0) if blk_key is None else (lambda o: tr * o[blk_key])
    own = pl.BlockSpec(el(tr), lambda i, o: (pl.multiple_of(first(o) + tr * i, 8), 0))

    def view(k):
        return pl.BlockSpec(el(tr), lambda i, o: (pl.multiple_of(jnp.maximum(first(o) + tr * i + o[k], 0), 8), 0))

    def kern(o_ref, w_ref, m_ref, v_ref, *refs):
        g_refs, (d_ref, mo_ref, vo_ref, go_ref) = refs[:len(views)], refs[-4:]
        gv = g_refs[0][...]
        if len(views) == 2:
            row = first(o_ref) + tr * pl.program_id(0) + lax.broadcasted_iota(jnp.int32, (tr, D), 0)
            gv = jnp.where(row < o_ref[2], gv, g_refs[1][...])
        mn = ADAM_B1 * m_ref[...] + (1.0 - ADAM_B1) * gv
        vn = ADAM_B2 * v_ref[...] + (1.0 - ADAM_B2) * (gv * gv)
        m_hat = mn / (1.0 - ADAM_B1 ** ADAM_STEP)
        v_hat = vn / (1.0 - ADAM_B2 ** ADAM_STEP)
        d_ref[...] = -ADAM_LR * (m_hat / (jnp.sqrt(v_hat) + ADAM_EPS) + ADAM_WD * w_ref[...])
        mo_ref[...] = mn
        vo_ref[...] = vn
        go_ref[...] = gv

    in_specs = [own, own, own] + [view(k) for k in views]
    args = [wt, mt, vt] + [gp] * len(views)
    aliases = {}
    if into is not None:
        in_specs += [pl.BlockSpec(memory_space=pl.ANY)] * 4
        args += list(into)
        aliases = {1 + len(args) - 4 + j: j for j in range(4)}
    grid_spec = pltpu.PrefetchScalarGridSpec(num_scalar_prefetch=1, grid=(nblk,), in_specs=in_specs,
                                             out_specs=(own,) * 4)
    sd = jax.ShapeDtypeStruct(wt.shape, F32)
    return pl.pallas_call(kern, grid_spec=grid_spec, out_shape=(sd,) * 4, input_output_aliases=aliases, name=name,
                          compiler_params=_cp(("parallel",)))(offs, *args)


def _to_piece(wt, s):
    z = lambda n: jnp.zeros((n, D), wt.dtype)
    pads = [functools.partial(lambda k, w: jnp.pad(w, ((8 * k, PIECE - W_SHARD - 8 * k), (0, 0))).astype(BF16), k)
            for k in range(3)]
    last = lambda w: jnp.concatenate([z(24), w[:744], w[776:], w[744:776], z(PIECE - 24 - W_SHARD)], axis=0).astype(BF16)
    return lax.switch(s, pads + [last], wt)


_SMALL = [("b_gate", 2048), ("ssm_conv_b", 4096), ("dt_bias", 32), ("A_log", 32), ("D_skip", 32),
          ("ssm_norm_w", 2048), ("norm_mlp", 1024), ("norm_final", 1024), ("sc_conv_w", 3072), ("ssm_conv_w", 16384),
          ("loss", 1)]


def _pack(vals, table, rows):
    parts = []
    for name, n in table:
        v = vals[name].reshape(-1).astype(F32)
        pad = (-n) % 128
        parts.append(jnp.pad(v, (0, pad)) if pad else v)
    flat = jnp.concatenate(parts)
    return jnp.pad(flat, (0, rows * 128 - flat.shape[0])).reshape(rows, 128)


def _unpack(arr, table):
    flat = arr.reshape(-1)
    out, off = {}, 0
    for name, n in table:
        out[name] = flat[off:off + n]
        off += n + ((-n) % 128)
    return out


def kernel(x, norm_mix, w_in, b_gate, sc_conv_w, ssm_conv_w, ssm_conv_b, dt_bias, A_log, D_skip, ssm_norm_w, w_branch_sc, w_branch_ssm, w_out, norm_mlp, w_mlp1, w_mlp2, norm_final, loss_target, m_norm_mix, m_w_in, m_b_gate, m_sc_conv_w, m_ssm_conv_w, m_ssm_conv_b, m_dt_bias, m_A_log, m_D_skip, m_ssm_norm_w, m_w_branch_sc, m_w_branch_ssm, m_w_out, m_norm_mlp, m_w_mlp1, m_w_mlp2, m_norm_final, v_norm_mix, v_w_in, v_b_gate, v_sc_conv_w, v_ssm_conv_w, v_ssm_conv_b, v_dt_bias, v_A_log, v_D_skip, v_ssm_norm_w, v_w_branch_sc, v_w_branch_ssm, v_w_out, v_norm_mlp, v_w_mlp1, v_w_mlp2, v_norm_final):
    L = x.shape[1]
    nc = L // Q
    xi, yi, ci = lax.axis_index("x"), lax.axis_index("y"), lax.axis_index("c")
    s = 2 * xi + yi
    idx = jnp.stack([s, ci]).astype(jnp.int32)
    x0 = x.reshape(L, D)
    tgt = loss_target.reshape(L, D)
    small_names = ["b_gate", "sc_conv_w", "ssm_conv_w", "ssm_conv_b", "dt_bias", "A_log", "D_skip", "ssm_norm_w",
                   "norm_mlp", "norm_final"]
    small_wmv = [dict(zip(small_names, vals)) for vals in (
        (b_gate, sc_conv_w, ssm_conv_w, ssm_conv_b, dt_bias, A_log, D_skip, ssm_norm_w, norm_mlp, norm_final),
        (m_b_gate, m_sc_conv_w, m_ssm_conv_w, m_ssm_conv_b, m_dt_bias, m_A_log, m_D_skip, m_ssm_norm_w, m_norm_mlp,
         m_norm_final),
        (v_b_gate, v_sc_conv_w, v_ssm_conv_w, v_ssm_conv_b, v_dt_bias, v_A_log, v_D_skip, v_ssm_norm_w, v_norm_mlp,
         v_norm_final))]
    small_table = [(n, int(small_wmv[0][n].size)) for n in small_names]
    small_rows = 136
    pk_w, pk_m, pk_v = [_pack(d, small_table, small_rows) for d in small_wmv]

    piece = _to_piece(w_in.T, s)
    nb = PMAIN // XTRA
    cws = jnp.zeros((8, 1280), F32)
    cws = cws.at[0:3, 0:256].set(sc_conv_w).at[0:4, 256:1280].set(ssm_conv_w)
    cw0 = lax.dynamic_update_slice(jnp.zeros((4, 8, 1280), F32), cws[None], (s, 0, 0))
    win_keys, win2_keys, mid_keys, end_keys = ["xt", "cw", "wq0"], ["wq1"], ["wa", "wb", "wo", "w1"], ["w2"]
    gw, sems_w, tok = _split_call(
        "ag_win_start", {"wct": lax.empty((NCW, D), BF16), "xt": lax.empty((4, XTRA, D), BF16), "cw": cw0, "piece": piece},
        start=_ag_chips_plan(win_keys))
    g2, sems_w2, tok = _split_call("ag_win2_start", {"wct": gw["wct"], "piece": gw["piece"]},
                                   start=_ag_chips_plan(win2_keys), after=tok)
    piece = g2["piece"]
    gw["wct"] = _place(piece, (NCW, D), (XTRA, D), lambda i, r: (nb * r[0] + i, 0), idx, "place_wct", nblk=nb,
                       dep=tok, into=g2["wct"])
    gw["xt"] = _place(piece, (4, XTRA, D), (1, XTRA, D), lambda i, r: (r[0], 0, 0), idx, "place_xt", blk0=nb, nblk=1,
                      dep=tok, into=gw["xt"])
    gw["piece"] = piece
    wa0 = _place(w_branch_sc, (D, D), (256, 1024), lambda i, r: (r[0], 0), idx, "place_wa", dep=tok)
    wb0 = _place(w_branch_ssm, (INNER, D), (512, 1024), lambda i, r: (r[0], 0), idx, "place_wb", dep=tok)
    wo0 = _place(w_out, (D, D), (256, 1024), lambda i, r: (r[0], 0), idx, "place_wo", dep=tok)
    w10 = _place(w_mlp1, (D, DFF), (256, 1024), lambda i, r: (i, r[0]), idx, "place_w1", dep=tok)
    gm, sems_m, tok = _split_call("ag_mid_start", {"wa": wa0, "wb": wb0, "wo": wo0, "w1": w10},
                                  start=_ag_chips_plan(mid_keys))
    w20 = _place(w_mlp2, (DFF, D), (256, 1024), lambda i, r: (4 * r[0] + i, 0), idx, "place_w2", dep=tok)
    ge, sems_e, tok = _split_call("ag_end_start", {"w2": w20}, start=_ag_chips_plan(end_keys))
    h = _rms_fwd(x0, norm_mix, "rms_mix", dep=tok)
    gw, sems_w, tok = _split_call("ag_win_pass", gw, wait=_ag_chips_plan(win_keys), wait_sems=sems_w,
                                  start=_ag_sibling_plan(win_keys), after=[h, pk_w, pk_m, pk_v])
    gw, _, _ = _split_call("ag_win_done", gw, wait=_ag_sibling_plan(win_keys), wait_sems=sems_w, after=tok)
    wc, cw_all = _fix_wct(gw["wct"], gw["xt"]), gw["cw"]
    sc_w_full = jnp.concatenate([cw_all[k, :, 0:256] for k in range(4)], axis=1)
    ssm_w_full = jnp.concatenate([cw_all[k, :, 256:1280] for k in range(4)], axis=1)
    cw4 = ssm_w_full.at[4].set(ssm_conv_b)
    vec = jnp.zeros((8, 128), F32).at[0, :NH].set(dt_bias).at[1, :NH].set(A_log)
    vecg = jnp.zeros((NG, 8, 128), F32).at[:, 0, :4].set(A_log.reshape(NG, 4)).at[:, 1, :4].set(D_skip.reshape(NG, 4))

    dtraw = _matmul(h, wc[C_DT:], "nt", F32, 512, 256, 1024, "in_proj_dt")
    proj = _in_proj_wave(h, wc, 0)
    g2, sems_w2, tok = _split_call("ag_win2_pass", {"wct": wc, "piece": gw["piece"]},
                                   wait=_ag_chips_plan(win2_keys), wait_sems=sems_w2,
                                   start=_ag_sibling_plan(win2_keys), after=[proj, dtraw])
    g2, _, _ = _split_call("ag_win2_done", g2, wait=_ag_sibling_plan(win2_keys), wait_sems=sems_w2, after=tok)
    wc = g2["wct"]
    proj = _in_proj_wave(h, wc, 1, proj=proj)
    ya = _sc_fwd(proj, sc_w_full)
    xbc = _ssm_conv_fwd(proj, cw4)
    dt4, cs4, sg4 = _dt_prep(dtraw, vec)
    y, s_all = _ssd_fwd(xbc, dt4, cs4, vecg)
    gm, sems_m, tok = _split_call("ag_mid_pass", gm, wait=_ag_chips_plan(mid_keys), wait_sems=sems_m,
                                  start=_ag_sibling_plan(mid_keys), after=[y, ya])
    y = _tie(y, tok, "tie_y")
    yb = _gnorm_fwd(y, proj, ssm_norm_w)
    gm, _, _ = _split_call("ag_mid_done", gm, wait=_ag_sibling_plan(mid_keys), wait_sems=sems_m, after=yb)
    wa, wb, wo, w1 = gm["wa"], gm["wb"], gm["wo"], gm["w1"]
    ge, sems_e, tok = _split_call("ag_end_pass", ge, wait=_ag_chips_plan(end_keys), wait_sems=sems_e,
                                  start=_ag_sibling_plan(end_keys), after=yb)
    br_a = _matmul(ya, wa, "nn", F32, 1024, 1024, 1024, "branch_sc", dep=tok)
    br_b = _matmul(yb, wb, "nn", F32, 1024, 1024, 2048, "branch_ssm")
    merged = _merge_fwd(proj, b_gate, br_a, br_b)
    x1 = _matmul(merged, wo, "nn", F32, 1024, 1024, 1024, "out_proj", epi="res", extra=x0)
    h2 = _rms_fwd(x1, norm_mlp, "rms_mlp")
    a1, rl = _matmul(h2, w1, "nn", BF16, 1024, 1024, 1024, "mlp1", epi="relu2", n_outer=True)
    ge, _, _ = _split_call("ag_end_done", ge, wait=_ag_sibling_plan(end_keys), wait_sems=sems_e, after=a1)
    w2 = ge["w2"]
    x2 = _matmul(rl, w2, "nn", F32, 512, 1024, 4096, "mlp2", epi="res", extra=x1)
    dx2, g_nf, loss8 = _final(x2, norm_final, tgt)

    da = _matmul(dx2, w2, "nt", BF16, 1024, 1024, 1024, "mlp2_dx", epi="drelu", extra=a1, n_outer=True)
    g_w2 = _matmul(rl, dx2, "tn", BF16, 1024, 1024, 2048, "mlp2_dw")
    g_w1 = _matmul(h2, da, "tn", BF16, 1024, 1024, 2048, "mlp1_dw")
    dh2 = _matmul(da, w1, "nt", F32, 512, 1024, 4096, "mlp1_dx")
    dx1, g_nmlp = _rms_bwd(dh2, x1, norm_mlp, dx2, "rms_mlp_bwd")
    dmerged = _matmul(dx1, wo, "nt", F32, 1024, 1024, 1024, "out_proj_dx")
    g_wo = _matmul(merged, dx1, "tn", BF16, 1024, 1024, 2048, "out_proj_dw")
    dproj = lax.empty((L, NCW), BF16)
    dbr, dproj, g_bg = _merge_bwd(dmerged, proj, b_gate, br_a, br_b, dproj)
    dya = _matmul(dbr[0], wa, "nt", F32, 1024, 1024, 1024, "branch_sc_dx")
    g_wa = _matmul(ya, dbr[0], "tn", BF16, 1024, 1024, 2048, "branch_sc_dw")
    dproj, g_scw = _sc_bwd(dya, proj, sc_w_full, dproj)
    dyb = _matmul(dbr[1], wb, "nt", F32, 1024, 1024, 1024, "branch_ssm_dx", n_outer=True)
    g_wb = _matmul(yb, dbr[1], "tn", BF16, 1024, 1024, 2048, "branch_ssm_dw")
    rs_a = _ReduceScatter([1, 2, 3, 4, 5], {1: g_w1, 2: g_w2, 3: g_wa, 4: g_wb, 5: g_wo}, idx, "a")
    dy, dproj, g_snw = _gnorm_bwd(_tie(dyb, rs_a.token, "tie_dyb"), y, proj, ssm_norm_w, dproj)
    tok = rs_a.chips(after=dy)
    dxs, dbm, dcm, ddt_g, st = _ssd_bwd(xbc, dt4, cs4, sg4, vecg, s_all, _tie(dy, tok, "tie_dy"))
    dproj, gx1 = _ssm_conv_bwd(dxs, proj, cw4, dproj, 0, "ssm_conv_bwd_x")
    dproj, gx2 = _ssm_conv_bwd(dbm, proj, cw4, dproj, INNER, "ssm_conv_bwd_b")
    dproj, gx3 = _ssm_conv_bwd(dcm, proj, cw4, dproj, INNER + NG * NS, "ssm_conv_bwd_c")
    g_cw4 = jnp.concatenate([gx1, gx2, gx3], axis=1)
    dproj, g_dtb = _dt_bwd(ddt_g, dproj)
    small = {"b_gate": g_bg[0], "ssm_conv_b": g_cw4[4], "dt_bias": g_dtb[0, :NH],
             "A_log": st[:, 0, :4], "D_skip": st[:, 1, :4], "ssm_norm_w": g_snw[0], "norm_mlp": g_nmlp[0],
             "norm_final": g_nf[0], "sc_conv_w": g_scw[0:3], "ssm_conv_w": g_cw4[0:4], "loss": loss8[0, 0:1]}
    me = 4 * xi + 2 * yi + ci
    sm8 = lax.dynamic_update_slice(jnp.zeros((8, SMALL_ROWS, 128), F32), _pack(small, _SMALL, SMALL_ROWS)[None], (me, 0, 0))
    sm_arr, sm_sems, tok = _split_call("small_start", {"sm": sm8}, start=_all8_plan("sm"))
    g_wc = _matmul(dproj, h, "tn", BF16, 1280, 1024, 2048, "in_proj_dw", dep=tok)
    rs_b = _ReduceScatter([0], {0: g_wc}, idx, "b")
    tok = rs_a.share(after=rs_b.token)
    tok = rs_b.chips(after=tok)
    dh = _matmul(dproj, wc, "nn", F32, 512, 1024, 5760, "in_proj_dx", dep=tok)
    grad_x, g_nm = _rms_bwd(dh, x0, norm_mix, dx1, "rms_mix_bwd")
    nm8 = lax.dynamic_update_slice(jnp.zeros((8, 8, 128), F32), g_nm[0].reshape(1, 8, 128), (me, 0, 0))
    nm_arr, nm_sems, tok = _split_call("norm_mix_start", {"nm": nm8}, start=_all8_plan("nm"))
    sm_arr, _, _ = _split_call("small_wait", sm_arr, wait=_all8_plan("sm"), wait_sems=sm_sems, after=tok)
    small_sum = _sum8(sm_arr["sm"])
    gs = _unpack(small_sum, _SMALL)
    red = rs_a.result(after=tok)
    big = {"w_mlp1": red[1], "w_mlp2": red[2], "w_branch_sc": red[3], "w_branch_ssm": red[4], "w_out": red[5]}

    given = dict(norm_mix=norm_mix, w_in=w_in, b_gate=b_gate, sc_conv_w=sc_conv_w, ssm_conv_w=ssm_conv_w, ssm_conv_b=ssm_conv_b, dt_bias=dt_bias, A_log=A_log, D_skip=D_skip, ssm_norm_w=ssm_norm_w, w_branch_sc=w_branch_sc, w_branch_ssm=w_branch_ssm, w_out=w_out, norm_mlp=norm_mlp, w_mlp1=w_mlp1, w_mlp2=w_mlp2, norm_final=norm_final,
                 m_norm_mix=m_norm_mix, m_w_in=m_w_in, m_b_gate=m_b_gate, m_sc_conv_w=m_sc_conv_w, m_ssm_conv_w=m_ssm_conv_w, m_ssm_conv_b=m_ssm_conv_b, m_dt_bias=m_dt_bias, m_A_log=m_A_log, m_D_skip=m_D_skip, m_ssm_norm_w=m_ssm_norm_w, m_w_branch_sc=m_w_branch_sc, m_w_branch_ssm=m_w_branch_ssm, m_w_out=m_w_out, m_norm_mlp=m_norm_mlp, m_w_mlp1=m_w_mlp1, m_w_mlp2=m_w_mlp2, m_norm_final=m_norm_final,
                 v_norm_mix=v_norm_mix, v_w_in=v_w_in, v_b_gate=v_b_gate, v_sc_conv_w=v_sc_conv_w, v_ssm_conv_w=v_ssm_conv_w, v_ssm_conv_b=v_ssm_conv_b, v_dt_bias=v_dt_bias, v_A_log=v_A_log, v_D_skip=v_D_skip, v_ssm_norm_w=v_ssm_norm_w, v_w_branch_sc=v_w_branch_sc, v_w_branch_ssm=v_w_branch_ssm, v_w_out=v_w_out, v_norm_mlp=v_norm_mlp, v_w_mlp1=v_w_mlp1, v_w_mlp2=v_w_mlp2, v_norm_final=v_norm_final)
    order = ["norm_mix", "w_in", "b_gate", "sc_conv_w", "ssm_conv_w", "ssm_conv_b", "dt_bias", "A_log", "D_skip",
             "ssm_norm_w", "w_branch_sc", "w_branch_ssm", "w_out", "norm_mlp", "w_mlp1", "w_mlp2", "norm_final"]
    grad, delta, new_m, new_v = {}, {}, {}, {}
    for n in big:
        delta[n], new_m[n], new_v[n], grad[n] = _adamw(given[n], big[n], given["m_" + n], given["v_" + n],
                                                       "adamw_" + n, copy_g=True)
    big["w_in"] = None
    grad_small = {n: gs[n].reshape(given[n].shape) for n in small_names if n not in ("sc_conv_w", "ssm_conv_w")}
    grad_small["sc_conv_w"] = lax.dynamic_slice(gs["sc_conv_w"].reshape(3, D), (0, 256 * s), (3, 256))
    grad_small["ssm_conv_w"] = lax.dynamic_slice(gs["ssm_conv_w"].reshape(4, XBC), (0, 1024 * s), (4, 1024))
    table = small_table
    ds_, ms_, vs_ = _adamw(pk_w, _pack(grad_small, table, small_rows), pk_m, pk_v, "adamw_small", tr=small_rows)
    ds_, ms_, vs_ = _unpack(ds_, table), _unpack(ms_, table), _unpack(vs_, table)
    for n in grad_small:
        shp = given[n].shape
        grad[n] = grad_small[n]
        delta[n], new_m[n], new_v[n] = ds_[n].reshape(shp), ms_[n].reshape(shp), vs_[n].reshape(shp)

    done = [new_v[n] for n in ("w_mlp1", "w_mlp2", "w_branch_sc", "w_branch_ssm", "w_out")] + [vs_["b_gate"]]
    tok = rs_b.share(after=done)
    offs = jnp.where(s == 3, jnp.array([24, -8, 744, 2072, -8], jnp.int32),
                     jnp.stack([8 * s, 8 * s, 0 * s, 8 * s, 8 * s]).astype(jnp.int32))
    offs = jnp.concatenate([offs, jnp.stack([7 * ci, 4 - 4 * ci]).astype(jnp.int32)])
    nmain = W_SHARD // 256
    wt_own = (w_in.T, rs_b.arr["f0"], m_w_in.T, v_w_in.T, offs)
    res = _adamw_w_in(*wt_own, "adamw_w_in_own", 0, 256, 4, (0, 1), blk_key=5)
    gp = rs_b.result(after=[tok, res[0]])[0]
    wt_args = (w_in.T, gp, m_w_in.T, v_w_in.T, offs)
    res = _adamw_w_in(*wt_args, "adamw_w_in", 0, 256, nmain - 4, (0, 1), into=res, blk_key=6)
    res = _adamw_w_in(*wt_args, "adamw_w_in_dt", 744, 32, 1, (3,), into=res)
    dt_, mt_, vt_, gwt = _adamw_w_in(*wt_args, "adamw_w_in_tail", 256 * nmain, 8, 1, (4,), into=res)
    grad["w_in"], delta["w_in"], new_m["w_in"], new_v["w_in"] = gwt.T, dt_.T, mt_.T, vt_.T
    nm_arr, _, _ = _split_call("norm_mix_wait", nm_arr, wait=_all8_plan("nm"), wait_sems=nm_sems, after=tok)
    g8 = _sum8(nm_arr["nm"], "norm_mix_sum")
    r8 = lambda a: a.reshape(8, 128)
    d8, m8, v8 = _adamw(r8(norm_mix), g8, r8(m_norm_mix), r8(v_norm_mix), "adamw_norm_mix", tr=8)
    grad["norm_mix"], delta["norm_mix"] = g8.reshape(D), d8.reshape(D)
    new_m["norm_mix"], new_v["norm_mix"] = m8.reshape(D), v8.reshape(D)

    loss = gs["loss"].reshape(())
    return (loss, grad_x.reshape(1, L, D), *[grad[n] for n in order], *[delta[n] for n in order],
            *[new_m[n] for n in order], *[new_v[n] for n in order])
```

```python
import functools

import jax
import jax.numpy as jnp
from jax import lax
from jax.experimental import pallas as pl
from jax.experimental.pallas import tpu as pltpu

F32 = jnp.float32
BF16 = jnp.bfloat16
MESH = pl.DeviceIdType.MESH
HBM = pltpu.HBM

D = 1024
INNER = 2048
HD = 64
NH = 32
NG = 8
NS = 128
Q = 128
GPS = 8
XBC = 4096
DFF = 4096
EPS = 1e-6
W_SHARD = 2824
NCW = 11520
PIECE = 3072
PMAIN = 2816
C_Z, C_XBC, C_GATE, C_DT = 3072, 5120, 9216, 11264
SMALL_ROWS = 256
VMEM_LIMIT = 56 * 1024 * 1024

ADAM_LR, ADAM_B1, ADAM_B2, ADAM_EPS, ADAM_WD, ADAM_STEP = 0.001, 0.9, 0.999, 1e-08, 0.01, 10


def _cp(sem=None, vmem=VMEM_LIMIT):
    return pltpu.CompilerParams(dimension_semantics=sem, vmem_limit_bytes=vmem)


def _sigmoid(v):
    return 1.0 / (1.0 + jnp.exp(-v))


_DIMS = {"nn": (((1,), (0,)), ((), ())), "nt": (((1,), (1,)), ((), ())), "tn": (((0,), (0,)), ((), ()))}


def _matmul(a, b, mode, out_dtype, tm, tn, tk, name, epi=None, extra=None, n_outer=False, dep=None):
    if mode == "tn":
        K, M = a.shape
    else:
        M, K = a.shape
    N = b.shape[0] if mode == "nt" else b.shape[1]
    tm, tn, tk = min(tm, M), min(tn, N), min(tk, K)
    assert M % tm == 0 and N % tn == 0 and K % tk == 0, (name, M, N, K, tm, tn, tk)
    nm, nn, nk = M // tm, N // tn, K // tk
    dims = _DIMS[mode]

    def ij(p0, p1):
        return (p1, p0) if n_outer else (p0, p1)

    if mode == "tn":
        a_spec = pl.BlockSpec((tk, tm), lambda p0, p1, k: (k, ij(p0, p1)[0]))
    else:
        a_spec = pl.BlockSpec((tm, tk), lambda p0, p1, k: (ij(p0, p1)[0], k))
    if mode == "nt":
        b_spec = pl.BlockSpec((tn, tk), lambda p0, p1, k: (ij(p0, p1)[1], k))
    else:
        b_spec = pl.BlockSpec((tk, tn), lambda p0, p1, k: (k, ij(p0, p1)[1]))
    o_spec = pl.BlockSpec((tm, tn), lambda p0, p1, k: ij(p0, p1))
    in_specs = [a_spec, b_spec]
    args = [a, b]
    if epi in ("res", "drelu"):
        in_specs.append(o_spec)
        args.append(extra)
    if dep is not None:
        in_specs.append(pl.BlockSpec(memory_space=pl.ANY))
        args.append(dep)
    n_in = len(args)
    if epi == "relu2":
        out_shape = (jax.ShapeDtypeStruct((M, N), out_dtype), jax.ShapeDtypeStruct((M, N), BF16))
        out_specs = (o_spec, o_spec)
    else:
        out_shape = jax.ShapeDtypeStruct((M, N), out_dtype)
        out_specs = o_spec

    def kern(*refs):
        a_ref, b_ref = refs[0], refs[1]
        e_ref = refs[2] if epi in ("res", "drelu") else None
        acc = refs[-1]
        outs = refs[n_in:-1] if nk > 1 else refs[n_in:]
        k = pl.program_id(2)

        def product():
            return lax.dot_general(a_ref[...].astype(BF16), b_ref[...].astype(BF16), dims, preferred_element_type=F32)

        def finish(r):
            if epi is None:
                outs[0][...] = r.astype(out_dtype)
            elif epi == "res":
                outs[0][...] = (r + e_ref[...]).astype(out_dtype)
            elif epi == "relu2":
                outs[0][...] = r.astype(out_dtype)
                t = jnp.maximum(r, 0.0)
                outs[1][...] = (t * t).astype(BF16)
            else:
                outs[0][...] = (r * (2.0 * jnp.maximum(e_ref[...].astype(F32), 0.0))).astype(out_dtype)

        if nk == 1:
            finish(product())
        else:
            @pl.when(k == 0)
            def _():
                acc[...] = jnp.zeros_like(acc)

            acc[...] += product()

            @pl.when(k == nk - 1)
            def _():
                finish(acc[...])

    grid = (nn, nm, nk) if n_outer else (nm, nn, nk)
    return pl.pallas_call(
        kern, grid=grid, in_specs=in_specs, out_specs=out_specs, out_shape=out_shape,
        scratch_shapes=[pltpu.VMEM((tm, tn), F32)] if nk > 1 else [], name=name,
        compiler_params=_cp(("parallel", "parallel", "arbitrary")),
    )(*args)


def _matmul_res_rms(a, b, x, w, tm, name):
    M, K = a.shape
    tm = min(tm, M)

    def kern(a_ref, b_ref, x_ref, w_ref, x1_ref, h_ref):
        x1 = x_ref[...] + lax.dot_general(a_ref[...].astype(BF16), b_ref[...].astype(BF16), _DIMS["nn"],
                                          preferred_element_type=F32)
        x1_ref[...] = x1
        r = lax.rsqrt(jnp.mean(x1 * x1, axis=-1, keepdims=True) + EPS)
        h_ref[...] = ((x1 * r) * w_ref[...]).astype(BF16)

    row = pl.BlockSpec((tm, D), lambda i: (i, 0))
    return pl.pallas_call(
        kern, grid=(M // tm,),
        in_specs=[pl.BlockSpec((tm, K), lambda i: (i, 0)), pl.BlockSpec((K, D), lambda i: (0, 0)), row,
                  pl.BlockSpec((1, D), lambda i: (0, 0))],
        out_specs=(row, row), out_shape=(jax.ShapeDtypeStruct((M, D), F32), jax.ShapeDtypeStruct((M, D), BF16)),
        name=name, compiler_params=_cp(("parallel",)),
    )(a, b, x, w.reshape(1, D))


def _matmul_rms_bwd(a, b, mode, x, w, res, tm, tk, name, dep=None):
    M, K = a.shape
    tm, tk = min(tm, M), min(tk, K)
    nk = K // tk
    assert M % tm == 0 and K % tk == 0
    b_spec = (pl.BlockSpec((tk, D), lambda i, k: (k, 0)) if mode == "nn" else pl.BlockSpec((D, tk), lambda i, k: (0, k)))
    row = pl.BlockSpec((tm, D), lambda i, k: (i, 0))
    deps = [] if dep is None else [dep]

    def kern(a_ref, b_ref, x_ref, w_ref, res_ref, *rest):
        dx_ref, gw_ref, acc = rest[-3:]
        i, k = pl.program_id(0), pl.program_id(1)

        @pl.when((i == 0) & (k == 0))
        def _():
            gw_ref[...] = jnp.zeros_like(gw_ref)

        def product():
            return lax.dot_general(a_ref[...].astype(BF16), b_ref[...].astype(BF16), _DIMS[mode],
                                   preferred_element_type=F32)

        def finish(dyv):
            xv = x_ref[...]
            r = lax.rsqrt(jnp.mean(xv * xv, axis=-1, keepdims=True) + EPS)
            xn = xv * r
            gw_ref[...] += jnp.broadcast_to(jnp.sum(dyv * xn, axis=0, keepdims=True), (8, D))
            dxn = dyv * w_ref[...]
            dx_ref[...] = res_ref[...] + r * (dxn - xn * jnp.mean(dxn * xn, axis=-1, keepdims=True))

        if nk == 1:
            finish(product())
        else:
            @pl.when(k == 0)
            def _():
                acc[...] = jnp.zeros_like(acc)

            acc[...] += product()

            @pl.when(k == nk - 1)
            def _():
                finish(acc[...])

    return pl.pallas_call(
        kern, grid=(M // tm, nk),
        in_specs=[pl.BlockSpec((tm, tk), lambda i, k: (i, k)), b_spec, row, pl.BlockSpec((1, D), lambda i, k: (0, 0)),
                  row] + [pl.BlockSpec(memory_space=pl.ANY)] * len(deps),
        out_specs=(row, pl.BlockSpec((8, D), lambda i, k: (0, 0))),
        out_shape=(jax.ShapeDtypeStruct((M, D), F32), jax.ShapeDtypeStruct((8, D), F32)),
        scratch_shapes=[pltpu.VMEM((tm, D), F32)], name=name, compiler_params=_cp(("arbitrary", "arbitrary")),
    )(a, b, x, w.reshape(1, D), res, *deps)


def _rms_fwd(x, w, name, tl=256, dep=None):
    L = x.shape[0]

    def kern(x_ref, w_ref, *rest):
        o_ref = rest[-1]
        xv = x_ref[...]
        r = lax.rsqrt(jnp.mean(xv * xv, axis=-1, keepdims=True) + EPS)
        o_ref[...] = ((xv * r) * w_ref[...]).astype(BF16)

    row = pl.BlockSpec((tl, D), lambda i: (i, 0))
    deps = [] if dep is None else [dep]
    return pl.pallas_call(
        kern, grid=(L // tl,),
        in_specs=[row, pl.BlockSpec((1, D), lambda i: (0, 0))] + [pl.BlockSpec(memory_space=pl.ANY)] * len(deps),
        out_specs=row, out_shape=jax.ShapeDtypeStruct((L, D), BF16), name=name, compiler_params=_cp(("parallel",)),
    )(x, w.reshape(1, D), *deps)


def _rms_bwd(dy, x, w, res, name, tl=256, dep=None):
    L = x.shape[0]
    deps = [] if dep is None else [dep]

    def kern(dy_ref, x_ref, w_ref, res_ref, *rest):
        dx_ref, gw_ref = rest[-2:]
        @pl.when(pl.program_id(0) == 0)
        def _():
            gw_ref[...] = jnp.zeros_like(gw_ref)

        xv = x_ref[...]
        dyv = dy_ref[...]
        r = lax.rsqrt(jnp.mean(xv * xv, axis=-1, keepdims=True) + EPS)
        xn = xv * r
        gw_ref[...] += jnp.broadcast_to(jnp.sum(dyv * xn, axis=0, keepdims=True), (8, D))
        dxn = dyv * w_ref[...]
        dx_ref[...] = res_ref[...] + r * (dxn - xn * jnp.mean(dxn * xn, axis=-1, keepdims=True))

    row = pl.BlockSpec((tl, D), lambda i: (i, 0))
    return pl.pallas_call(
        kern, grid=(L // tl,),
        in_specs=[row, row, pl.BlockSpec((1, D), lambda i: (0, 0)), row] + [pl.BlockSpec(memory_space=pl.ANY)] * len(deps),
        out_specs=(row, pl.BlockSpec((8, D), lambda i: (0, 0))),
        out_shape=(jax.ShapeDtypeStruct((L, D), F32), jax.ShapeDtypeStruct((8, D), F32)),
        name=name, compiler_params=_cp(("arbitrary",)),
    )(dy, x, w.reshape(1, D), res, *deps)


def _final(x2, w, tgt, tl=256):
    L = x2.shape[0]

    def kern(x_ref, w_ref, t_ref, dx_ref, gw_ref, loss_ref):
        @pl.when(pl.program_id(0) == 0)
        def _():
            gw_ref[...] = jnp.zeros_like(gw_ref)
            loss_ref[...] = jnp.zeros_like(loss_ref)

        xv = x_ref[...]
        r = lax.rsqrt(jnp.mean(xv * xv, axis=-1, keepdims=True) + EPS)
        xn = xv * r
        e = xn * w_ref[...] - t_ref[...]
        per_tok = jnp.mean(e * e, axis=-1, keepdims=True)
        loss_ref[...] += 0.5 * jnp.sum(per_tok)
        dyv = e * (1.0 / D)
        gw_ref[...] += jnp.broadcast_to(jnp.sum(dyv * xn, axis=0, keepdims=True), (8, D))
        dxn = dyv * w_ref[...]
        dx_ref[...] = r * (dxn - xn * jnp.mean(dxn * xn, axis=-1, keepdims=True))

    row = pl.BlockSpec((tl, D), lambda i: (i, 0))
    return pl.pallas_call(
        kern, grid=(L // tl,), in_specs=[row, pl.BlockSpec((1, D), lambda i: (0, 0)), row],
        out_specs=(row, pl.BlockSpec((8, D), lambda i: (0, 0)), pl.BlockSpec((8, 128), lambda i: (0, 0))),
        out_shape=(jax.ShapeDtypeStruct((L, D), F32), jax.ShapeDtypeStruct((8, D), F32),
                   jax.ShapeDtypeStruct((8, 128), F32)),
        name="final_norm_loss", compiler_params=_cp(("arbitrary",)),
    )(x2, w.reshape(1, D), tgt)


def _down(v, k):
    if k == 0:
        return v
    t = lax.broadcasted_iota(jnp.int32, v.shape, 0)
    return jnp.where(t >= k, pltpu.roll(v, k, axis=0), 0.0)


def _up(v, k):
    if k == 0:
        return v
    n = v.shape[0]
    t = lax.broadcasted_iota(jnp.int32, v.shape, 0)
    return jnp.where(t < n - k, pltpu.roll(v, n - k, axis=0), 0.0)


TW = 256


def _sc_fwd(proj, cw):
    L = proj.shape[0]
    nb = D // TW

    def kern(b_ref, c_ref, x_ref, w_ref, o_ref):
        u = c_ref[...].astype(F32) * x_ref[...].astype(F32)
        w = w_ref[...]
        cv = w[0:1] * _down(u, 2) + w[1:2] * _down(u, 1) + w[2:3] * u
        o_ref[...] = (b_ref[...].astype(F32) * cv).astype(BF16)

    col = lambda off: pl.BlockSpec((L, TW), lambda j: (0, off + j))
    return pl.pallas_call(
        kern, grid=(nb,), in_specs=[col(0), col(nb), col(2 * nb), pl.BlockSpec((8, TW), lambda j: (0, j))],
        out_specs=pl.BlockSpec((L, TW), lambda j: (0, j)), out_shape=jax.ShapeDtypeStruct((L, D), BF16),
        name="sc_fwd", compiler_params=_cp(("parallel",)),
    )(proj, proj, proj, cw)


def _sc_bwd(dya, proj, cw, dproj):
    L = proj.shape[0]
    nb = D // TW

    def kern(d_ref, b_ref, c_ref, x_ref, w_ref, _, dp_ref, gw_ref, keep):
        sec = pl.program_id(1)

        @pl.when(sec == 0)
        def _():
            cs, xs, dyv = c_ref[...].astype(F32), x_ref[...].astype(F32), d_ref[...]
            w = w_ref[...]
            u = cs * xs
            u1, u2 = _down(u, 1), _down(u, 2)
            cv = w[0:1] * u2 + w[1:2] * u1 + w[2:3] * u
            dcv = dyv * b_ref[...].astype(F32)
            du = w[2:3] * dcv + w[1:2] * _up(dcv, 1) + w[0:1] * _up(dcv, 2)
            g0 = jnp.sum(dcv * u2, axis=0, keepdims=True)
            g1 = jnp.sum(dcv * u1, axis=0, keepdims=True)
            g2 = jnp.sum(dcv * u, axis=0, keepdims=True)
            row = lax.broadcasted_iota(jnp.int32, (8, TW), 0)
            gw_ref[...] = jnp.where(row == 0, g0, jnp.where(row == 1, g1, jnp.where(row == 2, g2, 0.0)))
            dp_ref[...] = (dyv * cv).astype(BF16)
            keep[0] = (du * xs).astype(BF16)
            keep[1] = (du * cs).astype(BF16)

        @pl.when(sec > 0)
        def _():
            dp_ref[...] = keep[sec - 1]

    col = lambda off: pl.BlockSpec((L, TW), lambda j, s: (0, off + j))
    return pl.pallas_call(
        kern, grid=(nb, 3),
        in_specs=[col(0), col(0), col(nb), col(2 * nb), pl.BlockSpec((8, TW), lambda j, s: (0, j)),
                  pl.BlockSpec(memory_space=pl.ANY)],
        out_specs=(pl.BlockSpec((L, TW), lambda j, s: (0, s * nb + j)), pl.BlockSpec((8, TW), lambda j, s: (0, j))),
        out_shape=(jax.ShapeDtypeStruct(dproj.shape, BF16), jax.ShapeDtypeStruct((8, D), F32)),
        scratch_shapes=[pltpu.VMEM((2, L, TW), BF16)],
        input_output_aliases={5: 0}, name="sc_bwd", compiler_params=_cp(("parallel", "arbitrary")),
    )(dya, proj, proj, proj, cw, dproj)


def _ssm_conv_fwd(proj, cw4):
    L = proj.shape[0]
    off = C_XBC // TW

    def kern(r_ref, w_ref, o_ref):
        raw = r_ref[...].astype(F32)
        w = w_ref[...]
        c4 = w[0:1] * _down(raw, 3) + w[1:2] * _down(raw, 2) + w[2:3] * _down(raw, 1) + w[3:4] * raw + w[4:5]
        o_ref[...] = c4 * _sigmoid(c4)

    return pl.pallas_call(
        kern, grid=(XBC // TW,),
        in_specs=[pl.BlockSpec((L, TW), lambda j: (0, off + j)), pl.BlockSpec((8, TW), lambda j: (0, j))],
        out_specs=pl.BlockSpec((L, TW), lambda j: (0, j)), out_shape=jax.ShapeDtypeStruct((L, XBC), F32),
        name="ssm_conv_fwd", compiler_params=_cp(("parallel",)),
    )(proj, cw4)


def _ssm_conv_bwd(dx, proj, cw4, dproj, col0, name):
    L, width = dx.shape
    off_p = (C_XBC + col0) // TW
    off_w = col0 // TW

    def kern(d_ref, r_ref, w_ref, _, dp_ref, gw_ref):
        raw = r_ref[...].astype(F32)
        w = w_ref[...]
        r1, r2, r3 = _down(raw, 1), _down(raw, 2), _down(raw, 3)
        c4 = w[0:1] * r3 + w[1:2] * r2 + w[2:3] * r1 + w[3:4] * raw + w[4:5]
        sg = _sigmoid(c4)
        dc4 = d_ref[...] * (sg * (1.0 + c4 * (1.0 - sg)))
        draw = w[3:4] * dc4 + w[2:3] * _up(dc4, 1) + w[1:2] * _up(dc4, 2) + w[0:1] * _up(dc4, 3)
        dp_ref[...] = draw.astype(BF16)
        gs = [jnp.sum(dc4 * r3, axis=0, keepdims=True), jnp.sum(dc4 * r2, axis=0, keepdims=True),
              jnp.sum(dc4 * r1, axis=0, keepdims=True), jnp.sum(dc4 * raw, axis=0, keepdims=True),
              jnp.sum(dc4, axis=0, keepdims=True)]
        row = lax.broadcasted_iota(jnp.int32, (8, TW), 0)
        acc = jnp.zeros((8, TW), F32)
        for k, gk in enumerate(gs):
            acc = jnp.where(row == k, gk, acc)
        gw_ref[...] = acc

    return pl.pallas_call(
        kern, grid=(width // TW,),
        in_specs=[pl.BlockSpec((L, TW), lambda j: (0, j)), pl.BlockSpec((L, TW), lambda j: (0, off_p + j)),
                  pl.BlockSpec((8, TW), lambda j: (0, off_w + j)), pl.BlockSpec(memory_space=pl.ANY)],
        out_specs=(pl.BlockSpec((L, TW), lambda j: (0, off_p + j)), pl.BlockSpec((8, TW), lambda j: (0, j))),
        out_shape=(jax.ShapeDtypeStruct(dproj.shape, BF16), jax.ShapeDtypeStruct((8, width), F32)),
        input_output_aliases={3: 0}, name=name, compiler_params=_cp(("arbitrary",)),
    )(dx, proj, cw4, dproj)


def _split3(v):
    h1 = v.astype(BF16)
    r1 = v - h1.astype(F32)
    h2 = r1.astype(BF16)
    h3 = (r1 - h2.astype(F32)).astype(BF16)
    return h1, h2, h3


def _dot01(m01, v, dims=_DIMS["nn"], m_left=True, terms=3):
    out = None
    for part in _split3(v)[:terms]:
        ops = (m01, part) if m_left else (part, m01)
        t = lax.dot_general(ops[0], ops[1], dims, preferred_element_type=F32)
        out = t if out is None else out + t
    return out


def _bdot(a, b, mode="nn"):
    return lax.dot_general(a.astype(BF16), b.astype(BF16), _DIMS[mode], preferred_element_type=F32)


def _softplus(v):
    return jnp.maximum(v, 0.0) + jnp.log1p(jnp.exp(-jnp.abs(v)))


def _dt_prep(proj, vec):
    L = proj.shape[0]

    def kern(p_ref, v_ref, dt_ref, cs_ref, sg_ref):
        v = v_ref[...]
        pre = p_ref[:, 0:128] + v[0:1]
        dt = _softplus(pre)
        da = dt * (-jnp.exp(v[1:2]))
        ii = lax.broadcasted_iota(jnp.int32, (Q, Q), 0)
        jj = lax.broadcasted_iota(jnp.int32, (Q, Q), 1)
        ltri = (jj <= ii).astype(BF16)
        lane = lax.broadcasted_iota(jnp.int32, (Q, 128), 1)
        for val, ref in ((dt, dt_ref), (_dot01(ltri, da), cs_ref), (_sigmoid(pre), sg_ref)):
            for g in range(NG):
                moved = val if g == 0 else pltpu.roll(val, 128 - 4 * g, axis=1)
                ref[g] = jnp.where(lane < 4, moved, 0.0)

    blk = pl.BlockSpec((NG, Q, 128), lambda c: (0, c, 0))
    return pl.pallas_call(
        kern, grid=(L // Q,),
        in_specs=[pl.BlockSpec((Q, 256), lambda c: (c, 0)), pl.BlockSpec((8, 128), lambda c: (0, 0))],
        out_specs=(blk, blk, blk),
        out_shape=(jax.ShapeDtypeStruct((NG, L, 128), F32),) * 3,
        name="dt_prep", compiler_params=_cp(("parallel",)),
    )(proj, vec)


def _head_masks():
    lane = lax.broadcasted_iota(jnp.int32, (1, 4 * HD), 1)
    return [((lane >= HD * j) & (lane < HD * (j + 1))) for j in range(4)]


def _expand4(v4, masks):
    R = v4.shape[0]
    out = jnp.zeros((R, 4 * HD), F32)
    for j in range(4):
        out = jnp.where(masks[j], jnp.broadcast_to(v4[:, j:j + 1], (R, 4 * HD)), out)
    return out


def _decay_matrix(cs_col, tri):
    colb = jnp.broadcast_to(cs_col, (Q, Q))
    return jnp.exp(jnp.where(tri, colb - colb.T, -jnp.inf))


def _ssd_fwd(xbc, dt4, cs4, vecg):
    L = xbc.shape[0]
    nc = L // Q

    def kern(x_ref, b_ref, c_ref, dt_ref, cs_ref, v_ref, y_ref, s_ref, S):
        c = pl.program_id(1)

        @pl.when(c == 0)
        def _():
            S[...] = jnp.zeros_like(S)

        masks = _head_masks()
        ii = lax.broadcasted_iota(jnp.int32, (Q, Q), 0)
        jj = lax.broadcasted_iota(jnp.int32, (Q, Q), 1)
        tri = jj <= ii
        for gi in range(GPS):
            xs, ns = slice(256 * gi, 256 * (gi + 1)), slice(NS * gi, NS * (gi + 1))
            dt4v, cs4v = dt_ref[gi], cs_ref[gi]
            dt_b, cs_b = _expand4(dt4v, masks), _expand4(cs4v, masks)
            d_b = _expand4(v_ref[gi], masks)[1:2]
            cs_last = cs_b[Q - 1:Q, :]
            x4, bm, cm = x_ref[:, xs], b_ref[:, ns], c_ref[:, ns]
            xdt = x4 * dt_b
            gm = _bdot(cm, bm, "nt")
            s4 = S[gi]
            s_ref[gi, 0] = s4
            y = _bdot(cm, s4) * jnp.exp(cs_b) + d_b * x4
            m_all = jnp.concatenate([(gm * _decay_matrix(cs4v[:, j:j + 1], tri)).astype(BF16) for j in range(4)], axis=0)
            yd = _bdot(m_all, xdt)
            for j in range(4):
                y = y + jnp.where(masks[j], yd[Q * j:Q * (j + 1)], 0.0)
            y_ref[:, xs] = y
            S[gi] = jnp.exp(cs_last) * s4 + _bdot(bm, xdt * jnp.exp(cs_last - cs_b), "tn")

    sc = pl.BlockSpec((GPS, Q, 128), lambda g, c: (g, c, 0))
    bw = NS * GPS
    return pl.pallas_call(
        kern, grid=(NG // GPS, nc),
        in_specs=[pl.BlockSpec((Q, 256 * GPS), lambda g, c: (c, g)),
                  pl.BlockSpec((Q, bw), lambda g, c: (c, INNER // bw + g)),
                  pl.BlockSpec((Q, bw), lambda g, c: (c, (INNER + NG * NS) // bw + g)),
                  sc, sc, pl.BlockSpec((GPS, 8, 128), lambda g, c: (g, 0, 0))],
        out_specs=(pl.BlockSpec((Q, 256 * GPS), lambda g, c: (c, g)),
                   pl.BlockSpec((GPS, 1, NS, 256), lambda g, c: (g, c, 0, 0))),
        out_shape=(jax.ShapeDtypeStruct((L, INNER), F32), jax.ShapeDtypeStruct((NG, nc, NS, 256), F32)),
        scratch_shapes=[pltpu.VMEM((GPS, NS, 256), F32)], name="ssd_fwd",
        compiler_params=_cp(("parallel", "arbitrary")),
    )(xbc, xbc, xbc, dt4, cs4, vecg)


def _ssd_bwd(xbc, dt4, cs4, sg4, vecg, s_all, dy):
    L = xbc.shape[0]
    nc = L // Q

    def kern(x_ref, b_ref, c_ref, dt_ref, cs_ref, sg_ref, v_ref, s_ref, dy_ref,
             dx_ref, db_ref, dc_ref, ddt_ref, st_ref, dS):
        cc = pl.program_id(1)

        @pl.when(cc == 0)
        def _():
            dS[...] = jnp.zeros_like(dS)
            st_ref[...] = jnp.zeros_like(st_ref)

        masks = _head_masks()
        ii = lax.broadcasted_iota(jnp.int32, (Q, Q), 0)
        jj = lax.broadcasted_iota(jnp.int32, (Q, Q), 1)
        tri = jj <= ii
        utri = (jj >= ii).astype(BF16)
        hsel = ((lax.broadcasted_iota(jnp.int32, (4 * HD, 128), 0) // HD)
                == lax.broadcasted_iota(jnp.int32, (4 * HD, 128), 1)).astype(BF16)
        hrow = ((lax.broadcasted_iota(jnp.int32, (4 * Q, 128), 0) // Q)
                == lax.broadcasted_iota(jnp.int32, (4 * Q, 128), 1)).astype(BF16)
        ones_q = jnp.ones((Q, 128), BF16)
        lane128 = lax.broadcasted_iota(jnp.int32, (Q, 128), 1)

        for gi in range(GPS):
            xs, ns = slice(256 * gi, 256 * (gi + 1)), slice(NS * gi, NS * (gi + 1))
            dt4v, cs4v, sg4v = dt_ref[gi], cs_ref[gi], sg_ref[gi]
            dt_b, cs_b = _expand4(dt4v, masks), _expand4(cs4v, masks)
            vv = _expand4(v_ref[gi], masks)
            a_b = -jnp.exp(vv[0:1])
            d_b = vv[1:2]
            a4 = -jnp.exp(v_ref[gi][0:1, :])
            cs_last = cs_b[Q - 1:Q, :]
            ecs = jnp.exp(cs_b)
            decay = jnp.exp(cs_last - cs_b)
            elast = jnp.exp(cs_last)
            x4, bm, cm, dyv = x_ref[:, xs], b_ref[:, ns], c_ref[:, ns], dy_ref[:, xs]
            s4 = s_ref[gi, 0]
            dsn = dS[gi]
            xdt = x4 * dt_b
            gm = _bdot(cm, bm, "nt")
            dye = dyv * ecs
            yoff = ecs * _bdot(cm, s4)
            t4 = _bdot(bm, dsn) * decay
            lms, mhs = [], []
            for j in range(4):
                colb = jnp.broadcast_to(cs4v[:, j:j + 1], (Q, Q))
                lms.append(jnp.exp(jnp.where(tri, colb - colb.T, -jnp.inf)))
                mhs.append(gm * lms[j])
            m_all = jnp.concatenate([m.astype(BF16) for m in mhs], axis=0)
            dy_m = jnp.concatenate([jnp.where(masks[j], dyv, 0.0).astype(BF16) for j in range(4)], axis=0)
            dxdt = t4 + _bdot(m_all, dy_m, "tn")
            dm_all = _bdot(dy_m, xdt, "nt")
            dg = jnp.zeros((Q, Q), F32)
            for j in range(4):
                dg = dg + dm_all[Q * j:Q * (j + 1)] * lms[j]
            e_all = dm_all * jnp.concatenate(mhs, axis=0)
            rsum = _dot01(ones_q, e_all, m_left=False, terms=2)
            da4 = -_dot01(hrow, e_all, _DIMS["tn"], m_left=False, terms=2)
            for j in range(4):
                da4 = da4 + jnp.where(lane128 == j, rsum[Q * j:Q * (j + 1)], 0.0)
            xt = xdt * t4
            tail = jnp.sum(xt, axis=0, keepdims=True) + elast * jnp.sum(s4 * dsn, axis=0, keepdims=True)
            gd_raw = jnp.sum(dyv * x4, axis=0, keepdims=True)
            stacked = jnp.concatenate([dyv * yoff - xt, dxdt * x4, jnp.broadcast_to(tail, (8, 4 * HD)),
                                       jnp.broadcast_to(gd_raw, (8, 4 * HD))], axis=0)
            seg = _dot01(hsel, stacked, m_left=False, terms=2)
            dda4 = _dot01(utri, da4 + seg[0:Q], terms=2) + seg[2 * Q:2 * Q + 1]
            ddt_ref[gi] = (dda4 * a4 + seg[Q:2 * Q]) * sg4v
            ga = jnp.sum(dda4 * dt4v * a4, axis=0, keepdims=True)
            row = lax.broadcasted_iota(jnp.int32, (8, 128), 0)
            st_ref[gi] += jnp.where(row == 0, ga, jnp.where(row == 1, seg[2 * Q + 8:2 * Q + 9], 0.0))
            dx_ref[:, xs] = d_b * dyv + dxdt * dt_b
            dc_ref[:, ns] = _bdot(dg, bm) + _bdot(dye, s4, "nt")
            db_ref[:, ns] = _bdot(dg, cm, "tn") + _bdot(xdt * decay, dsn, "nt")
            dS[gi] = elast * dsn + _bdot(cm, dye, "tn")

    rv = lambda c: nc - 1 - c
    sc = pl.BlockSpec((GPS, Q, 128), lambda g, c: (g, rv(c), 0))
    bw = NS * GPS
    return pl.pallas_call(
        kern, grid=(NG // GPS, nc),
        in_specs=[pl.BlockSpec((Q, 256 * GPS), lambda g, c: (rv(c), g)),
                  pl.BlockSpec((Q, bw), lambda g, c: (rv(c), INNER // bw + g)),
                  pl.BlockSpec((Q, bw), lambda g, c: (rv(c), (INNER + NG * NS) // bw + g)),
                  sc, sc, sc, pl.BlockSpec((GPS, 8, 128), lambda g, c: (g, 0, 0)),
                  pl.BlockSpec((GPS, 1, NS, 256), lambda g, c: (g, rv(c), 0, 0)),
                  pl.BlockSpec((Q, 256 * GPS), lambda g, c: (rv(c), g))],
        out_specs=(pl.BlockSpec((Q, 256 * GPS), lambda g, c: (rv(c), g)),
                   pl.BlockSpec((Q, bw), lambda g, c: (rv(c), g)),
                   pl.BlockSpec((Q, bw), lambda g, c: (rv(c), g)),
                   pl.BlockSpec((GPS, Q, 128), lambda g, c: (g, rv(c), 0)),
                   pl.BlockSpec((GPS, 8, 128), lambda g, c: (g, 0, 0))),
        out_shape=(jax.ShapeDtypeStruct((L, INNER), F32), jax.ShapeDtypeStruct((L, NG * NS), F32),
                   jax.ShapeDtypeStruct((L, NG * NS), F32), jax.ShapeDtypeStruct((NG, L, 128), F32),
                   jax.ShapeDtypeStruct((NG, 8, 128), F32)),
        scratch_shapes=[pltpu.VMEM((GPS, NS, 256), F32)], name="ssd_bwd",
        compiler_params=_cp(("parallel", "arbitrary")),
    )(xbc, xbc, xbc, dt4, cs4, sg4, vecg, s_all, dy)


def _dt_bwd(ddt, dproj, tl=256):
    L = ddt.shape[1]

    def kern(d_ref, _, dp_ref, gs_ref):
        @pl.when(pl.program_id(0) == 0)
        def _():
            gs_ref[...] = jnp.zeros_like(gs_ref)

        d = d_ref[0]
        for g in range(1, NG):
            d = d + pltpu.roll(d_ref[g], 4 * g, axis=1)
        gs_ref[...] += jnp.broadcast_to(jnp.sum(d, axis=0, keepdims=True), (8, 128))
        dp_ref[...] = jnp.concatenate([d, jnp.zeros_like(d)], axis=1).astype(BF16)

    return pl.pallas_call(
        kern, grid=(L // tl,),
        in_specs=[pl.BlockSpec((NG, tl, 128), lambda i: (0, i, 0)), pl.BlockSpec(memory_space=pl.ANY)],
        out_specs=(pl.BlockSpec((tl, 256), lambda i: (i, C_DT // 256)), pl.BlockSpec((8, 128), lambda i: (0, 0))),
        out_shape=(jax.ShapeDtypeStruct(dproj.shape, BF16), jax.ShapeDtypeStruct((8, 128), F32)),
        input_output_aliases={1: 0}, name="dt_bwd", compiler_params=_cp(("arbitrary",)),
    )(ddt, dproj)


GW = INNER // NG


def _gnorm_fwd(y, proj, w, tl=256):
    L = y.shape[0]
    zoff = C_Z // 1024

    def kern(y_ref, z_ref, w_ref, o_ref):
        z = z_ref[...].astype(F32)
        yz = y_ref[...] * (z * _sigmoid(z))
        wv = w_ref[...]
        for k in range(1024 // GW):
            sl = slice(GW * k, GW * (k + 1))
            v = yz[:, sl]
            rg = lax.rsqrt(jnp.mean(v * v, axis=-1, keepdims=True) + EPS)
            o_ref[:, sl] = ((v * rg) * wv[:, sl]).astype(BF16)

    blk = pl.BlockSpec((tl, 1024), lambda i, j: (i, j))
    return pl.pallas_call(
        kern, grid=(L // tl, 2),
        in_specs=[blk, pl.BlockSpec((tl, 1024), lambda i, j: (i, zoff + j)), pl.BlockSpec((1, 1024), lambda i, j: (0, j))],
        out_specs=blk, out_shape=jax.ShapeDtypeStruct((L, INNER), BF16), name="gnorm_fwd",
        compiler_params=_cp(("parallel", "parallel")),
    )(y, proj, w.reshape(1, INNER))


def _gnorm_bwd(dyb, y, proj, w, dproj, tl=256):
    L = y.shape[0]
    zoff = C_Z // 1024

    def kern(d_ref, y_ref, z_ref, w_ref, _, dy_ref, dp_ref, gw_ref):
        @pl.when(pl.program_id(1) == 0)
        def _():
            gw_ref[...] = jnp.zeros_like(gw_ref)

        z = z_ref[...].astype(F32)
        sg = _sigmoid(z)
        sz = z * sg
        yv = y_ref[...]
        yz = yv * sz
        dv = d_ref[...]
        wv = w_ref[...]
        for k in range(1024 // GW):
            sl = slice(GW * k, GW * (k + 1))
            v = yz[:, sl]
            rg = lax.rsqrt(jnp.mean(v * v, axis=-1, keepdims=True) + EPS)
            vn = v * rg
            dk = dv[:, sl]
            gw_ref[:, sl] += jnp.broadcast_to(jnp.sum(dk * vn, axis=0, keepdims=True), (8, GW))
            dvn = dk * wv[:, sl]
            dyz = rg * (dvn - vn * jnp.mean(dvn * vn, axis=-1, keepdims=True))
            dy_ref[:, sl] = dyz * sz[:, sl]
            dp_ref[:, sl] = (dyz * yv[:, sl] * (sg[:, sl] * (1.0 + z[:, sl] * (1.0 - sg[:, sl])))).astype(BF16)

    blk = pl.BlockSpec((tl, 1024), lambda j, i: (i, j))
    zblk = pl.BlockSpec((tl, 1024), lambda j, i: (i, zoff + j))
    return pl.pallas_call(
        kern, grid=(2, L // tl),
        in_specs=[blk, blk, zblk, pl.BlockSpec((1, 1024), lambda j, i: (0, j)), pl.BlockSpec(memory_space=pl.ANY)],
        out_specs=(blk, zblk, pl.BlockSpec((8, 1024), lambda j, i: (0, j))),
        out_shape=(jax.ShapeDtypeStruct((L, INNER), F32), jax.ShapeDtypeStruct(dproj.shape, BF16),
                   jax.ShapeDtypeStruct((8, INNER), F32)),
        input_output_aliases={4: 1}, name="gnorm_bwd", compiler_params=_cp(("parallel", "arbitrary")),
    )(dyb, y, proj, w.reshape(1, INNER), dproj)


def _merge_fwd(proj, bg, br_a, br_b, tl=256):
    L = proj.shape[0]
    goff = C_GATE // 1024

    def kern(g1_ref, g2_ref, b1_ref, b2_ref, a_ref, b_ref, o_ref):
        g1 = _sigmoid(g1_ref[...].astype(F32) + b1_ref[...])
        g2 = _sigmoid(g2_ref[...].astype(F32) + b2_ref[...])
        o_ref[...] = (g1 * a_ref[...] + g2 * b_ref[...]).astype(BF16)

    row = pl.BlockSpec((tl, 1024), lambda i: (i, 0))
    bg2 = bg.reshape(1, 2 * D)
    return pl.pallas_call(
        kern, grid=(L // tl,),
        in_specs=[pl.BlockSpec((tl, 1024), lambda i: (i, goff)), pl.BlockSpec((tl, 1024), lambda i: (i, goff + 1)),
                  pl.BlockSpec((1, 1024), lambda i: (0, 0)), pl.BlockSpec((1, 1024), lambda i: (0, 1)), row, row],
        out_specs=row, out_shape=jax.ShapeDtypeStruct((L, D), BF16), name="merge_fwd",
        compiler_params=_cp(("parallel",)),
    )(proj, proj, bg2, bg2, br_a, br_b)


def _merge_bwd(dm, proj, bg, br_a, br_b, dproj, tl=256):
    L = proj.shape[0]
    goff = C_GATE // 1024

    def kern(dm_ref, g_ref, b_ref, a_ref, bb_ref, _, dbr_ref, dp_ref, gb_ref):
        j = pl.program_id(0)

        @pl.when(pl.program_id(1) == 0)
        def _():
            gb_ref[...] = jnp.zeros_like(gb_ref)

        g = _sigmoid(g_ref[...].astype(F32) + b_ref[...])
        br = jnp.where(j == 0, a_ref[...], bb_ref[...])
        dmv = dm_ref[...]
        dbr_ref[0] = (dmv * g).astype(BF16)
        dgate = dmv * br * g * (1.0 - g)
        gb_ref[...] += jnp.broadcast_to(jnp.sum(dgate, axis=0, keepdims=True), (8, 1024))
        dp_ref[...] = dgate.astype(BF16)

    row = pl.BlockSpec((tl, 1024), lambda j, i: (i, 0))
    gblk = pl.BlockSpec((tl, 1024), lambda j, i: (i, goff + j))
    return pl.pallas_call(
        kern, grid=(2, L // tl),
        in_specs=[row, gblk, pl.BlockSpec((1, 1024), lambda j, i: (0, j)), row, row, pl.BlockSpec(memory_space=pl.ANY)],
        out_specs=(pl.BlockSpec((1, tl, 1024), lambda j, i: (j, i, 0)), gblk, pl.BlockSpec((8, 1024), lambda j, i: (0, j))),
        out_shape=(jax.ShapeDtypeStruct((2, L, D), BF16), jax.ShapeDtypeStruct(dproj.shape, BF16),
                   jax.ShapeDtypeStruct((8, 2 * D), F32)),
        input_output_aliases={5: 1}, name="merge_bwd", compiler_params=_cp(("parallel", "arbitrary")),
    )(dm, proj, bg.reshape(1, 2 * D), br_a, br_b, dproj)


def _coords():
    return lax.axis_index("x"), lax.axis_index("y"), lax.axis_index("c")


def _other_chips(sk):
    xk, yk = sk // 2, sk % 2
    return [((1 - xk, yk), 2 * (1 - xk) + yk), ((xk, 1 - yk), 2 * xk + 1 - yk), ((1 - xk, 1 - yk), 2 * (1 - xk) + 1 - yk)]


def _rows(start, size):
    assert size % 128 == 0
    return pl.ds(pl.multiple_of(start, 128), size)


def _per_chip(fn):
    x, y, _ = _coords()
    s = 2 * x + y
    for sk in range(4):
        pl.when(s == sk)(functools.partial(fn, sk))


XTRA = PIECE - PMAIN


def _place(shard, full_shape, block, index_map, idx, name, blk0=0, nblk=None, dep=None, into=None):
    in_block = block[-2:]
    if nblk is None:
        nblk = shard.shape[0] // in_block[0]

    def kern(idx_ref, s_ref, *rest):
        o_ref = rest[-1]
        o_ref[...] = s_ref[...].astype(BF16).reshape(o_ref.shape)

    extra = ([dep] if dep is not None else []) + ([into] if into is not None else [])
    grid_spec = pltpu.PrefetchScalarGridSpec(
        num_scalar_prefetch=1, grid=(nblk,),
        in_specs=[pl.BlockSpec(in_block, lambda i, idx_ref: (blk0 + i, 0))] + [_ANY] * len(extra),
        out_specs=pl.BlockSpec(block, index_map))
    aliases = {1 + len(extra): 0} if into is not None else {}
    return pl.pallas_call(kern, grid_spec=grid_spec, out_shape=jax.ShapeDtypeStruct(full_shape, BF16), name=name,
                          input_output_aliases=aliases, compiler_params=_cp(("arbitrary",)))(idx, shard, *extra)


_SEM = pl.BlockSpec(memory_space=pltpu.SEMAPHORE)
_EFFECT = pltpu.SideEffectType.DATAFLOW_SIDE_EFFECTING


_ANY = pl.BlockSpec(memory_space=pl.ANY)


def _tie(v, dep, name):
    def body(v_ref, dep_ref, o_ref):
        del v_ref, dep_ref, o_ref

    return pl.pallas_call(body, out_shape=jax.ShapeDtypeStruct(v.shape, v.dtype), in_specs=[_ANY, _ANY],
                          out_specs=_ANY, input_output_aliases={0: 0}, name=name)(v, dep)


def _split_call(name, arrays, start=None, wait=None, wait_sems=None, after=None):
    keys = list(arrays)
    n = len(keys)
    n_start = start.n if start is not None else 0
    afters = [] if after is None else (list(after) if isinstance(after, (list, tuple)) else [after])

    def body(*refs):
        pos = n
        if wait is not None:
            wss, wrs = refs[pos], refs[pos + 1]
            pos += 2
        pos += len(afters)
        if start is not None:
            nss, nrs = refs[pos], refs[pos + 1]
            pos += 2
        R = dict(zip(keys, refs[pos:pos + n]))
        token = refs[pos + n]
        x, y, c = _coords()

        def desc(src, dst, dev, ss, rs, k):
            return pltpu.make_async_remote_copy(src_ref=src, dst_ref=dst, send_sem=ss.at[k], recv_sem=rs.at[k],
                                                device_id=dev, device_id_type=MESH)

        def run(sk):
            if wait is not None:
                for k, (snd, land) in enumerate(wait.copies(sk, R)):
                    if snd is not None:
                        desc(snd[0], snd[1], snd[2], wss, wrs, k).wait_send()
                    if land is not None:
                        desc(land, land, (x, y, c), wss, wrs, k).wait_recv()
            if start is not None:
                for k, (snd, land) in enumerate(start.copies(sk, R)):
                    if snd is not None:
                        desc(snd[0], snd[1], snd[2], nss, nrs, k).start()

        _per_chip(run)
        token[...] = jnp.zeros_like(token)

    hbm = pl.BlockSpec(memory_space=HBM)
    vals = [arrays[k] for k in keys]
    ins, in_specs = list(vals), [hbm] * n
    if wait is not None:
        ins += list(wait_sems)
        in_specs += [_SEM, _SEM]
    ins += afters
    in_specs += [pl.BlockSpec(memory_space=pl.ANY)] * len(afters)
    out_shape, out_specs = [], []
    if start is not None:
        out_shape += [pltpu.SemaphoreType.DMA((n_start,)), pltpu.SemaphoreType.DMA((n_start,))]
        out_specs += [_SEM, _SEM]
    first = len(out_shape)
    out_shape += [jax.ShapeDtypeStruct(v.shape, v.dtype) for v in vals] + [jax.ShapeDtypeStruct((8, 128), F32)]
    out_specs += [hbm] * n + [pl.BlockSpec(memory_space=pltpu.VMEM)]
    res = pl.pallas_call(
        body, out_shape=tuple(out_shape), in_specs=in_specs, out_specs=tuple(out_specs),
        input_output_aliases={i: first + i for i in range(n)}, name=name,
        compiler_params=pltpu.CompilerParams(has_side_effects=_EFFECT),
    )(*ins)
    sems = (res[0], res[1]) if start is not None else None
    return dict(zip(keys, res[first:first + n])), sems, res[-1]


class _Plan:
    def __init__(self, n, copies):
        self.n, self.copies = n, copies


_HM, _HX = PMAIN // 2, XTRA // 2
WAVE0 = 768
WAVES = ((0, WAVE0), (WAVE0, _HM - WAVE0))
_WIN = {
    "wq0": (True, "wct", lambda r, sc, hc: r.at[_rows(PMAIN * sc + _HM * hc + WAVES[0][0], WAVES[0][1]), :]),
    "wq1": (True, "wct", lambda r, sc, hc: r.at[_rows(PMAIN * sc + _HM * hc + WAVES[1][0], WAVES[1][1]), :]),
    "xt": (True, "xt", lambda r, sc, hc: r.at[sc, _rows(_HX * hc, _HX), :]),
    "w1": (True, "w1", lambda r, sc, hc: r.at[_rows(512 * hc, 512), pl.ds(1024 * sc, 1024)]),
    "w2": (True, "w2", lambda r, sc, hc: r.at[_rows(1024 * sc + 512 * hc, 512), :]),
    "wa": (True, "wa", lambda r, sc, hc: r.at[_rows(256 * sc + 128 * hc, 128), :]),
    "wb": (True, "wb", lambda r, sc, hc: r.at[_rows(512 * sc + 256 * hc, 256), :]),
    "wo": (True, "wo", lambda r, sc, hc: r.at[_rows(256 * sc + 128 * hc, 128), :]),
    "cw": (False, "cw", lambda r, sc, hc: r.at[sc]),
}


_PIECE_SRC = {
    "wq0": lambda p, hc: p.at[_rows(_HM * hc + WAVES[0][0], WAVES[0][1]), :],
    "wq1": lambda p, hc: p.at[_rows(_HM * hc + WAVES[1][0], WAVES[1][1]), :],
    "xt": lambda p, hc: p.at[_rows(PMAIN + _HX * hc, _HX), :],
}


def _ag_chips_plan(keys):
    def copies(sk, R):
        _, _, c = _coords()
        out = []
        for key in keys:
            _, arr, win = _WIN[key]
            for (px, py), ps in _other_chips(sk):
                dst = win(R[arr], sk, c)
                src = _PIECE_SRC[key](R["piece"], c) if key in _PIECE_SRC else dst
                out.append(((src, dst, (px, py, c)), win(R[arr], ps, c)))
        return out
    return _Plan(3 * len(keys), copies)


def _ag_sibling_plan(keys):
    keys = [k for k in keys if _WIN[k][0]]

    def copies(sk, R):
        x, y, c = _coords()
        out = []
        for key in keys:
            _, arr, win = _WIN[key]
            for _, ps in _other_chips(sk):
                w = win(R[arr], ps, c)
                out.append(((w, w, (x, y, 1 - c)), win(R[arr], ps, 1 - c)))
        return out
    return _Plan(3 * len(keys), copies)


def _in_proj_wave(h, wct, wave, proj=None, tm=2048):
    L = h.shape[0]
    tm = min(tm, L)
    off, size = WAVES[wave]
    start = lambda j: pl.multiple_of(_HM * j + off, 128)

    def kern(h_ref, w_ref, *rest):
        o_ref = rest[-1]
        o_ref[...] = lax.dot_general(h_ref[...], w_ref[...], _DIMS["nt"], preferred_element_type=F32).astype(BF16)

    in_specs = [pl.BlockSpec((tm, D), lambda j, i: (i, 0)),
                pl.BlockSpec((pl.Element(size), pl.Element(D)), lambda j, i: (start(j), 0))]
    args, aliases = [h, wct], {}
    if proj is not None:
        in_specs.append(pl.BlockSpec(memory_space=pl.ANY))
        args.append(proj)
        aliases = {2: 0}
    return pl.pallas_call(
        kern, grid=(8, L // tm), in_specs=in_specs,
        out_specs=pl.BlockSpec((pl.Element(tm), pl.Element(size)), lambda j, i: (i * tm, start(j))),
        out_shape=jax.ShapeDtypeStruct((L, NCW), BF16), input_output_aliases=aliases,
        name="in_proj_wave%d" % wave, compiler_params=_cp(("parallel", "parallel")),
    )(*args)


def _fix_wct(wct, xt):
    nb = PMAIN // XTRA

    def kern(w_ref, x_ref, o_ref):
        k = pl.program_id(0)
        xv = x_ref[0]
        o_ref[...] = jnp.where(k < 3, (w_ref[...].astype(F32) + xv.astype(F32)).astype(BF16), xv)

    blk = pl.BlockSpec((XTRA, D), lambda k: (nb * (k + 1), 0))
    rblk = pl.BlockSpec((XTRA, D), lambda k: (jnp.where(k < 3, nb * (k + 1), 0), 0))
    return pl.pallas_call(
        kern, grid=(4,), in_specs=[rblk, pl.BlockSpec((1, XTRA, D), lambda k: (k, 0, 0))], out_specs=blk,
        out_shape=jax.ShapeDtypeStruct(wct.shape, BF16), input_output_aliases={0: 0}, name="fix_wct",
        compiler_params=_cp(("arbitrary",)),
    )(wct, xt)


_HP = PIECE // 2
_GWIN = [
    lambda r, sc, hc: r.at[_rows(PMAIN * sc + _HP * hc, _HP), :],
    lambda r, sc, hc: r.at[_rows(512 * hc, 512), pl.ds(1024 * sc, 1024)],
    lambda r, sc, hc: r.at[_rows(1024 * sc + 512 * hc, 512), :],
    lambda r, sc, hc: r.at[_rows(256 * sc + 128 * hc, 128), :],
    lambda r, sc, hc: r.at[_rows(512 * sc + 256 * hc, 256), :],
    lambda r, sc, hc: r.at[_rows(256 * sc + 128 * hc, 128), :],
]
HALF_SHAPES = [(PIECE // 2, D), (512, 1024), (512, 1024), (128, 1024), (256, 1024), (128, 1024)]


def _rs_sibling_plan(ts):
    def copies(sk, R):
        x, y, c = _coords()
        out = []
        for t in ts:
            for sc in range(4):
                land = R["ra%d" % t].at[sc]
                out.append(((_GWIN[t](R["g%d" % t], sc, 1 - c), land, (x, y, 1 - c)), land))
        return out
    return _Plan(4 * len(ts), copies)


def _rs_chips_plan(ts):
    def copies(sk, R):
        _, _, c = _coords()
        out = []
        for t in ts:
            for j, ((px, py), ps) in enumerate(_other_chips(sk)):
                land = R["rb%d" % t].at[j]
                out.append(((R["hb%d" % t].at[ps], land, (px, py, c)), land))
        return out
    return _Plan(3 * len(ts), copies)


def _rs_share_plan(ts):
    def copies(sk, R):
        x, y, c = _coords()
        out = []
        for t in ts:
            rows = HALF_SHAPES[t][0]
            mine = R["f%d" % t].at[_rows(rows * c, rows), :]
            out.append(((mine, mine, (x, y, 1 - c)), R["f%d" % t].at[_rows(rows * (1 - c), rows), :]))
        return out
    return _Plan(len(ts), copies)


def _half_tiling(t):
    rows, cols = HALF_SHAPES[t]
    if t == 0:
        return (rows // 2, cols), 2, lambda i: (i, 0)
    return (rows, cols), 1, lambda i: (0, 0)


def _window_spec(t, blk):
    if t == 0:
        return pl.BlockSpec((pl.Element(blk[0]), pl.Element(blk[1])), lambda i, sc, idx_ref: (
            pl.multiple_of(PMAIN * sc + _HP * idx_ref[1] + blk[0] * i, 128), 0))
    if t == 1:
        return pl.BlockSpec(blk, lambda i, sc, idx_ref: (idx_ref[1], sc))
    return pl.BlockSpec(blk, lambda i, sc, idx_ref: (2 * sc + idx_ref[1], 0))


def _chip_sum(g, ra, t, idx, name):
    rows, cols = HALF_SHAPES[t]
    blk, nblk, inner = _half_tiling(t)

    def kern(idx_ref, g_ref, r_ref, hb_ref, hf_ref):
        v = g_ref[...].astype(F32) + r_ref[0].astype(F32)
        hb_ref[0] = v.astype(BF16)

        @pl.when(pl.program_id(1) == idx_ref[0])
        def _():
            hf_ref[...] = v

    omap = lambda i, sc, idx_ref: (sc,) + inner(i)
    grid_spec = pltpu.PrefetchScalarGridSpec(
        num_scalar_prefetch=1, grid=(nblk, 4),
        in_specs=[_window_spec(t, blk), pl.BlockSpec((1,) + blk, omap)],
        out_specs=(pl.BlockSpec((1,) + blk, omap), pl.BlockSpec(blk, lambda i, sc, idx_ref: inner(i))))
    return pl.pallas_call(
        kern, grid_spec=grid_spec,
        out_shape=(jax.ShapeDtypeStruct((4, rows, cols), BF16), jax.ShapeDtypeStruct((rows, cols), F32)),
        name=name, compiler_params=_cp(("parallel", "arbitrary")),
    )(idx, g, ra)


def _final_sum(hf, rb, t, idx, name):
    rows, cols = HALF_SHAPES[t]
    blk, nblk, inner = _half_tiling(t)
    nbr = rows // blk[0]

    def kern(idx_ref, h_ref, r_ref, o_ref):
        o_ref[...] = ((h_ref[...] + r_ref[0].astype(F32)) + r_ref[1].astype(F32)) + r_ref[2].astype(F32)

    def omap(i, idx_ref):
        r, cidx = inner(i)
        return nbr * idx_ref[1] + r, cidx

    grid_spec = pltpu.PrefetchScalarGridSpec(
        num_scalar_prefetch=1, grid=(nblk,),
        in_specs=[pl.BlockSpec(blk, lambda i, idx_ref: inner(i)),
                  pl.BlockSpec((3,) + blk, lambda i, idx_ref: (0,) + inner(i))],
        out_specs=pl.BlockSpec(blk, omap))
    return pl.pallas_call(
        kern, grid_spec=grid_spec, out_shape=jax.ShapeDtypeStruct((2 * rows, cols), F32),
        name=name, compiler_params=_cp(("parallel",)),
    )(idx, hf, rb)


class _ReduceScatter:
    def __init__(self, ts, grads, idx, tag):
        self.ts, self.idx, self.tag = ts, idx, tag
        arr = {}
        for t in ts:
            arr["g%d" % t] = grads[t]
            arr["ra%d" % t] = lax.empty((4,) + HALF_SHAPES[t], BF16)
        self.plan = _rs_sibling_plan(ts)
        self.arr, self.sems, self.token = _split_call("rs_sibling_start_" + tag, arr, start=self.plan)

    def chips(self, after):
        arr, _, _ = _split_call("rs_sibling_wait_" + self.tag, self.arr, wait=self.plan, wait_sems=self.sems, after=after)
        brr, self.hf = {}, {}
        for t in self.ts:
            hb, self.hf[t] = _chip_sum(arr["g%d" % t], arr["ra%d" % t], t, self.idx, "chip_sum_%d" % t)
            brr["hb%d" % t] = hb
            brr["rb%d" % t] = lax.empty((3,) + HALF_SHAPES[t], BF16)
        self.plan = _rs_chips_plan(self.ts)
        self.arr, self.sems, self.token = _split_call("rs_chips_start_" + self.tag, brr, start=self.plan)
        return self.token

    def share(self, after):
        brr, _, _ = _split_call("rs_chips_wait_" + self.tag, self.arr, wait=self.plan, wait_sems=self.sems, after=after)
        frr = {"f%d" % t: _final_sum(self.hf[t], brr["rb%d" % t], t, self.idx, "final_sum_%d" % t) for t in self.ts}
        self.plan = _rs_share_plan(self.ts)
        self.arr, self.sems, self.token = _split_call("rs_share_start_" + self.tag, frr, start=self.plan)
        return self.token

    def result(self, after):
        frr, _, _ = _split_call("rs_share_wait_" + self.tag, self.arr, wait=self.plan, wait_sems=self.sems, after=after)
        return {t: frr["f%d" % t] for t in self.ts}


def _all8_plan(key):
    def copies(sk, R):
        x, y, c = _coords()
        own = R[key].at[4 * x + 2 * y + c]
        out = []
        for k in range(1, 8):
            dev = ((1 - x) if (k >> 2) & 1 else x, (1 - y) if (k >> 1) & 1 else y, (1 - c) if k & 1 else c)
            out.append(((own, own, dev), R[key].at[4 * dev[0] + 2 * dev[1] + dev[2]]))
        return out
    return _Plan(7, copies)


def _sum8(v, name="small_sum"):
    def kern(v_ref, o_ref):
        acc = v_ref[0]
        for k in range(1, 8):
            acc = acc + v_ref[k]
        o_ref[...] = acc

    return pl.pallas_call(kern, out_shape=jax.ShapeDtypeStruct(v.shape[1:], F32), name=name)(v)


def _adamw(w, g, m, v, name, tr=128, blk0=0, nblk=None, into=None, copy_g=False):
    R, C = w.shape
    tr = min(tr, R)
    if nblk is None:
        assert R % tr == 0 and blk0 == 0
        nblk = R // tr
    n_out = 4 if copy_g else 3

    def kern(*refs):
        w_ref, g_ref, m_ref, v_ref = refs[:4]
        d_ref, mo_ref, vo_ref = refs[-n_out:][:3]
        gv = g_ref[...]
        mn = ADAM_B1 * m_ref[...] + (1.0 - ADAM_B1) * gv
        vn = ADAM_B2 * v_ref[...] + (1.0 - ADAM_B2) * (gv * gv)
        m_hat = mn / (1.0 - ADAM_B1 ** ADAM_STEP)
        v_hat = vn / (1.0 - ADAM_B2 ** ADAM_STEP)
        d_ref[...] = -ADAM_LR * (m_hat / (jnp.sqrt(v_hat) + ADAM_EPS) + ADAM_WD * w_ref[...])
        mo_ref[...] = mn
        vo_ref[...] = vn
        if copy_g:
            refs[-1][...] = gv

    blk = pl.BlockSpec((tr, C), lambda i: (blk0 + i, 0))
    sd = jax.ShapeDtypeStruct((R, C), F32)
    in_specs, args, aliases = [blk] * 4, [w, g, m, v], {}
    if into is not None:
        in_specs += [pl.BlockSpec(memory_space=pl.ANY)] * 3
        args += list(into)
        aliases = {4: 0, 5: 1, 6: 2}
    return pl.pallas_call(kern, grid=(nblk,), in_specs=in_specs, out_specs=(blk,) * n_out, out_shape=(sd,) * n_out,
                          input_output_aliases=aliases, name=name, compiler_params=_cp(("parallel",)))(*args)


def _adamw_w_in(wt, gp, mt, vt, offs, name, r0, tr, nblk, views, into=None, blk_key=None):
    el = lambda n: (pl.Element(n), pl.Element(D))
    first = (lambda o: r0) if blk_key is None else (lambda o: tr * o[blk_key])
    own = pl.BlockSpec(el(tr), lambda i, o: (pl.multiple_of(first(o) + tr * i, 8), 0))

    def view(k):
        return pl.BlockSpec(el(tr), lambda i, o: (pl.multiple_of(jnp.maximum(first(o) + tr * i + o[k], 0), 8), 0))

    def kern(o_ref, w_ref, m_ref, v_ref, *refs):
        g_refs, (d_ref, mo_ref, vo_ref, go_ref) = refs[:len(views)], refs[-4:]
        gv = g_refs[0][...]
        if len(views) == 2:
            row = first(o_ref) + tr * pl.program_id(0) + lax.broadcasted_iota(jnp.int32, (tr, D), 0)
            gv = jnp.where(row < o_ref[2], gv, g_refs[1][...])
        mn = ADAM_B1 * m_ref[...] + (1.0 - ADAM_B1) * gv
        vn = ADAM_B2 * v_ref[...] + (1.0 - ADAM_B2) * (gv * gv)
        m_hat = mn / (1.0 - ADAM_B1 ** ADAM_STEP)
        v_hat = vn / (1.0 - ADAM_B2 ** ADAM_STEP)
        d_ref[...] = -ADAM_LR * (m_hat / (jnp.sqrt(v_hat) + ADAM_EPS) + ADAM_WD * w_ref[...])
        mo_ref[...] = mn
        vo_ref[...] = vn
        go_ref[...] = gv

    in_specs = [own, own, own] + [view(k) for k in views]
    args = [wt, mt, vt] + [gp] * len(views)
    aliases = {}
    if into is not None:
        in_specs += [pl.BlockSpec(memory_space=pl.ANY)] * 4
        args += list(into)
        aliases = {1 + len(args) - 4 + j: j for j in range(4)}
    grid_spec = pltpu.PrefetchScalarGridSpec(num_scalar_prefetch=1, grid=(nblk,), in_specs=in_specs,
                                             out_specs=(own,) * 4)
    sd = jax.ShapeDtypeStruct(wt.shape, F32)
    return pl.pallas_call(kern, grid_spec=grid_spec, out_shape=(sd,) * 4, input_output_aliases=aliases, name=name,
                          compiler_params=_cp(("parallel",)))(offs, *args)


def _to_piece(wt, s):
    z = lambda n: jnp.zeros((n, D), wt.dtype)
    pads = [functools.partial(lambda k, w: jnp.pad(w, ((8 * k, PIECE - W_SHARD - 8 * k), (0, 0))).astype(BF16), k)
            for k in range(3)]
    last = lambda w: jnp.concatenate([z(24), w[:744], w[776:], w[744:776], z(PIECE - 24 - W_SHARD)], axis=0).astype(BF16)
    return lax.switch(s, pads + [last], wt)


_SMALL = [("b_gate", 2048), ("ssm_conv_b", 4096), ("dt_bias", 32), ("A_log", 32), ("D_skip", 32),
          ("ssm_norm_w", 2048), ("norm_mlp", 1024), ("norm_final", 1024), ("sc_conv_w", 3072), ("ssm_conv_w", 16384),
          ("loss", 1)]


def _pack(vals, table, rows):
    parts = []
    for name, n in table:
        v = vals[name].reshape(-1).astype(F32)
        pad = (-n) % 128
        parts.append(jnp.pad(v, (0, pad)) if pad else v)
    flat = jnp.concatenate(parts)
    return jnp.pad(flat, (0, rows * 128 - flat.shape[0])).reshape(rows, 128)


def _unpack(arr, table):
    flat = arr.reshape(-1)
    out, off = {}, 0
    for name, n in table:
        out[name] = flat[off:off + n]
        off += n + ((-n) % 128)
    return out


def kernel(x, norm_mix, w_in, b_gate, sc_conv_w, ssm_conv_w, ssm_conv_b, dt_bias, A_log, D_skip, ssm_norm_w, w_branch_sc, w_branch_ssm, w_out, norm_mlp, w_mlp1, w_mlp2, norm_final, loss_target, m_norm_mix, m_w_in, m_b_gate, m_sc_conv_w, m_ssm_conv_w, m_ssm_conv_b, m_dt_bias, m_A_log, m_D_skip, m_ssm_norm_w, m_w_branch_sc, m_w_branch_ssm, m_w_out, m_norm_mlp, m_w_mlp1, m_w_mlp2, m_norm_final, v_norm_mix, v_w_in, v_b_gate, v_sc_conv_w, v_ssm_conv_w, v_ssm_conv_b, v_dt_bias, v_A_log, v_D_skip, v_ssm_norm_w, v_w_branch_sc, v_w_branch_ssm, v_w_out, v_norm_mlp, v_w_mlp1, v_w_mlp2, v_norm_final):
    L = x.shape[1]
    nc = L // Q
    xi, yi, ci = lax.axis_index("x"), lax.axis_index("y"), lax.axis_index("c")
    s = 2 * xi + yi
    idx = jnp.stack([s, ci]).astype(jnp.int32)
    x0 = x.reshape(L, D)
    tgt = loss_target.reshape(L, D)
    small_names = ["b_gate", "sc_conv_w", "ssm_conv_w", "ssm_conv_b", "dt_bias", "A_log", "D_skip", "ssm_norm_w",
                   "norm_mlp", "norm_final"]
    small_wmv = [dict(zip(small_names, vals)) for vals in (
        (b_gate, sc_conv_w, ssm_conv_w, ssm_conv_b, dt_bias, A_log, D_skip, ssm_norm_w, norm_mlp, norm_final),
        (m_b_gate, m_sc_conv_w, m_ssm_conv_w, m_ssm_conv_b, m_dt_bias, m_A_log, m_D_skip, m_ssm_norm_w, m_norm_mlp,
         m_norm_final),
        (v_b_gate, v_sc_conv_w, v_ssm_conv_w, v_ssm_conv_b, v_dt_bias, v_A_log, v_D_skip, v_ssm_norm_w, v_norm_mlp,
         v_norm_final))]
    small_table = [(n, int(small_wmv[0][n].size)) for n in small_names]
    small_rows = 136
    pk_w, pk_m, pk_v = [_pack(d, small_table, small_rows) for d in small_wmv]

    piece = _to_piece(w_in.T, s)
    nb = PMAIN // XTRA
    cws = jnp.zeros((8, 1280), F32)
    cws = cws.at[0:3, 0:256].set(sc_conv_w).at[0:4, 256:1280].set(ssm_conv_w)
    cw0 = lax.dynamic_update_slice(jnp.zeros((4, 8, 1280), F32), cws[None], (s, 0, 0))
    win_keys, win2_keys, mid_keys, end_keys = ["xt", "cw", "wq0"], ["wq1"], ["wa", "wb", "wo", "w1"], ["w2"]
    gw, sems_w, tok = _split_call(
        "ag_win_start", {"wct": lax.empty((NCW, D), BF16), "xt": lax.empty((4, XTRA, D), BF16), "cw": cw0, "piece": piece},
        start=_ag_chips_plan(win_keys))
    g2, sems_w2, tok = _split_call("ag_win2_start", {"wct": gw["wct"], "piece": gw["piece"]},
                                   start=_ag_chips_plan(win2_keys), after=tok)
    piece = g2["piece"]
    gw["wct"] = _place(piece, (NCW, D), (XTRA, D), lambda i, r: (nb * r[0] + i, 0), idx, "place_wct", nblk=nb,
                       dep=tok, into=g2["wct"])
    gw["xt"] = _place(piece, (4, XTRA, D), (1, XTRA, D), lambda i, r: (r[0], 0, 0), idx, "place_xt", blk0=nb, nblk=1,
                      dep=tok, into=gw["xt"])
    gw["piece"] = piece
    wa0 = _place(w_branch_sc, (D, D), (256, 1024), lambda i, r: (r[0], 0), idx, "place_wa", dep=tok)
    wb0 = _place(w_branch_ssm, (INNER, D), (512, 1024), lambda i, r: (r[0], 0), idx, "place_wb", dep=tok)
    wo0 = _place(w_out, (D, D), (256, 1024), lambda i, r: (r[0], 0), idx, "place_wo", dep=tok)
    w10 = _place(w_mlp1, (D, DFF), (256, 1024), lambda i, r: (i, r[0]), idx, "place_w1", dep=tok)
    gm, sems_m, tok = _split_call("ag_mid_start", {"wa": wa0, "wb": wb0, "wo": wo0, "w1": w10},
                                  start=_ag_chips_plan(mid_keys))
    w20 = _place(w_mlp2, (DFF, D), (256, 1024), lambda i, r: (4 * r[0] + i, 0), idx, "place_w2", dep=tok)
    ge, sems_e, tok = _split_call("ag_end_start", {"w2": w20}, start=_ag_chips_plan(end_keys))
    h = _rms_fwd(x0, norm_mix, "rms_mix", dep=tok)
    gw, sems_w, tok = _split_call("ag_win_pass", gw, wait=_ag_chips_plan(win_keys), wait_sems=sems_w,
                                  start=_ag_sibling_plan(win_keys), after=[h, pk_w, pk_m, pk_v])
    gw, _, _ = _split_call("ag_win_done", gw, wait=_ag_sibling_plan(win_keys), wait_sems=sems_w, after=tok)
    wc, cw_all = _fix_wct(gw["wct"], gw["xt"]), gw["cw"]
    sc_w_full = jnp.concatenate([cw_all[k, :, 0:256] for k in range(4)], axis=1)
    ssm_w_full = jnp.concatenate([cw_all[k, :, 256:1280] for k in range(4)], axis=1)
    cw4 = ssm_w_full.at[4].set(ssm_conv_b)
    vec = jnp.zeros((8, 128), F32).at[0, :NH].set(dt_bias).at[1, :NH].set(A_log)
    vecg = jnp.zeros((NG, 8, 128), F32).at[:, 0, :4].set(A_log.reshape(NG, 4)).at[:, 1, :4].set(D_skip.reshape(NG, 4))

    dtraw = _matmul(h, wc[C_DT:], "nt", F32, 512, 256, 1024, "in_proj_dt")
    proj = _in_proj_wave(h, wc, 0)
    g2, sems_w2, tok = _split_call("ag_win2_pass", {"wct": wc, "piece": gw["piece"]},
                                   wait=_ag_chips_plan(win2_keys), wait_sems=sems_w2,
                                   start=_ag_sibling_plan(win2_keys), after=[proj, dtraw])
    g2, _, _ = _split_call("ag_win2_done", g2, wait=_ag_sibling_plan(win2_keys), wait_sems=sems_w2, after=tok)
    wc = g2["wct"]
    proj = _in_proj_wave(h, wc, 1, proj=proj)
    ya = _sc_fwd(proj, sc_w_full)
    xbc = _ssm_conv_fwd(proj, cw4)
    dt4, cs4, sg4 = _dt_prep(dtraw, vec)
    y, s_all = _ssd_fwd(xbc, dt4, cs4, vecg)
    gm, sems_m, tok = _split_call("ag_mid_pass", gm, wait=_ag_chips_plan(mid_keys), wait_sems=sems_m,
                                  start=_ag_sibling_plan(mid_keys), after=[y, ya])
    y = _tie(y, tok, "tie_y")
    yb = _gnorm_fwd(y, proj, ssm_norm_w)
    gm, _, _ = _split_call("ag_mid_done", gm, wait=_ag_sibling_plan(mid_keys), wait_sems=sems_m, after=yb)
    wa, wb, wo, w1 = gm["wa"], gm["wb"], gm["wo"], gm["w1"]
    ge, sems_e, tok = _split_call("ag_end_pass", ge, wait=_ag_chips_plan(end_keys), wait_sems=sems_e,
                                  start=_ag_sibling_plan(end_keys), after=yb)
    br_a = _matmul(ya, wa, "nn", F32, 1024, 1024, 1024, "branch_sc", dep=tok)
    br_b = _matmul(yb, wb, "nn", F32, 1024, 1024, 2048, "branch_ssm")
    merged = _merge_fwd(proj, b_gate, br_a, br_b)
    x1, h2 = _matmul_res_rms(merged, wo, x0, norm_mlp, 1024, "out_proj")
    a1, rl = _matmul(h2, w1, "nn", BF16, 1024, 1024, 1024, "mlp1", epi="relu2", n_outer=True)
    ge, _, _ = _split_call("ag_end_done", ge, wait=_ag_sibling_plan(end_keys), wait_sems=sems_e, after=a1)
    w2 = ge["w2"]
    x2 = _matmul(rl, w2, "nn", F32, 512, 1024, 4096, "mlp2", epi="res", extra=x1)
    dx2, g_nf, loss8 = _final(x2, norm_final, tgt)

    da = _matmul(dx2, w2, "nt", BF16, 1024, 1024, 1024, "mlp2_dx", epi="drelu", extra=a1, n_outer=True)
    g_w2 = _matmul(rl, dx2, "tn", BF16, 1024, 1024, 2048, "mlp2_dw")
    g_w1 = _matmul(h2, da, "tn", BF16, 1024, 1024, 2048, "mlp1_dw")
    dx1, g_nmlp = _matmul_rms_bwd(da, w1, "nt", x1, norm_mlp, dx2, 512, 4096, "mlp1_dx")
    dmerged = _matmul(dx1, wo, "nt", F32, 1024, 1024, 1024, "out_proj_dx")
    g_wo = _matmul(merged, dx1, "tn", BF16, 1024, 1024, 2048, "out_proj_dw")
    dproj = lax.empty((L, NCW), BF16)
    dbr, dproj, g_bg = _merge_bwd(dmerged, proj, b_gate, br_a, br_b, dproj)
    dya = _matmul(dbr[0], wa, "nt", F32, 1024, 1024, 1024, "branch_sc_dx")
    g_wa = _matmul(ya, dbr[0], "tn", BF16, 1024, 1024, 2048, "branch_sc_dw")
    dproj, g_scw = _sc_bwd(dya, proj, sc_w_full, dproj)
    dyb = _matmul(dbr[1], wb, "nt", F32, 1024, 1024, 1024, "branch_ssm_dx", n_outer=True)
    g_wb = _matmul(yb, dbr[1], "tn", BF16, 1024, 1024, 2048, "branch_ssm_dw")
    rs_a = _ReduceScatter([1, 2, 3, 4, 5], {1: g_w1, 2: g_w2, 3: g_wa, 4: g_wb, 5: g_wo}, idx, "a")
    dy, dproj, g_snw = _gnorm_bwd(_tie(dyb, rs_a.token, "tie_dyb"), y, proj, ssm_norm_w, dproj)
    tok = rs_a.chips(after=dy)
    dxs, dbm, dcm, ddt_g, st = _ssd_bwd(xbc, dt4, cs4, sg4, vecg, s_all, _tie(dy, tok, "tie_dy"))
    dproj, gx1 = _ssm_conv_bwd(dxs, proj, cw4, dproj, 0, "ssm_conv_bwd_x")
    dproj, gx2 = _ssm_conv_bwd(dbm, proj, cw4, dproj, INNER, "ssm_conv_bwd_b")
    dproj, gx3 = _ssm_conv_bwd(dcm, proj, cw4, dproj, INNER + NG * NS, "ssm_conv_bwd_c")
    g_cw4 = jnp.concatenate([gx1, gx2, gx3], axis=1)
    dproj, g_dtb = _dt_bwd(ddt_g, dproj)
    small = {"b_gate": g_bg[0], "ssm_conv_b": g_cw4[4], "dt_bias": g_dtb[0, :NH],
             "A_log": st[:, 0, :4], "D_skip": st[:, 1, :4], "ssm_norm_w": g_snw[0], "norm_mlp": g_nmlp[0],
             "norm_final": g_nf[0], "sc_conv_w": g_scw[0:3], "ssm_conv_w": g_cw4[0:4], "loss": loss8[0, 0:1]}
    me = 4 * xi + 2 * yi + ci
    sm8 = lax.dynamic_update_slice(jnp.zeros((8, SMALL_ROWS, 128), F32), _pack(small, _SMALL, SMALL_ROWS)[None], (me, 0, 0))
    sm_arr, sm_sems, tok = _split_call("small_start", {"sm": sm8}, start=_all8_plan("sm"))
    g_wc = _matmul(dproj, h, "tn", BF16, 1280, 1024, 2048, "in_proj_dw", dep=tok)
    rs_b = _ReduceScatter([0], {0: g_wc}, idx, "b")
    tok = rs_a.share(after=rs_b.token)
    tok = rs_b.chips(after=tok)
    grad_x, g_nm = _matmul_rms_bwd(dproj, wc, "nn", x0, norm_mix, dx1, 512, 5760, "in_proj_dx", dep=tok)
    nm8 = lax.dynamic_update_slice(jnp.zeros((8, 8, 128), F32), g_nm[0].reshape(1, 8, 128), (me, 0, 0))
    nm_arr, nm_sems, tok = _split_call("norm_mix_start", {"nm": nm8}, start=_all8_plan("nm"))
    sm_arr, _, _ = _split_call("small_wait", sm_arr, wait=_all8_plan("sm"), wait_sems=sm_sems, after=tok)
    small_sum = _sum8(sm_arr["sm"])
    gs = _unpack(small_sum, _SMALL)
    red = rs_a.result(after=tok)
    big = {"w_mlp1": red[1], "w_mlp2": red[2], "w_branch_sc": red[3], "w_branch_ssm": red[4], "w_out": red[5]}

    given = dict(norm_mix=norm_mix, w_in=w_in, b_gate=b_gate, sc_conv_w=sc_conv_w, ssm_conv_w=ssm_conv_w, ssm_conv_b=ssm_conv_b, dt_bias=dt_bias, A_log=A_log, D_skip=D_skip, ssm_norm_w=ssm_norm_w, w_branch_sc=w_branch_sc, w_branch_ssm=w_branch_ssm, w_out=w_out, norm_mlp=norm_mlp, w_mlp1=w_mlp1, w_mlp2=w_mlp2, norm_final=norm_final,
                 m_norm_mix=m_norm_mix, m_w_in=m_w_in, m_b_gate=m_b_gate, m_sc_conv_w=m_sc_conv_w, m_ssm_conv_w=m_ssm_conv_w, m_ssm_conv_b=m_ssm_conv_b, m_dt_bias=m_dt_bias, m_A_log=m_A_log, m_D_skip=m_D_skip, m_ssm_norm_w=m_ssm_norm_w, m_w_branch_sc=m_w_branch_sc, m_w_branch_ssm=m_w_branch_ssm, m_w_out=m_w_out, m_norm_mlp=m_norm_mlp, m_w_mlp1=m_w_mlp1, m_w_mlp2=m_w_mlp2, m_norm_final=m_norm_final,
                 v_norm_mix=v_norm_mix, v_w_in=v_w_in, v_b_gate=v_b_gate, v_sc_conv_w=v_sc_conv_w, v_ssm_conv_w=v_ssm_conv_w, v_ssm_conv_b=v_ssm_conv_b, v_dt_bias=v_dt_bias, v_A_log=v_A_log, v_D_skip=v_D_skip, v_ssm_norm_w=v_ssm_norm_w, v_w_branch_sc=v_w_branch_sc, v_w_branch_ssm=v_w_branch_ssm, v_w_out=v_w_out, v_norm_mlp=v_norm_mlp, v_w_mlp1=v_w_mlp1, v_w_mlp2=v_w_mlp2, v_norm_final=v_norm_final)
    order = ["norm_mix", "w_in", "b_gate", "sc_conv_w", "ssm_conv_w", "ssm_conv_b", "dt_bias", "A_log", "D_skip",
             "ssm_norm_w", "w_branch_sc", "w_branch_ssm", "w_out", "norm_mlp", "w_mlp1", "w_mlp2", "norm_final"]
    grad, delta, new_m, new_v = {}, {}, {}, {}
    for n in big:
        delta[n], new_m[n], new_v[n], grad[n] = _adamw(given[n], big[n], given["m_" + n], given["v_" + n],
                                                       "adamw_" + n, copy_g=True)
    big["w_in"] = None
    grad_small = {n: gs[n].reshape(given[n].shape) for n in small_names if n not in ("sc_conv_w", "ssm_conv_w")}
    grad_small["sc_conv_w"] = lax.dynamic_slice(gs["sc_conv_w"].reshape(3, D), (0, 256 * s), (3, 256))
    grad_small["ssm_conv_w"] = lax.dynamic_slice(gs["ssm_conv_w"].reshape(4, XBC), (0, 1024 * s), (4, 1024))
    table = small_table
    ds_, ms_, vs_ = _adamw(pk_w, _pack(grad_small, table, small_rows), pk_m, pk_v, "adamw_small", tr=small_rows)
    ds_, ms_, vs_ = _unpack(ds_, table), _unpack(ms_, table), _unpack(vs_, table)
    for n in grad_small:
        shp = given[n].shape
        grad[n] = grad_small[n]
        delta[n], new_m[n], new_v[n] = ds_[n].reshape(shp), ms_[n].reshape(shp), vs_[n].reshape(shp)

    done = [new_v[n] for n in ("w_mlp1", "w_mlp2", "w_branch_sc", "w_branch_ssm", "w_out")] + [vs_["b_gate"]]
    tok = rs_b.share(after=done)
    offs = jnp.where(s == 3, jnp.array([24, -8, 744, 2072, -8], jnp.int32),
                     jnp.stack([8 * s, 8 * s, 0 * s, 8 * s, 8 * s]).astype(jnp.int32))
    offs = jnp.concatenate([offs, jnp.stack([7 * ci, 4 - 4 * ci]).astype(jnp.int32)])
    nmain = W_SHARD // 256
    wt_own = (w_in.T, rs_b.arr["f0"], m_w_in.T, v_w_in.T, offs)
    res = _adamw_w_in(*wt_own, "adamw_w_in_own", 0, 256, 4, (0, 1), blk_key=5)
    gp = rs_b.result(after=[tok, res[0]])[0]
    wt_args = (w_in.T, gp, m_w_in.T, v_w_in.T, offs)
    res = _adamw_w_in(*wt_args, "adamw_w_in", 0, 256, nmain - 4, (0, 1), into=res, blk_key=6)
    res = _adamw_w_in(*wt_args, "adamw_w_in_dt", 744, 32, 1, (3,), into=res)
    dt_, mt_, vt_, gwt = _adamw_w_in(*wt_args, "adamw_w_in_tail", 256 * nmain, 8, 1, (4,), into=res)
    grad["w_in"], delta["w_in"], new_m["w_in"], new_v["w_in"] = gwt.T, dt_.T, mt_.T, vt_.T
    nm_arr, _, _ = _split_call("norm_mix_wait", nm_arr, wait=_all8_plan("nm"), wait_sems=nm_sems, after=tok)
    g8 = _sum8(nm_arr["nm"], "norm_mix_sum")
    r8 = lambda a: a.reshape(8, 128)
    d8, m8, v8 = _adamw(r8(norm_mix), g8, r8(m_norm_mix), r8(v_norm_mix), "adamw_norm_mix", tr=8)
    grad["norm_mix"], delta["norm_mix"] = g8.reshape(D), d8.reshape(D)
    new_m["norm_mix"], new_v["norm_mix"] = m8.reshape(D), v8.reshape(D)

    loss = gs["loss"].reshape(())
    return (loss, grad_x.reshape(1, L, D), *[grad[n] for n in order], *[delta[n] for n in order],
            *[new_m[n] for n in order], *[new_v[n] for n in order])
```

```python
import functools

import jax
import jax.numpy as jnp
from jax import lax
from jax.experimental import pallas as pl
from jax.experimental.pallas import tpu as pltpu

F32 = jnp.float32
BF16 = jnp.bfloat16
MESH = pl.DeviceIdType.MESH
HBM = pltpu.HBM

D = 1024
INNER = 2048
HD = 64
NH = 32
NG = 8
NS = 128
Q = 128
GPS = 8
XBC = 4096
DFF = 4096
EPS = 1e-6
W_SHARD = 2824
NCW = 11520
PIECE = 3072
PMAIN = 2816
C_Z, C_XBC, C_GATE, C_DT = 3072, 5120, 9216, 11264
SMALL_ROWS = 256
VMEM_LIMIT = 56 * 1024 * 1024

ADAM_LR, ADAM_B1, ADAM_B2, ADAM_EPS, ADAM_WD, ADAM_STEP = 0.001, 0.9, 0.999, 1e-08, 0.01, 10


def _cp(sem=None, vmem=VMEM_LIMIT):
    return pltpu.CompilerParams(dimension_semantics=sem, vmem_limit_bytes=vmem)


def _sigmoid(v):
    return 1.0 / (1.0 + jnp.exp(-v))


_DIMS = {"nn": (((1,), (0,)), ((), ())), "nt": (((1,), (1,)), ((), ())), "tn": (((0,), (0,)), ((), ()))}


def _matmul(a, b, mode, out_dtype, tm, tn, tk, name, epi=None, extra=None, n_outer=False, dep=None):
    if mode == "tn":
        K, M = a.shape
    else:
        M, K = a.shape
    N = b.shape[0] if mode == "nt" else b.shape[1]
    tm, tn, tk = min(tm, M), min(tn, N), min(tk, K)
    assert M % tm == 0 and N % tn == 0 and K % tk == 0, (name, M, N, K, tm, tn, tk)
    nm, nn, nk = M // tm, N // tn, K // tk
    dims = _DIMS[mode]

    def ij(p0, p1):
        return (p1, p0) if n_outer else (p0, p1)

    if mode == "tn":
        a_spec = pl.BlockSpec((tk, tm), lambda p0, p1, k: (k, ij(p0, p1)[0]))
    else:
        a_spec = pl.BlockSpec((tm, tk), lambda p0, p1, k: (ij(p0, p1)[0], k))
    if mode == "nt":
        b_spec = pl.BlockSpec((tn, tk), lambda p0, p1, k: (ij(p0, p1)[1], k))
    else:
        b_spec = pl.BlockSpec((tk, tn), lambda p0, p1, k: (k, ij(p0, p1)[1]))
    o_spec = pl.BlockSpec((tm, tn), lambda p0, p1, k: ij(p0, p1))
    in_specs = [a_spec, b_spec]
    args = [a, b]
    if epi in ("res", "drelu"):
        in_specs.append(o_spec)
        args.append(extra)
    if dep is not None:
        in_specs.append(pl.BlockSpec(memory_space=pl.ANY))
        args.append(dep)
    n_in = len(args)
    if epi == "relu2":
        out_shape = (jax.ShapeDtypeStruct((M, N), out_dtype), jax.ShapeDtypeStruct((M, N), BF16))
        out_specs = (o_spec, o_spec)
    else:
        out_shape = jax.ShapeDtypeStruct((M, N), out_dtype)
        out_specs = o_spec

    def kern(*refs):
        a_ref, b_ref = refs[0], refs[1]
        e_ref = refs[2] if epi in ("res", "drelu") else None
        acc = refs[-1]
        outs = refs[n_in:-1] if nk > 1 else refs[n_in:]
        k = pl.program_id(2)

        def product():
            return lax.dot_general(a_ref[...].astype(BF16), b_ref[...].astype(BF16), dims, preferred_element_type=F32)

        def finish(r):
            if epi is None:
                outs[0][...] = r.astype(out_dtype)
            elif epi == "res":
                outs[0][...] = (r + e_ref[...]).astype(out_dtype)
            elif epi == "relu2":
                outs[0][...] = r.astype(out_dtype)
                t = jnp.maximum(r, 0.0)
                outs[1][...] = (t * t).astype(BF16)
            else:
                outs[0][...] = (r * (2.0 * jnp.maximum(e_ref[...].astype(F32), 0.0))).astype(out_dtype)

        if nk == 1:
            finish(product())
        else:
            @pl.when(k == 0)
            def _():
                acc[...] = jnp.zeros_like(acc)

            acc[...] += product()

            @pl.when(k == nk - 1)
            def _():
                finish(acc[...])

    grid = (nn, nm, nk) if n_outer else (nm, nn, nk)
    return pl.pallas_call(
        kern, grid=grid, in_specs=in_specs, out_specs=out_specs, out_shape=out_shape,
        scratch_shapes=[pltpu.VMEM((tm, tn), F32)] if nk > 1 else [], name=name,
        compiler_params=_cp(("parallel", "parallel", "arbitrary")),
    )(*args)


def _matmul_res_rms(a, b, x, w, tm, name):
    M, K = a.shape
    tm = min(tm, M)

    def kern(a_ref, b_ref, x_ref, w_ref, x1_ref, h_ref):
        x1 = x_ref[...] + lax.dot_general(a_ref[...].astype(BF16), b_ref[...].astype(BF16), _DIMS["nn"],
                                          preferred_element_type=F32)
        x1_ref[...] = x1
        r = lax.rsqrt(jnp.mean(x1 * x1, axis=-1, keepdims=True) + EPS)
        h_ref[...] = ((x1 * r) * w_ref[...]).astype(BF16)

    row = pl.BlockSpec((tm, D), lambda i: (i, 0))
    return pl.pallas_call(
        kern, grid=(M // tm,),
        in_specs=[pl.BlockSpec((tm, K), lambda i: (i, 0)), pl.BlockSpec((K, D), lambda i: (0, 0)), row,
                  pl.BlockSpec((1, D), lambda i: (0, 0))],
        out_specs=(row, row), out_shape=(jax.ShapeDtypeStruct((M, D), F32), jax.ShapeDtypeStruct((M, D), BF16)),
        name=name, compiler_params=_cp(("parallel",)),
    )(a, b, x, w.reshape(1, D))


def _matmul_res_final(a, b, x, w, tgt, tm, name):
    M, K = a.shape
    tm = min(tm, M)

    def kern(a_ref, b_ref, x_ref, w_ref, t_ref, dx_ref, gw_ref, loss_ref):
        @pl.when(pl.program_id(0) == 0)
        def _():
            gw_ref[...] = jnp.zeros_like(gw_ref)
            loss_ref[...] = jnp.zeros_like(loss_ref)

        xv = x_ref[...] + lax.dot_general(a_ref[...].astype(BF16), b_ref[...].astype(BF16), _DIMS["nn"],
                                          preferred_element_type=F32)
        r = lax.rsqrt(jnp.mean(xv * xv, axis=-1, keepdims=True) + EPS)
        xn = xv * r
        e = xn * w_ref[...] - t_ref[...]
        loss_ref[...] += 0.5 * jnp.sum(jnp.mean(e * e, axis=-1, keepdims=True))
        dyv = e * (1.0 / D)
        gw_ref[...] += jnp.broadcast_to(jnp.sum(dyv * xn, axis=0, keepdims=True), (8, D))
        dxn = dyv * w_ref[...]
        dx_ref[...] = r * (dxn - xn * jnp.mean(dxn * xn, axis=-1, keepdims=True))

    row = pl.BlockSpec((tm, D), lambda i: (i, 0))
    return pl.pallas_call(
        kern, grid=(M // tm,),
        in_specs=[pl.BlockSpec((tm, K), lambda i: (i, 0)), pl.BlockSpec((K, D), lambda i: (0, 0)), row,
                  pl.BlockSpec((1, D), lambda i: (0, 0)), row],
        out_specs=(row, pl.BlockSpec((8, D), lambda i: (0, 0)), pl.BlockSpec((8, 128), lambda i: (0, 0))),
        out_shape=(jax.ShapeDtypeStruct((M, D), F32), jax.ShapeDtypeStruct((8, D), F32),
                   jax.ShapeDtypeStruct((8, 128), F32)),
        name=name, compiler_params=_cp(("arbitrary",)),
    )(a, b, x, w.reshape(1, D), tgt)


def _matmul_rms_bwd(a, b, mode, x, w, res, tm, tk, name, dep=None):
    M, K = a.shape
    tm, tk = min(tm, M), min(tk, K)
    nk = K // tk
    assert M % tm == 0 and K % tk == 0
    b_spec = (pl.BlockSpec((tk, D), lambda i, k: (k, 0)) if mode == "nn" else pl.BlockSpec((D, tk), lambda i, k: (0, k)))
    row = pl.BlockSpec((tm, D), lambda i, k: (i, 0))
    deps = [] if dep is None else [dep]

    def kern(a_ref, b_ref, x_ref, w_ref, res_ref, *rest):
        dx_ref, gw_ref, acc = rest[-3:]
        i, k = pl.program_id(0), pl.program_id(1)

        @pl.when((i == 0) & (k == 0))
        def _():
            gw_ref[...] = jnp.zeros_like(gw_ref)

        def product():
            return lax.dot_general(a_ref[...].astype(BF16), b_ref[...].astype(BF16), _DIMS[mode],
                                   preferred_element_type=F32)

        def finish(dyv):
            xv = x_ref[...]
            r = lax.rsqrt(jnp.mean(xv * xv, axis=-1, keepdims=True) + EPS)
            xn = xv * r
            gw_ref[...] += jnp.broadcast_to(jnp.sum(dyv * xn, axis=0, keepdims=True), (8, D))
            dxn = dyv * w_ref[...]
            dx_ref[...] = res_ref[...] + r * (dxn - xn * jnp.mean(dxn * xn, axis=-1, keepdims=True))

        if nk == 1:
            finish(product())
        else:
            @pl.when(k == 0)
            def _():
                acc[...] = jnp.zeros_like(acc)

            acc[...] += product()

            @pl.when(k == nk - 1)
            def _():
                finish(acc[...])

    return pl.pallas_call(
        kern, grid=(M // tm, nk),
        in_specs=[pl.BlockSpec((tm, tk), lambda i, k: (i, k)), b_spec, row, pl.BlockSpec((1, D), lambda i, k: (0, 0)),
                  row] + [pl.BlockSpec(memory_space=pl.ANY)] * len(deps),
        out_specs=(row, pl.BlockSpec((8, D), lambda i, k: (0, 0))),
        out_shape=(jax.ShapeDtypeStruct((M, D), F32), jax.ShapeDtypeStruct((8, D), F32)),
        scratch_shapes=[pltpu.VMEM((tm, D), F32)], name=name, compiler_params=_cp(("arbitrary", "arbitrary")),
    )(a, b, x, w.reshape(1, D), res, *deps)


def _rms_fwd(x, w, name, tl=256, dep=None):
    L = x.shape[0]

    def kern(x_ref, w_ref, *rest):
        o_ref = rest[-1]
        xv = x_ref[...]
        r = lax.rsqrt(jnp.mean(xv * xv, axis=-1, keepdims=True) + EPS)
        o_ref[...] = ((xv * r) * w_ref[...]).astype(BF16)

    row = pl.BlockSpec((tl, D), lambda i: (i, 0))
    deps = [] if dep is None else [dep]
    return pl.pallas_call(
        kern, grid=(L // tl,),
        in_specs=[row, pl.BlockSpec((1, D), lambda i: (0, 0))] + [pl.BlockSpec(memory_space=pl.ANY)] * len(deps),
        out_specs=row, out_shape=jax.ShapeDtypeStruct((L, D), BF16), name=name, compiler_params=_cp(("parallel",)),
    )(x, w.reshape(1, D), *deps)


def _rms_bwd(dy, x, w, res, name, tl=256, dep=None):
    L = x.shape[0]
    deps = [] if dep is None else [dep]

    def kern(dy_ref, x_ref, w_ref, res_ref, *rest):
        dx_ref, gw_ref = rest[-2:]
        @pl.when(pl.program_id(0) == 0)
        def _():
            gw_ref[...] = jnp.zeros_like(gw_ref)

        xv = x_ref[...]
        dyv = dy_ref[...]
        r = lax.rsqrt(jnp.mean(xv * xv, axis=-1, keepdims=True) + EPS)
        xn = xv * r
        gw_ref[...] += jnp.broadcast_to(jnp.sum(dyv * xn, axis=0, keepdims=True), (8, D))
        dxn = dyv * w_ref[...]
        dx_ref[...] = res_ref[...] + r * (dxn - xn * jnp.mean(dxn * xn, axis=-1, keepdims=True))

    row = pl.BlockSpec((tl, D), lambda i: (i, 0))
    return pl.pallas_call(
        kern, grid=(L // tl,),
        in_specs=[row, row, pl.BlockSpec((1, D), lambda i: (0, 0)), row] + [pl.BlockSpec(memory_space=pl.ANY)] * len(deps),
        out_specs=(row, pl.BlockSpec((8, D), lambda i: (0, 0))),
        out_shape=(jax.ShapeDtypeStruct((L, D), F32), jax.ShapeDtypeStruct((8, D), F32)),
        name=name, compiler_params=_cp(("arbitrary",)),
    )(dy, x, w.reshape(1, D), res, *deps)


def _final(x2, w, tgt, tl=256):
    L = x2.shape[0]

    def kern(x_ref, w_ref, t_ref, dx_ref, gw_ref, loss_ref):
        @pl.when(pl.program_id(0) == 0)
        def _():
            gw_ref[...] = jnp.zeros_like(gw_ref)
            loss_ref[...] = jnp.zeros_like(loss_ref)

        xv = x_ref[...]
        r = lax.rsqrt(jnp.mean(xv * xv, axis=-1, keepdims=True) + EPS)
        xn = xv * r
        e = xn * w_ref[...] - t_ref[...]
        per_tok = jnp.mean(e * e, axis=-1, keepdims=True)
        loss_ref[...] += 0.5 * jnp.sum(per_tok)
        dyv = e * (1.0 / D)
        gw_ref[...] += jnp.broadcast_to(jnp.sum(dyv * xn, axis=0, keepdims=True), (8, D))
        dxn = dyv * w_ref[...]
        dx_ref[...] = r * (dxn - xn * jnp.mean(dxn * xn, axis=-1, keepdims=True))

    row = pl.BlockSpec((tl, D), lambda i: (i, 0))
    return pl.pallas_call(
        kern, grid=(L // tl,), in_specs=[row, pl.BlockSpec((1, D), lambda i: (0, 0)), row],
        out_specs=(row, pl.BlockSpec((8, D), lambda i: (0, 0)), pl.BlockSpec((8, 128), lambda i: (0, 0))),
        out_shape=(jax.ShapeDtypeStruct((L, D), F32), jax.ShapeDtypeStruct((8, D), F32),
                   jax.ShapeDtypeStruct((8, 128), F32)),
        name="final_norm_loss", compiler_params=_cp(("arbitrary",)),
    )(x2, w.reshape(1, D), tgt)


def _down(v, k):
    if k == 0:
        return v
    t = lax.broadcasted_iota(jnp.int32, v.shape, 0)
    return jnp.where(t >= k, pltpu.roll(v, k, axis=0), 0.0)


def _up(v, k):
    if k == 0:
        return v
    n = v.shape[0]
    t = lax.broadcasted_iota(jnp.int32, v.shape, 0)
    return jnp.where(t < n - k, pltpu.roll(v, n - k, axis=0), 0.0)


TW = 256


def _sc_fwd(proj, cw):
    L = proj.shape[0]
    nb = D // TW

    def kern(b_ref, c_ref, x_ref, w_ref, o_ref):
        u = c_ref[...].astype(F32) * x_ref[...].astype(F32)
        w = w_ref[...]
        cv = w[0:1] * _down(u, 2) + w[1:2] * _down(u, 1) + w[2:3] * u
        o_ref[...] = (b_ref[...].astype(F32) * cv).astype(BF16)

    col = lambda off: pl.BlockSpec((L, TW), lambda j: (0, off + j))
    return pl.pallas_call(
        kern, grid=(nb,), in_specs=[col(0), col(nb), col(2 * nb), pl.BlockSpec((8, TW), lambda j: (0, j))],
        out_specs=pl.BlockSpec((L, TW), lambda j: (0, j)), out_shape=jax.ShapeDtypeStruct((L, D), BF16),
        name="sc_fwd", compiler_params=_cp(("parallel",)),
    )(proj, proj, proj, cw)


def _sc_bwd(dya, proj, cw, dproj):
    L = proj.shape[0]
    nb = D // TW

    def kern(d_ref, b_ref, c_ref, x_ref, w_ref, _, dp_ref, gw_ref, keep):
        sec = pl.program_id(1)

        @pl.when(sec == 0)
        def _():
            cs, xs, dyv = c_ref[...].astype(F32), x_ref[...].astype(F32), d_ref[...]
            w = w_ref[...]
            u = cs * xs
            u1, u2 = _down(u, 1), _down(u, 2)
            cv = w[0:1] * u2 + w[1:2] * u1 + w[2:3] * u
            dcv = dyv * b_ref[...].astype(F32)
            du = w[2:3] * dcv + w[1:2] * _up(dcv, 1) + w[0:1] * _up(dcv, 2)
            g0 = jnp.sum(dcv * u2, axis=0, keepdims=True)
            g1 = jnp.sum(dcv * u1, axis=0, keepdims=True)
            g2 = jnp.sum(dcv * u, axis=0, keepdims=True)
            row = lax.broadcasted_iota(jnp.int32, (8, TW), 0)
            gw_ref[...] = jnp.where(row == 0, g0, jnp.where(row == 1, g1, jnp.where(row == 2, g2, 0.0)))
            dp_ref[...] = (dyv * cv).astype(BF16)
            keep[0] = (du * xs).astype(BF16)
            keep[1] = (du * cs).astype(BF16)

        @pl.when(sec > 0)
        def _():
            dp_ref[...] = keep[sec - 1]

    col = lambda off: pl.BlockSpec((L, TW), lambda j, s: (0, off + j))
    return pl.pallas_call(
        kern, grid=(nb, 3),
        in_specs=[col(0), col(0), col(nb), col(2 * nb), pl.BlockSpec((8, TW), lambda j, s: (0, j)),
                  pl.BlockSpec(memory_space=pl.ANY)],
        out_specs=(pl.BlockSpec((L, TW), lambda j, s: (0, s * nb + j)), pl.BlockSpec((8, TW), lambda j, s: (0, j))),
        out_shape=(jax.ShapeDtypeStruct(dproj.shape, BF16), jax.ShapeDtypeStruct((8, D), F32)),
        scratch_shapes=[pltpu.VMEM((2, L, TW), BF16)],
        input_output_aliases={5: 0}, name="sc_bwd", compiler_params=_cp(("parallel", "arbitrary")),
    )(dya, proj, proj, proj, cw, dproj)


def _ssm_conv_fwd(proj, cw4):
    L = proj.shape[0]
    off = C_XBC // TW

    def kern(r_ref, w_ref, o_ref):
        raw = r_ref[...].astype(F32)
        w = w_ref[...]
        c4 = w[0:1] * _down(raw, 3) + w[1:2] * _down(raw, 2) + w[2:3] * _down(raw, 1) + w[3:4] * raw + w[4:5]
        o_ref[...] = c4 * _sigmoid(c4)

    return pl.pallas_call(
        kern, grid=(XBC // TW,),
        in_specs=[pl.BlockSpec((L, TW), lambda j: (0, off + j)), pl.BlockSpec((8, TW), lambda j: (0, j))],
        out_specs=pl.BlockSpec((L, TW), lambda j: (0, j)), out_shape=jax.ShapeDtypeStruct((L, XBC), F32),
        name="ssm_conv_fwd", compiler_params=_cp(("parallel",)),
    )(proj, cw4)


def _ssm_conv_bwd(dx, proj, cw4, dproj, col0, name):
    L, width = dx.shape
    off_p = (C_XBC + col0) // TW
    off_w = col0 // TW

    def kern(d_ref, r_ref, w_ref, _, dp_ref, gw_ref):
        raw = r_ref[...].astype(F32)
        w = w_ref[...]
        r1, r2, r3 = _down(raw, 1), _down(raw, 2), _down(raw, 3)
        c4 = w[0:1] * r3 + w[1:2] * r2 + w[2:3] * r1 + w[3:4] * raw + w[4:5]
        sg = _sigmoid(c4)
        dc4 = d_ref[...] * (sg * (1.0 + c4 * (1.0 - sg)))
        draw = w[3:4] * dc4 + w[2:3] * _up(dc4, 1) + w[1:2] * _up(dc4, 2) + w[0:1] * _up(dc4, 3)
        dp_ref[...] = draw.astype(BF16)
        gs = [jnp.sum(dc4 * r3, axis=0, keepdims=True), jnp.sum(dc4 * r2, axis=0, keepdims=True),
              jnp.sum(dc4 * r1, axis=0, keepdims=True), jnp.sum(dc4 * raw, axis=0, keepdims=True),
              jnp.sum(dc4, axis=0, keepdims=True)]
        row = lax.broadcasted_iota(jnp.int32, (8, TW), 0)
        acc = jnp.zeros((8, TW), F32)
        for k, gk in enumerate(gs):
            acc = jnp.where(row == k, gk, acc)
        gw_ref[...] = acc

    return pl.pallas_call(
        kern, grid=(width // TW,),
        in_specs=[pl.BlockSpec((L, TW), lambda j: (0, j)), pl.BlockSpec((L, TW), lambda j: (0, off_p + j)),
                  pl.BlockSpec((8, TW), lambda j: (0, off_w + j)), pl.BlockSpec(memory_space=pl.ANY)],
        out_specs=(pl.BlockSpec((L, TW), lambda j: (0, off_p + j)), pl.BlockSpec((8, TW), lambda j: (0, j))),
        out_shape=(jax.ShapeDtypeStruct(dproj.shape, BF16), jax.ShapeDtypeStruct((8, width), F32)),
        input_output_aliases={3: 0}, name=name, compiler_params=_cp(("arbitrary",)),
    )(dx, proj, cw4, dproj)


def _split3(v):
    h1 = v.astype(BF16)
    r1 = v - h1.astype(F32)
    h2 = r1.astype(BF16)
    h3 = (r1 - h2.astype(F32)).astype(BF16)
    return h1, h2, h3


def _dot01(m01, v, dims=_DIMS["nn"], m_left=True, terms=3):
    out = None
    for part in _split3(v)[:terms]:
        ops = (m01, part) if m_left else (part, m01)
        t = lax.dot_general(ops[0], ops[1], dims, preferred_element_type=F32)
        out = t if out is None else out + t
    return out


def _bdot(a, b, mode="nn"):
    return lax.dot_general(a.astype(BF16), b.astype(BF16), _DIMS[mode], preferred_element_type=F32)


def _softplus(v):
    return jnp.maximum(v, 0.0) + jnp.log1p(jnp.exp(-jnp.abs(v)))


def _dt_prep(proj, vec):
    L = proj.shape[0]

    def kern(p_ref, v_ref, dt_ref, cs_ref, sg_ref):
        v = v_ref[...]
        pre = p_ref[:, 0:128] + v[0:1]
        dt = _softplus(pre)
        da = dt * (-jnp.exp(v[1:2]))
        ii = lax.broadcasted_iota(jnp.int32, (Q, Q), 0)
        jj = lax.broadcasted_iota(jnp.int32, (Q, Q), 1)
        ltri = (jj <= ii).astype(BF16)
        lane = lax.broadcasted_iota(jnp.int32, (Q, 128), 1)
        for val, ref in ((dt, dt_ref), (_dot01(ltri, da), cs_ref), (_sigmoid(pre), sg_ref)):
            for g in range(NG):
                moved = val if g == 0 else pltpu.roll(val, 128 - 4 * g, axis=1)
                ref[g] = jnp.where(lane < 4, moved, 0.0)

    blk = pl.BlockSpec((NG, Q, 128), lambda c: (0, c, 0))
    return pl.pallas_call(
        kern, grid=(L // Q,),
        in_specs=[pl.BlockSpec((Q, 256), lambda c: (c, 0)), pl.BlockSpec((8, 128), lambda c: (0, 0))],
        out_specs=(blk, blk, blk),
        out_shape=(jax.ShapeDtypeStruct((NG, L, 128), F32),) * 3,
        name="dt_prep", compiler_params=_cp(("parallel",)),
    )(proj, vec)


def _head_masks():
    lane = lax.broadcasted_iota(jnp.int32, (1, 4 * HD), 1)
    return [((lane >= HD * j) & (lane < HD * (j + 1))) for j in range(4)]


def _expand4(v4, masks):
    R = v4.shape[0]
    out = jnp.zeros((R, 4 * HD), F32)
    for j in range(4):
        out = jnp.where(masks[j], jnp.broadcast_to(v4[:, j:j + 1], (R, 4 * HD)), out)
    return out


def _decay_matrix(cs_col, tri):
    colb = jnp.broadcast_to(cs_col, (Q, Q))
    return jnp.exp(jnp.where(tri, colb - colb.T, -jnp.inf))


def _ssd_fwd(xbc, dt4, cs4, vecg):
    L = xbc.shape[0]
    nc = L // Q

    def kern(x_ref, b_ref, c_ref, dt_ref, cs_ref, v_ref, y_ref, s_ref, S):
        c = pl.program_id(1)

        @pl.when(c == 0)
        def _():
            S[...] = jnp.zeros_like(S)

        masks = _head_masks()
        ii = lax.broadcasted_iota(jnp.int32, (Q, Q), 0)
        jj = lax.broadcasted_iota(jnp.int32, (Q, Q), 1)
        tri = jj <= ii
        for gi in range(GPS):
            xs, ns = slice(256 * gi, 256 * (gi + 1)), slice(NS * gi, NS * (gi + 1))
            dt4v, cs4v = dt_ref[gi], cs_ref[gi]
            dt_b, cs_b = _expand4(dt4v, masks), _expand4(cs4v, masks)
            d_b = _expand4(v_ref[gi], masks)[1:2]
            cs_last = cs_b[Q - 1:Q, :]
            x4, bm, cm = x_ref[:, xs], b_ref[:, ns], c_ref[:, ns]
            xdt = x4 * dt_b
            gm = _bdot(cm, bm, "nt")
            s4 = S[gi]
            s_ref[gi, 0] = s4
            y = _bdot(cm, s4) * jnp.exp(cs_b) + d_b * x4
            m_all = jnp.concatenate([(gm * _decay_matrix(cs4v[:, j:j + 1], tri)).astype(BF16) for j in range(4)], axis=0)
            yd = _bdot(m_all, xdt)
            for j in range(4):
                y = y + jnp.where(masks[j], yd[Q * j:Q * (j + 1)], 0.0)
            y_ref[:, xs] = y
            S[gi] = jnp.exp(cs_last) * s4 + _bdot(bm, xdt * jnp.exp(cs_last - cs_b), "tn")

    sc = pl.BlockSpec((GPS, Q, 128), lambda g, c: (g, c, 0))
    bw = NS * GPS
    return pl.pallas_call(
        kern, grid=(NG // GPS, nc),
        in_specs=[pl.BlockSpec((Q, 256 * GPS), lambda g, c: (c, g)),
                  pl.BlockSpec((Q, bw), lambda g, c: (c, INNER // bw + g)),
                  pl.BlockSpec((Q, bw), lambda g, c: (c, (INNER + NG * NS) // bw + g)),
                  sc, sc, pl.BlockSpec((GPS, 8, 128), lambda g, c: (g, 0, 0))],
        out_specs=(pl.BlockSpec((Q, 256 * GPS), lambda g, c: (c, g)),
                   pl.BlockSpec((GPS, 1, NS, 256), lambda g, c: (g, c, 0, 0))),
        out_shape=(jax.ShapeDtypeStruct((L, INNER), F32), jax.ShapeDtypeStruct((NG, nc, NS, 256), F32)),
        scratch_shapes=[pltpu.VMEM((GPS, NS, 256), F32)], name="ssd_fwd",
        compiler_params=_cp(("parallel", "arbitrary")),
    )(xbc, xbc, xbc, dt4, cs4, vecg)


def _ssd_bwd(xbc, dt4, cs4, sg4, vecg, s_all, dy):
    L = xbc.shape[0]
    nc = L // Q

    def kern(x_ref, b_ref, c_ref, dt_ref, cs_ref, sg_ref, v_ref, s_ref, dy_ref,
             dx_ref, db_ref, dc_ref, ddt_ref, st_ref, dS):
        cc = pl.program_id(1)

        @pl.when(cc == 0)
        def _():
            dS[...] = jnp.zeros_like(dS)
            st_ref[...] = jnp.zeros_like(st_ref)

        masks = _head_masks()
        ii = lax.broadcasted_iota(jnp.int32, (Q, Q), 0)
        jj = lax.broadcasted_iota(jnp.int32, (Q, Q), 1)
        tri = jj <= ii
        utri = (jj >= ii).astype(BF16)
        hsel = ((lax.broadcasted_iota(jnp.int32, (4 * HD, 128), 0) // HD)
                == lax.broadcasted_iota(jnp.int32, (4 * HD, 128), 1)).astype(BF16)
        hrow = ((lax.broadcasted_iota(jnp.int32, (4 * Q, 128), 0) // Q)
                == lax.broadcasted_iota(jnp.int32, (4 * Q, 128), 1)).astype(BF16)
        ones_q = jnp.ones((Q, 128), BF16)
        lane128 = lax.broadcasted_iota(jnp.int32, (Q, 128), 1)

        for gi in range(GPS):
            xs, ns = slice(256 * gi, 256 * (gi + 1)), slice(NS * gi, NS * (gi + 1))
            dt4v, cs4v, sg4v = dt_ref[gi], cs_ref[gi], sg_ref[gi]
            dt_b, cs_b = _expand4(dt4v, masks), _expand4(cs4v, masks)
            vv = _expand4(v_ref[gi], masks)
            a_b = -jnp.exp(vv[0:1])
            d_b = vv[1:2]
            a4 = -jnp.exp(v_ref[gi][0:1, :])
            cs_last = cs_b[Q - 1:Q, :]
            ecs = jnp.exp(cs_b)
            decay = jnp.exp(cs_last - cs_b)
            elast = jnp.exp(cs_last)
            x4, bm, cm, dyv = x_ref[:, xs], b_ref[:, ns], c_ref[:, ns], dy_ref[:, xs]
            s4 = s_ref[gi, 0]
            dsn = dS[gi]
            xdt = x4 * dt_b
            gm = _bdot(cm, bm, "nt")
            dye = dyv * ecs
            yoff = ecs * _bdot(cm, s4)
            t4 = _bdot(bm, dsn) * decay
            lms, mhs = [], []
            for j in range(4):
                colb = jnp.broadcast_to(cs4v[:, j:j + 1], (Q, Q))
                lms.append(jnp.exp(jnp.where(tri, colb - colb.T, -jnp.inf)))
                mhs.append(gm * lms[j])
            m_all = jnp.concatenate([m.astype(BF16) for m in mhs], axis=0)
            dy_m = jnp.concatenate([jnp.where(masks[j], dyv, 0.0).astype(BF16) for j in range(4)], axis=0)
            dxdt = t4 + _bdot(m_all, dy_m, "tn")
            dm_all = _bdot(dy_m, xdt, "nt")
            dg = jnp.zeros((Q, Q), F32)
            for j in range(4):
                dg = dg + dm_all[Q * j:Q * (j + 1)] * lms[j]
            e_all = dm_all * jnp.concatenate(mhs, axis=0)
            rsum = _dot01(ones_q, e_all, m_left=False, terms=2)
            da4 = -_dot01(hrow, e_all, _DIMS["tn"], m_left=False, terms=2)
            for j in range(4):
                da4 = da4 + jnp.where(lane128 == j, rsum[Q * j:Q * (j + 1)], 0.0)
            xt = xdt * t4
            tail = jnp.sum(xt, axis=0, keepdims=True) + elast * jnp.sum(s4 * dsn, axis=0, keepdims=True)
            gd_raw = jnp.sum(dyv * x4, axis=0, keepdims=True)
            stacked = jnp.concatenate([dyv * yoff - xt, dxdt * x4, jnp.broadcast_to(tail, (8, 4 * HD)),
                                       jnp.broadcast_to(gd_raw, (8, 4 * HD))], axis=0)
            seg = _dot01(hsel, stacked, m_left=False, terms=2)
            dda4 = _dot01(utri, da4 + seg[0:Q], terms=2) + seg[2 * Q:2 * Q + 1]
            ddt_ref[gi] = (dda4 * a4 + seg[Q:2 * Q]) * sg4v
            ga = jnp.sum(dda4 * dt4v * a4, axis=0, keepdims=True)
            row = lax.broadcasted_iota(jnp.int32, (8, 128), 0)
            st_ref[gi] += jnp.where(row == 0, ga, jnp.where(row == 1, seg[2 * Q + 8:2 * Q + 9], 0.0))
            dx_ref[:, xs] = d_b * dyv + dxdt * dt_b
            dc_ref[:, ns] = _bdot(dg, bm) + _bdot(dye, s4, "nt")
            db_ref[:, ns] = _bdot(dg, cm, "tn") + _bdot(xdt * decay, dsn, "nt")
            dS[gi] = elast * dsn + _bdot(cm, dye, "tn")

    rv = lambda c: nc - 1 - c
    sc = pl.BlockSpec((GPS, Q, 128), lambda g, c: (g, rv(c), 0))
    bw = NS * GPS
    return pl.pallas_call(
        kern, grid=(NG // GPS, nc),
        in_specs=[pl.BlockSpec((Q, 256 * GPS), lambda g, c: (rv(c), g)),
                  pl.BlockSpec((Q, bw), lambda g, c: (rv(c), INNER // bw + g)),
                  pl.BlockSpec((Q, bw), lambda g, c: (rv(c), (INNER + NG * NS) // bw + g)),
                  sc, sc, sc, pl.BlockSpec((GPS, 8, 128), lambda g, c: (g, 0, 0)),
                  pl.BlockSpec((GPS, 1, NS, 256), lambda g, c: (g, rv(c), 0, 0)),
                  pl.BlockSpec((Q, 256 * GPS), lambda g, c: (rv(c), g))],
        out_specs=(pl.BlockSpec((Q, 256 * GPS), lambda g, c: (rv(c), g)),
                   pl.BlockSpec((Q, bw), lambda g, c: (rv(c), g)),
                   pl.BlockSpec((Q, bw), lambda g, c: (rv(c), g)),
                   pl.BlockSpec((GPS, Q, 128), lambda g, c: (g, rv(c), 0)),
                   pl.BlockSpec((GPS, 8, 128), lambda g, c: (g, 0, 0))),
        out_shape=(jax.ShapeDtypeStruct((L, INNER), F32), jax.ShapeDtypeStruct((L, NG * NS), F32),
                   jax.ShapeDtypeStruct((L, NG * NS), F32), jax.ShapeDtypeStruct((NG, L, 128), F32),
                   jax.ShapeDtypeStruct((NG, 8, 128), F32)),
        scratch_shapes=[pltpu.VMEM((GPS, NS, 256), F32)], name="ssd_bwd",
        compiler_params=_cp(("parallel", "arbitrary")),
    )(xbc, xbc, xbc, dt4, cs4, sg4, vecg, s_all, dy)


def _dt_bwd(ddt, dproj, tl=256):
    L = ddt.shape[1]

    def kern(d_ref, _, dp_ref, gs_ref):
        @pl.when(pl.program_id(0) == 0)
        def _():
            gs_ref[...] = jnp.zeros_like(gs_ref)

        d = d_ref[0]
        for g in range(1, NG):
            d = d + pltpu.roll(d_ref[g], 4 * g, axis=1)
        gs_ref[...] += jnp.broadcast_to(jnp.sum(d, axis=0, keepdims=True), (8, 128))
        dp_ref[...] = jnp.concatenate([d, jnp.zeros_like(d)], axis=1).astype(BF16)

    return pl.pallas_call(
        kern, grid=(L // tl,),
        in_specs=[pl.BlockSpec((NG, tl, 128), lambda i: (0, i, 0)), pl.BlockSpec(memory_space=pl.ANY)],
        out_specs=(pl.BlockSpec((tl, 256), lambda i: (i, C_DT // 256)), pl.BlockSpec((8, 128), lambda i: (0, 0))),
        out_shape=(jax.ShapeDtypeStruct(dproj.shape, BF16), jax.ShapeDtypeStruct((8, 128), F32)),
        input_output_aliases={1: 0}, name="dt_bwd", compiler_params=_cp(("arbitrary",)),
    )(ddt, dproj)


GW = INNER // NG


def _gnorm_fwd(y, proj, w, tl=256):
    L = y.shape[0]
    zoff = C_Z // 1024

    def kern(y_ref, z_ref, w_ref, o_ref):
        z = z_ref[...].astype(F32)
        yz = y_ref[...] * (z * _sigmoid(z))
        wv = w_ref[...]
        for k in range(1024 // GW):
            sl = slice(GW * k, GW * (k + 1))
            v = yz[:, sl]
            rg = lax.rsqrt(jnp.mean(v * v, axis=-1, keepdims=True) + EPS)
            o_ref[:, sl] = ((v * rg) * wv[:, sl]).astype(BF16)

    blk = pl.BlockSpec((tl, 1024), lambda i, j: (i, j))
    return pl.pallas_call(
        kern, grid=(L // tl, 2),
        in_specs=[blk, pl.BlockSpec((tl, 1024), lambda i, j: (i, zoff + j)), pl.BlockSpec((1, 1024), lambda i, j: (0, j))],
        out_specs=blk, out_shape=jax.ShapeDtypeStruct((L, INNER), BF16), name="gnorm_fwd",
        compiler_params=_cp(("parallel", "parallel")),
    )(y, proj, w.reshape(1, INNER))


def _gnorm_bwd(dbr, wb, y, proj, w, dproj, dep, tl=512):
    L = y.shape[0]
    tl = min(tl, L)
    zoff = C_Z // 1024

    def kern(d_ref, b_ref, y_ref, z_ref, w_ref, _, __, dy_ref, dp_ref, gw_ref):
        @pl.when(pl.program_id(1) == 0)
        def _():
            gw_ref[...] = jnp.zeros_like(gw_ref)

        z = z_ref[...].astype(F32)
        sg = _sigmoid(z)
        sz = z * sg
        yv = y_ref[...]
        yz = yv * sz
        dv = lax.dot_general(d_ref[0], b_ref[...], _DIMS["nt"], preferred_element_type=F32)
        wv = w_ref[...]
        for k in range(1024 // GW):
            sl = slice(GW * k, GW * (k + 1))
            v = yz[:, sl]
            rg = lax.rsqrt(jnp.mean(v * v, axis=-1, keepdims=True) + EPS)
            vn = v * rg
            dk = dv[:, sl]
            gw_ref[:, sl] += jnp.broadcast_to(jnp.sum(dk * vn, axis=0, keepdims=True), (8, GW))
            dvn = dk * wv[:, sl]
            dyz = rg * (dvn - vn * jnp.mean(dvn * vn, axis=-1, keepdims=True))
            dy_ref[:, sl] = dyz * sz[:, sl]
            dp_ref[:, sl] = (dyz * yv[:, sl] * (sg[:, sl] * (1.0 + z[:, sl] * (1.0 - sg[:, sl])))).astype(BF16)

    blk = pl.BlockSpec((tl, 1024), lambda j, i: (i, j))
    zblk = pl.BlockSpec((tl, 1024), lambda j, i: (i, zoff + j))
    return pl.pallas_call(
        kern, grid=(2, L // tl),
        in_specs=[pl.BlockSpec((1, tl, D), lambda j, i: (1, i, 0)), pl.BlockSpec((1024, D), lambda j, i: (j, 0)),
                  blk, zblk, pl.BlockSpec((1, 1024), lambda j, i: (0, j)), pl.BlockSpec(memory_space=pl.ANY),
                  pl.BlockSpec(memory_space=pl.ANY)],
        out_specs=(blk, zblk, pl.BlockSpec((8, 1024), lambda j, i: (0, j))),
        out_shape=(jax.ShapeDtypeStruct((L, INNER), F32), jax.ShapeDtypeStruct(dproj.shape, BF16),
                   jax.ShapeDtypeStruct((8, INNER), F32)),
        input_output_aliases={5: 1}, name="gnorm_bwd", compiler_params=_cp(("parallel", "arbitrary")),
    )(dbr, wb, y, proj, w.reshape(1, INNER), dproj, dep)


def _merge_fwd(proj, bg, br_a, br_b, tl=256):
    L = proj.shape[0]
    goff = C_GATE // 1024

    def kern(g1_ref, g2_ref, b1_ref, b2_ref, a_ref, b_ref, o_ref):
        g1 = _sigmoid(g1_ref[...].astype(F32) + b1_ref[...])
        g2 = _sigmoid(g2_ref[...].astype(F32) + b2_ref[...])
        o_ref[...] = (g1 * a_ref[...] + g2 * b_ref[...]).astype(BF16)

    row = pl.BlockSpec((tl, 1024), lambda i: (i, 0))
    bg2 = bg.reshape(1, 2 * D)
    return pl.pallas_call(
        kern, grid=(L // tl,),
        in_specs=[pl.BlockSpec((tl, 1024), lambda i: (i, goff)), pl.BlockSpec((tl, 1024), lambda i: (i, goff + 1)),
                  pl.BlockSpec((1, 1024), lambda i: (0, 0)), pl.BlockSpec((1, 1024), lambda i: (0, 1)), row, row],
        out_specs=row, out_shape=jax.ShapeDtypeStruct((L, D), BF16), name="merge_fwd",
        compiler_params=_cp(("parallel",)),
    )(proj, proj, bg2, bg2, br_a, br_b)


def _merge_bwd(dm, proj, bg, br_a, br_b, dproj, tl=256):
    L = proj.shape[0]
    goff = C_GATE // 1024

    def kern(dm_ref, g_ref, b_ref, a_ref, bb_ref, _, dbr_ref, dp_ref, gb_ref):
        j = pl.program_id(0)

        @pl.when(pl.program_id(1) == 0)
        def _():
            gb_ref[...] = jnp.zeros_like(gb_ref)

        g = _sigmoid(g_ref[...].astype(F32) + b_ref[...])
        br = jnp.where(j == 0, a_ref[...], bb_ref[...])
        dmv = dm_ref[...]
        dbr_ref[0] = (dmv * g).astype(BF16)
        dgate = dmv * br * g * (1.0 - g)
        gb_ref[...] += jnp.broadcast_to(jnp.sum(dgate, axis=0, keepdims=True), (8, 1024))
        dp_ref[...] = dgate.astype(BF16)

    row = pl.BlockSpec((tl, 1024), lambda j, i: (i, 0))
    gblk = pl.BlockSpec((tl, 1024), lambda j, i: (i, goff + j))
    return pl.pallas_call(
        kern, grid=(2, L // tl),
        in_specs=[row, gblk, pl.BlockSpec((1, 1024), lambda j, i: (0, j)), row, row, pl.BlockSpec(memory_space=pl.ANY)],
        out_specs=(pl.BlockSpec((1, tl, 1024), lambda j, i: (j, i, 0)), gblk, pl.BlockSpec((8, 1024), lambda j, i: (0, j))),
        out_shape=(jax.ShapeDtypeStruct((2, L, D), BF16), jax.ShapeDtypeStruct(dproj.shape, BF16),
                   jax.ShapeDtypeStruct((8, 2 * D), F32)),
        input_output_aliases={5: 1}, name="merge_bwd", compiler_params=_cp(("parallel", "arbitrary")),
    )(dm, proj, bg.reshape(1, 2 * D), br_a, br_b, dproj)


def _coords():
    return lax.axis_index("x"), lax.axis_index("y"), lax.axis_index("c")


def _other_chips(sk):
    xk, yk = sk // 2, sk % 2
    return [((1 - xk, yk), 2 * (1 - xk) + yk), ((xk, 1 - yk), 2 * xk + 1 - yk), ((1 - xk, 1 - yk), 2 * (1 - xk) + 1 - yk)]


def _rows(start, size):
    assert size % 128 == 0
    return pl.ds(pl.multiple_of(start, 128), size)


def _per_chip(fn):
    x, y, _ = _coords()
    s = 2 * x + y
    for sk in range(4):
        pl.when(s == sk)(functools.partial(fn, sk))


XTRA = PIECE - PMAIN


def _place(shard, full_shape, block, index_map, idx, name, blk0=0, nblk=None, dep=None, into=None):
    in_block = block[-2:]
    if nblk is None:
        nblk = shard.shape[0] // in_block[0]

    def kern(idx_ref, s_ref, *rest):
        o_ref = rest[-1]
        o_ref[...] = s_ref[...].astype(BF16).reshape(o_ref.shape)

    extra = ([dep] if dep is not None else []) + ([into] if into is not None else [])
    grid_spec = pltpu.PrefetchScalarGridSpec(
        num_scalar_prefetch=1, grid=(nblk,),
        in_specs=[pl.BlockSpec(in_block, lambda i, idx_ref: (blk0 + i, 0))] + [_ANY] * len(extra),
        out_specs=pl.BlockSpec(block, index_map))
    aliases = {1 + len(extra): 0} if into is not None else {}
    return pl.pallas_call(kern, grid_spec=grid_spec, out_shape=jax.ShapeDtypeStruct(full_shape, BF16), name=name,
                          input_output_aliases=aliases, compiler_params=_cp(("arbitrary",)))(idx, shard, *extra)


_SEM = pl.BlockSpec(memory_space=pltpu.SEMAPHORE)
_EFFECT = pltpu.SideEffectType.DATAFLOW_SIDE_EFFECTING


_ANY = pl.BlockSpec(memory_space=pl.ANY)


def _tie(v, dep, name):
    def body(v_ref, dep_ref, o_ref):
        del v_ref, dep_ref, o_ref

    return pl.pallas_call(body, out_shape=jax.ShapeDtypeStruct(v.shape, v.dtype), in_specs=[_ANY, _ANY],
                          out_specs=_ANY, input_output_aliases={0: 0}, name=name)(v, dep)


def _split_call(name, arrays, start=None, wait=None, wait_sems=None, after=None):
    keys = list(arrays)
    n = len(keys)
    n_start = start.n if start is not None else 0
    afters = [] if after is None else (list(after) if isinstance(after, (list, tuple)) else [after])

    def body(*refs):
        pos = n
        if wait is not None:
            wss, wrs = refs[pos], refs[pos + 1]
            pos += 2
        pos += len(afters)
        if start is not None:
            nss, nrs = refs[pos], refs[pos + 1]
            pos += 2
        R = dict(zip(keys, refs[pos:pos + n]))
        token = refs[pos + n]
        x, y, c = _coords()

        def desc(src, dst, dev, ss, rs, k):
            return pltpu.make_async_remote_copy(src_ref=src, dst_ref=dst, send_sem=ss.at[k], recv_sem=rs.at[k],
                                                device_id=dev, device_id_type=MESH)

        def run(sk):
            if wait is not None:
                for k, (snd, land) in enumerate(wait.copies(sk, R)):
                    if snd is not None:
                        desc(snd[0], snd[1], snd[2], wss, wrs, k).wait_send()
                    if land is not None:
                        desc(land, land, (x, y, c), wss, wrs, k).wait_recv()
            if start is not None:
                for k, (snd, land) in enumerate(start.copies(sk, R)):
                    if snd is not None:
                        desc(snd[0], snd[1], snd[2], nss, nrs, k).start()

        _per_chip(run)
        token[...] = jnp.zeros_like(token)

    hbm = pl.BlockSpec(memory_space=HBM)
    vals = [arrays[k] for k in keys]
    ins, in_specs = list(vals), [hbm] * n
    if wait is not None:
        ins += list(wait_sems)
        in_specs += [_SEM, _SEM]
    ins += afters
    in_specs += [pl.BlockSpec(memory_space=pl.ANY)] * len(afters)
    out_shape, out_specs = [], []
    if start is not None:
        out_shape += [pltpu.SemaphoreType.DMA((n_start,)), pltpu.SemaphoreType.DMA((n_start,))]
        out_specs += [_SEM, _SEM]
    first = len(out_shape)
    out_shape += [jax.ShapeDtypeStruct(v.shape, v.dtype) for v in vals] + [jax.ShapeDtypeStruct((8, 128), F32)]
    out_specs += [hbm] * n + [pl.BlockSpec(memory_space=pltpu.VMEM)]
    res = pl.pallas_call(
        body, out_shape=tuple(out_shape), in_specs=in_specs, out_specs=tuple(out_specs),
        input_output_aliases={i: first + i for i in range(n)}, name=name,
        compiler_params=pltpu.CompilerParams(has_side_effects=_EFFECT),
    )(*ins)
    sems = (res[0], res[1]) if start is not None else None
    return dict(zip(keys, res[first:first + n])), sems, res[-1]


class _Plan:
    def __init__(self, n, copies):
        self.n, self.copies = n, copies


_HM, _HX = PMAIN // 2, XTRA // 2
WAVE0 = 768
WAVES = ((0, WAVE0), (WAVE0, _HM - WAVE0))
_WIN = {
    "wq0": (True, "wct", lambda r, sc, hc: r.at[_rows(PMAIN * sc + _HM * hc + WAVES[0][0], WAVES[0][1]), :]),
    "wq1": (True, "wct", lambda r, sc, hc: r.at[_rows(PMAIN * sc + _HM * hc + WAVES[1][0], WAVES[1][1]), :]),
    "xt": (True, "xt", lambda r, sc, hc: r.at[sc, _rows(_HX * hc, _HX), :]),
    "w1": (True, "w1", lambda r, sc, hc: r.at[_rows(512 * hc, 512), pl.ds(1024 * sc, 1024)]),
    "w2": (True, "w2", lambda r, sc, hc: r.at[_rows(1024 * sc + 512 * hc, 512), :]),
    "wa": (True, "wa", lambda r, sc, hc: r.at[_rows(256 * sc + 128 * hc, 128), :]),
    "wb": (True, "wb", lambda r, sc, hc: r.at[_rows(512 * sc + 256 * hc, 256), :]),
    "wo": (True, "wo", lambda r, sc, hc: r.at[_rows(256 * sc + 128 * hc, 128), :]),
    "cw": (False, "cw", lambda r, sc, hc: r.at[sc]),
}


_PIECE_SRC = {
    "wq0": lambda p, hc: p.at[_rows(_HM * hc + WAVES[0][0], WAVES[0][1]), :],
    "wq1": lambda p, hc: p.at[_rows(_HM * hc + WAVES[1][0], WAVES[1][1]), :],
    "xt": lambda p, hc: p.at[_rows(PMAIN + _HX * hc, _HX), :],
}


def _ag_chips_plan(keys):
    def copies(sk, R):
        _, _, c = _coords()
        out = []
        for key in keys:
            _, arr, win = _WIN[key]
            for (px, py), ps in _other_chips(sk):
                dst = win(R[arr], sk, c)
                src = _PIECE_SRC[key](R["piece"], c) if key in _PIECE_SRC else dst
                out.append(((src, dst, (px, py, c)), win(R[arr], ps, c)))
        return out
    return _Plan(3 * len(keys), copies)


def _ag_sibling_plan(keys):
    keys = [k for k in keys if _WIN[k][0]]

    def copies(sk, R):
        x, y, c = _coords()
        out = []
        for key in keys:
            _, arr, win = _WIN[key]
            for _, ps in _other_chips(sk):
                w = win(R[arr], ps, c)
                out.append(((w, w, (x, y, 1 - c)), win(R[arr], ps, 1 - c)))
        return out
    return _Plan(3 * len(keys), copies)


def _in_proj_wave(h, wct, wave, proj=None, tm=2048):
    L = h.shape[0]
    tm = min(tm, L)
    off, size = WAVES[wave]
    start = lambda j: pl.multiple_of(_HM * j + off, 128)

    def kern(h_ref, w_ref, *rest):
        o_ref = rest[-1]
        o_ref[...] = lax.dot_general(h_ref[...], w_ref[...], _DIMS["nt"], preferred_element_type=F32).astype(BF16)

    in_specs = [pl.BlockSpec((tm, D), lambda j, i: (i, 0)),
                pl.BlockSpec((pl.Element(size), pl.Element(D)), lambda j, i: (start(j), 0))]
    args, aliases = [h, wct], {}
    if proj is not None:
        in_specs.append(pl.BlockSpec(memory_space=pl.ANY))
        args.append(proj)
        aliases = {2: 0}
    return pl.pallas_call(
        kern, grid=(8, L // tm), in_specs=in_specs,
        out_specs=pl.BlockSpec((pl.Element(tm), pl.Element(size)), lambda j, i: (i * tm, start(j))),
        out_shape=jax.ShapeDtypeStruct((L, NCW), BF16), input_output_aliases=aliases,
        name="in_proj_wave%d" % wave, compiler_params=_cp(("parallel", "parallel")),
    )(*args)


def _fix_wct(wct, xt):
    nb = PMAIN // XTRA

    def kern(w_ref, x_ref, o_ref):
        k = pl.program_id(0)
        xv = x_ref[0]
        o_ref[...] = jnp.where(k < 3, (w_ref[...].astype(F32) + xv.astype(F32)).astype(BF16), xv)

    blk = pl.BlockSpec((XTRA, D), lambda k: (nb * (k + 1), 0))
    rblk = pl.BlockSpec((XTRA, D), lambda k: (jnp.where(k < 3, nb * (k + 1), 0), 0))
    return pl.pallas_call(
        kern, grid=(4,), in_specs=[rblk, pl.BlockSpec((1, XTRA, D), lambda k: (k, 0, 0))], out_specs=blk,
        out_shape=jax.ShapeDtypeStruct(wct.shape, BF16), input_output_aliases={0: 0}, name="fix_wct",
        compiler_params=_cp(("arbitrary",)),
    )(wct, xt)


_HP = PIECE // 2
_GWIN = [
    lambda r, sc, hc: r.at[_rows(PMAIN * sc + _HP * hc, _HP), :],
    lambda r, sc, hc: r.at[_rows(512 * hc, 512), pl.ds(1024 * sc, 1024)],
    lambda r, sc, hc: r.at[_rows(1024 * sc + 512 * hc, 512), :],
    lambda r, sc, hc: r.at[_rows(256 * sc + 128 * hc, 128), :],
    lambda r, sc, hc: r.at[_rows(512 * sc + 256 * hc, 256), :],
    lambda r, sc, hc: r.at[_rows(256 * sc + 128 * hc, 128), :],
]
HALF_SHAPES = [(PIECE // 2, D), (512, 1024), (512, 1024), (128, 1024), (256, 1024), (128, 1024)]


def _rs_sibling_plan(ts):
    def copies(sk, R):
        x, y, c = _coords()
        out = []
        for t in ts:
            for sc in range(4):
                land = R["ra%d" % t].at[sc]
                out.append(((_GWIN[t](R["g%d" % t], sc, 1 - c), land, (x, y, 1 - c)), land))
        return out
    return _Plan(4 * len(ts), copies)


def _rs_chips_plan(ts):
    def copies(sk, R):
        _, _, c = _coords()
        out = []
        for t in ts:
            for j, ((px, py), ps) in enumerate(_other_chips(sk)):
                land = R["rb%d" % t].at[j]
                out.append(((R["hb%d" % t].at[ps], land, (px, py, c)), land))
        return out
    return _Plan(3 * len(ts), copies)


def _rs_share_plan(ts):
    def copies(sk, R):
        x, y, c = _coords()
        out = []
        for t in ts:
            rows = HALF_SHAPES[t][0]
            mine = R["f%d" % t].at[_rows(rows * c, rows), :]
            out.append(((mine, mine, (x, y, 1 - c)), R["f%d" % t].at[_rows(rows * (1 - c), rows), :]))
        return out
    return _Plan(len(ts), copies)


def _half_tiling(t):
    rows, cols = HALF_SHAPES[t]
    if t == 0:
        return (rows // 2, cols), 2, lambda i: (i, 0)
    return (rows, cols), 1, lambda i: (0, 0)


def _window_spec(t, blk):
    if t == 0:
        return pl.BlockSpec((pl.Element(blk[0]), pl.Element(blk[1])), lambda i, sc, idx_ref: (
            pl.multiple_of(PMAIN * sc + _HP * idx_ref[1] + blk[0] * i, 128), 0))
    if t == 1:
        return pl.BlockSpec(blk, lambda i, sc, idx_ref: (idx_ref[1], sc))
    return pl.BlockSpec(blk, lambda i, sc, idx_ref: (2 * sc + idx_ref[1], 0))


def _chip_sum(g, ra, t, idx, name):
    rows, cols = HALF_SHAPES[t]
    blk, nblk, inner = _half_tiling(t)

    def kern(idx_ref, g_ref, r_ref, hb_ref, hf_ref):
        v = g_ref[...].astype(F32) + r_ref[0].astype(F32)
        hb_ref[0] = v.astype(BF16)

        @pl.when(pl.program_id(1) == idx_ref[0])
        def _():
            hf_ref[...] = v

    omap = lambda i, sc, idx_ref: (sc,) + inner(i)
    grid_spec = pltpu.PrefetchScalarGridSpec(
        num_scalar_prefetch=1, grid=(nblk, 4),
        in_specs=[_window_spec(t, blk), pl.BlockSpec((1,) + blk, omap)],
        out_specs=(pl.BlockSpec((1,) + blk, omap), pl.BlockSpec(blk, lambda i, sc, idx_ref: inner(i))))
    return pl.pallas_call(
        kern, grid_spec=grid_spec,
        out_shape=(jax.ShapeDtypeStruct((4, rows, cols), BF16), jax.ShapeDtypeStruct((rows, cols), F32)),
        name=name, compiler_params=_cp(("parallel", "arbitrary")),
    )(idx, g, ra)


def _final_sum(hf, rb, t, idx, name):
    rows, cols = HALF_SHAPES[t]
    blk, nblk, inner = _half_tiling(t)
    nbr = rows // blk[0]

    def kern(idx_ref, h_ref, r_ref, o_ref):
        o_ref[...] = ((h_ref[...] + r_ref[0].astype(F32)) + r_ref[1].astype(F32)) + r_ref[2].astype(F32)

    def omap(i, idx_ref):
        r, cidx = inner(i)
        return nbr * idx_ref[1] + r, cidx

    grid_spec = pltpu.PrefetchScalarGridSpec(
        num_scalar_prefetch=1, grid=(nblk,),
        in_specs=[pl.BlockSpec(blk, lambda i, idx_ref: inner(i)),
                  pl.BlockSpec((3,) + blk, lambda i, idx_ref: (0,) + inner(i))],
        out_specs=pl.BlockSpec(blk, omap))
    return pl.pallas_call(
        kern, grid_spec=grid_spec, out_shape=jax.ShapeDtypeStruct((2 * rows, cols), F32),
        name=name, compiler_params=_cp(("parallel",)),
    )(idx, hf, rb)


class _ReduceScatter:
    def __init__(self, ts, grads, idx, tag):
        self.ts, self.idx, self.tag = ts, idx, tag
        arr = {}
        for t in ts:
            arr["g%d" % t] = grads[t]
            arr["ra%d" % t] = lax.empty((4,) + HALF_SHAPES[t], BF16)
        self.plan = _rs_sibling_plan(ts)
        self.arr, self.sems, self.token = _split_call("rs_sibling_start_" + tag, arr, start=self.plan)

    def chips(self, after):
        arr, _, _ = _split_call("rs_sibling_wait_" + self.tag, self.arr, wait=self.plan, wait_sems=self.sems, after=after)
        brr, self.hf = {}, {}
        for t in self.ts:
            hb, self.hf[t] = _chip_sum(arr["g%d" % t], arr["ra%d" % t], t, self.idx, "chip_sum_%d" % t)
            brr["hb%d" % t] = hb
            brr["rb%d" % t] = lax.empty((3,) + HALF_SHAPES[t], BF16)
        self.plan = _rs_chips_plan(self.ts)
        self.arr, self.sems, self.token = _split_call("rs_chips_start_" + self.tag, brr, start=self.plan)
        return self.token

    def share(self, after):
        brr, _, _ = _split_call("rs_chips_wait_" + self.tag, self.arr, wait=self.plan, wait_sems=self.sems, after=after)
        frr = {"f%d" % t: _final_sum(self.hf[t], brr["rb%d" % t], t, self.idx, "final_sum_%d" % t) for t in self.ts}
        self.plan = _rs_share_plan(self.ts)
        self.arr, self.sems, self.token = _split_call("rs_share_start_" + self.tag, frr, start=self.plan)
        return self.token

    def result(self, after):
        frr, _, _ = _split_call("rs_share_wait_" + self.tag, self.arr, wait=self.plan, wait_sems=self.sems, after=after)
        return {t: frr["f%d" % t] for t in self.ts}


def _all8_plan(key):
    def copies(sk, R):
        x, y, c = _coords()
        own = R[key].at[4 * x + 2 * y + c]
        out = []
        for k in range(1, 8):
            dev = ((1 - x) if (k >> 2) & 1 else x, (1 - y) if (k >> 1) & 1 else y, (1 - c) if k & 1 else c)
            out.append(((own, own, dev), R[key].at[4 * dev[0] + 2 * dev[1] + dev[2]]))
        return out
    return _Plan(7, copies)


def _sum8(v, name="small_sum"):
    def kern(v_ref, o_ref):
        acc = v_ref[0]
        for k in range(1, 8):
            acc = acc + v_ref[k]
        o_ref[...] = acc

    return pl.pallas_call(kern, out_shape=jax.ShapeDtypeStruct(v.shape[1:], F32), name=name)(v)


def _adamw(w, g, m, v, name, tr=128, blk0=0, nblk=None, into=None, copy_g=False):
    R, C = w.shape
    tr = min(tr, R)
    if nblk is None:
        assert R % tr == 0 and blk0 == 0
        nblk = R // tr
    n_out = 4 if copy_g else 3

    def kern(*refs):
        w_ref, g_ref, m_ref, v_ref = refs[:4]
        d_ref, mo_ref, vo_ref = refs[-n_out:][:3]
        gv = g_ref[...]
        mn = ADAM_B1 * m_ref[...] + (1.0 - ADAM_B1) * gv
        vn = ADAM_B2 * v_ref[...] + (1.0 - ADAM_B2) * (gv * gv)
        m_hat = mn / (1.0 - ADAM_B1 ** ADAM_STEP)
        v_hat = vn / (1.0 - ADAM_B2 ** ADAM_STEP)
        d_ref[...] = -ADAM_LR * (m_hat / (jnp.sqrt(v_hat) + ADAM_EPS) + ADAM_WD * w_ref[...])
        mo_ref[...] = mn
        vo_ref[...] = vn
        if copy_g:
            refs[-1][...] = gv

    blk = pl.BlockSpec((tr, C), lambda i: (blk0 + i, 0))
    sd = jax.ShapeDtypeStruct((R, C), F32)
    in_specs, args, aliases = [blk] * 4, [w, g, m, v], {}
    if into is not None:
        in_specs += [pl.BlockSpec(memory_space=pl.ANY)] * 3
        args += list(into)
        aliases = {4: 0, 5: 1, 6: 2}
    return pl.pallas_call(kern, grid=(nblk,), in_specs=in_specs, out_specs=(blk,) * n_out, out_shape=(sd,) * n_out,
                          input_output_aliases=aliases, name=name, compiler_params=_cp(("parallel",)))(*args)


def _adamw_w_in(wt, gp, mt, vt, offs, name, r0, tr, nblk, views, into=None, blk_key=None):
    el = lambda n: (pl.Element(n), pl.Element(D))
    first = (lambda o: r0) if blk_key is None else (lambda o: tr * o[blk_key])
    own = pl.BlockSpec(el(tr), lambda i, o: (pl.multiple_of(first(o) + tr * i, 8), 0))

    def view(k):
        return pl.BlockSpec(el(tr), lambda i, o: (pl.multiple_of(jnp.maximum(first(o) + tr * i + o[k], 0), 8), 0))

    def kern(o_ref, w_ref, m_ref, v_ref, *refs):
        g_refs, (d_ref, mo_ref, vo_ref, go_ref) = refs[:len(views)], refs[-4:]
        gv = g_refs[0][...]
        if len(views) == 2:
            row = first(o_ref) + tr * pl.program_id(0) + lax.broadcasted_iota(jnp.int32, (tr, D), 0)
            gv = jnp.where(row < o_ref[2], gv, g_refs[1][...])
        mn = ADAM_B1 * m_ref[...] + (1.0 - ADAM_B1) * gv
        vn = ADAM_B2 * v_ref[...] + (1.0 - ADAM_B2) * (gv * gv)
        m_hat = mn / (1.0 - ADAM_B1 ** ADAM_STEP)
        v_hat = vn / (1.0 - ADAM_B2 ** ADAM_STEP)
        d_ref[...] = -ADAM_LR * (m_hat / (jnp.sqrt(v_hat) + ADAM_EPS) + ADAM_WD * w_ref[...])
        mo_ref[...] = mn
        vo_ref[...] = vn
        go_ref[...] = gv

    in_specs = [own, own, own] + [view(k) for k in views]
    args = [wt, mt, vt] + [gp] * len(views)
    aliases = {}
    if into is not None:
        in_specs += [pl.BlockSpec(memory_space=pl.ANY)] * 4
        args += list(into)
        aliases = {1 + len(args) - 4 + j: j for j in range(4)}
    grid_spec = pltpu.PrefetchScalarGridSpec(num_scalar_prefetch=1, grid=(nblk,), in_specs=in_specs,
                                             out_specs=(own,) * 4)
    sd = jax.ShapeDtypeStruct(wt.shape, F32)
    return pl.pallas_call(kern, grid_spec=grid_spec, out_shape=(sd,) * 4, input_output_aliases=aliases, name=name,
                          compiler_params=_cp(("parallel",)))(offs, *args)


def _to_piece(wt, s):
    z = lambda n: jnp.zeros((n, D), wt.dtype)
    pads = [functools.partial(lambda k, w: jnp.pad(w, ((8 * k, PIECE - W_SHARD - 8 * k), (0, 0))).astype(BF16), k)
            for k in range(3)]
    last = lambda w: jnp.concatenate([z(24), w[:744], w[776:], w[744:776], z(PIECE - 24 - W_SHARD)], axis=0).astype(BF16)
    return lax.switch(s, pads + [last], wt)


_SMALL = [("b_gate", 2048), ("ssm_conv_b", 4096), ("dt_bias", 32), ("A_log", 32), ("D_skip", 32),
          ("ssm_norm_w", 2048), ("norm_mlp", 1024), ("norm_final", 1024), ("sc_conv_w", 3072), ("ssm_conv_w", 16384),
          ("loss", 1)]


def _pack(vals, table, rows):
    parts = []
    for name, n in table:
        v = vals[name].reshape(-1).astype(F32)
        pad = (-n) % 128
        parts.append(jnp.pad(v, (0, pad)) if pad else v)
    flat = jnp.concatenate(parts)
    return jnp.pad(flat, (0, rows * 128 - flat.shape[0])).reshape(rows, 128)


def _unpack(arr, table):
    flat = arr.reshape(-1)
    out, off = {}, 0
    for name, n in table:
        out[name] = flat[off:off + n]
        off += n + ((-n) % 128)
    return out


def kernel(x, norm_mix, w_in, b_gate, sc_conv_w, ssm_conv_w, ssm_conv_b, dt_bias, A_log, D_skip, ssm_norm_w, w_branch_sc, w_branch_ssm, w_out, norm_mlp, w_mlp1, w_mlp2, norm_final, loss_target, m_norm_mix, m_w_in, m_b_gate, m_sc_conv_w, m_ssm_conv_w, m_ssm_conv_b, m_dt_bias, m_A_log, m_D_skip, m_ssm_norm_w, m_w_branch_sc, m_w_branch_ssm, m_w_out, m_norm_mlp, m_w_mlp1, m_w_mlp2, m_norm_final, v_norm_mix, v_w_in, v_b_gate, v_sc_conv_w, v_ssm_conv_w, v_ssm_conv_b, v_dt_bias, v_A_log, v_D_skip, v_ssm_norm_w, v_w_branch_sc, v_w_branch_ssm, v_w_out, v_norm_mlp, v_w_mlp1, v_w_mlp2, v_norm_final):
    L = x.shape[1]
    nc = L // Q
    xi, yi, ci = lax.axis_index("x"), lax.axis_index("y"), lax.axis_index("c")
    s = 2 * xi + yi
    idx = jnp.stack([s, ci]).astype(jnp.int32)
    x0 = x.reshape(L, D)
    tgt = loss_target.reshape(L, D)
    small_names = ["b_gate", "sc_conv_w", "ssm_conv_w", "ssm_conv_b", "dt_bias", "A_log", "D_skip", "ssm_norm_w",
                   "norm_mlp", "norm_final"]
    small_wmv = [dict(zip(small_names, vals)) for vals in (
        (b_gate, sc_conv_w, ssm_conv_w, ssm_conv_b, dt_bias, A_log, D_skip, ssm_norm_w, norm_mlp, norm_final),
        (m_b_gate, m_sc_conv_w, m_ssm_conv_w, m_ssm_conv_b, m_dt_bias, m_A_log, m_D_skip, m_ssm_norm_w, m_norm_mlp,
         m_norm_final),
        (v_b_gate, v_sc_conv_w, v_ssm_conv_w, v_ssm_conv_b, v_dt_bias, v_A_log, v_D_skip, v_ssm_norm_w, v_norm_mlp,
         v_norm_final))]
    small_table = [(n, int(small_wmv[0][n].size)) for n in small_names]
    small_rows = 136
    pk_w, pk_m, pk_v = [_pack(d, small_table, small_rows) for d in small_wmv]

    piece = _to_piece(w_in.T, s)
    nb = PMAIN // XTRA
    cws = jnp.zeros((8, 1280), F32)
    cws = cws.at[0:3, 0:256].set(sc_conv_w).at[0:4, 256:1280].set(ssm_conv_w)
    cw0 = lax.dynamic_update_slice(jnp.zeros((4, 8, 1280), F32), cws[None], (s, 0, 0))
    win_keys, win2_keys, mid_keys, end_keys = ["xt", "cw", "wq0"], ["wq1"], ["wa", "wb", "wo", "w1"], ["w2"]
    gw, sems_w, tok = _split_call(
        "ag_win_start", {"wct": lax.empty((NCW, D), BF16), "xt": lax.empty((4, XTRA, D), BF16), "cw": cw0, "piece": piece},
        start=_ag_chips_plan(win_keys))
    g2, sems_w2, tok = _split_call("ag_win2_start", {"wct": gw["wct"], "piece": gw["piece"]},
                                   start=_ag_chips_plan(win2_keys), after=tok)
    piece = g2["piece"]
    gw["wct"] = _place(piece, (NCW, D), (XTRA, D), lambda i, r: (nb * r[0] + i, 0), idx, "place_wct", nblk=nb,
                       dep=tok, into=g2["wct"])
    gw["xt"] = _place(piece, (4, XTRA, D), (1, XTRA, D), lambda i, r: (r[0], 0, 0), idx, "place_xt", blk0=nb, nblk=1,
                      dep=tok, into=gw["xt"])
    gw["piece"] = piece
    wa0 = _place(w_branch_sc, (D, D), (256, 1024), lambda i, r: (r[0], 0), idx, "place_wa", dep=tok)
    wb0 = _place(w_branch_ssm, (INNER, D), (512, 1024), lambda i, r: (r[0], 0), idx, "place_wb", dep=tok)
    wo0 = _place(w_out, (D, D), (256, 1024), lambda i, r: (r[0], 0), idx, "place_wo", dep=tok)
    w10 = _place(w_mlp1, (D, DFF), (256, 1024), lambda i, r: (i, r[0]), idx, "place_w1", dep=tok)
    gm, sems_m, tok = _split_call("ag_mid_start", {"wa": wa0, "wb": wb0, "wo": wo0, "w1": w10},
                                  start=_ag_chips_plan(mid_keys))
    w20 = _place(w_mlp2, (DFF, D), (256, 1024), lambda i, r: (4 * r[0] + i, 0), idx, "place_w2", dep=tok)
    ge, sems_e, tok = _split_call("ag_end_start", {"w2": w20}, start=_ag_chips_plan(end_keys))
    h = _rms_fwd(x0, norm_mix, "rms_mix", dep=tok)
    gw, sems_w, tok = _split_call("ag_win_pass", gw, wait=_ag_chips_plan(win_keys), wait_sems=sems_w,
                                  start=_ag_sibling_plan(win_keys), after=[h, pk_w, pk_m, pk_v])
    gw, _, _ = _split_call("ag_win_done", gw, wait=_ag_sibling_plan(win_keys), wait_sems=sems_w, after=tok)
    wc, cw_all = _fix_wct(gw["wct"], gw["xt"]), gw["cw"]
    sc_w_full = jnp.concatenate([cw_all[k, :, 0:256] for k in range(4)], axis=1)
    ssm_w_full = jnp.concatenate([cw_all[k, :, 256:1280] for k in range(4)], axis=1)
    cw4 = ssm_w_full.at[4].set(ssm_conv_b)
    vec = jnp.zeros((8, 128), F32).at[0, :NH].set(dt_bias).at[1, :NH].set(A_log)
    vecg = jnp.zeros((NG, 8, 128), F32).at[:, 0, :4].set(A_log.reshape(NG, 4)).at[:, 1, :4].set(D_skip.reshape(NG, 4))

    dtraw = _matmul(h, wc[C_DT:], "nt", F32, 512, 256, 1024, "in_proj_dt")
    proj = _in_proj_wave(h, wc, 0)
    g2, sems_w2, tok = _split_call("ag_win2_pass", {"wct": wc, "piece": gw["piece"]},
                                   wait=_ag_chips_plan(win2_keys), wait_sems=sems_w2,
                                   start=_ag_sibling_plan(win2_keys), after=[proj, dtraw])
    g2, _, _ = _split_call("ag_win2_done", g2, wait=_ag_sibling_plan(win2_keys), wait_sems=sems_w2, after=tok)
    wc = g2["wct"]
    proj = _in_proj_wave(h, wc, 1, proj=proj)
    ya = _sc_fwd(proj, sc_w_full)
    xbc = _ssm_conv_fwd(proj, cw4)
    dt4, cs4, sg4 = _dt_prep(dtraw, vec)
    y, s_all = _ssd_fwd(xbc, dt4, cs4, vecg)
    gm, sems_m, tok = _split_call("ag_mid_pass", gm, wait=_ag_chips_plan(mid_keys), wait_sems=sems_m,
                                  start=_ag_sibling_plan(mid_keys), after=[y, ya])
    y = _tie(y, tok, "tie_y")
    yb = _gnorm_fwd(y, proj, ssm_norm_w)
    gm, _, _ = _split_call("ag_mid_done", gm, wait=_ag_sibling_plan(mid_keys), wait_sems=sems_m, after=yb)
    wa, wb, wo, w1 = gm["wa"], gm["wb"], gm["wo"], gm["w1"]
    ge, sems_e, tok = _split_call("ag_end_pass", ge, wait=_ag_chips_plan(end_keys), wait_sems=sems_e,
                                  start=_ag_sibling_plan(end_keys), after=yb)
    br_a = _matmul(ya, wa, "nn", F32, 1024, 1024, 1024, "branch_sc", dep=tok)
    br_b = _matmul(yb, wb, "nn", F32, 1024, 1024, 2048, "branch_ssm")
    merged = _merge_fwd(proj, b_gate, br_a, br_b)
    x1, h2 = _matmul_res_rms(merged, wo, x0, norm_mlp, 1024, "out_proj")
    a1, rl = _matmul(h2, w1, "nn", BF16, 1024, 1024, 1024, "mlp1", epi="relu2", n_outer=True)
    ge, _, _ = _split_call("ag_end_done", ge, wait=_ag_sibling_plan(end_keys), wait_sems=sems_e, after=a1)
    w2 = ge["w2"]
    dx2, g_nf, loss8 = _matmul_res_final(rl, w2, x1, norm_final, tgt, 512, "mlp2")

    da = _matmul(dx2, w2, "nt", BF16, 1024, 1024, 1024, "mlp2_dx", epi="drelu", extra=a1, n_outer=True)
    g_w2 = _matmul(rl, dx2, "tn", BF16, 1024, 1024, 2048, "mlp2_dw")
    g_w1 = _matmul(h2, da, "tn", BF16, 1024, 1024, 2048, "mlp1_dw")
    dx1, g_nmlp = _matmul_rms_bwd(da, w1, "nt", x1, norm_mlp, dx2, 512, 4096, "mlp1_dx")
    dmerged = _matmul(dx1, wo, "nt", F32, 1024, 1024, 1024, "out_proj_dx")
    g_wo = _matmul(merged, dx1, "tn", BF16, 1024, 1024, 2048, "out_proj_dw")
    dproj = lax.empty((L, NCW), BF16)
    dbr, dproj, g_bg = _merge_bwd(dmerged, proj, b_gate, br_a, br_b, dproj)
    dya = _matmul(dbr[0], wa, "nt", F32, 1024, 1024, 1024, "branch_sc_dx")
    g_wa = _matmul(ya, dbr[0], "tn", BF16, 1024, 1024, 2048, "branch_sc_dw")
    dproj, g_scw = _sc_bwd(dya, proj, sc_w_full, dproj)
    g_wb = _matmul(yb, dbr[1], "tn", BF16, 1024, 1024, 2048, "branch_ssm_dw")
    rs_a = _ReduceScatter([1, 2, 3, 4, 5], {1: g_w1, 2: g_w2, 3: g_wa, 4: g_wb, 5: g_wo}, idx, "a")
    dy, dproj, g_snw = _gnorm_bwd(dbr, wb, y, proj, ssm_norm_w, dproj, rs_a.token)
    tok = rs_a.chips(after=dy)
    dxs, dbm, dcm, ddt_g, st = _ssd_bwd(xbc, dt4, cs4, sg4, vecg, s_all, _tie(dy, tok, "tie_dy"))
    dproj, gx1 = _ssm_conv_bwd(dxs, proj, cw4, dproj, 0, "ssm_conv_bwd_x")
    dproj, gx2 = _ssm_conv_bwd(dbm, proj, cw4, dproj, INNER, "ssm_conv_bwd_b")
    dproj, gx3 = _ssm_conv_bwd(dcm, proj, cw4, dproj, INNER + NG * NS, "ssm_conv_bwd_c")
    g_cw4 = jnp.concatenate([gx1, gx2, gx3], axis=1)
    dproj, g_dtb = _dt_bwd(ddt_g, dproj)
    small = {"b_gate": g_bg[0], "ssm_conv_b": g_cw4[4], "dt_bias": g_dtb[0, :NH],
             "A_log": st[:, 0, :4], "D_skip": st[:, 1, :4], "ssm_norm_w": g_snw[0], "norm_mlp": g_nmlp[0],
             "norm_final": g_nf[0], "sc_conv_w": g_scw[0:3], "ssm_conv_w": g_cw4[0:4], "loss": loss8[0, 0:1]}
    me = 4 * xi + 2 * yi + ci
    sm8 = lax.dynamic_update_slice(jnp.zeros((8, SMALL_ROWS, 128), F32), _pack(small, _SMALL, SMALL_ROWS)[None], (me, 0, 0))
    sm_arr, sm_sems, tok = _split_call("small_start", {"sm": sm8}, start=_all8_plan("sm"))
    g_wc = _matmul(dproj, h, "tn", BF16, 1280, 1024, 2048, "in_proj_dw", dep=tok)
    rs_b = _ReduceScatter([0], {0: g_wc}, idx, "b")
    tok = rs_a.share(after=rs_b.token)
    tok = rs_b.chips(after=tok)
    grad_x, g_nm = _matmul_rms_bwd(dproj, wc, "nn", x0, norm_mix, dx1, 512, 5760, "in_proj_dx", dep=tok)
    nm8 = lax.dynamic_update_slice(jnp.zeros((8, 8, 128), F32), g_nm[0].reshape(1, 8, 128), (me, 0, 0))
    nm_arr, nm_sems, tok = _split_call("norm_mix_start", {"nm": nm8}, start=_all8_plan("nm"))
    sm_arr, _, _ = _split_call("small_wait", sm_arr, wait=_all8_plan("sm"), wait_sems=sm_sems, after=tok)
    small_sum = _sum8(sm_arr["sm"])
    gs = _unpack(small_sum, _SMALL)
    red = rs_a.result(after=tok)
    big = {"w_mlp1": red[1], "w_mlp2": red[2], "w_branch_sc": red[3], "w_branch_ssm": red[4], "w_out": red[5]}

    given = dict(norm_mix=norm_mix, w_in=w_in, b_gate=b_gate, sc_conv_w=sc_conv_w, ssm_conv_w=ssm_conv_w, ssm_conv_b=ssm_conv_b, dt_bias=dt_bias, A_log=A_log, D_skip=D_skip, ssm_norm_w=ssm_norm_w, w_branch_sc=w_branch_sc, w_branch_ssm=w_branch_ssm, w_out=w_out, norm_mlp=norm_mlp, w_mlp1=w_mlp1, w_mlp2=w_mlp2, norm_final=norm_final,
                 m_norm_mix=m_norm_mix, m_w_in=m_w_in, m_b_gate=m_b_gate, m_sc_conv_w=m_sc_conv_w, m_ssm_conv_w=m_ssm_conv_w, m_ssm_conv_b=m_ssm_conv_b, m_dt_bias=m_dt_bias, m_A_log=m_A_log, m_D_skip=m_D_skip, m_ssm_norm_w=m_ssm_norm_w, m_w_branch_sc=m_w_branch_sc, m_w_branch_ssm=m_w_branch_ssm, m_w_out=m_w_out, m_norm_mlp=m_norm_mlp, m_w_mlp1=m_w_mlp1, m_w_mlp2=m_w_mlp2, m_norm_final=m_norm_final,
                 v_norm_mix=v_norm_mix, v_w_in=v_w_in, v_b_gate=v_b_gate, v_sc_conv_w=v_sc_conv_w, v_ssm_conv_w=v_ssm_conv_w, v_ssm_conv_b=v_ssm_conv_b, v_dt_bias=v_dt_bias, v_A_log=v_A_log, v_D_skip=v_D_skip, v_ssm_norm_w=v_ssm_norm_w, v_w_branch_sc=v_w_branch_sc, v_w_branch_ssm=v_w_branch_ssm, v_w_out=v_w_out, v_norm_mlp=v_norm_mlp, v_w_mlp1=v_w_mlp1, v_w_mlp2=v_w_mlp2, v_norm_final=v_norm_final)
    order = ["norm_mix", "w_in", "b_gate", "sc_conv_w", "ssm_conv_w", "ssm_conv_b", "dt_bias", "A_log", "D_skip",
             "ssm_norm_w", "w_branch_sc", "w_branch_ssm", "w_out", "norm_mlp", "w_mlp1", "w_mlp2", "norm_final"]
    grad, delta, new_m, new_v = {}, {}, {}, {}
    for n in big:
        delta[n], new_m[n], new_v[n], grad[n] = _adamw(given[n], big[n], given["m_" + n], given["v_" + n],
                                                       "adamw_" + n, copy_g=True)
    big["w_in"] = None
    grad_small = {n: gs[n].reshape(given[n].shape) for n in small_names if n not in ("sc_conv_w", "ssm_conv_w")}
    grad_small["sc_conv_w"] = lax.dynamic_slice(gs["sc_conv_w"].reshape(3, D), (0, 256 * s), (3, 256))
    grad_small["ssm_conv_w"] = lax.dynamic_slice(gs["ssm_conv_w"].reshape(4, XBC), (0, 1024 * s), (4, 1024))
    table = small_table
    ds_, ms_, vs_ = _adamw(pk_w, _pack(grad_small, table, small_rows), pk_m, pk_v, "adamw_small", tr=small_rows)
    ds_, ms_, vs_ = _unpack(ds_, table), _unpack(ms_, table), _unpack(vs_, table)
    for n in grad_small:
        shp = given[n].shape
        grad[n] = grad_small[n]
        delta[n], new_m[n], new_v[n] = ds_[n].reshape(shp), ms_[n].reshape(shp), vs_[n].reshape(shp)

    done = [new_v[n] for n in ("w_mlp1", "w_mlp2", "w_branch_sc", "w_branch_ssm", "w_out")] + [vs_["b_gate"]]
    tok = rs_b.share(after=done)
    offs = jnp.where(s == 3, jnp.array([24, -8, 744, 2072, -8], jnp.int32),
                     jnp.stack([8 * s, 8 * s, 0 * s, 8 * s, 8 * s]).astype(jnp.int32))
    offs = jnp.concatenate([offs, jnp.stack([7 * ci, 4 - 4 * ci]).astype(jnp.int32)])
    nmain = W_SHARD // 256
    wt_own = (w_in.T, rs_b.arr["f0"], m_w_in.T, v_w_in.T, offs)
    res = _adamw_w_in(*wt_own, "adamw_w_in_own", 0, 256, 4, (0, 1), blk_key=5)
    gp = rs_b.result(after=[tok, res[0]])[0]
    wt_args = (w_in.T, gp, m_w_in.T, v_w_in.T, offs)
    res = _adamw_w_in(*wt_args, "adamw_w_in", 0, 256, nmain - 4, (0, 1), into=res, blk_key=6)
    res = _adamw_w_in(*wt_args, "adamw_w_in_dt", 744, 32, 1, (3,), into=res)
    dt_, mt_, vt_, gwt = _adamw_w_in(*wt_args, "adamw_w_in_tail", 256 * nmain, 8, 1, (4,), into=res)
    grad["w_in"], delta["w_in"], new_m["w_in"], new_v["w_in"] = gwt.T, dt_.T, mt_.T, vt_.T
    nm_arr, _, _ = _split_call("norm_mix_wait", nm_arr, wait=_all8_plan("nm"), wait_sems=nm_sems, after=tok)
    g8 = _sum8(nm_arr["nm"], "norm_mix_sum")
    r8 = lambda a: a.reshape(8, 128)
    d8, m8, v8 = _adamw(r8(norm_mix), g8, r8(m_norm_mix), r8(v_norm_mix), "adamw_norm_mix", tr=8)
    grad["norm_mix"], delta["norm_mix"] = g8.reshape(D), d8.reshape(D)
    new_m["norm_mix"], new_v["norm_mix"] = m8.reshape(D), v8.reshape(D)

    loss = gs["loss"].reshape(())
    return (loss, grad_x.reshape(1, L, D), *[grad[n] for n in order], *[delta[n] for n in order],
            *[new_m[n] for n in order], *[new_v[n] for n in order])
```

```python
import functools

import jax
import jax.numpy as jnp
from jax import lax
from jax.experimental import pallas as pl
from jax.experimental.pallas import tpu as pltpu

F32 = jnp.float32
BF16 = jnp.bfloat16
MESH = pl.DeviceIdType.MESH
HBM = pltpu.HBM

D = 1024
INNER = 2048
HD = 64
NH = 32
NG = 8
NS = 128
Q = 128
GPS = 8
XBC = 4096
DFF = 4096
EPS = 1e-6
W_SHARD = 2824
NCW = 11520
PIECE = 3072
PMAIN = 2816
C_Z, C_XBC, C_GATE, C_DT = 3072, 5120, 9216, 11264
SMALL_ROWS = 256
VMEM_LIMIT = 56 * 1024 * 1024

ADAM_LR, ADAM_B1, ADAM_B2, ADAM_EPS, ADAM_WD, ADAM_STEP = 0.001, 0.9, 0.999, 1e-08, 0.01, 10


def _cp(sem=None, vmem=VMEM_LIMIT):
    return pltpu.CompilerParams(dimension_semantics=sem, vmem_limit_bytes=vmem)


def _sigmoid(v):
    return 1.0 / (1.0 + jnp.exp(-v))


_DIMS = {"nn": (((1,), (0,)), ((), ())), "nt": (((1,), (1,)), ((), ())), "tn": (((0,), (0,)), ((), ()))}


def _matmul(a, b, mode, out_dtype, tm, tn, tk, name, epi=None, extra=None, n_outer=False, dep=None):
    if mode == "tn":
        K, M = a.shape
    else:
        M, K = a.shape
    N = b.shape[0] if mode == "nt" else b.shape[1]
    tm, tn, tk = min(tm, M), min(tn, N), min(tk, K)
    assert M % tm == 0 and N % tn == 0 and K % tk == 0, (name, M, N, K, tm, tn, tk)
    nm, nn, nk = M // tm, N // tn, K // tk
    dims = _DIMS[mode]

    def ij(p0, p1):
        return (p1, p0) if n_outer else (p0, p1)

    if mode == "tn":
        a_spec = pl.BlockSpec((tk, tm), lambda p0, p1, k: (k, ij(p0, p1)[0]))
    else:
        a_spec = pl.BlockSpec((tm, tk), lambda p0, p1, k: (ij(p0, p1)[0], k))
    if mode == "nt":
        b_spec = pl.BlockSpec((tn, tk), lambda p0, p1, k: (ij(p0, p1)[1], k))
    else:
        b_spec = pl.BlockSpec((tk, tn), lambda p0, p1, k: (k, ij(p0, p1)[1]))
    o_spec = pl.BlockSpec((tm, tn), lambda p0, p1, k: ij(p0, p1))
    in_specs = [a_spec, b_spec]
    args = [a, b]
    if epi in ("res", "drelu"):
        in_specs.append(o_spec)
        args.append(extra)
    if dep is not None:
        in_specs.append(pl.BlockSpec(memory_space=pl.ANY))
        args.append(dep)
    n_in = len(args)
    if epi == "relu2":
        out_shape = (jax.ShapeDtypeStruct((M, N), out_dtype), jax.ShapeDtypeStruct((M, N), BF16))
        out_specs = (o_spec, o_spec)
    else:
        out_shape = jax.ShapeDtypeStruct((M, N), out_dtype)
        out_specs = o_spec

    def kern(*refs):
        a_ref, b_ref = refs[0], refs[1]
        e_ref = refs[2] if epi in ("res", "drelu") else None
        acc = refs[-1]
        outs = refs[n_in:-1] if nk > 1 else refs[n_in:]
        k = pl.program_id(2)

        def product():
            return lax.dot_general(a_ref[...].astype(BF16), b_ref[...].astype(BF16), dims, preferred_element_type=F32)

        def finish(r):
            if epi is None:
                outs[0][...] = r.astype(out_dtype)
            elif epi == "res":
                outs[0][...] = (r + e_ref[...]).astype(out_dtype)
            elif epi == "relu2":
                outs[0][...] = r.astype(out_dtype)
                t = jnp.maximum(r, 0.0)
                outs[1][...] = (t * t).astype(BF16)
            else:
                outs[0][...] = (r * (2.0 * jnp.maximum(e_ref[...].astype(F32), 0.0))).astype(out_dtype)

        if nk == 1:
            finish(product())
        else:
            @pl.when(k == 0)
            def _():
                acc[...] = jnp.zeros_like(acc)

            acc[...] += product()

            @pl.when(k == nk - 1)
            def _():
                finish(acc[...])

    grid = (nn, nm, nk) if n_outer else (nm, nn, nk)
    return pl.pallas_call(
        kern, grid=grid, in_specs=in_specs, out_specs=out_specs, out_shape=out_shape,
        scratch_shapes=[pltpu.VMEM((tm, tn), F32)] if nk > 1 else [], name=name,
        compiler_params=_cp(("parallel", "parallel", "arbitrary")),
    )(*args)


def _matmul_res_rms(a, b, x, w, tm, name):
    M, K = a.shape
    tm = min(tm, M)

    def kern(a_ref, b_ref, x_ref, w_ref, x1_ref, h_ref):
        x1 = x_ref[...] + lax.dot_general(a_ref[...].astype(BF16), b_ref[...].astype(BF16), _DIMS["nn"],
                                          preferred_element_type=F32)
        x1_ref[...] = x1
        r = lax.rsqrt(jnp.mean(x1 * x1, axis=-1, keepdims=True) + EPS)
        h_ref[...] = ((x1 * r) * w_ref[...]).astype(BF16)

    row = pl.BlockSpec((tm, D), lambda i: (i, 0))
    return pl.pallas_call(
        kern, grid=(M // tm,),
        in_specs=[pl.BlockSpec((tm, K), lambda i: (i, 0)), pl.BlockSpec((K, D), lambda i: (0, 0)), row,
                  pl.BlockSpec((1, D), lambda i: (0, 0))],
        out_specs=(row, row), out_shape=(jax.ShapeDtypeStruct((M, D), F32), jax.ShapeDtypeStruct((M, D), BF16)),
        name=name, compiler_params=_cp(("parallel",)),
    )(a, b, x, w.reshape(1, D))


def _matmul_res_final(a, b, x, w, tgt, tm, name):
    M, K = a.shape
    tm = min(tm, M)

    def kern(a_ref, b_ref, x_ref, w_ref, t_ref, dx_ref, gw_ref, loss_ref):
        @pl.when(pl.program_id(0) == 0)
        def _():
            gw_ref[...] = jnp.zeros_like(gw_ref)
            loss_ref[...] = jnp.zeros_like(loss_ref)

        xv = x_ref[...] + lax.dot_general(a_ref[...].astype(BF16), b_ref[...].astype(BF16), _DIMS["nn"],
                                          preferred_element_type=F32)
        r = lax.rsqrt(jnp.mean(xv * xv, axis=-1, keepdims=True) + EPS)
        xn = xv * r
        e = xn * w_ref[...] - t_ref[...]
        loss_ref[...] += 0.5 * jnp.sum(jnp.mean(e * e, axis=-1, keepdims=True))
        dyv = e * (1.0 / D)
        gw_ref[...] += jnp.broadcast_to(jnp.sum(dyv * xn, axis=0, keepdims=True), (8, D))
        dxn = dyv * w_ref[...]
        dx_ref[...] = r * (dxn - xn * jnp.mean(dxn * xn, axis=-1, keepdims=True))

    row = pl.BlockSpec((tm, D), lambda i: (i, 0))
    return pl.pallas_call(
        kern, grid=(M // tm,),
        in_specs=[pl.BlockSpec((tm, K), lambda i: (i, 0)), pl.BlockSpec((K, D), lambda i: (0, 0)), row,
                  pl.BlockSpec((1, D), lambda i: (0, 0)), row],
        out_specs=(row, pl.BlockSpec((8, D), lambda i: (0, 0)), pl.BlockSpec((8, 128), lambda i: (0, 0))),
        out_shape=(jax.ShapeDtypeStruct((M, D), F32), jax.ShapeDtypeStruct((8, D), F32),
                   jax.ShapeDtypeStruct((8, 128), F32)),
        name=name, compiler_params=_cp(("arbitrary",)),
    )(a, b, x, w.reshape(1, D), tgt)


def _matmul_rms_bwd(a, b, mode, x, w, res, tm, tk, name, dep=None):
    M, K = a.shape
    tm, tk = min(tm, M), min(tk, K)
    nk = K // tk
    assert M % tm == 0 and K % tk == 0
    b_spec = (pl.BlockSpec((tk, D), lambda i, k: (k, 0)) if mode == "nn" else pl.BlockSpec((D, tk), lambda i, k: (0, k)))
    row = pl.BlockSpec((tm, D), lambda i, k: (i, 0))
    deps = [] if dep is None else [dep]

    def kern(a_ref, b_ref, x_ref, w_ref, res_ref, *rest):
        dx_ref, gw_ref, acc = rest[-3:]
        i, k = pl.program_id(0), pl.program_id(1)

        @pl.when((i == 0) & (k == 0))
        def _():
            gw_ref[...] = jnp.zeros_like(gw_ref)

        def product():
            return lax.dot_general(a_ref[...].astype(BF16), b_ref[...].astype(BF16), _DIMS[mode],
                                   preferred_element_type=F32)

        def finish(dyv):
            xv = x_ref[...]
            r = lax.rsqrt(jnp.mean(xv * xv, axis=-1, keepdims=True) + EPS)
            xn = xv * r
            gw_ref[...] += jnp.broadcast_to(jnp.sum(dyv * xn, axis=0, keepdims=True), (8, D))
            dxn = dyv * w_ref[...]
            dx_ref[...] = res_ref[...] + r * (dxn - xn * jnp.mean(dxn * xn, axis=-1, keepdims=True))

        if nk == 1:
            finish(product())
        else:
            @pl.when(k == 0)
            def _():
                acc[...] = jnp.zeros_like(acc)

            acc[...] += product()

            @pl.when(k == nk - 1)
            def _():
                finish(acc[...])

    return pl.pallas_call(
        kern, grid=(M // tm, nk),
        in_specs=[pl.BlockSpec((tm, tk), lambda i, k: (i, k)), b_spec, row, pl.BlockSpec((1, D), lambda i, k: (0, 0)),
                  row] + [pl.BlockSpec(memory_space=pl.ANY)] * len(deps),
        out_specs=(row, pl.BlockSpec((8, D), lambda i, k: (0, 0))),
        out_shape=(jax.ShapeDtypeStruct((M, D), F32), jax.ShapeDtypeStruct((8, D), F32)),
        scratch_shapes=[pltpu.VMEM((tm, D), F32)], name=name, compiler_params=_cp(("arbitrary", "arbitrary")),
    )(a, b, x, w.reshape(1, D), res, *deps)


def _rms_fwd(x, w, name, tl=256, dep=None):
    L = x.shape[0]

    def kern(x_ref, w_ref, *rest):
        o_ref = rest[-1]
        xv = x_ref[...]
        r = lax.rsqrt(jnp.mean(xv * xv, axis=-1, keepdims=True) + EPS)
        o_ref[...] = ((xv * r) * w_ref[...]).astype(BF16)

    row = pl.BlockSpec((tl, D), lambda i: (i, 0))
    deps = [] if dep is None else [dep]
    return pl.pallas_call(
        kern, grid=(L // tl,),
        in_specs=[row, pl.BlockSpec((1, D), lambda i: (0, 0))] + [pl.BlockSpec(memory_space=pl.ANY)] * len(deps),
        out_specs=row, out_shape=jax.ShapeDtypeStruct((L, D), BF16), name=name, compiler_params=_cp(("parallel",)),
    )(x, w.reshape(1, D), *deps)


def _down(v, k):
    if k == 0:
        return v
    t = lax.broadcasted_iota(jnp.int32, v.shape, 0)
    return jnp.where(t >= k, pltpu.roll(v, k, axis=0), 0.0)


def _up(v, k):
    if k == 0:
        return v
    n = v.shape[0]
    t = lax.broadcasted_iota(jnp.int32, v.shape, 0)
    return jnp.where(t < n - k, pltpu.roll(v, n - k, axis=0), 0.0)


TW = 256


def _sc_fwd(proj, cw):
    L = proj.shape[0]
    nb = D // TW

    def kern(b_ref, c_ref, x_ref, w_ref, o_ref):
        u = c_ref[...].astype(F32) * x_ref[...].astype(F32)
        w = w_ref[...]
        cv = w[0:1] * _down(u, 2) + w[1:2] * _down(u, 1) + w[2:3] * u
        o_ref[...] = (b_ref[...].astype(F32) * cv).astype(BF16)

    col = lambda off: pl.BlockSpec((L, TW), lambda j: (0, off + j))
    return pl.pallas_call(
        kern, grid=(nb,), in_specs=[col(0), col(nb), col(2 * nb), pl.BlockSpec((8, TW), lambda j: (0, j))],
        out_specs=pl.BlockSpec((L, TW), lambda j: (0, j)), out_shape=jax.ShapeDtypeStruct((L, D), BF16),
        name="sc_fwd", compiler_params=_cp(("parallel",)),
    )(proj, proj, proj, cw)


def _sc_bwd(dya, proj, cw, dproj):
    L = proj.shape[0]
    nb = D // TW

    def kern(d_ref, b_ref, c_ref, x_ref, w_ref, _, dp_ref, gw_ref, keep):
        sec = pl.program_id(1)

        @pl.when(sec == 0)
        def _():
            cs, xs, dyv = c_ref[...].astype(F32), x_ref[...].astype(F32), d_ref[...]
            w = w_ref[...]
            u = cs * xs
            u1, u2 = _down(u, 1), _down(u, 2)
            cv = w[0:1] * u2 + w[1:2] * u1 + w[2:3] * u
            dcv = dyv * b_ref[...].astype(F32)
            du = w[2:3] * dcv + w[1:2] * _up(dcv, 1) + w[0:1] * _up(dcv, 2)
            g0 = jnp.sum(dcv * u2, axis=0, keepdims=True)
            g1 = jnp.sum(dcv * u1, axis=0, keepdims=True)
            g2 = jnp.sum(dcv * u, axis=0, keepdims=True)
            row = lax.broadcasted_iota(jnp.int32, (8, TW), 0)
            gw_ref[...] = jnp.where(row == 0, g0, jnp.where(row == 1, g1, jnp.where(row == 2, g2, 0.0)))
            dp_ref[...] = (dyv * cv).astype(BF16)
            keep[0] = (du * xs).astype(BF16)
            keep[1] = (du * cs).astype(BF16)

        @pl.when(sec > 0)
        def _():
            dp_ref[...] = keep[sec - 1]

    col = lambda off: pl.BlockSpec((L, TW), lambda j, s: (0, off + j))
    return pl.pallas_call(
        kern, grid=(nb, 3),
        in_specs=[col(0), col(0), col(nb), col(2 * nb), pl.BlockSpec((8, TW), lambda j, s: (0, j)),
                  pl.BlockSpec(memory_space=pl.ANY)],
        out_specs=(pl.BlockSpec((L, TW), lambda j, s: (0, s * nb + j)), pl.BlockSpec((8, TW), lambda j, s: (0, j))),
        out_shape=(jax.ShapeDtypeStruct(dproj.shape, BF16), jax.ShapeDtypeStruct((8, D), F32)),
        scratch_shapes=[pltpu.VMEM((2, L, TW), BF16)],
        input_output_aliases={5: 0}, name="sc_bwd", compiler_params=_cp(("parallel", "arbitrary")),
    )(dya, proj, proj, proj, cw, dproj)


def _ssm_conv_fwd(proj, cw4):
    L = proj.shape[0]
    off = C_XBC // TW

    def kern(r_ref, w_ref, o_ref):
        raw = r_ref[...].astype(F32)
        w = w_ref[...]
        c4 = w[0:1] * _down(raw, 3) + w[1:2] * _down(raw, 2) + w[2:3] * _down(raw, 1) + w[3:4] * raw + w[4:5]
        o_ref[...] = c4 * _sigmoid(c4)

    return pl.pallas_call(
        kern, grid=(XBC // TW,),
        in_specs=[pl.BlockSpec((L, TW), lambda j: (0, off + j)), pl.BlockSpec((8, TW), lambda j: (0, j))],
        out_specs=pl.BlockSpec((L, TW), lambda j: (0, j)), out_shape=jax.ShapeDtypeStruct((L, XBC), F32),
        name="ssm_conv_fwd", compiler_params=_cp(("parallel",)),
    )(proj, cw4)


def _ssm_conv_bwd(dx, proj, cw4, dproj, col0, name):
    L, width = dx.shape
    off_p = (C_XBC + col0) // TW
    off_w = col0 // TW

    def kern(d_ref, r_ref, w_ref, _, dp_ref, gw_ref):
        raw = r_ref[...].astype(F32)
        w = w_ref[...]
        r1, r2, r3 = _down(raw, 1), _down(raw, 2), _down(raw, 3)
        c4 = w[0:1] * r3 + w[1:2] * r2 + w[2:3] * r1 + w[3:4] * raw + w[4:5]
        sg = _sigmoid(c4)
        dc4 = d_ref[...] * (sg * (1.0 + c4 * (1.0 - sg)))
        draw = w[3:4] * dc4 + w[2:3] * _up(dc4, 1) + w[1:2] * _up(dc4, 2) + w[0:1] * _up(dc4, 3)
        dp_ref[...] = draw.astype(BF16)
        gs = [jnp.sum(dc4 * r3, axis=0, keepdims=True), jnp.sum(dc4 * r2, axis=0, keepdims=True),
              jnp.sum(dc4 * r1, axis=0, keepdims=True), jnp.sum(dc4 * raw, axis=0, keepdims=True),
              jnp.sum(dc4, axis=0, keepdims=True)]
        row = lax.broadcasted_iota(jnp.int32, (8, TW), 0)
        acc = jnp.zeros((8, TW), F32)
        for k, gk in enumerate(gs):
            acc = jnp.where(row == k, gk, acc)
        gw_ref[...] = acc

    return pl.pallas_call(
        kern, grid=(width // TW,),
        in_specs=[pl.BlockSpec((L, TW), lambda j: (0, j)), pl.BlockSpec((L, TW), lambda j: (0, off_p + j)),
                  pl.BlockSpec((8, TW), lambda j: (0, off_w + j)), pl.BlockSpec(memory_space=pl.ANY)],
        out_specs=(pl.BlockSpec((L, TW), lambda j: (0, off_p + j)), pl.BlockSpec((8, TW), lambda j: (0, j))),
        out_shape=(jax.ShapeDtypeStruct(dproj.shape, BF16), jax.ShapeDtypeStruct((8, width), F32)),
        input_output_aliases={3: 0}, name=name, compiler_params=_cp(("arbitrary",)),
    )(dx, proj, cw4, dproj)


def _split3(v):
    h1 = v.astype(BF16)
    r1 = v - h1.astype(F32)
    h2 = r1.astype(BF16)
    h3 = (r1 - h2.astype(F32)).astype(BF16)
    return h1, h2, h3


def _dot01(m01, v, dims=_DIMS["nn"], m_left=True, terms=3):
    out = None
    for part in _split3(v)[:terms]:
        ops = (m01, part) if m_left else (part, m01)
        t = lax.dot_general(ops[0], ops[1], dims, preferred_element_type=F32)
        out = t if out is None else out + t
    return out


def _bdot(a, b, mode="nn"):
    return lax.dot_general(a.astype(BF16), b.astype(BF16), _DIMS[mode], preferred_element_type=F32)


def _softplus(v):
    return jnp.maximum(v, 0.0) + jnp.log1p(jnp.exp(-jnp.abs(v)))


def _dt_prep(proj, vec):
    L = proj.shape[0]

    def kern(p_ref, v_ref, dt_ref, cs_ref, sg_ref):
        v = v_ref[...]
        pre = p_ref[:, 0:128] + v[0:1]
        dt = _softplus(pre)
        da = dt * (-jnp.exp(v[1:2]))
        ii = lax.broadcasted_iota(jnp.int32, (Q, Q), 0)
        jj = lax.broadcasted_iota(jnp.int32, (Q, Q), 1)
        ltri = (jj <= ii).astype(BF16)
        lane = lax.broadcasted_iota(jnp.int32, (Q, 128), 1)
        for val, ref in ((dt, dt_ref), (_dot01(ltri, da), cs_ref), (_sigmoid(pre), sg_ref)):
            for g in range(NG):
                moved = val if g == 0 else pltpu.roll(val, 128 - 4 * g, axis=1)
                ref[g] = jnp.where(lane < 4, moved, 0.0)

    blk = pl.BlockSpec((NG, Q, 128), lambda c: (0, c, 0))
    return pl.pallas_call(
        kern, grid=(L // Q,),
        in_specs=[pl.BlockSpec((Q, 256), lambda c: (c, 0)), pl.BlockSpec((8, 128), lambda c: (0, 0))],
        out_specs=(blk, blk, blk),
        out_shape=(jax.ShapeDtypeStruct((NG, L, 128), F32),) * 3,
        name="dt_prep", compiler_params=_cp(("parallel",)),
    )(proj, vec)


def _head_masks():
    lane = lax.broadcasted_iota(jnp.int32, (1, 4 * HD), 1)
    return [((lane >= HD * j) & (lane < HD * (j + 1))) for j in range(4)]


def _expand4(v4, masks):
    R = v4.shape[0]
    out = jnp.zeros((R, 4 * HD), F32)
    for j in range(4):
        out = jnp.where(masks[j], jnp.broadcast_to(v4[:, j:j + 1], (R, 4 * HD)), out)
    return out


def _decay_matrix(cs_col, tri):
    colb = jnp.broadcast_to(cs_col, (Q, Q))
    return jnp.exp(jnp.where(tri, colb - colb.T, -jnp.inf))


def _ssd_fwd(xbc, dt4, cs4, vecg, proj, wn):
    L = xbc.shape[0]
    nc = L // Q
    assert GPS == NG
    zoff = C_Z // 1024

    def kern(x_ref, b_ref, c_ref, dt_ref, cs_ref, v_ref, z0_ref, z1_ref, wn_ref, y_ref, s_ref, yb_ref, S):
        c = pl.program_id(1)

        @pl.when(c == 0)
        def _():
            S[...] = jnp.zeros_like(S)

        masks = _head_masks()
        ii = lax.broadcasted_iota(jnp.int32, (Q, Q), 0)
        jj = lax.broadcasted_iota(jnp.int32, (Q, Q), 1)
        tri = jj <= ii
        for gi in range(GPS):
            xs, ns = slice(256 * gi, 256 * (gi + 1)), slice(NS * gi, NS * (gi + 1))
            dt4v, cs4v = dt_ref[gi], cs_ref[gi]
            dt_b, cs_b = _expand4(dt4v, masks), _expand4(cs4v, masks)
            d_b = _expand4(v_ref[gi], masks)[1:2]
            cs_last = cs_b[Q - 1:Q, :]
            x4, bm, cm = x_ref[:, xs], b_ref[:, ns], c_ref[:, ns]
            xdt = x4 * dt_b
            gm = _bdot(cm, bm, "nt")
            s4 = S[gi]
            s_ref[gi, 0] = s4
            y = _bdot(cm, s4) * jnp.exp(cs_b) + d_b * x4
            m_all = jnp.concatenate([(gm * _decay_matrix(cs4v[:, j:j + 1], tri)).astype(BF16) for j in range(4)], axis=0)
            yd = _bdot(m_all, xdt)
            for j in range(4):
                y = y + jnp.where(masks[j], yd[Q * j:Q * (j + 1)], 0.0)
            y_ref[:, xs] = y
            S[gi] = jnp.exp(cs_last) * s4 + _bdot(bm, xdt * jnp.exp(cs_last - cs_b), "tn")
            zs = slice(GW * (gi % 4), GW * (gi % 4 + 1))
            z = (z0_ref if gi < 4 else z1_ref)[:, zs].astype(F32)
            yz = y * (z * _sigmoid(z))
            rg = lax.rsqrt(jnp.mean(yz * yz, axis=-1, keepdims=True) + EPS)
            yb_ref[:, xs] = ((yz * rg) * wn_ref[:, xs]).astype(BF16)

    sc = pl.BlockSpec((GPS, Q, 128), lambda g, c: (g, c, 0))
    bw = NS * GPS
    wide = pl.BlockSpec((Q, 256 * GPS), lambda g, c: (c, g))
    return pl.pallas_call(
        kern, grid=(NG // GPS, nc),
        in_specs=[wide,
                  pl.BlockSpec((Q, bw), lambda g, c: (c, INNER // bw + g)),
                  pl.BlockSpec((Q, bw), lambda g, c: (c, (INNER + NG * NS) // bw + g)),
                  sc, sc, pl.BlockSpec((GPS, 8, 128), lambda g, c: (g, 0, 0)),
                  pl.BlockSpec((Q, 1024), lambda g, c: (c, zoff)), pl.BlockSpec((Q, 1024), lambda g, c: (c, zoff + 1)),
                  pl.BlockSpec((1, INNER), lambda g, c: (0, 0))],
        out_specs=(wide, pl.BlockSpec((GPS, 1, NS, 256), lambda g, c: (g, c, 0, 0)), wide),
        out_shape=(jax.ShapeDtypeStruct((L, INNER), F32), jax.ShapeDtypeStruct((NG, nc, NS, 256), F32),
                   jax.ShapeDtypeStruct((L, INNER), BF16)),
        scratch_shapes=[pltpu.VMEM((GPS, NS, 256), F32)], name="ssd_fwd",
        compiler_params=_cp(("parallel", "arbitrary")),
    )(xbc, xbc, xbc, dt4, cs4, vecg, proj, proj, wn.reshape(1, INNER))


def _ssd_bwd(xbc, dt4, cs4, sg4, vecg, s_all, dy):
    L = xbc.shape[0]
    nc = L // Q

    def kern(x_ref, b_ref, c_ref, dt_ref, cs_ref, sg_ref, v_ref, s_ref, dy_ref,
             dx_ref, db_ref, dc_ref, ddt_ref, st_ref, dS):
        cc = pl.program_id(1)

        @pl.when(cc == 0)
        def _():
            dS[...] = jnp.zeros_like(dS)
            st_ref[...] = jnp.zeros_like(st_ref)

        masks = _head_masks()
        ii = lax.broadcasted_iota(jnp.int32, (Q, Q), 0)
        jj = lax.broadcasted_iota(jnp.int32, (Q, Q), 1)
        tri = jj <= ii
        utri = (jj >= ii).astype(BF16)
        hsel = ((lax.broadcasted_iota(jnp.int32, (4 * HD, 128), 0) // HD)
                == lax.broadcasted_iota(jnp.int32, (4 * HD, 128), 1)).astype(BF16)
        hrow = ((lax.broadcasted_iota(jnp.int32, (4 * Q, 128), 0) // Q)
                == lax.broadcasted_iota(jnp.int32, (4 * Q, 128), 1)).astype(BF16)
        ones_q = jnp.ones((Q, 128), BF16)
        lane128 = lax.broadcasted_iota(jnp.int32, (Q, 128), 1)

        for gi in range(GPS):
            xs, ns = slice(256 * gi, 256 * (gi + 1)), slice(NS * gi, NS * (gi + 1))
            dt4v, cs4v, sg4v = dt_ref[gi], cs_ref[gi], sg_ref[gi]
            dt_b, cs_b = _expand4(dt4v, masks), _expand4(cs4v, masks)
            vv = _expand4(v_ref[gi], masks)
            a_b = -jnp.exp(vv[0:1])
            d_b = vv[1:2]
            a4 = -jnp.exp(v_ref[gi][0:1, :])
            cs_last = cs_b[Q - 1:Q, :]
            ecs = jnp.exp(cs_b)
            decay = jnp.exp(cs_last - cs_b)
            elast = jnp.exp(cs_last)
            x4, bm, cm, dyv = x_ref[:, xs], b_ref[:, ns], c_ref[:, ns], dy_ref[:, xs]
            s4 = s_ref[gi, 0]
            dsn = dS[gi]
            xdt = x4 * dt_b
            gm = _bdot(cm, bm, "nt")
            dye = dyv * ecs
            yoff = ecs * _bdot(cm, s4)
            t4 = _bdot(bm, dsn) * decay
            lms, mhs = [], []
            for j in range(4):
                colb = jnp.broadcast_to(cs4v[:, j:j + 1], (Q, Q))
                lms.append(jnp.exp(jnp.where(tri, colb - colb.T, -jnp.inf)))
                mhs.append(gm * lms[j])
            m_all = jnp.concatenate([m.astype(BF16) for m in mhs], axis=0)
            dy_m = jnp.concatenate([jnp.where(masks[j], dyv, 0.0).astype(BF16) for j in range(4)], axis=0)
            dxdt = t4 + _bdot(m_all, dy_m, "tn")
            dm_all = _bdot(dy_m, xdt, "nt")
            dg = jnp.zeros((Q, Q), F32)
            for j in range(4):
                dg = dg + dm_all[Q * j:Q * (j + 1)] * lms[j]
            e_all = dm_all * jnp.concatenate(mhs, axis=0)
            rsum = _dot01(ones_q, e_all, m_left=False, terms=2)
            da4 = -_dot01(hrow, e_all, _DIMS["tn"], m_left=False, terms=2)
            for j in range(4):
                da4 = da4 + jnp.where(lane128 == j, rsum[Q * j:Q * (j + 1)], 0.0)
            xt = xdt * t4
            tail = jnp.sum(xt, axis=0, keepdims=True) + elast * jnp.sum(s4 * dsn, axis=0, keepdims=True)
            gd_raw = jnp.sum(dyv * x4, axis=0, keepdims=True)
            stacked = jnp.concatenate([dyv * yoff - xt, dxdt * x4, jnp.broadcast_to(tail, (8, 4 * HD)),
                                       jnp.broadcast_to(gd_raw, (8, 4 * HD))], axis=0)
            seg = _dot01(hsel, stacked, m_left=False, terms=2)
            dda4 = _dot01(utri, da4 + seg[0:Q], terms=2) + seg[2 * Q:2 * Q + 1]
            ddt_ref[gi] = (dda4 * a4 + seg[Q:2 * Q]) * sg4v
            ga = jnp.sum(dda4 * dt4v * a4, axis=0, keepdims=True)
            row = lax.broadcasted_iota(jnp.int32, (8, 128), 0)
            st_ref[gi] += jnp.where(row == 0, ga, jnp.where(row == 1, seg[2 * Q + 8:2 * Q + 9], 0.0))
            dx_ref[:, xs] = d_b * dyv + dxdt * dt_b
            dc_ref[:, ns] = _bdot(dg, bm) + _bdot(dye, s4, "nt")
            db_ref[:, ns] = _bdot(dg, cm, "tn") + _bdot(xdt * decay, dsn, "nt")
            dS[gi] = elast * dsn + _bdot(cm, dye, "tn")

    rv = lambda c: nc - 1 - c
    sc = pl.BlockSpec((GPS, Q, 128), lambda g, c: (g, rv(c), 0))
    bw = NS * GPS
    return pl.pallas_call(
        kern, grid=(NG // GPS, nc),
        in_specs=[pl.BlockSpec((Q, 256 * GPS), lambda g, c: (rv(c), g)),
                  pl.BlockSpec((Q, bw), lambda g, c: (rv(c), INNER // bw + g)),
                  pl.BlockSpec((Q, bw), lambda g, c: (rv(c), (INNER + NG * NS) // bw + g)),
                  sc, sc, sc, pl.BlockSpec((GPS, 8, 128), lambda g, c: (g, 0, 0)),
                  pl.BlockSpec((GPS, 1, NS, 256), lambda g, c: (g, rv(c), 0, 0)),
                  pl.BlockSpec((Q, 256 * GPS), lambda g, c: (rv(c), g))],
        out_specs=(pl.BlockSpec((Q, 256 * GPS), lambda g, c: (rv(c), g)),
                   pl.BlockSpec((Q, bw), lambda g, c: (rv(c), g)),
                   pl.BlockSpec((Q, bw), lambda g, c: (rv(c), g)),
                   pl.BlockSpec((GPS, Q, 128), lambda g, c: (g, rv(c), 0)),
                   pl.BlockSpec((GPS, 8, 128), lambda g, c: (g, 0, 0))),
        out_shape=(jax.ShapeDtypeStruct((L, INNER), F32), jax.ShapeDtypeStruct((L, NG * NS), F32),
                   jax.ShapeDtypeStruct((L, NG * NS), F32), jax.ShapeDtypeStruct((NG, L, 128), F32),
                   jax.ShapeDtypeStruct((NG, 8, 128), F32)),
        scratch_shapes=[pltpu.VMEM((GPS, NS, 256), F32)], name="ssd_bwd",
        compiler_params=_cp(("parallel", "arbitrary")),
    )(xbc, xbc, xbc, dt4, cs4, sg4, vecg, s_all, dy)


def _dt_bwd(ddt, dproj, tl=256):
    L = ddt.shape[1]

    def kern(d_ref, _, dp_ref, gs_ref):
        @pl.when(pl.program_id(0) == 0)
        def _():
            gs_ref[...] = jnp.zeros_like(gs_ref)

        d = d_ref[0]
        for g in range(1, NG):
            d = d + pltpu.roll(d_ref[g], 4 * g, axis=1)
        gs_ref[...] += jnp.broadcast_to(jnp.sum(d, axis=0, keepdims=True), (8, 128))
        dp_ref[...] = jnp.concatenate([d, jnp.zeros_like(d)], axis=1).astype(BF16)

    return pl.pallas_call(
        kern, grid=(L // tl,),
        in_specs=[pl.BlockSpec((NG, tl, 128), lambda i: (0, i, 0)), pl.BlockSpec(memory_space=pl.ANY)],
        out_specs=(pl.BlockSpec((tl, 256), lambda i: (i, C_DT // 256)), pl.BlockSpec((8, 128), lambda i: (0, 0))),
        out_shape=(jax.ShapeDtypeStruct(dproj.shape, BF16), jax.ShapeDtypeStruct((8, 128), F32)),
        input_output_aliases={1: 0}, name="dt_bwd", compiler_params=_cp(("arbitrary",)),
    )(ddt, dproj)


GW = INNER // NG


def _gnorm_bwd(dbr, wb, y, proj, w, dproj, dep, tl=512):
    L = y.shape[0]
    tl = min(tl, L)
    zoff = C_Z // 1024

    def kern(d_ref, b_ref, y_ref, z_ref, w_ref, _, __, dy_ref, dp_ref, gw_ref):
        @pl.when(pl.program_id(1) == 0)
        def _():
            gw_ref[...] = jnp.zeros_like(gw_ref)

        z = z_ref[...].astype(F32)
        sg = _sigmoid(z)
        sz = z * sg
        yv = y_ref[...]
        yz = yv * sz
        dv = lax.dot_general(d_ref[0], b_ref[...], _DIMS["nt"], preferred_element_type=F32)
        wv = w_ref[...]
        for k in range(1024 // GW):
            sl = slice(GW * k, GW * (k + 1))
            v = yz[:, sl]
            rg = lax.rsqrt(jnp.mean(v * v, axis=-1, keepdims=True) + EPS)
            vn = v * rg
            dk = dv[:, sl]
            gw_ref[:, sl] += jnp.broadcast_to(jnp.sum(dk * vn, axis=0, keepdims=True), (8, GW))
            dvn = dk * wv[:, sl]
            dyz = rg * (dvn - vn * jnp.mean(dvn * vn, axis=-1, keepdims=True))
            dy_ref[:, sl] = dyz * sz[:, sl]
            dp_ref[:, sl] = (dyz * yv[:, sl] * (sg[:, sl] * (1.0 + z[:, sl] * (1.0 - sg[:, sl])))).astype(BF16)

    blk = pl.BlockSpec((tl, 1024), lambda j, i: (i, j))
    zblk = pl.BlockSpec((tl, 1024), lambda j, i: (i, zoff + j))
    return pl.pallas_call(
        kern, grid=(2, L // tl),
        in_specs=[pl.BlockSpec((1, tl, D), lambda j, i: (1, i, 0)), pl.BlockSpec((1024, D), lambda j, i: (j, 0)),
                  blk, zblk, pl.BlockSpec((1, 1024), lambda j, i: (0, j)), pl.BlockSpec(memory_space=pl.ANY),
                  pl.BlockSpec(memory_space=pl.ANY)],
        out_specs=(blk, zblk, pl.BlockSpec((8, 1024), lambda j, i: (0, j))),
        out_shape=(jax.ShapeDtypeStruct((L, INNER), F32), jax.ShapeDtypeStruct(dproj.shape, BF16),
                   jax.ShapeDtypeStruct((8, INNER), F32)),
        input_output_aliases={5: 1}, name="gnorm_bwd", compiler_params=_cp(("parallel", "arbitrary")),
    )(dbr, wb, y, proj, w.reshape(1, INNER), dproj, dep)


def _merge_fwd(proj, bg, br_a, br_b, tl=256):
    L = proj.shape[0]
    goff = C_GATE // 1024

    def kern(g1_ref, g2_ref, b1_ref, b2_ref, a_ref, b_ref, o_ref):
        g1 = _sigmoid(g1_ref[...].astype(F32) + b1_ref[...])
        g2 = _sigmoid(g2_ref[...].astype(F32) + b2_ref[...])
        o_ref[...] = (g1 * a_ref[...] + g2 * b_ref[...]).astype(BF16)

    row = pl.BlockSpec((tl, 1024), lambda i: (i, 0))
    bg2 = bg.reshape(1, 2 * D)
    return pl.pallas_call(
        kern, grid=(L // tl,),
        in_specs=[pl.BlockSpec((tl, 1024), lambda i: (i, goff)), pl.BlockSpec((tl, 1024), lambda i: (i, goff + 1)),
                  pl.BlockSpec((1, 1024), lambda i: (0, 0)), pl.BlockSpec((1, 1024), lambda i: (0, 1)), row, row],
        out_specs=row, out_shape=jax.ShapeDtypeStruct((L, D), BF16), name="merge_fwd",
        compiler_params=_cp(("parallel",)),
    )(proj, proj, bg2, bg2, br_a, br_b)


def _merge_bwd(dx1, wo, proj, bg, br_a, br_b, dproj, tl=512):
    L = proj.shape[0]
    tl = min(tl, L)
    goff = C_GATE // 1024

    def kern(dm_ref, wo_ref, g_ref, b_ref, a_ref, bb_ref, _, dbr_ref, dp_ref, gb_ref):
        j = pl.program_id(0)

        @pl.when(pl.program_id(1) == 0)
        def _():
            gb_ref[...] = jnp.zeros_like(gb_ref)

        g = _sigmoid(g_ref[...].astype(F32) + b_ref[...])
        br = jnp.where(j == 0, a_ref[...], bb_ref[...])
        dmv = lax.dot_general(dm_ref[...].astype(BF16), wo_ref[...], _DIMS["nt"], preferred_element_type=F32)
        dbr_ref[0] = (dmv * g).astype(BF16)
        dgate = dmv * br * g * (1.0 - g)
        gb_ref[...] += jnp.broadcast_to(jnp.sum(dgate, axis=0, keepdims=True), (8, 1024))
        dp_ref[...] = dgate.astype(BF16)

    row = pl.BlockSpec((tl, 1024), lambda j, i: (i, 0))
    gblk = pl.BlockSpec((tl, 1024), lambda j, i: (i, goff + j))
    return pl.pallas_call(
        kern, grid=(2, L // tl),
        in_specs=[row, pl.BlockSpec((D, D), lambda j, i: (0, 0)), gblk, pl.BlockSpec((1, 1024), lambda j, i: (0, j)),
                  row, row, pl.BlockSpec(memory_space=pl.ANY)],
        out_specs=(pl.BlockSpec((1, tl, 1024), lambda j, i: (j, i, 0)), gblk, pl.BlockSpec((8, 1024), lambda j, i: (0, j))),
        out_shape=(jax.ShapeDtypeStruct((2, L, D), BF16), jax.ShapeDtypeStruct(dproj.shape, BF16),
                   jax.ShapeDtypeStruct((8, 2 * D), F32)),
        input_output_aliases={6: 1}, name="merge_bwd", compiler_params=_cp(("parallel", "arbitrary")),
    )(dx1, wo, proj, bg.reshape(1, 2 * D), br_a, br_b, dproj)


def _coords():
    return lax.axis_index("x"), lax.axis_index("y"), lax.axis_index("c")


def _other_chips(sk):
    xk, yk = sk // 2, sk % 2
    return [((1 - xk, yk), 2 * (1 - xk) + yk), ((xk, 1 - yk), 2 * xk + 1 - yk), ((1 - xk, 1 - yk), 2 * (1 - xk) + 1 - yk)]


def _rows(start, size):
    assert size % 128 == 0
    return pl.ds(pl.multiple_of(start, 128), size)


def _per_chip(fn):
    x, y, _ = _coords()
    s = 2 * x + y
    for sk in range(4):
        pl.when(s == sk)(functools.partial(fn, sk))


XTRA = PIECE - PMAIN


def _place(shard, full_shape, block, index_map, idx, name, blk0=0, nblk=None, dep=None, into=None):
    in_block = block[-2:]
    if nblk is None:
        nblk = shard.shape[0] // in_block[0]

    def kern(idx_ref, s_ref, *rest):
        o_ref = rest[-1]
        o_ref[...] = s_ref[...].astype(BF16).reshape(o_ref.shape)

    extra = ([dep] if dep is not None else []) + ([into] if into is not None else [])
    grid_spec = pltpu.PrefetchScalarGridSpec(
        num_scalar_prefetch=1, grid=(nblk,),
        in_specs=[pl.BlockSpec(in_block, lambda i, idx_ref: (blk0 + i, 0))] + [_ANY] * len(extra),
        out_specs=pl.BlockSpec(block, index_map))
    aliases = {1 + len(extra): 0} if into is not None else {}
    return pl.pallas_call(kern, grid_spec=grid_spec, out_shape=jax.ShapeDtypeStruct(full_shape, BF16), name=name,
                          input_output_aliases=aliases, compiler_params=_cp(("arbitrary",)))(idx, shard, *extra)


_SEM = pl.BlockSpec(memory_space=pltpu.SEMAPHORE)
_EFFECT = pltpu.SideEffectType.DATAFLOW_SIDE_EFFECTING


_ANY = pl.BlockSpec(memory_space=pl.ANY)


def _tie(v, dep, name):
    def body(v_ref, dep_ref, o_ref):
        del v_ref, dep_ref, o_ref

    return pl.pallas_call(body, out_shape=jax.ShapeDtypeStruct(v.shape, v.dtype), in_specs=[_ANY, _ANY],
                          out_specs=_ANY, input_output_aliases={0: 0}, name=name)(v, dep)


def _split_call(name, arrays, start=None, wait=None, wait_sems=None, after=None):
    keys = list(arrays)
    n = len(keys)
    n_start = start.n if start is not None else 0
    afters = [] if after is None else (list(after) if isinstance(after, (list, tuple)) else [after])

    def body(*refs):
        pos = n
        if wait is not None:
            wss, wrs = refs[pos], refs[pos + 1]
            pos += 2
        pos += len(afters)
        if start is not None:
            nss, nrs = refs[pos], refs[pos + 1]
            pos += 2
        R = dict(zip(keys, refs[pos:pos + n]))
        token = refs[pos + n]
        x, y, c = _coords()

        def desc(src, dst, dev, ss, rs, k):
            return pltpu.make_async_remote_copy(src_ref=src, dst_ref=dst, send_sem=ss.at[k], recv_sem=rs.at[k],
                                                device_id=dev, device_id_type=MESH)

        def run(sk):
            if wait is not None:
                for k, (snd, land) in enumerate(wait.copies(sk, R)):
                    if snd is not None:
                        desc(snd[0], snd[1], snd[2], wss, wrs, k).wait_send()
                    if land is not None:
                        desc(land, land, (x, y, c), wss, wrs, k).wait_recv()
            if start is not None:
                for k, (snd, land) in enumerate(start.copies(sk, R)):
                    if snd is not None:
                        desc(snd[0], snd[1], snd[2], nss, nrs, k).start()

        _per_chip(run)
        token[...] = jnp.zeros_like(token)

    hbm = pl.BlockSpec(memory_space=HBM)
    vals = [arrays[k] for k in keys]
    ins, in_specs = list(vals), [hbm] * n
    if wait is not None:
        ins += list(wait_sems)
        in_specs += [_SEM, _SEM]
    ins += afters
    in_specs += [pl.BlockSpec(memory_space=pl.ANY)] * len(afters)
    out_shape, out_specs = [], []
    if start is not None:
        out_shape += [pltpu.SemaphoreType.DMA((n_start,)), pltpu.SemaphoreType.DMA((n_start,))]
        out_specs += [_SEM, _SEM]
    first = len(out_shape)
    out_shape += [jax.ShapeDtypeStruct(v.shape, v.dtype) for v in vals] + [jax.ShapeDtypeStruct((8, 128), F32)]
    out_specs += [hbm] * n + [pl.BlockSpec(memory_space=pltpu.VMEM)]
    res = pl.pallas_call(
        body, out_shape=tuple(out_shape), in_specs=in_specs, out_specs=tuple(out_specs),
        input_output_aliases={i: first + i for i in range(n)}, name=name,
        compiler_params=pltpu.CompilerParams(has_side_effects=_EFFECT),
    )(*ins)
    sems = (res[0], res[1]) if start is not None else None
    return dict(zip(keys, res[first:first + n])), sems, res[-1]


class _Plan:
    def __init__(self, n, copies):
        self.n, self.copies = n, copies


_HM, _HX = PMAIN // 2, XTRA // 2
WAVE0 = 768
WAVES = ((0, WAVE0), (WAVE0, _HM - WAVE0))
_WIN = {
    "wq0": (True, "wct", lambda r, sc, hc: r.at[_rows(PMAIN * sc + _HM * hc + WAVES[0][0], WAVES[0][1]), :]),
    "wq1": (True, "wct", lambda r, sc, hc: r.at[_rows(PMAIN * sc + _HM * hc + WAVES[1][0], WAVES[1][1]), :]),
    "xt": (True, "xt", lambda r, sc, hc: r.at[sc, _rows(_HX * hc, _HX), :]),
    "w1": (True, "w1", lambda r, sc, hc: r.at[_rows(512 * hc, 512), pl.ds(1024 * sc, 1024)]),
    "w2": (True, "w2", lambda r, sc, hc: r.at[_rows(1024 * sc + 512 * hc, 512), :]),
    "wa": (True, "wa", lambda r, sc, hc: r.at[_rows(256 * sc + 128 * hc, 128), :]),
    "wb": (True, "wb", lambda r, sc, hc: r.at[_rows(512 * sc + 256 * hc, 256), :]),
    "wo": (True, "wo", lambda r, sc, hc: r.at[_rows(256 * sc + 128 * hc, 128), :]),
    "cw": (False, "cw", lambda r, sc, hc: r.at[sc]),
}


_PIECE_SRC = {
    "wq0": lambda p, hc: p.at[_rows(_HM * hc + WAVES[0][0], WAVES[0][1]), :],
    "wq1": lambda p, hc: p.at[_rows(_HM * hc + WAVES[1][0], WAVES[1][1]), :],
    "xt": lambda p, hc: p.at[_rows(PMAIN + _HX * hc, _HX), :],
}


def _ag_chips_plan(keys):
    def copies(sk, R):
        _, _, c = _coords()
        out = []
        for key in keys:
            _, arr, win = _WIN[key]
            for (px, py), ps in _other_chips(sk):
                dst = win(R[arr], sk, c)
                src = _PIECE_SRC[key](R["piece"], c) if key in _PIECE_SRC else dst
                out.append(((src, dst, (px, py, c)), win(R[arr], ps, c)))
        return out
    return _Plan(3 * len(keys), copies)


def _ag_sibling_plan(keys):
    keys = [k for k in keys if _WIN[k][0]]

    def copies(sk, R):
        x, y, c = _coords()
        out = []
        for key in keys:
            _, arr, win = _WIN[key]
            for _, ps in _other_chips(sk):
                w = win(R[arr], ps, c)
                out.append(((w, w, (x, y, 1 - c)), win(R[arr], ps, 1 - c)))
        return out
    return _Plan(3 * len(keys), copies)


def _in_proj_wave(h, wct, wave, proj=None, tm=2048):
    L = h.shape[0]
    tm = min(tm, L)
    off, size = WAVES[wave]
    start = lambda j: pl.multiple_of(_HM * j + off, 128)

    def kern(h_ref, w_ref, *rest):
        o_ref = rest[-1]
        o_ref[...] = lax.dot_general(h_ref[...], w_ref[...], _DIMS["nt"], preferred_element_type=F32).astype(BF16)

    in_specs = [pl.BlockSpec((tm, D), lambda j, i: (i, 0)),
                pl.BlockSpec((pl.Element(size), pl.Element(D)), lambda j, i: (start(j), 0))]
    args, aliases = [h, wct], {}
    if proj is not None:
        in_specs.append(pl.BlockSpec(memory_space=pl.ANY))
        args.append(proj)
        aliases = {2: 0}
    return pl.pallas_call(
        kern, grid=(8, L // tm), in_specs=in_specs,
        out_specs=pl.BlockSpec((pl.Element(tm), pl.Element(size)), lambda j, i: (i * tm, start(j))),
        out_shape=jax.ShapeDtypeStruct((L, NCW), BF16), input_output_aliases=aliases,
        name="in_proj_wave%d" % wave, compiler_params=_cp(("parallel", "parallel")),
    )(*args)


def _fix_wct(wct, xt):
    nb = PMAIN // XTRA

    def kern(w_ref, x_ref, o_ref):
        k = pl.program_id(0)
        xv = x_ref[0]
        o_ref[...] = jnp.where(k < 3, (w_ref[...].astype(F32) + xv.astype(F32)).astype(BF16), xv)

    blk = pl.BlockSpec((XTRA, D), lambda k: (nb * (k + 1), 0))
    rblk = pl.BlockSpec((XTRA, D), lambda k: (jnp.where(k < 3, nb * (k + 1), 0), 0))
    return pl.pallas_call(
        kern, grid=(4,), in_specs=[rblk, pl.BlockSpec((1, XTRA, D), lambda k: (k, 0, 0))], out_specs=blk,
        out_shape=jax.ShapeDtypeStruct(wct.shape, BF16), input_output_aliases={0: 0}, name="fix_wct",
        compiler_params=_cp(("arbitrary",)),
    )(wct, xt)


_HP = PIECE // 2
_GWIN = [
    lambda r, sc, hc: r.at[_rows(PMAIN * sc + _HP * hc, _HP), :],
    lambda r, sc, hc: r.at[_rows(512 * hc, 512), pl.ds(1024 * sc, 1024)],
    lambda r, sc, hc: r.at[_rows(1024 * sc + 512 * hc, 512), :],
    lambda r, sc, hc: r.at[_rows(256 * sc + 128 * hc, 128), :],
    lambda r, sc, hc: r.at[_rows(512 * sc + 256 * hc, 256), :],
    lambda r, sc, hc: r.at[_rows(256 * sc + 128 * hc, 128), :],
]
HALF_SHAPES = [(PIECE // 2, D), (512, 1024), (512, 1024), (128, 1024), (256, 1024), (128, 1024)]


def _rs_sibling_plan(ts):
    def copies(sk, R):
        x, y, c = _coords()
        out = []
        for t in ts:
            for sc in range(4):
                land = R["ra%d" % t].at[sc]
                out.append(((_GWIN[t](R["g%d" % t], sc, 1 - c), land, (x, y, 1 - c)), land))
        return out
    return _Plan(4 * len(ts), copies)


def _rs_chips_plan(ts):
    def copies(sk, R):
        _, _, c = _coords()
        out = []
        for t in ts:
            for j, ((px, py), ps) in enumerate(_other_chips(sk)):
                land = R["rb%d" % t].at[j]
                out.append(((R["hb%d" % t].at[ps], land, (px, py, c)), land))
        return out
    return _Plan(3 * len(ts), copies)


def _rs_share_plan(ts):
    def copies(sk, R):
        x, y, c = _coords()
        out = []
        for t in ts:
            rows = HALF_SHAPES[t][0]
            mine = R["f%d" % t].at[_rows(rows * c, rows), :]
            out.append(((mine, mine, (x, y, 1 - c)), R["f%d" % t].at[_rows(rows * (1 - c), rows), :]))
        return out
    return _Plan(len(ts), copies)


def _half_tiling(t):
    rows, cols = HALF_SHAPES[t]
    if t == 0:
        return (rows // 2, cols), 2, lambda i: (i, 0)
    return (rows, cols), 1, lambda i: (0, 0)


def _window_spec(t, blk):
    if t == 0:
        return pl.BlockSpec((pl.Element(blk[0]), pl.Element(blk[1])), lambda i, sc, idx_ref: (
            pl.multiple_of(PMAIN * sc + _HP * idx_ref[1] + blk[0] * i, 128), 0))
    if t == 1:
        return pl.BlockSpec(blk, lambda i, sc, idx_ref: (idx_ref[1], sc))
    return pl.BlockSpec(blk, lambda i, sc, idx_ref: (2 * sc + idx_ref[1], 0))


def _chip_sum(g, ra, t, idx, name):
    rows, cols = HALF_SHAPES[t]
    blk, nblk, inner = _half_tiling(t)

    def kern(idx_ref, g_ref, r_ref, hb_ref, hf_ref):
        v = g_ref[...].astype(F32) + r_ref[0].astype(F32)
        hb_ref[0] = v.astype(BF16)

        @pl.when(pl.program_id(1) == idx_ref[0])
        def _():
            hf_ref[...] = v

    omap = lambda i, sc, idx_ref: (sc,) + inner(i)
    grid_spec = pltpu.PrefetchScalarGridSpec(
        num_scalar_prefetch=1, grid=(nblk, 4),
        in_specs=[_window_spec(t, blk), pl.BlockSpec((1,) + blk, omap)],
        out_specs=(pl.BlockSpec((1,) + blk, omap), pl.BlockSpec(blk, lambda i, sc, idx_ref: inner(i))))
    return pl.pallas_call(
        kern, grid_spec=grid_spec,
        out_shape=(jax.ShapeDtypeStruct((4, rows, cols), BF16), jax.ShapeDtypeStruct((rows, cols), F32)),
        name=name, compiler_params=_cp(("parallel", "arbitrary")),
    )(idx, g, ra)


def _final_sum(hf, rb, t, idx, name):
    rows, cols = HALF_SHAPES[t]
    blk, nblk, inner = _half_tiling(t)
    nbr = rows // blk[0]

    def kern(idx_ref, h_ref, r_ref, o_ref):
        o_ref[...] = ((h_ref[...] + r_ref[0].astype(F32)) + r_ref[1].astype(F32)) + r_ref[2].astype(F32)

    def omap(i, idx_ref):
        r, cidx = inner(i)
        return nbr * idx_ref[1] + r, cidx

    grid_spec = pltpu.PrefetchScalarGridSpec(
        num_scalar_prefetch=1, grid=(nblk,),
        in_specs=[pl.BlockSpec(blk, lambda i, idx_ref: inner(i)),
                  pl.BlockSpec((3,) + blk, lambda i, idx_ref: (0,) + inner(i))],
        out_specs=pl.BlockSpec(blk, omap))
    return pl.pallas_call(
        kern, grid_spec=grid_spec, out_shape=jax.ShapeDtypeStruct((2 * rows, cols), F32),
        name=name, compiler_params=_cp(("parallel",)),
    )(idx, hf, rb)


class _ReduceScatter:
    def __init__(self, ts, grads, idx, tag):
        self.ts, self.idx, self.tag = ts, idx, tag
        arr = {}
        for t in ts:
            arr["g%d" % t] = grads[t]
            arr["ra%d" % t] = lax.empty((4,) + HALF_SHAPES[t], BF16)
        self.plan = _rs_sibling_plan(ts)
        self.arr, self.sems, self.token = _split_call("rs_sibling_start_" + tag, arr, start=self.plan)

    def chips(self, after):
        arr, _, _ = _split_call("rs_sibling_wait_" + self.tag, self.arr, wait=self.plan, wait_sems=self.sems, after=after)
        brr, self.hf = {}, {}
        for t in self.ts:
            hb, self.hf[t] = _chip_sum(arr["g%d" % t], arr["ra%d" % t], t, self.idx, "chip_sum_%d" % t)
            brr["hb%d" % t] = hb
            brr["rb%d" % t] = lax.empty((3,) + HALF_SHAPES[t], BF16)
        self.plan = _rs_chips_plan(self.ts)
        self.arr, self.sems, self.token = _split_call("rs_chips_start_" + self.tag, brr, start=self.plan)
        return self.token

    def share(self, after):
        brr, _, _ = _split_call("rs_chips_wait_" + self.tag, self.arr, wait=self.plan, wait_sems=self.sems, after=after)
        frr = {"f%d" % t: _final_sum(self.hf[t], brr["rb%d" % t], t, self.idx, "final_sum_%d" % t) for t in self.ts}
        self.plan = _rs_share_plan(self.ts)
        self.arr, self.sems, self.token = _split_call("rs_share_start_" + self.tag, frr, start=self.plan)
        return self.token

    def result(self, after):
        frr, _, _ = _split_call("rs_share_wait_" + self.tag, self.arr, wait=self.plan, wait_sems=self.sems, after=after)
        return {t: frr["f%d" % t] for t in self.ts}


def _all8_plan(key):
    def copies(sk, R):
        x, y, c = _coords()
        own = R[key].at[4 * x + 2 * y + c]
        out = []
        for k in range(1, 8):
            dev = ((1 - x) if (k >> 2) & 1 else x, (1 - y) if (k >> 1) & 1 else y, (1 - c) if k & 1 else c)
            out.append(((own, own, dev), R[key].at[4 * dev[0] + 2 * dev[1] + dev[2]]))
        return out
    return _Plan(7, copies)


def _sum8(v, name="small_sum"):
    def kern(v_ref, o_ref):
        acc = v_ref[0]
        for k in range(1, 8):
            acc = acc + v_ref[k]
        o_ref[...] = acc

    return pl.pallas_call(kern, out_shape=jax.ShapeDtypeStruct(v.shape[1:], F32), name=name)(v)


def _adamw(w, g, m, v, name, tr=128, blk0=0, nblk=None, into=None, copy_g=False):
    R, C = w.shape
    tr = min(tr, R)
    if nblk is None:
        assert R % tr == 0 and blk0 == 0
        nblk = R // tr
    n_out = 4 if copy_g else 3

    def kern(*refs):
        w_ref, g_ref, m_ref, v_ref = refs[:4]
        d_ref, mo_ref, vo_ref = refs[-n_out:][:3]
        gv = g_ref[...]
        mn = ADAM_B1 * m_ref[...] + (1.0 - ADAM_B1) * gv
        vn = ADAM_B2 * v_ref[...] + (1.0 - ADAM_B2) * (gv * gv)
        m_hat = mn / (1.0 - ADAM_B1 ** ADAM_STEP)
        v_hat = vn / (1.0 - ADAM_B2 ** ADAM_STEP)
        d_ref[...] = -ADAM_LR * (m_hat / (jnp.sqrt(v_hat) + ADAM_EPS) + ADAM_WD * w_ref[...])
        mo_ref[...] = mn
        vo_ref[...] = vn
        if copy_g:
            refs[-1][...] = gv

    blk = pl.BlockSpec((tr, C), lambda i: (blk0 + i, 0))
    sd = jax.ShapeDtypeStruct((R, C), F32)
    in_specs, args, aliases = [blk] * 4, [w, g, m, v], {}
    if into is not None:
        in_specs += [pl.BlockSpec(memory_space=pl.ANY)] * 3
        args += list(into)
        aliases = {4: 0, 5: 1, 6: 2}
    return pl.pallas_call(kern, grid=(nblk,), in_specs=in_specs, out_specs=(blk,) * n_out, out_shape=(sd,) * n_out,
                          input_output_aliases=aliases, name=name, compiler_params=_cp(("parallel",)))(*args)


def _adamw_w_in(wt, gp, mt, vt, offs, name, r0, tr, nblk, views, into=None, blk_key=None):
    el = lambda n: (pl.Element(n), pl.Element(D))
    first = (lambda o: r0) if blk_key is None else (lambda o: tr * o[blk_key])
    own = pl.BlockSpec(el(tr), lambda i, o: (pl.multiple_of(first(o) + tr * i, 8), 0))

    def view(k):
        return pl.BlockSpec(el(tr), lambda i, o: (pl.multiple_of(jnp.maximum(first(o) + tr * i + o[k], 0), 8), 0))

    def kern(o_ref, w_ref, m_ref, v_ref, *refs):
        g_refs, (d_ref, mo_ref, vo_ref, go_ref) = refs[:len(views)], refs[-4:]
        gv = g_refs[0][...]
        if len(views) == 2:
            row = first(o_ref) + tr * pl.program_id(0) + lax.broadcasted_iota(jnp.int32, (tr, D), 0)
            gv = jnp.where(row < o_ref[2], gv, g_refs[1][...])
        mn = ADAM_B1 * m_ref[...] + (1.0 - ADAM_B1) * gv
        vn = ADAM_B2 * v_ref[...] + (1.0 - ADAM_B2) * (gv * gv)
        m_hat = mn / (1.0 - ADAM_B1 ** ADAM_STEP)
        v_hat = vn / (1.0 - ADAM_B2 ** ADAM_STEP)
        d_ref[...] = -ADAM_LR * (m_hat / (jnp.sqrt(v_hat) + ADAM_EPS) + ADAM_WD * w_ref[...])
        mo_ref[...] = mn
        vo_ref[...] = vn
        go_ref[...] = gv

    in_specs = [own, own, own] + [view(k) for k in views]
    args = [wt, mt, vt] + [gp] * len(views)
    aliases = {}
    if into is not None:
        in_specs += [pl.BlockSpec(memory_space=pl.ANY)] * 4
        args += list(into)
        aliases = {1 + len(args) - 4 + j: j for j in range(4)}
    grid_spec = pltpu.PrefetchScalarGridSpec(num_scalar_prefetch=1, grid=(nblk,), in_specs=in_specs,
                                             out_specs=(own,) * 4)
    sd = jax.ShapeDtypeStruct(wt.shape, F32)
    return pl.pallas_call(kern, grid_spec=grid_spec, out_shape=(sd,) * 4, input_output_aliases=aliases, name=name,
                          compiler_params=_cp(("parallel",)))(offs, *args)


def _to_piece(wt, s):
    z = lambda n: jnp.zeros((n, D), wt.dtype)
    pads = [functools.partial(lambda k, w: jnp.pad(w, ((8 * k, PIECE - W_SHARD - 8 * k), (0, 0))).astype(BF16), k)
            for k in range(3)]
    last = lambda w: jnp.concatenate([z(24), w[:744], w[776:], w[744:776], z(PIECE - 24 - W_SHARD)], axis=0).astype(BF16)
    return lax.switch(s, pads + [last], wt)


_SMALL = [("b_gate", 2048), ("ssm_conv_b", 4096), ("dt_bias", 32), ("A_log", 32), ("D_skip", 32),
          ("ssm_norm_w", 2048), ("norm_mlp", 1024), ("norm_final", 1024), ("sc_conv_w", 3072), ("ssm_conv_w", 16384),
          ("loss", 1)]


def _pack(vals, table, rows):
    parts = []
    for name, n in table:
        v = vals[name].reshape(-1).astype(F32)
        pad = (-n) % 128
        parts.append(jnp.pad(v, (0, pad)) if pad else v)
    flat = jnp.concatenate(parts)
    return jnp.pad(flat, (0, rows * 128 - flat.shape[0])).reshape(rows, 128)


def _unpack(arr, table):
    flat = arr.reshape(-1)
    out, off = {}, 0
    for name, n in table:
        out[name] = flat[off:off + n]
        off += n + ((-n) % 128)
    return out


def kernel(x, norm_mix, w_in, b_gate, sc_conv_w, ssm_conv_w, ssm_conv_b, dt_bias, A_log, D_skip, ssm_norm_w, w_branch_sc, w_branch_ssm, w_out, norm_mlp, w_mlp1, w_mlp2, norm_final, loss_target, m_norm_mix, m_w_in, m_b_gate, m_sc_conv_w, m_ssm_conv_w, m_ssm_conv_b, m_dt_bias, m_A_log, m_D_skip, m_ssm_norm_w, m_w_branch_sc, m_w_branch_ssm, m_w_out, m_norm_mlp, m_w_mlp1, m_w_mlp2, m_norm_final, v_norm_mix, v_w_in, v_b_gate, v_sc_conv_w, v_ssm_conv_w, v_ssm_conv_b, v_dt_bias, v_A_log, v_D_skip, v_ssm_norm_w, v_w_branch_sc, v_w_branch_ssm, v_w_out, v_norm_mlp, v_w_mlp1, v_w_mlp2, v_norm_final):
    L = x.shape[1]
    nc = L // Q
    xi, yi, ci = lax.axis_index("x"), lax.axis_index("y"), lax.axis_index("c")
    s = 2 * xi + yi
    idx = jnp.stack([s, ci]).astype(jnp.int32)
    x0 = x.reshape(L, D)
    tgt = loss_target.reshape(L, D)
    small_names = ["b_gate", "sc_conv_w", "ssm_conv_w", "ssm_conv_b", "dt_bias", "A_log", "D_skip", "ssm_norm_w",
                   "norm_mlp", "norm_final"]
    small_wmv = [dict(zip(small_names, vals)) for vals in (
        (b_gate, sc_conv_w, ssm_conv_w, ssm_conv_b, dt_bias, A_log, D_skip, ssm_norm_w, norm_mlp, norm_final),
        (m_b_gate, m_sc_conv_w, m_ssm_conv_w, m_ssm_conv_b, m_dt_bias, m_A_log, m_D_skip, m_ssm_norm_w, m_norm_mlp,
         m_norm_final),
        (v_b_gate, v_sc_conv_w, v_ssm_conv_w, v_ssm_conv_b, v_dt_bias, v_A_log, v_D_skip, v_ssm_norm_w, v_norm_mlp,
         v_norm_final))]
    small_table = [(n, int(small_wmv[0][n].size)) for n in small_names]
    small_rows = 136
    pk_w, pk_m, pk_v = [_pack(d, small_table, small_rows) for d in small_wmv]

    piece = _to_piece(w_in.T, s)
    nb = PMAIN // XTRA
    cws = jnp.zeros((8, 1280), F32)
    cws = cws.at[0:3, 0:256].set(sc_conv_w).at[0:4, 256:1280].set(ssm_conv_w)
    cw0 = lax.dynamic_update_slice(jnp.zeros((4, 8, 1280), F32), cws[None], (s, 0, 0))
    win_keys, win2_keys, mid_keys, end_keys = ["xt", "cw", "wq0"], ["wq1"], ["wa", "wb", "wo", "w1"], ["w2"]
    gw, sems_w, tok = _split_call(
        "ag_win_start", {"wct": lax.empty((NCW, D), BF16), "xt": lax.empty((4, XTRA, D), BF16), "cw": cw0, "piece": piece},
        start=_ag_chips_plan(win_keys))
    g2, sems_w2, tok = _split_call("ag_win2_start", {"wct": gw["wct"], "piece": gw["piece"]},
                                   start=_ag_chips_plan(win2_keys), after=tok)
    piece = g2["piece"]
    gw["wct"] = _place(piece, (NCW, D), (XTRA, D), lambda i, r: (nb * r[0] + i, 0), idx, "place_wct", nblk=nb,
                       dep=tok, into=g2["wct"])
    gw["xt"] = _place(piece, (4, XTRA, D), (1, XTRA, D), lambda i, r: (r[0], 0, 0), idx, "place_xt", blk0=nb, nblk=1,
                      dep=tok, into=gw["xt"])
    gw["piece"] = piece
    wa0 = _place(w_branch_sc, (D, D), (256, 1024), lambda i, r: (r[0], 0), idx, "place_wa", dep=tok)
    wb0 = _place(w_branch_ssm, (INNER, D), (512, 1024), lambda i, r: (r[0], 0), idx, "place_wb", dep=tok)
    wo0 = _place(w_out, (D, D), (256, 1024), lambda i, r: (r[0], 0), idx, "place_wo", dep=tok)
    w10 = _place(w_mlp1, (D, DFF), (256, 1024), lambda i, r: (i, r[0]), idx, "place_w1", dep=tok)
    gm, sems_m, tok = _split_call("ag_mid_start", {"wa": wa0, "wb": wb0, "wo": wo0, "w1": w10},
                                  start=_ag_chips_plan(mid_keys))
    w20 = _place(w_mlp2, (DFF, D), (256, 1024), lambda i, r: (4 * r[0] + i, 0), idx, "place_w2", dep=tok)
    ge, sems_e, tok = _split_call("ag_end_start", {"w2": w20}, start=_ag_chips_plan(end_keys))
    h = _rms_fwd(x0, norm_mix, "rms_mix", dep=tok)
    gw, sems_w, tok = _split_call("ag_win_pass", gw, wait=_ag_chips_plan(win_keys), wait_sems=sems_w,
                                  start=_ag_sibling_plan(win_keys), after=[h, pk_w, pk_m, pk_v])
    gw, _, _ = _split_call("ag_win_done", gw, wait=_ag_sibling_plan(win_keys), wait_sems=sems_w, after=tok)
    wc, cw_all = _fix_wct(gw["wct"], gw["xt"]), gw["cw"]
    sc_w_full = jnp.concatenate([cw_all[k, :, 0:256] for k in range(4)], axis=1)
    ssm_w_full = jnp.concatenate([cw_all[k, :, 256:1280] for k in range(4)], axis=1)
    cw4 = ssm_w_full.at[4].set(ssm_conv_b)
    vec = jnp.zeros((8, 128), F32).at[0, :NH].set(dt_bias).at[1, :NH].set(A_log)
    vecg = jnp.zeros((NG, 8, 128), F32).at[:, 0, :4].set(A_log.reshape(NG, 4)).at[:, 1, :4].set(D_skip.reshape(NG, 4))

    dtraw = _matmul(h, wc[C_DT:], "nt", F32, 512, 256, 1024, "in_proj_dt")
    proj = _in_proj_wave(h, wc, 0)
    g2, sems_w2, tok = _split_call("ag_win2_pass", {"wct": wc, "piece": gw["piece"]},
                                   wait=_ag_chips_plan(win2_keys), wait_sems=sems_w2,
                                   start=_ag_sibling_plan(win2_keys), after=[proj, dtraw])
    g2, _, _ = _split_call("ag_win2_done", g2, wait=_ag_sibling_plan(win2_keys), wait_sems=sems_w2, after=tok)
    wc = g2["wct"]
    proj = _in_proj_wave(h, wc, 1, proj=proj)
    xbc = _ssm_conv_fwd(proj, cw4)
    dt4, cs4, sg4 = _dt_prep(dtraw, vec)
    y, s_all, yb = _ssd_fwd(xbc, dt4, cs4, vecg, proj, ssm_norm_w)
    gm, sems_m, tok = _split_call("ag_mid_pass", gm, wait=_ag_chips_plan(mid_keys), wait_sems=sems_m,
                                  start=_ag_sibling_plan(mid_keys), after=yb)
    sc_w_full = _tie(sc_w_full, tok, "tie_sc_w")
    ya = _sc_fwd(proj, sc_w_full)
    gm, _, _ = _split_call("ag_mid_done", gm, wait=_ag_sibling_plan(mid_keys), wait_sems=sems_m, after=ya)
    wa, wb, wo, w1 = gm["wa"], gm["wb"], gm["wo"], gm["w1"]
    ge, sems_e, tok = _split_call("ag_end_pass", ge, wait=_ag_chips_plan(end_keys), wait_sems=sems_e,
                                  start=_ag_sibling_plan(end_keys), after=yb)
    br_a = _matmul(ya, wa, "nn", F32, 1024, 1024, 1024, "branch_sc", dep=tok)
    br_b = _matmul(yb, wb, "nn", F32, 1024, 1024, 2048, "branch_ssm")
    merged = _merge_fwd(proj, b_gate, br_a, br_b)
    x1, h2 = _matmul_res_rms(merged, wo, x0, norm_mlp, 1024, "out_proj")
    a1, rl = _matmul(h2, w1, "nn", BF16, 1024, 1024, 1024, "mlp1", epi="relu2", n_outer=True)
    ge, _, _ = _split_call("ag_end_done", ge, wait=_ag_sibling_plan(end_keys), wait_sems=sems_e, after=a1)
    w2 = ge["w2"]
    dx2, g_nf, loss8 = _matmul_res_final(rl, w2, x1, norm_final, tgt, 512, "mlp2")

    da = _matmul(dx2, w2, "nt", BF16, 1024, 1024, 1024, "mlp2_dx", epi="drelu", extra=a1, n_outer=True)
    g_w2 = _matmul(rl, dx2, "tn", BF16, 1024, 1024, 2048, "mlp2_dw")
    g_w1 = _matmul(h2, da, "tn", BF16, 1024, 1024, 2048, "mlp1_dw")
    dx1, g_nmlp = _matmul_rms_bwd(da, w1, "nt", x1, norm_mlp, dx2, 512, 4096, "mlp1_dx")
    g_wo = _matmul(merged, dx1, "tn", BF16, 1024, 1024, 2048, "out_proj_dw")
    dproj = lax.empty((L, NCW), BF16)
    dbr, dproj, g_bg = _merge_bwd(dx1, wo, proj, b_gate, br_a, br_b, dproj)
    dya = _matmul(dbr[0], wa, "nt", F32, 1024, 1024, 1024, "branch_sc_dx")
    g_wa = _matmul(ya, dbr[0], "tn", BF16, 1024, 1024, 2048, "branch_sc_dw")
    dproj, g_scw = _sc_bwd(dya, proj, sc_w_full, dproj)
    g_wb = _matmul(yb, dbr[1], "tn", BF16, 1024, 1024, 2048, "branch_ssm_dw")
    rs_a = _ReduceScatter([1, 2, 3, 4, 5], {1: g_w1, 2: g_w2, 3: g_wa, 4: g_wb, 5: g_wo}, idx, "a")
    dy, dproj, g_snw = _gnorm_bwd(dbr, wb, y, proj, ssm_norm_w, dproj, rs_a.token)
    tok = rs_a.chips(after=dy)
    dxs, dbm, dcm, ddt_g, st = _ssd_bwd(xbc, dt4, cs4, sg4, vecg, s_all, _tie(dy, tok, "tie_dy"))
    dproj, gx1 = _ssm_conv_bwd(dxs, proj, cw4, dproj, 0, "ssm_conv_bwd_x")
    dproj, gx2 = _ssm_conv_bwd(dbm, proj, cw4, dproj, INNER, "ssm_conv_bwd_b")
    dproj, gx3 = _ssm_conv_bwd(dcm, proj, cw4, dproj, INNER + NG * NS, "ssm_conv_bwd_c")
    g_cw4 = jnp.concatenate([gx1, gx2, gx3], axis=1)
    dproj, g_dtb = _dt_bwd(ddt_g, dproj)
    small = {"b_gate": g_bg[0], "ssm_conv_b": g_cw4[4], "dt_bias": g_dtb[0, :NH],
             "A_log": st[:, 0, :4], "D_skip": st[:, 1, :4], "ssm_norm_w": g_snw[0], "norm_mlp": g_nmlp[0],
             "norm_final": g_nf[0], "sc_conv_w": g_scw[0:3], "ssm_conv_w": g_cw4[0:4], "loss": loss8[0, 0:1]}
    me = 4 * xi + 2 * yi + ci
    sm8 = lax.dynamic_update_slice(jnp.zeros((8, SMALL_ROWS, 128), F32), _pack(small, _SMALL, SMALL_ROWS)[None], (me, 0, 0))
    sm_arr, sm_sems, tok = _split_call("small_start", {"sm": sm8}, start=_all8_plan("sm"))
    g_wc = _matmul(dproj, h, "tn", BF16, 1280, 1024, 2048, "in_proj_dw", dep=tok)
    rs_b = _ReduceScatter([0], {0: g_wc}, idx, "b")
    tok = rs_a.share(after=rs_b.token)
    tok = rs_b.chips(after=tok)
    grad_x, g_nm = _matmul_rms_bwd(dproj, wc, "nn", x0, norm_mix, dx1, 512, 5760, "in_proj_dx", dep=tok)
    nm8 = lax.dynamic_update_slice(jnp.zeros((8, 8, 128), F32), g_nm[0].reshape(1, 8, 128), (me, 0, 0))
    nm_arr, nm_sems, tok = _split_call("norm_mix_start", {"nm": nm8}, start=_all8_plan("nm"))
    sm_arr, _, _ = _split_call("small_wait", sm_arr, wait=_all8_plan("sm"), wait_sems=sm_sems, after=tok)
    small_sum = _sum8(sm_arr["sm"])
    gs = _unpack(small_sum, _SMALL)
    red = rs_a.result(after=tok)
    big = {"w_mlp1": red[1], "w_mlp2": red[2], "w_branch_sc": red[3], "w_branch_ssm": red[4], "w_out": red[5]}

    given = dict(norm_mix=norm_mix, w_in=w_in, b_gate=b_gate, sc_conv_w=sc_conv_w, ssm_conv_w=ssm_conv_w, ssm_conv_b=ssm_conv_b, dt_bias=dt_bias, A_log=A_log, D_skip=D_skip, ssm_norm_w=ssm_norm_w, w_branch_sc=w_branch_sc, w_branch_ssm=w_branch_ssm, w_out=w_out, norm_mlp=norm_mlp, w_mlp1=w_mlp1, w_mlp2=w_mlp2, norm_final=norm_final,
                 m_norm_mix=m_norm_mix, m_w_in=m_w_in, m_b_gate=m_b_gate, m_sc_conv_w=m_sc_conv_w, m_ssm_conv_w=m_ssm_conv_w, m_ssm_conv_b=m_ssm_conv_b, m_dt_bias=m_dt_bias, m_A_log=m_A_log, m_D_skip=m_D_skip, m_ssm_norm_w=m_ssm_norm_w, m_w_branch_sc=m_w_branch_sc, m_w_branch_ssm=m_w_branch_ssm, m_w_out=m_w_out, m_norm_mlp=m_norm_mlp, m_w_mlp1=m_w_mlp1, m_w_mlp2=m_w_mlp2, m_norm_final=m_norm_final,
                 v_norm_mix=v_norm_mix, v_w_in=v_w_in, v_b_gate=v_b_gate, v_sc_conv_w=v_sc_conv_w, v_ssm_conv_w=v_ssm_conv_w, v_ssm_conv_b=v_ssm_conv_b, v_dt_bias=v_dt_bias, v_A_log=v_A_log, v_D_skip=v_D_skip, v_ssm_norm_w=v_ssm_norm_w, v_w_branch_sc=v_w_branch_sc, v_w_branch_ssm=v_w_branch_ssm, v_w_out=v_w_out, v_norm_mlp=v_norm_mlp, v_w_mlp1=v_w_mlp1, v_w_mlp2=v_w_mlp2, v_norm_final=v_norm_final)
    order = ["norm_mix", "w_in", "b_gate", "sc_conv_w", "ssm_conv_w", "ssm_conv_b", "dt_bias", "A_log", "D_skip",
             "ssm_norm_w", "w_branch_sc", "w_branch_ssm", "w_out", "norm_mlp", "w_mlp1", "w_mlp2", "norm_final"]
    grad, delta, new_m, new_v = {}, {}, {}, {}
    for n in big:
        delta[n], new_m[n], new_v[n], grad[n] = _adamw(given[n], big[n], given["m_" + n], given["v_" + n],
                                                       "adamw_" + n, copy_g=True)
    big["w_in"] = None
    grad_small = {n: gs[n].reshape(given[n].shape) for n in small_names if n not in ("sc_conv_w", "ssm_conv_w")}
    grad_small["sc_conv_w"] = lax.dynamic_slice(gs["sc_conv_w"].reshape(3, D), (0, 256 * s), (3, 256))
    grad_small["ssm_conv_w"] = lax.dynamic_slice(gs["ssm_conv_w"].reshape(4, XBC), (0, 1024 * s), (4, 1024))
    table = small_table
    ds_, ms_, vs_ = _adamw(pk_w, _pack(grad_small, table, small_rows), pk_m, pk_v, "adamw_small", tr=small_rows)
    ds_, ms_, vs_ = _unpack(ds_, table), _unpack(ms_, table), _unpack(vs_, table)
    for n in grad_small:
        shp = given[n].shape
        grad[n] = grad_small[n]
        delta[n], new_m[n], new_v[n] = ds_[n].reshape(shp), ms_[n].reshape(shp), vs_[n].reshape(shp)

    done = [new_v[n] for n in ("w_mlp1", "w_mlp2", "w_branch_sc", "w_branch_ssm", "w_out")] + [vs_["b_gate"]]
    tok = rs_b.share(after=done)
    offs = jnp.where(s == 3, jnp.array([24, -8, 744, 2072, -8], jnp.int32),
                     jnp.stack([8 * s, 8 * s, 0 * s, 8 * s, 8 * s]).astype(jnp.int32))
    offs = jnp.concatenate([offs, jnp.stack([7 * ci, 4 - 4 * ci]).astype(jnp.int32)])
    nmain = W_SHARD // 256
    wt_own = (w_in.T, rs_b.arr["f0"], m_w_in.T, v_w_in.T, offs)
    res = _adamw_w_in(*wt_own, "adamw_w_in_own", 0, 256, 4, (0, 1), blk_key=5)
    gp = rs_b.result(after=[tok, res[0]])[0]
    wt_args = (w_in.T, gp, m_w_in.T, v_w_in.T, offs)
    res = _adamw_w_in(*wt_args, "adamw_w_in", 0, 256, nmain - 4, (0, 1), into=res, blk_key=6)
    res = _adamw_w_in(*wt_args, "adamw_w_in_dt", 744, 32, 1, (3,), into=res)
    dt_, mt_, vt_, gwt = _adamw_w_in(*wt_args, "adamw_w_in_tail", 256 * nmain, 8, 1, (4,), into=res)
    grad["w_in"], delta["w_in"], new_m["w_in"], new_v["w_in"] = gwt.T, dt_.T, mt_.T, vt_.T
    nm_arr, _, _ = _split_call("norm_mix_wait", nm_arr, wait=_all8_plan("nm"), wait_sems=nm_sems, after=tok)
    g8 = _sum8(nm_arr["nm"], "norm_mix_sum")
    r8 = lambda a: a.reshape(8, 128)
    d8, m8, v8 = _adamw(r8(norm_mix), g8, r8(m_norm_mix), r8(v_norm_mix), "adamw_norm_mix", tr=8)
    grad["norm_mix"], delta["norm_mix"] = g8.reshape(D), d8.reshape(D)
    new_m["norm_mix"], new_v["norm_mix"] = m8.reshape(D), v8.reshape(D)

    loss = gs["loss"].reshape(())
    return (loss, grad_x.reshape(1, L, D), *[grad[n] for n in order], *[delta[n] for n in order],
            *[new_m[n] for n in order], *[new_v[n] for n in order])
```

```python
import functools

import jax
import jax.numpy as jnp
from jax import lax
from jax.experimental import pallas as pl
from jax.experimental.pallas import tpu as pltpu

F32 = jnp.float32
BF16 = jnp.bfloat16
MESH = pl.DeviceIdType.MESH
HBM = pltpu.HBM

D = 1024
INNER = 2048
HD = 64
NH = 32
NG = 8
NS = 128
Q = 128
GPS = 8
XBC = 4096
DFF = 4096
EPS = 1e-6
W_SHARD = 2824
NCW = 11520
PIECE = 3072
PMAIN = 2816
C_Z, C_XBC, C_GATE, C_DT = 3072, 5120, 9216, 11264
SMALL_ROWS = 256
VMEM_LIMIT = 56 * 1024 * 1024

ADAM_LR, ADAM_B1, ADAM_B2, ADAM_EPS, ADAM_WD, ADAM_STEP = 0.001, 0.9, 0.999, 1e-08, 0.01, 10


def _cp(sem=None, vmem=VMEM_LIMIT):
    return pltpu.CompilerParams(dimension_semantics=sem, vmem_limit_bytes=vmem)


def _sigmoid(v):
    return 1.0 / (1.0 + jnp.exp(-v))


_DIMS = {"nn": (((1,), (0,)), ((), ())), "nt": (((1,), (1,)), ((), ())), "tn": (((0,), (0,)), ((), ()))}


def _matmul(a, b, mode, out_dtype, tm, tn, tk, name, epi=None, extra=None, n_outer=False, dep=None):
    if mode == "tn":
        K, M = a.shape
    else:
        M, K = a.shape
    N = b.shape[0] if mode == "nt" else b.shape[1]
    tm, tn, tk = min(tm, M), min(tn, N), min(tk, K)
    assert M % tm == 0 and N % tn == 0 and K % tk == 0, (name, M, N, K, tm, tn, tk)
    nm, nn, nk = M // tm, N // tn, K // tk
    dims = _DIMS[mode]

    def ij(p0, p1):
        return (p1, p0) if n_outer else (p0, p1)

    if mode == "tn":
        a_spec = pl.BlockSpec((tk, tm), lambda p0, p1, k: (k, ij(p0, p1)[0]))
    else:
        a_spec = pl.BlockSpec((tm, tk), lambda p0, p1, k: (ij(p0, p1)[0], k))
    if mode == "nt":
        b_spec = pl.BlockSpec((tn, tk), lambda p0, p1, k: (ij(p0, p1)[1], k))
    else:
        b_spec = pl.BlockSpec((tk, tn), lambda p0, p1, k: (k, ij(p0, p1)[1]))
    o_spec = pl.BlockSpec((tm, tn), lambda p0, p1, k: ij(p0, p1))
    in_specs = [a_spec, b_spec]
    args = [a, b]
    if epi in ("res", "drelu"):
        in_specs.append(o_spec)
        args.append(extra)
    if dep is not None:
        in_specs.append(pl.BlockSpec(memory_space=pl.ANY))
        args.append(dep)
    n_in = len(args)
    if epi == "relu2":
        out_shape = (jax.ShapeDtypeStruct((M, N), out_dtype), jax.ShapeDtypeStruct((M, N), BF16))
        out_specs = (o_spec, o_spec)
    else:
        out_shape = jax.ShapeDtypeStruct((M, N), out_dtype)
        out_specs = o_spec

    def kern(*refs):
        a_ref, b_ref = refs[0], refs[1]
        e_ref = refs[2] if epi in ("res", "drelu") else None
        acc = refs[-1]
        outs = refs[n_in:-1] if nk > 1 else refs[n_in:]
        k = pl.program_id(2)

        def product():
            return lax.dot_general(a_ref[...].astype(BF16), b_ref[...].astype(BF16), dims, preferred_element_type=F32)

        def finish(r):
            if epi is None:
                outs[0][...] = r.astype(out_dtype)
            elif epi == "res":
                outs[0][...] = (r + e_ref[...]).astype(out_dtype)
            elif epi == "relu2":
                outs[0][...] = r.astype(out_dtype)
                t = jnp.maximum(r, 0.0)
                outs[1][...] = (t * t).astype(BF16)
            else:
                outs[0][...] = (r * (2.0 * jnp.maximum(e_ref[...].astype(F32), 0.0))).astype(out_dtype)

        if nk == 1:
            finish(product())
        else:
            @pl.when(k == 0)
            def _():
                acc[...] = jnp.zeros_like(acc)

            acc[...] += product()

            @pl.when(k == nk - 1)
            def _():
                finish(acc[...])

    grid = (nn, nm, nk) if n_outer else (nm, nn, nk)
    return pl.pallas_call(
        kern, grid=grid, in_specs=in_specs, out_specs=out_specs, out_shape=out_shape,
        scratch_shapes=[pltpu.VMEM((tm, tn), F32)] if nk > 1 else [], name=name,
        compiler_params=_cp(("parallel", "parallel", "arbitrary")),
    )(*args)


def _matmul_res_rms(a, b, x, w, tm, name):
    M, K = a.shape
    tm = min(tm, M)

    def kern(a_ref, b_ref, x_ref, w_ref, x1_ref, h_ref):
        x1 = x_ref[...] + lax.dot_general(a_ref[...].astype(BF16), b_ref[...].astype(BF16), _DIMS["nn"],
                                          preferred_element_type=F32)
        x1_ref[...] = x1
        r = lax.rsqrt(jnp.mean(x1 * x1, axis=-1, keepdims=True) + EPS)
        h_ref[...] = ((x1 * r) * w_ref[...]).astype(BF16)

    row = pl.BlockSpec((tm, D), lambda i: (i, 0))
    return pl.pallas_call(
        kern, grid=(M // tm,),
        in_specs=[pl.BlockSpec((tm, K), lambda i: (i, 0)), pl.BlockSpec((K, D), lambda i: (0, 0)), row,
                  pl.BlockSpec((1, D), lambda i: (0, 0))],
        out_specs=(row, row), out_shape=(jax.ShapeDtypeStruct((M, D), F32), jax.ShapeDtypeStruct((M, D), BF16)),
        name=name, compiler_params=_cp(("parallel",)),
    )(a, b, x, w.reshape(1, D))


def _matmul_res_final(a, b, x, w, tgt, tm, name):
    M, K = a.shape
    tm = min(tm, M)

    def kern(a_ref, b_ref, x_ref, w_ref, t_ref, dx_ref, gw_ref, loss_ref):
        @pl.when(pl.program_id(0) == 0)
        def _():
            gw_ref[...] = jnp.zeros_like(gw_ref)
            loss_ref[...] = jnp.zeros_like(loss_ref)

        xv = x_ref[...] + lax.dot_general(a_ref[...].astype(BF16), b_ref[...].astype(BF16), _DIMS["nn"],
                                          preferred_element_type=F32)
        r = lax.rsqrt(jnp.mean(xv * xv, axis=-1, keepdims=True) + EPS)
        xn = xv * r
        e = xn * w_ref[...] - t_ref[...]
        loss_ref[...] += 0.5 * jnp.sum(jnp.mean(e * e, axis=-1, keepdims=True))
        dyv = e * (1.0 / D)
        gw_ref[...] += jnp.broadcast_to(jnp.sum(dyv * xn, axis=0, keepdims=True), (8, D))
        dxn = dyv * w_ref[...]
        dx_ref[...] = r * (dxn - xn * jnp.mean(dxn * xn, axis=-1, keepdims=True))

    row = pl.BlockSpec((tm, D), lambda i: (i, 0))
    return pl.pallas_call(
        kern, grid=(M // tm,),
        in_specs=[pl.BlockSpec((tm, K), lambda i: (i, 0)), pl.BlockSpec((K, D), lambda i: (0, 0)), row,
                  pl.BlockSpec((1, D), lambda i: (0, 0)), row],
        out_specs=(row, pl.BlockSpec((8, D), lambda i: (0, 0)), pl.BlockSpec((8, 128), lambda i: (0, 0))),
        out_shape=(jax.ShapeDtypeStruct((M, D), F32), jax.ShapeDtypeStruct((8, D), F32),
                   jax.ShapeDtypeStruct((8, 128), F32)),
        name=name, compiler_params=_cp(("arbitrary",)),
    )(a, b, x, w.reshape(1, D), tgt)


def _matmul_rms_bwd(a, b, mode, x, w, res, tm, tk, name, dep=None):
    M, K = a.shape
    tm, tk = min(tm, M), min(tk, K)
    nk = K // tk
    assert M % tm == 0 and K % tk == 0
    b_spec = (pl.BlockSpec((tk, D), lambda i, k: (k, 0)) if mode == "nn" else pl.BlockSpec((D, tk), lambda i, k: (0, k)))
    row = pl.BlockSpec((tm, D), lambda i, k: (i, 0))
    deps = [] if dep is None else [dep]

    def kern(a_ref, b_ref, x_ref, w_ref, res_ref, *rest):
        dx_ref, gw_ref, acc = rest[-3:]
        i, k = pl.program_id(0), pl.program_id(1)

        @pl.when((i == 0) & (k == 0))
        def _():
            gw_ref[...] = jnp.zeros_like(gw_ref)

        def product():
            return lax.dot_general(a_ref[...].astype(BF16), b_ref[...].astype(BF16), _DIMS[mode],
                                   preferred_element_type=F32)

        def finish(dyv):
            xv = x_ref[...]
            r = lax.rsqrt(jnp.mean(xv * xv, axis=-1, keepdims=True) + EPS)
            xn = xv * r
            gw_ref[...] += jnp.broadcast_to(jnp.sum(dyv * xn, axis=0, keepdims=True), (8, D))
            dxn = dyv * w_ref[...]
            dx_ref[...] = res_ref[...] + r * (dxn - xn * jnp.mean(dxn * xn, axis=-1, keepdims=True))

        if nk == 1:
            finish(product())
        else:
            @pl.when(k == 0)
            def _():
                acc[...] = jnp.zeros_like(acc)

            acc[...] += product()

            @pl.when(k == nk - 1)
            def _():
                finish(acc[...])

    return pl.pallas_call(
        kern, grid=(M // tm, nk),
        in_specs=[pl.BlockSpec((tm, tk), lambda i, k: (i, k)), b_spec, row, pl.BlockSpec((1, D), lambda i, k: (0, 0)),
                  row] + [pl.BlockSpec(memory_space=pl.ANY)] * len(deps),
        out_specs=(row, pl.BlockSpec((8, D), lambda i, k: (0, 0))),
        out_shape=(jax.ShapeDtypeStruct((M, D), F32), jax.ShapeDtypeStruct((8, D), F32)),
        scratch_shapes=[pltpu.VMEM((tm, D), F32)], name=name, compiler_params=_cp(("arbitrary", "arbitrary")),
    )(a, b, x, w.reshape(1, D), res, *deps)


def _rms_fwd(x, w, name, tl=256, dep=None):
    L = x.shape[0]

    def kern(x_ref, w_ref, *rest):
        o_ref = rest[-1]
        xv = x_ref[...]
        r = lax.rsqrt(jnp.mean(xv * xv, axis=-1, keepdims=True) + EPS)
        o_ref[...] = ((xv * r) * w_ref[...]).astype(BF16)

    row = pl.BlockSpec((tl, D), lambda i: (i, 0))
    deps = [] if dep is None else [dep]
    return pl.pallas_call(
        kern, grid=(L // tl,),
        in_specs=[row, pl.BlockSpec((1, D), lambda i: (0, 0))] + [pl.BlockSpec(memory_space=pl.ANY)] * len(deps),
        out_specs=row, out_shape=jax.ShapeDtypeStruct((L, D), BF16), name=name, compiler_params=_cp(("parallel",)),
    )(x, w.reshape(1, D), *deps)


def _down(v, k):
    if k == 0:
        return v
    t = lax.broadcasted_iota(jnp.int32, v.shape, 0)
    return jnp.where(t >= k, pltpu.roll(v, k, axis=0), 0.0)


def _up(v, k):
    if k == 0:
        return v
    n = v.shape[0]
    t = lax.broadcasted_iota(jnp.int32, v.shape, 0)
    return jnp.where(t < n - k, pltpu.roll(v, n - k, axis=0), 0.0)


TW = 256


def _sc_fwd(proj, cw):
    L = proj.shape[0]
    nb = D // TW

    def kern(b_ref, c_ref, x_ref, w_ref, o_ref):
        u = c_ref[...].astype(F32) * x_ref[...].astype(F32)
        w = w_ref[...]
        cv = w[0:1] * _down(u, 2) + w[1:2] * _down(u, 1) + w[2:3] * u
        o_ref[...] = (b_ref[...].astype(F32) * cv).astype(BF16)

    col = lambda off: pl.BlockSpec((L, TW), lambda j: (0, off + j))
    return pl.pallas_call(
        kern, grid=(nb,), in_specs=[col(0), col(nb), col(2 * nb), pl.BlockSpec((8, TW), lambda j: (0, j))],
        out_specs=pl.BlockSpec((L, TW), lambda j: (0, j)), out_shape=jax.ShapeDtypeStruct((L, D), BF16),
        name="sc_fwd", compiler_params=_cp(("parallel",)),
    )(proj, proj, proj, cw)


def _sc_bwd(dya, proj, cw, dproj):
    L = proj.shape[0]
    nb = D // TW

    def kern(d_ref, b_ref, c_ref, x_ref, w_ref, _, dp_ref, gw_ref, keep):
        sec = pl.program_id(1)

        @pl.when(sec == 0)
        def _():
            cs, xs, dyv = c_ref[...].astype(F32), x_ref[...].astype(F32), d_ref[...]
            w = w_ref[...]
            u = cs * xs
            u1, u2 = _down(u, 1), _down(u, 2)
            cv = w[0:1] * u2 + w[1:2] * u1 + w[2:3] * u
            dcv = dyv * b_ref[...].astype(F32)
            du = w[2:3] * dcv + w[1:2] * _up(dcv, 1) + w[0:1] * _up(dcv, 2)
            g0 = jnp.sum(dcv * u2, axis=0, keepdims=True)
            g1 = jnp.sum(dcv * u1, axis=0, keepdims=True)
            g2 = jnp.sum(dcv * u, axis=0, keepdims=True)
            row = lax.broadcasted_iota(jnp.int32, (8, TW), 0)
            gw_ref[...] = jnp.where(row == 0, g0, jnp.where(row == 1, g1, jnp.where(row == 2, g2, 0.0)))
            dp_ref[...] = (dyv * cv).astype(BF16)
            keep[0] = (du * xs).astype(BF16)
            keep[1] = (du * cs).astype(BF16)

        @pl.when(sec > 0)
        def _():
            dp_ref[...] = keep[sec - 1]

    col = lambda off: pl.BlockSpec((L, TW), lambda j, s: (0, off + j))
    return pl.pallas_call(
        kern, grid=(nb, 3),
        in_specs=[col(0), col(0), col(nb), col(2 * nb), pl.BlockSpec((8, TW), lambda j, s: (0, j)),
                  pl.BlockSpec(memory_space=pl.ANY)],
        out_specs=(pl.BlockSpec((L, TW), lambda j, s: (0, s * nb + j)), pl.BlockSpec((8, TW), lambda j, s: (0, j))),
        out_shape=(jax.ShapeDtypeStruct(dproj.shape, BF16), jax.ShapeDtypeStruct((8, D), F32)),
        scratch_shapes=[pltpu.VMEM((2, L, TW), BF16)],
        input_output_aliases={5: 0}, name="sc_bwd", compiler_params=_cp(("parallel", "arbitrary")),
    )(dya, proj, proj, proj, cw, dproj)


def _ssm_conv_fwd(proj, cw4):
    L = proj.shape[0]
    off = C_XBC // TW

    def kern(r_ref, w_ref, o_ref):
        raw = r_ref[...].astype(F32)
        w = w_ref[...]
        c4 = w[0:1] * _down(raw, 3) + w[1:2] * _down(raw, 2) + w[2:3] * _down(raw, 1) + w[3:4] * raw + w[4:5]
        o_ref[...] = c4 * _sigmoid(c4)

    return pl.pallas_call(
        kern, grid=(XBC // TW,),
        in_specs=[pl.BlockSpec((L, TW), lambda j: (0, off + j)), pl.BlockSpec((8, TW), lambda j: (0, j))],
        out_specs=pl.BlockSpec((L, TW), lambda j: (0, j)), out_shape=jax.ShapeDtypeStruct((L, XBC), F32),
        name="ssm_conv_fwd", compiler_params=_cp(("parallel",)),
    )(proj, cw4)


def _ssm_conv_bwd(dx, proj, cw4, dproj, col0, name):
    L, width = dx.shape
    off_p = (C_XBC + col0) // TW
    off_w = col0 // TW

    def kern(d_ref, r_ref, w_ref, _, dp_ref, gw_ref):
        raw = r_ref[...].astype(F32)
        w = w_ref[...]
        r1, r2, r3 = _down(raw, 1), _down(raw, 2), _down(raw, 3)
        c4 = w[0:1] * r3 + w[1:2] * r2 + w[2:3] * r1 + w[3:4] * raw + w[4:5]
        sg = _sigmoid(c4)
        dc4 = d_ref[...] * (sg * (1.0 + c4 * (1.0 - sg)))
        draw = w[3:4] * dc4 + w[2:3] * _up(dc4, 1) + w[1:2] * _up(dc4, 2) + w[0:1] * _up(dc4, 3)
        dp_ref[...] = draw.astype(BF16)
        gs = [jnp.sum(dc4 * r3, axis=0, keepdims=True), jnp.sum(dc4 * r2, axis=0, keepdims=True),
              jnp.sum(dc4 * r1, axis=0, keepdims=True), jnp.sum(dc4 * raw, axis=0, keepdims=True),
              jnp.sum(dc4, axis=0, keepdims=True)]
        row = lax.broadcasted_iota(jnp.int32, (8, TW), 0)
        acc = jnp.zeros((8, TW), F32)
        for k, gk in enumerate(gs):
            acc = jnp.where(row == k, gk, acc)
        gw_ref[...] = acc

    return pl.pallas_call(
        kern, grid=(width // TW,),
        in_specs=[pl.BlockSpec((L, TW), lambda j: (0, j)), pl.BlockSpec((L, TW), lambda j: (0, off_p + j)),
                  pl.BlockSpec((8, TW), lambda j: (0, off_w + j)), pl.BlockSpec(memory_space=pl.ANY)],
        out_specs=(pl.BlockSpec((L, TW), lambda j: (0, off_p + j)), pl.BlockSpec((8, TW), lambda j: (0, j))),
        out_shape=(jax.ShapeDtypeStruct(dproj.shape, BF16), jax.ShapeDtypeStruct((8, width), F32)),
        input_output_aliases={3: 0}, name=name, compiler_params=_cp(("arbitrary",)),
    )(dx, proj, cw4, dproj)


def _split3(v):
    h1 = v.astype(BF16)
    r1 = v - h1.astype(F32)
    h2 = r1.astype(BF16)
    h3 = (r1 - h2.astype(F32)).astype(BF16)
    return h1, h2, h3


def _dot01(m01, v, dims=_DIMS["nn"], m_left=True, terms=3):
    out = None
    for part in _split3(v)[:terms]:
        ops = (m01, part) if m_left else (part, m01)
        t = lax.dot_general(ops[0], ops[1], dims, preferred_element_type=F32)
        out = t if out is None else out + t
    return out


def _bdot(a, b, mode="nn"):
    return lax.dot_general(a.astype(BF16), b.astype(BF16), _DIMS[mode], preferred_element_type=F32)


def _softplus(v):
    return jnp.maximum(v, 0.0) + jnp.log1p(jnp.exp(-jnp.abs(v)))


def _dt_prep(proj, vec):
    L = proj.shape[0]

    def kern(p_ref, v_ref, dt_ref, cs_ref, sg_ref):
        v = v_ref[...]
        pre = p_ref[:, 0:128] + v[0:1]
        dt = _softplus(pre)
        da = dt * (-jnp.exp(v[1:2]))
        ii = lax.broadcasted_iota(jnp.int32, (Q, Q), 0)
        jj = lax.broadcasted_iota(jnp.int32, (Q, Q), 1)
        ltri = (jj <= ii).astype(BF16)
        lane = lax.broadcasted_iota(jnp.int32, (Q, 128), 1)
        for val, ref in ((dt, dt_ref), (_dot01(ltri, da), cs_ref), (_sigmoid(pre), sg_ref)):
            for g in range(NG):
                moved = val if g == 0 else pltpu.roll(val, 128 - 4 * g, axis=1)
                ref[g] = jnp.where(lane < 4, moved, 0.0)

    blk = pl.BlockSpec((NG, Q, 128), lambda c: (0, c, 0))
    return pl.pallas_call(
        kern, grid=(L // Q,),
        in_specs=[pl.BlockSpec((Q, 256), lambda c: (c, 0)), pl.BlockSpec((8, 128), lambda c: (0, 0))],
        out_specs=(blk, blk, blk),
        out_shape=(jax.ShapeDtypeStruct((NG, L, 128), F32),) * 3,
        name="dt_prep", compiler_params=_cp(("parallel",)),
    )(proj, vec)


def _head_masks():
    lane = lax.broadcasted_iota(jnp.int32, (1, 4 * HD), 1)
    return [((lane >= HD * j) & (lane < HD * (j + 1))) for j in range(4)]


def _expand4(v4, masks):
    R = v4.shape[0]
    out = jnp.zeros((R, 4 * HD), F32)
    for j in range(4):
        out = jnp.where(masks[j], jnp.broadcast_to(v4[:, j:j + 1], (R, 4 * HD)), out)
    return out


def _decay_matrix(cs_col, tri):
    colb = jnp.broadcast_to(cs_col, (Q, Q))
    return jnp.exp(jnp.where(tri, colb - colb.T, -jnp.inf))


def _ssd_fwd(xbc, dt4, cs4, vecg):
    L = xbc.shape[0]
    nc = L // Q

    def kern(x_ref, b_ref, c_ref, dt_ref, cs_ref, v_ref, y_ref, s_ref, S):
        c = pl.program_id(1)

        @pl.when(c == 0)
        def _():
            S[...] = jnp.zeros_like(S)

        masks = _head_masks()
        ii = lax.broadcasted_iota(jnp.int32, (Q, Q), 0)
        jj = lax.broadcasted_iota(jnp.int32, (Q, Q), 1)
        tri = jj <= ii
        for gi in range(GPS):
            xs, ns = slice(256 * gi, 256 * (gi + 1)), slice(NS * gi, NS * (gi + 1))
            dt4v, cs4v = dt_ref[gi], cs_ref[gi]
            dt_b, cs_b = _expand4(dt4v, masks), _expand4(cs4v, masks)
            d_b = _expand4(v_ref[gi], masks)[1:2]
            cs_last = cs_b[Q - 1:Q, :]
            x4, bm, cm = x_ref[:, xs], b_ref[:, ns], c_ref[:, ns]
            xdt = x4 * dt_b
            gm = _bdot(cm, bm, "nt")
            s4 = S[gi]
            s_ref[gi, 0] = s4
            y = _bdot(cm, s4) * jnp.exp(cs_b) + d_b * x4
            m_all = jnp.concatenate([(gm * _decay_matrix(cs4v[:, j:j + 1], tri)).astype(BF16) for j in range(4)], axis=0)
            yd = _bdot(m_all, xdt)
            for j in range(4):
                y = y + jnp.where(masks[j], yd[Q * j:Q * (j + 1)], 0.0)
            y_ref[:, xs] = y
            S[gi] = jnp.exp(cs_last) * s4 + _bdot(bm, xdt * jnp.exp(cs_last - cs_b), "tn")

    sc = pl.BlockSpec((GPS, Q, 128), lambda g, c: (g, c, 0))
    bw = NS * GPS
    return pl.pallas_call(
        kern, grid=(NG // GPS, nc),
        in_specs=[pl.BlockSpec((Q, 256 * GPS), lambda g, c: (c, g)),
                  pl.BlockSpec((Q, bw), lambda g, c: (c, INNER // bw + g)),
                  pl.BlockSpec((Q, bw), lambda g, c: (c, (INNER + NG * NS) // bw + g)),
                  sc, sc, pl.BlockSpec((GPS, 8, 128), lambda g, c: (g, 0, 0))],
        out_specs=(pl.BlockSpec((Q, 256 * GPS), lambda g, c: (c, g)),
                   pl.BlockSpec((GPS, 1, NS, 256), lambda g, c: (g, c, 0, 0))),
        out_shape=(jax.ShapeDtypeStruct((L, INNER), F32), jax.ShapeDtypeStruct((NG, nc, NS, 256), F32)),
        scratch_shapes=[pltpu.VMEM((GPS, NS, 256), F32)], name="ssd_fwd",
        compiler_params=_cp(("parallel", "arbitrary")),
    )(xbc, xbc, xbc, dt4, cs4, vecg)


def _ssd_bwd(xbc, dt4, cs4, sg4, vecg, s_all, dy):
    L = xbc.shape[0]
    nc = L // Q

    def kern(x_ref, b_ref, c_ref, dt_ref, cs_ref, sg_ref, v_ref, s_ref, dy_ref,
             dx_ref, db_ref, dc_ref, ddt_ref, st_ref, dS):
        cc = pl.program_id(1)

        @pl.when(cc == 0)
        def _():
            dS[...] = jnp.zeros_like(dS)
            st_ref[...] = jnp.zeros_like(st_ref)

        masks = _head_masks()
        ii = lax.broadcasted_iota(jnp.int32, (Q, Q), 0)
        jj = lax.broadcasted_iota(jnp.int32, (Q, Q), 1)
        tri = jj <= ii
        utri = (jj >= ii).astype(BF16)
        hsel = ((lax.broadcasted_iota(jnp.int32, (4 * HD, 128), 0) // HD)
                == lax.broadcasted_iota(jnp.int32, (4 * HD, 128), 1)).astype(BF16)
        hrow = ((lax.broadcasted_iota(jnp.int32, (4 * Q, 128), 0) // Q)
                == lax.broadcasted_iota(jnp.int32, (4 * Q, 128), 1)).astype(BF16)
        ones_q = jnp.ones((Q, 128), BF16)
        lane128 = lax.broadcasted_iota(jnp.int32, (Q, 128), 1)

        for gi in range(GPS):
            xs, ns = slice(256 * gi, 256 * (gi + 1)), slice(NS * gi, NS * (gi + 1))
            dt4v, cs4v, sg4v = dt_ref[gi], cs_ref[gi], sg_ref[gi]
            dt_b, cs_b = _expand4(dt4v, masks), _expand4(cs4v, masks)
            vv = _expand4(v_ref[gi], masks)
            a_b = -jnp.exp(vv[0:1])
            d_b = vv[1:2]
            a4 = -jnp.exp(v_ref[gi][0:1, :])
            cs_last = cs_b[Q - 1:Q, :]
            ecs = jnp.exp(cs_b)
            decay = jnp.exp(cs_last - cs_b)
            elast = jnp.exp(cs_last)
            x4, bm, cm, dyv = x_ref[:, xs], b_ref[:, ns], c_ref[:, ns], dy_ref[:, xs]
            s4 = s_ref[gi, 0]
            dsn = dS[gi]
            xdt = x4 * dt_b
            gm = _bdot(cm, bm, "nt")
            dye = dyv * ecs
            yoff = ecs * _bdot(cm, s4)
            t4 = _bdot(bm, dsn) * decay
            lms, mhs = [], []
            for j in range(4):
                colb = jnp.broadcast_to(cs4v[:, j:j + 1], (Q, Q))
                lms.append(jnp.exp(jnp.where(tri, colb - colb.T, -jnp.inf)))
                mhs.append(gm * lms[j])
            m_all = jnp.concatenate([m.astype(BF16) for m in mhs], axis=0)
            dy_m = jnp.concatenate([jnp.where(masks[j], dyv, 0.0).astype(BF16) for j in range(4)], axis=0)
            dxdt = t4 + _bdot(m_all, dy_m, "tn")
            dm_all = _bdot(dy_m, xdt, "nt")
            dg = jnp.zeros((Q, Q), F32)
            for j in range(4):
                dg = dg + dm_all[Q * j:Q * (j + 1)] * lms[j]
            e_all = dm_all * jnp.concatenate(mhs, axis=0)
            rsum = _dot01(ones_q, e_all, m_left=False, terms=2)
            da4 = -_dot01(hrow, e_all, _DIMS["tn"], m_left=False, terms=2)
            for j in range(4):
                da4 = da4 + jnp.where(lane128 == j, rsum[Q * j:Q * (j + 1)], 0.0)
            xt = xdt * t4
            tail = jnp.sum(xt, axis=0, keepdims=True) + elast * jnp.sum(s4 * dsn, axis=0, keepdims=True)
            gd_raw = jnp.sum(dyv * x4, axis=0, keepdims=True)
            stacked = jnp.concatenate([dyv * yoff - xt, dxdt * x4, jnp.broadcast_to(tail, (8, 4 * HD)),
                                       jnp.broadcast_to(gd_raw, (8, 4 * HD))], axis=0)
            seg = _dot01(hsel, stacked, m_left=False, terms=2)
            dda4 = _dot01(utri, da4 + seg[0:Q], terms=2) + seg[2 * Q:2 * Q + 1]
            ddt_ref[gi] = (dda4 * a4 + seg[Q:2 * Q]) * sg4v
            ga = jnp.sum(dda4 * dt4v * a4, axis=0, keepdims=True)
            row = lax.broadcasted_iota(jnp.int32, (8, 128), 0)
            st_ref[gi] += jnp.where(row == 0, ga, jnp.where(row == 1, seg[2 * Q + 8:2 * Q + 9], 0.0))
            dx_ref[:, xs] = d_b * dyv + dxdt * dt_b
            dc_ref[:, ns] = _bdot(dg, bm) + _bdot(dye, s4, "nt")
            db_ref[:, ns] = _bdot(dg, cm, "tn") + _bdot(xdt * decay, dsn, "nt")
            dS[gi] = elast * dsn + _bdot(cm, dye, "tn")

    rv = lambda c: nc - 1 - c
    sc = pl.BlockSpec((GPS, Q, 128), lambda g, c: (g, rv(c), 0))
    bw = NS * GPS
    return pl.pallas_call(
        kern, grid=(NG // GPS, nc),
        in_specs=[pl.BlockSpec((Q, 256 * GPS), lambda g, c: (rv(c), g)),
                  pl.BlockSpec((Q, bw), lambda g, c: (rv(c), INNER // bw + g)),
                  pl.BlockSpec((Q, bw), lambda g, c: (rv(c), (INNER + NG * NS) // bw + g)),
                  sc, sc, sc, pl.BlockSpec((GPS, 8, 128), lambda g, c: (g, 0, 0)),
                  pl.BlockSpec((GPS, 1, NS, 256), lambda g, c: (g, rv(c), 0, 0)),
                  pl.BlockSpec((Q, 256 * GPS), lambda g, c: (rv(c), g))],
        out_specs=(pl.BlockSpec((Q, 256 * GPS), lambda g, c: (rv(c), g)),
                   pl.BlockSpec((Q, bw), lambda g, c: (rv(c), g)),
                   pl.BlockSpec((Q, bw), lambda g, c: (rv(c), g)),
                   pl.BlockSpec((GPS, Q, 128), lambda g, c: (g, rv(c), 0)),
                   pl.BlockSpec((GPS, 8, 128), lambda g, c: (g, 0, 0))),
        out_shape=(jax.ShapeDtypeStruct((L, INNER), F32), jax.ShapeDtypeStruct((L, NG * NS), F32),
                   jax.ShapeDtypeStruct((L, NG * NS), F32), jax.ShapeDtypeStruct((NG, L, 128), F32),
                   jax.ShapeDtypeStruct((NG, 8, 128), F32)),
        scratch_shapes=[pltpu.VMEM((GPS, NS, 256), F32)], name="ssd_bwd",
        compiler_params=_cp(("parallel", "arbitrary")),
    )(xbc, xbc, xbc, dt4, cs4, sg4, vecg, s_all, dy)


def _dt_bwd(ddt, dproj, tl=256):
    L = ddt.shape[1]

    def kern(d_ref, _, dp_ref, gs_ref):
        @pl.when(pl.program_id(0) == 0)
        def _():
            gs_ref[...] = jnp.zeros_like(gs_ref)

        d = d_ref[0]
        for g in range(1, NG):
            d = d + pltpu.roll(d_ref[g], 4 * g, axis=1)
        gs_ref[...] += jnp.broadcast_to(jnp.sum(d, axis=0, keepdims=True), (8, 128))
        dp_ref[...] = jnp.concatenate([d, jnp.zeros_like(d)], axis=1).astype(BF16)

    return pl.pallas_call(
        kern, grid=(L // tl,),
        in_specs=[pl.BlockSpec((NG, tl, 128), lambda i: (0, i, 0)), pl.BlockSpec(memory_space=pl.ANY)],
        out_specs=(pl.BlockSpec((tl, 256), lambda i: (i, C_DT // 256)), pl.BlockSpec((8, 128), lambda i: (0, 0))),
        out_shape=(jax.ShapeDtypeStruct(dproj.shape, BF16), jax.ShapeDtypeStruct((8, 128), F32)),
        input_output_aliases={1: 0}, name="dt_bwd", compiler_params=_cp(("arbitrary",)),
    )(ddt, dproj)


GW = INNER // NG


def _gnorm_fwd(y, proj, w, tl=256):
    L = y.shape[0]
    zoff = C_Z // 1024

    def kern(y_ref, z_ref, w_ref, o_ref):
        z = z_ref[...].astype(F32)
        yz = y_ref[...] * (z * _sigmoid(z))
        wv = w_ref[...]
        for k in range(1024 // GW):
            sl = slice(GW * k, GW * (k + 1))
            v = yz[:, sl]
            rg = lax.rsqrt(jnp.mean(v * v, axis=-1, keepdims=True) + EPS)
            o_ref[:, sl] = ((v * rg) * wv[:, sl]).astype(BF16)

    blk = pl.BlockSpec((tl, 1024), lambda i, j: (i, j))
    return pl.pallas_call(
        kern, grid=(L // tl, 2),
        in_specs=[blk, pl.BlockSpec((tl, 1024), lambda i, j: (i, zoff + j)), pl.BlockSpec((1, 1024), lambda i, j: (0, j))],
        out_specs=blk, out_shape=jax.ShapeDtypeStruct((L, INNER), BF16), name="gnorm_fwd",
        compiler_params=_cp(("parallel", "parallel")),
    )(y, proj, w.reshape(1, INNER))


def _gnorm_bwd(dbr, wb, y, proj, w, dproj, dep, tl=512):
    L = y.shape[0]
    tl = min(tl, L)
    zoff = C_Z // 1024

    def kern(d_ref, b_ref, y_ref, z_ref, w_ref, _, __, dy_ref, dp_ref, gw_ref):
        @pl.when(pl.program_id(1) == 0)
        def _():
            gw_ref[...] = jnp.zeros_like(gw_ref)

        z = z_ref[...].astype(F32)
        sg = _sigmoid(z)
        sz = z * sg
        yv = y_ref[...]
        yz = yv * sz
        dv = lax.dot_general(d_ref[0], b_ref[...], _DIMS["nt"], preferred_element_type=F32)
        wv = w_ref[...]
        for k in range(1024 // GW):
            sl = slice(GW * k, GW * (k + 1))
            v = yz[:, sl]
            rg = lax.rsqrt(jnp.mean(v * v, axis=-1, keepdims=True) + EPS)
            vn = v * rg
            dk = dv[:, sl]
            gw_ref[:, sl] += jnp.broadcast_to(jnp.sum(dk * vn, axis=0, keepdims=True), (8, GW))
            dvn = dk * wv[:, sl]
            dyz = rg * (dvn - vn * jnp.mean(dvn * vn, axis=-1, keepdims=True))
            dy_ref[:, sl] = dyz * sz[:, sl]
            dp_ref[:, sl] = (dyz * yv[:, sl] * (sg[:, sl] * (1.0 + z[:, sl] * (1.0 - sg[:, sl])))).astype(BF16)

    blk = pl.BlockSpec((tl, 1024), lambda j, i: (i, j))
    zblk = pl.BlockSpec((tl, 1024), lambda j, i: (i, zoff + j))
    return pl.pallas_call(
        kern, grid=(2, L // tl),
        in_specs=[pl.BlockSpec((1, tl, D), lambda j, i: (1, i, 0)), pl.BlockSpec((1024, D), lambda j, i: (j, 0)),
                  blk, zblk, pl.BlockSpec((1, 1024), lambda j, i: (0, j)), pl.BlockSpec(memory_space=pl.ANY),
                  pl.BlockSpec(memory_space=pl.ANY)],
        out_specs=(blk, zblk, pl.BlockSpec((8, 1024), lambda j, i: (0, j))),
        out_shape=(jax.ShapeDtypeStruct((L, INNER), F32), jax.ShapeDtypeStruct(dproj.shape, BF16),
                   jax.ShapeDtypeStruct((8, INNER), F32)),
        input_output_aliases={5: 1}, name="gnorm_bwd", compiler_params=_cp(("parallel", "arbitrary")),
    )(dbr, wb, y, proj, w.reshape(1, INNER), dproj, dep)


def _merge_fwd(proj, bg, br_a, br_b, tl=256):
    L = proj.shape[0]
    goff = C_GATE // 1024

    def kern(g1_ref, g2_ref, b1_ref, b2_ref, a_ref, b_ref, o_ref):
        g1 = _sigmoid(g1_ref[...].astype(F32) + b1_ref[...])
        g2 = _sigmoid(g2_ref[...].astype(F32) + b2_ref[...])
        o_ref[...] = (g1 * a_ref[...] + g2 * b_ref[...]).astype(BF16)

    row = pl.BlockSpec((tl, 1024), lambda i: (i, 0))
    bg2 = bg.reshape(1, 2 * D)
    return pl.pallas_call(
        kern, grid=(L // tl,),
        in_specs=[pl.BlockSpec((tl, 1024), lambda i: (i, goff)), pl.BlockSpec((tl, 1024), lambda i: (i, goff + 1)),
                  pl.BlockSpec((1, 1024), lambda i: (0, 0)), pl.BlockSpec((1, 1024), lambda i: (0, 1)), row, row],
        out_specs=row, out_shape=jax.ShapeDtypeStruct((L, D), BF16), name="merge_fwd",
        compiler_params=_cp(("parallel",)),
    )(proj, proj, bg2, bg2, br_a, br_b)


def _branch_ssm_merge(yb, wb, proj, bg, br_a, tm=512):
    L, K = yb.shape
    tm = min(tm, L)
    goff = C_GATE // 1024

    def kern(a_ref, b_ref, g1_ref, g2_ref, b1_ref, b2_ref, bra_ref, brb_ref, m_ref):
        brb = lax.dot_general(a_ref[...], b_ref[...], _DIMS["nn"], preferred_element_type=F32)
        brb_ref[...] = brb
        g1 = _sigmoid(g1_ref[...].astype(F32) + b1_ref[...])
        g2 = _sigmoid(g2_ref[...].astype(F32) + b2_ref[...])
        m_ref[...] = (g1 * bra_ref[...] + g2 * brb).astype(BF16)

    row = pl.BlockSpec((tm, D), lambda i: (i, 0))
    bg2 = bg.reshape(1, 2 * D)
    return pl.pallas_call(
        kern, grid=(L // tm,),
        in_specs=[pl.BlockSpec((tm, K), lambda i: (i, 0)), pl.BlockSpec((K, D), lambda i: (0, 0)),
                  pl.BlockSpec((tm, D), lambda i: (i, goff)), pl.BlockSpec((tm, D), lambda i: (i, goff + 1)),
                  pl.BlockSpec((1, D), lambda i: (0, 0)), pl.BlockSpec((1, D), lambda i: (0, 1)), row],
        out_specs=(row, row), out_shape=(jax.ShapeDtypeStruct((L, D), F32), jax.ShapeDtypeStruct((L, D), BF16)),
        name="branch_ssm_merge", compiler_params=_cp(("parallel",)),
    )(yb, wb, proj, proj, bg2, bg2, br_a)


def _merge_bwd(dx1, wo, proj, bg, br_a, br_b, dproj, tl=512):
    L = proj.shape[0]
    tl = min(tl, L)
    goff = C_GATE // 1024

    def kern(dm_ref, wo_ref, g_ref, b_ref, a_ref, bb_ref, _, dbr_ref, dp_ref, gb_ref):
        j = pl.program_id(0)

        @pl.when(pl.program_id(1) == 0)
        def _():
            gb_ref[...] = jnp.zeros_like(gb_ref)

        g = _sigmoid(g_ref[...].astype(F32) + b_ref[...])
        br = jnp.where(j == 0, a_ref[...], bb_ref[...])
        dmv = lax.dot_general(dm_ref[...].astype(BF16), wo_ref[...], _DIMS["nt"], preferred_element_type=F32)
        dbr_ref[0] = (dmv * g).astype(BF16)
        dgate = dmv * br * g * (1.0 - g)
        gb_ref[...] += jnp.broadcast_to(jnp.sum(dgate, axis=0, keepdims=True), (8, 1024))
        dp_ref[...] = dgate.astype(BF16)

    row = pl.BlockSpec((tl, 1024), lambda j, i: (i, 0))
    gblk = pl.BlockSpec((tl, 1024), lambda j, i: (i, goff + j))
    return pl.pallas_call(
        kern, grid=(2, L // tl),
        in_specs=[row, pl.BlockSpec((D, D), lambda j, i: (0, 0)), gblk, pl.BlockSpec((1, 1024), lambda j, i: (0, j)),
                  row, row, pl.BlockSpec(memory_space=pl.ANY)],
        out_specs=(pl.BlockSpec((1, tl, 1024), lambda j, i: (j, i, 0)), gblk, pl.BlockSpec((8, 1024), lambda j, i: (0, j))),
        out_shape=(jax.ShapeDtypeStruct((2, L, D), BF16), jax.ShapeDtypeStruct(dproj.shape, BF16),
                   jax.ShapeDtypeStruct((8, 2 * D), F32)),
        input_output_aliases={6: 1}, name="merge_bwd", compiler_params=_cp(("parallel", "arbitrary")),
    )(dx1, wo, proj, bg.reshape(1, 2 * D), br_a, br_b, dproj)


def _coords():
    return lax.axis_index("x"), lax.axis_index("y"), lax.axis_index("c")


def _other_chips(sk):
    xk, yk = sk // 2, sk % 2
    return [((1 - xk, yk), 2 * (1 - xk) + yk), ((xk, 1 - yk), 2 * xk + 1 - yk), ((1 - xk, 1 - yk), 2 * (1 - xk) + 1 - yk)]


def _rows(start, size):
    assert size % 128 == 0
    return pl.ds(pl.multiple_of(start, 128), size)


def _per_chip(fn):
    x, y, _ = _coords()
    s = 2 * x + y
    for sk in range(4):
        pl.when(s == sk)(functools.partial(fn, sk))


XTRA = PIECE - PMAIN


def _place(shard, full_shape, block, index_map, idx, name, blk0=0, nblk=None, dep=None, into=None):
    in_block = block[-2:]
    if nblk is None:
        nblk = shard.shape[0] // in_block[0]

    def kern(idx_ref, s_ref, *rest):
        o_ref = rest[-1]
        o_ref[...] = s_ref[...].astype(BF16).reshape(o_ref.shape)

    extra = ([dep] if dep is not None else []) + ([into] if into is not None else [])
    grid_spec = pltpu.PrefetchScalarGridSpec(
        num_scalar_prefetch=1, grid=(nblk,),
        in_specs=[pl.BlockSpec(in_block, lambda i, idx_ref: (blk0 + i, 0))] + [_ANY] * len(extra),
        out_specs=pl.BlockSpec(block, index_map))
    aliases = {1 + len(extra): 0} if into is not None else {}
    return pl.pallas_call(kern, grid_spec=grid_spec, out_shape=jax.ShapeDtypeStruct(full_shape, BF16), name=name,
                          input_output_aliases=aliases, compiler_params=_cp(("arbitrary",)))(idx, shard, *extra)


_SEM = pl.BlockSpec(memory_space=pltpu.SEMAPHORE)
_EFFECT = pltpu.SideEffectType.DATAFLOW_SIDE_EFFECTING


_ANY = pl.BlockSpec(memory_space=pl.ANY)


def _tie(v, dep, name):
    def body(v_ref, dep_ref, o_ref):
        del v_ref, dep_ref, o_ref

    return pl.pallas_call(body, out_shape=jax.ShapeDtypeStruct(v.shape, v.dtype), in_specs=[_ANY, _ANY],
                          out_specs=_ANY, input_output_aliases={0: 0}, name=name)(v, dep)


def _split_call(name, arrays, start=None, wait=None, wait_sems=None, after=None):
    keys = list(arrays)
    n = len(keys)
    n_start = start.n if start is not None else 0
    afters = [] if after is None else (list(after) if isinstance(after, (list, tuple)) else [after])

    def body(*refs):
        pos = n
        if wait is not None:
            wss, wrs = refs[pos], refs[pos + 1]
            pos += 2
        pos += len(afters)
        if start is not None:
            nss, nrs = refs[pos], refs[pos + 1]
            pos += 2
        R = dict(zip(keys, refs[pos:pos + n]))
        token = refs[pos + n]
        x, y, c = _coords()

        def desc(src, dst, dev, ss, rs, k):
            return pltpu.make_async_remote_copy(src_ref=src, dst_ref=dst, send_sem=ss.at[k], recv_sem=rs.at[k],
                                                device_id=dev, device_id_type=MESH)

        def run(sk):
            if wait is not None:
                for k, (snd, land) in enumerate(wait.copies(sk, R)):
                    if snd is not None:
                        desc(snd[0], snd[1], snd[2], wss, wrs, k).wait_send()
                    if land is not None:
                        desc(land, land, (x, y, c), wss, wrs, k).wait_recv()
            if start is not None:
                for k, (snd, land) in enumerate(start.copies(sk, R)):
                    if snd is not None:
                        desc(snd[0], snd[1], snd[2], nss, nrs, k).start()

        _per_chip(run)
        token[...] = jnp.zeros_like(token)

    hbm = pl.BlockSpec(memory_space=HBM)
    vals = [arrays[k] for k in keys]
    ins, in_specs = list(vals), [hbm] * n
    if wait is not None:
        ins += list(wait_sems)
        in_specs += [_SEM, _SEM]
    ins += afters
    in_specs += [pl.BlockSpec(memory_space=pl.ANY)] * len(afters)
    out_shape, out_specs = [], []
    if start is not None:
        out_shape += [pltpu.SemaphoreType.DMA((n_start,)), pltpu.SemaphoreType.DMA((n_start,))]
        out_specs += [_SEM, _SEM]
    first = len(out_shape)
    out_shape += [jax.ShapeDtypeStruct(v.shape, v.dtype) for v in vals] + [jax.ShapeDtypeStruct((8, 128), F32)]
    out_specs += [hbm] * n + [pl.BlockSpec(memory_space=pltpu.VMEM)]
    res = pl.pallas_call(
        body, out_shape=tuple(out_shape), in_specs=in_specs, out_specs=tuple(out_specs),
        input_output_aliases={i: first + i for i in range(n)}, name=name,
        compiler_params=pltpu.CompilerParams(has_side_effects=_EFFECT),
    )(*ins)
    sems = (res[0], res[1]) if start is not None else None
    return dict(zip(keys, res[first:first + n])), sems, res[-1]


class _Plan:
    def __init__(self, n, copies):
        self.n, self.copies = n, copies


_HM, _HX = PMAIN // 2, XTRA // 2
WAVE0 = 768
WAVES = ((0, WAVE0), (WAVE0, _HM - WAVE0))
_WIN = {
    "wq0": (True, "wct", lambda r, sc, hc: r.at[_rows(PMAIN * sc + _HM * hc + WAVES[0][0], WAVES[0][1]), :]),
    "wq1": (True, "wct", lambda r, sc, hc: r.at[_rows(PMAIN * sc + _HM * hc + WAVES[1][0], WAVES[1][1]), :]),
    "xt": (True, "xt", lambda r, sc, hc: r.at[sc, _rows(_HX * hc, _HX), :]),
    "w1": (True, "w1", lambda r, sc, hc: r.at[_rows(512 * hc, 512), pl.ds(1024 * sc, 1024)]),
    "w2": (True, "w2", lambda r, sc, hc: r.at[_rows(1024 * sc + 512 * hc, 512), :]),
    "wa": (True, "wa", lambda r, sc, hc: r.at[_rows(256 * sc + 128 * hc, 128), :]),
    "wb": (True, "wb", lambda r, sc, hc: r.at[_rows(512 * sc + 256 * hc, 256), :]),
    "wo": (True, "wo", lambda r, sc, hc: r.at[_rows(256 * sc + 128 * hc, 128), :]),
    "cw": (False, "cw", lambda r, sc, hc: r.at[sc]),
}


_PIECE_SRC = {
    "wq0": lambda p, hc: p.at[_rows(_HM * hc + WAVES[0][0], WAVES[0][1]), :],
    "wq1": lambda p, hc: p.at[_rows(_HM * hc + WAVES[1][0], WAVES[1][1]), :],
    "xt": lambda p, hc: p.at[_rows(PMAIN + _HX * hc, _HX), :],
}


def _ag_chips_plan(keys):
    def copies(sk, R):
        _, _, c = _coords()
        out = []
        for key in keys:
            _, arr, win = _WIN[key]
            for (px, py), ps in _other_chips(sk):
                dst = win(R[arr], sk, c)
                src = _PIECE_SRC[key](R["piece"], c) if key in _PIECE_SRC else dst
                out.append(((src, dst, (px, py, c)), win(R[arr], ps, c)))
        return out
    return _Plan(3 * len(keys), copies)


def _ag_sibling_plan(keys):
    keys = [k for k in keys if _WIN[k][0]]

    def copies(sk, R):
        x, y, c = _coords()
        out = []
        for key in keys:
            _, arr, win = _WIN[key]
            for _, ps in _other_chips(sk):
                w = win(R[arr], ps, c)
                out.append(((w, w, (x, y, 1 - c)), win(R[arr], ps, 1 - c)))
        return out
    return _Plan(3 * len(keys), copies)


def _in_proj_wave(h, wct, wave, proj=None, tm=2048):
    L = h.shape[0]
    tm = min(tm, L)
    off, size = WAVES[wave]
    start = lambda j: pl.multiple_of(_HM * j + off, 128)

    def kern(h_ref, w_ref, *rest):
        o_ref = rest[-1]
        o_ref[...] = lax.dot_general(h_ref[...], w_ref[...], _DIMS["nt"], preferred_element_type=F32).astype(BF16)

    in_specs = [pl.BlockSpec((tm, D), lambda j, i: (i, 0)),
                pl.BlockSpec((pl.Element(size), pl.Element(D)), lambda j, i: (start(j), 0))]
    args, aliases = [h, wct], {}
    if proj is not None:
        in_specs.append(pl.BlockSpec(memory_space=pl.ANY))
        args.append(proj)
        aliases = {2: 0}
    return pl.pallas_call(
        kern, grid=(8, L // tm), in_specs=in_specs,
        out_specs=pl.BlockSpec((pl.Element(tm), pl.Element(size)), lambda j, i: (i * tm, start(j))),
        out_shape=jax.ShapeDtypeStruct((L, NCW), BF16), input_output_aliases=aliases,
        name="in_proj_wave%d" % wave, compiler_params=_cp(("parallel", "parallel")),
    )(*args)


def _fix_wct(wct, xt):
    nb = PMAIN // XTRA

    def kern(w_ref, x_ref, o_ref):
        k = pl.program_id(0)
        xv = x_ref[0]
        o_ref[...] = jnp.where(k < 3, (w_ref[...].astype(F32) + xv.astype(F32)).astype(BF16), xv)

    blk = pl.BlockSpec((XTRA, D), lambda k: (nb * (k + 1), 0))
    rblk = pl.BlockSpec((XTRA, D), lambda k: (jnp.where(k < 3, nb * (k + 1), 0), 0))
    return pl.pallas_call(
        kern, grid=(4,), in_specs=[rblk, pl.BlockSpec((1, XTRA, D), lambda k: (k, 0, 0))], out_specs=blk,
        out_shape=jax.ShapeDtypeStruct(wct.shape, BF16), input_output_aliases={0: 0}, name="fix_wct",
        compiler_params=_cp(("arbitrary",)),
    )(wct, xt)


_HP = PIECE // 2
_GWIN = [
    lambda r, sc, hc: r.at[_rows(PMAIN * sc + _HP * hc, _HP), :],
    lambda r, sc, hc: r.at[_rows(512 * hc, 512), pl.ds(1024 * sc, 1024)],
    lambda r, sc, hc: r.at[_rows(1024 * sc + 512 * hc, 512), :],
    lambda r, sc, hc: r.at[_rows(256 * sc + 128 * hc, 128), :],
    lambda r, sc, hc: r.at[_rows(512 * sc + 256 * hc, 256), :],
    lambda r, sc, hc: r.at[_rows(256 * sc + 128 * hc, 128), :],
]
HALF_SHAPES = [(PIECE // 2, D), (512, 1024), (512, 1024), (128, 1024), (256, 1024), (128, 1024)]


def _rs_sibling_plan(ts):
    def copies(sk, R):
        x, y, c = _coords()
        out = []
        for t in ts:
            for sc in range(4):
                land = R["ra%d" % t].at[sc]
                out.append(((_GWIN[t](R["g%d" % t], sc, 1 - c), land, (x, y, 1 - c)), land))
        return out
    return _Plan(4 * len(ts), copies)


def _rs_chips_plan(ts):
    def copies(sk, R):
        _, _, c = _coords()
        out = []
        for t in ts:
            for j, ((px, py), ps) in enumerate(_other_chips(sk)):
                land = R["rb%d" % t].at[j]
                out.append(((R["hb%d" % t].at[ps], land, (px, py, c)), land))
        return out
    return _Plan(3 * len(ts), copies)


def _rs_share_plan(ts):
    def copies(sk, R):
        x, y, c = _coords()
        out = []
        for t in ts:
            rows = HALF_SHAPES[t][0]
            mine = R["f%d" % t].at[_rows(rows * c, rows), :]
            out.append(((mine, mine, (x, y, 1 - c)), R["f%d" % t].at[_rows(rows * (1 - c), rows), :]))
        return out
    return _Plan(len(ts), copies)


def _half_tiling(t):
    rows, cols = HALF_SHAPES[t]
    if t == 0:
        return (rows // 2, cols), 2, lambda i: (i, 0)
    return (rows, cols), 1, lambda i: (0, 0)


def _window_spec(t, blk):
    if t == 0:
        return pl.BlockSpec((pl.Element(blk[0]), pl.Element(blk[1])), lambda i, sc, idx_ref: (
            pl.multiple_of(PMAIN * sc + _HP * idx_ref[1] + blk[0] * i, 128), 0))
    if t == 1:
        return pl.BlockSpec(blk, lambda i, sc, idx_ref: (idx_ref[1], sc))
    return pl.BlockSpec(blk, lambda i, sc, idx_ref: (2 * sc + idx_ref[1], 0))


def _chip_sum(g, ra, t, idx, name):
    rows, cols = HALF_SHAPES[t]
    blk, nblk, inner = _half_tiling(t)

    def kern(idx_ref, g_ref, r_ref, hb_ref, hf_ref):
        v = g_ref[...].astype(F32) + r_ref[0].astype(F32)
        hb_ref[0] = v.astype(BF16)

        @pl.when(pl.program_id(1) == idx_ref[0])
        def _():
            hf_ref[...] = v

    omap = lambda i, sc, idx_ref: (sc,) + inner(i)
    grid_spec = pltpu.PrefetchScalarGridSpec(
        num_scalar_prefetch=1, grid=(nblk, 4),
        in_specs=[_window_spec(t, blk), pl.BlockSpec((1,) + blk, omap)],
        out_specs=(pl.BlockSpec((1,) + blk, omap), pl.BlockSpec(blk, lambda i, sc, idx_ref: inner(i))))
    return pl.pallas_call(
        kern, grid_spec=grid_spec,
        out_shape=(jax.ShapeDtypeStruct((4, rows, cols), BF16), jax.ShapeDtypeStruct((rows, cols), F32)),
        name=name, compiler_params=_cp(("parallel", "arbitrary")),
    )(idx, g, ra)


def _final_sum(hf, rb, t, idx, name):
    rows, cols = HALF_SHAPES[t]
    blk, nblk, inner = _half_tiling(t)
    nbr = rows // blk[0]

    def kern(idx_ref, h_ref, r_ref, o_ref):
        o_ref[...] = ((h_ref[...] + r_ref[0].astype(F32)) + r_ref[1].astype(F32)) + r_ref[2].astype(F32)

    def omap(i, idx_ref):
        r, cidx = inner(i)
        return nbr * idx_ref[1] + r, cidx

    grid_spec = pltpu.PrefetchScalarGridSpec(
        num_scalar_prefetch=1, grid=(nblk,),
        in_specs=[pl.BlockSpec(blk, lambda i, idx_ref: inner(i)),
                  pl.BlockSpec((3,) + blk, lambda i, idx_ref: (0,) + inner(i))],
        out_specs=pl.BlockSpec(blk, omap))
    return pl.pallas_call(
        kern, grid_spec=grid_spec, out_shape=jax.ShapeDtypeStruct((2 * rows, cols), F32),
        name=name, compiler_params=_cp(("parallel",)),
    )(idx, hf, rb)


class _ReduceScatter:
    def __init__(self, ts, grads, idx, tag):
        self.ts, self.idx, self.tag = ts, idx, tag
        arr = {}
        for t in ts:
            arr["g%d" % t] = grads[t]
            arr["ra%d" % t] = lax.empty((4,) + HALF_SHAPES[t], BF16)
        self.plan = _rs_sibling_plan(ts)
        self.arr, self.sems, self.token = _split_call("rs_sibling_start_" + tag, arr, start=self.plan)

    def chips(self, after):
        arr, _, _ = _split_call("rs_sibling_wait_" + self.tag, self.arr, wait=self.plan, wait_sems=self.sems, after=after)
        brr, self.hf = {}, {}
        for t in self.ts:
            hb, self.hf[t] = _chip_sum(arr["g%d" % t], arr["ra%d" % t], t, self.idx, "chip_sum_%d" % t)
            brr["hb%d" % t] = hb
            brr["rb%d" % t] = lax.empty((3,) + HALF_SHAPES[t], BF16)
        self.plan = _rs_chips_plan(self.ts)
        self.arr, self.sems, self.token = _split_call("rs_chips_start_" + self.tag, brr, start=self.plan)
        return self.token

    def share(self, after):
        brr, _, _ = _split_call("rs_chips_wait_" + self.tag, self.arr, wait=self.plan, wait_sems=self.sems, after=after)
        frr = {"f%d" % t: _final_sum(self.hf[t], brr["rb%d" % t], t, self.idx, "final_sum_%d" % t) for t in self.ts}
        self.plan = _rs_share_plan(self.ts)
        self.arr, self.sems, self.token = _split_call("rs_share_start_" + self.tag, frr, start=self.plan)
        return self.token

    def result(self, after):
        frr, _, _ = _split_call("rs_share_wait_" + self.tag, self.arr, wait=self.plan, wait_sems=self.sems, after=after)
        return {t: frr["f%d" % t] for t in self.ts}


def _all8_plan(key):
    def copies(sk, R):
        x, y, c = _coords()
        own = R[key].at[4 * x + 2 * y + c]
        out = []
        for k in range(1, 8):
            dev = ((1 - x) if (k >> 2) & 1 else x, (1 - y) if (k >> 1) & 1 else y, (1 - c) if k & 1 else c)
            out.append(((own, own, dev), R[key].at[4 * dev[0] + 2 * dev[1] + dev[2]]))
        return out
    return _Plan(7, copies)


def _sum8(v, name="small_sum"):
    def kern(v_ref, o_ref):
        acc = v_ref[0]
        for k in range(1, 8):
            acc = acc + v_ref[k]
        o_ref[...] = acc

    return pl.pallas_call(kern, out_shape=jax.ShapeDtypeStruct(v.shape[1:], F32), name=name)(v)


def _adamw(w, g, m, v, name, tr=128, blk0=0, nblk=None, into=None, copy_g=False):
    R, C = w.shape
    tr = min(tr, R)
    if nblk is None:
        assert R % tr == 0 and blk0 == 0
        nblk = R // tr
    n_out = 4 if copy_g else 3

    def kern(*refs):
        w_ref, g_ref, m_ref, v_ref = refs[:4]
        d_ref, mo_ref, vo_ref = refs[-n_out:][:3]
        gv = g_ref[...]
        mn = ADAM_B1 * m_ref[...] + (1.0 - ADAM_B1) * gv
        vn = ADAM_B2 * v_ref[...] + (1.0 - ADAM_B2) * (gv * gv)
        m_hat = mn / (1.0 - ADAM_B1 ** ADAM_STEP)
        v_hat = vn / (1.0 - ADAM_B2 ** ADAM_STEP)
        d_ref[...] = -ADAM_LR * (m_hat / (jnp.sqrt(v_hat) + ADAM_EPS) + ADAM_WD * w_ref[...])
        mo_ref[...] = mn
        vo_ref[...] = vn
        if copy_g:
            refs[-1][...] = gv

    blk = pl.BlockSpec((tr, C), lambda i: (blk0 + i, 0))
    sd = jax.ShapeDtypeStruct((R, C), F32)
    in_specs, args, aliases = [blk] * 4, [w, g, m, v], {}
    if into is not None:
        in_specs += [pl.BlockSpec(memory_space=pl.ANY)] * 3
        args += list(into)
        aliases = {4: 0, 5: 1, 6: 2}
    return pl.pallas_call(kern, grid=(nblk,), in_specs=in_specs, out_specs=(blk,) * n_out, out_shape=(sd,) * n_out,
                          input_output_aliases=aliases, name=name, compiler_params=_cp(("parallel",)))(*args)


def _adamw_w_in(wt, gp, mt, vt, offs, name, r0, tr, nblk, views, into=None, blk_key=None):
    el = lambda n: (pl.Element(n), pl.Element(D))
    first = (lambda o: r0) if blk_key is None else (lambda o: tr * o[blk_key])
    own = pl.BlockSpec(el(tr), lambda i, o: (pl.multiple_of(first(o) + tr * i, 8), 0))

    def view(k):
        return pl.BlockSpec(el(tr), lambda i, o: (pl.multiple_of(jnp.maximum(first(o) + tr * i + o[k], 0), 8), 0))

    def kern(o_ref, w_ref, m_ref, v_ref, *refs):
        g_refs, (d_ref, mo_ref, vo_ref, go_ref) = refs[:len(views)], refs[-4:]
        gv = g_refs[0][...]
        if len(views) == 2:
            row = first(o_ref) + tr * pl.program_id(0) + lax.broadcasted_iota(jnp.int32, (tr, D), 0)
            gv = jnp.where(row < o_ref[2], gv, g_refs[1][...])
        mn = ADAM_B1 * m_ref[...] + (1.0 - ADAM_B1) * gv
        vn = ADAM_B2 * v_ref[...] + (1.0 - ADAM_B2) * (gv * gv)
        m_hat = mn / (1.0 - ADAM_B1 ** ADAM_STEP)
        v_hat = vn / (1.0 - ADAM_B2 ** ADAM_STEP)
        d_ref[...] = -ADAM_LR * (m_hat / (jnp.sqrt(v_hat) + ADAM_EPS) + ADAM_WD * w_ref[...])
        mo_ref[...] = mn
        vo_ref[...] = vn
        go_ref[...] = gv

    in_specs = [own, own, own] + [view(k) for k in views]
    args = [wt, mt, vt] + [gp] * len(views)
    aliases = {}
    if into is not None:
        in_specs += [pl.BlockSpec(memory_space=pl.ANY)] * 4
        args += list(into)
        aliases = {1 + len(args) - 4 + j: j for j in range(4)}
    grid_spec = pltpu.PrefetchScalarGridSpec(num_scalar_prefetch=1, grid=(nblk,), in_specs=in_specs,
                                             out_specs=(own,) * 4)
    sd = jax.ShapeDtypeStruct(wt.shape, F32)
    return pl.pallas_call(kern, grid_spec=grid_spec, out_shape=(sd,) * 4, input_output_aliases=aliases, name=name,
                          compiler_params=_cp(("parallel",)))(offs, *args)


def _to_piece(wt, s):
    z = lambda n: jnp.zeros((n, D), wt.dtype)
    pads = [functools.partial(lambda k, w: jnp.pad(w, ((8 * k, PIECE - W_SHARD - 8 * k), (0, 0))).astype(BF16), k)
            for k in range(3)]
    last = lambda w: jnp.concatenate([z(24), w[:744], w[776:], w[744:776], z(PIECE - 24 - W_SHARD)], axis=0).astype(BF16)
    return lax.switch(s, pads + [last], wt)


_SMALL = [("b_gate", 2048), ("ssm_conv_b", 4096), ("dt_bias", 32), ("A_log", 32), ("D_skip", 32),
          ("ssm_norm_w", 2048), ("norm_mlp", 1024), ("norm_final", 1024), ("sc_conv_w", 3072), ("ssm_conv_w", 16384),
          ("loss", 1)]


def _pack(vals, table, rows):
    parts = []
    for name, n in table:
        v = vals[name].reshape(-1).astype(F32)
        pad = (-n) % 128
        parts.append(jnp.pad(v, (0, pad)) if pad else v)
    flat = jnp.concatenate(parts)
    return jnp.pad(flat, (0, rows * 128 - flat.shape[0])).reshape(rows, 128)


def _unpack(arr, table):
    flat = arr.reshape(-1)
    out, off = {}, 0
    for name, n in table:
        out[name] = flat[off:off + n]
        off += n + ((-n) % 128)
    return out


def kernel(x, norm_mix, w_in, b_gate, sc_conv_w, ssm_conv_w, ssm_conv_b, dt_bias, A_log, D_skip, ssm_norm_w, w_branch_sc, w_branch_ssm, w_out, norm_mlp, w_mlp1, w_mlp2, norm_final, loss_target, m_norm_mix, m_w_in, m_b_gate, m_sc_conv_w, m_ssm_conv_w, m_ssm_conv_b, m_dt_bias, m_A_log, m_D_skip, m_ssm_norm_w, m_w_branch_sc, m_w_branch_ssm, m_w_out, m_norm_mlp, m_w_mlp1, m_w_mlp2, m_norm_final, v_norm_mix, v_w_in, v_b_gate, v_sc_conv_w, v_ssm_conv_w, v_ssm_conv_b, v_dt_bias, v_A_log, v_D_skip, v_ssm_norm_w, v_w_branch_sc, v_w_branch_ssm, v_w_out, v_norm_mlp, v_w_mlp1, v_w_mlp2, v_norm_final):
    L = x.shape[1]
    nc = L // Q
    xi, yi, ci = lax.axis_index("x"), lax.axis_index("y"), lax.axis_index("c")
    s = 2 * xi + yi
    idx = jnp.stack([s, ci]).astype(jnp.int32)
    x0 = x.reshape(L, D)
    tgt = loss_target.reshape(L, D)
    small_names = ["b_gate", "sc_conv_w", "ssm_conv_w", "ssm_conv_b", "dt_bias", "A_log", "D_skip", "ssm_norm_w",
                   "norm_mlp", "norm_final"]
    small_wmv = [dict(zip(small_names, vals)) for vals in (
        (b_gate, sc_conv_w, ssm_conv_w, ssm_conv_b, dt_bias, A_log, D_skip, ssm_norm_w, norm_mlp, norm_final),
        (m_b_gate, m_sc_conv_w, m_ssm_conv_w, m_ssm_conv_b, m_dt_bias, m_A_log, m_D_skip, m_ssm_norm_w, m_norm_mlp,
         m_norm_final),
        (v_b_gate, v_sc_conv_w, v_ssm_conv_w, v_ssm_conv_b, v_dt_bias, v_A_log, v_D_skip, v_ssm_norm_w, v_norm_mlp,
         v_norm_final))]
    small_table = [(n, int(small_wmv[0][n].size)) for n in small_names]
    small_rows = 136
    pk_w, pk_m, pk_v = [_pack(d, small_table, small_rows) for d in small_wmv]

    piece = _to_piece(w_in.T, s)
    nb = PMAIN // XTRA
    cws = jnp.zeros((8, 1280), F32)
    cws = cws.at[0:3, 0:256].set(sc_conv_w).at[0:4, 256:1280].set(ssm_conv_w)
    cw0 = lax.dynamic_update_slice(jnp.zeros((4, 8, 1280), F32), cws[None], (s, 0, 0))
    win_keys, win2_keys, mid_keys, end_keys = ["xt", "cw", "wq0"], ["wq1"], ["wa", "wb", "wo", "w1"], ["w2"]
    gw, sems_w, tok = _split_call(
        "ag_win_start", {"wct": lax.empty((NCW, D), BF16), "xt": lax.empty((4, XTRA, D), BF16), "cw": cw0, "piece": piece},
        start=_ag_chips_plan(win_keys))
    g2, sems_w2, tok = _split_call("ag_win2_start", {"wct": gw["wct"], "piece": gw["piece"]},
                                   start=_ag_chips_plan(win2_keys), after=tok)
    piece = g2["piece"]
    gw["wct"] = _place(piece, (NCW, D), (XTRA, D), lambda i, r: (nb * r[0] + i, 0), idx, "place_wct", nblk=nb,
                       dep=tok, into=g2["wct"])
    gw["xt"] = _place(piece, (4, XTRA, D), (1, XTRA, D), lambda i, r: (r[0], 0, 0), idx, "place_xt", blk0=nb, nblk=1,
                      dep=tok, into=gw["xt"])
    gw["piece"] = piece
    wa0 = _place(w_branch_sc, (D, D), (256, 1024), lambda i, r: (r[0], 0), idx, "place_wa", dep=tok)
    wb0 = _place(w_branch_ssm, (INNER, D), (512, 1024), lambda i, r: (r[0], 0), idx, "place_wb", dep=tok)
    wo0 = _place(w_out, (D, D), (256, 1024), lambda i, r: (r[0], 0), idx, "place_wo", dep=tok)
    w10 = _place(w_mlp1, (D, DFF), (256, 1024), lambda i, r: (i, r[0]), idx, "place_w1", dep=tok)
    gm, sems_m, tok = _split_call("ag_mid_start", {"wa": wa0, "wb": wb0, "wo": wo0, "w1": w10},
                                  start=_ag_chips_plan(mid_keys))
    w20 = _place(w_mlp2, (DFF, D), (256, 1024), lambda i, r: (4 * r[0] + i, 0), idx, "place_w2", dep=tok)
    ge, sems_e, tok = _split_call("ag_end_start", {"w2": w20}, start=_ag_chips_plan(end_keys))
    h = _rms_fwd(x0, norm_mix, "rms_mix", dep=tok)
    gw, sems_w, tok = _split_call("ag_win_pass", gw, wait=_ag_chips_plan(win_keys), wait_sems=sems_w,
                                  start=_ag_sibling_plan(win_keys), after=[h, pk_w, pk_m, pk_v])
    gw, _, _ = _split_call("ag_win_done", gw, wait=_ag_sibling_plan(win_keys), wait_sems=sems_w, after=tok)
    wc, cw_all = _fix_wct(gw["wct"], gw["xt"]), gw["cw"]
    sc_w_full = jnp.concatenate([cw_all[k, :, 0:256] for k in range(4)], axis=1)
    ssm_w_full = jnp.concatenate([cw_all[k, :, 256:1280] for k in range(4)], axis=1)
    cw4 = ssm_w_full.at[4].set(ssm_conv_b)
    vec = jnp.zeros((8, 128), F32).at[0, :NH].set(dt_bias).at[1, :NH].set(A_log)
    vecg = jnp.zeros((NG, 8, 128), F32).at[:, 0, :4].set(A_log.reshape(NG, 4)).at[:, 1, :4].set(D_skip.reshape(NG, 4))

    dtraw = _matmul(h, wc[C_DT:], "nt", F32, 512, 256, 1024, "in_proj_dt")
    proj = _in_proj_wave(h, wc, 0)
    g2, sems_w2, tok = _split_call("ag_win2_pass", {"wct": wc, "piece": gw["piece"]},
                                   wait=_ag_chips_plan(win2_keys), wait_sems=sems_w2,
                                   start=_ag_sibling_plan(win2_keys), after=[proj, dtraw])
    g2, _, _ = _split_call("ag_win2_done", g2, wait=_ag_sibling_plan(win2_keys), wait_sems=sems_w2, after=tok)
    wc = g2["wct"]
    proj = _in_proj_wave(h, wc, 1, proj=proj)
    ya = _sc_fwd(proj, sc_w_full)
    xbc = _ssm_conv_fwd(proj, cw4)
    dt4, cs4, sg4 = _dt_prep(dtraw, vec)
    y, s_all = _ssd_fwd(xbc, dt4, cs4, vecg)
    gm, sems_m, tok = _split_call("ag_mid_pass", gm, wait=_ag_chips_plan(mid_keys), wait_sems=sems_m,
                                  start=_ag_sibling_plan(mid_keys), after=[y, ya])
    y = _tie(y, tok, "tie_y")
    yb = _gnorm_fwd(y, proj, ssm_norm_w)
    gm, _, _ = _split_call("ag_mid_done", gm, wait=_ag_sibling_plan(mid_keys), wait_sems=sems_m, after=yb)
    wa, wb, wo, w1 = gm["wa"], gm["wb"], gm["wo"], gm["w1"]
    ge, sems_e, tok = _split_call("ag_end_pass", ge, wait=_ag_chips_plan(end_keys), wait_sems=sems_e,
                                  start=_ag_sibling_plan(end_keys), after=yb)
    br_a = _matmul(ya, wa, "nn", F32, 1024, 1024, 1024, "branch_sc", dep=tok)
    br_b, merged = _branch_ssm_merge(yb, wb, proj, b_gate, br_a)
    x1, h2 = _matmul_res_rms(merged, wo, x0, norm_mlp, 1024, "out_proj")
    a1, rl = _matmul(h2, w1, "nn", BF16, 1024, 1024, 1024, "mlp1", epi="relu2", n_outer=True)
    ge, _, _ = _split_call("ag_end_done", ge, wait=_ag_sibling_plan(end_keys), wait_sems=sems_e, after=a1)
    w2 = ge["w2"]
    dx2, g_nf, loss8 = _matmul_res_final(rl, w2, x1, norm_final, tgt, 512, "mlp2")

    da = _matmul(dx2, w2, "nt", BF16, 1024, 1024, 1024, "mlp2_dx", epi="drelu", extra=a1, n_outer=True)
    g_w2 = _matmul(rl, dx2, "tn", BF16, 1024, 1024, 2048, "mlp2_dw")
    g_w1 = _matmul(h2, da, "tn", BF16, 1024, 1024, 2048, "mlp1_dw")
    dx1, g_nmlp = _matmul_rms_bwd(da, w1, "nt", x1, norm_mlp, dx2, 512, 4096, "mlp1_dx")
    g_wo = _matmul(merged, dx1, "tn", BF16, 1024, 1024, 2048, "out_proj_dw")
    dproj = lax.empty((L, NCW), BF16)
    dbr, dproj, g_bg = _merge_bwd(dx1, wo, proj, b_gate, br_a, br_b, dproj)
    dya = _matmul(dbr[0], wa, "nt", F32, 1024, 1024, 1024, "branch_sc_dx")
    g_wa = _matmul(ya, dbr[0], "tn", BF16, 1024, 1024, 2048, "branch_sc_dw")
    dproj, g_scw = _sc_bwd(dya, proj, sc_w_full, dproj)
    g_wb = _matmul(yb, dbr[1], "tn", BF16, 1024, 1024, 2048, "branch_ssm_dw")
    rs_a = _ReduceScatter([1, 2, 3, 4, 5], {1: g_w1, 2: g_w2, 3: g_wa, 4: g_wb, 5: g_wo}, idx, "a")
    dy, dproj, g_snw = _gnorm_bwd(dbr, wb, y, proj, ssm_norm_w, dproj, rs_a.token)
    tok = rs_a.chips(after=dy)
    dxs, dbm, dcm, ddt_g, st = _ssd_bwd(xbc, dt4, cs4, sg4, vecg, s_all, _tie(dy, tok, "tie_dy"))
    dproj, gx1 = _ssm_conv_bwd(dxs, proj, cw4, dproj, 0, "ssm_conv_bwd_x")
    dproj, gx2 = _ssm_conv_bwd(dbm, proj, cw4, dproj, INNER, "ssm_conv_bwd_b")
    dproj, gx3 = _ssm_conv_bwd(dcm, proj, cw4, dproj, INNER + NG * NS, "ssm_conv_bwd_c")
    g_cw4 = jnp.concatenate([gx1, gx2, gx3], axis=1)
    dproj, g_dtb = _dt_bwd(ddt_g, dproj)
    small = {"b_gate": g_bg[0], "ssm_conv_b": g_cw4[4], "dt_bias": g_dtb[0, :NH],
             "A_log": st[:, 0, :4], "D_skip": st[:, 1, :4], "ssm_norm_w": g_snw[0], "norm_mlp": g_nmlp[0],
             "norm_final": g_nf[0], "sc_conv_w": g_scw[0:3], "ssm_conv_w": g_cw4[0:4], "loss": loss8[0, 0:1]}
    me = 4 * xi + 2 * yi + ci
    sm8 = lax.dynamic_update_slice(jnp.zeros((8, SMALL_ROWS, 128), F32), _pack(small, _SMALL, SMALL_ROWS)[None], (me, 0, 0))
    sm_arr, sm_sems, tok = _split_call("small_start", {"sm": sm8}, start=_all8_plan("sm"))
    g_wc = _matmul(dproj, h, "tn", BF16, 1280, 1024, 2048, "in_proj_dw", dep=tok)
    rs_b = _ReduceScatter([0], {0: g_wc}, idx, "b")
    tok = rs_a.share(after=rs_b.token)
    tok = rs_b.chips(after=tok)
    grad_x, g_nm = _matmul_rms_bwd(dproj, wc, "nn", x0, norm_mix, dx1, 512, 5760, "in_proj_dx", dep=tok)
    nm8 = lax.dynamic_update_slice(jnp.zeros((8, 8, 128), F32), g_nm[0].reshape(1, 8, 128), (me, 0, 0))
    nm_arr, nm_sems, tok = _split_call("norm_mix_start", {"nm": nm8}, start=_all8_plan("nm"))
    sm_arr, _, _ = _split_call("small_wait", sm_arr, wait=_all8_plan("sm"), wait_sems=sm_sems, after=tok)
    small_sum = _sum8(sm_arr["sm"])
    gs = _unpack(small_sum, _SMALL)
    red = rs_a.result(after=tok)
    big = {"w_mlp1": red[1], "w_mlp2": red[2], "w_branch_sc": red[3], "w_branch_ssm": red[4], "w_out": red[5]}

    given = dict(norm_mix=norm_mix, w_in=w_in, b_gate=b_gate, sc_conv_w=sc_conv_w, ssm_conv_w=ssm_conv_w, ssm_conv_b=ssm_conv_b, dt_bias=dt_bias, A_log=A_log, D_skip=D_skip, ssm_norm_w=ssm_norm_w, w_branch_sc=w_branch_sc, w_branch_ssm=w_branch_ssm, w_out=w_out, norm_mlp=norm_mlp, w_mlp1=w_mlp1, w_mlp2=w_mlp2, norm_final=norm_final,
                 m_norm_mix=m_norm_mix, m_w_in=m_w_in, m_b_gate=m_b_gate, m_sc_conv_w=m_sc_conv_w, m_ssm_conv_w=m_ssm_conv_w, m_ssm_conv_b=m_ssm_conv_b, m_dt_bias=m_dt_bias, m_A_log=m_A_log, m_D_skip=m_D_skip, m_ssm_norm_w=m_ssm_norm_w, m_w_branch_sc=m_w_branch_sc, m_w_branch_ssm=m_w_branch_ssm, m_w_out=m_w_out, m_norm_mlp=m_norm_mlp, m_w_mlp1=m_w_mlp1, m_w_mlp2=m_w_mlp2, m_norm_final=m_norm_final,
                 v_norm_mix=v_norm_mix, v_w_in=v_w_in, v_b_gate=v_b_gate, v_sc_conv_w=v_sc_conv_w, v_ssm_conv_w=v_ssm_conv_w, v_ssm_conv_b=v_ssm_conv_b, v_dt_bias=v_dt_bias, v_A_log=v_A_log, v_D_skip=v_D_skip, v_ssm_norm_w=v_ssm_norm_w, v_w_branch_sc=v_w_branch_sc, v_w_branch_ssm=v_w_branch_ssm, v_w_out=v_w_out, v_norm_mlp=v_norm_mlp, v_w_mlp1=v_w_mlp1, v_w_mlp2=v_w_mlp2, v_norm_final=v_norm_final)
    order = ["norm_mix", "w_in", "b_gate", "sc_conv_w", "ssm_conv_w", "ssm_conv_b", "dt_bias", "A_log", "D_skip",
             "ssm_norm_w", "w_branch_sc", "w_branch_ssm", "w_out", "norm_mlp", "w_mlp1", "w_mlp2", "norm_final"]
    grad, delta, new_m, new_v = {}, {}, {}, {}
    for n in big:
        delta[n], new_m[n], new_v[n], grad[n] = _adamw(given[n], big[n], given["m_" + n], given["v_" + n],
                                                       "adamw_" + n, copy_g=True)
    big["w_in"] = None
    grad_small = {n: gs[n].reshape(given[n].shape) for n in small_names if n not in ("sc_conv_w", "ssm_conv_w")}
    grad_small["sc_conv_w"] = lax.dynamic_slice(gs["sc_conv_w"].reshape(3, D), (0, 256 * s), (3, 256))
    grad_small["ssm_conv_w"] = lax.dynamic_slice(gs["ssm_conv_w"].reshape(4, XBC), (0, 1024 * s), (4, 1024))
    table = small_table
    ds_, ms_, vs_ = _adamw(pk_w, _pack(grad_small, table, small_rows), pk_m, pk_v, "adamw_small", tr=small_rows)
    ds_, ms_, vs_ = _unpack(ds_, table), _unpack(ms_, table), _unpack(vs_, table)
    for n in grad_small:
        shp = given[n].shape
        grad[n] = grad_small[n]
        delta[n], new_m[n], new_v[n] = ds_[n].reshape(shp), ms_[n].reshape(shp), vs_[n].reshape(shp)

    done = [new_v[n] for n in ("w_mlp1", "w_mlp2", "w_branch_sc", "w_branch_ssm", "w_out")] + [vs_["b_gate"]]
    tok = rs_b.share(after=done)
    offs = jnp.where(s == 3, jnp.array([24, -8, 744, 2072, -8], jnp.int32),
                     jnp.stack([8 * s, 8 * s, 0 * s, 8 * s, 8 * s]).astype(jnp.int32))
    offs = jnp.concatenate([offs, jnp.stack([7 * ci, 4 - 4 * ci]).astype(jnp.int32)])
    nmain = W_SHARD // 256
    wt_own = (w_in.T, rs_b.arr["f0"], m_w_in.T, v_w_in.T, offs)
    res = _adamw_w_in(*wt_own, "adamw_w_in_own", 0, 256, 4, (0, 1), blk_key=5)
    gp = rs_b.result(after=[tok, res[0]])[0]
    wt_args = (w_in.T, gp, m_w_in.T, v_w_in.T, offs)
    res = _adamw_w_in(*wt_args, "adamw_w_in", 0, 256, nmain - 4, (0, 1), into=res, blk_key=6)
    res = _adamw_w_in(*wt_args, "adamw_w_in_dt", 744, 32, 1, (3,), into=res)
    dt_, mt_, vt_, gwt = _adamw_w_in(*wt_args, "adamw_w_in_tail", 256 * nmain, 8, 1, (4,), into=res)
    grad["w_in"], delta["w_in"], new_m["w_in"], new_v["w_in"] = gwt.T, dt_.T, mt_.T, vt_.T
    nm_arr, _, _ = _split_call("norm_mix_wait", nm_arr, wait=_all8_plan("nm"), wait_sems=nm_sems, after=tok)
    g8 = _sum8(nm_arr["nm"], "norm_mix_sum")
    r8 = lambda a: a.reshape(8, 128)
    d8, m8, v8 = _adamw(r8(norm_mix), g8, r8(m_norm_mix), r8(v_norm_mix), "adamw_norm_mix", tr=8)
    grad["norm_mix"], delta["norm_mix"] = g8.reshape(D), d8.reshape(D)
    new_m["norm_mix"], new_v["norm_mix"] = m8.reshape(D), v8.reshape(D)

    loss = gs["loss"].reshape(())
    return (loss, grad_x.reshape(1, L, D), *[grad[n] for n in order], *[delta[n] for n in order],
            *[new_m[n] for n in order], *[new_v[n] for n in order])
```

```python
import functools

import jax
import jax.numpy as jnp
from jax import lax
from jax.experimental import pallas as pl
from jax.experimental.pallas import tpu as pltpu

F32 = jnp.float32
BF16 = jnp.bfloat16
MESH = pl.DeviceIdType.MESH
HBM = pltpu.HBM

D = 1024
INNER = 2048
HD = 64
NH = 32
NG = 8
NS = 128
Q = 128
GPS = 8
XBC = 4096
DFF = 4096
EPS = 1e-6
W_SHARD = 2824
NCW = 11520
PIECE = 3072
PMAIN = 2816
C_Z, C_XBC, C_GATE, C_DT = 3072, 5120, 9216, 11264
SMALL_ROWS = 256
VMEM_LIMIT = 56 * 1024 * 1024

ADAM_LR, ADAM_B1, ADAM_B2, ADAM_EPS, ADAM_WD, ADAM_STEP = 0.001, 0.9, 0.999, 1e-08, 0.01, 10


def _cp(sem=None, vmem=VMEM_LIMIT):
    return pltpu.CompilerParams(dimension_semantics=sem, vmem_limit_bytes=vmem)


def _sigmoid(v):
    return 1.0 / (1.0 + jnp.exp(-v))


_DIMS = {"nn": (((1,), (0,)), ((), ())), "nt": (((1,), (1,)), ((), ())), "tn": (((0,), (0,)), ((), ()))}


def _matmul(a, b, mode, out_dtype, tm, tn, tk, name, epi=None, extra=None, n_outer=False, dep=None):
    if mode == "tn":
        K, M = a.shape
    else:
        M, K = a.shape
    N = b.shape[0] if mode == "nt" else b.shape[1]
    tm, tn, tk = min(tm, M), min(tn, N), min(tk, K)
    assert M % tm == 0 and N % tn == 0 and K % tk == 0, (name, M, N, K, tm, tn, tk)
    nm, nn, nk = M // tm, N // tn, K // tk
    dims = _DIMS[mode]

    def ij(p0, p1):
        return (p1, p0) if n_outer else (p0, p1)

    if mode == "tn":
        a_spec = pl.BlockSpec((tk, tm), lambda p0, p1, k: (k, ij(p0, p1)[0]))
    else:
        a_spec = pl.BlockSpec((tm, tk), lambda p0, p1, k: (ij(p0, p1)[0], k))
    if mode == "nt":
        b_spec = pl.BlockSpec((tn, tk), lambda p0, p1, k: (ij(p0, p1)[1], k))
    else:
        b_spec = pl.BlockSpec((tk, tn), lambda p0, p1, k: (k, ij(p0, p1)[1]))
    o_spec = pl.BlockSpec((tm, tn), lambda p0, p1, k: ij(p0, p1))
    in_specs = [a_spec, b_spec]
    args = [a, b]
    if epi in ("res", "drelu"):
        in_specs.append(o_spec)
        args.append(extra)
    if dep is not None:
        in_specs.append(pl.BlockSpec(memory_space=pl.ANY))
        args.append(dep)
    n_in = len(args)
    if epi == "relu2":
        out_shape = (jax.ShapeDtypeStruct((M, N), out_dtype), jax.ShapeDtypeStruct((M, N), BF16))
        out_specs = (o_spec, o_spec)
    else:
        out_shape = jax.ShapeDtypeStruct((M, N), out_dtype)
        out_specs = o_spec

    def kern(*refs):
        a_ref, b_ref = refs[0], refs[1]
        e_ref = refs[2] if epi in ("res", "drelu") else None
        acc = refs[-1]
        outs = refs[n_in:-1] if nk > 1 else refs[n_in:]
        k = pl.program_id(2)

        def product():
            return lax.dot_general(a_ref[...].astype(BF16), b_ref[...].astype(BF16), dims, preferred_element_type=F32)

        def finish(r):
            if epi is None:
                outs[0][...] = r.astype(out_dtype)
            elif epi == "res":
                outs[0][...] = (r + e_ref[...]).astype(out_dtype)
            elif epi == "relu2":
                outs[0][...] = r.astype(out_dtype)
                t = jnp.maximum(r, 0.0)
                outs[1][...] = (t * t).astype(BF16)
            else:
                outs[0][...] = (r * (2.0 * jnp.maximum(e_ref[...].astype(F32), 0.0))).astype(out_dtype)

        if nk == 1:
            finish(product())
        else:
            @pl.when(k == 0)
            def _():
                acc[...] = jnp.zeros_like(acc)

            acc[...] += product()

            @pl.when(k == nk - 1)
            def _():
                finish(acc[...])

    grid = (nn, nm, nk) if n_outer else (nm, nn, nk)
    return pl.pallas_call(
        kern, grid=grid, in_specs=in_specs, out_specs=out_specs, out_shape=out_shape,
        scratch_shapes=[pltpu.VMEM((tm, tn), F32)] if nk > 1 else [], name=name,
        compiler_params=_cp(("parallel", "parallel", "arbitrary")),
    )(*args)


def _in_proj_dw_rows(dproj, h, gwct, row0, nrows, tm, name, dep=None):
    L = h.shape[0]
    assert row0 % tm == 0 and nrows % tm == 0
    b0 = row0 // tm
    deps = [] if dep is None else [dep]

    def kern(a_ref, h_ref, *rest):
        rest[-1][...] = lax.dot_general(a_ref[...], h_ref[...], _DIMS["tn"], preferred_element_type=F32).astype(BF16)

    return pl.pallas_call(
        kern, grid=(nrows // tm,),
        in_specs=[pl.BlockSpec((L, tm), lambda i: (0, b0 + i)), pl.BlockSpec((L, D), lambda i: (0, 0)),
                  pl.BlockSpec(memory_space=pl.ANY)] + [pl.BlockSpec(memory_space=pl.ANY)] * len(deps),
        out_specs=pl.BlockSpec((tm, D), lambda i: (b0 + i, 0)), out_shape=jax.ShapeDtypeStruct(gwct.shape, BF16),
        input_output_aliases={2: 0}, name=name, compiler_params=_cp(("parallel",)),
    )(dproj, h, gwct, *deps)


def _matmul_res_rms(a, b, x, w, tm, name):
    M, K = a.shape
    tm = min(tm, M)

    def kern(a_ref, b_ref, x_ref, w_ref, x1_ref, h_ref):
        x1 = x_ref[...] + lax.dot_general(a_ref[...].astype(BF16), b_ref[...].astype(BF16), _DIMS["nn"],
                                          preferred_element_type=F32)
        x1_ref[...] = x1
        r = lax.rsqrt(jnp.mean(x1 * x1, axis=-1, keepdims=True) + EPS)
        h_ref[...] = ((x1 * r) * w_ref[...]).astype(BF16)

    row = pl.BlockSpec((tm, D), lambda i: (i, 0))
    return pl.pallas_call(
        kern, grid=(M // tm,),
        in_specs=[pl.BlockSpec((tm, K), lambda i: (i, 0)), pl.BlockSpec((K, D), lambda i: (0, 0)), row,
                  pl.BlockSpec((1, D), lambda i: (0, 0))],
        out_specs=(row, row), out_shape=(jax.ShapeDtypeStruct((M, D), F32), jax.ShapeDtypeStruct((M, D), BF16)),
        name=name, compiler_params=_cp(("parallel",)),
    )(a, b, x, w.reshape(1, D))


def _matmul_res_final(a, b, x, w, tgt, tm, name):
    M, K = a.shape
    tm = min(tm, M)

    def kern(a_ref, b_ref, x_ref, w_ref, t_ref, dx_ref, gw_ref, loss_ref):
        @pl.when(pl.program_id(0) == 0)
        def _():
            gw_ref[...] = jnp.zeros_like(gw_ref)
            loss_ref[...] = jnp.zeros_like(loss_ref)

        xv = x_ref[...] + lax.dot_general(a_ref[...].astype(BF16), b_ref[...].astype(BF16), _DIMS["nn"],
                                          preferred_element_type=F32)
        r = lax.rsqrt(jnp.mean(xv * xv, axis=-1, keepdims=True) + EPS)
        xn = xv * r
        e = xn * w_ref[...] - t_ref[...]
        loss_ref[...] += 0.5 * jnp.sum(jnp.mean(e * e, axis=-1, keepdims=True))
        dyv = e * (1.0 / D)
        gw_ref[...] += jnp.broadcast_to(jnp.sum(dyv * xn, axis=0, keepdims=True), (8, D))
        dxn = dyv * w_ref[...]
        dx_ref[...] = r * (dxn - xn * jnp.mean(dxn * xn, axis=-1, keepdims=True))

    row = pl.BlockSpec((tm, D), lambda i: (i, 0))
    return pl.pallas_call(
        kern, grid=(M // tm,),
        in_specs=[pl.BlockSpec((tm, K), lambda i: (i, 0)), pl.BlockSpec((K, D), lambda i: (0, 0)), row,
                  pl.BlockSpec((1, D), lambda i: (0, 0)), row],
        out_specs=(row, pl.BlockSpec((8, D), lambda i: (0, 0)), pl.BlockSpec((8, 128), lambda i: (0, 0))),
        out_shape=(jax.ShapeDtypeStruct((M, D), F32), jax.ShapeDtypeStruct((8, D), F32),
                   jax.ShapeDtypeStruct((8, 128), F32)),
        name=name, compiler_params=_cp(("arbitrary",)),
    )(a, b, x, w.reshape(1, D), tgt)


def _matmul_rms_bwd(a, b, mode, x, w, res, tm, tk, name, dep=None):
    M, K = a.shape
    tm, tk = min(tm, M), min(tk, K)
    nk = K // tk
    assert M % tm == 0 and K % tk == 0
    b_spec = (pl.BlockSpec((tk, D), lambda i, k: (k, 0)) if mode == "nn" else pl.BlockSpec((D, tk), lambda i, k: (0, k)))
    row = pl.BlockSpec((tm, D), lambda i, k: (i, 0))
    deps = [] if dep is None else [dep]

    def kern(a_ref, b_ref, x_ref, w_ref, res_ref, *rest):
        dx_ref, gw_ref, acc = rest[-3:]
        i, k = pl.program_id(0), pl.program_id(1)

        @pl.when((i == 0) & (k == 0))
        def _():
            gw_ref[...] = jnp.zeros_like(gw_ref)

        def product():
            return lax.dot_general(a_ref[...].astype(BF16), b_ref[...].astype(BF16), _DIMS[mode],
                                   preferred_element_type=F32)

        def finish(dyv):
            xv = x_ref[...]
            r = lax.rsqrt(jnp.mean(xv * xv, axis=-1, keepdims=True) + EPS)
            xn = xv * r
            gw_ref[...] += jnp.broadcast_to(jnp.sum(dyv * xn, axis=0, keepdims=True), (8, D))
            dxn = dyv * w_ref[...]
            dx_ref[...] = res_ref[...] + r * (dxn - xn * jnp.mean(dxn * xn, axis=-1, keepdims=True))

        if nk == 1:
            finish(product())
        else:
            @pl.when(k == 0)
            def _():
                acc[...] = jnp.zeros_like(acc)

            acc[...] += product()

            @pl.when(k == nk - 1)
            def _():
                finish(acc[...])

    return pl.pallas_call(
        kern, grid=(M // tm, nk),
        in_specs=[pl.BlockSpec((tm, tk), lambda i, k: (i, k)), b_spec, row, pl.BlockSpec((1, D), lambda i, k: (0, 0)),
                  row] + [pl.BlockSpec(memory_space=pl.ANY)] * len(deps),
        out_specs=(row, pl.BlockSpec((8, D), lambda i, k: (0, 0))),
        out_shape=(jax.ShapeDtypeStruct((M, D), F32), jax.ShapeDtypeStruct((8, D), F32)),
        scratch_shapes=[pltpu.VMEM((tm, D), F32)], name=name, compiler_params=_cp(("arbitrary", "arbitrary")),
    )(a, b, x, w.reshape(1, D), res, *deps)


def _rms_fwd(x, w, name, tl=256, dep=None):
    L = x.shape[0]

    def kern(x_ref, w_ref, *rest):
        o_ref = rest[-1]
        xv = x_ref[...]
        r = lax.rsqrt(jnp.mean(xv * xv, axis=-1, keepdims=True) + EPS)
        o_ref[...] = ((xv * r) * w_ref[...]).astype(BF16)

    row = pl.BlockSpec((tl, D), lambda i: (i, 0))
    deps = [] if dep is None else [dep]
    return pl.pallas_call(
        kern, grid=(L // tl,),
        in_specs=[row, pl.BlockSpec((1, D), lambda i: (0, 0))] + [pl.BlockSpec(memory_space=pl.ANY)] * len(deps),
        out_specs=row, out_shape=jax.ShapeDtypeStruct((L, D), BF16), name=name, compiler_params=_cp(("parallel",)),
    )(x, w.reshape(1, D), *deps)


def _down(v, k):
    if k == 0:
        return v
    t = lax.broadcasted_iota(jnp.int32, v.shape, 0)
    return jnp.where(t >= k, pltpu.roll(v, k, axis=0), 0.0)


def _up(v, k):
    if k == 0:
        return v
    n = v.shape[0]
    t = lax.broadcasted_iota(jnp.int32, v.shape, 0)
    return jnp.where(t < n - k, pltpu.roll(v, n - k, axis=0), 0.0)


TW = 256


def _sc_fwd(proj, cw):
    L = proj.shape[0]
    nb = D // TW

    def kern(b_ref, c_ref, x_ref, w_ref, o_ref):
        u = c_ref[...].astype(F32) * x_ref[...].astype(F32)
        w = w_ref[...]
        cv = w[0:1] * _down(u, 2) + w[1:2] * _down(u, 1) + w[2:3] * u
        o_ref[...] = (b_ref[...].astype(F32) * cv).astype(BF16)

    col = lambda off: pl.BlockSpec((L, TW), lambda j: (0, off + j))
    return pl.pallas_call(
        kern, grid=(nb,), in_specs=[col(0), col(nb), col(2 * nb), pl.BlockSpec((8, TW), lambda j: (0, j))],
        out_specs=pl.BlockSpec((L, TW), lambda j: (0, j)), out_shape=jax.ShapeDtypeStruct((L, D), BF16),
        name="sc_fwd", compiler_params=_cp(("parallel",)),
    )(proj, proj, proj, cw)


def _sc_bwd(dya, proj, cw, dproj, h, gwct):
    L = proj.shape[0]
    nb = D // TW

    def kern(d_ref, b_ref, c_ref, x_ref, w_ref, h_ref, _, __, dp_ref, gw_ref, gx_ref, keep):
        sec = pl.program_id(1)

        @pl.when(sec == 0)
        def _():
            cs, xs, dyv = c_ref[...].astype(F32), x_ref[...].astype(F32), d_ref[...]
            w = w_ref[...]
            u = cs * xs
            u1, u2 = _down(u, 1), _down(u, 2)
            cv = w[0:1] * u2 + w[1:2] * u1 + w[2:3] * u
            dcv = dyv * b_ref[...].astype(F32)
            du = w[2:3] * dcv + w[1:2] * _up(dcv, 1) + w[0:1] * _up(dcv, 2)
            g0 = jnp.sum(dcv * u2, axis=0, keepdims=True)
            g1 = jnp.sum(dcv * u1, axis=0, keepdims=True)
            g2 = jnp.sum(dcv * u, axis=0, keepdims=True)
            row = lax.broadcasted_iota(jnp.int32, (8, TW), 0)
            gw_ref[...] = jnp.where(row == 0, g0, jnp.where(row == 1, g1, jnp.where(row == 2, g2, 0.0)))
            dp_ref[...] = (dyv * cv).astype(BF16)
            keep[0] = (du * xs).astype(BF16)
            keep[1] = (du * cs).astype(BF16)

        @pl.when(sec > 0)
        def _():
            dp_ref[...] = keep[sec - 1]

        gx_ref[...] = lax.dot_general(dp_ref[...], h_ref[...], _DIMS["tn"], preferred_element_type=F32).astype(BF16)

    col = lambda off: pl.BlockSpec((L, TW), lambda j, s: (0, off + j))
    return pl.pallas_call(
        kern, grid=(nb, 3),
        in_specs=[col(0), col(0), col(nb), col(2 * nb), pl.BlockSpec((8, TW), lambda j, s: (0, j)),
                  pl.BlockSpec((L, D), lambda j, s: (0, 0)), pl.BlockSpec(memory_space=pl.ANY),
                  pl.BlockSpec(memory_space=pl.ANY)],
        out_specs=(pl.BlockSpec((L, TW), lambda j, s: (0, s * nb + j)), pl.BlockSpec((8, TW), lambda j, s: (0, j)),
                   pl.BlockSpec((TW, D), lambda j, s: (s * nb + j, 0))),
        out_shape=(jax.ShapeDtypeStruct(dproj.shape, BF16), jax.ShapeDtypeStruct((8, D), F32),
                   jax.ShapeDtypeStruct(gwct.shape, BF16)),
        scratch_shapes=[pltpu.VMEM((2, L, TW), BF16)],
        input_output_aliases={6: 0, 7: 2}, name="sc_bwd", compiler_params=_cp(("parallel", "arbitrary")),
    )(dya, proj, proj, proj, cw, h, dproj, gwct)


def _ssm_conv_fwd(proj, cw4):
    L = proj.shape[0]
    off = C_XBC // TW

    def kern(r_ref, w_ref, o_ref):
        raw = r_ref[...].astype(F32)
        w = w_ref[...]
        c4 = w[0:1] * _down(raw, 3) + w[1:2] * _down(raw, 2) + w[2:3] * _down(raw, 1) + w[3:4] * raw + w[4:5]
        o_ref[...] = c4 * _sigmoid(c4)

    return pl.pallas_call(
        kern, grid=(XBC // TW,),
        in_specs=[pl.BlockSpec((L, TW), lambda j: (0, off + j)), pl.BlockSpec((8, TW), lambda j: (0, j))],
        out_specs=pl.BlockSpec((L, TW), lambda j: (0, j)), out_shape=jax.ShapeDtypeStruct((L, XBC), F32),
        name="ssm_conv_fwd", compiler_params=_cp(("parallel",)),
    )(proj, cw4)


def _ssm_conv_bwd(dx, proj, cw4, dproj, col0, name, h, gwct):
    L, width = dx.shape
    off_p = (C_XBC + col0) // TW
    off_w = col0 // TW

    def kern(d_ref, r_ref, w_ref, h_ref, _, __, dp_ref, gw_ref, gx_ref):
        raw = r_ref[...].astype(F32)
        w = w_ref[...]
        r1, r2, r3 = _down(raw, 1), _down(raw, 2), _down(raw, 3)
        c4 = w[0:1] * r3 + w[1:2] * r2 + w[2:3] * r1 + w[3:4] * raw + w[4:5]
        sg = _sigmoid(c4)
        dc4 = d_ref[...] * (sg * (1.0 + c4 * (1.0 - sg)))
        draw = w[3:4] * dc4 + w[2:3] * _up(dc4, 1) + w[1:2] * _up(dc4, 2) + w[0:1] * _up(dc4, 3)
        dp_ref[...] = draw.astype(BF16)
        gx_ref[...] = lax.dot_general(dp_ref[...], h_ref[...], _DIMS["tn"], preferred_element_type=F32).astype(BF16)
        gs = [jnp.sum(dc4 * r3, axis=0, keepdims=True), jnp.sum(dc4 * r2, axis=0, keepdims=True),
              jnp.sum(dc4 * r1, axis=0, keepdims=True), jnp.sum(dc4 * raw, axis=0, keepdims=True),
              jnp.sum(dc4, axis=0, keepdims=True)]
        row = lax.broadcasted_iota(jnp.int32, (8, TW), 0)
        acc = jnp.zeros((8, TW), F32)
        for k, gk in enumerate(gs):
            acc = jnp.where(row == k, gk, acc)
        gw_ref[...] = acc

    return pl.pallas_call(
        kern, grid=(width // TW,),
        in_specs=[pl.BlockSpec((L, TW), lambda j: (0, j)), pl.BlockSpec((L, TW), lambda j: (0, off_p + j)),
                  pl.BlockSpec((8, TW), lambda j: (0, off_w + j)), pl.BlockSpec((L, D), lambda j: (0, 0)),
                  pl.BlockSpec(memory_space=pl.ANY), pl.BlockSpec(memory_space=pl.ANY)],
        out_specs=(pl.BlockSpec((L, TW), lambda j: (0, off_p + j)), pl.BlockSpec((8, TW), lambda j: (0, j)),
                   pl.BlockSpec((TW, D), lambda j: (off_p + j, 0))),
        out_shape=(jax.ShapeDtypeStruct(dproj.shape, BF16), jax.ShapeDtypeStruct((8, width), F32),
                   jax.ShapeDtypeStruct(gwct.shape, BF16)),
        input_output_aliases={4: 0, 5: 2}, name=name, compiler_params=_cp(("arbitrary",)),
    )(dx, proj, cw4, h, dproj, gwct)


def _split3(v):
    h1 = v.astype(BF16)
    r1 = v - h1.astype(F32)
    h2 = r1.astype(BF16)
    h3 = (r1 - h2.astype(F32)).astype(BF16)
    return h1, h2, h3


def _dot01(m01, v, dims=_DIMS["nn"], m_left=True, terms=3):
    out = None
    for part in _split3(v)[:terms]:
        ops = (m01, part) if m_left else (part, m01)
        t = lax.dot_general(ops[0], ops[1], dims, preferred_element_type=F32)
        out = t if out is None else out + t
    return out


def _bdot(a, b, mode="nn"):
    return lax.dot_general(a.astype(BF16), b.astype(BF16), _DIMS[mode], preferred_element_type=F32)


def _softplus(v):
    return jnp.maximum(v, 0.0) + jnp.log1p(jnp.exp(-jnp.abs(v)))


def _dt_prep(proj, vec):
    L = proj.shape[0]

    def kern(p_ref, v_ref, dt_ref, cs_ref, sg_ref):
        v = v_ref[...]
        pre = p_ref[:, 0:128] + v[0:1]
        dt = _softplus(pre)
        da = dt * (-jnp.exp(v[1:2]))
        ii = lax.broadcasted_iota(jnp.int32, (Q, Q), 0)
        jj = lax.broadcasted_iota(jnp.int32, (Q, Q), 1)
        ltri = (jj <= ii).astype(BF16)
        lane = lax.broadcasted_iota(jnp.int32, (Q, 128), 1)
        for val, ref in ((dt, dt_ref), (_dot01(ltri, da), cs_ref), (_sigmoid(pre), sg_ref)):
            for g in range(NG):
                moved = val if g == 0 else pltpu.roll(val, 128 - 4 * g, axis=1)
                ref[g] = jnp.where(lane < 4, moved, 0.0)

    blk = pl.BlockSpec((NG, Q, 128), lambda c: (0, c, 0))
    return pl.pallas_call(
        kern, grid=(L // Q,),
        in_specs=[pl.BlockSpec((Q, 256), lambda c: (c, 0)), pl.BlockSpec((8, 128), lambda c: (0, 0))],
        out_specs=(blk, blk, blk),
        out_shape=(jax.ShapeDtypeStruct((NG, L, 128), F32),) * 3,
        name="dt_prep", compiler_params=_cp(("parallel",)),
    )(proj, vec)


def _head_masks():
    lane = lax.broadcasted_iota(jnp.int32, (1, 4 * HD), 1)
    return [((lane >= HD * j) & (lane < HD * (j + 1))) for j in range(4)]


def _expand4(v4, masks):
    R = v4.shape[0]
    out = jnp.zeros((R, 4 * HD), F32)
    for j in range(4):
        out = jnp.where(masks[j], jnp.broadcast_to(v4[:, j:j + 1], (R, 4 * HD)), out)
    return out


def _decay_matrix(cs_col, tri):
    colb = jnp.broadcast_to(cs_col, (Q, Q))
    return jnp.exp(jnp.where(tri, colb - colb.T, -jnp.inf))


def _ssd_fwd(xbc, dt4, cs4, vecg):
    L = xbc.shape[0]
    nc = L // Q

    def kern(x_ref, b_ref, c_ref, dt_ref, cs_ref, v_ref, y_ref, s_ref, S):
        c = pl.program_id(1)

        @pl.when(c == 0)
        def _():
            S[...] = jnp.zeros_like(S)

        masks = _head_masks()
        ii = lax.broadcasted_iota(jnp.int32, (Q, Q), 0)
        jj = lax.broadcasted_iota(jnp.int32, (Q, Q), 1)
        tri = jj <= ii
        for gi in range(GPS):
            xs, ns = slice(256 * gi, 256 * (gi + 1)), slice(NS * gi, NS * (gi + 1))
            dt4v, cs4v = dt_ref[gi], cs_ref[gi]
            dt_b, cs_b = _expand4(dt4v, masks), _expand4(cs4v, masks)
            d_b = _expand4(v_ref[gi], masks)[1:2]
            cs_last = cs_b[Q - 1:Q, :]
            x4, bm, cm = x_ref[:, xs], b_ref[:, ns], c_ref[:, ns]
            xdt = x4 * dt_b
            gm = _bdot(cm, bm, "nt")
            s4 = S[gi]
            s_ref[gi, 0] = s4
            y = _bdot(cm, s4) * jnp.exp(cs_b) + d_b * x4
            m_all = jnp.concatenate([(gm * _decay_matrix(cs4v[:, j:j + 1], tri)).astype(BF16) for j in range(4)], axis=0)
            yd = _bdot(m_all, xdt)
            for j in range(4):
                y = y + jnp.where(masks[j], yd[Q * j:Q * (j + 1)], 0.0)
            y_ref[:, xs] = y
            S[gi] = jnp.exp(cs_last) * s4 + _bdot(bm, xdt * jnp.exp(cs_last - cs_b), "tn")

    sc = pl.BlockSpec((GPS, Q, 128), lambda g, c: (g, c, 0))
    bw = NS * GPS
    return pl.pallas_call(
        kern, grid=(NG // GPS, nc),
        in_specs=[pl.BlockSpec((Q, 256 * GPS), lambda g, c: (c, g)),
                  pl.BlockSpec((Q, bw), lambda g, c: (c, INNER // bw + g)),
                  pl.BlockSpec((Q, bw), lambda g, c: (c, (INNER + NG * NS) // bw + g)),
                  sc, sc, pl.BlockSpec((GPS, 8, 128), lambda g, c: (g, 0, 0))],
        out_specs=(pl.BlockSpec((Q, 256 * GPS), lambda g, c: (c, g)),
                   pl.BlockSpec((GPS, 1, NS, 256), lambda g, c: (g, c, 0, 0))),
        out_shape=(jax.ShapeDtypeStruct((L, INNER), F32), jax.ShapeDtypeStruct((NG, nc, NS, 256), F32)),
        scratch_shapes=[pltpu.VMEM((GPS, NS, 256), F32)], name="ssd_fwd",
        compiler_params=_cp(("parallel", "arbitrary")),
    )(xbc, xbc, xbc, dt4, cs4, vecg)


def _ssd_bwd(xbc, dt4, cs4, sg4, vecg, s_all, dy):
    L = xbc.shape[0]
    nc = L // Q

    def kern(x_ref, b_ref, c_ref, dt_ref, cs_ref, sg_ref, v_ref, s_ref, dy_ref,
             dx_ref, db_ref, dc_ref, ddt_ref, st_ref, dS):
        cc = pl.program_id(1)

        @pl.when(cc == 0)
        def _():
            dS[...] = jnp.zeros_like(dS)
            st_ref[...] = jnp.zeros_like(st_ref)

        masks = _head_masks()
        ii = lax.broadcasted_iota(jnp.int32, (Q, Q), 0)
        jj = lax.broadcasted_iota(jnp.int32, (Q, Q), 1)
        tri = jj <= ii
        utri = (jj >= ii).astype(BF16)
        hsel = ((lax.broadcasted_iota(jnp.int32, (4 * HD, 128), 0) // HD)
                == lax.broadcasted_iota(jnp.int32, (4 * HD, 128), 1)).astype(BF16)
        hrow = ((lax.broadcasted_iota(jnp.int32, (4 * Q, 128), 0) // Q)
                == lax.broadcasted_iota(jnp.int32, (4 * Q, 128), 1)).astype(BF16)
        ones_q = jnp.ones((Q, 128), BF16)
        lane128 = lax.broadcasted_iota(jnp.int32, (Q, 128), 1)

        for gi in range(GPS):
            xs, ns = slice(256 * gi, 256 * (gi + 1)), slice(NS * gi, NS * (gi + 1))
            dt4v, cs4v, sg4v = dt_ref[gi], cs_ref[gi], sg_ref[gi]
            dt_b, cs_b = _expand4(dt4v, masks), _expand4(cs4v, masks)
            vv = _expand4(v_ref[gi], masks)
            a_b = -jnp.exp(vv[0:1])
            d_b = vv[1:2]
            a4 = -jnp.exp(v_ref[gi][0:1, :])
            cs_last = cs_b[Q - 1:Q, :]
            ecs = jnp.exp(cs_b)
            decay = jnp.exp(cs_last - cs_b)
            elast = jnp.exp(cs_last)
            x4, bm, cm, dyv = x_ref[:, xs], b_ref[:, ns], c_ref[:, ns], dy_ref[:, xs]
            s4 = s_ref[gi, 0]
            dsn = dS[gi]
            xdt = x4 * dt_b
            gm = _bdot(cm, bm, "nt")
            dye = dyv * ecs
            yoff = ecs * _bdot(cm, s4)
            t4 = _bdot(bm, dsn) * decay
            lms, mhs = [], []
            for j in range(4):
                colb = jnp.broadcast_to(cs4v[:, j:j + 1], (Q, Q))
                lms.append(jnp.exp(jnp.where(tri, colb - colb.T, -jnp.inf)))
                mhs.append(gm * lms[j])
            m_all = jnp.concatenate([m.astype(BF16) for m in mhs], axis=0)
            dy_m = jnp.concatenate([jnp.where(masks[j], dyv, 0.0).astype(BF16) for j in range(4)], axis=0)
            dxdt = t4 + _bdot(m_all, dy_m, "tn")
            dm_all = _bdot(dy_m, xdt, "nt")
            dg = jnp.zeros((Q, Q), F32)
            for j in range(4):
                dg = dg + dm_all[Q * j:Q * (j + 1)] * lms[j]
            e_all = dm_all * jnp.concatenate(mhs, axis=0)
            rsum = _dot01(ones_q, e_all, m_left=False, terms=2)
            da4 = -_dot01(hrow, e_all, _DIMS["tn"], m_left=False, terms=2)
            for j in range(4):
                da4 = da4 + jnp.where(lane128 == j, rsum[Q * j:Q * (j + 1)], 0.0)
            xt = xdt * t4
            tail = jnp.sum(xt, axis=0, keepdims=True) + elast * jnp.sum(s4 * dsn, axis=0, keepdims=True)
            gd_raw = jnp.sum(dyv * x4, axis=0, keepdims=True)
            stacked = jnp.concatenate([dyv * yoff - xt, dxdt * x4, jnp.broadcast_to(tail, (8, 4 * HD)),
                                       jnp.broadcast_to(gd_raw, (8, 4 * HD))], axis=0)
            seg = _dot01(hsel, stacked, m_left=False, terms=2)
            dda4 = _dot01(utri, da4 + seg[0:Q], terms=2) + seg[2 * Q:2 * Q + 1]
            ddt_ref[gi] = (dda4 * a4 + seg[Q:2 * Q]) * sg4v
            ga = jnp.sum(dda4 * dt4v * a4, axis=0, keepdims=True)
            row = lax.broadcasted_iota(jnp.int32, (8, 128), 0)
            st_ref[gi] += jnp.where(row == 0, ga, jnp.where(row == 1, seg[2 * Q + 8:2 * Q + 9], 0.0))
            dx_ref[:, xs] = d_b * dyv + dxdt * dt_b
            dc_ref[:, ns] = _bdot(dg, bm) + _bdot(dye, s4, "nt")
            db_ref[:, ns] = _bdot(dg, cm, "tn") + _bdot(xdt * decay, dsn, "nt")
            dS[gi] = elast * dsn + _bdot(cm, dye, "tn")

    rv = lambda c: nc - 1 - c
    sc = pl.BlockSpec((GPS, Q, 128), lambda g, c: (g, rv(c), 0))
    bw = NS * GPS
    return pl.pallas_call(
        kern, grid=(NG // GPS, nc),
        in_specs=[pl.BlockSpec((Q, 256 * GPS), lambda g, c: (rv(c), g)),
                  pl.BlockSpec((Q, bw), lambda g, c: (rv(c), INNER // bw + g)),
                  pl.BlockSpec((Q, bw), lambda g, c: (rv(c), (INNER + NG * NS) // bw + g)),
                  sc, sc, sc, pl.BlockSpec((GPS, 8, 128), lambda g, c: (g, 0, 0)),
                  pl.BlockSpec((GPS, 1, NS, 256), lambda g, c: (g, rv(c), 0, 0)),
                  pl.BlockSpec((Q, 256 * GPS), lambda g, c: (rv(c), g))],
        out_specs=(pl.BlockSpec((Q, 256 * GPS), lambda g, c: (rv(c), g)),
                   pl.BlockSpec((Q, bw), lambda g, c: (rv(c), g)),
                   pl.BlockSpec((Q, bw), lambda g, c: (rv(c), g)),
                   pl.BlockSpec((GPS, Q, 128), lambda g, c: (g, rv(c), 0)),
                   pl.BlockSpec((GPS, 8, 128), lambda g, c: (g, 0, 0))),
        out_shape=(jax.ShapeDtypeStruct((L, INNER), F32), jax.ShapeDtypeStruct((L, NG * NS), F32),
                   jax.ShapeDtypeStruct((L, NG * NS), F32), jax.ShapeDtypeStruct((NG, L, 128), F32),
                   jax.ShapeDtypeStruct((NG, 8, 128), F32)),
        scratch_shapes=[pltpu.VMEM((GPS, NS, 256), F32)], name="ssd_bwd",
        compiler_params=_cp(("parallel", "arbitrary")),
    )(xbc, xbc, xbc, dt4, cs4, sg4, vecg, s_all, dy)


def _dt_bwd(ddt, dproj, tl=256):
    L = ddt.shape[1]

    def kern(d_ref, _, dp_ref, gs_ref):
        @pl.when(pl.program_id(0) == 0)
        def _():
            gs_ref[...] = jnp.zeros_like(gs_ref)

        d = d_ref[0]
        for g in range(1, NG):
            d = d + pltpu.roll(d_ref[g], 4 * g, axis=1)
        gs_ref[...] += jnp.broadcast_to(jnp.sum(d, axis=0, keepdims=True), (8, 128))
        dp_ref[...] = jnp.concatenate([d, jnp.zeros_like(d)], axis=1).astype(BF16)

    return pl.pallas_call(
        kern, grid=(L // tl,),
        in_specs=[pl.BlockSpec((NG, tl, 128), lambda i: (0, i, 0)), pl.BlockSpec(memory_space=pl.ANY)],
        out_specs=(pl.BlockSpec((tl, 256), lambda i: (i, C_DT // 256)), pl.BlockSpec((8, 128), lambda i: (0, 0))),
        out_shape=(jax.ShapeDtypeStruct(dproj.shape, BF16), jax.ShapeDtypeStruct((8, 128), F32)),
        input_output_aliases={1: 0}, name="dt_bwd", compiler_params=_cp(("arbitrary",)),
    )(ddt, dproj)


GW = INNER // NG


def _gnorm_fwd(y, proj, w, tl=256):
    L = y.shape[0]
    zoff = C_Z // 1024

    def kern(y_ref, z_ref, w_ref, o_ref):
        z = z_ref[...].astype(F32)
        yz = y_ref[...] * (z * _sigmoid(z))
        wv = w_ref[...]
        for k in range(1024 // GW):
            sl = slice(GW * k, GW * (k + 1))
            v = yz[:, sl]
            rg = lax.rsqrt(jnp.mean(v * v, axis=-1, keepdims=True) + EPS)
            o_ref[:, sl] = ((v * rg) * wv[:, sl]).astype(BF16)

    blk = pl.BlockSpec((tl, 1024), lambda i, j: (i, j))
    return pl.pallas_call(
        kern, grid=(L // tl, 2),
        in_specs=[blk, pl.BlockSpec((tl, 1024), lambda i, j: (i, zoff + j)), pl.BlockSpec((1, 1024), lambda i, j: (0, j))],
        out_specs=blk, out_shape=jax.ShapeDtypeStruct((L, INNER), BF16), name="gnorm_fwd",
        compiler_params=_cp(("parallel", "parallel")),
    )(y, proj, w.reshape(1, INNER))


def _gnorm_bwd(dbr, wb, y, proj, w, dproj, dep, tl=512):
    L = y.shape[0]
    tl = min(tl, L)
    zoff = C_Z // 1024

    def kern(d_ref, b_ref, y_ref, z_ref, w_ref, _, __, dy_ref, dp_ref, gw_ref):
        @pl.when(pl.program_id(1) == 0)
        def _():
            gw_ref[...] = jnp.zeros_like(gw_ref)

        z = z_ref[...].astype(F32)
        sg = _sigmoid(z)
        sz = z * sg
        yv = y_ref[...]
        yz = yv * sz
        dv = lax.dot_general(d_ref[0], b_ref[...], _DIMS["nt"], preferred_element_type=F32)
        wv = w_ref[...]
        for k in range(1024 // GW):
            sl = slice(GW * k, GW * (k + 1))
            v = yz[:, sl]
            rg = lax.rsqrt(jnp.mean(v * v, axis=-1, keepdims=True) + EPS)
            vn = v * rg
            dk = dv[:, sl]
            gw_ref[:, sl] += jnp.broadcast_to(jnp.sum(dk * vn, axis=0, keepdims=True), (8, GW))
            dvn = dk * wv[:, sl]
            dyz = rg * (dvn - vn * jnp.mean(dvn * vn, axis=-1, keepdims=True))
            dy_ref[:, sl] = dyz * sz[:, sl]
            dp_ref[:, sl] = (dyz * yv[:, sl] * (sg[:, sl] * (1.0 + z[:, sl] * (1.0 - sg[:, sl])))).astype(BF16)

    blk = pl.BlockSpec((tl, 1024), lambda j, i: (i, j))
    zblk = pl.BlockSpec((tl, 1024), lambda j, i: (i, zoff + j))
    return pl.pallas_call(
        kern, grid=(2, L // tl),
        in_specs=[pl.BlockSpec((1, tl, D), lambda j, i: (1, i, 0)), pl.BlockSpec((1024, D), lambda j, i: (j, 0)),
                  blk, zblk, pl.BlockSpec((1, 1024), lambda j, i: (0, j)), pl.BlockSpec(memory_space=pl.ANY),
                  pl.BlockSpec(memory_space=pl.ANY)],
        out_specs=(blk, zblk, pl.BlockSpec((8, 1024), lambda j, i: (0, j))),
        out_shape=(jax.ShapeDtypeStruct((L, INNER), F32), jax.ShapeDtypeStruct(dproj.shape, BF16),
                   jax.ShapeDtypeStruct((8, INNER), F32)),
        input_output_aliases={5: 1}, name="gnorm_bwd", compiler_params=_cp(("parallel", "arbitrary")),
    )(dbr, wb, y, proj, w.reshape(1, INNER), dproj, dep)


def _merge_fwd(proj, bg, br_a, br_b, tl=256):
    L = proj.shape[0]
    goff = C_GATE // 1024

    def kern(g1_ref, g2_ref, b1_ref, b2_ref, a_ref, b_ref, o_ref):
        g1 = _sigmoid(g1_ref[...].astype(F32) + b1_ref[...])
        g2 = _sigmoid(g2_ref[...].astype(F32) + b2_ref[...])
        o_ref[...] = (g1 * a_ref[...] + g2 * b_ref[...]).astype(BF16)

    row = pl.BlockSpec((tl, 1024), lambda i: (i, 0))
    bg2 = bg.reshape(1, 2 * D)
    return pl.pallas_call(
        kern, grid=(L // tl,),
        in_specs=[pl.BlockSpec((tl, 1024), lambda i: (i, goff)), pl.BlockSpec((tl, 1024), lambda i: (i, goff + 1)),
                  pl.BlockSpec((1, 1024), lambda i: (0, 0)), pl.BlockSpec((1, 1024), lambda i: (0, 1)), row, row],
        out_specs=row, out_shape=jax.ShapeDtypeStruct((L, D), BF16), name="merge_fwd",
        compiler_params=_cp(("parallel",)),
    )(proj, proj, bg2, bg2, br_a, br_b)


def _branch_ssm_merge(yb, wb, proj, bg, br_a, tm=512):
    L, K = yb.shape
    tm = min(tm, L)
    goff = C_GATE // 1024

    def kern(a_ref, b_ref, g1_ref, g2_ref, b1_ref, b2_ref, bra_ref, brb_ref, m_ref):
        brb = lax.dot_general(a_ref[...], b_ref[...], _DIMS["nn"], preferred_element_type=F32)
        brb_ref[...] = brb
        g1 = _sigmoid(g1_ref[...].astype(F32) + b1_ref[...])
        g2 = _sigmoid(g2_ref[...].astype(F32) + b2_ref[...])
        m_ref[...] = (g1 * bra_ref[...] + g2 * brb).astype(BF16)

    row = pl.BlockSpec((tm, D), lambda i: (i, 0))
    bg2 = bg.reshape(1, 2 * D)
    return pl.pallas_call(
        kern, grid=(L // tm,),
        in_specs=[pl.BlockSpec((tm, K), lambda i: (i, 0)), pl.BlockSpec((K, D), lambda i: (0, 0)),
                  pl.BlockSpec((tm, D), lambda i: (i, goff)), pl.BlockSpec((tm, D), lambda i: (i, goff + 1)),
                  pl.BlockSpec((1, D), lambda i: (0, 0)), pl.BlockSpec((1, D), lambda i: (0, 1)), row],
        out_specs=(row, row), out_shape=(jax.ShapeDtypeStruct((L, D), F32), jax.ShapeDtypeStruct((L, D), BF16)),
        name="branch_ssm_merge", compiler_params=_cp(("parallel",)),
    )(yb, wb, proj, proj, bg2, bg2, br_a)


def _merge_bwd(dx1, wo, proj, bg, br_a, br_b, dproj, tl=512):
    L = proj.shape[0]
    tl = min(tl, L)
    goff = C_GATE // 1024

    def kern(dm_ref, wo_ref, g_ref, b_ref, a_ref, bb_ref, _, dbr_ref, dp_ref, gb_ref):
        j = pl.program_id(0)

        @pl.when(pl.program_id(1) == 0)
        def _():
            gb_ref[...] = jnp.zeros_like(gb_ref)

        g = _sigmoid(g_ref[...].astype(F32) + b_ref[...])
        br = jnp.where(j == 0, a_ref[...], bb_ref[...])
        dmv = lax.dot_general(dm_ref[...].astype(BF16), wo_ref[...], _DIMS["nt"], preferred_element_type=F32)
        dbr_ref[0] = (dmv * g).astype(BF16)
        dgate = dmv * br * g * (1.0 - g)
        gb_ref[...] += jnp.broadcast_to(jnp.sum(dgate, axis=0, keepdims=True), (8, 1024))
        dp_ref[...] = dgate.astype(BF16)

    row = pl.BlockSpec((tl, 1024), lambda j, i: (i, 0))
    gblk = pl.BlockSpec((tl, 1024), lambda j, i: (i, goff + j))
    return pl.pallas_call(
        kern, grid=(2, L // tl),
        in_specs=[row, pl.BlockSpec((D, D), lambda j, i: (0, 0)), gblk, pl.BlockSpec((1, 1024), lambda j, i: (0, j)),
                  row, row, pl.BlockSpec(memory_space=pl.ANY)],
        out_specs=(pl.BlockSpec((1, tl, 1024), lambda j, i: (j, i, 0)), gblk, pl.BlockSpec((8, 1024), lambda j, i: (0, j))),
        out_shape=(jax.ShapeDtypeStruct((2, L, D), BF16), jax.ShapeDtypeStruct(dproj.shape, BF16),
                   jax.ShapeDtypeStruct((8, 2 * D), F32)),
        input_output_aliases={6: 1}, name="merge_bwd", compiler_params=_cp(("parallel", "arbitrary")),
    )(dx1, wo, proj, bg.reshape(1, 2 * D), br_a, br_b, dproj)


def _coords():
    return lax.axis_index("x"), lax.axis_index("y"), lax.axis_index("c")


def _other_chips(sk):
    xk, yk = sk // 2, sk % 2
    return [((1 - xk, yk), 2 * (1 - xk) + yk), ((xk, 1 - yk), 2 * xk + 1 - yk), ((1 - xk, 1 - yk), 2 * (1 - xk) + 1 - yk)]


def _rows(start, size):
    assert size % 128 == 0
    return pl.ds(pl.multiple_of(start, 128), size)


def _per_chip(fn):
    x, y, _ = _coords()
    s = 2 * x + y
    for sk in range(4):
        pl.when(s == sk)(functools.partial(fn, sk))


XTRA = PIECE - PMAIN


def _place(shard, full_shape, block, index_map, idx, name, blk0=0, nblk=None, dep=None, into=None):
    in_block = block[-2:]
    if nblk is None:
        nblk = shard.shape[0] // in_block[0]

    def kern(idx_ref, s_ref, *rest):
        o_ref = rest[-1]
        o_ref[...] = s_ref[...].astype(BF16).reshape(o_ref.shape)

    extra = ([dep] if dep is not None else []) + ([into] if into is not None else [])
    grid_spec = pltpu.PrefetchScalarGridSpec(
        num_scalar_prefetch=1, grid=(nblk,),
        in_specs=[pl.BlockSpec(in_block, lambda i, idx_ref: (blk0 + i, 0))] + [_ANY] * len(extra),
        out_specs=pl.BlockSpec(block, index_map))
    aliases = {1 + len(extra): 0} if into is not None else {}
    return pl.pallas_call(kern, grid_spec=grid_spec, out_shape=jax.ShapeDtypeStruct(full_shape, BF16), name=name,
                          input_output_aliases=aliases, compiler_params=_cp(("arbitrary",)))(idx, shard, *extra)


_SEM = pl.BlockSpec(memory_space=pltpu.SEMAPHORE)
_EFFECT = pltpu.SideEffectType.DATAFLOW_SIDE_EFFECTING


_ANY = pl.BlockSpec(memory_space=pl.ANY)


def _tie(v, dep, name):
    def body(v_ref, dep_ref, o_ref):
        del v_ref, dep_ref, o_ref

    return pl.pallas_call(body, out_shape=jax.ShapeDtypeStruct(v.shape, v.dtype), in_specs=[_ANY, _ANY],
                          out_specs=_ANY, input_output_aliases={0: 0}, name=name)(v, dep)


def _split_call(name, arrays, start=None, wait=None, wait_sems=None, after=None):
    keys = list(arrays)
    n = len(keys)
    n_start = start.n if start is not None else 0
    afters = [] if after is None else (list(after) if isinstance(after, (list, tuple)) else [after])

    def body(*refs):
        pos = n
        if wait is not None:
            wss, wrs = refs[pos], refs[pos + 1]
            pos += 2
        pos += len(afters)
        if start is not None:
            nss, nrs = refs[pos], refs[pos + 1]
            pos += 2
        R = dict(zip(keys, refs[pos:pos + n]))
        token = refs[pos + n]
        x, y, c = _coords()

        def desc(src, dst, dev, ss, rs, k):
            return pltpu.make_async_remote_copy(src_ref=src, dst_ref=dst, send_sem=ss.at[k], recv_sem=rs.at[k],
                                                device_id=dev, device_id_type=MESH)

        def run(sk):
            if wait is not None:
                for k, (snd, land) in enumerate(wait.copies(sk, R)):
                    if snd is not None:
                        desc(snd[0], snd[1], snd[2], wss, wrs, k).wait_send()
                    if land is not None:
                        desc(land, land, (x, y, c), wss, wrs, k).wait_recv()
            if start is not None:
                for k, (snd, land) in enumerate(start.copies(sk, R)):
                    if snd is not None:
                        desc(snd[0], snd[1], snd[2], nss, nrs, k).start()

        _per_chip(run)
        token[...] = jnp.zeros_like(token)

    hbm = pl.BlockSpec(memory_space=HBM)
    vals = [arrays[k] for k in keys]
    ins, in_specs = list(vals), [hbm] * n
    if wait is not None:
        ins += list(wait_sems)
        in_specs += [_SEM, _SEM]
    ins += afters
    in_specs += [pl.BlockSpec(memory_space=pl.ANY)] * len(afters)
    out_shape, out_specs = [], []
    if start is not None:
        out_shape += [pltpu.SemaphoreType.DMA((n_start,)), pltpu.SemaphoreType.DMA((n_start,))]
        out_specs += [_SEM, _SEM]
    first = len(out_shape)
    out_shape += [jax.ShapeDtypeStruct(v.shape, v.dtype) for v in vals] + [jax.ShapeDtypeStruct((8, 128), F32)]
    out_specs += [hbm] * n + [pl.BlockSpec(memory_space=pltpu.VMEM)]
    res = pl.pallas_call(
        body, out_shape=tuple(out_shape), in_specs=in_specs, out_specs=tuple(out_specs),
        input_output_aliases={i: first + i for i in range(n)}, name=name,
        compiler_params=pltpu.CompilerParams(has_side_effects=_EFFECT),
    )(*ins)
    sems = (res[0], res[1]) if start is not None else None
    return dict(zip(keys, res[first:first + n])), sems, res[-1]


class _Plan:
    def __init__(self, n, copies):
        self.n, self.copies = n, copies


_HM, _HX = PMAIN // 2, XTRA // 2
WAVE0 = 768
WAVES = ((0, WAVE0), (WAVE0, _HM - WAVE0))
_WIN = {
    "wq0": (True, "wct", lambda r, sc, hc: r.at[_rows(PMAIN * sc + _HM * hc + WAVES[0][0], WAVES[0][1]), :]),
    "wq1": (True, "wct", lambda r, sc, hc: r.at[_rows(PMAIN * sc + _HM * hc + WAVES[1][0], WAVES[1][1]), :]),
    "xt": (True, "xt", lambda r, sc, hc: r.at[sc, _rows(_HX * hc, _HX), :]),
    "w1": (True, "w1", lambda r, sc, hc: r.at[_rows(512 * hc, 512), pl.ds(1024 * sc, 1024)]),
    "w2": (True, "w2", lambda r, sc, hc: r.at[_rows(1024 * sc + 512 * hc, 512), :]),
    "wa": (True, "wa", lambda r, sc, hc: r.at[_rows(256 * sc + 128 * hc, 128), :]),
    "wb": (True, "wb", lambda r, sc, hc: r.at[_rows(512 * sc + 256 * hc, 256), :]),
    "wo": (True, "wo", lambda r, sc, hc: r.at[_rows(256 * sc + 128 * hc, 128), :]),
    "cw": (False, "cw", lambda r, sc, hc: r.at[sc]),
}


_PIECE_SRC = {
    "wq0": lambda p, hc: p.at[_rows(_HM * hc + WAVES[0][0], WAVES[0][1]), :],
    "wq1": lambda p, hc: p.at[_rows(_HM * hc + WAVES[1][0], WAVES[1][1]), :],
    "xt": lambda p, hc: p.at[_rows(PMAIN + _HX * hc, _HX), :],
}


def _ag_chips_plan(keys):
    def copies(sk, R):
        _, _, c = _coords()
        out = []
        for key in keys:
            _, arr, win = _WIN[key]
            for (px, py), ps in _other_chips(sk):
                dst = win(R[arr], sk, c)
                src = _PIECE_SRC[key](R["piece"], c) if key in _PIECE_SRC else dst
                out.append(((src, dst, (px, py, c)), win(R[arr], ps, c)))
        return out
    return _Plan(3 * len(keys), copies)


def _ag_sibling_plan(keys):
    keys = [k for k in keys if _WIN[k][0]]

    def copies(sk, R):
        x, y, c = _coords()
        out = []
        for key in keys:
            _, arr, win = _WIN[key]
            for _, ps in _other_chips(sk):
                w = win(R[arr], ps, c)
                out.append(((w, w, (x, y, 1 - c)), win(R[arr], ps, 1 - c)))
        return out
    return _Plan(3 * len(keys), copies)


def _in_proj_wave(h, wct, wave, proj=None, tm=2048):
    L = h.shape[0]
    tm = min(tm, L)
    off, size = WAVES[wave]
    start = lambda j: pl.multiple_of(_HM * j + off, 128)

    def kern(h_ref, w_ref, *rest):
        o_ref = rest[-1]
        o_ref[...] = lax.dot_general(h_ref[...], w_ref[...], _DIMS["nt"], preferred_element_type=F32).astype(BF16)

    in_specs = [pl.BlockSpec((tm, D), lambda j, i: (i, 0)),
                pl.BlockSpec((pl.Element(size), pl.Element(D)), lambda j, i: (start(j), 0))]
    args, aliases = [h, wct], {}
    if proj is not None:
        in_specs.append(pl.BlockSpec(memory_space=pl.ANY))
        args.append(proj)
        aliases = {2: 0}
    return pl.pallas_call(
        kern, grid=(8, L // tm), in_specs=in_specs,
        out_specs=pl.BlockSpec((pl.Element(tm), pl.Element(size)), lambda j, i: (i * tm, start(j))),
        out_shape=jax.ShapeDtypeStruct((L, NCW), BF16), input_output_aliases=aliases,
        name="in_proj_wave%d" % wave, compiler_params=_cp(("parallel", "parallel")),
    )(*args)


def _fix_wct(wct, xt):
    nb = PMAIN // XTRA

    def kern(w_ref, x_ref, o_ref):
        k = pl.program_id(0)
        xv = x_ref[0]
        o_ref[...] = jnp.where(k < 3, (w_ref[...].astype(F32) + xv.astype(F32)).astype(BF16), xv)

    blk = pl.BlockSpec((XTRA, D), lambda k: (nb * (k + 1), 0))
    rblk = pl.BlockSpec((XTRA, D), lambda k: (jnp.where(k < 3, nb * (k + 1), 0), 0))
    return pl.pallas_call(
        kern, grid=(4,), in_specs=[rblk, pl.BlockSpec((1, XTRA, D), lambda k: (k, 0, 0))], out_specs=blk,
        out_shape=jax.ShapeDtypeStruct(wct.shape, BF16), input_output_aliases={0: 0}, name="fix_wct",
        compiler_params=_cp(("arbitrary",)),
    )(wct, xt)


_HP = PIECE // 2
_GWIN = [
    lambda r, sc, hc: r.at[_rows(PMAIN * sc + _HP * hc, _HP), :],
    lambda r, sc, hc: r.at[_rows(512 * hc, 512), pl.ds(1024 * sc, 1024)],
    lambda r, sc, hc: r.at[_rows(1024 * sc + 512 * hc, 512), :],
    lambda r, sc, hc: r.at[_rows(256 * sc + 128 * hc, 128), :],
    lambda r, sc, hc: r.at[_rows(512 * sc + 256 * hc, 256), :],
    lambda r, sc, hc: r.at[_rows(256 * sc + 128 * hc, 128), :],
]
HALF_SHAPES = [(PIECE // 2, D), (512, 1024), (512, 1024), (128, 1024), (256, 1024), (128, 1024)]


def _rs_sibling_plan(ts):
    def copies(sk, R):
        x, y, c = _coords()
        out = []
        for t in ts:
            for sc in range(4):
                land = R["ra%d" % t].at[sc]
                out.append(((_GWIN[t](R["g%d" % t], sc, 1 - c), land, (x, y, 1 - c)), land))
        return out
    return _Plan(4 * len(ts), copies)


def _rs_chips_plan(ts):
    def copies(sk, R):
        _, _, c = _coords()
        out = []
        for t in ts:
            for j, ((px, py), ps) in enumerate(_other_chips(sk)):
                land = R["rb%d" % t].at[j]
                out.append(((R["hb%d" % t].at[ps], land, (px, py, c)), land))
        return out
    return _Plan(3 * len(ts), copies)


def _rs_share_plan(ts):
    def copies(sk, R):
        x, y, c = _coords()
        out = []
        for t in ts:
            rows = HALF_SHAPES[t][0]
            mine = R["f%d" % t].at[_rows(rows * c, rows), :]
            out.append(((mine, mine, (x, y, 1 - c)), R["f%d" % t].at[_rows(rows * (1 - c), rows), :]))
        return out
    return _Plan(len(ts), copies)


def _half_tiling(t):
    rows, cols = HALF_SHAPES[t]
    if t == 0:
        return (rows // 2, cols), 2, lambda i: (i, 0)
    return (rows, cols), 1, lambda i: (0, 0)


def _window_spec(t, blk):
    if t == 0:
        return pl.BlockSpec((pl.Element(blk[0]), pl.Element(blk[1])), lambda i, sc, idx_ref: (
            pl.multiple_of(PMAIN * sc + _HP * idx_ref[1] + blk[0] * i, 128), 0))
    if t == 1:
        return pl.BlockSpec(blk, lambda i, sc, idx_ref: (idx_ref[1], sc))
    return pl.BlockSpec(blk, lambda i, sc, idx_ref: (2 * sc + idx_ref[1], 0))


def _chip_sum(g, ra, t, idx, name):
    rows, cols = HALF_SHAPES[t]
    blk, nblk, inner = _half_tiling(t)

    def kern(idx_ref, g_ref, r_ref, hb_ref, hf_ref):
        v = g_ref[...].astype(F32) + r_ref[0].astype(F32)
        hb_ref[0] = v.astype(BF16)

        @pl.when(pl.program_id(1) == idx_ref[0])
        def _():
            hf_ref[...] = v

    omap = lambda i, sc, idx_ref: (sc,) + inner(i)
    grid_spec = pltpu.PrefetchScalarGridSpec(
        num_scalar_prefetch=1, grid=(nblk, 4),
        in_specs=[_window_spec(t, blk), pl.BlockSpec((1,) + blk, omap)],
        out_specs=(pl.BlockSpec((1,) + blk, omap), pl.BlockSpec(blk, lambda i, sc, idx_ref: inner(i))))
    return pl.pallas_call(
        kern, grid_spec=grid_spec,
        out_shape=(jax.ShapeDtypeStruct((4, rows, cols), BF16), jax.ShapeDtypeStruct((rows, cols), F32)),
        name=name, compiler_params=_cp(("parallel", "arbitrary")),
    )(idx, g, ra)


def _final_sum(hf, rb, t, idx, name):
    rows, cols = HALF_SHAPES[t]
    blk, nblk, inner = _half_tiling(t)
    nbr = rows // blk[0]

    def kern(idx_ref, h_ref, r_ref, o_ref):
        o_ref[...] = ((h_ref[...] + r_ref[0].astype(F32)) + r_ref[1].astype(F32)) + r_ref[2].astype(F32)

    def omap(i, idx_ref):
        r, cidx = inner(i)
        return nbr * idx_ref[1] + r, cidx

    grid_spec = pltpu.PrefetchScalarGridSpec(
        num_scalar_prefetch=1, grid=(nblk,),
        in_specs=[pl.BlockSpec(blk, lambda i, idx_ref: inner(i)),
                  pl.BlockSpec((3,) + blk, lambda i, idx_ref: (0,) + inner(i))],
        out_specs=pl.BlockSpec(blk, omap))
    return pl.pallas_call(
        kern, grid_spec=grid_spec, out_shape=jax.ShapeDtypeStruct((2 * rows, cols), F32),
        name=name, compiler_params=_cp(("parallel",)),
    )(idx, hf, rb)


class _ReduceScatter:
    def __init__(self, ts, grads, idx, tag):
        self.ts, self.idx, self.tag = ts, idx, tag
        arr = {}
        for t in ts:
            arr["g%d" % t] = grads[t]
            arr["ra%d" % t] = lax.empty((4,) + HALF_SHAPES[t], BF16)
        self.plan = _rs_sibling_plan(ts)
        self.arr, self.sems, self.token = _split_call("rs_sibling_start_" + tag, arr, start=self.plan)

    def chips(self, after):
        arr, _, _ = _split_call("rs_sibling_wait_" + self.tag, self.arr, wait=self.plan, wait_sems=self.sems, after=after)
        brr, self.hf = {}, {}
        for t in self.ts:
            hb, self.hf[t] = _chip_sum(arr["g%d" % t], arr["ra%d" % t], t, self.idx, "chip_sum_%d" % t)
            brr["hb%d" % t] = hb
            brr["rb%d" % t] = lax.empty((3,) + HALF_SHAPES[t], BF16)
        self.plan = _rs_chips_plan(self.ts)
        self.arr, self.sems, self.token = _split_call("rs_chips_start_" + self.tag, brr, start=self.plan)
        return self.token

    def share(self, after):
        brr, _, _ = _split_call("rs_chips_wait_" + self.tag, self.arr, wait=self.plan, wait_sems=self.sems, after=after)
        frr = {"f%d" % t: _final_sum(self.hf[t], brr["rb%d" % t], t, self.idx, "final_sum_%d" % t) for t in self.ts}
        self.plan = _rs_share_plan(self.ts)
        self.arr, self.sems, self.token = _split_call("rs_share_start_" + self.tag, frr, start=self.plan)
        return self.token

    def result(self, after):
        frr, _, _ = _split_call("rs_share_wait_" + self.tag, self.arr, wait=self.plan, wait_sems=self.sems, after=after)
        return {t: frr["f%d" % t] for t in self.ts}


def _all8_plan(key):
    def copies(sk, R):
        x, y, c = _coords()
        own = R[key].at[4 * x + 2 * y + c]
        out = []
        for k in range(1, 8):
            dev = ((1 - x) if (k >> 2) & 1 else x, (1 - y) if (k >> 1) & 1 else y, (1 - c) if k & 1 else c)
            out.append(((own, own, dev), R[key].at[4 * dev[0] + 2 * dev[1] + dev[2]]))
        return out
    return _Plan(7, copies)


def _sum8(v, name="small_sum"):
    def kern(v_ref, o_ref):
        acc = v_ref[0]
        for k in range(1, 8):
            acc = acc + v_ref[k]
        o_ref[...] = acc

    return pl.pallas_call(kern, out_shape=jax.ShapeDtypeStruct(v.shape[1:], F32), name=name)(v)


def _adamw(w, g, m, v, name, tr=128, blk0=0, nblk=None, into=None, copy_g=False):
    R, C = w.shape
    tr = min(tr, R)
    if nblk is None:
        assert R % tr == 0 and blk0 == 0
        nblk = R // tr
    n_out = 4 if copy_g else 3

    def kern(*refs):
        w_ref, g_ref, m_ref, v_ref = refs[:4]
        d_ref, mo_ref, vo_ref = refs[-n_out:][:3]
        gv = g_ref[...]
        mn = ADAM_B1 * m_ref[...] + (1.0 - ADAM_B1) * gv
        vn = ADAM_B2 * v_ref[...] + (1.0 - ADAM_B2) * (gv * gv)
        m_hat = mn / (1.0 - ADAM_B1 ** ADAM_STEP)
        v_hat = vn / (1.0 - ADAM_B2 ** ADAM_STEP)
        d_ref[...] = -ADAM_LR * (m_hat / (jnp.sqrt(v_hat) + ADAM_EPS) + ADAM_WD * w_ref[...])
        mo_ref[...] = mn
        vo_ref[...] = vn
        if copy_g:
            refs[-1][...] = gv

    blk = pl.BlockSpec((tr, C), lambda i: (blk0 + i, 0))
    sd = jax.ShapeDtypeStruct((R, C), F32)
    in_specs, args, aliases = [blk] * 4, [w, g, m, v], {}
    if into is not None:
        in_specs += [pl.BlockSpec(memory_space=pl.ANY)] * 3
        args += list(into)
        aliases = {4: 0, 5: 1, 6: 2}
    return pl.pallas_call(kern, grid=(nblk,), in_specs=in_specs, out_specs=(blk,) * n_out, out_shape=(sd,) * n_out,
                          input_output_aliases=aliases, name=name, compiler_params=_cp(("parallel",)))(*args)


def _adamw_w_in(wt, gp, mt, vt, offs, name, r0, tr, nblk, views, into=None, blk_key=None):
    el = lambda n: (pl.Element(n), pl.Element(D))
    first = (lambda o: r0) if blk_key is None else (lambda o: tr * o[blk_key])
    own = pl.BlockSpec(el(tr), lambda i, o: (pl.multiple_of(first(o) + tr * i, 8), 0))

    def view(k):
        return pl.BlockSpec(el(tr), lambda i, o: (pl.multiple_of(jnp.maximum(first(o) + tr * i + o[k], 0), 8), 0))

    def kern(o_ref, w_ref, m_ref, v_ref, *refs):
        g_refs, (d_ref, mo_ref, vo_ref, go_ref) = refs[:len(views)], refs[-4:]
        gv = g_refs[0][...]
        if len(views) == 2:
            row = first(o_ref) + tr * pl.program_id(0) + lax.broadcasted_iota(jnp.int32, (tr, D), 0)
            gv = jnp.where(row < o_ref[2], gv, g_refs[1][...])
        mn = ADAM_B1 * m_ref[...] + (1.0 - ADAM_B1) * gv
        vn = ADAM_B2 * v_ref[...] + (1.0 - ADAM_B2) * (gv * gv)
        m_hat = mn / (1.0 - ADAM_B1 ** ADAM_STEP)
        v_hat = vn / (1.0 - ADAM_B2 ** ADAM_STEP)
        d_ref[...] = -ADAM_LR * (m_hat / (jnp.sqrt(v_hat) + ADAM_EPS) + ADAM_WD * w_ref[...])
        mo_ref[...] = mn
        vo_ref[...] = vn
        go_ref[...] = gv

    in_specs = [own, own, own] + [view(k) for k in views]
    args = [wt, mt, vt] + [gp] * len(views)
    aliases = {}
    if into is not None:
        in_specs += [pl.BlockSpec(memory_space=pl.ANY)] * 4
        args += list(into)
        aliases = {1 + len(args) - 4 + j: j for j in range(4)}
    grid_spec = pltpu.PrefetchScalarGridSpec(num_scalar_prefetch=1, grid=(nblk,), in_specs=in_specs,
                                             out_specs=(own,) * 4)
    sd = jax.ShapeDtypeStruct(wt.shape, F32)
    return pl.pallas_call(kern, grid_spec=grid_spec, out_shape=(sd,) * 4, input_output_aliases=aliases, name=name,
                          compiler_params=_cp(("parallel",)))(offs, *args)


def _to_piece(wt, s):
    z = lambda n: jnp.zeros((n, D), wt.dtype)
    pads = [functools.partial(lambda k, w: jnp.pad(w, ((8 * k, PIECE - W_SHARD - 8 * k), (0, 0))).astype(BF16), k)
            for k in range(3)]
    last = lambda w: jnp.concatenate([z(24), w[:744], w[776:], w[744:776], z(PIECE - 24 - W_SHARD)], axis=0).astype(BF16)
    return lax.switch(s, pads + [last], wt)


_SMALL = [("b_gate", 2048), ("ssm_conv_b", 4096), ("dt_bias", 32), ("A_log", 32), ("D_skip", 32),
          ("ssm_norm_w", 2048), ("norm_mlp", 1024), ("norm_final", 1024), ("sc_conv_w", 3072), ("ssm_conv_w", 16384),
          ("loss", 1)]


def _pack(vals, table, rows):
    parts = []
    for name, n in table:
        v = vals[name].reshape(-1).astype(F32)
        pad = (-n) % 128
        parts.append(jnp.pad(v, (0, pad)) if pad else v)
    flat = jnp.concatenate(parts)
    return jnp.pad(flat, (0, rows * 128 - flat.shape[0])).reshape(rows, 128)


def _unpack(arr, table):
    flat = arr.reshape(-1)
    out, off = {}, 0
    for name, n in table:
        out[name] = flat[off:off + n]
        off += n + ((-n) % 128)
    return out


def kernel(x, norm_mix, w_in, b_gate, sc_conv_w, ssm_conv_w, ssm_conv_b, dt_bias, A_log, D_skip, ssm_norm_w, w_branch_sc, w_branch_ssm, w_out, norm_mlp, w_mlp1, w_mlp2, norm_final, loss_target, m_norm_mix, m_w_in, m_b_gate, m_sc_conv_w, m_ssm_conv_w, m_ssm_conv_b, m_dt_bias, m_A_log, m_D_skip, m_ssm_norm_w, m_w_branch_sc, m_w_branch_ssm, m_w_out, m_norm_mlp, m_w_mlp1, m_w_mlp2, m_norm_final, v_norm_mix, v_w_in, v_b_gate, v_sc_conv_w, v_ssm_conv_w, v_ssm_conv_b, v_dt_bias, v_A_log, v_D_skip, v_ssm_norm_w, v_w_branch_sc, v_w_branch_ssm, v_w_out, v_norm_mlp, v_w_mlp1, v_w_mlp2, v_norm_final):
    L = x.shape[1]
    nc = L // Q
    xi, yi, ci = lax.axis_index("x"), lax.axis_index("y"), lax.axis_index("c")
    s = 2 * xi + yi
    idx = jnp.stack([s, ci]).astype(jnp.int32)
    x0 = x.reshape(L, D)
    tgt = loss_target.reshape(L, D)
    small_names = ["b_gate", "sc_conv_w", "ssm_conv_w", "ssm_conv_b", "dt_bias", "A_log", "D_skip", "ssm_norm_w",
                   "norm_mlp", "norm_final"]
    small_wmv = [dict(zip(small_names, vals)) for vals in (
        (b_gate, sc_conv_w, ssm_conv_w, ssm_conv_b, dt_bias, A_log, D_skip, ssm_norm_w, norm_mlp, norm_final),
        (m_b_gate, m_sc_conv_w, m_ssm_conv_w, m_ssm_conv_b, m_dt_bias, m_A_log, m_D_skip, m_ssm_norm_w, m_norm_mlp,
         m_norm_final),
        (v_b_gate, v_sc_conv_w, v_ssm_conv_w, v_ssm_conv_b, v_dt_bias, v_A_log, v_D_skip, v_ssm_norm_w, v_norm_mlp,
         v_norm_final))]
    small_table = [(n, int(small_wmv[0][n].size)) for n in small_names]
    small_rows = 136
    pk_w, pk_m, pk_v = [_pack(d, small_table, small_rows) for d in small_wmv]

    piece = _to_piece(w_in.T, s)
    nb = PMAIN // XTRA
    cws = jnp.zeros((8, 1280), F32)
    cws = cws.at[0:3, 0:256].set(sc_conv_w).at[0:4, 256:1280].set(ssm_conv_w)
    cw0 = lax.dynamic_update_slice(jnp.zeros((4, 8, 1280), F32), cws[None], (s, 0, 0))
    win_keys, win2_keys, mid_keys, end_keys = ["xt", "cw", "wq0"], ["wq1"], ["wa", "wb", "wo", "w1"], ["w2"]
    gw, sems_w, tok = _split_call(
        "ag_win_start", {"wct": lax.empty((NCW, D), BF16), "xt": lax.empty((4, XTRA, D), BF16), "cw": cw0, "piece": piece},
        start=_ag_chips_plan(win_keys))
    g2, sems_w2, tok = _split_call("ag_win2_start", {"wct": gw["wct"], "piece": gw["piece"]},
                                   start=_ag_chips_plan(win2_keys), after=tok)
    piece = g2["piece"]
    gw["wct"] = _place(piece, (NCW, D), (XTRA, D), lambda i, r: (nb * r[0] + i, 0), idx, "place_wct", nblk=nb,
                       dep=tok, into=g2["wct"])
    gw["xt"] = _place(piece, (4, XTRA, D), (1, XTRA, D), lambda i, r: (r[0], 0, 0), idx, "place_xt", blk0=nb, nblk=1,
                      dep=tok, into=gw["xt"])
    gw["piece"] = piece
    wa0 = _place(w_branch_sc, (D, D), (256, 1024), lambda i, r: (r[0], 0), idx, "place_wa", dep=tok)
    wb0 = _place(w_branch_ssm, (INNER, D), (512, 1024), lambda i, r: (r[0], 0), idx, "place_wb", dep=tok)
    wo0 = _place(w_out, (D, D), (256, 1024), lambda i, r: (r[0], 0), idx, "place_wo", dep=tok)
    w10 = _place(w_mlp1, (D, DFF), (256, 1024), lambda i, r: (i, r[0]), idx, "place_w1", dep=tok)
    gm, sems_m, tok = _split_call("ag_mid_start", {"wa": wa0, "wb": wb0, "wo": wo0, "w1": w10},
                                  start=_ag_chips_plan(mid_keys))
    w20 = _place(w_mlp2, (DFF, D), (256, 1024), lambda i, r: (4 * r[0] + i, 0), idx, "place_w2", dep=tok)
    ge, sems_e, tok = _split_call("ag_end_start", {"w2": w20}, start=_ag_chips_plan(end_keys))
    h = _rms_fwd(x0, norm_mix, "rms_mix", dep=tok)
    gw, sems_w, tok = _split_call("ag_win_pass", gw, wait=_ag_chips_plan(win_keys), wait_sems=sems_w,
                                  start=_ag_sibling_plan(win_keys), after=[h, pk_w, pk_m, pk_v])
    gw, _, _ = _split_call("ag_win_done", gw, wait=_ag_sibling_plan(win_keys), wait_sems=sems_w, after=tok)
    wc, cw_all = _fix_wct(gw["wct"], gw["xt"]), gw["cw"]
    sc_w_full = jnp.concatenate([cw_all[k, :, 0:256] for k in range(4)], axis=1)
    ssm_w_full = jnp.concatenate([cw_all[k, :, 256:1280] for k in range(4)], axis=1)
    cw4 = ssm_w_full.at[4].set(ssm_conv_b)
    vec = jnp.zeros((8, 128), F32).at[0, :NH].set(dt_bias).at[1, :NH].set(A_log)
    vecg = jnp.zeros((NG, 8, 128), F32).at[:, 0, :4].set(A_log.reshape(NG, 4)).at[:, 1, :4].set(D_skip.reshape(NG, 4))

    dtraw = _matmul(h, wc[C_DT:], "nt", F32, 512, 256, 1024, "in_proj_dt")
    proj = _in_proj_wave(h, wc, 0)
    g2, sems_w2, tok = _split_call("ag_win2_pass", {"wct": wc, "piece": gw["piece"]},
                                   wait=_ag_chips_plan(win2_keys), wait_sems=sems_w2,
                                   start=_ag_sibling_plan(win2_keys), after=[proj, dtraw])
    g2, _, _ = _split_call("ag_win2_done", g2, wait=_ag_sibling_plan(win2_keys), wait_sems=sems_w2, after=tok)
    wc = g2["wct"]
    proj = _in_proj_wave(h, wc, 1, proj=proj)
    ya = _sc_fwd(proj, sc_w_full)
    xbc = _ssm_conv_fwd(proj, cw4)
    dt4, cs4, sg4 = _dt_prep(dtraw, vec)
    y, s_all = _ssd_fwd(xbc, dt4, cs4, vecg)
    gm, sems_m, tok = _split_call("ag_mid_pass", gm, wait=_ag_chips_plan(mid_keys), wait_sems=sems_m,
                                  start=_ag_sibling_plan(mid_keys), after=[y, ya])
    y = _tie(y, tok, "tie_y")
    yb = _gnorm_fwd(y, proj, ssm_norm_w)
    gm, _, _ = _split_call("ag_mid_done", gm, wait=_ag_sibling_plan(mid_keys), wait_sems=sems_m, after=yb)
    wa, wb, wo, w1 = gm["wa"], gm["wb"], gm["wo"], gm["w1"]
    ge, sems_e, tok = _split_call("ag_end_pass", ge, wait=_ag_chips_plan(end_keys), wait_sems=sems_e,
                                  start=_ag_sibling_plan(end_keys), after=yb)
    br_a = _matmul(ya, wa, "nn", F32, 1024, 1024, 1024, "branch_sc", dep=tok)
    br_b, merged = _branch_ssm_merge(yb, wb, proj, b_gate, br_a)
    x1, h2 = _matmul_res_rms(merged, wo, x0, norm_mlp, 1024, "out_proj")
    a1, rl = _matmul(h2, w1, "nn", BF16, 1024, 1024, 1024, "mlp1", epi="relu2", n_outer=True)
    ge, _, _ = _split_call("ag_end_done", ge, wait=_ag_sibling_plan(end_keys), wait_sems=sems_e, after=a1)
    w2 = ge["w2"]
    dx2, g_nf, loss8 = _matmul_res_final(rl, w2, x1, norm_final, tgt, 512, "mlp2")

    da = _matmul(dx2, w2, "nt", BF16, 1024, 1024, 1024, "mlp2_dx", epi="drelu", extra=a1, n_outer=True)
    g_w2 = _matmul(rl, dx2, "tn", BF16, 1024, 1024, 2048, "mlp2_dw")
    g_w1 = _matmul(h2, da, "tn", BF16, 1024, 1024, 2048, "mlp1_dw")
    dx1, g_nmlp = _matmul_rms_bwd(da, w1, "nt", x1, norm_mlp, dx2, 512, 4096, "mlp1_dx")
    g_wo = _matmul(merged, dx1, "tn", BF16, 1024, 1024, 2048, "out_proj_dw")
    dproj = lax.empty((L, NCW), BF16)
    dbr, dproj, g_bg = _merge_bwd(dx1, wo, proj, b_gate, br_a, br_b, dproj)
    dya = _matmul(dbr[0], wa, "nt", F32, 1024, 1024, 1024, "branch_sc_dx")
    g_wa = _matmul(ya, dbr[0], "tn", BF16, 1024, 1024, 2048, "branch_sc_dw")
    dproj, g_scw, g_wc = _sc_bwd(dya, proj, sc_w_full, dproj, h, lax.empty((NCW, D), BF16))
    g_wb = _matmul(yb, dbr[1], "tn", BF16, 1024, 1024, 2048, "branch_ssm_dw")
    rs_a = _ReduceScatter([1, 2, 3, 4, 5], {1: g_w1, 2: g_w2, 3: g_wa, 4: g_wb, 5: g_wo}, idx, "a")
    dy, dproj, g_snw = _gnorm_bwd(dbr, wb, y, proj, ssm_norm_w, dproj, rs_a.token)
    tok = rs_a.chips(after=dy)
    dxs, dbm, dcm, ddt_g, st = _ssd_bwd(xbc, dt4, cs4, sg4, vecg, s_all, _tie(dy, tok, "tie_dy"))
    dproj, gx1, g_wc = _ssm_conv_bwd(dxs, proj, cw4, dproj, 0, "ssm_conv_bwd_x", h, g_wc)
    dproj, gx2, g_wc = _ssm_conv_bwd(dbm, proj, cw4, dproj, INNER, "ssm_conv_bwd_b", h, g_wc)
    dproj, gx3, g_wc = _ssm_conv_bwd(dcm, proj, cw4, dproj, INNER + NG * NS, "ssm_conv_bwd_c", h, g_wc)
    g_cw4 = jnp.concatenate([gx1, gx2, gx3], axis=1)
    dproj, g_dtb = _dt_bwd(ddt_g, dproj)
    small = {"b_gate": g_bg[0], "ssm_conv_b": g_cw4[4], "dt_bias": g_dtb[0, :NH],
             "A_log": st[:, 0, :4], "D_skip": st[:, 1, :4], "ssm_norm_w": g_snw[0], "norm_mlp": g_nmlp[0],
             "norm_final": g_nf[0], "sc_conv_w": g_scw[0:3], "ssm_conv_w": g_cw4[0:4], "loss": loss8[0, 0:1]}
    me = 4 * xi + 2 * yi + ci
    sm8 = lax.dynamic_update_slice(jnp.zeros((8, SMALL_ROWS, 128), F32), _pack(small, _SMALL, SMALL_ROWS)[None], (me, 0, 0))
    sm_arr, sm_sems, tok = _split_call("small_start", {"sm": sm8}, start=_all8_plan("sm"))
    g_wc = _in_proj_dw_rows(dproj, h, g_wc, C_Z, C_XBC - C_Z, 1024, "in_proj_dw_z", dep=tok)
    g_wc = _in_proj_dw_rows(dproj, h, g_wc, C_GATE, C_DT - C_GATE, 1024, "in_proj_dw_gate")
    g_wc = _in_proj_dw_rows(dproj, h, g_wc, C_DT, NCW - C_DT, 256, "in_proj_dw_dt")
    rs_b = _ReduceScatter([0], {0: g_wc}, idx, "b")
    tok = rs_a.share(after=rs_b.token)
    tok = rs_b.chips(after=tok)
    grad_x, g_nm = _matmul_rms_bwd(dproj, wc, "nn", x0, norm_mix, dx1, 512, 5760, "in_proj_dx", dep=tok)
    nm8 = lax.dynamic_update_slice(jnp.zeros((8, 8, 128), F32), g_nm[0].reshape(1, 8, 128), (me, 0, 0))
    nm_arr, nm_sems, tok = _split_call("norm_mix_start", {"nm": nm8}, start=_all8_plan("nm"))
    sm_arr, _, _ = _split_call("small_wait", sm_arr, wait=_all8_plan("sm"), wait_sems=sm_sems, after=tok)
    small_sum = _sum8(sm_arr["sm"])
    gs = _unpack(small_sum, _SMALL)
    red = rs_a.result(after=tok)
    big = {"w_mlp1": red[1], "w_mlp2": red[2], "w_branch_sc": red[3], "w_branch_ssm": red[4], "w_out": red[5]}

    given = dict(norm_mix=norm_mix, w_in=w_in, b_gate=b_gate, sc_conv_w=sc_conv_w, ssm_conv_w=ssm_conv_w, ssm_conv_b=ssm_conv_b, dt_bias=dt_bias, A_log=A_log, D_skip=D_skip, ssm_norm_w=ssm_norm_w, w_branch_sc=w_branch_sc, w_branch_ssm=w_branch_ssm, w_out=w_out, norm_mlp=norm_mlp, w_mlp1=w_mlp1, w_mlp2=w_mlp2, norm_final=norm_final,
                 m_norm_mix=m_norm_mix, m_w_in=m_w_in, m_b_gate=m_b_gate, m_sc_conv_w=m_sc_conv_w, m_ssm_conv_w=m_ssm_conv_w, m_ssm_conv_b=m_ssm_conv_b, m_dt_bias=m_dt_bias, m_A_log=m_A_log, m_D_skip=m_D_skip, m_ssm_norm_w=m_ssm_norm_w, m_w_branch_sc=m_w_branch_sc, m_w_branch_ssm=m_w_branch_ssm, m_w_out=m_w_out, m_norm_mlp=m_norm_mlp, m_w_mlp1=m_w_mlp1, m_w_mlp2=m_w_mlp2, m_norm_final=m_norm_final,
                 v_norm_mix=v_norm_mix, v_w_in=v_w_in, v_b_gate=v_b_gate, v_sc_conv_w=v_sc_conv_w, v_ssm_conv_w=v_ssm_conv_w, v_ssm_conv_b=v_ssm_conv_b, v_dt_bias=v_dt_bias, v_A_log=v_A_log, v_D_skip=v_D_skip, v_ssm_norm_w=v_ssm_norm_w, v_w_branch_sc=v_w_branch_sc, v_w_branch_ssm=v_w_branch_ssm, v_w_out=v_w_out, v_norm_mlp=v_norm_mlp, v_w_mlp1=v_w_mlp1, v_w_mlp2=v_w_mlp2, v_norm_final=v_norm_final)
    order = ["norm_mix", "w_in", "b_gate", "sc_conv_w", "ssm_conv_w", "ssm_conv_b", "dt_bias", "A_log", "D_skip",
             "ssm_norm_w", "w_branch_sc", "w_branch_ssm", "w_out", "norm_mlp", "w_mlp1", "w_mlp2", "norm_final"]
    grad, delta, new_m, new_v = {}, {}, {}, {}
    for n in big:
        delta[n], new_m[n], new_v[n], grad[n] = _adamw(given[n], big[n], given["m_" + n], given["v_" + n],
                                                       "adamw_" + n, copy_g=True)
    big["w_in"] = None
    grad_small = {n: gs[n].reshape(given[n].shape) for n in small_names if n not in ("sc_conv_w", "ssm_conv_w")}
    grad_small["sc_conv_w"] = lax.dynamic_slice(gs["sc_conv_w"].reshape(3, D), (0, 256 * s), (3, 256))
    grad_small["ssm_conv_w"] = lax.dynamic_slice(gs["ssm_conv_w"].reshape(4, XBC), (0, 1024 * s), (4, 1024))
    table = small_table
    ds_, ms_, vs_ = _adamw(pk_w, _pack(grad_small, table, small_rows), pk_m, pk_v, "adamw_small", tr=small_rows)
    ds_, ms_, vs_ = _unpack(ds_, table), _unpack(ms_, table), _unpack(vs_, table)
    for n in grad_small:
        shp = given[n].shape
        grad[n] = grad_small[n]
        delta[n], new_m[n], new_v[n] = ds_[n].reshape(shp), ms_[n].reshape(shp), vs_[n].reshape(shp)

    done = [new_v[n] for n in ("w_mlp1", "w_mlp2", "w_branch_sc", "w_branch_ssm", "w_out")] + [vs_["b_gate"]]
    tok = rs_b.share(after=done)
    offs = jnp.where(s == 3, jnp.array([24, -8, 744, 2072, -8], jnp.int32),
                     jnp.stack([8 * s, 8 * s, 0 * s, 8 * s, 8 * s]).astype(jnp.int32))
    offs = jnp.concatenate([offs, jnp.stack([7 * ci, 4 - 4 * ci]).astype(jnp.int32)])
    nmain = W_SHARD // 256
    wt_own = (w_in.T, rs_b.arr["f0"], m_w_in.T, v_w_in.T, offs)
    res = _adamw_w_in(*wt_own, "adamw_w_in_own", 0, 256, 4, (0, 1), blk_key=5)
    gp = rs_b.result(after=[tok, res[0]])[0]
    wt_args = (w_in.T, gp, m_w_in.T, v_w_in.T, offs)
    res = _adamw_w_in(*wt_args, "adamw_w_in", 0, 256, nmain - 4, (0, 1), into=res, blk_key=6)
    res = _adamw_w_in(*wt_args, "adamw_w_in_dt", 744, 32, 1, (3,), into=res)
    dt_, mt_, vt_, gwt = _adamw_w_in(*wt_args, "adamw_w_in_tail", 256 * nmain, 8, 1, (4,), into=res)
    grad["w_in"], delta["w_in"], new_m["w_in"], new_v["w_in"] = gwt.T, dt_.T, mt_.T, vt_.T
    nm_arr, _, _ = _split_call("norm_mix_wait", nm_arr, wait=_all8_plan("nm"), wait_sems=nm_sems, after=tok)
    g8 = _sum8(nm_arr["nm"], "norm_mix_sum")
    r8 = lambda a: a.reshape(8, 128)
    d8, m8, v8 = _adamw(r8(norm_mix), g8, r8(m_norm_mix), r8(v_norm_mix), "adamw_norm_mix", tr=8)
    grad["norm_mix"], delta["norm_mix"] = g8.reshape(D), d8.reshape(D)
    new_m["norm_mix"], new_v["norm_mix"] = m8.reshape(D), v8.reshape(D)

    loss = gs["loss"].reshape(())
    return (loss, grad_x.reshape(1, L, D), *[grad[n] for n in order], *[delta[n] for n in order],
            *[new_m[n] for n in order], *[new_v[n] for n in order])
```

```python
import functools

import jax
import jax.numpy as jnp
from jax import lax
from jax.experimental import pallas as pl
from jax.experimental.pallas import tpu as pltpu

F32 = jnp.float32
BF16 = jnp.bfloat16
MESH = pl.DeviceIdType.MESH
HBM = pltpu.HBM

D = 1024
INNER = 2048
HD = 64
NH = 32
NG = 8
NS = 128
Q = 128
GPS = 8
XBC = 4096
DFF = 4096
EPS = 1e-6
W_SHARD = 2824
NCW = 11520
PIECE = 3072
PMAIN = 2816
C_Z, C_XBC, C_GATE, C_DT = 3072, 5120, 9216, 11264
SMALL_ROWS = 256
VMEM_LIMIT = 56 * 1024 * 1024

ADAM_LR, ADAM_B1, ADAM_B2, ADAM_EPS, ADAM_WD, ADAM_STEP = 0.001, 0.9, 0.999, 1e-08, 0.01, 10


def _cp(sem=None, vmem=VMEM_LIMIT):
    return pltpu.CompilerParams(dimension_semantics=sem, vmem_limit_bytes=vmem)


def _sigmoid(v):
    return 1.0 / (1.0 + jnp.exp(-v))


_DIMS = {"nn": (((1,), (0,)), ((), ())), "nt": (((1,), (1,)), ((), ())), "tn": (((0,), (0,)), ((), ()))}


def _matmul(a, b, mode, out_dtype, tm, tn, tk, name, epi=None, extra=None, n_outer=False, dep=None):
    if mode == "tn":
        K, M = a.shape
    else:
        M, K = a.shape
    N = b.shape[0] if mode == "nt" else b.shape[1]
    tm, tn, tk = min(tm, M), min(tn, N), min(tk, K)
    assert M % tm == 0 and N % tn == 0 and K % tk == 0, (name, M, N, K, tm, tn, tk)
    nm, nn, nk = M // tm, N // tn, K // tk
    dims = _DIMS[mode]

    def ij(p0, p1):
        return (p1, p0) if n_outer else (p0, p1)

    if mode == "tn":
        a_spec = pl.BlockSpec((tk, tm), lambda p0, p1, k: (k, ij(p0, p1)[0]))
    else:
        a_spec = pl.BlockSpec((tm, tk), lambda p0, p1, k: (ij(p0, p1)[0], k))
    if mode == "nt":
        b_spec = pl.BlockSpec((tn, tk), lambda p0, p1, k: (ij(p0, p1)[1], k))
    else:
        b_spec = pl.BlockSpec((tk, tn), lambda p0, p1, k: (k, ij(p0, p1)[1]))
    o_spec = pl.BlockSpec((tm, tn), lambda p0, p1, k: ij(p0, p1))
    in_specs = [a_spec, b_spec]
    args = [a, b]
    if epi in ("res", "drelu"):
        in_specs.append(o_spec)
        args.append(extra)
    if dep is not None:
        in_specs.append(pl.BlockSpec(memory_space=pl.ANY))
        args.append(dep)
    n_in = len(args)
    if epi == "relu2":
        out_shape = (jax.ShapeDtypeStruct((M, N), out_dtype), jax.ShapeDtypeStruct((M, N), BF16))
        out_specs = (o_spec, o_spec)
    else:
        out_shape = jax.ShapeDtypeStruct((M, N), out_dtype)
        out_specs = o_spec

    def kern(*refs):
        a_ref, b_ref = refs[0], refs[1]
        e_ref = refs[2] if epi in ("res", "drelu") else None
        acc = refs[-1]
        outs = refs[n_in:-1] if nk > 1 else refs[n_in:]
        k = pl.program_id(2)

        def product():
            return lax.dot_general(a_ref[...].astype(BF16), b_ref[...].astype(BF16), dims, preferred_element_type=F32)

        def finish(r):
            if epi is None:
                outs[0][...] = r.astype(out_dtype)
            elif epi == "res":
                outs[0][...] = (r + e_ref[...]).astype(out_dtype)
            elif epi == "relu2":
                outs[0][...] = r.astype(out_dtype)
                t = jnp.maximum(r, 0.0)
                outs[1][...] = (t * t).astype(BF16)
            else:
                outs[0][...] = (r * (2.0 * jnp.maximum(e_ref[...].astype(F32), 0.0))).astype(out_dtype)

        if nk == 1:
            finish(product())
        else:
            @pl.when(k == 0)
            def _():
                acc[...] = jnp.zeros_like(acc)

            acc[...] += product()

            @pl.when(k == nk - 1)
            def _():
                finish(acc[...])

    grid = (nn, nm, nk) if n_outer else (nm, nn, nk)
    return pl.pallas_call(
        kern, grid=grid, in_specs=in_specs, out_specs=out_specs, out_shape=out_shape,
        scratch_shapes=[pltpu.VMEM((tm, tn), F32)] if nk > 1 else [], name=name,
        compiler_params=_cp(("parallel", "parallel", "arbitrary")),
    )(*args)


def _matmul_res_rms(a, b, x, w, tm, name):
    M, K = a.shape
    tm = min(tm, M)

    def kern(a_ref, b_ref, x_ref, w_ref, x1_ref, h_ref):
        x1 = x_ref[...] + lax.dot_general(a_ref[...].astype(BF16), b_ref[...].astype(BF16), _DIMS["nn"],
                                          preferred_element_type=F32)
        x1_ref[...] = x1
        r = lax.rsqrt(jnp.mean(x1 * x1, axis=-1, keepdims=True) + EPS)
        h_ref[...] = ((x1 * r) * w_ref[...]).astype(BF16)

    row = pl.BlockSpec((tm, D), lambda i: (i, 0))
    return pl.pallas_call(
        kern, grid=(M // tm,),
        in_specs=[pl.BlockSpec((tm, K), lambda i: (i, 0)), pl.BlockSpec((K, D), lambda i: (0, 0)), row,
                  pl.BlockSpec((1, D), lambda i: (0, 0))],
        out_specs=(row, row), out_shape=(jax.ShapeDtypeStruct((M, D), F32), jax.ShapeDtypeStruct((M, D), BF16)),
        name=name, compiler_params=_cp(("parallel",)),
    )(a, b, x, w.reshape(1, D))


def _matmul_res_final(a, b, x, w, tgt, tm, name):
    M, K = a.shape
    tm = min(tm, M)

    def kern(a_ref, b_ref, x_ref, w_ref, t_ref, dx_ref, gw_ref, loss_ref):
        @pl.when(pl.program_id(0) == 0)
        def _():
            gw_ref[...] = jnp.zeros_like(gw_ref)
            loss_ref[...] = jnp.zeros_like(loss_ref)

        xv = x_ref[...] + lax.dot_general(a_ref[...].astype(BF16), b_ref[...].astype(BF16), _DIMS["nn"],
                                          preferred_element_type=F32)
        r = lax.rsqrt(jnp.mean(xv * xv, axis=-1, keepdims=True) + EPS)
        xn = xv * r
        e = xn * w_ref[...] - t_ref[...]
        loss_ref[...] += 0.5 * jnp.sum(jnp.mean(e * e, axis=-1, keepdims=True))
        dyv = e * (1.0 / D)
        gw_ref[...] += jnp.broadcast_to(jnp.sum(dyv * xn, axis=0, keepdims=True), (8, D))
        dxn = dyv * w_ref[...]
        dx_ref[...] = r * (dxn - xn * jnp.mean(dxn * xn, axis=-1, keepdims=True))

    row = pl.BlockSpec((tm, D), lambda i: (i, 0))
    return pl.pallas_call(
        kern, grid=(M // tm,),
        in_specs=[pl.BlockSpec((tm, K), lambda i: (i, 0)), pl.BlockSpec((K, D), lambda i: (0, 0)), row,
                  pl.BlockSpec((1, D), lambda i: (0, 0)), row],
        out_specs=(row, pl.BlockSpec((8, D), lambda i: (0, 0)), pl.BlockSpec((8, 128), lambda i: (0, 0))),
        out_shape=(jax.ShapeDtypeStruct((M, D), F32), jax.ShapeDtypeStruct((8, D), F32),
                   jax.ShapeDtypeStruct((8, 128), F32)),
        name=name, compiler_params=_cp(("arbitrary",)),
    )(a, b, x, w.reshape(1, D), tgt)


def _matmul_rms_bwd(a, b, mode, x, w, res, tm, tk, name, dep=None):
    M, K = a.shape
    tm, tk = min(tm, M), min(tk, K)
    nk = K // tk
    assert M % tm == 0 and K % tk == 0
    b_spec = (pl.BlockSpec((tk, D), lambda k, i: (k, 0)) if mode == "nn" else pl.BlockSpec((D, tk), lambda k, i: (0, k)))
    row = pl.BlockSpec((tm, D), lambda k, i: (jnp.where(k == nk - 1, i, 0), 0))
    deps = [] if dep is None else [dep]

    def kern(a_ref, b_ref, x_ref, w_ref, res_ref, *rest):
        dx_ref, gw_ref, acc = rest[-3:]
        k, i = pl.program_id(0), pl.program_id(1)

        @pl.when((i == 0) & (k == 0))
        def _():
            gw_ref[...] = jnp.zeros_like(gw_ref)

        def product():
            return lax.dot_general(a_ref[...].astype(BF16), b_ref[...].astype(BF16), _DIMS[mode],
                                   preferred_element_type=F32)

        def finish(dyv):
            xv = x_ref[...]
            r = lax.rsqrt(jnp.mean(xv * xv, axis=-1, keepdims=True) + EPS)
            xn = xv * r
            gw_ref[...] += jnp.broadcast_to(jnp.sum(dyv * xn, axis=0, keepdims=True), (8, D))
            dxn = dyv * w_ref[...]
            dx_ref[...] = res_ref[...] + r * (dxn - xn * jnp.mean(dxn * xn, axis=-1, keepdims=True))

        if nk == 1:
            finish(product())
        else:
            rows = pl.ds(pl.multiple_of(i * tm, tm), tm)

            @pl.when(k == 0)
            def _():
                acc[rows, :] = jnp.zeros((tm, D), F32)

            acc[rows, :] += product()

            @pl.when(k == nk - 1)
            def _():
                finish(acc[rows, :])

    return pl.pallas_call(
        kern, grid=(nk, M // tm),
        in_specs=[pl.BlockSpec((tm, tk), lambda k, i: (i, k)), b_spec, row, pl.BlockSpec((1, D), lambda k, i: (0, 0)),
                  row] + [pl.BlockSpec(memory_space=pl.ANY)] * len(deps),
        out_specs=(row, pl.BlockSpec((8, D), lambda k, i: (0, 0))),
        out_shape=(jax.ShapeDtypeStruct((M, D), F32), jax.ShapeDtypeStruct((8, D), F32)),
        scratch_shapes=[pltpu.VMEM((M, D) if nk > 1 else (8, 128), F32)], name=name,
        compiler_params=_cp(("arbitrary", "arbitrary")),
    )(a, b, x, w.reshape(1, D), res, *deps)


def _rms_fwd(x, w, name, tl=256, dep=None):
    L = x.shape[0]

    def kern(x_ref, w_ref, *rest):
        o_ref = rest[-1]
        xv = x_ref[...]
        r = lax.rsqrt(jnp.mean(xv * xv, axis=-1, keepdims=True) + EPS)
        o_ref[...] = ((xv * r) * w_ref[...]).astype(BF16)

    row = pl.BlockSpec((tl, D), lambda i: (i, 0))
    deps = [] if dep is None else [dep]
    return pl.pallas_call(
        kern, grid=(L // tl,),
        in_specs=[row, pl.BlockSpec((1, D), lambda i: (0, 0))] + [pl.BlockSpec(memory_space=pl.ANY)] * len(deps),
        out_specs=row, out_shape=jax.ShapeDtypeStruct((L, D), BF16), name=name, compiler_params=_cp(("parallel",)),
    )(x, w.reshape(1, D), *deps)


def _down(v, k):
    if k == 0:
        return v
    t = lax.broadcasted_iota(jnp.int32, v.shape, 0)
    return jnp.where(t >= k, pltpu.roll(v, k, axis=0), 0.0)


def _up(v, k):
    if k == 0:
        return v
    n = v.shape[0]
    t = lax.broadcasted_iota(jnp.int32, v.shape, 0)
    return jnp.where(t < n - k, pltpu.roll(v, n - k, axis=0), 0.0)


TW = 256


def _sc_fwd(proj, cw):
    L = proj.shape[0]
    nb = D // TW

    def kern(b_ref, c_ref, x_ref, w_ref, o_ref):
        u = c_ref[...].astype(F32) * x_ref[...].astype(F32)
        w = w_ref[...]
        cv = w[0:1] * _down(u, 2) + w[1:2] * _down(u, 1) + w[2:3] * u
        o_ref[...] = (b_ref[...].astype(F32) * cv).astype(BF16)

    col = lambda off: pl.BlockSpec((L, TW), lambda j: (0, off + j))
    return pl.pallas_call(
        kern, grid=(nb,), in_specs=[col(0), col(nb), col(2 * nb), pl.BlockSpec((8, TW), lambda j: (0, j))],
        out_specs=pl.BlockSpec((L, TW), lambda j: (0, j)), out_shape=jax.ShapeDtypeStruct((L, D), BF16),
        name="sc_fwd", compiler_params=_cp(("parallel",)),
    )(proj, proj, proj, cw)


def _sc_bwd(dya, proj, cw, dproj):
    L = proj.shape[0]
    nb = D // TW

    def kern(d_ref, b_ref, c_ref, x_ref, w_ref, _, dp_ref, gw_ref, keep):
        sec = pl.program_id(1)

        @pl.when(sec == 0)
        def _():
            cs, xs, dyv = c_ref[...].astype(F32), x_ref[...].astype(F32), d_ref[...]
            w = w_ref[...]
            u = cs * xs
            u1, u2 = _down(u, 1), _down(u, 2)
            cv = w[0:1] * u2 + w[1:2] * u1 + w[2:3] * u
            dcv = dyv * b_ref[...].astype(F32)
            du = w[2:3] * dcv + w[1:2] * _up(dcv, 1) + w[0:1] * _up(dcv, 2)
            g0 = jnp.sum(dcv * u2, axis=0, keepdims=True)
            g1 = jnp.sum(dcv * u1, axis=0, keepdims=True)
            g2 = jnp.sum(dcv * u, axis=0, keepdims=True)
            row = lax.broadcasted_iota(jnp.int32, (8, TW), 0)
            gw_ref[...] = jnp.where(row == 0, g0, jnp.where(row == 1, g1, jnp.where(row == 2, g2, 0.0)))
            dp_ref[...] = (dyv * cv).astype(BF16)
            keep[0] = (du * xs).astype(BF16)
            keep[1] = (du * cs).astype(BF16)

        @pl.when(sec > 0)
        def _():
            dp_ref[...] = keep[sec - 1]

    col = lambda off: pl.BlockSpec((L, TW), lambda j, s: (0, off + j))
    return pl.pallas_call(
        kern, grid=(nb, 3),
        in_specs=[col(0), col(0), col(nb), col(2 * nb), pl.BlockSpec((8, TW), lambda j, s: (0, j)),
                  pl.BlockSpec(memory_space=pl.ANY)],
        out_specs=(pl.BlockSpec((L, TW), lambda j, s: (0, s * nb + j)), pl.BlockSpec((8, TW), lambda j, s: (0, j))),
        out_shape=(jax.ShapeDtypeStruct(dproj.shape, BF16), jax.ShapeDtypeStruct((8, D), F32)),
        scratch_shapes=[pltpu.VMEM((2, L, TW), BF16)],
        input_output_aliases={5: 0}, name="sc_bwd", compiler_params=_cp(("parallel", "arbitrary")),
    )(dya, proj, proj, proj, cw, dproj)


def _ssm_conv_fwd(proj, cw4):
    L = proj.shape[0]
    off = C_XBC // TW

    def kern(r_ref, w_ref, o_ref):
        raw = r_ref[...].astype(F32)
        w = w_ref[...]
        c4 = w[0:1] * _down(raw, 3) + w[1:2] * _down(raw, 2) + w[2:3] * _down(raw, 1) + w[3:4] * raw + w[4:5]
        o_ref[...] = c4 * _sigmoid(c4)

    return pl.pallas_call(
        kern, grid=(XBC // TW,),
        in_specs=[pl.BlockSpec((L, TW), lambda j: (0, off + j)), pl.BlockSpec((8, TW), lambda j: (0, j))],
        out_specs=pl.BlockSpec((L, TW), lambda j: (0, j)), out_shape=jax.ShapeDtypeStruct((L, XBC), F32),
        name="ssm_conv_fwd", compiler_params=_cp(("parallel",)),
    )(proj, cw4)


def _ssm_conv_bwd(dx, proj, cw4, dproj, col0, name):
    L, width = dx.shape
    off_p = (C_XBC + col0) // TW
    off_w = col0 // TW

    def kern(d_ref, r_ref, w_ref, _, dp_ref, gw_ref):
        raw = r_ref[...].astype(F32)
        w = w_ref[...]
        r1, r2, r3 = _down(raw, 1), _down(raw, 2), _down(raw, 3)
        c4 = w[0:1] * r3 + w[1:2] * r2 + w[2:3] * r1 + w[3:4] * raw + w[4:5]
        sg = _sigmoid(c4)
        dc4 = d_ref[...] * (sg * (1.0 + c4 * (1.0 - sg)))
        draw = w[3:4] * dc4 + w[2:3] * _up(dc4, 1) + w[1:2] * _up(dc4, 2) + w[0:1] * _up(dc4, 3)
        dp_ref[...] = draw.astype(BF16)
        gs = [jnp.sum(dc4 * r3, axis=0, keepdims=True), jnp.sum(dc4 * r2, axis=0, keepdims=True),
              jnp.sum(dc4 * r1, axis=0, keepdims=True), jnp.sum(dc4 * raw, axis=0, keepdims=True),
              jnp.sum(dc4, axis=0, keepdims=True)]
        row = lax.broadcasted_iota(jnp.int32, (8, TW), 0)
        acc = jnp.zeros((8, TW), F32)
        for k, gk in enumerate(gs):
            acc = jnp.where(row == k, gk, acc)
        gw_ref[...] = acc

    return pl.pallas_call(
        kern, grid=(width // TW,),
        in_specs=[pl.BlockSpec((L, TW), lambda j: (0, j)), pl.BlockSpec((L, TW), lambda j: (0, off_p + j)),
                  pl.BlockSpec((8, TW), lambda j: (0, off_w + j)), pl.BlockSpec(memory_space=pl.ANY)],
        out_specs=(pl.BlockSpec((L, TW), lambda j: (0, off_p + j)), pl.BlockSpec((8, TW), lambda j: (0, j))),
        out_shape=(jax.ShapeDtypeStruct(dproj.shape, BF16), jax.ShapeDtypeStruct((8, width), F32)),
        input_output_aliases={3: 0}, name=name, compiler_params=_cp(("arbitrary",)),
    )(dx, proj, cw4, dproj)


def _split3(v):
    h1 = v.astype(BF16)
    r1 = v - h1.astype(F32)
    h2 = r1.astype(BF16)
    h3 = (r1 - h2.astype(F32)).astype(BF16)
    return h1, h2, h3


def _dot01(m01, v, dims=_DIMS["nn"], m_left=True, terms=3):
    out = None
    for part in _split3(v)[:terms]:
        ops = (m01, part) if m_left else (part, m01)
        t = lax.dot_general(ops[0], ops[1], dims, preferred_element_type=F32)
        out = t if out is None else out + t
    return out


def _bdot(a, b, mode="nn"):
    return lax.dot_general(a.astype(BF16), b.astype(BF16), _DIMS[mode], preferred_element_type=F32)


def _softplus(v):
    return jnp.maximum(v, 0.0) + jnp.log1p(jnp.exp(-jnp.abs(v)))


def _dt_prep(proj, vec):
    L = proj.shape[0]

    def kern(p_ref, v_ref, dt_ref, cs_ref, sg_ref):
        v = v_ref[...]
        pre = p_ref[:, 0:128] + v[0:1]
        dt = _softplus(pre)
        da = dt * (-jnp.exp(v[1:2]))
        ii = lax.broadcasted_iota(jnp.int32, (Q, Q), 0)
        jj = lax.broadcasted_iota(jnp.int32, (Q, Q), 1)
        ltri = (jj <= ii).astype(BF16)
        lane = lax.broadcasted_iota(jnp.int32, (Q, 128), 1)
        for val, ref in ((dt, dt_ref), (_dot01(ltri, da), cs_ref), (_sigmoid(pre), sg_ref)):
            for g in range(NG):
                moved = val if g == 0 else pltpu.roll(val, 128 - 4 * g, axis=1)
                ref[g] = jnp.where(lane < 4, moved, 0.0)

    blk = pl.BlockSpec((NG, Q, 128), lambda c: (0, c, 0))
    return pl.pallas_call(
        kern, grid=(L // Q,),
        in_specs=[pl.BlockSpec((Q, 256), lambda c: (c, 0)), pl.BlockSpec((8, 128), lambda c: (0, 0))],
        out_specs=(blk, blk, blk),
        out_shape=(jax.ShapeDtypeStruct((NG, L, 128), F32),) * 3,
        name="dt_prep", compiler_params=_cp(("parallel",)),
    )(proj, vec)


def _head_masks():
    lane = lax.broadcasted_iota(jnp.int32, (1, 4 * HD), 1)
    return [((lane >= HD * j) & (lane < HD * (j + 1))) for j in range(4)]


def _expand4(v4, masks):
    R = v4.shape[0]
    out = jnp.zeros((R, 4 * HD), F32)
    for j in range(4):
        out = jnp.where(masks[j], jnp.broadcast_to(v4[:, j:j + 1], (R, 4 * HD)), out)
    return out


def _decay_matrix(cs_col, tri):
    colb = jnp.broadcast_to(cs_col, (Q, Q))
    return jnp.exp(jnp.where(tri, colb - colb.T, -jnp.inf))


def _ssd_fwd(xbc, dt4, cs4, vecg):
    L = xbc.shape[0]
    nc = L // Q

    def kern(x_ref, b_ref, c_ref, dt_ref, cs_ref, v_ref, y_ref, s_ref, S):
        c = pl.program_id(1)

        @pl.when(c == 0)
        def _():
            S[...] = jnp.zeros_like(S)

        masks = _head_masks()
        ii = lax.broadcasted_iota(jnp.int32, (Q, Q), 0)
        jj = lax.broadcasted_iota(jnp.int32, (Q, Q), 1)
        tri = jj <= ii
        for gi in range(GPS):
            xs, ns = slice(256 * gi, 256 * (gi + 1)), slice(NS * gi, NS * (gi + 1))
            dt4v, cs4v = dt_ref[gi], cs_ref[gi]
            dt_b, cs_b = _expand4(dt4v, masks), _expand4(cs4v, masks)
            d_b = _expand4(v_ref[gi], masks)[1:2]
            cs_last = cs_b[Q - 1:Q, :]
            x4, bm, cm = x_ref[:, xs], b_ref[:, ns], c_ref[:, ns]
            xdt = x4 * dt_b
            gm = _bdot(cm, bm, "nt")
            s4 = S[gi]
            s_ref[gi, 0] = s4
            y = _bdot(cm, s4) * jnp.exp(cs_b) + d_b * x4
            m_all = jnp.concatenate([(gm * _decay_matrix(cs4v[:, j:j + 1], tri)).astype(BF16) for j in range(4)], axis=0)
            yd = _bdot(m_all, xdt)
            for j in range(4):
                y = y + jnp.where(masks[j], yd[Q * j:Q * (j + 1)], 0.0)
            y_ref[:, xs] = y
            S[gi] = jnp.exp(cs_last) * s4 + _bdot(bm, xdt * jnp.exp(cs_last - cs_b), "tn")

    sc = pl.BlockSpec((GPS, Q, 128), lambda g, c: (g, c, 0))
    bw = NS * GPS
    return pl.pallas_call(
        kern, grid=(NG // GPS, nc),
        in_specs=[pl.BlockSpec((Q, 256 * GPS), lambda g, c: (c, g)),
                  pl.BlockSpec((Q, bw), lambda g, c: (c, INNER // bw + g)),
                  pl.BlockSpec((Q, bw), lambda g, c: (c, (INNER + NG * NS) // bw + g)),
                  sc, sc, pl.BlockSpec((GPS, 8, 128), lambda g, c: (g, 0, 0))],
        out_specs=(pl.BlockSpec((Q, 256 * GPS), lambda g, c: (c, g)),
                   pl.BlockSpec((GPS, 1, NS, 256), lambda g, c: (g, c, 0, 0))),
        out_shape=(jax.ShapeDtypeStruct((L, INNER), F32), jax.ShapeDtypeStruct((NG, nc, NS, 256), F32)),
        scratch_shapes=[pltpu.VMEM((GPS, NS, 256), F32)], name="ssd_fwd",
        compiler_params=_cp(("parallel", "arbitrary")),
    )(xbc, xbc, xbc, dt4, cs4, vecg)


def _ssd_bwd(xbc, dt4, cs4, sg4, vecg, s_all, dy):
    L = xbc.shape[0]
    nc = L // Q

    def kern(x_ref, b_ref, c_ref, dt_ref, cs_ref, sg_ref, v_ref, s_ref, dy_ref,
             dx_ref, db_ref, dc_ref, ddt_ref, st_ref, dS):
        cc = pl.program_id(1)

        @pl.when(cc == 0)
        def _():
            dS[...] = jnp.zeros_like(dS)
            st_ref[...] = jnp.zeros_like(st_ref)

        masks = _head_masks()
        ii = lax.broadcasted_iota(jnp.int32, (Q, Q), 0)
        jj = lax.broadcasted_iota(jnp.int32, (Q, Q), 1)
        tri = jj <= ii
        utri = (jj >= ii).astype(BF16)
        hsel = ((lax.broadcasted_iota(jnp.int32, (4 * HD, 128), 0) // HD)
                == lax.broadcasted_iota(jnp.int32, (4 * HD, 128), 1)).astype(BF16)
        hrow = ((lax.broadcasted_iota(jnp.int32, (4 * Q, 128), 0) // Q)
                == lax.broadcasted_iota(jnp.int32, (4 * Q, 128), 1)).astype(BF16)
        ones_q = jnp.ones((Q, 128), BF16)
        lane128 = lax.broadcasted_iota(jnp.int32, (Q, 128), 1)

        for gi in range(GPS):
            xs, ns = slice(256 * gi, 256 * (gi + 1)), slice(NS * gi, NS * (gi + 1))
            dt4v, cs4v, sg4v = dt_ref[gi], cs_ref[gi], sg_ref[gi]
            dt_b, cs_b = _expand4(dt4v, masks), _expand4(cs4v, masks)
            vv = _expand4(v_ref[gi], masks)
            a_b = -jnp.exp(vv[0:1])
            d_b = vv[1:2]
            a4 = -jnp.exp(v_ref[gi][0:1, :])
            cs_last = cs_b[Q - 1:Q, :]
            ecs = jnp.exp(cs_b)
            decay = jnp.exp(cs_last - cs_b)
            elast = jnp.exp(cs_last)
            x4, bm, cm, dyv = x_ref[:, xs], b_ref[:, ns], c_ref[:, ns], dy_ref[:, xs]
            s4 = s_ref[gi, 0]
            dsn = dS[gi]
            xdt = x4 * dt_b
            gm = _bdot(cm, bm, "nt")
            dye = dyv * ecs
            yoff = ecs * _bdot(cm, s4)
            t4 = _bdot(bm, dsn) * decay
            lms, mhs = [], []
            for j in range(4):
                colb = jnp.broadcast_to(cs4v[:, j:j + 1], (Q, Q))
                lms.append(jnp.exp(jnp.where(tri, colb - colb.T, -jnp.inf)))
                mhs.append(gm * lms[j])
            m_all = jnp.concatenate([m.astype(BF16) for m in mhs], axis=0)
            dy_m = jnp.concatenate([jnp.where(masks[j], dyv, 0.0).astype(BF16) for j in range(4)], axis=0)
            dxdt = t4 + _bdot(m_all, dy_m, "tn")
            dm_all = _bdot(dy_m, xdt, "nt")
            dg = jnp.zeros((Q, Q), F32)
            for j in range(4):
                dg = dg + dm_all[Q * j:Q * (j + 1)] * lms[j]
            e_all = dm_all * jnp.concatenate(mhs, axis=0)
            rsum = _dot01(ones_q, e_all, m_left=False, terms=2)
            da4 = -_dot01(hrow, e_all, _DIMS["tn"], m_left=False, terms=2)
            for j in range(4):
                da4 = da4 + jnp.where(lane128 == j, rsum[Q * j:Q * (j + 1)], 0.0)
            xt = xdt * t4
            tail = jnp.sum(xt, axis=0, keepdims=True) + elast * jnp.sum(s4 * dsn, axis=0, keepdims=True)
            gd_raw = jnp.sum(dyv * x4, axis=0, keepdims=True)
            stacked = jnp.concatenate([dyv * yoff - xt, dxdt * x4, jnp.broadcast_to(tail, (8, 4 * HD)),
                                       jnp.broadcast_to(gd_raw, (8, 4 * HD))], axis=0)
            seg = _dot01(hsel, stacked, m_left=False, terms=2)
            dda4 = _dot01(utri, da4 + seg[0:Q], terms=2) + seg[2 * Q:2 * Q + 1]
            ddt_ref[gi] = (dda4 * a4 + seg[Q:2 * Q]) * sg4v
            ga = jnp.sum(dda4 * dt4v * a4, axis=0, keepdims=True)
            row = lax.broadcasted_iota(jnp.int32, (8, 128), 0)
            st_ref[gi] += jnp.where(row == 0, ga, jnp.where(row == 1, seg[2 * Q + 8:2 * Q + 9], 0.0))
            dx_ref[:, xs] = d_b * dyv + dxdt * dt_b
            dc_ref[:, ns] = _bdot(dg, bm) + _bdot(dye, s4, "nt")
            db_ref[:, ns] = _bdot(dg, cm, "tn") + _bdot(xdt * decay, dsn, "nt")
            dS[gi] = elast * dsn + _bdot(cm, dye, "tn")

    rv = lambda c: nc - 1 - c
    sc = pl.BlockSpec((GPS, Q, 128), lambda g, c: (g, rv(c), 0))
    bw = NS * GPS
    return pl.pallas_call(
        kern, grid=(NG // GPS, nc),
        in_specs=[pl.BlockSpec((Q, 256 * GPS), lambda g, c: (rv(c), g)),
                  pl.BlockSpec((Q, bw), lambda g, c: (rv(c), INNER // bw + g)),
                  pl.BlockSpec((Q, bw), lambda g, c: (rv(c), (INNER + NG * NS) // bw + g)),
                  sc, sc, sc, pl.BlockSpec((GPS, 8, 128), lambda g, c: (g, 0, 0)),
                  pl.BlockSpec((GPS, 1, NS, 256), lambda g, c: (g, rv(c), 0, 0)),
                  pl.BlockSpec((Q, 256 * GPS), lambda g, c: (rv(c), g))],
        out_specs=(pl.BlockSpec((Q, 256 * GPS), lambda g, c: (rv(c), g)),
                   pl.BlockSpec((Q, bw), lambda g, c: (rv(c), g)),
                   pl.BlockSpec((Q, bw), lambda g, c: (rv(c), g)),
                   pl.BlockSpec((GPS, Q, 128), lambda g, c: (g, rv(c), 0)),
                   pl.BlockSpec((GPS, 8, 128), lambda g, c: (g, 0, 0))),
        out_shape=(jax.ShapeDtypeStruct((L, INNER), F32), jax.ShapeDtypeStruct((L, NG * NS), F32),
                   jax.ShapeDtypeStruct((L, NG * NS), F32), jax.ShapeDtypeStruct((NG, L, 128), F32),
                   jax.ShapeDtypeStruct((NG, 8, 128), F32)),
        scratch_shapes=[pltpu.VMEM((GPS, NS, 256), F32)], name="ssd_bwd",
        compiler_params=_cp(("parallel", "arbitrary")),
    )(xbc, xbc, xbc, dt4, cs4, sg4, vecg, s_all, dy)


def _dt_bwd(ddt, dproj, tl=256):
    L = ddt.shape[1]

    def kern(d_ref, _, dp_ref, gs_ref):
        @pl.when(pl.program_id(0) == 0)
        def _():
            gs_ref[...] = jnp.zeros_like(gs_ref)

        d = d_ref[0]
        for g in range(1, NG):
            d = d + pltpu.roll(d_ref[g], 4 * g, axis=1)
        gs_ref[...] += jnp.broadcast_to(jnp.sum(d, axis=0, keepdims=True), (8, 128))
        dp_ref[...] = jnp.concatenate([d, jnp.zeros_like(d)], axis=1).astype(BF16)

    return pl.pallas_call(
        kern, grid=(L // tl,),
        in_specs=[pl.BlockSpec((NG, tl, 128), lambda i: (0, i, 0)), pl.BlockSpec(memory_space=pl.ANY)],
        out_specs=(pl.BlockSpec((tl, 256), lambda i: (i, C_DT // 256)), pl.BlockSpec((8, 128), lambda i: (0, 0))),
        out_shape=(jax.ShapeDtypeStruct(dproj.shape, BF16), jax.ShapeDtypeStruct((8, 128), F32)),
        input_output_aliases={1: 0}, name="dt_bwd", compiler_params=_cp(("arbitrary",)),
    )(ddt, dproj)


GW = INNER // NG


def _gnorm_fwd(y, proj, w, tl=256):
    L = y.shape[0]
    zoff = C_Z // 1024

    def kern(y_ref, z_ref, w_ref, o_ref):
        z = z_ref[...].astype(F32)
        yz = y_ref[...] * (z * _sigmoid(z))
        wv = w_ref[...]
        for k in range(1024 // GW):
            sl = slice(GW * k, GW * (k + 1))
            v = yz[:, sl]
            rg = lax.rsqrt(jnp.mean(v * v, axis=-1, keepdims=True) + EPS)
            o_ref[:, sl] = ((v * rg) * wv[:, sl]).astype(BF16)

    blk = pl.BlockSpec((tl, 1024), lambda i, j: (i, j))
    return pl.pallas_call(
        kern, grid=(L // tl, 2),
        in_specs=[blk, pl.BlockSpec((tl, 1024), lambda i, j: (i, zoff + j)), pl.BlockSpec((1, 1024), lambda i, j: (0, j))],
        out_specs=blk, out_shape=jax.ShapeDtypeStruct((L, INNER), BF16), name="gnorm_fwd",
        compiler_params=_cp(("parallel", "parallel")),
    )(y, proj, w.reshape(1, INNER))


def _gnorm_bwd(dbr, wb, y, proj, w, dproj, dep, tl=512):
    L = y.shape[0]
    tl = min(tl, L)
    zoff = C_Z // 1024

    def kern(d_ref, b_ref, y_ref, z_ref, w_ref, _, __, dy_ref, dp_ref, gw_ref):
        @pl.when(pl.program_id(1) == 0)
        def _():
            gw_ref[...] = jnp.zeros_like(gw_ref)

        z = z_ref[...].astype(F32)
        sg = _sigmoid(z)
        sz = z * sg
        yv = y_ref[...]
        yz = yv * sz
        dv = lax.dot_general(d_ref[0], b_ref[...], _DIMS["nt"], preferred_element_type=F32)
        wv = w_ref[...]
        for k in range(1024 // GW):
            sl = slice(GW * k, GW * (k + 1))
            v = yz[:, sl]
            rg = lax.rsqrt(jnp.mean(v * v, axis=-1, keepdims=True) + EPS)
            vn = v * rg
            dk = dv[:, sl]
            gw_ref[:, sl] += jnp.broadcast_to(jnp.sum(dk * vn, axis=0, keepdims=True), (8, GW))
            dvn = dk * wv[:, sl]
            dyz = rg * (dvn - vn * jnp.mean(dvn * vn, axis=-1, keepdims=True))
            dy_ref[:, sl] = dyz * sz[:, sl]
            dp_ref[:, sl] = (dyz * yv[:, sl] * (sg[:, sl] * (1.0 + z[:, sl] * (1.0 - sg[:, sl])))).astype(BF16)

    blk = pl.BlockSpec((tl, 1024), lambda j, i: (i, j))
    zblk = pl.BlockSpec((tl, 1024), lambda j, i: (i, zoff + j))
    return pl.pallas_call(
        kern, grid=(2, L // tl),
        in_specs=[pl.BlockSpec((1, tl, D), lambda j, i: (1, i, 0)), pl.BlockSpec((1024, D), lambda j, i: (j, 0)),
                  blk, zblk, pl.BlockSpec((1, 1024), lambda j, i: (0, j)), pl.BlockSpec(memory_space=pl.ANY),
                  pl.BlockSpec(memory_space=pl.ANY)],
        out_specs=(blk, zblk, pl.BlockSpec((8, 1024), lambda j, i: (0, j))),
        out_shape=(jax.ShapeDtypeStruct((L, INNER), F32), jax.ShapeDtypeStruct(dproj.shape, BF16),
                   jax.ShapeDtypeStruct((8, INNER), F32)),
        input_output_aliases={5: 1}, name="gnorm_bwd", compiler_params=_cp(("parallel", "arbitrary")),
    )(dbr, wb, y, proj, w.reshape(1, INNER), dproj, dep)


def _merge_fwd(proj, bg, br_a, br_b, tl=256):
    L = proj.shape[0]
    goff = C_GATE // 1024

    def kern(g1_ref, g2_ref, b1_ref, b2_ref, a_ref, b_ref, o_ref):
        g1 = _sigmoid(g1_ref[...].astype(F32) + b1_ref[...])
        g2 = _sigmoid(g2_ref[...].astype(F32) + b2_ref[...])
        o_ref[...] = (g1 * a_ref[...] + g2 * b_ref[...]).astype(BF16)

    row = pl.BlockSpec((tl, 1024), lambda i: (i, 0))
    bg2 = bg.reshape(1, 2 * D)
    return pl.pallas_call(
        kern, grid=(L // tl,),
        in_specs=[pl.BlockSpec((tl, 1024), lambda i: (i, goff)), pl.BlockSpec((tl, 1024), lambda i: (i, goff + 1)),
                  pl.BlockSpec((1, 1024), lambda i: (0, 0)), pl.BlockSpec((1, 1024), lambda i: (0, 1)), row, row],
        out_specs=row, out_shape=jax.ShapeDtypeStruct((L, D), BF16), name="merge_fwd",
        compiler_params=_cp(("parallel",)),
    )(proj, proj, bg2, bg2, br_a, br_b)


def _branch_ssm_merge(yb, wb, proj, bg, br_a, tm=512):
    L, K = yb.shape
    tm = min(tm, L)
    goff = C_GATE // 1024

    def kern(a_ref, b_ref, g1_ref, g2_ref, b1_ref, b2_ref, bra_ref, brb_ref, m_ref):
        brb = lax.dot_general(a_ref[...], b_ref[...], _DIMS["nn"], preferred_element_type=F32)
        brb_ref[...] = brb
        g1 = _sigmoid(g1_ref[...].astype(F32) + b1_ref[...])
        g2 = _sigmoid(g2_ref[...].astype(F32) + b2_ref[...])
        m_ref[...] = (g1 * bra_ref[...] + g2 * brb).astype(BF16)

    row = pl.BlockSpec((tm, D), lambda i: (i, 0))
    bg2 = bg.reshape(1, 2 * D)
    return pl.pallas_call(
        kern, grid=(L // tm,),
        in_specs=[pl.BlockSpec((tm, K), lambda i: (i, 0)), pl.BlockSpec((K, D), lambda i: (0, 0)),
                  pl.BlockSpec((tm, D), lambda i: (i, goff)), pl.BlockSpec((tm, D), lambda i: (i, goff + 1)),
                  pl.BlockSpec((1, D), lambda i: (0, 0)), pl.BlockSpec((1, D), lambda i: (0, 1)), row],
        out_specs=(row, row), out_shape=(jax.ShapeDtypeStruct((L, D), F32), jax.ShapeDtypeStruct((L, D), BF16)),
        name="branch_ssm_merge", compiler_params=_cp(("parallel",)),
    )(yb, wb, proj, proj, bg2, bg2, br_a)


def _merge_bwd(dx1, wo, proj, bg, br_a, br_b, dproj, tl=512):
    L = proj.shape[0]
    tl = min(tl, L)
    goff = C_GATE // 1024

    def kern(dm_ref, wo_ref, g_ref, b_ref, a_ref, bb_ref, _, dbr_ref, dp_ref, gb_ref):
        j = pl.program_id(0)

        @pl.when(pl.program_id(1) == 0)
        def _():
            gb_ref[...] = jnp.zeros_like(gb_ref)

        g = _sigmoid(g_ref[...].astype(F32) + b_ref[...])
        br = jnp.where(j == 0, a_ref[...], bb_ref[...])
        dmv = lax.dot_general(dm_ref[...].astype(BF16), wo_ref[...], _DIMS["nt"], preferred_element_type=F32)
        dbr_ref[0] = (dmv * g).astype(BF16)
        dgate = dmv * br * g * (1.0 - g)
        gb_ref[...] += jnp.broadcast_to(jnp.sum(dgate, axis=0, keepdims=True), (8, 1024))
        dp_ref[...] = dgate.astype(BF16)

    row = pl.BlockSpec((tl, 1024), lambda j, i: (i, 0))
    gblk = pl.BlockSpec((tl, 1024), lambda j, i: (i, goff + j))
    return pl.pallas_call(
        kern, grid=(2, L // tl),
        in_specs=[row, pl.BlockSpec((D, D), lambda j, i: (0, 0)), gblk, pl.BlockSpec((1, 1024), lambda j, i: (0, j)),
                  row, row, pl.BlockSpec(memory_space=pl.ANY)],
        out_specs=(pl.BlockSpec((1, tl, 1024), lambda j, i: (j, i, 0)), gblk, pl.BlockSpec((8, 1024), lambda j, i: (0, j))),
        out_shape=(jax.ShapeDtypeStruct((2, L, D), BF16), jax.ShapeDtypeStruct(dproj.shape, BF16),
                   jax.ShapeDtypeStruct((8, 2 * D), F32)),
        input_output_aliases={6: 1}, name="merge_bwd", compiler_params=_cp(("parallel", "arbitrary")),
    )(dx1, wo, proj, bg.reshape(1, 2 * D), br_a, br_b, dproj)


def _coords():
    return lax.axis_index("x"), lax.axis_index("y"), lax.axis_index("c")


def _other_chips(sk):
    xk, yk = sk // 2, sk % 2
    return [((1 - xk, yk), 2 * (1 - xk) + yk), ((xk, 1 - yk), 2 * xk + 1 - yk), ((1 - xk, 1 - yk), 2 * (1 - xk) + 1 - yk)]


def _rows(start, size):
    assert size % 128 == 0
    return pl.ds(pl.multiple_of(start, 128), size)


def _per_chip(fn):
    x, y, _ = _coords()
    s = 2 * x + y
    for sk in range(4):
        pl.when(s == sk)(functools.partial(fn, sk))


XTRA = PIECE - PMAIN


def _place(shard, full_shape, block, index_map, idx, name, blk0=0, nblk=None, dep=None, into=None):
    in_block = block[-2:]
    if nblk is None:
        nblk = shard.shape[0] // in_block[0]

    def kern(idx_ref, s_ref, *rest):
        o_ref = rest[-1]
        o_ref[...] = s_ref[...].astype(BF16).reshape(o_ref.shape)

    extra = ([dep] if dep is not None else []) + ([into] if into is not None else [])
    grid_spec = pltpu.PrefetchScalarGridSpec(
        num_scalar_prefetch=1, grid=(nblk,),
        in_specs=[pl.BlockSpec(in_block, lambda i, idx_ref: (blk0 + i, 0))] + [_ANY] * len(extra),
        out_specs=pl.BlockSpec(block, index_map))
    aliases = {1 + len(extra): 0} if into is not None else {}
    return pl.pallas_call(kern, grid_spec=grid_spec, out_shape=jax.ShapeDtypeStruct(full_shape, BF16), name=name,
                          input_output_aliases=aliases, compiler_params=_cp(("arbitrary",)))(idx, shard, *extra)


_SEM = pl.BlockSpec(memory_space=pltpu.SEMAPHORE)
_EFFECT = pltpu.SideEffectType.DATAFLOW_SIDE_EFFECTING


_ANY = pl.BlockSpec(memory_space=pl.ANY)


def _tie(v, dep, name):
    def body(v_ref, dep_ref, o_ref):
        del v_ref, dep_ref, o_ref

    return pl.pallas_call(body, out_shape=jax.ShapeDtypeStruct(v.shape, v.dtype), in_specs=[_ANY, _ANY],
                          out_specs=_ANY, input_output_aliases={0: 0}, name=name)(v, dep)


def _split_call(name, arrays, start=None, wait=None, wait_sems=None, after=None):
    keys = list(arrays)
    n = len(keys)
    n_start = start.n if start is not None else 0
    afters = [] if after is None else (list(after) if isinstance(after, (list, tuple)) else [after])

    def body(*refs):
        pos = n
        if wait is not None:
            wss, wrs = refs[pos], refs[pos + 1]
            pos += 2
        pos += len(afters)
        if start is not None:
            nss, nrs = refs[pos], refs[pos + 1]
            pos += 2
        R = dict(zip(keys, refs[pos:pos + n]))
        token = refs[pos + n]
        x, y, c = _coords()

        def desc(src, dst, dev, ss, rs, k):
            return pltpu.make_async_remote_copy(src_ref=src, dst_ref=dst, send_sem=ss.at[k], recv_sem=rs.at[k],
                                                device_id=dev, device_id_type=MESH)

        def run(sk):
            if wait is not None:
                for k, (snd, land) in enumerate(wait.copies(sk, R)):
                    if snd is not None:
                        desc(snd[0], snd[1], snd[2], wss, wrs, k).wait_send()
                    if land is not None:
                        desc(land, land, (x, y, c), wss, wrs, k).wait_recv()
            if start is not None:
                for k, (snd, land) in enumerate(start.copies(sk, R)):
                    if snd is not None:
                        desc(snd[0], snd[1], snd[2], nss, nrs, k).start()

        _per_chip(run)
        token[...] = jnp.zeros_like(token)

    hbm = pl.BlockSpec(memory_space=HBM)
    vals = [arrays[k] for k in keys]
    ins, in_specs = list(vals), [hbm] * n
    if wait is not None:
        ins += list(wait_sems)
        in_specs += [_SEM, _SEM]
    ins += afters
    in_specs += [pl.BlockSpec(memory_space=pl.ANY)] * len(afters)
    out_shape, out_specs = [], []
    if start is not None:
        out_shape += [pltpu.SemaphoreType.DMA((n_start,)), pltpu.SemaphoreType.DMA((n_start,))]
        out_specs += [_SEM, _SEM]
    first = len(out_shape)
    out_shape += [jax.ShapeDtypeStruct(v.shape, v.dtype) for v in vals] + [jax.ShapeDtypeStruct((8, 128), F32)]
    out_specs += [hbm] * n + [pl.BlockSpec(memory_space=pltpu.VMEM)]
    res = pl.pallas_call(
        body, out_shape=tuple(out_shape), in_specs=in_specs, out_specs=tuple(out_specs),
        input_output_aliases={i: first + i for i in range(n)}, name=name,
        compiler_params=pltpu.CompilerParams(has_side_effects=_EFFECT),
    )(*ins)
    sems = (res[0], res[1]) if start is not None else None
    return dict(zip(keys, res[first:first + n])), sems, res[-1]


class _Plan:
    def __init__(self, n, copies):
        self.n, self.copies = n, copies


_HM, _HX = PMAIN // 2, XTRA // 2
WAVE0 = 768
WAVES = ((0, WAVE0), (WAVE0, _HM - WAVE0))
_WIN = {
    "wq0": (True, "wct", lambda r, sc, hc: r.at[_rows(PMAIN * sc + _HM * hc + WAVES[0][0], WAVES[0][1]), :]),
    "wq1": (True, "wct", lambda r, sc, hc: r.at[_rows(PMAIN * sc + _HM * hc + WAVES[1][0], WAVES[1][1]), :]),
    "xt": (True, "xt", lambda r, sc, hc: r.at[sc, _rows(_HX * hc, _HX), :]),
    "w1": (True, "w1", lambda r, sc, hc: r.at[_rows(512 * hc, 512), pl.ds(1024 * sc, 1024)]),
    "w2": (True, "w2", lambda r, sc, hc: r.at[_rows(1024 * sc + 512 * hc, 512), :]),
    "wa": (True, "wa", lambda r, sc, hc: r.at[_rows(256 * sc + 128 * hc, 128), :]),
    "wb": (True, "wb", lambda r, sc, hc: r.at[_rows(512 * sc + 256 * hc, 256), :]),
    "wo": (True, "wo", lambda r, sc, hc: r.at[_rows(256 * sc + 128 * hc, 128), :]),
    "cw": (False, "cw", lambda r, sc, hc: r.at[sc]),
}


_PIECE_SRC = {
    "wq0": lambda p, hc: p.at[_rows(_HM * hc + WAVES[0][0], WAVES[0][1]), :],
    "wq1": lambda p, hc: p.at[_rows(_HM * hc + WAVES[1][0], WAVES[1][1]), :],
    "xt": lambda p, hc: p.at[_rows(PMAIN + _HX * hc, _HX), :],
}


def _ag_chips_plan(keys):
    def copies(sk, R):
        _, _, c = _coords()
        out = []
        for key in keys:
            _, arr, win = _WIN[key]
            for (px, py), ps in _other_chips(sk):
                dst = win(R[arr], sk, c)
                src = _PIECE_SRC[key](R["piece"], c) if key in _PIECE_SRC else dst
                out.append(((src, dst, (px, py, c)), win(R[arr], ps, c)))
        return out
    return _Plan(3 * len(keys), copies)


def _ag_sibling_plan(keys):
    keys = [k for k in keys if _WIN[k][0]]

    def copies(sk, R):
        x, y, c = _coords()
        out = []
        for key in keys:
            _, arr, win = _WIN[key]
            for _, ps in _other_chips(sk):
                w = win(R[arr], ps, c)
                out.append(((w, w, (x, y, 1 - c)), win(R[arr], ps, 1 - c)))
        return out
    return _Plan(3 * len(keys), copies)


def _in_proj_wave(h, wct, wave, proj=None, tm=2048):
    L = h.shape[0]
    tm = min(tm, L)
    off, size = WAVES[wave]
    start = lambda j: pl.multiple_of(_HM * j + off, 128)

    def kern(h_ref, w_ref, *rest):
        o_ref = rest[-1]
        o_ref[...] = lax.dot_general(h_ref[...], w_ref[...], _DIMS["nt"], preferred_element_type=F32).astype(BF16)

    in_specs = [pl.BlockSpec((tm, D), lambda j, i: (i, 0)),
                pl.BlockSpec((pl.Element(size), pl.Element(D)), lambda j, i: (start(j), 0))]
    args, aliases = [h, wct], {}
    if proj is not None:
        in_specs.append(pl.BlockSpec(memory_space=pl.ANY))
        args.append(proj)
        aliases = {2: 0}
    return pl.pallas_call(
        kern, grid=(8, L // tm), in_specs=in_specs,
        out_specs=pl.BlockSpec((pl.Element(tm), pl.Element(size)), lambda j, i: (i * tm, start(j))),
        out_shape=jax.ShapeDtypeStruct((L, NCW), BF16), input_output_aliases=aliases,
        name="in_proj_wave%d" % wave, compiler_params=_cp(("parallel", "parallel")),
    )(*args)


def _fix_wct(wct, xt):
    nb = PMAIN // XTRA

    def kern(w_ref, x_ref, o_ref):
        k = pl.program_id(0)
        xv = x_ref[0]
        o_ref[...] = jnp.where(k < 3, (w_ref[...].astype(F32) + xv.astype(F32)).astype(BF16), xv)

    blk = pl.BlockSpec((XTRA, D), lambda k: (nb * (k + 1), 0))
    rblk = pl.BlockSpec((XTRA, D), lambda k: (jnp.where(k < 3, nb * (k + 1), 0), 0))
    return pl.pallas_call(
        kern, grid=(4,), in_specs=[rblk, pl.BlockSpec((1, XTRA, D), lambda k: (k, 0, 0))], out_specs=blk,
        out_shape=jax.ShapeDtypeStruct(wct.shape, BF16), input_output_aliases={0: 0}, name="fix_wct",
        compiler_params=_cp(("arbitrary",)),
    )(wct, xt)


_HP = PIECE // 2
_GWIN = [
    lambda r, sc, hc: r.at[_rows(PMAIN * sc + _HP * hc, _HP), :],
    lambda r, sc, hc: r.at[_rows(512 * hc, 512), pl.ds(1024 * sc, 1024)],
    lambda r, sc, hc: r.at[_rows(1024 * sc + 512 * hc, 512), :],
    lambda r, sc, hc: r.at[_rows(256 * sc + 128 * hc, 128), :],
    lambda r, sc, hc: r.at[_rows(512 * sc + 256 * hc, 256), :],
    lambda r, sc, hc: r.at[_rows(256 * sc + 128 * hc, 128), :],
]
HALF_SHAPES = [(PIECE // 2, D), (512, 1024), (512, 1024), (128, 1024), (256, 1024), (128, 1024)]


def _rs_sibling_plan(ts):
    def copies(sk, R):
        x, y, c = _coords()
        out = []
        for t in ts:
            for sc in range(4):
                land = R["ra%d" % t].at[sc]
                out.append(((_GWIN[t](R["g%d" % t], sc, 1 - c), land, (x, y, 1 - c)), land))
        return out
    return _Plan(4 * len(ts), copies)


def _rs_chips_plan(ts):
    def copies(sk, R):
        _, _, c = _coords()
        out = []
        for t in ts:
            for j, ((px, py), ps) in enumerate(_other_chips(sk)):
                land = R["rb%d" % t].at[j]
                out.append(((R["hb%d" % t].at[ps], land, (px, py, c)), land))
        return out
    return _Plan(3 * len(ts), copies)


def _rs_share_plan(ts):
    def copies(sk, R):
        x, y, c = _coords()
        out = []
        for t in ts:
            rows = HALF_SHAPES[t][0]
            mine = R["f%d" % t].at[_rows(rows * c, rows), :]
            out.append(((mine, mine, (x, y, 1 - c)), R["f%d" % t].at[_rows(rows * (1 - c), rows), :]))
        return out
    return _Plan(len(ts), copies)


def _half_tiling(t):
    rows, cols = HALF_SHAPES[t]
    if t == 0:
        return (rows // 2, cols), 2, lambda i: (i, 0)
    return (rows, cols), 1, lambda i: (0, 0)


def _window_spec(t, blk):
    if t == 0:
        return pl.BlockSpec((pl.Element(blk[0]), pl.Element(blk[1])), lambda i, sc, idx_ref: (
            pl.multiple_of(PMAIN * sc + _HP * idx_ref[1] + blk[0] * i, 128), 0))
    if t == 1:
        return pl.BlockSpec(blk, lambda i, sc, idx_ref: (idx_ref[1], sc))
    return pl.BlockSpec(blk, lambda i, sc, idx_ref: (2 * sc + idx_ref[1], 0))


def _chip_sum(g, ra, t, idx, name):
    rows, cols = HALF_SHAPES[t]
    blk, nblk, inner = _half_tiling(t)

    def kern(idx_ref, g_ref, r_ref, hb_ref, hf_ref):
        v = g_ref[...].astype(F32) + r_ref[0].astype(F32)
        hb_ref[0] = v.astype(BF16)

        @pl.when(pl.program_id(1) == idx_ref[0])
        def _():
            hf_ref[...] = v

    omap = lambda i, sc, idx_ref: (sc,) + inner(i)
    grid_spec = pltpu.PrefetchScalarGridSpec(
        num_scalar_prefetch=1, grid=(nblk, 4),
        in_specs=[_window_spec(t, blk), pl.BlockSpec((1,) + blk, omap)],
        out_specs=(pl.BlockSpec((1,) + blk, omap), pl.BlockSpec(blk, lambda i, sc, idx_ref: inner(i))))
    return pl.pallas_call(
        kern, grid_spec=grid_spec,
        out_shape=(jax.ShapeDtypeStruct((4, rows, cols), BF16), jax.ShapeDtypeStruct((rows, cols), F32)),
        name=name, compiler_params=_cp(("parallel", "arbitrary")),
    )(idx, g, ra)


def _final_sum(hf, rb, t, idx, name):
    rows, cols = HALF_SHAPES[t]
    blk, nblk, inner = _half_tiling(t)
    nbr = rows // blk[0]

    def kern(idx_ref, h_ref, r_ref, o_ref):
        o_ref[...] = ((h_ref[...] + r_ref[0].astype(F32)) + r_ref[1].astype(F32)) + r_ref[2].astype(F32)

    def omap(i, idx_ref):
        r, cidx = inner(i)
        return nbr * idx_ref[1] + r, cidx

    grid_spec = pltpu.PrefetchScalarGridSpec(
        num_scalar_prefetch=1, grid=(nblk,),
        in_specs=[pl.BlockSpec(blk, lambda i, idx_ref: inner(i)),
                  pl.BlockSpec((3,) + blk, lambda i, idx_ref: (0,) + inner(i))],
        out_specs=pl.BlockSpec(blk, omap))
    return pl.pallas_call(
        kern, grid_spec=grid_spec, out_shape=jax.ShapeDtypeStruct((2 * rows, cols), F32),
        name=name, compiler_params=_cp(("parallel",)),
    )(idx, hf, rb)


class _ReduceScatter:
    def __init__(self, ts, grads, idx, tag):
        self.ts, self.idx, self.tag = ts, idx, tag
        arr = {}
        for t in ts:
            arr["g%d" % t] = grads[t]
            arr["ra%d" % t] = lax.empty((4,) + HALF_SHAPES[t], BF16)
        self.plan = _rs_sibling_plan(ts)
        self.arr, self.sems, self.token = _split_call("rs_sibling_start_" + tag, arr, start=self.plan)

    def chips(self, after):
        arr, _, _ = _split_call("rs_sibling_wait_" + self.tag, self.arr, wait=self.plan, wait_sems=self.sems, after=after)
        brr, self.hf = {}, {}
        for t in self.ts:
            hb, self.hf[t] = _chip_sum(arr["g%d" % t], arr["ra%d" % t], t, self.idx, "chip_sum_%d" % t)
            brr["hb%d" % t] = hb
            brr["rb%d" % t] = lax.empty((3,) + HALF_SHAPES[t], BF16)
        self.plan = _rs_chips_plan(self.ts)
        self.arr, self.sems, self.token = _split_call("rs_chips_start_" + self.tag, brr, start=self.plan)
        return self.token

    def share(self, after):
        brr, _, _ = _split_call("rs_chips_wait_" + self.tag, self.arr, wait=self.plan, wait_sems=self.sems, after=after)
        frr = {"f%d" % t: _final_sum(self.hf[t], brr["rb%d" % t], t, self.idx, "final_sum_%d" % t) for t in self.ts}
        self.plan = _rs_share_plan(self.ts)
        self.arr, self.sems, self.token = _split_call("rs_share_start_" + self.tag, frr, start=self.plan)
        return self.token

    def result(self, after):
        frr, _, _ = _split_call("rs_share_wait_" + self.tag, self.arr, wait=self.plan, wait_sems=self.sems, after=after)
        return {t: frr["f%d" % t] for t in self.ts}


def _all8_plan(key):
    def copies(sk, R):
        x, y, c = _coords()
        own = R[key].at[4 * x + 2 * y + c]
        out = []
        for k in range(1, 8):
            dev = ((1 - x) if (k >> 2) & 1 else x, (1 - y) if (k >> 1) & 1 else y, (1 - c) if k & 1 else c)
            out.append(((own, own, dev), R[key].at[4 * dev[0] + 2 * dev[1] + dev[2]]))
        return out
    return _Plan(7, copies)


def _sum8(v, name="small_sum"):
    def kern(v_ref, o_ref):
        acc = v_ref[0]
        for k in range(1, 8):
            acc = acc + v_ref[k]
        o_ref[...] = acc

    return pl.pallas_call(kern, out_shape=jax.ShapeDtypeStruct(v.shape[1:], F32), name=name)(v)


def _adamw(w, g, m, v, name, tr=128, blk0=0, nblk=None, into=None, copy_g=False):
    R, C = w.shape
    tr = min(tr, R)
    if nblk is None:
        assert R % tr == 0 and blk0 == 0
        nblk = R // tr
    n_out = 4 if copy_g else 3

    def kern(*refs):
        w_ref, g_ref, m_ref, v_ref = refs[:4]
        d_ref, mo_ref, vo_ref = refs[-n_out:][:3]
        gv = g_ref[...]
        mn = ADAM_B1 * m_ref[...] + (1.0 - ADAM_B1) * gv
        vn = ADAM_B2 * v_ref[...] + (1.0 - ADAM_B2) * (gv * gv)
        m_hat = mn / (1.0 - ADAM_B1 ** ADAM_STEP)
        v_hat = vn / (1.0 - ADAM_B2 ** ADAM_STEP)
        d_ref[...] = -ADAM_LR * (m_hat / (jnp.sqrt(v_hat) + ADAM_EPS) + ADAM_WD * w_ref[...])
        mo_ref[...] = mn
        vo_ref[...] = vn
        if copy_g:
            refs[-1][...] = gv

    blk = pl.BlockSpec((tr, C), lambda i: (blk0 + i, 0))
    sd = jax.ShapeDtypeStruct((R, C), F32)
    in_specs, args, aliases = [blk] * 4, [w, g, m, v], {}
    if into is not None:
        in_specs += [pl.BlockSpec(memory_space=pl.ANY)] * 3
        args += list(into)
        aliases = {4: 0, 5: 1, 6: 2}
    return pl.pallas_call(kern, grid=(nblk,), in_specs=in_specs, out_specs=(blk,) * n_out, out_shape=(sd,) * n_out,
                          input_output_aliases=aliases, name=name, compiler_params=_cp(("parallel",)))(*args)


def _adamw_w_in(wt, gp, mt, vt, offs, name, r0, tr, nblk, views, into=None, blk_key=None):
    el = lambda n: (pl.Element(n), pl.Element(D))
    first = (lambda o: r0) if blk_key is None else (lambda o: tr * o[blk_key])
    own = pl.BlockSpec(el(tr), lambda i, o: (pl.multiple_of(first(o) + tr * i, 8), 0))

    def view(k):
        return pl.BlockSpec(el(tr), lambda i, o: (pl.multiple_of(jnp.maximum(first(o) + tr * i + o[k], 0), 8), 0))

    def kern(o_ref, w_ref, m_ref, v_ref, *refs):
        g_refs, (d_ref, mo_ref, vo_ref, go_ref) = refs[:len(views)], refs[-4:]
        gv = g_refs[0][...]
        if len(views) == 2:
            row = first(o_ref) + tr * pl.program_id(0) + lax.broadcasted_iota(jnp.int32, (tr, D), 0)
            gv = jnp.where(row < o_ref[2], gv, g_refs[1][...])
        mn = ADAM_B1 * m_ref[...] + (1.0 - ADAM_B1) * gv
        vn = ADAM_B2 * v_ref[...] + (1.0 - ADAM_B2) * (gv * gv)
        m_hat = mn / (1.0 - ADAM_B1 ** ADAM_STEP)
        v_hat = vn / (1.0 - ADAM_B2 ** ADAM_STEP)
        d_ref[...] = -ADAM_LR * (m_hat / (jnp.sqrt(v_hat) + ADAM_EPS) + ADAM_WD * w_ref[...])
        mo_ref[...] = mn
        vo_ref[...] = vn
        go_ref[...] = gv

    in_specs = [own, own, own] + [view(k) for k in views]
    args = [wt, mt, vt] + [gp] * len(views)
    aliases = {}
    if into is not None:
        in_specs += [pl.BlockSpec(memory_space=pl.ANY)] * 4
        args += list(into)
        aliases = {1 + len(args) - 4 + j: j for j in range(4)}
    grid_spec = pltpu.PrefetchScalarGridSpec(num_scalar_prefetch=1, grid=(nblk,), in_specs=in_specs,
                                             out_specs=(own,) * 4)
    sd = jax.ShapeDtypeStruct(wt.shape, F32)
    return pl.pallas_call(kern, grid_spec=grid_spec, out_shape=(sd,) * 4, input_output_aliases=aliases, name=name,
                          compiler_params=_cp(("parallel",)))(offs, *args)


def _to_piece(wt, s):
    z = lambda n: jnp.zeros((n, D), wt.dtype)
    pads = [functools.partial(lambda k, w: jnp.pad(w, ((8 * k, PIECE - W_SHARD - 8 * k), (0, 0))).astype(BF16), k)
            for k in range(3)]
    last = lambda w: jnp.concatenate([z(24), w[:744], w[776:], w[744:776], z(PIECE - 24 - W_SHARD)], axis=0).astype(BF16)
    return lax.switch(s, pads + [last], wt)


_SMALL = [("b_gate", 2048), ("ssm_conv_b", 4096), ("dt_bias", 32), ("A_log", 32), ("D_skip", 32),
          ("ssm_norm_w", 2048), ("norm_mlp", 1024), ("norm_final", 1024), ("sc_conv_w", 3072), ("ssm_conv_w", 16384),
          ("loss", 1)]


def _pack(vals, table, rows):
    parts = []
    for name, n in table:
        v = vals[name].reshape(-1).astype(F32)
        pad = (-n) % 128
        parts.append(jnp.pad(v, (0, pad)) if pad else v)
    flat = jnp.concatenate(parts)
    return jnp.pad(flat, (0, rows * 128 - flat.shape[0])).reshape(rows, 128)


def _unpack(arr, table):
    flat = arr.reshape(-1)
    out, off = {}, 0
    for name, n in table:
        out[name] = flat[off:off + n]
        off += n + ((-n) % 128)
    return out


def kernel(x, norm_mix, w_in, b_gate, sc_conv_w, ssm_conv_w, ssm_conv_b, dt_bias, A_log, D_skip, ssm_norm_w, w_branch_sc, w_branch_ssm, w_out, norm_mlp, w_mlp1, w_mlp2, norm_final, loss_target, m_norm_mix, m_w_in, m_b_gate, m_sc_conv_w, m_ssm_conv_w, m_ssm_conv_b, m_dt_bias, m_A_log, m_D_skip, m_ssm_norm_w, m_w_branch_sc, m_w_branch_ssm, m_w_out, m_norm_mlp, m_w_mlp1, m_w_mlp2, m_norm_final, v_norm_mix, v_w_in, v_b_gate, v_sc_conv_w, v_ssm_conv_w, v_ssm_conv_b, v_dt_bias, v_A_log, v_D_skip, v_ssm_norm_w, v_w_branch_sc, v_w_branch_ssm, v_w_out, v_norm_mlp, v_w_mlp1, v_w_mlp2, v_norm_final):
    L = x.shape[1]
    nc = L // Q
    xi, yi, ci = lax.axis_index("x"), lax.axis_index("y"), lax.axis_index("c")
    s = 2 * xi + yi
    idx = jnp.stack([s, ci]).astype(jnp.int32)
    x0 = x.reshape(L, D)
    tgt = loss_target.reshape(L, D)
    small_names = ["b_gate", "sc_conv_w", "ssm_conv_w", "ssm_conv_b", "dt_bias", "A_log", "D_skip", "ssm_norm_w",
                   "norm_mlp", "norm_final"]
    small_wmv = [dict(zip(small_names, vals)) for vals in (
        (b_gate, sc_conv_w, ssm_conv_w, ssm_conv_b, dt_bias, A_log, D_skip, ssm_norm_w, norm_mlp, norm_final),
        (m_b_gate, m_sc_conv_w, m_ssm_conv_w, m_ssm_conv_b, m_dt_bias, m_A_log, m_D_skip, m_ssm_norm_w, m_norm_mlp,
         m_norm_final),
        (v_b_gate, v_sc_conv_w, v_ssm_conv_w, v_ssm_conv_b, v_dt_bias, v_A_log, v_D_skip, v_ssm_norm_w, v_norm_mlp,
         v_norm_final))]
    small_table = [(n, int(small_wmv[0][n].size)) for n in small_names]
    small_rows = 136
    pk_w, pk_m, pk_v = [_pack(d, small_table, small_rows) for d in small_wmv]

    piece = _to_piece(w_in.T, s)
    nb = PMAIN // XTRA
    cws = jnp.zeros((8, 1280), F32)
    cws = cws.at[0:3, 0:256].set(sc_conv_w).at[0:4, 256:1280].set(ssm_conv_w)
    cw0 = lax.dynamic_update_slice(jnp.zeros((4, 8, 1280), F32), cws[None], (s, 0, 0))
    win_keys, win2_keys, mid_keys, end_keys = ["xt", "cw", "wq0"], ["wq1"], ["wa", "wb", "wo", "w1"], ["w2"]
    gw, sems_w, tok = _split_call(
        "ag_win_start", {"wct": lax.empty((NCW, D), BF16), "xt": lax.empty((4, XTRA, D), BF16), "cw": cw0, "piece": piece},
        start=_ag_chips_plan(win_keys))
    g2, sems_w2, tok = _split_call("ag_win2_start", {"wct": gw["wct"], "piece": gw["piece"]},
                                   start=_ag_chips_plan(win2_keys), after=tok)
    piece = g2["piece"]
    gw["wct"] = _place(piece, (NCW, D), (XTRA, D), lambda i, r: (nb * r[0] + i, 0), idx, "place_wct", nblk=nb,
                       dep=tok, into=g2["wct"])
    gw["xt"] = _place(piece, (4, XTRA, D), (1, XTRA, D), lambda i, r: (r[0], 0, 0), idx, "place_xt", blk0=nb, nblk=1,
                      dep=tok, into=gw["xt"])
    gw["piece"] = piece
    wa0 = _place(w_branch_sc, (D, D), (256, 1024), lambda i, r: (r[0], 0), idx, "place_wa", dep=tok)
    wb0 = _place(w_branch_ssm, (INNER, D), (512, 1024), lambda i, r: (r[0], 0), idx, "place_wb", dep=tok)
    wo0 = _place(w_out, (D, D), (256, 1024), lambda i, r: (r[0], 0), idx, "place_wo", dep=tok)
    w10 = _place(w_mlp1, (D, DFF), (256, 1024), lambda i, r: (i, r[0]), idx, "place_w1", dep=tok)
    gm, sems_m, tok = _split_call("ag_mid_start", {"wa": wa0, "wb": wb0, "wo": wo0, "w1": w10},
                                  start=_ag_chips_plan(mid_keys))
    w20 = _place(w_mlp2, (DFF, D), (256, 1024), lambda i, r: (4 * r[0] + i, 0), idx, "place_w2", dep=tok)
    ge, sems_e, tok = _split_call("ag_end_start", {"w2": w20}, start=_ag_chips_plan(end_keys))
    h = _rms_fwd(x0, norm_mix, "rms_mix", dep=tok)
    gw, sems_w, tok = _split_call("ag_win_pass", gw, wait=_ag_chips_plan(win_keys), wait_sems=sems_w,
                                  start=_ag_sibling_plan(win_keys), after=[h, pk_w, pk_m, pk_v])
    gw, _, _ = _split_call("ag_win_done", gw, wait=_ag_sibling_plan(win_keys), wait_sems=sems_w, after=tok)
    wc, cw_all = _fix_wct(gw["wct"], gw["xt"]), gw["cw"]
    sc_w_full = jnp.concatenate([cw_all[k, :, 0:256] for k in range(4)], axis=1)
    ssm_w_full = jnp.concatenate([cw_all[k, :, 256:1280] for k in range(4)], axis=1)
    cw4 = ssm_w_full.at[4].set(ssm_conv_b)
    vec = jnp.zeros((8, 128), F32).at[0, :NH].set(dt_bias).at[1, :NH].set(A_log)
    vecg = jnp.zeros((NG, 8, 128), F32).at[:, 0, :4].set(A_log.reshape(NG, 4)).at[:, 1, :4].set(D_skip.reshape(NG, 4))

    dtraw = _matmul(h, wc[C_DT:], "nt", F32, 512, 256, 1024, "in_proj_dt")
    proj = _in_proj_wave(h, wc, 0)
    g2, sems_w2, tok = _split_call("ag_win2_pass", {"wct": wc, "piece": gw["piece"]},
                                   wait=_ag_chips_plan(win2_keys), wait_sems=sems_w2,
                                   start=_ag_sibling_plan(win2_keys), after=[proj, dtraw])
    g2, _, _ = _split_call("ag_win2_done", g2, wait=_ag_sibling_plan(win2_keys), wait_sems=sems_w2, after=tok)
    wc = g2["wct"]
    proj = _in_proj_wave(h, wc, 1, proj=proj)
    ya = _sc_fwd(proj, sc_w_full)
    xbc = _ssm_conv_fwd(proj, cw4)
    dt4, cs4, sg4 = _dt_prep(dtraw, vec)
    y, s_all = _ssd_fwd(xbc, dt4, cs4, vecg)
    gm, sems_m, tok = _split_call("ag_mid_pass", gm, wait=_ag_chips_plan(mid_keys), wait_sems=sems_m,
                                  start=_ag_sibling_plan(mid_keys), after=[y, ya])
    y = _tie(y, tok, "tie_y")
    yb = _gnorm_fwd(y, proj, ssm_norm_w)
    gm, _, _ = _split_call("ag_mid_done", gm, wait=_ag_sibling_plan(mid_keys), wait_sems=sems_m, after=yb)
    wa, wb, wo, w1 = gm["wa"], gm["wb"], gm["wo"], gm["w1"]
    ge, sems_e, tok = _split_call("ag_end_pass", ge, wait=_ag_chips_plan(end_keys), wait_sems=sems_e,
                                  start=_ag_sibling_plan(end_keys), after=yb)
    br_a = _matmul(ya, wa, "nn", F32, 1024, 1024, 1024, "branch_sc", dep=tok)
    br_b, merged = _branch_ssm_merge(yb, wb, proj, b_gate, br_a)
    x1, h2 = _matmul_res_rms(merged, wo, x0, norm_mlp, 1024, "out_proj")
    a1, rl = _matmul(h2, w1, "nn", BF16, 1024, 1024, 1024, "mlp1", epi="relu2", n_outer=True)
    ge, _, _ = _split_call("ag_end_done", ge, wait=_ag_sibling_plan(end_keys), wait_sems=sems_e, after=a1)
    w2 = ge["w2"]
    dx2, g_nf, loss8 = _matmul_res_final(rl, w2, x1, norm_final, tgt, 512, "mlp2")

    da = _matmul(dx2, w2, "nt", BF16, 1024, 1024, 1024, "mlp2_dx", epi="drelu", extra=a1, n_outer=True)
    g_w2 = _matmul(rl, dx2, "tn", BF16, 1024, 1024, 2048, "mlp2_dw")
    g_w1 = _matmul(h2, da, "tn", BF16, 1024, 1024, 2048, "mlp1_dw")
    dx1, g_nmlp = _matmul_rms_bwd(da, w1, "nt", x1, norm_mlp, dx2, 512, 4096, "mlp1_dx")
    g_wo = _matmul(merged, dx1, "tn", BF16, 1024, 1024, 2048, "out_proj_dw")
    dproj = lax.empty((L, NCW), BF16)
    dbr, dproj, g_bg = _merge_bwd(dx1, wo, proj, b_gate, br_a, br_b, dproj)
    dya = _matmul(dbr[0], wa, "nt", F32, 1024, 1024, 1024, "branch_sc_dx")
    g_wa = _matmul(ya, dbr[0], "tn", BF16, 1024, 1024, 2048, "branch_sc_dw")
    dproj, g_scw = _sc_bwd(dya, proj, sc_w_full, dproj)
    g_wb = _matmul(yb, dbr[1], "tn", BF16, 1024, 1024, 2048, "branch_ssm_dw")
    rs_a = _ReduceScatter([1, 2, 3, 4, 5], {1: g_w1, 2: g_w2, 3: g_wa, 4: g_wb, 5: g_wo}, idx, "a")
    dy, dproj, g_snw = _gnorm_bwd(dbr, wb, y, proj, ssm_norm_w, dproj, rs_a.token)
    tok = rs_a.chips(after=dy)
    dxs, dbm, dcm, ddt_g, st = _ssd_bwd(xbc, dt4, cs4, sg4, vecg, s_all, _tie(dy, tok, "tie_dy"))
    dproj, gx1 = _ssm_conv_bwd(dxs, proj, cw4, dproj, 0, "ssm_conv_bwd_x")
    dproj, gx2 = _ssm_conv_bwd(dbm, proj, cw4, dproj, INNER, "ssm_conv_bwd_b")
    dproj, gx3 = _ssm_conv_bwd(dcm, proj, cw4, dproj, INNER + NG * NS, "ssm_conv_bwd_c")
    g_cw4 = jnp.concatenate([gx1, gx2, gx3], axis=1)
    dproj, g_dtb = _dt_bwd(ddt_g, dproj)
    small = {"b_gate": g_bg[0], "ssm_conv_b": g_cw4[4], "dt_bias": g_dtb[0, :NH],
             "A_log": st[:, 0, :4], "D_skip": st[:, 1, :4], "ssm_norm_w": g_snw[0], "norm_mlp": g_nmlp[0],
             "norm_final": g_nf[0], "sc_conv_w": g_scw[0:3], "ssm_conv_w": g_cw4[0:4], "loss": loss8[0, 0:1]}
    me = 4 * xi + 2 * yi + ci
    sm8 = lax.dynamic_update_slice(jnp.zeros((8, SMALL_ROWS, 128), F32), _pack(small, _SMALL, SMALL_ROWS)[None], (me, 0, 0))
    sm_arr, sm_sems, tok = _split_call("small_start", {"sm": sm8}, start=_all8_plan("sm"))
    g_wc = _matmul(dproj, h, "tn", BF16, 1280, 1024, 2048, "in_proj_dw", dep=tok)
    rs_b = _ReduceScatter([0], {0: g_wc}, idx, "b")
    tok = rs_a.share(after=rs_b.token)
    tok = rs_b.chips(after=tok)
    grad_x, g_nm = _matmul_rms_bwd(dproj, wc, "nn", x0, norm_mix, dx1, 512, 3840, "in_proj_dx", dep=tok)
    nm8 = lax.dynamic_update_slice(jnp.zeros((8, 8, 128), F32), g_nm[0].reshape(1, 8, 128), (me, 0, 0))
    nm_arr, nm_sems, tok = _split_call("norm_mix_start", {"nm": nm8}, start=_all8_plan("nm"))
    sm_arr, _, _ = _split_call("small_wait", sm_arr, wait=_all8_plan("sm"), wait_sems=sm_sems, after=tok)
    small_sum = _sum8(sm_arr["sm"])
    gs = _unpack(small_sum, _SMALL)
    red = rs_a.result(after=tok)
    big = {"w_mlp1": red[1], "w_mlp2": red[2], "w_branch_sc": red[3], "w_branch_ssm": red[4], "w_out": red[5]}

    given = dict(norm_mix=norm_mix, w_in=w_in, b_gate=b_gate, sc_conv_w=sc_conv_w, ssm_conv_w=ssm_conv_w, ssm_conv_b=ssm_conv_b, dt_bias=dt_bias, A_log=A_log, D_skip=D_skip, ssm_norm_w=ssm_norm_w, w_branch_sc=w_branch_sc, w_branch_ssm=w_branch_ssm, w_out=w_out, norm_mlp=norm_mlp, w_mlp1=w_mlp1, w_mlp2=w_mlp2, norm_final=norm_final,
                 m_norm_mix=m_norm_mix, m_w_in=m_w_in, m_b_gate=m_b_gate, m_sc_conv_w=m_sc_conv_w, m_ssm_conv_w=m_ssm_conv_w, m_ssm_conv_b=m_ssm_conv_b, m_dt_bias=m_dt_bias, m_A_log=m_A_log, m_D_skip=m_D_skip, m_ssm_norm_w=m_ssm_norm_w, m_w_branch_sc=m_w_branch_sc, m_w_branch_ssm=m_w_branch_ssm, m_w_out=m_w_out, m_norm_mlp=m_norm_mlp, m_w_mlp1=m_w_mlp1, m_w_mlp2=m_w_mlp2, m_norm_final=m_norm_final,
                 v_norm_mix=v_norm_mix, v_w_in=v_w_in, v_b_gate=v_b_gate, v_sc_conv_w=v_sc_conv_w, v_ssm_conv_w=v_ssm_conv_w, v_ssm_conv_b=v_ssm_conv_b, v_dt_bias=v_dt_bias, v_A_log=v_A_log, v_D_skip=v_D_skip, v_ssm_norm_w=v_ssm_norm_w, v_w_branch_sc=v_w_branch_sc, v_w_branch_ssm=v_w_branch_ssm, v_w_out=v_w_out, v_norm_mlp=v_norm_mlp, v_w_mlp1=v_w_mlp1, v_w_mlp2=v_w_mlp2, v_norm_final=v_norm_final)
    order = ["norm_mix", "w_in", "b_gate", "sc_conv_w", "ssm_conv_w", "ssm_conv_b", "dt_bias", "A_log", "D_skip",
             "ssm_norm_w", "w_branch_sc", "w_branch_ssm", "w_out", "norm_mlp", "w_mlp1", "w_mlp2", "norm_final"]
    grad, delta, new_m, new_v = {}, {}, {}, {}
    for n in big:
        delta[n], new_m[n], new_v[n], grad[n] = _adamw(given[n], big[n], given["m_" + n], given["v_" + n],
                                                       "adamw_" + n, copy_g=True)
    big["w_in"] = None
    grad_small = {n: gs[n].reshape(given[n].shape) for n in small_names if n not in ("sc_conv_w", "ssm_conv_w")}
    grad_small["sc_conv_w"] = lax.dynamic_slice(gs["sc_conv_w"].reshape(3, D), (0, 256 * s), (3, 256))
    grad_small["ssm_conv_w"] = lax.dynamic_slice(gs["ssm_conv_w"].reshape(4, XBC), (0, 1024 * s), (4, 1024))
    table = small_table
    ds_, ms_, vs_ = _adamw(pk_w, _pack(grad_small, table, small_rows), pk_m, pk_v, "adamw_small", tr=small_rows)
    ds_, ms_, vs_ = _unpack(ds_, table), _unpack(ms_, table), _unpack(vs_, table)
    for n in grad_small:
        shp = given[n].shape
        grad[n] = grad_small[n]
        delta[n], new_m[n], new_v[n] = ds_[n].reshape(shp), ms_[n].reshape(shp), vs_[n].reshape(shp)

    done = [new_v[n] for n in ("w_mlp1", "w_mlp2", "w_branch_sc", "w_branch_ssm", "w_out")] + [vs_["b_gate"]]
    tok = rs_b.share(after=done)
    offs = jnp.where(s == 3, jnp.array([24, -8, 744, 2072, -8], jnp.int32),
                     jnp.stack([8 * s, 8 * s, 0 * s, 8 * s, 8 * s]).astype(jnp.int32))
    offs = jnp.concatenate([offs, jnp.stack([7 * ci, 4 - 4 * ci]).astype(jnp.int32)])
    nmain = W_SHARD // 256
    wt_own = (w_in.T, rs_b.arr["f0"], m_w_in.T, v_w_in.T, offs)
    res = _adamw_w_in(*wt_own, "adamw_w_in_own", 0, 256, 4, (0, 1), blk_key=5)
    gp = rs_b.result(after=[tok, res[0]])[0]
    wt_args = (w_in.T, gp, m_w_in.T, v_w_in.T, offs)
    res = _adamw_w_in(*wt_args, "adamw_w_in", 0, 256, nmain - 4, (0, 1), into=res, blk_key=6)
    res = _adamw_w_in(*wt_args, "adamw_w_in_dt", 744, 32, 1, (3,), into=res)
    dt_, mt_, vt_, gwt = _adamw_w_in(*wt_args, "adamw_w_in_tail", 256 * nmain, 8, 1, (4,), into=res)
    grad["w_in"], delta["w_in"], new_m["w_in"], new_v["w_in"] = gwt.T, dt_.T, mt_.T, vt_.T
    nm_arr, _, _ = _split_call("norm_mix_wait", nm_arr, wait=_all8_plan("nm"), wait_sems=nm_sems, after=tok)
    g8 = _sum8(nm_arr["nm"], "norm_mix_sum")
    r8 = lambda a: a.reshape(8, 128)
    d8, m8, v8 = _adamw(r8(norm_mix), g8, r8(m_norm_mix), r8(v_norm_mix), "adamw_norm_mix", tr=8)
    grad["norm_mix"], delta["norm_mix"] = g8.reshape(D), d8.reshape(D)
    new_m["norm_mix"], new_v["norm_mix"] = m8.reshape(D), v8.reshape(D)

    loss = gs["loss"].reshape(())
    return (loss, grad_x.reshape(1, L, D), *[grad[n] for n in order], *[delta[n] for n in order],
            *[new_m[n] for n in order], *[new_v[n] for n in order])
```

```python
import functools

import jax
import jax.numpy as jnp
from jax import lax
from jax.experimental import pallas as pl
from jax.experimental.pallas import tpu as pltpu

F32 = jnp.float32
BF16 = jnp.bfloat16
MESH = pl.DeviceIdType.MESH
HBM = pltpu.HBM

D = 1024
INNER = 2048
HD = 64
NH = 32
NG = 8
NS = 128
Q = 128
GPS = 8
XBC = 4096
DFF = 4096
EPS = 1e-6
W_SHARD = 2824
NCW = 11520
PIECE = 3072
PMAIN = 2816
C_Z, C_XBC, C_GATE, C_DT = 3072, 5120, 9216, 11264
SMALL_ROWS = 256
VMEM_LIMIT = 56 * 1024 * 1024

ADAM_LR, ADAM_B1, ADAM_B2, ADAM_EPS, ADAM_WD, ADAM_STEP = 0.001, 0.9, 0.999, 1e-08, 0.01, 10


def _cp(sem=None, vmem=VMEM_LIMIT):
    return pltpu.CompilerParams(dimension_semantics=sem, vmem_limit_bytes=vmem)


def _sigmoid(v):
    return pl.reciprocal(1.0 + jnp.exp(-v), approx=True)


_DIMS = {"nn": (((1,), (0,)), ((), ())), "nt": (((1,), (1,)), ((), ())), "tn": (((0,), (0,)), ((), ()))}


def _matmul(a, b, mode, out_dtype, tm, tn, tk, name, epi=None, extra=None, n_outer=False, dep=None):
    if mode == "tn":
        K, M = a.shape
    else:
        M, K = a.shape
    N = b.shape[0] if mode == "nt" else b.shape[1]
    tm, tn, tk = min(tm, M), min(tn, N), min(tk, K)
    assert M % tm == 0 and N % tn == 0 and K % tk == 0, (name, M, N, K, tm, tn, tk)
    nm, nn, nk = M // tm, N // tn, K // tk
    dims = _DIMS[mode]

    def ij(p0, p1):
        return (p1, p0) if n_outer else (p0, p1)

    if mode == "tn":
        a_spec = pl.BlockSpec((tk, tm), lambda p0, p1, k: (k, ij(p0, p1)[0]))
    else:
        a_spec = pl.BlockSpec((tm, tk), lambda p0, p1, k: (ij(p0, p1)[0], k))
    if mode == "nt":
        b_spec = pl.BlockSpec((tn, tk), lambda p0, p1, k: (ij(p0, p1)[1], k))
    else:
        b_spec = pl.BlockSpec((tk, tn), lambda p0, p1, k: (k, ij(p0, p1)[1]))
    o_spec = pl.BlockSpec((tm, tn), lambda p0, p1, k: ij(p0, p1))
    in_specs = [a_spec, b_spec]
    args = [a, b]
    if epi in ("res", "drelu"):
        in_specs.append(o_spec)
        args.append(extra)
    if dep is not None:
        in_specs.append(pl.BlockSpec(memory_space=pl.ANY))
        args.append(dep)
    n_in = len(args)
    if epi == "relu2":
        out_shape = (jax.ShapeDtypeStruct((M, N), out_dtype), jax.ShapeDtypeStruct((M, N), BF16))
        out_specs = (o_spec, o_spec)
    else:
        out_shape = jax.ShapeDtypeStruct((M, N), out_dtype)
        out_specs = o_spec

    def kern(*refs):
        a_ref, b_ref = refs[0], refs[1]
        e_ref = refs[2] if epi in ("res", "drelu") else None
        acc = refs[-1]
        outs = refs[n_in:-1] if nk > 1 else refs[n_in:]
        k = pl.program_id(2)

        def product():
            return lax.dot_general(a_ref[...].astype(BF16), b_ref[...].astype(BF16), dims, preferred_element_type=F32)

        def finish(r):
            if epi is None:
                outs[0][...] = r.astype(out_dtype)
            elif epi == "res":
                outs[0][...] = (r + e_ref[...]).astype(out_dtype)
            elif epi == "relu2":
                outs[0][...] = r.astype(out_dtype)
                t = jnp.maximum(r, 0.0)
                outs[1][...] = (t * t).astype(BF16)
            else:
                outs[0][...] = (r * (2.0 * jnp.maximum(e_ref[...].astype(F32), 0.0))).astype(out_dtype)

        if nk == 1:
            finish(product())
        else:
            @pl.when(k == 0)
            def _():
                acc[...] = jnp.zeros_like(acc)

            acc[...] += product()

            @pl.when(k == nk - 1)
            def _():
                finish(acc[...])

    grid = (nn, nm, nk) if n_outer else (nm, nn, nk)
    return pl.pallas_call(
        kern, grid=grid, in_specs=in_specs, out_specs=out_specs, out_shape=out_shape,
        scratch_shapes=[pltpu.VMEM((tm, tn), F32)] if nk > 1 else [], name=name,
        compiler_params=_cp(("parallel", "parallel", "arbitrary")),
    )(*args)


def _matmul_res_rms(a, b, x, w, tm, name):
    M, K = a.shape
    tm = min(tm, M)

    def kern(a_ref, b_ref, x_ref, w_ref, x1_ref, h_ref):
        x1 = x_ref[...] + lax.dot_general(a_ref[...].astype(BF16), b_ref[...].astype(BF16), _DIMS["nn"],
                                          preferred_element_type=F32)
        x1_ref[...] = x1
        r = lax.rsqrt(jnp.mean(x1 * x1, axis=-1, keepdims=True) + EPS)
        h_ref[...] = ((x1 * r) * w_ref[...]).astype(BF16)

    row = pl.BlockSpec((tm, D), lambda i: (i, 0))
    return pl.pallas_call(
        kern, grid=(M // tm,),
        in_specs=[pl.BlockSpec((tm, K), lambda i: (i, 0)), pl.BlockSpec((K, D), lambda i: (0, 0)), row,
                  pl.BlockSpec((1, D), lambda i: (0, 0))],
        out_specs=(row, row), out_shape=(jax.ShapeDtypeStruct((M, D), F32), jax.ShapeDtypeStruct((M, D), BF16)),
        name=name, compiler_params=_cp(("parallel",)),
    )(a, b, x, w.reshape(1, D))


def _matmul_res_final(a, b, x, w, tgt, tm, name):
    M, K = a.shape
    tm = min(tm, M)

    def kern(a_ref, b_ref, x_ref, w_ref, t_ref, dx_ref, gw_ref, loss_ref):
        @pl.when(pl.program_id(0) == 0)
        def _():
            gw_ref[...] = jnp.zeros_like(gw_ref)
            loss_ref[...] = jnp.zeros_like(loss_ref)

        xv = x_ref[...] + lax.dot_general(a_ref[...].astype(BF16), b_ref[...].astype(BF16), _DIMS["nn"],
                                          preferred_element_type=F32)
        r = lax.rsqrt(jnp.mean(xv * xv, axis=-1, keepdims=True) + EPS)
        xn = xv * r
        e = xn * w_ref[...] - t_ref[...]
        loss_ref[...] += 0.5 * jnp.sum(jnp.mean(e * e, axis=-1, keepdims=True))
        dyv = e * (1.0 / D)
        gw_ref[...] += jnp.broadcast_to(jnp.sum(dyv * xn, axis=0, keepdims=True), (8, D))
        dxn = dyv * w_ref[...]
        dx_ref[...] = r * (dxn - xn * jnp.mean(dxn * xn, axis=-1, keepdims=True))

    row = pl.BlockSpec((tm, D), lambda i: (i, 0))
    return pl.pallas_call(
        kern, grid=(M // tm,),
        in_specs=[pl.BlockSpec((tm, K), lambda i: (i, 0)), pl.BlockSpec((K, D), lambda i: (0, 0)), row,
                  pl.BlockSpec((1, D), lambda i: (0, 0)), row],
        out_specs=(row, pl.BlockSpec((8, D), lambda i: (0, 0)), pl.BlockSpec((8, 128), lambda i: (0, 0))),
        out_shape=(jax.ShapeDtypeStruct((M, D), F32), jax.ShapeDtypeStruct((8, D), F32),
                   jax.ShapeDtypeStruct((8, 128), F32)),
        name=name, compiler_params=_cp(("arbitrary",)),
    )(a, b, x, w.reshape(1, D), tgt)


def _matmul_rms_bwd(a, b, mode, x, w, res, tm, tk, name, dep=None):
    M, K = a.shape
    tm, tk = min(tm, M), min(tk, K)
    nk = K // tk
    assert M % tm == 0 and K % tk == 0
    b_spec = (pl.BlockSpec((tk, D), lambda k, i: (k, 0)) if mode == "nn" else pl.BlockSpec((D, tk), lambda k, i: (0, k)))
    row = pl.BlockSpec((tm, D), lambda k, i: (jnp.where(k == nk - 1, i, 0), 0))
    deps = [] if dep is None else [dep]

    def kern(a_ref, b_ref, x_ref, w_ref, res_ref, *rest):
        dx_ref, gw_ref, acc = rest[-3:]
        k, i = pl.program_id(0), pl.program_id(1)

        @pl.when((i == 0) & (k == 0))
        def _():
            gw_ref[...] = jnp.zeros_like(gw_ref)

        def product():
            return lax.dot_general(a_ref[...].astype(BF16), b_ref[...].astype(BF16), _DIMS[mode],
                                   preferred_element_type=F32)

        def finish(dyv):
            xv = x_ref[...]
            r = lax.rsqrt(jnp.mean(xv * xv, axis=-1, keepdims=True) + EPS)
            xn = xv * r
            gw_ref[...] += jnp.broadcast_to(jnp.sum(dyv * xn, axis=0, keepdims=True), (8, D))
            dxn = dyv * w_ref[...]
            dx_ref[...] = res_ref[...] + r * (dxn - xn * jnp.mean(dxn * xn, axis=-1, keepdims=True))

        if nk == 1:
            finish(product())
        else:
            rows = pl.ds(pl.multiple_of(i * tm, tm), tm)

            @pl.when(k == 0)
            def _():
                acc[rows, :] = jnp.zeros((tm, D), F32)

            acc[rows, :] += product()

            @pl.when(k == nk - 1)
            def _():
                finish(acc[rows, :])

    return pl.pallas_call(
        kern, grid=(nk, M // tm),
        in_specs=[pl.BlockSpec((tm, tk), lambda k, i: (i, k)), b_spec, row, pl.BlockSpec((1, D), lambda k, i: (0, 0)),
                  row] + [pl.BlockSpec(memory_space=pl.ANY)] * len(deps),
        out_specs=(row, pl.BlockSpec((8, D), lambda k, i: (0, 0))),
        out_shape=(jax.ShapeDtypeStruct((M, D), F32), jax.ShapeDtypeStruct((8, D), F32)),
        scratch_shapes=[pltpu.VMEM((M, D) if nk > 1 else (8, 128), F32)], name=name,
        compiler_params=_cp(("arbitrary", "arbitrary")),
    )(a, b, x, w.reshape(1, D), res, *deps)


def _rms_fwd(x, w, name, tl=256, dep=None):
    L = x.shape[0]

    def kern(x_ref, w_ref, *rest):
        o_ref = rest[-1]
        xv = x_ref[...]
        r = lax.rsqrt(jnp.mean(xv * xv, axis=-1, keepdims=True) + EPS)
        o_ref[...] = ((xv * r) * w_ref[...]).astype(BF16)

    row = pl.BlockSpec((tl, D), lambda i: (i, 0))
    deps = [] if dep is None else [dep]
    return pl.pallas_call(
        kern, grid=(L // tl,),
        in_specs=[row, pl.BlockSpec((1, D), lambda i: (0, 0))] + [pl.BlockSpec(memory_space=pl.ANY)] * len(deps),
        out_specs=row, out_shape=jax.ShapeDtypeStruct((L, D), BF16), name=name, compiler_params=_cp(("parallel",)),
    )(x, w.reshape(1, D), *deps)


def _down(v, k):
    if k == 0:
        return v
    t = lax.broadcasted_iota(jnp.int32, v.shape, 0)
    return jnp.where(t >= k, pltpu.roll(v, k, axis=0), 0.0)


def _up(v, k):
    if k == 0:
        return v
    n = v.shape[0]
    t = lax.broadcasted_iota(jnp.int32, v.shape, 0)
    return jnp.where(t < n - k, pltpu.roll(v, n - k, axis=0), 0.0)


TW = 256


def _sc_fwd(proj, cw):
    L = proj.shape[0]
    nb = D // TW

    def kern(b_ref, c_ref, x_ref, w_ref, o_ref):
        u = c_ref[...].astype(F32) * x_ref[...].astype(F32)
        w = w_ref[...]
        cv = w[0:1] * _down(u, 2) + w[1:2] * _down(u, 1) + w[2:3] * u
        o_ref[...] = (b_ref[...].astype(F32) * cv).astype(BF16)

    col = lambda off: pl.BlockSpec((L, TW), lambda j: (0, off + j))
    return pl.pallas_call(
        kern, grid=(nb,), in_specs=[col(0), col(nb), col(2 * nb), pl.BlockSpec((8, TW), lambda j: (0, j))],
        out_specs=pl.BlockSpec((L, TW), lambda j: (0, j)), out_shape=jax.ShapeDtypeStruct((L, D), BF16),
        name="sc_fwd", compiler_params=_cp(("parallel",)),
    )(proj, proj, proj, cw)


def _sc_bwd(dya, proj, cw, dproj):
    L = proj.shape[0]
    nb = D // TW

    def kern(d_ref, b_ref, c_ref, x_ref, w_ref, _, dp_ref, gw_ref, keep):
        sec = pl.program_id(1)

        @pl.when(sec == 0)
        def _():
            cs, xs, dyv = c_ref[...].astype(F32), x_ref[...].astype(F32), d_ref[...]
            w = w_ref[...]
            u = cs * xs
            u1, u2 = _down(u, 1), _down(u, 2)
            cv = w[0:1] * u2 + w[1:2] * u1 + w[2:3] * u
            dcv = dyv * b_ref[...].astype(F32)
            du = w[2:3] * dcv + w[1:2] * _up(dcv, 1) + w[0:1] * _up(dcv, 2)
            g0 = jnp.sum(dcv * u2, axis=0, keepdims=True)
            g1 = jnp.sum(dcv * u1, axis=0, keepdims=True)
            g2 = jnp.sum(dcv * u, axis=0, keepdims=True)
            row = lax.broadcasted_iota(jnp.int32, (8, TW), 0)
            gw_ref[...] = jnp.where(row == 0, g0, jnp.where(row == 1, g1, jnp.where(row == 2, g2, 0.0)))
            dp_ref[...] = (dyv * cv).astype(BF16)
            keep[0] = (du * xs).astype(BF16)
            keep[1] = (du * cs).astype(BF16)

        @pl.when(sec > 0)
        def _():
            dp_ref[...] = keep[sec - 1]

    col = lambda off: pl.BlockSpec((L, TW), lambda j, s: (0, off + j))
    return pl.pallas_call(
        kern, grid=(nb, 3),
        in_specs=[col(0), col(0), col(nb), col(2 * nb), pl.BlockSpec((8, TW), lambda j, s: (0, j)),
                  pl.BlockSpec(memory_space=pl.ANY)],
        out_specs=(pl.BlockSpec((L, TW), lambda j, s: (0, s * nb + j)), pl.BlockSpec((8, TW), lambda j, s: (0, j))),
        out_shape=(jax.ShapeDtypeStruct(dproj.shape, BF16), jax.ShapeDtypeStruct((8, D), F32)),
        scratch_shapes=[pltpu.VMEM((2, L, TW), BF16)],
        input_output_aliases={5: 0}, name="sc_bwd", compiler_params=_cp(("parallel", "arbitrary")),
    )(dya, proj, proj, proj, cw, dproj)


def _ssm_conv_fwd(proj, cw4):
    L = proj.shape[0]
    off = C_XBC // TW

    def kern(r_ref, w_ref, o_ref):
        raw = r_ref[...].astype(F32)
        w = w_ref[...]
        c4 = w[0:1] * _down(raw, 3) + w[1:2] * _down(raw, 2) + w[2:3] * _down(raw, 1) + w[3:4] * raw + w[4:5]
        o_ref[...] = c4 * _sigmoid(c4)

    return pl.pallas_call(
        kern, grid=(XBC // TW,),
        in_specs=[pl.BlockSpec((L, TW), lambda j: (0, off + j)), pl.BlockSpec((8, TW), lambda j: (0, j))],
        out_specs=pl.BlockSpec((L, TW), lambda j: (0, j)), out_shape=jax.ShapeDtypeStruct((L, XBC), F32),
        name="ssm_conv_fwd", compiler_params=_cp(("parallel",)),
    )(proj, cw4)


def _ssm_conv_bwd(dx, proj, cw4, dproj, col0, name):
    L, width = dx.shape
    off_p = (C_XBC + col0) // TW
    off_w = col0 // TW

    def kern(d_ref, r_ref, w_ref, _, dp_ref, gw_ref):
        raw = r_ref[...].astype(F32)
        w = w_ref[...]
        r1, r2, r3 = _down(raw, 1), _down(raw, 2), _down(raw, 3)
        c4 = w[0:1] * r3 + w[1:2] * r2 + w[2:3] * r1 + w[3:4] * raw + w[4:5]
        sg = _sigmoid(c4)
        dc4 = d_ref[...] * (sg * (1.0 + c4 * (1.0 - sg)))
        draw = w[3:4] * dc4 + w[2:3] * _up(dc4, 1) + w[1:2] * _up(dc4, 2) + w[0:1] * _up(dc4, 3)
        dp_ref[...] = draw.astype(BF16)
        gs = [jnp.sum(dc4 * r3, axis=0, keepdims=True), jnp.sum(dc4 * r2, axis=0, keepdims=True),
              jnp.sum(dc4 * r1, axis=0, keepdims=True), jnp.sum(dc4 * raw, axis=0, keepdims=True),
              jnp.sum(dc4, axis=0, keepdims=True)]
        row = lax.broadcasted_iota(jnp.int32, (8, TW), 0)
        acc = jnp.zeros((8, TW), F32)
        for k, gk in enumerate(gs):
            acc = jnp.where(row == k, gk, acc)
        gw_ref[...] = acc

    return pl.pallas_call(
        kern, grid=(width // TW,),
        in_specs=[pl.BlockSpec((L, TW), lambda j: (0, j)), pl.BlockSpec((L, TW), lambda j: (0, off_p + j)),
                  pl.BlockSpec((8, TW), lambda j: (0, off_w + j)), pl.BlockSpec(memory_space=pl.ANY)],
        out_specs=(pl.BlockSpec((L, TW), lambda j: (0, off_p + j)), pl.BlockSpec((8, TW), lambda j: (0, j))),
        out_shape=(jax.ShapeDtypeStruct(dproj.shape, BF16), jax.ShapeDtypeStruct((8, width), F32)),
        input_output_aliases={3: 0}, name=name, compiler_params=_cp(("arbitrary",)),
    )(dx, proj, cw4, dproj)


def _split3(v):
    h1 = v.astype(BF16)
    r1 = v - h1.astype(F32)
    h2 = r1.astype(BF16)
    h3 = (r1 - h2.astype(F32)).astype(BF16)
    return h1, h2, h3


def _dot01(m01, v, dims=_DIMS["nn"], m_left=True, terms=3):
    out = None
    for part in _split3(v)[:terms]:
        ops = (m01, part) if m_left else (part, m01)
        t = lax.dot_general(ops[0], ops[1], dims, preferred_element_type=F32)
        out = t if out is None else out + t
    return out


def _bdot(a, b, mode="nn"):
    return lax.dot_general(a.astype(BF16), b.astype(BF16), _DIMS[mode], preferred_element_type=F32)


def _softplus(v):
    return jnp.maximum(v, 0.0) + jnp.log1p(jnp.exp(-jnp.abs(v)))


def _dt_prep(proj, vec):
    L = proj.shape[0]

    def kern(p_ref, v_ref, dt_ref, cs_ref, sg_ref):
        v = v_ref[...]
        pre = p_ref[:, 0:128] + v[0:1]
        dt = _softplus(pre)
        da = dt * (-jnp.exp(v[1:2]))
        ii = lax.broadcasted_iota(jnp.int32, (Q, Q), 0)
        jj = lax.broadcasted_iota(jnp.int32, (Q, Q), 1)
        ltri = (jj <= ii).astype(BF16)
        lane = lax.broadcasted_iota(jnp.int32, (Q, 128), 1)
        for val, ref in ((dt, dt_ref), (_dot01(ltri, da), cs_ref), (_sigmoid(pre), sg_ref)):
            for g in range(NG):
                moved = val if g == 0 else pltpu.roll(val, 128 - 4 * g, axis=1)
                ref[g] = jnp.where(lane < 4, moved, 0.0)

    blk = pl.BlockSpec((NG, Q, 128), lambda c: (0, c, 0))
    return pl.pallas_call(
        kern, grid=(L // Q,),
        in_specs=[pl.BlockSpec((Q, 256), lambda c: (c, 0)), pl.BlockSpec((8, 128), lambda c: (0, 0))],
        out_specs=(blk, blk, blk),
        out_shape=(jax.ShapeDtypeStruct((NG, L, 128), F32),) * 3,
        name="dt_prep", compiler_params=_cp(("parallel",)),
    )(proj, vec)


def _head_masks():
    lane = lax.broadcasted_iota(jnp.int32, (1, 4 * HD), 1)
    return [((lane >= HD * j) & (lane < HD * (j + 1))) for j in range(4)]


def _expand4(v4, masks):
    R = v4.shape[0]
    out = jnp.zeros((R, 4 * HD), F32)
    for j in range(4):
        out = jnp.where(masks[j], jnp.broadcast_to(v4[:, j:j + 1], (R, 4 * HD)), out)
    return out


def _decay_matrix(cs_col, tri):
    colb = jnp.broadcast_to(cs_col, (Q, Q))
    return jnp.exp(jnp.where(tri, colb - colb.T, -jnp.inf))


def _ssd_fwd(xbc, dt4, cs4, vecg):
    L = xbc.shape[0]
    nc = L // Q

    def kern(x_ref, b_ref, c_ref, dt_ref, cs_ref, v_ref, y_ref, s_ref, S):
        c = pl.program_id(1)

        @pl.when(c == 0)
        def _():
            S[...] = jnp.zeros_like(S)

        masks = _head_masks()
        ii = lax.broadcasted_iota(jnp.int32, (Q, Q), 0)
        jj = lax.broadcasted_iota(jnp.int32, (Q, Q), 1)
        tri = jj <= ii
        for gi in range(GPS):
            xs, ns = slice(256 * gi, 256 * (gi + 1)), slice(NS * gi, NS * (gi + 1))
            dt4v, cs4v = dt_ref[gi], cs_ref[gi]
            dt_b, cs_b = _expand4(dt4v, masks), _expand4(cs4v, masks)
            d_b = _expand4(v_ref[gi], masks)[1:2]
            cs_last = cs_b[Q - 1:Q, :]
            x4, bm, cm = x_ref[:, xs], b_ref[:, ns], c_ref[:, ns]
            xdt = x4 * dt_b
            gm = _bdot(cm, bm, "nt")
            s4 = S[gi]
            s_ref[gi, 0] = s4
            y = _bdot(cm, s4) * jnp.exp(cs_b) + d_b * x4
            m_all = jnp.concatenate([(gm * _decay_matrix(cs4v[:, j:j + 1], tri)).astype(BF16) for j in range(4)], axis=0)
            yd = _bdot(m_all, xdt)
            for j in range(4):
                y = y + jnp.where(masks[j], yd[Q * j:Q * (j + 1)], 0.0)
            y_ref[:, xs] = y
            S[gi] = jnp.exp(cs_last) * s4 + _bdot(bm, xdt * jnp.exp(cs_last - cs_b), "tn")

    sc = pl.BlockSpec((GPS, Q, 128), lambda g, c: (g, c, 0))
    bw = NS * GPS
    return pl.pallas_call(
        kern, grid=(NG // GPS, nc),
        in_specs=[pl.BlockSpec((Q, 256 * GPS), lambda g, c: (c, g)),
                  pl.BlockSpec((Q, bw), lambda g, c: (c, INNER // bw + g)),
                  pl.BlockSpec((Q, bw), lambda g, c: (c, (INNER + NG * NS) // bw + g)),
                  sc, sc, pl.BlockSpec((GPS, 8, 128), lambda g, c: (g, 0, 0))],
        out_specs=(pl.BlockSpec((Q, 256 * GPS), lambda g, c: (c, g)),
                   pl.BlockSpec((GPS, 1, NS, 256), lambda g, c: (g, c, 0, 0))),
        out_shape=(jax.ShapeDtypeStruct((L, INNER), F32), jax.ShapeDtypeStruct((NG, nc, NS, 256), F32)),
        scratch_shapes=[pltpu.VMEM((GPS, NS, 256), F32)], name="ssd_fwd",
        compiler_params=_cp(("parallel", "arbitrary")),
    )(xbc, xbc, xbc, dt4, cs4, vecg)


def _ssd_bwd(xbc, dt4, cs4, sg4, vecg, s_all, dy):
    L = xbc.shape[0]
    nc = L // Q

    def kern(x_ref, b_ref, c_ref, dt_ref, cs_ref, sg_ref, v_ref, s_ref, dy_ref,
             dx_ref, db_ref, dc_ref, ddt_ref, st_ref, dS):
        cc = pl.program_id(1)

        @pl.when(cc == 0)
        def _():
            dS[...] = jnp.zeros_like(dS)
            st_ref[...] = jnp.zeros_like(st_ref)

        masks = _head_masks()
        ii = lax.broadcasted_iota(jnp.int32, (Q, Q), 0)
        jj = lax.broadcasted_iota(jnp.int32, (Q, Q), 1)
        tri = jj <= ii
        utri = (jj >= ii).astype(BF16)
        hsel = ((lax.broadcasted_iota(jnp.int32, (4 * HD, 128), 0) // HD)
                == lax.broadcasted_iota(jnp.int32, (4 * HD, 128), 1)).astype(BF16)
        hrow = ((lax.broadcasted_iota(jnp.int32, (4 * Q, 128), 0) // Q)
                == lax.broadcasted_iota(jnp.int32, (4 * Q, 128), 1)).astype(BF16)
        ones_q = jnp.ones((Q, 128), BF16)
        lane128 = lax.broadcasted_iota(jnp.int32, (Q, 128), 1)

        for gi in range(GPS):
            xs, ns = slice(256 * gi, 256 * (gi + 1)), slice(NS * gi, NS * (gi + 1))
            dt4v, cs4v, sg4v = dt_ref[gi], cs_ref[gi], sg_ref[gi]
            dt_b, cs_b = _expand4(dt4v, masks), _expand4(cs4v, masks)
            vv = _expand4(v_ref[gi], masks)
            a_b = -jnp.exp(vv[0:1])
            d_b = vv[1:2]
            a4 = -jnp.exp(v_ref[gi][0:1, :])
            cs_last = cs_b[Q - 1:Q, :]
            ecs = jnp.exp(cs_b)
            decay = jnp.exp(cs_last - cs_b)
            elast = jnp.exp(cs_last)
            x4, bm, cm, dyv = x_ref[:, xs], b_ref[:, ns], c_ref[:, ns], dy_ref[:, xs]
            s4 = s_ref[gi, 0]
            dsn = dS[gi]
            xdt = x4 * dt_b
            gm = _bdot(cm, bm, "nt")
            dye = dyv * ecs
            yoff = ecs * _bdot(cm, s4)
            t4 = _bdot(bm, dsn) * decay
            lms, mhs = [], []
            for j in range(4):
                colb = jnp.broadcast_to(cs4v[:, j:j + 1], (Q, Q))
                lms.append(jnp.exp(jnp.where(tri, colb - colb.T, -jnp.inf)))
                mhs.append(gm * lms[j])
            m_all = jnp.concatenate([m.astype(BF16) for m in mhs], axis=0)
            dy_m = jnp.concatenate([jnp.where(masks[j], dyv, 0.0).astype(BF16) for j in range(4)], axis=0)
            dxdt = t4 + _bdot(m_all, dy_m, "tn")
            dm_all = _bdot(dy_m, xdt, "nt")
            dg = jnp.zeros((Q, Q), F32)
            for j in range(4):
                dg = dg + dm_all[Q * j:Q * (j + 1)] * lms[j]
            e_all = dm_all * jnp.concatenate(mhs, axis=0)
            rsum = _dot01(ones_q, e_all, m_left=False, terms=2)
            da4 = -_dot01(hrow, e_all, _DIMS["tn"], m_left=False, terms=2)
            for j in range(4):
                da4 = da4 + jnp.where(lane128 == j, rsum[Q * j:Q * (j + 1)], 0.0)
            xt = xdt * t4
            tail = jnp.sum(xt, axis=0, keepdims=True) + elast * jnp.sum(s4 * dsn, axis=0, keepdims=True)
            gd_raw = jnp.sum(dyv * x4, axis=0, keepdims=True)
            stacked = jnp.concatenate([dyv * yoff - xt, dxdt * x4, jnp.broadcast_to(tail, (8, 4 * HD)),
                                       jnp.broadcast_to(gd_raw, (8, 4 * HD))], axis=0)
            seg = _dot01(hsel, stacked, m_left=False, terms=2)
            dda4 = _dot01(utri, da4 + seg[0:Q], terms=2) + seg[2 * Q:2 * Q + 1]
            ddt_ref[gi] = (dda4 * a4 + seg[Q:2 * Q]) * sg4v
            ga = jnp.sum(dda4 * dt4v * a4, axis=0, keepdims=True)
            row = lax.broadcasted_iota(jnp.int32, (8, 128), 0)
            st_ref[gi] += jnp.where(row == 0, ga, jnp.where(row == 1, seg[2 * Q + 8:2 * Q + 9], 0.0))
            dx_ref[:, xs] = d_b * dyv + dxdt * dt_b
            dc_ref[:, ns] = _bdot(dg, bm) + _bdot(dye, s4, "nt")
            db_ref[:, ns] = _bdot(dg, cm, "tn") + _bdot(xdt * decay, dsn, "nt")
            dS[gi] = elast * dsn + _bdot(cm, dye, "tn")

    rv = lambda c: nc - 1 - c
    sc = pl.BlockSpec((GPS, Q, 128), lambda g, c: (g, rv(c), 0))
    bw = NS * GPS
    return pl.pallas_call(
        kern, grid=(NG // GPS, nc),
        in_specs=[pl.BlockSpec((Q, 256 * GPS), lambda g, c: (rv(c), g)),
                  pl.BlockSpec((Q, bw), lambda g, c: (rv(c), INNER // bw + g)),
                  pl.BlockSpec((Q, bw), lambda g, c: (rv(c), (INNER + NG * NS) // bw + g)),
                  sc, sc, sc, pl.BlockSpec((GPS, 8, 128), lambda g, c: (g, 0, 0)),
                  pl.BlockSpec((GPS, 1, NS, 256), lambda g, c: (g, rv(c), 0, 0)),
                  pl.BlockSpec((Q, 256 * GPS), lambda g, c: (rv(c), g))],
        out_specs=(pl.BlockSpec((Q, 256 * GPS), lambda g, c: (rv(c), g)),
                   pl.BlockSpec((Q, bw), lambda g, c: (rv(c), g)),
                   pl.BlockSpec((Q, bw), lambda g, c: (rv(c), g)),
                   pl.BlockSpec((GPS, Q, 128), lambda g, c: (g, rv(c), 0)),
                   pl.BlockSpec((GPS, 8, 128), lambda g, c: (g, 0, 0))),
        out_shape=(jax.ShapeDtypeStruct((L, INNER), F32), jax.ShapeDtypeStruct((L, NG * NS), F32),
                   jax.ShapeDtypeStruct((L, NG * NS), F32), jax.ShapeDtypeStruct((NG, L, 128), F32),
                   jax.ShapeDtypeStruct((NG, 8, 128), F32)),
        scratch_shapes=[pltpu.VMEM((GPS, NS, 256), F32)], name="ssd_bwd",
        compiler_params=_cp(("parallel", "arbitrary")),
    )(xbc, xbc, xbc, dt4, cs4, sg4, vecg, s_all, dy)


def _dt_bwd(ddt, dproj, tl=256):
    L = ddt.shape[1]

    def kern(d_ref, _, dp_ref, gs_ref):
        @pl.when(pl.program_id(0) == 0)
        def _():
            gs_ref[...] = jnp.zeros_like(gs_ref)

        d = d_ref[0]
        for g in range(1, NG):
            d = d + pltpu.roll(d_ref[g], 4 * g, axis=1)
        gs_ref[...] += jnp.broadcast_to(jnp.sum(d, axis=0, keepdims=True), (8, 128))
        dp_ref[...] = jnp.concatenate([d, jnp.zeros_like(d)], axis=1).astype(BF16)

    return pl.pallas_call(
        kern, grid=(L // tl,),
        in_specs=[pl.BlockSpec((NG, tl, 128), lambda i: (0, i, 0)), pl.BlockSpec(memory_space=pl.ANY)],
        out_specs=(pl.BlockSpec((tl, 256), lambda i: (i, C_DT // 256)), pl.BlockSpec((8, 128), lambda i: (0, 0))),
        out_shape=(jax.ShapeDtypeStruct(dproj.shape, BF16), jax.ShapeDtypeStruct((8, 128), F32)),
        input_output_aliases={1: 0}, name="dt_bwd", compiler_params=_cp(("arbitrary",)),
    )(ddt, dproj)


GW = INNER // NG


def _gnorm_fwd(y, proj, w, tl=256):
    L = y.shape[0]
    zoff = C_Z // 1024

    def kern(y_ref, z_ref, w_ref, o_ref):
        z = z_ref[...].astype(F32)
        yz = y_ref[...] * (z * _sigmoid(z))
        wv = w_ref[...]
        for k in range(1024 // GW):
            sl = slice(GW * k, GW * (k + 1))
            v = yz[:, sl]
            rg = lax.rsqrt(jnp.mean(v * v, axis=-1, keepdims=True) + EPS)
            o_ref[:, sl] = ((v * rg) * wv[:, sl]).astype(BF16)

    blk = pl.BlockSpec((tl, 1024), lambda i, j: (i, j))
    return pl.pallas_call(
        kern, grid=(L // tl, 2),
        in_specs=[blk, pl.BlockSpec((tl, 1024), lambda i, j: (i, zoff + j)), pl.BlockSpec((1, 1024), lambda i, j: (0, j))],
        out_specs=blk, out_shape=jax.ShapeDtypeStruct((L, INNER), BF16), name="gnorm_fwd",
        compiler_params=_cp(("parallel", "parallel")),
    )(y, proj, w.reshape(1, INNER))


def _gnorm_bwd(dbr, wb, y, proj, w, dproj, dep, tl=512):
    L = y.shape[0]
    tl = min(tl, L)
    zoff = C_Z // 1024

    def kern(d_ref, b_ref, y_ref, z_ref, w_ref, _, __, dy_ref, dp_ref, gw_ref):
        @pl.when(pl.program_id(1) == 0)
        def _():
            gw_ref[...] = jnp.zeros_like(gw_ref)

        z = z_ref[...].astype(F32)
        sg = _sigmoid(z)
        sz = z * sg
        yv = y_ref[...]
        yz = yv * sz
        dv = lax.dot_general(d_ref[0], b_ref[...], _DIMS["nt"], preferred_element_type=F32)
        wv = w_ref[...]
        for k in range(1024 // GW):
            sl = slice(GW * k, GW * (k + 1))
            v = yz[:, sl]
            rg = lax.rsqrt(jnp.mean(v * v, axis=-1, keepdims=True) + EPS)
            vn = v * rg
            dk = dv[:, sl]
            gw_ref[:, sl] += jnp.broadcast_to(jnp.sum(dk * vn, axis=0, keepdims=True), (8, GW))
            dvn = dk * wv[:, sl]
            dyz = rg * (dvn - vn * jnp.mean(dvn * vn, axis=-1, keepdims=True))
            dy_ref[:, sl] = dyz * sz[:, sl]
            dp_ref[:, sl] = (dyz * yv[:, sl] * (sg[:, sl] * (1.0 + z[:, sl] * (1.0 - sg[:, sl])))).astype(BF16)

    blk = pl.BlockSpec((tl, 1024), lambda j, i: (i, j))
    zblk = pl.BlockSpec((tl, 1024), lambda j, i: (i, zoff + j))
    return pl.pallas_call(
        kern, grid=(2, L // tl),
        in_specs=[pl.BlockSpec((1, tl, D), lambda j, i: (1, i, 0)), pl.BlockSpec((1024, D), lambda j, i: (j, 0)),
                  blk, zblk, pl.BlockSpec((1, 1024), lambda j, i: (0, j)), pl.BlockSpec(memory_space=pl.ANY),
                  pl.BlockSpec(memory_space=pl.ANY)],
        out_specs=(blk, zblk, pl.BlockSpec((8, 1024), lambda j, i: (0, j))),
        out_shape=(jax.ShapeDtypeStruct((L, INNER), F32), jax.ShapeDtypeStruct(dproj.shape, BF16),
                   jax.ShapeDtypeStruct((8, INNER), F32)),
        input_output_aliases={5: 1}, name="gnorm_bwd", compiler_params=_cp(("parallel", "arbitrary")),
    )(dbr, wb, y, proj, w.reshape(1, INNER), dproj, dep)


def _merge_fwd(proj, bg, br_a, br_b, tl=256):
    L = proj.shape[0]
    goff = C_GATE // 1024

    def kern(g1_ref, g2_ref, b1_ref, b2_ref, a_ref, b_ref, o_ref):
        g1 = _sigmoid(g1_ref[...].astype(F32) + b1_ref[...])
        g2 = _sigmoid(g2_ref[...].astype(F32) + b2_ref[...])
        o_ref[...] = (g1 * a_ref[...] + g2 * b_ref[...]).astype(BF16)

    row = pl.BlockSpec((tl, 1024), lambda i: (i, 0))
    bg2 = bg.reshape(1, 2 * D)
    return pl.pallas_call(
        kern, grid=(L // tl,),
        in_specs=[pl.BlockSpec((tl, 1024), lambda i: (i, goff)), pl.BlockSpec((tl, 1024), lambda i: (i, goff + 1)),
                  pl.BlockSpec((1, 1024), lambda i: (0, 0)), pl.BlockSpec((1, 1024), lambda i: (0, 1)), row, row],
        out_specs=row, out_shape=jax.ShapeDtypeStruct((L, D), BF16), name="merge_fwd",
        compiler_params=_cp(("parallel",)),
    )(proj, proj, bg2, bg2, br_a, br_b)


def _branch_ssm_merge(yb, wb, proj, bg, br_a, tm=512):
    L, K = yb.shape
    tm = min(tm, L)
    goff = C_GATE // 1024

    def kern(a_ref, b_ref, g1_ref, g2_ref, b1_ref, b2_ref, bra_ref, brb_ref, m_ref):
        brb = lax.dot_general(a_ref[...], b_ref[...], _DIMS["nn"], preferred_element_type=F32)
        brb_ref[...] = brb
        g1 = _sigmoid(g1_ref[...].astype(F32) + b1_ref[...])
        g2 = _sigmoid(g2_ref[...].astype(F32) + b2_ref[...])
        m_ref[...] = (g1 * bra_ref[...] + g2 * brb).astype(BF16)

    row = pl.BlockSpec((tm, D), lambda i: (i, 0))
    bg2 = bg.reshape(1, 2 * D)
    return pl.pallas_call(
        kern, grid=(L // tm,),
        in_specs=[pl.BlockSpec((tm, K), lambda i: (i, 0)), pl.BlockSpec((K, D), lambda i: (0, 0)),
                  pl.BlockSpec((tm, D), lambda i: (i, goff)), pl.BlockSpec((tm, D), lambda i: (i, goff + 1)),
                  pl.BlockSpec((1, D), lambda i: (0, 0)), pl.BlockSpec((1, D), lambda i: (0, 1)), row],
        out_specs=(row, row), out_shape=(jax.ShapeDtypeStruct((L, D), F32), jax.ShapeDtypeStruct((L, D), BF16)),
        name="branch_ssm_merge", compiler_params=_cp(("parallel",)),
    )(yb, wb, proj, proj, bg2, bg2, br_a)


def _merge_bwd(dx1, wo, proj, bg, br_a, br_b, dproj, tl=512):
    L = proj.shape[0]
    tl = min(tl, L)
    goff = C_GATE // 1024

    def kern(dm_ref, wo_ref, g_ref, b_ref, a_ref, bb_ref, _, dbr_ref, dp_ref, gb_ref):
        j = pl.program_id(0)

        @pl.when(pl.program_id(1) == 0)
        def _():
            gb_ref[...] = jnp.zeros_like(gb_ref)

        g = _sigmoid(g_ref[...].astype(F32) + b_ref[...])
        br = jnp.where(j == 0, a_ref[...], bb_ref[...])
        dmv = lax.dot_general(dm_ref[...].astype(BF16), wo_ref[...], _DIMS["nt"], preferred_element_type=F32)
        dbr_ref[0] = (dmv * g).astype(BF16)
        dgate = dmv * br * g * (1.0 - g)
        gb_ref[...] += jnp.broadcast_to(jnp.sum(dgate, axis=0, keepdims=True), (8, 1024))
        dp_ref[...] = dgate.astype(BF16)

    row = pl.BlockSpec((tl, 1024), lambda j, i: (i, 0))
    gblk = pl.BlockSpec((tl, 1024), lambda j, i: (i, goff + j))
    return pl.pallas_call(
        kern, grid=(2, L // tl),
        in_specs=[row, pl.BlockSpec((D, D), lambda j, i: (0, 0)), gblk, pl.BlockSpec((1, 1024), lambda j, i: (0, j)),
                  row, row, pl.BlockSpec(memory_space=pl.ANY)],
        out_specs=(pl.BlockSpec((1, tl, 1024), lambda j, i: (j, i, 0)), gblk, pl.BlockSpec((8, 1024), lambda j, i: (0, j))),
        out_shape=(jax.ShapeDtypeStruct((2, L, D), BF16), jax.ShapeDtypeStruct(dproj.shape, BF16),
                   jax.ShapeDtypeStruct((8, 2 * D), F32)),
        input_output_aliases={6: 1}, name="merge_bwd", compiler_params=_cp(("parallel", "arbitrary")),
    )(dx1, wo, proj, bg.reshape(1, 2 * D), br_a, br_b, dproj)


def _coords():
    return lax.axis_index("x"), lax.axis_index("y"), lax.axis_index("c")


def _other_chips(sk):
    xk, yk = sk // 2, sk % 2
    return [((1 - xk, yk), 2 * (1 - xk) + yk), ((xk, 1 - yk), 2 * xk + 1 - yk), ((1 - xk, 1 - yk), 2 * (1 - xk) + 1 - yk)]


def _rows(start, size):
    assert size % 128 == 0
    return pl.ds(pl.multiple_of(start, 128), size)


def _per_chip(fn):
    x, y, _ = _coords()
    s = 2 * x + y
    for sk in range(4):
        pl.when(s == sk)(functools.partial(fn, sk))


XTRA = PIECE - PMAIN


def _place(shard, full_shape, block, index_map, idx, name, blk0=0, nblk=None, dep=None, into=None):
    in_block = block[-2:]
    if nblk is None:
        nblk = shard.shape[0] // in_block[0]

    def kern(idx_ref, s_ref, *rest):
        o_ref = rest[-1]
        o_ref[...] = s_ref[...].astype(BF16).reshape(o_ref.shape)

    extra = ([dep] if dep is not None else []) + ([into] if into is not None else [])
    grid_spec = pltpu.PrefetchScalarGridSpec(
        num_scalar_prefetch=1, grid=(nblk,),
        in_specs=[pl.BlockSpec(in_block, lambda i, idx_ref: (blk0 + i, 0))] + [_ANY] * len(extra),
        out_specs=pl.BlockSpec(block, index_map))
    aliases = {1 + len(extra): 0} if into is not None else {}
    return pl.pallas_call(kern, grid_spec=grid_spec, out_shape=jax.ShapeDtypeStruct(full_shape, BF16), name=name,
                          input_output_aliases=aliases, compiler_params=_cp(("arbitrary",)))(idx, shard, *extra)


_SEM = pl.BlockSpec(memory_space=pltpu.SEMAPHORE)
_EFFECT = pltpu.SideEffectType.DATAFLOW_SIDE_EFFECTING


_ANY = pl.BlockSpec(memory_space=pl.ANY)


def _tie(v, dep, name):
    def body(v_ref, dep_ref, o_ref):
        del v_ref, dep_ref, o_ref

    return pl.pallas_call(body, out_shape=jax.ShapeDtypeStruct(v.shape, v.dtype), in_specs=[_ANY, _ANY],
                          out_specs=_ANY, input_output_aliases={0: 0}, name=name)(v, dep)


def _split_call(name, arrays, start=None, wait=None, wait_sems=None, after=None):
    keys = list(arrays)
    n = len(keys)
    n_start = start.n if start is not None else 0
    afters = [] if after is None else (list(after) if isinstance(after, (list, tuple)) else [after])

    def body(*refs):
        pos = n
        if wait is not None:
            wss, wrs = refs[pos], refs[pos + 1]
            pos += 2
        pos += len(afters)
        if start is not None:
            nss, nrs = refs[pos], refs[pos + 1]
            pos += 2
        R = dict(zip(keys, refs[pos:pos + n]))
        token = refs[pos + n]
        x, y, c = _coords()

        def desc(src, dst, dev, ss, rs, k):
            return pltpu.make_async_remote_copy(src_ref=src, dst_ref=dst, send_sem=ss.at[k], recv_sem=rs.at[k],
                                                device_id=dev, device_id_type=MESH)

        def run(sk):
            if wait is not None:
                for k, (snd, land) in enumerate(wait.copies(sk, R)):
                    if snd is not None:
                        desc(snd[0], snd[1], snd[2], wss, wrs, k).wait_send()
                    if land is not None:
                        desc(land, land, (x, y, c), wss, wrs, k).wait_recv()
            if start is not None:
                for k, (snd, land) in enumerate(start.copies(sk, R)):
                    if snd is not None:
                        desc(snd[0], snd[1], snd[2], nss, nrs, k).start()

        _per_chip(run)
        token[...] = jnp.zeros_like(token)

    hbm = pl.BlockSpec(memory_space=HBM)
    vals = [arrays[k] for k in keys]
    ins, in_specs = list(vals), [hbm] * n
    if wait is not None:
        ins += list(wait_sems)
        in_specs += [_SEM, _SEM]
    ins += afters
    in_specs += [pl.BlockSpec(memory_space=pl.ANY)] * len(afters)
    out_shape, out_specs = [], []
    if start is not None:
        out_shape += [pltpu.SemaphoreType.DMA((n_start,)), pltpu.SemaphoreType.DMA((n_start,))]
        out_specs += [_SEM, _SEM]
    first = len(out_shape)
    out_shape += [jax.ShapeDtypeStruct(v.shape, v.dtype) for v in vals] + [jax.ShapeDtypeStruct((8, 128), F32)]
    out_specs += [hbm] * n + [pl.BlockSpec(memory_space=pltpu.VMEM)]
    res = pl.pallas_call(
        body, out_shape=tuple(out_shape), in_specs=in_specs, out_specs=tuple(out_specs),
        input_output_aliases={i: first + i for i in range(n)}, name=name,
        compiler_params=pltpu.CompilerParams(has_side_effects=_EFFECT),
    )(*ins)
    sems = (res[0], res[1]) if start is not None else None
    return dict(zip(keys, res[first:first + n])), sems, res[-1]


class _Plan:
    def __init__(self, n, copies):
        self.n, self.copies = n, copies


_HM, _HX = PMAIN // 2, XTRA // 2
WAVE0 = 768
WAVES = ((0, WAVE0), (WAVE0, _HM - WAVE0))
_WIN = {
    "wq0": (True, "wct", lambda r, sc, hc: r.at[_rows(PMAIN * sc + _HM * hc + WAVES[0][0], WAVES[0][1]), :]),
    "wq1": (True, "wct", lambda r, sc, hc: r.at[_rows(PMAIN * sc + _HM * hc + WAVES[1][0], WAVES[1][1]), :]),
    "xt": (True, "xt", lambda r, sc, hc: r.at[sc, _rows(_HX * hc, _HX), :]),
    "w1": (True, "w1", lambda r, sc, hc: r.at[_rows(512 * hc, 512), pl.ds(1024 * sc, 1024)]),
    "w2": (True, "w2", lambda r, sc, hc: r.at[_rows(1024 * sc + 512 * hc, 512), :]),
    "wa": (True, "wa", lambda r, sc, hc: r.at[_rows(256 * sc + 128 * hc, 128), :]),
    "wb": (True, "wb", lambda r, sc, hc: r.at[_rows(512 * sc + 256 * hc, 256), :]),
    "wo": (True, "wo", lambda r, sc, hc: r.at[_rows(256 * sc + 128 * hc, 128), :]),
    "cw": (False, "cw", lambda r, sc, hc: r.at[sc]),
}


_PIECE_SRC = {
    "wq0": lambda p, hc: p.at[_rows(_HM * hc + WAVES[0][0], WAVES[0][1]), :],
    "wq1": lambda p, hc: p.at[_rows(_HM * hc + WAVES[1][0], WAVES[1][1]), :],
    "xt": lambda p, hc: p.at[_rows(PMAIN + _HX * hc, _HX), :],
}


def _ag_chips_plan(keys):
    def copies(sk, R):
        _, _, c = _coords()
        out = []
        for key in keys:
            _, arr, win = _WIN[key]
            for (px, py), ps in _other_chips(sk):
                dst = win(R[arr], sk, c)
                src = _PIECE_SRC[key](R["piece"], c) if key in _PIECE_SRC else dst
                out.append(((src, dst, (px, py, c)), win(R[arr], ps, c)))
        return out
    return _Plan(3 * len(keys), copies)


def _ag_sibling_plan(keys):
    keys = [k for k in keys if _WIN[k][0]]

    def copies(sk, R):
        x, y, c = _coords()
        out = []
        for key in keys:
            _, arr, win = _WIN[key]
            for _, ps in _other_chips(sk):
                w = win(R[arr], ps, c)
                out.append(((w, w, (x, y, 1 - c)), win(R[arr], ps, 1 - c)))
        return out
    return _Plan(3 * len(keys), copies)


def _in_proj_wave(h, wct, wave, proj=None, tm=2048):
    L = h.shape[0]
    tm = min(tm, L)
    off, size = WAVES[wave]
    start = lambda j: pl.multiple_of(_HM * j + off, 128)

    def kern(h_ref, w_ref, *rest):
        o_ref = rest[-1]
        o_ref[...] = lax.dot_general(h_ref[...], w_ref[...], _DIMS["nt"], preferred_element_type=F32).astype(BF16)

    in_specs = [pl.BlockSpec((tm, D), lambda j, i: (i, 0)),
                pl.BlockSpec((pl.Element(size), pl.Element(D)), lambda j, i: (start(j), 0))]
    args, aliases = [h, wct], {}
    if proj is not None:
        in_specs.append(pl.BlockSpec(memory_space=pl.ANY))
        args.append(proj)
        aliases = {2: 0}
    return pl.pallas_call(
        kern, grid=(8, L // tm), in_specs=in_specs,
        out_specs=pl.BlockSpec((pl.Element(tm), pl.Element(size)), lambda j, i: (i * tm, start(j))),
        out_shape=jax.ShapeDtypeStruct((L, NCW), BF16), input_output_aliases=aliases,
        name="in_proj_wave%d" % wave, compiler_params=_cp(("parallel", "parallel")),
    )(*args)


def _fix_wct(wct, xt):
    nb = PMAIN // XTRA

    def kern(w_ref, x_ref, o_ref):
        k = pl.program_id(0)
        xv = x_ref[0]
        o_ref[...] = jnp.where(k < 3, (w_ref[...].astype(F32) + xv.astype(F32)).astype(BF16), xv)

    blk = pl.BlockSpec((XTRA, D), lambda k: (nb * (k + 1), 0))
    rblk = pl.BlockSpec((XTRA, D), lambda k: (jnp.where(k < 3, nb * (k + 1), 0), 0))
    return pl.pallas_call(
        kern, grid=(4,), in_specs=[rblk, pl.BlockSpec((1, XTRA, D), lambda k: (k, 0, 0))], out_specs=blk,
        out_shape=jax.ShapeDtypeStruct(wct.shape, BF16), input_output_aliases={0: 0}, name="fix_wct",
        compiler_params=_cp(("arbitrary",)),
    )(wct, xt)


_HP = PIECE // 2
_GWIN = [
    lambda r, sc, hc: r.at[_rows(PMAIN * sc + _HP * hc, _HP), :],
    lambda r, sc, hc: r.at[_rows(512 * hc, 512), pl.ds(1024 * sc, 1024)],
    lambda r, sc, hc: r.at[_rows(1024 * sc + 512 * hc, 512), :],
    lambda r, sc, hc: r.at[_rows(256 * sc + 128 * hc, 128), :],
    lambda r, sc, hc: r.at[_rows(512 * sc + 256 * hc, 256), :],
    lambda r, sc, hc: r.at[_rows(256 * sc + 128 * hc, 128), :],
]
HALF_SHAPES = [(PIECE // 2, D), (512, 1024), (512, 1024), (128, 1024), (256, 1024), (128, 1024)]


def _rs_sibling_plan(ts):
    def copies(sk, R):
        x, y, c = _coords()
        out = []
        for t in ts:
            for sc in range(4):
                land = R["ra%d" % t].at[sc]
                out.append(((_GWIN[t](R["g%d" % t], sc, 1 - c), land, (x, y, 1 - c)), land))
        return out
    return _Plan(4 * len(ts), copies)


def _rs_chips_plan(ts):
    def copies(sk, R):
        _, _, c = _coords()
        out = []
        for t in ts:
            for j, ((px, py), ps) in enumerate(_other_chips(sk)):
                land = R["rb%d" % t].at[j]
                out.append(((R["hb%d" % t].at[ps], land, (px, py, c)), land))
        return out
    return _Plan(3 * len(ts), copies)


def _rs_share_plan(ts):
    def copies(sk, R):
        x, y, c = _coords()
        out = []
        for t in ts:
            rows = HALF_SHAPES[t][0]
            mine = R["f%d" % t].at[_rows(rows * c, rows), :]
            out.append(((mine, mine, (x, y, 1 - c)), R["f%d" % t].at[_rows(rows * (1 - c), rows), :]))
        return out
    return _Plan(len(ts), copies)


def _half_tiling(t):
    rows, cols = HALF_SHAPES[t]
    if t == 0:
        return (rows // 2, cols), 2, lambda i: (i, 0)
    return (rows, cols), 1, lambda i: (0, 0)


def _window_spec(t, blk):
    if t == 0:
        return pl.BlockSpec((pl.Element(blk[0]), pl.Element(blk[1])), lambda i, sc, idx_ref: (
            pl.multiple_of(PMAIN * sc + _HP * idx_ref[1] + blk[0] * i, 128), 0))
    if t == 1:
        return pl.BlockSpec(blk, lambda i, sc, idx_ref: (idx_ref[1], sc))
    return pl.BlockSpec(blk, lambda i, sc, idx_ref: (2 * sc + idx_ref[1], 0))


def _chip_sum(g, ra, t, idx, name):
    rows, cols = HALF_SHAPES[t]
    blk, nblk, inner = _half_tiling(t)

    def kern(idx_ref, g_ref, r_ref, hb_ref, hf_ref):
        v = g_ref[...].astype(F32) + r_ref[0].astype(F32)
        hb_ref[0] = v.astype(BF16)

        @pl.when(pl.program_id(1) == idx_ref[0])
        def _():
            hf_ref[...] = v

    omap = lambda i, sc, idx_ref: (sc,) + inner(i)
    grid_spec = pltpu.PrefetchScalarGridSpec(
        num_scalar_prefetch=1, grid=(nblk, 4),
        in_specs=[_window_spec(t, blk), pl.BlockSpec((1,) + blk, omap)],
        out_specs=(pl.BlockSpec((1,) + blk, omap), pl.BlockSpec(blk, lambda i, sc, idx_ref: inner(i))))
    return pl.pallas_call(
        kern, grid_spec=grid_spec,
        out_shape=(jax.ShapeDtypeStruct((4, rows, cols), BF16), jax.ShapeDtypeStruct((rows, cols), F32)),
        name=name, compiler_params=_cp(("parallel", "arbitrary")),
    )(idx, g, ra)


def _final_sum(hf, rb, t, idx, name):
    rows, cols = HALF_SHAPES[t]
    blk, nblk, inner = _half_tiling(t)
    nbr = rows // blk[0]

    def kern(idx_ref, h_ref, r_ref, o_ref):
        o_ref[...] = ((h_ref[...] + r_ref[0].astype(F32)) + r_ref[1].astype(F32)) + r_ref[2].astype(F32)

    def omap(i, idx_ref):
        r, cidx = inner(i)
        return nbr * idx_ref[1] + r, cidx

    grid_spec = pltpu.PrefetchScalarGridSpec(
        num_scalar_prefetch=1, grid=(nblk,),
        in_specs=[pl.BlockSpec(blk, lambda i, idx_ref: inner(i)),
                  pl.BlockSpec((3,) + blk, lambda i, idx_ref: (0,) + inner(i))],
        out_specs=pl.BlockSpec(blk, omap))
    return pl.pallas_call(
        kern, grid_spec=grid_spec, out_shape=jax.ShapeDtypeStruct((2 * rows, cols), F32),
        name=name, compiler_params=_cp(("parallel",)),
    )(idx, hf, rb)


class _ReduceScatter:
    def __init__(self, ts, grads, idx, tag):
        self.ts, self.idx, self.tag = ts, idx, tag
        arr = {}
        for t in ts:
            arr["g%d" % t] = grads[t]
            arr["ra%d" % t] = lax.empty((4,) + HALF_SHAPES[t], BF16)
        self.plan = _rs_sibling_plan(ts)
        self.arr, self.sems, self.token = _split_call("rs_sibling_start_" + tag, arr, start=self.plan)

    def chips(self, after):
        arr, _, _ = _split_call("rs_sibling_wait_" + self.tag, self.arr, wait=self.plan, wait_sems=self.sems, after=after)
        brr, self.hf = {}, {}
        for t in self.ts:
            hb, self.hf[t] = _chip_sum(arr["g%d" % t], arr["ra%d" % t], t, self.idx, "chip_sum_%d" % t)
            brr["hb%d" % t] = hb
            brr["rb%d" % t] = lax.empty((3,) + HALF_SHAPES[t], BF16)
        self.plan = _rs_chips_plan(self.ts)
        self.arr, self.sems, self.token = _split_call("rs_chips_start_" + self.tag, brr, start=self.plan)
        return self.token

    def share(self, after):
        brr, _, _ = _split_call("rs_chips_wait_" + self.tag, self.arr, wait=self.plan, wait_sems=self.sems, after=after)
        frr = {"f%d" % t: _final_sum(self.hf[t], brr["rb%d" % t], t, self.idx, "final_sum_%d" % t) for t in self.ts}
        self.plan = _rs_share_plan(self.ts)
        self.arr, self.sems, self.token = _split_call("rs_share_start_" + self.tag, frr, start=self.plan)
        return self.token

    def result(self, after):
        frr, _, _ = _split_call("rs_share_wait_" + self.tag, self.arr, wait=self.plan, wait_sems=self.sems, after=after)
        return {t: frr["f%d" % t] for t in self.ts}


def _all8_plan(key):
    def copies(sk, R):
        x, y, c = _coords()
        own = R[key].at[4 * x + 2 * y + c]
        out = []
        for k in range(1, 8):
            dev = ((1 - x) if (k >> 2) & 1 else x, (1 - y) if (k >> 1) & 1 else y, (1 - c) if k & 1 else c)
            out.append(((own, own, dev), R[key].at[4 * dev[0] + 2 * dev[1] + dev[2]]))
        return out
    return _Plan(7, copies)


def _sum8(v, name="small_sum"):
    def kern(v_ref, o_ref):
        acc = v_ref[0]
        for k in range(1, 8):
            acc = acc + v_ref[k]
        o_ref[...] = acc

    return pl.pallas_call(kern, out_shape=jax.ShapeDtypeStruct(v.shape[1:], F32), name=name)(v)


def _adamw(w, g, m, v, name, tr=128, blk0=0, nblk=None, into=None, copy_g=False):
    R, C = w.shape
    tr = min(tr, R)
    if nblk is None:
        assert R % tr == 0 and blk0 == 0
        nblk = R // tr
    n_out = 4 if copy_g else 3

    def kern(*refs):
        w_ref, g_ref, m_ref, v_ref = refs[:4]
        d_ref, mo_ref, vo_ref = refs[-n_out:][:3]
        gv = g_ref[...]
        mn = ADAM_B1 * m_ref[...] + (1.0 - ADAM_B1) * gv
        vn = ADAM_B2 * v_ref[...] + (1.0 - ADAM_B2) * (gv * gv)
        m_hat = mn / (1.0 - ADAM_B1 ** ADAM_STEP)
        v_hat = vn / (1.0 - ADAM_B2 ** ADAM_STEP)
        d_ref[...] = -ADAM_LR * (m_hat / (jnp.sqrt(v_hat) + ADAM_EPS) + ADAM_WD * w_ref[...])
        mo_ref[...] = mn
        vo_ref[...] = vn
        if copy_g:
            refs[-1][...] = gv

    blk = pl.BlockSpec((tr, C), lambda i: (blk0 + i, 0))
    sd = jax.ShapeDtypeStruct((R, C), F32)
    in_specs, args, aliases = [blk] * 4, [w, g, m, v], {}
    if into is not None:
        in_specs += [pl.BlockSpec(memory_space=pl.ANY)] * 3
        args += list(into)
        aliases = {4: 0, 5: 1, 6: 2}
    return pl.pallas_call(kern, grid=(nblk,), in_specs=in_specs, out_specs=(blk,) * n_out, out_shape=(sd,) * n_out,
                          input_output_aliases=aliases, name=name, compiler_params=_cp(("parallel",)))(*args)


def _adamw_w_in(wt, gp, mt, vt, offs, name, r0, tr, nblk, views, into=None, blk_key=None):
    el = lambda n: (pl.Element(n), pl.Element(D))
    first = (lambda o: r0) if blk_key is None else (lambda o: tr * o[blk_key])
    own = pl.BlockSpec(el(tr), lambda i, o: (pl.multiple_of(first(o) + tr * i, 8), 0))

    def view(k):
        return pl.BlockSpec(el(tr), lambda i, o: (pl.multiple_of(jnp.maximum(first(o) + tr * i + o[k], 0), 8), 0))

    def kern(o_ref, w_ref, m_ref, v_ref, *refs):
        g_refs, (d_ref, mo_ref, vo_ref, go_ref) = refs[:len(views)], refs[-4:]
        gv = g_refs[0][...]
        if len(views) == 2:
            row = first(o_ref) + tr * pl.program_id(0) + lax.broadcasted_iota(jnp.int32, (tr, D), 0)
            gv = jnp.where(row < o_ref[2], gv, g_refs[1][...])
        mn = ADAM_B1 * m_ref[...] + (1.0 - ADAM_B1) * gv
        vn = ADAM_B2 * v_ref[...] + (1.0 - ADAM_B2) * (gv * gv)
        m_hat = mn / (1.0 - ADAM_B1 ** ADAM_STEP)
        v_hat = vn / (1.0 - ADAM_B2 ** ADAM_STEP)
        d_ref[...] = -ADAM_LR * (m_hat / (jnp.sqrt(v_hat) + ADAM_EPS) + ADAM_WD * w_ref[...])
        mo_ref[...] = mn
        vo_ref[...] = vn
        go_ref[...] = gv

    in_specs = [own, own, own] + [view(k) for k in views]
    args = [wt, mt, vt] + [gp] * len(views)
    aliases = {}
    if into is not None:
        in_specs += [pl.BlockSpec(memory_space=pl.ANY)] * 4
        args += list(into)
        aliases = {1 + len(args) - 4 + j: j for j in range(4)}
    grid_spec = pltpu.PrefetchScalarGridSpec(num_scalar_prefetch=1, grid=(nblk,), in_specs=in_specs,
                                             out_specs=(own,) * 4)
    sd = jax.ShapeDtypeStruct(wt.shape, F32)
    return pl.pallas_call(kern, grid_spec=grid_spec, out_shape=(sd,) * 4, input_output_aliases=aliases, name=name,
                          compiler_params=_cp(("parallel",)))(offs, *args)


def _to_piece(wt, s):
    z = lambda n: jnp.zeros((n, D), wt.dtype)
    pads = [functools.partial(lambda k, w: jnp.pad(w, ((8 * k, PIECE - W_SHARD - 8 * k), (0, 0))).astype(BF16), k)
            for k in range(3)]
    last = lambda w: jnp.concatenate([z(24), w[:744], w[776:], w[744:776], z(PIECE - 24 - W_SHARD)], axis=0).astype(BF16)
    return lax.switch(s, pads + [last], wt)


_SMALL = [("b_gate", 2048), ("ssm_conv_b", 4096), ("dt_bias", 32), ("A_log", 32), ("D_skip", 32),
          ("ssm_norm_w", 2048), ("norm_mlp", 1024), ("norm_final", 1024), ("sc_conv_w", 3072), ("ssm_conv_w", 16384),
          ("loss", 1)]


def _pack(vals, table, rows):
    parts = []
    for name, n in table:
        v = vals[name].reshape(-1).astype(F32)
        pad = (-n) % 128
        parts.append(jnp.pad(v, (0, pad)) if pad else v)
    flat = jnp.concatenate(parts)
    return jnp.pad(flat, (0, rows * 128 - flat.shape[0])).reshape(rows, 128)


def _unpack(arr, table):
    flat = arr.reshape(-1)
    out, off = {}, 0
    for name, n in table:
        out[name] = flat[off:off + n]
        off += n + ((-n) % 128)
    return out


def kernel(x, norm_mix, w_in, b_gate, sc_conv_w, ssm_conv_w, ssm_conv_b, dt_bias, A_log, D_skip, ssm_norm_w, w_branch_sc, w_branch_ssm, w_out, norm_mlp, w_mlp1, w_mlp2, norm_final, loss_target, m_norm_mix, m_w_in, m_b_gate, m_sc_conv_w, m_ssm_conv_w, m_ssm_conv_b, m_dt_bias, m_A_log, m_D_skip, m_ssm_norm_w, m_w_branch_sc, m_w_branch_ssm, m_w_out, m_norm_mlp, m_w_mlp1, m_w_mlp2, m_norm_final, v_norm_mix, v_w_in, v_b_gate, v_sc_conv_w, v_ssm_conv_w, v_ssm_conv_b, v_dt_bias, v_A_log, v_D_skip, v_ssm_norm_w, v_w_branch_sc, v_w_branch_ssm, v_w_out, v_norm_mlp, v_w_mlp1, v_w_mlp2, v_norm_final):
    L = x.shape[1]
    nc = L // Q
    xi, yi, ci = lax.axis_index("x"), lax.axis_index("y"), lax.axis_index("c")
    s = 2 * xi + yi
    idx = jnp.stack([s, ci]).astype(jnp.int32)
    x0 = x.reshape(L, D)
    tgt = loss_target.reshape(L, D)
    small_names = ["b_gate", "sc_conv_w", "ssm_conv_w", "ssm_conv_b", "dt_bias", "A_log", "D_skip", "ssm_norm_w",
                   "norm_mlp", "norm_final"]
    small_wmv = [dict(zip(small_names, vals)) for vals in (
        (b_gate, sc_conv_w, ssm_conv_w, ssm_conv_b, dt_bias, A_log, D_skip, ssm_norm_w, norm_mlp, norm_final),
        (m_b_gate, m_sc_conv_w, m_ssm_conv_w, m_ssm_conv_b, m_dt_bias, m_A_log, m_D_skip, m_ssm_norm_w, m_norm_mlp,
         m_norm_final),
        (v_b_gate, v_sc_conv_w, v_ssm_conv_w, v_ssm_conv_b, v_dt_bias, v_A_log, v_D_skip, v_ssm_norm_w, v_norm_mlp,
         v_norm_final))]
    small_table = [(n, int(small_wmv[0][n].size)) for n in small_names]
    small_rows = 136
    pk_w, pk_m, pk_v = [_pack(d, small_table, small_rows) for d in small_wmv]

    piece = _to_piece(w_in.T, s)
    nb = PMAIN // XTRA
    cws = jnp.zeros((8, 1280), F32)
    cws = cws.at[0:3, 0:256].set(sc_conv_w).at[0:4, 256:1280].set(ssm_conv_w)
    cw0 = lax.dynamic_update_slice(jnp.zeros((4, 8, 1280), F32), cws[None], (s, 0, 0))
    win_keys, win2_keys, mid_keys, end_keys = ["xt", "cw", "wq0"], ["wq1"], ["wa", "wb", "wo", "w1"], ["w2"]
    gw, sems_w, tok = _split_call(
        "ag_win_start", {"wct": lax.empty((NCW, D), BF16), "xt": lax.empty((4, XTRA, D), BF16), "cw": cw0, "piece": piece},
        start=_ag_chips_plan(win_keys))
    g2, sems_w2, tok = _split_call("ag_win2_start", {"wct": gw["wct"], "piece": gw["piece"]},
                                   start=_ag_chips_plan(win2_keys), after=tok)
    piece = g2["piece"]
    gw["wct"] = _place(piece, (NCW, D), (XTRA, D), lambda i, r: (nb * r[0] + i, 0), idx, "place_wct", nblk=nb,
                       dep=tok, into=g2["wct"])
    gw["xt"] = _place(piece, (4, XTRA, D), (1, XTRA, D), lambda i, r: (r[0], 0, 0), idx, "place_xt", blk0=nb, nblk=1,
                      dep=tok, into=gw["xt"])
    gw["piece"] = piece
    wa0 = _place(w_branch_sc, (D, D), (256, 1024), lambda i, r: (r[0], 0), idx, "place_wa", dep=tok)
    wb0 = _place(w_branch_ssm, (INNER, D), (512, 1024), lambda i, r: (r[0], 0), idx, "place_wb", dep=tok)
    wo0 = _place(w_out, (D, D), (256, 1024), lambda i, r: (r[0], 0), idx, "place_wo", dep=tok)
    w10 = _place(w_mlp1, (D, DFF), (256, 1024), lambda i, r: (i, r[0]), idx, "place_w1", dep=tok)
    gm, sems_m, tok = _split_call("ag_mid_start", {"wa": wa0, "wb": wb0, "wo": wo0, "w1": w10},
                                  start=_ag_chips_plan(mid_keys))
    w20 = _place(w_mlp2, (DFF, D), (256, 1024), lambda i, r: (4 * r[0] + i, 0), idx, "place_w2", dep=tok)
    ge, sems_e, tok = _split_call("ag_end_start", {"w2": w20}, start=_ag_chips_plan(end_keys))
    h = _rms_fwd(x0, norm_mix, "rms_mix", dep=tok)
    gw, sems_w, tok = _split_call("ag_win_pass", gw, wait=_ag_chips_plan(win_keys), wait_sems=sems_w,
                                  start=_ag_sibling_plan(win_keys), after=[h, pk_w, pk_m, pk_v])
    gw, _, _ = _split_call("ag_win_done", gw, wait=_ag_sibling_plan(win_keys), wait_sems=sems_w, after=tok)
    wc, cw_all = _fix_wct(gw["wct"], gw["xt"]), gw["cw"]
    sc_w_full = jnp.concatenate([cw_all[k, :, 0:256] for k in range(4)], axis=1)
    ssm_w_full = jnp.concatenate([cw_all[k, :, 256:1280] for k in range(4)], axis=1)
    cw4 = ssm_w_full.at[4].set(ssm_conv_b)
    vec = jnp.zeros((8, 128), F32).at[0, :NH].set(dt_bias).at[1, :NH].set(A_log)
    vecg = jnp.zeros((NG, 8, 128), F32).at[:, 0, :4].set(A_log.reshape(NG, 4)).at[:, 1, :4].set(D_skip.reshape(NG, 4))

    dtraw = _matmul(h, wc[C_DT:], "nt", F32, 512, 256, 1024, "in_proj_dt")
    proj = _in_proj_wave(h, wc, 0)
    g2, sems_w2, tok = _split_call("ag_win2_pass", {"wct": wc, "piece": gw["piece"]},
                                   wait=_ag_chips_plan(win2_keys), wait_sems=sems_w2,
                                   start=_ag_sibling_plan(win2_keys), after=[proj, dtraw])
    g2, _, _ = _split_call("ag_win2_done", g2, wait=_ag_sibling_plan(win2_keys), wait_sems=sems_w2, after=tok)
    wc = g2["wct"]
    proj = _in_proj_wave(h, wc, 1, proj=proj)
    ya = _sc_fwd(proj, sc_w_full)
    xbc = _ssm_conv_fwd(proj, cw4)
    dt4, cs4, sg4 = _dt_prep(dtraw, vec)
    y, s_all = _ssd_fwd(xbc, dt4, cs4, vecg)
    gm, sems_m, tok = _split_call("ag_mid_pass", gm, wait=_ag_chips_plan(mid_keys), wait_sems=sems_m,
                                  start=_ag_sibling_plan(mid_keys), after=[y, ya])
    y = _tie(y, tok, "tie_y")
    yb = _gnorm_fwd(y, proj, ssm_norm_w)
    gm, _, _ = _split_call("ag_mid_done", gm, wait=_ag_sibling_plan(mid_keys), wait_sems=sems_m, after=yb)
    wa, wb, wo, w1 = gm["wa"], gm["wb"], gm["wo"], gm["w1"]
    ge, sems_e, tok = _split_call("ag_end_pass", ge, wait=_ag_chips_plan(end_keys), wait_sems=sems_e,
                                  start=_ag_sibling_plan(end_keys), after=yb)
    br_a = _matmul(ya, wa, "nn", F32, 1024, 1024, 1024, "branch_sc", dep=tok)
    br_b, merged = _branch_ssm_merge(yb, wb, proj, b_gate, br_a)
    x1, h2 = _matmul_res_rms(merged, wo, x0, norm_mlp, 1024, "out_proj")
    a1, rl = _matmul(h2, w1, "nn", BF16, 1024, 1024, 1024, "mlp1", epi="relu2", n_outer=True)
    ge, _, _ = _split_call("ag_end_done", ge, wait=_ag_sibling_plan(end_keys), wait_sems=sems_e, after=a1)
    w2 = ge["w2"]
    dx2, g_nf, loss8 = _matmul_res_final(rl, w2, x1, norm_final, tgt, 512, "mlp2")

    da = _matmul(dx2, w2, "nt", BF16, 1024, 1024, 1024, "mlp2_dx", epi="drelu", extra=a1, n_outer=True)
    g_w2 = _matmul(rl, dx2, "tn", BF16, 1024, 1024, 2048, "mlp2_dw")
    g_w1 = _matmul(h2, da, "tn", BF16, 1024, 1024, 2048, "mlp1_dw")
    dx1, g_nmlp = _matmul_rms_bwd(da, w1, "nt", x1, norm_mlp, dx2, 512, 4096, "mlp1_dx")
    g_wo = _matmul(merged, dx1, "tn", BF16, 1024, 1024, 2048, "out_proj_dw")
    dproj = lax.empty((L, NCW), BF16)
    dbr, dproj, g_bg = _merge_bwd(dx1, wo, proj, b_gate, br_a, br_b, dproj)
    dya = _matmul(dbr[0], wa, "nt", F32, 1024, 1024, 1024, "branch_sc_dx")
    g_wa = _matmul(ya, dbr[0], "tn", BF16, 1024, 1024, 2048, "branch_sc_dw")
    dproj, g_scw = _sc_bwd(dya, proj, sc_w_full, dproj)
    g_wb = _matmul(yb, dbr[1], "tn", BF16, 1024, 1024, 2048, "branch_ssm_dw")
    rs_a = _ReduceScatter([1, 2, 3, 4, 5], {1: g_w1, 2: g_w2, 3: g_wa, 4: g_wb, 5: g_wo}, idx, "a")
    dy, dproj, g_snw = _gnorm_bwd(dbr, wb, y, proj, ssm_norm_w, dproj, rs_a.token)
    tok = rs_a.chips(after=dy)
    dxs, dbm, dcm, ddt_g, st = _ssd_bwd(xbc, dt4, cs4, sg4, vecg, s_all, _tie(dy, tok, "tie_dy"))
    dproj, gx1 = _ssm_conv_bwd(dxs, proj, cw4, dproj, 0, "ssm_conv_bwd_x")
    dproj, gx2 = _ssm_conv_bwd(dbm, proj, cw4, dproj, INNER, "ssm_conv_bwd_b")
    dproj, gx3 = _ssm_conv_bwd(dcm, proj, cw4, dproj, INNER + NG * NS, "ssm_conv_bwd_c")
    g_cw4 = jnp.concatenate([gx1, gx2, gx3], axis=1)
    dproj, g_dtb = _dt_bwd(ddt_g, dproj)
    small = {"b_gate": g_bg[0], "ssm_conv_b": g_cw4[4], "dt_bias": g_dtb[0, :NH],
             "A_log": st[:, 0, :4], "D_skip": st[:, 1, :4], "ssm_norm_w": g_snw[0], "norm_mlp": g_nmlp[0],
             "norm_final": g_nf[0], "sc_conv_w": g_scw[0:3], "ssm_conv_w": g_cw4[0:4], "loss": loss8[0, 0:1]}
    me = 4 * xi + 2 * yi + ci
    sm8 = lax.dynamic_update_slice(jnp.zeros((8, SMALL_ROWS, 128), F32), _pack(small, _SMALL, SMALL_ROWS)[None], (me, 0, 0))
    sm_arr, sm_sems, tok = _split_call("small_start", {"sm": sm8}, start=_all8_plan("sm"))
    g_wc = _matmul(dproj, h, "tn", BF16, 1280, 1024, 2048, "in_proj_dw", dep=tok)
    rs_b = _ReduceScatter([0], {0: g_wc}, idx, "b")
    tok = rs_a.share(after=rs_b.token)
    tok = rs_b.chips(after=tok)
    grad_x, g_nm = _matmul_rms_bwd(dproj, wc, "nn", x0, norm_mix, dx1, 512, 3840, "in_proj_dx", dep=tok)
    nm8 = lax.dynamic_update_slice(jnp.zeros((8, 8, 128), F32), g_nm[0].reshape(1, 8, 128), (me, 0, 0))
    nm_arr, nm_sems, tok = _split_call("norm_mix_start", {"nm": nm8}, start=_all8_plan("nm"))
    sm_arr, _, _ = _split_call("small_wait", sm_arr, wait=_all8_plan("sm"), wait_sems=sm_sems, after=tok)
    small_sum = _sum8(sm_arr["sm"])
    gs = _unpack(small_sum, _SMALL)
    red = rs_a.result(after=tok)
    big = {"w_mlp1": red[1], "w_mlp2": red[2], "w_branch_sc": red[3], "w_branch_ssm": red[4], "w_out": red[5]}

    given = dict(norm_mix=norm_mix, w_in=w_in, b_gate=b_gate, sc_conv_w=sc_conv_w, ssm_conv_w=ssm_conv_w, ssm_conv_b=ssm_conv_b, dt_bias=dt_bias, A_log=A_log, D_skip=D_skip, ssm_norm_w=ssm_norm_w, w_branch_sc=w_branch_sc, w_branch_ssm=w_branch_ssm, w_out=w_out, norm_mlp=norm_mlp, w_mlp1=w_mlp1, w_mlp2=w_mlp2, norm_final=norm_final,
                 m_norm_mix=m_norm_mix, m_w_in=m_w_in, m_b_gate=m_b_gate, m_sc_conv_w=m_sc_conv_w, m_ssm_conv_w=m_ssm_conv_w, m_ssm_conv_b=m_ssm_conv_b, m_dt_bias=m_dt_bias, m_A_log=m_A_log, m_D_skip=m_D_skip, m_ssm_norm_w=m_ssm_norm_w, m_w_branch_sc=m_w_branch_sc, m_w_branch_ssm=m_w_branch_ssm, m_w_out=m_w_out, m_norm_mlp=m_norm_mlp, m_w_mlp1=m_w_mlp1, m_w_mlp2=m_w_mlp2, m_norm_final=m_norm_final,
                 v_norm_mix=v_norm_mix, v_w_in=v_w_in, v_b_gate=v_b_gate, v_sc_conv_w=v_sc_conv_w, v_ssm_conv_w=v_ssm_conv_w, v_ssm_conv_b=v_ssm_conv_b, v_dt_bias=v_dt_bias, v_A_log=v_A_log, v_D_skip=v_D_skip, v_ssm_norm_w=v_ssm_norm_w, v_w_branch_sc=v_w_branch_sc, v_w_branch_ssm=v_w_branch_ssm, v_w_out=v_w_out, v_norm_mlp=v_norm_mlp, v_w_mlp1=v_w_mlp1, v_w_mlp2=v_w_mlp2, v_norm_final=v_norm_final)
    order = ["norm_mix", "w_in", "b_gate", "sc_conv_w", "ssm_conv_w", "ssm_conv_b", "dt_bias", "A_log", "D_skip",
             "ssm_norm_w", "w_branch_sc", "w_branch_ssm", "w_out", "norm_mlp", "w_mlp1", "w_mlp2", "norm_final"]
    grad, delta, new_m, new_v = {}, {}, {}, {}
    for n in big:
        delta[n], new_m[n], new_v[n], grad[n] = _adamw(given[n], big[n], given["m_" + n], given["v_" + n],
                                                       "adamw_" + n, copy_g=True)
    big["w_in"] = None
    grad_small = {n: gs[n].reshape(given[n].shape) for n in small_names if n not in ("sc_conv_w", "ssm_conv_w")}
    grad_small["sc_conv_w"] = lax.dynamic_slice(gs["sc_conv_w"].reshape(3, D), (0, 256 * s), (3, 256))
    grad_small["ssm_conv_w"] = lax.dynamic_slice(gs["ssm_conv_w"].reshape(4, XBC), (0, 1024 * s), (4, 1024))
    table = small_table
    ds_, ms_, vs_ = _adamw(pk_w, _pack(grad_small, table, small_rows), pk_m, pk_v, "adamw_small", tr=small_rows)
    ds_, ms_, vs_ = _unpack(ds_, table), _unpack(ms_, table), _unpack(vs_, table)
    for n in grad_small:
        shp = given[n].shape
        grad[n] = grad_small[n]
        delta[n], new_m[n], new_v[n] = ds_[n].reshape(shp), ms_[n].reshape(shp), vs_[n].reshape(shp)

    done = [new_v[n] for n in ("w_mlp1", "w_mlp2", "w_branch_sc", "w_branch_ssm", "w_out")] + [vs_["b_gate"]]
    tok = rs_b.share(after=done)
    offs = jnp.where(s == 3, jnp.array([24, -8, 744, 2072, -8], jnp.int32),
                     jnp.stack([8 * s, 8 * s, 0 * s, 8 * s, 8 * s]).astype(jnp.int32))
    offs = jnp.concatenate([offs, jnp.stack([7 * ci, 4 - 4 * ci]).astype(jnp.int32)])
    nmain = W_SHARD // 256
    wt_own = (w_in.T, rs_b.arr["f0"], m_w_in.T, v_w_in.T, offs)
    res = _adamw_w_in(*wt_own, "adamw_w_in_own", 0, 256, 4, (0, 1), blk_key=5)
    gp = rs_b.result(after=[tok, res[0]])[0]
    wt_args = (w_in.T, gp, m_w_in.T, v_w_in.T, offs)
    res = _adamw_w_in(*wt_args, "adamw_w_in", 0, 256, nmain - 4, (0, 1), into=res, blk_key=6)
    res = _adamw_w_in(*wt_args, "adamw_w_in_dt", 744, 32, 1, (3,), into=res)
    dt_, mt_, vt_, gwt = _adamw_w_in(*wt_args, "adamw_w_in_tail", 256 * nmain, 8, 1, (4,), into=res)
    grad["w_in"], delta["w_in"], new_m["w_in"], new_v["w_in"] = gwt.T, dt_.T, mt_.T, vt_.T
    nm_arr, _, _ = _split_call("norm_mix_wait", nm_arr, wait=_all8_plan("nm"), wait_sems=nm_sems, after=tok)
    g8 = _sum8(nm_arr["nm"], "norm_mix_sum")
    r8 = lambda a: a.reshape(8, 128)
    d8, m8, v8 = _adamw(r8(norm_mix), g8, r8(m_norm_mix), r8(v_norm_mix), "adamw_norm_mix", tr=8)
    grad["norm_mix"], delta["norm_mix"] = g8.reshape(D), d8.reshape(D)
    new_m["norm_mix"], new_v["norm_mix"] = m8.reshape(D), v8.reshape(D)

    loss = gs["loss"].reshape(())
    return (loss, grad_x.reshape(1, L, D), *[grad[n] for n in order], *[delta[n] for n in order],
            *[new_m[n] for n in order], *[new_v[n] for n in order])
```

```python
import functools

import jax
import jax.numpy as jnp
from jax import lax
from jax.experimental import pallas as pl
from jax.experimental.pallas import tpu as pltpu

F32 = jnp.float32
BF16 = jnp.bfloat16
MESH = pl.DeviceIdType.MESH
HBM = pltpu.HBM

D = 1024
INNER = 2048
HD = 64
NH = 32
NG = 8
NS = 128
Q = 128
GPS = 8
XBC = 4096
DFF = 4096
EPS = 1e-6
W_SHARD = 2824
NCW = 11520
PIECE = 3072
PMAIN = 2816
C_Z, C_XBC, C_GATE, C_DT = 3072, 5120, 9216, 11264
SMALL_ROWS = 256
VMEM_LIMIT = 56 * 1024 * 1024

ADAM_LR, ADAM_B1, ADAM_B2, ADAM_EPS, ADAM_WD, ADAM_STEP = 0.001, 0.9, 0.999, 1e-08, 0.01, 10


def _cp(sem=None, vmem=VMEM_LIMIT):
    return pltpu.CompilerParams(dimension_semantics=sem, vmem_limit_bytes=vmem)


def _sigmoid(v):
    return 1.0 / (1.0 + jnp.exp(-v))


_DIMS = {"nn": (((1,), (0,)), ((), ())), "nt": (((1,), (1,)), ((), ())), "tn": (((0,), (0,)), ((), ()))}


def _matmul(a, b, mode, out_dtype, tm, tn, tk, name, epi=None, extra=None, n_outer=False, dep=None):
    if mode == "tn":
        K, M = a.shape
    else:
        M, K = a.shape
    N = b.shape[0] if mode == "nt" else b.shape[1]
    tm, tn, tk = min(tm, M), min(tn, N), min(tk, K)
    assert M % tm == 0 and N % tn == 0 and K % tk == 0, (name, M, N, K, tm, tn, tk)
    nm, nn, nk = M // tm, N // tn, K // tk
    dims = _DIMS[mode]

    def ij(p0, p1):
        return (p1, p0) if n_outer else (p0, p1)

    if mode == "tn":
        a_spec = pl.BlockSpec((tk, tm), lambda p0, p1, k: (k, ij(p0, p1)[0]))
    else:
        a_spec = pl.BlockSpec((tm, tk), lambda p0, p1, k: (ij(p0, p1)[0], k))
    if mode == "nt":
        b_spec = pl.BlockSpec((tn, tk), lambda p0, p1, k: (ij(p0, p1)[1], k))
    else:
        b_spec = pl.BlockSpec((tk, tn), lambda p0, p1, k: (k, ij(p0, p1)[1]))
    o_spec = pl.BlockSpec((tm, tn), lambda p0, p1, k: ij(p0, p1))
    in_specs = [a_spec, b_spec]
    args = [a, b]
    if epi in ("res", "drelu"):
        in_specs.append(o_spec)
        args.append(extra)
    if dep is not None:
        in_specs.append(pl.BlockSpec(memory_space=pl.ANY))
        args.append(dep)
    n_in = len(args)
    if epi == "relu2":
        out_shape = (jax.ShapeDtypeStruct((M, N), out_dtype), jax.ShapeDtypeStruct((M, N), BF16))
        out_specs = (o_spec, o_spec)
    else:
        out_shape = jax.ShapeDtypeStruct((M, N), out_dtype)
        out_specs = o_spec

    def kern(*refs):
        a_ref, b_ref = refs[0], refs[1]
        e_ref = refs[2] if epi in ("res", "drelu") else None
        acc = refs[-1]
        outs = refs[n_in:-1] if nk > 1 else refs[n_in:]
        k = pl.program_id(2)

        def product():
            return lax.dot_general(a_ref[...].astype(BF16), b_ref[...].astype(BF16), dims, preferred_element_type=F32)

        def finish(r):
            if epi is None:
                outs[0][...] = r.astype(out_dtype)
            elif epi == "res":
                outs[0][...] = (r + e_ref[...]).astype(out_dtype)
            elif epi == "relu2":
                outs[0][...] = r.astype(out_dtype)
                t = jnp.maximum(r, 0.0)
                outs[1][...] = (t * t).astype(BF16)
            else:
                outs[0][...] = (r * (2.0 * jnp.maximum(e_ref[...].astype(F32), 0.0))).astype(out_dtype)

        if nk == 1:
            finish(product())
        else:
            @pl.when(k == 0)
            def _():
                acc[...] = jnp.zeros_like(acc)

            acc[...] += product()

            @pl.when(k == nk - 1)
            def _():
                finish(acc[...])

    grid = (nn, nm, nk) if n_outer else (nm, nn, nk)
    return pl.pallas_call(
        kern, grid=grid, in_specs=in_specs, out_specs=out_specs, out_shape=out_shape,
        scratch_shapes=[pltpu.VMEM((tm, tn), F32)] if nk > 1 else [], name=name,
        compiler_params=_cp(("parallel", "parallel", "arbitrary")),
    )(*args)


def _matmul_res_rms(a, b, x, w, tm, name):
    M, K = a.shape
    tm = min(tm, M)

    def kern(a_ref, b_ref, x_ref, w_ref, x1_ref, h_ref):
        x1 = x_ref[...] + lax.dot_general(a_ref[...].astype(BF16), b_ref[...].astype(BF16), _DIMS["nn"],
                                          preferred_element_type=F32)
        x1_ref[...] = x1
        r = lax.rsqrt(jnp.mean(x1 * x1, axis=-1, keepdims=True) + EPS)
        h_ref[...] = ((x1 * r) * w_ref[...]).astype(BF16)

    row = pl.BlockSpec((tm, D), lambda i: (i, 0))
    return pl.pallas_call(
        kern, grid=(M // tm,),
        in_specs=[pl.BlockSpec((tm, K), lambda i: (i, 0)), pl.BlockSpec((K, D), lambda i: (0, 0)), row,
                  pl.BlockSpec((1, D), lambda i: (0, 0))],
        out_specs=(row, row), out_shape=(jax.ShapeDtypeStruct((M, D), F32), jax.ShapeDtypeStruct((M, D), BF16)),
        name=name, compiler_params=_cp(("parallel",)),
    )(a, b, x, w.reshape(1, D))


def _matmul_res_final(a, b, x, w, tgt, tm, name):
    M, K = a.shape
    tm = min(tm, M)

    def kern(a_ref, b_ref, x_ref, w_ref, t_ref, dx_ref, gw_ref, loss_ref):
        @pl.when(pl.program_id(0) == 0)
        def _():
            gw_ref[...] = jnp.zeros_like(gw_ref)
            loss_ref[...] = jnp.zeros_like(loss_ref)

        xv = x_ref[...] + lax.dot_general(a_ref[...].astype(BF16), b_ref[...].astype(BF16), _DIMS["nn"],
                                          preferred_element_type=F32)
        r = lax.rsqrt(jnp.mean(xv * xv, axis=-1, keepdims=True) + EPS)
        xn = xv * r
        e = xn * w_ref[...] - t_ref[...]
        loss_ref[...] += 0.5 * jnp.sum(jnp.mean(e * e, axis=-1, keepdims=True))
        dyv = e * (1.0 / D)
        gw_ref[...] += jnp.broadcast_to(jnp.sum(dyv * xn, axis=0, keepdims=True), (8, D))
        dxn = dyv * w_ref[...]
        dx_ref[...] = r * (dxn - xn * jnp.mean(dxn * xn, axis=-1, keepdims=True))

    row = pl.BlockSpec((tm, D), lambda i: (i, 0))
    return pl.pallas_call(
        kern, grid=(M // tm,),
        in_specs=[pl.BlockSpec((tm, K), lambda i: (i, 0)), pl.BlockSpec((K, D), lambda i: (0, 0)), row,
                  pl.BlockSpec((1, D), lambda i: (0, 0)), row],
        out_specs=(row, pl.BlockSpec((8, D), lambda i: (0, 0)), pl.BlockSpec((8, 128), lambda i: (0, 0))),
        out_shape=(jax.ShapeDtypeStruct((M, D), F32), jax.ShapeDtypeStruct((8, D), F32),
                   jax.ShapeDtypeStruct((8, 128), F32)),
        name=name, compiler_params=_cp(("arbitrary",)),
    )(a, b, x, w.reshape(1, D), tgt)


def _matmul_rms_bwd(a, b, mode, x, w, res, tm, tk, name, dep=None):
    M, K = a.shape
    tm, tk = min(tm, M), min(tk, K)
    nk = K // tk
    assert M % tm == 0 and K % tk == 0
    b_spec = (pl.BlockSpec((tk, D), lambda k, i: (k, 0)) if mode == "nn" else pl.BlockSpec((D, tk), lambda k, i: (0, k)))
    row = pl.BlockSpec((tm, D), lambda k, i: (jnp.where(k == nk - 1, i, 0), 0))
    deps = [] if dep is None else [dep]

    def kern(a_ref, b_ref, x_ref, w_ref, res_ref, *rest):
        dx_ref, gw_ref, acc = rest[-3:]
        k, i = pl.program_id(0), pl.program_id(1)

        @pl.when((i == 0) & (k == 0))
        def _():
            gw_ref[...] = jnp.zeros_like(gw_ref)

        def product():
            return lax.dot_general(a_ref[...].astype(BF16), b_ref[...].astype(BF16), _DIMS[mode],
                                   preferred_element_type=F32)

        def finish(dyv):
            xv = x_ref[...]
            r = lax.rsqrt(jnp.mean(xv * xv, axis=-1, keepdims=True) + EPS)
            xn = xv * r
            gw_ref[...] += jnp.broadcast_to(jnp.sum(dyv * xn, axis=0, keepdims=True), (8, D))
            dxn = dyv * w_ref[...]
            dx_ref[...] = res_ref[...] + r * (dxn - xn * jnp.mean(dxn * xn, axis=-1, keepdims=True))

        if nk == 1:
            finish(product())
        else:
            rows = pl.ds(pl.multiple_of(i * tm, tm), tm)

            @pl.when(k == 0)
            def _():
                acc[rows, :] = jnp.zeros((tm, D), F32)

            acc[rows, :] += product()

            @pl.when(k == nk - 1)
            def _():
                finish(acc[rows, :])

    return pl.pallas_call(
        kern, grid=(nk, M // tm),
        in_specs=[pl.BlockSpec((tm, tk), lambda k, i: (i, k)), b_spec, row, pl.BlockSpec((1, D), lambda k, i: (0, 0)),
                  row] + [pl.BlockSpec(memory_space=pl.ANY)] * len(deps),
        out_specs=(row, pl.BlockSpec((8, D), lambda k, i: (0, 0))),
        out_shape=(jax.ShapeDtypeStruct((M, D), F32), jax.ShapeDtypeStruct((8, D), F32)),
        scratch_shapes=[pltpu.VMEM((M, D) if nk > 1 else (8, 128), F32)], name=name,
        compiler_params=_cp(("arbitrary", "arbitrary")),
    )(a, b, x, w.reshape(1, D), res, *deps)


def _rms_fwd(x, w, name, tl=256, dep=None):
    L = x.shape[0]

    def kern(x_ref, w_ref, *rest):
        o_ref = rest[-1]
        xv = x_ref[...]
        r = lax.rsqrt(jnp.mean(xv * xv, axis=-1, keepdims=True) + EPS)
        o_ref[...] = ((xv * r) * w_ref[...]).astype(BF16)

    row = pl.BlockSpec((tl, D), lambda i: (i, 0))
    deps = [] if dep is None else [dep]
    return pl.pallas_call(
        kern, grid=(L // tl,),
        in_specs=[row, pl.BlockSpec((1, D), lambda i: (0, 0))] + [pl.BlockSpec(memory_space=pl.ANY)] * len(deps),
        out_specs=row, out_shape=jax.ShapeDtypeStruct((L, D), BF16), name=name, compiler_params=_cp(("parallel",)),
    )(x, w.reshape(1, D), *deps)


def _down(v, k):
    if k == 0:
        return v
    t = lax.broadcasted_iota(jnp.int32, v.shape, 0)
    return jnp.where(t >= k, pltpu.roll(v, k, axis=0), 0.0)


def _up(v, k):
    if k == 0:
        return v
    n = v.shape[0]
    t = lax.broadcasted_iota(jnp.int32, v.shape, 0)
    return jnp.where(t < n - k, pltpu.roll(v, n - k, axis=0), 0.0)


TW = 256


def _sc_fwd(proj, cw):
    L = proj.shape[0]
    nb = D // TW

    def kern(b_ref, c_ref, x_ref, w_ref, o_ref):
        u = c_ref[...].astype(F32) * x_ref[...].astype(F32)
        w = w_ref[...]
        cv = w[0:1] * _down(u, 2) + w[1:2] * _down(u, 1) + w[2:3] * u
        o_ref[...] = (b_ref[...].astype(F32) * cv).astype(BF16)

    col = lambda off: pl.BlockSpec((L, TW), lambda j: (0, off + j))
    return pl.pallas_call(
        kern, grid=(nb,), in_specs=[col(0), col(nb), col(2 * nb), pl.BlockSpec((8, TW), lambda j: (0, j))],
        out_specs=pl.BlockSpec((L, TW), lambda j: (0, j)), out_shape=jax.ShapeDtypeStruct((L, D), BF16),
        name="sc_fwd", compiler_params=_cp(("parallel",)),
    )(proj, proj, proj, cw)


def _sc_bwd(dya, proj, cw, dproj):
    L = proj.shape[0]
    nb = D // TW

    def kern(d_ref, b_ref, c_ref, x_ref, w_ref, _, dp_ref, gw_ref, keep):
        sec = pl.program_id(1)

        @pl.when(sec == 0)
        def _():
            cs, xs, dyv = c_ref[...].astype(F32), x_ref[...].astype(F32), d_ref[...]
            w = w_ref[...]
            u = cs * xs
            u1, u2 = _down(u, 1), _down(u, 2)
            cv = w[0:1] * u2 + w[1:2] * u1 + w[2:3] * u
            dcv = dyv * b_ref[...].astype(F32)
            du = w[2:3] * dcv + w[1:2] * _up(dcv, 1) + w[0:1] * _up(dcv, 2)
            g0 = jnp.sum(dcv * u2, axis=0, keepdims=True)
            g1 = jnp.sum(dcv * u1, axis=0, keepdims=True)
            g2 = jnp.sum(dcv * u, axis=0, keepdims=True)
            row = lax.broadcasted_iota(jnp.int32, (8, TW), 0)
            gw_ref[...] = jnp.where(row == 0, g0, jnp.where(row == 1, g1, jnp.where(row == 2, g2, 0.0)))
            dp_ref[...] = (dyv * cv).astype(BF16)
            keep[0] = (du * xs).astype(BF16)
            keep[1] = (du * cs).astype(BF16)

        @pl.when(sec > 0)
        def _():
            dp_ref[...] = keep[sec - 1]

    col = lambda off: pl.BlockSpec((L, TW), lambda j, s: (0, off + j))
    return pl.pallas_call(
        kern, grid=(nb, 3),
        in_specs=[col(0), col(0), col(nb), col(2 * nb), pl.BlockSpec((8, TW), lambda j, s: (0, j)),
                  pl.BlockSpec(memory_space=pl.ANY)],
        out_specs=(pl.BlockSpec((L, TW), lambda j, s: (0, s * nb + j)), pl.BlockSpec((8, TW), lambda j, s: (0, j))),
        out_shape=(jax.ShapeDtypeStruct(dproj.shape, BF16), jax.ShapeDtypeStruct((8, D), F32)),
        scratch_shapes=[pltpu.VMEM((2, L, TW), BF16)],
        input_output_aliases={5: 0}, name="sc_bwd", compiler_params=_cp(("parallel", "arbitrary")),
    )(dya, proj, proj, proj, cw, dproj)


def _ssm_conv_fwd(proj, cw4):
    L = proj.shape[0]
    off = C_XBC // TW

    def kern(r_ref, w_ref, o_ref):
        raw = r_ref[...].astype(F32)
        w = w_ref[...]
        c4 = w[0:1] * _down(raw, 3) + w[1:2] * _down(raw, 2) + w[2:3] * _down(raw, 1) + w[3:4] * raw + w[4:5]
        o_ref[...] = (c4 * _sigmoid(c4)).astype(BF16)

    return pl.pallas_call(
        kern, grid=(XBC // TW,),
        in_specs=[pl.BlockSpec((L, TW), lambda j: (0, off + j)), pl.BlockSpec((8, TW), lambda j: (0, j))],
        out_specs=pl.BlockSpec((L, TW), lambda j: (0, j)), out_shape=jax.ShapeDtypeStruct((L, XBC), BF16),
        name="ssm_conv_fwd", compiler_params=_cp(("parallel",)),
    )(proj, cw4)


def _ssm_conv_bwd(dx, proj, cw4, dproj, col0, name):
    L, width = dx.shape
    off_p = (C_XBC + col0) // TW
    off_w = col0 // TW

    def kern(d_ref, r_ref, w_ref, _, dp_ref, gw_ref):
        raw = r_ref[...].astype(F32)
        w = w_ref[...]
        r1, r2, r3 = _down(raw, 1), _down(raw, 2), _down(raw, 3)
        c4 = w[0:1] * r3 + w[1:2] * r2 + w[2:3] * r1 + w[3:4] * raw + w[4:5]
        sg = _sigmoid(c4)
        dc4 = d_ref[...] * (sg * (1.0 + c4 * (1.0 - sg)))
        draw = w[3:4] * dc4 + w[2:3] * _up(dc4, 1) + w[1:2] * _up(dc4, 2) + w[0:1] * _up(dc4, 3)
        dp_ref[...] = draw.astype(BF16)
        gs = [jnp.sum(dc4 * r3, axis=0, keepdims=True), jnp.sum(dc4 * r2, axis=0, keepdims=True),
              jnp.sum(dc4 * r1, axis=0, keepdims=True), jnp.sum(dc4 * raw, axis=0, keepdims=True),
              jnp.sum(dc4, axis=0, keepdims=True)]
        row = lax.broadcasted_iota(jnp.int32, (8, TW), 0)
        acc = jnp.zeros((8, TW), F32)
        for k, gk in enumerate(gs):
            acc = jnp.where(row == k, gk, acc)
        gw_ref[...] = acc

    return pl.pallas_call(
        kern, grid=(width // TW,),
        in_specs=[pl.BlockSpec((L, TW), lambda j: (0, j)), pl.BlockSpec((L, TW), lambda j: (0, off_p + j)),
                  pl.BlockSpec((8, TW), lambda j: (0, off_w + j)), pl.BlockSpec(memory_space=pl.ANY)],
        out_specs=(pl.BlockSpec((L, TW), lambda j: (0, off_p + j)), pl.BlockSpec((8, TW), lambda j: (0, j))),
        out_shape=(jax.ShapeDtypeStruct(dproj.shape, BF16), jax.ShapeDtypeStruct((8, width), F32)),
        input_output_aliases={3: 0}, name=name, compiler_params=_cp(("arbitrary",)),
    )(dx, proj, cw4, dproj)


def _split3(v):
    h1 = v.astype(BF16)
    r1 = v - h1.astype(F32)
    h2 = r1.astype(BF16)
    h3 = (r1 - h2.astype(F32)).astype(BF16)
    return h1, h2, h3


def _dot01(m01, v, dims=_DIMS["nn"], m_left=True, terms=3):
    out = None
    for part in _split3(v)[:terms]:
        ops = (m01, part) if m_left else (part, m01)
        t = lax.dot_general(ops[0], ops[1], dims, preferred_element_type=F32)
        out = t if out is None else out + t
    return out


def _bdot(a, b, mode="nn"):
    return lax.dot_general(a.astype(BF16), b.astype(BF16), _DIMS[mode], preferred_element_type=F32)


def _softplus(v):
    return jnp.maximum(v, 0.0) + jnp.log1p(jnp.exp(-jnp.abs(v)))


def _dt_prep(proj, vec):
    L = proj.shape[0]

    def kern(p_ref, v_ref, dt_ref, cs_ref, sg_ref):
        v = v_ref[...]
        pre = p_ref[:, 0:128] + v[0:1]
        dt = _softplus(pre)
        da = dt * (-jnp.exp(v[1:2]))
        ii = lax.broadcasted_iota(jnp.int32, (Q, Q), 0)
        jj = lax.broadcasted_iota(jnp.int32, (Q, Q), 1)
        ltri = (jj <= ii).astype(BF16)
        lane = lax.broadcasted_iota(jnp.int32, (Q, 128), 1)
        for val, ref in ((dt, dt_ref), (_dot01(ltri, da), cs_ref), (_sigmoid(pre), sg_ref)):
            for g in range(NG):
                moved = val if g == 0 else pltpu.roll(val, 128 - 4 * g, axis=1)
                ref[g] = jnp.where(lane < 4, moved, 0.0)

    blk = pl.BlockSpec((NG, Q, 128), lambda c: (0, c, 0))
    return pl.pallas_call(
        kern, grid=(L // Q,),
        in_specs=[pl.BlockSpec((Q, 256), lambda c: (c, 0)), pl.BlockSpec((8, 128), lambda c: (0, 0))],
        out_specs=(blk, blk, blk),
        out_shape=(jax.ShapeDtypeStruct((NG, L, 128), F32),) * 3,
        name="dt_prep", compiler_params=_cp(("parallel",)),
    )(proj, vec)


def _head_masks():
    lane = lax.broadcasted_iota(jnp.int32, (1, 4 * HD), 1)
    return [((lane >= HD * j) & (lane < HD * (j + 1))) for j in range(4)]


def _expand4(v4, masks):
    R = v4.shape[0]
    out = jnp.zeros((R, 4 * HD), F32)
    for j in range(4):
        out = jnp.where(masks[j], jnp.broadcast_to(v4[:, j:j + 1], (R, 4 * HD)), out)
    return out


def _decay_matrix(cs_col, tri):
    colb = jnp.broadcast_to(cs_col, (Q, Q))
    return jnp.exp(jnp.where(tri, colb - colb.T, -jnp.inf))


def _ssd_fwd(xbc, dt4, cs4, vecg):
    L = xbc.shape[0]
    nc = L // Q

    def kern(x_ref, b_ref, c_ref, dt_ref, cs_ref, v_ref, y_ref, s_ref, S):
        c = pl.program_id(1)

        @pl.when(c == 0)
        def _():
            S[...] = jnp.zeros_like(S)

        masks = _head_masks()
        ii = lax.broadcasted_iota(jnp.int32, (Q, Q), 0)
        jj = lax.broadcasted_iota(jnp.int32, (Q, Q), 1)
        tri = jj <= ii
        for gi in range(GPS):
            xs, ns = slice(256 * gi, 256 * (gi + 1)), slice(NS * gi, NS * (gi + 1))
            dt4v, cs4v = dt_ref[gi], cs_ref[gi]
            dt_b, cs_b = _expand4(dt4v, masks), _expand4(cs4v, masks)
            d_b = _expand4(v_ref[gi], masks)[1:2]
            cs_last = cs_b[Q - 1:Q, :]
            x4, bm, cm = x_ref[:, xs].astype(F32), b_ref[:, ns], c_ref[:, ns]
            xdt = x4 * dt_b
            gm = _bdot(cm, bm, "nt")
            s4 = S[gi]
            s_ref[gi, 0] = s4
            y = _bdot(cm, s4) * jnp.exp(cs_b) + d_b * x4
            m_all = jnp.concatenate([(gm * _decay_matrix(cs4v[:, j:j + 1], tri)).astype(BF16) for j in range(4)], axis=0)
            yd = _bdot(m_all, xdt)
            for j in range(4):
                y = y + jnp.where(masks[j], yd[Q * j:Q * (j + 1)], 0.0)
            y_ref[:, xs] = y
            S[gi] = jnp.exp(cs_last) * s4 + _bdot(bm, xdt * jnp.exp(cs_last - cs_b), "tn")

    sc = pl.BlockSpec((GPS, Q, 128), lambda g, c: (g, c, 0))
    bw = NS * GPS
    return pl.pallas_call(
        kern, grid=(NG // GPS, nc),
        in_specs=[pl.BlockSpec((Q, 256 * GPS), lambda g, c: (c, g)),
                  pl.BlockSpec((Q, bw), lambda g, c: (c, INNER // bw + g)),
                  pl.BlockSpec((Q, bw), lambda g, c: (c, (INNER + NG * NS) // bw + g)),
                  sc, sc, pl.BlockSpec((GPS, 8, 128), lambda g, c: (g, 0, 0))],
        out_specs=(pl.BlockSpec((Q, 256 * GPS), lambda g, c: (c, g)),
                   pl.BlockSpec((GPS, 1, NS, 256), lambda g, c: (g, c, 0, 0))),
        out_shape=(jax.ShapeDtypeStruct((L, INNER), F32), jax.ShapeDtypeStruct((NG, nc, NS, 256), F32)),
        scratch_shapes=[pltpu.VMEM((GPS, NS, 256), F32)], name="ssd_fwd",
        compiler_params=_cp(("parallel", "arbitrary")),
    )(xbc, xbc, xbc, dt4, cs4, vecg)


def _ssd_bwd(xbc, dt4, cs4, sg4, vecg, s_all, dy):
    L = xbc.shape[0]
    nc = L // Q

    def kern(x_ref, b_ref, c_ref, dt_ref, cs_ref, sg_ref, v_ref, s_ref, dy_ref,
             dx_ref, db_ref, dc_ref, ddt_ref, st_ref, dS):
        cc = pl.program_id(1)

        @pl.when(cc == 0)
        def _():
            dS[...] = jnp.zeros_like(dS)
            st_ref[...] = jnp.zeros_like(st_ref)

        masks = _head_masks()
        ii = lax.broadcasted_iota(jnp.int32, (Q, Q), 0)
        jj = lax.broadcasted_iota(jnp.int32, (Q, Q), 1)
        tri = jj <= ii
        utri = (jj >= ii).astype(BF16)
        hsel = ((lax.broadcasted_iota(jnp.int32, (4 * HD, 128), 0) // HD)
                == lax.broadcasted_iota(jnp.int32, (4 * HD, 128), 1)).astype(BF16)
        hrow = ((lax.broadcasted_iota(jnp.int32, (4 * Q, 128), 0) // Q)
                == lax.broadcasted_iota(jnp.int32, (4 * Q, 128), 1)).astype(BF16)
        ones_q = jnp.ones((Q, 128), BF16)
        lane128 = lax.broadcasted_iota(jnp.int32, (Q, 128), 1)

        for gi in range(GPS):
            xs, ns = slice(256 * gi, 256 * (gi + 1)), slice(NS * gi, NS * (gi + 1))
            dt4v, cs4v, sg4v = dt_ref[gi], cs_ref[gi], sg_ref[gi]
            dt_b, cs_b = _expand4(dt4v, masks), _expand4(cs4v, masks)
            vv = _expand4(v_ref[gi], masks)
            a_b = -jnp.exp(vv[0:1])
            d_b = vv[1:2]
            a4 = -jnp.exp(v_ref[gi][0:1, :])
            cs_last = cs_b[Q - 1:Q, :]
            ecs = jnp.exp(cs_b)
            decay = jnp.exp(cs_last - cs_b)
            elast = jnp.exp(cs_last)
            x4, bm, cm, dyv = x_ref[:, xs].astype(F32), b_ref[:, ns], c_ref[:, ns], dy_ref[:, xs]
            s4 = s_ref[gi, 0]
            dsn = dS[gi]
            xdt = x4 * dt_b
            gm = _bdot(cm, bm, "nt")
            dye = dyv * ecs
            yoff = ecs * _bdot(cm, s4)
            t4 = _bdot(bm, dsn) * decay
            lms, mhs = [], []
            for j in range(4):
                colb = jnp.broadcast_to(cs4v[:, j:j + 1], (Q, Q))
                lms.append(jnp.exp(jnp.where(tri, colb - colb.T, -jnp.inf)))
                mhs.append(gm * lms[j])
            m_all = jnp.concatenate([m.astype(BF16) for m in mhs], axis=0)
            dy_m = jnp.concatenate([jnp.where(masks[j], dyv, 0.0).astype(BF16) for j in range(4)], axis=0)
            dxdt = t4 + _bdot(m_all, dy_m, "tn")
            dm_all = _bdot(dy_m, xdt, "nt")
            dg = jnp.zeros((Q, Q), F32)
            for j in range(4):
                dg = dg + dm_all[Q * j:Q * (j + 1)] * lms[j]
            e_all = dm_all * jnp.concatenate(mhs, axis=0)
            rsum = _dot01(ones_q, e_all, m_left=False, terms=2)
            da4 = -_dot01(hrow, e_all, _DIMS["tn"], m_left=False, terms=2)
            for j in range(4):
                da4 = da4 + jnp.where(lane128 == j, rsum[Q * j:Q * (j + 1)], 0.0)
            xt = xdt * t4
            tail = jnp.sum(xt, axis=0, keepdims=True) + elast * jnp.sum(s4 * dsn, axis=0, keepdims=True)
            gd_raw = jnp.sum(dyv * x4, axis=0, keepdims=True)
            stacked = jnp.concatenate([dyv * yoff - xt, dxdt * x4, jnp.broadcast_to(tail, (8, 4 * HD)),
                                       jnp.broadcast_to(gd_raw, (8, 4 * HD))], axis=0)
            seg = _dot01(hsel, stacked, m_left=False, terms=2)
            dda4 = _dot01(utri, da4 + seg[0:Q], terms=2) + seg[2 * Q:2 * Q + 1]
            ddt_ref[gi] = (dda4 * a4 + seg[Q:2 * Q]) * sg4v
            ga = jnp.sum(dda4 * dt4v * a4, axis=0, keepdims=True)
            row = lax.broadcasted_iota(jnp.int32, (8, 128), 0)
            st_ref[gi] += jnp.where(row == 0, ga, jnp.where(row == 1, seg[2 * Q + 8:2 * Q + 9], 0.0))
            dx_ref[:, xs] = d_b * dyv + dxdt * dt_b
            dc_ref[:, ns] = _bdot(dg, bm) + _bdot(dye, s4, "nt")
            db_ref[:, ns] = _bdot(dg, cm, "tn") + _bdot(xdt * decay, dsn, "nt")
            dS[gi] = elast * dsn + _bdot(cm, dye, "tn")

    rv = lambda c: nc - 1 - c
    sc = pl.BlockSpec((GPS, Q, 128), lambda g, c: (g, rv(c), 0))
    bw = NS * GPS
    return pl.pallas_call(
        kern, grid=(NG // GPS, nc),
        in_specs=[pl.BlockSpec((Q, 256 * GPS), lambda g, c: (rv(c), g)),
                  pl.BlockSpec((Q, bw), lambda g, c: (rv(c), INNER // bw + g)),
                  pl.BlockSpec((Q, bw), lambda g, c: (rv(c), (INNER + NG * NS) // bw + g)),
                  sc, sc, sc, pl.BlockSpec((GPS, 8, 128), lambda g, c: (g, 0, 0)),
                  pl.BlockSpec((GPS, 1, NS, 256), lambda g, c: (g, rv(c), 0, 0)),
                  pl.BlockSpec((Q, 256 * GPS), lambda g, c: (rv(c), g))],
        out_specs=(pl.BlockSpec((Q, 256 * GPS), lambda g, c: (rv(c), g)),
                   pl.BlockSpec((Q, bw), lambda g, c: (rv(c), g)),
                   pl.BlockSpec((Q, bw), lambda g, c: (rv(c), g)),
                   pl.BlockSpec((GPS, Q, 128), lambda g, c: (g, rv(c), 0)),
                   pl.BlockSpec((GPS, 8, 128), lambda g, c: (g, 0, 0))),
        out_shape=(jax.ShapeDtypeStruct((L, INNER), F32), jax.ShapeDtypeStruct((L, NG * NS), F32),
                   jax.ShapeDtypeStruct((L, NG * NS), F32), jax.ShapeDtypeStruct((NG, L, 128), F32),
                   jax.ShapeDtypeStruct((NG, 8, 128), F32)),
        scratch_shapes=[pltpu.VMEM((GPS, NS, 256), F32)], name="ssd_bwd",
        compiler_params=_cp(("parallel", "arbitrary")),
    )(xbc, xbc, xbc, dt4, cs4, sg4, vecg, s_all, dy)


def _dt_bwd(ddt, dproj, tl=256):
    L = ddt.shape[1]

    def kern(d_ref, _, dp_ref, gs_ref):
        @pl.when(pl.program_id(0) == 0)
        def _():
            gs_ref[...] = jnp.zeros_like(gs_ref)

        d = d_ref[0]
        for g in range(1, NG):
            d = d + pltpu.roll(d_ref[g], 4 * g, axis=1)
        gs_ref[...] += jnp.broadcast_to(jnp.sum(d, axis=0, keepdims=True), (8, 128))
        dp_ref[...] = jnp.concatenate([d, jnp.zeros_like(d)], axis=1).astype(BF16)

    return pl.pallas_call(
        kern, grid=(L // tl,),
        in_specs=[pl.BlockSpec((NG, tl, 128), lambda i: (0, i, 0)), pl.BlockSpec(memory_space=pl.ANY)],
        out_specs=(pl.BlockSpec((tl, 256), lambda i: (i, C_DT // 256)), pl.BlockSpec((8, 128), lambda i: (0, 0))),
        out_shape=(jax.ShapeDtypeStruct(dproj.shape, BF16), jax.ShapeDtypeStruct((8, 128), F32)),
        input_output_aliases={1: 0}, name="dt_bwd", compiler_params=_cp(("arbitrary",)),
    )(ddt, dproj)


GW = INNER // NG


def _gnorm_fwd(y, proj, w, tl=256):
    L = y.shape[0]
    zoff = C_Z // 1024

    def kern(y_ref, z_ref, w_ref, o_ref):
        z = z_ref[...].astype(F32)
        yz = y_ref[...] * (z * _sigmoid(z))
        wv = w_ref[...]
        for k in range(1024 // GW):
            sl = slice(GW * k, GW * (k + 1))
            v = yz[:, sl]
            rg = lax.rsqrt(jnp.mean(v * v, axis=-1, keepdims=True) + EPS)
            o_ref[:, sl] = ((v * rg) * wv[:, sl]).astype(BF16)

    blk = pl.BlockSpec((tl, 1024), lambda i, j: (i, j))
    return pl.pallas_call(
        kern, grid=(L // tl, 2),
        in_specs=[blk, pl.BlockSpec((tl, 1024), lambda i, j: (i, zoff + j)), pl.BlockSpec((1, 1024), lambda i, j: (0, j))],
        out_specs=blk, out_shape=jax.ShapeDtypeStruct((L, INNER), BF16), name="gnorm_fwd",
        compiler_params=_cp(("parallel", "parallel")),
    )(y, proj, w.reshape(1, INNER))


def _gnorm_bwd(dbr, wb, y, proj, w, dproj, dep, tl=512):
    L = y.shape[0]
    tl = min(tl, L)
    zoff = C_Z // 1024

    def kern(d_ref, b_ref, y_ref, z_ref, w_ref, _, __, dy_ref, dp_ref, gw_ref):
        @pl.when(pl.program_id(1) == 0)
        def _():
            gw_ref[...] = jnp.zeros_like(gw_ref)

        z = z_ref[...].astype(F32)
        sg = _sigmoid(z)
        sz = z * sg
        yv = y_ref[...]
        yz = yv * sz
        dv = lax.dot_general(d_ref[0], b_ref[...], _DIMS["nt"], preferred_element_type=F32)
        wv = w_ref[...]
        for k in range(1024 // GW):
            sl = slice(GW * k, GW * (k + 1))
            v = yz[:, sl]
            rg = lax.rsqrt(jnp.mean(v * v, axis=-1, keepdims=True) + EPS)
            vn = v * rg
            dk = dv[:, sl]
            gw_ref[:, sl] += jnp.broadcast_to(jnp.sum(dk * vn, axis=0, keepdims=True), (8, GW))
            dvn = dk * wv[:, sl]
            dyz = rg * (dvn - vn * jnp.mean(dvn * vn, axis=-1, keepdims=True))
            dy_ref[:, sl] = dyz * sz[:, sl]
            dp_ref[:, sl] = (dyz * yv[:, sl] * (sg[:, sl] * (1.0 + z[:, sl] * (1.0 - sg[:, sl])))).astype(BF16)

    blk = pl.BlockSpec((tl, 1024), lambda j, i: (i, j))
    zblk = pl.BlockSpec((tl, 1024), lambda j, i: (i, zoff + j))
    return pl.pallas_call(
        kern, grid=(2, L // tl),
        in_specs=[pl.BlockSpec((1, tl, D), lambda j, i: (1, i, 0)), pl.BlockSpec((1024, D), lambda j, i: (j, 0)),
                  blk, zblk, pl.BlockSpec((1, 1024), lambda j, i: (0, j)), pl.BlockSpec(memory_space=pl.ANY),
                  pl.BlockSpec(memory_space=pl.ANY)],
        out_specs=(blk, zblk, pl.BlockSpec((8, 1024), lambda j, i: (0, j))),
        out_shape=(jax.ShapeDtypeStruct((L, INNER), F32), jax.ShapeDtypeStruct(dproj.shape, BF16),
                   jax.ShapeDtypeStruct((8, INNER), F32)),
        input_output_aliases={5: 1}, name="gnorm_bwd", compiler_params=_cp(("parallel", "arbitrary")),
    )(dbr, wb, y, proj, w.reshape(1, INNER), dproj, dep)


def _merge_fwd(proj, bg, br_a, br_b, tl=256):
    L = proj.shape[0]
    goff = C_GATE // 1024

    def kern(g1_ref, g2_ref, b1_ref, b2_ref, a_ref, b_ref, o_ref):
        g1 = _sigmoid(g1_ref[...].astype(F32) + b1_ref[...])
        g2 = _sigmoid(g2_ref[...].astype(F32) + b2_ref[...])
        o_ref[...] = (g1 * a_ref[...] + g2 * b_ref[...]).astype(BF16)

    row = pl.BlockSpec((tl, 1024), lambda i: (i, 0))
    bg2 = bg.reshape(1, 2 * D)
    return pl.pallas_call(
        kern, grid=(L // tl,),
        in_specs=[pl.BlockSpec((tl, 1024), lambda i: (i, goff)), pl.BlockSpec((tl, 1024), lambda i: (i, goff + 1)),
                  pl.BlockSpec((1, 1024), lambda i: (0, 0)), pl.BlockSpec((1, 1024), lambda i: (0, 1)), row, row],
        out_specs=row, out_shape=jax.ShapeDtypeStruct((L, D), BF16), name="merge_fwd",
        compiler_params=_cp(("parallel",)),
    )(proj, proj, bg2, bg2, br_a, br_b)


def _branch_ssm_merge(yb, wb, proj, bg, br_a, tm=512):
    L, K = yb.shape
    tm = min(tm, L)
    goff = C_GATE // 1024

    def kern(a_ref, b_ref, g1_ref, g2_ref, b1_ref, b2_ref, bra_ref, brb_ref, m_ref):
        brb = lax.dot_general(a_ref[...], b_ref[...], _DIMS["nn"], preferred_element_type=F32)
        brb_ref[...] = brb
        g1 = _sigmoid(g1_ref[...].astype(F32) + b1_ref[...])
        g2 = _sigmoid(g2_ref[...].astype(F32) + b2_ref[...])
        m_ref[...] = (g1 * bra_ref[...] + g2 * brb).astype(BF16)

    row = pl.BlockSpec((tm, D), lambda i: (i, 0))
    bg2 = bg.reshape(1, 2 * D)
    return pl.pallas_call(
        kern, grid=(L // tm,),
        in_specs=[pl.BlockSpec((tm, K), lambda i: (i, 0)), pl.BlockSpec((K, D), lambda i: (0, 0)),
                  pl.BlockSpec((tm, D), lambda i: (i, goff)), pl.BlockSpec((tm, D), lambda i: (i, goff + 1)),
                  pl.BlockSpec((1, D), lambda i: (0, 0)), pl.BlockSpec((1, D), lambda i: (0, 1)), row],
        out_specs=(row, row), out_shape=(jax.ShapeDtypeStruct((L, D), F32), jax.ShapeDtypeStruct((L, D), BF16)),
        name="branch_ssm_merge", compiler_params=_cp(("parallel",)),
    )(yb, wb, proj, proj, bg2, bg2, br_a)


def _merge_bwd(dx1, wo, proj, bg, br_a, br_b, dproj, tl=512):
    L = proj.shape[0]
    tl = min(tl, L)
    goff = C_GATE // 1024

    def kern(dm_ref, wo_ref, g_ref, b_ref, a_ref, bb_ref, _, dbr_ref, dp_ref, gb_ref):
        j = pl.program_id(0)

        @pl.when(pl.program_id(1) == 0)
        def _():
            gb_ref[...] = jnp.zeros_like(gb_ref)

        g = _sigmoid(g_ref[...].astype(F32) + b_ref[...])
        br = jnp.where(j == 0, a_ref[...], bb_ref[...])
        dmv = lax.dot_general(dm_ref[...].astype(BF16), wo_ref[...], _DIMS["nt"], preferred_element_type=F32)
        dbr_ref[0] = (dmv * g).astype(BF16)
        dgate = dmv * br * g * (1.0 - g)
        gb_ref[...] += jnp.broadcast_to(jnp.sum(dgate, axis=0, keepdims=True), (8, 1024))
        dp_ref[...] = dgate.astype(BF16)

    row = pl.BlockSpec((tl, 1024), lambda j, i: (i, 0))
    gblk = pl.BlockSpec((tl, 1024), lambda j, i: (i, goff + j))
    return pl.pallas_call(
        kern, grid=(2, L // tl),
        in_specs=[row, pl.BlockSpec((D, D), lambda j, i: (0, 0)), gblk, pl.BlockSpec((1, 1024), lambda j, i: (0, j)),
                  row, row, pl.BlockSpec(memory_space=pl.ANY)],
        out_specs=(pl.BlockSpec((1, tl, 1024), lambda j, i: (j, i, 0)), gblk, pl.BlockSpec((8, 1024), lambda j, i: (0, j))),
        out_shape=(jax.ShapeDtypeStruct((2, L, D), BF16), jax.ShapeDtypeStruct(dproj.shape, BF16),
                   jax.ShapeDtypeStruct((8, 2 * D), F32)),
        input_output_aliases={6: 1}, name="merge_bwd", compiler_params=_cp(("parallel", "arbitrary")),
    )(dx1, wo, proj, bg.reshape(1, 2 * D), br_a, br_b, dproj)


def _coords():
    return lax.axis_index("x"), lax.axis_index("y"), lax.axis_index("c")


def _other_chips(sk):
    xk, yk = sk // 2, sk % 2
    return [((1 - xk, yk), 2 * (1 - xk) + yk), ((xk, 1 - yk), 2 * xk + 1 - yk), ((1 - xk, 1 - yk), 2 * (1 - xk) + 1 - yk)]


def _rows(start, size):
    assert size % 128 == 0
    return pl.ds(pl.multiple_of(start, 128), size)


def _per_chip(fn):
    x, y, _ = _coords()
    s = 2 * x + y
    for sk in range(4):
        pl.when(s == sk)(functools.partial(fn, sk))


XTRA = PIECE - PMAIN


def _place(shard, full_shape, block, index_map, idx, name, blk0=0, nblk=None, dep=None, into=None):
    in_block = block[-2:]
    if nblk is None:
        nblk = shard.shape[0] // in_block[0]

    def kern(idx_ref, s_ref, *rest):
        o_ref = rest[-1]
        o_ref[...] = s_ref[...].astype(BF16).reshape(o_ref.shape)

    extra = ([dep] if dep is not None else []) + ([into] if into is not None else [])
    grid_spec = pltpu.PrefetchScalarGridSpec(
        num_scalar_prefetch=1, grid=(nblk,),
        in_specs=[pl.BlockSpec(in_block, lambda i, idx_ref: (blk0 + i, 0))] + [_ANY] * len(extra),
        out_specs=pl.BlockSpec(block, index_map))
    aliases = {1 + len(extra): 0} if into is not None else {}
    return pl.pallas_call(kern, grid_spec=grid_spec, out_shape=jax.ShapeDtypeStruct(full_shape, BF16), name=name,
                          input_output_aliases=aliases, compiler_params=_cp(("arbitrary",)))(idx, shard, *extra)


_SEM = pl.BlockSpec(memory_space=pltpu.SEMAPHORE)
_EFFECT = pltpu.SideEffectType.DATAFLOW_SIDE_EFFECTING


_ANY = pl.BlockSpec(memory_space=pl.ANY)


def _tie(v, dep, name):
    def body(v_ref, dep_ref, o_ref):
        del v_ref, dep_ref, o_ref

    return pl.pallas_call(body, out_shape=jax.ShapeDtypeStruct(v.shape, v.dtype), in_specs=[_ANY, _ANY],
                          out_specs=_ANY, input_output_aliases={0: 0}, name=name)(v, dep)


def _split_call(name, arrays, start=None, wait=None, wait_sems=None, after=None):
    keys = list(arrays)
    n = len(keys)
    n_start = start.n if start is not None else 0
    afters = [] if after is None else (list(after) if isinstance(after, (list, tuple)) else [after])

    def body(*refs):
        pos = n
        if wait is not None:
            wss, wrs = refs[pos], refs[pos + 1]
            pos += 2
        pos += len(afters)
        if start is not None:
            nss, nrs = refs[pos], refs[pos + 1]
            pos += 2
        R = dict(zip(keys, refs[pos:pos + n]))
        token = refs[pos + n]
        x, y, c = _coords()

        def desc(src, dst, dev, ss, rs, k):
            return pltpu.make_async_remote_copy(src_ref=src, dst_ref=dst, send_sem=ss.at[k], recv_sem=rs.at[k],
                                                device_id=dev, device_id_type=MESH)

        def run(sk):
            if wait is not None:
                for k, (snd, land) in enumerate(wait.copies(sk, R)):
                    if snd is not None:
                        desc(snd[0], snd[1], snd[2], wss, wrs, k).wait_send()
                    if land is not None:
                        desc(land, land, (x, y, c), wss, wrs, k).wait_recv()
            if start is not None:
                for k, (snd, land) in enumerate(start.copies(sk, R)):
                    if snd is not None:
                        desc(snd[0], snd[1], snd[2], nss, nrs, k).start()

        _per_chip(run)
        token[...] = jnp.zeros_like(token)

    hbm = pl.BlockSpec(memory_space=HBM)
    vals = [arrays[k] for k in keys]
    ins, in_specs = list(vals), [hbm] * n
    if wait is not None:
        ins += list(wait_sems)
        in_specs += [_SEM, _SEM]
    ins += afters
    in_specs += [pl.BlockSpec(memory_space=pl.ANY)] * len(afters)
    out_shape, out_specs = [], []
    if start is not None:
        out_shape += [pltpu.SemaphoreType.DMA((n_start,)), pltpu.SemaphoreType.DMA((n_start,))]
        out_specs += [_SEM, _SEM]
    first = len(out_shape)
    out_shape += [jax.ShapeDtypeStruct(v.shape, v.dtype) for v in vals] + [jax.ShapeDtypeStruct((8, 128), F32)]
    out_specs += [hbm] * n + [pl.BlockSpec(memory_space=pltpu.VMEM)]
    res = pl.pallas_call(
        body, out_shape=tuple(out_shape), in_specs=in_specs, out_specs=tuple(out_specs),
        input_output_aliases={i: first + i for i in range(n)}, name=name,
        compiler_params=pltpu.CompilerParams(has_side_effects=_EFFECT),
    )(*ins)
    sems = (res[0], res[1]) if start is not None else None
    return dict(zip(keys, res[first:first + n])), sems, res[-1]


class _Plan:
    def __init__(self, n, copies):
        self.n, self.copies = n, copies


_HM, _HX = PMAIN // 2, XTRA // 2
WAVE0 = 768
WAVES = ((0, WAVE0), (WAVE0, _HM - WAVE0))
_WIN = {
    "wq0": (True, "wct", lambda r, sc, hc: r.at[_rows(PMAIN * sc + _HM * hc + WAVES[0][0], WAVES[0][1]), :]),
    "wq1": (True, "wct", lambda r, sc, hc: r.at[_rows(PMAIN * sc + _HM * hc + WAVES[1][0], WAVES[1][1]), :]),
    "xt": (True, "xt", lambda r, sc, hc: r.at[sc, _rows(_HX * hc, _HX), :]),
    "w1": (True, "w1", lambda r, sc, hc: r.at[_rows(512 * hc, 512), pl.ds(1024 * sc, 1024)]),
    "w2": (True, "w2", lambda r, sc, hc: r.at[_rows(1024 * sc + 512 * hc, 512), :]),
    "wa": (True, "wa", lambda r, sc, hc: r.at[_rows(256 * sc + 128 * hc, 128), :]),
    "wb": (True, "wb", lambda r, sc, hc: r.at[_rows(512 * sc + 256 * hc, 256), :]),
    "wo": (True, "wo", lambda r, sc, hc: r.at[_rows(256 * sc + 128 * hc, 128), :]),
    "cw": (False, "cw", lambda r, sc, hc: r.at[sc]),
}


_PIECE_SRC = {
    "wq0": lambda p, hc: p.at[_rows(_HM * hc + WAVES[0][0], WAVES[0][1]), :],
    "wq1": lambda p, hc: p.at[_rows(_HM * hc + WAVES[1][0], WAVES[1][1]), :],
    "xt": lambda p, hc: p.at[_rows(PMAIN + _HX * hc, _HX), :],
}


def _ag_chips_plan(keys):
    def copies(sk, R):
        _, _, c = _coords()
        out = []
        for key in keys:
            _, arr, win = _WIN[key]
            for (px, py), ps in _other_chips(sk):
                dst = win(R[arr], sk, c)
                src = _PIECE_SRC[key](R["piece"], c) if key in _PIECE_SRC else dst
                out.append(((src, dst, (px, py, c)), win(R[arr], ps, c)))
        return out
    return _Plan(3 * len(keys), copies)


def _ag_sibling_plan(keys):
    keys = [k for k in keys if _WIN[k][0]]

    def copies(sk, R):
        x, y, c = _coords()
        out = []
        for key in keys:
            _, arr, win = _WIN[key]
            for _, ps in _other_chips(sk):
                w = win(R[arr], ps, c)
                out.append(((w, w, (x, y, 1 - c)), win(R[arr], ps, 1 - c)))
        return out
    return _Plan(3 * len(keys), copies)


def _in_proj_wave(h, wct, wave, proj=None, tm=2048):
    L = h.shape[0]
    tm = min(tm, L)
    off, size = WAVES[wave]
    start = lambda j: pl.multiple_of(_HM * j + off, 128)

    def kern(h_ref, w_ref, *rest):
        o_ref = rest[-1]
        o_ref[...] = lax.dot_general(h_ref[...], w_ref[...], _DIMS["nt"], preferred_element_type=F32).astype(BF16)

    in_specs = [pl.BlockSpec((tm, D), lambda j, i: (i, 0)),
                pl.BlockSpec((pl.Element(size), pl.Element(D)), lambda j, i: (start(j), 0))]
    args, aliases = [h, wct], {}
    if proj is not None:
        in_specs.append(pl.BlockSpec(memory_space=pl.ANY))
        args.append(proj)
        aliases = {2: 0}
    return pl.pallas_call(
        kern, grid=(8, L // tm), in_specs=in_specs,
        out_specs=pl.BlockSpec((pl.Element(tm), pl.Element(size)), lambda j, i: (i * tm, start(j))),
        out_shape=jax.ShapeDtypeStruct((L, NCW), BF16), input_output_aliases=aliases,
        name="in_proj_wave%d" % wave, compiler_params=_cp(("parallel", "parallel")),
    )(*args)


def _fix_wct(wct, xt):
    nb = PMAIN // XTRA

    def kern(w_ref, x_ref, o_ref):
        k = pl.program_id(0)
        xv = x_ref[0]
        o_ref[...] = jnp.where(k < 3, (w_ref[...].astype(F32) + xv.astype(F32)).astype(BF16), xv)

    blk = pl.BlockSpec((XTRA, D), lambda k: (nb * (k + 1), 0))
    rblk = pl.BlockSpec((XTRA, D), lambda k: (jnp.where(k < 3, nb * (k + 1), 0), 0))
    return pl.pallas_call(
        kern, grid=(4,), in_specs=[rblk, pl.BlockSpec((1, XTRA, D), lambda k: (k, 0, 0))], out_specs=blk,
        out_shape=jax.ShapeDtypeStruct(wct.shape, BF16), input_output_aliases={0: 0}, name="fix_wct",
        compiler_params=_cp(("arbitrary",)),
    )(wct, xt)


_HP = PIECE // 2
_GWIN = [
    lambda r, sc, hc: r.at[_rows(PMAIN * sc + _HP * hc, _HP), :],
    lambda r, sc, hc: r.at[_rows(512 * hc, 512), pl.ds(1024 * sc, 1024)],
    lambda r, sc, hc: r.at[_rows(1024 * sc + 512 * hc, 512), :],
    lambda r, sc, hc: r.at[_rows(256 * sc + 128 * hc, 128), :],
    lambda r, sc, hc: r.at[_rows(512 * sc + 256 * hc, 256), :],
    lambda r, sc, hc: r.at[_rows(256 * sc + 128 * hc, 128), :],
]
HALF_SHAPES = [(PIECE // 2, D), (512, 1024), (512, 1024), (128, 1024), (256, 1024), (128, 1024)]


def _rs_sibling_plan(ts):
    def copies(sk, R):
        x, y, c = _coords()
        out = []
        for t in ts:
            for sc in range(4):
                land = R["ra%d" % t].at[sc]
                out.append(((_GWIN[t](R["g%d" % t], sc, 1 - c), land, (x, y, 1 - c)), land))
        return out
    return _Plan(4 * len(ts), copies)


def _rs_chips_plan(ts):
    def copies(sk, R):
        _, _, c = _coords()
        out = []
        for t in ts:
            for j, ((px, py), ps) in enumerate(_other_chips(sk)):
                land = R["rb%d" % t].at[j]
                out.append(((R["hb%d" % t].at[ps], land, (px, py, c)), land))
        return out
    return _Plan(3 * len(ts), copies)


def _rs_share_plan(ts):
    def copies(sk, R):
        x, y, c = _coords()
        out = []
        for t in ts:
            rows = HALF_SHAPES[t][0]
            mine = R["f%d" % t].at[_rows(rows * c, rows), :]
            out.append(((mine, mine, (x, y, 1 - c)), R["f%d" % t].at[_rows(rows * (1 - c), rows), :]))
        return out
    return _Plan(len(ts), copies)


def _half_tiling(t):
    rows, cols = HALF_SHAPES[t]
    if t == 0:
        return (rows // 2, cols), 2, lambda i: (i, 0)
    return (rows, cols), 1, lambda i: (0, 0)


def _window_spec(t, blk):
    if t == 0:
        return pl.BlockSpec((pl.Element(blk[0]), pl.Element(blk[1])), lambda i, sc, idx_ref: (
            pl.multiple_of(PMAIN * sc + _HP * idx_ref[1] + blk[0] * i, 128), 0))
    if t == 1:
        return pl.BlockSpec(blk, lambda i, sc, idx_ref: (idx_ref[1], sc))
    return pl.BlockSpec(blk, lambda i, sc, idx_ref: (2 * sc + idx_ref[1], 0))


def _chip_sum(g, ra, t, idx, name):
    rows, cols = HALF_SHAPES[t]
    blk, nblk, inner = _half_tiling(t)

    def kern(idx_ref, g_ref, r_ref, hb_ref, hf_ref):
        v = g_ref[...].astype(F32) + r_ref[0].astype(F32)
        hb_ref[0] = v.astype(BF16)

        @pl.when(pl.program_id(1) == idx_ref[0])
        def _():
            hf_ref[...] = v

    omap = lambda i, sc, idx_ref: (sc,) + inner(i)
    grid_spec = pltpu.PrefetchScalarGridSpec(
        num_scalar_prefetch=1, grid=(nblk, 4),
        in_specs=[_window_spec(t, blk), pl.BlockSpec((1,) + blk, omap)],
        out_specs=(pl.BlockSpec((1,) + blk, omap), pl.BlockSpec(blk, lambda i, sc, idx_ref: inner(i))))
    return pl.pallas_call(
        kern, grid_spec=grid_spec,
        out_shape=(jax.ShapeDtypeStruct((4, rows, cols), BF16), jax.ShapeDtypeStruct((rows, cols), F32)),
        name=name, compiler_params=_cp(("parallel", "arbitrary")),
    )(idx, g, ra)


def _final_sum(hf, rb, t, idx, name):
    rows, cols = HALF_SHAPES[t]
    blk, nblk, inner = _half_tiling(t)
    nbr = rows // blk[0]

    def kern(idx_ref, h_ref, r_ref, o_ref):
        o_ref[...] = ((h_ref[...] + r_ref[0].astype(F32)) + r_ref[1].astype(F32)) + r_ref[2].astype(F32)

    def omap(i, idx_ref):
        r, cidx = inner(i)
        return nbr * idx_ref[1] + r, cidx

    grid_spec = pltpu.PrefetchScalarGridSpec(
        num_scalar_prefetch=1, grid=(nblk,),
        in_specs=[pl.BlockSpec(blk, lambda i, idx_ref: inner(i)),
                  pl.BlockSpec((3,) + blk, lambda i, idx_ref: (0,) + inner(i))],
        out_specs=pl.BlockSpec(blk, omap))
    return pl.pallas_call(
        kern, grid_spec=grid_spec, out_shape=jax.ShapeDtypeStruct((2 * rows, cols), F32),
        name=name, compiler_params=_cp(("parallel",)),
    )(idx, hf, rb)


class _ReduceScatter:
    def __init__(self, ts, grads, idx, tag):
        self.ts, self.idx, self.tag = ts, idx, tag
        arr = {}
        for t in ts:
            arr["g%d" % t] = grads[t]
            arr["ra%d" % t] = lax.empty((4,) + HALF_SHAPES[t], BF16)
        self.plan = _rs_sibling_plan(ts)
        self.arr, self.sems, self.token = _split_call("rs_sibling_start_" + tag, arr, start=self.plan)

    def chips(self, after):
        arr, _, _ = _split_call("rs_sibling_wait_" + self.tag, self.arr, wait=self.plan, wait_sems=self.sems, after=after)
        brr, self.hf = {}, {}
        for t in self.ts:
            hb, self.hf[t] = _chip_sum(arr["g%d" % t], arr["ra%d" % t], t, self.idx, "chip_sum_%d" % t)
            brr["hb%d" % t] = hb
            brr["rb%d" % t] = lax.empty((3,) + HALF_SHAPES[t], BF16)
        self.plan = _rs_chips_plan(self.ts)
        self.arr, self.sems, self.token = _split_call("rs_chips_start_" + self.tag, brr, start=self.plan)
        return self.token

    def share(self, after):
        brr, _, _ = _split_call("rs_chips_wait_" + self.tag, self.arr, wait=self.plan, wait_sems=self.sems, after=after)
        frr = {"f%d" % t: _final_sum(self.hf[t], brr["rb%d" % t], t, self.idx, "final_sum_%d" % t) for t in self.ts}
        self.plan = _rs_share_plan(self.ts)
        self.arr, self.sems, self.token = _split_call("rs_share_start_" + self.tag, frr, start=self.plan)
        return self.token

    def result(self, after):
        frr, _, _ = _split_call("rs_share_wait_" + self.tag, self.arr, wait=self.plan, wait_sems=self.sems, after=after)
        return {t: frr["f%d" % t] for t in self.ts}


def _all8_plan(key):
    def copies(sk, R):
        x, y, c = _coords()
        own = R[key].at[4 * x + 2 * y + c]
        out = []
        for k in range(1, 8):
            dev = ((1 - x) if (k >> 2) & 1 else x, (1 - y) if (k >> 1) & 1 else y, (1 - c) if k & 1 else c)
            out.append(((own, own, dev), R[key].at[4 * dev[0] + 2 * dev[1] + dev[2]]))
        return out
    return _Plan(7, copies)


def _sum8(v, name="small_sum"):
    def kern(v_ref, o_ref):
        acc = v_ref[0]
        for k in range(1, 8):
            acc = acc + v_ref[k]
        o_ref[...] = acc

    return pl.pallas_call(kern, out_shape=jax.ShapeDtypeStruct(v.shape[1:], F32), name=name)(v)


def _adamw(w, g, m, v, name, tr=128, blk0=0, nblk=None, into=None, copy_g=False):
    R, C = w.shape
    tr = min(tr, R)
    if nblk is None:
        assert R % tr == 0 and blk0 == 0
        nblk = R // tr
    n_out = 4 if copy_g else 3

    def kern(*refs):
        w_ref, g_ref, m_ref, v_ref = refs[:4]
        d_ref, mo_ref, vo_ref = refs[-n_out:][:3]
        gv = g_ref[...]
        mn = ADAM_B1 * m_ref[...] + (1.0 - ADAM_B1) * gv
        vn = ADAM_B2 * v_ref[...] + (1.0 - ADAM_B2) * (gv * gv)
        m_hat = mn / (1.0 - ADAM_B1 ** ADAM_STEP)
        v_hat = vn / (1.0 - ADAM_B2 ** ADAM_STEP)
        d_ref[...] = -ADAM_LR * (m_hat / (jnp.sqrt(v_hat) + ADAM_EPS) + ADAM_WD * w_ref[...])
        mo_ref[...] = mn
        vo_ref[...] = vn
        if copy_g:
            refs[-1][...] = gv

    blk = pl.BlockSpec((tr, C), lambda i: (blk0 + i, 0))
    sd = jax.ShapeDtypeStruct((R, C), F32)
    in_specs, args, aliases = [blk] * 4, [w, g, m, v], {}
    if into is not None:
        in_specs += [pl.BlockSpec(memory_space=pl.ANY)] * 3
        args += list(into)
        aliases = {4: 0, 5: 1, 6: 2}
    return pl.pallas_call(kern, grid=(nblk,), in_specs=in_specs, out_specs=(blk,) * n_out, out_shape=(sd,) * n_out,
                          input_output_aliases=aliases, name=name, compiler_params=_cp(("parallel",)))(*args)


def _adamw_w_in(wt, gp, mt, vt, offs, name, r0, tr, nblk, views, into=None, blk_key=None):
    el = lambda n: (pl.Element(n), pl.Element(D))
    first = (lambda o: r0) if blk_key is None else (lambda o: tr * o[blk_key])
    own = pl.BlockSpec(el(tr), lambda i, o: (pl.multiple_of(first(o) + tr * i, 8), 0))

    def view(k):
        return pl.BlockSpec(el(tr), lambda i, o: (pl.multiple_of(jnp.maximum(first(o) + tr * i + o[k], 0), 8), 0))

    def kern(o_ref, w_ref, m_ref, v_ref, *refs):
        g_refs, (d_ref, mo_ref, vo_ref, go_ref) = refs[:len(views)], refs[-4:]
        gv = g_refs[0][...]
        if len(views) == 2:
            row = first(o_ref) + tr * pl.program_id(0) + lax.broadcasted_iota(jnp.int32, (tr, D), 0)
            gv = jnp.where(row < o_ref[2], gv, g_refs[1][...])
        mn = ADAM_B1 * m_ref[...] + (1.0 - ADAM_B1) * gv
        vn = ADAM_B2 * v_ref[...] + (1.0 - ADAM_B2) * (gv * gv)
        m_hat = mn / (1.0 - ADAM_B1 ** ADAM_STEP)
        v_hat = vn / (1.0 - ADAM_B2 ** ADAM_STEP)
        d_ref[...] = -ADAM_LR * (m_hat / (jnp.sqrt(v_hat) + ADAM_EPS) + ADAM_WD * w_ref[...])
        mo_ref[...] = mn
        vo_ref[...] = vn
        go_ref[...] = gv

    in_specs = [own, own, own] + [view(k) for k in views]
    args = [wt, mt, vt] + [gp] * len(views)
    aliases = {}
    if into is not None:
        in_specs += [pl.BlockSpec(memory_space=pl.ANY)] * 4
        args += list(into)
        aliases = {1 + len(args) - 4 + j: j for j in range(4)}
    grid_spec = pltpu.PrefetchScalarGridSpec(num_scalar_prefetch=1, grid=(nblk,), in_specs=in_specs,
                                             out_specs=(own,) * 4)
    sd = jax.ShapeDtypeStruct(wt.shape, F32)
    return pl.pallas_call(kern, grid_spec=grid_spec, out_shape=(sd,) * 4, input_output_aliases=aliases, name=name,
                          compiler_params=_cp(("parallel",)))(offs, *args)


def _to_piece(wt, s):
    z = lambda n: jnp.zeros((n, D), wt.dtype)
    pads = [functools.partial(lambda k, w: jnp.pad(w, ((8 * k, PIECE - W_SHARD - 8 * k), (0, 0))).astype(BF16), k)
            for k in range(3)]
    last = lambda w: jnp.concatenate([z(24), w[:744], w[776:], w[744:776], z(PIECE - 24 - W_SHARD)], axis=0).astype(BF16)
    return lax.switch(s, pads + [last], wt)


_SMALL = [("b_gate", 2048), ("ssm_conv_b", 4096), ("dt_bias", 32), ("A_log", 32), ("D_skip", 32),
          ("ssm_norm_w", 2048), ("norm_mlp", 1024), ("norm_final", 1024), ("sc_conv_w", 3072), ("ssm_conv_w", 16384),
          ("loss", 1)]


def _pack(vals, table, rows):
    parts = []
    for name, n in table:
        v = vals[name].reshape(-1).astype(F32)
        pad = (-n) % 128
        parts.append(jnp.pad(v, (0, pad)) if pad else v)
    flat = jnp.concatenate(parts)
    return jnp.pad(flat, (0, rows * 128 - flat.shape[0])).reshape(rows, 128)


def _unpack(arr, table):
    flat = arr.reshape(-1)
    out, off = {}, 0
    for name, n in table:
        out[name] = flat[off:off + n]
        off += n + ((-n) % 128)
    return out


def kernel(x, norm_mix, w_in, b_gate, sc_conv_w, ssm_conv_w, ssm_conv_b, dt_bias, A_log, D_skip, ssm_norm_w, w_branch_sc, w_branch_ssm, w_out, norm_mlp, w_mlp1, w_mlp2, norm_final, loss_target, m_norm_mix, m_w_in, m_b_gate, m_sc_conv_w, m_ssm_conv_w, m_ssm_conv_b, m_dt_bias, m_A_log, m_D_skip, m_ssm_norm_w, m_w_branch_sc, m_w_branch_ssm, m_w_out, m_norm_mlp, m_w_mlp1, m_w_mlp2, m_norm_final, v_norm_mix, v_w_in, v_b_gate, v_sc_conv_w, v_ssm_conv_w, v_ssm_conv_b, v_dt_bias, v_A_log, v_D_skip, v_ssm_norm_w, v_w_branch_sc, v_w_branch_ssm, v_w_out, v_norm_mlp, v_w_mlp1, v_w_mlp2, v_norm_final):
    L = x.shape[1]
    nc = L // Q
    xi, yi, ci = lax.axis_index("x"), lax.axis_index("y"), lax.axis_index("c")
    s = 2 * xi + yi
    idx = jnp.stack([s, ci]).astype(jnp.int32)
    x0 = x.reshape(L, D)
    tgt = loss_target.reshape(L, D)
    small_names = ["b_gate", "sc_conv_w", "ssm_conv_w", "ssm_conv_b", "dt_bias", "A_log", "D_skip", "ssm_norm_w",
                   "norm_mlp", "norm_final"]
    small_wmv = [dict(zip(small_names, vals)) for vals in (
        (b_gate, sc_conv_w, ssm_conv_w, ssm_conv_b, dt_bias, A_log, D_skip, ssm_norm_w, norm_mlp, norm_final),
        (m_b_gate, m_sc_conv_w, m_ssm_conv_w, m_ssm_conv_b, m_dt_bias, m_A_log, m_D_skip, m_ssm_norm_w, m_norm_mlp,
         m_norm_final),
        (v_b_gate, v_sc_conv_w, v_ssm_conv_w, v_ssm_conv_b, v_dt_bias, v_A_log, v_D_skip, v_ssm_norm_w, v_norm_mlp,
         v_norm_final))]
    small_table = [(n, int(small_wmv[0][n].size)) for n in small_names]
    small_rows = 136
    pk_w, pk_m, pk_v = [_pack(d, small_table, small_rows) for d in small_wmv]

    piece = _to_piece(w_in.T, s)
    nb = PMAIN // XTRA
    cws = jnp.zeros((8, 1280), F32)
    cws = cws.at[0:3, 0:256].set(sc_conv_w).at[0:4, 256:1280].set(ssm_conv_w)
    cw0 = lax.dynamic_update_slice(jnp.zeros((4, 8, 1280), F32), cws[None], (s, 0, 0))
    win_keys, win2_keys, mid_keys, end_keys = ["xt", "cw", "wq0"], ["wq1"], ["wa", "wb", "wo", "w1"], ["w2"]
    gw, sems_w, tok = _split_call(
        "ag_win_start", {"wct": lax.empty((NCW, D), BF16), "xt": lax.empty((4, XTRA, D), BF16), "cw": cw0, "piece": piece},
        start=_ag_chips_plan(win_keys))
    g2, sems_w2, tok = _split_call("ag_win2_start", {"wct": gw["wct"], "piece": gw["piece"]},
                                   start=_ag_chips_plan(win2_keys), after=tok)
    piece = g2["piece"]
    gw["wct"] = _place(piece, (NCW, D), (XTRA, D), lambda i, r: (nb * r[0] + i, 0), idx, "place_wct", nblk=nb,
                       dep=tok, into=g2["wct"])
    gw["xt"] = _place(piece, (4, XTRA, D), (1, XTRA, D), lambda i, r: (r[0], 0, 0), idx, "place_xt", blk0=nb, nblk=1,
                      dep=tok, into=gw["xt"])
    gw["piece"] = piece
    wa0 = _place(w_branch_sc, (D, D), (256, 1024), lambda i, r: (r[0], 0), idx, "place_wa", dep=tok)
    wb0 = _place(w_branch_ssm, (INNER, D), (512, 1024), lambda i, r: (r[0], 0), idx, "place_wb", dep=tok)
    wo0 = _place(w_out, (D, D), (256, 1024), lambda i, r: (r[0], 0), idx, "place_wo", dep=tok)
    w10 = _place(w_mlp1, (D, DFF), (256, 1024), lambda i, r: (i, r[0]), idx, "place_w1", dep=tok)
    gm, sems_m, tok = _split_call("ag_mid_start", {"wa": wa0, "wb": wb0, "wo": wo0, "w1": w10},
                                  start=_ag_chips_plan(mid_keys))
    w20 = _place(w_mlp2, (DFF, D), (256, 1024), lambda i, r: (4 * r[0] + i, 0), idx, "place_w2", dep=tok)
    ge, sems_e, tok = _split_call("ag_end_start", {"w2": w20}, start=_ag_chips_plan(end_keys))
    h = _rms_fwd(x0, norm_mix, "rms_mix", dep=tok)
    gw, sems_w, tok = _split_call("ag_win_pass", gw, wait=_ag_chips_plan(win_keys), wait_sems=sems_w,
                                  start=_ag_sibling_plan(win_keys), after=[h, pk_w, pk_m, pk_v])
    gw, _, _ = _split_call("ag_win_done", gw, wait=_ag_sibling_plan(win_keys), wait_sems=sems_w, after=tok)
    wc, cw_all = _fix_wct(gw["wct"], gw["xt"]), gw["cw"]
    sc_w_full = jnp.concatenate([cw_all[k, :, 0:256] for k in range(4)], axis=1)
    ssm_w_full = jnp.concatenate([cw_all[k, :, 256:1280] for k in range(4)], axis=1)
    cw4 = ssm_w_full.at[4].set(ssm_conv_b)
    vec = jnp.zeros((8, 128), F32).at[0, :NH].set(dt_bias).at[1, :NH].set(A_log)
    vecg = jnp.zeros((NG, 8, 128), F32).at[:, 0, :4].set(A_log.reshape(NG, 4)).at[:, 1, :4].set(D_skip.reshape(NG, 4))

    dtraw = _matmul(h, wc[C_DT:], "nt", F32, 512, 256, 1024, "in_proj_dt")
    proj = _in_proj_wave(h, wc, 0)
    g2, sems_w2, tok = _split_call("ag_win2_pass", {"wct": wc, "piece": gw["piece"]},
                                   wait=_ag_chips_plan(win2_keys), wait_sems=sems_w2,
                                   start=_ag_sibling_plan(win2_keys), after=[proj, dtraw])
    g2, _, _ = _split_call("ag_win2_done", g2, wait=_ag_sibling_plan(win2_keys), wait_sems=sems_w2, after=tok)
    wc = g2["wct"]
    proj = _in_proj_wave(h, wc, 1, proj=proj)
    ya = _sc_fwd(proj, sc_w_full)
    xbc = _ssm_conv_fwd(proj, cw4)
    dt4, cs4, sg4 = _dt_prep(dtraw, vec)
    y, s_all = _ssd_fwd(xbc, dt4, cs4, vecg)
    gm, sems_m, tok = _split_call("ag_mid_pass", gm, wait=_ag_chips_plan(mid_keys), wait_sems=sems_m,
                                  start=_ag_sibling_plan(mid_keys), after=[y, ya])
    y = _tie(y, tok, "tie_y")
    yb = _gnorm_fwd(y, proj, ssm_norm_w)
    gm, _, _ = _split_call("ag_mid_done", gm, wait=_ag_sibling_plan(mid_keys), wait_sems=sems_m, after=yb)
    wa, wb, wo, w1 = gm["wa"], gm["wb"], gm["wo"], gm["w1"]
    ge, sems_e, tok = _split_call("ag_end_pass", ge, wait=_ag_chips_plan(end_keys), wait_sems=sems_e,
                                  start=_ag_sibling_plan(end_keys), after=yb)
    br_a = _matmul(ya, wa, "nn", F32, 1024, 1024, 1024, "branch_sc", dep=tok)
    br_b, merged = _branch_ssm_merge(yb, wb, proj, b_gate, br_a)
    x1, h2 = _matmul_res_rms(merged, wo, x0, norm_mlp, 1024, "out_proj")
    a1, rl = _matmul(h2, w1, "nn", BF16, 1024, 1024, 1024, "mlp1", epi="relu2", n_outer=True)
    ge, _, _ = _split_call("ag_end_done", ge, wait=_ag_sibling_plan(end_keys), wait_sems=sems_e, after=a1)
    w2 = ge["w2"]
    dx2, g_nf, loss8 = _matmul_res_final(rl, w2, x1, norm_final, tgt, 512, "mlp2")

    da = _matmul(dx2, w2, "nt", BF16, 1024, 1024, 1024, "mlp2_dx", epi="drelu", extra=a1, n_outer=True)
    g_w2 = _matmul(rl, dx2, "tn", BF16, 1024, 1024, 2048, "mlp2_dw")
    g_w1 = _matmul(h2, da, "tn", BF16, 1024, 1024, 2048, "mlp1_dw")
    dx1, g_nmlp = _matmul_rms_bwd(da, w1, "nt", x1, norm_mlp, dx2, 512, 4096, "mlp1_dx")
    g_wo = _matmul(merged, dx1, "tn", BF16, 1024, 1024, 2048, "out_proj_dw")
    dproj = lax.empty((L, NCW), BF16)
    dbr, dproj, g_bg = _merge_bwd(dx1, wo, proj, b_gate, br_a, br_b, dproj)
    dya = _matmul(dbr[0], wa, "nt", F32, 1024, 1024, 1024, "branch_sc_dx")
    g_wa = _matmul(ya, dbr[0], "tn", BF16, 1024, 1024, 2048, "branch_sc_dw")
    dproj, g_scw = _sc_bwd(dya, proj, sc_w_full, dproj)
    g_wb = _matmul(yb, dbr[1], "tn", BF16, 1024, 1024, 2048, "branch_ssm_dw")
    rs_a = _ReduceScatter([1, 2, 3, 4, 5], {1: g_w1, 2: g_w2, 3: g_wa, 4: g_wb, 5: g_wo}, idx, "a")
    dy, dproj, g_snw = _gnorm_bwd(dbr, wb, y, proj, ssm_norm_w, dproj, rs_a.token)
    tok = rs_a.chips(after=dy)
    dxs, dbm, dcm, ddt_g, st = _ssd_bwd(xbc, dt4, cs4, sg4, vecg, s_all, _tie(dy, tok, "tie_dy"))
    dproj, gx1 = _ssm_conv_bwd(dxs, proj, cw4, dproj, 0, "ssm_conv_bwd_x")
    dproj, gx2 = _ssm_conv_bwd(dbm, proj, cw4, dproj, INNER, "ssm_conv_bwd_b")
    dproj, gx3 = _ssm_conv_bwd(dcm, proj, cw4, dproj, INNER + NG * NS, "ssm_conv_bwd_c")
    g_cw4 = jnp.concatenate([gx1, gx2, gx3], axis=1)
    dproj, g_dtb = _dt_bwd(ddt_g, dproj)
    small = {"b_gate": g_bg[0], "ssm_conv_b": g_cw4[4], "dt_bias": g_dtb[0, :NH],
             "A_log": st[:, 0, :4], "D_skip": st[:, 1, :4], "ssm_norm_w": g_snw[0], "norm_mlp": g_nmlp[0],
             "norm_final": g_nf[0], "sc_conv_w": g_scw[0:3], "ssm_conv_w": g_cw4[0:4], "loss": loss8[0, 0:1]}
    me = 4 * xi + 2 * yi + ci
    sm8 = lax.dynamic_update_slice(jnp.zeros((8, SMALL_ROWS, 128), F32), _pack(small, _SMALL, SMALL_ROWS)[None], (me, 0, 0))
    sm_arr, sm_sems, tok = _split_call("small_start", {"sm": sm8}, start=_all8_plan("sm"))
    g_wc = _matmul(dproj, h, "tn", BF16, 1280, 1024, 2048, "in_proj_dw", dep=tok)
    rs_b = _ReduceScatter([0], {0: g_wc}, idx, "b")
    tok = rs_a.share(after=rs_b.token)
    tok = rs_b.chips(after=tok)
    grad_x, g_nm = _matmul_rms_bwd(dproj, wc, "nn", x0, norm_mix, dx1, 512, 3840, "in_proj_dx", dep=tok)
    nm8 = lax.dynamic_update_slice(jnp.zeros((8, 8, 128), F32), g_nm[0].reshape(1, 8, 128), (me, 0, 0))
    nm_arr, nm_sems, tok = _split_call("norm_mix_start", {"nm": nm8}, start=_all8_plan("nm"))
    sm_arr, _, _ = _split_call("small_wait", sm_arr, wait=_all8_plan("sm"), wait_sems=sm_sems, after=tok)
    small_sum = _sum8(sm_arr["sm"])
    gs = _unpack(small_sum, _SMALL)
    red = rs_a.result(after=tok)
    big = {"w_mlp1": red[1], "w_mlp2": red[2], "w_branch_sc": red[3], "w_branch_ssm": red[4], "w_out": red[5]}

    given = dict(norm_mix=norm_mix, w_in=w_in, b_gate=b_gate, sc_conv_w=sc_conv_w, ssm_conv_w=ssm_conv_w, ssm_conv_b=ssm_conv_b, dt_bias=dt_bias, A_log=A_log, D_skip=D_skip, ssm_norm_w=ssm_norm_w, w_branch_sc=w_branch_sc, w_branch_ssm=w_branch_ssm, w_out=w_out, norm_mlp=norm_mlp, w_mlp1=w_mlp1, w_mlp2=w_mlp2, norm_final=norm_final,
                 m_norm_mix=m_norm_mix, m_w_in=m_w_in, m_b_gate=m_b_gate, m_sc_conv_w=m_sc_conv_w, m_ssm_conv_w=m_ssm_conv_w, m_ssm_conv_b=m_ssm_conv_b, m_dt_bias=m_dt_bias, m_A_log=m_A_log, m_D_skip=m_D_skip, m_ssm_norm_w=m_ssm_norm_w, m_w_branch_sc=m_w_branch_sc, m_w_branch_ssm=m_w_branch_ssm, m_w_out=m_w_out, m_norm_mlp=m_norm_mlp, m_w_mlp1=m_w_mlp1, m_w_mlp2=m_w_mlp2, m_norm_final=m_norm_final,
                 v_norm_mix=v_norm_mix, v_w_in=v_w_in, v_b_gate=v_b_gate, v_sc_conv_w=v_sc_conv_w, v_ssm_conv_w=v_ssm_conv_w, v_ssm_conv_b=v_ssm_conv_b, v_dt_bias=v_dt_bias, v_A_log=v_A_log, v_D_skip=v_D_skip, v_ssm_norm_w=v_ssm_norm_w, v_w_branch_sc=v_w_branch_sc, v_w_branch_ssm=v_w_branch_ssm, v_w_out=v_w_out, v_norm_mlp=v_norm_mlp, v_w_mlp1=v_w_mlp1, v_w_mlp2=v_w_mlp2, v_norm_final=v_norm_final)
    order = ["norm_mix", "w_in", "b_gate", "sc_conv_w", "ssm_conv_w", "ssm_conv_b", "dt_bias", "A_log", "D_skip",
             "ssm_norm_w", "w_branch_sc", "w_branch_ssm", "w_out", "norm_mlp", "w_mlp1", "w_mlp2", "norm_final"]
    grad, delta, new_m, new_v = {}, {}, {}, {}
    for n in big:
        delta[n], new_m[n], new_v[n], grad[n] = _adamw(given[n], big[n], given["m_" + n], given["v_" + n],
                                                       "adamw_" + n, copy_g=True)
    big["w_in"] = None
    grad_small = {n: gs[n].reshape(given[n].shape) for n in small_names if n not in ("sc_conv_w", "ssm_conv_w")}
    grad_small["sc_conv_w"] = lax.dynamic_slice(gs["sc_conv_w"].reshape(3, D), (0, 256 * s), (3, 256))
    grad_small["ssm_conv_w"] = lax.dynamic_slice(gs["ssm_conv_w"].reshape(4, XBC), (0, 1024 * s), (4, 1024))
    table = small_table
    ds_, ms_, vs_ = _adamw(pk_w, _pack(grad_small, table, small_rows), pk_m, pk_v, "adamw_small", tr=small_rows)
    ds_, ms_, vs_ = _unpack(ds_, table), _unpack(ms_, table), _unpack(vs_, table)
    for n in grad_small:
        shp = given[n].shape
        grad[n] = grad_small[n]
        delta[n], new_m[n], new_v[n] = ds_[n].reshape(shp), ms_[n].reshape(shp), vs_[n].reshape(shp)

    done = [new_v[n] for n in ("w_mlp1", "w_mlp2", "w_branch_sc", "w_branch_ssm", "w_out")] + [vs_["b_gate"]]
    tok = rs_b.share(after=done)
    offs = jnp.where(s == 3, jnp.array([24, -8, 744, 2072, -8], jnp.int32),
                     jnp.stack([8 * s, 8 * s, 0 * s, 8 * s, 8 * s]).astype(jnp.int32))
    offs = jnp.concatenate([offs, jnp.stack([7 * ci, 4 - 4 * ci]).astype(jnp.int32)])
    nmain = W_SHARD // 256
    wt_own = (w_in.T, rs_b.arr["f0"], m_w_in.T, v_w_in.T, offs)
    res = _adamw_w_in(*wt_own, "adamw_w_in_own", 0, 256, 4, (0, 1), blk_key=5)
    gp = rs_b.result(after=[tok, res[0]])[0]
    wt_args = (w_in.T, gp, m_w_in.T, v_w_in.T, offs)
    res = _adamw_w_in(*wt_args, "adamw_w_in", 0, 256, nmain - 4, (0, 1), into=res, blk_key=6)
    res = _adamw_w_in(*wt_args, "adamw_w_in_dt", 744, 32, 1, (3,), into=res)
    dt_, mt_, vt_, gwt = _adamw_w_in(*wt_args, "adamw_w_in_tail", 256 * nmain, 8, 1, (4,), into=res)
    grad["w_in"], delta["w_in"], new_m["w_in"], new_v["w_in"] = gwt.T, dt_.T, mt_.T, vt_.T
    nm_arr, _, _ = _split_call("norm_mix_wait", nm_arr, wait=_all8_plan("nm"), wait_sems=nm_sems, after=tok)
    g8 = _sum8(nm_arr["nm"], "norm_mix_sum")
    r8 = lambda a: a.reshape(8, 128)
    d8, m8, v8 = _adamw(r8(norm_mix), g8, r8(m_norm_mix), r8(v_norm_mix), "adamw_norm_mix", tr=8)
    grad["norm_mix"], delta["norm_mix"] = g8.reshape(D), d8.reshape(D)
    new_m["norm_mix"], new_v["norm_mix"] = m8.reshape(D), v8.reshape(D)

    loss = gs["loss"].reshape(())
    return (loss, grad_x.reshape(1, L, D), *[grad[n] for n in order], *[delta[n] for n in order],
            *[new_m[n] for n in order], *[new_v[n] for n in order])
```

```python
import functools

import jax
import jax.numpy as jnp
from jax import lax
from jax.experimental import pallas as pl
from jax.experimental.pallas import tpu as pltpu

F32 = jnp.float32
BF16 = jnp.bfloat16
MESH = pl.DeviceIdType.MESH
HBM = pltpu.HBM

D = 1024
INNER = 2048
HD = 64
NH = 32
NG = 8
NS = 128
Q = 128
GPS = 8
XBC = 4096
DFF = 4096
EPS = 1e-6
W_SHARD = 2824
NCW = 11520
PIECE = 3072
PMAIN = 2816
C_Z, C_XBC, C_GATE, C_DT = 3072, 5120, 9216, 11264
SMALL_ROWS = 256
VMEM_LIMIT = 56 * 1024 * 1024

ADAM_LR, ADAM_B1, ADAM_B2, ADAM_EPS, ADAM_WD, ADAM_STEP = 0.001, 0.9, 0.999, 1e-08, 0.01, 10


def _cp(sem=None, vmem=VMEM_LIMIT):
    return pltpu.CompilerParams(dimension_semantics=sem, vmem_limit_bytes=vmem)


def _sigmoid(v):
    return 1.0 / (1.0 + jnp.exp(-v))


_DIMS = {"nn": (((1,), (0,)), ((), ())), "nt": (((1,), (1,)), ((), ())), "tn": (((0,), (0,)), ((), ()))}


def _matmul(a, b, mode, out_dtype, tm, tn, tk, name, epi=None, extra=None, n_outer=False, dep=None):
    if mode == "tn":
        K, M = a.shape
    else:
        M, K = a.shape
    N = b.shape[0] if mode == "nt" else b.shape[1]
    tm, tn, tk = min(tm, M), min(tn, N), min(tk, K)
    assert M % tm == 0 and N % tn == 0 and K % tk == 0, (name, M, N, K, tm, tn, tk)
    nm, nn, nk = M // tm, N // tn, K // tk
    dims = _DIMS[mode]

    def ij(p0, p1):
        return (p1, p0) if n_outer else (p0, p1)

    if mode == "tn":
        a_spec = pl.BlockSpec((tk, tm), lambda p0, p1, k: (k, ij(p0, p1)[0]))
    else:
        a_spec = pl.BlockSpec((tm, tk), lambda p0, p1, k: (ij(p0, p1)[0], k))
    if mode == "nt":
        b_spec = pl.BlockSpec((tn, tk), lambda p0, p1, k: (ij(p0, p1)[1], k))
    else:
        b_spec = pl.BlockSpec((tk, tn), lambda p0, p1, k: (k, ij(p0, p1)[1]))
    o_spec = pl.BlockSpec((tm, tn), lambda p0, p1, k: ij(p0, p1))
    in_specs = [a_spec, b_spec]
    args = [a, b]
    if epi in ("res", "drelu"):
        in_specs.append(o_spec)
        args.append(extra)
    if dep is not None:
        in_specs.append(pl.BlockSpec(memory_space=pl.ANY))
        args.append(dep)
    n_in = len(args)
    if epi == "relu2":
        out_shape = (jax.ShapeDtypeStruct((M, N), out_dtype), jax.ShapeDtypeStruct((M, N), BF16))
        out_specs = (o_spec, o_spec)
    else:
        out_shape = jax.ShapeDtypeStruct((M, N), out_dtype)
        out_specs = o_spec

    def kern(*refs):
        a_ref, b_ref = refs[0], refs[1]
        e_ref = refs[2] if epi in ("res", "drelu") else None
        acc = refs[-1]
        outs = refs[n_in:-1] if nk > 1 else refs[n_in:]
        k = pl.program_id(2)

        def product():
            return lax.dot_general(a_ref[...].astype(BF16), b_ref[...].astype(BF16), dims, preferred_element_type=F32)

        def finish(r):
            if epi is None:
                outs[0][...] = r.astype(out_dtype)
            elif epi == "res":
                outs[0][...] = (r + e_ref[...]).astype(out_dtype)
            elif epi == "relu2":
                outs[0][...] = r.astype(out_dtype)
                t = jnp.maximum(r, 0.0)
                outs[1][...] = (t * t).astype(BF16)
            else:
                outs[0][...] = (r * (2.0 * jnp.maximum(e_ref[...].astype(F32), 0.0))).astype(out_dtype)

        if nk == 1:
            finish(product())
        else:
            @pl.when(k == 0)
            def _():
                acc[...] = jnp.zeros_like(acc)

            acc[...] += product()

            @pl.when(k == nk - 1)
            def _():
                finish(acc[...])

    grid = (nn, nm, nk) if n_outer else (nm, nn, nk)
    return pl.pallas_call(
        kern, grid=grid, in_specs=in_specs, out_specs=out_specs, out_shape=out_shape,
        scratch_shapes=[pltpu.VMEM((tm, tn), F32)] if nk > 1 else [], name=name,
        compiler_params=_cp(("parallel", "parallel", "arbitrary")),
    )(*args)


def _matmul_res_rms(a, b, x, w, tm, name):
    M, K = a.shape
    tm = min(tm, M)

    def kern(a_ref, b_ref, x_ref, w_ref, x1_ref, h_ref):
        x1 = x_ref[...] + lax.dot_general(a_ref[...].astype(BF16), b_ref[...].astype(BF16), _DIMS["nn"],
                                          preferred_element_type=F32)
        x1_ref[...] = x1
        r = lax.rsqrt(jnp.mean(x1 * x1, axis=-1, keepdims=True) + EPS)
        h_ref[...] = ((x1 * r) * w_ref[...]).astype(BF16)

    row = pl.BlockSpec((tm, D), lambda i: (i, 0))
    return pl.pallas_call(
        kern, grid=(M // tm,),
        in_specs=[pl.BlockSpec((tm, K), lambda i: (i, 0)), pl.BlockSpec((K, D), lambda i: (0, 0)), row,
                  pl.BlockSpec((1, D), lambda i: (0, 0))],
        out_specs=(row, row), out_shape=(jax.ShapeDtypeStruct((M, D), F32), jax.ShapeDtypeStruct((M, D), BF16)),
        name=name, compiler_params=_cp(("parallel",)),
    )(a, b, x, w.reshape(1, D))


def _matmul_res_final(a, b, x, w, tgt, tm, name):
    M, K = a.shape
    tm = min(tm, M)

    def kern(a_ref, b_ref, x_ref, w_ref, t_ref, dx_ref, gw_ref, loss_ref):
        @pl.when(pl.program_id(0) == 0)
        def _():
            gw_ref[...] = jnp.zeros_like(gw_ref)
            loss_ref[...] = jnp.zeros_like(loss_ref)

        xv = x_ref[...] + lax.dot_general(a_ref[...].astype(BF16), b_ref[...].astype(BF16), _DIMS["nn"],
                                          preferred_element_type=F32)
        r = lax.rsqrt(jnp.mean(xv * xv, axis=-1, keepdims=True) + EPS)
        xn = xv * r
        e = xn * w_ref[...] - t_ref[...]
        loss_ref[...] += 0.5 * jnp.sum(jnp.mean(e * e, axis=-1, keepdims=True))
        dyv = e * (1.0 / D)
        gw_ref[...] += jnp.broadcast_to(jnp.sum(dyv * xn, axis=0, keepdims=True), (8, D))
        dxn = dyv * w_ref[...]
        dx_ref[...] = r * (dxn - xn * jnp.mean(dxn * xn, axis=-1, keepdims=True))

    row = pl.BlockSpec((tm, D), lambda i: (i, 0))
    return pl.pallas_call(
        kern, grid=(M // tm,),
        in_specs=[pl.BlockSpec((tm, K), lambda i: (i, 0)), pl.BlockSpec((K, D), lambda i: (0, 0)), row,
                  pl.BlockSpec((1, D), lambda i: (0, 0)), row],
        out_specs=(row, pl.BlockSpec((8, D), lambda i: (0, 0)), pl.BlockSpec((8, 128), lambda i: (0, 0))),
        out_shape=(jax.ShapeDtypeStruct((M, D), F32), jax.ShapeDtypeStruct((8, D), F32),
                   jax.ShapeDtypeStruct((8, 128), F32)),
        name=name, compiler_params=_cp(("arbitrary",)),
    )(a, b, x, w.reshape(1, D), tgt)


def _matmul_rms_bwd(a, b, mode, x, w, res, tm, tk, name, dep=None):
    M, K = a.shape
    tm, tk = min(tm, M), min(tk, K)
    nk = K // tk
    assert M % tm == 0 and K % tk == 0
    b_spec = (pl.BlockSpec((tk, D), lambda k, i: (k, 0)) if mode == "nn" else pl.BlockSpec((D, tk), lambda k, i: (0, k)))
    row = pl.BlockSpec((tm, D), lambda k, i: (jnp.where(k == nk - 1, i, 0), 0))
    deps = [] if dep is None else [dep]

    def kern(a_ref, b_ref, x_ref, w_ref, res_ref, *rest):
        dx_ref, gw_ref, acc = rest[-3:]
        k, i = pl.program_id(0), pl.program_id(1)

        @pl.when((i == 0) & (k == 0))
        def _():
            gw_ref[...] = jnp.zeros_like(gw_ref)

        def product():
            return lax.dot_general(a_ref[...].astype(BF16), b_ref[...].astype(BF16), _DIMS[mode],
                                   preferred_element_type=F32)

        def finish(dyv):
            xv = x_ref[...]
            r = lax.rsqrt(jnp.mean(xv * xv, axis=-1, keepdims=True) + EPS)
            xn = xv * r
            gw_ref[...] += jnp.broadcast_to(jnp.sum(dyv * xn, axis=0, keepdims=True), (8, D))
            dxn = dyv * w_ref[...]
            dx_ref[...] = res_ref[...] + r * (dxn - xn * jnp.mean(dxn * xn, axis=-1, keepdims=True))

        if nk == 1:
            finish(product())
        else:
            rows = pl.ds(pl.multiple_of(i * tm, tm), tm)

            @pl.when(k == 0)
            def _():
                acc[rows, :] = jnp.zeros((tm, D), F32)

            acc[rows, :] += product()

            @pl.when(k == nk - 1)
            def _():
                finish(acc[rows, :])

    return pl.pallas_call(
        kern, grid=(nk, M // tm),
        in_specs=[pl.BlockSpec((tm, tk), lambda k, i: (i, k)), b_spec, row, pl.BlockSpec((1, D), lambda k, i: (0, 0)),
                  row] + [pl.BlockSpec(memory_space=pl.ANY)] * len(deps),
        out_specs=(row, pl.BlockSpec((8, D), lambda k, i: (0, 0))),
        out_shape=(jax.ShapeDtypeStruct((M, D), F32), jax.ShapeDtypeStruct((8, D), F32)),
        scratch_shapes=[pltpu.VMEM((M, D) if nk > 1 else (8, 128), F32)], name=name,
        compiler_params=_cp(("arbitrary", "arbitrary")),
    )(a, b, x, w.reshape(1, D), res, *deps)


def _rms_fwd(x, w, name, tl=256, dep=None):
    L = x.shape[0]

    def kern(x_ref, w_ref, *rest):
        o_ref = rest[-1]
        xv = x_ref[...]
        r = lax.rsqrt(jnp.mean(xv * xv, axis=-1, keepdims=True) + EPS)
        o_ref[...] = ((xv * r) * w_ref[...]).astype(BF16)

    row = pl.BlockSpec((tl, D), lambda i: (i, 0))
    deps = [] if dep is None else [dep]
    return pl.pallas_call(
        kern, grid=(L // tl,),
        in_specs=[row, pl.BlockSpec((1, D), lambda i: (0, 0))] + [pl.BlockSpec(memory_space=pl.ANY)] * len(deps),
        out_specs=row, out_shape=jax.ShapeDtypeStruct((L, D), BF16), name=name, compiler_params=_cp(("parallel",)),
    )(x, w.reshape(1, D), *deps)


def _down(v, k):
    if k == 0:
        return v
    t = lax.broadcasted_iota(jnp.int32, v.shape, 0)
    return jnp.where(t >= k, pltpu.roll(v, k, axis=0), 0.0)


def _up(v, k):
    if k == 0:
        return v
    n = v.shape[0]
    t = lax.broadcasted_iota(jnp.int32, v.shape, 0)
    return jnp.where(t < n - k, pltpu.roll(v, n - k, axis=0), 0.0)


TW = 256


def _sc_fwd(proj, cw):
    L = proj.shape[0]
    nb = D // TW

    def kern(b_ref, c_ref, x_ref, w_ref, o_ref):
        u = c_ref[...].astype(F32) * x_ref[...].astype(F32)
        w = w_ref[...]
        cv = w[0:1] * _down(u, 2) + w[1:2] * _down(u, 1) + w[2:3] * u
        o_ref[...] = (b_ref[...].astype(F32) * cv).astype(BF16)

    col = lambda off: pl.BlockSpec((L, TW), lambda j: (0, off + j))
    return pl.pallas_call(
        kern, grid=(nb,), in_specs=[col(0), col(nb), col(2 * nb), pl.BlockSpec((8, TW), lambda j: (0, j))],
        out_specs=pl.BlockSpec((L, TW), lambda j: (0, j)), out_shape=jax.ShapeDtypeStruct((L, D), BF16),
        name="sc_fwd", compiler_params=_cp(("parallel",)),
    )(proj, proj, proj, cw)


def _sc_bwd(dya, proj, cw, dproj):
    L = proj.shape[0]
    nb = D // TW

    def kern(d_ref, b_ref, c_ref, x_ref, w_ref, _, dp_ref, gw_ref, keep):
        sec = pl.program_id(1)

        @pl.when(sec == 0)
        def _():
            cs, xs, dyv = c_ref[...].astype(F32), x_ref[...].astype(F32), d_ref[...]
            w = w_ref[...]
            u = cs * xs
            u1, u2 = _down(u, 1), _down(u, 2)
            cv = w[0:1] * u2 + w[1:2] * u1 + w[2:3] * u
            dcv = dyv * b_ref[...].astype(F32)
            du = w[2:3] * dcv + w[1:2] * _up(dcv, 1) + w[0:1] * _up(dcv, 2)
            g0 = jnp.sum(dcv * u2, axis=0, keepdims=True)
            g1 = jnp.sum(dcv * u1, axis=0, keepdims=True)
            g2 = jnp.sum(dcv * u, axis=0, keepdims=True)
            row = lax.broadcasted_iota(jnp.int32, (8, TW), 0)
            gw_ref[...] = jnp.where(row == 0, g0, jnp.where(row == 1, g1, jnp.where(row == 2, g2, 0.0)))
            dp_ref[...] = (dyv * cv).astype(BF16)
            keep[0] = (du * xs).astype(BF16)
            keep[1] = (du * cs).astype(BF16)

        @pl.when(sec > 0)
        def _():
            dp_ref[...] = keep[sec - 1]

    col = lambda off: pl.BlockSpec((L, TW), lambda j, s: (0, off + j))
    return pl.pallas_call(
        kern, grid=(nb, 3),
        in_specs=[col(0), col(0), col(nb), col(2 * nb), pl.BlockSpec((8, TW), lambda j, s: (0, j)),
                  pl.BlockSpec(memory_space=pl.ANY)],
        out_specs=(pl.BlockSpec((L, TW), lambda j, s: (0, s * nb + j)), pl.BlockSpec((8, TW), lambda j, s: (0, j))),
        out_shape=(jax.ShapeDtypeStruct(dproj.shape, BF16), jax.ShapeDtypeStruct((8, D), F32)),
        scratch_shapes=[pltpu.VMEM((2, L, TW), BF16)],
        input_output_aliases={5: 0}, name="sc_bwd", compiler_params=_cp(("parallel", "arbitrary")),
    )(dya, proj, proj, proj, cw, dproj)


def _ssm_conv_fwd(proj, cw4):
    L = proj.shape[0]
    off = C_XBC // TW

    def kern(r_ref, w_ref, o_ref):
        raw = r_ref[...].astype(F32)
        w = w_ref[...]
        c4 = w[0:1] * _down(raw, 3) + w[1:2] * _down(raw, 2) + w[2:3] * _down(raw, 1) + w[3:4] * raw + w[4:5]
        o_ref[...] = c4 * _sigmoid(c4)

    return pl.pallas_call(
        kern, grid=(XBC // TW,),
        in_specs=[pl.BlockSpec((L, TW), lambda j: (0, off + j)), pl.BlockSpec((8, TW), lambda j: (0, j))],
        out_specs=pl.BlockSpec((L, TW), lambda j: (0, j)), out_shape=jax.ShapeDtypeStruct((L, XBC), F32),
        name="ssm_conv_fwd", compiler_params=_cp(("parallel",)),
    )(proj, cw4)


def _ssm_conv_bwd(dx, proj, cw4, dproj, col0, name):
    L, width = dx.shape
    off_p = (C_XBC + col0) // TW
    off_w = col0 // TW

    def kern(d_ref, r_ref, w_ref, _, dp_ref, gw_ref):
        raw = r_ref[...].astype(F32)
        w = w_ref[...]
        r1, r2, r3 = _down(raw, 1), _down(raw, 2), _down(raw, 3)
        c4 = w[0:1] * r3 + w[1:2] * r2 + w[2:3] * r1 + w[3:4] * raw + w[4:5]
        sg = _sigmoid(c4)
        dc4 = d_ref[...] * (sg * (1.0 + c4 * (1.0 - sg)))
        draw = w[3:4] * dc4 + w[2:3] * _up(dc4, 1) + w[1:2] * _up(dc4, 2) + w[0:1] * _up(dc4, 3)
        dp_ref[...] = draw.astype(BF16)
        gs = [jnp.sum(dc4 * r3, axis=0, keepdims=True), jnp.sum(dc4 * r2, axis=0, keepdims=True),
              jnp.sum(dc4 * r1, axis=0, keepdims=True), jnp.sum(dc4 * raw, axis=0, keepdims=True),
              jnp.sum(dc4, axis=0, keepdims=True)]
        row = lax.broadcasted_iota(jnp.int32, (8, TW), 0)
        acc = jnp.zeros((8, TW), F32)
        for k, gk in enumerate(gs):
            acc = jnp.where(row == k, gk, acc)
        gw_ref[...] = acc

    return pl.pallas_call(
        kern, grid=(width // TW,),
        in_specs=[pl.BlockSpec((L, TW), lambda j: (0, j)), pl.BlockSpec((L, TW), lambda j: (0, off_p + j)),
                  pl.BlockSpec((8, TW), lambda j: (0, off_w + j)), pl.BlockSpec(memory_space=pl.ANY)],
        out_specs=(pl.BlockSpec((L, TW), lambda j: (0, off_p + j)), pl.BlockSpec((8, TW), lambda j: (0, j))),
        out_shape=(jax.ShapeDtypeStruct(dproj.shape, BF16), jax.ShapeDtypeStruct((8, width), F32)),
        input_output_aliases={3: 0}, name=name, compiler_params=_cp(("arbitrary",)),
    )(dx, proj, cw4, dproj)


def _split3(v):
    h1 = v.astype(BF16)
    r1 = v - h1.astype(F32)
    h2 = r1.astype(BF16)
    h3 = (r1 - h2.astype(F32)).astype(BF16)
    return h1, h2, h3


def _dot01(m01, v, dims=_DIMS["nn"], m_left=True, terms=3):
    out = None
    for part in _split3(v)[:terms]:
        ops = (m01, part) if m_left else (part, m01)
        t = lax.dot_general(ops[0], ops[1], dims, preferred_element_type=F32)
        out = t if out is None else out + t
    return out


def _bdot(a, b, mode="nn"):
    return lax.dot_general(a.astype(BF16), b.astype(BF16), _DIMS[mode], preferred_element_type=F32)


def _softplus(v):
    return jnp.maximum(v, 0.0) + jnp.log1p(jnp.exp(-jnp.abs(v)))


def _dt_prep(proj, vec):
    L = proj.shape[0]

    def kern(p_ref, v_ref, dt_ref, cs_ref, sg_ref):
        v = v_ref[...]
        pre = p_ref[:, 0:128] + v[0:1]
        dt = _softplus(pre)
        da = dt * (-jnp.exp(v[1:2]))
        ii = lax.broadcasted_iota(jnp.int32, (Q, Q), 0)
        jj = lax.broadcasted_iota(jnp.int32, (Q, Q), 1)
        ltri = (jj <= ii).astype(BF16)
        lane = lax.broadcasted_iota(jnp.int32, (Q, 128), 1)
        for val, ref in ((dt, dt_ref), (_dot01(ltri, da), cs_ref), (_sigmoid(pre), sg_ref)):
            for g in range(NG):
                moved = val if g == 0 else pltpu.roll(val, 128 - 4 * g, axis=1)
                ref[g] = jnp.where(lane < 4, moved, 0.0)

    blk = pl.BlockSpec((NG, Q, 128), lambda c: (0, c, 0))
    return pl.pallas_call(
        kern, grid=(L // Q,),
        in_specs=[pl.BlockSpec((Q, 256), lambda c: (c, 0)), pl.BlockSpec((8, 128), lambda c: (0, 0))],
        out_specs=(blk, blk, blk),
        out_shape=(jax.ShapeDtypeStruct((NG, L, 128), F32),) * 3,
        name="dt_prep", compiler_params=_cp(("parallel",)),
    )(proj, vec)


def _head_masks():
    lane = lax.broadcasted_iota(jnp.int32, (1, 4 * HD), 1)
    return [((lane >= HD * j) & (lane < HD * (j + 1))) for j in range(4)]


def _expand4(v4, masks):
    R = v4.shape[0]
    out = jnp.zeros((R, 4 * HD), F32)
    for j in range(4):
        out = jnp.where(masks[j], jnp.broadcast_to(v4[:, j:j + 1], (R, 4 * HD)), out)
    return out


def _decay_matrix(cs_col, tri):
    colb = jnp.broadcast_to(cs_col, (Q, Q))
    return jnp.exp(jnp.where(tri, colb - colb.T, -jnp.inf))


def _ssd_fwd(xbc, dt4, cs4, vecg):
    L = xbc.shape[0]
    nc = L // Q

    def kern(x_ref, b_ref, c_ref, dt_ref, cs_ref, v_ref, y_ref, s_ref, S):
        c = pl.program_id(1)

        @pl.when(c == 0)
        def _():
            S[...] = jnp.zeros_like(S)

        masks = _head_masks()
        ii = lax.broadcasted_iota(jnp.int32, (Q, Q), 0)
        jj = lax.broadcasted_iota(jnp.int32, (Q, Q), 1)
        tri = jj <= ii
        for gi in range(GPS):
            xs, ns = slice(256 * gi, 256 * (gi + 1)), slice(NS * gi, NS * (gi + 1))
            dt4v, cs4v = dt_ref[gi], cs_ref[gi]
            dt_b, cs_b = _expand4(dt4v, masks), _expand4(cs4v, masks)
            d_b = _expand4(v_ref[gi], masks)[1:2]
            cs_last = cs_b[Q - 1:Q, :]
            x4, bm, cm = x_ref[:, xs], b_ref[:, ns], c_ref[:, ns]
            xdt = x4 * dt_b
            gm = _bdot(cm, bm, "nt")
            s4 = S[gi]
            s_ref[gi, 0] = s4
            y = _bdot(cm, s4) * jnp.exp(cs_b) + d_b * x4
            m_all = jnp.concatenate([(gm * _decay_matrix(cs4v[:, j:j + 1], tri)).astype(BF16) for j in range(4)], axis=0)
            yd = _bdot(m_all, xdt)
            for j in range(4):
                y = y + jnp.where(masks[j], yd[Q * j:Q * (j + 1)], 0.0)
            y_ref[:, xs] = y
            S[gi] = jnp.exp(cs_last) * s4 + _bdot(bm, xdt * jnp.exp(cs_last - cs_b), "tn")

    sc = pl.BlockSpec((GPS, Q, 128), lambda g, c: (g, c, 0))
    bw = NS * GPS
    return pl.pallas_call(
        kern, grid=(NG // GPS, nc),
        in_specs=[pl.BlockSpec((Q, 256 * GPS), lambda g, c: (c, g)),
                  pl.BlockSpec((Q, bw), lambda g, c: (c, INNER // bw + g)),
                  pl.BlockSpec((Q, bw), lambda g, c: (c, (INNER + NG * NS) // bw + g)),
                  sc, sc, pl.BlockSpec((GPS, 8, 128), lambda g, c: (g, 0, 0))],
        out_specs=(pl.BlockSpec((Q, 256 * GPS), lambda g, c: (c, g)),
                   pl.BlockSpec((GPS, 1, NS, 256), lambda g, c: (g, c, 0, 0))),
        out_shape=(jax.ShapeDtypeStruct((L, INNER), F32), jax.ShapeDtypeStruct((NG, nc, NS, 256), F32)),
        scratch_shapes=[pltpu.VMEM((GPS, NS, 256), F32)], name="ssd_fwd",
        compiler_params=_cp(("parallel", "arbitrary")),
    )(xbc, xbc, xbc, dt4, cs4, vecg)


def _ssd_bwd(xbc, dt4, cs4, sg4, vecg, s_all, dy):
    L = xbc.shape[0]
    nc = L // Q

    def kern(x_ref, b_ref, c_ref, dt_ref, cs_ref, sg_ref, v_ref, s_ref, dy_ref,
             dx_ref, db_ref, dc_ref, ddt_ref, st_ref, dS):
        cc = pl.program_id(1)

        @pl.when(cc == 0)
        def _():
            dS[...] = jnp.zeros_like(dS)
            st_ref[...] = jnp.zeros_like(st_ref)

        masks = _head_masks()
        ii = lax.broadcasted_iota(jnp.int32, (Q, Q), 0)
        jj = lax.broadcasted_iota(jnp.int32, (Q, Q), 1)
        tri = jj <= ii
        utri = (jj >= ii).astype(BF16)
        hsel = ((lax.broadcasted_iota(jnp.int32, (4 * HD, 128), 0) // HD)
                == lax.broadcasted_iota(jnp.int32, (4 * HD, 128), 1)).astype(BF16)
        hrow = ((lax.broadcasted_iota(jnp.int32, (4 * Q, 128), 0) // Q)
                == lax.broadcasted_iota(jnp.int32, (4 * Q, 128), 1)).astype(BF16)
        ones_q = jnp.ones((Q, 128), BF16)
        lane128 = lax.broadcasted_iota(jnp.int32, (Q, 128), 1)

        for gi in range(GPS):
            xs, ns = slice(256 * gi, 256 * (gi + 1)), slice(NS * gi, NS * (gi + 1))
            dt4v, cs4v, sg4v = dt_ref[gi], cs_ref[gi], sg_ref[gi]
            dt_b, cs_b = _expand4(dt4v, masks), _expand4(cs4v, masks)
            vv = _expand4(v_ref[gi], masks)
            a_b = -jnp.exp(vv[0:1])
            d_b = vv[1:2]
            a4 = -jnp.exp(v_ref[gi][0:1, :])
            cs_last = cs_b[Q - 1:Q, :]
            ecs = jnp.exp(cs_b)
            decay = jnp.exp(cs_last - cs_b)
            elast = jnp.exp(cs_last)
            x4, bm, cm, dyv = x_ref[:, xs], b_ref[:, ns], c_ref[:, ns], dy_ref[:, xs]
            s4 = s_ref[gi, 0]
            dsn = dS[gi]
            xdt = x4 * dt_b
            gm = _bdot(cm, bm, "nt")
            dye = dyv * ecs
            yoff = ecs * _bdot(cm, s4)
            t4 = _bdot(bm, dsn) * decay
            lms, mhs = [], []
            for j in range(4):
                colb = jnp.broadcast_to(cs4v[:, j:j + 1], (Q, Q))
                lms.append(jnp.exp(jnp.where(tri, colb - colb.T, -jnp.inf)))
                mhs.append(gm * lms[j])
            m_all = jnp.concatenate([m.astype(BF16) for m in mhs], axis=0)
            dy_m = jnp.concatenate([jnp.where(masks[j], dyv, 0.0).astype(BF16) for j in range(4)], axis=0)
            dxdt = t4 + _bdot(m_all, dy_m, "tn")
            dm_all = _bdot(dy_m, xdt, "nt")
            dg = jnp.zeros((Q, Q), F32)
            for j in range(4):
                dg = dg + dm_all[Q * j:Q * (j + 1)] * lms[j]
            e_all = dm_all * jnp.concatenate(mhs, axis=0)
            rsum = _dot01(ones_q, e_all, m_left=False, terms=2)
            da4 = -_dot01(hrow, e_all, _DIMS["tn"], m_left=False, terms=2)
            for j in range(4):
                da4 = da4 + jnp.where(lane128 == j, rsum[Q * j:Q * (j + 1)], 0.0)
            xt = xdt * t4
            tail = jnp.sum(xt, axis=0, keepdims=True) + elast * jnp.sum(s4 * dsn, axis=0, keepdims=True)
            gd_raw = jnp.sum(dyv * x4, axis=0, keepdims=True)
            stacked = jnp.concatenate([dyv * yoff - xt, dxdt * x4, jnp.broadcast_to(tail, (8, 4 * HD)),
                                       jnp.broadcast_to(gd_raw, (8, 4 * HD))], axis=0)
            seg = _dot01(hsel, stacked, m_left=False, terms=2)
            dda4 = _dot01(utri, da4 + seg[0:Q], terms=2) + seg[2 * Q:2 * Q + 1]
            ddt_ref[gi] = (dda4 * a4 + seg[Q:2 * Q]) * sg4v
            ga = jnp.sum(dda4 * dt4v * a4, axis=0, keepdims=True)
            row = lax.broadcasted_iota(jnp.int32, (8, 128), 0)
            st_ref[gi] += jnp.where(row == 0, ga, jnp.where(row == 1, seg[2 * Q + 8:2 * Q + 9], 0.0))
            dx_ref[:, xs] = d_b * dyv + dxdt * dt_b
            dc_ref[:, ns] = _bdot(dg, bm) + _bdot(dye, s4, "nt")
            db_ref[:, ns] = _bdot(dg, cm, "tn") + _bdot(xdt * decay, dsn, "nt")
            dS[gi] = elast * dsn + _bdot(cm, dye, "tn")

    rv = lambda c: nc - 1 - c
    sc = pl.BlockSpec((GPS, Q, 128), lambda g, c: (g, rv(c), 0))
    bw = NS * GPS
    return pl.pallas_call(
        kern, grid=(NG // GPS, nc),
        in_specs=[pl.BlockSpec((Q, 256 * GPS), lambda g, c: (rv(c), g)),
                  pl.BlockSpec((Q, bw), lambda g, c: (rv(c), INNER // bw + g)),
                  pl.BlockSpec((Q, bw), lambda g, c: (rv(c), (INNER + NG * NS) // bw + g)),
                  sc, sc, sc, pl.BlockSpec((GPS, 8, 128), lambda g, c: (g, 0, 0)),
                  pl.BlockSpec((GPS, 1, NS, 256), lambda g, c: (g, rv(c), 0, 0)),
                  pl.BlockSpec((Q, 256 * GPS), lambda g, c: (rv(c), g))],
        out_specs=(pl.BlockSpec((Q, 256 * GPS), lambda g, c: (rv(c), g)),
                   pl.BlockSpec((Q, bw), lambda g, c: (rv(c), g)),
                   pl.BlockSpec((Q, bw), lambda g, c: (rv(c), g)),
                   pl.BlockSpec((GPS, Q, 128), lambda g, c: (g, rv(c), 0)),
                   pl.BlockSpec((GPS, 8, 128), lambda g, c: (g, 0, 0))),
        out_shape=(jax.ShapeDtypeStruct((L, INNER), F32), jax.ShapeDtypeStruct((L, NG * NS), F32),
                   jax.ShapeDtypeStruct((L, NG * NS), F32), jax.ShapeDtypeStruct((NG, L, 128), F32),
                   jax.ShapeDtypeStruct((NG, 8, 128), F32)),
        scratch_shapes=[pltpu.VMEM((GPS, NS, 256), F32)], name="ssd_bwd",
        compiler_params=_cp(("parallel", "arbitrary")),
    )(xbc, xbc, xbc, dt4, cs4, sg4, vecg, s_all, dy)


def _dt_bwd(ddt, dproj, tl=256):
    L = ddt.shape[1]

    def kern(d_ref, _, dp_ref, gs_ref):
        @pl.when(pl.program_id(0) == 0)
        def _():
            gs_ref[...] = jnp.zeros_like(gs_ref)

        d = d_ref[0]
        for g in range(1, NG):
            d = d + pltpu.roll(d_ref[g], 4 * g, axis=1)
        gs_ref[...] += jnp.broadcast_to(jnp.sum(d, axis=0, keepdims=True), (8, 128))
        dp_ref[...] = jnp.concatenate([d, jnp.zeros_like(d)], axis=1).astype(BF16)

    return pl.pallas_call(
        kern, grid=(L // tl,),
        in_specs=[pl.BlockSpec((NG, tl, 128), lambda i: (0, i, 0)), pl.BlockSpec(memory_space=pl.ANY)],
        out_specs=(pl.BlockSpec((tl, 256), lambda i: (i, C_DT // 256)), pl.BlockSpec((8, 128), lambda i: (0, 0))),
        out_shape=(jax.ShapeDtypeStruct(dproj.shape, BF16), jax.ShapeDtypeStruct((8, 128), F32)),
        input_output_aliases={1: 0}, name="dt_bwd", compiler_params=_cp(("arbitrary",)),
    )(ddt, dproj)


GW = INNER // NG


def _gnorm_fwd(y, proj, w, tl=256):
    L = y.shape[0]
    zoff = C_Z // 1024

    def kern(y_ref, z_ref, w_ref, o_ref):
        z = z_ref[...].astype(F32)
        yz = y_ref[...] * (z * _sigmoid(z))
        wv = w_ref[...]
        for k in range(1024 // GW):
            sl = slice(GW * k, GW * (k + 1))
            v = yz[:, sl]
            rg = lax.rsqrt(jnp.mean(v * v, axis=-1, keepdims=True) + EPS)
            o_ref[:, sl] = ((v * rg) * wv[:, sl]).astype(BF16)

    blk = pl.BlockSpec((tl, 1024), lambda i, j: (i, j))
    return pl.pallas_call(
        kern, grid=(L // tl, 2),
        in_specs=[blk, pl.BlockSpec((tl, 1024), lambda i, j: (i, zoff + j)), pl.BlockSpec((1, 1024), lambda i, j: (0, j))],
        out_specs=blk, out_shape=jax.ShapeDtypeStruct((L, INNER), BF16), name="gnorm_fwd",
        compiler_params=_cp(("parallel", "parallel")),
    )(y, proj, w.reshape(1, INNER))


def _gnorm_bwd(dbr, wb, y, proj, w, dproj, dep, tl=512):
    L = y.shape[0]
    tl = min(tl, L)
    zoff = C_Z // 1024

    def kern(d_ref, b_ref, y_ref, z_ref, w_ref, _, __, dy_ref, dp_ref, gw_ref):
        @pl.when(pl.program_id(1) == 0)
        def _():
            gw_ref[...] = jnp.zeros_like(gw_ref)

        z = z_ref[...].astype(F32)
        sg = _sigmoid(z)
        sz = z * sg
        yv = y_ref[...]
        yz = yv * sz
        dv = lax.dot_general(d_ref[0], b_ref[...], _DIMS["nt"], preferred_element_type=F32)
        wv = w_ref[...]
        for k in range(1024 // GW):
            sl = slice(GW * k, GW * (k + 1))
            v = yz[:, sl]
            rg = lax.rsqrt(jnp.mean(v * v, axis=-1, keepdims=True) + EPS)
            vn = v * rg
            dk = dv[:, sl]
            gw_ref[:, sl] += jnp.broadcast_to(jnp.sum(dk * vn, axis=0, keepdims=True), (8, GW))
            dvn = dk * wv[:, sl]
            dyz = rg * (dvn - vn * jnp.mean(dvn * vn, axis=-1, keepdims=True))
            dy_ref[:, sl] = dyz * sz[:, sl]
            dp_ref[:, sl] = (dyz * yv[:, sl] * (sg[:, sl] * (1.0 + z[:, sl] * (1.0 - sg[:, sl])))).astype(BF16)

    blk = pl.BlockSpec((tl, 1024), lambda j, i: (i, j))
    zblk = pl.BlockSpec((tl, 1024), lambda j, i: (i, zoff + j))
    return pl.pallas_call(
        kern, grid=(2, L // tl),
        in_specs=[pl.BlockSpec((1, tl, D), lambda j, i: (1, i, 0)), pl.BlockSpec((1024, D), lambda j, i: (j, 0)),
                  blk, zblk, pl.BlockSpec((1, 1024), lambda j, i: (0, j)), pl.BlockSpec(memory_space=pl.ANY),
                  pl.BlockSpec(memory_space=pl.ANY)],
        out_specs=(blk, zblk, pl.BlockSpec((8, 1024), lambda j, i: (0, j))),
        out_shape=(jax.ShapeDtypeStruct((L, INNER), F32), jax.ShapeDtypeStruct(dproj.shape, BF16),
                   jax.ShapeDtypeStruct((8, INNER), F32)),
        input_output_aliases={5: 1}, name="gnorm_bwd", compiler_params=_cp(("parallel", "arbitrary")),
    )(dbr, wb, y, proj, w.reshape(1, INNER), dproj, dep)


def _merge_fwd(proj, bg, br_a, br_b, tl=256):
    L = proj.shape[0]
    goff = C_GATE // 1024

    def kern(g1_ref, g2_ref, b1_ref, b2_ref, a_ref, b_ref, o_ref):
        g1 = _sigmoid(g1_ref[...].astype(F32) + b1_ref[...])
        g2 = _sigmoid(g2_ref[...].astype(F32) + b2_ref[...])
        o_ref[...] = (g1 * a_ref[...] + g2 * b_ref[...]).astype(BF16)

    row = pl.BlockSpec((tl, 1024), lambda i: (i, 0))
    bg2 = bg.reshape(1, 2 * D)
    return pl.pallas_call(
        kern, grid=(L // tl,),
        in_specs=[pl.BlockSpec((tl, 1024), lambda i: (i, goff)), pl.BlockSpec((tl, 1024), lambda i: (i, goff + 1)),
                  pl.BlockSpec((1, 1024), lambda i: (0, 0)), pl.BlockSpec((1, 1024), lambda i: (0, 1)), row, row],
        out_specs=row, out_shape=jax.ShapeDtypeStruct((L, D), BF16), name="merge_fwd",
        compiler_params=_cp(("parallel",)),
    )(proj, proj, bg2, bg2, br_a, br_b)


def _branch_ssm_merge(yb, wb, proj, bg, br_a, tm=512):
    L, K = yb.shape
    tm = min(tm, L)
    goff = C_GATE // 1024

    def kern(a_ref, b_ref, g1_ref, g2_ref, b1_ref, b2_ref, bra_ref, brb_ref, m_ref):
        brb = lax.dot_general(a_ref[...], b_ref[...], _DIMS["nn"], preferred_element_type=F32)
        brb_ref[...] = brb
        g1 = _sigmoid(g1_ref[...].astype(F32) + b1_ref[...])
        g2 = _sigmoid(g2_ref[...].astype(F32) + b2_ref[...])
        m_ref[...] = (g1 * bra_ref[...] + g2 * brb).astype(BF16)

    row = pl.BlockSpec((tm, D), lambda i: (i, 0))
    bg2 = bg.reshape(1, 2 * D)
    return pl.pallas_call(
        kern, grid=(L // tm,),
        in_specs=[pl.BlockSpec((tm, K), lambda i: (i, 0)), pl.BlockSpec((K, D), lambda i: (0, 0)),
                  pl.BlockSpec((tm, D), lambda i: (i, goff)), pl.BlockSpec((tm, D), lambda i: (i, goff + 1)),
                  pl.BlockSpec((1, D), lambda i: (0, 0)), pl.BlockSpec((1, D), lambda i: (0, 1)), row],
        out_specs=(row, row), out_shape=(jax.ShapeDtypeStruct((L, D), F32), jax.ShapeDtypeStruct((L, D), BF16)),
        name="branch_ssm_merge", compiler_params=_cp(("parallel",)),
    )(yb, wb, proj, proj, bg2, bg2, br_a)


def _merge_bwd(dx1, wo, proj, bg, br_a, br_b, dproj, tl=512):
    L = proj.shape[0]
    tl = min(tl, L)
    goff = C_GATE // 1024

    def kern(dm_ref, wo_ref, g_ref, b_ref, a_ref, bb_ref, _, dbr_ref, dp_ref, gb_ref):
        j = pl.program_id(0)

        @pl.when(pl.program_id(1) == 0)
        def _():
            gb_ref[...] = jnp.zeros_like(gb_ref)

        g = _sigmoid(g_ref[...].astype(F32) + b_ref[...])
        br = jnp.where(j == 0, a_ref[...], bb_ref[...])
        dmv = lax.dot_general(dm_ref[...].astype(BF16), wo_ref[...], _DIMS["nt"], preferred_element_type=F32)
        dbr_ref[0] = (dmv * g).astype(BF16)
        dgate = dmv * br * g * (1.0 - g)
        gb_ref[...] += jnp.broadcast_to(jnp.sum(dgate, axis=0, keepdims=True), (8, 1024))
        dp_ref[...] = dgate.astype(BF16)

    row = pl.BlockSpec((tl, 1024), lambda j, i: (i, 0))
    gblk = pl.BlockSpec((tl, 1024), lambda j, i: (i, goff + j))
    return pl.pallas_call(
        kern, grid=(2, L // tl),
        in_specs=[row, pl.BlockSpec((D, D), lambda j, i: (0, 0)), gblk, pl.BlockSpec((1, 1024), lambda j, i: (0, j)),
                  row, row, pl.BlockSpec(memory_space=pl.ANY)],
        out_specs=(pl.BlockSpec((1, tl, 1024), lambda j, i: (j, i, 0)), gblk, pl.BlockSpec((8, 1024), lambda j, i: (0, j))),
        out_shape=(jax.ShapeDtypeStruct((2, L, D), BF16), jax.ShapeDtypeStruct(dproj.shape, BF16),
                   jax.ShapeDtypeStruct((8, 2 * D), F32)),
        input_output_aliases={6: 1}, name="merge_bwd", compiler_params=_cp(("parallel", "arbitrary")),
    )(dx1, wo, proj, bg.reshape(1, 2 * D), br_a, br_b, dproj)


def _coords():
    return lax.axis_index("x"), lax.axis_index("y"), lax.axis_index("c")


def _other_chips(sk):
    xk, yk = sk // 2, sk % 2
    return [((1 - xk, yk), 2 * (1 - xk) + yk), ((xk, 1 - yk), 2 * xk + 1 - yk), ((1 - xk, 1 - yk), 2 * (1 - xk) + 1 - yk)]


def _rows(start, size):
    assert size % 128 == 0
    return pl.ds(pl.multiple_of(start, 128), size)


def _per_chip(fn):
    x, y, _ = _coords()
    s = 2 * x + y
    for sk in range(4):
        pl.when(s == sk)(functools.partial(fn, sk))


XTRA = PIECE - PMAIN


def _place(shard, full_shape, block, index_map, idx, name, blk0=0, nblk=None, dep=None, into=None):
    in_block = block[-2:]
    if nblk is None:
        nblk = shard.shape[0] // in_block[0]

    def kern(idx_ref, s_ref, *rest):
        o_ref = rest[-1]
        o_ref[...] = s_ref[...].astype(BF16).reshape(o_ref.shape)

    extra = ([dep] if dep is not None else []) + ([into] if into is not None else [])
    grid_spec = pltpu.PrefetchScalarGridSpec(
        num_scalar_prefetch=1, grid=(nblk,),
        in_specs=[pl.BlockSpec(in_block, lambda i, idx_ref: (blk0 + i, 0))] + [_ANY] * len(extra),
        out_specs=pl.BlockSpec(block, index_map))
    aliases = {1 + len(extra): 0} if into is not None else {}
    return pl.pallas_call(kern, grid_spec=grid_spec, out_shape=jax.ShapeDtypeStruct(full_shape, BF16), name=name,
                          input_output_aliases=aliases, compiler_params=_cp(("arbitrary",)))(idx, shard, *extra)


_SEM = pl.BlockSpec(memory_space=pltpu.SEMAPHORE)
_EFFECT = pltpu.SideEffectType.DATAFLOW_SIDE_EFFECTING


_ANY = pl.BlockSpec(memory_space=pl.ANY)


def _tie(v, dep, name):
    def body(v_ref, dep_ref, o_ref):
        del v_ref, dep_ref, o_ref

    return pl.pallas_call(body, out_shape=jax.ShapeDtypeStruct(v.shape, v.dtype), in_specs=[_ANY, _ANY],
                          out_specs=_ANY, input_output_aliases={0: 0}, name=name)(v, dep)


def _split_call(name, arrays, start=None, wait=None, wait_sems=None, after=None):
    keys = list(arrays)
    n = len(keys)
    n_start = start.n if start is not None else 0
    afters = [] if after is None else (list(after) if isinstance(after, (list, tuple)) else [after])

    def body(*refs):
        pos = n
        if wait is not None:
            wss, wrs = refs[pos], refs[pos + 1]
            pos += 2
        pos += len(afters)
        if start is not None:
            nss, nrs = refs[pos], refs[pos + 1]
            pos += 2
        R = dict(zip(keys, refs[pos:pos + n]))
        token = refs[pos + n]
        x, y, c = _coords()

        def desc(src, dst, dev, ss, rs, k):
            return pltpu.make_async_remote_copy(src_ref=src, dst_ref=dst, send_sem=ss.at[k], recv_sem=rs.at[k],
                                                device_id=dev, device_id_type=MESH)

        def run(sk):
            if wait is not None:
                for k, (snd, land) in enumerate(wait.copies(sk, R)):
                    if snd is not None:
                        desc(snd[0], snd[1], snd[2], wss, wrs, k).wait_send()
                    if land is not None:
                        desc(land, land, (x, y, c), wss, wrs, k).wait_recv()
            if start is not None:
                for k, (snd, land) in enumerate(start.copies(sk, R)):
                    if snd is not None:
                        desc(snd[0], snd[1], snd[2], nss, nrs, k).start()

        _per_chip(run)
        token[...] = jnp.zeros_like(token)

    hbm = pl.BlockSpec(memory_space=HBM)
    vals = [arrays[k] for k in keys]
    ins, in_specs = list(vals), [hbm] * n
    if wait is not None:
        ins += list(wait_sems)
        in_specs += [_SEM, _SEM]
    ins += afters
    in_specs += [pl.BlockSpec(memory_space=pl.ANY)] * len(afters)
    out_shape, out_specs = [], []
    if start is not None:
        out_shape += [pltpu.SemaphoreType.DMA((n_start,)), pltpu.SemaphoreType.DMA((n_start,))]
        out_specs += [_SEM, _SEM]
    first = len(out_shape)
    out_shape += [jax.ShapeDtypeStruct(v.shape, v.dtype) for v in vals] + [jax.ShapeDtypeStruct((8, 128), F32)]
    out_specs += [hbm] * n + [pl.BlockSpec(memory_space=pltpu.VMEM)]
    res = pl.pallas_call(
        body, out_shape=tuple(out_shape), in_specs=in_specs, out_specs=tuple(out_specs),
        input_output_aliases={i: first + i for i in range(n)}, name=name,
        compiler_params=pltpu.CompilerParams(has_side_effects=_EFFECT),
    )(*ins)
    sems = (res[0], res[1]) if start is not None else None
    return dict(zip(keys, res[first:first + n])), sems, res[-1]


class _Plan:
    def __init__(self, n, copies):
        self.n, self.copies = n, copies


_HM, _HX = PMAIN // 2, XTRA // 2
WAVE0 = 768
WAVES = ((0, WAVE0), (WAVE0, _HM - WAVE0))
_WIN = {
    "wq0": (True, "wct", lambda r, sc, hc: r.at[_rows(PMAIN * sc + _HM * hc + WAVES[0][0], WAVES[0][1]), :]),
    "wq1": (True, "wct", lambda r, sc, hc: r.at[_rows(PMAIN * sc + _HM * hc + WAVES[1][0], WAVES[1][1]), :]),
    "xt": (True, "xt", lambda r, sc, hc: r.at[sc, _rows(_HX * hc, _HX), :]),
    "w1": (True, "w1", lambda r, sc, hc: r.at[_rows(512 * hc, 512), pl.ds(1024 * sc, 1024)]),
    "w2": (True, "w2", lambda r, sc, hc: r.at[_rows(1024 * sc + 512 * hc, 512), :]),
    "wa": (True, "wa", lambda r, sc, hc: r.at[_rows(256 * sc + 128 * hc, 128), :]),
    "wb": (True, "wb", lambda r, sc, hc: r.at[_rows(512 * sc + 256 * hc, 256), :]),
    "wo": (True, "wo", lambda r, sc, hc: r.at[_rows(256 * sc + 128 * hc, 128), :]),
    "cw": (False, "cw", lambda r, sc, hc: r.at[sc]),
}


_PIECE_SRC = {
    "wq0": lambda p, hc: p.at[_rows(_HM * hc + WAVES[0][0], WAVES[0][1]), :],
    "wq1": lambda p, hc: p.at[_rows(_HM * hc + WAVES[1][0], WAVES[1][1]), :],
    "xt": lambda p, hc: p.at[_rows(PMAIN + _HX * hc, _HX), :],
}


def _ag_chips_plan(keys):
    def copies(sk, R):
        _, _, c = _coords()
        out = []
        for key in keys:
            _, arr, win = _WIN[key]
            for (px, py), ps in _other_chips(sk):
                dst = win(R[arr], sk, c)
                src = _PIECE_SRC[key](R["piece"], c) if key in _PIECE_SRC else dst
                out.append(((src, dst, (px, py, c)), win(R[arr], ps, c)))
        return out
    return _Plan(3 * len(keys), copies)


def _ag_sibling_plan(keys):
    keys = [k for k in keys if _WIN[k][0]]

    def copies(sk, R):
        x, y, c = _coords()
        out = []
        for key in keys:
            _, arr, win = _WIN[key]
            for _, ps in _other_chips(sk):
                w = win(R[arr], ps, c)
                out.append(((w, w, (x, y, 1 - c)), win(R[arr], ps, 1 - c)))
        return out
    return _Plan(3 * len(keys), copies)


def _in_proj_wave(h, wct, wave, proj=None, tm=2048):
    L = h.shape[0]
    tm = min(tm, L)
    off, size = WAVES[wave]
    start = lambda j: pl.multiple_of(_HM * j + off, 128)

    def kern(h_ref, w_ref, *rest):
        o_ref = rest[-1]
        o_ref[...] = lax.dot_general(h_ref[...], w_ref[...], _DIMS["nt"], preferred_element_type=F32).astype(BF16)

    in_specs = [pl.BlockSpec((tm, D), lambda j, i: (i, 0)),
                pl.BlockSpec((pl.Element(size), pl.Element(D)), lambda j, i: (start(j), 0))]
    args, aliases = [h, wct], {}
    if proj is not None:
        in_specs.append(pl.BlockSpec(memory_space=pl.ANY))
        args.append(proj)
        aliases = {2: 0}
    return pl.pallas_call(
        kern, grid=(8, L // tm), in_specs=in_specs,
        out_specs=pl.BlockSpec((pl.Element(tm), pl.Element(size)), lambda j, i: (i * tm, start(j))),
        out_shape=jax.ShapeDtypeStruct((L, NCW), BF16), input_output_aliases=aliases,
        name="in_proj_wave%d" % wave, compiler_params=_cp(("parallel", "parallel")),
    )(*args)


def _fix_wct(wct, xt):
    nb = PMAIN // XTRA

    def kern(w_ref, x_ref, o_ref):
        k = pl.program_id(0)
        xv = x_ref[0]
        o_ref[...] = jnp.where(k < 3, (w_ref[...].astype(F32) + xv.astype(F32)).astype(BF16), xv)

    blk = pl.BlockSpec((XTRA, D), lambda k: (nb * (k + 1), 0))
    rblk = pl.BlockSpec((XTRA, D), lambda k: (jnp.where(k < 3, nb * (k + 1), 0), 0))
    return pl.pallas_call(
        kern, grid=(4,), in_specs=[rblk, pl.BlockSpec((1, XTRA, D), lambda k: (k, 0, 0))], out_specs=blk,
        out_shape=jax.ShapeDtypeStruct(wct.shape, BF16), input_output_aliases={0: 0}, name="fix_wct",
        compiler_params=_cp(("arbitrary",)),
    )(wct, xt)


_HP = PIECE // 2
_GWIN = [
    lambda r, sc, hc: r.at[_rows(PMAIN * sc + _HP * hc, _HP), :],
    lambda r, sc, hc: r.at[_rows(512 * hc, 512), pl.ds(1024 * sc, 1024)],
    lambda r, sc, hc: r.at[_rows(1024 * sc + 512 * hc, 512), :],
    lambda r, sc, hc: r.at[_rows(256 * sc + 128 * hc, 128), :],
    lambda r, sc, hc: r.at[_rows(512 * sc + 256 * hc, 256), :],
    lambda r, sc, hc: r.at[_rows(256 * sc + 128 * hc, 128), :],
]
HALF_SHAPES = [(PIECE // 2, D), (512, 1024), (512, 1024), (128, 1024), (256, 1024), (128, 1024)]


def _rs_sibling_plan(ts):
    def copies(sk, R):
        x, y, c = _coords()
        out = []
        for t in ts:
            for sc in range(4):
                land = R["ra%d" % t].at[sc]
                out.append(((_GWIN[t](R["g%d" % t], sc, 1 - c), land, (x, y, 1 - c)), land))
        return out
    return _Plan(4 * len(ts), copies)


def _rs_chips_plan(ts):
    def copies(sk, R):
        _, _, c = _coords()
        out = []
        for t in ts:
            for j, ((px, py), ps) in enumerate(_other_chips(sk)):
                land = R["rb%d" % t].at[j]
                out.append(((R["hb%d" % t].at[ps], land, (px, py, c)), land))
        return out
    return _Plan(3 * len(ts), copies)


def _rs_share_plan(ts):
    def copies(sk, R):
        x, y, c = _coords()
        out = []
        for t in ts:
            rows = HALF_SHAPES[t][0]
            mine = R["f%d" % t].at[_rows(rows * c, rows), :]
            out.append(((mine, mine, (x, y, 1 - c)), R["f%d" % t].at[_rows(rows * (1 - c), rows), :]))
        return out
    return _Plan(len(ts), copies)


def _half_tiling(t):
    rows, cols = HALF_SHAPES[t]
    if t == 0:
        return (rows // 2, cols), 2, lambda i: (i, 0)
    return (rows, cols), 1, lambda i: (0, 0)


def _window_spec(t, blk):
    if t == 0:
        return pl.BlockSpec((pl.Element(blk[0]), pl.Element(blk[1])), lambda i, sc, idx_ref: (
            pl.multiple_of(PMAIN * sc + _HP * idx_ref[1] + blk[0] * i, 128), 0))
    if t == 1:
        return pl.BlockSpec(blk, lambda i, sc, idx_ref: (idx_ref[1], sc))
    return pl.BlockSpec(blk, lambda i, sc, idx_ref: (2 * sc + idx_ref[1], 0))


def _chip_sum(g, ra, t, idx, name):
    rows, cols = HALF_SHAPES[t]
    blk, nblk, inner = _half_tiling(t)

    def kern(idx_ref, g_ref, r_ref, hb_ref, hf_ref):
        v = g_ref[...].astype(F32) + r_ref[0].astype(F32)
        hb_ref[0] = v.astype(BF16)

        @pl.when(pl.program_id(1) == idx_ref[0])
        def _():
            hf_ref[...] = v

    omap = lambda i, sc, idx_ref: (sc,) + inner(i)
    grid_spec = pltpu.PrefetchScalarGridSpec(
        num_scalar_prefetch=1, grid=(nblk, 4),
        in_specs=[_window_spec(t, blk), pl.BlockSpec((1,) + blk, omap)],
        out_specs=(pl.BlockSpec((1,) + blk, omap), pl.BlockSpec(blk, lambda i, sc, idx_ref: inner(i))))
    return pl.pallas_call(
        kern, grid_spec=grid_spec,
        out_shape=(jax.ShapeDtypeStruct((4, rows, cols), BF16), jax.ShapeDtypeStruct((rows, cols), F32)),
        name=name, compiler_params=_cp(("parallel", "arbitrary")),
    )(idx, g, ra)


def _chip_sum_part(g, ra, t, idx, name, own, dep=None):
    rows, cols = HALF_SHAPES[t]
    blk, nblk, inner = _half_tiling(t)
    chip = (lambda k, idx_ref: idx_ref[0]) if own else (lambda k, idx_ref: lax.rem(idx_ref[0] + 1 + k, 4))
    win = _window_spec(t, blk)
    deps = [] if dep is None else [dep]

    def kern(idx_ref, g_ref, r_ref, *rest):
        v = g_ref[...].astype(F32) + r_ref[0].astype(F32)
        if own:
            rest[-1][...] = v
        else:
            rest[-1][0] = v.astype(BF16)

    omap = lambda i, k, idx_ref: (chip(k, idx_ref),) + inner(i)
    grid_spec = pltpu.PrefetchScalarGridSpec(
        num_scalar_prefetch=1, grid=(nblk, 1 if own else 3),
        in_specs=[pl.BlockSpec(win.block_shape, lambda i, k, idx_ref: win.index_map(i, chip(k, idx_ref), idx_ref)),
                  pl.BlockSpec((1,) + blk, omap)] + [pl.BlockSpec(memory_space=pl.ANY)] * len(deps),
        out_specs=pl.BlockSpec(blk, lambda i, k, idx_ref: inner(i)) if own else pl.BlockSpec((1,) + blk, omap))
    return pl.pallas_call(
        kern, grid_spec=grid_spec,
        out_shape=jax.ShapeDtypeStruct((rows, cols), F32) if own else jax.ShapeDtypeStruct((4, rows, cols), BF16),
        name=name, compiler_params=_cp(("parallel", "arbitrary")),
    )(idx, g, ra, *deps)


def _final_sum(hf, rb, t, idx, name):
    rows, cols = HALF_SHAPES[t]
    blk, nblk, inner = _half_tiling(t)
    nbr = rows // blk[0]

    def kern(idx_ref, h_ref, r_ref, o_ref):
        o_ref[...] = ((h_ref[...] + r_ref[0].astype(F32)) + r_ref[1].astype(F32)) + r_ref[2].astype(F32)

    def omap(i, idx_ref):
        r, cidx = inner(i)
        return nbr * idx_ref[1] + r, cidx

    grid_spec = pltpu.PrefetchScalarGridSpec(
        num_scalar_prefetch=1, grid=(nblk,),
        in_specs=[pl.BlockSpec(blk, lambda i, idx_ref: inner(i)),
                  pl.BlockSpec((3,) + blk, lambda i, idx_ref: (0,) + inner(i))],
        out_specs=pl.BlockSpec(blk, omap))
    return pl.pallas_call(
        kern, grid_spec=grid_spec, out_shape=jax.ShapeDtypeStruct((2 * rows, cols), F32),
        name=name, compiler_params=_cp(("parallel",)),
    )(idx, hf, rb)


class _ReduceScatter:
    def __init__(self, ts, grads, idx, tag):
        self.ts, self.idx, self.tag = ts, idx, tag
        arr = {}
        for t in ts:
            arr["g%d" % t] = grads[t]
            arr["ra%d" % t] = lax.empty((4,) + HALF_SHAPES[t], BF16)
        self.plan = _rs_sibling_plan(ts)
        self.arr, self.sems, self.token = _split_call("rs_sibling_start_" + tag, arr, start=self.plan)

    def chips(self, after, own_later=False):
        arr, _, _ = _split_call("rs_sibling_wait_" + self.tag, self.arr, wait=self.plan, wait_sems=self.sems, after=after)
        brr, self.hf = {}, {}
        for t in self.ts:
            if own_later:
                hb = _chip_sum_part(arr["g%d" % t], arr["ra%d" % t], t, self.idx, "chip_sum_others_%d" % t, False)
            else:
                hb, self.hf[t] = _chip_sum(arr["g%d" % t], arr["ra%d" % t], t, self.idx, "chip_sum_%d" % t)
            brr["hb%d" % t] = hb
            brr["rb%d" % t] = lax.empty((3,) + HALF_SHAPES[t], BF16)
        self.plan = _rs_chips_plan(self.ts)
        self.arr, self.sems, self.token = _split_call("rs_chips_start_" + self.tag, brr, start=self.plan)
        if own_later:
            for t in self.ts:
                self.hf[t] = _chip_sum_part(arr["g%d" % t], arr["ra%d" % t], t, self.idx, "chip_sum_own_%d" % t, True,
                                            dep=self.token)
        return self.token

    def share(self, after):
        brr, _, _ = _split_call("rs_chips_wait_" + self.tag, self.arr, wait=self.plan, wait_sems=self.sems, after=after)
        frr = {"f%d" % t: _final_sum(self.hf[t], brr["rb%d" % t], t, self.idx, "final_sum_%d" % t) for t in self.ts}
        self.plan = _rs_share_plan(self.ts)
        self.arr, self.sems, self.token = _split_call("rs_share_start_" + self.tag, frr, start=self.plan)
        return self.token

    def result(self, after):
        frr, _, _ = _split_call("rs_share_wait_" + self.tag, self.arr, wait=self.plan, wait_sems=self.sems, after=after)
        return {t: frr["f%d" % t] for t in self.ts}


def _all8_plan(key):
    def copies(sk, R):
        x, y, c = _coords()
        own = R[key].at[4 * x + 2 * y + c]
        out = []
        for k in range(1, 8):
            dev = ((1 - x) if (k >> 2) & 1 else x, (1 - y) if (k >> 1) & 1 else y, (1 - c) if k & 1 else c)
            out.append(((own, own, dev), R[key].at[4 * dev[0] + 2 * dev[1] + dev[2]]))
        return out
    return _Plan(7, copies)


def _sum8(v, name="small_sum"):
    def kern(v_ref, o_ref):
        acc = v_ref[0]
        for k in range(1, 8):
            acc = acc + v_ref[k]
        o_ref[...] = acc

    return pl.pallas_call(kern, out_shape=jax.ShapeDtypeStruct(v.shape[1:], F32), name=name)(v)


def _adamw(w, g, m, v, name, tr=128, blk0=0, nblk=None, into=None, copy_g=False):
    R, C = w.shape
    tr = min(tr, R)
    if nblk is None:
        assert R % tr == 0 and blk0 == 0
        nblk = R // tr
    n_out = 4 if copy_g else 3

    def kern(*refs):
        w_ref, g_ref, m_ref, v_ref = refs[:4]
        d_ref, mo_ref, vo_ref = refs[-n_out:][:3]
        gv = g_ref[...]
        mn = ADAM_B1 * m_ref[...] + (1.0 - ADAM_B1) * gv
        vn = ADAM_B2 * v_ref[...] + (1.0 - ADAM_B2) * (gv * gv)
        m_hat = mn / (1.0 - ADAM_B1 ** ADAM_STEP)
        v_hat = vn / (1.0 - ADAM_B2 ** ADAM_STEP)
        d_ref[...] = -ADAM_LR * (m_hat / (jnp.sqrt(v_hat) + ADAM_EPS) + ADAM_WD * w_ref[...])
        mo_ref[...] = mn
        vo_ref[...] = vn
        if copy_g:
            refs[-1][...] = gv

    blk = pl.BlockSpec((tr, C), lambda i: (blk0 + i, 0))
    sd = jax.ShapeDtypeStruct((R, C), F32)
    in_specs, args, aliases = [blk] * 4, [w, g, m, v], {}
    if into is not None:
        in_specs += [pl.BlockSpec(memory_space=pl.ANY)] * 3
        args += list(into)
        aliases = {4: 0, 5: 1, 6: 2}
    return pl.pallas_call(kern, grid=(nblk,), in_specs=in_specs, out_specs=(blk,) * n_out, out_shape=(sd,) * n_out,
                          input_output_aliases=aliases, name=name, compiler_params=_cp(("parallel",)))(*args)


def _adamw_w_in(wt, gp, mt, vt, offs, name, r0, tr, nblk, views, into=None, blk_key=None):
    el = lambda n: (pl.Element(n), pl.Element(D))
    first = (lambda o: r0) if blk_key is None else (lambda o: tr * o[blk_key])
    own = pl.BlockSpec(el(tr), lambda i, o: (pl.multiple_of(first(o) + tr * i, 8), 0))

    def view(k):
        return pl.BlockSpec(el(tr), lambda i, o: (pl.multiple_of(jnp.maximum(first(o) + tr * i + o[k], 0), 8), 0))

    def kern(o_ref, w_ref, m_ref, v_ref, *refs):
        g_refs, (d_ref, mo_ref, vo_ref, go_ref) = refs[:len(views)], refs[-4:]
        gv = g_refs[0][...]
        if len(views) == 2:
            row = first(o_ref) + tr * pl.program_id(0) + lax.broadcasted_iota(jnp.int32, (tr, D), 0)
            gv = jnp.where(row < o_ref[2], gv, g_refs[1][...])
        mn = ADAM_B1 * m_ref[...] + (1.0 - ADAM_B1) * gv
        vn = ADAM_B2 * v_ref[...] + (1.0 - ADAM_B2) * (gv * gv)
        m_hat = mn / (1.0 - ADAM_B1 ** ADAM_STEP)
        v_hat = vn / (1.0 - ADAM_B2 ** ADAM_STEP)
        d_ref[...] = -ADAM_LR * (m_hat / (jnp.sqrt(v_hat) + ADAM_EPS) + ADAM_WD * w_ref[...])
        mo_ref[...] = mn
        vo_ref[...] = vn
        go_ref[...] = gv

    in_specs = [own, own, own] + [view(k) for k in views]
    args = [wt, mt, vt] + [gp] * len(views)
    aliases = {}
    if into is not None:
        in_specs += [pl.BlockSpec(memory_space=pl.ANY)] * 4
        args += list(into)
        aliases = {1 + len(args) - 4 + j: j for j in range(4)}
    grid_spec = pltpu.PrefetchScalarGridSpec(num_scalar_prefetch=1, grid=(nblk,), in_specs=in_specs,
                                             out_specs=(own,) * 4)
    sd = jax.ShapeDtypeStruct(wt.shape, F32)
    return pl.pallas_call(kern, grid_spec=grid_spec, out_shape=(sd,) * 4, input_output_aliases=aliases, name=name,
                          compiler_params=_cp(("parallel",)))(offs, *args)


def _to_piece(wt, s):
    z = lambda n: jnp.zeros((n, D), wt.dtype)
    pads = [functools.partial(lambda k, w: jnp.pad(w, ((8 * k, PIECE - W_SHARD - 8 * k), (0, 0))).astype(BF16), k)
            for k in range(3)]
    last = lambda w: jnp.concatenate([z(24), w[:744], w[776:], w[744:776], z(PIECE - 24 - W_SHARD)], axis=0).astype(BF16)
    return lax.switch(s, pads + [last], wt)


_SMALL = [("b_gate", 2048), ("ssm_conv_b", 4096), ("dt_bias", 32), ("A_log", 32), ("D_skip", 32),
          ("ssm_norm_w", 2048), ("norm_mlp", 1024), ("norm_final", 1024), ("sc_conv_w", 3072), ("ssm_conv_w", 16384),
          ("loss", 1)]


def _pack(vals, table, rows):
    parts = []
    for name, n in table:
        v = vals[name].reshape(-1).astype(F32)
        pad = (-n) % 128
        parts.append(jnp.pad(v, (0, pad)) if pad else v)
    flat = jnp.concatenate(parts)
    return jnp.pad(flat, (0, rows * 128 - flat.shape[0])).reshape(rows, 128)


def _unpack(arr, table):
    flat = arr.reshape(-1)
    out, off = {}, 0
    for name, n in table:
        out[name] = flat[off:off + n]
        off += n + ((-n) % 128)
    return out


def kernel(x, norm_mix, w_in, b_gate, sc_conv_w, ssm_conv_w, ssm_conv_b, dt_bias, A_log, D_skip, ssm_norm_w, w_branch_sc, w_branch_ssm, w_out, norm_mlp, w_mlp1, w_mlp2, norm_final, loss_target, m_norm_mix, m_w_in, m_b_gate, m_sc_conv_w, m_ssm_conv_w, m_ssm_conv_b, m_dt_bias, m_A_log, m_D_skip, m_ssm_norm_w, m_w_branch_sc, m_w_branch_ssm, m_w_out, m_norm_mlp, m_w_mlp1, m_w_mlp2, m_norm_final, v_norm_mix, v_w_in, v_b_gate, v_sc_conv_w, v_ssm_conv_w, v_ssm_conv_b, v_dt_bias, v_A_log, v_D_skip, v_ssm_norm_w, v_w_branch_sc, v_w_branch_ssm, v_w_out, v_norm_mlp, v_w_mlp1, v_w_mlp2, v_norm_final):
    L = x.shape[1]
    nc = L // Q
    xi, yi, ci = lax.axis_index("x"), lax.axis_index("y"), lax.axis_index("c")
    s = 2 * xi + yi
    idx = jnp.stack([s, ci]).astype(jnp.int32)
    x0 = x.reshape(L, D)
    tgt = loss_target.reshape(L, D)
    small_names = ["b_gate", "sc_conv_w", "ssm_conv_w", "ssm_conv_b", "dt_bias", "A_log", "D_skip", "ssm_norm_w",
                   "norm_mlp", "norm_final"]
    small_wmv = [dict(zip(small_names, vals)) for vals in (
        (b_gate, sc_conv_w, ssm_conv_w, ssm_conv_b, dt_bias, A_log, D_skip, ssm_norm_w, norm_mlp, norm_final),
        (m_b_gate, m_sc_conv_w, m_ssm_conv_w, m_ssm_conv_b, m_dt_bias, m_A_log, m_D_skip, m_ssm_norm_w, m_norm_mlp,
         m_norm_final),
        (v_b_gate, v_sc_conv_w, v_ssm_conv_w, v_ssm_conv_b, v_dt_bias, v_A_log, v_D_skip, v_ssm_norm_w, v_norm_mlp,
         v_norm_final))]
    small_table = [(n, int(small_wmv[0][n].size)) for n in small_names]
    small_rows = 136
    pk_w, pk_m, pk_v = [_pack(d, small_table, small_rows) for d in small_wmv]

    piece = _to_piece(w_in.T, s)
    nb = PMAIN // XTRA
    cws = jnp.zeros((8, 1280), F32)
    cws = cws.at[0:3, 0:256].set(sc_conv_w).at[0:4, 256:1280].set(ssm_conv_w)
    cw0 = lax.dynamic_update_slice(jnp.zeros((4, 8, 1280), F32), cws[None], (s, 0, 0))
    win_keys, win2_keys, mid_keys, end_keys = ["xt", "cw", "wq0"], ["wq1"], ["wa", "wb", "wo", "w1"], ["w2"]
    gw, sems_w, tok = _split_call(
        "ag_win_start", {"wct": lax.empty((NCW, D), BF16), "xt": lax.empty((4, XTRA, D), BF16), "cw": cw0, "piece": piece},
        start=_ag_chips_plan(win_keys))
    g2, sems_w2, tok = _split_call("ag_win2_start", {"wct": gw["wct"], "piece": gw["piece"]},
                                   start=_ag_chips_plan(win2_keys), after=tok)
    piece = g2["piece"]
    gw["wct"] = _place(piece, (NCW, D), (XTRA, D), lambda i, r: (nb * r[0] + i, 0), idx, "place_wct", nblk=nb,
                       dep=tok, into=g2["wct"])
    gw["xt"] = _place(piece, (4, XTRA, D), (1, XTRA, D), lambda i, r: (r[0], 0, 0), idx, "place_xt", blk0=nb, nblk=1,
                      dep=tok, into=gw["xt"])
    gw["piece"] = piece
    wa0 = _place(w_branch_sc, (D, D), (256, 1024), lambda i, r: (r[0], 0), idx, "place_wa", dep=tok)
    wb0 = _place(w_branch_ssm, (INNER, D), (512, 1024), lambda i, r: (r[0], 0), idx, "place_wb", dep=tok)
    wo0 = _place(w_out, (D, D), (256, 1024), lambda i, r: (r[0], 0), idx, "place_wo", dep=tok)
    w10 = _place(w_mlp1, (D, DFF), (256, 1024), lambda i, r: (i, r[0]), idx, "place_w1", dep=tok)
    gm, sems_m, tok = _split_call("ag_mid_start", {"wa": wa0, "wb": wb0, "wo": wo0, "w1": w10},
                                  start=_ag_chips_plan(mid_keys))
    w20 = _place(w_mlp2, (DFF, D), (256, 1024), lambda i, r: (4 * r[0] + i, 0), idx, "place_w2", dep=tok)
    ge, sems_e, tok = _split_call("ag_end_start", {"w2": w20}, start=_ag_chips_plan(end_keys))
    h = _rms_fwd(x0, norm_mix, "rms_mix", dep=tok)
    gw, sems_w, tok = _split_call("ag_win_pass", gw, wait=_ag_chips_plan(win_keys), wait_sems=sems_w,
                                  start=_ag_sibling_plan(win_keys), after=[h, pk_w, pk_m, pk_v])
    gw, _, _ = _split_call("ag_win_done", gw, wait=_ag_sibling_plan(win_keys), wait_sems=sems_w, after=tok)
    wc, cw_all = _fix_wct(gw["wct"], gw["xt"]), gw["cw"]
    sc_w_full = jnp.concatenate([cw_all[k, :, 0:256] for k in range(4)], axis=1)
    ssm_w_full = jnp.concatenate([cw_all[k, :, 256:1280] for k in range(4)], axis=1)
    cw4 = ssm_w_full.at[4].set(ssm_conv_b)
    vec = jnp.zeros((8, 128), F32).at[0, :NH].set(dt_bias).at[1, :NH].set(A_log)
    vecg = jnp.zeros((NG, 8, 128), F32).at[:, 0, :4].set(A_log.reshape(NG, 4)).at[:, 1, :4].set(D_skip.reshape(NG, 4))

    dtraw = _matmul(h, wc[C_DT:], "nt", F32, 512, 256, 1024, "in_proj_dt")
    proj = _in_proj_wave(h, wc, 0)
    g2, sems_w2, tok = _split_call("ag_win2_pass", {"wct": wc, "piece": gw["piece"]},
                                   wait=_ag_chips_plan(win2_keys), wait_sems=sems_w2,
                                   start=_ag_sibling_plan(win2_keys), after=[proj, dtraw])
    g2, _, _ = _split_call("ag_win2_done", g2, wait=_ag_sibling_plan(win2_keys), wait_sems=sems_w2, after=tok)
    wc = g2["wct"]
    proj = _in_proj_wave(h, wc, 1, proj=proj)
    ya = _sc_fwd(proj, sc_w_full)
    xbc = _ssm_conv_fwd(proj, cw4)
    dt4, cs4, sg4 = _dt_prep(dtraw, vec)
    y, s_all = _ssd_fwd(xbc, dt4, cs4, vecg)
    gm, sems_m, tok = _split_call("ag_mid_pass", gm, wait=_ag_chips_plan(mid_keys), wait_sems=sems_m,
                                  start=_ag_sibling_plan(mid_keys), after=[y, ya])
    y = _tie(y, tok, "tie_y")
    yb = _gnorm_fwd(y, proj, ssm_norm_w)
    gm, _, _ = _split_call("ag_mid_done", gm, wait=_ag_sibling_plan(mid_keys), wait_sems=sems_m, after=yb)
    wa, wb, wo, w1 = gm["wa"], gm["wb"], gm["wo"], gm["w1"]
    ge, sems_e, tok = _split_call("ag_end_pass", ge, wait=_ag_chips_plan(end_keys), wait_sems=sems_e,
                                  start=_ag_sibling_plan(end_keys), after=yb)
    br_a = _matmul(ya, wa, "nn", F32, 1024, 1024, 1024, "branch_sc", dep=tok)
    br_b, merged = _branch_ssm_merge(yb, wb, proj, b_gate, br_a)
    x1, h2 = _matmul_res_rms(merged, wo, x0, norm_mlp, 1024, "out_proj")
    a1, rl = _matmul(h2, w1, "nn", BF16, 1024, 1024, 1024, "mlp1", epi="relu2", n_outer=True)
    ge, _, _ = _split_call("ag_end_done", ge, wait=_ag_sibling_plan(end_keys), wait_sems=sems_e, after=a1)
    w2 = ge["w2"]
    dx2, g_nf, loss8 = _matmul_res_final(rl, w2, x1, norm_final, tgt, 512, "mlp2")

    da = _matmul(dx2, w2, "nt", BF16, 1024, 1024, 1024, "mlp2_dx", epi="drelu", extra=a1, n_outer=True)
    g_w2 = _matmul(rl, dx2, "tn", BF16, 1024, 1024, 2048, "mlp2_dw")
    g_w1 = _matmul(h2, da, "tn", BF16, 1024, 1024, 2048, "mlp1_dw")
    dx1, g_nmlp = _matmul_rms_bwd(da, w1, "nt", x1, norm_mlp, dx2, 512, 4096, "mlp1_dx")
    g_wo = _matmul(merged, dx1, "tn", BF16, 1024, 1024, 2048, "out_proj_dw")
    dproj = lax.empty((L, NCW), BF16)
    dbr, dproj, g_bg = _merge_bwd(dx1, wo, proj, b_gate, br_a, br_b, dproj)
    dya = _matmul(dbr[0], wa, "nt", F32, 1024, 1024, 1024, "branch_sc_dx")
    g_wa = _matmul(ya, dbr[0], "tn", BF16, 1024, 1024, 2048, "branch_sc_dw")
    dproj, g_scw = _sc_bwd(dya, proj, sc_w_full, dproj)
    g_wb = _matmul(yb, dbr[1], "tn", BF16, 1024, 1024, 2048, "branch_ssm_dw")
    rs_a = _ReduceScatter([1, 2, 3, 4, 5], {1: g_w1, 2: g_w2, 3: g_wa, 4: g_wb, 5: g_wo}, idx, "a")
    dy, dproj, g_snw = _gnorm_bwd(dbr, wb, y, proj, ssm_norm_w, dproj, rs_a.token)
    tok = rs_a.chips(after=dy)
    dxs, dbm, dcm, ddt_g, st = _ssd_bwd(xbc, dt4, cs4, sg4, vecg, s_all, _tie(dy, tok, "tie_dy"))
    dproj, gx1 = _ssm_conv_bwd(dxs, proj, cw4, dproj, 0, "ssm_conv_bwd_x")
    dproj, gx2 = _ssm_conv_bwd(dbm, proj, cw4, dproj, INNER, "ssm_conv_bwd_b")
    dproj, gx3 = _ssm_conv_bwd(dcm, proj, cw4, dproj, INNER + NG * NS, "ssm_conv_bwd_c")
    g_cw4 = jnp.concatenate([gx1, gx2, gx3], axis=1)
    dproj, g_dtb = _dt_bwd(ddt_g, dproj)
    small = {"b_gate": g_bg[0], "ssm_conv_b": g_cw4[4], "dt_bias": g_dtb[0, :NH],
             "A_log": st[:, 0, :4], "D_skip": st[:, 1, :4], "ssm_norm_w": g_snw[0], "norm_mlp": g_nmlp[0],
             "norm_final": g_nf[0], "sc_conv_w": g_scw[0:3], "ssm_conv_w": g_cw4[0:4], "loss": loss8[0, 0:1]}
    me = 4 * xi + 2 * yi + ci
    sm8 = lax.dynamic_update_slice(jnp.zeros((8, SMALL_ROWS, 128), F32), _pack(small, _SMALL, SMALL_ROWS)[None], (me, 0, 0))
    sm_arr, sm_sems, tok = _split_call("small_start", {"sm": sm8}, start=_all8_plan("sm"))
    g_wc = _matmul(dproj, h, "tn", BF16, 1280, 1024, 2048, "in_proj_dw", dep=tok)
    rs_b = _ReduceScatter([0], {0: g_wc}, idx, "b")
    tok = rs_a.share(after=rs_b.token)
    tok = rs_b.chips(after=tok, own_later=True)
    grad_x, g_nm = _matmul_rms_bwd(dproj, wc, "nn", x0, norm_mix, dx1, 512, 3840, "in_proj_dx", dep=tok)
    nm8 = lax.dynamic_update_slice(jnp.zeros((8, 8, 128), F32), g_nm[0].reshape(1, 8, 128), (me, 0, 0))
    nm_arr, nm_sems, tok = _split_call("norm_mix_start", {"nm": nm8}, start=_all8_plan("nm"))
    sm_arr, _, _ = _split_call("small_wait", sm_arr, wait=_all8_plan("sm"), wait_sems=sm_sems, after=tok)
    small_sum = _sum8(sm_arr["sm"])
    gs = _unpack(small_sum, _SMALL)
    red = rs_a.result(after=tok)
    big = {"w_mlp1": red[1], "w_mlp2": red[2], "w_branch_sc": red[3], "w_branch_ssm": red[4], "w_out": red[5]}

    given = dict(norm_mix=norm_mix, w_in=w_in, b_gate=b_gate, sc_conv_w=sc_conv_w, ssm_conv_w=ssm_conv_w, ssm_conv_b=ssm_conv_b, dt_bias=dt_bias, A_log=A_log, D_skip=D_skip, ssm_norm_w=ssm_norm_w, w_branch_sc=w_branch_sc, w_branch_ssm=w_branch_ssm, w_out=w_out, norm_mlp=norm_mlp, w_mlp1=w_mlp1, w_mlp2=w_mlp2, norm_final=norm_final,
                 m_norm_mix=m_norm_mix, m_w_in=m_w_in, m_b_gate=m_b_gate, m_sc_conv_w=m_sc_conv_w, m_ssm_conv_w=m_ssm_conv_w, m_ssm_conv_b=m_ssm_conv_b, m_dt_bias=m_dt_bias, m_A_log=m_A_log, m_D_skip=m_D_skip, m_ssm_norm_w=m_ssm_norm_w, m_w_branch_sc=m_w_branch_sc, m_w_branch_ssm=m_w_branch_ssm, m_w_out=m_w_out, m_norm_mlp=m_norm_mlp, m_w_mlp1=m_w_mlp1, m_w_mlp2=m_w_mlp2, m_norm_final=m_norm_final,
                 v_norm_mix=v_norm_mix, v_w_in=v_w_in, v_b_gate=v_b_gate, v_sc_conv_w=v_sc_conv_w, v_ssm_conv_w=v_ssm_conv_w, v_ssm_conv_b=v_ssm_conv_b, v_dt_bias=v_dt_bias, v_A_log=v_A_log, v_D_skip=v_D_skip, v_ssm_norm_w=v_ssm_norm_w, v_w_branch_sc=v_w_branch_sc, v_w_branch_ssm=v_w_branch_ssm, v_w_out=v_w_out, v_norm_mlp=v_norm_mlp, v_w_mlp1=v_w_mlp1, v_w_mlp2=v_w_mlp2, v_norm_final=v_norm_final)
    order = ["norm_mix", "w_in", "b_gate", "sc_conv_w", "ssm_conv_w", "ssm_conv_b", "dt_bias", "A_log", "D_skip",
             "ssm_norm_w", "w_branch_sc", "w_branch_ssm", "w_out", "norm_mlp", "w_mlp1", "w_mlp2", "norm_final"]
    grad, delta, new_m, new_v = {}, {}, {}, {}
    for n in big:
        delta[n], new_m[n], new_v[n], grad[n] = _adamw(given[n], big[n], given["m_" + n], given["v_" + n],
                                                       "adamw_" + n, copy_g=True)
    big["w_in"] = None
    grad_small = {n: gs[n].reshape(given[n].shape) for n in small_names if n not in ("sc_conv_w", "ssm_conv_w")}
    grad_small["sc_conv_w"] = lax.dynamic_slice(gs["sc_conv_w"].reshape(3, D), (0, 256 * s), (3, 256))
    grad_small["ssm_conv_w"] = lax.dynamic_slice(gs["ssm_conv_w"].reshape(4, XBC), (0, 1024 * s), (4, 1024))
    table = small_table
    ds_, ms_, vs_ = _adamw(pk_w, _pack(grad_small, table, small_rows), pk_m, pk_v, "adamw_small", tr=small_rows)
    ds_, ms_, vs_ = _unpack(ds_, table), _unpack(ms_, table), _unpack(vs_, table)
    for n in grad_small:
        shp = given[n].shape
        grad[n] = grad_small[n]
        delta[n], new_m[n], new_v[n] = ds_[n].reshape(shp), ms_[n].reshape(shp), vs_[n].reshape(shp)

    done = [new_v[n] for n in ("w_mlp1", "w_mlp2", "w_branch_sc", "w_branch_ssm", "w_out")] + [vs_["b_gate"]]
    tok = rs_b.share(after=done)
    offs = jnp.where(s == 3, jnp.array([24, -8, 744, 2072, -8], jnp.int32),
                     jnp.stack([8 * s, 8 * s, 0 * s, 8 * s, 8 * s]).astype(jnp.int32))
    offs = jnp.concatenate([offs, jnp.stack([7 * ci, 4 - 4 * ci]).astype(jnp.int32)])
    nmain = W_SHARD // 256
    wt_own = (w_in.T, rs_b.arr["f0"], m_w_in.T, v_w_in.T, offs)
    res = _adamw_w_in(*wt_own, "adamw_w_in_own", 0, 256, 4, (0, 1), blk_key=5)
    gp = rs_b.result(after=[tok, res[0]])[0]
    wt_args = (w_in.T, gp, m_w_in.T, v_w_in.T, offs)
    res = _adamw_w_in(*wt_args, "adamw_w_in", 0, 256, nmain - 4, (0, 1), into=res, blk_key=6)
    res = _adamw_w_in(*wt_args, "adamw_w_in_dt", 744, 32, 1, (3,), into=res)
    dt_, mt_, vt_, gwt = _adamw_w_in(*wt_args, "adamw_w_in_tail", 256 * nmain, 8, 1, (4,), into=res)
    grad["w_in"], delta["w_in"], new_m["w_in"], new_v["w_in"] = gwt.T, dt_.T, mt_.T, vt_.T
    nm_arr, _, _ = _split_call("norm_mix_wait", nm_arr, wait=_all8_plan("nm"), wait_sems=nm_sems, after=tok)
    g8 = _sum8(nm_arr["nm"], "norm_mix_sum")
    r8 = lambda a: a.reshape(8, 128)
    d8, m8, v8 = _adamw(r8(norm_mix), g8, r8(m_norm_mix), r8(v_norm_mix), "adamw_norm_mix", tr=8)
    grad["norm_mix"], delta["norm_mix"] = g8.reshape(D), d8.reshape(D)
    new_m["norm_mix"], new_v["norm_mix"] = m8.reshape(D), v8.reshape(D)

    loss = gs["loss"].reshape(())
    return (loss, grad_x.reshape(1, L, D), *[grad[n] for n in order], *[delta[n] for n in order],
            *[new_m[n] for n in order], *[new_v[n] for n in order])
```

```python
import functools

import jax
import jax.numpy as jnp
from jax import lax
from jax.experimental import pallas as pl
from jax.experimental.pallas import tpu as pltpu

F32 = jnp.float32
BF16 = jnp.bfloat16
MESH = pl.DeviceIdType.MESH
HBM = pltpu.HBM

D = 1024
INNER = 2048
HD = 64
NH = 32
NG = 8
NS = 128
Q = 128
GPS = 8
XBC = 4096
DFF = 4096
EPS = 1e-6
W_SHARD = 2824
NCW = 11520
PIECE = 3072
PMAIN = 2816
C_Z, C_XBC, C_GATE, C_DT = 3072, 5120, 9216, 11264
SMALL_ROWS = 256
VMEM_LIMIT = 56 * 1024 * 1024

ADAM_LR, ADAM_B1, ADAM_B2, ADAM_EPS, ADAM_WD, ADAM_STEP = 0.001, 0.9, 0.999, 1e-08, 0.01, 10


def _cp(sem=None, vmem=VMEM_LIMIT):
    return pltpu.CompilerParams(dimension_semantics=sem, vmem_limit_bytes=vmem)


def _sigmoid(v):
    return 1.0 / (1.0 + jnp.exp(-v))


_DIMS = {"nn": (((1,), (0,)), ((), ())), "nt": (((1,), (1,)), ((), ())), "tn": (((0,), (0,)), ((), ()))}


def _matmul(a, b, mode, out_dtype, tm, tn, tk, name, epi=None, extra=None, n_outer=False, dep=None):
    if mode == "tn":
        K, M = a.shape
    else:
        M, K = a.shape
    N = b.shape[0] if mode == "nt" else b.shape[1]
    tm, tn, tk = min(tm, M), min(tn, N), min(tk, K)
    assert M % tm == 0 and N % tn == 0 and K % tk == 0, (name, M, N, K, tm, tn, tk)
    nm, nn, nk = M // tm, N // tn, K // tk
    dims = _DIMS[mode]

    def ij(p0, p1):
        return (p1, p0) if n_outer else (p0, p1)

    if mode == "tn":
        a_spec = pl.BlockSpec((tk, tm), lambda p0, p1, k: (k, ij(p0, p1)[0]))
    else:
        a_spec = pl.BlockSpec((tm, tk), lambda p0, p1, k: (ij(p0, p1)[0], k))
    if mode == "nt":
        b_spec = pl.BlockSpec((tn, tk), lambda p0, p1, k: (ij(p0, p1)[1], k))
    else:
        b_spec = pl.BlockSpec((tk, tn), lambda p0, p1, k: (k, ij(p0, p1)[1]))
    o_spec = pl.BlockSpec((tm, tn), lambda p0, p1, k: ij(p0, p1))
    in_specs = [a_spec, b_spec]
    args = [a, b]
    if epi in ("res", "drelu"):
        in_specs.append(o_spec)
        args.append(extra)
    if dep is not None:
        in_specs.append(pl.BlockSpec(memory_space=pl.ANY))
        args.append(dep)
    n_in = len(args)
    if epi == "relu2":
        out_shape = (jax.ShapeDtypeStruct((M, N), out_dtype), jax.ShapeDtypeStruct((M, N), BF16))
        out_specs = (o_spec, o_spec)
    else:
        out_shape = jax.ShapeDtypeStruct((M, N), out_dtype)
        out_specs = o_spec

    def kern(*refs):
        a_ref, b_ref = refs[0], refs[1]
        e_ref = refs[2] if epi in ("res", "drelu") else None
        acc = refs[-1]
        outs = refs[n_in:-1] if nk > 1 else refs[n_in:]
        k = pl.program_id(2)

        def product():
            return lax.dot_general(a_ref[...].astype(BF16), b_ref[...].astype(BF16), dims, preferred_element_type=F32)

        def finish(r):
            if epi is None:
                outs[0][...] = r.astype(out_dtype)
            elif epi == "res":
                outs[0][...] = (r + e_ref[...]).astype(out_dtype)
            elif epi == "relu2":
                outs[0][...] = r.astype(out_dtype)
                t = jnp.maximum(r, 0.0)
                outs[1][...] = (t * t).astype(BF16)
            else:
                outs[0][...] = (r * (2.0 * jnp.maximum(e_ref[...].astype(F32), 0.0))).astype(out_dtype)

        if nk == 1:
            finish(product())
        else:
            @pl.when(k == 0)
            def _():
                acc[...] = jnp.zeros_like(acc)

            acc[...] += product()

            @pl.when(k == nk - 1)
            def _():
                finish(acc[...])

    grid = (nn, nm, nk) if n_outer else (nm, nn, nk)
    return pl.pallas_call(
        kern, grid=grid, in_specs=in_specs, out_specs=out_specs, out_shape=out_shape,
        scratch_shapes=[pltpu.VMEM((tm, tn), F32)] if nk > 1 else [], name=name,
        compiler_params=_cp(("parallel", "parallel", "arbitrary")),
    )(*args)


def _matmul_res_rms(a, b, x, w, tm, name):
    M, K = a.shape
    tm = min(tm, M)

    def kern(a_ref, b_ref, x_ref, w_ref, x1_ref, h_ref):
        x1 = x_ref[...] + lax.dot_general(a_ref[...].astype(BF16), b_ref[...].astype(BF16), _DIMS["nn"],
                                          preferred_element_type=F32)
        x1_ref[...] = x1
        r = lax.rsqrt(jnp.mean(x1 * x1, axis=-1, keepdims=True) + EPS)
        h_ref[...] = ((x1 * r) * w_ref[...]).astype(BF16)

    row = pl.BlockSpec((tm, D), lambda i: (i, 0))
    return pl.pallas_call(
        kern, grid=(M // tm,),
        in_specs=[pl.BlockSpec((tm, K), lambda i: (i, 0)), pl.BlockSpec((K, D), lambda i: (0, 0)), row,
                  pl.BlockSpec((1, D), lambda i: (0, 0))],
        out_specs=(row, row), out_shape=(jax.ShapeDtypeStruct((M, D), F32), jax.ShapeDtypeStruct((M, D), BF16)),
        name=name, compiler_params=_cp(("parallel",)),
    )(a, b, x, w.reshape(1, D))


def _matmul_res_final(a, b, x, w, tgt, tm, name):
    M, K = a.shape
    tm = min(tm, M)

    def kern(a_ref, b_ref, x_ref, w_ref, t_ref, dx_ref, gw_ref, loss_ref):
        @pl.when(pl.program_id(0) == 0)
        def _():
            gw_ref[...] = jnp.zeros_like(gw_ref)
            loss_ref[...] = jnp.zeros_like(loss_ref)

        xv = x_ref[...] + lax.dot_general(a_ref[...].astype(BF16), b_ref[...].astype(BF16), _DIMS["nn"],
                                          preferred_element_type=F32)
        r = lax.rsqrt(jnp.mean(xv * xv, axis=-1, keepdims=True) + EPS)
        xn = xv * r
        e = xn * w_ref[...] - t_ref[...]
        loss_ref[...] += 0.5 * jnp.sum(jnp.mean(e * e, axis=-1, keepdims=True))
        dyv = e * (1.0 / D)
        gw_ref[...] += jnp.broadcast_to(jnp.sum(dyv * xn, axis=0, keepdims=True), (8, D))
        dxn = dyv * w_ref[...]
        dx_ref[...] = r * (dxn - xn * jnp.mean(dxn * xn, axis=-1, keepdims=True))

    row = pl.BlockSpec((tm, D), lambda i: (i, 0))
    return pl.pallas_call(
        kern, grid=(M // tm,),
        in_specs=[pl.BlockSpec((tm, K), lambda i: (i, 0)), pl.BlockSpec((K, D), lambda i: (0, 0)), row,
                  pl.BlockSpec((1, D), lambda i: (0, 0)), row],
        out_specs=(row, pl.BlockSpec((8, D), lambda i: (0, 0)), pl.BlockSpec((8, 128), lambda i: (0, 0))),
        out_shape=(jax.ShapeDtypeStruct((M, D), F32), jax.ShapeDtypeStruct((8, D), F32),
                   jax.ShapeDtypeStruct((8, 128), F32)),
        name=name, compiler_params=_cp(("arbitrary",)),
    )(a, b, x, w.reshape(1, D), tgt)


def _matmul_rms_bwd(a, b, mode, x, w, res, tm, tk, name, dep=None):
    M, K = a.shape
    tm, tk = min(tm, M), min(tk, K)
    nk = K // tk
    assert M % tm == 0 and K % tk == 0
    b_spec = (pl.BlockSpec((tk, D), lambda k, i: (k, 0)) if mode == "nn" else pl.BlockSpec((D, tk), lambda k, i: (0, k)))
    row = pl.BlockSpec((tm, D), lambda k, i: (jnp.where(k == nk - 1, i, 0), 0))
    deps = [] if dep is None else [dep]

    def kern(a_ref, b_ref, x_ref, w_ref, res_ref, *rest):
        dx_ref, gw_ref, acc = rest[-3:]
        k, i = pl.program_id(0), pl.program_id(1)

        @pl.when((i == 0) & (k == 0))
        def _():
            gw_ref[...] = jnp.zeros_like(gw_ref)

        def product():
            return lax.dot_general(a_ref[...].astype(BF16), b_ref[...].astype(BF16), _DIMS[mode],
                                   preferred_element_type=F32)

        def finish(dyv):
            xv = x_ref[...]
            r = lax.rsqrt(jnp.mean(xv * xv, axis=-1, keepdims=True) + EPS)
            xn = xv * r
            gw_ref[...] += jnp.broadcast_to(jnp.sum(dyv * xn, axis=0, keepdims=True), (8, D))
            dxn = dyv * w_ref[...]
            dx_ref[...] = res_ref[...] + r * (dxn - xn * jnp.mean(dxn * xn, axis=-1, keepdims=True))

        if nk == 1:
            finish(product())
        else:
            rows = pl.ds(pl.multiple_of(i * tm, tm), tm)

            @pl.when(k == 0)
            def _():
                acc[rows, :] = jnp.zeros((tm, D), F32)

            acc[rows, :] += product()

            @pl.when(k == nk - 1)
            def _():
                finish(acc[rows, :])

    return pl.pallas_call(
        kern, grid=(nk, M // tm),
        in_specs=[pl.BlockSpec((tm, tk), lambda k, i: (i, k)), b_spec, row, pl.BlockSpec((1, D), lambda k, i: (0, 0)),
                  row] + [pl.BlockSpec(memory_space=pl.ANY)] * len(deps),
        out_specs=(row, pl.BlockSpec((8, D), lambda k, i: (0, 0))),
        out_shape=(jax.ShapeDtypeStruct((M, D), F32), jax.ShapeDtypeStruct((8, D), F32)),
        scratch_shapes=[pltpu.VMEM((M, D) if nk > 1 else (8, 128), F32)], name=name,
        compiler_params=_cp(("arbitrary", "arbitrary")),
    )(a, b, x, w.reshape(1, D), res, *deps)


def _rms_fwd(x, w, name, tl=256, dep=None):
    L = x.shape[0]

    def kern(x_ref, w_ref, *rest):
        o_ref = rest[-1]
        xv = x_ref[...]
        r = lax.rsqrt(jnp.mean(xv * xv, axis=-1, keepdims=True) + EPS)
        o_ref[...] = ((xv * r) * w_ref[...]).astype(BF16)

    row = pl.BlockSpec((tl, D), lambda i: (i, 0))
    deps = [] if dep is None else [dep]
    return pl.pallas_call(
        kern, grid=(L // tl,),
        in_specs=[row, pl.BlockSpec((1, D), lambda i: (0, 0))] + [pl.BlockSpec(memory_space=pl.ANY)] * len(deps),
        out_specs=row, out_shape=jax.ShapeDtypeStruct((L, D), BF16), name=name, compiler_params=_cp(("parallel",)),
    )(x, w.reshape(1, D), *deps)


def _down(v, k):
    if k == 0:
        return v
    t = lax.broadcasted_iota(jnp.int32, v.shape, 0)
    return jnp.where(t >= k, pltpu.roll(v, k, axis=0), 0.0)


def _up(v, k):
    if k == 0:
        return v
    n = v.shape[0]
    t = lax.broadcasted_iota(jnp.int32, v.shape, 0)
    return jnp.where(t < n - k, pltpu.roll(v, n - k, axis=0), 0.0)


TW = 256


def _sc_fwd(proj, cw):
    L = proj.shape[0]
    nb = D // TW

    def kern(b_ref, c_ref, x_ref, w_ref, o_ref):
        u = c_ref[...].astype(F32) * x_ref[...].astype(F32)
        w = w_ref[...]
        cv = w[0:1] * _down(u, 2) + w[1:2] * _down(u, 1) + w[2:3] * u
        o_ref[...] = (b_ref[...].astype(F32) * cv).astype(BF16)

    col = lambda off: pl.BlockSpec((L, TW), lambda j: (0, off + j))
    return pl.pallas_call(
        kern, grid=(nb,), in_specs=[col(0), col(nb), col(2 * nb), pl.BlockSpec((8, TW), lambda j: (0, j))],
        out_specs=pl.BlockSpec((L, TW), lambda j: (0, j)), out_shape=jax.ShapeDtypeStruct((L, D), BF16),
        name="sc_fwd", compiler_params=_cp(("parallel",)),
    )(proj, proj, proj, cw)


def _sc_bwd(dya, proj, cw, dproj):
    L = proj.shape[0]
    nb = D // TW

    def kern(d_ref, b_ref, c_ref, x_ref, w_ref, _, dp_ref, gw_ref, keep):
        sec = pl.program_id(1)

        @pl.when(sec == 0)
        def _():
            cs, xs, dyv = c_ref[...].astype(F32), x_ref[...].astype(F32), d_ref[...]
            w = w_ref[...]
            u = cs * xs
            u1, u2 = _down(u, 1), _down(u, 2)
            cv = w[0:1] * u2 + w[1:2] * u1 + w[2:3] * u
            dcv = dyv * b_ref[...].astype(F32)
            du = w[2:3] * dcv + w[1:2] * _up(dcv, 1) + w[0:1] * _up(dcv, 2)
            g0 = jnp.sum(dcv * u2, axis=0, keepdims=True)
            g1 = jnp.sum(dcv * u1, axis=0, keepdims=True)
            g2 = jnp.sum(dcv * u, axis=0, keepdims=True)
            row = lax.broadcasted_iota(jnp.int32, (8, TW), 0)
            gw_ref[...] = jnp.where(row == 0, g0, jnp.where(row == 1, g1, jnp.where(row == 2, g2, 0.0)))
            dp_ref[...] = (dyv * cv).astype(BF16)
            keep[0] = (du * xs).astype(BF16)
            keep[1] = (du * cs).astype(BF16)

        @pl.when(sec > 0)
        def _():
            dp_ref[...] = keep[sec - 1]

    col = lambda off: pl.BlockSpec((L, TW), lambda j, s: (0, off + j))
    return pl.pallas_call(
        kern, grid=(nb, 3),
        in_specs=[col(0), col(0), col(nb), col(2 * nb), pl.BlockSpec((8, TW), lambda j, s: (0, j)),
                  pl.BlockSpec(memory_space=pl.ANY)],
        out_specs=(pl.BlockSpec((L, TW), lambda j, s: (0, s * nb + j)), pl.BlockSpec((8, TW), lambda j, s: (0, j))),
        out_shape=(jax.ShapeDtypeStruct(dproj.shape, BF16), jax.ShapeDtypeStruct((8, D), F32)),
        scratch_shapes=[pltpu.VMEM((2, L, TW), BF16)],
        input_output_aliases={5: 0}, name="sc_bwd", compiler_params=_cp(("parallel", "arbitrary")),
    )(dya, proj, proj, proj, cw, dproj)


def _ssm_conv_fwd(proj, cw4):
    L = proj.shape[0]
    off = C_XBC // TW

    def kern(r_ref, w_ref, o_ref):
        raw = r_ref[...].astype(F32)
        w = w_ref[...]
        c4 = w[0:1] * _down(raw, 3) + w[1:2] * _down(raw, 2) + w[2:3] * _down(raw, 1) + w[3:4] * raw + w[4:5]
        o_ref[...] = c4 * _sigmoid(c4)

    return pl.pallas_call(
        kern, grid=(XBC // TW,),
        in_specs=[pl.BlockSpec((L, TW), lambda j: (0, off + j)), pl.BlockSpec((8, TW), lambda j: (0, j))],
        out_specs=pl.BlockSpec((L, TW), lambda j: (0, j)), out_shape=jax.ShapeDtypeStruct((L, XBC), F32),
        name="ssm_conv_fwd", compiler_params=_cp(("parallel",)),
    )(proj, cw4)


def _ssm_conv_bwd(dx, proj, cw4, dproj, col0, name):
    L, width = dx.shape
    off_p = (C_XBC + col0) // TW
    off_w = col0 // TW

    def kern(d_ref, r_ref, w_ref, _, dp_ref, gw_ref):
        raw = r_ref[...].astype(F32)
        w = w_ref[...]
        r1, r2, r3 = _down(raw, 1), _down(raw, 2), _down(raw, 3)
        c4 = w[0:1] * r3 + w[1:2] * r2 + w[2:3] * r1 + w[3:4] * raw + w[4:5]
        sg = _sigmoid(c4)
        dc4 = d_ref[...] * (sg * (1.0 + c4 * (1.0 - sg)))
        draw = w[3:4] * dc4 + w[2:3] * _up(dc4, 1) + w[1:2] * _up(dc4, 2) + w[0:1] * _up(dc4, 3)
        dp_ref[...] = draw.astype(BF16)
        gs = [jnp.sum(dc4 * r3, axis=0, keepdims=True), jnp.sum(dc4 * r2, axis=0, keepdims=True),
              jnp.sum(dc4 * r1, axis=0, keepdims=True), jnp.sum(dc4 * raw, axis=0, keepdims=True),
              jnp.sum(dc4, axis=0, keepdims=True)]
        row = lax.broadcasted_iota(jnp.int32, (8, TW), 0)
        acc = jnp.zeros((8, TW), F32)
        for k, gk in enumerate(gs):
            acc = jnp.where(row == k, gk, acc)
        gw_ref[...] = acc

    return pl.pallas_call(
        kern, grid=(width // TW,),
        in_specs=[pl.BlockSpec((L, TW), lambda j: (0, j)), pl.BlockSpec((L, TW), lambda j: (0, off_p + j)),
                  pl.BlockSpec((8, TW), lambda j: (0, off_w + j)), pl.BlockSpec(memory_space=pl.ANY)],
        out_specs=(pl.BlockSpec((L, TW), lambda j: (0, off_p + j)), pl.BlockSpec((8, TW), lambda j: (0, j))),
        out_shape=(jax.ShapeDtypeStruct(dproj.shape, BF16), jax.ShapeDtypeStruct((8, width), F32)),
        input_output_aliases={3: 0}, name=name, compiler_params=_cp(("arbitrary",)),
    )(dx, proj, cw4, dproj)


def _split3(v):
    h1 = v.astype(BF16)
    r1 = v - h1.astype(F32)
    h2 = r1.astype(BF16)
    h3 = (r1 - h2.astype(F32)).astype(BF16)
    return h1, h2, h3


def _dot01(m01, v, dims=_DIMS["nn"], m_left=True, terms=3):
    out = None
    for part in _split3(v)[:terms]:
        ops = (m01, part) if m_left else (part, m01)
        t = lax.dot_general(ops[0], ops[1], dims, preferred_element_type=F32)
        out = t if out is None else out + t
    return out


def _bdot(a, b, mode="nn"):
    return lax.dot_general(a.astype(BF16), b.astype(BF16), _DIMS[mode], preferred_element_type=F32)


def _softplus(v):
    return jnp.maximum(v, 0.0) + jnp.log1p(jnp.exp(-jnp.abs(v)))


def _dt_prep(proj, vec):
    L = proj.shape[0]

    def kern(p_ref, v_ref, dt_ref, cs_ref, sg_ref):
        v = v_ref[...]
        pre = p_ref[:, 0:128] + v[0:1]
        dt = _softplus(pre)
        da = dt * (-jnp.exp(v[1:2]))
        ii = lax.broadcasted_iota(jnp.int32, (Q, Q), 0)
        jj = lax.broadcasted_iota(jnp.int32, (Q, Q), 1)
        ltri = (jj <= ii).astype(BF16)
        lane = lax.broadcasted_iota(jnp.int32, (Q, 128), 1)
        for val, ref in ((dt, dt_ref), (_dot01(ltri, da), cs_ref), (_sigmoid(pre), sg_ref)):
            for g in range(NG):
                moved = val if g == 0 else pltpu.roll(val, 128 - 4 * g, axis=1)
                ref[g] = jnp.where(lane < 4, moved, 0.0)

    blk = pl.BlockSpec((NG, Q, 128), lambda c: (0, c, 0))
    return pl.pallas_call(
        kern, grid=(L // Q,),
        in_specs=[pl.BlockSpec((Q, 256), lambda c: (c, 0)), pl.BlockSpec((8, 128), lambda c: (0, 0))],
        out_specs=(blk, blk, blk),
        out_shape=(jax.ShapeDtypeStruct((NG, L, 128), F32),) * 3,
        name="dt_prep", compiler_params=_cp(("parallel",)),
    )(proj, vec)


def _head_masks():
    lane = lax.broadcasted_iota(jnp.int32, (1, 4 * HD), 1)
    return [((lane >= HD * j) & (lane < HD * (j + 1))) for j in range(4)]


def _expand4(v4, masks):
    R = v4.shape[0]
    out = jnp.zeros((R, 4 * HD), F32)
    for j in range(4):
        out = jnp.where(masks[j], jnp.broadcast_to(v4[:, j:j + 1], (R, 4 * HD)), out)
    return out


def _decay_matrix(cs_col, tri):
    colb = jnp.broadcast_to(cs_col, (Q, Q))
    return jnp.exp(jnp.where(tri, colb - colb.T, -jnp.inf))


def _ssd_fwd(xbc, dt4, cs4, vecg):
    L = xbc.shape[0]
    nc = L // Q

    def kern(x_ref, b_ref, c_ref, dt_ref, cs_ref, v_ref, y_ref, s_ref, S):
        c = pl.program_id(1)

        @pl.when(c == 0)
        def _():
            S[...] = jnp.zeros_like(S)

        masks = _head_masks()
        ii = lax.broadcasted_iota(jnp.int32, (Q, Q), 0)
        jj = lax.broadcasted_iota(jnp.int32, (Q, Q), 1)
        tri = jj <= ii
        for gi in range(GPS):
            xs, ns = slice(256 * gi, 256 * (gi + 1)), slice(NS * gi, NS * (gi + 1))
            dt4v, cs4v = dt_ref[gi], cs_ref[gi]
            dt_b, cs_b = _expand4(dt4v, masks), _expand4(cs4v, masks)
            d_b = _expand4(v_ref[gi], masks)[1:2]
            cs_last = cs_b[Q - 1:Q, :]
            x4, bm, cm = x_ref[:, xs], b_ref[:, ns], c_ref[:, ns]
            xdt = x4 * dt_b
            gm = _bdot(cm, bm, "nt")
            s4 = S[gi]
            s_ref[gi, 0] = s4
            y = _bdot(cm, s4) * jnp.exp(cs_b) + d_b * x4
            m_all = jnp.concatenate([(gm * _decay_matrix(cs4v[:, j:j + 1], tri)).astype(BF16) for j in range(4)], axis=0)
            yd = _bdot(m_all, xdt)
            for j in range(4):
                y = y + jnp.where(masks[j], yd[Q * j:Q * (j + 1)], 0.0)
            y_ref[:, xs] = y
            S[gi] = jnp.exp(cs_last) * s4 + _bdot(bm, xdt * jnp.exp(cs_last - cs_b), "tn")

    sc = pl.BlockSpec((GPS, Q, 128), lambda g, c: (g, c, 0))
    bw = NS * GPS
    return pl.pallas_call(
        kern, grid=(NG // GPS, nc),
        in_specs=[pl.BlockSpec((Q, 256 * GPS), lambda g, c: (c, g)),
                  pl.BlockSpec((Q, bw), lambda g, c: (c, INNER // bw + g)),
                  pl.BlockSpec((Q, bw), lambda g, c: (c, (INNER + NG * NS) // bw + g)),
                  sc, sc, pl.BlockSpec((GPS, 8, 128), lambda g, c: (g, 0, 0))],
        out_specs=(pl.BlockSpec((Q, 256 * GPS), lambda g, c: (c, g)),
                   pl.BlockSpec((GPS, 1, NS, 256), lambda g, c: (g, c, 0, 0))),
        out_shape=(jax.ShapeDtypeStruct((L, INNER), F32), jax.ShapeDtypeStruct((NG, nc, NS, 256), F32)),
        scratch_shapes=[pltpu.VMEM((GPS, NS, 256), F32)], name="ssd_fwd",
        compiler_params=_cp(("parallel", "arbitrary")),
    )(xbc, xbc, xbc, dt4, cs4, vecg)


def _ssd_bwd(xbc, dt4, cs4, sg4, vecg, s_all, dy):
    L = xbc.shape[0]
    nc = L // Q

    def kern(x_ref, b_ref, c_ref, dt_ref, cs_ref, sg_ref, v_ref, s_ref, dy_ref,
             dx_ref, db_ref, dc_ref, ddt_ref, st_ref, dS):
        cc = pl.program_id(1)

        @pl.when(cc == 0)
        def _():
            dS[...] = jnp.zeros_like(dS)
            st_ref[...] = jnp.zeros_like(st_ref)

        masks = _head_masks()
        ii = lax.broadcasted_iota(jnp.int32, (Q, Q), 0)
        jj = lax.broadcasted_iota(jnp.int32, (Q, Q), 1)
        tri = jj <= ii
        utri = (jj >= ii).astype(BF16)
        hsel = ((lax.broadcasted_iota(jnp.int32, (4 * HD, 128), 0) // HD)
                == lax.broadcasted_iota(jnp.int32, (4 * HD, 128), 1)).astype(BF16)
        hrow = ((lax.broadcasted_iota(jnp.int32, (4 * Q, 128), 0) // Q)
                == lax.broadcasted_iota(jnp.int32, (4 * Q, 128), 1)).astype(BF16)
        ones_q = jnp.ones((Q, 128), BF16)
        lane128 = lax.broadcasted_iota(jnp.int32, (Q, 128), 1)

        for gi in range(GPS):
            xs, ns = slice(256 * gi, 256 * (gi + 1)), slice(NS * gi, NS * (gi + 1))
            dt4v, cs4v, sg4v = dt_ref[gi], cs_ref[gi], sg_ref[gi]
            dt_b, cs_b = _expand4(dt4v, masks), _expand4(cs4v, masks)
            vv = _expand4(v_ref[gi], masks)
            a_b = -jnp.exp(vv[0:1])
            d_b = vv[1:2]
            a4 = -jnp.exp(v_ref[gi][0:1, :])
            cs_last = cs_b[Q - 1:Q, :]
            ecs = jnp.exp(cs_b)
            decay = jnp.exp(cs_last - cs_b)
            elast = jnp.exp(cs_last)
            x4, bm, cm, dyv = x_ref[:, xs], b_ref[:, ns], c_ref[:, ns], dy_ref[:, xs]
            s4 = s_ref[gi, 0]
            dsn = dS[gi]
            xdt = x4 * dt_b
            gm = _bdot(cm, bm, "nt")
            dye = dyv * ecs
            yoff = ecs * _bdot(cm, s4)
            t4 = _bdot(bm, dsn) * decay
            lms, mhs = [], []
            for j in range(4):
                colb = jnp.broadcast_to(cs4v[:, j:j + 1], (Q, Q))
                lms.append(jnp.exp(jnp.where(tri, colb - colb.T, -jnp.inf)))
                mhs.append(gm * lms[j])
            m_all = jnp.concatenate([m.astype(BF16) for m in mhs], axis=0)
            dy_m = jnp.concatenate([jnp.where(masks[j], dyv, 0.0).astype(BF16) for j in range(4)], axis=0)
            dxdt = t4 + _bdot(m_all, dy_m, "tn")
            dm_all = _bdot(dy_m, xdt, "nt")
            dg = jnp.zeros((Q, Q), F32)
            for j in range(4):
                dg = dg + dm_all[Q * j:Q * (j + 1)] * lms[j]
            e_all = dm_all * jnp.concatenate(mhs, axis=0)
            rsum = _dot01(ones_q, e_all, m_left=False, terms=2)
            da4 = -_dot01(hrow, e_all, _DIMS["tn"], m_left=False, terms=2)
            for j in range(4):
                da4 = da4 + jnp.where(lane128 == j, rsum[Q * j:Q * (j + 1)], 0.0)
            xt = xdt * t4
            tail = jnp.sum(xt, axis=0, keepdims=True) + elast * jnp.sum(s4 * dsn, axis=0, keepdims=True)
            gd_raw = jnp.sum(dyv * x4, axis=0, keepdims=True)
            stacked = jnp.concatenate([dyv * yoff - xt, dxdt * x4, jnp.broadcast_to(tail, (8, 4 * HD)),
                                       jnp.broadcast_to(gd_raw, (8, 4 * HD))], axis=0)
            seg = _dot01(hsel, stacked, m_left=False, terms=2)
            dda4 = _dot01(utri, da4 + seg[0:Q], terms=2) + seg[2 * Q:2 * Q + 1]
            ddt_ref[gi] = (dda4 * a4 + seg[Q:2 * Q]) * sg4v
            ga = jnp.sum(dda4 * dt4v * a4, axis=0, keepdims=True)
            row = lax.broadcasted_iota(jnp.int32, (8, 128), 0)
            st_ref[gi] += jnp.where(row == 0, ga, jnp.where(row == 1, seg[2 * Q + 8:2 * Q + 9], 0.0))
            dx_ref[:, xs] = d_b * dyv + dxdt * dt_b
            dc_ref[:, ns] = _bdot(dg, bm) + _bdot(dye, s4, "nt")
            db_ref[:, ns] = _bdot(dg, cm, "tn") + _bdot(xdt * decay, dsn, "nt")
            dS[gi] = elast * dsn + _bdot(cm, dye, "tn")

    rv = lambda c: nc - 1 - c
    sc = pl.BlockSpec((GPS, Q, 128), lambda g, c: (g, rv(c), 0))
    bw = NS * GPS
    return pl.pallas_call(
        kern, grid=(NG // GPS, nc),
        in_specs=[pl.BlockSpec((Q, 256 * GPS), lambda g, c: (rv(c), g)),
                  pl.BlockSpec((Q, bw), lambda g, c: (rv(c), INNER // bw + g)),
                  pl.BlockSpec((Q, bw), lambda g, c: (rv(c), (INNER + NG * NS) // bw + g)),
                  sc, sc, sc, pl.BlockSpec((GPS, 8, 128), lambda g, c: (g, 0, 0)),
                  pl.BlockSpec((GPS, 1, NS, 256), lambda g, c: (g, rv(c), 0, 0)),
                  pl.BlockSpec((Q, 256 * GPS), lambda g, c: (rv(c), g))],
        out_specs=(pl.BlockSpec((Q, 256 * GPS), lambda g, c: (rv(c), g)),
                   pl.BlockSpec((Q, bw), lambda g, c: (rv(c), g)),
                   pl.BlockSpec((Q, bw), lambda g, c: (rv(c), g)),
                   pl.BlockSpec((GPS, Q, 128), lambda g, c: (g, rv(c), 0)),
                   pl.BlockSpec((GPS, 8, 128), lambda g, c: (g, 0, 0))),
        out_shape=(jax.ShapeDtypeStruct((L, INNER), F32), jax.ShapeDtypeStruct((L, NG * NS), F32),
                   jax.ShapeDtypeStruct((L, NG * NS), F32), jax.ShapeDtypeStruct((NG, L, 128), F32),
                   jax.ShapeDtypeStruct((NG, 8, 128), F32)),
        scratch_shapes=[pltpu.VMEM((GPS, NS, 256), F32)], name="ssd_bwd",
        compiler_params=_cp(("parallel", "arbitrary")),
    )(xbc, xbc, xbc, dt4, cs4, sg4, vecg, s_all, dy)


def _dt_bwd(ddt, dproj, tl=256):
    L = ddt.shape[1]

    def kern(d_ref, _, dp_ref, gs_ref):
        @pl.when(pl.program_id(0) == 0)
        def _():
            gs_ref[...] = jnp.zeros_like(gs_ref)

        d = d_ref[0]
        for g in range(1, NG):
            d = d + pltpu.roll(d_ref[g], 4 * g, axis=1)
        gs_ref[...] += jnp.broadcast_to(jnp.sum(d, axis=0, keepdims=True), (8, 128))
        dp_ref[...] = jnp.concatenate([d, jnp.zeros_like(d)], axis=1).astype(BF16)

    return pl.pallas_call(
        kern, grid=(L // tl,),
        in_specs=[pl.BlockSpec((NG, tl, 128), lambda i: (0, i, 0)), pl.BlockSpec(memory_space=pl.ANY)],
        out_specs=(pl.BlockSpec((tl, 256), lambda i: (i, C_DT // 256)), pl.BlockSpec((8, 128), lambda i: (0, 0))),
        out_shape=(jax.ShapeDtypeStruct(dproj.shape, BF16), jax.ShapeDtypeStruct((8, 128), F32)),
        input_output_aliases={1: 0}, name="dt_bwd", compiler_params=_cp(("arbitrary",)),
    )(ddt, dproj)


GW = INNER // NG


def _gnorm_fwd(y, proj, w, tl=256):
    L = y.shape[0]
    zoff = C_Z // 1024

    def kern(y_ref, z_ref, w_ref, o_ref):
        z = z_ref[...].astype(F32)
        yz = y_ref[...] * (z * _sigmoid(z))
        wv = w_ref[...]
        for k in range(1024 // GW):
            sl = slice(GW * k, GW * (k + 1))
            v = yz[:, sl]
            rg = lax.rsqrt(jnp.mean(v * v, axis=-1, keepdims=True) + EPS)
            o_ref[:, sl] = ((v * rg) * wv[:, sl]).astype(BF16)

    blk = pl.BlockSpec((tl, 1024), lambda i, j: (i, j))
    return pl.pallas_call(
        kern, grid=(L // tl, 2),
        in_specs=[blk, pl.BlockSpec((tl, 1024), lambda i, j: (i, zoff + j)), pl.BlockSpec((1, 1024), lambda i, j: (0, j))],
        out_specs=blk, out_shape=jax.ShapeDtypeStruct((L, INNER), BF16), name="gnorm_fwd",
        compiler_params=_cp(("parallel", "parallel")),
    )(y, proj, w.reshape(1, INNER))


def _gnorm_bwd(dbr, wb, y, proj, w, dproj, dep, tl=512):
    L = y.shape[0]
    tl = min(tl, L)
    zoff = C_Z // 1024

    def kern(d_ref, b_ref, y_ref, z_ref, w_ref, _, __, dy_ref, dp_ref, gw_ref):
        @pl.when(pl.program_id(1) == 0)
        def _():
            gw_ref[...] = jnp.zeros_like(gw_ref)

        z = z_ref[...].astype(F32)
        sg = _sigmoid(z)
        sz = z * sg
        yv = y_ref[...]
        yz = yv * sz
        dv = lax.dot_general(d_ref[0], b_ref[...], _DIMS["nt"], preferred_element_type=F32)
        wv = w_ref[...]
        for k in range(1024 // GW):
            sl = slice(GW * k, GW * (k + 1))
            v = yz[:, sl]
            rg = lax.rsqrt(jnp.mean(v * v, axis=-1, keepdims=True) + EPS)
            vn = v * rg
            dk = dv[:, sl]
            gw_ref[:, sl] += jnp.broadcast_to(jnp.sum(dk * vn, axis=0, keepdims=True), (8, GW))
            dvn = dk * wv[:, sl]
            dyz = rg * (dvn - vn * jnp.mean(dvn * vn, axis=-1, keepdims=True))
            dy_ref[:, sl] = dyz * sz[:, sl]
            dp_ref[:, sl] = (dyz * yv[:, sl] * (sg[:, sl] * (1.0 + z[:, sl] * (1.0 - sg[:, sl])))).astype(BF16)

    blk = pl.BlockSpec((tl, 1024), lambda j, i: (i, j))
    zblk = pl.BlockSpec((tl, 1024), lambda j, i: (i, zoff + j))
    return pl.pallas_call(
        kern, grid=(2, L // tl),
        in_specs=[pl.BlockSpec((1, tl, D), lambda j, i: (1, i, 0)), pl.BlockSpec((1024, D), lambda j, i: (j, 0)),
                  blk, zblk, pl.BlockSpec((1, 1024), lambda j, i: (0, j)), pl.BlockSpec(memory_space=pl.ANY),
                  pl.BlockSpec(memory_space=pl.ANY)],
        out_specs=(blk, zblk, pl.BlockSpec((8, 1024), lambda j, i: (0, j))),
        out_shape=(jax.ShapeDtypeStruct((L, INNER), F32), jax.ShapeDtypeStruct(dproj.shape, BF16),
                   jax.ShapeDtypeStruct((8, INNER), F32)),
        input_output_aliases={5: 1}, name="gnorm_bwd", compiler_params=_cp(("parallel", "arbitrary")),
    )(dbr, wb, y, proj, w.reshape(1, INNER), dproj, dep)


def _merge_fwd(proj, bg, br_a, br_b, tl=256):
    L = proj.shape[0]
    goff = C_GATE // 1024

    def kern(g1_ref, g2_ref, b1_ref, b2_ref, a_ref, b_ref, o_ref):
        g1 = _sigmoid(g1_ref[...].astype(F32) + b1_ref[...])
        g2 = _sigmoid(g2_ref[...].astype(F32) + b2_ref[...])
        o_ref[...] = (g1 * a_ref[...] + g2 * b_ref[...]).astype(BF16)

    row = pl.BlockSpec((tl, 1024), lambda i: (i, 0))
    bg2 = bg.reshape(1, 2 * D)
    return pl.pallas_call(
        kern, grid=(L // tl,),
        in_specs=[pl.BlockSpec((tl, 1024), lambda i: (i, goff)), pl.BlockSpec((tl, 1024), lambda i: (i, goff + 1)),
                  pl.BlockSpec((1, 1024), lambda i: (0, 0)), pl.BlockSpec((1, 1024), lambda i: (0, 1)), row, row],
        out_specs=row, out_shape=jax.ShapeDtypeStruct((L, D), BF16), name="merge_fwd",
        compiler_params=_cp(("parallel",)),
    )(proj, proj, bg2, bg2, br_a, br_b)


def _branch_ssm_merge(yb, wb, proj, bg, br_a, tm=512):
    L, K = yb.shape
    tm = min(tm, L)
    goff = C_GATE // 1024

    def kern(a_ref, b_ref, g1_ref, g2_ref, b1_ref, b2_ref, bra_ref, brb_ref, m_ref):
        brb = lax.dot_general(a_ref[...], b_ref[...], _DIMS["nn"], preferred_element_type=F32)
        brb_ref[...] = brb
        g1 = _sigmoid(g1_ref[...].astype(F32) + b1_ref[...])
        g2 = _sigmoid(g2_ref[...].astype(F32) + b2_ref[...])
        m_ref[...] = (g1 * bra_ref[...] + g2 * brb).astype(BF16)

    row = pl.BlockSpec((tm, D), lambda i: (i, 0))
    bg2 = bg.reshape(1, 2 * D)
    return pl.pallas_call(
        kern, grid=(L // tm,),
        in_specs=[pl.BlockSpec((tm, K), lambda i: (i, 0)), pl.BlockSpec((K, D), lambda i: (0, 0)),
                  pl.BlockSpec((tm, D), lambda i: (i, goff)), pl.BlockSpec((tm, D), lambda i: (i, goff + 1)),
                  pl.BlockSpec((1, D), lambda i: (0, 0)), pl.BlockSpec((1, D), lambda i: (0, 1)), row],
        out_specs=(row, row), out_shape=(jax.ShapeDtypeStruct((L, D), F32), jax.ShapeDtypeStruct((L, D), BF16)),
        name="branch_ssm_merge", compiler_params=_cp(("parallel",)),
    )(yb, wb, proj, proj, bg2, bg2, br_a)


def _merge_bwd(dx1, wo, proj, bg, br_a, br_b, dproj, tl=512):
    L = proj.shape[0]
    tl = min(tl, L)
    goff = C_GATE // 1024

    def kern(dm_ref, wo_ref, g_ref, b_ref, a_ref, bb_ref, _, dbr_ref, dp_ref, gb_ref):
        j = pl.program_id(0)

        @pl.when(pl.program_id(1) == 0)
        def _():
            gb_ref[...] = jnp.zeros_like(gb_ref)

        g = _sigmoid(g_ref[...].astype(F32) + b_ref[...])
        br = jnp.where(j == 0, a_ref[...], bb_ref[...])
        dmv = lax.dot_general(dm_ref[...].astype(BF16), wo_ref[...], _DIMS["nt"], preferred_element_type=F32)
        dbr_ref[0] = (dmv * g).astype(BF16)
        dgate = dmv * br * g * (1.0 - g)
        gb_ref[...] += jnp.broadcast_to(jnp.sum(dgate, axis=0, keepdims=True), (8, 1024))
        dp_ref[...] = dgate.astype(BF16)

    row = pl.BlockSpec((tl, 1024), lambda j, i: (i, 0))
    gblk = pl.BlockSpec((tl, 1024), lambda j, i: (i, goff + j))
    return pl.pallas_call(
        kern, grid=(2, L // tl),
        in_specs=[row, pl.BlockSpec((D, D), lambda j, i: (0, 0)), gblk, pl.BlockSpec((1, 1024), lambda j, i: (0, j)),
                  row, row, pl.BlockSpec(memory_space=pl.ANY)],
        out_specs=(pl.BlockSpec((1, tl, 1024), lambda j, i: (j, i, 0)), gblk, pl.BlockSpec((8, 1024), lambda j, i: (0, j))),
        out_shape=(jax.ShapeDtypeStruct((2, L, D), BF16), jax.ShapeDtypeStruct(dproj.shape, BF16),
                   jax.ShapeDtypeStruct((8, 2 * D), F32)),
        input_output_aliases={6: 1}, name="merge_bwd", compiler_params=_cp(("parallel", "arbitrary")),
    )(dx1, wo, proj, bg.reshape(1, 2 * D), br_a, br_b, dproj)


def _coords():
    return lax.axis_index("x"), lax.axis_index("y"), lax.axis_index("c")


def _other_chips(sk):
    xk, yk = sk // 2, sk % 2
    return [((1 - xk, yk), 2 * (1 - xk) + yk), ((xk, 1 - yk), 2 * xk + 1 - yk), ((1 - xk, 1 - yk), 2 * (1 - xk) + 1 - yk)]


def _rows(start, size):
    assert size % 128 == 0
    return pl.ds(pl.multiple_of(start, 128), size)


def _per_chip(fn):
    x, y, _ = _coords()
    s = 2 * x + y
    for sk in range(4):
        pl.when(s == sk)(functools.partial(fn, sk))


XTRA = PIECE - PMAIN


def _place(shard, full_shape, block, index_map, idx, name, blk0=0, nblk=None, dep=None, into=None):
    in_block = block[-2:]
    if nblk is None:
        nblk = shard.shape[0] // in_block[0]

    def kern(idx_ref, s_ref, *rest):
        o_ref = rest[-1]
        o_ref[...] = s_ref[...].astype(BF16).reshape(o_ref.shape)

    extra = ([dep] if dep is not None else []) + ([into] if into is not None else [])
    grid_spec = pltpu.PrefetchScalarGridSpec(
        num_scalar_prefetch=1, grid=(nblk,),
        in_specs=[pl.BlockSpec(in_block, lambda i, idx_ref: (blk0 + i, 0))] + [_ANY] * len(extra),
        out_specs=pl.BlockSpec(block, index_map))
    aliases = {1 + len(extra): 0} if into is not None else {}
    return pl.pallas_call(kern, grid_spec=grid_spec, out_shape=jax.ShapeDtypeStruct(full_shape, BF16), name=name,
                          input_output_aliases=aliases, compiler_params=_cp(("arbitrary",)))(idx, shard, *extra)


_SEM = pl.BlockSpec(memory_space=pltpu.SEMAPHORE)
_EFFECT = pltpu.SideEffectType.DATAFLOW_SIDE_EFFECTING


_ANY = pl.BlockSpec(memory_space=pl.ANY)


def _tie(v, dep, name):
    def body(v_ref, dep_ref, o_ref):
        del v_ref, dep_ref, o_ref

    return pl.pallas_call(body, out_shape=jax.ShapeDtypeStruct(v.shape, v.dtype), in_specs=[_ANY, _ANY],
                          out_specs=_ANY, input_output_aliases={0: 0}, name=name)(v, dep)


def _split_call(name, arrays, start=None, wait=None, wait_sems=None, after=None):
    keys = list(arrays)
    n = len(keys)
    n_start = start.n if start is not None else 0
    afters = [] if after is None else (list(after) if isinstance(after, (list, tuple)) else [after])

    def body(*refs):
        pos = n
        if wait is not None:
            wss, wrs = refs[pos], refs[pos + 1]
            pos += 2
        pos += len(afters)
        if start is not None:
            nss, nrs = refs[pos], refs[pos + 1]
            pos += 2
        R = dict(zip(keys, refs[pos:pos + n]))
        token = refs[pos + n]
        x, y, c = _coords()

        def desc(src, dst, dev, ss, rs, k):
            return pltpu.make_async_remote_copy(src_ref=src, dst_ref=dst, send_sem=ss.at[k], recv_sem=rs.at[k],
                                                device_id=dev, device_id_type=MESH)

        def run(sk):
            if wait is not None:
                for k, (snd, land) in enumerate(wait.copies(sk, R)):
                    if snd is not None:
                        desc(snd[0], snd[1], snd[2], wss, wrs, k).wait_send()
                    if land is not None:
                        desc(land, land, (x, y, c), wss, wrs, k).wait_recv()
            if start is not None:
                for k, (snd, land) in enumerate(start.copies(sk, R)):
                    if snd is not None:
                        desc(snd[0], snd[1], snd[2], nss, nrs, k).start()

        _per_chip(run)
        token[...] = jnp.zeros_like(token)

    hbm = pl.BlockSpec(memory_space=HBM)
    vals = [arrays[k] for k in keys]
    ins, in_specs = list(vals), [hbm] * n
    if wait is not None:
        ins += list(wait_sems)
        in_specs += [_SEM, _SEM]
    ins += afters
    in_specs += [pl.BlockSpec(memory_space=pl.ANY)] * len(afters)
    out_shape, out_specs = [], []
    if start is not None:
        out_shape += [pltpu.SemaphoreType.DMA((n_start,)), pltpu.SemaphoreType.DMA((n_start,))]
        out_specs += [_SEM, _SEM]
    first = len(out_shape)
    out_shape += [jax.ShapeDtypeStruct(v.shape, v.dtype) for v in vals] + [jax.ShapeDtypeStruct((8, 128), F32)]
    out_specs += [hbm] * n + [pl.BlockSpec(memory_space=pltpu.VMEM)]
    res = pl.pallas_call(
        body, out_shape=tuple(out_shape), in_specs=in_specs, out_specs=tuple(out_specs),
        input_output_aliases={i: first + i for i in range(n)}, name=name,
        compiler_params=pltpu.CompilerParams(has_side_effects=_EFFECT),
    )(*ins)
    sems = (res[0], res[1]) if start is not None else None
    return dict(zip(keys, res[first:first + n])), sems, res[-1]


class _Plan:
    def __init__(self, n, copies):
        self.n, self.copies = n, copies


_HM, _HX = PMAIN // 2, XTRA // 2
WAVE0 = 768
WAVES = ((0, WAVE0), (WAVE0, _HM - WAVE0))
_WIN = {
    "wq0": (True, "wct", lambda r, sc, hc: r.at[_rows(PMAIN * sc + _HM * hc + WAVES[0][0], WAVES[0][1]), :]),
    "wq1": (True, "wct", lambda r, sc, hc: r.at[_rows(PMAIN * sc + _HM * hc + WAVES[1][0], WAVES[1][1]), :]),
    "xt": (True, "xt", lambda r, sc, hc: r.at[sc, _rows(_HX * hc, _HX), :]),
    "w1": (True, "w1", lambda r, sc, hc: r.at[_rows(512 * hc, 512), pl.ds(1024 * sc, 1024)]),
    "w2": (True, "w2", lambda r, sc, hc: r.at[_rows(1024 * sc + 512 * hc, 512), :]),
    "wa": (True, "wa", lambda r, sc, hc: r.at[_rows(256 * sc + 128 * hc, 128), :]),
    "wb": (True, "wb", lambda r, sc, hc: r.at[_rows(512 * sc + 256 * hc, 256), :]),
    "wo": (True, "wo", lambda r, sc, hc: r.at[_rows(256 * sc + 128 * hc, 128), :]),
    "cw": (False, "cw", lambda r, sc, hc: r.at[sc]),
}


_PIECE_SRC = {
    "wq0": lambda p, hc: p.at[_rows(_HM * hc + WAVES[0][0], WAVES[0][1]), :],
    "wq1": lambda p, hc: p.at[_rows(_HM * hc + WAVES[1][0], WAVES[1][1]), :],
    "xt": lambda p, hc: p.at[_rows(PMAIN + _HX * hc, _HX), :],
}


def _ag_chips_plan(keys):
    def copies(sk, R):
        _, _, c = _coords()
        out = []
        for key in keys:
            _, arr, win = _WIN[key]
            for (px, py), ps in _other_chips(sk):
                dst = win(R[arr], sk, c)
                src = _PIECE_SRC[key](R["piece"], c) if key in _PIECE_SRC else dst
                out.append(((src, dst, (px, py, c)), win(R[arr], ps, c)))
        return out
    return _Plan(3 * len(keys), copies)


def _ag_sibling_plan(keys):
    keys = [k for k in keys if _WIN[k][0]]

    def copies(sk, R):
        x, y, c = _coords()
        out = []
        for key in keys:
            _, arr, win = _WIN[key]
            for _, ps in _other_chips(sk):
                w = win(R[arr], ps, c)
                out.append(((w, w, (x, y, 1 - c)), win(R[arr], ps, 1 - c)))
        return out
    return _Plan(3 * len(keys), copies)


def _in_proj_wave(h, wct, wave, proj=None, tm=2048):
    L = h.shape[0]
    tm = min(tm, L)
    off, size = WAVES[wave]
    start = lambda j: pl.multiple_of(_HM * j + off, 128)

    def kern(h_ref, w_ref, *rest):
        o_ref = rest[-1]
        o_ref[...] = lax.dot_general(h_ref[...], w_ref[...], _DIMS["nt"], preferred_element_type=F32).astype(BF16)

    in_specs = [pl.BlockSpec((tm, D), lambda j, i: (i, 0)),
                pl.BlockSpec((pl.Element(size), pl.Element(D)), lambda j, i: (start(j), 0))]
    args, aliases = [h, wct], {}
    if proj is not None:
        in_specs.append(pl.BlockSpec(memory_space=pl.ANY))
        args.append(proj)
        aliases = {2: 0}
    return pl.pallas_call(
        kern, grid=(8, L // tm), in_specs=in_specs,
        out_specs=pl.BlockSpec((pl.Element(tm), pl.Element(size)), lambda j, i: (i * tm, start(j))),
        out_shape=jax.ShapeDtypeStruct((L, NCW), BF16), input_output_aliases=aliases,
        name="in_proj_wave%d" % wave, compiler_params=_cp(("parallel", "parallel")),
    )(*args)


def _fix_wct(wct, xt):
    nb = PMAIN // XTRA

    def kern(w_ref, x_ref, o_ref):
        k = pl.program_id(0)
        xv = x_ref[0]
        o_ref[...] = jnp.where(k < 3, (w_ref[...].astype(F32) + xv.astype(F32)).astype(BF16), xv)

    blk = pl.BlockSpec((XTRA, D), lambda k: (nb * (k + 1), 0))
    rblk = pl.BlockSpec((XTRA, D), lambda k: (jnp.where(k < 3, nb * (k + 1), 0), 0))
    return pl.pallas_call(
        kern, grid=(4,), in_specs=[rblk, pl.BlockSpec((1, XTRA, D), lambda k: (k, 0, 0))], out_specs=blk,
        out_shape=jax.ShapeDtypeStruct(wct.shape, BF16), input_output_aliases={0: 0}, name="fix_wct",
        compiler_params=_cp(("arbitrary",)),
    )(wct, xt)


_HP = PIECE // 2
_GWIN = [
    lambda r, sc, hc: r.at[_rows(PMAIN * sc + _HP * hc, _HP), :],
    lambda r, sc, hc: r.at[_rows(512 * hc, 512), pl.ds(1024 * sc, 1024)],
    lambda r, sc, hc: r.at[_rows(1024 * sc + 512 * hc, 512), :],
    lambda r, sc, hc: r.at[_rows(256 * sc + 128 * hc, 128), :],
    lambda r, sc, hc: r.at[_rows(512 * sc + 256 * hc, 256), :],
    lambda r, sc, hc: r.at[_rows(256 * sc + 128 * hc, 128), :],
]
HALF_SHAPES = [(PIECE // 2, D), (512, 1024), (512, 1024), (128, 1024), (256, 1024), (128, 1024)]


def _rs_sibling_plan(ts):
    def copies(sk, R):
        x, y, c = _coords()
        out = []
        for t in ts:
            for sc in range(4):
                land = R["ra%d" % t].at[sc]
                out.append(((_GWIN[t](R["g%d" % t], sc, 1 - c), land, (x, y, 1 - c)), land))
        return out
    return _Plan(4 * len(ts), copies)


def _rs_chips_plan(ts):
    def copies(sk, R):
        _, _, c = _coords()
        out = []
        for t in ts:
            for j, ((px, py), ps) in enumerate(_other_chips(sk)):
                land = R["rb%d" % t].at[j]
                out.append(((R["hb%d" % t].at[ps], land, (px, py, c)), land))
        return out
    return _Plan(3 * len(ts), copies)


def _rs_share_plan(ts):
    def copies(sk, R):
        x, y, c = _coords()
        out = []
        for t in ts:
            rows = HALF_SHAPES[t][0]
            mine = R["f%d" % t].at[_rows(rows * c, rows), :]
            out.append(((mine, mine, (x, y, 1 - c)), R["f%d" % t].at[_rows(rows * (1 - c), rows), :]))
        return out
    return _Plan(len(ts), copies)


def _half_tiling(t):
    rows, cols = HALF_SHAPES[t]
    if t == 0:
        return (rows // 2, cols), 2, lambda i: (i, 0)
    return (rows, cols), 1, lambda i: (0, 0)


def _window_spec(t, blk):
    if t == 0:
        return pl.BlockSpec((pl.Element(blk[0]), pl.Element(blk[1])), lambda i, sc, idx_ref: (
            pl.multiple_of(PMAIN * sc + _HP * idx_ref[1] + blk[0] * i, 128), 0))
    if t == 1:
        return pl.BlockSpec(blk, lambda i, sc, idx_ref: (idx_ref[1], sc))
    return pl.BlockSpec(blk, lambda i, sc, idx_ref: (2 * sc + idx_ref[1], 0))


def _chip_sum(g, ra, t, idx, name):
    rows, cols = HALF_SHAPES[t]
    blk, nblk, inner = _half_tiling(t)

    def kern(idx_ref, g_ref, r_ref, hb_ref, hf_ref):
        v = g_ref[...].astype(F32) + r_ref[0].astype(F32)
        hb_ref[0] = v.astype(BF16)

        @pl.when(pl.program_id(1) == idx_ref[0])
        def _():
            hf_ref[...] = v

    omap = lambda i, sc, idx_ref: (sc,) + inner(i)
    grid_spec = pltpu.PrefetchScalarGridSpec(
        num_scalar_prefetch=1, grid=(nblk, 4),
        in_specs=[_window_spec(t, blk), pl.BlockSpec((1,) + blk, omap)],
        out_specs=(pl.BlockSpec((1,) + blk, omap), pl.BlockSpec(blk, lambda i, sc, idx_ref: inner(i))))
    return pl.pallas_call(
        kern, grid_spec=grid_spec,
        out_shape=(jax.ShapeDtypeStruct((4, rows, cols), BF16), jax.ShapeDtypeStruct((rows, cols), F32)),
        name=name, compiler_params=_cp(("parallel", "arbitrary")),
    )(idx, g, ra)


def _chip_sum_part(g, ra, t, idx, name, own, dep=None):
    rows, cols = HALF_SHAPES[t]
    blk, nblk, inner = _half_tiling(t)
    chip = (lambda k, idx_ref: idx_ref[0]) if own else (lambda k, idx_ref: lax.rem(idx_ref[0] + 1 + k, 4))
    win = _window_spec(t, blk)
    deps = [] if dep is None else [dep]

    def kern(idx_ref, g_ref, r_ref, *rest):
        v = g_ref[...].astype(F32) + r_ref[0].astype(F32)
        if own:
            rest[-1][...] = v
        else:
            rest[-1][0] = v.astype(BF16)

    omap = lambda i, k, idx_ref: (chip(k, idx_ref),) + inner(i)
    grid_spec = pltpu.PrefetchScalarGridSpec(
        num_scalar_prefetch=1, grid=(nblk, 1 if own else 3),
        in_specs=[pl.BlockSpec(win.block_shape, lambda i, k, idx_ref: win.index_map(i, chip(k, idx_ref), idx_ref)),
                  pl.BlockSpec((1,) + blk, omap)] + [pl.BlockSpec(memory_space=pl.ANY)] * len(deps),
        out_specs=pl.BlockSpec(blk, lambda i, k, idx_ref: inner(i)) if own else pl.BlockSpec((1,) + blk, omap))
    return pl.pallas_call(
        kern, grid_spec=grid_spec,
        out_shape=jax.ShapeDtypeStruct((rows, cols), F32) if own else jax.ShapeDtypeStruct((4, rows, cols), BF16),
        name=name, compiler_params=_cp(("parallel", "arbitrary")),
    )(idx, g, ra, *deps)


def _final_sum(hf, rb, t, idx, name):
    rows, cols = HALF_SHAPES[t]
    blk, nblk, inner = _half_tiling(t)
    nbr = rows // blk[0]

    def kern(idx_ref, h_ref, r_ref, o_ref):
        o_ref[...] = ((h_ref[...] + r_ref[0].astype(F32)) + r_ref[1].astype(F32)) + r_ref[2].astype(F32)

    def omap(i, idx_ref):
        r, cidx = inner(i)
        return nbr * idx_ref[1] + r, cidx

    grid_spec = pltpu.PrefetchScalarGridSpec(
        num_scalar_prefetch=1, grid=(nblk,),
        in_specs=[pl.BlockSpec(blk, lambda i, idx_ref: inner(i)),
                  pl.BlockSpec((3,) + blk, lambda i, idx_ref: (0,) + inner(i))],
        out_specs=pl.BlockSpec(blk, omap))
    return pl.pallas_call(
        kern, grid_spec=grid_spec, out_shape=jax.ShapeDtypeStruct((2 * rows, cols), F32),
        name=name, compiler_params=_cp(("parallel",)),
    )(idx, hf, rb)


class _ReduceScatter:
    def __init__(self, ts, grads, idx, tag):
        self.ts, self.idx, self.tag = ts, idx, tag
        arr = {}
        for t in ts:
            arr["g%d" % t] = grads[t]
            arr["ra%d" % t] = lax.empty((4,) + HALF_SHAPES[t], BF16)
        self.plan = _rs_sibling_plan(ts)
        self.arr, self.sems, self.token = _split_call("rs_sibling_start_" + tag, arr, start=self.plan)

    def chips(self, after, own_later=False):
        arr, _, _ = _split_call("rs_sibling_wait_" + self.tag, self.arr, wait=self.plan, wait_sems=self.sems, after=after)
        brr, self.hf = {}, {}
        for t in self.ts:
            if own_later:
                hb = _chip_sum_part(arr["g%d" % t], arr["ra%d" % t], t, self.idx, "chip_sum_others_%d" % t, False)
            else:
                hb, self.hf[t] = _chip_sum(arr["g%d" % t], arr["ra%d" % t], t, self.idx, "chip_sum_%d" % t)
            brr["hb%d" % t] = hb
            brr["rb%d" % t] = lax.empty((3,) + HALF_SHAPES[t], BF16)
        self.plan = _rs_chips_plan(self.ts)
        self.arr, self.sems, self.token = _split_call("rs_chips_start_" + self.tag, brr, start=self.plan)
        if own_later:
            for t in self.ts:
                self.hf[t] = _chip_sum_part(arr["g%d" % t], arr["ra%d" % t], t, self.idx, "chip_sum_own_%d" % t, True,
                                            dep=self.token)
        return self.token

    def share(self, after):
        brr, _, _ = _split_call("rs_chips_wait_" + self.tag, self.arr, wait=self.plan, wait_sems=self.sems, after=after)
        frr = {"f%d" % t: _final_sum(self.hf[t], brr["rb%d" % t], t, self.idx, "final_sum_%d" % t) for t in self.ts}
        self.plan = _rs_share_plan(self.ts)
        self.arr, self.sems, self.token = _split_call("rs_share_start_" + self.tag, frr, start=self.plan)
        return self.token

    def result(self, after):
        frr, _, _ = _split_call("rs_share_wait_" + self.tag, self.arr, wait=self.plan, wait_sems=self.sems, after=after)
        return {t: frr["f%d" % t] for t in self.ts}


def _all8_plan(key):
    def copies(sk, R):
        x, y, c = _coords()
        own = R[key].at[4 * x + 2 * y + c]
        out = []
        for k in range(1, 8):
            dev = ((1 - x) if (k >> 2) & 1 else x, (1 - y) if (k >> 1) & 1 else y, (1 - c) if k & 1 else c)
            out.append(((own, own, dev), R[key].at[4 * dev[0] + 2 * dev[1] + dev[2]]))
        return out
    return _Plan(7, copies)


def _sum8(v, name="small_sum"):
    def kern(v_ref, o_ref):
        acc = v_ref[0]
        for k in range(1, 8):
            acc = acc + v_ref[k]
        o_ref[...] = acc

    return pl.pallas_call(kern, out_shape=jax.ShapeDtypeStruct(v.shape[1:], F32), name=name)(v)


def _adamw(w, g, m, v, name, tr=128, blk0=0, nblk=None, into=None, copy_g=False):
    R, C = w.shape
    tr = min(tr, R)
    if nblk is None:
        assert R % tr == 0 and blk0 == 0
        nblk = R // tr
    n_out = 4 if copy_g else 3

    def kern(*refs):
        w_ref, g_ref, m_ref, v_ref = refs[:4]
        d_ref, mo_ref, vo_ref = refs[-n_out:][:3]
        gv = g_ref[...]
        mn = ADAM_B1 * m_ref[...] + (1.0 - ADAM_B1) * gv
        vn = ADAM_B2 * v_ref[...] + (1.0 - ADAM_B2) * (gv * gv)
        m_hat = mn / (1.0 - ADAM_B1 ** ADAM_STEP)
        v_hat = vn / (1.0 - ADAM_B2 ** ADAM_STEP)
        d_ref[...] = -ADAM_LR * (m_hat / (jnp.sqrt(v_hat) + ADAM_EPS) + ADAM_WD * w_ref[...])
        mo_ref[...] = mn
        vo_ref[...] = vn
        if copy_g:
            refs[-1][...] = gv

    blk = pl.BlockSpec((tr, C), lambda i: (blk0 + i, 0))
    sd = jax.ShapeDtypeStruct((R, C), F32)
    in_specs, args, aliases = [blk] * 4, [w, g, m, v], {}
    if into is not None:
        in_specs += [pl.BlockSpec(memory_space=pl.ANY)] * 3
        args += list(into)
        aliases = {4: 0, 5: 1, 6: 2}
    return pl.pallas_call(kern, grid=(nblk,), in_specs=in_specs, out_specs=(blk,) * n_out, out_shape=(sd,) * n_out,
                          input_output_aliases=aliases, name=name, compiler_params=_cp(("parallel",)))(*args)


def _adamw_w_in(wt, gp, mt, vt, offs, name, r0, tr, nblk, views, into=None, blk_key=None):
    el = lambda n: (pl.Element(n), pl.Element(D))
    first = (lambda o: r0) if blk_key is None else (lambda o: tr * o[blk_key])
    own = pl.BlockSpec(el(tr), lambda i, o: (pl.multiple_of(first(o) + tr * i, 8), 0))

    def view(k):
        return pl.BlockSpec(el(tr), lambda i, o: (pl.multiple_of(jnp.maximum(first(o) + tr * i + o[k], 0), 8), 0))

    def kern(o_ref, w_ref, m_ref, v_ref, *refs):
        g_refs, (d_ref, mo_ref, vo_ref, go_ref) = refs[:len(views)], refs[-4:]
        gv = g_refs[0][...]
        if len(views) == 2:
            row = first(o_ref) + tr * pl.program_id(0) + lax.broadcasted_iota(jnp.int32, (tr, D), 0)
            gv = jnp.where(row < o_ref[2], gv, g_refs[1][...])
        mn = ADAM_B1 * m_ref[...] + (1.0 - ADAM_B1) * gv
        vn = ADAM_B2 * v_ref[...] + (1.0 - ADAM_B2) * (gv * gv)
        m_hat = mn / (1.0 - ADAM_B1 ** ADAM_STEP)
        v_hat = vn / (1.0 - ADAM_B2 ** ADAM_STEP)
        d_ref[...] = -ADAM_LR * (m_hat / (jnp.sqrt(v_hat) + ADAM_EPS) + ADAM_WD * w_ref[...])
        mo_ref[...] = mn
        vo_ref[...] = vn
        go_ref[...] = gv

    in_specs = [own, own, own] + [view(k) for k in views]
    args = [wt, mt, vt] + [gp] * len(views)
    aliases = {}
    if into is not None:
        in_specs += [pl.BlockSpec(memory_space=pl.ANY)] * 4
        args += list(into)
        aliases = {1 + len(args) - 4 + j: j for j in range(4)}
    grid_spec = pltpu.PrefetchScalarGridSpec(num_scalar_prefetch=1, grid=(nblk,), in_specs=in_specs,
                                             out_specs=(own,) * 4)
    sd = jax.ShapeDtypeStruct(wt.shape, F32)
    return pl.pallas_call(kern, grid_spec=grid_spec, out_shape=(sd,) * 4, input_output_aliases=aliases, name=name,
                          compiler_params=_cp(("parallel",)))(offs, *args)


def _to_piece(wt, s):
    z = lambda n: jnp.zeros((n, D), wt.dtype)
    pads = [functools.partial(lambda k, w: jnp.pad(w, ((8 * k, PIECE - W_SHARD - 8 * k), (0, 0))).astype(BF16), k)
            for k in range(3)]
    last = lambda w: jnp.concatenate([z(24), w[:744], w[776:], w[744:776], z(PIECE - 24 - W_SHARD)], axis=0).astype(BF16)
    return lax.switch(s, pads + [last], wt)


_SMALL = [("b_gate", 2048), ("ssm_conv_b", 4096), ("dt_bias", 32), ("A_log", 32), ("D_skip", 32),
          ("ssm_norm_w", 2048), ("norm_mlp", 1024), ("norm_final", 1024), ("sc_conv_w", 3072), ("ssm_conv_w", 16384),
          ("loss", 1)]


def _pack(vals, table, rows):
    parts = []
    for name, n in table:
        v = vals[name].reshape(-1).astype(F32)
        pad = (-n) % 128
        parts.append(jnp.pad(v, (0, pad)) if pad else v)
    flat = jnp.concatenate(parts)
    return jnp.pad(flat, (0, rows * 128 - flat.shape[0])).reshape(rows, 128)


def _unpack(arr, table):
    flat = arr.reshape(-1)
    out, off = {}, 0
    for name, n in table:
        out[name] = flat[off:off + n]
        off += n + ((-n) % 128)
    return out


def kernel(x, norm_mix, w_in, b_gate, sc_conv_w, ssm_conv_w, ssm_conv_b, dt_bias, A_log, D_skip, ssm_norm_w, w_branch_sc, w_branch_ssm, w_out, norm_mlp, w_mlp1, w_mlp2, norm_final, loss_target, m_norm_mix, m_w_in, m_b_gate, m_sc_conv_w, m_ssm_conv_w, m_ssm_conv_b, m_dt_bias, m_A_log, m_D_skip, m_ssm_norm_w, m_w_branch_sc, m_w_branch_ssm, m_w_out, m_norm_mlp, m_w_mlp1, m_w_mlp2, m_norm_final, v_norm_mix, v_w_in, v_b_gate, v_sc_conv_w, v_ssm_conv_w, v_ssm_conv_b, v_dt_bias, v_A_log, v_D_skip, v_ssm_norm_w, v_w_branch_sc, v_w_branch_ssm, v_w_out, v_norm_mlp, v_w_mlp1, v_w_mlp2, v_norm_final):
    L = x.shape[1]
    nc = L // Q
    xi, yi, ci = lax.axis_index("x"), lax.axis_index("y"), lax.axis_index("c")
    s = 2 * xi + yi
    idx = jnp.stack([s, ci]).astype(jnp.int32)
    x0 = x.reshape(L, D)
    tgt = loss_target.reshape(L, D)
    small_names = ["b_gate", "sc_conv_w", "ssm_conv_w", "ssm_conv_b", "dt_bias", "A_log", "D_skip", "ssm_norm_w",
                   "norm_mlp", "norm_final"]
    small_wmv = [dict(zip(small_names, vals)) for vals in (
        (b_gate, sc_conv_w, ssm_conv_w, ssm_conv_b, dt_bias, A_log, D_skip, ssm_norm_w, norm_mlp, norm_final),
        (m_b_gate, m_sc_conv_w, m_ssm_conv_w, m_ssm_conv_b, m_dt_bias, m_A_log, m_D_skip, m_ssm_norm_w, m_norm_mlp,
         m_norm_final),
        (v_b_gate, v_sc_conv_w, v_ssm_conv_w, v_ssm_conv_b, v_dt_bias, v_A_log, v_D_skip, v_ssm_norm_w, v_norm_mlp,
         v_norm_final))]
    small_table = [(n, int(small_wmv[0][n].size)) for n in small_names]
    small_rows = 136
    pk_w, pk_m, pk_v = [_pack(d, small_table, small_rows) for d in small_wmv]

    piece = _to_piece(w_in.T, s)
    nb = PMAIN // XTRA
    cws = jnp.zeros((8, 1280), F32)
    cws = cws.at[0:3, 0:256].set(sc_conv_w).at[0:4, 256:1280].set(ssm_conv_w)
    cw0 = lax.dynamic_update_slice(jnp.zeros((4, 8, 1280), F32), cws[None], (s, 0, 0))
    win_keys, win2_keys, mid_keys, end_keys = ["xt", "cw", "wq0"], ["wq1"], ["wa", "wb", "wo", "w1"], ["w2"]
    gw, sems_w, tok = _split_call(
        "ag_win_start", {"wct": lax.empty((NCW, D), BF16), "xt": lax.empty((4, XTRA, D), BF16), "cw": cw0, "piece": piece},
        start=_ag_chips_plan(win_keys))
    g2, sems_w2, tok = _split_call("ag_win2_start", {"wct": gw["wct"], "piece": gw["piece"]},
                                   start=_ag_chips_plan(win2_keys), after=tok)
    piece = g2["piece"]
    gw["wct"] = _place(piece, (NCW, D), (XTRA, D), lambda i, r: (nb * r[0] + i, 0), idx, "place_wct", nblk=nb,
                       dep=tok, into=g2["wct"])
    gw["xt"] = _place(piece, (4, XTRA, D), (1, XTRA, D), lambda i, r: (r[0], 0, 0), idx, "place_xt", blk0=nb, nblk=1,
                      dep=tok, into=gw["xt"])
    gw["piece"] = piece
    wa0 = _place(w_branch_sc, (D, D), (256, 1024), lambda i, r: (r[0], 0), idx, "place_wa", dep=tok)
    wb0 = _place(w_branch_ssm, (INNER, D), (512, 1024), lambda i, r: (r[0], 0), idx, "place_wb", dep=tok)
    wo0 = _place(w_out, (D, D), (256, 1024), lambda i, r: (r[0], 0), idx, "place_wo", dep=tok)
    w10 = _place(w_mlp1, (D, DFF), (256, 1024), lambda i, r: (i, r[0]), idx, "place_w1", dep=tok)
    gm, sems_m, tok = _split_call("ag_mid_start", {"wa": wa0, "wb": wb0, "wo": wo0, "w1": w10},
                                  start=_ag_chips_plan(mid_keys))
    w20 = _place(w_mlp2, (DFF, D), (256, 1024), lambda i, r: (4 * r[0] + i, 0), idx, "place_w2", dep=tok)
    ge, sems_e, tok = _split_call("ag_end_start", {"w2": w20}, start=_ag_chips_plan(end_keys))
    h = _rms_fwd(x0, norm_mix, "rms_mix", dep=tok)
    gw, sems_w, tok = _split_call("ag_win_pass", gw, wait=_ag_chips_plan(win_keys), wait_sems=sems_w,
                                  start=_ag_sibling_plan(win_keys), after=[h, pk_w, pk_m, pk_v])
    gw, _, _ = _split_call("ag_win_done", gw, wait=_ag_sibling_plan(win_keys), wait_sems=sems_w, after=tok)
    wc, cw_all = _fix_wct(gw["wct"], gw["xt"]), gw["cw"]
    sc_w_full = jnp.concatenate([cw_all[k, :, 0:256] for k in range(4)], axis=1)
    ssm_w_full = jnp.concatenate([cw_all[k, :, 256:1280] for k in range(4)], axis=1)
    cw4 = ssm_w_full.at[4].set(ssm_conv_b)
    vec = jnp.zeros((8, 128), F32).at[0, :NH].set(dt_bias).at[1, :NH].set(A_log)
    vecg = jnp.zeros((NG, 8, 128), F32).at[:, 0, :4].set(A_log.reshape(NG, 4)).at[:, 1, :4].set(D_skip.reshape(NG, 4))

    dtraw = _matmul(h, wc[C_DT:], "nt", F32, 512, 256, 1024, "in_proj_dt")
    proj = _in_proj_wave(h, wc, 0)
    g2, sems_w2, tok = _split_call("ag_win2_pass", {"wct": wc, "piece": gw["piece"]},
                                   wait=_ag_chips_plan(win2_keys), wait_sems=sems_w2,
                                   start=_ag_sibling_plan(win2_keys), after=[proj, dtraw])
    g2, _, _ = _split_call("ag_win2_done", g2, wait=_ag_sibling_plan(win2_keys), wait_sems=sems_w2, after=tok)
    wc = g2["wct"]
    proj = _in_proj_wave(h, wc, 1, proj=proj)
    ya = _sc_fwd(proj, sc_w_full)
    xbc = _ssm_conv_fwd(proj, cw4)
    dt4, cs4, sg4 = _dt_prep(dtraw, vec)
    y, s_all = _ssd_fwd(xbc, dt4, cs4, vecg)
    gm, sems_m, tok = _split_call("ag_mid_pass", gm, wait=_ag_chips_plan(mid_keys), wait_sems=sems_m,
                                  start=_ag_sibling_plan(mid_keys), after=[y, ya])
    y = _tie(y, tok, "tie_y")
    yb = _gnorm_fwd(y, proj, ssm_norm_w)
    gm, _, _ = _split_call("ag_mid_done", gm, wait=_ag_sibling_plan(mid_keys), wait_sems=sems_m, after=yb)
    wa, wb, wo, w1 = gm["wa"], gm["wb"], gm["wo"], gm["w1"]
    ge, sems_e, tok = _split_call("ag_end_pass", ge, wait=_ag_chips_plan(end_keys), wait_sems=sems_e,
                                  start=_ag_sibling_plan(end_keys), after=yb)
    br_a = _matmul(ya, wa, "nn", F32, 1024, 1024, 1024, "branch_sc", dep=tok)
    br_b, merged = _branch_ssm_merge(yb, wb, proj, b_gate, br_a)
    x1, h2 = _matmul_res_rms(merged, wo, x0, norm_mlp, 1024, "out_proj")
    a1, rl = _matmul(h2, w1, "nn", BF16, 1024, 1024, 1024, "mlp1", epi="relu2", n_outer=True)
    ge, _, _ = _split_call("ag_end_done", ge, wait=_ag_sibling_plan(end_keys), wait_sems=sems_e, after=a1)
    w2 = ge["w2"]
    dx2, g_nf, loss8 = _matmul_res_final(rl, w2, x1, norm_final, tgt, 512, "mlp2")

    da = _matmul(dx2, w2, "nt", BF16, 1024, 1024, 1024, "mlp2_dx", epi="drelu", extra=a1, n_outer=True)
    g_w2 = _matmul(rl, dx2, "tn", BF16, 1024, 1024, 2048, "mlp2_dw")
    g_w1 = _matmul(h2, da, "tn", BF16, 1024, 1024, 2048, "mlp1_dw")
    dx1, g_nmlp = _matmul_rms_bwd(da, w1, "nt", x1, norm_mlp, dx2, 512, 4096, "mlp1_dx")
    g_wo = _matmul(merged, dx1, "tn", BF16, 1024, 1024, 2048, "out_proj_dw")
    dproj = lax.empty((L, NCW), BF16)
    dbr, dproj, g_bg = _merge_bwd(dx1, wo, proj, b_gate, br_a, br_b, dproj)
    dya = _matmul(dbr[0], wa, "nt", F32, 1024, 1024, 1024, "branch_sc_dx")
    g_wa = _matmul(ya, dbr[0], "tn", BF16, 1024, 1024, 2048, "branch_sc_dw")
    dproj, g_scw = _sc_bwd(dya, proj, sc_w_full, dproj)
    g_wb = _matmul(yb, dbr[1], "tn", BF16, 1024, 1024, 2048, "branch_ssm_dw")
    rs_a = _ReduceScatter([1, 2, 3, 4, 5], {1: g_w1, 2: g_w2, 3: g_wa, 4: g_wb, 5: g_wo}, idx, "a")
    dy, dproj, g_snw = _gnorm_bwd(dbr, wb, y, proj, ssm_norm_w, dproj, rs_a.token)
    tok = rs_a.chips(after=dy)
    dxs, dbm, dcm, ddt_g, st = _ssd_bwd(xbc, dt4, cs4, sg4, vecg, s_all, _tie(dy, tok, "tie_dy"))
    dproj, gx1 = _ssm_conv_bwd(dxs, proj, cw4, dproj, 0, "ssm_conv_bwd_x")
    dproj, gx2 = _ssm_conv_bwd(dbm, proj, cw4, dproj, INNER, "ssm_conv_bwd_b")
    dproj, gx3 = _ssm_conv_bwd(dcm, proj, cw4, dproj, INNER + NG * NS, "ssm_conv_bwd_c")
    g_cw4 = jnp.concatenate([gx1, gx2, gx3], axis=1)
    dproj, g_dtb = _dt_bwd(ddt_g, dproj)
    small = {"b_gate": g_bg[0], "ssm_conv_b": g_cw4[4], "dt_bias": g_dtb[0, :NH],
             "A_log": st[:, 0, :4], "D_skip": st[:, 1, :4], "ssm_norm_w": g_snw[0], "norm_mlp": g_nmlp[0],
             "norm_final": g_nf[0], "sc_conv_w": g_scw[0:3], "ssm_conv_w": g_cw4[0:4], "loss": loss8[0, 0:1]}
    me = 4 * xi + 2 * yi + ci
    sm8 = lax.dynamic_update_slice(jnp.zeros((8, SMALL_ROWS, 128), F32), _pack(small, _SMALL, SMALL_ROWS)[None], (me, 0, 0))
    sm_arr, sm_sems, tok = _split_call("small_start", {"sm": sm8}, start=_all8_plan("sm"))
    g_wc = _matmul(dproj, h, "tn", BF16, 1280, 1024, 2048, "in_proj_dw", dep=tok)
    rs_b = _ReduceScatter([0], {0: g_wc}, idx, "b")
    tok = rs_a.share(after=rs_b.token)
    tok = rs_b.chips(after=tok, own_later=True)
    grad_x, g_nm = _matmul_rms_bwd(dproj, wc, "nn", x0, norm_mix, dx1, 512, 3840, "in_proj_dx", dep=rs_b.hf[0])
    nm8 = lax.dynamic_update_slice(jnp.zeros((8, 8, 128), F32), g_nm[0].reshape(1, 8, 128), (me, 0, 0))
    nm_arr, nm_sems, tok = _split_call("norm_mix_start", {"nm": nm8}, start=_all8_plan("nm"))
    sm_arr, _, _ = _split_call("small_wait", sm_arr, wait=_all8_plan("sm"), wait_sems=sm_sems, after=tok)
    small_sum = _sum8(sm_arr["sm"])
    gs = _unpack(small_sum, _SMALL)
    red = rs_a.result(after=tok)
    big = {"w_mlp1": red[1], "w_mlp2": red[2], "w_branch_sc": red[3], "w_branch_ssm": red[4], "w_out": red[5]}

    given = dict(norm_mix=norm_mix, w_in=w_in, b_gate=b_gate, sc_conv_w=sc_conv_w, ssm_conv_w=ssm_conv_w, ssm_conv_b=ssm_conv_b, dt_bias=dt_bias, A_log=A_log, D_skip=D_skip, ssm_norm_w=ssm_norm_w, w_branch_sc=w_branch_sc, w_branch_ssm=w_branch_ssm, w_out=w_out, norm_mlp=norm_mlp, w_mlp1=w_mlp1, w_mlp2=w_mlp2, norm_final=norm_final,
                 m_norm_mix=m_norm_mix, m_w_in=m_w_in, m_b_gate=m_b_gate, m_sc_conv_w=m_sc_conv_w, m_ssm_conv_w=m_ssm_conv_w, m_ssm_conv_b=m_ssm_conv_b, m_dt_bias=m_dt_bias, m_A_log=m_A_log, m_D_skip=m_D_skip, m_ssm_norm_w=m_ssm_norm_w, m_w_branch_sc=m_w_branch_sc, m_w_branch_ssm=m_w_branch_ssm, m_w_out=m_w_out, m_norm_mlp=m_norm_mlp, m_w_mlp1=m_w_mlp1, m_w_mlp2=m_w_mlp2, m_norm_final=m_norm_final,
                 v_norm_mix=v_norm_mix, v_w_in=v_w_in, v_b_gate=v_b_gate, v_sc_conv_w=v_sc_conv_w, v_ssm_conv_w=v_ssm_conv_w, v_ssm_conv_b=v_ssm_conv_b, v_dt_bias=v_dt_bias, v_A_log=v_A_log, v_D_skip=v_D_skip, v_ssm_norm_w=v_ssm_norm_w, v_w_branch_sc=v_w_branch_sc, v_w_branch_ssm=v_w_branch_ssm, v_w_out=v_w_out, v_norm_mlp=v_norm_mlp, v_w_mlp1=v_w_mlp1, v_w_mlp2=v_w_mlp2, v_norm_final=v_norm_final)
    order = ["norm_mix", "w_in", "b_gate", "sc_conv_w", "ssm_conv_w", "ssm_conv_b", "dt_bias", "A_log", "D_skip",
             "ssm_norm_w", "w_branch_sc", "w_branch_ssm", "w_out", "norm_mlp", "w_mlp1", "w_mlp2", "norm_final"]
    grad, delta, new_m, new_v = {}, {}, {}, {}
    for n in big:
        delta[n], new_m[n], new_v[n], grad[n] = _adamw(given[n], big[n], given["m_" + n], given["v_" + n],
                                                       "adamw_" + n, copy_g=True)
    big["w_in"] = None
    grad_small = {n: gs[n].reshape(given[n].shape) for n in small_names if n not in ("sc_conv_w", "ssm_conv_w")}
    grad_small["sc_conv_w"] = lax.dynamic_slice(gs["sc_conv_w"].reshape(3, D), (0, 256 * s), (3, 256))
    grad_small["ssm_conv_w"] = lax.dynamic_slice(gs["ssm_conv_w"].reshape(4, XBC), (0, 1024 * s), (4, 1024))
    table = small_table
    ds_, ms_, vs_ = _adamw(pk_w, _pack(grad_small, table, small_rows), pk_m, pk_v, "adamw_small", tr=small_rows)
    ds_, ms_, vs_ = _unpack(ds_, table), _unpack(ms_, table), _unpack(vs_, table)
    for n in grad_small:
        shp = given[n].shape
        grad[n] = grad_small[n]
        delta[n], new_m[n], new_v[n] = ds_[n].reshape(shp), ms_[n].reshape(shp), vs_[n].reshape(shp)

    done = [new_v[n] for n in ("w_mlp1", "w_mlp2", "w_branch_sc", "w_branch_ssm", "w_out")] + [vs_["b_gate"]]
    tok = rs_b.share(after=done)
    offs = jnp.where(s == 3, jnp.array([24, -8, 744, 2072, -8], jnp.int32),
                     jnp.stack([8 * s, 8 * s, 0 * s, 8 * s, 8 * s]).astype(jnp.int32))
    offs = jnp.concatenate([offs, jnp.stack([7 * ci, 4 - 4 * ci]).astype(jnp.int32)])
    nmain = W_SHARD // 256
    wt_own = (w_in.T, rs_b.arr["f0"], m_w_in.T, v_w_in.T, offs)
    res = _adamw_w_in(*wt_own, "adamw_w_in_own", 0, 256, 4, (0, 1), blk_key=5)
    gp = rs_b.result(after=[tok, res[0]])[0]
    wt_args = (w_in.T, gp, m_w_in.T, v_w_in.T, offs)
    res = _adamw_w_in(*wt_args, "adamw_w_in", 0, 256, nmain - 4, (0, 1), into=res, blk_key=6)
    res = _adamw_w_in(*wt_args, "adamw_w_in_dt", 744, 32, 1, (3,), into=res)
    dt_, mt_, vt_, gwt = _adamw_w_in(*wt_args, "adamw_w_in_tail", 256 * nmain, 8, 1, (4,), into=res)
    grad["w_in"], delta["w_in"], new_m["w_in"], new_v["w_in"] = gwt.T, dt_.T, mt_.T, vt_.T
    nm_arr, _, _ = _split_call("norm_mix_wait", nm_arr, wait=_all8_plan("nm"), wait_sems=nm_sems, after=tok)
    g8 = _sum8(nm_arr["nm"], "norm_mix_sum")
    r8 = lambda a: a.reshape(8, 128)
    d8, m8, v8 = _adamw(r8(norm_mix), g8, r8(m_norm_mix), r8(v_norm_mix), "adamw_norm_mix", tr=8)
    grad["norm_mix"], delta["norm_mix"] = g8.reshape(D), d8.reshape(D)
    new_m["norm_mix"], new_v["norm_mix"] = m8.reshape(D), v8.reshape(D)

    loss = gs["loss"].reshape(())
    return (loss, grad_x.reshape(1, L, D), *[grad[n] for n in order], *[delta[n] for n in order],
            *[new_m[n] for n in order], *[new_v[n] for n in order])
```

```python
import functools

import jax
import jax.numpy as jnp
from jax import lax
from jax.experimental import pallas as pl
from jax.experimental.pallas import tpu as pltpu

F32 = jnp.float32
BF16 = jnp.bfloat16
MESH = pl.DeviceIdType.MESH
HBM = pltpu.HBM

D = 1024
INNER = 2048
HD = 64
NH = 32
NG = 8
NS = 128
Q = 128
GPS = 8
XBC = 4096
DFF = 4096
EPS = 1e-6
W_SHARD = 2824
NCW = 11520
PIECE = 3072
PMAIN = 2816
C_Z, C_XBC, C_GATE, C_DT = 3072, 5120, 9216, 11264
SMALL_ROWS = 256
VMEM_LIMIT = 56 * 1024 * 1024

ADAM_LR, ADAM_B1, ADAM_B2, ADAM_EPS, ADAM_WD, ADAM_STEP = 0.001, 0.9, 0.999, 1e-08, 0.01, 10


def _cp(sem=None, vmem=VMEM_LIMIT):
    return pltpu.CompilerParams(dimension_semantics=sem, vmem_limit_bytes=vmem)


def _sigmoid(v):
    return 1.0 / (1.0 + jnp.exp(-v))


_DIMS = {"nn": (((1,), (0,)), ((), ())), "nt": (((1,), (1,)), ((), ())), "tn": (((0,), (0,)), ((), ()))}


def _matmul(a, b, mode, out_dtype, tm, tn, tk, name, epi=None, extra=None, n_outer=False, dep=None):
    if mode == "tn":
        K, M = a.shape
    else:
        M, K = a.shape
    N = b.shape[0] if mode == "nt" else b.shape[1]
    tm, tn, tk = min(tm, M), min(tn, N), min(tk, K)
    assert M % tm == 0 and N % tn == 0 and K % tk == 0, (name, M, N, K, tm, tn, tk)
    nm, nn, nk = M // tm, N // tn, K // tk
    dims = _DIMS[mode]

    def ij(p0, p1):
        return (p1, p0) if n_outer else (p0, p1)

    if mode == "tn":
        a_spec = pl.BlockSpec((tk, tm), lambda p0, p1, k: (k, ij(p0, p1)[0]))
    else:
        a_spec = pl.BlockSpec((tm, tk), lambda p0, p1, k: (ij(p0, p1)[0], k))
    if mode == "nt":
        b_spec = pl.BlockSpec((tn, tk), lambda p0, p1, k: (ij(p0, p1)[1], k))
    else:
        b_spec = pl.BlockSpec((tk, tn), lambda p0, p1, k: (k, ij(p0, p1)[1]))
    o_spec = pl.BlockSpec((tm, tn), lambda p0, p1, k: ij(p0, p1))
    in_specs = [a_spec, b_spec]
    args = [a, b]
    if epi in ("res", "drelu"):
        in_specs.append(o_spec)
        args.append(extra)
    if dep is not None:
        in_specs.append(pl.BlockSpec(memory_space=pl.ANY))
        args.append(dep)
    n_in = len(args)
    if epi == "relu2":
        out_shape = (jax.ShapeDtypeStruct((M, N), out_dtype), jax.ShapeDtypeStruct((M, N), BF16))
        out_specs = (o_spec, o_spec)
    else:
        out_shape = jax.ShapeDtypeStruct((M, N), out_dtype)
        out_specs = o_spec

    def kern(*refs):
        a_ref, b_ref = refs[0], refs[1]
        e_ref = refs[2] if epi in ("res", "drelu") else None
        acc = refs[-1]
        outs = refs[n_in:-1] if nk > 1 else refs[n_in:]
        k = pl.program_id(2)

        def product():
            return lax.dot_general(a_ref[...].astype(BF16), b_ref[...].astype(BF16), dims, preferred_element_type=F32)

        def finish(r, cols=slice(None)):
            if epi is None:
                outs[0][:, cols] = r.astype(out_dtype)
            elif epi == "res":
                outs[0][:, cols] = (r + e_ref[:, cols]).astype(out_dtype)
            elif epi == "relu2":
                outs[0][:, cols] = r.astype(out_dtype)
                t = jnp.maximum(r, 0.0)
                outs[1][:, cols] = (t * t).astype(BF16)
            else:
                outs[0][:, cols] = (r * (2.0 * jnp.maximum(e_ref[:, cols].astype(F32), 0.0))).astype(out_dtype)

        if nk == 1:
            ch = 256 if tn % 256 == 0 else tn
            av = a_ref[...].astype(BF16)
            for c0 in range(0, tn, ch):
                cols = slice(c0, c0 + ch)
                bv = b_ref[cols, :] if mode == "nt" else b_ref[:, cols]
                finish(lax.dot_general(av, bv.astype(BF16), dims, preferred_element_type=F32), cols)
        else:
            @pl.when(k == 0)
            def _():
                acc[...] = jnp.zeros_like(acc)

            acc[...] += product()

            @pl.when(k == nk - 1)
            def _():
                finish(acc[...])

    grid = (nn, nm, nk) if n_outer else (nm, nn, nk)
    return pl.pallas_call(
        kern, grid=grid, in_specs=in_specs, out_specs=out_specs, out_shape=out_shape,
        scratch_shapes=[pltpu.VMEM((tm, tn), F32)] if nk > 1 else [], name=name,
        compiler_params=_cp(("parallel", "parallel", "arbitrary")),
    )(*args)


def _matmul_res_rms(a, b, x, w, tm, name):
    M, K = a.shape
    tm = min(tm, M)

    def kern(a_ref, b_ref, x_ref, w_ref, x1_ref, h_ref):
        x1 = x_ref[...] + lax.dot_general(a_ref[...].astype(BF16), b_ref[...].astype(BF16), _DIMS["nn"],
                                          preferred_element_type=F32)
        x1_ref[...] = x1
        r = lax.rsqrt(jnp.mean(x1 * x1, axis=-1, keepdims=True) + EPS)
        h_ref[...] = ((x1 * r) * w_ref[...]).astype(BF16)

    row = pl.BlockSpec((tm, D), lambda i: (i, 0))
    return pl.pallas_call(
        kern, grid=(M // tm,),
        in_specs=[pl.BlockSpec((tm, K), lambda i: (i, 0)), pl.BlockSpec((K, D), lambda i: (0, 0)), row,
                  pl.BlockSpec((1, D), lambda i: (0, 0))],
        out_specs=(row, row), out_shape=(jax.ShapeDtypeStruct((M, D), F32), jax.ShapeDtypeStruct((M, D), BF16)),
        name=name, compiler_params=_cp(("parallel",)),
    )(a, b, x, w.reshape(1, D))


def _matmul_res_final(a, b, x, w, tgt, tm, name):
    M, K = a.shape
    tm = min(tm, M)

    def kern(a_ref, b_ref, x_ref, w_ref, t_ref, dx_ref, gw_ref, loss_ref):
        @pl.when(pl.program_id(0) == 0)
        def _():
            gw_ref[...] = jnp.zeros_like(gw_ref)
            loss_ref[...] = jnp.zeros_like(loss_ref)

        xv = x_ref[...] + lax.dot_general(a_ref[...].astype(BF16), b_ref[...].astype(BF16), _DIMS["nn"],
                                          preferred_element_type=F32)
        r = lax.rsqrt(jnp.mean(xv * xv, axis=-1, keepdims=True) + EPS)
        xn = xv * r
        e = xn * w_ref[...] - t_ref[...]
        loss_ref[...] += 0.5 * jnp.sum(jnp.mean(e * e, axis=-1, keepdims=True))
        dyv = e * (1.0 / D)
        gw_ref[...] += jnp.broadcast_to(jnp.sum(dyv * xn, axis=0, keepdims=True), (8, D))
        dxn = dyv * w_ref[...]
        dx_ref[...] = r * (dxn - xn * jnp.mean(dxn * xn, axis=-1, keepdims=True))

    row = pl.BlockSpec((tm, D), lambda i: (i, 0))
    return pl.pallas_call(
        kern, grid=(M // tm,),
        in_specs=[pl.BlockSpec((tm, K), lambda i: (i, 0)), pl.BlockSpec((K, D), lambda i: (0, 0)), row,
                  pl.BlockSpec((1, D), lambda i: (0, 0)), row],
        out_specs=(row, pl.BlockSpec((8, D), lambda i: (0, 0)), pl.BlockSpec((8, 128), lambda i: (0, 0))),
        out_shape=(jax.ShapeDtypeStruct((M, D), F32), jax.ShapeDtypeStruct((8, D), F32),
                   jax.ShapeDtypeStruct((8, 128), F32)),
        name=name, compiler_params=_cp(("arbitrary",)),
    )(a, b, x, w.reshape(1, D), tgt)


def _matmul_rms_bwd(a, b, mode, x, w, res, tm, tk, name, dep=None):
    M, K = a.shape
    tm, tk = min(tm, M), min(tk, K)
    nk = K // tk
    assert M % tm == 0 and K % tk == 0
    b_spec = (pl.BlockSpec((tk, D), lambda k, i: (k, 0)) if mode == "nn" else pl.BlockSpec((D, tk), lambda k, i: (0, k)))
    row = pl.BlockSpec((tm, D), lambda k, i: (jnp.where(k == nk - 1, i, 0), 0))
    deps = [] if dep is None else [dep]

    def kern(a_ref, b_ref, x_ref, w_ref, res_ref, *rest):
        dx_ref, gw_ref, acc = rest[-3:]
        k, i = pl.program_id(0), pl.program_id(1)

        @pl.when((i == 0) & (k == 0))
        def _():
            gw_ref[...] = jnp.zeros_like(gw_ref)

        def product():
            return lax.dot_general(a_ref[...].astype(BF16), b_ref[...].astype(BF16), _DIMS[mode],
                                   preferred_element_type=F32)

        def finish(dyv):
            xv = x_ref[...]
            r = lax.rsqrt(jnp.mean(xv * xv, axis=-1, keepdims=True) + EPS)
            xn = xv * r
            gw_ref[...] += jnp.broadcast_to(jnp.sum(dyv * xn, axis=0, keepdims=True), (8, D))
            dxn = dyv * w_ref[...]
            dx_ref[...] = res_ref[...] + r * (dxn - xn * jnp.mean(dxn * xn, axis=-1, keepdims=True))

        if nk == 1:
            finish(product())
        else:
            rows = pl.ds(pl.multiple_of(i * tm, tm), tm)

            @pl.when(k == 0)
            def _():
                acc[rows, :] = jnp.zeros((tm, D), F32)

            acc[rows, :] += product()

            @pl.when(k == nk - 1)
            def _():
                finish(acc[rows, :])

    return pl.pallas_call(
        kern, grid=(nk, M // tm),
        in_specs=[pl.BlockSpec((tm, tk), lambda k, i: (i, k)), b_spec, row, pl.BlockSpec((1, D), lambda k, i: (0, 0)),
                  row] + [pl.BlockSpec(memory_space=pl.ANY)] * len(deps),
        out_specs=(row, pl.BlockSpec((8, D), lambda k, i: (0, 0))),
        out_shape=(jax.ShapeDtypeStruct((M, D), F32), jax.ShapeDtypeStruct((8, D), F32)),
        scratch_shapes=[pltpu.VMEM((M, D) if nk > 1 else (8, 128), F32)], name=name,
        compiler_params=_cp(("arbitrary", "arbitrary")),
    )(a, b, x, w.reshape(1, D), res, *deps)


def _rms_fwd(x, w, name, tl=256, dep=None):
    L = x.shape[0]

    def kern(x_ref, w_ref, *rest):
        o_ref = rest[-1]
        xv = x_ref[...]
        r = lax.rsqrt(jnp.mean(xv * xv, axis=-1, keepdims=True) + EPS)
        o_ref[...] = ((xv * r) * w_ref[...]).astype(BF16)

    row = pl.BlockSpec((tl, D), lambda i: (i, 0))
    deps = [] if dep is None else [dep]
    return pl.pallas_call(
        kern, grid=(L // tl,),
        in_specs=[row, pl.BlockSpec((1, D), lambda i: (0, 0))] + [pl.BlockSpec(memory_space=pl.ANY)] * len(deps),
        out_specs=row, out_shape=jax.ShapeDtypeStruct((L, D), BF16), name=name, compiler_params=_cp(("parallel",)),
    )(x, w.reshape(1, D), *deps)


def _down(v, k):
    if k == 0:
        return v
    t = lax.broadcasted_iota(jnp.int32, v.shape, 0)
    return jnp.where(t >= k, pltpu.roll(v, k, axis=0), 0.0)


def _up(v, k):
    if k == 0:
        return v
    n = v.shape[0]
    t = lax.broadcasted_iota(jnp.int32, v.shape, 0)
    return jnp.where(t < n - k, pltpu.roll(v, n - k, axis=0), 0.0)


TW = 256


def _sc_fwd(proj, cw):
    L = proj.shape[0]
    nb = D // TW

    def kern(b_ref, c_ref, x_ref, w_ref, o_ref):
        u = c_ref[...].astype(F32) * x_ref[...].astype(F32)
        w = w_ref[...]
        cv = w[0:1] * _down(u, 2) + w[1:2] * _down(u, 1) + w[2:3] * u
        o_ref[...] = (b_ref[...].astype(F32) * cv).astype(BF16)

    col = lambda off: pl.BlockSpec((L, TW), lambda j: (0, off + j))
    return pl.pallas_call(
        kern, grid=(nb,), in_specs=[col(0), col(nb), col(2 * nb), pl.BlockSpec((8, TW), lambda j: (0, j))],
        out_specs=pl.BlockSpec((L, TW), lambda j: (0, j)), out_shape=jax.ShapeDtypeStruct((L, D), BF16),
        name="sc_fwd", compiler_params=_cp(("parallel",)),
    )(proj, proj, proj, cw)


def _sc_bwd(dya, proj, cw, dproj):
    L = proj.shape[0]
    nb = D // TW

    def kern(d_ref, b_ref, c_ref, x_ref, w_ref, _, dp_ref, gw_ref, keep):
        sec = pl.program_id(1)

        @pl.when(sec == 0)
        def _():
            cs, xs, dyv = c_ref[...].astype(F32), x_ref[...].astype(F32), d_ref[...]
            w = w_ref[...]
            u = cs * xs
            u1, u2 = _down(u, 1), _down(u, 2)
            cv = w[0:1] * u2 + w[1:2] * u1 + w[2:3] * u
            dcv = dyv * b_ref[...].astype(F32)
            du = w[2:3] * dcv + w[1:2] * _up(dcv, 1) + w[0:1] * _up(dcv, 2)
            g0 = jnp.sum(dcv * u2, axis=0, keepdims=True)
            g1 = jnp.sum(dcv * u1, axis=0, keepdims=True)
            g2 = jnp.sum(dcv * u, axis=0, keepdims=True)
            row = lax.broadcasted_iota(jnp.int32, (8, TW), 0)
            gw_ref[...] = jnp.where(row == 0, g0, jnp.where(row == 1, g1, jnp.where(row == 2, g2, 0.0)))
            dp_ref[...] = (dyv * cv).astype(BF16)
            keep[0] = (du * xs).astype(BF16)
            keep[1] = (du * cs).astype(BF16)

        @pl.when(sec > 0)
        def _():
            dp_ref[...] = keep[sec - 1]

    col = lambda off: pl.BlockSpec((L, TW), lambda j, s: (0, off + j))
    return pl.pallas_call(
        kern, grid=(nb, 3),
        in_specs=[col(0), col(0), col(nb), col(2 * nb), pl.BlockSpec((8, TW), lambda j, s: (0, j)),
                  pl.BlockSpec(memory_space=pl.ANY)],
        out_specs=(pl.BlockSpec((L, TW), lambda j, s: (0, s * nb + j)), pl.BlockSpec((8, TW), lambda j, s: (0, j))),
        out_shape=(jax.ShapeDtypeStruct(dproj.shape, BF16), jax.ShapeDtypeStruct((8, D), F32)),
        scratch_shapes=[pltpu.VMEM((2, L, TW), BF16)],
        input_output_aliases={5: 0}, name="sc_bwd", compiler_params=_cp(("parallel", "arbitrary")),
    )(dya, proj, proj, proj, cw, dproj)


def _ssm_conv_fwd(proj, cw4):
    L = proj.shape[0]
    off = C_XBC // TW

    def kern(r_ref, w_ref, o_ref):
        raw = r_ref[...].astype(F32)
        w = w_ref[...]
        c4 = w[0:1] * _down(raw, 3) + w[1:2] * _down(raw, 2) + w[2:3] * _down(raw, 1) + w[3:4] * raw + w[4:5]
        o_ref[...] = c4 * _sigmoid(c4)

    return pl.pallas_call(
        kern, grid=(XBC // TW,),
        in_specs=[pl.BlockSpec((L, TW), lambda j: (0, off + j)), pl.BlockSpec((8, TW), lambda j: (0, j))],
        out_specs=pl.BlockSpec((L, TW), lambda j: (0, j)), out_shape=jax.ShapeDtypeStruct((L, XBC), F32),
        name="ssm_conv_fwd", compiler_params=_cp(("parallel",)),
    )(proj, cw4)


def _ssm_conv_bwd(dx, proj, cw4, dproj, col0, name):
    L, width = dx.shape
    off_p = (C_XBC + col0) // TW
    off_w = col0 // TW

    def kern(d_ref, r_ref, w_ref, _, dp_ref, gw_ref):
        raw = r_ref[...].astype(F32)
        w = w_ref[...]
        r1, r2, r3 = _down(raw, 1), _down(raw, 2), _down(raw, 3)
        c4 = w[0:1] * r3 + w[1:2] * r2 + w[2:3] * r1 + w[3:4] * raw + w[4:5]
        sg = _sigmoid(c4)
        dc4 = d_ref[...] * (sg * (1.0 + c4 * (1.0 - sg)))
        draw = w[3:4] * dc4 + w[2:3] * _up(dc4, 1) + w[1:2] * _up(dc4, 2) + w[0:1] * _up(dc4, 3)
        dp_ref[...] = draw.astype(BF16)
        gs = [jnp.sum(dc4 * r3, axis=0, keepdims=True), jnp.sum(dc4 * r2, axis=0, keepdims=True),
              jnp.sum(dc4 * r1, axis=0, keepdims=True), jnp.sum(dc4 * raw, axis=0, keepdims=True),
              jnp.sum(dc4, axis=0, keepdims=True)]
        row = lax.broadcasted_iota(jnp.int32, (8, TW), 0)
        acc = jnp.zeros((8, TW), F32)
        for k, gk in enumerate(gs):
            acc = jnp.where(row == k, gk, acc)
        gw_ref[...] = acc

    return pl.pallas_call(
        kern, grid=(width // TW,),
        in_specs=[pl.BlockSpec((L, TW), lambda j: (0, j)), pl.BlockSpec((L, TW), lambda j: (0, off_p + j)),
                  pl.BlockSpec((8, TW), lambda j: (0, off_w + j)), pl.BlockSpec(memory_space=pl.ANY)],
        out_specs=(pl.BlockSpec((L, TW), lambda j: (0, off_p + j)), pl.BlockSpec((8, TW), lambda j: (0, j))),
        out_shape=(jax.ShapeDtypeStruct(dproj.shape, BF16), jax.ShapeDtypeStruct((8, width), F32)),
        input_output_aliases={3: 0}, name=name, compiler_params=_cp(("arbitrary",)),
    )(dx, proj, cw4, dproj)


def _split3(v):
    h1 = v.astype(BF16)
    r1 = v - h1.astype(F32)
    h2 = r1.astype(BF16)
    h3 = (r1 - h2.astype(F32)).astype(BF16)
    return h1, h2, h3


def _dot01(m01, v, dims=_DIMS["nn"], m_left=True, terms=3):
    out = None
    for part in _split3(v)[:terms]:
        ops = (m01, part) if m_left else (part, m01)
        t = lax.dot_general(ops[0], ops[1], dims, preferred_element_type=F32)
        out = t if out is None else out + t
    return out


def _bdot(a, b, mode="nn"):
    return lax.dot_general(a.astype(BF16), b.astype(BF16), _DIMS[mode], preferred_element_type=F32)


def _softplus(v):
    return jnp.maximum(v, 0.0) + jnp.log1p(jnp.exp(-jnp.abs(v)))


def _dt_prep(proj, vec):
    L = proj.shape[0]

    def kern(p_ref, v_ref, dt_ref, cs_ref, sg_ref):
        v = v_ref[...]
        pre = p_ref[:, 0:128] + v[0:1]
        dt = _softplus(pre)
        da = dt * (-jnp.exp(v[1:2]))
        ii = lax.broadcasted_iota(jnp.int32, (Q, Q), 0)
        jj = lax.broadcasted_iota(jnp.int32, (Q, Q), 1)
        ltri = (jj <= ii).astype(BF16)
        lane = lax.broadcasted_iota(jnp.int32, (Q, 128), 1)
        for val, ref in ((dt, dt_ref), (_dot01(ltri, da), cs_ref), (_sigmoid(pre), sg_ref)):
            for g in range(NG):
                moved = val if g == 0 else pltpu.roll(val, 128 - 4 * g, axis=1)
                ref[g] = jnp.where(lane < 4, moved, 0.0)

    blk = pl.BlockSpec((NG, Q, 128), lambda c: (0, c, 0))
    return pl.pallas_call(
        kern, grid=(L // Q,),
        in_specs=[pl.BlockSpec((Q, 256), lambda c: (c, 0)), pl.BlockSpec((8, 128), lambda c: (0, 0))],
        out_specs=(blk, blk, blk),
        out_shape=(jax.ShapeDtypeStruct((NG, L, 128), F32),) * 3,
        name="dt_prep", compiler_params=_cp(("parallel",)),
    )(proj, vec)


def _head_masks():
    lane = lax.broadcasted_iota(jnp.int32, (1, 4 * HD), 1)
    return [((lane >= HD * j) & (lane < HD * (j + 1))) for j in range(4)]


def _expand4(v4, masks):
    R = v4.shape[0]
    out = jnp.zeros((R, 4 * HD), F32)
    for j in range(4):
        out = jnp.where(masks[j], jnp.broadcast_to(v4[:, j:j + 1], (R, 4 * HD)), out)
    return out


def _decay_matrix(cs_col, tri):
    colb = jnp.broadcast_to(cs_col, (Q, Q))
    return jnp.exp(jnp.where(tri, colb - colb.T, -jnp.inf))


def _ssd_fwd(xbc, dt4, cs4, vecg):
    L = xbc.shape[0]
    nc = L // Q

    def kern(x_ref, b_ref, c_ref, dt_ref, cs_ref, v_ref, y_ref, s_ref, S):
        c = pl.program_id(1)

        @pl.when(c == 0)
        def _():
            S[...] = jnp.zeros_like(S)

        masks = _head_masks()
        ii = lax.broadcasted_iota(jnp.int32, (Q, Q), 0)
        jj = lax.broadcasted_iota(jnp.int32, (Q, Q), 1)
        tri = jj <= ii
        for gi in range(GPS):
            xs, ns = slice(256 * gi, 256 * (gi + 1)), slice(NS * gi, NS * (gi + 1))
            dt4v, cs4v = dt_ref[gi], cs_ref[gi]
            dt_b, cs_b = _expand4(dt4v, masks), _expand4(cs4v, masks)
            d_b = _expand4(v_ref[gi], masks)[1:2]
            cs_last = cs_b[Q - 1:Q, :]
            x4, bm, cm = x_ref[:, xs], b_ref[:, ns], c_ref[:, ns]
            xdt = x4 * dt_b
            gm = _bdot(cm, bm, "nt")
            s4 = S[gi]
            s_ref[gi, 0] = s4
            y = _bdot(cm, s4) * jnp.exp(cs_b) + d_b * x4
            m_all = jnp.concatenate([(gm * _decay_matrix(cs4v[:, j:j + 1], tri)).astype(BF16) for j in range(4)], axis=0)
            yd = _bdot(m_all, xdt)
            for j in range(4):
                y = y + jnp.where(masks[j], yd[Q * j:Q * (j + 1)], 0.0)
            y_ref[:, xs] = y
            S[gi] = jnp.exp(cs_last) * s4 + _bdot(bm, xdt * jnp.exp(cs_last - cs_b), "tn")

    sc = pl.BlockSpec((GPS, Q, 128), lambda g, c: (g, c, 0))
    bw = NS * GPS
    return pl.pallas_call(
        kern, grid=(NG // GPS, nc),
        in_specs=[pl.BlockSpec((Q, 256 * GPS), lambda g, c: (c, g)),
                  pl.BlockSpec((Q, bw), lambda g, c: (c, INNER // bw + g)),
                  pl.BlockSpec((Q, bw), lambda g, c: (c, (INNER + NG * NS) // bw + g)),
                  sc, sc, pl.BlockSpec((GPS, 8, 128), lambda g, c: (g, 0, 0))],
        out_specs=(pl.BlockSpec((Q, 256 * GPS), lambda g, c: (c, g)),
                   pl.BlockSpec((GPS, 1, NS, 256), lambda g, c: (g, c, 0, 0))),
        out_shape=(jax.ShapeDtypeStruct((L, INNER), F32), jax.ShapeDtypeStruct((NG, nc, NS, 256), F32)),
        scratch_shapes=[pltpu.VMEM((GPS, NS, 256), F32)], name="ssd_fwd",
        compiler_params=_cp(("parallel", "arbitrary")),
    )(xbc, xbc, xbc, dt4, cs4, vecg)


def _ssd_bwd(xbc, dt4, cs4, sg4, vecg, s_all, dy):
    L = xbc.shape[0]
    nc = L // Q

    def kern(x_ref, b_ref, c_ref, dt_ref, cs_ref, sg_ref, v_ref, s_ref, dy_ref,
             dx_ref, db_ref, dc_ref, ddt_ref, st_ref, dS):
        cc = pl.program_id(1)

        @pl.when(cc == 0)
        def _():
            dS[...] = jnp.zeros_like(dS)
            st_ref[...] = jnp.zeros_like(st_ref)

        masks = _head_masks()
        ii = lax.broadcasted_iota(jnp.int32, (Q, Q), 0)
        jj = lax.broadcasted_iota(jnp.int32, (Q, Q), 1)
        tri = jj <= ii
        utri = (jj >= ii).astype(BF16)
        hsel = ((lax.broadcasted_iota(jnp.int32, (4 * HD, 128), 0) // HD)
                == lax.broadcasted_iota(jnp.int32, (4 * HD, 128), 1)).astype(BF16)
        hrow = ((lax.broadcasted_iota(jnp.int32, (4 * Q, 128), 0) // Q)
                == lax.broadcasted_iota(jnp.int32, (4 * Q, 128), 1)).astype(BF16)
        ones_q = jnp.ones((Q, 128), BF16)
        lane128 = lax.broadcasted_iota(jnp.int32, (Q, 128), 1)

        for gi in range(GPS):
            xs, ns = slice(256 * gi, 256 * (gi + 1)), slice(NS * gi, NS * (gi + 1))
            dt4v, cs4v, sg4v = dt_ref[gi], cs_ref[gi], sg_ref[gi]
            dt_b, cs_b = _expand4(dt4v, masks), _expand4(cs4v, masks)
            vv = _expand4(v_ref[gi], masks)
            a_b = -jnp.exp(vv[0:1])
            d_b = vv[1:2]
            a4 = -jnp.exp(v_ref[gi][0:1, :])
            cs_last = cs_b[Q - 1:Q, :]
            ecs = jnp.exp(cs_b)
            decay = jnp.exp(cs_last - cs_b)
            elast = jnp.exp(cs_last)
            x4, bm, cm, dyv = x_ref[:, xs], b_ref[:, ns], c_ref[:, ns], dy_ref[:, xs]
            s4 = s_ref[gi, 0]
            dsn = dS[gi]
            xdt = x4 * dt_b
            gm = _bdot(cm, bm, "nt")
            dye = dyv * ecs
            yoff = ecs * _bdot(cm, s4)
            t4 = _bdot(bm, dsn) * decay
            lms, mhs = [], []
            for j in range(4):
                colb = jnp.broadcast_to(cs4v[:, j:j + 1], (Q, Q))
                lms.append(jnp.exp(jnp.where(tri, colb - colb.T, -jnp.inf)))
                mhs.append(gm * lms[j])
            m_all = jnp.concatenate([m.astype(BF16) for m in mhs], axis=0)
            dy_m = jnp.concatenate([jnp.where(masks[j], dyv, 0.0).astype(BF16) for j in range(4)], axis=0)
            dxdt = t4 + _bdot(m_all, dy_m, "tn")
            dm_all = _bdot(dy_m, xdt, "nt")
            dg = jnp.zeros((Q, Q), F32)
            for j in range(4):
                dg = dg + dm_all[Q * j:Q * (j + 1)] * lms[j]
            e_all = dm_all * jnp.concatenate(mhs, axis=0)
            rsum = _dot01(ones_q, e_all, m_left=False, terms=2)
            da4 = -_dot01(hrow, e_all, _DIMS["tn"], m_left=False, terms=2)
            for j in range(4):
                da4 = da4 + jnp.where(lane128 == j, rsum[Q * j:Q * (j + 1)], 0.0)
            xt = xdt * t4
            tail = jnp.sum(xt, axis=0, keepdims=True) + elast * jnp.sum(s4 * dsn, axis=0, keepdims=True)
            gd_raw = jnp.sum(dyv * x4, axis=0, keepdims=True)
            stacked = jnp.concatenate([dyv * yoff - xt, dxdt * x4, jnp.broadcast_to(tail, (8, 4 * HD)),
                                       jnp.broadcast_to(gd_raw, (8, 4 * HD))], axis=0)
            seg = _dot01(hsel, stacked, m_left=False, terms=2)
            dda4 = _dot01(utri, da4 + seg[0:Q], terms=2) + seg[2 * Q:2 * Q + 1]
            ddt_ref[gi] = (dda4 * a4 + seg[Q:2 * Q]) * sg4v
            ga = jnp.sum(dda4 * dt4v * a4, axis=0, keepdims=True)
            row = lax.broadcasted_iota(jnp.int32, (8, 128), 0)
            st_ref[gi] += jnp.where(row == 0, ga, jnp.where(row == 1, seg[2 * Q + 8:2 * Q + 9], 0.0))
            dx_ref[:, xs] = d_b * dyv + dxdt * dt_b
            dc_ref[:, ns] = _bdot(dg, bm) + _bdot(dye, s4, "nt")
            db_ref[:, ns] = _bdot(dg, cm, "tn") + _bdot(xdt * decay, dsn, "nt")
            dS[gi] = elast * dsn + _bdot(cm, dye, "tn")

    rv = lambda c: nc - 1 - c
    sc = pl.BlockSpec((GPS, Q, 128), lambda g, c: (g, rv(c), 0))
    bw = NS * GPS
    return pl.pallas_call(
        kern, grid=(NG // GPS, nc),
        in_specs=[pl.BlockSpec((Q, 256 * GPS), lambda g, c: (rv(c), g)),
                  pl.BlockSpec((Q, bw), lambda g, c: (rv(c), INNER // bw + g)),
                  pl.BlockSpec((Q, bw), lambda g, c: (rv(c), (INNER + NG * NS) // bw + g)),
                  sc, sc, sc, pl.BlockSpec((GPS, 8, 128), lambda g, c: (g, 0, 0)),
                  pl.BlockSpec((GPS, 1, NS, 256), lambda g, c: (g, rv(c), 0, 0)),
                  pl.BlockSpec((Q, 256 * GPS), lambda g, c: (rv(c), g))],
        out_specs=(pl.BlockSpec((Q, 256 * GPS), lambda g, c: (rv(c), g)),
                   pl.BlockSpec((Q, bw), lambda g, c: (rv(c), g)),
                   pl.BlockSpec((Q, bw), lambda g, c: (rv(c), g)),
                   pl.BlockSpec((GPS, Q, 128), lambda g, c: (g, rv(c), 0)),
                   pl.BlockSpec((GPS, 8, 128), lambda g, c: (g, 0, 0))),
        out_shape=(jax.ShapeDtypeStruct((L, INNER), F32), jax.ShapeDtypeStruct((L, NG * NS), F32),
                   jax.ShapeDtypeStruct((L, NG * NS), F32), jax.ShapeDtypeStruct((NG, L, 128), F32),
                   jax.ShapeDtypeStruct((NG, 8, 128), F32)),
        scratch_shapes=[pltpu.VMEM((GPS, NS, 256), F32)], name="ssd_bwd",
        compiler_params=_cp(("parallel", "arbitrary")),
    )(xbc, xbc, xbc, dt4, cs4, sg4, vecg, s_all, dy)


def _dt_bwd(ddt, dproj, tl=256):
    L = ddt.shape[1]

    def kern(d_ref, _, dp_ref, gs_ref):
        @pl.when(pl.program_id(0) == 0)
        def _():
            gs_ref[...] = jnp.zeros_like(gs_ref)

        d = d_ref[0]
        for g in range(1, NG):
            d = d + pltpu.roll(d_ref[g], 4 * g, axis=1)
        gs_ref[...] += jnp.broadcast_to(jnp.sum(d, axis=0, keepdims=True), (8, 128))
        dp_ref[...] = jnp.concatenate([d, jnp.zeros_like(d)], axis=1).astype(BF16)

    return pl.pallas_call(
        kern, grid=(L // tl,),
        in_specs=[pl.BlockSpec((NG, tl, 128), lambda i: (0, i, 0)), pl.BlockSpec(memory_space=pl.ANY)],
        out_specs=(pl.BlockSpec((tl, 256), lambda i: (i, C_DT // 256)), pl.BlockSpec((8, 128), lambda i: (0, 0))),
        out_shape=(jax.ShapeDtypeStruct(dproj.shape, BF16), jax.ShapeDtypeStruct((8, 128), F32)),
        input_output_aliases={1: 0}, name="dt_bwd", compiler_params=_cp(("arbitrary",)),
    )(ddt, dproj)


GW = INNER // NG


def _gnorm_fwd(y, proj, w, tl=256):
    L = y.shape[0]
    zoff = C_Z // 1024

    def kern(y_ref, z_ref, w_ref, o_ref):
        z = z_ref[...].astype(F32)
        yz = y_ref[...] * (z * _sigmoid(z))
        wv = w_ref[...]
        for k in range(1024 // GW):
            sl = slice(GW * k, GW * (k + 1))
            v = yz[:, sl]
            rg = lax.rsqrt(jnp.mean(v * v, axis=-1, keepdims=True) + EPS)
            o_ref[:, sl] = ((v * rg) * wv[:, sl]).astype(BF16)

    blk = pl.BlockSpec((tl, 1024), lambda i, j: (i, j))
    return pl.pallas_call(
        kern, grid=(L // tl, 2),
        in_specs=[blk, pl.BlockSpec((tl, 1024), lambda i, j: (i, zoff + j)), pl.BlockSpec((1, 1024), lambda i, j: (0, j))],
        out_specs=blk, out_shape=jax.ShapeDtypeStruct((L, INNER), BF16), name="gnorm_fwd",
        compiler_params=_cp(("parallel", "parallel")),
    )(y, proj, w.reshape(1, INNER))


def _gnorm_bwd(dbr, wb, y, proj, w, dproj, dep, tl=512):
    L = y.shape[0]
    tl = min(tl, L)
    zoff = C_Z // 1024

    def kern(d_ref, b_ref, y_ref, z_ref, w_ref, _, __, dy_ref, dp_ref, gw_ref):
        @pl.when(pl.program_id(1) == 0)
        def _():
            gw_ref[...] = jnp.zeros_like(gw_ref)

        z = z_ref[...].astype(F32)
        sg = _sigmoid(z)
        sz = z * sg
        yv = y_ref[...]
        yz = yv * sz
        dv = lax.dot_general(d_ref[0], b_ref[...], _DIMS["nt"], preferred_element_type=F32)
        wv = w_ref[...]
        for k in range(1024 // GW):
            sl = slice(GW * k, GW * (k + 1))
            v = yz[:, sl]
            rg = lax.rsqrt(jnp.mean(v * v, axis=-1, keepdims=True) + EPS)
            vn = v * rg
            dk = dv[:, sl]
            gw_ref[:, sl] += jnp.broadcast_to(jnp.sum(dk * vn, axis=0, keepdims=True), (8, GW))
            dvn = dk * wv[:, sl]
            dyz = rg * (dvn - vn * jnp.mean(dvn * vn, axis=-1, keepdims=True))
            dy_ref[:, sl] = dyz * sz[:, sl]
            dp_ref[:, sl] = (dyz * yv[:, sl] * (sg[:, sl] * (1.0 + z[:, sl] * (1.0 - sg[:, sl])))).astype(BF16)

    blk = pl.BlockSpec((tl, 1024), lambda j, i: (i, j))
    zblk = pl.BlockSpec((tl, 1024), lambda j, i: (i, zoff + j))
    return pl.pallas_call(
        kern, grid=(2, L // tl),
        in_specs=[pl.BlockSpec((1, tl, D), lambda j, i: (1, i, 0)), pl.BlockSpec((1024, D), lambda j, i: (j, 0)),
                  blk, zblk, pl.BlockSpec((1, 1024), lambda j, i: (0, j)), pl.BlockSpec(memory_space=pl.ANY),
                  pl.BlockSpec(memory_space=pl.ANY)],
        out_specs=(blk, zblk, pl.BlockSpec((8, 1024), lambda j, i: (0, j))),
        out_shape=(jax.ShapeDtypeStruct((L, INNER), F32), jax.ShapeDtypeStruct(dproj.shape, BF16),
                   jax.ShapeDtypeStruct((8, INNER), F32)),
        input_output_aliases={5: 1}, name="gnorm_bwd", compiler_params=_cp(("parallel", "arbitrary")),
    )(dbr, wb, y, proj, w.reshape(1, INNER), dproj, dep)


def _merge_fwd(proj, bg, br_a, br_b, tl=256):
    L = proj.shape[0]
    goff = C_GATE // 1024

    def kern(g1_ref, g2_ref, b1_ref, b2_ref, a_ref, b_ref, o_ref):
        g1 = _sigmoid(g1_ref[...].astype(F32) + b1_ref[...])
        g2 = _sigmoid(g2_ref[...].astype(F32) + b2_ref[...])
        o_ref[...] = (g1 * a_ref[...] + g2 * b_ref[...]).astype(BF16)

    row = pl.BlockSpec((tl, 1024), lambda i: (i, 0))
    bg2 = bg.reshape(1, 2 * D)
    return pl.pallas_call(
        kern, grid=(L // tl,),
        in_specs=[pl.BlockSpec((tl, 1024), lambda i: (i, goff)), pl.BlockSpec((tl, 1024), lambda i: (i, goff + 1)),
                  pl.BlockSpec((1, 1024), lambda i: (0, 0)), pl.BlockSpec((1, 1024), lambda i: (0, 1)), row, row],
        out_specs=row, out_shape=jax.ShapeDtypeStruct((L, D), BF16), name="merge_fwd",
        compiler_params=_cp(("parallel",)),
    )(proj, proj, bg2, bg2, br_a, br_b)


def _branch_ssm_merge(yb, wb, proj, bg, br_a, tm=512):
    L, K = yb.shape
    tm = min(tm, L)
    goff = C_GATE // 1024

    def kern(a_ref, b_ref, g1_ref, g2_ref, b1_ref, b2_ref, bra_ref, brb_ref, m_ref):
        brb = lax.dot_general(a_ref[...], b_ref[...], _DIMS["nn"], preferred_element_type=F32)
        brb_ref[...] = brb
        g1 = _sigmoid(g1_ref[...].astype(F32) + b1_ref[...])
        g2 = _sigmoid(g2_ref[...].astype(F32) + b2_ref[...])
        m_ref[...] = (g1 * bra_ref[...] + g2 * brb).astype(BF16)

    row = pl.BlockSpec((tm, D), lambda i: (i, 0))
    bg2 = bg.reshape(1, 2 * D)
    return pl.pallas_call(
        kern, grid=(L // tm,),
        in_specs=[pl.BlockSpec((tm, K), lambda i: (i, 0)), pl.BlockSpec((K, D), lambda i: (0, 0)),
                  pl.BlockSpec((tm, D), lambda i: (i, goff)), pl.BlockSpec((tm, D), lambda i: (i, goff + 1)),
                  pl.BlockSpec((1, D), lambda i: (0, 0)), pl.BlockSpec((1, D), lambda i: (0, 1)), row],
        out_specs=(row, row), out_shape=(jax.ShapeDtypeStruct((L, D), F32), jax.ShapeDtypeStruct((L, D), BF16)),
        name="branch_ssm_merge", compiler_params=_cp(("parallel",)),
    )(yb, wb, proj, proj, bg2, bg2, br_a)


def _merge_bwd(dx1, wo, proj, bg, br_a, br_b, dproj, tl=512):
    L = proj.shape[0]
    tl = min(tl, L)
    goff = C_GATE // 1024

    def kern(dm_ref, wo_ref, g_ref, b_ref, a_ref, bb_ref, _, dbr_ref, dp_ref, gb_ref):
        j = pl.program_id(0)

        @pl.when(pl.program_id(1) == 0)
        def _():
            gb_ref[...] = jnp.zeros_like(gb_ref)

        g = _sigmoid(g_ref[...].astype(F32) + b_ref[...])
        br = jnp.where(j == 0, a_ref[...], bb_ref[...])
        dmv = lax.dot_general(dm_ref[...].astype(BF16), wo_ref[...], _DIMS["nt"], preferred_element_type=F32)
        dbr_ref[0] = (dmv * g).astype(BF16)
        dgate = dmv * br * g * (1.0 - g)
        gb_ref[...] += jnp.broadcast_to(jnp.sum(dgate, axis=0, keepdims=True), (8, 1024))
        dp_ref[...] = dgate.astype(BF16)

    row = pl.BlockSpec((tl, 1024), lambda j, i: (i, 0))
    gblk = pl.BlockSpec((tl, 1024), lambda j, i: (i, goff + j))
    return pl.pallas_call(
        kern, grid=(2, L // tl),
        in_specs=[row, pl.BlockSpec((D, D), lambda j, i: (0, 0)), gblk, pl.BlockSpec((1, 1024), lambda j, i: (0, j)),
                  row, row, pl.BlockSpec(memory_space=pl.ANY)],
        out_specs=(pl.BlockSpec((1, tl, 1024), lambda j, i: (j, i, 0)), gblk, pl.BlockSpec((8, 1024), lambda j, i: (0, j))),
        out_shape=(jax.ShapeDtypeStruct((2, L, D), BF16), jax.ShapeDtypeStruct(dproj.shape, BF16),
                   jax.ShapeDtypeStruct((8, 2 * D), F32)),
        input_output_aliases={6: 1}, name="merge_bwd", compiler_params=_cp(("parallel", "arbitrary")),
    )(dx1, wo, proj, bg.reshape(1, 2 * D), br_a, br_b, dproj)


def _coords():
    return lax.axis_index("x"), lax.axis_index("y"), lax.axis_index("c")


def _other_chips(sk):
    xk, yk = sk // 2, sk % 2
    return [((1 - xk, yk), 2 * (1 - xk) + yk), ((xk, 1 - yk), 2 * xk + 1 - yk), ((1 - xk, 1 - yk), 2 * (1 - xk) + 1 - yk)]


def _rows(start, size):
    assert size % 128 == 0
    return pl.ds(pl.multiple_of(start, 128), size)


def _per_chip(fn):
    x, y, _ = _coords()
    s = 2 * x + y
    for sk in range(4):
        pl.when(s == sk)(functools.partial(fn, sk))


XTRA = PIECE - PMAIN


def _place(shard, full_shape, block, index_map, idx, name, blk0=0, nblk=None, dep=None, into=None):
    in_block = block[-2:]
    if nblk is None:
        nblk = shard.shape[0] // in_block[0]

    def kern(idx_ref, s_ref, *rest):
        o_ref = rest[-1]
        o_ref[...] = s_ref[...].astype(BF16).reshape(o_ref.shape)

    extra = ([dep] if dep is not None else []) + ([into] if into is not None else [])
    grid_spec = pltpu.PrefetchScalarGridSpec(
        num_scalar_prefetch=1, grid=(nblk,),
        in_specs=[pl.BlockSpec(in_block, lambda i, idx_ref: (blk0 + i, 0))] + [_ANY] * len(extra),
        out_specs=pl.BlockSpec(block, index_map))
    aliases = {1 + len(extra): 0} if into is not None else {}
    return pl.pallas_call(kern, grid_spec=grid_spec, out_shape=jax.ShapeDtypeStruct(full_shape, BF16), name=name,
                          input_output_aliases=aliases, compiler_params=_cp(("arbitrary",)))(idx, shard, *extra)


_SEM = pl.BlockSpec(memory_space=pltpu.SEMAPHORE)
_EFFECT = pltpu.SideEffectType.DATAFLOW_SIDE_EFFECTING


_ANY = pl.BlockSpec(memory_space=pl.ANY)


def _tie(v, dep, name):
    def body(v_ref, dep_ref, o_ref):
        del v_ref, dep_ref, o_ref

    return pl.pallas_call(body, out_shape=jax.ShapeDtypeStruct(v.shape, v.dtype), in_specs=[_ANY, _ANY],
                          out_specs=_ANY, input_output_aliases={0: 0}, name=name)(v, dep)


def _split_call(name, arrays, start=None, wait=None, wait_sems=None, after=None):
    keys = list(arrays)
    n = len(keys)
    n_start = start.n if start is not None else 0
    afters = [] if after is None else (list(after) if isinstance(after, (list, tuple)) else [after])

    def body(*refs):
        pos = n
        if wait is not None:
            wss, wrs = refs[pos], refs[pos + 1]
            pos += 2
        pos += len(afters)
        if start is not None:
            nss, nrs = refs[pos], refs[pos + 1]
            pos += 2
        R = dict(zip(keys, refs[pos:pos + n]))
        token = refs[pos + n]
        x, y, c = _coords()

        def desc(src, dst, dev, ss, rs, k):
            return pltpu.make_async_remote_copy(src_ref=src, dst_ref=dst, send_sem=ss.at[k], recv_sem=rs.at[k],
                                                device_id=dev, device_id_type=MESH)

        def run(sk):
            if wait is not None:
                for k, (snd, land) in enumerate(wait.copies(sk, R)):
                    if snd is not None:
                        desc(snd[0], snd[1], snd[2], wss, wrs, k).wait_send()
                    if land is not None:
                        desc(land, land, (x, y, c), wss, wrs, k).wait_recv()
            if start is not None:
                for k, (snd, land) in enumerate(start.copies(sk, R)):
                    if snd is not None:
                        desc(snd[0], snd[1], snd[2], nss, nrs, k).start()

        _per_chip(run)
        token[...] = jnp.zeros_like(token)

    hbm = pl.BlockSpec(memory_space=HBM)
    vals = [arrays[k] for k in keys]
    ins, in_specs = list(vals), [hbm] * n
    if wait is not None:
        ins += list(wait_sems)
        in_specs += [_SEM, _SEM]
    ins += afters
    in_specs += [pl.BlockSpec(memory_space=pl.ANY)] * len(afters)
    out_shape, out_specs = [], []
    if start is not None:
        out_shape += [pltpu.SemaphoreType.DMA((n_start,)), pltpu.SemaphoreType.DMA((n_start,))]
        out_specs += [_SEM, _SEM]
    first = len(out_shape)
    out_shape += [jax.ShapeDtypeStruct(v.shape, v.dtype) for v in vals] + [jax.ShapeDtypeStruct((8, 128), F32)]
    out_specs += [hbm] * n + [pl.BlockSpec(memory_space=pltpu.VMEM)]
    res = pl.pallas_call(
        body, out_shape=tuple(out_shape), in_specs=in_specs, out_specs=tuple(out_specs),
        input_output_aliases={i: first + i for i in range(n)}, name=name,
        compiler_params=pltpu.CompilerParams(has_side_effects=_EFFECT),
    )(*ins)
    sems = (res[0], res[1]) if start is not None else None
    return dict(zip(keys, res[first:first + n])), sems, res[-1]


class _Plan:
    def __init__(self, n, copies):
        self.n, self.copies = n, copies


_HM, _HX = PMAIN // 2, XTRA // 2
WAVE0 = 768
WAVES = ((0, WAVE0), (WAVE0, _HM - WAVE0))
_WIN = {
    "wq0": (True, "wct", lambda r, sc, hc: r.at[_rows(PMAIN * sc + _HM * hc + WAVES[0][0], WAVES[0][1]), :]),
    "wq1": (True, "wct", lambda r, sc, hc: r.at[_rows(PMAIN * sc + _HM * hc + WAVES[1][0], WAVES[1][1]), :]),
    "xt": (True, "xt", lambda r, sc, hc: r.at[sc, _rows(_HX * hc, _HX), :]),
    "w1": (True, "w1", lambda r, sc, hc: r.at[_rows(512 * hc, 512), pl.ds(1024 * sc, 1024)]),
    "w2": (True, "w2", lambda r, sc, hc: r.at[_rows(1024 * sc + 512 * hc, 512), :]),
    "wa": (True, "wa", lambda r, sc, hc: r.at[_rows(256 * sc + 128 * hc, 128), :]),
    "wb": (True, "wb", lambda r, sc, hc: r.at[_rows(512 * sc + 256 * hc, 256), :]),
    "wo": (True, "wo", lambda r, sc, hc: r.at[_rows(256 * sc + 128 * hc, 128), :]),
    "cw": (False, "cw", lambda r, sc, hc: r.at[sc]),
}


_PIECE_SRC = {
    "wq0": lambda p, hc: p.at[_rows(_HM * hc + WAVES[0][0], WAVES[0][1]), :],
    "wq1": lambda p, hc: p.at[_rows(_HM * hc + WAVES[1][0], WAVES[1][1]), :],
    "xt": lambda p, hc: p.at[_rows(PMAIN + _HX * hc, _HX), :],
}


def _ag_chips_plan(keys):
    def copies(sk, R):
        _, _, c = _coords()
        out = []
        for key in keys:
            _, arr, win = _WIN[key]
            for (px, py), ps in _other_chips(sk):
                dst = win(R[arr], sk, c)
                src = _PIECE_SRC[key](R["piece"], c) if key in _PIECE_SRC else dst
                out.append(((src, dst, (px, py, c)), win(R[arr], ps, c)))
        return out
    return _Plan(3 * len(keys), copies)


def _ag_sibling_plan(keys):
    keys = [k for k in keys if _WIN[k][0]]

    def copies(sk, R):
        x, y, c = _coords()
        out = []
        for key in keys:
            _, arr, win = _WIN[key]
            for _, ps in _other_chips(sk):
                w = win(R[arr], ps, c)
                out.append(((w, w, (x, y, 1 - c)), win(R[arr], ps, 1 - c)))
        return out
    return _Plan(3 * len(keys), copies)


def _in_proj_wave(h, wct, wave, proj=None, tm=2048):
    L = h.shape[0]
    tm = min(tm, L)
    off, size = WAVES[wave]
    start = lambda j: pl.multiple_of(_HM * j + off, 128)

    def kern(h_ref, w_ref, *rest):
        o_ref = rest[-1]
        o_ref[...] = lax.dot_general(h_ref[...], w_ref[...], _DIMS["nt"], preferred_element_type=F32).astype(BF16)

    in_specs = [pl.BlockSpec((tm, D), lambda j, i: (i, 0)),
                pl.BlockSpec((pl.Element(size), pl.Element(D)), lambda j, i: (start(j), 0))]
    args, aliases = [h, wct], {}
    if proj is not None:
        in_specs.append(pl.BlockSpec(memory_space=pl.ANY))
        args.append(proj)
        aliases = {2: 0}
    return pl.pallas_call(
        kern, grid=(8, L // tm), in_specs=in_specs,
        out_specs=pl.BlockSpec((pl.Element(tm), pl.Element(size)), lambda j, i: (i * tm, start(j))),
        out_shape=jax.ShapeDtypeStruct((L, NCW), BF16), input_output_aliases=aliases,
        name="in_proj_wave%d" % wave, compiler_params=_cp(("parallel", "parallel")),
    )(*args)


def _fix_wct(wct, xt):
    nb = PMAIN // XTRA

    def kern(w_ref, x_ref, o_ref):
        k = pl.program_id(0)
        xv = x_ref[0]
        o_ref[...] = jnp.where(k < 3, (w_ref[...].astype(F32) + xv.astype(F32)).astype(BF16), xv)

    blk = pl.BlockSpec((XTRA, D), lambda k: (nb * (k + 1), 0))
    rblk = pl.BlockSpec((XTRA, D), lambda k: (jnp.where(k < 3, nb * (k + 1), 0), 0))
    return pl.pallas_call(
        kern, grid=(4,), in_specs=[rblk, pl.BlockSpec((1, XTRA, D), lambda k: (k, 0, 0))], out_specs=blk,
        out_shape=jax.ShapeDtypeStruct(wct.shape, BF16), input_output_aliases={0: 0}, name="fix_wct",
        compiler_params=_cp(("arbitrary",)),
    )(wct, xt)


_HP = PIECE // 2
_GWIN = [
    lambda r, sc, hc: r.at[_rows(PMAIN * sc + _HP * hc, _HP), :],
    lambda r, sc, hc: r.at[_rows(512 * hc, 512), pl.ds(1024 * sc, 1024)],
    lambda r, sc, hc: r.at[_rows(1024 * sc + 512 * hc, 512), :],
    lambda r, sc, hc: r.at[_rows(256 * sc + 128 * hc, 128), :],
    lambda r, sc, hc: r.at[_rows(512 * sc + 256 * hc, 256), :],
    lambda r, sc, hc: r.at[_rows(256 * sc + 128 * hc, 128), :],
]
HALF_SHAPES = [(PIECE // 2, D), (512, 1024), (512, 1024), (128, 1024), (256, 1024), (128, 1024)]


def _rs_sibling_plan(ts):
    def copies(sk, R):
        x, y, c = _coords()
        out = []
        for t in ts:
            for sc in range(4):
                land = R["ra%d" % t].at[sc]
                out.append(((_GWIN[t](R["g%d" % t], sc, 1 - c), land, (x, y, 1 - c)), land))
        return out
    return _Plan(4 * len(ts), copies)


def _rs_chips_plan(ts):
    def copies(sk, R):
        _, _, c = _coords()
        out = []
        for t in ts:
            for j, ((px, py), ps) in enumerate(_other_chips(sk)):
                land = R["rb%d" % t].at[j]
                out.append(((R["hb%d" % t].at[ps], land, (px, py, c)), land))
        return out
    return _Plan(3 * len(ts), copies)


def _rs_share_plan(ts):
    def copies(sk, R):
        x, y, c = _coords()
        out = []
        for t in ts:
            rows = HALF_SHAPES[t][0]
            mine = R["f%d" % t].at[_rows(rows * c, rows), :]
            out.append(((mine, mine, (x, y, 1 - c)), R["f%d" % t].at[_rows(rows * (1 - c), rows), :]))
        return out
    return _Plan(len(ts), copies)


def _half_tiling(t):
    rows, cols = HALF_SHAPES[t]
    if t == 0:
        return (rows // 2, cols), 2, lambda i: (i, 0)
    return (rows, cols), 1, lambda i: (0, 0)


def _window_spec(t, blk):
    if t == 0:
        return pl.BlockSpec((pl.Element(blk[0]), pl.Element(blk[1])), lambda i, sc, idx_ref: (
            pl.multiple_of(PMAIN * sc + _HP * idx_ref[1] + blk[0] * i, 128), 0))
    if t == 1:
        return pl.BlockSpec(blk, lambda i, sc, idx_ref: (idx_ref[1], sc))
    return pl.BlockSpec(blk, lambda i, sc, idx_ref: (2 * sc + idx_ref[1], 0))


def _chip_sum(g, ra, t, idx, name):
    rows, cols = HALF_SHAPES[t]
    blk, nblk, inner = _half_tiling(t)

    def kern(idx_ref, g_ref, r_ref, hb_ref, hf_ref):
        v = g_ref[...].astype(F32) + r_ref[0].astype(F32)
        hb_ref[0] = v.astype(BF16)

        @pl.when(pl.program_id(1) == idx_ref[0])
        def _():
            hf_ref[...] = v

    omap = lambda i, sc, idx_ref: (sc,) + inner(i)
    grid_spec = pltpu.PrefetchScalarGridSpec(
        num_scalar_prefetch=1, grid=(nblk, 4),
        in_specs=[_window_spec(t, blk), pl.BlockSpec((1,) + blk, omap)],
        out_specs=(pl.BlockSpec((1,) + blk, omap), pl.BlockSpec(blk, lambda i, sc, idx_ref: inner(i))))
    return pl.pallas_call(
        kern, grid_spec=grid_spec,
        out_shape=(jax.ShapeDtypeStruct((4, rows, cols), BF16), jax.ShapeDtypeStruct((rows, cols), F32)),
        name=name, compiler_params=_cp(("parallel", "arbitrary")),
    )(idx, g, ra)


def _chip_sum_part(g, ra, t, idx, name, own, dep=None):
    rows, cols = HALF_SHAPES[t]
    blk, nblk, inner = _half_tiling(t)
    chip = (lambda k, idx_ref: idx_ref[0]) if own else (lambda k, idx_ref: lax.rem(idx_ref[0] + 1 + k, 4))
    win = _window_spec(t, blk)
    deps = [] if dep is None else [dep]

    def kern(idx_ref, g_ref, r_ref, *rest):
        v = g_ref[...].astype(F32) + r_ref[0].astype(F32)
        if own:
            rest[-1][...] = v
        else:
            rest[-1][0] = v.astype(BF16)

    omap = lambda i, k, idx_ref: (chip(k, idx_ref),) + inner(i)
    grid_spec = pltpu.PrefetchScalarGridSpec(
        num_scalar_prefetch=1, grid=(nblk, 1 if own else 3),
        in_specs=[pl.BlockSpec(win.block_shape, lambda i, k, idx_ref: win.index_map(i, chip(k, idx_ref), idx_ref)),
                  pl.BlockSpec((1,) + blk, omap)] + [pl.BlockSpec(memory_space=pl.ANY)] * len(deps),
        out_specs=pl.BlockSpec(blk, lambda i, k, idx_ref: inner(i)) if own else pl.BlockSpec((1,) + blk, omap))
    return pl.pallas_call(
        kern, grid_spec=grid_spec,
        out_shape=jax.ShapeDtypeStruct((rows, cols), F32) if own else jax.ShapeDtypeStruct((4, rows, cols), BF16),
        name=name, compiler_params=_cp(("parallel", "arbitrary")),
    )(idx, g, ra, *deps)


def _final_sum(hf, rb, t, idx, name):
    rows, cols = HALF_SHAPES[t]
    blk, nblk, inner = _half_tiling(t)
    nbr = rows // blk[0]

    def kern(idx_ref, h_ref, r_ref, o_ref):
        o_ref[...] = ((h_ref[...] + r_ref[0].astype(F32)) + r_ref[1].astype(F32)) + r_ref[2].astype(F32)

    def omap(i, idx_ref):
        r, cidx = inner(i)
        return nbr * idx_ref[1] + r, cidx

    grid_spec = pltpu.PrefetchScalarGridSpec(
        num_scalar_prefetch=1, grid=(nblk,),
        in_specs=[pl.BlockSpec(blk, lambda i, idx_ref: inner(i)),
                  pl.BlockSpec((3,) + blk, lambda i, idx_ref: (0,) + inner(i))],
        out_specs=pl.BlockSpec(blk, omap))
    return pl.pallas_call(
        kern, grid_spec=grid_spec, out_shape=jax.ShapeDtypeStruct((2 * rows, cols), F32),
        name=name, compiler_params=_cp(("parallel",)),
    )(idx, hf, rb)


class _ReduceScatter:
    def __init__(self, ts, grads, idx, tag):
        self.ts, self.idx, self.tag = ts, idx, tag
        arr = {}
        for t in ts:
            arr["g%d" % t] = grads[t]
            arr["ra%d" % t] = lax.empty((4,) + HALF_SHAPES[t], BF16)
        self.plan = _rs_sibling_plan(ts)
        self.arr, self.sems, self.token = _split_call("rs_sibling_start_" + tag, arr, start=self.plan)

    def chips(self, after, own_later=False):
        arr, _, _ = _split_call("rs_sibling_wait_" + self.tag, self.arr, wait=self.plan, wait_sems=self.sems, after=after)
        brr, self.hf = {}, {}
        for t in self.ts:
            if own_later:
                hb = _chip_sum_part(arr["g%d" % t], arr["ra%d" % t], t, self.idx, "chip_sum_others_%d" % t, False)
            else:
                hb, self.hf[t] = _chip_sum(arr["g%d" % t], arr["ra%d" % t], t, self.idx, "chip_sum_%d" % t)
            brr["hb%d" % t] = hb
            brr["rb%d" % t] = lax.empty((3,) + HALF_SHAPES[t], BF16)
        self.plan = _rs_chips_plan(self.ts)
        self.arr, self.sems, self.token = _split_call("rs_chips_start_" + self.tag, brr, start=self.plan)
        if own_later:
            for t in self.ts:
                self.hf[t] = _chip_sum_part(arr["g%d" % t], arr["ra%d" % t], t, self.idx, "chip_sum_own_%d" % t, True,
                                            dep=self.token)
        return self.token

    def share(self, after):
        brr, _, _ = _split_call("rs_chips_wait_" + self.tag, self.arr, wait=self.plan, wait_sems=self.sems, after=after)
        frr = {"f%d" % t: _final_sum(self.hf[t], brr["rb%d" % t], t, self.idx, "final_sum_%d" % t) for t in self.ts}
        self.plan = _rs_share_plan(self.ts)
        self.arr, self.sems, self.token = _split_call("rs_share_start_" + self.tag, frr, start=self.plan)
        return self.token

    def result(self, after):
        frr, _, _ = _split_call("rs_share_wait_" + self.tag, self.arr, wait=self.plan, wait_sems=self.sems, after=after)
        return {t: frr["f%d" % t] for t in self.ts}


def _all8_plan(key):
    def copies(sk, R):
        x, y, c = _coords()
        own = R[key].at[4 * x + 2 * y + c]
        out = []
        for k in range(1, 8):
            dev = ((1 - x) if (k >> 2) & 1 else x, (1 - y) if (k >> 1) & 1 else y, (1 - c) if k & 1 else c)
            out.append(((own, own, dev), R[key].at[4 * dev[0] + 2 * dev[1] + dev[2]]))
        return out
    return _Plan(7, copies)


def _sum8(v, name="small_sum"):
    def kern(v_ref, o_ref):
        acc = v_ref[0]
        for k in range(1, 8):
            acc = acc + v_ref[k]
        o_ref[...] = acc

    return pl.pallas_call(kern, out_shape=jax.ShapeDtypeStruct(v.shape[1:], F32), name=name)(v)


def _adamw(w, g, m, v, name, tr=128, blk0=0, nblk=None, into=None, copy_g=False):
    R, C = w.shape
    tr = min(tr, R)
    if nblk is None:
        assert R % tr == 0 and blk0 == 0
        nblk = R // tr
    n_out = 4 if copy_g else 3

    def kern(*refs):
        w_ref, g_ref, m_ref, v_ref = refs[:4]
        d_ref, mo_ref, vo_ref = refs[-n_out:][:3]
        gv = g_ref[...]
        mn = ADAM_B1 * m_ref[...] + (1.0 - ADAM_B1) * gv
        vn = ADAM_B2 * v_ref[...] + (1.0 - ADAM_B2) * (gv * gv)
        m_hat = mn / (1.0 - ADAM_B1 ** ADAM_STEP)
        v_hat = vn / (1.0 - ADAM_B2 ** ADAM_STEP)
        d_ref[...] = -ADAM_LR * (m_hat / (jnp.sqrt(v_hat) + ADAM_EPS) + ADAM_WD * w_ref[...])
        mo_ref[...] = mn
        vo_ref[...] = vn
        if copy_g:
            refs[-1][...] = gv

    blk = pl.BlockSpec((tr, C), lambda i: (blk0 + i, 0))
    sd = jax.ShapeDtypeStruct((R, C), F32)
    in_specs, args, aliases = [blk] * 4, [w, g, m, v], {}
    if into is not None:
        in_specs += [pl.BlockSpec(memory_space=pl.ANY)] * 3
        args += list(into)
        aliases = {4: 0, 5: 1, 6: 2}
    return pl.pallas_call(kern, grid=(nblk,), in_specs=in_specs, out_specs=(blk,) * n_out, out_shape=(sd,) * n_out,
                          input_output_aliases=aliases, name=name, compiler_params=_cp(("parallel",)))(*args)


def _adamw_w_in(wt, gp, mt, vt, offs, name, r0, tr, nblk, views, into=None, blk_key=None):
    el = lambda n: (pl.Element(n), pl.Element(D))
    first = (lambda o: r0) if blk_key is None else (lambda o: tr * o[blk_key])
    own = pl.BlockSpec(el(tr), lambda i, o: (pl.multiple_of(first(o) + tr * i, 8), 0))

    def view(k):
        return pl.BlockSpec(el(tr), lambda i, o: (pl.multiple_of(jnp.maximum(first(o) + tr * i + o[k], 0), 8), 0))

    def kern(o_ref, w_ref, m_ref, v_ref, *refs):
        g_refs, (d_ref, mo_ref, vo_ref, go_ref) = refs[:len(views)], refs[-4:]
        gv = g_refs[0][...]
        if len(views) == 2:
            row = first(o_ref) + tr * pl.program_id(0) + lax.broadcasted_iota(jnp.int32, (tr, D), 0)
            gv = jnp.where(row < o_ref[2], gv, g_refs[1][...])
        mn = ADAM_B1 * m_ref[...] + (1.0 - ADAM_B1) * gv
        vn = ADAM_B2 * v_ref[...] + (1.0 - ADAM_B2) * (gv * gv)
        m_hat = mn / (1.0 - ADAM_B1 ** ADAM_STEP)
        v_hat = vn / (1.0 - ADAM_B2 ** ADAM_STEP)
        d_ref[...] = -ADAM_LR * (m_hat / (jnp.sqrt(v_hat) + ADAM_EPS) + ADAM_WD * w_ref[...])
        mo_ref[...] = mn
        vo_ref[...] = vn
        go_ref[...] = gv

    in_specs = [own, own, own] + [view(k) for k in views]
    args = [wt, mt, vt] + [gp] * len(views)
    aliases = {}
    if into is not None:
        in_specs += [pl.BlockSpec(memory_space=pl.ANY)] * 4
        args += list(into)
        aliases = {1 + len(args) - 4 + j: j for j in range(4)}
    grid_spec = pltpu.PrefetchScalarGridSpec(num_scalar_prefetch=1, grid=(nblk,), in_specs=in_specs,
                                             out_specs=(own,) * 4)
    sd = jax.ShapeDtypeStruct(wt.shape, F32)
    return pl.pallas_call(kern, grid_spec=grid_spec, out_shape=(sd,) * 4, input_output_aliases=aliases, name=name,
                          compiler_params=_cp(("parallel",)))(offs, *args)


def _to_piece(wt, s):
    z = lambda n: jnp.zeros((n, D), wt.dtype)
    pads = [functools.partial(lambda k, w: jnp.pad(w, ((8 * k, PIECE - W_SHARD - 8 * k), (0, 0))).astype(BF16), k)
            for k in range(3)]
    last = lambda w: jnp.concatenate([z(24), w[:744], w[776:], w[744:776], z(PIECE - 24 - W_SHARD)], axis=0).astype(BF16)
    return lax.switch(s, pads + [last], wt)


_SMALL = [("b_gate", 2048), ("ssm_conv_b", 4096), ("dt_bias", 32), ("A_log", 32), ("D_skip", 32),
          ("ssm_norm_w", 2048), ("norm_mlp", 1024), ("norm_final", 1024), ("sc_conv_w", 3072), ("ssm_conv_w", 16384),
          ("loss", 1)]


def _pack(vals, table, rows):
    parts = []
    for name, n in table:
        v = vals[name].reshape(-1).astype(F32)
        pad = (-n) % 128
        parts.append(jnp.pad(v, (0, pad)) if pad else v)
    flat = jnp.concatenate(parts)
    return jnp.pad(flat, (0, rows * 128 - flat.shape[0])).reshape(rows, 128)


def _unpack(arr, table):
    flat = arr.reshape(-1)
    out, off = {}, 0
    for name, n in table:
        out[name] = flat[off:off + n]
        off += n + ((-n) % 128)
    return out


def kernel(x, norm_mix, w_in, b_gate, sc_conv_w, ssm_conv_w, ssm_conv_b, dt_bias, A_log, D_skip, ssm_norm_w, w_branch_sc, w_branch_ssm, w_out, norm_mlp, w_mlp1, w_mlp2, norm_final, loss_target, m_norm_mix, m_w_in, m_b_gate, m_sc_conv_w, m_ssm_conv_w, m_ssm_conv_b, m_dt_bias, m_A_log, m_D_skip, m_ssm_norm_w, m_w_branch_sc, m_w_branch_ssm, m_w_out, m_norm_mlp, m_w_mlp1, m_w_mlp2, m_norm_final, v_norm_mix, v_w_in, v_b_gate, v_sc_conv_w, v_ssm_conv_w, v_ssm_conv_b, v_dt_bias, v_A_log, v_D_skip, v_ssm_norm_w, v_w_branch_sc, v_w_branch_ssm, v_w_out, v_norm_mlp, v_w_mlp1, v_w_mlp2, v_norm_final):
    L = x.shape[1]
    nc = L // Q
    xi, yi, ci = lax.axis_index("x"), lax.axis_index("y"), lax.axis_index("c")
    s = 2 * xi + yi
    idx = jnp.stack([s, ci]).astype(jnp.int32)
    x0 = x.reshape(L, D)
    tgt = loss_target.reshape(L, D)
    small_names = ["b_gate", "sc_conv_w", "ssm_conv_w", "ssm_conv_b", "dt_bias", "A_log", "D_skip", "ssm_norm_w",
                   "norm_mlp", "norm_final"]
    small_wmv = [dict(zip(small_names, vals)) for vals in (
        (b_gate, sc_conv_w, ssm_conv_w, ssm_conv_b, dt_bias, A_log, D_skip, ssm_norm_w, norm_mlp, norm_final),
        (m_b_gate, m_sc_conv_w, m_ssm_conv_w, m_ssm_conv_b, m_dt_bias, m_A_log, m_D_skip, m_ssm_norm_w, m_norm_mlp,
         m_norm_final),
        (v_b_gate, v_sc_conv_w, v_ssm_conv_w, v_ssm_conv_b, v_dt_bias, v_A_log, v_D_skip, v_ssm_norm_w, v_norm_mlp,
         v_norm_final))]
    small_table = [(n, int(small_wmv[0][n].size)) for n in small_names]
    small_rows = 136
    pk_w, pk_m, pk_v = [_pack(d, small_table, small_rows) for d in small_wmv]

    piece = _to_piece(w_in.T, s)
    nb = PMAIN // XTRA
    cws = jnp.zeros((8, 1280), F32)
    cws = cws.at[0:3, 0:256].set(sc_conv_w).at[0:4, 256:1280].set(ssm_conv_w)
    cw0 = lax.dynamic_update_slice(jnp.zeros((4, 8, 1280), F32), cws[None], (s, 0, 0))
    win_keys, win2_keys, mid_keys, end_keys = ["xt", "cw", "wq0"], ["wq1"], ["wa", "wb", "wo", "w1"], ["w2"]
    gw, sems_w, tok = _split_call(
        "ag_win_start", {"wct": lax.empty((NCW, D), BF16), "xt": lax.empty((4, XTRA, D), BF16), "cw": cw0, "piece": piece},
        start=_ag_chips_plan(win_keys))
    g2, sems_w2, tok = _split_call("ag_win2_start", {"wct": gw["wct"], "piece": gw["piece"]},
                                   start=_ag_chips_plan(win2_keys), after=tok)
    piece = g2["piece"]
    gw["wct"] = _place(piece, (NCW, D), (XTRA, D), lambda i, r: (nb * r[0] + i, 0), idx, "place_wct", nblk=nb,
                       dep=tok, into=g2["wct"])
    gw["xt"] = _place(piece, (4, XTRA, D), (1, XTRA, D), lambda i, r: (r[0], 0, 0), idx, "place_xt", blk0=nb, nblk=1,
                      dep=tok, into=gw["xt"])
    gw["piece"] = piece
    wa0 = _place(w_branch_sc, (D, D), (256, 1024), lambda i, r: (r[0], 0), idx, "place_wa", dep=tok)
    wb0 = _place(w_branch_ssm, (INNER, D), (512, 1024), lambda i, r: (r[0], 0), idx, "place_wb", dep=tok)
    wo0 = _place(w_out, (D, D), (256, 1024), lambda i, r: (r[0], 0), idx, "place_wo", dep=tok)
    w10 = _place(w_mlp1, (D, DFF), (256, 1024), lambda i, r: (i, r[0]), idx, "place_w1", dep=tok)
    gm, sems_m, tok = _split_call("ag_mid_start", {"wa": wa0, "wb": wb0, "wo": wo0, "w1": w10},
                                  start=_ag_chips_plan(mid_keys))
    w20 = _place(w_mlp2, (DFF, D), (256, 1024), lambda i, r: (4 * r[0] + i, 0), idx, "place_w2", dep=tok)
    ge, sems_e, tok = _split_call("ag_end_start", {"w2": w20}, start=_ag_chips_plan(end_keys))
    h = _rms_fwd(x0, norm_mix, "rms_mix", dep=tok)
    gw, sems_w, tok = _split_call("ag_win_pass", gw, wait=_ag_chips_plan(win_keys), wait_sems=sems_w,
                                  start=_ag_sibling_plan(win_keys), after=[h, pk_w, pk_m, pk_v])
    gw, _, _ = _split_call("ag_win_done", gw, wait=_ag_sibling_plan(win_keys), wait_sems=sems_w, after=tok)
    wc, cw_all = _fix_wct(gw["wct"], gw["xt"]), gw["cw"]
    sc_w_full = jnp.concatenate([cw_all[k, :, 0:256] for k in range(4)], axis=1)
    ssm_w_full = jnp.concatenate([cw_all[k, :, 256:1280] for k in range(4)], axis=1)
    cw4 = ssm_w_full.at[4].set(ssm_conv_b)
    vec = jnp.zeros((8, 128), F32).at[0, :NH].set(dt_bias).at[1, :NH].set(A_log)
    vecg = jnp.zeros((NG, 8, 128), F32).at[:, 0, :4].set(A_log.reshape(NG, 4)).at[:, 1, :4].set(D_skip.reshape(NG, 4))

    dtraw = _matmul(h, wc[C_DT:], "nt", F32, 512, 256, 1024, "in_proj_dt")
    proj = _in_proj_wave(h, wc, 0)
    g2, sems_w2, tok = _split_call("ag_win2_pass", {"wct": wc, "piece": gw["piece"]},
                                   wait=_ag_chips_plan(win2_keys), wait_sems=sems_w2,
                                   start=_ag_sibling_plan(win2_keys), after=[proj, dtraw])
    g2, _, _ = _split_call("ag_win2_done", g2, wait=_ag_sibling_plan(win2_keys), wait_sems=sems_w2, after=tok)
    wc = g2["wct"]
    proj = _in_proj_wave(h, wc, 1, proj=proj)
    ya = _sc_fwd(proj, sc_w_full)
    xbc = _ssm_conv_fwd(proj, cw4)
    dt4, cs4, sg4 = _dt_prep(dtraw, vec)
    y, s_all = _ssd_fwd(xbc, dt4, cs4, vecg)
    gm, sems_m, tok = _split_call("ag_mid_pass", gm, wait=_ag_chips_plan(mid_keys), wait_sems=sems_m,
                                  start=_ag_sibling_plan(mid_keys), after=[y, ya])
    y = _tie(y, tok, "tie_y")
    yb = _gnorm_fwd(y, proj, ssm_norm_w)
    gm, _, _ = _split_call("ag_mid_done", gm, wait=_ag_sibling_plan(mid_keys), wait_sems=sems_m, after=yb)
    wa, wb, wo, w1 = gm["wa"], gm["wb"], gm["wo"], gm["w1"]
    ge, sems_e, tok = _split_call("ag_end_pass", ge, wait=_ag_chips_plan(end_keys), wait_sems=sems_e,
                                  start=_ag_sibling_plan(end_keys), after=yb)
    br_a = _matmul(ya, wa, "nn", F32, 1024, 1024, 1024, "branch_sc", dep=tok)
    br_b, merged = _branch_ssm_merge(yb, wb, proj, b_gate, br_a)
    x1, h2 = _matmul_res_rms(merged, wo, x0, norm_mlp, 1024, "out_proj")
    a1, rl = _matmul(h2, w1, "nn", BF16, 1024, 1024, 1024, "mlp1", epi="relu2", n_outer=True)
    ge, _, _ = _split_call("ag_end_done", ge, wait=_ag_sibling_plan(end_keys), wait_sems=sems_e, after=a1)
    w2 = ge["w2"]
    dx2, g_nf, loss8 = _matmul_res_final(rl, w2, x1, norm_final, tgt, 512, "mlp2")

    da = _matmul(dx2, w2, "nt", BF16, 1024, 1024, 1024, "mlp2_dx", epi="drelu", extra=a1, n_outer=True)
    g_w2 = _matmul(rl, dx2, "tn", BF16, 1024, 1024, 2048, "mlp2_dw")
    g_w1 = _matmul(h2, da, "tn", BF16, 1024, 1024, 2048, "mlp1_dw")
    dx1, g_nmlp = _matmul_rms_bwd(da, w1, "nt", x1, norm_mlp, dx2, 512, 4096, "mlp1_dx")
    g_wo = _matmul(merged, dx1, "tn", BF16, 1024, 1024, 2048, "out_proj_dw")
    dproj = lax.empty((L, NCW), BF16)
    dbr, dproj, g_bg = _merge_bwd(dx1, wo, proj, b_gate, br_a, br_b, dproj)
    dya = _matmul(dbr[0], wa, "nt", F32, 1024, 1024, 1024, "branch_sc_dx")
    g_wa = _matmul(ya, dbr[0], "tn", BF16, 1024, 1024, 2048, "branch_sc_dw")
    dproj, g_scw = _sc_bwd(dya, proj, sc_w_full, dproj)
    g_wb = _matmul(yb, dbr[1], "tn", BF16, 1024, 1024, 2048, "branch_ssm_dw")
    rs_a = _ReduceScatter([1, 2, 3, 4, 5], {1: g_w1, 2: g_w2, 3: g_wa, 4: g_wb, 5: g_wo}, idx, "a")
    dy, dproj, g_snw = _gnorm_bwd(dbr, wb, y, proj, ssm_norm_w, dproj, rs_a.token)
    tok = rs_a.chips(after=dy)
    dxs, dbm, dcm, ddt_g, st = _ssd_bwd(xbc, dt4, cs4, sg4, vecg, s_all, _tie(dy, tok, "tie_dy"))
    dproj, gx1 = _ssm_conv_bwd(dxs, proj, cw4, dproj, 0, "ssm_conv_bwd_x")
    dproj, gx2 = _ssm_conv_bwd(dbm, proj, cw4, dproj, INNER, "ssm_conv_bwd_b")
    dproj, gx3 = _ssm_conv_bwd(dcm, proj, cw4, dproj, INNER + NG * NS, "ssm_conv_bwd_c")
    g_cw4 = jnp.concatenate([gx1, gx2, gx3], axis=1)
    dproj, g_dtb = _dt_bwd(ddt_g, dproj)
    small = {"b_gate": g_bg[0], "ssm_conv_b": g_cw4[4], "dt_bias": g_dtb[0, :NH],
             "A_log": st[:, 0, :4], "D_skip": st[:, 1, :4], "ssm_norm_w": g_snw[0], "norm_mlp": g_nmlp[0],
             "norm_final": g_nf[0], "sc_conv_w": g_scw[0:3], "ssm_conv_w": g_cw4[0:4], "loss": loss8[0, 0:1]}
    me = 4 * xi + 2 * yi + ci
    sm8 = lax.dynamic_update_slice(jnp.zeros((8, SMALL_ROWS, 128), F32), _pack(small, _SMALL, SMALL_ROWS)[None], (me, 0, 0))
    sm_arr, sm_sems, tok = _split_call("small_start", {"sm": sm8}, start=_all8_plan("sm"))
    g_wc = _matmul(dproj, h, "tn", BF16, 1280, 1024, 2048, "in_proj_dw", dep=tok)
    rs_b = _ReduceScatter([0], {0: g_wc}, idx, "b")
    tok = rs_a.share(after=rs_b.token)
    tok = rs_b.chips(after=tok, own_later=True)
    grad_x, g_nm = _matmul_rms_bwd(dproj, wc, "nn", x0, norm_mix, dx1, 512, 3840, "in_proj_dx", dep=rs_b.hf[0])
    nm8 = lax.dynamic_update_slice(jnp.zeros((8, 8, 128), F32), g_nm[0].reshape(1, 8, 128), (me, 0, 0))
    nm_arr, nm_sems, tok = _split_call("norm_mix_start", {"nm": nm8}, start=_all8_plan("nm"))
    sm_arr, _, _ = _split_call("small_wait", sm_arr, wait=_all8_plan("sm"), wait_sems=sm_sems, after=tok)
    small_sum = _sum8(sm_arr["sm"])
    gs = _unpack(small_sum, _SMALL)
    red = rs_a.result(after=tok)
    big = {"w_mlp1": red[1], "w_mlp2": red[2], "w_branch_sc": red[3], "w_branch_ssm": red[4], "w_out": red[5]}

    given = dict(norm_mix=norm_mix, w_in=w_in, b_gate=b_gate, sc_conv_w=sc_conv_w, ssm_conv_w=ssm_conv_w, ssm_conv_b=ssm_conv_b, dt_bias=dt_bias, A_log=A_log, D_skip=D_skip, ssm_norm_w=ssm_norm_w, w_branch_sc=w_branch_sc, w_branch_ssm=w_branch_ssm, w_out=w_out, norm_mlp=norm_mlp, w_mlp1=w_mlp1, w_mlp2=w_mlp2, norm_final=norm_final,
                 m_norm_mix=m_norm_mix, m_w_in=m_w_in, m_b_gate=m_b_gate, m_sc_conv_w=m_sc_conv_w, m_ssm_conv_w=m_ssm_conv_w, m_ssm_conv_b=m_ssm_conv_b, m_dt_bias=m_dt_bias, m_A_log=m_A_log, m_D_skip=m_D_skip, m_ssm_norm_w=m_ssm_norm_w, m_w_branch_sc=m_w_branch_sc, m_w_branch_ssm=m_w_branch_ssm, m_w_out=m_w_out, m_norm_mlp=m_norm_mlp, m_w_mlp1=m_w_mlp1, m_w_mlp2=m_w_mlp2, m_norm_final=m_norm_final,
                 v_norm_mix=v_norm_mix, v_w_in=v_w_in, v_b_gate=v_b_gate, v_sc_conv_w=v_sc_conv_w, v_ssm_conv_w=v_ssm_conv_w, v_ssm_conv_b=v_ssm_conv_b, v_dt_bias=v_dt_bias, v_A_log=v_A_log, v_D_skip=v_D_skip, v_ssm_norm_w=v_ssm_norm_w, v_w_branch_sc=v_w_branch_sc, v_w_branch_ssm=v_w_branch_ssm, v_w_out=v_w_out, v_norm_mlp=v_norm_mlp, v_w_mlp1=v_w_mlp1, v_w_mlp2=v_w_mlp2, v_norm_final=v_norm_final)
    order = ["norm_mix", "w_in", "b_gate", "sc_conv_w", "ssm_conv_w", "ssm_conv_b", "dt_bias", "A_log", "D_skip",
             "ssm_norm_w", "w_branch_sc", "w_branch_ssm", "w_out", "norm_mlp", "w_mlp1", "w_mlp2", "norm_final"]
    grad, delta, new_m, new_v = {}, {}, {}, {}
    for n in big:
        delta[n], new_m[n], new_v[n], grad[n] = _adamw(given[n], big[n], given["m_" + n], given["v_" + n],
                                                       "adamw_" + n, copy_g=True)
    big["w_in"] = None
    grad_small = {n: gs[n].reshape(given[n].shape) for n in small_names if n not in ("sc_conv_w", "ssm_conv_w")}
    grad_small["sc_conv_w"] = lax.dynamic_slice(gs["sc_conv_w"].reshape(3, D), (0, 256 * s), (3, 256))
    grad_small["ssm_conv_w"] = lax.dynamic_slice(gs["ssm_conv_w"].reshape(4, XBC), (0, 1024 * s), (4, 1024))
    table = small_table
    ds_, ms_, vs_ = _adamw(pk_w, _pack(grad_small, table, small_rows), pk_m, pk_v, "adamw_small", tr=small_rows)
    ds_, ms_, vs_ = _unpack(ds_, table), _unpack(ms_, table), _unpack(vs_, table)
    for n in grad_small:
        shp = given[n].shape
        grad[n] = grad_small[n]
        delta[n], new_m[n], new_v[n] = ds_[n].reshape(shp), ms_[n].reshape(shp), vs_[n].reshape(shp)

    done = [new_v[n] for n in ("w_mlp1", "w_mlp2", "w_branch_sc", "w_branch_ssm", "w_out")] + [vs_["b_gate"]]
    tok = rs_b.share(after=done)
    offs = jnp.where(s == 3, jnp.array([24, -8, 744, 2072, -8], jnp.int32),
                     jnp.stack([8 * s, 8 * s, 0 * s, 8 * s, 8 * s]).astype(jnp.int32))
    offs = jnp.concatenate([offs, jnp.stack([7 * ci, 4 - 4 * ci]).astype(jnp.int32)])
    nmain = W_SHARD // 256
    wt_own = (w_in.T, rs_b.arr["f0"], m_w_in.T, v_w_in.T, offs)
    res = _adamw_w_in(*wt_own, "adamw_w_in_own", 0, 256, 4, (0, 1), blk_key=5)
    gp = rs_b.result(after=[tok, res[0]])[0]
    wt_args = (w_in.T, gp, m_w_in.T, v_w_in.T, offs)
    res = _adamw_w_in(*wt_args, "adamw_w_in", 0, 256, nmain - 4, (0, 1), into=res, blk_key=6)
    res = _adamw_w_in(*wt_args, "adamw_w_in_dt", 744, 32, 1, (3,), into=res)
    dt_, mt_, vt_, gwt = _adamw_w_in(*wt_args, "adamw_w_in_tail", 256 * nmain, 8, 1, (4,), into=res)
    grad["w_in"], delta["w_in"], new_m["w_in"], new_v["w_in"] = gwt.T, dt_.T, mt_.T, vt_.T
    nm_arr, _, _ = _split_call("norm_mix_wait", nm_arr, wait=_all8_plan("nm"), wait_sems=nm_sems, after=tok)
    g8 = _sum8(nm_arr["nm"], "norm_mix_sum")
    r8 = lambda a: a.reshape(8, 128)
    d8, m8, v8 = _adamw(r8(norm_mix), g8, r8(m_norm_mix), r8(v_norm_mix), "adamw_norm_mix", tr=8)
    grad["norm_mix"], delta["norm_mix"] = g8.reshape(D), d8.reshape(D)
    new_m["norm_mix"], new_v["norm_mix"] = m8.reshape(D), v8.reshape(D)

    loss = gs["loss"].reshape(())
    return (loss, grad_x.reshape(1, L, D), *[grad[n] for n in order], *[delta[n] for n in order],
            *[new_m[n] for n in order], *[new_v[n] for n in order])
```

```python
import functools

import jax
import jax.numpy as jnp
from jax import lax
from jax.experimental import pallas as pl
from jax.experimental.pallas import tpu as pltpu

F32 = jnp.float32
BF16 = jnp.bfloat16
MESH = pl.DeviceIdType.MESH
HBM = pltpu.HBM

D = 1024
INNER = 2048
HD = 64
NH = 32
NG = 8
NS = 128
Q = 128
GPS = 8
XBC = 4096
DFF = 4096
EPS = 1e-6
W_SHARD = 2824
NCW = 11520
PIECE = 3072
PMAIN = 2816
C_Z, C_XBC, C_GATE, C_DT = 3072, 5120, 9216, 11264
SMALL_ROWS = 256
VMEM_LIMIT = 56 * 1024 * 1024

ADAM_LR, ADAM_B1, ADAM_B2, ADAM_EPS, ADAM_WD, ADAM_STEP = 0.001, 0.9, 0.999, 1e-08, 0.01, 10


VMEM_SMALL = 40 * 1024 * 1024


def _cp(sem=None, vmem=VMEM_SMALL):
    return pltpu.CompilerParams(dimension_semantics=sem, vmem_limit_bytes=vmem)


def _sigmoid(v):
    return 1.0 / (1.0 + jnp.exp(-v))


_DIMS = {"nn": (((1,), (0,)), ((), ())), "nt": (((1,), (1,)), ((), ())), "tn": (((0,), (0,)), ((), ()))}


def _matmul(a, b, mode, out_dtype, tm, tn, tk, name, epi=None, extra=None, n_outer=False, dep=None):
    if mode == "tn":
        K, M = a.shape
    else:
        M, K = a.shape
    N = b.shape[0] if mode == "nt" else b.shape[1]
    tm, tn, tk = min(tm, M), min(tn, N), min(tk, K)
    assert M % tm == 0 and N % tn == 0 and K % tk == 0, (name, M, N, K, tm, tn, tk)
    nm, nn, nk = M // tm, N // tn, K // tk
    dims = _DIMS[mode]

    def ij(p0, p1):
        return (p1, p0) if n_outer else (p0, p1)

    if mode == "tn":
        a_spec = pl.BlockSpec((tk, tm), lambda p0, p1, k: (k, ij(p0, p1)[0]))
    else:
        a_spec = pl.BlockSpec((tm, tk), lambda p0, p1, k: (ij(p0, p1)[0], k))
    if mode == "nt":
        b_spec = pl.BlockSpec((tn, tk), lambda p0, p1, k: (ij(p0, p1)[1], k))
    else:
        b_spec = pl.BlockSpec((tk, tn), lambda p0, p1, k: (k, ij(p0, p1)[1]))
    o_spec = pl.BlockSpec((tm, tn), lambda p0, p1, k: ij(p0, p1))
    in_specs = [a_spec, b_spec]
    args = [a, b]
    if epi in ("res", "drelu"):
        in_specs.append(o_spec)
        args.append(extra)
    if dep is not None:
        in_specs.append(pl.BlockSpec(memory_space=pl.ANY))
        args.append(dep)
    n_in = len(args)
    if epi == "relu2":
        out_shape = (jax.ShapeDtypeStruct((M, N), out_dtype), jax.ShapeDtypeStruct((M, N), BF16))
        out_specs = (o_spec, o_spec)
    else:
        out_shape = jax.ShapeDtypeStruct((M, N), out_dtype)
        out_specs = o_spec

    def kern(*refs):
        a_ref, b_ref = refs[0], refs[1]
        e_ref = refs[2] if epi in ("res", "drelu") else None
        acc = refs[-1]
        outs = refs[n_in:-1] if nk > 1 else refs[n_in:]
        k = pl.program_id(2)

        def product():
            return lax.dot_general(a_ref[...].astype(BF16), b_ref[...].astype(BF16), dims, preferred_element_type=F32)

        def finish(r):
            if epi is None:
                outs[0][...] = r.astype(out_dtype)
            elif epi == "res":
                outs[0][...] = (r + e_ref[...]).astype(out_dtype)
            elif epi == "relu2":
                outs[0][...] = r.astype(out_dtype)
                t = jnp.maximum(r, 0.0)
                outs[1][...] = (t * t).astype(BF16)
            else:
                outs[0][...] = (r * (2.0 * jnp.maximum(e_ref[...].astype(F32), 0.0))).astype(out_dtype)

        if nk == 1:
            finish(product())
        else:
            @pl.when(k == 0)
            def _():
                acc[...] = jnp.zeros_like(acc)

            acc[...] += product()

            @pl.when(k == nk - 1)
            def _():
                finish(acc[...])

    grid = (nn, nm, nk) if n_outer else (nm, nn, nk)
    return pl.pallas_call(
        kern, grid=grid, in_specs=in_specs, out_specs=out_specs, out_shape=out_shape,
        scratch_shapes=[pltpu.VMEM((tm, tn), F32)] if nk > 1 else [], name=name,
        compiler_params=_cp(("parallel", "parallel", "arbitrary")),
    )(*args)


def _matmul_res_rms(a, b, x, w, tm, name):
    M, K = a.shape
    tm = min(tm, M)

    def kern(a_ref, b_ref, x_ref, w_ref, x1_ref, h_ref):
        x1 = x_ref[...] + lax.dot_general(a_ref[...].astype(BF16), b_ref[...].astype(BF16), _DIMS["nn"],
                                          preferred_element_type=F32)
        x1_ref[...] = x1
        r = lax.rsqrt(jnp.mean(x1 * x1, axis=-1, keepdims=True) + EPS)
        h_ref[...] = ((x1 * r) * w_ref[...]).astype(BF16)

    row = pl.BlockSpec((tm, D), lambda i: (i, 0))
    return pl.pallas_call(
        kern, grid=(M // tm,),
        in_specs=[pl.BlockSpec((tm, K), lambda i: (i, 0)), pl.BlockSpec((K, D), lambda i: (0, 0)), row,
                  pl.BlockSpec((1, D), lambda i: (0, 0))],
        out_specs=(row, row), out_shape=(jax.ShapeDtypeStruct((M, D), F32), jax.ShapeDtypeStruct((M, D), BF16)),
        name=name, compiler_params=_cp(("parallel",)),
    )(a, b, x, w.reshape(1, D))


def _matmul_res_final(a, b, x, w, tgt, tm, name):
    M, K = a.shape
    tm = min(tm, M)

    def kern(a_ref, b_ref, x_ref, w_ref, t_ref, dx_ref, gw_ref, loss_ref):
        @pl.when(pl.program_id(0) == 0)
        def _():
            gw_ref[...] = jnp.zeros_like(gw_ref)
            loss_ref[...] = jnp.zeros_like(loss_ref)

        xv = x_ref[...] + lax.dot_general(a_ref[...].astype(BF16), b_ref[...].astype(BF16), _DIMS["nn"],
                                          preferred_element_type=F32)
        r = lax.rsqrt(jnp.mean(xv * xv, axis=-1, keepdims=True) + EPS)
        xn = xv * r
        e = xn * w_ref[...] - t_ref[...]
        loss_ref[...] += 0.5 * jnp.sum(jnp.mean(e * e, axis=-1, keepdims=True))
        dyv = e * (1.0 / D)
        gw_ref[...] += jnp.broadcast_to(jnp.sum(dyv * xn, axis=0, keepdims=True), (8, D))
        dxn = dyv * w_ref[...]
        dx_ref[...] = r * (dxn - xn * jnp.mean(dxn * xn, axis=-1, keepdims=True))

    row = pl.BlockSpec((tm, D), lambda i: (i, 0))
    return pl.pallas_call(
        kern, grid=(M // tm,),
        in_specs=[pl.BlockSpec((tm, K), lambda i: (i, 0)), pl.BlockSpec((K, D), lambda i: (0, 0)), row,
                  pl.BlockSpec((1, D), lambda i: (0, 0)), row],
        out_specs=(row, pl.BlockSpec((8, D), lambda i: (0, 0)), pl.BlockSpec((8, 128), lambda i: (0, 0))),
        out_shape=(jax.ShapeDtypeStruct((M, D), F32), jax.ShapeDtypeStruct((8, D), F32),
                   jax.ShapeDtypeStruct((8, 128), F32)),
        name=name, compiler_params=_cp(("arbitrary",), VMEM_LIMIT),
    )(a, b, x, w.reshape(1, D), tgt)


def _matmul_rms_bwd(a, b, mode, x, w, res, tm, tk, name, dep=None):
    M, K = a.shape
    tm, tk = min(tm, M), min(tk, K)
    nk = K // tk
    assert M % tm == 0 and K % tk == 0
    b_spec = (pl.BlockSpec((tk, D), lambda k, i: (k, 0)) if mode == "nn" else pl.BlockSpec((D, tk), lambda k, i: (0, k)))
    row = pl.BlockSpec((tm, D), lambda k, i: (jnp.where(k == nk - 1, i, 0), 0))
    deps = [] if dep is None else [dep]

    def kern(a_ref, b_ref, x_ref, w_ref, res_ref, *rest):
        dx_ref, gw_ref, acc = rest[-3:]
        k, i = pl.program_id(0), pl.program_id(1)

        @pl.when((i == 0) & (k == 0))
        def _():
            gw_ref[...] = jnp.zeros_like(gw_ref)

        def product():
            return lax.dot_general(a_ref[...].astype(BF16), b_ref[...].astype(BF16), _DIMS[mode],
                                   preferred_element_type=F32)

        def finish(dyv):
            xv = x_ref[...]
            r = lax.rsqrt(jnp.mean(xv * xv, axis=-1, keepdims=True) + EPS)
            xn = xv * r
            gw_ref[...] += jnp.broadcast_to(jnp.sum(dyv * xn, axis=0, keepdims=True), (8, D))
            dxn = dyv * w_ref[...]
            dx_ref[...] = res_ref[...] + r * (dxn - xn * jnp.mean(dxn * xn, axis=-1, keepdims=True))

        if nk == 1:
            finish(product())
        else:
            rows = pl.ds(pl.multiple_of(i * tm, tm), tm)

            @pl.when(k == 0)
            def _():
                acc[rows, :] = jnp.zeros((tm, D), F32)

            acc[rows, :] += product()

            @pl.when(k == nk - 1)
            def _():
                finish(acc[rows, :])

    return pl.pallas_call(
        kern, grid=(nk, M // tm),
        in_specs=[pl.BlockSpec((tm, tk), lambda k, i: (i, k)), b_spec, row, pl.BlockSpec((1, D), lambda k, i: (0, 0)),
                  row] + [pl.BlockSpec(memory_space=pl.ANY)] * len(deps),
        out_specs=(row, pl.BlockSpec((8, D), lambda k, i: (0, 0))),
        out_shape=(jax.ShapeDtypeStruct((M, D), F32), jax.ShapeDtypeStruct((8, D), F32)),
        scratch_shapes=[pltpu.VMEM((M, D) if nk > 1 else (8, 128), F32)], name=name,
        compiler_params=_cp(("arbitrary", "arbitrary"), VMEM_LIMIT),
    )(a, b, x, w.reshape(1, D), res, *deps)


def _rms_fwd(x, w, name, tl=256, dep=None):
    L = x.shape[0]

    def kern(x_ref, w_ref, *rest):
        o_ref = rest[-1]
        xv = x_ref[...]
        r = lax.rsqrt(jnp.mean(xv * xv, axis=-1, keepdims=True) + EPS)
        o_ref[...] = ((xv * r) * w_ref[...]).astype(BF16)

    row = pl.BlockSpec((tl, D), lambda i: (i, 0))
    deps = [] if dep is None else [dep]
    return pl.pallas_call(
        kern, grid=(L // tl,),
        in_specs=[row, pl.BlockSpec((1, D), lambda i: (0, 0))] + [pl.BlockSpec(memory_space=pl.ANY)] * len(deps),
        out_specs=row, out_shape=jax.ShapeDtypeStruct((L, D), BF16), name=name, compiler_params=_cp(("parallel",)),
    )(x, w.reshape(1, D), *deps)


def _down(v, k):
    if k == 0:
        return v
    t = lax.broadcasted_iota(jnp.int32, v.shape, 0)
    return jnp.where(t >= k, pltpu.roll(v, k, axis=0), 0.0)


def _up(v, k):
    if k == 0:
        return v
    n = v.shape[0]
    t = lax.broadcasted_iota(jnp.int32, v.shape, 0)
    return jnp.where(t < n - k, pltpu.roll(v, n - k, axis=0), 0.0)


TW = 256


def _sc_fwd(proj, cw):
    L = proj.shape[0]
    nb = D // TW

    def kern(b_ref, c_ref, x_ref, w_ref, o_ref):
        u = c_ref[...].astype(F32) * x_ref[...].astype(F32)
        w = w_ref[...]
        cv = w[0:1] * _down(u, 2) + w[1:2] * _down(u, 1) + w[2:3] * u
        o_ref[...] = (b_ref[...].astype(F32) * cv).astype(BF16)

    col = lambda off: pl.BlockSpec((L, TW), lambda j: (0, off + j))
    return pl.pallas_call(
        kern, grid=(nb,), in_specs=[col(0), col(nb), col(2 * nb), pl.BlockSpec((8, TW), lambda j: (0, j))],
        out_specs=pl.BlockSpec((L, TW), lambda j: (0, j)), out_shape=jax.ShapeDtypeStruct((L, D), BF16),
        name="sc_fwd", compiler_params=_cp(("parallel",)),
    )(proj, proj, proj, cw)


def _sc_bwd(dya, proj, cw, dproj):
    L = proj.shape[0]
    nb = D // TW

    def kern(d_ref, b_ref, c_ref, x_ref, w_ref, _, dp_ref, gw_ref, keep):
        sec = pl.program_id(1)

        @pl.when(sec == 0)
        def _():
            cs, xs, dyv = c_ref[...].astype(F32), x_ref[...].astype(F32), d_ref[...]
            w = w_ref[...]
            u = cs * xs
            u1, u2 = _down(u, 1), _down(u, 2)
            cv = w[0:1] * u2 + w[1:2] * u1 + w[2:3] * u
            dcv = dyv * b_ref[...].astype(F32)
            du = w[2:3] * dcv + w[1:2] * _up(dcv, 1) + w[0:1] * _up(dcv, 2)
            g0 = jnp.sum(dcv * u2, axis=0, keepdims=True)
            g1 = jnp.sum(dcv * u1, axis=0, keepdims=True)
            g2 = jnp.sum(dcv * u, axis=0, keepdims=True)
            row = lax.broadcasted_iota(jnp.int32, (8, TW), 0)
            gw_ref[...] = jnp.where(row == 0, g0, jnp.where(row == 1, g1, jnp.where(row == 2, g2, 0.0)))
            dp_ref[...] = (dyv * cv).astype(BF16)
            keep[0] = (du * xs).astype(BF16)
            keep[1] = (du * cs).astype(BF16)

        @pl.when(sec > 0)
        def _():
            dp_ref[...] = keep[sec - 1]

    col = lambda off: pl.BlockSpec((L, TW), lambda j, s: (0, off + j))
    return pl.pallas_call(
        kern, grid=(nb, 3),
        in_specs=[col(0), col(0), col(nb), col(2 * nb), pl.BlockSpec((8, TW), lambda j, s: (0, j)),
                  pl.BlockSpec(memory_space=pl.ANY)],
        out_specs=(pl.BlockSpec((L, TW), lambda j, s: (0, s * nb + j)), pl.BlockSpec((8, TW), lambda j, s: (0, j))),
        out_shape=(jax.ShapeDtypeStruct(dproj.shape, BF16), jax.ShapeDtypeStruct((8, D), F32)),
        scratch_shapes=[pltpu.VMEM((2, L, TW), BF16)],
        input_output_aliases={5: 0}, name="sc_bwd", compiler_params=_cp(("parallel", "arbitrary")),
    )(dya, proj, proj, proj, cw, dproj)


def _ssm_conv_fwd(proj, cw4):
    L = proj.shape[0]
    off = C_XBC // TW

    def kern(r_ref, w_ref, o_ref):
        raw = r_ref[...].astype(F32)
        w = w_ref[...]
        c4 = w[0:1] * _down(raw, 3) + w[1:2] * _down(raw, 2) + w[2:3] * _down(raw, 1) + w[3:4] * raw + w[4:5]
        o_ref[...] = c4 * _sigmoid(c4)

    return pl.pallas_call(
        kern, grid=(XBC // TW,),
        in_specs=[pl.BlockSpec((L, TW), lambda j: (0, off + j)), pl.BlockSpec((8, TW), lambda j: (0, j))],
        out_specs=pl.BlockSpec((L, TW), lambda j: (0, j)), out_shape=jax.ShapeDtypeStruct((L, XBC), F32),
        name="ssm_conv_fwd", compiler_params=_cp(("parallel",)),
    )(proj, cw4)


def _ssm_conv_bwd(dx, proj, cw4, dproj, col0, name):
    L, width = dx.shape
    off_p = (C_XBC + col0) // TW
    off_w = col0 // TW

    def kern(d_ref, r_ref, w_ref, _, dp_ref, gw_ref):
        raw = r_ref[...].astype(F32)
        w = w_ref[...]
        r1, r2, r3 = _down(raw, 1), _down(raw, 2), _down(raw, 3)
        c4 = w[0:1] * r3 + w[1:2] * r2 + w[2:3] * r1 + w[3:4] * raw + w[4:5]
        sg = _sigmoid(c4)
        dc4 = d_ref[...] * (sg * (1.0 + c4 * (1.0 - sg)))
        draw = w[3:4] * dc4 + w[2:3] * _up(dc4, 1) + w[1:2] * _up(dc4, 2) + w[0:1] * _up(dc4, 3)
        dp_ref[...] = draw.astype(BF16)
        gs = [jnp.sum(dc4 * r3, axis=0, keepdims=True), jnp.sum(dc4 * r2, axis=0, keepdims=True),
              jnp.sum(dc4 * r1, axis=0, keepdims=True), jnp.sum(dc4 * raw, axis=0, keepdims=True),
              jnp.sum(dc4, axis=0, keepdims=True)]
        row = lax.broadcasted_iota(jnp.int32, (8, TW), 0)
        acc = jnp.zeros((8, TW), F32)
        for k, gk in enumerate(gs):
            acc = jnp.where(row == k, gk, acc)
        gw_ref[...] = acc

    return pl.pallas_call(
        kern, grid=(width // TW,),
        in_specs=[pl.BlockSpec((L, TW), lambda j: (0, j)), pl.BlockSpec((L, TW), lambda j: (0, off_p + j)),
                  pl.BlockSpec((8, TW), lambda j: (0, off_w + j)), pl.BlockSpec(memory_space=pl.ANY)],
        out_specs=(pl.BlockSpec((L, TW), lambda j: (0, off_p + j)), pl.BlockSpec((8, TW), lambda j: (0, j))),
        out_shape=(jax.ShapeDtypeStruct(dproj.shape, BF16), jax.ShapeDtypeStruct((8, width), F32)),
        input_output_aliases={3: 0}, name=name, compiler_params=_cp(("arbitrary",)),
    )(dx, proj, cw4, dproj)


def _split3(v):
    h1 = v.astype(BF16)
    r1 = v - h1.astype(F32)
    h2 = r1.astype(BF16)
    h3 = (r1 - h2.astype(F32)).astype(BF16)
    return h1, h2, h3


def _dot01(m01, v, dims=_DIMS["nn"], m_left=True, terms=3):
    out = None
    for part in _split3(v)[:terms]:
        ops = (m01, part) if m_left else (part, m01)
        t = lax.dot_general(ops[0], ops[1], dims, preferred_element_type=F32)
        out = t if out is None else out + t
    return out


def _bdot(a, b, mode="nn"):
    return lax.dot_general(a.astype(BF16), b.astype(BF16), _DIMS[mode], preferred_element_type=F32)


def _softplus(v):
    return jnp.maximum(v, 0.0) + jnp.log1p(jnp.exp(-jnp.abs(v)))


def _dt_prep(proj, vec):
    L = proj.shape[0]

    def kern(p_ref, v_ref, dt_ref, cs_ref, sg_ref):
        v = v_ref[...]
        pre = p_ref[:, 0:128] + v[0:1]
        dt = _softplus(pre)
        da = dt * (-jnp.exp(v[1:2]))
        ii = lax.broadcasted_iota(jnp.int32, (Q, Q), 0)
        jj = lax.broadcasted_iota(jnp.int32, (Q, Q), 1)
        ltri = (jj <= ii).astype(BF16)
        lane = lax.broadcasted_iota(jnp.int32, (Q, 128), 1)
        for val, ref in ((dt, dt_ref), (_dot01(ltri, da), cs_ref), (_sigmoid(pre), sg_ref)):
            for g in range(NG):
                moved = val if g == 0 else pltpu.roll(val, 128 - 4 * g, axis=1)
                ref[g] = jnp.where(lane < 4, moved, 0.0)

    blk = pl.BlockSpec((NG, Q, 128), lambda c: (0, c, 0))
    return pl.pallas_call(
        kern, grid=(L // Q,),
        in_specs=[pl.BlockSpec((Q, 256), lambda c: (c, 0)), pl.BlockSpec((8, 128), lambda c: (0, 0))],
        out_specs=(blk, blk, blk),
        out_shape=(jax.ShapeDtypeStruct((NG, L, 128), F32),) * 3,
        name="dt_prep", compiler_params=_cp(("parallel",)),
    )(proj, vec)


def _head_masks():
    lane = lax.broadcasted_iota(jnp.int32, (1, 4 * HD), 1)
    return [((lane >= HD * j) & (lane < HD * (j + 1))) for j in range(4)]


def _expand4(v4, masks):
    R = v4.shape[0]
    out = jnp.zeros((R, 4 * HD), F32)
    for j in range(4):
        out = jnp.where(masks[j], jnp.broadcast_to(v4[:, j:j + 1], (R, 4 * HD)), out)
    return out


def _decay_matrix(cs_col, tri):
    colb = jnp.broadcast_to(cs_col, (Q, Q))
    return jnp.exp(jnp.where(tri, colb - colb.T, -jnp.inf))


def _ssd_fwd(xbc, dt4, cs4, vecg):
    L = xbc.shape[0]
    nc = L // Q

    def kern(x_ref, b_ref, c_ref, dt_ref, cs_ref, v_ref, y_ref, s_ref, S):
        c = pl.program_id(1)

        @pl.when(c == 0)
        def _():
            S[...] = jnp.zeros_like(S)

        masks = _head_masks()
        ii = lax.broadcasted_iota(jnp.int32, (Q, Q), 0)
        jj = lax.broadcasted_iota(jnp.int32, (Q, Q), 1)
        tri = jj <= ii
        for gi in range(GPS):
            xs, ns = slice(256 * gi, 256 * (gi + 1)), slice(NS * gi, NS * (gi + 1))
            dt4v, cs4v = dt_ref[gi], cs_ref[gi]
            dt_b, cs_b = _expand4(dt4v, masks), _expand4(cs4v, masks)
            d_b = _expand4(v_ref[gi], masks)[1:2]
            cs_last = cs_b[Q - 1:Q, :]
            x4, bm, cm = x_ref[:, xs], b_ref[:, ns], c_ref[:, ns]
            xdt = x4 * dt_b
            gm = _bdot(cm, bm, "nt")
            s4 = S[gi]
            s_ref[gi, 0] = s4
            y = _bdot(cm, s4) * jnp.exp(cs_b) + d_b * x4
            m_all = jnp.concatenate([(gm * _decay_matrix(cs4v[:, j:j + 1], tri)).astype(BF16) for j in range(4)], axis=0)
            yd = _bdot(m_all, xdt)
            for j in range(4):
                y = y + jnp.where(masks[j], yd[Q * j:Q * (j + 1)], 0.0)
            y_ref[:, xs] = y
            S[gi] = jnp.exp(cs_last) * s4 + _bdot(bm, xdt * jnp.exp(cs_last - cs_b), "tn")

    sc = pl.BlockSpec((GPS, Q, 128), lambda g, c: (g, c, 0))
    bw = NS * GPS
    return pl.pallas_call(
        kern, grid=(NG // GPS, nc),
        in_specs=[pl.BlockSpec((Q, 256 * GPS), lambda g, c: (c, g)),
                  pl.BlockSpec((Q, bw), lambda g, c: (c, INNER // bw + g)),
                  pl.BlockSpec((Q, bw), lambda g, c: (c, (INNER + NG * NS) // bw + g)),
                  sc, sc, pl.BlockSpec((GPS, 8, 128), lambda g, c: (g, 0, 0))],
        out_specs=(pl.BlockSpec((Q, 256 * GPS), lambda g, c: (c, g)),
                   pl.BlockSpec((GPS, 1, NS, 256), lambda g, c: (g, c, 0, 0))),
        out_shape=(jax.ShapeDtypeStruct((L, INNER), F32), jax.ShapeDtypeStruct((NG, nc, NS, 256), F32)),
        scratch_shapes=[pltpu.VMEM((GPS, NS, 256), F32)], name="ssd_fwd",
        compiler_params=_cp(("parallel", "arbitrary")),
    )(xbc, xbc, xbc, dt4, cs4, vecg)


def _ssd_bwd(xbc, dt4, cs4, sg4, vecg, s_all, dy):
    L = xbc.shape[0]
    nc = L // Q

    def kern(x_ref, b_ref, c_ref, dt_ref, cs_ref, sg_ref, v_ref, s_ref, dy_ref,
             dx_ref, db_ref, dc_ref, ddt_ref, st_ref, dS):
        cc = pl.program_id(1)

        @pl.when(cc == 0)
        def _():
            dS[...] = jnp.zeros_like(dS)
            st_ref[...] = jnp.zeros_like(st_ref)

        masks = _head_masks()
        ii = lax.broadcasted_iota(jnp.int32, (Q, Q), 0)
        jj = lax.broadcasted_iota(jnp.int32, (Q, Q), 1)
        tri = jj <= ii
        utri = (jj >= ii).astype(BF16)
        hsel = ((lax.broadcasted_iota(jnp.int32, (4 * HD, 128), 0) // HD)
                == lax.broadcasted_iota(jnp.int32, (4 * HD, 128), 1)).astype(BF16)
        hrow = ((lax.broadcasted_iota(jnp.int32, (4 * Q, 128), 0) // Q)
                == lax.broadcasted_iota(jnp.int32, (4 * Q, 128), 1)).astype(BF16)
        ones_q = jnp.ones((Q, 128), BF16)
        lane128 = lax.broadcasted_iota(jnp.int32, (Q, 128), 1)

        for gi in range(GPS):
            xs, ns = slice(256 * gi, 256 * (gi + 1)), slice(NS * gi, NS * (gi + 1))
            dt4v, cs4v, sg4v = dt_ref[gi], cs_ref[gi], sg_ref[gi]
            dt_b, cs_b = _expand4(dt4v, masks), _expand4(cs4v, masks)
            vv = _expand4(v_ref[gi], masks)
            a_b = -jnp.exp(vv[0:1])
            d_b = vv[1:2]
            a4 = -jnp.exp(v_ref[gi][0:1, :])
            cs_last = cs_b[Q - 1:Q, :]
            ecs = jnp.exp(cs_b)
            decay = jnp.exp(cs_last - cs_b)
            elast = jnp.exp(cs_last)
            x4, bm, cm, dyv = x_ref[:, xs], b_ref[:, ns], c_ref[:, ns], dy_ref[:, xs]
            s4 = s_ref[gi, 0]
            dsn = dS[gi]
            xdt = x4 * dt_b
            gm = _bdot(cm, bm, "nt")
            dye = dyv * ecs
            yoff = ecs * _bdot(cm, s4)
            t4 = _bdot(bm, dsn) * decay
            lms, mhs = [], []
            for j in range(4):
                colb = jnp.broadcast_to(cs4v[:, j:j + 1], (Q, Q))
                lms.append(jnp.exp(jnp.where(tri, colb - colb.T, -jnp.inf)))
                mhs.append(gm * lms[j])
            m_all = jnp.concatenate([m.astype(BF16) for m in mhs], axis=0)
            dy_m = jnp.concatenate([jnp.where(masks[j], dyv, 0.0).astype(BF16) for j in range(4)], axis=0)
            dxdt = t4 + _bdot(m_all, dy_m, "tn")
            dm_all = _bdot(dy_m, xdt, "nt")
            dg = jnp.zeros((Q, Q), F32)
            for j in range(4):
                dg = dg + dm_all[Q * j:Q * (j + 1)] * lms[j]
            e_all = dm_all * jnp.concatenate(mhs, axis=0)
            rsum = _dot01(ones_q, e_all, m_left=False, terms=2)
            da4 = -_dot01(hrow, e_all, _DIMS["tn"], m_left=False, terms=2)
            for j in range(4):
                da4 = da4 + jnp.where(lane128 == j, rsum[Q * j:Q * (j + 1)], 0.0)
            xt = xdt * t4
            tail = jnp.sum(xt, axis=0, keepdims=True) + elast * jnp.sum(s4 * dsn, axis=0, keepdims=True)
            gd_raw = jnp.sum(dyv * x4, axis=0, keepdims=True)
            stacked = jnp.concatenate([dyv * yoff - xt, dxdt * x4, jnp.broadcast_to(tail, (8, 4 * HD)),
                                       jnp.broadcast_to(gd_raw, (8, 4 * HD))], axis=0)
            seg = _dot01(hsel, stacked, m_left=False, terms=2)
            dda4 = _dot01(utri, da4 + seg[0:Q], terms=2) + seg[2 * Q:2 * Q + 1]
            ddt_ref[gi] = (dda4 * a4 + seg[Q:2 * Q]) * sg4v
            ga = jnp.sum(dda4 * dt4v * a4, axis=0, keepdims=True)
            row = lax.broadcasted_iota(jnp.int32, (8, 128), 0)
            st_ref[gi] += jnp.where(row == 0, ga, jnp.where(row == 1, seg[2 * Q + 8:2 * Q + 9], 0.0))
            dx_ref[:, xs] = d_b * dyv + dxdt * dt_b
            dc_ref[:, ns] = _bdot(dg, bm) + _bdot(dye, s4, "nt")
            db_ref[:, ns] = _bdot(dg, cm, "tn") + _bdot(xdt * decay, dsn, "nt")
            dS[gi] = elast * dsn + _bdot(cm, dye, "tn")

    rv = lambda c: nc - 1 - c
    sc = pl.BlockSpec((GPS, Q, 128), lambda g, c: (g, rv(c), 0))
    bw = NS * GPS
    return pl.pallas_call(
        kern, grid=(NG // GPS, nc),
        in_specs=[pl.BlockSpec((Q, 256 * GPS), lambda g, c: (rv(c), g)),
                  pl.BlockSpec((Q, bw), lambda g, c: (rv(c), INNER // bw + g)),
                  pl.BlockSpec((Q, bw), lambda g, c: (rv(c), (INNER + NG * NS) // bw + g)),
                  sc, sc, sc, pl.BlockSpec((GPS, 8, 128), lambda g, c: (g, 0, 0)),
                  pl.BlockSpec((GPS, 1, NS, 256), lambda g, c: (g, rv(c), 0, 0)),
                  pl.BlockSpec((Q, 256 * GPS), lambda g, c: (rv(c), g))],
        out_specs=(pl.BlockSpec((Q, 256 * GPS), lambda g, c: (rv(c), g)),
                   pl.BlockSpec((Q, bw), lambda g, c: (rv(c), g)),
                   pl.BlockSpec((Q, bw), lambda g, c: (rv(c), g)),
                   pl.BlockSpec((GPS, Q, 128), lambda g, c: (g, rv(c), 0)),
                   pl.BlockSpec((GPS, 8, 128), lambda g, c: (g, 0, 0))),
        out_shape=(jax.ShapeDtypeStruct((L, INNER), F32), jax.ShapeDtypeStruct((L, NG * NS), F32),
                   jax.ShapeDtypeStruct((L, NG * NS), F32), jax.ShapeDtypeStruct((NG, L, 128), F32),
                   jax.ShapeDtypeStruct((NG, 8, 128), F32)),
        scratch_shapes=[pltpu.VMEM((GPS, NS, 256), F32)], name="ssd_bwd",
        compiler_params=_cp(("parallel", "arbitrary")),
    )(xbc, xbc, xbc, dt4, cs4, sg4, vecg, s_all, dy)


def _dt_bwd(ddt, dproj, tl=256):
    L = ddt.shape[1]

    def kern(d_ref, _, dp_ref, gs_ref):
        @pl.when(pl.program_id(0) == 0)
        def _():
            gs_ref[...] = jnp.zeros_like(gs_ref)

        d = d_ref[0]
        for g in range(1, NG):
            d = d + pltpu.roll(d_ref[g], 4 * g, axis=1)
        gs_ref[...] += jnp.broadcast_to(jnp.sum(d, axis=0, keepdims=True), (8, 128))
        dp_ref[...] = jnp.concatenate([d, jnp.zeros_like(d)], axis=1).astype(BF16)

    return pl.pallas_call(
        kern, grid=(L // tl,),
        in_specs=[pl.BlockSpec((NG, tl, 128), lambda i: (0, i, 0)), pl.BlockSpec(memory_space=pl.ANY)],
        out_specs=(pl.BlockSpec((tl, 256), lambda i: (i, C_DT // 256)), pl.BlockSpec((8, 128), lambda i: (0, 0))),
        out_shape=(jax.ShapeDtypeStruct(dproj.shape, BF16), jax.ShapeDtypeStruct((8, 128), F32)),
        input_output_aliases={1: 0}, name="dt_bwd", compiler_params=_cp(("arbitrary",)),
    )(ddt, dproj)


GW = INNER // NG


def _gnorm_fwd(y, proj, w, tl=256):
    L = y.shape[0]
    zoff = C_Z // 1024

    def kern(y_ref, z_ref, w_ref, o_ref):
        z = z_ref[...].astype(F32)
        yz = y_ref[...] * (z * _sigmoid(z))
        wv = w_ref[...]
        for k in range(1024 // GW):
            sl = slice(GW * k, GW * (k + 1))
            v = yz[:, sl]
            rg = lax.rsqrt(jnp.mean(v * v, axis=-1, keepdims=True) + EPS)
            o_ref[:, sl] = ((v * rg) * wv[:, sl]).astype(BF16)

    blk = pl.BlockSpec((tl, 1024), lambda i, j: (i, j))
    return pl.pallas_call(
        kern, grid=(L // tl, 2),
        in_specs=[blk, pl.BlockSpec((tl, 1024), lambda i, j: (i, zoff + j)), pl.BlockSpec((1, 1024), lambda i, j: (0, j))],
        out_specs=blk, out_shape=jax.ShapeDtypeStruct((L, INNER), BF16), name="gnorm_fwd",
        compiler_params=_cp(("parallel", "parallel")),
    )(y, proj, w.reshape(1, INNER))


def _gnorm_bwd(dbr, wb, y, proj, w, dproj, dep, tl=512):
    L = y.shape[0]
    tl = min(tl, L)
    zoff = C_Z // 1024

    def kern(d_ref, b_ref, y_ref, z_ref, w_ref, _, __, dy_ref, dp_ref, gw_ref):
        @pl.when(pl.program_id(1) == 0)
        def _():
            gw_ref[...] = jnp.zeros_like(gw_ref)

        z = z_ref[...].astype(F32)
        sg = _sigmoid(z)
        sz = z * sg
        yv = y_ref[...]
        yz = yv * sz
        dv = lax.dot_general(d_ref[0], b_ref[...], _DIMS["nt"], preferred_element_type=F32)
        wv = w_ref[...]
        for k in range(1024 // GW):
            sl = slice(GW * k, GW * (k + 1))
            v = yz[:, sl]
            rg = lax.rsqrt(jnp.mean(v * v, axis=-1, keepdims=True) + EPS)
            vn = v * rg
            dk = dv[:, sl]
            gw_ref[:, sl] += jnp.broadcast_to(jnp.sum(dk * vn, axis=0, keepdims=True), (8, GW))
            dvn = dk * wv[:, sl]
            dyz = rg * (dvn - vn * jnp.mean(dvn * vn, axis=-1, keepdims=True))
            dy_ref[:, sl] = dyz * sz[:, sl]
            dp_ref[:, sl] = (dyz * yv[:, sl] * (sg[:, sl] * (1.0 + z[:, sl] * (1.0 - sg[:, sl])))).astype(BF16)

    blk = pl.BlockSpec((tl, 1024), lambda j, i: (i, j))
    zblk = pl.BlockSpec((tl, 1024), lambda j, i: (i, zoff + j))
    return pl.pallas_call(
        kern, grid=(2, L // tl),
        in_specs=[pl.BlockSpec((1, tl, D), lambda j, i: (1, i, 0)), pl.BlockSpec((1024, D), lambda j, i: (j, 0)),
                  blk, zblk, pl.BlockSpec((1, 1024), lambda j, i: (0, j)), pl.BlockSpec(memory_space=pl.ANY),
                  pl.BlockSpec(memory_space=pl.ANY)],
        out_specs=(blk, zblk, pl.BlockSpec((8, 1024), lambda j, i: (0, j))),
        out_shape=(jax.ShapeDtypeStruct((L, INNER), F32), jax.ShapeDtypeStruct(dproj.shape, BF16),
                   jax.ShapeDtypeStruct((8, INNER), F32)),
        input_output_aliases={5: 1}, name="gnorm_bwd", compiler_params=_cp(("parallel", "arbitrary")),
    )(dbr, wb, y, proj, w.reshape(1, INNER), dproj, dep)


def _merge_fwd(proj, bg, br_a, br_b, tl=256):
    L = proj.shape[0]
    goff = C_GATE // 1024

    def kern(g1_ref, g2_ref, b1_ref, b2_ref, a_ref, b_ref, o_ref):
        g1 = _sigmoid(g1_ref[...].astype(F32) + b1_ref[...])
        g2 = _sigmoid(g2_ref[...].astype(F32) + b2_ref[...])
        o_ref[...] = (g1 * a_ref[...] + g2 * b_ref[...]).astype(BF16)

    row = pl.BlockSpec((tl, 1024), lambda i: (i, 0))
    bg2 = bg.reshape(1, 2 * D)
    return pl.pallas_call(
        kern, grid=(L // tl,),
        in_specs=[pl.BlockSpec((tl, 1024), lambda i: (i, goff)), pl.BlockSpec((tl, 1024), lambda i: (i, goff + 1)),
                  pl.BlockSpec((1, 1024), lambda i: (0, 0)), pl.BlockSpec((1, 1024), lambda i: (0, 1)), row, row],
        out_specs=row, out_shape=jax.ShapeDtypeStruct((L, D), BF16), name="merge_fwd",
        compiler_params=_cp(("parallel",)),
    )(proj, proj, bg2, bg2, br_a, br_b)


def _branch_ssm_merge(yb, wb, proj, bg, br_a, tm=512):
    L, K = yb.shape
    tm = min(tm, L)
    goff = C_GATE // 1024

    def kern(a_ref, b_ref, g1_ref, g2_ref, b1_ref, b2_ref, bra_ref, brb_ref, m_ref):
        brb = lax.dot_general(a_ref[...], b_ref[...], _DIMS["nn"], preferred_element_type=F32)
        brb_ref[...] = brb
        g1 = _sigmoid(g1_ref[...].astype(F32) + b1_ref[...])
        g2 = _sigmoid(g2_ref[...].astype(F32) + b2_ref[...])
        m_ref[...] = (g1 * bra_ref[...] + g2 * brb).astype(BF16)

    row = pl.BlockSpec((tm, D), lambda i: (i, 0))
    bg2 = bg.reshape(1, 2 * D)
    return pl.pallas_call(
        kern, grid=(L // tm,),
        in_specs=[pl.BlockSpec((tm, K), lambda i: (i, 0)), pl.BlockSpec((K, D), lambda i: (0, 0)),
                  pl.BlockSpec((tm, D), lambda i: (i, goff)), pl.BlockSpec((tm, D), lambda i: (i, goff + 1)),
                  pl.BlockSpec((1, D), lambda i: (0, 0)), pl.BlockSpec((1, D), lambda i: (0, 1)), row],
        out_specs=(row, row), out_shape=(jax.ShapeDtypeStruct((L, D), F32), jax.ShapeDtypeStruct((L, D), BF16)),
        name="branch_ssm_merge", compiler_params=_cp(("parallel",)),
    )(yb, wb, proj, proj, bg2, bg2, br_a)


def _merge_bwd(dx1, wo, proj, bg, br_a, br_b, dproj, tl=512):
    L = proj.shape[0]
    tl = min(tl, L)
    goff = C_GATE // 1024

    def kern(dm_ref, wo_ref, g_ref, b_ref, a_ref, bb_ref, _, dbr_ref, dp_ref, gb_ref):
        j = pl.program_id(0)

        @pl.when(pl.program_id(1) == 0)
        def _():
            gb_ref[...] = jnp.zeros_like(gb_ref)

        g = _sigmoid(g_ref[...].astype(F32) + b_ref[...])
        br = jnp.where(j == 0, a_ref[...], bb_ref[...])
        dmv = lax.dot_general(dm_ref[...].astype(BF16), wo_ref[...], _DIMS["nt"], preferred_element_type=F32)
        dbr_ref[0] = (dmv * g).astype(BF16)
        dgate = dmv * br * g * (1.0 - g)
        gb_ref[...] += jnp.broadcast_to(jnp.sum(dgate, axis=0, keepdims=True), (8, 1024))
        dp_ref[...] = dgate.astype(BF16)

    row = pl.BlockSpec((tl, 1024), lambda j, i: (i, 0))
    gblk = pl.BlockSpec((tl, 1024), lambda j, i: (i, goff + j))
    return pl.pallas_call(
        kern, grid=(2, L // tl),
        in_specs=[row, pl.BlockSpec((D, D), lambda j, i: (0, 0)), gblk, pl.BlockSpec((1, 1024), lambda j, i: (0, j)),
                  row, row, pl.BlockSpec(memory_space=pl.ANY)],
        out_specs=(pl.BlockSpec((1, tl, 1024), lambda j, i: (j, i, 0)), gblk, pl.BlockSpec((8, 1024), lambda j, i: (0, j))),
        out_shape=(jax.ShapeDtypeStruct((2, L, D), BF16), jax.ShapeDtypeStruct(dproj.shape, BF16),
                   jax.ShapeDtypeStruct((8, 2 * D), F32)),
        input_output_aliases={6: 1}, name="merge_bwd", compiler_params=_cp(("parallel", "arbitrary")),
    )(dx1, wo, proj, bg.reshape(1, 2 * D), br_a, br_b, dproj)


def _coords():
    return lax.axis_index("x"), lax.axis_index("y"), lax.axis_index("c")


def _other_chips(sk):
    xk, yk = sk // 2, sk % 2
    return [((1 - xk, yk), 2 * (1 - xk) + yk), ((xk, 1 - yk), 2 * xk + 1 - yk), ((1 - xk, 1 - yk), 2 * (1 - xk) + 1 - yk)]


def _rows(start, size):
    assert size % 128 == 0
    return pl.ds(pl.multiple_of(start, 128), size)


def _per_chip(fn):
    x, y, _ = _coords()
    s = 2 * x + y
    for sk in range(4):
        pl.when(s == sk)(functools.partial(fn, sk))


XTRA = PIECE - PMAIN


def _place(shard, full_shape, block, index_map, idx, name, blk0=0, nblk=None, dep=None, into=None):
    in_block = block[-2:]
    if nblk is None:
        nblk = shard.shape[0] // in_block[0]

    def kern(idx_ref, s_ref, *rest):
        o_ref = rest[-1]
        o_ref[...] = s_ref[...].astype(BF16).reshape(o_ref.shape)

    extra = ([dep] if dep is not None else []) + ([into] if into is not None else [])
    grid_spec = pltpu.PrefetchScalarGridSpec(
        num_scalar_prefetch=1, grid=(nblk,),
        in_specs=[pl.BlockSpec(in_block, lambda i, idx_ref: (blk0 + i, 0))] + [_ANY] * len(extra),
        out_specs=pl.BlockSpec(block, index_map))
    aliases = {1 + len(extra): 0} if into is not None else {}
    return pl.pallas_call(kern, grid_spec=grid_spec, out_shape=jax.ShapeDtypeStruct(full_shape, BF16), name=name,
                          input_output_aliases=aliases, compiler_params=_cp(("arbitrary",)))(idx, shard, *extra)


_SEM = pl.BlockSpec(memory_space=pltpu.SEMAPHORE)
_EFFECT = pltpu.SideEffectType.DATAFLOW_SIDE_EFFECTING


_ANY = pl.BlockSpec(memory_space=pl.ANY)


def _tie(v, dep, name):
    def body(v_ref, dep_ref, o_ref):
        del v_ref, dep_ref, o_ref

    return pl.pallas_call(body, out_shape=jax.ShapeDtypeStruct(v.shape, v.dtype), in_specs=[_ANY, _ANY],
                          out_specs=_ANY, input_output_aliases={0: 0}, name=name)(v, dep)


def _split_call(name, arrays, start=None, wait=None, wait_sems=None, after=None):
    keys = list(arrays)
    n = len(keys)
    n_start = start.n if start is not None else 0
    afters = [] if after is None else (list(after) if isinstance(after, (list, tuple)) else [after])

    def body(*refs):
        pos = n
        if wait is not None:
            wss, wrs = refs[pos], refs[pos + 1]
            pos += 2
        pos += len(afters)
        if start is not None:
            nss, nrs = refs[pos], refs[pos + 1]
            pos += 2
        R = dict(zip(keys, refs[pos:pos + n]))
        token = refs[pos + n]
        x, y, c = _coords()

        def desc(src, dst, dev, ss, rs, k):
            return pltpu.make_async_remote_copy(src_ref=src, dst_ref=dst, send_sem=ss.at[k], recv_sem=rs.at[k],
                                                device_id=dev, device_id_type=MESH)

        def run(sk):
            if wait is not None:
                for k, (snd, land) in enumerate(wait.copies(sk, R)):
                    if snd is not None:
                        desc(snd[0], snd[1], snd[2], wss, wrs, k).wait_send()
                    if land is not None:
                        desc(land, land, (x, y, c), wss, wrs, k).wait_recv()
            if start is not None:
                for k, (snd, land) in enumerate(start.copies(sk, R)):
                    if snd is not None:
                        desc(snd[0], snd[1], snd[2], nss, nrs, k).start()

        _per_chip(run)
        token[...] = jnp.zeros_like(token)

    hbm = pl.BlockSpec(memory_space=HBM)
    vals = [arrays[k] for k in keys]
    ins, in_specs = list(vals), [hbm] * n
    if wait is not None:
        ins += list(wait_sems)
        in_specs += [_SEM, _SEM]
    ins += afters
    in_specs += [pl.BlockSpec(memory_space=pl.ANY)] * len(afters)
    out_shape, out_specs = [], []
    if start is not None:
        out_shape += [pltpu.SemaphoreType.DMA((n_start,)), pltpu.SemaphoreType.DMA((n_start,))]
        out_specs += [_SEM, _SEM]
    first = len(out_shape)
    out_shape += [jax.ShapeDtypeStruct(v.shape, v.dtype) for v in vals] + [jax.ShapeDtypeStruct((8, 128), F32)]
    out_specs += [hbm] * n + [pl.BlockSpec(memory_space=pltpu.VMEM)]
    res = pl.pallas_call(
        body, out_shape=tuple(out_shape), in_specs=in_specs, out_specs=tuple(out_specs),
        input_output_aliases={i: first + i for i in range(n)}, name=name,
        compiler_params=pltpu.CompilerParams(has_side_effects=_EFFECT),
    )(*ins)
    sems = (res[0], res[1]) if start is not None else None
    return dict(zip(keys, res[first:first + n])), sems, res[-1]


class _Plan:
    def __init__(self, n, copies):
        self.n, self.copies = n, copies


_HM, _HX = PMAIN // 2, XTRA // 2
WAVE0 = 768
WAVES = ((0, WAVE0), (WAVE0, _HM - WAVE0))
_WIN = {
    "wq0": (True, "wct", lambda r, sc, hc: r.at[_rows(PMAIN * sc + _HM * hc + WAVES[0][0], WAVES[0][1]), :]),
    "wq1": (True, "wct", lambda r, sc, hc: r.at[_rows(PMAIN * sc + _HM * hc + WAVES[1][0], WAVES[1][1]), :]),
    "xt": (True, "xt", lambda r, sc, hc: r.at[sc, _rows(_HX * hc, _HX), :]),
    "w1": (True, "w1", lambda r, sc, hc: r.at[_rows(512 * hc, 512), pl.ds(1024 * sc, 1024)]),
    "w2": (True, "w2", lambda r, sc, hc: r.at[_rows(1024 * sc + 512 * hc, 512), :]),
    "wa": (True, "wa", lambda r, sc, hc: r.at[_rows(256 * sc + 128 * hc, 128), :]),
    "wb": (True, "wb", lambda r, sc, hc: r.at[_rows(512 * sc + 256 * hc, 256), :]),
    "wo": (True, "wo", lambda r, sc, hc: r.at[_rows(256 * sc + 128 * hc, 128), :]),
    "cw": (False, "cw", lambda r, sc, hc: r.at[sc]),
}


_PIECE_SRC = {
    "wq0": lambda p, hc: p.at[_rows(_HM * hc + WAVES[0][0], WAVES[0][1]), :],
    "wq1": lambda p, hc: p.at[_rows(_HM * hc + WAVES[1][0], WAVES[1][1]), :],
    "xt": lambda p, hc: p.at[_rows(PMAIN + _HX * hc, _HX), :],
}


def _ag_chips_plan(keys):
    def copies(sk, R):
        _, _, c = _coords()
        out = []
        for key in keys:
            _, arr, win = _WIN[key]
            for (px, py), ps in _other_chips(sk):
                dst = win(R[arr], sk, c)
                src = _PIECE_SRC[key](R["piece"], c) if key in _PIECE_SRC else dst
                out.append(((src, dst, (px, py, c)), win(R[arr], ps, c)))
        return out
    return _Plan(3 * len(keys), copies)


def _ag_sibling_plan(keys):
    keys = [k for k in keys if _WIN[k][0]]

    def copies(sk, R):
        x, y, c = _coords()
        out = []
        for key in keys:
            _, arr, win = _WIN[key]
            for _, ps in _other_chips(sk):
                w = win(R[arr], ps, c)
                out.append(((w, w, (x, y, 1 - c)), win(R[arr], ps, 1 - c)))
        return out
    return _Plan(3 * len(keys), copies)


def _in_proj_wave(h, wct, wave, proj=None, tm=2048):
    L = h.shape[0]
    tm = min(tm, L)
    off, size = WAVES[wave]
    start = lambda j: pl.multiple_of(_HM * j + off, 128)

    def kern(h_ref, w_ref, *rest):
        o_ref = rest[-1]
        o_ref[...] = lax.dot_general(h_ref[...], w_ref[...], _DIMS["nt"], preferred_element_type=F32).astype(BF16)

    in_specs = [pl.BlockSpec((tm, D), lambda j, i: (i, 0)),
                pl.BlockSpec((pl.Element(size), pl.Element(D)), lambda j, i: (start(j), 0))]
    args, aliases = [h, wct], {}
    if proj is not None:
        in_specs.append(pl.BlockSpec(memory_space=pl.ANY))
        args.append(proj)
        aliases = {2: 0}
    return pl.pallas_call(
        kern, grid=(8, L // tm), in_specs=in_specs,
        out_specs=pl.BlockSpec((pl.Element(tm), pl.Element(size)), lambda j, i: (i * tm, start(j))),
        out_shape=jax.ShapeDtypeStruct((L, NCW), BF16), input_output_aliases=aliases,
        name="in_proj_wave%d" % wave, compiler_params=_cp(("parallel", "parallel")),
    )(*args)


def _fix_wct(wct, xt):
    nb = PMAIN // XTRA

    def kern(w_ref, x_ref, o_ref):
        k = pl.program_id(0)
        xv = x_ref[0]
        o_ref[...] = jnp.where(k < 3, (w_ref[...].astype(F32) + xv.astype(F32)).astype(BF16), xv)

    blk = pl.BlockSpec((XTRA, D), lambda k: (nb * (k + 1), 0))
    rblk = pl.BlockSpec((XTRA, D), lambda k: (jnp.where(k < 3, nb * (k + 1), 0), 0))
    return pl.pallas_call(
        kern, grid=(4,), in_specs=[rblk, pl.BlockSpec((1, XTRA, D), lambda k: (k, 0, 0))], out_specs=blk,
        out_shape=jax.ShapeDtypeStruct(wct.shape, BF16), input_output_aliases={0: 0}, name="fix_wct",
        compiler_params=_cp(("arbitrary",)),
    )(wct, xt)


_HP = PIECE // 2
_GWIN = [
    lambda r, sc, hc: r.at[_rows(PMAIN * sc + _HP * hc, _HP), :],
    lambda r, sc, hc: r.at[_rows(512 * hc, 512), pl.ds(1024 * sc, 1024)],
    lambda r, sc, hc: r.at[_rows(1024 * sc + 512 * hc, 512), :],
    lambda r, sc, hc: r.at[_rows(256 * sc + 128 * hc, 128), :],
    lambda r, sc, hc: r.at[_rows(512 * sc + 256 * hc, 256), :],
    lambda r, sc, hc: r.at[_rows(256 * sc + 128 * hc, 128), :],
]
HALF_SHAPES = [(PIECE // 2, D), (512, 1024), (512, 1024), (128, 1024), (256, 1024), (128, 1024)]


def _rs_sibling_plan(ts):
    def copies(sk, R):
        x, y, c = _coords()
        out = []
        for t in ts:
            for sc in range(4):
                land = R["ra%d" % t].at[sc]
                out.append(((_GWIN[t](R["g%d" % t], sc, 1 - c), land, (x, y, 1 - c)), land))
        return out
    return _Plan(4 * len(ts), copies)


def _rs_chips_plan(ts):
    def copies(sk, R):
        _, _, c = _coords()
        out = []
        for t in ts:
            for j, ((px, py), ps) in enumerate(_other_chips(sk)):
                land = R["rb%d" % t].at[j]
                out.append(((R["hb%d" % t].at[ps], land, (px, py, c)), land))
        return out
    return _Plan(3 * len(ts), copies)


def _rs_share_plan(ts):
    def copies(sk, R):
        x, y, c = _coords()
        out = []
        for t in ts:
            rows = HALF_SHAPES[t][0]
            mine = R["f%d" % t].at[_rows(rows * c, rows), :]
            out.append(((mine, mine, (x, y, 1 - c)), R["f%d" % t].at[_rows(rows * (1 - c), rows), :]))
        return out
    return _Plan(len(ts), copies)


def _half_tiling(t):
    rows, cols = HALF_SHAPES[t]
    if t == 0:
        return (rows // 2, cols), 2, lambda i: (i, 0)
    return (rows, cols), 1, lambda i: (0, 0)


def _window_spec(t, blk):
    if t == 0:
        return pl.BlockSpec((pl.Element(blk[0]), pl.Element(blk[1])), lambda i, sc, idx_ref: (
            pl.multiple_of(PMAIN * sc + _HP * idx_ref[1] + blk[0] * i, 128), 0))
    if t == 1:
        return pl.BlockSpec(blk, lambda i, sc, idx_ref: (idx_ref[1], sc))
    return pl.BlockSpec(blk, lambda i, sc, idx_ref: (2 * sc + idx_ref[1], 0))


def _chip_sum(g, ra, t, idx, name):
    rows, cols = HALF_SHAPES[t]
    blk, nblk, inner = _half_tiling(t)

    def kern(idx_ref, g_ref, r_ref, hb_ref, hf_ref):
        v = g_ref[...].astype(F32) + r_ref[0].astype(F32)
        hb_ref[0] = v.astype(BF16)

        @pl.when(pl.program_id(1) == idx_ref[0])
        def _():
            hf_ref[...] = v

    omap = lambda i, sc, idx_ref: (sc,) + inner(i)
    grid_spec = pltpu.PrefetchScalarGridSpec(
        num_scalar_prefetch=1, grid=(nblk, 4),
        in_specs=[_window_spec(t, blk), pl.BlockSpec((1,) + blk, omap)],
        out_specs=(pl.BlockSpec((1,) + blk, omap), pl.BlockSpec(blk, lambda i, sc, idx_ref: inner(i))))
    return pl.pallas_call(
        kern, grid_spec=grid_spec,
        out_shape=(jax.ShapeDtypeStruct((4, rows, cols), BF16), jax.ShapeDtypeStruct((rows, cols), F32)),
        name=name, compiler_params=_cp(("parallel", "arbitrary")),
    )(idx, g, ra)


def _chip_sum_part(g, ra, t, idx, name, own, dep=None):
    rows, cols = HALF_SHAPES[t]
    blk, nblk, inner = _half_tiling(t)
    chip = (lambda k, idx_ref: idx_ref[0]) if own else (lambda k, idx_ref: lax.rem(idx_ref[0] + 1 + k, 4))
    win = _window_spec(t, blk)
    deps = [] if dep is None else [dep]

    def kern(idx_ref, g_ref, r_ref, *rest):
        v = g_ref[...].astype(F32) + r_ref[0].astype(F32)
        if own:
            rest[-1][...] = v
        else:
            rest[-1][0] = v.astype(BF16)

    omap = lambda i, k, idx_ref: (chip(k, idx_ref),) + inner(i)
    grid_spec = pltpu.PrefetchScalarGridSpec(
        num_scalar_prefetch=1, grid=(nblk, 1 if own else 3),
        in_specs=[pl.BlockSpec(win.block_shape, lambda i, k, idx_ref: win.index_map(i, chip(k, idx_ref), idx_ref)),
                  pl.BlockSpec((1,) + blk, omap)] + [pl.BlockSpec(memory_space=pl.ANY)] * len(deps),
        out_specs=pl.BlockSpec(blk, lambda i, k, idx_ref: inner(i)) if own else pl.BlockSpec((1,) + blk, omap))
    return pl.pallas_call(
        kern, grid_spec=grid_spec,
        out_shape=jax.ShapeDtypeStruct((rows, cols), F32) if own else jax.ShapeDtypeStruct((4, rows, cols), BF16),
        name=name, compiler_params=_cp(("parallel", "arbitrary")),
    )(idx, g, ra, *deps)


def _final_sum(hf, rb, t, idx, name):
    rows, cols = HALF_SHAPES[t]
    blk, nblk, inner = _half_tiling(t)
    nbr = rows // blk[0]

    def kern(idx_ref, h_ref, r_ref, o_ref):
        o_ref[...] = ((h_ref[...] + r_ref[0].astype(F32)) + r_ref[1].astype(F32)) + r_ref[2].astype(F32)

    def omap(i, idx_ref):
        r, cidx = inner(i)
        return nbr * idx_ref[1] + r, cidx

    grid_spec = pltpu.PrefetchScalarGridSpec(
        num_scalar_prefetch=1, grid=(nblk,),
        in_specs=[pl.BlockSpec(blk, lambda i, idx_ref: inner(i)),
                  pl.BlockSpec((3,) + blk, lambda i, idx_ref: (0,) + inner(i))],
        out_specs=pl.BlockSpec(blk, omap))
    return pl.pallas_call(
        kern, grid_spec=grid_spec, out_shape=jax.ShapeDtypeStruct((2 * rows, cols), F32),
        name=name, compiler_params=_cp(("parallel",)),
    )(idx, hf, rb)


class _ReduceScatter:
    def __init__(self, ts, grads, idx, tag):
        self.ts, self.idx, self.tag = ts, idx, tag
        arr = {}
        for t in ts:
            arr["g%d" % t] = grads[t]
            arr["ra%d" % t] = lax.empty((4,) + HALF_SHAPES[t], BF16)
        self.plan = _rs_sibling_plan(ts)
        self.arr, self.sems, self.token = _split_call("rs_sibling_start_" + tag, arr, start=self.plan)

    def chips(self, after, own_later=False):
        arr, _, _ = _split_call("rs_sibling_wait_" + self.tag, self.arr, wait=self.plan, wait_sems=self.sems, after=after)
        brr, self.hf = {}, {}
        for t in self.ts:
            if own_later:
                hb = _chip_sum_part(arr["g%d" % t], arr["ra%d" % t], t, self.idx, "chip_sum_others_%d" % t, False)
            else:
                hb, self.hf[t] = _chip_sum(arr["g%d" % t], arr["ra%d" % t], t, self.idx, "chip_sum_%d" % t)
            brr["hb%d" % t] = hb
            brr["rb%d" % t] = lax.empty((3,) + HALF_SHAPES[t], BF16)
        self.plan = _rs_chips_plan(self.ts)
        self.arr, self.sems, self.token = _split_call("rs_chips_start_" + self.tag, brr, start=self.plan)
        if own_later:
            for t in self.ts:
                self.hf[t] = _chip_sum_part(arr["g%d" % t], arr["ra%d" % t], t, self.idx, "chip_sum_own_%d" % t, True,
                                            dep=self.token)
        return self.token

    def share(self, after):
        brr, _, _ = _split_call("rs_chips_wait_" + self.tag, self.arr, wait=self.plan, wait_sems=self.sems, after=after)
        frr = {"f%d" % t: _final_sum(self.hf[t], brr["rb%d" % t], t, self.idx, "final_sum_%d" % t) for t in self.ts}
        self.plan = _rs_share_plan(self.ts)
        self.arr, self.sems, self.token = _split_call("rs_share_start_" + self.tag, frr, start=self.plan)
        return self.token

    def result(self, after):
        frr, _, _ = _split_call("rs_share_wait_" + self.tag, self.arr, wait=self.plan, wait_sems=self.sems, after=after)
        return {t: frr["f%d" % t] for t in self.ts}


def _all8_plan(key):
    def copies(sk, R):
        x, y, c = _coords()
        own = R[key].at[4 * x + 2 * y + c]
        out = []
        for k in range(1, 8):
            dev = ((1 - x) if (k >> 2) & 1 else x, (1 - y) if (k >> 1) & 1 else y, (1 - c) if k & 1 else c)
            out.append(((own, own, dev), R[key].at[4 * dev[0] + 2 * dev[1] + dev[2]]))
        return out
    return _Plan(7, copies)


def _sum8(v, name="small_sum"):
    def kern(v_ref, o_ref):
        acc = v_ref[0]
        for k in range(1, 8):
            acc = acc + v_ref[k]
        o_ref[...] = acc

    return pl.pallas_call(kern, out_shape=jax.ShapeDtypeStruct(v.shape[1:], F32), name=name)(v)


def _adamw(w, g, m, v, name, tr=128, blk0=0, nblk=None, into=None, copy_g=False):
    R, C = w.shape
    tr = min(tr, R)
    if nblk is None:
        assert R % tr == 0 and blk0 == 0
        nblk = R // tr
    n_out = 4 if copy_g else 3

    def kern(*refs):
        w_ref, g_ref, m_ref, v_ref = refs[:4]
        d_ref, mo_ref, vo_ref = refs[-n_out:][:3]
        gv = g_ref[...]
        mn = ADAM_B1 * m_ref[...] + (1.0 - ADAM_B1) * gv
        vn = ADAM_B2 * v_ref[...] + (1.0 - ADAM_B2) * (gv * gv)
        m_hat = mn / (1.0 - ADAM_B1 ** ADAM_STEP)
        v_hat = vn / (1.0 - ADAM_B2 ** ADAM_STEP)
        d_ref[...] = -ADAM_LR * (m_hat / (jnp.sqrt(v_hat) + ADAM_EPS) + ADAM_WD * w_ref[...])
        mo_ref[...] = mn
        vo_ref[...] = vn
        if copy_g:
            refs[-1][...] = gv

    blk = pl.BlockSpec((tr, C), lambda i: (blk0 + i, 0))
    sd = jax.ShapeDtypeStruct((R, C), F32)
    in_specs, args, aliases = [blk] * 4, [w, g, m, v], {}
    if into is not None:
        in_specs += [pl.BlockSpec(memory_space=pl.ANY)] * 3
        args += list(into)
        aliases = {4: 0, 5: 1, 6: 2}
    return pl.pallas_call(kern, grid=(nblk,), in_specs=in_specs, out_specs=(blk,) * n_out, out_shape=(sd,) * n_out,
                          input_output_aliases=aliases, name=name, compiler_params=_cp(("parallel",)))(*args)


def _adamw_w_in(wt, gp, mt, vt, offs, name, r0, tr, nblk, views, into=None, blk_key=None):
    el = lambda n: (pl.Element(n), pl.Element(D))
    first = (lambda o: r0) if blk_key is None else (lambda o: tr * o[blk_key])
    own = pl.BlockSpec(el(tr), lambda i, o: (pl.multiple_of(first(o) + tr * i, 8), 0))

    def view(k):
        return pl.BlockSpec(el(tr), lambda i, o: (pl.multiple_of(jnp.maximum(first(o) + tr * i + o[k], 0), 8), 0))

    def kern(o_ref, w_ref, m_ref, v_ref, *refs):
        g_refs, (d_ref, mo_ref, vo_ref, go_ref) = refs[:len(views)], refs[-4:]
        gv = g_refs[0][...]
        if len(views) == 2:
            row = first(o_ref) + tr * pl.program_id(0) + lax.broadcasted_iota(jnp.int32, (tr, D), 0)
            gv = jnp.where(row < o_ref[2], gv, g_refs[1][...])
        mn = ADAM_B1 * m_ref[...] + (1.0 - ADAM_B1) * gv
        vn = ADAM_B2 * v_ref[...] + (1.0 - ADAM_B2) * (gv * gv)
        m_hat = mn / (1.0 - ADAM_B1 ** ADAM_STEP)
        v_hat = vn / (1.0 - ADAM_B2 ** ADAM_STEP)
        d_ref[...] = -ADAM_LR * (m_hat / (jnp.sqrt(v_hat) + ADAM_EPS) + ADAM_WD * w_ref[...])
        mo_ref[...] = mn
        vo_ref[...] = vn
        go_ref[...] = gv

    in_specs = [own, own, own] + [view(k) for k in views]
    args = [wt, mt, vt] + [gp] * len(views)
    aliases = {}
    if into is not None:
        in_specs += [pl.BlockSpec(memory_space=pl.ANY)] * 4
        args += list(into)
        aliases = {1 + len(args) - 4 + j: j for j in range(4)}
    grid_spec = pltpu.PrefetchScalarGridSpec(num_scalar_prefetch=1, grid=(nblk,), in_specs=in_specs,
                                             out_specs=(own,) * 4)
    sd = jax.ShapeDtypeStruct(wt.shape, F32)
    return pl.pallas_call(kern, grid_spec=grid_spec, out_shape=(sd,) * 4, input_output_aliases=aliases, name=name,
                          compiler_params=_cp(("parallel",)))(offs, *args)


def _to_piece(wt, s):
    z = lambda n: jnp.zeros((n, D), wt.dtype)
    pads = [functools.partial(lambda k, w: jnp.pad(w, ((8 * k, PIECE - W_SHARD - 8 * k), (0, 0))).astype(BF16), k)
            for k in range(3)]
    last = lambda w: jnp.concatenate([z(24), w[:744], w[776:], w[744:776], z(PIECE - 24 - W_SHARD)], axis=0).astype(BF16)
    return lax.switch(s, pads + [last], wt)


_SMALL = [("b_gate", 2048), ("ssm_conv_b", 4096), ("dt_bias", 32), ("A_log", 32), ("D_skip", 32),
          ("ssm_norm_w", 2048), ("norm_mlp", 1024), ("norm_final", 1024), ("sc_conv_w", 3072), ("ssm_conv_w", 16384),
          ("loss", 1)]


def _pack(vals, table, rows):
    parts = []
    for name, n in table:
        v = vals[name].reshape(-1).astype(F32)
        pad = (-n) % 128
        parts.append(jnp.pad(v, (0, pad)) if pad else v)
    flat = jnp.concatenate(parts)
    return jnp.pad(flat, (0, rows * 128 - flat.shape[0])).reshape(rows, 128)


def _unpack(arr, table):
    flat = arr.reshape(-1)
    out, off = {}, 0
    for name, n in table:
        out[name] = flat[off:off + n]
        off += n + ((-n) % 128)
    return out


def kernel(x, norm_mix, w_in, b_gate, sc_conv_w, ssm_conv_w, ssm_conv_b, dt_bias, A_log, D_skip, ssm_norm_w, w_branch_sc, w_branch_ssm, w_out, norm_mlp, w_mlp1, w_mlp2, norm_final, loss_target, m_norm_mix, m_w_in, m_b_gate, m_sc_conv_w, m_ssm_conv_w, m_ssm_conv_b, m_dt_bias, m_A_log, m_D_skip, m_ssm_norm_w, m_w_branch_sc, m_w_branch_ssm, m_w_out, m_norm_mlp, m_w_mlp1, m_w_mlp2, m_norm_final, v_norm_mix, v_w_in, v_b_gate, v_sc_conv_w, v_ssm_conv_w, v_ssm_conv_b, v_dt_bias, v_A_log, v_D_skip, v_ssm_norm_w, v_w_branch_sc, v_w_branch_ssm, v_w_out, v_norm_mlp, v_w_mlp1, v_w_mlp2, v_norm_final):
    L = x.shape[1]
    nc = L // Q
    xi, yi, ci = lax.axis_index("x"), lax.axis_index("y"), lax.axis_index("c")
    s = 2 * xi + yi
    idx = jnp.stack([s, ci]).astype(jnp.int32)
    x0 = x.reshape(L, D)
    tgt = loss_target.reshape(L, D)
    small_names = ["b_gate", "sc_conv_w", "ssm_conv_w", "ssm_conv_b", "dt_bias", "A_log", "D_skip", "ssm_norm_w",
                   "norm_mlp", "norm_final"]
    small_wmv = [dict(zip(small_names, vals)) for vals in (
        (b_gate, sc_conv_w, ssm_conv_w, ssm_conv_b, dt_bias, A_log, D_skip, ssm_norm_w, norm_mlp, norm_final),
        (m_b_gate, m_sc_conv_w, m_ssm_conv_w, m_ssm_conv_b, m_dt_bias, m_A_log, m_D_skip, m_ssm_norm_w, m_norm_mlp,
         m_norm_final),
        (v_b_gate, v_sc_conv_w, v_ssm_conv_w, v_ssm_conv_b, v_dt_bias, v_A_log, v_D_skip, v_ssm_norm_w, v_norm_mlp,
         v_norm_final))]
    small_table = [(n, int(small_wmv[0][n].size)) for n in small_names]
    small_rows = 136
    pk_w, pk_m, pk_v = [_pack(d, small_table, small_rows) for d in small_wmv]

    piece = _to_piece(w_in.T, s)
    nb = PMAIN // XTRA
    cws = jnp.zeros((8, 1280), F32)
    cws = cws.at[0:3, 0:256].set(sc_conv_w).at[0:4, 256:1280].set(ssm_conv_w)
    cw0 = lax.dynamic_update_slice(jnp.zeros((4, 8, 1280), F32), cws[None], (s, 0, 0))
    win_keys, win2_keys, mid_keys, end_keys = ["xt", "cw", "wq0"], ["wq1"], ["wa", "wb", "wo", "w1"], ["w2"]
    gw, sems_w, tok = _split_call(
        "ag_win_start", {"wct": lax.empty((NCW, D), BF16), "xt": lax.empty((4, XTRA, D), BF16), "cw": cw0, "piece": piece},
        start=_ag_chips_plan(win_keys))
    g2, sems_w2, tok = _split_call("ag_win2_start", {"wct": gw["wct"], "piece": gw["piece"]},
                                   start=_ag_chips_plan(win2_keys), after=tok)
    piece = g2["piece"]
    gw["wct"] = _place(piece, (NCW, D), (XTRA, D), lambda i, r: (nb * r[0] + i, 0), idx, "place_wct", nblk=nb,
                       dep=tok, into=g2["wct"])
    gw["xt"] = _place(piece, (4, XTRA, D), (1, XTRA, D), lambda i, r: (r[0], 0, 0), idx, "place_xt", blk0=nb, nblk=1,
                      dep=tok, into=gw["xt"])
    gw["piece"] = piece
    wa0 = _place(w_branch_sc, (D, D), (256, 1024), lambda i, r: (r[0], 0), idx, "place_wa", dep=tok)
    wb0 = _place(w_branch_ssm, (INNER, D), (512, 1024), lambda i, r: (r[0], 0), idx, "place_wb", dep=tok)
    wo0 = _place(w_out, (D, D), (256, 1024), lambda i, r: (r[0], 0), idx, "place_wo", dep=tok)
    w10 = _place(w_mlp1, (D, DFF), (256, 1024), lambda i, r: (i, r[0]), idx, "place_w1", dep=tok)
    gm, sems_m, tok = _split_call("ag_mid_start", {"wa": wa0, "wb": wb0, "wo": wo0, "w1": w10},
                                  start=_ag_chips_plan(mid_keys))
    w20 = _place(w_mlp2, (DFF, D), (256, 1024), lambda i, r: (4 * r[0] + i, 0), idx, "place_w2", dep=tok)
    ge, sems_e, tok = _split_call("ag_end_start", {"w2": w20}, start=_ag_chips_plan(end_keys))
    h = _rms_fwd(x0, norm_mix, "rms_mix", dep=tok)
    gw, sems_w, tok = _split_call("ag_win_pass", gw, wait=_ag_chips_plan(win_keys), wait_sems=sems_w,
                                  start=_ag_sibling_plan(win_keys), after=[h, pk_w, pk_m, pk_v])
    gw, _, _ = _split_call("ag_win_done", gw, wait=_ag_sibling_plan(win_keys), wait_sems=sems_w, after=tok)
    wc, cw_all = _fix_wct(gw["wct"], gw["xt"]), gw["cw"]
    sc_w_full = jnp.concatenate([cw_all[k, :, 0:256] for k in range(4)], axis=1)
    ssm_w_full = jnp.concatenate([cw_all[k, :, 256:1280] for k in range(4)], axis=1)
    cw4 = ssm_w_full.at[4].set(ssm_conv_b)
    vec = jnp.zeros((8, 128), F32).at[0, :NH].set(dt_bias).at[1, :NH].set(A_log)
    vecg = jnp.zeros((NG, 8, 128), F32).at[:, 0, :4].set(A_log.reshape(NG, 4)).at[:, 1, :4].set(D_skip.reshape(NG, 4))

    dtraw = _matmul(h, wc[C_DT:], "nt", F32, 512, 256, 1024, "in_proj_dt")
    proj = _in_proj_wave(h, wc, 0)
    g2, sems_w2, tok = _split_call("ag_win2_pass", {"wct": wc, "piece": gw["piece"]},
                                   wait=_ag_chips_plan(win2_keys), wait_sems=sems_w2,
                                   start=_ag_sibling_plan(win2_keys), after=[proj, dtraw])
    g2, _, _ = _split_call("ag_win2_done", g2, wait=_ag_sibling_plan(win2_keys), wait_sems=sems_w2, after=tok)
    wc = g2["wct"]
    proj = _in_proj_wave(h, wc, 1, proj=proj)
    ya = _sc_fwd(proj, sc_w_full)
    xbc = _ssm_conv_fwd(proj, cw4)
    dt4, cs4, sg4 = _dt_prep(dtraw, vec)
    y, s_all = _ssd_fwd(xbc, dt4, cs4, vecg)
    gm, sems_m, tok = _split_call("ag_mid_pass", gm, wait=_ag_chips_plan(mid_keys), wait_sems=sems_m,
                                  start=_ag_sibling_plan(mid_keys), after=[y, ya])
    y = _tie(y, tok, "tie_y")
    yb = _gnorm_fwd(y, proj, ssm_norm_w)
    gm, _, _ = _split_call("ag_mid_done", gm, wait=_ag_sibling_plan(mid_keys), wait_sems=sems_m, after=yb)
    wa, wb, wo, w1 = gm["wa"], gm["wb"], gm["wo"], gm["w1"]
    ge, sems_e, tok = _split_call("ag_end_pass", ge, wait=_ag_chips_plan(end_keys), wait_sems=sems_e,
                                  start=_ag_sibling_plan(end_keys), after=yb)
    br_a = _matmul(ya, wa, "nn", F32, 1024, 1024, 1024, "branch_sc", dep=tok)
    br_b, merged = _branch_ssm_merge(yb, wb, proj, b_gate, br_a)
    x1, h2 = _matmul_res_rms(merged, wo, x0, norm_mlp, 1024, "out_proj")
    a1, rl = _matmul(h2, w1, "nn", BF16, 1024, 1024, 1024, "mlp1", epi="relu2", n_outer=True)
    ge, _, _ = _split_call("ag_end_done", ge, wait=_ag_sibling_plan(end_keys), wait_sems=sems_e, after=a1)
    w2 = ge["w2"]
    dx2, g_nf, loss8 = _matmul_res_final(rl, w2, x1, norm_final, tgt, 512, "mlp2")

    da = _matmul(dx2, w2, "nt", BF16, 1024, 1024, 1024, "mlp2_dx", epi="drelu", extra=a1, n_outer=True)
    g_w2 = _matmul(rl, dx2, "tn", BF16, 1024, 1024, 2048, "mlp2_dw")
    g_w1 = _matmul(h2, da, "tn", BF16, 1024, 1024, 2048, "mlp1_dw")
    dx1, g_nmlp = _matmul_rms_bwd(da, w1, "nt", x1, norm_mlp, dx2, 512, 4096, "mlp1_dx")
    g_wo = _matmul(merged, dx1, "tn", BF16, 1024, 1024, 2048, "out_proj_dw")
    dproj = lax.empty((L, NCW), BF16)
    dbr, dproj, g_bg = _merge_bwd(dx1, wo, proj, b_gate, br_a, br_b, dproj)
    dya = _matmul(dbr[0], wa, "nt", F32, 1024, 1024, 1024, "branch_sc_dx")
    g_wa = _matmul(ya, dbr[0], "tn", BF16, 1024, 1024, 2048, "branch_sc_dw")
    dproj, g_scw = _sc_bwd(dya, proj, sc_w_full, dproj)
    g_wb = _matmul(yb, dbr[1], "tn", BF16, 1024, 1024, 2048, "branch_ssm_dw")
    rs_a = _ReduceScatter([1, 2, 3, 4, 5], {1: g_w1, 2: g_w2, 3: g_wa, 4: g_wb, 5: g_wo}, idx, "a")
    dy, dproj, g_snw = _gnorm_bwd(dbr, wb, y, proj, ssm_norm_w, dproj, rs_a.token)
    tok = rs_a.chips(after=dy)
    dxs, dbm, dcm, ddt_g, st = _ssd_bwd(xbc, dt4, cs4, sg4, vecg, s_all, _tie(dy, tok, "tie_dy"))
    dproj, gx1 = _ssm_conv_bwd(dxs, proj, cw4, dproj, 0, "ssm_conv_bwd_x")
    dproj, gx2 = _ssm_conv_bwd(dbm, proj, cw4, dproj, INNER, "ssm_conv_bwd_b")
    dproj, gx3 = _ssm_conv_bwd(dcm, proj, cw4, dproj, INNER + NG * NS, "ssm_conv_bwd_c")
    g_cw4 = jnp.concatenate([gx1, gx2, gx3], axis=1)
    dproj, g_dtb = _dt_bwd(ddt_g, dproj)
    small = {"b_gate": g_bg[0], "ssm_conv_b": g_cw4[4], "dt_bias": g_dtb[0, :NH],
             "A_log": st[:, 0, :4], "D_skip": st[:, 1, :4], "ssm_norm_w": g_snw[0], "norm_mlp": g_nmlp[0],
             "norm_final": g_nf[0], "sc_conv_w": g_scw[0:3], "ssm_conv_w": g_cw4[0:4], "loss": loss8[0, 0:1]}
    me = 4 * xi + 2 * yi + ci
    sm8 = lax.dynamic_update_slice(jnp.zeros((8, SMALL_ROWS, 128), F32), _pack(small, _SMALL, SMALL_ROWS)[None], (me, 0, 0))
    sm_arr, sm_sems, tok = _split_call("small_start", {"sm": sm8}, start=_all8_plan("sm"))
    g_wc = _matmul(dproj, h, "tn", BF16, 1280, 1024, 2048, "in_proj_dw", dep=tok)
    rs_b = _ReduceScatter([0], {0: g_wc}, idx, "b")
    tok = rs_a.share(after=rs_b.token)
    tok = rs_b.chips(after=tok, own_later=True)
    grad_x, g_nm = _matmul_rms_bwd(dproj, wc, "nn", x0, norm_mix, dx1, 512, 3840, "in_proj_dx", dep=rs_b.hf[0])
    nm8 = lax.dynamic_update_slice(jnp.zeros((8, 8, 128), F32), g_nm[0].reshape(1, 8, 128), (me, 0, 0))
    nm_arr, nm_sems, tok = _split_call("norm_mix_start", {"nm": nm8}, start=_all8_plan("nm"))
    sm_arr, _, _ = _split_call("small_wait", sm_arr, wait=_all8_plan("sm"), wait_sems=sm_sems, after=tok)
    small_sum = _sum8(sm_arr["sm"])
    gs = _unpack(small_sum, _SMALL)
    red = rs_a.result(after=tok)
    big = {"w_mlp1": red[1], "w_mlp2": red[2], "w_branch_sc": red[3], "w_branch_ssm": red[4], "w_out": red[5]}

    given = dict(norm_mix=norm_mix, w_in=w_in, b_gate=b_gate, sc_conv_w=sc_conv_w, ssm_conv_w=ssm_conv_w, ssm_conv_b=ssm_conv_b, dt_bias=dt_bias, A_log=A_log, D_skip=D_skip, ssm_norm_w=ssm_norm_w, w_branch_sc=w_branch_sc, w_branch_ssm=w_branch_ssm, w_out=w_out, norm_mlp=norm_mlp, w_mlp1=w_mlp1, w_mlp2=w_mlp2, norm_final=norm_final,
                 m_norm_mix=m_norm_mix, m_w_in=m_w_in, m_b_gate=m_b_gate, m_sc_conv_w=m_sc_conv_w, m_ssm_conv_w=m_ssm_conv_w, m_ssm_conv_b=m_ssm_conv_b, m_dt_bias=m_dt_bias, m_A_log=m_A_log, m_D_skip=m_D_skip, m_ssm_norm_w=m_ssm_norm_w, m_w_branch_sc=m_w_branch_sc, m_w_branch_ssm=m_w_branch_ssm, m_w_out=m_w_out, m_norm_mlp=m_norm_mlp, m_w_mlp1=m_w_mlp1, m_w_mlp2=m_w_mlp2, m_norm_final=m_norm_final,
                 v_norm_mix=v_norm_mix, v_w_in=v_w_in, v_b_gate=v_b_gate, v_sc_conv_w=v_sc_conv_w, v_ssm_conv_w=v_ssm_conv_w, v_ssm_conv_b=v_ssm_conv_b, v_dt_bias=v_dt_bias, v_A_log=v_A_log, v_D_skip=v_D_skip, v_ssm_norm_w=v_ssm_norm_w, v_w_branch_sc=v_w_branch_sc, v_w_branch_ssm=v_w_branch_ssm, v_w_out=v_w_out, v_norm_mlp=v_norm_mlp, v_w_mlp1=v_w_mlp1, v_w_mlp2=v_w_mlp2, v_norm_final=v_norm_final)
    order = ["norm_mix", "w_in", "b_gate", "sc_conv_w", "ssm_conv_w", "ssm_conv_b", "dt_bias", "A_log", "D_skip",
             "ssm_norm_w", "w_branch_sc", "w_branch_ssm", "w_out", "norm_mlp", "w_mlp1", "w_mlp2", "norm_final"]
    grad, delta, new_m, new_v = {}, {}, {}, {}
    for n in big:
        delta[n], new_m[n], new_v[n], grad[n] = _adamw(given[n], big[n], given["m_" + n], given["v_" + n],
                                                       "adamw_" + n, copy_g=True)
    big["w_in"] = None
    grad_small = {n: gs[n].reshape(given[n].shape) for n in small_names if n not in ("sc_conv_w", "ssm_conv_w")}
    grad_small["sc_conv_w"] = lax.dynamic_slice(gs["sc_conv_w"].reshape(3, D), (0, 256 * s), (3, 256))
    grad_small["ssm_conv_w"] = lax.dynamic_slice(gs["ssm_conv_w"].reshape(4, XBC), (0, 1024 * s), (4, 1024))
    table = small_table
    ds_, ms_, vs_ = _adamw(pk_w, _pack(grad_small, table, small_rows), pk_m, pk_v, "adamw_small", tr=small_rows)
    ds_, ms_, vs_ = _unpack(ds_, table), _unpack(ms_, table), _unpack(vs_, table)
    for n in grad_small:
        shp = given[n].shape
        grad[n] = grad_small[n]
        delta[n], new_m[n], new_v[n] = ds_[n].reshape(shp), ms_[n].reshape(shp), vs_[n].reshape(shp)

    done = [new_v[n] for n in ("w_mlp1", "w_mlp2", "w_branch_sc", "w_branch_ssm", "w_out")] + [vs_["b_gate"]]
    tok = rs_b.share(after=done)
    offs = jnp.where(s == 3, jnp.array([24, -8, 744, 2072, -8], jnp.int32),
                     jnp.stack([8 * s, 8 * s, 0 * s, 8 * s, 8 * s]).astype(jnp.int32))
    offs = jnp.concatenate([offs, jnp.stack([7 * ci, 4 - 4 * ci]).astype(jnp.int32)])
    nmain = W_SHARD // 256
    wt_own = (w_in.T, rs_b.arr["f0"], m_w_in.T, v_w_in.T, offs)
    res = _adamw_w_in(*wt_own, "adamw_w_in_own", 0, 256, 4, (0, 1), blk_key=5)
    gp = rs_b.result(after=[tok, res[0]])[0]
    wt_args = (w_in.T, gp, m_w_in.T, v_w_in.T, offs)
    res = _adamw_w_in(*wt_args, "adamw_w_in", 0, 256, nmain - 4, (0, 1), into=res, blk_key=6)
    res = _adamw_w_in(*wt_args, "adamw_w_in_dt", 744, 32, 1, (3,), into=res)
    dt_, mt_, vt_, gwt = _adamw_w_in(*wt_args, "adamw_w_in_tail", 256 * nmain, 8, 1, (4,), into=res)
    grad["w_in"], delta["w_in"], new_m["w_in"], new_v["w_in"] = gwt.T, dt_.T, mt_.T, vt_.T
    nm_arr, _, _ = _split_call("norm_mix_wait", nm_arr, wait=_all8_plan("nm"), wait_sems=nm_sems, after=tok)
    g8 = _sum8(nm_arr["nm"], "norm_mix_sum")
    r8 = lambda a: a.reshape(8, 128)
    d8, m8, v8 = _adamw(r8(norm_mix), g8, r8(m_norm_mix), r8(v_norm_mix), "adamw_norm_mix", tr=8)
    grad["norm_mix"], delta["norm_mix"] = g8.reshape(D), d8.reshape(D)
    new_m["norm_mix"], new_v["norm_mix"] = m8.reshape(D), v8.reshape(D)

    loss = gs["loss"].reshape(())
    return (loss, grad_x.reshape(1, L, D), *[grad[n] for n in order], *[delta[n] for n in order],
            *[new_m[n] for n in order], *[new_v[n] for n in order])
```

```python
import functools

import jax
import jax.numpy as jnp
from jax import lax
from jax.experimental import pallas as pl
from jax.experimental.pallas import tpu as pltpu

F32 = jnp.float32
BF16 = jnp.bfloat16
MESH = pl.DeviceIdType.MESH
HBM = pltpu.HBM

D = 1024
INNER = 2048
HD = 64
NH = 32
NG = 8
NS = 128
Q = 128
GPS = 8
XBC = 4096
DFF = 4096
EPS = 1e-6
W_SHARD = 2824
NCW = 11520
PIECE = 3072
PMAIN = 2816
C_Z, C_XBC, C_GATE, C_DT = 3072, 5120, 9216, 11264
SMALL_ROWS = 256
VMEM_LIMIT = 56 * 1024 * 1024

ADAM_LR, ADAM_B1, ADAM_B2, ADAM_EPS, ADAM_WD, ADAM_STEP = 0.001, 0.9, 0.999, 1e-08, 0.01, 10


VMEM_SMALL = 48 * 1024 * 1024


def _cp(sem=None, vmem=VMEM_SMALL):
    return pltpu.CompilerParams(dimension_semantics=sem, vmem_limit_bytes=vmem)


def _sigmoid(v):
    return 1.0 / (1.0 + jnp.exp(-v))


_DIMS = {"nn": (((1,), (0,)), ((), ())), "nt": (((1,), (1,)), ((), ())), "tn": (((0,), (0,)), ((), ()))}


def _matmul(a, b, mode, out_dtype, tm, tn, tk, name, epi=None, extra=None, n_outer=False, dep=None):
    if mode == "tn":
        K, M = a.shape
    else:
        M, K = a.shape
    N = b.shape[0] if mode == "nt" else b.shape[1]
    tm, tn, tk = min(tm, M), min(tn, N), min(tk, K)
    assert M % tm == 0 and N % tn == 0 and K % tk == 0, (name, M, N, K, tm, tn, tk)
    nm, nn, nk = M // tm, N // tn, K // tk
    dims = _DIMS[mode]

    def ij(p0, p1):
        return (p1, p0) if n_outer else (p0, p1)

    if mode == "tn":
        a_spec = pl.BlockSpec((tk, tm), lambda p0, p1, k: (k, ij(p0, p1)[0]))
    else:
        a_spec = pl.BlockSpec((tm, tk), lambda p0, p1, k: (ij(p0, p1)[0], k))
    if mode == "nt":
        b_spec = pl.BlockSpec((tn, tk), lambda p0, p1, k: (ij(p0, p1)[1], k))
    else:
        b_spec = pl.BlockSpec((tk, tn), lambda p0, p1, k: (k, ij(p0, p1)[1]))
    o_spec = pl.BlockSpec((tm, tn), lambda p0, p1, k: ij(p0, p1))
    in_specs = [a_spec, b_spec]
    args = [a, b]
    if epi in ("res", "drelu"):
        in_specs.append(o_spec)
        args.append(extra)
    if dep is not None:
        in_specs.append(pl.BlockSpec(memory_space=pl.ANY))
        args.append(dep)
    n_in = len(args)
    if epi == "relu2":
        out_shape = (jax.ShapeDtypeStruct((M, N), out_dtype), jax.ShapeDtypeStruct((M, N), BF16))
        out_specs = (o_spec, o_spec)
    else:
        out_shape = jax.ShapeDtypeStruct((M, N), out_dtype)
        out_specs = o_spec

    def kern(*refs):
        a_ref, b_ref = refs[0], refs[1]
        e_ref = refs[2] if epi in ("res", "drelu") else None
        acc = refs[-1]
        outs = refs[n_in:-1] if nk > 1 else refs[n_in:]
        k = pl.program_id(2)

        def product():
            return lax.dot_general(a_ref[...].astype(BF16), b_ref[...].astype(BF16), dims, preferred_element_type=F32)

        def finish(r):
            if epi is None:
                outs[0][...] = r.astype(out_dtype)
            elif epi == "res":
                outs[0][...] = (r + e_ref[...]).astype(out_dtype)
            elif epi == "relu2":
                outs[0][...] = r.astype(out_dtype)
                t = jnp.maximum(r, 0.0)
                outs[1][...] = (t * t).astype(BF16)
            else:
                outs[0][...] = (r * (2.0 * jnp.maximum(e_ref[...].astype(F32), 0.0))).astype(out_dtype)

        if nk == 1:
            finish(product())
        else:
            @pl.when(k == 0)
            def _():
                acc[...] = jnp.zeros_like(acc)

            acc[...] += product()

            @pl.when(k == nk - 1)
            def _():
                finish(acc[...])

    grid = (nn, nm, nk) if n_outer else (nm, nn, nk)
    return pl.pallas_call(
        kern, grid=grid, in_specs=in_specs, out_specs=out_specs, out_shape=out_shape,
        scratch_shapes=[pltpu.VMEM((tm, tn), F32)] if nk > 1 else [], name=name,
        compiler_params=_cp(("parallel", "parallel", "arbitrary")),
    )(*args)


def _matmul_res_rms(a, b, x, w, tm, name):
    M, K = a.shape
    tm = min(tm, M)

    def kern(a_ref, b_ref, x_ref, w_ref, x1_ref, h_ref):
        x1 = x_ref[...] + lax.dot_general(a_ref[...].astype(BF16), b_ref[...].astype(BF16), _DIMS["nn"],
                                          preferred_element_type=F32)
        x1_ref[...] = x1
        r = lax.rsqrt(jnp.mean(x1 * x1, axis=-1, keepdims=True) + EPS)
        h_ref[...] = ((x1 * r) * w_ref[...]).astype(BF16)

    row = pl.BlockSpec((tm, D), lambda i: (i, 0))
    return pl.pallas_call(
        kern, grid=(M // tm,),
        in_specs=[pl.BlockSpec((tm, K), lambda i: (i, 0)), pl.BlockSpec((K, D), lambda i: (0, 0)), row,
                  pl.BlockSpec((1, D), lambda i: (0, 0))],
        out_specs=(row, row), out_shape=(jax.ShapeDtypeStruct((M, D), F32), jax.ShapeDtypeStruct((M, D), BF16)),
        name=name, compiler_params=_cp(("parallel",)),
    )(a, b, x, w.reshape(1, D))


def _matmul_res_final(a, b, x, w, tgt, tm, name):
    M, K = a.shape
    tm = min(tm, M)

    def kern(a_ref, b_ref, x_ref, w_ref, t_ref, dx_ref, gw_ref, loss_ref):
        @pl.when(pl.program_id(0) == 0)
        def _():
            gw_ref[...] = jnp.zeros_like(gw_ref)
            loss_ref[...] = jnp.zeros_like(loss_ref)

        xv = x_ref[...] + lax.dot_general(a_ref[...].astype(BF16), b_ref[...].astype(BF16), _DIMS["nn"],
                                          preferred_element_type=F32)
        r = lax.rsqrt(jnp.mean(xv * xv, axis=-1, keepdims=True) + EPS)
        xn = xv * r
        e = xn * w_ref[...] - t_ref[...]
        loss_ref[...] += 0.5 * jnp.sum(jnp.mean(e * e, axis=-1, keepdims=True))
        dyv = e * (1.0 / D)
        gw_ref[...] += jnp.broadcast_to(jnp.sum(dyv * xn, axis=0, keepdims=True), (8, D))
        dxn = dyv * w_ref[...]
        dx_ref[...] = r * (dxn - xn * jnp.mean(dxn * xn, axis=-1, keepdims=True))

    row = pl.BlockSpec((tm, D), lambda i: (i, 0))
    return pl.pallas_call(
        kern, grid=(M // tm,),
        in_specs=[pl.BlockSpec((tm, K), lambda i: (i, 0)), pl.BlockSpec((K, D), lambda i: (0, 0)), row,
                  pl.BlockSpec((1, D), lambda i: (0, 0)), row],
        out_specs=(row, pl.BlockSpec((8, D), lambda i: (0, 0)), pl.BlockSpec((8, 128), lambda i: (0, 0))),
        out_shape=(jax.ShapeDtypeStruct((M, D), F32), jax.ShapeDtypeStruct((8, D), F32),
                   jax.ShapeDtypeStruct((8, 128), F32)),
        name=name, compiler_params=_cp(("arbitrary",), VMEM_LIMIT),
    )(a, b, x, w.reshape(1, D), tgt)


def _matmul_rms_bwd(a, b, mode, x, w, res, tm, tk, name, dep=None):
    M, K = a.shape
    tm, tk = min(tm, M), min(tk, K)
    nk = K // tk
    assert M % tm == 0 and K % tk == 0
    b_spec = (pl.BlockSpec((tk, D), lambda k, i: (k, 0)) if mode == "nn" else pl.BlockSpec((D, tk), lambda k, i: (0, k)))
    row = pl.BlockSpec((tm, D), lambda k, i: (jnp.where(k == nk - 1, i, 0), 0))
    deps = [] if dep is None else [dep]

    def kern(a_ref, b_ref, x_ref, w_ref, res_ref, *rest):
        dx_ref, gw_ref, acc = rest[-3:]
        k, i = pl.program_id(0), pl.program_id(1)

        @pl.when((i == 0) & (k == 0))
        def _():
            gw_ref[...] = jnp.zeros_like(gw_ref)

        def product():
            return lax.dot_general(a_ref[...].astype(BF16), b_ref[...].astype(BF16), _DIMS[mode],
                                   preferred_element_type=F32)

        def finish(dyv):
            xv = x_ref[...]
            r = lax.rsqrt(jnp.mean(xv * xv, axis=-1, keepdims=True) + EPS)
            xn = xv * r
            gw_ref[...] += jnp.broadcast_to(jnp.sum(dyv * xn, axis=0, keepdims=True), (8, D))
            dxn = dyv * w_ref[...]
            dx_ref[...] = res_ref[...] + r * (dxn - xn * jnp.mean(dxn * xn, axis=-1, keepdims=True))

        if nk == 1:
            finish(product())
        else:
            rows = pl.ds(pl.multiple_of(i * tm, tm), tm)

            @pl.when(k == 0)
            def _():
                acc[rows, :] = jnp.zeros((tm, D), F32)

            acc[rows, :] += product()

            @pl.when(k == nk - 1)
            def _():
                finish(acc[rows, :])

    return pl.pallas_call(
        kern, grid=(nk, M // tm),
        in_specs=[pl.BlockSpec((tm, tk), lambda k, i: (i, k)), b_spec, row, pl.BlockSpec((1, D), lambda k, i: (0, 0)),
                  row] + [pl.BlockSpec(memory_space=pl.ANY)] * len(deps),
        out_specs=(row, pl.BlockSpec((8, D), lambda k, i: (0, 0))),
        out_shape=(jax.ShapeDtypeStruct((M, D), F32), jax.ShapeDtypeStruct((8, D), F32)),
        scratch_shapes=[pltpu.VMEM((M, D) if nk > 1 else (8, 128), F32)], name=name,
        compiler_params=_cp(("arbitrary", "arbitrary"), VMEM_LIMIT),
    )(a, b, x, w.reshape(1, D), res, *deps)


def _rms_fwd(x, w, name, tl=256, dep=None):
    L = x.shape[0]

    def kern(x_ref, w_ref, *rest):
        o_ref = rest[-1]
        xv = x_ref[...]
        r = lax.rsqrt(jnp.mean(xv * xv, axis=-1, keepdims=True) + EPS)
        o_ref[...] = ((xv * r) * w_ref[...]).astype(BF16)

    row = pl.BlockSpec((tl, D), lambda i: (i, 0))
    deps = [] if dep is None else [dep]
    return pl.pallas_call(
        kern, grid=(L // tl,),
        in_specs=[row, pl.BlockSpec((1, D), lambda i: (0, 0))] + [pl.BlockSpec(memory_space=pl.ANY)] * len(deps),
        out_specs=row, out_shape=jax.ShapeDtypeStruct((L, D), BF16), name=name, compiler_params=_cp(("parallel",)),
    )(x, w.reshape(1, D), *deps)


def _down(v, k):
    if k == 0:
        return v
    t = lax.broadcasted_iota(jnp.int32, v.shape, 0)
    return jnp.where(t >= k, pltpu.roll(v, k, axis=0), 0.0)


def _up(v, k):
    if k == 0:
        return v
    n = v.shape[0]
    t = lax.broadcasted_iota(jnp.int32, v.shape, 0)
    return jnp.where(t < n - k, pltpu.roll(v, n - k, axis=0), 0.0)


TW = 256


def _sc_fwd(proj, cw):
    L = proj.shape[0]
    nb = D // TW

    def kern(b_ref, c_ref, x_ref, w_ref, o_ref):
        u = c_ref[...].astype(F32) * x_ref[...].astype(F32)
        w = w_ref[...]
        cv = w[0:1] * _down(u, 2) + w[1:2] * _down(u, 1) + w[2:3] * u
        o_ref[...] = (b_ref[...].astype(F32) * cv).astype(BF16)

    col = lambda off: pl.BlockSpec((L, TW), lambda j: (0, off + j))
    return pl.pallas_call(
        kern, grid=(nb,), in_specs=[col(0), col(nb), col(2 * nb), pl.BlockSpec((8, TW), lambda j: (0, j))],
        out_specs=pl.BlockSpec((L, TW), lambda j: (0, j)), out_shape=jax.ShapeDtypeStruct((L, D), BF16),
        name="sc_fwd", compiler_params=_cp(("parallel",)),
    )(proj, proj, proj, cw)


def _sc_bwd(dya, proj, cw, dproj):
    L = proj.shape[0]
    nb = D // TW

    def kern(d_ref, b_ref, c_ref, x_ref, w_ref, _, dp_ref, gw_ref, keep):
        sec = pl.program_id(1)

        @pl.when(sec == 0)
        def _():
            cs, xs, dyv = c_ref[...].astype(F32), x_ref[...].astype(F32), d_ref[...]
            w = w_ref[...]
            u = cs * xs
            u1, u2 = _down(u, 1), _down(u, 2)
            cv = w[0:1] * u2 + w[1:2] * u1 + w[2:3] * u
            dcv = dyv * b_ref[...].astype(F32)
            du = w[2:3] * dcv + w[1:2] * _up(dcv, 1) + w[0:1] * _up(dcv, 2)
            g0 = jnp.sum(dcv * u2, axis=0, keepdims=True)
            g1 = jnp.sum(dcv * u1, axis=0, keepdims=True)
            g2 = jnp.sum(dcv * u, axis=0, keepdims=True)
            row = lax.broadcasted_iota(jnp.int32, (8, TW), 0)
            gw_ref[...] = jnp.where(row == 0, g0, jnp.where(row == 1, g1, jnp.where(row == 2, g2, 0.0)))
            dp_ref[...] = (dyv * cv).astype(BF16)
            keep[0] = (du * xs).astype(BF16)
            keep[1] = (du * cs).astype(BF16)

        @pl.when(sec > 0)
        def _():
            dp_ref[...] = keep[sec - 1]

    col = lambda off: pl.BlockSpec((L, TW), lambda j, s: (0, off + j))
    return pl.pallas_call(
        kern, grid=(nb, 3),
        in_specs=[col(0), col(0), col(nb), col(2 * nb), pl.BlockSpec((8, TW), lambda j, s: (0, j)),
                  pl.BlockSpec(memory_space=pl.ANY)],
        out_specs=(pl.BlockSpec((L, TW), lambda j, s: (0, s * nb + j)), pl.BlockSpec((8, TW), lambda j, s: (0, j))),
        out_shape=(jax.ShapeDtypeStruct(dproj.shape, BF16), jax.ShapeDtypeStruct((8, D), F32)),
        scratch_shapes=[pltpu.VMEM((2, L, TW), BF16)],
        input_output_aliases={5: 0}, name="sc_bwd", compiler_params=_cp(("parallel", "arbitrary")),
    )(dya, proj, proj, proj, cw, dproj)


def _ssm_conv_fwd(proj, cw4):
    L = proj.shape[0]
    off = C_XBC // TW

    def kern(r_ref, w_ref, o_ref):
        raw = r_ref[...].astype(F32)
        w = w_ref[...]
        c4 = w[0:1] * _down(raw, 3) + w[1:2] * _down(raw, 2) + w[2:3] * _down(raw, 1) + w[3:4] * raw + w[4:5]
        o_ref[...] = c4 * _sigmoid(c4)

    return pl.pallas_call(
        kern, grid=(XBC // TW,),
        in_specs=[pl.BlockSpec((L, TW), lambda j: (0, off + j)), pl.BlockSpec((8, TW), lambda j: (0, j))],
        out_specs=pl.BlockSpec((L, TW), lambda j: (0, j)), out_shape=jax.ShapeDtypeStruct((L, XBC), F32),
        name="ssm_conv_fwd", compiler_params=_cp(("parallel",)),
    )(proj, cw4)


def _ssm_conv_bwd(dx, proj, cw4, dproj, col0, name):
    L, width = dx.shape
    off_p = (C_XBC + col0) // TW
    off_w = col0 // TW

    def kern(d_ref, r_ref, w_ref, _, dp_ref, gw_ref):
        raw = r_ref[...].astype(F32)
        w = w_ref[...]
        r1, r2, r3 = _down(raw, 1), _down(raw, 2), _down(raw, 3)
        c4 = w[0:1] * r3 + w[1:2] * r2 + w[2:3] * r1 + w[3:4] * raw + w[4:5]
        sg = _sigmoid(c4)
        dc4 = d_ref[...] * (sg * (1.0 + c4 * (1.0 - sg)))
        draw = w[3:4] * dc4 + w[2:3] * _up(dc4, 1) + w[1:2] * _up(dc4, 2) + w[0:1] * _up(dc4, 3)
        dp_ref[...] = draw.astype(BF16)
        gs = [jnp.sum(dc4 * r3, axis=0, keepdims=True), jnp.sum(dc4 * r2, axis=0, keepdims=True),
              jnp.sum(dc4 * r1, axis=0, keepdims=True), jnp.sum(dc4 * raw, axis=0, keepdims=True),
              jnp.sum(dc4, axis=0, keepdims=True)]
        row = lax.broadcasted_iota(jnp.int32, (8, TW), 0)
        acc = jnp.zeros((8, TW), F32)
        for k, gk in enumerate(gs):
            acc = jnp.where(row == k, gk, acc)
        gw_ref[...] = acc

    return pl.pallas_call(
        kern, grid=(width // TW,),
        in_specs=[pl.BlockSpec((L, TW), lambda j: (0, j)), pl.BlockSpec((L, TW), lambda j: (0, off_p + j)),
                  pl.BlockSpec((8, TW), lambda j: (0, off_w + j)), pl.BlockSpec(memory_space=pl.ANY)],
        out_specs=(pl.BlockSpec((L, TW), lambda j: (0, off_p + j)), pl.BlockSpec((8, TW), lambda j: (0, j))),
        out_shape=(jax.ShapeDtypeStruct(dproj.shape, BF16), jax.ShapeDtypeStruct((8, width), F32)),
        input_output_aliases={3: 0}, name=name, compiler_params=_cp(("arbitrary",)),
    )(dx, proj, cw4, dproj)


def _split3(v):
    h1 = v.astype(BF16)
    r1 = v - h1.astype(F32)
    h2 = r1.astype(BF16)
    h3 = (r1 - h2.astype(F32)).astype(BF16)
    return h1, h2, h3


def _dot01(m01, v, dims=_DIMS["nn"], m_left=True, terms=3):
    out = None
    for part in _split3(v)[:terms]:
        ops = (m01, part) if m_left else (part, m01)
        t = lax.dot_general(ops[0], ops[1], dims, preferred_element_type=F32)
        out = t if out is None else out + t
    return out


def _bdot(a, b, mode="nn"):
    return lax.dot_general(a.astype(BF16), b.astype(BF16), _DIMS[mode], preferred_element_type=F32)


def _softplus(v):
    return jnp.maximum(v, 0.0) + jnp.log1p(jnp.exp(-jnp.abs(v)))


def _dt_prep(proj, vec):
    L = proj.shape[0]

    def kern(p_ref, v_ref, dt_ref, cs_ref, sg_ref):
        v = v_ref[...]
        pre = p_ref[:, 0:128] + v[0:1]
        dt = _softplus(pre)
        da = dt * (-jnp.exp(v[1:2]))
        ii = lax.broadcasted_iota(jnp.int32, (Q, Q), 0)
        jj = lax.broadcasted_iota(jnp.int32, (Q, Q), 1)
        ltri = (jj <= ii).astype(BF16)
        lane = lax.broadcasted_iota(jnp.int32, (Q, 128), 1)
        for val, ref in ((dt, dt_ref), (_dot01(ltri, da), cs_ref), (_sigmoid(pre), sg_ref)):
            for g in range(NG):
                moved = val if g == 0 else pltpu.roll(val, 128 - 4 * g, axis=1)
                ref[g] = jnp.where(lane < 4, moved, 0.0)

    blk = pl.BlockSpec((NG, Q, 128), lambda c: (0, c, 0))
    return pl.pallas_call(
        kern, grid=(L // Q,),
        in_specs=[pl.BlockSpec((Q, 256), lambda c: (c, 0)), pl.BlockSpec((8, 128), lambda c: (0, 0))],
        out_specs=(blk, blk, blk),
        out_shape=(jax.ShapeDtypeStruct((NG, L, 128), F32),) * 3,
        name="dt_prep", compiler_params=_cp(("parallel",)),
    )(proj, vec)


def _head_masks():
    lane = lax.broadcasted_iota(jnp.int32, (1, 4 * HD), 1)
    return [((lane >= HD * j) & (lane < HD * (j + 1))) for j in range(4)]


def _expand4(v4, masks):
    R = v4.shape[0]
    out = jnp.zeros((R, 4 * HD), F32)
    for j in range(4):
        out = jnp.where(masks[j], jnp.broadcast_to(v4[:, j:j + 1], (R, 4 * HD)), out)
    return out


def _decay_matrix(cs_col, tri):
    colb = jnp.broadcast_to(cs_col, (Q, Q))
    return jnp.exp(jnp.where(tri, colb - colb.T, -jnp.inf))


def _ssd_fwd(xbc, dt4, cs4, vecg):
    L = xbc.shape[0]
    nc = L // Q

    def kern(x_ref, b_ref, c_ref, dt_ref, cs_ref, v_ref, y_ref, s_ref, S):
        c = pl.program_id(1)

        @pl.when(c == 0)
        def _():
            S[...] = jnp.zeros_like(S)

        masks = _head_masks()
        ii = lax.broadcasted_iota(jnp.int32, (Q, Q), 0)
        jj = lax.broadcasted_iota(jnp.int32, (Q, Q), 1)
        tri = jj <= ii
        for gi in range(GPS):
            xs, ns = slice(256 * gi, 256 * (gi + 1)), slice(NS * gi, NS * (gi + 1))
            dt4v, cs4v = dt_ref[gi], cs_ref[gi]
            dt_b, cs_b = _expand4(dt4v, masks), _expand4(cs4v, masks)
            d_b = _expand4(v_ref[gi], masks)[1:2]
            cs_last = cs_b[Q - 1:Q, :]
            x4, bm, cm = x_ref[:, xs], b_ref[:, ns], c_ref[:, ns]
            xdt = x4 * dt_b
            gm = _bdot(cm, bm, "nt")
            s4 = S[gi]
            s_ref[gi, 0] = s4
            y = _bdot(cm, s4) * jnp.exp(cs_b) + d_b * x4
            m_all = jnp.concatenate([(gm * _decay_matrix(cs4v[:, j:j + 1], tri)).astype(BF16) for j in range(4)], axis=0)
            yd = _bdot(m_all, xdt)
            for j in range(4):
                y = y + jnp.where(masks[j], yd[Q * j:Q * (j + 1)], 0.0)
            y_ref[:, xs] = y
            S[gi] = jnp.exp(cs_last) * s4 + _bdot(bm, xdt * jnp.exp(cs_last - cs_b), "tn")

    sc = pl.BlockSpec((GPS, Q, 128), lambda g, c: (g, c, 0))
    bw = NS * GPS
    return pl.pallas_call(
        kern, grid=(NG // GPS, nc),
        in_specs=[pl.BlockSpec((Q, 256 * GPS), lambda g, c: (c, g)),
                  pl.BlockSpec((Q, bw), lambda g, c: (c, INNER // bw + g)),
                  pl.BlockSpec((Q, bw), lambda g, c: (c, (INNER + NG * NS) // bw + g)),
                  sc, sc, pl.BlockSpec((GPS, 8, 128), lambda g, c: (g, 0, 0))],
        out_specs=(pl.BlockSpec((Q, 256 * GPS), lambda g, c: (c, g)),
                   pl.BlockSpec((GPS, 1, NS, 256), lambda g, c: (g, c, 0, 0))),
        out_shape=(jax.ShapeDtypeStruct((L, INNER), F32), jax.ShapeDtypeStruct((NG, nc, NS, 256), F32)),
        scratch_shapes=[pltpu.VMEM((GPS, NS, 256), F32)], name="ssd_fwd",
        compiler_params=_cp(("parallel", "arbitrary")),
    )(xbc, xbc, xbc, dt4, cs4, vecg)


def _ssd_bwd(xbc, dt4, cs4, sg4, vecg, s_all, dy):
    L = xbc.shape[0]
    nc = L // Q

    def kern(x_ref, b_ref, c_ref, dt_ref, cs_ref, sg_ref, v_ref, s_ref, dy_ref,
             dx_ref, db_ref, dc_ref, ddt_ref, st_ref, dS):
        cc = pl.program_id(1)

        @pl.when(cc == 0)
        def _():
            dS[...] = jnp.zeros_like(dS)
            st_ref[...] = jnp.zeros_like(st_ref)

        masks = _head_masks()
        ii = lax.broadcasted_iota(jnp.int32, (Q, Q), 0)
        jj = lax.broadcasted_iota(jnp.int32, (Q, Q), 1)
        tri = jj <= ii
        utri = (jj >= ii).astype(BF16)
        hsel = ((lax.broadcasted_iota(jnp.int32, (4 * HD, 128), 0) // HD)
                == lax.broadcasted_iota(jnp.int32, (4 * HD, 128), 1)).astype(BF16)
        hrow = ((lax.broadcasted_iota(jnp.int32, (4 * Q, 128), 0) // Q)
                == lax.broadcasted_iota(jnp.int32, (4 * Q, 128), 1)).astype(BF16)
        ones_q = jnp.ones((Q, 128), BF16)
        lane128 = lax.broadcasted_iota(jnp.int32, (Q, 128), 1)

        for gi in range(GPS):
            xs, ns = slice(256 * gi, 256 * (gi + 1)), slice(NS * gi, NS * (gi + 1))
            dt4v, cs4v, sg4v = dt_ref[gi], cs_ref[gi], sg_ref[gi]
            dt_b, cs_b = _expand4(dt4v, masks), _expand4(cs4v, masks)
            vv = _expand4(v_ref[gi], masks)
            a_b = -jnp.exp(vv[0:1])
            d_b = vv[1:2]
            a4 = -jnp.exp(v_ref[gi][0:1, :])
            cs_last = cs_b[Q - 1:Q, :]
            ecs = jnp.exp(cs_b)
            decay = jnp.exp(cs_last - cs_b)
            elast = jnp.exp(cs_last)
            x4, bm, cm, dyv = x_ref[:, xs], b_ref[:, ns], c_ref[:, ns], dy_ref[:, xs]
            s4 = s_ref[gi, 0]
            dsn = dS[gi]
            xdt = x4 * dt_b
            gm = _bdot(cm, bm, "nt")
            dye = dyv * ecs
            yoff = ecs * _bdot(cm, s4)
            t4 = _bdot(bm, dsn) * decay
            lms, mhs = [], []
            for j in range(4):
                colb = jnp.broadcast_to(cs4v[:, j:j + 1], (Q, Q))
                lms.append(jnp.exp(jnp.where(tri, colb - colb.T, -jnp.inf)))
                mhs.append(gm * lms[j])
            m_all = jnp.concatenate([m.astype(BF16) for m in mhs], axis=0)
            dy_m = jnp.concatenate([jnp.where(masks[j], dyv, 0.0).astype(BF16) for j in range(4)], axis=0)
            dxdt = t4 + _bdot(m_all, dy_m, "tn")
            dm_all = _bdot(dy_m, xdt, "nt")
            dg = jnp.zeros((Q, Q), F32)
            for j in range(4):
                dg = dg + dm_all[Q * j:Q * (j + 1)] * lms[j]
            e_all = dm_all * jnp.concatenate(mhs, axis=0)
            rsum = _dot01(ones_q, e_all, m_left=False, terms=2)
            da4 = -_dot01(hrow, e_all, _DIMS["tn"], m_left=False, terms=2)
            for j in range(4):
                da4 = da4 + jnp.where(lane128 == j, rsum[Q * j:Q * (j + 1)], 0.0)
            xt = xdt * t4
            tail = jnp.sum(xt, axis=0, keepdims=True) + elast * jnp.sum(s4 * dsn, axis=0, keepdims=True)
            gd_raw = jnp.sum(dyv * x4, axis=0, keepdims=True)
            stacked = jnp.concatenate([dyv * yoff - xt, dxdt * x4, jnp.broadcast_to(tail, (8, 4 * HD)),
                                       jnp.broadcast_to(gd_raw, (8, 4 * HD))], axis=0)
            seg = _dot01(hsel, stacked, m_left=False, terms=2)
            dda4 = _dot01(utri, da4 + seg[0:Q], terms=2) + seg[2 * Q:2 * Q + 1]
            ddt_ref[gi] = (dda4 * a4 + seg[Q:2 * Q]) * sg4v
            ga = jnp.sum(dda4 * dt4v * a4, axis=0, keepdims=True)
            row = lax.broadcasted_iota(jnp.int32, (8, 128), 0)
            st_ref[gi] += jnp.where(row == 0, ga, jnp.where(row == 1, seg[2 * Q + 8:2 * Q + 9], 0.0))
            dx_ref[:, xs] = d_b * dyv + dxdt * dt_b
            dc_ref[:, ns] = _bdot(dg, bm) + _bdot(dye, s4, "nt")
            db_ref[:, ns] = _bdot(dg, cm, "tn") + _bdot(xdt * decay, dsn, "nt")
            dS[gi] = elast * dsn + _bdot(cm, dye, "tn")

    rv = lambda c: nc - 1 - c
    sc = pl.BlockSpec((GPS, Q, 128), lambda g, c: (g, rv(c), 0))
    bw = NS * GPS
    return pl.pallas_call(
        kern, grid=(NG // GPS, nc),
        in_specs=[pl.BlockSpec((Q, 256 * GPS), lambda g, c: (rv(c), g)),
                  pl.BlockSpec((Q, bw), lambda g, c: (rv(c), INNER // bw + g)),
                  pl.BlockSpec((Q, bw), lambda g, c: (rv(c), (INNER + NG * NS) // bw + g)),
                  sc, sc, sc, pl.BlockSpec((GPS, 8, 128), lambda g, c: (g, 0, 0)),
                  pl.BlockSpec((GPS, 1, NS, 256), lambda g, c: (g, rv(c), 0, 0)),
                  pl.BlockSpec((Q, 256 * GPS), lambda g, c: (rv(c), g))],
        out_specs=(pl.BlockSpec((Q, 256 * GPS), lambda g, c: (rv(c), g)),
                   pl.BlockSpec((Q, bw), lambda g, c: (rv(c), g)),
                   pl.BlockSpec((Q, bw), lambda g, c: (rv(c), g)),
                   pl.BlockSpec((GPS, Q, 128), lambda g, c: (g, rv(c), 0)),
                   pl.BlockSpec((GPS, 8, 128), lambda g, c: (g, 0, 0))),
        out_shape=(jax.ShapeDtypeStruct((L, INNER), F32), jax.ShapeDtypeStruct((L, NG * NS), F32),
                   jax.ShapeDtypeStruct((L, NG * NS), F32), jax.ShapeDtypeStruct((NG, L, 128), F32),
                   jax.ShapeDtypeStruct((NG, 8, 128), F32)),
        scratch_shapes=[pltpu.VMEM((GPS, NS, 256), F32)], name="ssd_bwd",
        compiler_params=_cp(("parallel", "arbitrary")),
    )(xbc, xbc, xbc, dt4, cs4, sg4, vecg, s_all, dy)


def _dt_bwd(ddt, dproj, tl=256):
    L = ddt.shape[1]

    def kern(d_ref, _, dp_ref, gs_ref):
        @pl.when(pl.program_id(0) == 0)
        def _():
            gs_ref[...] = jnp.zeros_like(gs_ref)

        d = d_ref[0]
        for g in range(1, NG):
            d = d + pltpu.roll(d_ref[g], 4 * g, axis=1)
        gs_ref[...] += jnp.broadcast_to(jnp.sum(d, axis=0, keepdims=True), (8, 128))
        dp_ref[...] = jnp.concatenate([d, jnp.zeros_like(d)], axis=1).astype(BF16)

    return pl.pallas_call(
        kern, grid=(L // tl,),
        in_specs=[pl.BlockSpec((NG, tl, 128), lambda i: (0, i, 0)), pl.BlockSpec(memory_space=pl.ANY)],
        out_specs=(pl.BlockSpec((tl, 256), lambda i: (i, C_DT // 256)), pl.BlockSpec((8, 128), lambda i: (0, 0))),
        out_shape=(jax.ShapeDtypeStruct(dproj.shape, BF16), jax.ShapeDtypeStruct((8, 128), F32)),
        input_output_aliases={1: 0}, name="dt_bwd", compiler_params=_cp(("arbitrary",)),
    )(ddt, dproj)


GW = INNER // NG


def _gnorm_fwd(y, proj, w, tl=256):
    L = y.shape[0]
    zoff = C_Z // 1024

    def kern(y_ref, z_ref, w_ref, o_ref):
        z = z_ref[...].astype(F32)
        yz = y_ref[...] * (z * _sigmoid(z))
        wv = w_ref[...]
        for k in range(1024 // GW):
            sl = slice(GW * k, GW * (k + 1))
            v = yz[:, sl]
            rg = lax.rsqrt(jnp.mean(v * v, axis=-1, keepdims=True) + EPS)
            o_ref[:, sl] = ((v * rg) * wv[:, sl]).astype(BF16)

    blk = pl.BlockSpec((tl, 1024), lambda i, j: (i, j))
    return pl.pallas_call(
        kern, grid=(L // tl, 2),
        in_specs=[blk, pl.BlockSpec((tl, 1024), lambda i, j: (i, zoff + j)), pl.BlockSpec((1, 1024), lambda i, j: (0, j))],
        out_specs=blk, out_shape=jax.ShapeDtypeStruct((L, INNER), BF16), name="gnorm_fwd",
        compiler_params=_cp(("parallel", "parallel")),
    )(y, proj, w.reshape(1, INNER))


def _gnorm_bwd(dbr, wb, y, proj, w, dproj, dep, tl=512):
    L = y.shape[0]
    tl = min(tl, L)
    zoff = C_Z // 1024

    def kern(d_ref, b_ref, y_ref, z_ref, w_ref, _, __, dy_ref, dp_ref, gw_ref):
        @pl.when(pl.program_id(1) == 0)
        def _():
            gw_ref[...] = jnp.zeros_like(gw_ref)

        z = z_ref[...].astype(F32)
        sg = _sigmoid(z)
        sz = z * sg
        yv = y_ref[...]
        yz = yv * sz
        dv = lax.dot_general(d_ref[0], b_ref[...], _DIMS["nt"], preferred_element_type=F32)
        wv = w_ref[...]
        for k in range(1024 // GW):
            sl = slice(GW * k, GW * (k + 1))
            v = yz[:, sl]
            rg = lax.rsqrt(jnp.mean(v * v, axis=-1, keepdims=True) + EPS)
            vn = v * rg
            dk = dv[:, sl]
            gw_ref[:, sl] += jnp.broadcast_to(jnp.sum(dk * vn, axis=0, keepdims=True), (8, GW))
            dvn = dk * wv[:, sl]
            dyz = rg * (dvn - vn * jnp.mean(dvn * vn, axis=-1, keepdims=True))
            dy_ref[:, sl] = dyz * sz[:, sl]
            dp_ref[:, sl] = (dyz * yv[:, sl] * (sg[:, sl] * (1.0 + z[:, sl] * (1.0 - sg[:, sl])))).astype(BF16)

    blk = pl.BlockSpec((tl, 1024), lambda j, i: (i, j))
    zblk = pl.BlockSpec((tl, 1024), lambda j, i: (i, zoff + j))
    return pl.pallas_call(
        kern, grid=(2, L // tl),
        in_specs=[pl.BlockSpec((1, tl, D), lambda j, i: (1, i, 0)), pl.BlockSpec((1024, D), lambda j, i: (j, 0)),
                  blk, zblk, pl.BlockSpec((1, 1024), lambda j, i: (0, j)), pl.BlockSpec(memory_space=pl.ANY),
                  pl.BlockSpec(memory_space=pl.ANY)],
        out_specs=(blk, zblk, pl.BlockSpec((8, 1024), lambda j, i: (0, j))),
        out_shape=(jax.ShapeDtypeStruct((L, INNER), F32), jax.ShapeDtypeStruct(dproj.shape, BF16),
                   jax.ShapeDtypeStruct((8, INNER), F32)),
        input_output_aliases={5: 1}, name="gnorm_bwd", compiler_params=_cp(("parallel", "arbitrary")),
    )(dbr, wb, y, proj, w.reshape(1, INNER), dproj, dep)


def _merge_fwd(proj, bg, br_a, br_b, tl=256):
    L = proj.shape[0]
    goff = C_GATE // 1024

    def kern(g1_ref, g2_ref, b1_ref, b2_ref, a_ref, b_ref, o_ref):
        g1 = _sigmoid(g1_ref[...].astype(F32) + b1_ref[...])
        g2 = _sigmoid(g2_ref[...].astype(F32) + b2_ref[...])
        o_ref[...] = (g1 * a_ref[...] + g2 * b_ref[...]).astype(BF16)

    row = pl.BlockSpec((tl, 1024), lambda i: (i, 0))
    bg2 = bg.reshape(1, 2 * D)
    return pl.pallas_call(
        kern, grid=(L // tl,),
        in_specs=[pl.BlockSpec((tl, 1024), lambda i: (i, goff)), pl.BlockSpec((tl, 1024), lambda i: (i, goff + 1)),
                  pl.BlockSpec((1, 1024), lambda i: (0, 0)), pl.BlockSpec((1, 1024), lambda i: (0, 1)), row, row],
        out_specs=row, out_shape=jax.ShapeDtypeStruct((L, D), BF16), name="merge_fwd",
        compiler_params=_cp(("parallel",)),
    )(proj, proj, bg2, bg2, br_a, br_b)


def _branch_ssm_merge(yb, wb, proj, bg, br_a, tm=512):
    L, K = yb.shape
    tm = min(tm, L)
    goff = C_GATE // 1024

    def kern(a_ref, b_ref, g1_ref, g2_ref, b1_ref, b2_ref, bra_ref, brb_ref, m_ref):
        brb = lax.dot_general(a_ref[...], b_ref[...], _DIMS["nn"], preferred_element_type=F32)
        brb_ref[...] = brb
        g1 = _sigmoid(g1_ref[...].astype(F32) + b1_ref[...])
        g2 = _sigmoid(g2_ref[...].astype(F32) + b2_ref[...])
        m_ref[...] = (g1 * bra_ref[...] + g2 * brb).astype(BF16)

    row = pl.BlockSpec((tm, D), lambda i: (i, 0))
    bg2 = bg.reshape(1, 2 * D)
    return pl.pallas_call(
        kern, grid=(L // tm,),
        in_specs=[pl.BlockSpec((tm, K), lambda i: (i, 0)), pl.BlockSpec((K, D), lambda i: (0, 0)),
                  pl.BlockSpec((tm, D), lambda i: (i, goff)), pl.BlockSpec((tm, D), lambda i: (i, goff + 1)),
                  pl.BlockSpec((1, D), lambda i: (0, 0)), pl.BlockSpec((1, D), lambda i: (0, 1)), row],
        out_specs=(row, row), out_shape=(jax.ShapeDtypeStruct((L, D), F32), jax.ShapeDtypeStruct((L, D), BF16)),
        name="branch_ssm_merge", compiler_params=_cp(("parallel",)),
    )(yb, wb, proj, proj, bg2, bg2, br_a)


def _merge_bwd(dx1, wo, proj, bg, br_a, br_b, dproj, tl=512):
    L = proj.shape[0]
    tl = min(tl, L)
    goff = C_GATE // 1024

    def kern(dm_ref, wo_ref, g_ref, b_ref, a_ref, bb_ref, _, dbr_ref, dp_ref, gb_ref):
        j = pl.program_id(0)

        @pl.when(pl.program_id(1) == 0)
        def _():
            gb_ref[...] = jnp.zeros_like(gb_ref)

        g = _sigmoid(g_ref[...].astype(F32) + b_ref[...])
        br = jnp.where(j == 0, a_ref[...], bb_ref[...])
        dmv = lax.dot_general(dm_ref[...].astype(BF16), wo_ref[...], _DIMS["nt"], preferred_element_type=F32)
        dbr_ref[0] = (dmv * g).astype(BF16)
        dgate = dmv * br * g * (1.0 - g)
        gb_ref[...] += jnp.broadcast_to(jnp.sum(dgate, axis=0, keepdims=True), (8, 1024))
        dp_ref[...] = dgate.astype(BF16)

    row = pl.BlockSpec((tl, 1024), lambda j, i: (i, 0))
    gblk = pl.BlockSpec((tl, 1024), lambda j, i: (i, goff + j))
    return pl.pallas_call(
        kern, grid=(2, L // tl),
        in_specs=[row, pl.BlockSpec((D, D), lambda j, i: (0, 0)), gblk, pl.BlockSpec((1, 1024), lambda j, i: (0, j)),
                  row, row, pl.BlockSpec(memory_space=pl.ANY)],
        out_specs=(pl.BlockSpec((1, tl, 1024), lambda j, i: (j, i, 0)), gblk, pl.BlockSpec((8, 1024), lambda j, i: (0, j))),
        out_shape=(jax.ShapeDtypeStruct((2, L, D), BF16), jax.ShapeDtypeStruct(dproj.shape, BF16),
                   jax.ShapeDtypeStruct((8, 2 * D), F32)),
        input_output_aliases={6: 1}, name="merge_bwd", compiler_params=_cp(("parallel", "arbitrary")),
    )(dx1, wo, proj, bg.reshape(1, 2 * D), br_a, br_b, dproj)


def _coords():
    return lax.axis_index("x"), lax.axis_index("y"), lax.axis_index("c")


def _other_chips(sk):
    xk, yk = sk // 2, sk % 2
    return [((1 - xk, yk), 2 * (1 - xk) + yk), ((xk, 1 - yk), 2 * xk + 1 - yk), ((1 - xk, 1 - yk), 2 * (1 - xk) + 1 - yk)]


def _rows(start, size):
    assert size % 128 == 0
    return pl.ds(pl.multiple_of(start, 128), size)


def _per_chip(fn):
    x, y, _ = _coords()
    s = 2 * x + y
    for sk in range(4):
        pl.when(s == sk)(functools.partial(fn, sk))


XTRA = PIECE - PMAIN


def _place(shard, full_shape, block, index_map, idx, name, blk0=0, nblk=None, dep=None, into=None):
    in_block = block[-2:]
    if nblk is None:
        nblk = shard.shape[0] // in_block[0]

    def kern(idx_ref, s_ref, *rest):
        o_ref = rest[-1]
        o_ref[...] = s_ref[...].astype(BF16).reshape(o_ref.shape)

    extra = ([dep] if dep is not None else []) + ([into] if into is not None else [])
    grid_spec = pltpu.PrefetchScalarGridSpec(
        num_scalar_prefetch=1, grid=(nblk,),
        in_specs=[pl.BlockSpec(in_block, lambda i, idx_ref: (blk0 + i, 0))] + [_ANY] * len(extra),
        out_specs=pl.BlockSpec(block, index_map))
    aliases = {1 + len(extra): 0} if into is not None else {}
    return pl.pallas_call(kern, grid_spec=grid_spec, out_shape=jax.ShapeDtypeStruct(full_shape, BF16), name=name,
                          input_output_aliases=aliases, compiler_params=_cp(("arbitrary",)))(idx, shard, *extra)


_SEM = pl.BlockSpec(memory_space=pltpu.SEMAPHORE)
_EFFECT = pltpu.SideEffectType.DATAFLOW_SIDE_EFFECTING


_ANY = pl.BlockSpec(memory_space=pl.ANY)


def _tie(v, dep, name):
    def body(v_ref, dep_ref, o_ref):
        del v_ref, dep_ref, o_ref

    return pl.pallas_call(body, out_shape=jax.ShapeDtypeStruct(v.shape, v.dtype), in_specs=[_ANY, _ANY],
                          out_specs=_ANY, input_output_aliases={0: 0}, name=name)(v, dep)


def _split_call(name, arrays, start=None, wait=None, wait_sems=None, after=None):
    keys = list(arrays)
    n = len(keys)
    n_start = start.n if start is not None else 0
    afters = [] if after is None else (list(after) if isinstance(after, (list, tuple)) else [after])

    def body(*refs):
        pos = n
        if wait is not None:
            wss, wrs = refs[pos], refs[pos + 1]
            pos += 2
        pos += len(afters)
        if start is not None:
            nss, nrs = refs[pos], refs[pos + 1]
            pos += 2
        R = dict(zip(keys, refs[pos:pos + n]))
        token = refs[pos + n]
        x, y, c = _coords()

        def desc(src, dst, dev, ss, rs, k):
            return pltpu.make_async_remote_copy(src_ref=src, dst_ref=dst, send_sem=ss.at[k], recv_sem=rs.at[k],
                                                device_id=dev, device_id_type=MESH)

        def run(sk):
            if wait is not None:
                for k, (snd, land) in enumerate(wait.copies(sk, R)):
                    if snd is not None:
                        desc(snd[0], snd[1], snd[2], wss, wrs, k).wait_send()
                    if land is not None:
                        desc(land, land, (x, y, c), wss, wrs, k).wait_recv()
            if start is not None:
                for k, (snd, land) in enumerate(start.copies(sk, R)):
                    if snd is not None:
                        desc(snd[0], snd[1], snd[2], nss, nrs, k).start()

        _per_chip(run)
        token[...] = jnp.zeros_like(token)

    hbm = pl.BlockSpec(memory_space=HBM)
    vals = [arrays[k] for k in keys]
    ins, in_specs = list(vals), [hbm] * n
    if wait is not None:
        ins += list(wait_sems)
        in_specs += [_SEM, _SEM]
    ins += afters
    in_specs += [pl.BlockSpec(memory_space=pl.ANY)] * len(afters)
    out_shape, out_specs = [], []
    if start is not None:
        out_shape += [pltpu.SemaphoreType.DMA((n_start,)), pltpu.SemaphoreType.DMA((n_start,))]
        out_specs += [_SEM, _SEM]
    first = len(out_shape)
    out_shape += [jax.ShapeDtypeStruct(v.shape, v.dtype) for v in vals] + [jax.ShapeDtypeStruct((8, 128), F32)]
    out_specs += [hbm] * n + [pl.BlockSpec(memory_space=pltpu.VMEM)]
    res = pl.pallas_call(
        body, out_shape=tuple(out_shape), in_specs=in_specs, out_specs=tuple(out_specs),
        input_output_aliases={i: first + i for i in range(n)}, name=name,
        compiler_params=pltpu.CompilerParams(has_side_effects=_EFFECT),
    )(*ins)
    sems = (res[0], res[1]) if start is not None else None
    return dict(zip(keys, res[first:first + n])), sems, res[-1]


class _Plan:
    def __init__(self, n, copies):
        self.n, self.copies = n, copies


_HM, _HX = PMAIN // 2, XTRA // 2
WAVE0 = 768
WAVES = ((0, WAVE0), (WAVE0, _HM - WAVE0))
_WIN = {
    "wq0": (True, "wct", lambda r, sc, hc: r.at[_rows(PMAIN * sc + _HM * hc + WAVES[0][0], WAVES[0][1]), :]),
    "wq1": (True, "wct", lambda r, sc, hc: r.at[_rows(PMAIN * sc + _HM * hc + WAVES[1][0], WAVES[1][1]), :]),
    "xt": (True, "xt", lambda r, sc, hc: r.at[sc, _rows(_HX * hc, _HX), :]),
    "w1": (True, "w1", lambda r, sc, hc: r.at[_rows(512 * hc, 512), pl.ds(1024 * sc, 1024)]),
    "w2": (True, "w2", lambda r, sc, hc: r.at[_rows(1024 * sc + 512 * hc, 512), :]),
    "wa": (True, "wa", lambda r, sc, hc: r.at[_rows(256 * sc + 128 * hc, 128), :]),
    "wb": (True, "wb", lambda r, sc, hc: r.at[_rows(512 * sc + 256 * hc, 256), :]),
    "wo": (True, "wo", lambda r, sc, hc: r.at[_rows(256 * sc + 128 * hc, 128), :]),
    "cw": (False, "cw", lambda r, sc, hc: r.at[sc]),
}


_PIECE_SRC = {
    "wq0": lambda p, hc: p.at[_rows(_HM * hc + WAVES[0][0], WAVES[0][1]), :],
    "wq1": lambda p, hc: p.at[_rows(_HM * hc + WAVES[1][0], WAVES[1][1]), :],
    "xt": lambda p, hc: p.at[_rows(PMAIN + _HX * hc, _HX), :],
}


def _ag_chips_plan(keys):
    def copies(sk, R):
        _, _, c = _coords()
        out = []
        for key in keys:
            _, arr, win = _WIN[key]
            for (px, py), ps in _other_chips(sk):
                dst = win(R[arr], sk, c)
                src = _PIECE_SRC[key](R["piece"], c) if key in _PIECE_SRC else dst
                out.append(((src, dst, (px, py, c)), win(R[arr], ps, c)))
        return out
    return _Plan(3 * len(keys), copies)


def _ag_sibling_plan(keys):
    keys = [k for k in keys if _WIN[k][0]]

    def copies(sk, R):
        x, y, c = _coords()
        out = []
        for key in keys:
            _, arr, win = _WIN[key]
            for _, ps in _other_chips(sk):
                w = win(R[arr], ps, c)
                out.append(((w, w, (x, y, 1 - c)), win(R[arr], ps, 1 - c)))
        return out
    return _Plan(3 * len(keys), copies)


def _in_proj_wave(h, wct, wave, proj=None, tm=2048):
    L = h.shape[0]
    tm = min(tm, L)
    off, size = WAVES[wave]
    start = lambda j: pl.multiple_of(_HM * j + off, 128)

    def kern(h_ref, w_ref, *rest):
        o_ref = rest[-1]
        o_ref[...] = lax.dot_general(h_ref[...], w_ref[...], _DIMS["nt"], preferred_element_type=F32).astype(BF16)

    in_specs = [pl.BlockSpec((tm, D), lambda j, i: (i, 0)),
                pl.BlockSpec((pl.Element(size), pl.Element(D)), lambda j, i: (start(j), 0))]
    args, aliases = [h, wct], {}
    if proj is not None:
        in_specs.append(pl.BlockSpec(memory_space=pl.ANY))
        args.append(proj)
        aliases = {2: 0}
    return pl.pallas_call(
        kern, grid=(8, L // tm), in_specs=in_specs,
        out_specs=pl.BlockSpec((pl.Element(tm), pl.Element(size)), lambda j, i: (i * tm, start(j))),
        out_shape=jax.ShapeDtypeStruct((L, NCW), BF16), input_output_aliases=aliases,
        name="in_proj_wave%d" % wave, compiler_params=_cp(("parallel", "parallel")),
    )(*args)


def _fix_wct(wct, xt):
    nb = PMAIN // XTRA

    def kern(w_ref, x_ref, o_ref):
        k = pl.program_id(0)
        xv = x_ref[0]
        o_ref[...] = jnp.where(k < 3, (w_ref[...].astype(F32) + xv.astype(F32)).astype(BF16), xv)

    blk = pl.BlockSpec((XTRA, D), lambda k: (nb * (k + 1), 0))
    rblk = pl.BlockSpec((XTRA, D), lambda k: (jnp.where(k < 3, nb * (k + 1), 0), 0))
    return pl.pallas_call(
        kern, grid=(4,), in_specs=[rblk, pl.BlockSpec((1, XTRA, D), lambda k: (k, 0, 0))], out_specs=blk,
        out_shape=jax.ShapeDtypeStruct(wct.shape, BF16), input_output_aliases={0: 0}, name="fix_wct",
        compiler_params=_cp(("arbitrary",)),
    )(wct, xt)


_HP = PIECE // 2
_GWIN = [
    lambda r, sc, hc: r.at[_rows(PMAIN * sc + _HP * hc, _HP), :],
    lambda r, sc, hc: r.at[_rows(512 * hc, 512), pl.ds(1024 * sc, 1024)],
    lambda r, sc, hc: r.at[_rows(1024 * sc + 512 * hc, 512), :],
    lambda r, sc, hc: r.at[_rows(256 * sc + 128 * hc, 128), :],
    lambda r, sc, hc: r.at[_rows(512 * sc + 256 * hc, 256), :],
    lambda r, sc, hc: r.at[_rows(256 * sc + 128 * hc, 128), :],
]
HALF_SHAPES = [(PIECE // 2, D), (512, 1024), (512, 1024), (128, 1024), (256, 1024), (128, 1024)]


def _rs_sibling_plan(ts):
    def copies(sk, R):
        x, y, c = _coords()
        out = []
        for t in ts:
            for sc in range(4):
                land = R["ra%d" % t].at[sc]
                out.append(((_GWIN[t](R["g%d" % t], sc, 1 - c), land, (x, y, 1 - c)), land))
        return out
    return _Plan(4 * len(ts), copies)


def _rs_chips_plan(ts):
    def copies(sk, R):
        _, _, c = _coords()
        out = []
        for t in ts:
            for j, ((px, py), ps) in enumerate(_other_chips(sk)):
                land = R["rb%d" % t].at[j]
                out.append(((R["hb%d" % t].at[ps], land, (px, py, c)), land))
        return out
    return _Plan(3 * len(ts), copies)


def _rs_share_plan(ts):
    def copies(sk, R):
        x, y, c = _coords()
        out = []
        for t in ts:
            rows = HALF_SHAPES[t][0]
            mine = R["f%d" % t].at[_rows(rows * c, rows), :]
            out.append(((mine, mine, (x, y, 1 - c)), R["f%d" % t].at[_rows(rows * (1 - c), rows), :]))
        return out
    return _Plan(len(ts), copies)


def _half_tiling(t):
    rows, cols = HALF_SHAPES[t]
    if t == 0:
        return (rows // 2, cols), 2, lambda i: (i, 0)
    return (rows, cols), 1, lambda i: (0, 0)


def _window_spec(t, blk):
    if t == 0:
        return pl.BlockSpec((pl.Element(blk[0]), pl.Element(blk[1])), lambda i, sc, idx_ref: (
            pl.multiple_of(PMAIN * sc + _HP * idx_ref[1] + blk[0] * i, 128), 0))
    if t == 1:
        return pl.BlockSpec(blk, lambda i, sc, idx_ref: (idx_ref[1], sc))
    return pl.BlockSpec(blk, lambda i, sc, idx_ref: (2 * sc + idx_ref[1], 0))


def _chip_sum(g, ra, t, idx, name):
    rows, cols = HALF_SHAPES[t]
    blk, nblk, inner = _half_tiling(t)

    def kern(idx_ref, g_ref, r_ref, hb_ref, hf_ref):
        v = g_ref[...].astype(F32) + r_ref[0].astype(F32)
        hb_ref[0] = v.astype(BF16)

        @pl.when(pl.program_id(1) == idx_ref[0])
        def _():
            hf_ref[...] = v

    omap = lambda i, sc, idx_ref: (sc,) + inner(i)
    grid_spec = pltpu.PrefetchScalarGridSpec(
        num_scalar_prefetch=1, grid=(nblk, 4),
        in_specs=[_window_spec(t, blk), pl.BlockSpec((1,) + blk, omap)],
        out_specs=(pl.BlockSpec((1,) + blk, omap), pl.BlockSpec(blk, lambda i, sc, idx_ref: inner(i))))
    return pl.pallas_call(
        kern, grid_spec=grid_spec,
        out_shape=(jax.ShapeDtypeStruct((4, rows, cols), BF16), jax.ShapeDtypeStruct((rows, cols), F32)),
        name=name, compiler_params=_cp(("parallel", "arbitrary")),
    )(idx, g, ra)


def _chip_sum_part(g, ra, t, idx, name, own, dep=None):
    rows, cols = HALF_SHAPES[t]
    blk, nblk, inner = _half_tiling(t)
    chip = (lambda k, idx_ref: idx_ref[0]) if own else (lambda k, idx_ref: lax.rem(idx_ref[0] + 1 + k, 4))
    win = _window_spec(t, blk)
    deps = [] if dep is None else [dep]

    def kern(idx_ref, g_ref, r_ref, *rest):
        v = g_ref[...].astype(F32) + r_ref[0].astype(F32)
        if own:
            rest[-1][...] = v
        else:
            rest[-1][0] = v.astype(BF16)

    omap = lambda i, k, idx_ref: (chip(k, idx_ref),) + inner(i)
    grid_spec = pltpu.PrefetchScalarGridSpec(
        num_scalar_prefetch=1, grid=(nblk, 1 if own else 3),
        in_specs=[pl.BlockSpec(win.block_shape, lambda i, k, idx_ref: win.index_map(i, chip(k, idx_ref), idx_ref)),
                  pl.BlockSpec((1,) + blk, omap)] + [pl.BlockSpec(memory_space=pl.ANY)] * len(deps),
        out_specs=pl.BlockSpec(blk, lambda i, k, idx_ref: inner(i)) if own else pl.BlockSpec((1,) + blk, omap))
    return pl.pallas_call(
        kern, grid_spec=grid_spec,
        out_shape=jax.ShapeDtypeStruct((rows, cols), F32) if own else jax.ShapeDtypeStruct((4, rows, cols), BF16),
        name=name, compiler_params=_cp(("parallel", "arbitrary")),
    )(idx, g, ra, *deps)


def _final_sum(hf, rb, t, idx, name):
    rows, cols = HALF_SHAPES[t]
    blk, nblk, inner = _half_tiling(t)
    nbr = rows // blk[0]

    def kern(idx_ref, h_ref, r_ref, o_ref):
        o_ref[...] = ((h_ref[...] + r_ref[0].astype(F32)) + r_ref[1].astype(F32)) + r_ref[2].astype(F32)

    def omap(i, idx_ref):
        r, cidx = inner(i)
        return nbr * idx_ref[1] + r, cidx

    grid_spec = pltpu.PrefetchScalarGridSpec(
        num_scalar_prefetch=1, grid=(nblk,),
        in_specs=[pl.BlockSpec(blk, lambda i, idx_ref: inner(i)),
                  pl.BlockSpec((3,) + blk, lambda i, idx_ref: (0,) + inner(i))],
        out_specs=pl.BlockSpec(blk, omap))
    return pl.pallas_call(
        kern, grid_spec=grid_spec, out_shape=jax.ShapeDtypeStruct((2 * rows, cols), F32),
        name=name, compiler_params=_cp(("parallel",)),
    )(idx, hf, rb)


class _ReduceScatter:
    def __init__(self, ts, grads, idx, tag):
        self.ts, self.idx, self.tag = ts, idx, tag
        arr = {}
        for t in ts:
            arr["g%d" % t] = grads[t]
            arr["ra%d" % t] = lax.empty((4,) + HALF_SHAPES[t], BF16)
        self.plan = _rs_sibling_plan(ts)
        self.arr, self.sems, self.token = _split_call("rs_sibling_start_" + tag, arr, start=self.plan)

    def chips(self, after, own_later=False):
        arr, _, _ = _split_call("rs_sibling_wait_" + self.tag, self.arr, wait=self.plan, wait_sems=self.sems, after=after)
        brr, self.hf = {}, {}
        for t in self.ts:
            if own_later:
                hb = _chip_sum_part(arr["g%d" % t], arr["ra%d" % t], t, self.idx, "chip_sum_others_%d" % t, False)
            else:
                hb, self.hf[t] = _chip_sum(arr["g%d" % t], arr["ra%d" % t], t, self.idx, "chip_sum_%d" % t)
            brr["hb%d" % t] = hb
            brr["rb%d" % t] = lax.empty((3,) + HALF_SHAPES[t], BF16)
        self.plan = _rs_chips_plan(self.ts)
        self.arr, self.sems, self.token = _split_call("rs_chips_start_" + self.tag, brr, start=self.plan)
        if own_later:
            for t in self.ts:
                self.hf[t] = _chip_sum_part(arr["g%d" % t], arr["ra%d" % t], t, self.idx, "chip_sum_own_%d" % t, True,
                                            dep=self.token)
        return self.token

    def share(self, after):
        brr, _, _ = _split_call("rs_chips_wait_" + self.tag, self.arr, wait=self.plan, wait_sems=self.sems, after=after)
        frr = {"f%d" % t: _final_sum(self.hf[t], brr["rb%d" % t], t, self.idx, "final_sum_%d" % t) for t in self.ts}
        self.plan = _rs_share_plan(self.ts)
        self.arr, self.sems, self.token = _split_call("rs_share_start_" + self.tag, frr, start=self.plan)
        return self.token

    def result(self, after):
        frr, _, _ = _split_call("rs_share_wait_" + self.tag, self.arr, wait=self.plan, wait_sems=self.sems, after=after)
        return {t: frr["f%d" % t] for t in self.ts}


def _all8_plan(key):
    def copies(sk, R):
        x, y, c = _coords()
        own = R[key].at[4 * x + 2 * y + c]
        out = []
        for k in range(1, 8):
            dev = ((1 - x) if (k >> 2) & 1 else x, (1 - y) if (k >> 1) & 1 else y, (1 - c) if k & 1 else c)
            out.append(((own, own, dev), R[key].at[4 * dev[0] + 2 * dev[1] + dev[2]]))
        return out
    return _Plan(7, copies)


def _sum8(v, name="small_sum"):
    def kern(v_ref, o_ref):
        acc = v_ref[0]
        for k in range(1, 8):
            acc = acc + v_ref[k]
        o_ref[...] = acc

    return pl.pallas_call(kern, out_shape=jax.ShapeDtypeStruct(v.shape[1:], F32), name=name)(v)


def _adamw(w, g, m, v, name, tr=128, blk0=0, nblk=None, into=None, copy_g=False):
    R, C = w.shape
    tr = min(tr, R)
    if nblk is None:
        assert R % tr == 0 and blk0 == 0
        nblk = R // tr
    n_out = 4 if copy_g else 3

    def kern(*refs):
        w_ref, g_ref, m_ref, v_ref = refs[:4]
        d_ref, mo_ref, vo_ref = refs[-n_out:][:3]
        gv = g_ref[...]
        mn = ADAM_B1 * m_ref[...] + (1.0 - ADAM_B1) * gv
        vn = ADAM_B2 * v_ref[...] + (1.0 - ADAM_B2) * (gv * gv)
        m_hat = mn / (1.0 - ADAM_B1 ** ADAM_STEP)
        v_hat = vn / (1.0 - ADAM_B2 ** ADAM_STEP)
        d_ref[...] = -ADAM_LR * (m_hat / (jnp.sqrt(v_hat) + ADAM_EPS) + ADAM_WD * w_ref[...])
        mo_ref[...] = mn
        vo_ref[...] = vn
        if copy_g:
            refs[-1][...] = gv

    blk = pl.BlockSpec((tr, C), lambda i: (blk0 + i, 0))
    sd = jax.ShapeDtypeStruct((R, C), F32)
    in_specs, args, aliases = [blk] * 4, [w, g, m, v], {}
    if into is not None:
        in_specs += [pl.BlockSpec(memory_space=pl.ANY)] * 3
        args += list(into)
        aliases = {4: 0, 5: 1, 6: 2}
    return pl.pallas_call(kern, grid=(nblk,), in_specs=in_specs, out_specs=(blk,) * n_out, out_shape=(sd,) * n_out,
                          input_output_aliases=aliases, name=name, compiler_params=_cp(("parallel",)))(*args)


def _adamw_w_in(wt, gp, mt, vt, offs, name, r0, tr, nblk, views, into=None, blk_key=None):
    el = lambda n: (pl.Element(n), pl.Element(D))
    first = (lambda o: r0) if blk_key is None else (lambda o: tr * o[blk_key])
    own = pl.BlockSpec(el(tr), lambda i, o: (pl.multiple_of(first(o) + tr * i, 8), 0))

    def view(k):
        return pl.BlockSpec(el(tr), lambda i, o: (pl.multiple_of(jnp.maximum(first(o) + tr * i + o[k], 0), 8), 0))

    def kern(o_ref, w_ref, m_ref, v_ref, *refs):
        g_refs, (d_ref, mo_ref, vo_ref, go_ref) = refs[:len(views)], refs[-4:]
        gv = g_refs[0][...]
        if len(views) == 2:
            row = first(o_ref) + tr * pl.program_id(0) + lax.broadcasted_iota(jnp.int32, (tr, D), 0)
            gv = jnp.where(row < o_ref[2], gv, g_refs[1][...])
        mn = ADAM_B1 * m_ref[...] + (1.0 - ADAM_B1) * gv
        vn = ADAM_B2 * v_ref[...] + (1.0 - ADAM_B2) * (gv * gv)
        m_hat = mn / (1.0 - ADAM_B1 ** ADAM_STEP)
        v_hat = vn / (1.0 - ADAM_B2 ** ADAM_STEP)
        d_ref[...] = -ADAM_LR * (m_hat / (jnp.sqrt(v_hat) + ADAM_EPS) + ADAM_WD * w_ref[...])
        mo_ref[...] = mn
        vo_ref[...] = vn
        go_ref[...] = gv

    in_specs = [own, own, own] + [view(k) for k in views]
    args = [wt, mt, vt] + [gp] * len(views)
    aliases = {}
    if into is not None:
        in_specs += [pl.BlockSpec(memory_space=pl.ANY)] * 4
        args += list(into)
        aliases = {1 + len(args) - 4 + j: j for j in range(4)}
    grid_spec = pltpu.PrefetchScalarGridSpec(num_scalar_prefetch=1, grid=(nblk,), in_specs=in_specs,
                                             out_specs=(own,) * 4)
    sd = jax.ShapeDtypeStruct(wt.shape, F32)
    return pl.pallas_call(kern, grid_spec=grid_spec, out_shape=(sd,) * 4, input_output_aliases=aliases, name=name,
                          compiler_params=_cp(("parallel",)))(offs, *args)


def _to_piece(wt, s):
    z = lambda n: jnp.zeros((n, D), wt.dtype)
    pads = [functools.partial(lambda k, w: jnp.pad(w, ((8 * k, PIECE - W_SHARD - 8 * k), (0, 0))).astype(BF16), k)
            for k in range(3)]
    last = lambda w: jnp.concatenate([z(24), w[:744], w[776:], w[744:776], z(PIECE - 24 - W_SHARD)], axis=0).astype(BF16)
    return lax.switch(s, pads + [last], wt)


_SMALL = [("b_gate", 2048), ("ssm_conv_b", 4096), ("dt_bias", 32), ("A_log", 32), ("D_skip", 32),
          ("ssm_norm_w", 2048), ("norm_mlp", 1024), ("norm_final", 1024), ("sc_conv_w", 3072), ("ssm_conv_w", 16384),
          ("loss", 1)]


def _pack(vals, table, rows):
    parts = []
    for name, n in table:
        v = vals[name].reshape(-1).astype(F32)
        pad = (-n) % 128
        parts.append(jnp.pad(v, (0, pad)) if pad else v)
    flat = jnp.concatenate(parts)
    return jnp.pad(flat, (0, rows * 128 - flat.shape[0])).reshape(rows, 128)


def _unpack(arr, table):
    flat = arr.reshape(-1)
    out, off = {}, 0
    for name, n in table:
        out[name] = flat[off:off + n]
        off += n + ((-n) % 128)
    return out


def kernel(x, norm_mix, w_in, b_gate, sc_conv_w, ssm_conv_w, ssm_conv_b, dt_bias, A_log, D_skip, ssm_norm_w, w_branch_sc, w_branch_ssm, w_out, norm_mlp, w_mlp1, w_mlp2, norm_final, loss_target, m_norm_mix, m_w_in, m_b_gate, m_sc_conv_w, m_ssm_conv_w, m_ssm_conv_b, m_dt_bias, m_A_log, m_D_skip, m_ssm_norm_w, m_w_branch_sc, m_w_branch_ssm, m_w_out, m_norm_mlp, m_w_mlp1, m_w_mlp2, m_norm_final, v_norm_mix, v_w_in, v_b_gate, v_sc_conv_w, v_ssm_conv_w, v_ssm_conv_b, v_dt_bias, v_A_log, v_D_skip, v_ssm_norm_w, v_w_branch_sc, v_w_branch_ssm, v_w_out, v_norm_mlp, v_w_mlp1, v_w_mlp2, v_norm_final):
    L = x.shape[1]
    nc = L // Q
    xi, yi, ci = lax.axis_index("x"), lax.axis_index("y"), lax.axis_index("c")
    s = 2 * xi + yi
    idx = jnp.stack([s, ci]).astype(jnp.int32)
    x0 = x.reshape(L, D)
    tgt = loss_target.reshape(L, D)
    small_names = ["b_gate", "sc_conv_w", "ssm_conv_w", "ssm_conv_b", "dt_bias", "A_log", "D_skip", "ssm_norm_w",
                   "norm_mlp", "norm_final"]
    small_wmv = [dict(zip(small_names, vals)) for vals in (
        (b_gate, sc_conv_w, ssm_conv_w, ssm_conv_b, dt_bias, A_log, D_skip, ssm_norm_w, norm_mlp, norm_final),
        (m_b_gate, m_sc_conv_w, m_ssm_conv_w, m_ssm_conv_b, m_dt_bias, m_A_log, m_D_skip, m_ssm_norm_w, m_norm_mlp,
         m_norm_final),
        (v_b_gate, v_sc_conv_w, v_ssm_conv_w, v_ssm_conv_b, v_dt_bias, v_A_log, v_D_skip, v_ssm_norm_w, v_norm_mlp,
         v_norm_final))]
    small_table = [(n, int(small_wmv[0][n].size)) for n in small_names]
    small_rows = 136
    pk_w, pk_m, pk_v = [_pack(d, small_table, small_rows) for d in small_wmv]

    piece = _to_piece(w_in.T, s)
    nb = PMAIN // XTRA
    cws = jnp.zeros((8, 1280), F32)
    cws = cws.at[0:3, 0:256].set(sc_conv_w).at[0:4, 256:1280].set(ssm_conv_w)
    cw0 = lax.dynamic_update_slice(jnp.zeros((4, 8, 1280), F32), cws[None], (s, 0, 0))
    win_keys, win2_keys, mid_keys, end_keys = ["xt", "cw", "wq0"], ["wq1"], ["wa", "wb", "wo", "w1"], ["w2"]
    gw, sems_w, tok = _split_call(
        "ag_win_start", {"wct": lax.empty((NCW, D), BF16), "xt": lax.empty((4, XTRA, D), BF16), "cw": cw0, "piece": piece},
        start=_ag_chips_plan(win_keys))
    g2, sems_w2, tok = _split_call("ag_win2_start", {"wct": gw["wct"], "piece": gw["piece"]},
                                   start=_ag_chips_plan(win2_keys), after=tok)
    piece = g2["piece"]
    gw["wct"] = _place(piece, (NCW, D), (XTRA, D), lambda i, r: (nb * r[0] + i, 0), idx, "place_wct", nblk=nb,
                       dep=tok, into=g2["wct"])
    gw["xt"] = _place(piece, (4, XTRA, D), (1, XTRA, D), lambda i, r: (r[0], 0, 0), idx, "place_xt", blk0=nb, nblk=1,
                      dep=tok, into=gw["xt"])
    gw["piece"] = piece
    wa0 = _place(w_branch_sc, (D, D), (256, 1024), lambda i, r: (r[0], 0), idx, "place_wa", dep=tok)
    wb0 = _place(w_branch_ssm, (INNER, D), (512, 1024), lambda i, r: (r[0], 0), idx, "place_wb", dep=tok)
    wo0 = _place(w_out, (D, D), (256, 1024), lambda i, r: (r[0], 0), idx, "place_wo", dep=tok)
    w10 = _place(w_mlp1, (D, DFF), (256, 1024), lambda i, r: (i, r[0]), idx, "place_w1", dep=tok)
    gm, sems_m, tok = _split_call("ag_mid_start", {"wa": wa0, "wb": wb0, "wo": wo0, "w1": w10},
                                  start=_ag_chips_plan(mid_keys))
    w20 = _place(w_mlp2, (DFF, D), (256, 1024), lambda i, r: (4 * r[0] + i, 0), idx, "place_w2", dep=tok)
    ge, sems_e, tok = _split_call("ag_end_start", {"w2": w20}, start=_ag_chips_plan(end_keys))
    h = _rms_fwd(x0, norm_mix, "rms_mix", dep=tok)
    gw, sems_w, tok = _split_call("ag_win_pass", gw, wait=_ag_chips_plan(win_keys), wait_sems=sems_w,
                                  start=_ag_sibling_plan(win_keys), after=[h, pk_w, pk_m, pk_v])
    gw, _, _ = _split_call("ag_win_done", gw, wait=_ag_sibling_plan(win_keys), wait_sems=sems_w, after=tok)
    wc, cw_all = _fix_wct(gw["wct"], gw["xt"]), gw["cw"]
    sc_w_full = jnp.concatenate([cw_all[k, :, 0:256] for k in range(4)], axis=1)
    ssm_w_full = jnp.concatenate([cw_all[k, :, 256:1280] for k in range(4)], axis=1)
    cw4 = ssm_w_full.at[4].set(ssm_conv_b)
    vec = jnp.zeros((8, 128), F32).at[0, :NH].set(dt_bias).at[1, :NH].set(A_log)
    vecg = jnp.zeros((NG, 8, 128), F32).at[:, 0, :4].set(A_log.reshape(NG, 4)).at[:, 1, :4].set(D_skip.reshape(NG, 4))

    dtraw = _matmul(h, wc[C_DT:], "nt", F32, 512, 256, 1024, "in_proj_dt")
    proj = _in_proj_wave(h, wc, 0)
    g2, sems_w2, tok = _split_call("ag_win2_pass", {"wct": wc, "piece": gw["piece"]},
                                   wait=_ag_chips_plan(win2_keys), wait_sems=sems_w2,
                                   start=_ag_sibling_plan(win2_keys), after=[proj, dtraw])
    g2, _, _ = _split_call("ag_win2_done", g2, wait=_ag_sibling_plan(win2_keys), wait_sems=sems_w2, after=tok)
    wc = g2["wct"]
    proj = _in_proj_wave(h, wc, 1, proj=proj)
    ya = _sc_fwd(proj, sc_w_full)
    xbc = _ssm_conv_fwd(proj, cw4)
    dt4, cs4, sg4 = _dt_prep(dtraw, vec)
    y, s_all = _ssd_fwd(xbc, dt4, cs4, vecg)
    gm, sems_m, tok = _split_call("ag_mid_pass", gm, wait=_ag_chips_plan(mid_keys), wait_sems=sems_m,
                                  start=_ag_sibling_plan(mid_keys), after=[y, ya])
    y = _tie(y, tok, "tie_y")
    yb = _gnorm_fwd(y, proj, ssm_norm_w)
    gm, _, _ = _split_call("ag_mid_done", gm, wait=_ag_sibling_plan(mid_keys), wait_sems=sems_m, after=yb)
    wa, wb, wo, w1 = gm["wa"], gm["wb"], gm["wo"], gm["w1"]
    ge, sems_e, tok = _split_call("ag_end_pass", ge, wait=_ag_chips_plan(end_keys), wait_sems=sems_e,
                                  start=_ag_sibling_plan(end_keys), after=yb)
    br_a = _matmul(ya, wa, "nn", F32, 1024, 1024, 1024, "branch_sc", dep=tok)
    br_b, merged = _branch_ssm_merge(yb, wb, proj, b_gate, br_a)
    x1, h2 = _matmul_res_rms(merged, wo, x0, norm_mlp, 1024, "out_proj")
    a1, rl = _matmul(h2, w1, "nn", BF16, 1024, 1024, 1024, "mlp1", epi="relu2", n_outer=True)
    ge, _, _ = _split_call("ag_end_done", ge, wait=_ag_sibling_plan(end_keys), wait_sems=sems_e, after=a1)
    w2 = ge["w2"]
    dx2, g_nf, loss8 = _matmul_res_final(rl, w2, x1, norm_final, tgt, 512, "mlp2")

    da = _matmul(dx2, w2, "nt", BF16, 1024, 1024, 1024, "mlp2_dx", epi="drelu", extra=a1, n_outer=True)
    g_w2 = _matmul(rl, dx2, "tn", BF16, 1024, 1024, 2048, "mlp2_dw")
    g_w1 = _matmul(h2, da, "tn", BF16, 1024, 1024, 2048, "mlp1_dw")
    dx1, g_nmlp = _matmul_rms_bwd(da, w1, "nt", x1, norm_mlp, dx2, 512, 4096, "mlp1_dx")
    g_wo = _matmul(merged, dx1, "tn", BF16, 1024, 1024, 2048, "out_proj_dw")
    dproj = lax.empty((L, NCW), BF16)
    dbr, dproj, g_bg = _merge_bwd(dx1, wo, proj, b_gate, br_a, br_b, dproj)
    dya = _matmul(dbr[0], wa, "nt", F32, 1024, 1024, 1024, "branch_sc_dx")
    g_wa = _matmul(ya, dbr[0], "tn", BF16, 1024, 1024, 2048, "branch_sc_dw")
    dproj, g_scw = _sc_bwd(dya, proj, sc_w_full, dproj)
    g_wb = _matmul(yb, dbr[1], "tn", BF16, 1024, 1024, 2048, "branch_ssm_dw")
    rs_a = _ReduceScatter([1, 2, 3, 4, 5], {1: g_w1, 2: g_w2, 3: g_wa, 4: g_wb, 5: g_wo}, idx, "a")
    dy, dproj, g_snw = _gnorm_bwd(dbr, wb, y, proj, ssm_norm_w, dproj, rs_a.token)
    tok = rs_a.chips(after=dy)
    dxs, dbm, dcm, ddt_g, st = _ssd_bwd(xbc, dt4, cs4, sg4, vecg, s_all, _tie(dy, tok, "tie_dy"))
    dproj, gx1 = _ssm_conv_bwd(dxs, proj, cw4, dproj, 0, "ssm_conv_bwd_x")
    dproj, gx2 = _ssm_conv_bwd(dbm, proj, cw4, dproj, INNER, "ssm_conv_bwd_b")
    dproj, gx3 = _ssm_conv_bwd(dcm, proj, cw4, dproj, INNER + NG * NS, "ssm_conv_bwd_c")
    g_cw4 = jnp.concatenate([gx1, gx2, gx3], axis=1)
    dproj, g_dtb = _dt_bwd(ddt_g, dproj)
    small = {"b_gate": g_bg[0], "ssm_conv_b": g_cw4[4], "dt_bias": g_dtb[0, :NH],
             "A_log": st[:, 0, :4], "D_skip": st[:, 1, :4], "ssm_norm_w": g_snw[0], "norm_mlp": g_nmlp[0],
             "norm_final": g_nf[0], "sc_conv_w": g_scw[0:3], "ssm_conv_w": g_cw4[0:4], "loss": loss8[0, 0:1]}
    me = 4 * xi + 2 * yi + ci
    sm8 = lax.dynamic_update_slice(jnp.zeros((8, SMALL_ROWS, 128), F32), _pack(small, _SMALL, SMALL_ROWS)[None], (me, 0, 0))
    sm_arr, sm_sems, tok = _split_call("small_start", {"sm": sm8}, start=_all8_plan("sm"))
    g_wc = _matmul(dproj, h, "tn", BF16, 1280, 1024, 2048, "in_proj_dw", dep=tok)
    rs_b = _ReduceScatter([0], {0: g_wc}, idx, "b")
    tok = rs_a.share(after=rs_b.token)
    tok = rs_b.chips(after=tok, own_later=True)
    grad_x, g_nm = _matmul_rms_bwd(dproj, wc, "nn", x0, norm_mix, dx1, 512, 3840, "in_proj_dx", dep=rs_b.hf[0])
    nm8 = lax.dynamic_update_slice(jnp.zeros((8, 8, 128), F32), g_nm[0].reshape(1, 8, 128), (me, 0, 0))
    nm_arr, nm_sems, tok = _split_call("norm_mix_start", {"nm": nm8}, start=_all8_plan("nm"))
    sm_arr, _, _ = _split_call("small_wait", sm_arr, wait=_all8_plan("sm"), wait_sems=sm_sems, after=tok)
    small_sum = _sum8(sm_arr["sm"])
    gs = _unpack(small_sum, _SMALL)
    red = rs_a.result(after=tok)
    big = {"w_mlp1": red[1], "w_mlp2": red[2], "w_branch_sc": red[3], "w_branch_ssm": red[4], "w_out": red[5]}

    given = dict(norm_mix=norm_mix, w_in=w_in, b_gate=b_gate, sc_conv_w=sc_conv_w, ssm_conv_w=ssm_conv_w, ssm_conv_b=ssm_conv_b, dt_bias=dt_bias, A_log=A_log, D_skip=D_skip, ssm_norm_w=ssm_norm_w, w_branch_sc=w_branch_sc, w_branch_ssm=w_branch_ssm, w_out=w_out, norm_mlp=norm_mlp, w_mlp1=w_mlp1, w_mlp2=w_mlp2, norm_final=norm_final,
                 m_norm_mix=m_norm_mix, m_w_in=m_w_in, m_b_gate=m_b_gate, m_sc_conv_w=m_sc_conv_w, m_ssm_conv_w=m_ssm_conv_w, m_ssm_conv_b=m_ssm_conv_b, m_dt_bias=m_dt_bias, m_A_log=m_A_log, m_D_skip=m_D_skip, m_ssm_norm_w=m_ssm_norm_w, m_w_branch_sc=m_w_branch_sc, m_w_branch_ssm=m_w_branch_ssm, m_w_out=m_w_out, m_norm_mlp=m_norm_mlp, m_w_mlp1=m_w_mlp1, m_w_mlp2=m_w_mlp2, m_norm_final=m_norm_final,
                 v_norm_mix=v_norm_mix, v_w_in=v_w_in, v_b_gate=v_b_gate, v_sc_conv_w=v_sc_conv_w, v_ssm_conv_w=v_ssm_conv_w, v_ssm_conv_b=v_ssm_conv_b, v_dt_bias=v_dt_bias, v_A_log=v_A_log, v_D_skip=v_D_skip, v_ssm_norm_w=v_ssm_norm_w, v_w_branch_sc=v_w_branch_sc, v_w_branch_ssm=v_w_branch_ssm, v_w_out=v_w_out, v_norm_mlp=v_norm_mlp, v_w_mlp1=v_w_mlp1, v_w_mlp2=v_w_mlp2, v_norm_final=v_norm_final)
    order = ["norm_mix", "w_in", "b_gate", "sc_conv_w", "ssm_conv_w", "ssm_conv_b", "dt_bias", "A_log", "D_skip",
             "ssm_norm_w", "w_branch_sc", "w_branch_ssm", "w_out", "norm_mlp", "w_mlp1", "w_mlp2", "norm_final"]
    grad, delta, new_m, new_v = {}, {}, {}, {}
    for n in big:
        delta[n], new_m[n], new_v[n], grad[n] = _adamw(given[n], big[n], given["m_" + n], given["v_" + n],
                                                       "adamw_" + n, copy_g=True)
    big["w_in"] = None
    grad_small = {n: gs[n].reshape(given[n].shape) for n in small_names if n not in ("sc_conv_w", "ssm_conv_w")}
    grad_small["sc_conv_w"] = lax.dynamic_slice(gs["sc_conv_w"].reshape(3, D), (0, 256 * s), (3, 256))
    grad_small["ssm_conv_w"] = lax.dynamic_slice(gs["ssm_conv_w"].reshape(4, XBC), (0, 1024 * s), (4, 1024))
    table = small_table
    ds_, ms_, vs_ = _adamw(pk_w, _pack(grad_small, table, small_rows), pk_m, pk_v, "adamw_small", tr=small_rows)
    ds_, ms_, vs_ = _unpack(ds_, table), _unpack(ms_, table), _unpack(vs_, table)
    for n in grad_small:
        shp = given[n].shape
        grad[n] = grad_small[n]
        delta[n], new_m[n], new_v[n] = ds_[n].reshape(shp), ms_[n].reshape(shp), vs_[n].reshape(shp)

    done = [new_v[n] for n in ("w_mlp1", "w_mlp2", "w_branch_sc", "w_branch_ssm", "w_out")] + [vs_["b_gate"]]
    tok = rs_b.share(after=done)
    offs = jnp.where(s == 3, jnp.array([24, -8, 744, 2072, -8], jnp.int32),
                     jnp.stack([8 * s, 8 * s, 0 * s, 8 * s, 8 * s]).astype(jnp.int32))
    offs = jnp.concatenate([offs, jnp.stack([7 * ci, 4 - 4 * ci]).astype(jnp.int32)])
    nmain = W_SHARD // 256
    wt_own = (w_in.T, rs_b.arr["f0"], m_w_in.T, v_w_in.T, offs)
    res = _adamw_w_in(*wt_own, "adamw_w_in_own", 0, 256, 4, (0, 1), blk_key=5)
    gp = rs_b.result(after=[tok, res[0]])[0]
    wt_args = (w_in.T, gp, m_w_in.T, v_w_in.T, offs)
    res = _adamw_w_in(*wt_args, "adamw_w_in", 0, 256, nmain - 4, (0, 1), into=res, blk_key=6)
    res = _adamw_w_in(*wt_args, "adamw_w_in_dt", 744, 32, 1, (3,), into=res)
    dt_, mt_, vt_, gwt = _adamw_w_in(*wt_args, "adamw_w_in_tail", 256 * nmain, 8, 1, (4,), into=res)
    grad["w_in"], delta["w_in"], new_m["w_in"], new_v["w_in"] = gwt.T, dt_.T, mt_.T, vt_.T
    nm_arr, _, _ = _split_call("norm_mix_wait", nm_arr, wait=_all8_plan("nm"), wait_sems=nm_sems, after=tok)
    g8 = _sum8(nm_arr["nm"], "norm_mix_sum")
    r8 = lambda a: a.reshape(8, 128)
    d8, m8, v8 = _adamw(r8(norm_mix), g8, r8(m_norm_mix), r8(v_norm_mix), "adamw_norm_mix", tr=8)
    grad["norm_mix"], delta["norm_mix"] = g8.reshape(D), d8.reshape(D)
    new_m["norm_mix"], new_v["norm_mix"] = m8.reshape(D), v8.reshape(D)

    loss = gs["loss"].reshape(())
    return (loss, grad_x.reshape(1, L, D), *[grad[n] for n in order], *[delta[n] for n in order],
            *[new_m[n] for n in order], *[new_v[n] for n in order])
```

```python
import functools

import jax
import jax.numpy as jnp
from jax import lax
from jax.experimental import pallas as pl
from jax.experimental.pallas import tpu as pltpu

F32 = jnp.float32
BF16 = jnp.bfloat16
MESH = pl.DeviceIdType.MESH
HBM = pltpu.HBM

D = 1024
INNER = 2048
HD = 64
NH = 32
NG = 8
NS = 128
Q = 128
GPS = 8
XBC = 4096
DFF = 4096
EPS = 1e-6
W_SHARD = 2824
NCW = 11520
PIECE = 3072
PMAIN = 2816
C_Z, C_XBC, C_GATE, C_DT = 3072, 5120, 9216, 11264
SMALL_ROWS = 256
VMEM_LIMIT = 56 * 1024 * 1024

ADAM_LR, ADAM_B1, ADAM_B2, ADAM_EPS, ADAM_WD, ADAM_STEP = 0.001, 0.9, 0.999, 1e-08, 0.01, 10


VMEM_SMALL = 52 * 1024 * 1024


def _cp(sem=None, vmem=VMEM_SMALL):
    return pltpu.CompilerParams(dimension_semantics=sem, vmem_limit_bytes=vmem)


def _sigmoid(v):
    return 1.0 / (1.0 + jnp.exp(-v))


_DIMS = {"nn": (((1,), (0,)), ((), ())), "nt": (((1,), (1,)), ((), ())), "tn": (((0,), (0,)), ((), ()))}


def _matmul(a, b, mode, out_dtype, tm, tn, tk, name, epi=None, extra=None, n_outer=False, dep=None):
    if mode == "tn":
        K, M = a.shape
    else:
        M, K = a.shape
    N = b.shape[0] if mode == "nt" else b.shape[1]
    tm, tn, tk = min(tm, M), min(tn, N), min(tk, K)
    assert M % tm == 0 and N % tn == 0 and K % tk == 0, (name, M, N, K, tm, tn, tk)
    nm, nn, nk = M // tm, N // tn, K // tk
    dims = _DIMS[mode]

    def ij(p0, p1):
        return (p1, p0) if n_outer else (p0, p1)

    if mode == "tn":
        a_spec = pl.BlockSpec((tk, tm), lambda p0, p1, k: (k, ij(p0, p1)[0]))
    else:
        a_spec = pl.BlockSpec((tm, tk), lambda p0, p1, k: (ij(p0, p1)[0], k))
    if mode == "nt":
        b_spec = pl.BlockSpec((tn, tk), lambda p0, p1, k: (ij(p0, p1)[1], k))
    else:
        b_spec = pl.BlockSpec((tk, tn), lambda p0, p1, k: (k, ij(p0, p1)[1]))
    o_spec = pl.BlockSpec((tm, tn), lambda p0, p1, k: ij(p0, p1))
    in_specs = [a_spec, b_spec]
    args = [a, b]
    if epi in ("res", "drelu"):
        in_specs.append(o_spec)
        args.append(extra)
    if dep is not None:
        in_specs.append(pl.BlockSpec(memory_space=pl.ANY))
        args.append(dep)
    n_in = len(args)
    if epi == "relu2":
        out_shape = (jax.ShapeDtypeStruct((M, N), out_dtype), jax.ShapeDtypeStruct((M, N), BF16))
        out_specs = (o_spec, o_spec)
    else:
        out_shape = jax.ShapeDtypeStruct((M, N), out_dtype)
        out_specs = o_spec

    def kern(*refs):
        a_ref, b_ref = refs[0], refs[1]
        e_ref = refs[2] if epi in ("res", "drelu") else None
        acc = refs[-1]
        outs = refs[n_in:-1] if nk > 1 else refs[n_in:]
        k = pl.program_id(2)

        def product():
            return lax.dot_general(a_ref[...].astype(BF16), b_ref[...].astype(BF16), dims, preferred_element_type=F32)

        def finish(r):
            if epi is None:
                outs[0][...] = r.astype(out_dtype)
            elif epi == "res":
                outs[0][...] = (r + e_ref[...]).astype(out_dtype)
            elif epi == "relu2":
                outs[0][...] = r.astype(out_dtype)
                t = jnp.maximum(r, 0.0)
                outs[1][...] = (t * t).astype(BF16)
            else:
                outs[0][...] = (r * (2.0 * jnp.maximum(e_ref[...].astype(F32), 0.0))).astype(out_dtype)

        if nk == 1:
            finish(product())
        else:
            @pl.when(k == 0)
            def _():
                acc[...] = jnp.zeros_like(acc)

            acc[...] += product()

            @pl.when(k == nk - 1)
            def _():
                finish(acc[...])

    grid = (nn, nm, nk) if n_outer else (nm, nn, nk)
    return pl.pallas_call(
        kern, grid=grid, in_specs=in_specs, out_specs=out_specs, out_shape=out_shape,
        scratch_shapes=[pltpu.VMEM((tm, tn), F32)] if nk > 1 else [], name=name,
        compiler_params=_cp(("parallel", "parallel", "arbitrary")),
    )(*args)


def _matmul_res_rms(a, b, x, w, tm, name):
    M, K = a.shape
    tm = min(tm, M)

    def kern(a_ref, b_ref, x_ref, w_ref, x1_ref, h_ref):
        x1 = x_ref[...] + lax.dot_general(a_ref[...].astype(BF16), b_ref[...].astype(BF16), _DIMS["nn"],
                                          preferred_element_type=F32)
        x1_ref[...] = x1
        r = lax.rsqrt(jnp.mean(x1 * x1, axis=-1, keepdims=True) + EPS)
        h_ref[...] = ((x1 * r) * w_ref[...]).astype(BF16)

    row = pl.BlockSpec((tm, D), lambda i: (i, 0))
    return pl.pallas_call(
        kern, grid=(M // tm,),
        in_specs=[pl.BlockSpec((tm, K), lambda i: (i, 0)), pl.BlockSpec((K, D), lambda i: (0, 0)), row,
                  pl.BlockSpec((1, D), lambda i: (0, 0))],
        out_specs=(row, row), out_shape=(jax.ShapeDtypeStruct((M, D), F32), jax.ShapeDtypeStruct((M, D), BF16)),
        name=name, compiler_params=_cp(("parallel",)),
    )(a, b, x, w.reshape(1, D))


def _matmul_res_final(a, b, x, w, tgt, tm, name):
    M, K = a.shape
    tm = min(tm, M)

    def kern(a_ref, b_ref, x_ref, w_ref, t_ref, dx_ref, gw_ref, loss_ref):
        @pl.when(pl.program_id(0) == 0)
        def _():
            gw_ref[...] = jnp.zeros_like(gw_ref)
            loss_ref[...] = jnp.zeros_like(loss_ref)

        xv = x_ref[...] + lax.dot_general(a_ref[...].astype(BF16), b_ref[...].astype(BF16), _DIMS["nn"],
                                          preferred_element_type=F32)
        r = lax.rsqrt(jnp.mean(xv * xv, axis=-1, keepdims=True) + EPS)
        xn = xv * r
        e = xn * w_ref[...] - t_ref[...]
        loss_ref[...] += 0.5 * jnp.sum(jnp.mean(e * e, axis=-1, keepdims=True))
        dyv = e * (1.0 / D)
        gw_ref[...] += jnp.broadcast_to(jnp.sum(dyv * xn, axis=0, keepdims=True), (8, D))
        dxn = dyv * w_ref[...]
        dx_ref[...] = r * (dxn - xn * jnp.mean(dxn * xn, axis=-1, keepdims=True))

    row = pl.BlockSpec((tm, D), lambda i: (i, 0))
    return pl.pallas_call(
        kern, grid=(M // tm,),
        in_specs=[pl.BlockSpec((tm, K), lambda i: (i, 0)), pl.BlockSpec((K, D), lambda i: (0, 0)), row,
                  pl.BlockSpec((1, D), lambda i: (0, 0)), row],
        out_specs=(row, pl.BlockSpec((8, D), lambda i: (0, 0)), pl.BlockSpec((8, 128), lambda i: (0, 0))),
        out_shape=(jax.ShapeDtypeStruct((M, D), F32), jax.ShapeDtypeStruct((8, D), F32),
                   jax.ShapeDtypeStruct((8, 128), F32)),
        name=name, compiler_params=_cp(("arbitrary",), VMEM_LIMIT),
    )(a, b, x, w.reshape(1, D), tgt)


def _matmul_rms_bwd(a, b, mode, x, w, res, tm, tk, name, dep=None):
    M, K = a.shape
    tm, tk = min(tm, M), min(tk, K)
    nk = K // tk
    assert M % tm == 0 and K % tk == 0
    b_spec = (pl.BlockSpec((tk, D), lambda k, i: (k, 0)) if mode == "nn" else pl.BlockSpec((D, tk), lambda k, i: (0, k)))
    row = pl.BlockSpec((tm, D), lambda k, i: (jnp.where(k == nk - 1, i, 0), 0))
    deps = [] if dep is None else [dep]

    def kern(a_ref, b_ref, x_ref, w_ref, res_ref, *rest):
        dx_ref, gw_ref, acc = rest[-3:]
        k, i = pl.program_id(0), pl.program_id(1)

        @pl.when((i == 0) & (k == 0))
        def _():
            gw_ref[...] = jnp.zeros_like(gw_ref)

        def product():
            return lax.dot_general(a_ref[...].astype(BF16), b_ref[...].astype(BF16), _DIMS[mode],
                                   preferred_element_type=F32)

        def finish(dyv):
            xv = x_ref[...]
            r = lax.rsqrt(jnp.mean(xv * xv, axis=-1, keepdims=True) + EPS)
            xn = xv * r
            gw_ref[...] += jnp.broadcast_to(jnp.sum(dyv * xn, axis=0, keepdims=True), (8, D))
            dxn = dyv * w_ref[...]
            dx_ref[...] = res_ref[...] + r * (dxn - xn * jnp.mean(dxn * xn, axis=-1, keepdims=True))

        if nk == 1:
            finish(product())
        else:
            rows = pl.ds(pl.multiple_of(i * tm, tm), tm)

            @pl.when(k == 0)
            def _():
                acc[rows, :] = jnp.zeros((tm, D), F32)

            acc[rows, :] += product()

            @pl.when(k == nk - 1)
            def _():
                finish(acc[rows, :])

    return pl.pallas_call(
        kern, grid=(nk, M // tm),
        in_specs=[pl.BlockSpec((tm, tk), lambda k, i: (i, k)), b_spec, row, pl.BlockSpec((1, D), lambda k, i: (0, 0)),
                  row] + [pl.BlockSpec(memory_space=pl.ANY)] * len(deps),
        out_specs=(row, pl.BlockSpec((8, D), lambda k, i: (0, 0))),
        out_shape=(jax.ShapeDtypeStruct((M, D), F32), jax.ShapeDtypeStruct((8, D), F32)),
        scratch_shapes=[pltpu.VMEM((M, D) if nk > 1 else (8, 128), F32)], name=name,
        compiler_params=_cp(("arbitrary", "arbitrary"), VMEM_LIMIT),
    )(a, b, x, w.reshape(1, D), res, *deps)


def _rms_fwd(x, w, name, tl=256, dep=None):
    L = x.shape[0]

    def kern(x_ref, w_ref, *rest):
        o_ref = rest[-1]
        xv = x_ref[...]
        r = lax.rsqrt(jnp.mean(xv * xv, axis=-1, keepdims=True) + EPS)
        o_ref[...] = ((xv * r) * w_ref[...]).astype(BF16)

    row = pl.BlockSpec((tl, D), lambda i: (i, 0))
    deps = [] if dep is None else [dep]
    return pl.pallas_call(
        kern, grid=(L // tl,),
        in_specs=[row, pl.BlockSpec((1, D), lambda i: (0, 0))] + [pl.BlockSpec(memory_space=pl.ANY)] * len(deps),
        out_specs=row, out_shape=jax.ShapeDtypeStruct((L, D), BF16), name=name, compiler_params=_cp(("parallel",)),
    )(x, w.reshape(1, D), *deps)


def _down(v, k):
    if k == 0:
        return v
    t = lax.broadcasted_iota(jnp.int32, v.shape, 0)
    return jnp.where(t >= k, pltpu.roll(v, k, axis=0), 0.0)


def _up(v, k):
    if k == 0:
        return v
    n = v.shape[0]
    t = lax.broadcasted_iota(jnp.int32, v.shape, 0)
    return jnp.where(t < n - k, pltpu.roll(v, n - k, axis=0), 0.0)


TW = 256


def _sc_fwd(proj, cw):
    L = proj.shape[0]
    nb = D // TW

    def kern(b_ref, c_ref, x_ref, w_ref, o_ref):
        u = c_ref[...].astype(F32) * x_ref[...].astype(F32)
        w = w_ref[...]
        cv = w[0:1] * _down(u, 2) + w[1:2] * _down(u, 1) + w[2:3] * u
        o_ref[...] = (b_ref[...].astype(F32) * cv).astype(BF16)

    col = lambda off: pl.BlockSpec((L, TW), lambda j: (0, off + j))
    return pl.pallas_call(
        kern, grid=(nb,), in_specs=[col(0), col(nb), col(2 * nb), pl.BlockSpec((8, TW), lambda j: (0, j))],
        out_specs=pl.BlockSpec((L, TW), lambda j: (0, j)), out_shape=jax.ShapeDtypeStruct((L, D), BF16),
        name="sc_fwd", compiler_params=_cp(("parallel",)),
    )(proj, proj, proj, cw)


def _sc_bwd(dya, proj, cw, dproj):
    L = proj.shape[0]
    nb = D // TW

    def kern(d_ref, b_ref, c_ref, x_ref, w_ref, _, dp_ref, gw_ref, keep):
        sec = pl.program_id(1)

        @pl.when(sec == 0)
        def _():
            cs, xs, dyv = c_ref[...].astype(F32), x_ref[...].astype(F32), d_ref[...]
            w = w_ref[...]
            u = cs * xs
            u1, u2 = _down(u, 1), _down(u, 2)
            cv = w[0:1] * u2 + w[1:2] * u1 + w[2:3] * u
            dcv = dyv * b_ref[...].astype(F32)
            du = w[2:3] * dcv + w[1:2] * _up(dcv, 1) + w[0:1] * _up(dcv, 2)
            g0 = jnp.sum(dcv * u2, axis=0, keepdims=True)
            g1 = jnp.sum(dcv * u1, axis=0, keepdims=True)
            g2 = jnp.sum(dcv * u, axis=0, keepdims=True)
            row = lax.broadcasted_iota(jnp.int32, (8, TW), 0)
            gw_ref[...] = jnp.where(row == 0, g0, jnp.where(row == 1, g1, jnp.where(row == 2, g2, 0.0)))
            dp_ref[...] = (dyv * cv).astype(BF16)
            keep[0] = (du * xs).astype(BF16)
            keep[1] = (du * cs).astype(BF16)

        @pl.when(sec > 0)
        def _():
            dp_ref[...] = keep[sec - 1]

    col = lambda off: pl.BlockSpec((L, TW), lambda j, s: (0, off + j))
    return pl.pallas_call(
        kern, grid=(nb, 3),
        in_specs=[col(0), col(0), col(nb), col(2 * nb), pl.BlockSpec((8, TW), lambda j, s: (0, j)),
                  pl.BlockSpec(memory_space=pl.ANY)],
        out_specs=(pl.BlockSpec((L, TW), lambda j, s: (0, s * nb + j)), pl.BlockSpec((8, TW), lambda j, s: (0, j))),
        out_shape=(jax.ShapeDtypeStruct(dproj.shape, BF16), jax.ShapeDtypeStruct((8, D), F32)),
        scratch_shapes=[pltpu.VMEM((2, L, TW), BF16)],
        input_output_aliases={5: 0}, name="sc_bwd", compiler_params=_cp(("parallel", "arbitrary")),
    )(dya, proj, proj, proj, cw, dproj)


def _ssm_conv_fwd(proj, cw4):
    L = proj.shape[0]
    off = C_XBC // TW

    def kern(r_ref, w_ref, o_ref):
        raw = r_ref[...].astype(F32)
        w = w_ref[...]
        c4 = w[0:1] * _down(raw, 3) + w[1:2] * _down(raw, 2) + w[2:3] * _down(raw, 1) + w[3:4] * raw + w[4:5]
        o_ref[...] = c4 * _sigmoid(c4)

    return pl.pallas_call(
        kern, grid=(XBC // TW,),
        in_specs=[pl.BlockSpec((L, TW), lambda j: (0, off + j)), pl.BlockSpec((8, TW), lambda j: (0, j))],
        out_specs=pl.BlockSpec((L, TW), lambda j: (0, j)), out_shape=jax.ShapeDtypeStruct((L, XBC), F32),
        name="ssm_conv_fwd", compiler_params=_cp(("parallel",)),
    )(proj, cw4)


def _ssm_conv_bwd(dx, proj, cw4, dproj, col0, name):
    L, width = dx.shape
    off_p = (C_XBC + col0) // TW
    off_w = col0 // TW

    def kern(d_ref, r_ref, w_ref, _, dp_ref, gw_ref):
        raw = r_ref[...].astype(F32)
        w = w_ref[...]
        r1, r2, r3 = _down(raw, 1), _down(raw, 2), _down(raw, 3)
        c4 = w[0:1] * r3 + w[1:2] * r2 + w[2:3] * r1 + w[3:4] * raw + w[4:5]
        sg = _sigmoid(c4)
        dc4 = d_ref[...] * (sg * (1.0 + c4 * (1.0 - sg)))
        draw = w[3:4] * dc4 + w[2:3] * _up(dc4, 1) + w[1:2] * _up(dc4, 2) + w[0:1] * _up(dc4, 3)
        dp_ref[...] = draw.astype(BF16)
        gs = [jnp.sum(dc4 * r3, axis=0, keepdims=True), jnp.sum(dc4 * r2, axis=0, keepdims=True),
              jnp.sum(dc4 * r1, axis=0, keepdims=True), jnp.sum(dc4 * raw, axis=0, keepdims=True),
              jnp.sum(dc4, axis=0, keepdims=True)]
        row = lax.broadcasted_iota(jnp.int32, (8, TW), 0)
        acc = jnp.zeros((8, TW), F32)
        for k, gk in enumerate(gs):
            acc = jnp.where(row == k, gk, acc)
        gw_ref[...] = acc

    return pl.pallas_call(
        kern, grid=(width // TW,),
        in_specs=[pl.BlockSpec((L, TW), lambda j: (0, j)), pl.BlockSpec((L, TW), lambda j: (0, off_p + j)),
                  pl.BlockSpec((8, TW), lambda j: (0, off_w + j)), pl.BlockSpec(memory_space=pl.ANY)],
        out_specs=(pl.BlockSpec((L, TW), lambda j: (0, off_p + j)), pl.BlockSpec((8, TW), lambda j: (0, j))),
        out_shape=(jax.ShapeDtypeStruct(dproj.shape, BF16), jax.ShapeDtypeStruct((8, width), F32)),
        input_output_aliases={3: 0}, name=name, compiler_params=_cp(("arbitrary",)),
    )(dx, proj, cw4, dproj)


def _split3(v):
    h1 = v.astype(BF16)
    r1 = v - h1.astype(F32)
    h2 = r1.astype(BF16)
    h3 = (r1 - h2.astype(F32)).astype(BF16)
    return h1, h2, h3


def _dot01(m01, v, dims=_DIMS["nn"], m_left=True, terms=3):
    out = None
    for part in _split3(v)[:terms]:
        ops = (m01, part) if m_left else (part, m01)
        t = lax.dot_general(ops[0], ops[1], dims, preferred_element_type=F32)
        out = t if out is None else out + t
    return out


def _bdot(a, b, mode="nn"):
    return lax.dot_general(a.astype(BF16), b.astype(BF16), _DIMS[mode], preferred_element_type=F32)


def _softplus(v):
    return jnp.maximum(v, 0.0) + jnp.log1p(jnp.exp(-jnp.abs(v)))


def _dt_prep(proj, vec):
    L = proj.shape[0]

    def kern(p_ref, v_ref, dt_ref, cs_ref, sg_ref):
        v = v_ref[...]
        pre = p_ref[:, 0:128] + v[0:1]
        dt = _softplus(pre)
        da = dt * (-jnp.exp(v[1:2]))
        ii = lax.broadcasted_iota(jnp.int32, (Q, Q), 0)
        jj = lax.broadcasted_iota(jnp.int32, (Q, Q), 1)
        ltri = (jj <= ii).astype(BF16)
        lane = lax.broadcasted_iota(jnp.int32, (Q, 128), 1)
        for val, ref in ((dt, dt_ref), (_dot01(ltri, da), cs_ref), (_sigmoid(pre), sg_ref)):
            for g in range(NG):
                moved = val if g == 0 else pltpu.roll(val, 128 - 4 * g, axis=1)
                ref[g] = jnp.where(lane < 4, moved, 0.0)

    blk = pl.BlockSpec((NG, Q, 128), lambda c: (0, c, 0))
    return pl.pallas_call(
        kern, grid=(L // Q,),
        in_specs=[pl.BlockSpec((Q, 256), lambda c: (c, 0)), pl.BlockSpec((8, 128), lambda c: (0, 0))],
        out_specs=(blk, blk, blk),
        out_shape=(jax.ShapeDtypeStruct((NG, L, 128), F32),) * 3,
        name="dt_prep", compiler_params=_cp(("parallel",)),
    )(proj, vec)


def _head_masks():
    lane = lax.broadcasted_iota(jnp.int32, (1, 4 * HD), 1)
    return [((lane >= HD * j) & (lane < HD * (j + 1))) for j in range(4)]


def _expand4(v4, masks):
    R = v4.shape[0]
    out = jnp.zeros((R, 4 * HD), F32)
    for j in range(4):
        out = jnp.where(masks[j], jnp.broadcast_to(v4[:, j:j + 1], (R, 4 * HD)), out)
    return out


def _decay_matrix(cs_col, tri):
    colb = jnp.broadcast_to(cs_col, (Q, Q))
    return jnp.exp(jnp.where(tri, colb - colb.T, -jnp.inf))


def _ssd_fwd(xbc, dt4, cs4, vecg):
    L = xbc.shape[0]
    nc = L // Q

    def kern(x_ref, b_ref, c_ref, dt_ref, cs_ref, v_ref, y_ref, s_ref, S):
        c = pl.program_id(1)

        @pl.when(c == 0)
        def _():
            S[...] = jnp.zeros_like(S)

        masks = _head_masks()
        ii = lax.broadcasted_iota(jnp.int32, (Q, Q), 0)
        jj = lax.broadcasted_iota(jnp.int32, (Q, Q), 1)
        tri = jj <= ii
        for gi in range(GPS):
            xs, ns = slice(256 * gi, 256 * (gi + 1)), slice(NS * gi, NS * (gi + 1))
            dt4v, cs4v = dt_ref[gi], cs_ref[gi]
            dt_b, cs_b = _expand4(dt4v, masks), _expand4(cs4v, masks)
            d_b = _expand4(v_ref[gi], masks)[1:2]
            cs_last = cs_b[Q - 1:Q, :]
            x4, bm, cm = x_ref[:, xs], b_ref[:, ns], c_ref[:, ns]
            xdt = x4 * dt_b
            gm = _bdot(cm, bm, "nt")
            s4 = S[gi]
            s_ref[gi, 0] = s4
            y = _bdot(cm, s4) * jnp.exp(cs_b) + d_b * x4
            m_all = jnp.concatenate([(gm * _decay_matrix(cs4v[:, j:j + 1], tri)).astype(BF16) for j in range(4)], axis=0)
            yd = _bdot(m_all, xdt)
            for j in range(4):
                y = y + jnp.where(masks[j], yd[Q * j:Q * (j + 1)], 0.0)
            y_ref[:, xs] = y
            S[gi] = jnp.exp(cs_last) * s4 + _bdot(bm, xdt * jnp.exp(cs_last - cs_b), "tn")

    sc = pl.BlockSpec((GPS, Q, 128), lambda g, c: (g, c, 0))
    bw = NS * GPS
    return pl.pallas_call(
        kern, grid=(NG // GPS, nc),
        in_specs=[pl.BlockSpec((Q, 256 * GPS), lambda g, c: (c, g)),
                  pl.BlockSpec((Q, bw), lambda g, c: (c, INNER // bw + g)),
                  pl.BlockSpec((Q, bw), lambda g, c: (c, (INNER + NG * NS) // bw + g)),
                  sc, sc, pl.BlockSpec((GPS, 8, 128), lambda g, c: (g, 0, 0))],
        out_specs=(pl.BlockSpec((Q, 256 * GPS), lambda g, c: (c, g)),
                   pl.BlockSpec((GPS, 1, NS, 256), lambda g, c: (g, c, 0, 0))),
        out_shape=(jax.ShapeDtypeStruct((L, INNER), F32), jax.ShapeDtypeStruct((NG, nc, NS, 256), F32)),
        scratch_shapes=[pltpu.VMEM((GPS, NS, 256), F32)], name="ssd_fwd",
        compiler_params=_cp(("parallel", "arbitrary")),
    )(xbc, xbc, xbc, dt4, cs4, vecg)


def _ssd_bwd(xbc, dt4, cs4, sg4, vecg, s_all, dy):
    L = xbc.shape[0]
    nc = L // Q

    def kern(x_ref, b_ref, c_ref, dt_ref, cs_ref, sg_ref, v_ref, s_ref, dy_ref,
             dx_ref, db_ref, dc_ref, ddt_ref, st_ref, dS):
        cc = pl.program_id(1)

        @pl.when(cc == 0)
        def _():
            dS[...] = jnp.zeros_like(dS)
            st_ref[...] = jnp.zeros_like(st_ref)

        masks = _head_masks()
        ii = lax.broadcasted_iota(jnp.int32, (Q, Q), 0)
        jj = lax.broadcasted_iota(jnp.int32, (Q, Q), 1)
        tri = jj <= ii
        utri = (jj >= ii).astype(BF16)
        hsel = ((lax.broadcasted_iota(jnp.int32, (4 * HD, 128), 0) // HD)
                == lax.broadcasted_iota(jnp.int32, (4 * HD, 128), 1)).astype(BF16)
        hrow = ((lax.broadcasted_iota(jnp.int32, (4 * Q, 128), 0) // Q)
                == lax.broadcasted_iota(jnp.int32, (4 * Q, 128), 1)).astype(BF16)
        ones_q = jnp.ones((Q, 128), BF16)
        lane128 = lax.broadcasted_iota(jnp.int32, (Q, 128), 1)

        for gi in range(GPS):
            xs, ns = slice(256 * gi, 256 * (gi + 1)), slice(NS * gi, NS * (gi + 1))
            dt4v, cs4v, sg4v = dt_ref[gi], cs_ref[gi], sg_ref[gi]
            dt_b, cs_b = _expand4(dt4v, masks), _expand4(cs4v, masks)
            vv = _expand4(v_ref[gi], masks)
            a_b = -jnp.exp(vv[0:1])
            d_b = vv[1:2]
            a4 = -jnp.exp(v_ref[gi][0:1, :])
            cs_last = cs_b[Q - 1:Q, :]
            ecs = jnp.exp(cs_b)
            decay = jnp.exp(cs_last - cs_b)
            elast = jnp.exp(cs_last)
            x4, bm, cm, dyv = x_ref[:, xs], b_ref[:, ns], c_ref[:, ns], dy_ref[:, xs]
            s4 = s_ref[gi, 0]
            dsn = dS[gi]
            xdt = x4 * dt_b
            gm = _bdot(cm, bm, "nt")
            dye = dyv * ecs
            yoff = ecs * _bdot(cm, s4)
            t4 = _bdot(bm, dsn) * decay
            lms, mhs = [], []
            for j in range(4):
                colb = jnp.broadcast_to(cs4v[:, j:j + 1], (Q, Q))
                lms.append(jnp.exp(jnp.where(tri, colb - colb.T, -jnp.inf)))
                mhs.append(gm * lms[j])
            m_all = jnp.concatenate([m.astype(BF16) for m in mhs], axis=0)
            dy_m = jnp.concatenate([jnp.where(masks[j], dyv, 0.0).astype(BF16) for j in range(4)], axis=0)
            dxdt = t4 + _bdot(m_all, dy_m, "tn")
            dm_all = _bdot(dy_m, xdt, "nt")
            dg = jnp.zeros((Q, Q), F32)
            for j in range(4):
                dg = dg + dm_all[Q * j:Q * (j + 1)] * lms[j]
            e_all = dm_all * jnp.concatenate(mhs, axis=0)
            rsum = _dot01(ones_q, e_all, m_left=False, terms=2)
            da4 = -_dot01(hrow, e_all, _DIMS["tn"], m_left=False, terms=2)
            for j in range(4):
                da4 = da4 + jnp.where(lane128 == j, rsum[Q * j:Q * (j + 1)], 0.0)
            xt = xdt * t4
            tail = jnp.sum(xt, axis=0, keepdims=True) + elast * jnp.sum(s4 * dsn, axis=0, keepdims=True)
            gd_raw = jnp.sum(dyv * x4, axis=0, keepdims=True)
            stacked = jnp.concatenate([dyv * yoff - xt, dxdt * x4, jnp.broadcast_to(tail, (8, 4 * HD)),
                                       jnp.broadcast_to(gd_raw, (8, 4 * HD))], axis=0)
            seg = _dot01(hsel, stacked, m_left=False, terms=2)
            dda4 = _dot01(utri, da4 + seg[0:Q], terms=2) + seg[2 * Q:2 * Q + 1]
            ddt_ref[gi] = (dda4 * a4 + seg[Q:2 * Q]) * sg4v
            ga = jnp.sum(dda4 * dt4v * a4, axis=0, keepdims=True)
            row = lax.broadcasted_iota(jnp.int32, (8, 128), 0)
            st_ref[gi] += jnp.where(row == 0, ga, jnp.where(row == 1, seg[2 * Q + 8:2 * Q + 9], 0.0))
            dx_ref[:, xs] = d_b * dyv + dxdt * dt_b
            dc_ref[:, ns] = _bdot(dg, bm) + _bdot(dye, s4, "nt")
            db_ref[:, ns] = _bdot(dg, cm, "tn") + _bdot(xdt * decay, dsn, "nt")
            dS[gi] = elast * dsn + _bdot(cm, dye, "tn")

    rv = lambda c: nc - 1 - c
    sc = pl.BlockSpec((GPS, Q, 128), lambda g, c: (g, rv(c), 0))
    bw = NS * GPS
    return pl.pallas_call(
        kern, grid=(NG // GPS, nc),
        in_specs=[pl.BlockSpec((Q, 256 * GPS), lambda g, c: (rv(c), g)),
                  pl.BlockSpec((Q, bw), lambda g, c: (rv(c), INNER // bw + g)),
                  pl.BlockSpec((Q, bw), lambda g, c: (rv(c), (INNER + NG * NS) // bw + g)),
                  sc, sc, sc, pl.BlockSpec((GPS, 8, 128), lambda g, c: (g, 0, 0)),
                  pl.BlockSpec((GPS, 1, NS, 256), lambda g, c: (g, rv(c), 0, 0)),
                  pl.BlockSpec((Q, 256 * GPS), lambda g, c: (rv(c), g))],
        out_specs=(pl.BlockSpec((Q, 256 * GPS), lambda g, c: (rv(c), g)),
                   pl.BlockSpec((Q, bw), lambda g, c: (rv(c), g)),
                   pl.BlockSpec((Q, bw), lambda g, c: (rv(c), g)),
                   pl.BlockSpec((GPS, Q, 128), lambda g, c: (g, rv(c), 0)),
                   pl.BlockSpec((GPS, 8, 128), lambda g, c: (g, 0, 0))),
        out_shape=(jax.ShapeDtypeStruct((L, INNER), F32), jax.ShapeDtypeStruct((L, NG * NS), F32),
                   jax.ShapeDtypeStruct((L, NG * NS), F32), jax.ShapeDtypeStruct((NG, L, 128), F32),
                   jax.ShapeDtypeStruct((NG, 8, 128), F32)),
        scratch_shapes=[pltpu.VMEM((GPS, NS, 256), F32)], name="ssd_bwd",
        compiler_params=_cp(("parallel", "arbitrary")),
    )(xbc, xbc, xbc, dt4, cs4, sg4, vecg, s_all, dy)


def _dt_bwd(ddt, dproj, tl=256):
    L = ddt.shape[1]

    def kern(d_ref, _, dp_ref, gs_ref):
        @pl.when(pl.program_id(0) == 0)
        def _():
            gs_ref[...] = jnp.zeros_like(gs_ref)

        d = d_ref[0]
        for g in range(1, NG):
            d = d + pltpu.roll(d_ref[g], 4 * g, axis=1)
        gs_ref[...] += jnp.broadcast_to(jnp.sum(d, axis=0, keepdims=True), (8, 128))
        dp_ref[...] = jnp.concatenate([d, jnp.zeros_like(d)], axis=1).astype(BF16)

    return pl.pallas_call(
        kern, grid=(L // tl,),
        in_specs=[pl.BlockSpec((NG, tl, 128), lambda i: (0, i, 0)), pl.BlockSpec(memory_space=pl.ANY)],
        out_specs=(pl.BlockSpec((tl, 256), lambda i: (i, C_DT // 256)), pl.BlockSpec((8, 128), lambda i: (0, 0))),
        out_shape=(jax.ShapeDtypeStruct(dproj.shape, BF16), jax.ShapeDtypeStruct((8, 128), F32)),
        input_output_aliases={1: 0}, name="dt_bwd", compiler_params=_cp(("arbitrary",)),
    )(ddt, dproj)


GW = INNER // NG


def _gnorm_fwd(y, proj, w, tl=256):
    L = y.shape[0]
    zoff = C_Z // 1024

    def kern(y_ref, z_ref, w_ref, o_ref):
        z = z_ref[...].astype(F32)
        yz = y_ref[...] * (z * _sigmoid(z))
        wv = w_ref[...]
        for k in range(1024 // GW):
            sl = slice(GW * k, GW * (k + 1))
            v = yz[:, sl]
            rg = lax.rsqrt(jnp.mean(v * v, axis=-1, keepdims=True) + EPS)
            o_ref[:, sl] = ((v * rg) * wv[:, sl]).astype(BF16)

    blk = pl.BlockSpec((tl, 1024), lambda i, j: (i, j))
    return pl.pallas_call(
        kern, grid=(L // tl, 2),
        in_specs=[blk, pl.BlockSpec((tl, 1024), lambda i, j: (i, zoff + j)), pl.BlockSpec((1, 1024), lambda i, j: (0, j))],
        out_specs=blk, out_shape=jax.ShapeDtypeStruct((L, INNER), BF16), name="gnorm_fwd",
        compiler_params=_cp(("parallel", "parallel")),
    )(y, proj, w.reshape(1, INNER))


def _gnorm_bwd(dbr, wb, y, proj, w, dproj, dep, tl=512):
    L = y.shape[0]
    tl = min(tl, L)
    zoff = C_Z // 1024

    def kern(d_ref, b_ref, y_ref, z_ref, w_ref, _, __, dy_ref, dp_ref, gw_ref):
        @pl.when(pl.program_id(1) == 0)
        def _():
            gw_ref[...] = jnp.zeros_like(gw_ref)

        z = z_ref[...].astype(F32)
        sg = _sigmoid(z)
        sz = z * sg
        yv = y_ref[...]
        yz = yv * sz
        dv = lax.dot_general(d_ref[0], b_ref[...], _DIMS["nt"], preferred_element_type=F32)
        wv = w_ref[...]
        for k in range(1024 // GW):
            sl = slice(GW * k, GW * (k + 1))
            v = yz[:, sl]
            rg = lax.rsqrt(jnp.mean(v * v, axis=-1, keepdims=True) + EPS)
            vn = v * rg
            dk = dv[:, sl]
            gw_ref[:, sl] += jnp.broadcast_to(jnp.sum(dk * vn, axis=0, keepdims=True), (8, GW))
            dvn = dk * wv[:, sl]
            dyz = rg * (dvn - vn * jnp.mean(dvn * vn, axis=-1, keepdims=True))
            dy_ref[:, sl] = dyz * sz[:, sl]
            dp_ref[:, sl] = (dyz * yv[:, sl] * (sg[:, sl] * (1.0 + z[:, sl] * (1.0 - sg[:, sl])))).astype(BF16)

    blk = pl.BlockSpec((tl, 1024), lambda j, i: (i, j))
    zblk = pl.BlockSpec((tl, 1024), lambda j, i: (i, zoff + j))
    return pl.pallas_call(
        kern, grid=(2, L // tl),
        in_specs=[pl.BlockSpec((1, tl, D), lambda j, i: (1, i, 0)), pl.BlockSpec((1024, D), lambda j, i: (j, 0)),
                  blk, zblk, pl.BlockSpec((1, 1024), lambda j, i: (0, j)), pl.BlockSpec(memory_space=pl.ANY),
                  pl.BlockSpec(memory_space=pl.ANY)],
        out_specs=(blk, zblk, pl.BlockSpec((8, 1024), lambda j, i: (0, j))),
        out_shape=(jax.ShapeDtypeStruct((L, INNER), F32), jax.ShapeDtypeStruct(dproj.shape, BF16),
                   jax.ShapeDtypeStruct((8, INNER), F32)),
        input_output_aliases={5: 1}, name="gnorm_bwd", compiler_params=_cp(("parallel", "arbitrary")),
    )(dbr, wb, y, proj, w.reshape(1, INNER), dproj, dep)


def _merge_fwd(proj, bg, br_a, br_b, tl=256):
    L = proj.shape[0]
    goff = C_GATE // 1024

    def kern(g1_ref, g2_ref, b1_ref, b2_ref, a_ref, b_ref, o_ref):
        g1 = _sigmoid(g1_ref[...].astype(F32) + b1_ref[...])
        g2 = _sigmoid(g2_ref[...].astype(F32) + b2_ref[...])
        o_ref[...] = (g1 * a_ref[...] + g2 * b_ref[...]).astype(BF16)

    row = pl.BlockSpec((tl, 1024), lambda i: (i, 0))
    bg2 = bg.reshape(1, 2 * D)
    return pl.pallas_call(
        kern, grid=(L // tl,),
        in_specs=[pl.BlockSpec((tl, 1024), lambda i: (i, goff)), pl.BlockSpec((tl, 1024), lambda i: (i, goff + 1)),
                  pl.BlockSpec((1, 1024), lambda i: (0, 0)), pl.BlockSpec((1, 1024), lambda i: (0, 1)), row, row],
        out_specs=row, out_shape=jax.ShapeDtypeStruct((L, D), BF16), name="merge_fwd",
        compiler_params=_cp(("parallel",)),
    )(proj, proj, bg2, bg2, br_a, br_b)


def _branch_ssm_merge(yb, wb, proj, bg, br_a, tm=512):
    L, K = yb.shape
    tm = min(tm, L)
    goff = C_GATE // 1024

    def kern(a_ref, b_ref, g1_ref, g2_ref, b1_ref, b2_ref, bra_ref, brb_ref, m_ref):
        brb = lax.dot_general(a_ref[...], b_ref[...], _DIMS["nn"], preferred_element_type=F32)
        brb_ref[...] = brb
        g1 = _sigmoid(g1_ref[...].astype(F32) + b1_ref[...])
        g2 = _sigmoid(g2_ref[...].astype(F32) + b2_ref[...])
        m_ref[...] = (g1 * bra_ref[...] + g2 * brb).astype(BF16)

    row = pl.BlockSpec((tm, D), lambda i: (i, 0))
    bg2 = bg.reshape(1, 2 * D)
    return pl.pallas_call(
        kern, grid=(L // tm,),
        in_specs=[pl.BlockSpec((tm, K), lambda i: (i, 0)), pl.BlockSpec((K, D), lambda i: (0, 0)),
                  pl.BlockSpec((tm, D), lambda i: (i, goff)), pl.BlockSpec((tm, D), lambda i: (i, goff + 1)),
                  pl.BlockSpec((1, D), lambda i: (0, 0)), pl.BlockSpec((1, D), lambda i: (0, 1)), row],
        out_specs=(row, row), out_shape=(jax.ShapeDtypeStruct((L, D), F32), jax.ShapeDtypeStruct((L, D), BF16)),
        name="branch_ssm_merge", compiler_params=_cp(("parallel",)),
    )(yb, wb, proj, proj, bg2, bg2, br_a)


def _merge_bwd(dx1, wo, proj, bg, br_a, br_b, dproj, tl=512):
    L = proj.shape[0]
    tl = min(tl, L)
    goff = C_GATE // 1024

    def kern(dm_ref, wo_ref, g_ref, b_ref, a_ref, bb_ref, _, dbr_ref, dp_ref, gb_ref):
        j = pl.program_id(0)

        @pl.when(pl.program_id(1) == 0)
        def _():
            gb_ref[...] = jnp.zeros_like(gb_ref)

        g = _sigmoid(g_ref[...].astype(F32) + b_ref[...])
        br = jnp.where(j == 0, a_ref[...], bb_ref[...])
        dmv = lax.dot_general(dm_ref[...].astype(BF16), wo_ref[...], _DIMS["nt"], preferred_element_type=F32)
        dbr_ref[0] = (dmv * g).astype(BF16)
        dgate = dmv * br * g * (1.0 - g)
        gb_ref[...] += jnp.broadcast_to(jnp.sum(dgate, axis=0, keepdims=True), (8, 1024))
        dp_ref[...] = dgate.astype(BF16)

    row = pl.BlockSpec((tl, 1024), lambda j, i: (i, 0))
    gblk = pl.BlockSpec((tl, 1024), lambda j, i: (i, goff + j))
    return pl.pallas_call(
        kern, grid=(2, L // tl),
        in_specs=[row, pl.BlockSpec((D, D), lambda j, i: (0, 0)), gblk, pl.BlockSpec((1, 1024), lambda j, i: (0, j)),
                  row, row, pl.BlockSpec(memory_space=pl.ANY)],
        out_specs=(pl.BlockSpec((1, tl, 1024), lambda j, i: (j, i, 0)), gblk, pl.BlockSpec((8, 1024), lambda j, i: (0, j))),
        out_shape=(jax.ShapeDtypeStruct((2, L, D), BF16), jax.ShapeDtypeStruct(dproj.shape, BF16),
                   jax.ShapeDtypeStruct((8, 2 * D), F32)),
        input_output_aliases={6: 1}, name="merge_bwd", compiler_params=_cp(("parallel", "arbitrary")),
    )(dx1, wo, proj, bg.reshape(1, 2 * D), br_a, br_b, dproj)


def _coords():
    return lax.axis_index("x"), lax.axis_index("y"), lax.axis_index("c")


def _other_chips(sk):
    xk, yk = sk // 2, sk % 2
    return [((1 - xk, yk), 2 * (1 - xk) + yk), ((xk, 1 - yk), 2 * xk + 1 - yk), ((1 - xk, 1 - yk), 2 * (1 - xk) + 1 - yk)]


def _rows(start, size):
    assert size % 128 == 0
    return pl.ds(pl.multiple_of(start, 128), size)


def _per_chip(fn):
    x, y, _ = _coords()
    s = 2 * x + y
    for sk in range(4):
        pl.when(s == sk)(functools.partial(fn, sk))


XTRA = PIECE - PMAIN


def _place(shard, full_shape, block, index_map, idx, name, blk0=0, nblk=None, dep=None, into=None):
    in_block = block[-2:]
    if nblk is None:
        nblk = shard.shape[0] // in_block[0]

    def kern(idx_ref, s_ref, *rest):
        o_ref = rest[-1]
        o_ref[...] = s_ref[...].astype(BF16).reshape(o_ref.shape)

    extra = ([dep] if dep is not None else []) + ([into] if into is not None else [])
    grid_spec = pltpu.PrefetchScalarGridSpec(
        num_scalar_prefetch=1, grid=(nblk,),
        in_specs=[pl.BlockSpec(in_block, lambda i, idx_ref: (blk0 + i, 0))] + [_ANY] * len(extra),
        out_specs=pl.BlockSpec(block, index_map))
    aliases = {1 + len(extra): 0} if into is not None else {}
    return pl.pallas_call(kern, grid_spec=grid_spec, out_shape=jax.ShapeDtypeStruct(full_shape, BF16), name=name,
                          input_output_aliases=aliases, compiler_params=_cp(("arbitrary",)))(idx, shard, *extra)


_SEM = pl.BlockSpec(memory_space=pltpu.SEMAPHORE)
_EFFECT = pltpu.SideEffectType.DATAFLOW_SIDE_EFFECTING


_ANY = pl.BlockSpec(memory_space=pl.ANY)


def _tie(v, dep, name):
    def body(v_ref, dep_ref, o_ref):
        del v_ref, dep_ref, o_ref

    return pl.pallas_call(body, out_shape=jax.ShapeDtypeStruct(v.shape, v.dtype), in_specs=[_ANY, _ANY],
                          out_specs=_ANY, input_output_aliases={0: 0}, name=name)(v, dep)


def _split_call(name, arrays, start=None, wait=None, wait_sems=None, after=None):
    keys = list(arrays)
    n = len(keys)
    n_start = start.n if start is not None else 0
    afters = [] if after is None else (list(after) if isinstance(after, (list, tuple)) else [after])

    def body(*refs):
        pos = n
        if wait is not None:
            wss, wrs = refs[pos], refs[pos + 1]
            pos += 2
        pos += len(afters)
        if start is not None:
            nss, nrs = refs[pos], refs[pos + 1]
            pos += 2
        R = dict(zip(keys, refs[pos:pos + n]))
        token = refs[pos + n]
        x, y, c = _coords()

        def desc(src, dst, dev, ss, rs, k):
            return pltpu.make_async_remote_copy(src_ref=src, dst_ref=dst, send_sem=ss.at[k], recv_sem=rs.at[k],
                                                device_id=dev, device_id_type=MESH)

        def run(sk):
            if wait is not None:
                for k, (snd, land) in enumerate(wait.copies(sk, R)):
                    if snd is not None:
                        desc(snd[0], snd[1], snd[2], wss, wrs, k).wait_send()
                    if land is not None:
                        desc(land, land, (x, y, c), wss, wrs, k).wait_recv()
            if start is not None:
                for k, (snd, land) in enumerate(start.copies(sk, R)):
                    if snd is not None:
                        desc(snd[0], snd[1], snd[2], nss, nrs, k).start()

        _per_chip(run)
        token[...] = jnp.zeros_like(token)

    hbm = pl.BlockSpec(memory_space=HBM)
    vals = [arrays[k] for k in keys]
    ins, in_specs = list(vals), [hbm] * n
    if wait is not None:
        ins += list(wait_sems)
        in_specs += [_SEM, _SEM]
    ins += afters
    in_specs += [pl.BlockSpec(memory_space=pl.ANY)] * len(afters)
    out_shape, out_specs = [], []
    if start is not None:
        out_shape += [pltpu.SemaphoreType.DMA((n_start,)), pltpu.SemaphoreType.DMA((n_start,))]
        out_specs += [_SEM, _SEM]
    first = len(out_shape)
    out_shape += [jax.ShapeDtypeStruct(v.shape, v.dtype) for v in vals] + [jax.ShapeDtypeStruct((8, 128), F32)]
    out_specs += [hbm] * n + [pl.BlockSpec(memory_space=pltpu.VMEM)]
    res = pl.pallas_call(
        body, out_shape=tuple(out_shape), in_specs=in_specs, out_specs=tuple(out_specs),
        input_output_aliases={i: first + i for i in range(n)}, name=name,
        compiler_params=pltpu.CompilerParams(has_side_effects=_EFFECT),
    )(*ins)
    sems = (res[0], res[1]) if start is not None else None
    return dict(zip(keys, res[first:first + n])), sems, res[-1]


class _Plan:
    def __init__(self, n, copies):
        self.n, self.copies = n, copies


_HM, _HX = PMAIN // 2, XTRA // 2
WAVE0 = 768
WAVES = ((0, WAVE0), (WAVE0, _HM - WAVE0))
_WIN = {
    "wq0": (True, "wct", lambda r, sc, hc: r.at[_rows(PMAIN * sc + _HM * hc + WAVES[0][0], WAVES[0][1]), :]),
    "wq1": (True, "wct", lambda r, sc, hc: r.at[_rows(PMAIN * sc + _HM * hc + WAVES[1][0], WAVES[1][1]), :]),
    "xt": (True, "xt", lambda r, sc, hc: r.at[sc, _rows(_HX * hc, _HX), :]),
    "w1": (True, "w1", lambda r, sc, hc: r.at[_rows(512 * hc, 512), pl.ds(1024 * sc, 1024)]),
    "w2": (True, "w2", lambda r, sc, hc: r.at[_rows(1024 * sc + 512 * hc, 512), :]),
    "wa": (True, "wa", lambda r, sc, hc: r.at[_rows(256 * sc + 128 * hc, 128), :]),
    "wb": (True, "wb", lambda r, sc, hc: r.at[_rows(512 * sc + 256 * hc, 256), :]),
    "wo": (True, "wo", lambda r, sc, hc: r.at[_rows(256 * sc + 128 * hc, 128), :]),
    "cw": (False, "cw", lambda r, sc, hc: r.at[sc]),
}


_PIECE_SRC = {
    "wq0": lambda p, hc: p.at[_rows(_HM * hc + WAVES[0][0], WAVES[0][1]), :],
    "wq1": lambda p, hc: p.at[_rows(_HM * hc + WAVES[1][0], WAVES[1][1]), :],
    "xt": lambda p, hc: p.at[_rows(PMAIN + _HX * hc, _HX), :],
}


def _ag_chips_plan(keys):
    def copies(sk, R):
        _, _, c = _coords()
        out = []
        for key in keys:
            _, arr, win = _WIN[key]
            for (px, py), ps in _other_chips(sk):
                dst = win(R[arr], sk, c)
                src = _PIECE_SRC[key](R["piece"], c) if key in _PIECE_SRC else dst
                out.append(((src, dst, (px, py, c)), win(R[arr], ps, c)))
        return out
    return _Plan(3 * len(keys), copies)


def _ag_sibling_plan(keys):
    keys = [k for k in keys if _WIN[k][0]]

    def copies(sk, R):
        x, y, c = _coords()
        out = []
        for key in keys:
            _, arr, win = _WIN[key]
            for _, ps in _other_chips(sk):
                w = win(R[arr], ps, c)
                out.append(((w, w, (x, y, 1 - c)), win(R[arr], ps, 1 - c)))
        return out
    return _Plan(3 * len(keys), copies)


def _in_proj_wave(h, wct, wave, proj=None, tm=2048):
    L = h.shape[0]
    tm = min(tm, L)
    off, size = WAVES[wave]
    start = lambda j: pl.multiple_of(_HM * j + off, 128)

    def kern(h_ref, w_ref, *rest):
        o_ref = rest[-1]
        o_ref[...] = lax.dot_general(h_ref[...], w_ref[...], _DIMS["nt"], preferred_element_type=F32).astype(BF16)

    in_specs = [pl.BlockSpec((tm, D), lambda j, i: (i, 0)),
                pl.BlockSpec((pl.Element(size), pl.Element(D)), lambda j, i: (start(j), 0))]
    args, aliases = [h, wct], {}
    if proj is not None:
        in_specs.append(pl.BlockSpec(memory_space=pl.ANY))
        args.append(proj)
        aliases = {2: 0}
    return pl.pallas_call(
        kern, grid=(8, L // tm), in_specs=in_specs,
        out_specs=pl.BlockSpec((pl.Element(tm), pl.Element(size)), lambda j, i: (i * tm, start(j))),
        out_shape=jax.ShapeDtypeStruct((L, NCW), BF16), input_output_aliases=aliases,
        name="in_proj_wave%d" % wave, compiler_params=_cp(("parallel", "parallel")),
    )(*args)


def _fix_wct(wct, xt):
    nb = PMAIN // XTRA

    def kern(w_ref, x_ref, o_ref):
        k = pl.program_id(0)
        xv = x_ref[0]
        o_ref[...] = jnp.where(k < 3, (w_ref[...].astype(F32) + xv.astype(F32)).astype(BF16), xv)

    blk = pl.BlockSpec((XTRA, D), lambda k: (nb * (k + 1), 0))
    rblk = pl.BlockSpec((XTRA, D), lambda k: (jnp.where(k < 3, nb * (k + 1), 0), 0))
    return pl.pallas_call(
        kern, grid=(4,), in_specs=[rblk, pl.BlockSpec((1, XTRA, D), lambda k: (k, 0, 0))], out_specs=blk,
        out_shape=jax.ShapeDtypeStruct(wct.shape, BF16), input_output_aliases={0: 0}, name="fix_wct",
        compiler_params=_cp(("arbitrary",)),
    )(wct, xt)


_HP = PIECE // 2
_GWIN = [
    lambda r, sc, hc: r.at[_rows(PMAIN * sc + _HP * hc, _HP), :],
    lambda r, sc, hc: r.at[_rows(512 * hc, 512), pl.ds(1024 * sc, 1024)],
    lambda r, sc, hc: r.at[_rows(1024 * sc + 512 * hc, 512), :],
    lambda r, sc, hc: r.at[_rows(256 * sc + 128 * hc, 128), :],
    lambda r, sc, hc: r.at[_rows(512 * sc + 256 * hc, 256), :],
    lambda r, sc, hc: r.at[_rows(256 * sc + 128 * hc, 128), :],
]
HALF_SHAPES = [(PIECE // 2, D), (512, 1024), (512, 1024), (128, 1024), (256, 1024), (128, 1024)]


def _rs_sibling_plan(ts):
    def copies(sk, R):
        x, y, c = _coords()
        out = []
        for t in ts:
            for sc in range(4):
                land = R["ra%d" % t].at[sc]
                out.append(((_GWIN[t](R["g%d" % t], sc, 1 - c), land, (x, y, 1 - c)), land))
        return out
    return _Plan(4 * len(ts), copies)


def _rs_chips_plan(ts):
    def copies(sk, R):
        _, _, c = _coords()
        out = []
        for t in ts:
            for j, ((px, py), ps) in enumerate(_other_chips(sk)):
                land = R["rb%d" % t].at[j]
                out.append(((R["hb%d" % t].at[ps], land, (px, py, c)), land))
        return out
    return _Plan(3 * len(ts), copies)


def _rs_share_plan(ts):
    def copies(sk, R):
        x, y, c = _coords()
        out = []
        for t in ts:
            rows = HALF_SHAPES[t][0]
            mine = R["f%d" % t].at[_rows(rows * c, rows), :]
            out.append(((mine, mine, (x, y, 1 - c)), R["f%d" % t].at[_rows(rows * (1 - c), rows), :]))
        return out
    return _Plan(len(ts), copies)


def _half_tiling(t):
    rows, cols = HALF_SHAPES[t]
    if t == 0:
        return (rows // 2, cols), 2, lambda i: (i, 0)
    return (rows, cols), 1, lambda i: (0, 0)


def _window_spec(t, blk):
    if t == 0:
        return pl.BlockSpec((pl.Element(blk[0]), pl.Element(blk[1])), lambda i, sc, idx_ref: (
            pl.multiple_of(PMAIN * sc + _HP * idx_ref[1] + blk[0] * i, 128), 0))
    if t == 1:
        return pl.BlockSpec(blk, lambda i, sc, idx_ref: (idx_ref[1], sc))
    return pl.BlockSpec(blk, lambda i, sc, idx_ref: (2 * sc + idx_ref[1], 0))


def _chip_sum(g, ra, t, idx, name):
    rows, cols = HALF_SHAPES[t]
    blk, nblk, inner = _half_tiling(t)

    def kern(idx_ref, g_ref, r_ref, hb_ref, hf_ref):
        v = g_ref[...].astype(F32) + r_ref[0].astype(F32)
        hb_ref[0] = v.astype(BF16)

        @pl.when(pl.program_id(1) == idx_ref[0])
        def _():
            hf_ref[...] = v

    omap = lambda i, sc, idx_ref: (sc,) + inner(i)
    grid_spec = pltpu.PrefetchScalarGridSpec(
        num_scalar_prefetch=1, grid=(nblk, 4),
        in_specs=[_window_spec(t, blk), pl.BlockSpec((1,) + blk, omap)],
        out_specs=(pl.BlockSpec((1,) + blk, omap), pl.BlockSpec(blk, lambda i, sc, idx_ref: inner(i))))
    return pl.pallas_call(
        kern, grid_spec=grid_spec,
        out_shape=(jax.ShapeDtypeStruct((4, rows, cols), BF16), jax.ShapeDtypeStruct((rows, cols), F32)),
        name=name, compiler_params=_cp(("parallel", "arbitrary")),
    )(idx, g, ra)


def _chip_sum_part(g, ra, t, idx, name, own, dep=None):
    rows, cols = HALF_SHAPES[t]
    blk, nblk, inner = _half_tiling(t)
    chip = (lambda k, idx_ref: idx_ref[0]) if own else (lambda k, idx_ref: lax.rem(idx_ref[0] + 1 + k, 4))
    win = _window_spec(t, blk)
    deps = [] if dep is None else [dep]

    def kern(idx_ref, g_ref, r_ref, *rest):
        v = g_ref[...].astype(F32) + r_ref[0].astype(F32)
        if own:
            rest[-1][...] = v
        else:
            rest[-1][0] = v.astype(BF16)

    omap = lambda i, k, idx_ref: (chip(k, idx_ref),) + inner(i)
    grid_spec = pltpu.PrefetchScalarGridSpec(
        num_scalar_prefetch=1, grid=(nblk, 1 if own else 3),
        in_specs=[pl.BlockSpec(win.block_shape, lambda i, k, idx_ref: win.index_map(i, chip(k, idx_ref), idx_ref)),
                  pl.BlockSpec((1,) + blk, omap)] + [pl.BlockSpec(memory_space=pl.ANY)] * len(deps),
        out_specs=pl.BlockSpec(blk, lambda i, k, idx_ref: inner(i)) if own else pl.BlockSpec((1,) + blk, omap))
    return pl.pallas_call(
        kern, grid_spec=grid_spec,
        out_shape=jax.ShapeDtypeStruct((rows, cols), F32) if own else jax.ShapeDtypeStruct((4, rows, cols), BF16),
        name=name, compiler_params=_cp(("parallel", "arbitrary")),
    )(idx, g, ra, *deps)


def _final_sum(hf, rb, t, idx, name):
    rows, cols = HALF_SHAPES[t]
    blk, nblk, inner = _half_tiling(t)
    nbr = rows // blk[0]

    def kern(idx_ref, h_ref, r_ref, o_ref):
        o_ref[...] = ((h_ref[...] + r_ref[0].astype(F32)) + r_ref[1].astype(F32)) + r_ref[2].astype(F32)

    def omap(i, idx_ref):
        r, cidx = inner(i)
        return nbr * idx_ref[1] + r, cidx

    grid_spec = pltpu.PrefetchScalarGridSpec(
        num_scalar_prefetch=1, grid=(nblk,),
        in_specs=[pl.BlockSpec(blk, lambda i, idx_ref: inner(i)),
                  pl.BlockSpec((3,) + blk, lambda i, idx_ref: (0,) + inner(i))],
        out_specs=pl.BlockSpec(blk, omap))
    return pl.pallas_call(
        kern, grid_spec=grid_spec, out_shape=jax.ShapeDtypeStruct((2 * rows, cols), F32),
        name=name, compiler_params=_cp(("parallel",)),
    )(idx, hf, rb)


class _ReduceScatter:
    def __init__(self, ts, grads, idx, tag):
        self.ts, self.idx, self.tag = ts, idx, tag
        arr = {}
        for t in ts:
            arr["g%d" % t] = grads[t]
            arr["ra%d" % t] = lax.empty((4,) + HALF_SHAPES[t], BF16)
        self.plan = _rs_sibling_plan(ts)
        self.arr, self.sems, self.token = _split_call("rs_sibling_start_" + tag, arr, start=self.plan)

    def chips(self, after, own_later=False):
        arr, _, _ = _split_call("rs_sibling_wait_" + self.tag, self.arr, wait=self.plan, wait_sems=self.sems, after=after)
        brr, self.hf = {}, {}
        for t in self.ts:
            if own_later:
                hb = _chip_sum_part(arr["g%d" % t], arr["ra%d" % t], t, self.idx, "chip_sum_others_%d" % t, False)
            else:
                hb, self.hf[t] = _chip_sum(arr["g%d" % t], arr["ra%d" % t], t, self.idx, "chip_sum_%d" % t)
            brr["hb%d" % t] = hb
            brr["rb%d" % t] = lax.empty((3,) + HALF_SHAPES[t], BF16)
        self.plan = _rs_chips_plan(self.ts)
        self.arr, self.sems, self.token = _split_call("rs_chips_start_" + self.tag, brr, start=self.plan)
        if own_later:
            for t in self.ts:
                self.hf[t] = _chip_sum_part(arr["g%d" % t], arr["ra%d" % t], t, self.idx, "chip_sum_own_%d" % t, True,
                                            dep=self.token)
        return self.token

    def share(self, after):
        brr, _, _ = _split_call("rs_chips_wait_" + self.tag, self.arr, wait=self.plan, wait_sems=self.sems, after=after)
        frr = {"f%d" % t: _final_sum(self.hf[t], brr["rb%d" % t], t, self.idx, "final_sum_%d" % t) for t in self.ts}
        self.plan = _rs_share_plan(self.ts)
        self.arr, self.sems, self.token = _split_call("rs_share_start_" + self.tag, frr, start=self.plan)
        return self.token

    def result(self, after):
        frr, _, _ = _split_call("rs_share_wait_" + self.tag, self.arr, wait=self.plan, wait_sems=self.sems, after=after)
        return {t: frr["f%d" % t] for t in self.ts}


def _all8_plan(key):
    def copies(sk, R):
        x, y, c = _coords()
        own = R[key].at[4 * x + 2 * y + c]
        out = []
        for k in range(1, 8):
            dev = ((1 - x) if (k >> 2) & 1 else x, (1 - y) if (k >> 1) & 1 else y, (1 - c) if k & 1 else c)
            out.append(((own, own, dev), R[key].at[4 * dev[0] + 2 * dev[1] + dev[2]]))
        return out
    return _Plan(7, copies)


def _sum8(v, name="small_sum"):
    def kern(v_ref, o_ref):
        acc = v_ref[0]
        for k in range(1, 8):
            acc = acc + v_ref[k]
        o_ref[...] = acc

    return pl.pallas_call(kern, out_shape=jax.ShapeDtypeStruct(v.shape[1:], F32), name=name)(v)


def _adamw(w, g, m, v, name, tr=128, blk0=0, nblk=None, into=None, copy_g=False):
    R, C = w.shape
    tr = min(tr, R)
    if nblk is None:
        assert R % tr == 0 and blk0 == 0
        nblk = R // tr
    n_out = 4 if copy_g else 3

    def kern(*refs):
        w_ref, g_ref, m_ref, v_ref = refs[:4]
        d_ref, mo_ref, vo_ref = refs[-n_out:][:3]
        gv = g_ref[...]
        mn = ADAM_B1 * m_ref[...] + (1.0 - ADAM_B1) * gv
        vn = ADAM_B2 * v_ref[...] + (1.0 - ADAM_B2) * (gv * gv)
        m_hat = mn / (1.0 - ADAM_B1 ** ADAM_STEP)
        v_hat = vn / (1.0 - ADAM_B2 ** ADAM_STEP)
        d_ref[...] = -ADAM_LR * (m_hat / (jnp.sqrt(v_hat) + ADAM_EPS) + ADAM_WD * w_ref[...])
        mo_ref[...] = mn
        vo_ref[...] = vn
        if copy_g:
            refs[-1][...] = gv

    blk = pl.BlockSpec((tr, C), lambda i: (blk0 + i, 0))
    sd = jax.ShapeDtypeStruct((R, C), F32)
    in_specs, args, aliases = [blk] * 4, [w, g, m, v], {}
    if into is not None:
        in_specs += [pl.BlockSpec(memory_space=pl.ANY)] * 3
        args += list(into)
        aliases = {4: 0, 5: 1, 6: 2}
    return pl.pallas_call(kern, grid=(nblk,), in_specs=in_specs, out_specs=(blk,) * n_out, out_shape=(sd,) * n_out,
                          input_output_aliases=aliases, name=name, compiler_params=_cp(("parallel",)))(*args)


def _adamw_w_in(wt, gp, mt, vt, offs, name, r0, tr, nblk, views, into=None, blk_key=None):
    el = lambda n: (pl.Element(n), pl.Element(D))
    first = (lambda o: r0) if blk_key is None else (lambda o: tr * o[blk_key])
    own = pl.BlockSpec(el(tr), lambda i, o: (pl.multiple_of(first(o) + tr * i, 8), 0))

    def view(k):
        return pl.BlockSpec(el(tr), lambda i, o: (pl.multiple_of(jnp.maximum(first(o) + tr * i + o[k], 0), 8), 0))

    def kern(o_ref, w_ref, m_ref, v_ref, *refs):
        g_refs, (d_ref, mo_ref, vo_ref, go_ref) = refs[:len(views)], refs[-4:]
        gv = g_refs[0][...]
        if len(views) == 2:
            row = first(o_ref) + tr * pl.program_id(0) + lax.broadcasted_iota(jnp.int32, (tr, D), 0)
            gv = jnp.where(row < o_ref[2], gv, g_refs[1][...])
        mn = ADAM_B1 * m_ref[...] + (1.0 - ADAM_B1) * gv
        vn = ADAM_B2 * v_ref[...] + (1.0 - ADAM_B2) * (gv * gv)
        m_hat = mn / (1.0 - ADAM_B1 ** ADAM_STEP)
        v_hat = vn / (1.0 - ADAM_B2 ** ADAM_STEP)
        d_ref[...] = -ADAM_LR * (m_hat / (jnp.sqrt(v_hat) + ADAM_EPS) + ADAM_WD * w_ref[...])
        mo_ref[...] = mn
        vo_ref[...] = vn
        go_ref[...] = gv

    in_specs = [own, own, own] + [view(k) for k in views]
    args = [wt, mt, vt] + [gp] * len(views)
    aliases = {}
    if into is not None:
        in_specs += [pl.BlockSpec(memory_space=pl.ANY)] * 4
        args += list(into)
        aliases = {1 + len(args) - 4 + j: j for j in range(4)}
    grid_spec = pltpu.PrefetchScalarGridSpec(num_scalar_prefetch=1, grid=(nblk,), in_specs=in_specs,
                                             out_specs=(own,) * 4)
    sd = jax.ShapeDtypeStruct(wt.shape, F32)
    return pl.pallas_call(kern, grid_spec=grid_spec, out_shape=(sd,) * 4, input_output_aliases=aliases, name=name,
                          compiler_params=_cp(("parallel",)))(offs, *args)


def _to_piece(wt, s):
    z = lambda n: jnp.zeros((n, D), wt.dtype)
    pads = [functools.partial(lambda k, w: jnp.pad(w, ((8 * k, PIECE - W_SHARD - 8 * k), (0, 0))).astype(BF16), k)
            for k in range(3)]
    last = lambda w: jnp.concatenate([z(24), w[:744], w[776:], w[744:776], z(PIECE - 24 - W_SHARD)], axis=0).astype(BF16)
    return lax.switch(s, pads + [last], wt)


_SMALL = [("b_gate", 2048), ("ssm_conv_b", 4096), ("dt_bias", 32), ("A_log", 32), ("D_skip", 32),
          ("ssm_norm_w", 2048), ("norm_mlp", 1024), ("norm_final", 1024), ("sc_conv_w", 3072), ("ssm_conv_w", 16384),
          ("loss", 1)]


def _pack(vals, table, rows):
    parts = []
    for name, n in table:
        v = vals[name].reshape(-1).astype(F32)
        pad = (-n) % 128
        parts.append(jnp.pad(v, (0, pad)) if pad else v)
    flat = jnp.concatenate(parts)
    return jnp.pad(flat, (0, rows * 128 - flat.shape[0])).reshape(rows, 128)


def _unpack(arr, table):
    flat = arr.reshape(-1)
    out, off = {}, 0
    for name, n in table:
        out[name] = flat[off:off + n]
        off += n + ((-n) % 128)
    return out


def kernel(x, norm_mix, w_in, b_gate, sc_conv_w, ssm_conv_w, ssm_conv_b, dt_bias, A_log, D_skip, ssm_norm_w, w_branch_sc, w_branch_ssm, w_out, norm_mlp, w_mlp1, w_mlp2, norm_final, loss_target, m_norm_mix, m_w_in, m_b_gate, m_sc_conv_w, m_ssm_conv_w, m_ssm_conv_b, m_dt_bias, m_A_log, m_D_skip, m_ssm_norm_w, m_w_branch_sc, m_w_branch_ssm, m_w_out, m_norm_mlp, m_w_mlp1, m_w_mlp2, m_norm_final, v_norm_mix, v_w_in, v_b_gate, v_sc_conv_w, v_ssm_conv_w, v_ssm_conv_b, v_dt_bias, v_A_log, v_D_skip, v_ssm_norm_w, v_w_branch_sc, v_w_branch_ssm, v_w_out, v_norm_mlp, v_w_mlp1, v_w_mlp2, v_norm_final):
    L = x.shape[1]
    nc = L // Q
    xi, yi, ci = lax.axis_index("x"), lax.axis_index("y"), lax.axis_index("c")
    s = 2 * xi + yi
    idx = jnp.stack([s, ci]).astype(jnp.int32)
    x0 = x.reshape(L, D)
    tgt = loss_target.reshape(L, D)
    small_names = ["b_gate", "sc_conv_w", "ssm_conv_w", "ssm_conv_b", "dt_bias", "A_log", "D_skip", "ssm_norm_w",
                   "norm_mlp", "norm_final"]
    small_wmv = [dict(zip(small_names, vals)) for vals in (
        (b_gate, sc_conv_w, ssm_conv_w, ssm_conv_b, dt_bias, A_log, D_skip, ssm_norm_w, norm_mlp, norm_final),
        (m_b_gate, m_sc_conv_w, m_ssm_conv_w, m_ssm_conv_b, m_dt_bias, m_A_log, m_D_skip, m_ssm_norm_w, m_norm_mlp,
         m_norm_final),
        (v_b_gate, v_sc_conv_w, v_ssm_conv_w, v_ssm_conv_b, v_dt_bias, v_A_log, v_D_skip, v_ssm_norm_w, v_norm_mlp,
         v_norm_final))]
    small_table = [(n, int(small_wmv[0][n].size)) for n in small_names]
    small_rows = 136
    pk_w, pk_m, pk_v = [_pack(d, small_table, small_rows) for d in small_wmv]

    piece = _to_piece(w_in.T, s)
    nb = PMAIN // XTRA
    cws = jnp.zeros((8, 1280), F32)
    cws = cws.at[0:3, 0:256].set(sc_conv_w).at[0:4, 256:1280].set(ssm_conv_w)
    cw0 = lax.dynamic_update_slice(jnp.zeros((4, 8, 1280), F32), cws[None], (s, 0, 0))
    win_keys, win2_keys, mid_keys, end_keys = ["xt", "cw", "wq0"], ["wq1"], ["wa", "wb", "wo", "w1"], ["w2"]
    gw, sems_w, tok = _split_call(
        "ag_win_start", {"wct": lax.empty((NCW, D), BF16), "xt": lax.empty((4, XTRA, D), BF16), "cw": cw0, "piece": piece},
        start=_ag_chips_plan(win_keys))
    g2, sems_w2, tok = _split_call("ag_win2_start", {"wct": gw["wct"], "piece": gw["piece"]},
                                   start=_ag_chips_plan(win2_keys), after=tok)
    piece = g2["piece"]
    gw["wct"] = _place(piece, (NCW, D), (XTRA, D), lambda i, r: (nb * r[0] + i, 0), idx, "place_wct", nblk=nb,
                       dep=tok, into=g2["wct"])
    gw["xt"] = _place(piece, (4, XTRA, D), (1, XTRA, D), lambda i, r: (r[0], 0, 0), idx, "place_xt", blk0=nb, nblk=1,
                      dep=tok, into=gw["xt"])
    gw["piece"] = piece
    wa0 = _place(w_branch_sc, (D, D), (256, 1024), lambda i, r: (r[0], 0), idx, "place_wa", dep=tok)
    wb0 = _place(w_branch_ssm, (INNER, D), (512, 1024), lambda i, r: (r[0], 0), idx, "place_wb", dep=tok)
    wo0 = _place(w_out, (D, D), (256, 1024), lambda i, r: (r[0], 0), idx, "place_wo", dep=tok)
    w10 = _place(w_mlp1, (D, DFF), (256, 1024), lambda i, r: (i, r[0]), idx, "place_w1", dep=tok)
    gm, sems_m, tok = _split_call("ag_mid_start", {"wa": wa0, "wb": wb0, "wo": wo0, "w1": w10},
                                  start=_ag_chips_plan(mid_keys))
    w20 = _place(w_mlp2, (DFF, D), (256, 1024), lambda i, r: (4 * r[0] + i, 0), idx, "place_w2", dep=tok)
    ge, sems_e, tok = _split_call("ag_end_start", {"w2": w20}, start=_ag_chips_plan(end_keys))
    h = _rms_fwd(x0, norm_mix, "rms_mix", dep=tok)
    gw, sems_w, tok = _split_call("ag_win_pass", gw, wait=_ag_chips_plan(win_keys), wait_sems=sems_w,
                                  start=_ag_sibling_plan(win_keys), after=[h, pk_w, pk_m, pk_v])
    gw, _, _ = _split_call("ag_win_done", gw, wait=_ag_sibling_plan(win_keys), wait_sems=sems_w, after=tok)
    wc, cw_all = _fix_wct(gw["wct"], gw["xt"]), gw["cw"]
    sc_w_full = jnp.concatenate([cw_all[k, :, 0:256] for k in range(4)], axis=1)
    ssm_w_full = jnp.concatenate([cw_all[k, :, 256:1280] for k in range(4)], axis=1)
    cw4 = ssm_w_full.at[4].set(ssm_conv_b)
    vec = jnp.zeros((8, 128), F32).at[0, :NH].set(dt_bias).at[1, :NH].set(A_log)
    vecg = jnp.zeros((NG, 8, 128), F32).at[:, 0, :4].set(A_log.reshape(NG, 4)).at[:, 1, :4].set(D_skip.reshape(NG, 4))

    dtraw = _matmul(h, wc[C_DT:], "nt", F32, 512, 256, 1024, "in_proj_dt")
    proj = _in_proj_wave(h, wc, 0)
    g2, sems_w2, tok = _split_call("ag_win2_pass", {"wct": wc, "piece": gw["piece"]},
                                   wait=_ag_chips_plan(win2_keys), wait_sems=sems_w2,
                                   start=_ag_sibling_plan(win2_keys), after=[proj, dtraw])
    g2, _, _ = _split_call("ag_win2_done", g2, wait=_ag_sibling_plan(win2_keys), wait_sems=sems_w2, after=tok)
    wc = g2["wct"]
    proj = _in_proj_wave(h, wc, 1, proj=proj)
    ya = _sc_fwd(proj, sc_w_full)
    xbc = _ssm_conv_fwd(proj, cw4)
    dt4, cs4, sg4 = _dt_prep(dtraw, vec)
    y, s_all = _ssd_fwd(xbc, dt4, cs4, vecg)
    gm, sems_m, tok = _split_call("ag_mid_pass", gm, wait=_ag_chips_plan(mid_keys), wait_sems=sems_m,
                                  start=_ag_sibling_plan(mid_keys), after=[y, ya])
    y = _tie(y, tok, "tie_y")
    yb = _gnorm_fwd(y, proj, ssm_norm_w)
    gm, _, _ = _split_call("ag_mid_done", gm, wait=_ag_sibling_plan(mid_keys), wait_sems=sems_m, after=yb)
    wa, wb, wo, w1 = gm["wa"], gm["wb"], gm["wo"], gm["w1"]
    ge, sems_e, tok = _split_call("ag_end_pass", ge, wait=_ag_chips_plan(end_keys), wait_sems=sems_e,
                                  start=_ag_sibling_plan(end_keys), after=yb)
    br_a = _matmul(ya, wa, "nn", F32, 1024, 1024, 1024, "branch_sc", dep=tok)
    br_b, merged = _branch_ssm_merge(yb, wb, proj, b_gate, br_a)
    x1, h2 = _matmul_res_rms(merged, wo, x0, norm_mlp, 1024, "out_proj")
    a1, rl = _matmul(h2, w1, "nn", BF16, 1024, 1024, 1024, "mlp1", epi="relu2", n_outer=True)
    ge, _, _ = _split_call("ag_end_done", ge, wait=_ag_sibling_plan(end_keys), wait_sems=sems_e, after=a1)
    w2 = ge["w2"]
    dx2, g_nf, loss8 = _matmul_res_final(rl, w2, x1, norm_final, tgt, 512, "mlp2")

    da = _matmul(dx2, w2, "nt", BF16, 1024, 1024, 1024, "mlp2_dx", epi="drelu", extra=a1, n_outer=True)
    g_w2 = _matmul(rl, dx2, "tn", BF16, 1024, 1024, 2048, "mlp2_dw")
    g_w1 = _matmul(h2, da, "tn", BF16, 1024, 1024, 2048, "mlp1_dw")
    dx1, g_nmlp = _matmul_rms_bwd(da, w1, "nt", x1, norm_mlp, dx2, 512, 4096, "mlp1_dx")
    g_wo = _matmul(merged, dx1, "tn", BF16, 1024, 1024, 2048, "out_proj_dw")
    dproj = lax.empty((L, NCW), BF16)
    dbr, dproj, g_bg = _merge_bwd(dx1, wo, proj, b_gate, br_a, br_b, dproj)
    dya = _matmul(dbr[0], wa, "nt", F32, 1024, 1024, 1024, "branch_sc_dx")
    g_wa = _matmul(ya, dbr[0], "tn", BF16, 1024, 1024, 2048, "branch_sc_dw")
    dproj, g_scw = _sc_bwd(dya, proj, sc_w_full, dproj)
    g_wb = _matmul(yb, dbr[1], "tn", BF16, 1024, 1024, 2048, "branch_ssm_dw")
    rs_a = _ReduceScatter([1, 2, 3, 4, 5], {1: g_w1, 2: g_w2, 3: g_wa, 4: g_wb, 5: g_wo}, idx, "a")
    dy, dproj, g_snw = _gnorm_bwd(dbr, wb, y, proj, ssm_norm_w, dproj, rs_a.token)
    tok = rs_a.chips(after=dy)
    dxs, dbm, dcm, ddt_g, st = _ssd_bwd(xbc, dt4, cs4, sg4, vecg, s_all, _tie(dy, tok, "tie_dy"))
    dproj, gx1 = _ssm_conv_bwd(dxs, proj, cw4, dproj, 0, "ssm_conv_bwd_x")
    dproj, gx2 = _ssm_conv_bwd(dbm, proj, cw4, dproj, INNER, "ssm_conv_bwd_b")
    dproj, gx3 = _ssm_conv_bwd(dcm, proj, cw4, dproj, INNER + NG * NS, "ssm_conv_bwd_c")
    g_cw4 = jnp.concatenate([gx1, gx2, gx3], axis=1)
    dproj, g_dtb = _dt_bwd(ddt_g, dproj)
    small = {"b_gate": g_bg[0], "ssm_conv_b": g_cw4[4], "dt_bias": g_dtb[0, :NH],
             "A_log": st[:, 0, :4], "D_skip": st[:, 1, :4], "ssm_norm_w": g_snw[0], "norm_mlp": g_nmlp[0],
             "norm_final": g_nf[0], "sc_conv_w": g_scw[0:3], "ssm_conv_w": g_cw4[0:4], "loss": loss8[0, 0:1]}
    me = 4 * xi + 2 * yi + ci
    sm8 = lax.dynamic_update_slice(jnp.zeros((8, SMALL_ROWS, 128), F32), _pack(small, _SMALL, SMALL_ROWS)[None], (me, 0, 0))
    sm_arr, sm_sems, tok = _split_call("small_start", {"sm": sm8}, start=_all8_plan("sm"))
    g_wc = _matmul(dproj, h, "tn", BF16, 1280, 1024, 2048, "in_proj_dw", dep=tok)
    rs_b = _ReduceScatter([0], {0: g_wc}, idx, "b")
    tok = rs_a.share(after=rs_b.token)
    tok = rs_b.chips(after=tok, own_later=True)
    grad_x, g_nm = _matmul_rms_bwd(dproj, wc, "nn", x0, norm_mix, dx1, 512, 3840, "in_proj_dx", dep=rs_b.hf[0])
    nm8 = lax.dynamic_update_slice(jnp.zeros((8, 8, 128), F32), g_nm[0].reshape(1, 8, 128), (me, 0, 0))
    nm_arr, nm_sems, tok = _split_call("norm_mix_start", {"nm": nm8}, start=_all8_plan("nm"))
    sm_arr, _, _ = _split_call("small_wait", sm_arr, wait=_all8_plan("sm"), wait_sems=sm_sems, after=tok)
    small_sum = _sum8(sm_arr["sm"])
    gs = _unpack(small_sum, _SMALL)
    red = rs_a.result(after=tok)
    big = {"w_mlp1": red[1], "w_mlp2": red[2], "w_branch_sc": red[3], "w_branch_ssm": red[4], "w_out": red[5]}

    given = dict(norm_mix=norm_mix, w_in=w_in, b_gate=b_gate, sc_conv_w=sc_conv_w, ssm_conv_w=ssm_conv_w, ssm_conv_b=ssm_conv_b, dt_bias=dt_bias, A_log=A_log, D_skip=D_skip, ssm_norm_w=ssm_norm_w, w_branch_sc=w_branch_sc, w_branch_ssm=w_branch_ssm, w_out=w_out, norm_mlp=norm_mlp, w_mlp1=w_mlp1, w_mlp2=w_mlp2, norm_final=norm_final,
                 m_norm_mix=m_norm_mix, m_w_in=m_w_in, m_b_gate=m_b_gate, m_sc_conv_w=m_sc_conv_w, m_ssm_conv_w=m_ssm_conv_w, m_ssm_conv_b=m_ssm_conv_b, m_dt_bias=m_dt_bias, m_A_log=m_A_log, m_D_skip=m_D_skip, m_ssm_norm_w=m_ssm_norm_w, m_w_branch_sc=m_w_branch_sc, m_w_branch_ssm=m_w_branch_ssm, m_w_out=m_w_out, m_norm_mlp=m_norm_mlp, m_w_mlp1=m_w_mlp1, m_w_mlp2=m_w_mlp2, m_norm_final=m_norm_final,
                 v_norm_mix=v_norm_mix, v_w_in=v_w_in, v_b_gate=v_b_gate, v_sc_conv_w=v_sc_conv_w, v_ssm_conv_w=v_ssm_conv_w, v_ssm_conv_b=v_ssm_conv_b, v_dt_bias=v_dt_bias, v_A_log=v_A_log, v_D_skip=v_D_skip, v_ssm_norm_w=v_ssm_norm_w, v_w_branch_sc=v_w_branch_sc, v_w_branch_ssm=v_w_branch_ssm, v_w_out=v_w_out, v_norm_mlp=v_norm_mlp, v_w_mlp1=v_w_mlp1, v_w_mlp2=v_w_mlp2, v_norm_final=v_norm_final)
    order = ["norm_mix", "w_in", "b_gate", "sc_conv_w", "ssm_conv_w", "ssm_conv_b", "dt_bias", "A_log", "D_skip",
             "ssm_norm_w", "w_branch_sc", "w_branch_ssm", "w_out", "norm_mlp", "w_mlp1", "w_mlp2", "norm_final"]
    grad, delta, new_m, new_v = {}, {}, {}, {}
    for n in big:
        delta[n], new_m[n], new_v[n], grad[n] = _adamw(given[n], big[n], given["m_" + n], given["v_" + n],
                                                       "adamw_" + n, copy_g=True)
    big["w_in"] = None
    grad_small = {n: gs[n].reshape(given[n].shape) for n in small_names if n not in ("sc_conv_w", "ssm_conv_w")}
    grad_small["sc_conv_w"] = lax.dynamic_slice(gs["sc_conv_w"].reshape(3, D), (0, 256 * s), (3, 256))
    grad_small["ssm_conv_w"] = lax.dynamic_slice(gs["ssm_conv_w"].reshape(4, XBC), (0, 1024 * s), (4, 1024))
    table = small_table
    ds_, ms_, vs_ = _adamw(pk_w, _pack(grad_small, table, small_rows), pk_m, pk_v, "adamw_small", tr=small_rows)
    ds_, ms_, vs_ = _unpack(ds_, table), _unpack(ms_, table), _unpack(vs_, table)
    for n in grad_small:
        shp = given[n].shape
        grad[n] = grad_small[n]
        delta[n], new_m[n], new_v[n] = ds_[n].reshape(shp), ms_[n].reshape(shp), vs_[n].reshape(shp)

    done = [new_v[n] for n in ("w_mlp1", "w_mlp2", "w_branch_sc", "w_branch_ssm", "w_out")] + [vs_["b_gate"]]
    tok = rs_b.share(after=done)
    offs = jnp.where(s == 3, jnp.array([24, -8, 744, 2072, -8], jnp.int32),
                     jnp.stack([8 * s, 8 * s, 0 * s, 8 * s, 8 * s]).astype(jnp.int32))
    offs = jnp.concatenate([offs, jnp.stack([7 * ci, 4 - 4 * ci]).astype(jnp.int32)])
    nmain = W_SHARD // 256
    wt_own = (w_in.T, rs_b.arr["f0"], m_w_in.T, v_w_in.T, offs)
    res = _adamw_w_in(*wt_own, "adamw_w_in_own", 0, 256, 4, (0, 1), blk_key=5)
    gp = rs_b.result(after=[tok, res[0]])[0]
    wt_args = (w_in.T, gp, m_w_in.T, v_w_in.T, offs)
    res = _adamw_w_in(*wt_args, "adamw_w_in", 0, 256, nmain - 4, (0, 1), into=res, blk_key=6)
    res = _adamw_w_in(*wt_args, "adamw_w_in_dt", 744, 32, 1, (3,), into=res)
    dt_, mt_, vt_, gwt = _adamw_w_in(*wt_args, "adamw_w_in_tail", 256 * nmain, 8, 1, (4,), into=res)
    grad["w_in"], delta["w_in"], new_m["w_in"], new_v["w_in"] = gwt.T, dt_.T, mt_.T, vt_.T
    nm_arr, _, _ = _split_call("norm_mix_wait", nm_arr, wait=_all8_plan("nm"), wait_sems=nm_sems, after=tok)
    g8 = _sum8(nm_arr["nm"], "norm_mix_sum")
    r8 = lambda a: a.reshape(8, 128)
    d8, m8, v8 = _adamw(r8(norm_mix), g8, r8(m_norm_mix), r8(v_norm_mix), "adamw_norm_mix", tr=8)
    grad["norm_mix"], delta["norm_mix"] = g8.reshape(D), d8.reshape(D)
    new_m["norm_mix"], new_v["norm_mix"] = m8.reshape(D), v8.reshape(D)

    loss = gs["loss"].reshape(())
    return (loss, grad_x.reshape(1, L, D), *[grad[n] for n in order], *[delta[n] for n in order],
            *[new_m[n] for n in order], *[new_v[n] for n in order])
```
